```python
import jax, jax.numpy as jnp
from jax import lax
import numpy as np

D_MODEL = 1024
BATCH = 8
SEQ = 4096
DEPTH = 1

HEAD_DIM = 64
W_CONV = D_MODEL // 2
W_ATTN = D_MODEL - W_CONV
N_CONV_GROUPS = W_CONV // HEAD_DIM
N_ATTN_HEADS = W_ATTN // HEAD_DIM
CONV_K = 3
D_FF = 4 * D_MODEL
PLE_DIM = 256
Q_BLOCK = 128
EPS = 1e-6
IN_COLS = 3 * W_CONV + 3 * W_ATTN

kernel_name = "hymba_shortconv_stickbreaking_hybrid"


def rmsnorm(x, g):
    xf = x.astype(jnp.float32)
    y = xf * lax.rsqrt(jnp.mean(xf * xf, axis=-1, keepdims=True) + EPS)
    return (y * g.astype(jnp.float32)).astype(x.dtype)


def head_rmsnorm(y, g):
    b, s, w = y.shape
    yf = y.astype(jnp.float32).reshape(b, s, w // HEAD_DIM, HEAD_DIM)
    yf = yf * lax.rsqrt(jnp.mean(yf * yf, axis=-1, keepdims=True) + EPS)
    return (yf.reshape(b, s, w) * g.astype(jnp.float32)).astype(y.dtype)


def short_gated_conv(b_gate, c_gate, u, w_conv):
    v = c_gate * u
    y = lax.conv_general_dilated(
        v, w_conv[:, None, :].astype(v.dtype),
        window_strides=(1,), padding=[(CONV_K - 1, 0)],
        dimension_numbers=('NWC', 'WIO', 'NWC'),
        feature_group_count=v.shape[-1])
    return b_gate * y


def stick_breaking_attention(q, k, v):
    b, h, s, dh = q.shape
    n_blk = s // Q_BLOCK
    scale = dh ** -0.5
    qb = q.reshape(b, h, n_blk, Q_BLOCK, dh).transpose(2, 0, 1, 3, 4)
    kf = k.astype(jnp.float32)
    vf = v.astype(jnp.float32)
    key_pos = jnp.arange(s)

    def one_block(args):
        qi, blk = args
        z = jnp.einsum('bhqd,bhkd->bhqk', qi.astype(jnp.float32), kf) * scale
        q_pos = blk * Q_BLOCK + jnp.arange(Q_BLOCK)
        mask = key_pos[None, :] < q_pos[:, None]
        log_keep = jnp.where(mask, jax.nn.log_sigmoid(-z), 0.0)
        suffix = lax.cumsum(log_keep, axis=3, reverse=True) - log_keep
        a = jnp.where(mask, jnp.exp(jax.nn.log_sigmoid(z) + suffix), 0.0)
        return jnp.einsum('bhqk,bhkd->bhqd', a, vf)

    out = lax.map(one_block, (qb, jnp.arange(n_blk)))
    return out.transpose(1, 2, 0, 3, 4).reshape(b, h, s, dh).astype(q.dtype)


def _fwd_setup_inputs(seed: int = 0) -> dict:
    key = jax.random.key(seed)
    ks = jax.random.split(key, 20)
    f32 = jnp.float32

    def nrm(k, shape, fan_in):
        return jax.random.normal(k, shape, f32) * (fan_in ** -0.5)

    def gain(k, shape):
        return 1.0 + 0.02 * jax.random.normal(k, shape, f32)

    return {
        "x": jax.random.normal(ks[0], (BATCH, SEQ, D_MODEL), f32),
        "p": jax.random.normal(ks[1], (DEPTH, BATCH, SEQ, PLE_DIM), f32),
        "g_mix": gain(ks[2], (DEPTH, D_MODEL)),
        "w_in": nrm(ks[3], (DEPTH, D_MODEL, IN_COLS), D_MODEL),
        "conv_w": nrm(ks[4], (DEPTH, CONV_K, W_CONV), CONV_K),
        "g_conv_out": gain(ks[5], (DEPTH, W_CONV)),
        "g_attn_out": gain(ks[6], (DEPTH, W_ATTN)),
        "w_out": nrm(ks[7], (DEPTH, W_CONV + W_ATTN, D_MODEL), W_CONV + W_ATTN),
        "g_mlp": gain(ks[8], (DEPTH, D_MODEL)),
        "w_up": nrm(ks[9], (DEPTH, D_MODEL, D_FF), D_MODEL),
        "w_down": nrm(ks[10], (DEPTH, D_FF, D_MODEL), D_FF),
        "g_ple": gain(ks[11], (DEPTH, D_MODEL)),
        "w_ple_gate": nrm(ks[12], (DEPTH, D_MODEL, D_MODEL), D_MODEL),
        "w_ple_proj": nrm(ks[13], (DEPTH, PLE_DIM, D_MODEL), PLE_DIM),
        "g_final": gain(ks[14], (D_MODEL,)),
    }


def _fwd_reference(x, p, g_mix, w_in, conv_w, g_conv_out, g_attn_out, w_out, g_mlp, w_up, w_down,
              g_ple, w_ple_gate, w_ple_proj, g_final):
    b, s, _ = x.shape
    splits = [W_CONV, 2 * W_CONV, 3 * W_CONV, 3 * W_CONV + W_ATTN, 3 * W_CONV + 2 * W_ATTN]
    h = x
    for i in range(DEPTH):
        a = rmsnorm(h, g_mix[i])
        proj = a @ w_in[i]
        cb, cc, cu, q, k, v = jnp.split(proj, splits, axis=-1)
        conv_out = head_rmsnorm(short_gated_conv(cb, cc, cu, conv_w[i]), g_conv_out[i])
        to_heads = lambda t: t.reshape(b, s, N_ATTN_HEADS, HEAD_DIM).transpose(0, 2, 1, 3)
        attn = stick_breaking_attention(to_heads(q), to_heads(k), to_heads(v))
        attn = head_rmsnorm(attn.transpose(0, 2, 1, 3).reshape(b, s, W_ATTN), g_attn_out[i])
        h = h + jnp.concatenate([conv_out, attn], axis=-1) @ w_out[i]
        m = rmsnorm(h, g_mlp[i])
        h = h + jnp.square(jax.nn.relu(m @ w_up[i])) @ w_down[i]
        gate = jax.nn.sigmoid(rmsnorm(h, g_ple[i]) @ w_ple_gate[i])
        h = h + gate * (p[i] @ w_ple_proj[i])
    return rmsnorm(h, g_final)


import jax as _jax
import jax.numpy as _jnp

TWIN_FORMAT = 'train_step'
FWD_PARAMS = ['x', 'p', 'g_mix', 'w_in', 'conv_w', 'g_conv_out', 'g_attn_out', 'w_out', 'g_mlp', 'w_up', 'w_down', 'g_ple', 'w_ple_gate', 'w_ple_proj', 'g_final']
TWIN_WEIGHTS = ['g_mix', 'w_in', 'conv_w', 'g_conv_out', 'g_attn_out', 'w_out', 'g_mlp', 'w_up', 'w_down', 'g_ple', 'w_ple_gate', 'w_ple_proj', 'g_final']
TWIN_DIFF_INPUT = 'x'
TWIN_INPUTS = ['x', 'p', 'g_mix', 'w_in', 'conv_w', 'g_conv_out', 'g_attn_out', 'w_out', 'g_mlp', 'w_up', 'w_down', 'g_ple', 'w_ple_gate', 'w_ple_proj', 'g_final', 'loss_target', 'm_g_mix', 'm_w_in', 'm_conv_w', 'm_g_conv_out', 'm_g_attn_out', 'm_w_out', 'm_g_mlp', 'm_w_up', 'm_w_down', 'm_g_ple', 'm_w_ple_gate', 'm_w_ple_proj', 'm_g_final', 'v_g_mix', 'v_w_in', 'v_conv_w', 'v_g_conv_out', 'v_g_attn_out', 'v_w_out', 'v_g_mlp', 'v_w_up', 'v_w_down', 'v_g_ple', 'v_w_ple_gate', 'v_w_ple_proj', 'v_g_final']
TWIN_OUTPUTS = ['loss', 'grad_x', 'grad_g_mix', 'grad_w_in', 'grad_conv_w', 'grad_g_conv_out', 'grad_g_attn_out', 'grad_w_out', 'grad_g_mlp', 'grad_w_up', 'grad_w_down', 'grad_g_ple', 'grad_w_ple_gate', 'grad_w_ple_proj', 'grad_g_final', 'delta_g_mix', 'delta_w_in', 'delta_conv_w', 'delta_g_conv_out', 'delta_g_attn_out', 'delta_w_out', 'delta_g_mlp', 'delta_w_up', 'delta_w_down', 'delta_g_ple', 'delta_w_ple_gate', 'delta_w_ple_proj', 'delta_g_final', 'new_m_g_mix', 'new_m_w_in', 'new_m_conv_w', 'new_m_g_conv_out', 'new_m_g_attn_out', 'new_m_w_out', 'new_m_g_mlp', 'new_m_w_up', 'new_m_w_down', 'new_m_g_ple', 'new_m_w_ple_gate', 'new_m_w_ple_proj', 'new_m_g_final', 'new_v_g_mix', 'new_v_w_in', 'new_v_conv_w', 'new_v_g_conv_out', 'new_v_g_attn_out', 'new_v_w_out', 'new_v_g_mlp', 'new_v_w_up', 'new_v_w_down', 'new_v_g_ple', 'new_v_w_ple_gate', 'new_v_w_ple_proj', 'new_v_g_final']
TWIN_LEAF_KINDS = {'loss': 'loss', 'grad_x': 'grad_x', 'grad_g_mix': 'grad_w', 'grad_w_in': 'grad_w', 'grad_conv_w': 'grad_w', 'grad_g_conv_out': 'grad_w', 'grad_g_attn_out': 'grad_w', 'grad_w_out': 'grad_w', 'grad_g_mlp': 'grad_w', 'grad_w_up': 'grad_w', 'grad_w_down': 'grad_w', 'grad_g_ple': 'grad_w', 'grad_w_ple_gate': 'grad_w', 'grad_w_ple_proj': 'grad_w', 'grad_g_final': 'grad_w', 'delta_g_mix': 'delta_w', 'delta_w_in': 'delta_w', 'delta_conv_w': 'delta_w', 'delta_g_conv_out': 'delta_w', 'delta_g_attn_out': 'delta_w', 'delta_w_out': 'delta_w', 'delta_g_mlp': 'delta_w', 'delta_w_up': 'delta_w', 'delta_w_down': 'delta_w', 'delta_g_ple': 'delta_w', 'delta_w_ple_gate': 'delta_w', 'delta_w_ple_proj': 'delta_w', 'delta_g_final': 'delta_w', 'new_m_g_mix': 'new_m', 'new_m_w_in': 'new_m', 'new_m_conv_w': 'new_m', 'new_m_g_conv_out': 'new_m', 'new_m_g_attn_out': 'new_m', 'new_m_w_out': 'new_m', 'new_m_g_mlp': 'new_m', 'new_m_w_up': 'new_m', 'new_m_w_down': 'new_m', 'new_m_g_ple': 'new_m', 'new_m_w_ple_gate': 'new_m', 'new_m_w_ple_proj': 'new_m', 'new_m_g_final': 'new_m', 'new_v_g_mix': 'new_v', 'new_v_w_in': 'new_v', 'new_v_conv_w': 'new_v', 'new_v_g_conv_out': 'new_v', 'new_v_g_attn_out': 'new_v', 'new_v_w_out': 'new_v', 'new_v_g_mlp': 'new_v', 'new_v_w_up': 'new_v', 'new_v_w_down': 'new_v', 'new_v_g_ple': 'new_v', 'new_v_w_ple_gate': 'new_v', 'new_v_w_ple_proj': 'new_v', 'new_v_g_final': 'new_v'}


def _forward(args):
    return _fwd_reference(*[args[k] for k in FWD_PARAMS])


def _output_shape():
    def fwd():
        inp = _fwd_setup_inputs(0)
        return _fwd_reference(*[inp[k] for k in FWD_PARAMS])
    out = _jax.eval_shape(fwd)
    return out.shape, out.dtype

N_MICROBATCH = 1
ADAM_LR = 0.001
ADAM_B1 = 0.9
ADAM_B2 = 0.999
ADAM_EPS = 1e-08
ADAM_WD = 0.01
ADAM_STEP = 10
PER_EXAMPLE_BATCH_AXIS = {'x': 0, 'p': 1, 'loss_target': 0}
SHARED_INPUTS = []
_WEIGHT_DTYPES = {'g_mix': _jnp.float32, 'w_in': _jnp.float32, 'conv_w': _jnp.float32, 'g_conv_out': _jnp.float32, 'g_attn_out': _jnp.float32, 'w_out': _jnp.float32, 'g_mlp': _jnp.float32, 'w_up': _jnp.float32, 'w_down': _jnp.float32, 'g_ple': _jnp.float32, 'w_ple_gate': _jnp.float32, 'w_ple_proj': _jnp.float32, 'g_final': _jnp.float32}
MOMENT_SCALE = {'g_mix': 2.037038e-01, 'w_in': 1.158339e-01, 'conv_w': 1.470581e-01, 'g_conv_out': 1.289252e-01, 'g_attn_out': 1.352076e-01, 'w_out': 1.304558e-01, 'g_mlp': 1.384749e-01, 'w_up': 6.520255e-02, 'w_down': 1.271835e-01, 'g_ple': 1.981805e-02, 'w_ple_gate': 2.032432e-02, 'w_ple_proj': 4.948113e-02, 'g_final': 3.216711e+01}


def _to_microbatches(a, axis):
    t = _jnp.moveaxis(a, axis, 0)
    t = t.reshape((N_MICROBATCH, t.shape[0] // N_MICROBATCH) + t.shape[1:])
    return _jnp.moveaxis(t, 1, axis + 1)


def setup_inputs(seed: int = 0) -> dict:
    inp = _fwd_setup_inputs(seed)
    key = _jax.random.fold_in(_jax.random.key(seed), 7919)
    shape, _ = _output_shape()
    out = dict(inp)
    out["loss_target"] = _jax.random.normal(_jax.random.fold_in(key, 0), shape, _jnp.float32)
    for i, name in enumerate(TWIN_WEIGHTS):
        w = inp[name].astype(_jnp.float32)
        if MOMENT_SCALE is None:
            s = _jnp.sqrt(_jnp.mean(_jnp.square(w)) + 1e-30)
        else:
            s = MOMENT_SCALE[name]
        km, kv = _jax.random.split(_jax.random.fold_in(key, i + 1))
        out[name] = w
        out["m_" + name] = s * _jax.random.normal(km, w.shape, _jnp.float32)
        out["v_" + name] = (s * s) * _jax.random.uniform(kv, w.shape, _jnp.float32, 0.5, 1.5)
    if N_MICROBATCH > 1:
        for name, axis in PER_EXAMPLE_BATCH_AXIS.items():
            out[name] = _to_microbatches(out[name], axis)
    return {'x': out['x'], 'p': out['p'], 'g_mix': out['g_mix'], 'w_in': out['w_in'], 'conv_w': out['conv_w'], 'g_conv_out': out['g_conv_out'], 'g_attn_out': out['g_attn_out'], 'w_out': out['w_out'], 'g_mlp': out['g_mlp'], 'w_up': out['w_up'], 'w_down': out['w_down'], 'g_ple': out['g_ple'], 'w_ple_gate': out['w_ple_gate'], 'w_ple_proj': out['w_ple_proj'], 'g_final': out['g_final'], 'loss_target': out['loss_target'], 'm_g_mix': out['m_g_mix'], 'm_w_in': out['m_w_in'], 'm_conv_w': out['m_conv_w'], 'm_g_conv_out': out['m_g_conv_out'], 'm_g_attn_out': out['m_g_attn_out'], 'm_w_out': out['m_w_out'], 'm_g_mlp': out['m_g_mlp'], 'm_w_up': out['m_w_up'], 'm_w_down': out['m_w_down'], 'm_g_ple': out['m_g_ple'], 'm_w_ple_gate': out['m_w_ple_gate'], 'm_w_ple_proj': out['m_w_ple_proj'], 'm_g_final': out['m_g_final'], 'v_g_mix': out['v_g_mix'], 'v_w_in': out['v_w_in'], 'v_conv_w': out['v_conv_w'], 'v_g_conv_out': out['v_g_conv_out'], 'v_g_attn_out': out['v_g_attn_out'], 'v_w_out': out['v_w_out'], 'v_g_mlp': out['v_g_mlp'], 'v_w_up': out['v_w_up'], 'v_w_down': out['v_w_down'], 'v_g_ple': out['v_g_ple'], 'v_w_ple_gate': out['v_w_ple_gate'], 'v_w_ple_proj': out['v_w_ple_proj'], 'v_g_final': out['v_g_final']}


def _loss(weights, diff, rest, loss_target):
    with _jax.named_scope("forward"):
        args = {**rest, TWIN_DIFF_INPUT: diff, **{k: w.astype(_WEIGHT_DTYPES[k]) for k, w in weights.items()}}
        y = _forward(args)
    with _jax.named_scope("loss_head"):
        err = _jnp.square(y.astype(_jnp.float32) - loss_target)
        return 0.5 * _jnp.sum(_jnp.mean(err, axis=-1)) if err.ndim else 0.5 * err


def _adamw(w, g, m, v):
    m = ADAM_B1 * m + (1.0 - ADAM_B1) * g
    v = ADAM_B2 * v + (1.0 - ADAM_B2) * _jnp.square(g)
    m_hat = m / (1.0 - ADAM_B1 ** ADAM_STEP)
    v_hat = v / (1.0 - ADAM_B2 ** ADAM_STEP)
    delta = -ADAM_LR * (m_hat / (_jnp.sqrt(v_hat) + ADAM_EPS) + ADAM_WD * w)
    return delta, m, v


def reference(x, p, g_mix, w_in, conv_w, g_conv_out, g_attn_out, w_out, g_mlp, w_up, w_down, g_ple, w_ple_gate, w_ple_proj, g_final, loss_target, m_g_mix, m_w_in, m_conv_w, m_g_conv_out, m_g_attn_out, m_w_out, m_g_mlp, m_w_up, m_w_down, m_g_ple, m_w_ple_gate, m_w_ple_proj, m_g_final, v_g_mix, v_w_in, v_conv_w, v_g_conv_out, v_g_attn_out, v_w_out, v_g_mlp, v_w_up, v_w_down, v_g_ple, v_w_ple_gate, v_w_ple_proj, v_g_final):
    given = dict(x=x, p=p, g_mix=g_mix, w_in=w_in, conv_w=conv_w, g_conv_out=g_conv_out, g_attn_out=g_attn_out, w_out=w_out, g_mlp=g_mlp, w_up=w_up, w_down=w_down, g_ple=g_ple, w_ple_gate=w_ple_gate, w_ple_proj=w_ple_proj, g_final=g_final, loss_target=loss_target, m_g_mix=m_g_mix, m_w_in=m_w_in, m_conv_w=m_conv_w, m_g_conv_out=m_g_conv_out, m_g_attn_out=m_g_attn_out, m_w_out=m_w_out, m_g_mlp=m_g_mlp, m_w_up=m_w_up, m_w_down=m_w_down, m_g_ple=m_g_ple, m_w_ple_gate=m_w_ple_gate, m_w_ple_proj=m_w_ple_proj, m_g_final=m_g_final, v_g_mix=v_g_mix, v_w_in=v_w_in, v_conv_w=v_conv_w, v_g_conv_out=v_g_conv_out, v_g_attn_out=v_g_attn_out, v_w_out=v_w_out, v_g_mlp=v_g_mlp, v_w_up=v_w_up, v_w_down=v_w_down, v_g_ple=v_g_ple, v_w_ple_gate=v_w_ple_gate, v_w_ple_proj=v_w_ple_proj, v_g_final=v_g_final)
    weights = {n: given[n] for n in TWIN_WEIGHTS}
    shared = {n: given[n] for n in SHARED_INPUTS}
    per_example = {n: given[n] for n in ['x', 'p']}
    grad_fn = _jax.value_and_grad(_loss, argnums=(0, 1))

    def one_microbatch(ex, loss_target):
        ex = dict(ex)
        diff = ex.pop(TWIN_DIFF_INPUT)
        return grad_fn(weights, diff, {**shared, **ex}, loss_target)

    if N_MICROBATCH == 1:
        loss, (grad_w, grad_x) = one_microbatch(per_example, given["loss_target"])
    else:
        def body(carry, xs):
            loss_sum, grad_sum = carry
            l_k, (gw_k, gx_k) = one_microbatch(xs[0], xs[1])
            with _jax.named_scope("update"):
                return (loss_sum + l_k, _jax.tree.map(_jnp.add, grad_sum, gw_k)), gx_k

        init = (_jnp.zeros((), _jnp.float32), _jax.tree.map(_jnp.zeros_like, weights))
        (loss, grad_w), grad_x = _jax.lax.scan(body, init, (per_example, given["loss_target"]))
    with _jax.named_scope("update"):
        delta_w, new_m, new_v = {}, {}, {}
        for n in TWIN_WEIGHTS:
            delta_w[n], new_m[n], new_v[n] = _adamw(weights[n], grad_w[n], given["m_" + n], given["v_" + n])
    return (loss, grad_x, *[grad_w[n] for n in TWIN_WEIGHTS], *[delta_w[n] for n in TWIN_WEIGHTS],
            *[new_m[n] for n in TWIN_WEIGHTS], *[new_v[n] for n in TWIN_WEIGHTS])
```

```python
import functools

import jax
import jax.numpy as jnp
from jax import lax
from jax.experimental import pallas as pl
from jax.experimental.pallas import tpu as pltpu

F32 = jnp.float32
BF16 = jnp.bfloat16
EPS = 1e-6
HEAD_DIM = 64
LANES = 128
CONV_K = 3
ATTN_BLOCK = 256
HALO = 8
N_DEV = 8
MESH = pl.DeviceIdType.MESH
VMEM_LIMIT = 56 * 1024 * 1024

ADAM_LR = 0.001
ADAM_B1 = 0.9
ADAM_B2 = 0.999
ADAM_EPS = 1e-08
ADAM_WD = 0.01
ADAM_STEP = 10


def _pcall(body, **kw):
    return pl.pallas_call(body, **kw)


def _params(sem=None, **kw):
    return pltpu.CompilerParams(dimension_semantics=sem, vmem_limit_bytes=VMEM_LIMIT, **kw)


def _tile(dim, pref):
    t = min(dim, pref)
    while dim % t:
        t -= LANES
    assert t > 0, (dim, pref)
    return t


_NN = (((1,), (0,)), ((), ()))
_NT = (((1,), (1,)), ((), ()))
_TN = (((0,), (0,)), ((), ()))


def _ep_store(acc, outs):
    outs[0][...] = acc.astype(outs[0].dtype)


def _ep_residual(acc, res, outs):
    outs[0][...] = acc + res[...]


def _ep_up(acc, outs):
    outs[0][...] = acc
    outs[1][...] = jnp.square(jnp.maximum(acc, 0.0)).astype(BF16)


def _ep_dact(acc, u, outs):
    outs[0][...] = (acc * (2.0 * jnp.maximum(u[...], 0.0))).astype(BF16)


def _matmul(name, a, b, *, dims, grid, a_spec, b_spec, acc_shape, out_shapes, out_specs,
            epilogue=_ep_store, extras=(), extra_specs=()):
    nk = grid[2]
    n_ex = len(extras)

    def body(a_ref, b_ref, *rest):
        ex, outs, acc = rest[:n_ex], rest[n_ex:-1], rest[-1]
        k = pl.program_id(2)

        @pl.when(k == 0)
        def _():
            acc[...] = jnp.zeros_like(acc)

        acc[...] += lax.dot_general(a_ref[...].astype(BF16), b_ref[...].astype(BF16), dims,
                                    preferred_element_type=F32)

        @pl.when(k == nk - 1)
        def _():
            epilogue(acc[...], *ex, outs)

    res = _pcall(
        body, name=name, grid=grid,
        in_specs=[a_spec, b_spec, *extra_specs],
        out_specs=list(out_specs), out_shape=list(out_shapes),
        scratch_shapes=[pltpu.VMEM(acc_shape, F32)],
        compiler_params=_params(("parallel", "parallel", "arbitrary")),
    )(a, b, *extras)
    return res


def _mm_nn(name, a, w, *, n_shard=None, epilogue=_ep_store, out_dtypes=(F32,), extras=(), tm=512, tn=512, tk=512):
    m, kd = a.shape
    if n_shard is None:
        n = w.shape[1]
        tn = _tile(n, tn)
        tk = _tile(kd, tk)
        b_spec = pl.BlockSpec((tk, tn), lambda i, j, k: (k, j))
    else:
        n = N_DEV * n_shard
        tn = _tile(n_shard, tn)
        tk = _tile(kd, tk)
        per = n_shard // tn
        b_spec = pl.BlockSpec((None, tk, tn), lambda i, j, k: (j // per, k, j % per))
    tm = _tile(m, tm)
    o_spec = pl.BlockSpec((tm, tn), lambda i, j, k: (i, j))
    return _matmul(
        name, a, w, dims=_NN, grid=(m // tm, n // tn, kd // tk),
        a_spec=pl.BlockSpec((tm, tk), lambda i, j, k: (i, k)), b_spec=b_spec,
        acc_shape=(tm, tn),
        out_shapes=[jax.ShapeDtypeStruct((m, n), d) for d in out_dtypes],
        out_specs=[o_spec] * len(out_dtypes),
        epilogue=epilogue, extras=extras, extra_specs=[o_spec] * len(extras))


def _mm_nt(name, a, w, *, k_shard=None, epilogue=_ep_store, out_dtypes=(F32,), extras=(), tm=512, tn=512, tk=512):
    m, kd = a.shape
    if k_shard is None:
        n = w.shape[0]
        tn = _tile(n, tn)
        tk = _tile(kd, tk)
        b_spec = pl.BlockSpec((tn, tk), lambda i, j, k: (j, k))
    else:
        n = w.shape[1]
        tn = _tile(n, tn)
        tk = _tile(k_shard, tk)
        per = k_shard // tk
        b_spec = pl.BlockSpec((None, tn, tk), lambda i, j, k: (k // per, j, k % per))
    tm = _tile(m, tm)
    o_spec = pl.BlockSpec((tm, tn), lambda i, j, k: (i, j))
    return _matmul(
        name, a, w, dims=_NT, grid=(m // tm, n // tn, kd // tk),
        a_spec=pl.BlockSpec((tm, tk), lambda i, j, k: (i, k)), b_spec=b_spec,
        acc_shape=(tm, tn),
        out_shapes=[jax.ShapeDtypeStruct((m, n), d) for d in out_dtypes],
        out_specs=[o_spec] * len(out_dtypes),
        epilogue=epilogue, extras=extras, extra_specs=[o_spec] * len(extras))


def _mm_tn(name, a, b, *, n_shard=None, tm=512, tn=512, tk=512):
    t, m = a.shape
    n = b.shape[1]
    tm = _tile(m, tm)
    tk = _tile(t, tk)
    if n_shard is None:
        tn = _tile(n, tn)
        o_spec = pl.BlockSpec((tm, tn), lambda i, j, k: (i, j))
        o_shape = jax.ShapeDtypeStruct((m, n), F32)
    else:
        tn = _tile(n_shard, tn)
        per = n_shard // tn
        o_spec = pl.BlockSpec((None, tm, tn), lambda i, j, k: (j // per, i, j % per))
        o_shape = jax.ShapeDtypeStruct((N_DEV, m, n_shard), F32)
    return _matmul(
        name, a, b, dims=_TN, grid=(m // tm, n // tn, t // tk),
        a_spec=pl.BlockSpec((tk, tm), lambda i, j, k: (k, i)),
        b_spec=pl.BlockSpec((tk, tn), lambda i, j, k: (k, j)),
        acc_shape=(tm, tn), out_shapes=[o_shape], out_specs=[o_spec])[0]


def _row_spec(tr, d):
    return pl.BlockSpec((tr, d), lambda i: (i, 0))


def _vec_spec(d):
    return pl.BlockSpec((1, d), lambda i: (0, 0))


def _rmsnorm_fwd(name, x, g, tr=512):
    s, d = x.shape
    tr = _tile(s, tr)

    def body(x_ref, g_ref, o_ref):
        xv = x_ref[...]
        r = lax.rsqrt(jnp.mean(xv * xv, axis=-1, keepdims=True) + EPS)
        o_ref[...] = (xv * r * g_ref[...]).astype(BF16)

    return _pcall(body, name=name, grid=(s // tr,),
                  in_specs=[_row_spec(tr, d), _vec_spec(d)], out_specs=_row_spec(tr, d),
                  out_shape=jax.ShapeDtypeStruct((s, d), BF16),
                  compiler_params=_params(("parallel",)))(x, g)


def _rmsnorm_bwd(name, dn, h, g, dres, tr=512):
    s, d = h.shape
    tr = _tile(s, tr)

    def body(dn_ref, h_ref, g_ref, dres_ref, dh_ref, dhb_ref, dg_ref):
        @pl.when(pl.program_id(0) == 0)
        def _():
            dg_ref[...] = jnp.zeros_like(dg_ref)

        hv = h_ref[...]
        dnv = dn_ref[...]
        r = lax.rsqrt(jnp.mean(hv * hv, axis=-1, keepdims=True) + EPS)
        hn = hv * r
        dg_ref[...] += jnp.sum(dnv * hn, axis=0, keepdims=True)
        dhn = dnv * g_ref[...]
        dh = dres_ref[...] + r * (dhn - hn * jnp.mean(dhn * hn, axis=-1, keepdims=True))
        dh_ref[...] = dh
        dhb_ref[...] = dh.astype(BF16)

    return _pcall(body, name=name, grid=(s // tr,),
                  in_specs=[_row_spec(tr, d), _row_spec(tr, d), _vec_spec(d), _row_spec(tr, d)],
                  out_specs=[_row_spec(tr, d), _row_spec(tr, d), _vec_spec(d)],
                  out_shape=[jax.ShapeDtypeStruct((s, d), F32), jax.ShapeDtypeStruct((s, d), BF16),
                             jax.ShapeDtypeStruct((1, d), F32)],
                  compiler_params=_params(("arbitrary",)))(dn, h, g, dres)


def _ple_loss(h2, gl, pp, tgt, g_final, tr=512):
    s, d = h2.shape
    tr = _tile(s, tr)

    def body(h2_ref, gl_ref, pp_ref, t_ref, g_ref, loss_ref, dh3_ref, dgl_ref, dpp_ref, dg_ref):
        @pl.when(pl.program_id(0) == 0)
        def _():
            dg_ref[...] = jnp.zeros_like(dg_ref)
            loss_ref[...] = jnp.zeros_like(loss_ref)

        gate = jax.nn.sigmoid(gl_ref[...])
        ppv = pp_ref[...]
        h3 = h2_ref[...] + gate * ppv
        r = lax.rsqrt(jnp.mean(h3 * h3, axis=-1, keepdims=True) + EPS)
        hn = h3 * r
        gv = g_ref[...]
        diff = hn * gv - t_ref[...]
        row = jnp.mean(diff * diff, axis=-1, keepdims=True)
        loss_ref[...] += 0.5 * jnp.sum(row, axis=0, keepdims=True)
        dy = diff * (1.0 / d)
        dg_ref[...] += jnp.sum(dy * hn, axis=0, keepdims=True)
        dhn = dy * gv
        dh3 = r * (dhn - hn * jnp.mean(dhn * hn, axis=-1, keepdims=True))
        dh3_ref[...] = dh3
        dgl_ref[...] = (dh3 * ppv * gate * (1.0 - gate)).astype(BF16)
        dpp_ref[...] = (dh3 * gate).astype(BF16)

    return _pcall(body, name="ple_loss", grid=(s // tr,),
                  in_specs=[_row_spec(tr, d)] * 4 + [_vec_spec(d)],
                  out_specs=[_vec_spec(LANES), _row_spec(tr, d), _row_spec(tr, d), _row_spec(tr, d), _vec_spec(d)],
                  out_shape=[jax.ShapeDtypeStruct((1, LANES), F32), jax.ShapeDtypeStruct((s, d), F32),
                             jax.ShapeDtypeStruct((s, d), BF16), jax.ShapeDtypeStruct((s, d), BF16),
                             jax.ShapeDtypeStruct((1, d), F32)],
                  compiler_params=_params(("arbitrary",)))(h2, gl, pp, tgt, g_final)


def _low_half():
    return lax.broadcasted_iota(jnp.int32, (1, LANES), 1) < HEAD_DIM


def _half_mean(v, low):
    s_lo = jnp.sum(jnp.where(low, v, 0.0), axis=-1, keepdims=True)
    s_hi = jnp.sum(jnp.where(low, 0.0, v), axis=-1, keepdims=True)
    return jnp.where(low, s_lo, s_hi) * (1.0 / HEAD_DIM)


def _head_norm_bwd(val, dout, g, low):
    r = lax.rsqrt(_half_mean(val * val, low) + EPS)
    vn = val * r
    dvn = dout * g
    return r * (dvn - vn * _half_mean(dvn * vn, low)), dout * vn


def _conv_taps(vv_ext, w_ref, rows):
    v0 = vv_ext[HALO:]
    v1 = pltpu.roll(vv_ext, 1, 0)[HALO:]
    v2 = pltpu.roll(vv_ext, 2, 0)[HALO:]
    del rows
    return w_ref[2:3, :] * v0 + w_ref[1:2, :] * v1 + w_ref[0:1, :] * v2, (v0, v1, v2)


def _conv_fwd(proj, conv_w, g_conv, w_conv, d_model, tr=512):
    s = proj.shape[0]
    nb = w_conv // LANES
    tr = _tile(s, tr)
    hb = tr // HALO

    def main(col0):
        return pl.BlockSpec((tr, LANES), lambda j, i: (i, col0 + j))

    def prev(col0):
        return pl.BlockSpec((HALO, LANES), lambda j, i: (jnp.maximum(i * hb - 1, 0), col0 + j))

    def body(cb_ref, cc_ref, cu_ref, ccp_ref, cup_ref, w_ref, g_ref, o_ref):
        i = pl.program_id(1)
        vv_prev = jnp.where(i > 0, ccp_ref[...] * cup_ref[...], 0.0)
        vv_ext = jnp.concatenate([vv_prev, cc_ref[...] * cu_ref[...]], axis=0)
        y, _ = _conv_taps(vv_ext, w_ref, tr)
        co = cb_ref[...] * y
        low = _low_half()
        r = lax.rsqrt(_half_mean(co * co, low) + EPS)
        o_ref[...] = (co * r * g_ref[...]).astype(BF16)

    return _pcall(
        body, name="conv_fwd", grid=(nb, s // tr),
        in_specs=[main(0), main(nb), main(2 * nb), prev(nb), prev(2 * nb),
                  pl.BlockSpec((CONV_K, LANES), lambda j, i: (0, j)),
                  pl.BlockSpec((1, LANES), lambda j, i: (0, j))],
        out_specs=pl.BlockSpec((tr, LANES), lambda j, i: (i, j)),
        out_shape=jax.ShapeDtypeStruct((s, d_model), BF16),
        compiler_params=_params(("parallel", "parallel")),
    )(proj, proj, proj, proj, proj, conv_w, g_conv)


def _conv_bwd(proj, dcat, conv_w, g_conv, dproj, w_conv, tr=512):
    s = proj.shape[0]
    nb = w_conv // LANES
    tr = _tile(s, tr)
    hb = tr // HALO
    last = s // HALO - 1
    nt = s // tr

    def main(col0):
        return pl.BlockSpec((tr, LANES), lambda j, i, w: (i, col0 + j))

    def prev(col0):
        return pl.BlockSpec((HALO, LANES), lambda j, i, w: (jnp.maximum(i * hb - 1, 0), col0 + j))

    def nxt(col0):
        return pl.BlockSpec((HALO, LANES), lambda j, i, w: (jnp.minimum((i + 1) * hb, last), col0 + j))

    def body(cb_ref, cc_ref, cu_ref, dc_ref, ccp_ref, cup_ref, cbn_ref, ccn_ref, cun_ref, dcn_ref,
             w_ref, g_ref, dproj_in, dproj_ref, dw_ref, dg_ref, stash):
        del dproj_in
        i = pl.program_id(1)
        which = pl.program_id(2)

        @pl.when(which == 0)
        def _():
            @pl.when(i == 0)
            def _():
                dw_ref[...] = jnp.zeros_like(dw_ref)
                dg_ref[...] = jnp.zeros_like(dg_ref)

            low = _low_half()
            gv = g_ref[...]
            cc, cu = cc_ref[...], cu_ref[...]
            vv_prev = jnp.where(i > 0, ccp_ref[...] * cup_ref[...], 0.0)
            vv_ext = jnp.concatenate([vv_prev, cc * cu, ccn_ref[...] * cun_ref[...]], axis=0)
            y_ext, (v0, v1, v2) = _conv_taps(vv_ext, w_ref, tr + HALO)
            cb_ext = jnp.concatenate([cb_ref[...], cbn_ref[...]], axis=0)
            dc_ext = jnp.concatenate([dc_ref[...], dcn_ref[...]], axis=0)
            dco, dgn = _head_norm_bwd(cb_ext * y_ext, dc_ext, gv, low)
            rowid = lax.broadcasted_iota(jnp.int32, (tr + HALO, 1), 0)
            dyc = jnp.where((rowid < tr) | (i < nt - 1), dco * cb_ext, 0.0)
            n_ext = tr + HALO
            dvv = (w_ref[2:3, :] * dyc[:tr] + w_ref[1:2, :] * pltpu.roll(dyc, n_ext - 1, 0)[:tr]
                   + w_ref[0:1, :] * pltpu.roll(dyc, n_ext - 2, 0)[:tr])
            stash[0] = (dco[:tr] * y_ext[:tr]).astype(BF16)
            stash[1] = (dvv * cu).astype(BF16)
            stash[2] = (dvv * cc).astype(BF16)
            dyt = dyc[:tr]
            for tap, shifted in enumerate((v2, v1, v0)):
                dw_ref[tap:tap + 1, :] += jnp.sum(dyt * shifted[:tr], axis=0, keepdims=True)
            dg_ref[...] += jnp.sum(dgn[:tr], axis=0, keepdims=True)

        dproj_ref[...] = stash[which]

    n_cols = dproj.shape[1]
    return _pcall(
        body, name="conv_bwd", grid=(nb, nt, 3),
        in_specs=[main(0), main(nb), main(2 * nb), main(0),
                  prev(nb), prev(2 * nb), nxt(0), nxt(nb), nxt(2 * nb), nxt(0),
                  pl.BlockSpec((CONV_K, LANES), lambda j, i, w: (0, j)),
                  pl.BlockSpec((1, LANES), lambda j, i, w: (0, j)),
                  pl.BlockSpec(memory_space=pl.ANY)],
        out_specs=[pl.BlockSpec((tr, LANES), lambda j, i, w: (i, w * nb + j)),
                   pl.BlockSpec((CONV_K, LANES), lambda j, i, w: (0, j)),
                   pl.BlockSpec((1, LANES), lambda j, i, w: (0, j))],
        out_shape=[jax.ShapeDtypeStruct((s, n_cols), BF16),
                   jax.ShapeDtypeStruct((CONV_K, w_conv), F32),
                   jax.ShapeDtypeStruct((1, w_conv), F32)],
        scratch_shapes=[pltpu.VMEM((3, tr, LANES), BF16)],
        input_output_aliases={12: 0},
        compiler_params=_params(("parallel", "arbitrary", "arbitrary")),
    )(proj, proj, proj, dcat, proj, proj, proj, proj, proj, dcat, conv_w, g_conv, dproj)


def _split_dot(val, tri):
    hi = val.astype(BF16)
    lo = (val - hi.astype(F32)).astype(BF16)
    return (jnp.dot(hi, tri, preferred_element_type=F32) + jnp.dot(lo, tri, preferred_element_type=F32))


def _attn_masks(t):
    r = lax.broadcasted_iota(jnp.int32, (t, t), 0)
    c = lax.broadcasted_iota(jnp.int32, (t, t), 1)
    return (r > c), jnp.where(r > c, 1.0, 0.0).astype(BF16)


def _sb_block(qm, ks, run, causal, strict):
    z = lax.dot_general(qm, ks, _NT, preferred_element_type=F32)
    e = jnp.exp(-jnp.abs(z))
    log_keep = jnp.minimum(-z, 0.0) - jnp.log(1.0 + e)
    if causal is not None:
        log_keep = jnp.where(causal, log_keep, 0.0)
    suffix = _split_dot(log_keep, strict)
    total = suffix[:, 0:1] + log_keep[:, 0:1]
    a = jnp.exp(z + log_keep + suffix + run)
    if causal is not None:
        a = jnp.where(causal, a, 0.0)
    return z, e, a, total


def _attn_fwd(proj, g_attn, cat, w_conv, t=ATTN_BLOCK):
    s = proj.shape[0]
    w_attn = g_attn.shape[1]
    nh = w_attn // LANES
    t = _tile(s, t)
    nq = s // t
    q0 = 3 * w_conv // LANES
    scale = HEAD_DIM ** -0.5

    def body(q_ref, k_ref, v_ref, g_ref, cat_in, o_ref, cat_ref, kb, vb):
        del cat_in
        qi = pl.program_id(1)

        @pl.when(qi == 0)
        def _():
            kb[...] = k_ref[...].astype(BF16)
            vb[...] = v_ref[...].astype(BF16)

        low = _low_half()
        causal, strict = _attn_masks(t)
        q = q_ref[...] * scale
        halves = []
        for msk in (low, jnp.logical_not(low)):
            qm = jnp.where(msk, q, 0.0).astype(BF16)

            def step(kblk, run, acc, causal_mask, qm=qm):
                rows = pl.ds(pl.multiple_of(kblk * t, t), t)
                _, _, a, total = _sb_block(qm, kb[rows, :], run, causal_mask, strict)
                acc = acc + jnp.dot(a.astype(BF16), vb[rows, :], preferred_element_type=F32)
                return run + total, acc

            run, acc = step(qi, jnp.zeros((t, 1), F32), jnp.zeros((t, LANES), F32), causal)

            def loop(it, carry, step=step):
                return step(qi - 1 - it, carry[0], carry[1], None)

            run, acc = lax.fori_loop(0, qi, loop, (run, acc))
            halves.append(acc)
        o = jnp.where(low, halves[0], halves[1])
        o_ref[...] = o
        r = lax.rsqrt(_half_mean(o * o, low) + EPS)
        cat_ref[...] = (o * r * g_ref[...]).astype(BF16)

    whole = lambda col0: pl.BlockSpec((s, LANES), lambda h, i: (0, col0 + h))
    return _pcall(
        body, name="attn_fwd", grid=(nh, nq),
        in_specs=[pl.BlockSpec((t, LANES), lambda h, i: (i, q0 + h)),
                  whole(q0 + nh), whole(q0 + 2 * nh),
                  pl.BlockSpec((1, LANES), lambda h, i: (0, h)),
                  pl.BlockSpec(memory_space=pl.ANY)],
        out_specs=[pl.BlockSpec((t, LANES), lambda h, i: (i, h)),
                   pl.BlockSpec((t, LANES), lambda h, i: (i, w_conv // LANES + h))],
        out_shape=[jax.ShapeDtypeStruct((s, w_attn), F32),
                   jax.ShapeDtypeStruct(cat.shape, BF16)],
        scratch_shapes=[pltpu.VMEM((s, LANES), BF16), pltpu.VMEM((s, LANES), BF16)],
        input_output_aliases={4: 1},
        compiler_params=_params(("parallel", "arbitrary")),
    )(proj, proj, proj, g_attn, cat)


def _attn_bwd(proj, o, dcat, g_attn, w_conv, t=ATTN_BLOCK):
    s, n_cols = proj.shape
    w_attn = g_attn.shape[1]
    nh = w_attn // LANES
    t = _tile(s, t)
    nq = s // t
    q0 = 3 * w_conv // LANES
    scale = HEAD_DIM ** -0.5

    def body(q_ref, k_ref, v_ref, o_ref, do_ref, g_ref, dproj_ref, dg_ref, kb, vb, dk_acc, dv_acc, stash):
        step_i = pl.program_id(1)
        which = pl.program_id(2)
        qi = nq - 1 - step_i

        @pl.when(which == 0)
        def _():
            @pl.when(step_i == 0)
            def _():
                kb[...] = k_ref[...].astype(BF16)
                vb[...] = v_ref[...].astype(BF16)
                dk_acc[...] = jnp.zeros_like(dk_acc)
                dv_acc[...] = jnp.zeros_like(dv_acc)
                dg_ref[...] = jnp.zeros_like(dg_ref)

            low = _low_half()
            causal, strict = _attn_masks(t)
            q = q_ref[...] * scale
            ov = o_ref[...]
            d_o, dgn = _head_norm_bwd(ov, do_ref[...], g_ref[...], low)
            dg_ref[...] += jnp.sum(dgn, axis=0, keepdims=True)
            halves = []
            for msk in (low, jnp.logical_not(low)):
                qm = jnp.where(msk, q, 0.0).astype(BF16)
                dom = jnp.where(msk, d_o, 0.0).astype(BF16)
                row_total = jnp.sum(dom.astype(F32) * ov, axis=-1, keepdims=True)

                def step(kblk, run, grun, dq, causal_mask, qm=qm, dom=dom, row_total=row_total):
                    rows = pl.ds(pl.multiple_of(kblk * t, t), t)
                    ks, vs = kb[rows, :], vb[rows, :]
                    z, e, a, total = _sb_block(qm, ks, run, causal_mask, strict)
                    da = lax.dot_general(dom, vs, _NT, preferred_element_type=F32)
                    ab = a.astype(BF16)
                    glog = ab.astype(F32) * da
                    later = _split_dot(glog, strict) + glog + grun
                    before = row_total - later
                    inv = 1.0 / (1.0 + e)
                    beta = jnp.where(z >= 0.0, 1.0, e) * inv
                    keep = jnp.where(z >= 0.0, e, 1.0) * inv
                    dz = glog * keep - before * beta
                    if causal_mask is not None:
                        dz = jnp.where(causal_mask, dz, 0.0)
                    dzb = dz.astype(BF16)
                    dq = dq + jnp.dot(dzb, ks, preferred_element_type=F32)
                    dk_acc[rows, :] += lax.dot_general(dzb, qm, _TN, preferred_element_type=F32)
                    dv_acc[rows, :] += lax.dot_general(ab, dom, _TN, preferred_element_type=F32)
                    return run + total, later[:, 0:1], dq

                zero = jnp.zeros((t, 1), F32)
                carry = step(qi, zero, zero, jnp.zeros((t, LANES), F32), causal)

                def loop(it, c, step=step):
                    return step(qi - 1 - it, c[0], c[1], c[2], None)

                carry = lax.fori_loop(0, qi, loop, carry)
                halves.append(carry[2])
            rows = pl.ds(pl.multiple_of(qi * t, t), t)
            stash[0] = (jnp.where(low, halves[0], halves[1]) * scale).astype(BF16)
            stash[1] = dk_acc[rows, :].astype(BF16)
            stash[2] = dv_acc[rows, :].astype(BF16)

        dproj_ref[...] = stash[which]

    whole = lambda col0: pl.BlockSpec((s, LANES), lambda h, i, w: (0, col0 + h))
    blk = lambda col0: pl.BlockSpec((t, LANES), lambda h, i, w: (nq - 1 - i, col0 + h))
    return _pcall(
        body, name="attn_bwd", grid=(nh, nq, 3),
        in_specs=[blk(q0), whole(q0 + nh), whole(q0 + 2 * nh), blk(0), blk(w_conv // LANES),
                  pl.BlockSpec((1, LANES), lambda h, i, w: (0, h))],
        out_specs=[pl.BlockSpec((t, LANES), lambda h, i, w: (nq - 1 - i, q0 + w * nh + h)),
                   pl.BlockSpec((1, LANES), lambda h, i, w: (0, h))],
        out_shape=[jax.ShapeDtypeStruct((s, n_cols), BF16), jax.ShapeDtypeStruct((1, w_attn), F32)],
        scratch_shapes=[pltpu.VMEM((s, LANES), BF16), pltpu.VMEM((s, LANES), BF16),
                        pltpu.VMEM((s, LANES), F32), pltpu.VMEM((s, LANES), F32),
                        pltpu.VMEM((3, t, LANES), BF16)],
        compiler_params=_params(("parallel", "arbitrary", "arbitrary")),
    )(proj, proj, proj, o, dcat, g_attn)


def _place():
    return lax.axis_index("x"), lax.axis_index("y"), lax.axis_index("c")


def _other_chips(x, y):
    return [(1 - x, y), (x, 1 - y), (1 - x, 1 - y)]


def _slot(px, py, pc):
    return 4 * px + 2 * py + pc


def _all_gather(shards, out_dtypes):
    nw = len(shards)

    def body(*refs):
        ins, outs, stage = refs[:nw], refs[nw:2 * nw], refs[2 * nw:3 * nw]
        send_sems, recv_sems, local_sems = refs[3 * nw:]
        x, y, c = _place()
        me, sibling = (x, y, c), (x, y, 1 - c)
        chips = _other_chips(x, y)

        def copy(w, k, block, to, src=None):
            dst = outs[w].at[_slot(*block)]
            return pltpu.make_async_remote_copy(
                src_ref=dst if src is None else src, dst_ref=dst,
                send_sem=send_sems.at[w * 7 + k], recv_sem=recv_sems.at[w * 7 + k],
                device_id=to, device_id_type=MESH)

        started = []
        local = []
        for w in range(nw):
            stage[w][...] = ins[w][...].astype(stage[w].dtype)
            cp = pltpu.make_async_copy(stage[w], outs[w].at[_slot(*me)], local_sems.at[w])
            cp.start()
            local.append(cp)
            started.append(copy(w, 0, me, sibling, src=stage[w]))
            started[-1].start()
            for j, chip in enumerate(chips):
                started.append(copy(w, 1 + j, me, (*chip, c), src=stage[w]))
                started[-1].start()
        for j, chip in enumerate(chips):
            for w in range(nw):
                copy(w, 1 + j, (*chip, c), me).wait_recv()
                started.append(copy(w, 4 + j, (*chip, c), sibling))
                started[-1].start()
        for w in range(nw):
            copy(w, 0, sibling, me).wait_recv()
            for j, chip in enumerate(chips):
                copy(w, 4 + j, (*chip, 1 - c), me).wait_recv()
        for cp in started:
            cp.wait_send()
        for cp in local:
            cp.wait()

    return _pcall(
        body, name="all_gather_weights",
        in_specs=[pl.BlockSpec(memory_space=pltpu.VMEM)] * nw,
        out_specs=[pl.BlockSpec(memory_space=pl.ANY)] * nw,
        out_shape=[jax.ShapeDtypeStruct((N_DEV, *a.shape), d) for a, d in zip(shards, out_dtypes)],
        scratch_shapes=[pltpu.VMEM(a.shape, d) for a, d in zip(shards, out_dtypes)]
        + [pltpu.SemaphoreType.DMA((7 * nw,)), pltpu.SemaphoreType.DMA((7 * nw,)),
           pltpu.SemaphoreType.DMA((nw,))],
        compiler_params=_params(),
    )(*shards)


def _pair_exchange(partials):
    nw = len(partials)

    def body(*refs):
        ins, outs = refs[:nw], refs[nw:2 * nw]
        send_sems, recv_sems = refs[2 * nw:]
        x, y, c = _place()
        started = []
        for w in range(nw):
            for q in range(4):
                cp = pltpu.make_async_remote_copy(
                    src_ref=ins[w].at[2 * q + (1 - c)], dst_ref=outs[w].at[q],
                    send_sem=send_sems.at[w * 4 + q], recv_sem=recv_sems.at[w * 4 + q],
                    device_id=(x, y, 1 - c), device_id_type=MESH)
                cp.start()
                started.append(cp)
        for cp in started:
            cp.wait()

    return _pcall(
        body, name="grad_pair_exchange",
        in_specs=[pl.BlockSpec(memory_space=pl.ANY)] * nw,
        out_specs=[pl.BlockSpec(memory_space=pl.ANY)] * nw,
        out_shape=[jax.ShapeDtypeStruct((4, *a.shape[1:]), F32) for a in partials],
        scratch_shapes=[pltpu.SemaphoreType.DMA((4 * nw,)), pltpu.SemaphoreType.DMA((4 * nw,))],
        compiler_params=_params(),
    )(*partials)


def _chip_exchange(sums):
    nw = len(sums)

    def body(*refs):
        ins, outs = refs[:nw], refs[nw:2 * nw]
        send_sems, recv_sems = refs[2 * nw:]
        x, y, c = _place()
        started = []
        for w in range(nw):
            for j, (px, py) in enumerate(_other_chips(x, y)):
                cp = pltpu.make_async_remote_copy(
                    src_ref=ins[w].at[2 * px + py], dst_ref=outs[w].at[j],
                    send_sem=send_sems.at[w * 3 + j], recv_sem=recv_sems.at[w * 3 + j],
                    device_id=(px, py, c), device_id_type=MESH)
                cp.start()
                started.append(cp)
        for cp in started:
            cp.wait()

    return _pcall(
        body, name="grad_chip_exchange",
        in_specs=[pl.BlockSpec(memory_space=pl.ANY)] * nw,
        out_specs=[pl.BlockSpec(memory_space=pl.ANY)] * nw,
        out_shape=[jax.ShapeDtypeStruct((3, *a.shape[1:]), BF16) for a in sums],
        scratch_shapes=[pltpu.SemaphoreType.DMA((3 * nw,)), pltpu.SemaphoreType.DMA((3 * nw,))],
        compiler_params=_params(),
    )(*sums)


def _all_reduce_small(packed):
    r = packed.shape[0]

    def body(x_ref, o_ref, gathered, send_sems, recv_sems):
        x, y, c = _place()
        me = _slot(x, y, c)
        gathered[me] = x_ref[...]
        peers = [(px, py, pc) for px in range(2) for py in range(2) for pc in range(2)]
        started = []
        for k in range(1, N_DEV):
            to = (x ^ (k >> 2), y ^ ((k >> 1) & 1), c ^ (k & 1))
            cp = pltpu.make_async_remote_copy(
                src_ref=x_ref, dst_ref=gathered.at[me],
                send_sem=send_sems.at[k - 1], recv_sem=recv_sems.at[k - 1],
                device_id=to, device_id_type=MESH)
            cp.start()
            started.append(cp)
        del peers
        for cp in started:
            cp.wait()
        total = gathered[0]
        for k in range(1, N_DEV):
            total = total + gathered[k]
        o_ref[...] = total

    return _pcall(
        body, name="all_reduce_small",
        in_specs=[pl.BlockSpec(memory_space=pltpu.VMEM)],
        out_specs=pl.BlockSpec(memory_space=pltpu.VMEM),
        out_shape=jax.ShapeDtypeStruct(packed.shape, F32),
        scratch_shapes=[pltpu.VMEM((N_DEV, r, LANES), F32),
                        pltpu.SemaphoreType.DMA((N_DEV - 1,)), pltpu.SemaphoreType.DMA((N_DEV - 1,))],
        compiler_params=_params(),
    )(packed)


def _pair_sum(name, partial, received, place, tr=256):
    _, r, cdim = partial.shape
    tr = _tile(r, tr) if r % LANES == 0 else r

    def body(place_ref, p_ref, a_ref, of_ref, ob_ref):
        del place_ref
        tot = p_ref[...] + a_ref[...]
        of_ref[...] = tot
        ob_ref[...] = tot.astype(BF16)

    blk = pl.BlockSpec((None, tr, cdim), lambda q, i, pr: (q, i, 0))
    grid_spec = pltpu.PrefetchScalarGridSpec(
        num_scalar_prefetch=1, grid=(4, r // tr),
        in_specs=[pl.BlockSpec((None, tr, cdim), lambda q, i, pr: (2 * q + pr[2], i, 0)), blk],
        out_specs=[blk, blk])
    return _pcall(body, name=name, grid_spec=grid_spec,
                  out_shape=[jax.ShapeDtypeStruct((4, r, cdim), F32), jax.ShapeDtypeStruct((4, r, cdim), BF16)],
                  compiler_params=_params(("parallel", "parallel")))(place, partial, received)


def _adam_math(w, g, m, v):
    m = ADAM_B1 * m + (1.0 - ADAM_B1) * g
    v = ADAM_B2 * v + (1.0 - ADAM_B2) * jnp.square(g)
    m_hat = m / (1.0 - ADAM_B1 ** ADAM_STEP)
    v_hat = v / (1.0 - ADAM_B2 ** ADAM_STEP)
    delta = -ADAM_LR * (m_hat / (jnp.sqrt(v_hat) + ADAM_EPS) + ADAM_WD * w)
    return delta, m, v


def _adam_sharded(name, own, received, w, m, v, place, tr=256):
    r, cdim = w.shape
    tr = _tile(r, tr) if r % LANES == 0 else r

    def body(place_ref, own_ref, rec_ref, w_ref, m_ref, v_ref, g_ref, d_ref, nm_ref, nv_ref):
        del place_ref
        g = own_ref[...]
        for j in range(3):
            g = g + rec_ref[j].astype(F32)
        delta, nm, nv = _adam_math(w_ref[...], g, m_ref[...], v_ref[...])
        g_ref[...] = g
        d_ref[...] = delta
        nm_ref[...] = nm
        nv_ref[...] = nv

    blk = pl.BlockSpec((tr, cdim), lambda i, pr: (i, 0))
    grid_spec = pltpu.PrefetchScalarGridSpec(
        num_scalar_prefetch=1, grid=(r // tr,),
        in_specs=[pl.BlockSpec((None, tr, cdim), lambda i, pr: (2 * pr[0] + pr[1], i, 0)),
                  pl.BlockSpec((3, tr, cdim), lambda i, pr: (0, i, 0)), blk, blk, blk],
        out_specs=[blk] * 4)
    return _pcall(body, name=name, grid_spec=grid_spec,
                  out_shape=[jax.ShapeDtypeStruct((r, cdim), F32)] * 4,
                  compiler_params=_params(("parallel",)))(place, own, received, w, m, v)


def _adam_small(w, g, m, v):
    def body(w_ref, g_ref, m_ref, v_ref, d_ref, nm_ref, nv_ref):
        delta, nm, nv = _adam_math(w_ref[...], g_ref[...], m_ref[...], v_ref[...])
        d_ref[...] = delta
        nm_ref[...] = nm
        nv_ref[...] = nv

    return _pcall(body, name="adam_small",
                  in_specs=[pl.BlockSpec(memory_space=pltpu.VMEM)] * 4,
                  out_specs=[pl.BlockSpec(memory_space=pltpu.VMEM)] * 3,
                  out_shape=[jax.ShapeDtypeStruct(w.shape, F32)] * 3,
                  compiler_params=_params())(w, g, m, v)


def _rows(vec):
    return vec.reshape(-1, LANES)


def kernel(x, p, g_mix, w_in, conv_w, g_conv_out, g_attn_out, w_out, g_mlp, w_up, w_down, g_ple, w_ple_gate, w_ple_proj, g_final, loss_target, m_g_mix, m_w_in, m_conv_w, m_g_conv_out, m_g_attn_out, m_w_out, m_g_mlp, m_w_up, m_w_down, m_g_ple, m_w_ple_gate, m_w_ple_proj, m_g_final, v_g_mix, v_w_in, v_conv_w, v_g_conv_out, v_g_attn_out, v_w_out, v_g_mlp, v_w_up, v_w_down, v_g_ple, v_w_ple_gate, v_w_ple_proj, v_g_final):
    s, d = x.shape[1], x.shape[2]
    w_conv = g_conv_out.shape[1]
    w_attn = g_attn_out.shape[1]
    cw = conv_w.shape[2]
    xs, ps, tgt = x[0], p[0, 0], loss_target[0]
    place = jnp.stack([lax.axis_index("x"), lax.axis_index("y"), lax.axis_index("c")]).astype(jnp.int32)
    my_slot = 4 * place[0] + 2 * place[1] + place[2]

    conv_tile = jnp.pad(conv_w[0], ((0, HALO - CONV_K), (0, LANES - cw)))
    big = [w_in[0], w_out[0], w_up[0], w_down[0], w_ple_gate[0], w_ple_proj[0]]
    gathered = _all_gather(big + [conv_tile], [BF16] * 6 + [F32])
    win_g, wout_g, wup_g, wdown_g, wgate_g, wproj_g, conv_g = gathered
    wout_f = wout_g.reshape(-1, wout_g.shape[-1])
    wdown_f = wdown_g.reshape(-1, wdown_g.shape[-1])
    wgate_f = wgate_g.reshape(-1, wgate_g.shape[-1])
    conv_full = jnp.transpose(conv_g[:, :CONV_K, :cw], (1, 0, 2)).reshape(CONV_K, w_conv)
    in_shard, up_shard, proj_shard = win_g.shape[2], wup_g.shape[2], wproj_g.shape[2]

    a = _rmsnorm_fwd("norm_mix", xs, g_mix)
    proj, = _mm_nn("in_proj", a, win_g, n_shard=in_shard, tn=in_shard)
    cat = _conv_fwd(proj, conv_full, g_conv_out, w_conv, d)
    o, cat = _attn_fwd(proj, g_attn_out, cat, w_conv)
    h1, = _mm_nn("out_proj", cat, wout_f, epilogue=_ep_residual, extras=(xs,))
    mn = _rmsnorm_fwd("norm_mlp", h1, g_mlp)
    u, act = _mm_nn("mlp_up", mn, wup_g, n_shard=up_shard, epilogue=_ep_up, out_dtypes=(F32, BF16))
    h2, = _mm_nn("mlp_down", act, wdown_f, epilogue=_ep_residual, extras=(h1,))
    n3 = _rmsnorm_fwd("norm_ple", h2, g_ple)
    gl, = _mm_nn("ple_gate", n3, wgate_f)
    pp, = _mm_nn("ple_proj", ps, wproj_g, n_shard=proj_shard)
    loss_part, dh3, dgl, dpp, dg_final = _ple_loss(h2, gl, pp, tgt, g_final.reshape(1, d))
    loss = lax.psum(loss_part[0, 0], ("x", "y", "c"))

    dw_proj = _mm_tn("d_w_ple_proj", ps, dpp, n_shard=proj_shard)
    dw_gate = _mm_tn("d_w_ple_gate", n3, dgl)
    dn3, = _mm_nt("d_norm_ple", dgl, wgate_f)
    dh2, dh2b, dg_ple = _rmsnorm_bwd("norm_ple_bwd", dn3, h2, g_ple, dh3)
    du, = _mm_nt("d_mlp_act", dh2b, wdown_f, epilogue=_ep_dact, out_dtypes=(BF16,), extras=(u,))
    dw_down = _mm_tn("d_w_down", act, dh2b)
    dw_up = _mm_tn("d_w_up", mn, du, n_shard=up_shard)
    dmn, = _mm_nt("d_norm_mlp", du, wup_g, k_shard=up_shard)
    dh1, dh1b, dg_mlp = _rmsnorm_bwd("norm_mlp_bwd", dmn, h1, g_mlp, dh2)
    dcat, = _mm_nt("d_cat", dh1b, wout_f)
    dw_out = _mm_tn("d_w_out", cat, dh1b)
    dproj, dg_attn = _attn_bwd(proj, o, dcat, g_attn_out, w_conv)
    dproj, dconv, dg_conv = _conv_bwd(proj, dcat, conv_full, g_conv_out, dproj, w_conv)
    dw_in = _mm_tn("d_w_in", a, dproj, n_shard=in_shard, tn=in_shard)
    da, = _mm_nt("d_norm_mix", dproj, win_g, k_shard=in_shard, tk=in_shard)
    grad_x, _, dg_mix = _rmsnorm_bwd("norm_mix_bwd", da, xs, g_mix, dh1)

    partials = [dw_in,
                dw_out.reshape(N_DEV, -1, dw_out.shape[-1]),
                dw_up,
                dw_down.reshape(N_DEV, -1, dw_down.shape[-1]),
                dw_gate.reshape(N_DEV, -1, dw_gate.shape[-1]),
                dw_proj]
    names = ["w_in", "w_out", "w_up", "w_down", "w_ple_gate", "w_ple_proj"]
    from_sibling = _pair_exchange(partials)
    sums = [_pair_sum("pair_sum_" + n, pa, rc, place) for n, pa, rc in zip(names, partials, from_sibling)]
    from_chips = _chip_exchange([sb for _, sb in sums])
    moments = [(m_w_in, v_w_in), (m_w_out, v_w_out), (m_w_up, v_w_up), (m_w_down, v_w_down),
               (m_w_ple_gate, v_w_ple_gate), (m_w_ple_proj, v_w_ple_proj)]
    big_out = {}
    for n, (sf, _), rc, wt, (mm, vv) in zip(names, sums, from_chips, big, moments):
        big_out[n] = [t[None] for t in _adam_sharded("adam_" + n, sf, rc, wt, mm[0], vv[0], place)]

    small_g = jnp.concatenate(
        [_rows(dg_mix[0]), _rows(dg_conv[0]), _rows(dg_attn[0]), _rows(dg_mlp[0]), _rows(dg_ple[0]),
         _rows(dg_final[0]), _rows(dconv.reshape(-1))], axis=0)
    n_gain_rows = small_g.shape[0] - CONV_K * w_conv // LANES
    pad_rows = (-small_g.shape[0]) % HALO
    small_g = _all_reduce_small(jnp.pad(small_g, ((0, pad_rows), (0, 0))))
    dconv_full = small_g[n_gain_rows:n_gain_rows + CONV_K * w_conv // LANES].reshape(CONV_K, w_conv)
    dconv_mine = lax.dynamic_slice(dconv_full, (0, my_slot * cw), (CONV_K, cw))

    def pack(vecs, conv_part):
        rows = [_rows(t.reshape(-1)) for t in vecs]
        rows.append(jnp.pad(conv_part, ((0, HALO - CONV_K), (0, LANES - cw))))
        return jnp.concatenate(rows, axis=0)

    gains = [g_mix, g_conv_out, g_attn_out, g_mlp, g_ple, g_final]
    gains_m = [m_g_mix, m_g_conv_out, m_g_attn_out, m_g_mlp, m_g_ple, m_g_final]
    gains_v = [v_g_mix, v_g_conv_out, v_g_attn_out, v_g_mlp, v_g_ple, v_g_final]
    gpack = jnp.concatenate([small_g[:n_gain_rows], jnp.pad(dconv_mine, ((0, HALO - CONV_K), (0, LANES - cw)))], axis=0)
    sd, sm, sv = _adam_small(pack(gains, conv_w[0]), gpack, pack(gains_m, m_conv_w[0]), pack(gains_v, v_conv_w[0]))

    def unpack(packed):
        out, r0 = [], 0
        for t in gains:
            nr = t.size // LANES
            out.append(packed[r0:r0 + nr].reshape(t.shape))
            r0 += nr
        out.append(packed[r0:r0 + CONV_K, :cw][None])
        return out

    sg_l, sd_l, sm_l, sv_l = unpack(gpack), unpack(sd), unpack(sm), unpack(sv)
    small_names = ["g_mix", "g_conv_out", "g_attn_out", "g_mlp", "g_ple", "g_final", "conv_w"]
    small_out = {n: [sg_l[i], sd_l[i], sm_l[i], sv_l[i]] for i, n in enumerate(small_names)}

    order = ["g_mix", "w_in", "conv_w", "g_conv_out", "g_attn_out", "w_out", "g_mlp", "w_up", "w_down",
             "g_ple", "w_ple_gate", "w_ple_proj", "g_final"]
    table = {**big_out, **small_out}
    outs = [loss, grad_x[None]]
    for kind in range(4):
        outs.extend(table[n][kind] for n in order)
    return tuple(outs)
```

```python
import functools

import jax
import jax.numpy as jnp
from jax import lax
from jax.experimental import pallas as pl
from jax.experimental.pallas import tpu as pltpu

F32 = jnp.float32
BF16 = jnp.bfloat16
EPS = 1e-6
HEAD_DIM = 64
LANES = 128
CONV_K = 3
ATTN_BLOCK = 256
HALO = 8
N_DEV = 8
MESH = pl.DeviceIdType.MESH
VMEM_LIMIT = 56 * 1024 * 1024

ADAM_LR = 0.001
ADAM_B1 = 0.9
ADAM_B2 = 0.999
ADAM_EPS = 1e-08
ADAM_WD = 0.01
ADAM_STEP = 10


def _pcall(body, **kw):
    return pl.pallas_call(body, **kw)


def _params(sem=None, **kw):
    return pltpu.CompilerParams(dimension_semantics=sem, vmem_limit_bytes=VMEM_LIMIT, **kw)


def _tile(dim, pref):
    t = min(dim, pref)
    while dim % t:
        t -= LANES
    assert t > 0, (dim, pref)
    return t


_NN = (((1,), (0,)), ((), ()))
_NT = (((1,), (1,)), ((), ()))
_TN = (((0,), (0,)), ((), ()))


def _ep_store(acc, outs):
    outs[0][...] = acc.astype(outs[0].dtype)


def _ep_residual(acc, res, outs):
    outs[0][...] = acc + res[...]


def _ep_up(acc, outs):
    outs[0][...] = acc
    outs[1][...] = jnp.square(jnp.maximum(acc, 0.0)).astype(BF16)


def _ep_dact(acc, u, outs):
    outs[0][...] = (acc * (2.0 * jnp.maximum(u[...], 0.0))).astype(BF16)


def _matmul(name, a, b, *, dims, grid, a_spec, b_spec, acc_shape, out_shapes, out_specs,
            epilogue=_ep_store, extras=(), extra_specs=()):
    nk = grid[2]
    n_ex = len(extras)

    def product(a_ref, b_ref):
        return lax.dot_general(a_ref[...].astype(BF16), b_ref[...].astype(BF16), dims,
                               preferred_element_type=F32)

    def body_one(a_ref, b_ref, *rest):
        epilogue(product(a_ref, b_ref), *rest[:n_ex], rest[n_ex:])

    def body_acc(a_ref, b_ref, *rest):
        ex, outs, acc = rest[:n_ex], rest[n_ex:-1], rest[-1]
        k = pl.program_id(2)

        @pl.when(k == 0)
        def _():
            acc[...] = product(a_ref, b_ref)

        @pl.when(k > 0)
        def _():
            acc[...] += product(a_ref, b_ref)

        @pl.when(k == nk - 1)
        def _():
            epilogue(acc[...], *ex, outs)

    return _pcall(
        body_one if nk == 1 else body_acc, name=name, grid=grid,
        in_specs=[a_spec, b_spec, *extra_specs],
        out_specs=list(out_specs), out_shape=list(out_shapes),
        scratch_shapes=[] if nk == 1 else [pltpu.VMEM(acc_shape, F32)],
        compiler_params=_params(("parallel", "parallel", "arbitrary")),
    )(a, b, *extras)


def _mm_nn(name, a, w, *, n_shard=None, epilogue=_ep_store, out_dtypes=(F32,), extras=(), tm=1024, tn=1024, tk=1024):
    m, kd = a.shape
    if n_shard is None:
        n = w.shape[1]
        tn = _tile(n, tn)
        tk = _tile(kd, tk)
        b_spec = pl.BlockSpec((tk, tn), lambda i, j, k: (k, j))
    else:
        n = N_DEV * n_shard
        tn = _tile(n_shard, tn)
        tk = _tile(kd, tk)
        per = n_shard // tn
        b_spec = pl.BlockSpec((None, tk, tn), lambda i, j, k: (j // per, k, j % per))
    tm = _tile(m, tm)
    o_spec = pl.BlockSpec((tm, tn), lambda i, j, k: (i, j))
    return _matmul(
        name, a, w, dims=_NN, grid=(m // tm, n // tn, kd // tk),
        a_spec=pl.BlockSpec((tm, tk), lambda i, j, k: (i, k)), b_spec=b_spec,
        acc_shape=(tm, tn),
        out_shapes=[jax.ShapeDtypeStruct((m, n), d) for d in out_dtypes],
        out_specs=[o_spec] * len(out_dtypes),
        epilogue=epilogue, extras=extras, extra_specs=[o_spec] * len(extras))


def _mm_nt(name, a, w, *, k_shard=None, epilogue=_ep_store, out_dtypes=(F32,), extras=(), tm=1024, tn=1024, tk=1024):
    m, kd = a.shape
    if k_shard is None:
        n = w.shape[0]
        tn = _tile(n, tn)
        tk = _tile(kd, tk)
        b_spec = pl.BlockSpec((tn, tk), lambda i, j, k: (j, k))
    else:
        n = w.shape[1]
        tn = _tile(n, tn)
        tk = _tile(k_shard, tk)
        per = k_shard // tk
        b_spec = pl.BlockSpec((None, tn, tk), lambda i, j, k: (k // per, j, k % per))
    tm = _tile(m, tm)
    o_spec = pl.BlockSpec((tm, tn), lambda i, j, k: (i, j))
    return _matmul(
        name, a, w, dims=_NT, grid=(m // tm, n // tn, kd // tk),
        a_spec=pl.BlockSpec((tm, tk), lambda i, j, k: (i, k)), b_spec=b_spec,
        acc_shape=(tm, tn),
        out_shapes=[jax.ShapeDtypeStruct((m, n), d) for d in out_dtypes],
        out_specs=[o_spec] * len(out_dtypes),
        epilogue=epilogue, extras=extras, extra_specs=[o_spec] * len(extras))


def _mm_tn(name, a, b, *, n_shard=None, tm=1024, tn=1024, tk=1024):
    t, m = a.shape
    n = b.shape[1]
    tm = _tile(m, tm)
    tk = _tile(t, tk)
    if n_shard is None:
        tn = _tile(n, tn)
        o_spec = pl.BlockSpec((tm, tn), lambda i, j, k: (i, j))
        o_shape = jax.ShapeDtypeStruct((m, n), F32)
    else:
        tn = _tile(n_shard, tn)
        per = n_shard // tn
        o_spec = pl.BlockSpec((None, tm, tn), lambda i, j, k: (j // per, i, j % per))
        o_shape = jax.ShapeDtypeStruct((N_DEV, m, n_shard), F32)
    return _matmul(
        name, a, b, dims=_TN, grid=(m // tm, n // tn, t // tk),
        a_spec=pl.BlockSpec((tk, tm), lambda i, j, k: (k, i)),
        b_spec=pl.BlockSpec((tk, tn), lambda i, j, k: (k, j)),
        acc_shape=(tm, tn), out_shapes=[o_shape], out_specs=[o_spec])[0]


def _row_spec(tr, d):
    return pl.BlockSpec((tr, d), lambda i: (i, 0))


def _vec_spec(d):
    return pl.BlockSpec((1, d), lambda i: (0, 0))


def _rmsnorm_fwd(name, x, g, tr=512):
    s, d = x.shape
    tr = _tile(s, tr)

    def body(x_ref, g_ref, o_ref):
        xv = x_ref[...]
        r = lax.rsqrt(jnp.mean(xv * xv, axis=-1, keepdims=True) + EPS)
        o_ref[...] = (xv * r * g_ref[...]).astype(BF16)

    return _pcall(body, name=name, grid=(s // tr,),
                  in_specs=[_row_spec(tr, d), _vec_spec(d)], out_specs=_row_spec(tr, d),
                  out_shape=jax.ShapeDtypeStruct((s, d), BF16),
                  compiler_params=_params(("parallel",)))(x, g)


def _rmsnorm_bwd(name, dn, h, g, dres, tr=512):
    s, d = h.shape
    tr = _tile(s, tr)

    def body(dn_ref, h_ref, g_ref, dres_ref, dh_ref, dhb_ref, dg_ref):
        @pl.when(pl.program_id(0) == 0)
        def _():
            dg_ref[...] = jnp.zeros_like(dg_ref)

        hv = h_ref[...]
        dnv = dn_ref[...]
        r = lax.rsqrt(jnp.mean(hv * hv, axis=-1, keepdims=True) + EPS)
        hn = hv * r
        dg_ref[...] += jnp.sum(dnv * hn, axis=0, keepdims=True)
        dhn = dnv * g_ref[...]
        dh = dres_ref[...] + r * (dhn - hn * jnp.mean(dhn * hn, axis=-1, keepdims=True))
        dh_ref[...] = dh
        dhb_ref[...] = dh.astype(BF16)

    return _pcall(body, name=name, grid=(s // tr,),
                  in_specs=[_row_spec(tr, d), _row_spec(tr, d), _vec_spec(d), _row_spec(tr, d)],
                  out_specs=[_row_spec(tr, d), _row_spec(tr, d), _vec_spec(d)],
                  out_shape=[jax.ShapeDtypeStruct((s, d), F32), jax.ShapeDtypeStruct((s, d), BF16),
                             jax.ShapeDtypeStruct((1, d), F32)],
                  compiler_params=_params(("arbitrary",)))(dn, h, g, dres)


def _ple_loss(h2, gl, pp, tgt, g_final, tr=512):
    s, d = h2.shape
    tr = _tile(s, tr)

    def body(h2_ref, gl_ref, pp_ref, t_ref, g_ref, loss_ref, dh3_ref, dgl_ref, dpp_ref, dg_ref):
        @pl.when(pl.program_id(0) == 0)
        def _():
            dg_ref[...] = jnp.zeros_like(dg_ref)
            loss_ref[...] = jnp.zeros_like(loss_ref)

        gate = jax.nn.sigmoid(gl_ref[...])
        ppv = pp_ref[...]
        h3 = h2_ref[...] + gate * ppv
        r = lax.rsqrt(jnp.mean(h3 * h3, axis=-1, keepdims=True) + EPS)
        hn = h3 * r
        gv = g_ref[...]
        diff = hn * gv - t_ref[...]
        row = jnp.mean(diff * diff, axis=-1, keepdims=True)
        loss_ref[...] += 0.5 * jnp.sum(row, axis=0, keepdims=True)
        dy = diff * (1.0 / d)
        dg_ref[...] += jnp.sum(dy * hn, axis=0, keepdims=True)
        dhn = dy * gv
        dh3 = r * (dhn - hn * jnp.mean(dhn * hn, axis=-1, keepdims=True))
        dh3_ref[...] = dh3
        dgl_ref[...] = (dh3 * ppv * gate * (1.0 - gate)).astype(BF16)
        dpp_ref[...] = (dh3 * gate).astype(BF16)

    return _pcall(body, name="ple_loss", grid=(s // tr,),
                  in_specs=[_row_spec(tr, d)] * 4 + [_vec_spec(d)],
                  out_specs=[_vec_spec(LANES), _row_spec(tr, d), _row_spec(tr, d), _row_spec(tr, d), _vec_spec(d)],
                  out_shape=[jax.ShapeDtypeStruct((1, LANES), F32), jax.ShapeDtypeStruct((s, d), F32),
                             jax.ShapeDtypeStruct((s, d), BF16), jax.ShapeDtypeStruct((s, d), BF16),
                             jax.ShapeDtypeStruct((1, d), F32)],
                  compiler_params=_params(("arbitrary",)))(h2, gl, pp, tgt, g_final)


def _low_half():
    return lax.broadcasted_iota(jnp.int32, (1, LANES), 1) < HEAD_DIM


def _half_mean(v, low):
    s_lo = jnp.sum(jnp.where(low, v, 0.0), axis=-1, keepdims=True)
    s_hi = jnp.sum(jnp.where(low, 0.0, v), axis=-1, keepdims=True)
    return jnp.where(low, s_lo, s_hi) * (1.0 / HEAD_DIM)


def _head_norm_bwd(val, dout, g, low):
    r = lax.rsqrt(_half_mean(val * val, low) + EPS)
    vn = val * r
    dvn = dout * g
    return r * (dvn - vn * _half_mean(dvn * vn, low)), dout * vn


def _conv_taps(vv_ext, w_ref, rows):
    v0 = vv_ext[HALO:]
    v1 = pltpu.roll(vv_ext, 1, 0)[HALO:]
    v2 = pltpu.roll(vv_ext, 2, 0)[HALO:]
    del rows
    return w_ref[2:3, :] * v0 + w_ref[1:2, :] * v1 + w_ref[0:1, :] * v2, (v0, v1, v2)


def _conv_fwd(proj, conv_w, g_conv, w_conv, d_model, tr=512):
    s = proj.shape[0]
    nb = w_conv // LANES
    tr = _tile(s, tr)
    hb = tr // HALO

    def main(col0):
        return pl.BlockSpec((tr, LANES), lambda j, i: (i, col0 + j))

    def prev(col0):
        return pl.BlockSpec((HALO, LANES), lambda j, i: (jnp.maximum(i * hb - 1, 0), col0 + j))

    def body(cb_ref, cc_ref, cu_ref, ccp_ref, cup_ref, w_ref, g_ref, o_ref):
        i = pl.program_id(1)
        vv_prev = jnp.where(i > 0, ccp_ref[...] * cup_ref[...], 0.0)
        vv_ext = jnp.concatenate([vv_prev, cc_ref[...] * cu_ref[...]], axis=0)
        y, _ = _conv_taps(vv_ext, w_ref, tr)
        co = cb_ref[...] * y
        low = _low_half()
        r = lax.rsqrt(_half_mean(co * co, low) + EPS)
        o_ref[...] = (co * r * g_ref[...]).astype(BF16)

    return _pcall(
        body, name="conv_fwd", grid=(nb, s // tr),
        in_specs=[main(0), main(nb), main(2 * nb), prev(nb), prev(2 * nb),
                  pl.BlockSpec((CONV_K, LANES), lambda j, i: (0, j)),
                  pl.BlockSpec((1, LANES), lambda j, i: (0, j))],
        out_specs=pl.BlockSpec((tr, LANES), lambda j, i: (i, j)),
        out_shape=jax.ShapeDtypeStruct((s, d_model), BF16),
        compiler_params=_params(("parallel", "parallel")),
    )(proj, proj, proj, proj, proj, conv_w, g_conv)


def _conv_bwd(proj, dcat, conv_w, g_conv, dproj, w_conv, tr=512):
    s = proj.shape[0]
    nb = w_conv // LANES
    tr = _tile(s, tr)
    hb = tr // HALO
    last = s // HALO - 1
    nt = s // tr

    def main(col0):
        return pl.BlockSpec((tr, LANES), lambda j, i, w: (i, col0 + j))

    def prev(col0):
        return pl.BlockSpec((HALO, LANES), lambda j, i, w: (jnp.maximum(i * hb - 1, 0), col0 + j))

    def nxt(col0):
        return pl.BlockSpec((HALO, LANES), lambda j, i, w: (jnp.minimum((i + 1) * hb, last), col0 + j))

    def body(cb_ref, cc_ref, cu_ref, dc_ref, ccp_ref, cup_ref, cbn_ref, ccn_ref, cun_ref, dcn_ref,
             w_ref, g_ref, dproj_in, dproj_ref, dw_ref, dg_ref, stash):
        del dproj_in
        i = pl.program_id(1)
        which = pl.program_id(2)

        @pl.when(which == 0)
        def _():
            @pl.when(i == 0)
            def _():
                dw_ref[...] = jnp.zeros_like(dw_ref)
                dg_ref[...] = jnp.zeros_like(dg_ref)

            low = _low_half()
            gv = g_ref[...]
            cc, cu = cc_ref[...], cu_ref[...]
            vv_prev = jnp.where(i > 0, ccp_ref[...] * cup_ref[...], 0.0)
            vv_ext = jnp.concatenate([vv_prev, cc * cu, ccn_ref[...] * cun_ref[...]], axis=0)
            y_ext, (v0, v1, v2) = _conv_taps(vv_ext, w_ref, tr + HALO)
            cb_ext = jnp.concatenate([cb_ref[...], cbn_ref[...]], axis=0)
            dc_ext = jnp.concatenate([dc_ref[...], dcn_ref[...]], axis=0)
            dco, dgn = _head_norm_bwd(cb_ext * y_ext, dc_ext, gv, low)
            rowid = lax.broadcasted_iota(jnp.int32, (tr + HALO, 1), 0)
            dyc = jnp.where((rowid < tr) | (i < nt - 1), dco * cb_ext, 0.0)
            n_ext = tr + HALO
            dvv = (w_ref[2:3, :] * dyc[:tr] + w_ref[1:2, :] * pltpu.roll(dyc, n_ext - 1, 0)[:tr]
                   + w_ref[0:1, :] * pltpu.roll(dyc, n_ext - 2, 0)[:tr])
            stash[0] = (dco[:tr] * y_ext[:tr]).astype(BF16)
            stash[1] = (dvv * cu).astype(BF16)
            stash[2] = (dvv * cc).astype(BF16)
            dyt = dyc[:tr]
            for tap, shifted in enumerate((v2, v1, v0)):
                dw_ref[tap:tap + 1, :] += jnp.sum(dyt * shifted[:tr], axis=0, keepdims=True)
            dg_ref[...] += jnp.sum(dgn[:tr], axis=0, keepdims=True)

        dproj_ref[...] = stash[which]

    n_cols = dproj.shape[1]
    return _pcall(
        body, name="conv_bwd", grid=(nb, nt, 3),
        in_specs=[main(0), main(nb), main(2 * nb), main(0),
                  prev(nb), prev(2 * nb), nxt(0), nxt(nb), nxt(2 * nb), nxt(0),
                  pl.BlockSpec((CONV_K, LANES), lambda j, i, w: (0, j)),
                  pl.BlockSpec((1, LANES), lambda j, i, w: (0, j)),
                  pl.BlockSpec(memory_space=pl.ANY)],
        out_specs=[pl.BlockSpec((tr, LANES), lambda j, i, w: (i, w * nb + j)),
                   pl.BlockSpec((CONV_K, LANES), lambda j, i, w: (0, j)),
                   pl.BlockSpec((1, LANES), lambda j, i, w: (0, j))],
        out_shape=[jax.ShapeDtypeStruct((s, n_cols), BF16),
                   jax.ShapeDtypeStruct((CONV_K, w_conv), F32),
                   jax.ShapeDtypeStruct((1, w_conv), F32)],
        scratch_shapes=[pltpu.VMEM((3, tr, LANES), BF16)],
        input_output_aliases={12: 0},
        compiler_params=_params(("parallel", "arbitrary", "arbitrary")),
    )(proj, proj, proj, dcat, proj, proj, proj, proj, proj, dcat, conv_w, g_conv, dproj)


def _split_dot(val, tri):
    hi = val.astype(BF16)
    lo = (val - hi.astype(F32)).astype(BF16)
    return jnp.dot(jnp.concatenate([hi, lo], axis=1), tri, preferred_element_type=F32)


def _attn_masks(t):
    r = lax.broadcasted_iota(jnp.int32, (t, t), 0)
    c = lax.broadcasted_iota(jnp.int32, (t, t), 1)
    r2 = lax.broadcasted_iota(jnp.int32, (2 * t, t), 0)
    c2 = lax.broadcasted_iota(jnp.int32, (2 * t, t), 1)
    tri2 = jnp.where((r2 > c2) & ((r2 < t) | (r2 - t > c2)), 1.0, 0.0).astype(BF16)
    return (r > c), tri2


def _sb_block(qm, ks, run, causal, strict):
    z = lax.dot_general(qm, ks, _NT, preferred_element_type=F32)
    e = jnp.exp(-jnp.abs(z))
    log_keep = jnp.minimum(-z, 0.0) - jnp.log(1.0 + e)
    if causal is not None:
        log_keep = jnp.where(causal, log_keep, 0.0)
    suffix = _split_dot(log_keep, strict)
    total = suffix[:, 0:1] + log_keep[:, 0:1]
    a = jnp.exp(z + log_keep + suffix + run)
    if causal is not None:
        a = jnp.where(causal, a, 0.0)
    return z, e, a, total


def _attn_fwd(proj, g_attn, cat, w_conv, t=ATTN_BLOCK):
    s = proj.shape[0]
    w_attn = g_attn.shape[1]
    nh = w_attn // LANES
    t = _tile(s, t)
    nq = s // t
    q0 = 3 * w_conv // LANES
    scale = HEAD_DIM ** -0.5

    def body(q_ref, k_ref, v_ref, g_ref, cat_in, o_ref, cat_ref, kb, vb):
        del cat_in
        qi = pl.program_id(1)

        @pl.when(qi == 0)
        def _():
            kb[...] = k_ref[...].astype(BF16)
            vb[...] = v_ref[...].astype(BF16)

        low = _low_half()
        causal, strict = _attn_masks(t)
        q = q_ref[...] * scale
        qms = [jnp.where(msk, q, 0.0).astype(BF16) for msk in (low, jnp.logical_not(low))]

        def step(kblk, carry, causal_mask):
            rows = pl.ds(pl.multiple_of(kblk * t, t), t)
            ks, vs = kb[rows, :], vb[rows, :]
            out = []
            for qm, (run, acc) in zip(qms, carry):
                _, _, a, total = _sb_block(qm, ks, run, causal_mask, strict)
                out.append((run + total, acc + jnp.dot(a.astype(BF16), vs, preferred_element_type=F32)))
            return tuple(out)

        init = (jnp.zeros((t, 1), F32), jnp.zeros((t, LANES), F32))
        carry = step(qi, (init, init), causal)
        carry = lax.fori_loop(0, qi, lambda it, c: step(qi - 1 - it, c, None), carry)
        o = jnp.where(low, carry[0][1], carry[1][1])
        o_ref[...] = o
        r = lax.rsqrt(_half_mean(o * o, low) + EPS)
        cat_ref[...] = (o * r * g_ref[...]).astype(BF16)

    whole = lambda col0: pl.BlockSpec((s, LANES), lambda h, i: (0, col0 + h))
    return _pcall(
        body, name="attn_fwd", grid=(nh, nq),
        in_specs=[pl.BlockSpec((t, LANES), lambda h, i: (i, q0 + h)),
                  whole(q0 + nh), whole(q0 + 2 * nh),
                  pl.BlockSpec((1, LANES), lambda h, i: (0, h)),
                  pl.BlockSpec(memory_space=pl.ANY)],
        out_specs=[pl.BlockSpec((t, LANES), lambda h, i: (i, h)),
                   pl.BlockSpec((t, LANES), lambda h, i: (i, w_conv // LANES + h))],
        out_shape=[jax.ShapeDtypeStruct((s, w_attn), F32),
                   jax.ShapeDtypeStruct(cat.shape, BF16)],
        scratch_shapes=[pltpu.VMEM((s, LANES), BF16), pltpu.VMEM((s, LANES), BF16)],
        input_output_aliases={4: 1},
        compiler_params=_params(("parallel", "arbitrary")),
    )(proj, proj, proj, g_attn, cat)


def _attn_bwd(proj, o, dcat, g_attn, w_conv, t=ATTN_BLOCK):
    s, n_cols = proj.shape
    w_attn = g_attn.shape[1]
    nh = w_attn // LANES
    t = _tile(s, t)
    nq = s // t
    q0 = 3 * w_conv // LANES
    scale = HEAD_DIM ** -0.5

    def body(q_ref, k_ref, v_ref, o_ref, do_ref, g_ref, dproj_ref, dg_ref, kb, vb, dk_acc, dv_acc, stash):
        step_i = pl.program_id(1)
        which = pl.program_id(2)
        qi = nq - 1 - step_i

        @pl.when(which == 0)
        def _():
            @pl.when(step_i == 0)
            def _():
                kb[...] = k_ref[...].astype(BF16)
                vb[...] = v_ref[...].astype(BF16)
                dk_acc[...] = jnp.zeros_like(dk_acc)
                dv_acc[...] = jnp.zeros_like(dv_acc)
                dg_ref[...] = jnp.zeros_like(dg_ref)

            low = _low_half()
            causal, strict = _attn_masks(t)
            q = q_ref[...] * scale
            ov = o_ref[...]
            d_o, dgn = _head_norm_bwd(ov, do_ref[...], g_ref[...], low)
            dg_ref[...] += jnp.sum(dgn, axis=0, keepdims=True)
            heads = []
            for msk in (low, jnp.logical_not(low)):
                qm = jnp.where(msk, q, 0.0).astype(BF16)
                dom = jnp.where(msk, d_o, 0.0).astype(BF16)
                heads.append((qm, dom, jnp.sum(dom.astype(F32) * ov, axis=-1, keepdims=True)))

            def step(kblk, carry, causal_mask):
                rows = pl.ds(pl.multiple_of(kblk * t, t), t)
                ks, vs = kb[rows, :], vb[rows, :]
                out = []
                dk = dk_acc[rows, :]
                dv = dv_acc[rows, :]
                for (qm, dom, row_total), (run, grun, dq) in zip(heads, carry):
                    z, e, a, total = _sb_block(qm, ks, run, causal_mask, strict)
                    da = lax.dot_general(dom, vs, _NT, preferred_element_type=F32)
                    ab = a.astype(BF16)
                    glog = ab.astype(F32) * da
                    later = _split_dot(glog, strict) + glog + grun
                    before = row_total - later
                    inv = 1.0 / (1.0 + e)
                    beta = jnp.where(z >= 0.0, 1.0, e) * inv
                    keep = jnp.where(z >= 0.0, e, 1.0) * inv
                    dz = glog * keep - before * beta
                    if causal_mask is not None:
                        dz = jnp.where(causal_mask, dz, 0.0)
                    dzb = dz.astype(BF16)
                    dq = dq + jnp.dot(dzb, ks, preferred_element_type=F32)
                    dk = dk + lax.dot_general(dzb, qm, _TN, preferred_element_type=F32)
                    dv = dv + lax.dot_general(ab, dom, _TN, preferred_element_type=F32)
                    out.append((run + total, later[:, 0:1], dq))
                dk_acc[rows, :] = dk
                dv_acc[rows, :] = dv
                return tuple(out)

            zero = jnp.zeros((t, 1), F32)
            init = (zero, zero, jnp.zeros((t, LANES), F32))
            carry = step(qi, (init, init), causal)
            carry = lax.fori_loop(0, qi, lambda it, c: step(qi - 1 - it, c, None), carry)
            rows = pl.ds(pl.multiple_of(qi * t, t), t)
            stash[0] = (jnp.where(low, carry[0][2], carry[1][2]) * scale).astype(BF16)
            stash[1] = dk_acc[rows, :].astype(BF16)
            stash[2] = dv_acc[rows, :].astype(BF16)

        dproj_ref[...] = stash[which]

    whole = lambda col0: pl.BlockSpec((s, LANES), lambda h, i, w: (0, col0 + h))
    blk = lambda col0: pl.BlockSpec((t, LANES), lambda h, i, w: (nq - 1 - i, col0 + h))
    return _pcall(
        body, name="attn_bwd", grid=(nh, nq, 3),
        in_specs=[blk(q0), whole(q0 + nh), whole(q0 + 2 * nh), blk(0), blk(w_conv // LANES),
                  pl.BlockSpec((1, LANES), lambda h, i, w: (0, h))],
        out_specs=[pl.BlockSpec((t, LANES), lambda h, i, w: (nq - 1 - i, q0 + w * nh + h)),
                   pl.BlockSpec((1, LANES), lambda h, i, w: (0, h))],
        out_shape=[jax.ShapeDtypeStruct((s, n_cols), BF16), jax.ShapeDtypeStruct((1, w_attn), F32)],
        scratch_shapes=[pltpu.VMEM((s, LANES), BF16), pltpu.VMEM((s, LANES), BF16),
                        pltpu.VMEM((s, LANES), F32), pltpu.VMEM((s, LANES), F32),
                        pltpu.VMEM((3, t, LANES), BF16)],
        compiler_params=_params(("parallel", "arbitrary", "arbitrary")),
    )(proj, proj, proj, o, dcat, g_attn)


def _place():
    return lax.axis_index("x"), lax.axis_index("y"), lax.axis_index("c")


def _other_chips(x, y):
    return [(1 - x, y), (x, 1 - y), (1 - x, 1 - y)]


def _slot(px, py, pc):
    return 4 * px + 2 * py + pc


def _all_gather(shards, out_dtypes):
    nw = len(shards)

    def body(*refs):
        ins, outs, stage = refs[:nw], refs[nw:2 * nw], refs[2 * nw:3 * nw]
        send_sems, recv_sems, local_sems = refs[3 * nw:]
        x, y, c = _place()
        me, sibling = (x, y, c), (x, y, 1 - c)
        chips = _other_chips(x, y)

        def copy(w, k, block, to, src=None):
            dst = outs[w].at[_slot(*block)]
            return pltpu.make_async_remote_copy(
                src_ref=dst if src is None else src, dst_ref=dst,
                send_sem=send_sems.at[w * 7 + k], recv_sem=recv_sems.at[w * 7 + k],
                device_id=to, device_id_type=MESH)

        started = []
        local = []
        for w in range(nw):
            stage[w][...] = ins[w][...].astype(stage[w].dtype)
            cp = pltpu.make_async_copy(stage[w], outs[w].at[_slot(*me)], local_sems.at[w])
            cp.start()
            local.append(cp)
            started.append(copy(w, 0, me, sibling, src=stage[w]))
            started[-1].start()
            for j, chip in enumerate(chips):
                started.append(copy(w, 1 + j, me, (*chip, c), src=stage[w]))
                started[-1].start()
        for j, chip in enumerate(chips):
            for w in range(nw):
                copy(w, 1 + j, (*chip, c), me).wait_recv()
                started.append(copy(w, 4 + j, (*chip, c), sibling))
                started[-1].start()
        for w in range(nw):
            copy(w, 0, sibling, me).wait_recv()
            for j, chip in enumerate(chips):
                copy(w, 4 + j, (*chip, 1 - c), me).wait_recv()
        for cp in started:
            cp.wait_send()
        for cp in local:
            cp.wait()

    return _pcall(
        body, name="all_gather_weights",
        in_specs=[pl.BlockSpec(memory_space=pltpu.VMEM)] * nw,
        out_specs=[pl.BlockSpec(memory_space=pl.ANY)] * nw,
        out_shape=[jax.ShapeDtypeStruct((N_DEV, *a.shape), d) for a, d in zip(shards, out_dtypes)],
        scratch_shapes=[pltpu.VMEM(a.shape, d) for a, d in zip(shards, out_dtypes)]
        + [pltpu.SemaphoreType.DMA((7 * nw,)), pltpu.SemaphoreType.DMA((7 * nw,)),
           pltpu.SemaphoreType.DMA((nw,))],
        compiler_params=_params(),
    )(*shards)


def _pair_exchange(partials):
    nw = len(partials)

    def body(*refs):
        ins, outs = refs[:nw], refs[nw:2 * nw]
        send_sems, recv_sems = refs[2 * nw:]
        x, y, c = _place()
        started = []
        for w in range(nw):
            for q in range(4):
                cp = pltpu.make_async_remote_copy(
                    src_ref=ins[w].at[2 * q + (1 - c)], dst_ref=outs[w].at[q],
                    send_sem=send_sems.at[w * 4 + q], recv_sem=recv_sems.at[w * 4 + q],
                    device_id=(x, y, 1 - c), device_id_type=MESH)
                cp.start()
                started.append(cp)
        for cp in started:
            cp.wait()

    return _pcall(
        body, name="grad_pair_exchange",
        in_specs=[pl.BlockSpec(memory_space=pl.ANY)] * nw,
        out_specs=[pl.BlockSpec(memory_space=pl.ANY)] * nw,
        out_shape=[jax.ShapeDtypeStruct((4, *a.shape[1:]), F32) for a in partials],
        scratch_shapes=[pltpu.SemaphoreType.DMA((4 * nw,)), pltpu.SemaphoreType.DMA((4 * nw,))],
        compiler_params=_params(),
    )(*partials)


def _chip_exchange(sums):
    nw = len(sums)

    def body(*refs):
        ins, outs = refs[:nw], refs[nw:2 * nw]
        send_sems, recv_sems = refs[2 * nw:]
        x, y, c = _place()
        started = []
        for w in range(nw):
            for j, (px, py) in enumerate(_other_chips(x, y)):
                cp = pltpu.make_async_remote_copy(
                    src_ref=ins[w].at[2 * px + py], dst_ref=outs[w].at[j],
                    send_sem=send_sems.at[w * 3 + j], recv_sem=recv_sems.at[w * 3 + j],
                    device_id=(px, py, c), device_id_type=MESH)
                cp.start()
                started.append(cp)
        for cp in started:
            cp.wait()

    return _pcall(
        body, name="grad_chip_exchange",
        in_specs=[pl.BlockSpec(memory_space=pl.ANY)] * nw,
        out_specs=[pl.BlockSpec(memory_space=pl.ANY)] * nw,
        out_shape=[jax.ShapeDtypeStruct((3, *a.shape[1:]), BF16) for a in sums],
        scratch_shapes=[pltpu.SemaphoreType.DMA((3 * nw,)), pltpu.SemaphoreType.DMA((3 * nw,))],
        compiler_params=_params(),
    )(*sums)


def _all_reduce_small(packed):
    r = packed.shape[0]

    def body(x_ref, o_ref, gathered, send_sems, recv_sems):
        x, y, c = _place()
        me = _slot(x, y, c)
        gathered[me] = x_ref[...]
        peers = [(px, py, pc) for px in range(2) for py in range(2) for pc in range(2)]
        started = []
        for k in range(1, N_DEV):
            to = (x ^ (k >> 2), y ^ ((k >> 1) & 1), c ^ (k & 1))
            cp = pltpu.make_async_remote_copy(
                src_ref=x_ref, dst_ref=gathered.at[me],
                send_sem=send_sems.at[k - 1], recv_sem=recv_sems.at[k - 1],
                device_id=to, device_id_type=MESH)
            cp.start()
            started.append(cp)
        del peers
        for cp in started:
            cp.wait()
        total = gathered[0]
        for k in range(1, N_DEV):
            total = total + gathered[k]
        o_ref[...] = total

    return _pcall(
        body, name="all_reduce_small",
        in_specs=[pl.BlockSpec(memory_space=pltpu.VMEM)],
        out_specs=pl.BlockSpec(memory_space=pltpu.VMEM),
        out_shape=jax.ShapeDtypeStruct(packed.shape, F32),
        scratch_shapes=[pltpu.VMEM((N_DEV, r, LANES), F32),
                        pltpu.SemaphoreType.DMA((N_DEV - 1,)), pltpu.SemaphoreType.DMA((N_DEV - 1,))],
        compiler_params=_params(),
    )(packed)


def _pair_sum(name, partial, received, place, tr=256):
    _, r, cdim = partial.shape
    tr = _tile(r, tr) if r % LANES == 0 else r

    def body(place_ref, p_ref, a_ref, of_ref, ob_ref):
        del place_ref
        tot = p_ref[...] + a_ref[...]
        of_ref[...] = tot
        ob_ref[...] = tot.astype(BF16)

    blk = pl.BlockSpec((None, tr, cdim), lambda q, i, pr: (q, i, 0))
    grid_spec = pltpu.PrefetchScalarGridSpec(
        num_scalar_prefetch=1, grid=(4, r // tr),
        in_specs=[pl.BlockSpec((None, tr, cdim), lambda q, i, pr: (2 * q + pr[2], i, 0)), blk],
        out_specs=[blk, blk])
    return _pcall(body, name=name, grid_spec=grid_spec,
                  out_shape=[jax.ShapeDtypeStruct((4, r, cdim), F32), jax.ShapeDtypeStruct((4, r, cdim), BF16)],
                  compiler_params=_params(("parallel", "parallel")))(place, partial, received)


def _adam_math(w, g, m, v):
    m = ADAM_B1 * m + (1.0 - ADAM_B1) * g
    v = ADAM_B2 * v + (1.0 - ADAM_B2) * jnp.square(g)
    m_hat = m / (1.0 - ADAM_B1 ** ADAM_STEP)
    v_hat = v / (1.0 - ADAM_B2 ** ADAM_STEP)
    delta = -ADAM_LR * (m_hat / (jnp.sqrt(v_hat) + ADAM_EPS) + ADAM_WD * w)
    return delta, m, v


def _adam_sharded(name, own, received, w, m, v, place, tr=256):
    r, cdim = w.shape
    tr = _tile(r, tr) if r % LANES == 0 else r

    def body(place_ref, own_ref, rec_ref, w_ref, m_ref, v_ref, g_ref, d_ref, nm_ref, nv_ref):
        del place_ref
        g = own_ref[...]
        for j in range(3):
            g = g + rec_ref[j].astype(F32)
        delta, nm, nv = _adam_math(w_ref[...], g, m_ref[...], v_ref[...])
        g_ref[...] = g
        d_ref[...] = delta
        nm_ref[...] = nm
        nv_ref[...] = nv

    blk = pl.BlockSpec((tr, cdim), lambda i, pr: (i, 0))
    grid_spec = pltpu.PrefetchScalarGridSpec(
        num_scalar_prefetch=1, grid=(r // tr,),
        in_specs=[pl.BlockSpec((None, tr, cdim), lambda i, pr: (2 * pr[0] + pr[1], i, 0)),
                  pl.BlockSpec((3, tr, cdim), lambda i, pr: (0, i, 0)), blk, blk, blk],
        out_specs=[blk] * 4)
    return _pcall(body, name=name, grid_spec=grid_spec,
                  out_shape=[jax.ShapeDtypeStruct((r, cdim), F32)] * 4,
                  compiler_params=_params(("parallel",)))(place, own, received, w, m, v)


def _adam_small(w, g, m, v):
    def body(w_ref, g_ref, m_ref, v_ref, d_ref, nm_ref, nv_ref):
        delta, nm, nv = _adam_math(w_ref[...], g_ref[...], m_ref[...], v_ref[...])
        d_ref[...] = delta
        nm_ref[...] = nm
        nv_ref[...] = nv

    return _pcall(body, name="adam_small",
                  in_specs=[pl.BlockSpec(memory_space=pltpu.VMEM)] * 4,
                  out_specs=[pl.BlockSpec(memory_space=pltpu.VMEM)] * 3,
                  out_shape=[jax.ShapeDtypeStruct(w.shape, F32)] * 3,
                  compiler_params=_params())(w, g, m, v)


def _rows(vec):
    return vec.reshape(-1, LANES)


def kernel(x, p, g_mix, w_in, conv_w, g_conv_out, g_attn_out, w_out, g_mlp, w_up, w_down, g_ple, w_ple_gate, w_ple_proj, g_final, loss_target, m_g_mix, m_w_in, m_conv_w, m_g_conv_out, m_g_attn_out, m_w_out, m_g_mlp, m_w_up, m_w_down, m_g_ple, m_w_ple_gate, m_w_ple_proj, m_g_final, v_g_mix, v_w_in, v_conv_w, v_g_conv_out, v_g_attn_out, v_w_out, v_g_mlp, v_w_up, v_w_down, v_g_ple, v_w_ple_gate, v_w_ple_proj, v_g_final):
    s, d = x.shape[1], x.shape[2]
    w_conv = g_conv_out.shape[1]
    w_attn = g_attn_out.shape[1]
    cw = conv_w.shape[2]
    xs, ps, tgt = x[0], p[0, 0], loss_target[0]
    place = jnp.stack([lax.axis_index("x"), lax.axis_index("y"), lax.axis_index("c")]).astype(jnp.int32)
    my_slot = 4 * place[0] + 2 * place[1] + place[2]

    conv_tile = jnp.pad(conv_w[0], ((0, HALO - CONV_K), (0, LANES - cw)))
    big = [w_in[0], w_out[0], w_up[0], w_down[0], w_ple_gate[0], w_ple_proj[0]]
    gathered = _all_gather(big + [conv_tile], [BF16] * 6 + [F32])
    win_g, wout_g, wup_g, wdown_g, wgate_g, wproj_g, conv_g = gathered
    wout_f = wout_g.reshape(-1, wout_g.shape[-1])
    wdown_f = wdown_g.reshape(-1, wdown_g.shape[-1])
    wgate_f = wgate_g.reshape(-1, wgate_g.shape[-1])
    conv_full = jnp.transpose(conv_g[:, :CONV_K, :cw], (1, 0, 2)).reshape(CONV_K, w_conv)
    in_shard, up_shard, proj_shard = win_g.shape[2], wup_g.shape[2], wproj_g.shape[2]

    a = _rmsnorm_fwd("norm_mix", xs, g_mix)
    proj, = _mm_nn("in_proj", a, win_g, n_shard=in_shard, tn=in_shard)
    cat = _conv_fwd(proj, conv_full, g_conv_out, w_conv, d)
    o, cat = _attn_fwd(proj, g_attn_out, cat, w_conv)
    h1, = _mm_nn("out_proj", cat, wout_f, epilogue=_ep_residual, extras=(xs,))
    mn = _rmsnorm_fwd("norm_mlp", h1, g_mlp)
    u, act = _mm_nn("mlp_up", mn, wup_g, n_shard=up_shard, epilogue=_ep_up, out_dtypes=(F32, BF16))
    h2, = _mm_nn("mlp_down", act, wdown_f, epilogue=_ep_residual, extras=(h1,))
    n3 = _rmsnorm_fwd("norm_ple", h2, g_ple)
    gl, = _mm_nn("ple_gate", n3, wgate_f)
    pp, = _mm_nn("ple_proj", ps, wproj_g, n_shard=proj_shard)
    loss_part, dh3, dgl, dpp, dg_final = _ple_loss(h2, gl, pp, tgt, g_final.reshape(1, d))
    loss = lax.psum(loss_part[0, 0], ("x", "y", "c"))

    dw_proj = _mm_tn("d_w_ple_proj", ps, dpp, n_shard=proj_shard)
    dw_gate = _mm_tn("d_w_ple_gate", n3, dgl)
    dn3, = _mm_nt("d_norm_ple", dgl, wgate_f)
    dh2, dh2b, dg_ple = _rmsnorm_bwd("norm_ple_bwd", dn3, h2, g_ple, dh3)
    du, = _mm_nt("d_mlp_act", dh2b, wdown_f, epilogue=_ep_dact, out_dtypes=(BF16,), extras=(u,))
    dw_down = _mm_tn("d_w_down", act, dh2b)
    dw_up = _mm_tn("d_w_up", mn, du, n_shard=up_shard)
    dmn, = _mm_nt("d_norm_mlp", du, wup_g, k_shard=up_shard)
    dh1, dh1b, dg_mlp = _rmsnorm_bwd("norm_mlp_bwd", dmn, h1, g_mlp, dh2)
    dcat, = _mm_nt("d_cat", dh1b, wout_f)
    dw_out = _mm_tn("d_w_out", cat, dh1b)
    dproj, dg_attn = _attn_bwd(proj, o, dcat, g_attn_out, w_conv)
    dproj, dconv, dg_conv = _conv_bwd(proj, dcat, conv_full, g_conv_out, dproj, w_conv)
    dw_in = _mm_tn("d_w_in", a, dproj, n_shard=in_shard, tn=in_shard)
    da, = _mm_nt("d_norm_mix", dproj, win_g, k_shard=in_shard, tk=in_shard)
    grad_x, _, dg_mix = _rmsnorm_bwd("norm_mix_bwd", da, xs, g_mix, dh1)

    partials = [dw_in,
                dw_out.reshape(N_DEV, -1, dw_out.shape[-1]),
                dw_up,
                dw_down.reshape(N_DEV, -1, dw_down.shape[-1]),
                dw_gate.reshape(N_DEV, -1, dw_gate.shape[-1]),
                dw_proj]
    names = ["w_in", "w_out", "w_up", "w_down", "w_ple_gate", "w_ple_proj"]
    from_sibling = _pair_exchange(partials)
    sums = [_pair_sum("pair_sum_" + n, pa, rc, place) for n, pa, rc in zip(names, partials, from_sibling)]
    from_chips = _chip_exchange([sb for _, sb in sums])
    moments = [(m_w_in, v_w_in), (m_w_out, v_w_out), (m_w_up, v_w_up), (m_w_down, v_w_down),
               (m_w_ple_gate, v_w_ple_gate), (m_w_ple_proj, v_w_ple_proj)]
    big_out = {}
    for n, (sf, _), rc, wt, (mm, vv) in zip(names, sums, from_chips, big, moments):
        big_out[n] = [t[None] for t in _adam_sharded("adam_" + n, sf, rc, wt, mm[0], vv[0], place)]

    small_g = jnp.concatenate(
        [_rows(dg_mix[0]), _rows(dg_conv[0]), _rows(dg_attn[0]), _rows(dg_mlp[0]), _rows(dg_ple[0]),
         _rows(dg_final[0]), _rows(dconv.reshape(-1))], axis=0)
    n_gain_rows = small_g.shape[0] - CONV_K * w_conv // LANES
    pad_rows = (-small_g.shape[0]) % HALO
    small_g = _all_reduce_small(jnp.pad(small_g, ((0, pad_rows), (0, 0))))
    dconv_full = small_g[n_gain_rows:n_gain_rows + CONV_K * w_conv // LANES].reshape(CONV_K, w_conv)
    dconv_mine = lax.dynamic_slice(dconv_full, (0, my_slot * cw), (CONV_K, cw))

    def pack(vecs, conv_part):
        rows = [_rows(t.reshape(-1)) for t in vecs]
        rows.append(jnp.pad(conv_part, ((0, HALO - CONV_K), (0, LANES - cw))))
        return jnp.concatenate(rows, axis=0)

    gains = [g_mix, g_conv_out, g_attn_out, g_mlp, g_ple, g_final]
    gains_m = [m_g_mix, m_g_conv_out, m_g_attn_out, m_g_mlp, m_g_ple, m_g_final]
    gains_v = [v_g_mix, v_g_conv_out, v_g_attn_out, v_g_mlp, v_g_ple, v_g_final]
    gpack = jnp.concatenate([small_g[:n_gain_rows], jnp.pad(dconv_mine, ((0, HALO - CONV_K), (0, LANES - cw)))], axis=0)
    sd, sm, sv = _adam_small(pack(gains, conv_w[0]), gpack, pack(gains_m, m_conv_w[0]), pack(gains_v, v_conv_w[0]))

    def unpack(packed):
        out, r0 = [], 0
        for t in gains:
            nr = t.size // LANES
            out.append(packed[r0:r0 + nr].reshape(t.shape))
            r0 += nr
        out.append(packed[r0:r0 + CONV_K, :cw][None])
        return out

    sg_l, sd_l, sm_l, sv_l = unpack(gpack), unpack(sd), unpack(sm), unpack(sv)
    small_names = ["g_mix", "g_conv_out", "g_attn_out", "g_mlp", "g_ple", "g_final", "conv_w"]
    small_out = {n: [sg_l[i], sd_l[i], sm_l[i], sv_l[i]] for i, n in enumerate(small_names)}

    order = ["g_mix", "w_in", "conv_w", "g_conv_out", "g_attn_out", "w_out", "g_mlp", "w_up", "w_down",
             "g_ple", "w_ple_gate", "w_ple_proj", "g_final"]
    table = {**big_out, **small_out}
    outs = [loss, grad_x[None]]
    for kind in range(4):
        outs.extend(table[n][kind] for n in order)
    return tuple(outs)
```

```python
import functools

import jax
import jax.numpy as jnp
from jax import lax
from jax.experimental import pallas as pl
from jax.experimental.pallas import tpu as pltpu

F32 = jnp.float32
BF16 = jnp.bfloat16
EPS = 1e-6
HEAD_DIM = 64
LANES = 128
CONV_K = 3
ATTN_BLOCK = 256
HALO = 8
N_DEV = 8
MESH = pl.DeviceIdType.MESH
VMEM_LIMIT = 56 * 1024 * 1024

ADAM_LR = 0.001
ADAM_B1 = 0.9
ADAM_B2 = 0.999
ADAM_EPS = 1e-08
ADAM_WD = 0.01
ADAM_STEP = 10


def _pcall(body, **kw):
    return pl.pallas_call(body, **kw)


def _params(sem=None, **kw):
    return pltpu.CompilerParams(dimension_semantics=sem, vmem_limit_bytes=VMEM_LIMIT, **kw)


def _tile(dim, pref):
    t = min(dim, pref)
    while dim % t:
        t -= LANES
    assert t > 0, (dim, pref)
    return t


_NN = (((1,), (0,)), ((), ()))
_NT = (((1,), (1,)), ((), ()))
_TN = (((0,), (0,)), ((), ()))


def _ep_store(acc, outs):
    outs[0][...] = acc.astype(outs[0].dtype)


def _ep_residual(acc, res, outs):
    outs[0][...] = acc + res[...]


def _ep_up(acc, outs):
    outs[0][...] = acc
    outs[1][...] = jnp.square(jnp.maximum(acc, 0.0)).astype(BF16)


def _ep_dact(acc, u, outs):
    outs[0][...] = (acc * (2.0 * jnp.maximum(u[...], 0.0))).astype(BF16)


def _matmul(name, a, b, *, dims, grid, a_spec, b_spec, acc_shape, out_shapes, out_specs,
            epilogue=_ep_store, extras=(), extra_specs=()):
    nk = grid[2]
    n_ex = len(extras)

    def product(a_ref, b_ref):
        return lax.dot_general(a_ref[...].astype(BF16), b_ref[...].astype(BF16), dims,
                               preferred_element_type=F32)

    def body_one(a_ref, b_ref, *rest):
        epilogue(product(a_ref, b_ref), *rest[:n_ex], rest[n_ex:])

    def body_acc(a_ref, b_ref, *rest):
        ex, outs, acc = rest[:n_ex], rest[n_ex:-1], rest[-1]
        k = pl.program_id(2)

        @pl.when(k == 0)
        def _():
            acc[...] = product(a_ref, b_ref)

        @pl.when(k > 0)
        def _():
            acc[...] += product(a_ref, b_ref)

        @pl.when(k == nk - 1)
        def _():
            epilogue(acc[...], *ex, outs)

    return _pcall(
        body_one if nk == 1 else body_acc, name=name, grid=grid,
        in_specs=[a_spec, b_spec, *extra_specs],
        out_specs=list(out_specs), out_shape=list(out_shapes),
        scratch_shapes=[] if nk == 1 else [pltpu.VMEM(acc_shape, F32)],
        compiler_params=_params(("parallel", "parallel", "arbitrary")),
    )(a, b, *extras)


def _mm_nn(name, a, w, *, n_shard=None, epilogue=_ep_store, out_dtypes=(F32,), extras=(), tm=1024, tn=1024, tk=1024):
    m, kd = a.shape
    if n_shard is None:
        n = w.shape[1]
        tn = _tile(n, tn)
        tk = _tile(kd, tk)
        b_spec = pl.BlockSpec((tk, tn), lambda i, j, k: (k, j))
    else:
        n = N_DEV * n_shard
        tn = _tile(n_shard, tn)
        tk = _tile(kd, tk)
        per = n_shard // tn
        b_spec = pl.BlockSpec((None, tk, tn), lambda i, j, k: (j // per, k, j % per))
    tm = _tile(m, tm)
    o_spec = pl.BlockSpec((tm, tn), lambda i, j, k: (i, j))
    return _matmul(
        name, a, w, dims=_NN, grid=(m // tm, n // tn, kd // tk),
        a_spec=pl.BlockSpec((tm, tk), lambda i, j, k: (i, k)), b_spec=b_spec,
        acc_shape=(tm, tn),
        out_shapes=[jax.ShapeDtypeStruct((m, n), d) for d in out_dtypes],
        out_specs=[o_spec] * len(out_dtypes),
        epilogue=epilogue, extras=extras, extra_specs=[o_spec] * len(extras))


def _mm_nt(name, a, w, *, k_shard=None, epilogue=_ep_store, out_dtypes=(F32,), extras=(), tm=1024, tn=1024, tk=1024):
    m, kd = a.shape
    if k_shard is None:
        n = w.shape[0]
        tn = _tile(n, tn)
        tk = _tile(kd, tk)
        b_spec = pl.BlockSpec((tn, tk), lambda i, j, k: (j, k))
    else:
        n = w.shape[1]
        tn = _tile(n, tn)
        tk = _tile(k_shard, tk)
        per = k_shard // tk
        b_spec = pl.BlockSpec((None, tn, tk), lambda i, j, k: (k // per, j, k % per))
    tm = _tile(m, tm)
    o_spec = pl.BlockSpec((tm, tn), lambda i, j, k: (i, j))
    return _matmul(
        name, a, w, dims=_NT, grid=(m // tm, n // tn, kd // tk),
        a_spec=pl.BlockSpec((tm, tk), lambda i, j, k: (i, k)), b_spec=b_spec,
        acc_shape=(tm, tn),
        out_shapes=[jax.ShapeDtypeStruct((m, n), d) for d in out_dtypes],
        out_specs=[o_spec] * len(out_dtypes),
        epilogue=epilogue, extras=extras, extra_specs=[o_spec] * len(extras))


def _mm_tn(name, a, b, *, n_shard=None, tm=1024, tn=1024, tk=1024):
    t, m = a.shape
    n = b.shape[1]
    tm = _tile(m, tm)
    tk = _tile(t, tk)
    if n_shard is None:
        tn = _tile(n, tn)
        o_spec = pl.BlockSpec((tm, tn), lambda i, j, k: (i, j))
        o_shape = jax.ShapeDtypeStruct((m, n), F32)
    else:
        tn = _tile(n_shard, tn)
        per = n_shard // tn
        o_spec = pl.BlockSpec((None, tm, tn), lambda i, j, k: (j // per, i, j % per))
        o_shape = jax.ShapeDtypeStruct((N_DEV, m, n_shard), F32)
    return _matmul(
        name, a, b, dims=_TN, grid=(m // tm, n // tn, t // tk),
        a_spec=pl.BlockSpec((tk, tm), lambda i, j, k: (k, i)),
        b_spec=pl.BlockSpec((tk, tn), lambda i, j, k: (k, j)),
        acc_shape=(tm, tn), out_shapes=[o_shape], out_specs=[o_spec])[0]


def _row_spec(tr, d):
    return pl.BlockSpec((tr, d), lambda i: (i, 0))


def _vec_spec(d):
    return pl.BlockSpec((1, d), lambda i: (0, 0))


def _rmsnorm_fwd(name, x, g, tr=512):
    s, d = x.shape
    tr = _tile(s, tr)

    def body(x_ref, g_ref, o_ref):
        xv = x_ref[...]
        r = lax.rsqrt(jnp.mean(xv * xv, axis=-1, keepdims=True) + EPS)
        o_ref[...] = (xv * r * g_ref[...]).astype(BF16)

    return _pcall(body, name=name, grid=(s // tr,),
                  in_specs=[_row_spec(tr, d), _vec_spec(d)], out_specs=_row_spec(tr, d),
                  out_shape=jax.ShapeDtypeStruct((s, d), BF16),
                  compiler_params=_params(("parallel",)))(x, g)


def _rmsnorm_bwd(name, dn, h, g, dres, tr=512):
    s, d = h.shape
    tr = _tile(s, tr)

    def body(dn_ref, h_ref, g_ref, dres_ref, dh_ref, dhb_ref, dg_ref):
        @pl.when(pl.program_id(0) == 0)
        def _():
            dg_ref[...] = jnp.zeros_like(dg_ref)

        hv = h_ref[...]
        dnv = dn_ref[...]
        r = lax.rsqrt(jnp.mean(hv * hv, axis=-1, keepdims=True) + EPS)
        hn = hv * r
        dg_ref[...] += jnp.sum(dnv * hn, axis=0, keepdims=True)
        dhn = dnv * g_ref[...]
        dh = dres_ref[...] + r * (dhn - hn * jnp.mean(dhn * hn, axis=-1, keepdims=True))
        dh_ref[...] = dh
        dhb_ref[...] = dh.astype(BF16)

    return _pcall(body, name=name, grid=(s // tr,),
                  in_specs=[_row_spec(tr, d), _row_spec(tr, d), _vec_spec(d), _row_spec(tr, d)],
                  out_specs=[_row_spec(tr, d), _row_spec(tr, d), _vec_spec(d)],
                  out_shape=[jax.ShapeDtypeStruct((s, d), F32), jax.ShapeDtypeStruct((s, d), BF16),
                             jax.ShapeDtypeStruct((1, d), F32)],
                  compiler_params=_params(("arbitrary",)))(dn, h, g, dres)


def _ple_loss(h2, gl, pp, tgt, g_final, tr=512):
    s, d = h2.shape
    tr = _tile(s, tr)

    def body(h2_ref, gl_ref, pp_ref, t_ref, g_ref, loss_ref, dh3_ref, dgl_ref, dpp_ref, dg_ref):
        @pl.when(pl.program_id(0) == 0)
        def _():
            dg_ref[...] = jnp.zeros_like(dg_ref)
            loss_ref[...] = jnp.zeros_like(loss_ref)

        gate = jax.nn.sigmoid(gl_ref[...])
        ppv = pp_ref[...]
        h3 = h2_ref[...] + gate * ppv
        r = lax.rsqrt(jnp.mean(h3 * h3, axis=-1, keepdims=True) + EPS)
        hn = h3 * r
        gv = g_ref[...]
        diff = hn * gv - t_ref[...]
        row = jnp.mean(diff * diff, axis=-1, keepdims=True)
        loss_ref[...] += 0.5 * jnp.sum(row, axis=0, keepdims=True)
        dy = diff * (1.0 / d)
        dg_ref[...] += jnp.sum(dy * hn, axis=0, keepdims=True)
        dhn = dy * gv
        dh3 = r * (dhn - hn * jnp.mean(dhn * hn, axis=-1, keepdims=True))
        dh3_ref[...] = dh3
        dgl_ref[...] = (dh3 * ppv * gate * (1.0 - gate)).astype(BF16)
        dpp_ref[...] = (dh3 * gate).astype(BF16)

    return _pcall(body, name="ple_loss", grid=(s // tr,),
                  in_specs=[_row_spec(tr, d)] * 4 + [_vec_spec(d)],
                  out_specs=[_vec_spec(LANES), _row_spec(tr, d), _row_spec(tr, d), _row_spec(tr, d), _vec_spec(d)],
                  out_shape=[jax.ShapeDtypeStruct((1, LANES), F32), jax.ShapeDtypeStruct((s, d), F32),
                             jax.ShapeDtypeStruct((s, d), BF16), jax.ShapeDtypeStruct((s, d), BF16),
                             jax.ShapeDtypeStruct((1, d), F32)],
                  compiler_params=_params(("arbitrary",)))(h2, gl, pp, tgt, g_final)


def _low_half():
    return lax.broadcasted_iota(jnp.int32, (1, LANES), 1) < HEAD_DIM


def _half_mean(v, low):
    s_lo = jnp.sum(jnp.where(low, v, 0.0), axis=-1, keepdims=True)
    s_hi = jnp.sum(jnp.where(low, 0.0, v), axis=-1, keepdims=True)
    return jnp.where(low, s_lo, s_hi) * (1.0 / HEAD_DIM)


def _head_norm_bwd(val, dout, g, low):
    r = lax.rsqrt(_half_mean(val * val, low) + EPS)
    vn = val * r
    dvn = dout * g
    return r * (dvn - vn * _half_mean(dvn * vn, low)), dout * vn


def _conv_taps(vv_ext, w_ref, rows):
    v0 = vv_ext[HALO:]
    v1 = pltpu.roll(vv_ext, 1, 0)[HALO:]
    v2 = pltpu.roll(vv_ext, 2, 0)[HALO:]
    del rows
    return w_ref[2:3, :] * v0 + w_ref[1:2, :] * v1 + w_ref[0:1, :] * v2, (v0, v1, v2)


def _conv_fwd(proj, conv_w, g_conv, w_conv, d_model, tr=512):
    s = proj.shape[0]
    nb = w_conv // LANES
    tr = _tile(s, tr)
    hb = tr // HALO

    def main(col0):
        return pl.BlockSpec((tr, LANES), lambda j, i: (i, col0 + j))

    def prev(col0):
        return pl.BlockSpec((HALO, LANES), lambda j, i: (jnp.maximum(i * hb - 1, 0), col0 + j))

    def body(cb_ref, cc_ref, cu_ref, ccp_ref, cup_ref, w_ref, g_ref, o_ref):
        i = pl.program_id(1)
        vv_prev = jnp.where(i > 0, ccp_ref[...] * cup_ref[...], 0.0)
        vv_ext = jnp.concatenate([vv_prev, cc_ref[...] * cu_ref[...]], axis=0)
        y, _ = _conv_taps(vv_ext, w_ref, tr)
        co = cb_ref[...] * y
        low = _low_half()
        r = lax.rsqrt(_half_mean(co * co, low) + EPS)
        o_ref[...] = (co * r * g_ref[...]).astype(BF16)

    return _pcall(
        body, name="conv_fwd", grid=(nb, s // tr),
        in_specs=[main(0), main(nb), main(2 * nb), prev(nb), prev(2 * nb),
                  pl.BlockSpec((CONV_K, LANES), lambda j, i: (0, j)),
                  pl.BlockSpec((1, LANES), lambda j, i: (0, j))],
        out_specs=pl.BlockSpec((tr, LANES), lambda j, i: (i, j)),
        out_shape=jax.ShapeDtypeStruct((s, d_model), BF16),
        compiler_params=_params(("parallel", "parallel")),
    )(proj, proj, proj, proj, proj, conv_w, g_conv)


def _conv_bwd(proj, dcat, conv_w, g_conv, dproj, w_conv, tr=512):
    s = proj.shape[0]
    nb = w_conv // LANES
    tr = _tile(s, tr)
    hb = tr // HALO
    last = s // HALO - 1
    nt = s // tr

    def main(col0):
        return pl.BlockSpec((tr, LANES), lambda j, i, w: (i, col0 + j))

    def prev(col0):
        return pl.BlockSpec((HALO, LANES), lambda j, i, w: (jnp.maximum(i * hb - 1, 0), col0 + j))

    def nxt(col0):
        return pl.BlockSpec((HALO, LANES), lambda j, i, w: (jnp.minimum((i + 1) * hb, last), col0 + j))

    def body(cb_ref, cc_ref, cu_ref, dc_ref, ccp_ref, cup_ref, cbn_ref, ccn_ref, cun_ref, dcn_ref,
             w_ref, g_ref, dproj_in, dproj_ref, dw_ref, dg_ref, stash):
        del dproj_in
        i = pl.program_id(1)
        which = pl.program_id(2)

        @pl.when(which == 0)
        def _():
            @pl.when(i == 0)
            def _():
                dw_ref[...] = jnp.zeros_like(dw_ref)
                dg_ref[...] = jnp.zeros_like(dg_ref)

            low = _low_half()
            gv = g_ref[...]
            cc, cu = cc_ref[...], cu_ref[...]
            vv_prev = jnp.where(i > 0, ccp_ref[...] * cup_ref[...], 0.0)
            vv_ext = jnp.concatenate([vv_prev, cc * cu, ccn_ref[...] * cun_ref[...]], axis=0)
            y_ext, (v0, v1, v2) = _conv_taps(vv_ext, w_ref, tr + HALO)
            cb_ext = jnp.concatenate([cb_ref[...], cbn_ref[...]], axis=0)
            dc_ext = jnp.concatenate([dc_ref[...], dcn_ref[...]], axis=0)
            dco, dgn = _head_norm_bwd(cb_ext * y_ext, dc_ext, gv, low)
            rowid = lax.broadcasted_iota(jnp.int32, (tr + HALO, 1), 0)
            dyc = jnp.where((rowid < tr) | (i < nt - 1), dco * cb_ext, 0.0)
            n_ext = tr + HALO
            dvv = (w_ref[2:3, :] * dyc[:tr] + w_ref[1:2, :] * pltpu.roll(dyc, n_ext - 1, 0)[:tr]
                   + w_ref[0:1, :] * pltpu.roll(dyc, n_ext - 2, 0)[:tr])
            stash[0] = (dco[:tr] * y_ext[:tr]).astype(BF16)
            stash[1] = (dvv * cu).astype(BF16)
            stash[2] = (dvv * cc).astype(BF16)
            dyt = dyc[:tr]
            for tap, shifted in enumerate((v2, v1, v0)):
                dw_ref[tap:tap + 1, :] += jnp.sum(dyt * shifted[:tr], axis=0, keepdims=True)
            dg_ref[...] += jnp.sum(dgn[:tr], axis=0, keepdims=True)

        dproj_ref[...] = stash[which]

    n_cols = dproj.shape[1]
    return _pcall(
        body, name="conv_bwd", grid=(nb, nt, 3),
        in_specs=[main(0), main(nb), main(2 * nb), main(0),
                  prev(nb), prev(2 * nb), nxt(0), nxt(nb), nxt(2 * nb), nxt(0),
                  pl.BlockSpec((CONV_K, LANES), lambda j, i, w: (0, j)),
                  pl.BlockSpec((1, LANES), lambda j, i, w: (0, j)),
                  pl.BlockSpec(memory_space=pl.ANY)],
        out_specs=[pl.BlockSpec((tr, LANES), lambda j, i, w: (i, w * nb + j)),
                   pl.BlockSpec((CONV_K, LANES), lambda j, i, w: (0, j)),
                   pl.BlockSpec((1, LANES), lambda j, i, w: (0, j))],
        out_shape=[jax.ShapeDtypeStruct((s, n_cols), BF16),
                   jax.ShapeDtypeStruct((CONV_K, w_conv), F32),
                   jax.ShapeDtypeStruct((1, w_conv), F32)],
        scratch_shapes=[pltpu.VMEM((3, tr, LANES), BF16)],
        input_output_aliases={12: 0},
        compiler_params=_params(("parallel", "arbitrary", "arbitrary")),
    )(proj, proj, proj, dcat, proj, proj, proj, proj, proj, dcat, conv_w, g_conv, dproj)


STRIP = 16


def _suffix_operator(t):
    r = lax.broadcasted_iota(jnp.int32, (2 * t, t), 0)
    c = lax.broadcasted_iota(jnp.int32, (2 * t, t), 1)
    return jnp.where((r > c) & ((r < t) | (r - t > c)), 1.0, 0.0).astype(BF16)


def _strips(t):
    return [(i, slice(i * STRIP, (i + 1) * STRIP)) for i in range(t // STRIP)]


def _strip_mask(i, t):
    r = lax.broadcasted_iota(jnp.int32, (STRIP, t), 0) + i * STRIP
    c = lax.broadcasted_iota(jnp.int32, (STRIP, t), 1)
    return r > c


def _store_split(ref, rows, val, t):
    hi = val.astype(BF16)
    ref[rows, 0:t] = hi
    ref[rows, t:2 * t] = (val - hi.astype(F32)).astype(BF16)


def _sb_scores(z_s, split_s, zl_s, tot_s, keep_s, t, diag):
    for i, rows in _strips(t):
        z = z_s[rows, :]
        e = jnp.exp(-jnp.abs(z))
        den = 1.0 + e
        log_keep = jnp.minimum(-z, 0.0) - jnp.log(den)
        if diag:
            log_keep = jnp.where(_strip_mask(i, t), log_keep, 0.0)
        _store_split(split_s, rows, log_keep, t)
        zl_s[rows, :] = z + log_keep
        tot_s[rows, :] = _row_sum(log_keep)
        if keep_s is not None:
            keep_s[rows, :] = jnp.where(z >= 0.0, e, 1.0) / den


def _row_sum(v):
    return jnp.broadcast_to(jnp.sum(v, axis=-1, keepdims=True), (v.shape[0], LANES))


def _wide(r, t):
    return jnp.concatenate([r] * (t // LANES), axis=1)


def _sb_weights(zl_s, suf_s, run_s, tot_s, a_s, t, diag, da_s=None, glog_s=None, gsplit_s=None, gtot_s=None):
    for i, rows in _strips(t):
        run = run_s[rows, :]
        a = jnp.exp(zl_s[rows, :] + suf_s[rows, :] + _wide(run, t))
        if diag:
            a = jnp.where(_strip_mask(i, t), a, 0.0)
        ab = a.astype(BF16)
        a_s[rows, :] = ab
        run_s[rows, :] = run + tot_s[rows, :]
        if da_s is not None:
            glog = ab.astype(F32) * da_s[rows, :]
            glog_s[rows, :] = glog
            _store_split(gsplit_s, rows, glog, t)
            gtot_s[rows, :] = _row_sum(glog)


def _sb_dscores(glog_s, cum_s, rest_s, gtot_s, keep_s, dz_s, t, diag):
    for i, rows in _strips(t):
        glog = glog_s[rows, :]
        rest = rest_s[rows, :]
        before = _wide(rest, t) - cum_s[rows, :] - glog
        dz = (glog + before) * keep_s[rows, :] - before
        if diag:
            dz = jnp.where(_strip_mask(i, t), dz, 0.0)
        dz_s[rows, :] = dz.astype(BF16)
        rest_s[rows, :] = rest - gtot_s[rows, :]


def _attn_fwd(proj, g_attn, cat, w_conv, t=ATTN_BLOCK):
    s = proj.shape[0]
    w_attn = g_attn.shape[1]
    nh = w_attn // LANES
    t = _tile(s, t)
    nq = s // t
    q0 = 3 * w_conv // LANES
    scale = HEAD_DIM ** -0.5

    def body(q_ref, k_ref, v_ref, g_ref, cat_in, o_ref, cat_ref,
             kb, vb, tri_s, qm_s, z_s, split_s, zl_s, suf_s, a_s, run_s, tot_s, acc_s):
        del cat_in
        qi = pl.program_id(1)

        @pl.when(qi == 0)
        def _():
            kb[...] = k_ref[...].astype(BF16)
            vb[...] = v_ref[...].astype(BF16)
            tri_s[...] = _suffix_operator(t)

        low = _low_half()
        q = q_ref[...] * scale
        for h, msk in enumerate((low, jnp.logical_not(low))):
            qm_s[h] = jnp.where(msk, q, 0.0).astype(BF16)
            run_s[h] = jnp.zeros((t, LANES), F32)
            acc_s[h] = jnp.zeros((t, LANES), F32)

        def key_rows(kblk):
            return pl.ds(pl.multiple_of(kblk * t, t), t)

        def scores_matmul(kblk):
            ks = kb[key_rows(kblk), :]
            for h in range(2):
                z_s[h] = lax.dot_general(qm_s[h], ks, _NT, preferred_element_type=F32)

        def step(kblk, diag):
            vs = vb[key_rows(kblk), :]
            for h in range(2):
                _sb_scores(z_s.at[h], split_s.at[h], zl_s.at[h], tot_s.at[h], None, t, diag)
                suf_s[h] = jnp.dot(split_s[h], tri_s[...], preferred_element_type=F32)
            scores_matmul(jnp.maximum(kblk - 1, 0))
            for h in range(2):
                _sb_weights(zl_s.at[h], suf_s.at[h], run_s.at[h], tot_s.at[h], a_s.at[h], t, diag)
                acc_s[h] += jnp.dot(a_s[h], vs, preferred_element_type=F32)

        scores_matmul(qi)
        step(qi, True)

        def loop(it, carry):
            step(qi - 1 - it, False)
            return carry

        lax.fori_loop(0, qi, loop, 0)
        o = jnp.where(low, acc_s[0], acc_s[1])
        o_ref[...] = o
        r = lax.rsqrt(_half_mean(o * o, low) + EPS)
        cat_ref[...] = (o * r * g_ref[...]).astype(BF16)

    whole = lambda col0: pl.BlockSpec((s, LANES), lambda h, i: (0, col0 + h))
    return _pcall(
        body, name="attn_fwd", grid=(nh, nq),
        in_specs=[pl.BlockSpec((t, LANES), lambda h, i: (i, q0 + h)),
                  whole(q0 + nh), whole(q0 + 2 * nh),
                  pl.BlockSpec((1, LANES), lambda h, i: (0, h)),
                  pl.BlockSpec(memory_space=pl.ANY)],
        out_specs=[pl.BlockSpec((t, LANES), lambda h, i: (i, h)),
                   pl.BlockSpec((t, LANES), lambda h, i: (i, w_conv // LANES + h))],
        out_shape=[jax.ShapeDtypeStruct((s, w_attn), F32),
                   jax.ShapeDtypeStruct(cat.shape, BF16)],
        scratch_shapes=[pltpu.VMEM((s, LANES), BF16), pltpu.VMEM((s, LANES), BF16),
                        pltpu.VMEM((2 * t, t), BF16),
                        pltpu.VMEM((2, t, LANES), BF16),
                        pltpu.VMEM((2, t, t), F32),
                        pltpu.VMEM((2, t, 2 * t), BF16),
                        pltpu.VMEM((2, t, t), F32),
                        pltpu.VMEM((2, t, t), F32),
                        pltpu.VMEM((2, t, t), BF16),
                        pltpu.VMEM((2, t, LANES), F32),
                        pltpu.VMEM((2, t, LANES), F32),
                        pltpu.VMEM((2, t, LANES), F32)],
        input_output_aliases={4: 1},
        compiler_params=_params(("parallel", "arbitrary")),
    )(proj, proj, proj, g_attn, cat)


def _attn_bwd(proj, o, dcat, g_attn, w_conv, t=ATTN_BLOCK):
    s, n_cols = proj.shape
    w_attn = g_attn.shape[1]
    nh = w_attn // LANES
    t = _tile(s, t)
    nq = s // t
    q0 = 3 * w_conv // LANES
    scale = HEAD_DIM ** -0.5

    def body(q_ref, k_ref, v_ref, o_ref, do_ref, g_ref, dproj_ref, dg_ref, kb, vb, dkt_acc, dvt_acc, stash,
             tri_s, qm_s, dom_s, qt_s, dot_s, z_s, da_s, split_s, zl_s, keep_s, suf_s, a_s, glog_s, gsplit_s,
             cum_s, dz_s, run_s, tot_s, rest_s, gtot_s, dq_s):
        step_i = pl.program_id(1)
        which = pl.program_id(2)
        qi = nq - 1 - step_i

        @pl.when(which == 0)
        def _():
            @pl.when(step_i == 0)
            def _():
                kb[...] = k_ref[...].astype(BF16)
                vb[...] = v_ref[...].astype(BF16)
                tri_s[...] = _suffix_operator(t)
                dkt_acc[...] = jnp.zeros_like(dkt_acc)
                dvt_acc[...] = jnp.zeros_like(dvt_acc)
                dg_ref[...] = jnp.zeros_like(dg_ref)

            low = _low_half()
            q = q_ref[...] * scale
            ov = o_ref[...]
            d_o, dgn = _head_norm_bwd(ov, do_ref[...], g_ref[...], low)
            dg_ref[...] += jnp.sum(dgn, axis=0, keepdims=True)
            for h, msk in enumerate((low, jnp.logical_not(low))):
                qh = jnp.where(msk, q, 0.0)
                doh = jnp.where(msk, d_o, 0.0)
                dom = doh.astype(BF16)
                qm_s[h] = qh.astype(BF16)
                dom_s[h] = dom
                qt_s[h] = qh.T.astype(BF16)
                dot_s[h] = doh.T.astype(BF16)
                rest_s[h] = _row_sum(dom.astype(F32) * ov)
                run_s[h] = jnp.zeros((t, LANES), F32)
                dq_s[h] = jnp.zeros((t, LANES), F32)

            def key_rows(kblk):
                return pl.ds(pl.multiple_of(kblk * t, t), t)

            def scores_matmul(kblk):
                ks = kb[key_rows(kblk), :]
                for h in range(2):
                    z_s[h] = lax.dot_general(qm_s[h], ks, _NT, preferred_element_type=F32)

            def da_matmul(kblk):
                vs = vb[key_rows(kblk), :]
                for h in range(2):
                    da_s[h] = lax.dot_general(dom_s[h], vs, _NT, preferred_element_type=F32)

            def step(kblk, diag):
                ks = kb[key_rows(kblk), :]
                nxt = jnp.maximum(kblk - 1, 0)
                for h in range(2):
                    _sb_scores(z_s.at[h], split_s.at[h], zl_s.at[h], tot_s.at[h], keep_s.at[h], t, diag)
                    suf_s[h] = jnp.dot(split_s[h], tri_s[...], preferred_element_type=F32)
                scores_matmul(nxt)
                for h in range(2):
                    _sb_weights(zl_s.at[h], suf_s.at[h], run_s.at[h], tot_s.at[h], a_s.at[h], t, diag,
                                da_s.at[h], glog_s.at[h], gsplit_s.at[h], gtot_s.at[h])
                    cum_s[h] = jnp.dot(gsplit_s[h], tri_s[...], preferred_element_type=F32)
                da_matmul(nxt)
                dkt = dkt_acc[kblk]
                dvt = dvt_acc[kblk]
                for h in range(2):
                    _sb_dscores(glog_s.at[h], cum_s.at[h], rest_s.at[h], gtot_s.at[h], keep_s.at[h],
                                dz_s.at[h], t, diag)
                    dq_s[h] += jnp.dot(dz_s[h], ks, preferred_element_type=F32)
                    dkt = dkt + jnp.dot(qt_s[h], dz_s[h], preferred_element_type=F32)
                    dvt = dvt + jnp.dot(dot_s[h], a_s[h], preferred_element_type=F32)
                dkt_acc[kblk] = dkt
                dvt_acc[kblk] = dvt

            scores_matmul(qi)
            da_matmul(qi)
            step(qi, True)

            def loop(it, carry):
                step(qi - 1 - it, False)
                return carry

            lax.fori_loop(0, qi, loop, 0)
            stash[0] = (jnp.where(low, dq_s[0], dq_s[1]) * scale).astype(BF16)
            stash[1] = dkt_acc[qi].T.astype(BF16)
            stash[2] = dvt_acc[qi].T.astype(BF16)

        dproj_ref[...] = stash[which]

    whole = lambda col0: pl.BlockSpec((s, LANES), lambda h, i, w: (0, col0 + h))
    blk = lambda col0: pl.BlockSpec((t, LANES), lambda h, i, w: (nq - 1 - i, col0 + h))
    return _pcall(
        body, name="attn_bwd", grid=(nh, nq, 3),
        in_specs=[blk(q0), whole(q0 + nh), whole(q0 + 2 * nh), blk(0), blk(w_conv // LANES),
                  pl.BlockSpec((1, LANES), lambda h, i, w: (0, h))],
        out_specs=[pl.BlockSpec((t, LANES), lambda h, i, w: (nq - 1 - i, q0 + w * nh + h)),
                   pl.BlockSpec((1, LANES), lambda h, i, w: (0, h))],
        out_shape=[jax.ShapeDtypeStruct((s, n_cols), BF16), jax.ShapeDtypeStruct((1, w_attn), F32)],
        scratch_shapes=[pltpu.VMEM((s, LANES), BF16), pltpu.VMEM((s, LANES), BF16),
                        pltpu.VMEM((nq, LANES, t), F32),
                        pltpu.VMEM((nq, LANES, t), F32),
                        pltpu.VMEM((3, t, LANES), BF16),
                        pltpu.VMEM((2 * t, t), BF16),
                        pltpu.VMEM((2, t, LANES), BF16),
                        pltpu.VMEM((2, t, LANES), BF16),
                        pltpu.VMEM((2, LANES, t), BF16),
                        pltpu.VMEM((2, LANES, t), BF16),
                        pltpu.VMEM((2, t, t), F32),
                        pltpu.VMEM((2, t, t), F32),
                        pltpu.VMEM((2, t, 2 * t), BF16),
                        pltpu.VMEM((2, t, t), F32),
                        pltpu.VMEM((2, t, t), F32),
                        pltpu.VMEM((2, t, t), F32),
                        pltpu.VMEM((2, t, t), BF16),
                        pltpu.VMEM((2, t, t), F32),
                        pltpu.VMEM((2, t, 2 * t), BF16),
                        pltpu.VMEM((2, t, t), F32),
                        pltpu.VMEM((2, t, t), BF16),
                        pltpu.VMEM((2, t, LANES), F32),
                        pltpu.VMEM((2, t, LANES), F32),
                        pltpu.VMEM((2, t, LANES), F32),
                        pltpu.VMEM((2, t, LANES), F32),
                        pltpu.VMEM((2, t, LANES), F32)],
        compiler_params=_params(("parallel", "arbitrary", "arbitrary")),
    )(proj, proj, proj, o, dcat, g_attn)


def _place():
    return lax.axis_index("x"), lax.axis_index("y"), lax.axis_index("c")


def _other_chips(x, y):
    return [(1 - x, y), (x, 1 - y), (1 - x, 1 - y)]


def _slot(px, py, pc):
    return 4 * px + 2 * py + pc


def _all_gather(shards, out_dtypes):
    nw = len(shards)

    def body(*refs):
        ins, outs, stage = refs[:nw], refs[nw:2 * nw], refs[2 * nw:3 * nw]
        send_sems, recv_sems, local_sems = refs[3 * nw:]
        x, y, c = _place()
        me, sibling = (x, y, c), (x, y, 1 - c)
        chips = _other_chips(x, y)

        def copy(w, k, block, to, src=None):
            dst = outs[w].at[_slot(*block)]
            return pltpu.make_async_remote_copy(
                src_ref=dst if src is None else src, dst_ref=dst,
                send_sem=send_sems.at[w * 7 + k], recv_sem=recv_sems.at[w * 7 + k],
                device_id=to, device_id_type=MESH)

        started = []
        local = []
        for w in range(nw):
            stage[w][...] = ins[w][...].astype(stage[w].dtype)
            cp = pltpu.make_async_copy(stage[w], outs[w].at[_slot(*me)], local_sems.at[w])
            cp.start()
            local.append(cp)
            started.append(copy(w, 0, me, sibling, src=stage[w]))
            started[-1].start()
            for j, chip in enumerate(chips):
                started.append(copy(w, 1 + j, me, (*chip, c), src=stage[w]))
                started[-1].start()
        for j, chip in enumerate(chips):
            for w in range(nw):
                copy(w, 1 + j, (*chip, c), me).wait_recv()
                started.append(copy(w, 4 + j, (*chip, c), sibling))
                started[-1].start()
        for w in range(nw):
            copy(w, 0, sibling, me).wait_recv()
            for j, chip in enumerate(chips):
                copy(w, 4 + j, (*chip, 1 - c), me).wait_recv()
        for cp in started:
            cp.wait_send()
        for cp in local:
            cp.wait()

    return _pcall(
        body, name="all_gather_weights",
        in_specs=[pl.BlockSpec(memory_space=pltpu.VMEM)] * nw,
        out_specs=[pl.BlockSpec(memory_space=pl.ANY)] * nw,
        out_shape=[jax.ShapeDtypeStruct((N_DEV, *a.shape), d) for a, d in zip(shards, out_dtypes)],
        scratch_shapes=[pltpu.VMEM(a.shape, d) for a, d in zip(shards, out_dtypes)]
        + [pltpu.SemaphoreType.DMA((7 * nw,)), pltpu.SemaphoreType.DMA((7 * nw,)),
           pltpu.SemaphoreType.DMA((nw,))],
        compiler_params=_params(),
    )(*shards)


def _pair_exchange(partials):
    nw = len(partials)

    def body(*refs):
        ins, outs = refs[:nw], refs[nw:2 * nw]
        send_sems, recv_sems = refs[2 * nw:]
        x, y, c = _place()
        started = []
        for w in range(nw):
            for q in range(4):
                cp = pltpu.make_async_remote_copy(
                    src_ref=ins[w].at[2 * q + (1 - c)], dst_ref=outs[w].at[q],
                    send_sem=send_sems.at[w * 4 + q], recv_sem=recv_sems.at[w * 4 + q],
                    device_id=(x, y, 1 - c), device_id_type=MESH)
                cp.start()
                started.append(cp)
        for cp in started:
            cp.wait()

    return _pcall(
        body, name="grad_pair_exchange",
        in_specs=[pl.BlockSpec(memory_space=pl.ANY)] * nw,
        out_specs=[pl.BlockSpec(memory_space=pl.ANY)] * nw,
        out_shape=[jax.ShapeDtypeStruct((4, *a.shape[1:]), F32) for a in partials],
        scratch_shapes=[pltpu.SemaphoreType.DMA((4 * nw,)), pltpu.SemaphoreType.DMA((4 * nw,))],
        compiler_params=_params(),
    )(*partials)


def _chip_exchange(sums):
    nw = len(sums)

    def body(*refs):
        ins, outs = refs[:nw], refs[nw:2 * nw]
        send_sems, recv_sems = refs[2 * nw:]
        x, y, c = _place()
        started = []
        for w in range(nw):
            for j, (px, py) in enumerate(_other_chips(x, y)):
                cp = pltpu.make_async_remote_copy(
                    src_ref=ins[w].at[2 * px + py], dst_ref=outs[w].at[j],
                    send_sem=send_sems.at[w * 3 + j], recv_sem=recv_sems.at[w * 3 + j],
                    device_id=(px, py, c), device_id_type=MESH)
                cp.start()
                started.append(cp)
        for cp in started:
            cp.wait()

    return _pcall(
        body, name="grad_chip_exchange",
        in_specs=[pl.BlockSpec(memory_space=pl.ANY)] * nw,
        out_specs=[pl.BlockSpec(memory_space=pl.ANY)] * nw,
        out_shape=[jax.ShapeDtypeStruct((3, *a.shape[1:]), BF16) for a in sums],
        scratch_shapes=[pltpu.SemaphoreType.DMA((3 * nw,)), pltpu.SemaphoreType.DMA((3 * nw,))],
        compiler_params=_params(),
    )(*sums)


def _all_reduce_small(packed):
    r = packed.shape[0]

    def body(x_ref, o_ref, gathered, send_sems, recv_sems):
        x, y, c = _place()
        me = _slot(x, y, c)
        gathered[me] = x_ref[...]
        peers = [(px, py, pc) for px in range(2) for py in range(2) for pc in range(2)]
        started = []
        for k in range(1, N_DEV):
            to = (x ^ (k >> 2), y ^ ((k >> 1) & 1), c ^ (k & 1))
            cp = pltpu.make_async_remote_copy(
                src_ref=x_ref, dst_ref=gathered.at[me],
                send_sem=send_sems.at[k - 1], recv_sem=recv_sems.at[k - 1],
                device_id=to, device_id_type=MESH)
            cp.start()
            started.append(cp)
        del peers
        for cp in started:
            cp.wait()
        total = gathered[0]
        for k in range(1, N_DEV):
            total = total + gathered[k]
        o_ref[...] = total

    return _pcall(
        body, name="all_reduce_small",
        in_specs=[pl.BlockSpec(memory_space=pltpu.VMEM)],
        out_specs=pl.BlockSpec(memory_space=pltpu.VMEM),
        out_shape=jax.ShapeDtypeStruct(packed.shape, F32),
        scratch_shapes=[pltpu.VMEM((N_DEV, r, LANES), F32),
                        pltpu.SemaphoreType.DMA((N_DEV - 1,)), pltpu.SemaphoreType.DMA((N_DEV - 1,))],
        compiler_params=_params(),
    )(packed)


def _pair_sum(name, partial, received, place, tr=256):
    _, r, cdim = partial.shape
    tr = _tile(r, tr) if r % LANES == 0 else r

    def body(place_ref, p_ref, a_ref, of_ref, ob_ref):
        del place_ref
        tot = p_ref[...] + a_ref[...]
        of_ref[...] = tot
        ob_ref[...] = tot.astype(BF16)

    blk = pl.BlockSpec((None, tr, cdim), lambda q, i, pr: (q, i, 0))
    grid_spec = pltpu.PrefetchScalarGridSpec(
        num_scalar_prefetch=1, grid=(4, r // tr),
        in_specs=[pl.BlockSpec((None, tr, cdim), lambda q, i, pr: (2 * q + pr[2], i, 0)), blk],
        out_specs=[blk, blk])
    return _pcall(body, name=name, grid_spec=grid_spec,
                  out_shape=[jax.ShapeDtypeStruct((4, r, cdim), F32), jax.ShapeDtypeStruct((4, r, cdim), BF16)],
                  compiler_params=_params(("parallel", "parallel")))(place, partial, received)


def _adam_math(w, g, m, v):
    m = ADAM_B1 * m + (1.0 - ADAM_B1) * g
    v = ADAM_B2 * v + (1.0 - ADAM_B2) * jnp.square(g)
    m_hat = m / (1.0 - ADAM_B1 ** ADAM_STEP)
    v_hat = v / (1.0 - ADAM_B2 ** ADAM_STEP)
    delta = -ADAM_LR * (m_hat / (jnp.sqrt(v_hat) + ADAM_EPS) + ADAM_WD * w)
    return delta, m, v


def _adam_sharded(name, own, received, w, m, v, place, tr=256):
    r, cdim = w.shape
    tr = _tile(r, tr) if r % LANES == 0 else r

    def body(place_ref, own_ref, rec_ref, w_ref, m_ref, v_ref, g_ref, d_ref, nm_ref, nv_ref):
        del place_ref
        g = own_ref[...]
        for j in range(3):
            g = g + rec_ref[j].astype(F32)
        delta, nm, nv = _adam_math(w_ref[...], g, m_ref[...], v_ref[...])
        g_ref[...] = g
        d_ref[...] = delta
        nm_ref[...] = nm
        nv_ref[...] = nv

    blk = pl.BlockSpec((tr, cdim), lambda i, pr: (i, 0))
    grid_spec = pltpu.PrefetchScalarGridSpec(
        num_scalar_prefetch=1, grid=(r // tr,),
        in_specs=[pl.BlockSpec((None, tr, cdim), lambda i, pr: (2 * pr[0] + pr[1], i, 0)),
                  pl.BlockSpec((3, tr, cdim), lambda i, pr: (0, i, 0)), blk, blk, blk],
        out_specs=[blk] * 4)
    return _pcall(body, name=name, grid_spec=grid_spec,
                  out_shape=[jax.ShapeDtypeStruct((r, cdim), F32)] * 4,
                  compiler_params=_params(("parallel",)))(place, own, received, w, m, v)


def _adam_small(w, g, m, v):
    def body(w_ref, g_ref, m_ref, v_ref, d_ref, nm_ref, nv_ref):
        delta, nm, nv = _adam_math(w_ref[...], g_ref[...], m_ref[...], v_ref[...])
        d_ref[...] = delta
        nm_ref[...] = nm
        nv_ref[...] = nv

    return _pcall(body, name="adam_small",
                  in_specs=[pl.BlockSpec(memory_space=pltpu.VMEM)] * 4,
                  out_specs=[pl.BlockSpec(memory_space=pltpu.VMEM)] * 3,
                  out_shape=[jax.ShapeDtypeStruct(w.shape, F32)] * 3,
                  compiler_params=_params())(w, g, m, v)


def _rows(vec):
    return vec.reshape(-1, LANES)


def kernel(x, p, g_mix, w_in, conv_w, g_conv_out, g_attn_out, w_out, g_mlp, w_up, w_down, g_ple, w_ple_gate, w_ple_proj, g_final, loss_target, m_g_mix, m_w_in, m_conv_w, m_g_conv_out, m_g_attn_out, m_w_out, m_g_mlp, m_w_up, m_w_down, m_g_ple, m_w_ple_gate, m_w_ple_proj, m_g_final, v_g_mix, v_w_in, v_conv_w, v_g_conv_out, v_g_attn_out, v_w_out, v_g_mlp, v_w_up, v_w_down, v_g_ple, v_w_ple_gate, v_w_ple_proj, v_g_final):
    s, d = x.shape[1], x.shape[2]
    w_conv = g_conv_out.shape[1]
    w_attn = g_attn_out.shape[1]
    cw = conv_w.shape[2]
    xs, ps, tgt = x[0], p[0, 0], loss_target[0]
    place = jnp.stack([lax.axis_index("x"), lax.axis_index("y"), lax.axis_index("c")]).astype(jnp.int32)
    my_slot = 4 * place[0] + 2 * place[1] + place[2]

    conv_tile = jnp.pad(conv_w[0], ((0, HALO - CONV_K), (0, LANES - cw)))
    big = [w_in[0], w_out[0], w_up[0], w_down[0], w_ple_gate[0], w_ple_proj[0]]
    gathered = _all_gather(big + [conv_tile], [BF16] * 6 + [F32])
    win_g, wout_g, wup_g, wdown_g, wgate_g, wproj_g, conv_g = gathered
    wout_f = wout_g.reshape(-1, wout_g.shape[-1])
    wdown_f = wdown_g.reshape(-1, wdown_g.shape[-1])
    wgate_f = wgate_g.reshape(-1, wgate_g.shape[-1])
    conv_full = jnp.transpose(conv_g[:, :CONV_K, :cw], (1, 0, 2)).reshape(CONV_K, w_conv)
    in_shard, up_shard, proj_shard = win_g.shape[2], wup_g.shape[2], wproj_g.shape[2]

    a = _rmsnorm_fwd("norm_mix", xs, g_mix)
    proj, = _mm_nn("in_proj", a, win_g, n_shard=in_shard, tn=in_shard)
    cat = _conv_fwd(proj, conv_full, g_conv_out, w_conv, d)
    o, cat = _attn_fwd(proj, g_attn_out, cat, w_conv)
    h1, = _mm_nn("out_proj", cat, wout_f, epilogue=_ep_residual, extras=(xs,))
    mn = _rmsnorm_fwd("norm_mlp", h1, g_mlp)
    u, act = _mm_nn("mlp_up", mn, wup_g, n_shard=up_shard, epilogue=_ep_up, out_dtypes=(F32, BF16))
    h2, = _mm_nn("mlp_down", act, wdown_f, epilogue=_ep_residual, extras=(h1,))
    n3 = _rmsnorm_fwd("norm_ple", h2, g_ple)
    gl, = _mm_nn("ple_gate", n3, wgate_f)
    pp, = _mm_nn("ple_proj", ps, wproj_g, n_shard=proj_shard)
    loss_part, dh3, dgl, dpp, dg_final = _ple_loss(h2, gl, pp, tgt, g_final.reshape(1, d))
    loss = lax.psum(loss_part[0, 0], ("x", "y", "c"))

    dw_proj = _mm_tn("d_w_ple_proj", ps, dpp, n_shard=proj_shard)
    dw_gate = _mm_tn("d_w_ple_gate", n3, dgl)
    dn3, = _mm_nt("d_norm_ple", dgl, wgate_f)
    dh2, dh2b, dg_ple = _rmsnorm_bwd("norm_ple_bwd", dn3, h2, g_ple, dh3)
    du, = _mm_nt("d_mlp_act", dh2b, wdown_f, epilogue=_ep_dact, out_dtypes=(BF16,), extras=(u,))
    dw_down = _mm_tn("d_w_down", act, dh2b)
    dw_up = _mm_tn("d_w_up", mn, du, n_shard=up_shard)
    dmn, = _mm_nt("d_norm_mlp", du, wup_g, k_shard=up_shard)
    dh1, dh1b, dg_mlp = _rmsnorm_bwd("norm_mlp_bwd", dmn, h1, g_mlp, dh2)
    dcat, = _mm_nt("d_cat", dh1b, wout_f)
    dw_out = _mm_tn("d_w_out", cat, dh1b)
    dproj, dg_attn = _attn_bwd(proj, o, dcat, g_attn_out, w_conv)
    dproj, dconv, dg_conv = _conv_bwd(proj, dcat, conv_full, g_conv_out, dproj, w_conv)
    dw_in = _mm_tn("d_w_in", a, dproj, n_shard=in_shard, tn=in_shard)
    da, = _mm_nt("d_norm_mix", dproj, win_g, k_shard=in_shard, tk=in_shard)
    grad_x, _, dg_mix = _rmsnorm_bwd("norm_mix_bwd", da, xs, g_mix, dh1)

    partials = [dw_in,
                dw_out.reshape(N_DEV, -1, dw_out.shape[-1]),
                dw_up,
                dw_down.reshape(N_DEV, -1, dw_down.shape[-1]),
                dw_gate.reshape(N_DEV, -1, dw_gate.shape[-1]),
                dw_proj]
    names = ["w_in", "w_out", "w_up", "w_down", "w_ple_gate", "w_ple_proj"]
    from_sibling = _pair_exchange(partials)
    sums = [_pair_sum("pair_sum_" + n, pa, rc, place) for n, pa, rc in zip(names, partials, from_sibling)]
    from_chips = _chip_exchange([sb for _, sb in sums])
    moments = [(m_w_in, v_w_in), (m_w_out, v_w_out), (m_w_up, v_w_up), (m_w_down, v_w_down),
               (m_w_ple_gate, v_w_ple_gate), (m_w_ple_proj, v_w_ple_proj)]
    big_out = {}
    for n, (sf, _), rc, wt, (mm, vv) in zip(names, sums, from_chips, big, moments):
        big_out[n] = [t[None] for t in _adam_sharded("adam_" + n, sf, rc, wt, mm[0], vv[0], place)]

    small_g = jnp.concatenate(
        [_rows(dg_mix[0]), _rows(dg_conv[0]), _rows(dg_attn[0]), _rows(dg_mlp[0]), _rows(dg_ple[0]),
         _rows(dg_final[0]), _rows(dconv.reshape(-1))], axis=0)
    n_gain_rows = small_g.shape[0] - CONV_K * w_conv // LANES
    pad_rows = (-small_g.shape[0]) % HALO
    small_g = _all_reduce_small(jnp.pad(small_g, ((0, pad_rows), (0, 0))))
    dconv_full = small_g[n_gain_rows:n_gain_rows + CONV_K * w_conv // LANES].reshape(CONV_K, w_conv)
    dconv_mine = lax.dynamic_slice(dconv_full, (0, my_slot * cw), (CONV_K, cw))

    def pack(vecs, conv_part):
        rows = [_rows(t.reshape(-1)) for t in vecs]
        rows.append(jnp.pad(conv_part, ((0, HALO - CONV_K), (0, LANES - cw))))
        return jnp.concatenate(rows, axis=0)

    gains = [g_mix, g_conv_out, g_attn_out, g_mlp, g_ple, g_final]
    gains_m = [m_g_mix, m_g_conv_out, m_g_attn_out, m_g_mlp, m_g_ple, m_g_final]
    gains_v = [v_g_mix, v_g_conv_out, v_g_attn_out, v_g_mlp, v_g_ple, v_g_final]
    gpack = jnp.concatenate([small_g[:n_gain_rows], jnp.pad(dconv_mine, ((0, HALO - CONV_K), (0, LANES - cw)))], axis=0)
    sd, sm, sv = _adam_small(pack(gains, conv_w[0]), gpack, pack(gains_m, m_conv_w[0]), pack(gains_v, v_conv_w[0]))

    def unpack(packed):
        out, r0 = [], 0
        for t in gains:
            nr = t.size // LANES
            out.append(packed[r0:r0 + nr].reshape(t.shape))
            r0 += nr
        out.append(packed[r0:r0 + CONV_K, :cw][None])
        return out

    sg_l, sd_l, sm_l, sv_l = unpack(gpack), unpack(sd), unpack(sm), unpack(sv)
    small_names = ["g_mix", "g_conv_out", "g_attn_out", "g_mlp", "g_ple", "g_final", "conv_w"]
    small_out = {n: [sg_l[i], sd_l[i], sm_l[i], sv_l[i]] for i, n in enumerate(small_names)}

    order = ["g_mix", "w_in", "conv_w", "g_conv_out", "g_attn_out", "w_out", "g_mlp", "w_up", "w_down",
             "g_ple", "w_ple_gate", "w_ple_proj", "g_final"]
    table = {**big_out, **small_out}
    outs = [loss, grad_x[None]]
    for kind in range(4):
        outs.extend(table[n][kind] for n in order)
    return tuple(outs)
```

```python
import functools

import jax
import jax.numpy as jnp
from jax import lax
from jax.experimental import pallas as pl
from jax.experimental.pallas import tpu as pltpu

F32 = jnp.float32
BF16 = jnp.bfloat16
EPS = 1e-6
HEAD_DIM = 64
LANES = 128
CONV_K = 3
ATTN_BLOCK = 256
HALO = 8
N_DEV = 8
MESH = pl.DeviceIdType.MESH
VMEM_LIMIT = 56 * 1024 * 1024

ADAM_LR = 0.001
ADAM_B1 = 0.9
ADAM_B2 = 0.999
ADAM_EPS = 1e-08
ADAM_WD = 0.01
ADAM_STEP = 10


def _pcall(body, **kw):
    return pl.pallas_call(body, **kw)


def _params(sem=None, **kw):
    return pltpu.CompilerParams(dimension_semantics=sem, vmem_limit_bytes=VMEM_LIMIT, **kw)


def _tile(dim, pref):
    t = min(dim, pref)
    while dim % t:
        t -= LANES
    assert t > 0, (dim, pref)
    return t


_NN = (((1,), (0,)), ((), ()))
_NT = (((1,), (1,)), ((), ()))
_TN = (((0,), (0,)), ((), ()))


def _ep_store(acc, outs):
    outs[0][...] = acc.astype(outs[0].dtype)


def _ep_both(acc, outs):
    outs[0][...] = acc
    outs[1][...] = acc.astype(BF16)


def _ep_residual(acc, res, outs):
    outs[0][...] = acc + res[...]


def _ep_up(acc, outs):
    outs[0][...] = acc
    outs[1][...] = jnp.square(jnp.maximum(acc, 0.0)).astype(BF16)


def _ep_dact(acc, u, outs):
    outs[0][...] = (acc * (2.0 * jnp.maximum(u[...], 0.0))).astype(BF16)


def _matmul(name, a, b, *, dims, grid, a_spec, b_spec, acc_shape, out_shapes, out_specs,
            epilogue=_ep_store, extras=(), extra_specs=()):
    nk = grid[2]
    n_ex = len(extras)

    def product(a_ref, b_ref):
        return lax.dot_general(a_ref[...].astype(BF16), b_ref[...].astype(BF16), dims,
                               preferred_element_type=F32)

    def body_one(a_ref, b_ref, *rest):
        epilogue(product(a_ref, b_ref), *rest[:n_ex], rest[n_ex:])

    def body_acc(a_ref, b_ref, *rest):
        ex, outs, acc = rest[:n_ex], rest[n_ex:-1], rest[-1]
        k = pl.program_id(2)

        @pl.when(k == 0)
        def _():
            acc[...] = product(a_ref, b_ref)

        @pl.when(k > 0)
        def _():
            acc[...] += product(a_ref, b_ref)

        @pl.when(k == nk - 1)
        def _():
            epilogue(acc[...], *ex, outs)

    return _pcall(
        body_one if nk == 1 else body_acc, name=name, grid=grid,
        in_specs=[a_spec, b_spec, *extra_specs],
        out_specs=list(out_specs), out_shape=list(out_shapes),
        scratch_shapes=[] if nk == 1 else [pltpu.VMEM(acc_shape, F32)],
        compiler_params=_params(("parallel", "parallel", "arbitrary")),
    )(a, b, *extras)


def _mm_nn(name, a, w, *, n_shard=None, epilogue=_ep_store, out_dtypes=(F32,), extras=(), tm=1024, tn=1024, tk=1024):
    m, kd = a.shape
    if n_shard is None:
        n = w.shape[1]
        tn = _tile(n, tn)
        tk = _tile(kd, tk)
        b_spec = pl.BlockSpec((tk, tn), lambda i, j, k: (k, j))
    else:
        n = N_DEV * n_shard
        tn = _tile(n_shard, tn)
        tk = _tile(kd, tk)
        per = n_shard // tn
        b_spec = pl.BlockSpec((None, tk, tn), lambda i, j, k: (j // per, k, j % per))
    tm = _tile(m, tm)
    o_spec = pl.BlockSpec((tm, tn), lambda i, j, k: (i, j))
    return _matmul(
        name, a, w, dims=_NN, grid=(m // tm, n // tn, kd // tk),
        a_spec=pl.BlockSpec((tm, tk), lambda i, j, k: (i, k)), b_spec=b_spec,
        acc_shape=(tm, tn),
        out_shapes=[jax.ShapeDtypeStruct((m, n), d) for d in out_dtypes],
        out_specs=[o_spec] * len(out_dtypes),
        epilogue=epilogue, extras=extras, extra_specs=[o_spec] * len(extras))


def _mm_nt(name, a, w, *, k_shard=None, epilogue=_ep_store, out_dtypes=(F32,), extras=(), tm=1024, tn=1024, tk=1024):
    m, kd = a.shape
    if k_shard is None:
        n = w.shape[0]
        tn = _tile(n, tn)
        tk = _tile(kd, tk)
        b_spec = pl.BlockSpec((tn, tk), lambda i, j, k: (j, k))
    else:
        n = w.shape[1]
        tn = _tile(n, tn)
        tk = _tile(k_shard, tk)
        per = k_shard // tk
        b_spec = pl.BlockSpec((None, tn, tk), lambda i, j, k: (k // per, j, k % per))
    tm = _tile(m, tm)
    o_spec = pl.BlockSpec((tm, tn), lambda i, j, k: (i, j))
    return _matmul(
        name, a, w, dims=_NT, grid=(m // tm, n // tn, kd // tk),
        a_spec=pl.BlockSpec((tm, tk), lambda i, j, k: (i, k)), b_spec=b_spec,
        acc_shape=(tm, tn),
        out_shapes=[jax.ShapeDtypeStruct((m, n), d) for d in out_dtypes],
        out_specs=[o_spec] * len(out_dtypes),
        epilogue=epilogue, extras=extras, extra_specs=[o_spec] * len(extras))


def _mm_tn(name, a, b, *, n_shard=None, tm=1024, tn=1024, tk=1024):
    t, m = a.shape
    n = b.shape[1]
    tm = _tile(m, tm)
    tk = _tile(t, tk)
    if n_shard is None:
        tn = _tile(n, tn)
        o_spec = pl.BlockSpec((tm, tn), lambda i, j, k: (i, j))
        shape = (m, n)
    else:
        tn = _tile(n_shard, tn)
        per = n_shard // tn
        o_spec = pl.BlockSpec((None, tm, tn), lambda i, j, k: (j // per, i, j % per))
        shape = (N_DEV, m, n_shard)
    return _matmul(
        name, a, b, dims=_TN, grid=(m // tm, n // tn, t // tk),
        a_spec=pl.BlockSpec((tk, tm), lambda i, j, k: (k, i)),
        b_spec=pl.BlockSpec((tk, tn), lambda i, j, k: (k, j)),
        acc_shape=(tm, tn), epilogue=_ep_both,
        out_shapes=[jax.ShapeDtypeStruct(shape, F32), jax.ShapeDtypeStruct(shape, BF16)],
        out_specs=[o_spec, o_spec])


def _row_spec(tr, d):
    return pl.BlockSpec((tr, d), lambda i: (i, 0))


def _vec_spec(d):
    return pl.BlockSpec((1, d), lambda i: (0, 0))


def _rmsnorm_fwd(name, x, g, tr=512):
    s, d = x.shape
    tr = _tile(s, tr)

    def body(x_ref, g_ref, o_ref):
        xv = x_ref[...]
        r = lax.rsqrt(jnp.mean(xv * xv, axis=-1, keepdims=True) + EPS)
        o_ref[...] = (xv * r * g_ref[...]).astype(BF16)

    return _pcall(body, name=name, grid=(s // tr,),
                  in_specs=[_row_spec(tr, d), _vec_spec(d)], out_specs=_row_spec(tr, d),
                  out_shape=jax.ShapeDtypeStruct((s, d), BF16),
                  compiler_params=_params(("parallel",)))(x, g)


def _rmsnorm_bwd(name, dn, h, g, dres, tr=512):
    s, d = h.shape
    tr = _tile(s, tr)

    def body(dn_ref, h_ref, g_ref, dres_ref, dh_ref, dhb_ref, dg_ref):
        @pl.when(pl.program_id(0) == 0)
        def _():
            dg_ref[...] = jnp.zeros_like(dg_ref)

        hv = h_ref[...]
        dnv = dn_ref[...]
        r = lax.rsqrt(jnp.mean(hv * hv, axis=-1, keepdims=True) + EPS)
        hn = hv * r
        dg_ref[...] += jnp.sum(dnv * hn, axis=0, keepdims=True)
        dhn = dnv * g_ref[...]
        dh = dres_ref[...] + r * (dhn - hn * jnp.mean(dhn * hn, axis=-1, keepdims=True))
        dh_ref[...] = dh
        dhb_ref[...] = dh.astype(BF16)

    return _pcall(body, name=name, grid=(s // tr,),
                  in_specs=[_row_spec(tr, d), _row_spec(tr, d), _vec_spec(d), _row_spec(tr, d)],
                  out_specs=[_row_spec(tr, d), _row_spec(tr, d), _vec_spec(d)],
                  out_shape=[jax.ShapeDtypeStruct((s, d), F32), jax.ShapeDtypeStruct((s, d), BF16),
                             jax.ShapeDtypeStruct((1, d), F32)],
                  compiler_params=_params(("arbitrary",)))(dn, h, g, dres)


def _ple_loss(h2, gl, pp, tgt, g_final, tr=512):
    s, d = h2.shape
    tr = _tile(s, tr)

    def body(h2_ref, gl_ref, pp_ref, t_ref, g_ref, loss_ref, dh3_ref, dgl_ref, dpp_ref, dg_ref):
        @pl.when(pl.program_id(0) == 0)
        def _():
            dg_ref[...] = jnp.zeros_like(dg_ref)
            loss_ref[...] = jnp.zeros_like(loss_ref)

        gate = jax.nn.sigmoid(gl_ref[...])
        ppv = pp_ref[...]
        h3 = h2_ref[...] + gate * ppv
        r = lax.rsqrt(jnp.mean(h3 * h3, axis=-1, keepdims=True) + EPS)
        hn = h3 * r
        gv = g_ref[...]
        diff = hn * gv - t_ref[...]
        row = jnp.mean(diff * diff, axis=-1, keepdims=True)
        loss_ref[...] += 0.5 * jnp.sum(row, axis=0, keepdims=True)
        dy = diff * (1.0 / d)
        dg_ref[...] += jnp.sum(dy * hn, axis=0, keepdims=True)
        dhn = dy * gv
        dh3 = r * (dhn - hn * jnp.mean(dhn * hn, axis=-1, keepdims=True))
        dh3_ref[...] = dh3
        dgl_ref[...] = (dh3 * ppv * gate * (1.0 - gate)).astype(BF16)
        dpp_ref[...] = (dh3 * gate).astype(BF16)

    return _pcall(body, name="ple_loss", grid=(s // tr,),
                  in_specs=[_row_spec(tr, d)] * 4 + [_vec_spec(d)],
                  out_specs=[_vec_spec(LANES), _row_spec(tr, d), _row_spec(tr, d), _row_spec(tr, d), _vec_spec(d)],
                  out_shape=[jax.ShapeDtypeStruct((1, LANES), F32), jax.ShapeDtypeStruct((s, d), F32),
                             jax.ShapeDtypeStruct((s, d), BF16), jax.ShapeDtypeStruct((s, d), BF16),
                             jax.ShapeDtypeStruct((1, d), F32)],
                  compiler_params=_params(("arbitrary",)))(h2, gl, pp, tgt, g_final)


def _low_half():
    return lax.broadcasted_iota(jnp.int32, (1, LANES), 1) < HEAD_DIM


def _half_mean(v, low):
    s_lo = jnp.sum(jnp.where(low, v, 0.0), axis=-1, keepdims=True)
    s_hi = jnp.sum(jnp.where(low, 0.0, v), axis=-1, keepdims=True)
    return jnp.where(low, s_lo, s_hi) * (1.0 / HEAD_DIM)


def _head_norm_bwd(val, dout, g, low):
    r = lax.rsqrt(_half_mean(val * val, low) + EPS)
    vn = val * r
    dvn = dout * g
    return r * (dvn - vn * _half_mean(dvn * vn, low)), dout * vn


def _conv_taps(vv_ext, w_ref, rows):
    v0 = vv_ext[HALO:]
    v1 = pltpu.roll(vv_ext, 1, 0)[HALO:]
    v2 = pltpu.roll(vv_ext, 2, 0)[HALO:]
    del rows
    return w_ref[2:3, :] * v0 + w_ref[1:2, :] * v1 + w_ref[0:1, :] * v2, (v0, v1, v2)


def _conv_fwd(proj, conv_w, g_conv, w_conv, d_model, tr=512):
    s = proj.shape[0]
    nb = w_conv // LANES
    tr = _tile(s, tr)
    hb = tr // HALO

    def main(col0):
        return pl.BlockSpec((tr, LANES), lambda j, i: (i, col0 + j))

    def prev(col0):
        return pl.BlockSpec((HALO, LANES), lambda j, i: (jnp.maximum(i * hb - 1, 0), col0 + j))

    def body(cb_ref, cc_ref, cu_ref, ccp_ref, cup_ref, w_ref, g_ref, o_ref):
        i = pl.program_id(1)
        vv_prev = jnp.where(i > 0, ccp_ref[...] * cup_ref[...], 0.0)
        vv_ext = jnp.concatenate([vv_prev, cc_ref[...] * cu_ref[...]], axis=0)
        y, _ = _conv_taps(vv_ext, w_ref, tr)
        co = cb_ref[...] * y
        low = _low_half()
        r = lax.rsqrt(_half_mean(co * co, low) + EPS)
        o_ref[...] = (co * r * g_ref[...]).astype(BF16)

    return _pcall(
        body, name="conv_fwd", grid=(nb, s // tr),
        in_specs=[main(0), main(nb), main(2 * nb), prev(nb), prev(2 * nb),
                  pl.BlockSpec((CONV_K, LANES), lambda j, i: (0, j)),
                  pl.BlockSpec((1, LANES), lambda j, i: (0, j))],
        out_specs=pl.BlockSpec((tr, LANES), lambda j, i: (i, j)),
        out_shape=jax.ShapeDtypeStruct((s, d_model), BF16),
        compiler_params=_params(("parallel", "parallel")),
    )(proj, proj, proj, proj, proj, conv_w, g_conv)


def _conv_bwd(proj, dcat, conv_w, g_conv, dproj, w_conv, tr=512):
    s = proj.shape[0]
    nb = w_conv // LANES
    tr = _tile(s, tr)
    hb = tr // HALO
    last = s // HALO - 1
    nt = s // tr

    def main(col0):
        return pl.BlockSpec((tr, LANES), lambda j, i, w: (i, col0 + j))

    def prev(col0):
        return pl.BlockSpec((HALO, LANES), lambda j, i, w: (jnp.maximum(i * hb - 1, 0), col0 + j))

    def nxt(col0):
        return pl.BlockSpec((HALO, LANES), lambda j, i, w: (jnp.minimum((i + 1) * hb, last), col0 + j))

    def body(cb_ref, cc_ref, cu_ref, dc_ref, ccp_ref, cup_ref, cbn_ref, ccn_ref, cun_ref, dcn_ref,
             w_ref, g_ref, dproj_in, dproj_ref, dw_ref, dg_ref, stash):
        del dproj_in
        i = pl.program_id(1)
        which = pl.program_id(2)

        @pl.when(which == 0)
        def _():
            @pl.when(i == 0)
            def _():
                dw_ref[...] = jnp.zeros_like(dw_ref)
                dg_ref[...] = jnp.zeros_like(dg_ref)

            low = _low_half()
            gv = g_ref[...]
            cc, cu = cc_ref[...], cu_ref[...]
            vv_prev = jnp.where(i > 0, ccp_ref[...] * cup_ref[...], 0.0)
            vv_ext = jnp.concatenate([vv_prev, cc * cu, ccn_ref[...] * cun_ref[...]], axis=0)
            y_ext, (v0, v1, v2) = _conv_taps(vv_ext, w_ref, tr + HALO)
            cb_ext = jnp.concatenate([cb_ref[...], cbn_ref[...]], axis=0)
            dc_ext = jnp.concatenate([dc_ref[...], dcn_ref[...]], axis=0)
            dco, dgn = _head_norm_bwd(cb_ext * y_ext, dc_ext, gv, low)
            rowid = lax.broadcasted_iota(jnp.int32, (tr + HALO, 1), 0)
            dyc = jnp.where((rowid < tr) | (i < nt - 1), dco * cb_ext, 0.0)
            n_ext = tr + HALO
            dvv = (w_ref[2:3, :] * dyc[:tr] + w_ref[1:2, :] * pltpu.roll(dyc, n_ext - 1, 0)[:tr]
                   + w_ref[0:1, :] * pltpu.roll(dyc, n_ext - 2, 0)[:tr])
            stash[0] = (dco[:tr] * y_ext[:tr]).astype(BF16)
            stash[1] = (dvv * cu).astype(BF16)
            stash[2] = (dvv * cc).astype(BF16)
            dyt = dyc[:tr]
            for tap, shifted in enumerate((v2, v1, v0)):
                dw_ref[tap:tap + 1, :] += jnp.sum(dyt * shifted[:tr], axis=0, keepdims=True)
            dg_ref[...] += jnp.sum(dgn[:tr], axis=0, keepdims=True)

        dproj_ref[...] = stash[which]

    n_cols = dproj.shape[1]
    return _pcall(
        body, name="conv_bwd", grid=(nb, nt, 3),
        in_specs=[main(0), main(nb), main(2 * nb), main(0),
                  prev(nb), prev(2 * nb), nxt(0), nxt(nb), nxt(2 * nb), nxt(0),
                  pl.BlockSpec((CONV_K, LANES), lambda j, i, w: (0, j)),
                  pl.BlockSpec((1, LANES), lambda j, i, w: (0, j)),
                  pl.BlockSpec(memory_space=pl.ANY)],
        out_specs=[pl.BlockSpec((tr, LANES), lambda j, i, w: (i, w * nb + j)),
                   pl.BlockSpec((CONV_K, LANES), lambda j, i, w: (0, j)),
                   pl.BlockSpec((1, LANES), lambda j, i, w: (0, j))],
        out_shape=[jax.ShapeDtypeStruct((s, n_cols), BF16),
                   jax.ShapeDtypeStruct((CONV_K, w_conv), F32),
                   jax.ShapeDtypeStruct((1, w_conv), F32)],
        scratch_shapes=[pltpu.VMEM((3, tr, LANES), BF16)],
        input_output_aliases={12: 0},
        compiler_params=_params(("parallel", "arbitrary", "arbitrary")),
    )(proj, proj, proj, dcat, proj, proj, proj, proj, proj, dcat, conv_w, g_conv, dproj)


STRIP = 16


def _suffix_operator(t):
    r = lax.broadcasted_iota(jnp.int32, (2 * t, t), 0)
    c = lax.broadcasted_iota(jnp.int32, (2 * t, t), 1)
    return jnp.where((r > c) & ((r < t) | (r - t > c)), 1.0, 0.0).astype(BF16)


def _strips(t):
    return [(i, slice(i * STRIP, (i + 1) * STRIP)) for i in range(t // STRIP)]


def _strip_mask(i, t):
    r = lax.broadcasted_iota(jnp.int32, (STRIP, t), 0) + i * STRIP
    c = lax.broadcasted_iota(jnp.int32, (STRIP, t), 1)
    return r > c


def _store_split(ref, rows, val, t):
    hi = val.astype(BF16)
    ref[rows, 0:t] = hi
    ref[rows, t:2 * t] = (val - hi.astype(F32)).astype(BF16)


def _sb_scores(z_s, split_s, zl_s, tot_s, keep_s, t, diag):
    for i, rows in _strips(t):
        z = z_s[rows, :]
        e = jnp.exp(-jnp.abs(z))
        den = 1.0 + e
        log_keep = jnp.minimum(-z, 0.0) - jnp.log(den)
        if diag:
            log_keep = jnp.where(_strip_mask(i, t), log_keep, 0.0)
        _store_split(split_s, rows, log_keep, t)
        zl_s[rows, :] = z + log_keep
        tot_s[rows, :] = _row_sum(log_keep)
        if keep_s is not None:
            keep_s[rows, :] = jnp.where(z >= 0.0, e, 1.0) / den


def _row_sum(v):
    return jnp.broadcast_to(jnp.sum(v, axis=-1, keepdims=True), (v.shape[0], LANES))


def _wide(r, t):
    return jnp.concatenate([r] * (t // LANES), axis=1)


def _sb_weights(zl_s, suf_s, run_s, tot_s, a_s, t, diag, da_s=None, glog_s=None, gsplit_s=None, gtot_s=None):
    for i, rows in _strips(t):
        run = run_s[rows, :]
        a = jnp.exp(zl_s[rows, :] + suf_s[rows, :] + _wide(run, t))
        if diag:
            a = jnp.where(_strip_mask(i, t), a, 0.0)
        ab = a.astype(BF16)
        a_s[rows, :] = ab
        run_s[rows, :] = run + tot_s[rows, :]
        if da_s is not None:
            glog = ab.astype(F32) * da_s[rows, :]
            glog_s[rows, :] = glog
            _store_split(gsplit_s, rows, glog, t)
            gtot_s[rows, :] = _row_sum(glog)


def _sb_dscores(glog_s, cum_s, rest_s, gtot_s, keep_s, dz_s, t, diag):
    for i, rows in _strips(t):
        glog = glog_s[rows, :]
        rest = rest_s[rows, :]
        before = _wide(rest, t) - cum_s[rows, :] - glog
        dz = (glog + before) * keep_s[rows, :] - before
        if diag:
            dz = jnp.where(_strip_mask(i, t), dz, 0.0)
        dz_s[rows, :] = dz.astype(BF16)
        rest_s[rows, :] = rest - gtot_s[rows, :]


def _attn_fwd(proj, g_attn, cat, w_conv, staged, t=ATTN_BLOCK):
    s = proj.shape[0]
    w_attn = g_attn.shape[1]
    nh = w_attn // LANES
    t = _tile(s, t)
    nq = s // t
    q0 = 3 * w_conv // LANES
    scale = HEAD_DIM ** -0.5
    nw = len(staged)

    def body(q_ref, k_ref, v_ref, g_ref, cat_in, *rest):
        staged_refs, rest = rest[:nw], rest[nw:]
        o_ref, cat_ref = rest[:2]
        gathered_refs, rest = rest[2:2 + nw], rest[2 + nw:]
        kb, vb, tri_s, qm_s, z_s, split_s, zl_s, suf_s, a_s, run_s, tot_s, acc_s = rest[:12]
        gather_sems = rest[12:]
        del cat_in
        qi = pl.program_id(1)

        @pl.when((pl.program_id(0) == 0) & (qi == 0))
        def _():
            for cp in _gather_copies(staged_refs, gathered_refs, *gather_sems):
                cp.start()

        @pl.when(qi == 0)
        def _():
            kb[...] = k_ref[...].astype(BF16)
            vb[...] = v_ref[...].astype(BF16)
            tri_s[...] = _suffix_operator(t)

        low = _low_half()
        q = q_ref[...] * scale
        for h, msk in enumerate((low, jnp.logical_not(low))):
            qm_s[h] = jnp.where(msk, q, 0.0).astype(BF16)
            run_s[h] = jnp.zeros((t, LANES), F32)
            acc_s[h] = jnp.zeros((t, LANES), F32)

        def key_rows(kblk):
            return pl.ds(pl.multiple_of(kblk * t, t), t)

        def scores_matmul(kblk):
            ks = kb[key_rows(kblk), :]
            for h in range(2):
                z_s[h] = lax.dot_general(qm_s[h], ks, _NT, preferred_element_type=F32)

        def step(kblk, diag):
            vs = vb[key_rows(kblk), :]
            for h in range(2):
                _sb_scores(z_s.at[h], split_s.at[h], zl_s.at[h], tot_s.at[h], None, t, diag)
                suf_s[h] = jnp.dot(split_s[h], tri_s[...], preferred_element_type=F32)
            scores_matmul(jnp.maximum(kblk - 1, 0))
            for h in range(2):
                _sb_weights(zl_s.at[h], suf_s.at[h], run_s.at[h], tot_s.at[h], a_s.at[h], t, diag)
                acc_s[h] += jnp.dot(a_s[h], vs, preferred_element_type=F32)

        scores_matmul(qi)
        step(qi, True)

        def loop(it, carry):
            step(qi - 1 - it, False)
            return carry

        lax.fori_loop(0, qi, loop, 0)
        o = jnp.where(low, acc_s[0], acc_s[1])
        o_ref[...] = o
        r = lax.rsqrt(_half_mean(o * o, low) + EPS)
        cat_ref[...] = (o * r * g_ref[...]).astype(BF16)

        @pl.when((pl.program_id(0) == nh - 1) & (qi == nq - 1))
        def _():
            for cp in _gather_copies(staged_refs, gathered_refs, *gather_sems):
                cp.wait()

    whole = lambda col0: pl.BlockSpec((s, LANES), lambda h, i: (0, col0 + h))
    res = _pcall(
        body, name="attn_fwd", grid=(nh, nq),
        in_specs=[pl.BlockSpec((t, LANES), lambda h, i: (i, q0 + h)),
                  whole(q0 + nh), whole(q0 + 2 * nh),
                  pl.BlockSpec((1, LANES), lambda h, i: (0, h)),
                  pl.BlockSpec(memory_space=pl.ANY)] + [pl.BlockSpec(memory_space=pl.ANY)] * nw,
        out_specs=[pl.BlockSpec((t, LANES), lambda h, i: (i, h)),
                   pl.BlockSpec((t, LANES), lambda h, i: (i, w_conv // LANES + h))]
        + [pl.BlockSpec(memory_space=pl.ANY)] * nw,
        out_shape=[jax.ShapeDtypeStruct((s, w_attn), F32),
                   jax.ShapeDtypeStruct(cat.shape, BF16)]
        + [jax.ShapeDtypeStruct((N_DEV, *a.shape), BF16) for a in staged],
        scratch_shapes=[pltpu.VMEM((s, LANES), BF16), pltpu.VMEM((s, LANES), BF16),
                        pltpu.VMEM((2 * t, t), BF16),
                        pltpu.VMEM((2, t, LANES), BF16),
                        pltpu.VMEM((2, t, t), F32),
                        pltpu.VMEM((2, t, 2 * t), BF16),
                        pltpu.VMEM((2, t, t), F32),
                        pltpu.VMEM((2, t, t), F32),
                        pltpu.VMEM((2, t, t), BF16),
                        pltpu.VMEM((2, t, LANES), F32),
                        pltpu.VMEM((2, t, LANES), F32),
                        pltpu.VMEM((2, t, LANES), F32)]
        + _exchange_sems(nw, local=True),
        input_output_aliases={4: 1},
        compiler_params=_params(("arbitrary", "arbitrary")),
    )(proj, proj, proj, g_attn, cat, *staged)
    return res[0], res[1], res[2:]


def _attn_bwd(proj, o, dcat, g_attn, w_conv, partials, t=ATTN_BLOCK):
    s, n_cols = proj.shape
    w_attn = g_attn.shape[1]
    nh = w_attn // LANES
    t = _tile(s, t)
    nq = s // t
    q0 = 3 * w_conv // LANES
    scale = HEAD_DIM ** -0.5
    nw = len(partials)

    def body(q_ref, k_ref, v_ref, o_ref, do_ref, g_ref, *rest):
        partial_refs, rest = rest[:nw], rest[nw:]
        dproj_ref, dg_ref = rest[:2]
        received_refs, rest = rest[2:2 + nw], rest[2 + nw:]
        (kb, vb, dkt_acc, dvt_acc, stash, tri_s, qm_s, dom_s, qt_s, dot_s, z_s, da_s, split_s, zl_s,
         keep_s, suf_s, a_s, glog_s, gsplit_s, cum_s, dz_s, run_s, tot_s, rest_s, gtot_s, dq_s) = rest[:26]
        scatter_sems = rest[26:]
        step_i = pl.program_id(1)
        which = pl.program_id(2)
        qi = nq - 1 - step_i
        head_pair = pl.program_id(0)

        @pl.when((head_pair == 0) & (step_i == 0) & (which == 0))
        def _():
            for cp in _scatter_copies(partial_refs, received_refs, *scatter_sems):
                cp.start()

        @pl.when((head_pair == nh - 1) & (step_i == nq - 1) & (which == 2))
        def _():
            for cp in _scatter_copies(partial_refs, received_refs, *scatter_sems):
                cp.wait()

        @pl.when(which == 0)
        def _():
            @pl.when(step_i == 0)
            def _():
                kb[...] = k_ref[...].astype(BF16)
                vb[...] = v_ref[...].astype(BF16)
                tri_s[...] = _suffix_operator(t)
                dkt_acc[...] = jnp.zeros_like(dkt_acc)
                dvt_acc[...] = jnp.zeros_like(dvt_acc)
                dg_ref[...] = jnp.zeros_like(dg_ref)

            low = _low_half()
            q = q_ref[...] * scale
            ov = o_ref[...]
            d_o, dgn = _head_norm_bwd(ov, do_ref[...], g_ref[...], low)
            dg_ref[...] += jnp.sum(dgn, axis=0, keepdims=True)
            for h, msk in enumerate((low, jnp.logical_not(low))):
                qh = jnp.where(msk, q, 0.0)
                doh = jnp.where(msk, d_o, 0.0)
                dom = doh.astype(BF16)
                qm_s[h] = qh.astype(BF16)
                dom_s[h] = dom
                qt_s[h] = qh.T.astype(BF16)
                dot_s[h] = doh.T.astype(BF16)
                rest_s[h] = _row_sum(dom.astype(F32) * ov)
                run_s[h] = jnp.zeros((t, LANES), F32)
                dq_s[h] = jnp.zeros((t, LANES), F32)

            def key_rows(kblk):
                return pl.ds(pl.multiple_of(kblk * t, t), t)

            def scores_matmul(kblk):
                ks = kb[key_rows(kblk), :]
                for h in range(2):
                    z_s[h] = lax.dot_general(qm_s[h], ks, _NT, preferred_element_type=F32)

            def da_matmul(kblk):
                vs = vb[key_rows(kblk), :]
                for h in range(2):
                    da_s[h] = lax.dot_general(dom_s[h], vs, _NT, preferred_element_type=F32)

            def step(kblk, diag):
                ks = kb[key_rows(kblk), :]
                nxt = jnp.maximum(kblk - 1, 0)
                for h in range(2):
                    _sb_scores(z_s.at[h], split_s.at[h], zl_s.at[h], tot_s.at[h], keep_s.at[h], t, diag)
                    suf_s[h] = jnp.dot(split_s[h], tri_s[...], preferred_element_type=F32)
                scores_matmul(nxt)
                for h in range(2):
                    _sb_weights(zl_s.at[h], suf_s.at[h], run_s.at[h], tot_s.at[h], a_s.at[h], t, diag,
                                da_s.at[h], glog_s.at[h], gsplit_s.at[h], gtot_s.at[h])
                    cum_s[h] = jnp.dot(gsplit_s[h], tri_s[...], preferred_element_type=F32)
                da_matmul(nxt)
                dkt = dkt_acc[kblk]
                dvt = dvt_acc[kblk]
                for h in range(2):
                    _sb_dscores(glog_s.at[h], cum_s.at[h], rest_s.at[h], gtot_s.at[h], keep_s.at[h],
                                dz_s.at[h], t, diag)
                    dq_s[h] += jnp.dot(dz_s[h], ks, preferred_element_type=F32)
                    dkt = dkt + jnp.dot(qt_s[h], dz_s[h], preferred_element_type=F32)
                    dvt = dvt + jnp.dot(dot_s[h], a_s[h], preferred_element_type=F32)
                dkt_acc[kblk] = dkt
                dvt_acc[kblk] = dvt

            scores_matmul(qi)
            da_matmul(qi)
            step(qi, True)

            def loop(it, carry):
                step(qi - 1 - it, False)
                return carry

            lax.fori_loop(0, qi, loop, 0)
            stash[0] = (jnp.where(low, dq_s[0], dq_s[1]) * scale).astype(BF16)
            stash[1] = dkt_acc[qi].T.astype(BF16)
            stash[2] = dvt_acc[qi].T.astype(BF16)

        dproj_ref[...] = stash[which]

    whole = lambda col0: pl.BlockSpec((s, LANES), lambda h, i, w: (0, col0 + h))
    blk = lambda col0: pl.BlockSpec((t, LANES), lambda h, i, w: (nq - 1 - i, col0 + h))
    res = _pcall(
        body, name="attn_bwd", grid=(nh, nq, 3),
        in_specs=[blk(q0), whole(q0 + nh), whole(q0 + 2 * nh), blk(0), blk(w_conv // LANES),
                  pl.BlockSpec((1, LANES), lambda h, i, w: (0, h))] + [pl.BlockSpec(memory_space=pl.ANY)] * nw,
        out_specs=[pl.BlockSpec((t, LANES), lambda h, i, w: (nq - 1 - i, q0 + w * nh + h)),
                   pl.BlockSpec((1, LANES), lambda h, i, w: (0, h))] + [pl.BlockSpec(memory_space=pl.ANY)] * nw,
        out_shape=[jax.ShapeDtypeStruct((s, n_cols), BF16), jax.ShapeDtypeStruct((1, w_attn), F32)]
        + [jax.ShapeDtypeStruct((N_PEERS, *a.shape[1:]), BF16) for a in partials],
        scratch_shapes=[pltpu.VMEM((s, LANES), BF16), pltpu.VMEM((s, LANES), BF16),
                        pltpu.VMEM((nq, LANES, t), F32),
                        pltpu.VMEM((nq, LANES, t), F32),
                        pltpu.VMEM((3, t, LANES), BF16),
                        pltpu.VMEM((2 * t, t), BF16),
                        pltpu.VMEM((2, t, LANES), BF16),
                        pltpu.VMEM((2, t, LANES), BF16),
                        pltpu.VMEM((2, LANES, t), BF16),
                        pltpu.VMEM((2, LANES, t), BF16),
                        pltpu.VMEM((2, t, t), F32),
                        pltpu.VMEM((2, t, t), F32),
                        pltpu.VMEM((2, t, 2 * t), BF16),
                        pltpu.VMEM((2, t, t), F32),
                        pltpu.VMEM((2, t, t), F32),
                        pltpu.VMEM((2, t, t), F32),
                        pltpu.VMEM((2, t, t), BF16),
                        pltpu.VMEM((2, t, t), F32),
                        pltpu.VMEM((2, t, 2 * t), BF16),
                        pltpu.VMEM((2, t, t), F32),
                        pltpu.VMEM((2, t, t), BF16),
                        pltpu.VMEM((2, t, LANES), F32),
                        pltpu.VMEM((2, t, LANES), F32),
                        pltpu.VMEM((2, t, LANES), F32),
                        pltpu.VMEM((2, t, LANES), F32),
                        pltpu.VMEM((2, t, LANES), F32)]
        + _exchange_sems(nw),
        compiler_params=_params(("arbitrary", "arbitrary", "arbitrary")),
    )(proj, proj, proj, o, dcat, g_attn, *partials)
    return res[0], res[1], res[2:]


def _place():
    return lax.axis_index("x"), lax.axis_index("y"), lax.axis_index("c")


def _other_chips(x, y):
    return [(1 - x, y), (x, 1 - y), (1 - x, 1 - y)]


def _slot(px, py, pc):
    return 4 * px + 2 * py + pc


def _all_gather(shards, out_dtypes):
    nw = len(shards)

    def body(*refs):
        ins, outs, stage = refs[:nw], refs[nw:2 * nw], refs[2 * nw:3 * nw]
        send_sems, recv_sems, local_sems = refs[3 * nw:]
        x, y, c = _place()
        me, sibling = (x, y, c), (x, y, 1 - c)
        chips = _other_chips(x, y)

        def copy(w, k, block, to, src=None):
            dst = outs[w].at[_slot(*block)]
            return pltpu.make_async_remote_copy(
                src_ref=dst if src is None else src, dst_ref=dst,
                send_sem=send_sems.at[w * 7 + k], recv_sem=recv_sems.at[w * 7 + k],
                device_id=to, device_id_type=MESH)

        started = []
        local = []
        for w in range(nw):
            stage[w][...] = ins[w][...].astype(stage[w].dtype)
            cp = pltpu.make_async_copy(stage[w], outs[w].at[_slot(*me)], local_sems.at[w])
            cp.start()
            local.append(cp)
            started.append(copy(w, 0, me, sibling, src=stage[w]))
            started[-1].start()
            for j, chip in enumerate(chips):
                started.append(copy(w, 1 + j, me, (*chip, c), src=stage[w]))
                started[-1].start()
        for j, chip in enumerate(chips):
            for w in range(nw):
                copy(w, 1 + j, (*chip, c), me).wait_recv()
                started.append(copy(w, 4 + j, (*chip, c), sibling))
                started[-1].start()
        for w in range(nw):
            copy(w, 0, sibling, me).wait_recv()
            for j, chip in enumerate(chips):
                copy(w, 4 + j, (*chip, 1 - c), me).wait_recv()
        for cp in started:
            cp.wait_send()
        for cp in local:
            cp.wait()

    return _pcall(
        body, name="all_gather_weights",
        in_specs=[pl.BlockSpec(memory_space=pltpu.VMEM)] * nw,
        out_specs=[pl.BlockSpec(memory_space=pl.ANY)] * nw,
        out_shape=[jax.ShapeDtypeStruct((N_DEV, *a.shape), d) for a, d in zip(shards, out_dtypes)],
        scratch_shapes=[pltpu.VMEM(a.shape, d) for a, d in zip(shards, out_dtypes)]
        + [pltpu.SemaphoreType.DMA((7 * nw,)), pltpu.SemaphoreType.DMA((7 * nw,)),
           pltpu.SemaphoreType.DMA((nw,))],
        compiler_params=_params(),
    )(*shards)


N_PEERS = N_DEV - 1


def _peer(k):
    x, y, c = _place()
    return (x ^ (k >> 2), y ^ ((k >> 1) & 1), c ^ (k & 1))


def _fanout(src_of, dst_of, nw, send_sems, recv_sems):
    return [pltpu.make_async_remote_copy(
        src_ref=src_of(w, k), dst_ref=dst_of(w, k),
        send_sem=send_sems.at[w * N_PEERS + k - 1], recv_sem=recv_sems.at[w * N_PEERS + k - 1],
        device_id=_peer(k), device_id_type=MESH) for w in range(nw) for k in range(1, N_DEV)]


def _gather_copies(staged, gathered, send_sems, recv_sems, local_sems):
    me = _slot(*_place())
    nw = len(staged)
    remote = _fanout(lambda w, k: staged[w], lambda w, k: gathered[w].at[me], nw, send_sems, recv_sems)
    local = [pltpu.make_async_copy(staged[w], gathered[w].at[me], local_sems.at[w]) for w in range(nw)]
    return remote + local


def _scatter_copies(partials, received, send_sems, recv_sems):
    me = _slot(*_place())
    return _fanout(lambda w, k: partials[w].at[me ^ k], lambda w, k: received[w].at[k - 1],
                   len(partials), send_sems, recv_sems)


def _exchange_sems(nw, local=False):
    sems = [pltpu.SemaphoreType.DMA((N_PEERS * nw,)), pltpu.SemaphoreType.DMA((N_PEERS * nw,))]
    return sems + ([pltpu.SemaphoreType.DMA((nw,))] if local else [])


def _grad_exchange(partials):
    nw = len(partials)

    def body(*refs):
        copies = _scatter_copies(refs[:nw], refs[nw:2 * nw], *refs[2 * nw:])
        for cp in copies:
            cp.start()
        for cp in copies:
            cp.wait()

    return _pcall(
        body, name="grad_exchange",
        in_specs=[pl.BlockSpec(memory_space=pl.ANY)] * nw,
        out_specs=[pl.BlockSpec(memory_space=pl.ANY)] * nw,
        out_shape=[jax.ShapeDtypeStruct((N_PEERS, *a.shape[1:]), BF16) for a in partials],
        scratch_shapes=_exchange_sems(nw),
        compiler_params=_params(),
    )(*partials)


def _cast_shards(shards):
    def body(*refs):
        for src, dst in zip(refs[:len(shards)], refs[len(shards):]):
            dst[...] = src[...].astype(BF16)

    return _pcall(
        body, name="cast_shards",
        in_specs=[pl.BlockSpec(memory_space=pltpu.VMEM)] * len(shards),
        out_specs=[pl.BlockSpec(memory_space=pltpu.VMEM)] * len(shards),
        out_shape=[jax.ShapeDtypeStruct(a.shape, BF16) for a in shards],
        compiler_params=_params(),
    )(*shards)


def _all_reduce_small(packed):
    r = packed.shape[0]

    def body(x_ref, o_ref, gathered, send_sems, recv_sems):
        x, y, c = _place()
        me = _slot(x, y, c)
        gathered[me] = x_ref[...]
        peers = [(px, py, pc) for px in range(2) for py in range(2) for pc in range(2)]
        started = []
        for k in range(1, N_DEV):
            to = (x ^ (k >> 2), y ^ ((k >> 1) & 1), c ^ (k & 1))
            cp = pltpu.make_async_remote_copy(
                src_ref=x_ref, dst_ref=gathered.at[me],
                send_sem=send_sems.at[k - 1], recv_sem=recv_sems.at[k - 1],
                device_id=to, device_id_type=MESH)
            cp.start()
            started.append(cp)
        del peers
        for cp in started:
            cp.wait()
        total = gathered[0]
        for k in range(1, N_DEV):
            total = total + gathered[k]
        o_ref[...] = total

    return _pcall(
        body, name="all_reduce_small",
        in_specs=[pl.BlockSpec(memory_space=pltpu.VMEM)],
        out_specs=pl.BlockSpec(memory_space=pltpu.VMEM),
        out_shape=jax.ShapeDtypeStruct(packed.shape, F32),
        scratch_shapes=[pltpu.VMEM((N_DEV, r, LANES), F32),
                        pltpu.SemaphoreType.DMA((N_DEV - 1,)), pltpu.SemaphoreType.DMA((N_DEV - 1,))],
        compiler_params=_params(),
    )(packed)


def _adam_math(w, g, m, v):
    m = ADAM_B1 * m + (1.0 - ADAM_B1) * g
    v = ADAM_B2 * v + (1.0 - ADAM_B2) * jnp.square(g)
    m_hat = m / (1.0 - ADAM_B1 ** ADAM_STEP)
    v_hat = v / (1.0 - ADAM_B2 ** ADAM_STEP)
    delta = -ADAM_LR * (m_hat / (jnp.sqrt(v_hat) + ADAM_EPS) + ADAM_WD * w)
    return delta, m, v


def _adam_sharded(name, own, received, w, m, v, place, tr=256):
    r, cdim = w.shape
    tr = _tile(r, tr) if r % LANES == 0 else r

    def body(place_ref, own_ref, rec_ref, w_ref, m_ref, v_ref, g_ref, d_ref, nm_ref, nv_ref):
        del place_ref
        g = own_ref[...]
        for j in range(N_PEERS):
            g = g + rec_ref[j].astype(F32)
        delta, nm, nv = _adam_math(w_ref[...], g, m_ref[...], v_ref[...])
        g_ref[...] = g
        d_ref[...] = delta
        nm_ref[...] = nm
        nv_ref[...] = nv

    blk = pl.BlockSpec((tr, cdim), lambda i, pr: (i, 0))
    grid_spec = pltpu.PrefetchScalarGridSpec(
        num_scalar_prefetch=1, grid=(r // tr,),
        in_specs=[pl.BlockSpec((None, tr, cdim), lambda i, pr: (4 * pr[0] + 2 * pr[1] + pr[2], i, 0)),
                  pl.BlockSpec((N_PEERS, tr, cdim), lambda i, pr: (0, i, 0)), blk, blk, blk],
        out_specs=[blk] * 4)
    return _pcall(body, name=name, grid_spec=grid_spec,
                  out_shape=[jax.ShapeDtypeStruct((r, cdim), F32)] * 4,
                  compiler_params=_params(("parallel",)))(place, own, received, w, m, v)


def _adam_small(w, g, m, v):
    def body(w_ref, g_ref, m_ref, v_ref, d_ref, nm_ref, nv_ref):
        delta, nm, nv = _adam_math(w_ref[...], g_ref[...], m_ref[...], v_ref[...])
        d_ref[...] = delta
        nm_ref[...] = nm
        nv_ref[...] = nv

    return _pcall(body, name="adam_small",
                  in_specs=[pl.BlockSpec(memory_space=pltpu.VMEM)] * 4,
                  out_specs=[pl.BlockSpec(memory_space=pltpu.VMEM)] * 3,
                  out_shape=[jax.ShapeDtypeStruct(w.shape, F32)] * 3,
                  compiler_params=_params())(w, g, m, v)


def _rows(vec):
    return vec.reshape(-1, LANES)


def kernel(x, p, g_mix, w_in, conv_w, g_conv_out, g_attn_out, w_out, g_mlp, w_up, w_down, g_ple, w_ple_gate, w_ple_proj, g_final, loss_target, m_g_mix, m_w_in, m_conv_w, m_g_conv_out, m_g_attn_out, m_w_out, m_g_mlp, m_w_up, m_w_down, m_g_ple, m_w_ple_gate, m_w_ple_proj, m_g_final, v_g_mix, v_w_in, v_conv_w, v_g_conv_out, v_g_attn_out, v_w_out, v_g_mlp, v_w_up, v_w_down, v_g_ple, v_w_ple_gate, v_w_ple_proj, v_g_final):
    s, d = x.shape[1], x.shape[2]
    w_conv = g_conv_out.shape[1]
    w_attn = g_attn_out.shape[1]
    cw = conv_w.shape[2]
    xs, ps, tgt = x[0], p[0, 0], loss_target[0]
    place = jnp.stack([lax.axis_index("x"), lax.axis_index("y"), lax.axis_index("c")]).astype(jnp.int32)
    my_slot = 4 * place[0] + 2 * place[1] + place[2]

    conv_tile = jnp.pad(conv_w[0], ((0, HALO - CONV_K), (0, LANES - cw)))
    big = [w_in[0], w_out[0], w_up[0], w_down[0], w_ple_gate[0], w_ple_proj[0]]
    win_g, conv_g = _all_gather([big[0], conv_tile], [BF16, F32])
    staged = _cast_shards(big[1:])
    conv_full = jnp.transpose(conv_g[:, :CONV_K, :cw], (1, 0, 2)).reshape(CONV_K, w_conv)
    in_shard, up_shard, proj_shard = big[0].shape[1], big[2].shape[1], big[5].shape[1]

    a = _rmsnorm_fwd("norm_mix", xs, g_mix)
    proj, = _mm_nn("in_proj", a, win_g, n_shard=in_shard, tn=in_shard)
    cat = _conv_fwd(proj, conv_full, g_conv_out, w_conv, d)
    o, cat, (wout_g, wup_g, wdown_g, wgate_g, wproj_g) = _attn_fwd(proj, g_attn_out, cat, w_conv, staged)
    wout_f = wout_g.reshape(-1, wout_g.shape[-1])
    wdown_f = wdown_g.reshape(-1, wdown_g.shape[-1])
    wgate_f = wgate_g.reshape(-1, wgate_g.shape[-1])
    h1, = _mm_nn("out_proj", cat, wout_f, epilogue=_ep_residual, extras=(xs,))
    mn = _rmsnorm_fwd("norm_mlp", h1, g_mlp)
    u, act = _mm_nn("mlp_up", mn, wup_g, n_shard=up_shard, epilogue=_ep_up, out_dtypes=(F32, BF16))
    h2, = _mm_nn("mlp_down", act, wdown_f, epilogue=_ep_residual, extras=(h1,))
    n3 = _rmsnorm_fwd("norm_ple", h2, g_ple)
    gl, = _mm_nn("ple_gate", n3, wgate_f)
    pp, = _mm_nn("ple_proj", ps, wproj_g, n_shard=proj_shard)
    loss_part, dh3, dgl, dpp, dg_final = _ple_loss(h2, gl, pp, tgt, g_final.reshape(1, d))
    loss = lax.psum(loss_part[0, 0], ("x", "y", "c"))

    def slots(t2d):
        return t2d.reshape(N_DEV, -1, t2d.shape[-1])

    dw_proj = _mm_tn("d_w_ple_proj", ps, dpp, n_shard=proj_shard)
    dw_gate = [slots(t) for t in _mm_tn("d_w_ple_gate", n3, dgl)]
    dn3, = _mm_nt("d_norm_ple", dgl, wgate_f)
    dh2, dh2b, dg_ple = _rmsnorm_bwd("norm_ple_bwd", dn3, h2, g_ple, dh3)
    du, = _mm_nt("d_mlp_act", dh2b, wdown_f, epilogue=_ep_dact, out_dtypes=(BF16,), extras=(u,))
    dw_down = [slots(t) for t in _mm_tn("d_w_down", act, dh2b)]
    dw_up = _mm_tn("d_w_up", mn, du, n_shard=up_shard)
    dmn, = _mm_nt("d_norm_mlp", du, wup_g, k_shard=up_shard)
    dh1, dh1b, dg_mlp = _rmsnorm_bwd("norm_mlp_bwd", dmn, h1, g_mlp, dh2)
    dcat, = _mm_nt("d_cat", dh1b, wout_f)
    dw_out = [slots(t) for t in _mm_tn("d_w_out", cat, dh1b)]
    late = [dw_out, dw_up, dw_down, dw_gate, dw_proj]
    dproj, dg_attn, late_recv = _attn_bwd(proj, o, dcat, g_attn_out, w_conv, [pb for _, pb in late])
    dproj, dconv, dg_conv = _conv_bwd(proj, dcat, conv_full, g_conv_out, dproj, w_conv)
    dw_in = _mm_tn("d_w_in", a, dproj, n_shard=in_shard, tn=in_shard)
    in_recv, = _grad_exchange([dw_in[1]])
    da, = _mm_nt("d_norm_mix", dproj, win_g, k_shard=in_shard, tk=in_shard)
    grad_x, _, dg_mix = _rmsnorm_bwd("norm_mix_bwd", da, xs, g_mix, dh1)

    names = ["w_in", "w_out", "w_up", "w_down", "w_ple_gate", "w_ple_proj"]
    owns = [dw_in[0]] + [pf for pf, _ in late]
    recvs = [in_recv, *late_recv]
    moments = [(m_w_in, v_w_in), (m_w_out, v_w_out), (m_w_up, v_w_up), (m_w_down, v_w_down),
               (m_w_ple_gate, v_w_ple_gate), (m_w_ple_proj, v_w_ple_proj)]
    big_out = {}
    for n, own, rc, wt, (mm, vv) in zip(names, owns, recvs, big, moments):
        big_out[n] = [t[None] for t in _adam_sharded("adam_" + n, own, rc, wt, mm[0], vv[0], place)]

    small_g = jnp.concatenate(
        [_rows(dg_mix[0]), _rows(dg_conv[0]), _rows(dg_attn[0]), _rows(dg_mlp[0]), _rows(dg_ple[0]),
         _rows(dg_final[0]), _rows(dconv.reshape(-1))], axis=0)
    n_gain_rows = small_g.shape[0] - CONV_K * w_conv // LANES
    pad_rows = (-small_g.shape[0]) % HALO
    small_g = _all_reduce_small(jnp.pad(small_g, ((0, pad_rows), (0, 0))))
    dconv_full = small_g[n_gain_rows:n_gain_rows + CONV_K * w_conv // LANES].reshape(CONV_K, w_conv)
    dconv_mine = lax.dynamic_slice(dconv_full, (0, my_slot * cw), (CONV_K, cw))

    def pack(vecs, conv_part):
        rows = [_rows(t.reshape(-1)) for t in vecs]
        rows.append(jnp.pad(conv_part, ((0, HALO - CONV_K), (0, LANES - cw))))
        return jnp.concatenate(rows, axis=0)

    gains = [g_mix, g_conv_out, g_attn_out, g_mlp, g_ple, g_final]
    gains_m = [m_g_mix, m_g_conv_out, m_g_attn_out, m_g_mlp, m_g_ple, m_g_final]
    gains_v = [v_g_mix, v_g_conv_out, v_g_attn_out, v_g_mlp, v_g_ple, v_g_final]
    gpack = jnp.concatenate([small_g[:n_gain_rows], jnp.pad(dconv_mine, ((0, HALO - CONV_K), (0, LANES - cw)))], axis=0)
    sd, sm, sv = _adam_small(pack(gains, conv_w[0]), gpack, pack(gains_m, m_conv_w[0]), pack(gains_v, v_conv_w[0]))

    def unpack(packed):
        out, r0 = [], 0
        for t in gains:
            nr = t.size // LANES
            out.append(packed[r0:r0 + nr].reshape(t.shape))
            r0 += nr
        out.append(packed[r0:r0 + CONV_K, :cw][None])
        return out

    sg_l, sd_l, sm_l, sv_l = unpack(gpack), unpack(sd), unpack(sm), unpack(sv)
    small_names = ["g_mix", "g_conv_out", "g_attn_out", "g_mlp", "g_ple", "g_final", "conv_w"]
    small_out = {n: [sg_l[i], sd_l[i], sm_l[i], sv_l[i]] for i, n in enumerate(small_names)}

    order = ["g_mix", "w_in", "conv_w", "g_conv_out", "g_attn_out", "w_out", "g_mlp", "w_up", "w_down",
             "g_ple", "w_ple_gate", "w_ple_proj", "g_final"]
    table = {**big_out, **small_out}
    outs = [loss, grad_x[None]]
    for kind in range(4):
        outs.extend(table[n][kind] for n in order)
    return tuple(outs)
```

```python
import functools

import jax
import jax.numpy as jnp
from jax import lax
from jax.experimental import pallas as pl
from jax.experimental.pallas import tpu as pltpu

F32 = jnp.float32
BF16 = jnp.bfloat16
EPS = 1e-6
HEAD_DIM = 64
LANES = 128
CONV_K = 3
ATTN_BLOCK = 256
HALO = 8
N_DEV = 8
MESH = pl.DeviceIdType.MESH
VMEM_LIMIT = 56 * 1024 * 1024

ADAM_LR = 0.001
ADAM_B1 = 0.9
ADAM_B2 = 0.999
ADAM_EPS = 1e-08
ADAM_WD = 0.01
ADAM_STEP = 10


def _pcall(body, **kw):
    return pl.pallas_call(body, **kw)


def _params(sem=None, **kw):
    return pltpu.CompilerParams(dimension_semantics=sem, vmem_limit_bytes=VMEM_LIMIT, **kw)


def _tile(dim, pref):
    t = min(dim, pref)
    while dim % t:
        t -= LANES
    assert t > 0, (dim, pref)
    return t


_NN = (((1,), (0,)), ((), ()))
_NT = (((1,), (1,)), ((), ()))
_TN = (((0,), (0,)), ((), ()))


def _ep_store(acc, outs):
    outs[0][...] = acc.astype(outs[0].dtype)


def _ep_both(acc, outs):
    outs[0][...] = acc
    outs[1][...] = acc.astype(BF16)


def _ep_residual(acc, res, outs):
    outs[0][...] = acc + res[...]


def _ep_up(acc, outs):
    outs[0][...] = jnp.square(jnp.maximum(acc, 0.0)).astype(BF16)


def _ep_dact(acc, act, outs):
    outs[0][...] = (acc * (2.0 * jnp.sqrt(act[...].astype(F32)))).astype(BF16)


def _matmul(name, a, b, *, dims, grid, a_spec, b_spec, acc_shape, out_shapes, out_specs,
            epilogue=_ep_store, extras=(), extra_specs=(), exchange=()):
    nk = grid[2]
    n_ex, n_out, n_xc = len(extras), len(out_shapes), len(exchange)
    last = tuple(g - 1 for g in grid)

    def product(a_ref, b_ref):
        return lax.dot_general(a_ref[...].astype(BF16), b_ref[...].astype(BF16), dims,
                               preferred_element_type=F32)

    def body(a_ref, b_ref, *rest):
        ex, rest = rest[:n_ex], rest[n_ex:]
        partials, rest = rest[:n_xc], rest[n_xc:]
        outs, rest = rest[:n_out], rest[n_out:]
        received, rest = rest[:n_xc], rest[n_xc:]
        ids = [pl.program_id(axis) for axis in range(3)]
        if n_xc:
            @pl.when((ids[0] == 0) & (ids[1] == 0) & (ids[2] == 0))
            def _():
                for cp in _scatter_copies(partials, received, *rest[-2:]):
                    cp.start()

        if nk == 1:
            epilogue(product(a_ref, b_ref), *ex, outs)
        else:
            acc = rest[0]

            @pl.when(ids[2] == 0)
            def _():
                acc[...] = product(a_ref, b_ref)

            @pl.when(ids[2] > 0)
            def _():
                acc[...] += product(a_ref, b_ref)

            @pl.when(ids[2] == nk - 1)
            def _():
                epilogue(acc[...], *ex, outs)

        if n_xc:
            @pl.when((ids[0] == last[0]) & (ids[1] == last[1]) & (ids[2] == last[2]))
            def _():
                for cp in _scatter_copies(partials, received, *rest[-2:]):
                    cp.wait()

    anywhere = [pl.BlockSpec(memory_space=pl.ANY)] * n_xc
    return _pcall(
        body, name=name, grid=grid,
        in_specs=[a_spec, b_spec, *extra_specs, *anywhere],
        out_specs=[*out_specs, *anywhere],
        out_shape=[*out_shapes, *(jax.ShapeDtypeStruct((N_PEERS, *p.shape[1:]), BF16) for p in exchange)],
        scratch_shapes=([] if nk == 1 else [pltpu.VMEM(acc_shape, F32)]) + (_exchange_sems(n_xc) if n_xc else []),
        compiler_params=_params(("arbitrary",) * 3 if n_xc else ("parallel", "parallel", "arbitrary")),
    )(a, b, *extras, *exchange)


def _mm_nn(name, a, w, *, n_shard=None, epilogue=_ep_store, out_dtypes=(F32,), extras=(), tm=1024, tn=1024, tk=1024):
    m, kd = a.shape
    if n_shard is None:
        n = w.shape[1]
        tn = _tile(n, tn)
        tk = _tile(kd, tk)
        b_spec = pl.BlockSpec((tk, tn), lambda i, j, k: (k, j))
    else:
        n = N_DEV * n_shard
        tn = _tile(n_shard, tn)
        tk = _tile(kd, tk)
        per = n_shard // tn
        b_spec = pl.BlockSpec((None, tk, tn), lambda i, j, k: (j // per, k, j % per))
    tm = _tile(m, tm)
    o_spec = pl.BlockSpec((tm, tn), lambda i, j, k: (i, j))
    return _matmul(
        name, a, w, dims=_NN, grid=(m // tm, n // tn, kd // tk),
        a_spec=pl.BlockSpec((tm, tk), lambda i, j, k: (i, k)), b_spec=b_spec,
        acc_shape=(tm, tn),
        out_shapes=[jax.ShapeDtypeStruct((m, n), d) for d in out_dtypes],
        out_specs=[o_spec] * len(out_dtypes),
        epilogue=epilogue, extras=extras, extra_specs=[o_spec] * len(extras))


def _mm_nt(name, a, w, *, k_shard=None, epilogue=_ep_store, out_dtypes=(F32,), extras=(), exchange=(),
           tm=1024, tn=1024, tk=1024):
    m, kd = a.shape
    if k_shard is None:
        n = w.shape[0]
        tn = _tile(n, tn)
        tk = _tile(kd, tk)
        b_spec = pl.BlockSpec((tn, tk), lambda i, j, k: (j, k))
    else:
        n = w.shape[1]
        tn = _tile(n, tn)
        tk = _tile(k_shard, tk)
        per = k_shard // tk
        b_spec = pl.BlockSpec((None, tn, tk), lambda i, j, k: (k // per, j, k % per))
    tm = _tile(m, tm)
    o_spec = pl.BlockSpec((tm, tn), lambda i, j, k: (i, j))
    return _matmul(
        name, a, w, dims=_NT, grid=(m // tm, n // tn, kd // tk),
        a_spec=pl.BlockSpec((tm, tk), lambda i, j, k: (i, k)), b_spec=b_spec,
        acc_shape=(tm, tn),
        out_shapes=[jax.ShapeDtypeStruct((m, n), d) for d in out_dtypes],
        out_specs=[o_spec] * len(out_dtypes),
        epilogue=epilogue, extras=extras, extra_specs=[o_spec] * len(extras), exchange=exchange)


def _mm_tn(name, a, b, *, n_shard=None, tm=1024, tn=1024, tk=1024):
    t, m = a.shape
    n = b.shape[1]
    tm = _tile(m, tm)
    tk = _tile(t, tk)
    if n_shard is None:
        tn = _tile(n, tn)
        o_spec = pl.BlockSpec((tm, tn), lambda i, j, k: (i, j))
        shape = (m, n)
    else:
        tn = _tile(n_shard, tn)
        per = n_shard // tn
        o_spec = pl.BlockSpec((None, tm, tn), lambda i, j, k: (j // per, i, j % per))
        shape = (N_DEV, m, n_shard)
    return _matmul(
        name, a, b, dims=_TN, grid=(m // tm, n // tn, t // tk),
        a_spec=pl.BlockSpec((tk, tm), lambda i, j, k: (k, i)),
        b_spec=pl.BlockSpec((tk, tn), lambda i, j, k: (k, j)),
        acc_shape=(tm, tn), epilogue=_ep_both,
        out_shapes=[jax.ShapeDtypeStruct(shape, F32), jax.ShapeDtypeStruct(shape, BF16)],
        out_specs=[o_spec, o_spec])


def _ple_proj(p, w_g, tm=1024):
    s, kd = p.shape
    ns = w_g.shape[2]
    tm = _tile(s, tm)

    def body(p_ref, w_ref, o_ref):
        pv = p_ref[...].astype(BF16)
        for j in range(N_DEV):
            o_ref[:, j * ns:(j + 1) * ns] = jnp.dot(pv, w_ref[j], preferred_element_type=F32)

    return _pcall(body, name="ple_proj", grid=(s // tm,),
                  in_specs=[pl.BlockSpec((tm, kd), lambda i: (i, 0)),
                            pl.BlockSpec((N_DEV, kd, ns), lambda i: (0, 0, 0))],
                  out_specs=pl.BlockSpec((tm, N_DEV * ns), lambda i: (i, 0)),
                  out_shape=jax.ShapeDtypeStruct((s, N_DEV * ns), F32),
                  compiler_params=_params(("parallel",)))(p, w_g)


def _d_ple_proj(p, dpp, ns, tk=1024):
    s, kd = p.shape
    tk = _tile(s, tk)
    nk = s // tk

    def body(p_ref, d_ref, of_ref, ob_ref, acc):
        k = pl.program_id(0)

        @pl.when(k == 0)
        def _():
            acc[...] = jnp.zeros_like(acc)

        pv = p_ref[...].astype(BF16)
        for j in range(N_DEV):
            acc[j] += lax.dot_general(pv, d_ref[:, j * ns:(j + 1) * ns], _TN, preferred_element_type=F32)

        @pl.when(k == nk - 1)
        def _():
            of_ref[...] = acc[...]
            ob_ref[...] = acc[...].astype(BF16)

    whole = pl.BlockSpec((N_DEV, kd, ns), lambda k: (0, 0, 0))
    return _pcall(body, name="d_w_ple_proj", grid=(nk,),
                  in_specs=[pl.BlockSpec((tk, kd), lambda k: (k, 0)),
                            pl.BlockSpec((tk, N_DEV * ns), lambda k: (k, 0))],
                  out_specs=[whole, whole],
                  out_shape=[jax.ShapeDtypeStruct((N_DEV, kd, ns), F32), jax.ShapeDtypeStruct((N_DEV, kd, ns), BF16)],
                  scratch_shapes=[pltpu.VMEM((N_DEV, kd, ns), F32)],
                  compiler_params=_params(("arbitrary",)))(p, dpp)


def _row_spec(tr, d):
    return pl.BlockSpec((tr, d), lambda i: (i, 0))


def _vec_spec(d):
    return pl.BlockSpec((1, d), lambda i: (0, 0))


def _rmsnorm_fwd(name, x, g, tr=512):
    s, d = x.shape
    tr = _tile(s, tr)

    def body(x_ref, g_ref, o_ref):
        xv = x_ref[...]
        r = lax.rsqrt(jnp.mean(xv * xv, axis=-1, keepdims=True) + EPS)
        o_ref[...] = (xv * r * g_ref[...]).astype(BF16)

    return _pcall(body, name=name, grid=(s // tr,),
                  in_specs=[_row_spec(tr, d), _vec_spec(d)], out_specs=_row_spec(tr, d),
                  out_shape=jax.ShapeDtypeStruct((s, d), BF16),
                  compiler_params=_params(("parallel",)))(x, g)


def _rmsnorm_bwd(name, dn, h, g, dres, tr=512):
    s, d = h.shape
    tr = _tile(s, tr)

    def body(dn_ref, h_ref, g_ref, dres_ref, dh_ref, dhb_ref, dg_ref):
        @pl.when(pl.program_id(0) == 0)
        def _():
            dg_ref[...] = jnp.zeros_like(dg_ref)

        hv = h_ref[...]
        dnv = dn_ref[...]
        r = lax.rsqrt(jnp.mean(hv * hv, axis=-1, keepdims=True) + EPS)
        hn = hv * r
        dg_ref[...] += jnp.sum(dnv * hn, axis=0, keepdims=True)
        dhn = dnv * g_ref[...]
        dh = dres_ref[...] + r * (dhn - hn * jnp.mean(dhn * hn, axis=-1, keepdims=True))
        dh_ref[...] = dh
        dhb_ref[...] = dh.astype(BF16)

    return _pcall(body, name=name, grid=(s // tr,),
                  in_specs=[_row_spec(tr, d), _row_spec(tr, d), _vec_spec(d), _row_spec(tr, d)],
                  out_specs=[_row_spec(tr, d), _row_spec(tr, d), _vec_spec(d)],
                  out_shape=[jax.ShapeDtypeStruct((s, d), F32), jax.ShapeDtypeStruct((s, d), BF16),
                             jax.ShapeDtypeStruct((1, d), F32)],
                  compiler_params=_params(("arbitrary",)))(dn, h, g, dres)


def _ple_loss(h2, gl, pp, tgt, g_final, tr=512):
    s, d = h2.shape
    tr = _tile(s, tr)

    def body(h2_ref, gl_ref, pp_ref, t_ref, g_ref, loss_ref, dh3_ref, dgl_ref, dpp_ref, dg_ref):
        @pl.when(pl.program_id(0) == 0)
        def _():
            dg_ref[...] = jnp.zeros_like(dg_ref)
            loss_ref[...] = jnp.zeros_like(loss_ref)

        gate = jax.nn.sigmoid(gl_ref[...])
        ppv = pp_ref[...]
        h3 = h2_ref[...] + gate * ppv
        r = lax.rsqrt(jnp.mean(h3 * h3, axis=-1, keepdims=True) + EPS)
        hn = h3 * r
        gv = g_ref[...]
        diff = hn * gv - t_ref[...]
        row = jnp.mean(diff * diff, axis=-1, keepdims=True)
        loss_ref[...] += 0.5 * jnp.sum(row, axis=0, keepdims=True)
        dy = diff * (1.0 / d)
        dg_ref[...] += jnp.sum(dy * hn, axis=0, keepdims=True)
        dhn = dy * gv
        dh3 = r * (dhn - hn * jnp.mean(dhn * hn, axis=-1, keepdims=True))
        dh3_ref[...] = dh3
        dgl_ref[...] = (dh3 * ppv * gate * (1.0 - gate)).astype(BF16)
        dpp_ref[...] = (dh3 * gate).astype(BF16)

    return _pcall(body, name="ple_loss", grid=(s // tr,),
                  in_specs=[_row_spec(tr, d)] * 4 + [_vec_spec(d)],
                  out_specs=[_vec_spec(LANES), _row_spec(tr, d), _row_spec(tr, d), _row_spec(tr, d), _vec_spec(d)],
                  out_shape=[jax.ShapeDtypeStruct((1, LANES), F32), jax.ShapeDtypeStruct((s, d), F32),
                             jax.ShapeDtypeStruct((s, d), BF16), jax.ShapeDtypeStruct((s, d), BF16),
                             jax.ShapeDtypeStruct((1, d), F32)],
                  compiler_params=_params(("arbitrary",)))(h2, gl, pp, tgt, g_final)


def _low_half():
    return lax.broadcasted_iota(jnp.int32, (1, LANES), 1) < HEAD_DIM


def _half_mean(v, low):
    s_lo = jnp.sum(jnp.where(low, v, 0.0), axis=-1, keepdims=True)
    s_hi = jnp.sum(jnp.where(low, 0.0, v), axis=-1, keepdims=True)
    return jnp.where(low, s_lo, s_hi) * (1.0 / HEAD_DIM)


def _head_norm_bwd(val, dout, g, low):
    r = lax.rsqrt(_half_mean(val * val, low) + EPS)
    vn = val * r
    dvn = dout * g
    return r * (dvn - vn * _half_mean(dvn * vn, low)), dout * vn


def _conv_taps(vv_ext, w_ref, rows):
    v0 = vv_ext[HALO:]
    v1 = pltpu.roll(vv_ext, 1, 0)[HALO:]
    v2 = pltpu.roll(vv_ext, 2, 0)[HALO:]
    del rows
    return w_ref[2:3, :] * v0 + w_ref[1:2, :] * v1 + w_ref[0:1, :] * v2, (v0, v1, v2)


def _conv_fwd(proj, conv_w, g_conv, w_conv, d_model, tr=512):
    s = proj.shape[0]
    tr = _tile(s, tr)
    hb = tr // HALO

    def main(part):
        return pl.BlockSpec((tr, w_conv), lambda i: (i, part))

    def prev(part):
        return pl.BlockSpec((HALO, w_conv), lambda i: (jnp.maximum(i * hb - 1, 0), part))

    def body(cb_ref, cc_ref, cu_ref, ccp_ref, cup_ref, w_ref, g_ref, o_ref):
        i = pl.program_id(0)
        low = _low_half()
        for j in range(w_conv // LANES):
            cols = slice(j * LANES, (j + 1) * LANES)
            vv_prev = jnp.where(i > 0, ccp_ref[:, cols] * cup_ref[:, cols], 0.0)
            vv_ext = jnp.concatenate([vv_prev, cc_ref[:, cols] * cu_ref[:, cols]], axis=0)
            y, _ = _conv_taps(vv_ext, w_ref.at[:, cols], tr)
            co = cb_ref[:, cols] * y
            r = lax.rsqrt(_half_mean(co * co, low) + EPS)
            o_ref[:, cols] = (co * r * g_ref[:, cols]).astype(BF16)

    return _pcall(
        body, name="conv_fwd", grid=(s // tr,),
        in_specs=[main(0), main(1), main(2), prev(1), prev(2),
                  pl.BlockSpec((CONV_K, w_conv), lambda i: (0, 0)),
                  pl.BlockSpec((1, w_conv), lambda i: (0, 0))],
        out_specs=pl.BlockSpec((tr, w_conv), lambda i: (i, 0)),
        out_shape=jax.ShapeDtypeStruct((s, d_model), BF16),
        compiler_params=_params(("parallel",)),
    )(proj, proj, proj, proj, proj, conv_w, g_conv)


def _conv_bwd(proj, dcat, conv_w, g_conv, dproj, w_conv, tr=512):
    s = proj.shape[0]
    tr = _tile(s, tr)
    hb = tr // HALO
    last = s // HALO - 1
    nt = s // tr

    def main(part):
        return pl.BlockSpec((tr, w_conv), lambda i: (i, part))

    def prev(part):
        return pl.BlockSpec((HALO, w_conv), lambda i: (jnp.maximum(i * hb - 1, 0), part))

    def nxt(part):
        return pl.BlockSpec((HALO, w_conv), lambda i: (jnp.minimum((i + 1) * hb, last), part))

    def body(cb_ref, cc_ref, cu_ref, dc_ref, ccp_ref, cup_ref, cbn_ref, ccn_ref, cun_ref, dcn_ref,
             w_ref, g_ref, dproj_in, dproj_ref, dw_ref, dg_ref):
        del dproj_in
        i = pl.program_id(0)

        @pl.when(i == 0)
        def _():
            dw_ref[...] = jnp.zeros_like(dw_ref)
            dg_ref[...] = jnp.zeros_like(dg_ref)

        low = _low_half()
        n_ext = tr + HALO
        rowid = lax.broadcasted_iota(jnp.int32, (n_ext, 1), 0)
        for j in range(w_conv // LANES):
            cols = slice(j * LANES, (j + 1) * LANES)
            wj = w_ref.at[:, cols]
            cc, cu = cc_ref[:, cols], cu_ref[:, cols]
            vv_prev = jnp.where(i > 0, ccp_ref[:, cols] * cup_ref[:, cols], 0.0)
            vv_ext = jnp.concatenate([vv_prev, cc * cu, ccn_ref[:, cols] * cun_ref[:, cols]], axis=0)
            y_ext, (v0, v1, v2) = _conv_taps(vv_ext, wj, n_ext)
            cb_ext = jnp.concatenate([cb_ref[:, cols], cbn_ref[:, cols]], axis=0)
            dc_ext = jnp.concatenate([dc_ref[:, cols], dcn_ref[:, cols]], axis=0)
            dco, dgn = _head_norm_bwd(cb_ext * y_ext, dc_ext, g_ref[:, cols], low)
            dyc = jnp.where((rowid < tr) | (i < nt - 1), dco * cb_ext, 0.0)
            dvv = (wj[2:3, :] * dyc[:tr] + wj[1:2, :] * pltpu.roll(dyc, n_ext - 1, 0)[:tr]
                   + wj[0:1, :] * pltpu.roll(dyc, n_ext - 2, 0)[:tr])
            dproj_ref[:, cols] = (dco[:tr] * y_ext[:tr]).astype(BF16)
            dproj_ref[:, w_conv + j * LANES:w_conv + (j + 1) * LANES] = (dvv * cu).astype(BF16)
            dproj_ref[:, 2 * w_conv + j * LANES:2 * w_conv + (j + 1) * LANES] = (dvv * cc).astype(BF16)
            dyt = dyc[:tr]
            for tap, shifted in enumerate((v2, v1, v0)):
                dw_ref[tap:tap + 1, cols] += jnp.sum(dyt * shifted[:tr], axis=0, keepdims=True)
            dg_ref[:, cols] += jnp.sum(dgn[:tr], axis=0, keepdims=True)

    n_cols = dproj.shape[1]
    return _pcall(
        body, name="conv_bwd", grid=(nt,),
        in_specs=[main(0), main(1), main(2), main(0),
                  prev(1), prev(2), nxt(0), nxt(1), nxt(2), nxt(0),
                  pl.BlockSpec((CONV_K, w_conv), lambda i: (0, 0)),
                  pl.BlockSpec((1, w_conv), lambda i: (0, 0)),
                  pl.BlockSpec(memory_space=pl.ANY)],
        out_specs=[pl.BlockSpec((tr, 3 * w_conv), lambda i: (i, 0)),
                   pl.BlockSpec((CONV_K, w_conv), lambda i: (0, 0)),
                   pl.BlockSpec((1, w_conv), lambda i: (0, 0))],
        out_shape=[jax.ShapeDtypeStruct((s, n_cols), BF16),
                   jax.ShapeDtypeStruct((CONV_K, w_conv), F32),
                   jax.ShapeDtypeStruct((1, w_conv), F32)],
        input_output_aliases={12: 0},
        compiler_params=_params(("arbitrary",)),
    )(proj, proj, proj, dcat, proj, proj, proj, proj, proj, dcat, conv_w, g_conv, dproj)


STRIP = 16


def _suffix_operator(t):
    r = lax.broadcasted_iota(jnp.int32, (2 * t, t), 0)
    c = lax.broadcasted_iota(jnp.int32, (2 * t, t), 1)
    return jnp.where((r > c) & ((r < t) | (r - t > c)), 1.0, 0.0).astype(BF16)


def _strips(t):
    return [(i, slice(i * STRIP, (i + 1) * STRIP)) for i in range(t // STRIP)]


def _strip_mask(i, t):
    r = lax.broadcasted_iota(jnp.int32, (STRIP, t), 0) + i * STRIP
    c = lax.broadcasted_iota(jnp.int32, (STRIP, t), 1)
    return r > c


def _store_split(ref, rows, val, t):
    hi = val.astype(BF16)
    ref[rows, 0:t] = hi
    ref[rows, t:2 * t] = (val - hi.astype(F32)).astype(BF16)


def _sb_scores(z_s, split_s, zl_s, tot_s, keep_s, t, diag):
    for i, rows in _strips(t):
        z = z_s[rows, :]
        e = jnp.exp(-jnp.abs(z))
        den = 1.0 + e
        log_keep = jnp.minimum(-z, 0.0) - jnp.log(den)
        if diag:
            log_keep = jnp.where(_strip_mask(i, t), log_keep, 0.0)
        _store_split(split_s, rows, log_keep, t)
        zl_s[rows, :] = z + log_keep
        tot_s[rows, :] = _row_sum(log_keep)
        if keep_s is not None:
            keep_s[rows, :] = jnp.where(z >= 0.0, e, 1.0) / den


def _row_sum(v):
    return jnp.broadcast_to(jnp.sum(v, axis=-1, keepdims=True), (v.shape[0], LANES))


def _wide(r, t):
    return jnp.concatenate([r] * (t // LANES), axis=1)


def _sb_weights(zl_s, suf_s, run_s, tot_s, a_s, t, diag, da_s=None, glog_s=None, gsplit_s=None, gtot_s=None):
    for i, rows in _strips(t):
        run = run_s[rows, :]
        a = jnp.exp(zl_s[rows, :] + suf_s[rows, :] + _wide(run, t))
        if diag:
            a = jnp.where(_strip_mask(i, t), a, 0.0)
        ab = a.astype(BF16)
        a_s[rows, :] = ab
        run_s[rows, :] = run + tot_s[rows, :]
        if da_s is not None:
            glog = ab.astype(F32) * da_s[rows, :]
            glog_s[rows, :] = glog
            _store_split(gsplit_s, rows, glog, t)
            gtot_s[rows, :] = _row_sum(glog)


def _sb_dscores(glog_s, cum_s, rest_s, gtot_s, keep_s, dz_s, t, diag):
    for i, rows in _strips(t):
        glog = glog_s[rows, :]
        rest = rest_s[rows, :]
        before = _wide(rest, t) - cum_s[rows, :] - glog
        dz = (glog + before) * keep_s[rows, :] - before
        if diag:
            dz = jnp.where(_strip_mask(i, t), dz, 0.0)
        dz_s[rows, :] = dz.astype(BF16)
        rest_s[rows, :] = rest - gtot_s[rows, :]


def _attn_fwd(proj, g_attn, cat, w_conv, staged, t=ATTN_BLOCK):
    s = proj.shape[0]
    w_attn = g_attn.shape[1]
    nh = w_attn // LANES
    t = _tile(s, t)
    nq = s // t
    q0 = 3 * w_conv // LANES
    scale = HEAD_DIM ** -0.5
    nw = len(staged)

    def body(q_ref, k_ref, v_ref, g_ref, cat_in, *rest):
        staged_refs, rest = rest[:nw], rest[nw:]
        o_ref, cat_ref = rest[:2]
        gathered_refs, rest = rest[2:2 + nw], rest[2 + nw:]
        kb, vb, tri_s, qm_s, z_s, split_s, zl_s, suf_s, a_s, run_s, tot_s, acc_s = rest[:12]
        gather_sems = rest[12:]
        del cat_in
        qi = pl.program_id(1)

        @pl.when((pl.program_id(0) == 0) & (qi == 0))
        def _():
            for cp in _gather_copies(staged_refs, gathered_refs, *gather_sems):
                cp.start()

        @pl.when(qi == 0)
        def _():
            kb[...] = k_ref[...].astype(BF16)
            vb[...] = v_ref[...].astype(BF16)
            tri_s[...] = _suffix_operator(t)

        low = _low_half()
        q = q_ref[...] * scale
        for h, msk in enumerate((low, jnp.logical_not(low))):
            qm_s[h] = jnp.where(msk, q, 0.0).astype(BF16)
            run_s[h] = jnp.zeros((t, LANES), F32)
            acc_s[h] = jnp.zeros((t, LANES), F32)

        def key_rows(kblk):
            return pl.ds(pl.multiple_of(kblk * t, t), t)

        def scores_matmul(kblk):
            ks = kb[key_rows(kblk), :]
            for h in range(2):
                z_s[h] = lax.dot_general(qm_s[h], ks, _NT, preferred_element_type=F32)

        def step(kblk, diag):
            vs = vb[key_rows(kblk), :]
            for h in range(2):
                _sb_scores(z_s.at[h], split_s.at[h], zl_s.at[h], tot_s.at[h], None, t, diag)
                suf_s[h] = jnp.dot(split_s[h], tri_s[...], preferred_element_type=F32)
            scores_matmul(jnp.maximum(kblk - 1, 0))
            for h in range(2):
                _sb_weights(zl_s.at[h], suf_s.at[h], run_s.at[h], tot_s.at[h], a_s.at[h], t, diag)
                acc_s[h] += jnp.dot(a_s[h], vs, preferred_element_type=F32)

        scores_matmul(qi)
        step(qi, True)

        def loop(it, carry):
            step(qi - 1 - it, False)
            return carry

        lax.fori_loop(0, qi, loop, 0)
        o = jnp.where(low, acc_s[0], acc_s[1])
        o_ref[...] = o
        r = lax.rsqrt(_half_mean(o * o, low) + EPS)
        cat_ref[...] = (o * r * g_ref[...]).astype(BF16)

        @pl.when((pl.program_id(0) == nh - 1) & (qi == nq - 1))
        def _():
            for cp in _gather_copies(staged_refs, gathered_refs, *gather_sems):
                cp.wait()

    whole = lambda col0: pl.BlockSpec((s, LANES), lambda h, i: (0, col0 + h))
    res = _pcall(
        body, name="attn_fwd", grid=(nh, nq),
        in_specs=[pl.BlockSpec((t, LANES), lambda h, i: (i, q0 + h)),
                  whole(q0 + nh), whole(q0 + 2 * nh),
                  pl.BlockSpec((1, LANES), lambda h, i: (0, h)),
                  pl.BlockSpec(memory_space=pl.ANY)] + [pl.BlockSpec(memory_space=pl.ANY)] * nw,
        out_specs=[pl.BlockSpec((t, LANES), lambda h, i: (i, h)),
                   pl.BlockSpec((t, LANES), lambda h, i: (i, w_conv // LANES + h))]
        + [pl.BlockSpec(memory_space=pl.ANY)] * nw,
        out_shape=[jax.ShapeDtypeStruct((s, w_attn), F32),
                   jax.ShapeDtypeStruct(cat.shape, BF16)]
        + [jax.ShapeDtypeStruct((N_DEV, *a.shape), BF16) for a in staged],
        scratch_shapes=[pltpu.VMEM((s, LANES), BF16), pltpu.VMEM((s, LANES), BF16),
                        pltpu.VMEM((2 * t, t), BF16),
                        pltpu.VMEM((2, t, LANES), BF16),
                        pltpu.VMEM((2, t, t), F32),
                        pltpu.VMEM((2, t, 2 * t), BF16),
                        pltpu.VMEM((2, t, t), F32),
                        pltpu.VMEM((2, t, t), F32),
                        pltpu.VMEM((2, t, t), BF16),
                        pltpu.VMEM((2, t, LANES), F32),
                        pltpu.VMEM((2, t, LANES), F32),
                        pltpu.VMEM((2, t, LANES), F32)]
        + _exchange_sems(nw, local=True),
        input_output_aliases={4: 1},
        compiler_params=_params(("arbitrary", "arbitrary")),
    )(proj, proj, proj, g_attn, cat, *staged)
    return res[0], res[1], res[2:]


def _attn_bwd(proj, o, dcat, g_attn, w_conv, partials, t=ATTN_BLOCK):
    s, n_cols = proj.shape
    w_attn = g_attn.shape[1]
    nh = w_attn // LANES
    t = _tile(s, t)
    nq = s // t
    q0 = 3 * w_conv // LANES
    scale = HEAD_DIM ** -0.5
    nw = len(partials)

    def body(q_ref, k_ref, v_ref, o_ref, do_ref, g_ref, *rest):
        partial_refs, rest = rest[:nw], rest[nw:]
        dproj_ref, dg_ref = rest[:2]
        received_refs, rest = rest[2:2 + nw], rest[2 + nw:]
        (kb, vb, dkt_acc, dvt_acc, stash, tri_s, qm_s, dom_s, qt_s, dot_s, z_s, da_s, split_s, zl_s,
         keep_s, suf_s, a_s, glog_s, gsplit_s, cum_s, dz_s, run_s, tot_s, rest_s, gtot_s, dq_s) = rest[:26]
        scatter_sems = rest[26:]
        step_i = pl.program_id(1)
        which = pl.program_id(2)
        qi = nq - 1 - step_i
        head_pair = pl.program_id(0)

        @pl.when((head_pair == 0) & (step_i == 0) & (which == 0))
        def _():
            for cp in _scatter_copies(partial_refs, received_refs, *scatter_sems):
                cp.start()

        @pl.when((head_pair == nh - 1) & (step_i == nq - 1) & (which == 2))
        def _():
            for cp in _scatter_copies(partial_refs, received_refs, *scatter_sems):
                cp.wait()

        @pl.when(which == 0)
        def _():
            @pl.when(step_i == 0)
            def _():
                kb[...] = k_ref[...].astype(BF16)
                vb[...] = v_ref[...].astype(BF16)
                tri_s[...] = _suffix_operator(t)
                dkt_acc[...] = jnp.zeros_like(dkt_acc)
                dvt_acc[...] = jnp.zeros_like(dvt_acc)
                dg_ref[...] = jnp.zeros_like(dg_ref)

            low = _low_half()
            q = q_ref[...] * scale
            ov = o_ref[...]
            d_o, dgn = _head_norm_bwd(ov, do_ref[...], g_ref[...], low)
            dg_ref[...] += jnp.sum(dgn, axis=0, keepdims=True)
            for h, msk in enumerate((low, jnp.logical_not(low))):
                qh = jnp.where(msk, q, 0.0)
                doh = jnp.where(msk, d_o, 0.0)
                dom = doh.astype(BF16)
                qm_s[h] = qh.astype(BF16)
                dom_s[h] = dom
                qt_s[h] = qh.T.astype(BF16)
                dot_s[h] = doh.T.astype(BF16)
                rest_s[h] = _row_sum(dom.astype(F32) * ov)
                run_s[h] = jnp.zeros((t, LANES), F32)
                dq_s[h] = jnp.zeros((t, LANES), F32)

            def key_rows(kblk):
                return pl.ds(pl.multiple_of(kblk * t, t), t)

            def scores_matmul(kblk):
                ks = kb[key_rows(kblk), :]
                for h in range(2):
                    z_s[h] = lax.dot_general(qm_s[h], ks, _NT, preferred_element_type=F32)

            def da_matmul(kblk):
                vs = vb[key_rows(kblk), :]
                for h in range(2):
                    da_s[h] = lax.dot_general(dom_s[h], vs, _NT, preferred_element_type=F32)

            def step(kblk, diag):
                ks = kb[key_rows(kblk), :]
                nxt = jnp.maximum(kblk - 1, 0)
                for h in range(2):
                    _sb_scores(z_s.at[h], split_s.at[h], zl_s.at[h], tot_s.at[h], keep_s.at[h], t, diag)
                    suf_s[h] = jnp.dot(split_s[h], tri_s[...], preferred_element_type=F32)
                scores_matmul(nxt)
                for h in range(2):
                    _sb_weights(zl_s.at[h], suf_s.at[h], run_s.at[h], tot_s.at[h], a_s.at[h], t, diag,
                                da_s.at[h], glog_s.at[h], gsplit_s.at[h], gtot_s.at[h])
                    cum_s[h] = jnp.dot(gsplit_s[h], tri_s[...], preferred_element_type=F32)
                da_matmul(nxt)
                dkt = dkt_acc[kblk]
                dvt = dvt_acc[kblk]
                for h in range(2):
                    _sb_dscores(glog_s.at[h], cum_s.at[h], rest_s.at[h], gtot_s.at[h], keep_s.at[h],
                                dz_s.at[h], t, diag)
                    dq_s[h] += jnp.dot(dz_s[h], ks, preferred_element_type=F32)
                    dkt = dkt + jnp.dot(qt_s[h], dz_s[h], preferred_element_type=F32)
                    dvt = dvt + jnp.dot(dot_s[h], a_s[h], preferred_element_type=F32)
                dkt_acc[kblk] = dkt
                dvt_acc[kblk] = dvt

            scores_matmul(qi)
            da_matmul(qi)
            step(qi, True)

            def loop(it, carry):
                step(qi - 1 - it, False)
                return carry

            lax.fori_loop(0, qi, loop, 0)
            stash[0] = (jnp.where(low, dq_s[0], dq_s[1]) * scale).astype(BF16)
            stash[1] = dkt_acc[qi].T.astype(BF16)
            stash[2] = dvt_acc[qi].T.astype(BF16)

        dproj_ref[...] = stash[which]

    whole = lambda col0: pl.BlockSpec((s, LANES), lambda h, i, w: (0, col0 + h))
    blk = lambda col0: pl.BlockSpec((t, LANES), lambda h, i, w: (nq - 1 - i, col0 + h))
    res = _pcall(
        body, name="attn_bwd", grid=(nh, nq, 3),
        in_specs=[blk(q0), whole(q0 + nh), whole(q0 + 2 * nh), blk(0), blk(w_conv // LANES),
                  pl.BlockSpec((1, LANES), lambda h, i, w: (0, h))] + [pl.BlockSpec(memory_space=pl.ANY)] * nw,
        out_specs=[pl.BlockSpec((t, LANES), lambda h, i, w: (nq - 1 - i, q0 + w * nh + h)),
                   pl.BlockSpec((1, LANES), lambda h, i, w: (0, h))] + [pl.BlockSpec(memory_space=pl.ANY)] * nw,
        out_shape=[jax.ShapeDtypeStruct((s, n_cols), BF16), jax.ShapeDtypeStruct((1, w_attn), F32)]
        + [jax.ShapeDtypeStruct((N_PEERS, *a.shape[1:]), BF16) for a in partials],
        scratch_shapes=[pltpu.VMEM((s, LANES), BF16), pltpu.VMEM((s, LANES), BF16),
                        pltpu.VMEM((nq, LANES, t), F32),
                        pltpu.VMEM((nq, LANES, t), F32),
                        pltpu.VMEM((3, t, LANES), BF16),
                        pltpu.VMEM((2 * t, t), BF16),
                        pltpu.VMEM((2, t, LANES), BF16),
                        pltpu.VMEM((2, t, LANES), BF16),
                        pltpu.VMEM((2, LANES, t), BF16),
                        pltpu.VMEM((2, LANES, t), BF16),
                        pltpu.VMEM((2, t, t), F32),
                        pltpu.VMEM((2, t, t), F32),
                        pltpu.VMEM((2, t, 2 * t), BF16),
                        pltpu.VMEM((2, t, t), F32),
                        pltpu.VMEM((2, t, t), F32),
                        pltpu.VMEM((2, t, t), F32),
                        pltpu.VMEM((2, t, t), BF16),
                        pltpu.VMEM((2, t, t), F32),
                        pltpu.VMEM((2, t, 2 * t), BF16),
                        pltpu.VMEM((2, t, t), F32),
                        pltpu.VMEM((2, t, t), BF16),
                        pltpu.VMEM((2, t, LANES), F32),
                        pltpu.VMEM((2, t, LANES), F32),
                        pltpu.VMEM((2, t, LANES), F32),
                        pltpu.VMEM((2, t, LANES), F32),
                        pltpu.VMEM((2, t, LANES), F32)]
        + _exchange_sems(nw),
        compiler_params=_params(("arbitrary", "arbitrary", "arbitrary")),
    )(proj, proj, proj, o, dcat, g_attn, *partials)
    return res[0], res[1], res[2:]


def _place():
    return lax.axis_index("x"), lax.axis_index("y"), lax.axis_index("c")


def _other_chips(x, y):
    return [(1 - x, y), (x, 1 - y), (1 - x, 1 - y)]


def _slot(px, py, pc):
    return 4 * px + 2 * py + pc


def _all_gather(shards, out_dtypes):
    nw = len(shards)

    def body(*refs):
        ins, outs, stage = refs[:nw], refs[nw:2 * nw], refs[2 * nw:3 * nw]
        send_sems, recv_sems, local_sems = refs[3 * nw:]
        x, y, c = _place()
        me, sibling = (x, y, c), (x, y, 1 - c)
        chips = _other_chips(x, y)

        def copy(w, k, block, to, src=None):
            dst = outs[w].at[_slot(*block)]
            return pltpu.make_async_remote_copy(
                src_ref=dst if src is None else src, dst_ref=dst,
                send_sem=send_sems.at[w * 7 + k], recv_sem=recv_sems.at[w * 7 + k],
                device_id=to, device_id_type=MESH)

        started = []
        local = []
        for w in range(nw):
            stage[w][...] = ins[w][...].astype(stage[w].dtype)
            cp = pltpu.make_async_copy(stage[w], outs[w].at[_slot(*me)], local_sems.at[w])
            cp.start()
            local.append(cp)
            started.append(copy(w, 0, me, sibling, src=stage[w]))
            started[-1].start()
            for j, chip in enumerate(chips):
                started.append(copy(w, 1 + j, me, (*chip, c), src=stage[w]))
                started[-1].start()
        for j, chip in enumerate(chips):
            for w in range(nw):
                copy(w, 1 + j, (*chip, c), me).wait_recv()
                started.append(copy(w, 4 + j, (*chip, c), sibling))
                started[-1].start()
        for w in range(nw):
            copy(w, 0, sibling, me).wait_recv()
            for j, chip in enumerate(chips):
                copy(w, 4 + j, (*chip, 1 - c), me).wait_recv()
        for cp in started:
            cp.wait_send()
        for cp in local:
            cp.wait()

    return _pcall(
        body, name="all_gather_weights",
        in_specs=[pl.BlockSpec(memory_space=pltpu.VMEM)] * nw,
        out_specs=[pl.BlockSpec(memory_space=pl.ANY)] * nw,
        out_shape=[jax.ShapeDtypeStruct((N_DEV, *a.shape), d) for a, d in zip(shards, out_dtypes)],
        scratch_shapes=[pltpu.VMEM(a.shape, d) for a, d in zip(shards, out_dtypes)]
        + [pltpu.SemaphoreType.DMA((7 * nw,)), pltpu.SemaphoreType.DMA((7 * nw,)),
           pltpu.SemaphoreType.DMA((nw,))],
        compiler_params=_params(),
    )(*shards)


N_PEERS = N_DEV - 1


def _peer(k):
    x, y, c = _place()
    return (x ^ (k >> 2), y ^ ((k >> 1) & 1), c ^ (k & 1))


def _fanout(src_of, dst_of, nw, send_sems, recv_sems):
    return [pltpu.make_async_remote_copy(
        src_ref=src_of(w, k), dst_ref=dst_of(w, k),
        send_sem=send_sems.at[w * N_PEERS + k - 1], recv_sem=recv_sems.at[w * N_PEERS + k - 1],
        device_id=_peer(k), device_id_type=MESH) for w in range(nw) for k in range(1, N_DEV)]


def _gather_copies(staged, gathered, send_sems, recv_sems, local_sems):
    me = _slot(*_place())
    nw = len(staged)
    remote = _fanout(lambda w, k: staged[w], lambda w, k: gathered[w].at[me], nw, send_sems, recv_sems)
    local = [pltpu.make_async_copy(staged[w], gathered[w].at[me], local_sems.at[w]) for w in range(nw)]
    return remote + local


def _scatter_copies(partials, received, send_sems, recv_sems):
    me = _slot(*_place())
    return _fanout(lambda w, k: partials[w].at[me ^ k], lambda w, k: received[w].at[k - 1],
                   len(partials), send_sems, recv_sems)


def _exchange_sems(nw, local=False):
    sems = [pltpu.SemaphoreType.DMA((N_PEERS * nw,)), pltpu.SemaphoreType.DMA((N_PEERS * nw,))]
    return sems + ([pltpu.SemaphoreType.DMA((nw,))] if local else [])


def _cast_shards(shards):
    def body(*refs):
        for src, dst in zip(refs[:len(shards)], refs[len(shards):]):
            dst[...] = src[...].astype(BF16)

    return _pcall(
        body, name="cast_shards",
        in_specs=[pl.BlockSpec(memory_space=pltpu.VMEM)] * len(shards),
        out_specs=[pl.BlockSpec(memory_space=pltpu.VMEM)] * len(shards),
        out_shape=[jax.ShapeDtypeStruct(a.shape, BF16) for a in shards],
        compiler_params=_params(),
    )(*shards)


def _all_reduce_small(packed):
    r = packed.shape[0]

    def body(x_ref, o_ref, gathered, send_sems, recv_sems):
        x, y, c = _place()
        me = _slot(x, y, c)
        gathered[me] = x_ref[...]
        peers = [(px, py, pc) for px in range(2) for py in range(2) for pc in range(2)]
        started = []
        for k in range(1, N_DEV):
            to = (x ^ (k >> 2), y ^ ((k >> 1) & 1), c ^ (k & 1))
            cp = pltpu.make_async_remote_copy(
                src_ref=x_ref, dst_ref=gathered.at[me],
                send_sem=send_sems.at[k - 1], recv_sem=recv_sems.at[k - 1],
                device_id=to, device_id_type=MESH)
            cp.start()
            started.append(cp)
        del peers
        for cp in started:
            cp.wait()
        total = gathered[0]
        for k in range(1, N_DEV):
            total = total + gathered[k]
        o_ref[...] = total

    return _pcall(
        body, name="all_reduce_small",
        in_specs=[pl.BlockSpec(memory_space=pltpu.VMEM)],
        out_specs=pl.BlockSpec(memory_space=pltpu.VMEM),
        out_shape=jax.ShapeDtypeStruct(packed.shape, F32),
        scratch_shapes=[pltpu.VMEM((N_DEV, r, LANES), F32),
                        pltpu.SemaphoreType.DMA((N_DEV - 1,)), pltpu.SemaphoreType.DMA((N_DEV - 1,))],
        compiler_params=_params(),
    )(packed)


def _adam_math(w, g, m, v):
    m = ADAM_B1 * m + (1.0 - ADAM_B1) * g
    v = ADAM_B2 * v + (1.0 - ADAM_B2) * jnp.square(g)
    m_hat = m / (1.0 - ADAM_B1 ** ADAM_STEP)
    v_hat = v / (1.0 - ADAM_B2 ** ADAM_STEP)
    delta = -ADAM_LR * (m_hat / (jnp.sqrt(v_hat) + ADAM_EPS) + ADAM_WD * w)
    return delta, m, v


def _adam_sharded(name, own, received, w, m, v, place, tr=256):
    r, cdim = w.shape
    tr = _tile(r, tr) if r % LANES == 0 else r

    def body(place_ref, own_ref, rec_ref, w_ref, m_ref, v_ref, g_ref, d_ref, nm_ref, nv_ref):
        del place_ref
        g = own_ref[...]
        for j in range(N_PEERS):
            g = g + rec_ref[j].astype(F32)
        delta, nm, nv = _adam_math(w_ref[...], g, m_ref[...], v_ref[...])
        g_ref[...] = g
        d_ref[...] = delta
        nm_ref[...] = nm
        nv_ref[...] = nv

    blk = pl.BlockSpec((tr, cdim), lambda i, pr: (i, 0))
    grid_spec = pltpu.PrefetchScalarGridSpec(
        num_scalar_prefetch=1, grid=(r // tr,),
        in_specs=[pl.BlockSpec((None, tr, cdim), lambda i, pr: (4 * pr[0] + 2 * pr[1] + pr[2], i, 0)),
                  pl.BlockSpec((N_PEERS, tr, cdim), lambda i, pr: (0, i, 0)), blk, blk, blk],
        out_specs=[blk] * 4)
    return _pcall(body, name=name, grid_spec=grid_spec,
                  out_shape=[jax.ShapeDtypeStruct((r, cdim), F32)] * 4,
                  compiler_params=_params(("parallel",)))(place, own, received, w, m, v)


def _adam_small(w, g, m, v):
    def body(w_ref, g_ref, m_ref, v_ref, d_ref, nm_ref, nv_ref):
        delta, nm, nv = _adam_math(w_ref[...], g_ref[...], m_ref[...], v_ref[...])
        d_ref[...] = delta
        nm_ref[...] = nm
        nv_ref[...] = nv

    return _pcall(body, name="adam_small",
                  in_specs=[pl.BlockSpec(memory_space=pltpu.VMEM)] * 4,
                  out_specs=[pl.BlockSpec(memory_space=pltpu.VMEM)] * 3,
                  out_shape=[jax.ShapeDtypeStruct(w.shape, F32)] * 3,
                  compiler_params=_params())(w, g, m, v)


def _rows(vec):
    return vec.reshape(-1, LANES)


def kernel(x, p, g_mix, w_in, conv_w, g_conv_out, g_attn_out, w_out, g_mlp, w_up, w_down, g_ple, w_ple_gate, w_ple_proj, g_final, loss_target, m_g_mix, m_w_in, m_conv_w, m_g_conv_out, m_g_attn_out, m_w_out, m_g_mlp, m_w_up, m_w_down, m_g_ple, m_w_ple_gate, m_w_ple_proj, m_g_final, v_g_mix, v_w_in, v_conv_w, v_g_conv_out, v_g_attn_out, v_w_out, v_g_mlp, v_w_up, v_w_down, v_g_ple, v_w_ple_gate, v_w_ple_proj, v_g_final):
    s, d = x.shape[1], x.shape[2]
    w_conv = g_conv_out.shape[1]
    w_attn = g_attn_out.shape[1]
    cw = conv_w.shape[2]
    xs, ps, tgt = x[0], p[0, 0], loss_target[0]
    place = jnp.stack([lax.axis_index("x"), lax.axis_index("y"), lax.axis_index("c")]).astype(jnp.int32)
    my_slot = 4 * place[0] + 2 * place[1] + place[2]

    conv_tile = jnp.pad(conv_w[0], ((0, HALO - CONV_K), (0, LANES - cw)))
    big = [w_in[0], w_out[0], w_up[0], w_down[0], w_ple_gate[0], w_ple_proj[0]]
    win_g, conv_g = _all_gather([big[0], conv_tile], [BF16, F32])
    staged = _cast_shards(big[1:])
    conv_full = jnp.transpose(conv_g[:, :CONV_K, :cw], (1, 0, 2)).reshape(CONV_K, w_conv)
    in_shard, up_shard, proj_shard = big[0].shape[1], big[2].shape[1], big[5].shape[1]

    a = _rmsnorm_fwd("norm_mix", xs, g_mix)
    proj, = _mm_nn("in_proj", a, win_g, n_shard=in_shard, tn=in_shard)
    cat = _conv_fwd(proj, conv_full, g_conv_out, w_conv, d)
    o, cat, (wout_g, wup_g, wdown_g, wgate_g, wproj_g) = _attn_fwd(proj, g_attn_out, cat, w_conv, staged)
    wout_f = wout_g.reshape(-1, wout_g.shape[-1])
    wdown_f = wdown_g.reshape(-1, wdown_g.shape[-1])
    wgate_f = wgate_g.reshape(-1, wgate_g.shape[-1])
    h1, = _mm_nn("out_proj", cat, wout_f, epilogue=_ep_residual, extras=(xs,))
    mn = _rmsnorm_fwd("norm_mlp", h1, g_mlp)
    act, = _mm_nn("mlp_up", mn, wup_g, n_shard=up_shard, epilogue=_ep_up, out_dtypes=(BF16,))
    h2, = _mm_nn("mlp_down", act, wdown_f, epilogue=_ep_residual, extras=(h1,))
    n3 = _rmsnorm_fwd("norm_ple", h2, g_ple)
    gl, = _mm_nn("ple_gate", n3, wgate_f)
    pp = _ple_proj(ps, wproj_g)
    loss_part, dh3, dgl, dpp, dg_final = _ple_loss(h2, gl, pp, tgt, g_final.reshape(1, d))
    loss = lax.psum(loss_part[0, 0], ("x", "y", "c"))

    def slots(t2d):
        return t2d.reshape(N_DEV, -1, t2d.shape[-1])

    dw_proj = _d_ple_proj(ps, dpp, proj_shard)
    dw_gate = [slots(t) for t in _mm_tn("d_w_ple_gate", n3, dgl)]
    dn3, = _mm_nt("d_norm_ple", dgl, wgate_f)
    dh2, dh2b, dg_ple = _rmsnorm_bwd("norm_ple_bwd", dn3, h2, g_ple, dh3)
    du, = _mm_nt("d_mlp_act", dh2b, wdown_f, epilogue=_ep_dact, out_dtypes=(BF16,), extras=(act,))
    dw_down = [slots(t) for t in _mm_tn("d_w_down", act, dh2b)]
    dw_up = _mm_tn("d_w_up", mn, du, n_shard=up_shard)
    dmn, = _mm_nt("d_norm_mlp", du, wup_g, k_shard=up_shard)
    dh1, dh1b, dg_mlp = _rmsnorm_bwd("norm_mlp_bwd", dmn, h1, g_mlp, dh2)
    dcat, = _mm_nt("d_cat", dh1b, wout_f)
    dw_out = [slots(t) for t in _mm_tn("d_w_out", cat, dh1b)]
    late = [dw_out, dw_up, dw_down, dw_gate, dw_proj]
    dproj, dg_attn, late_recv = _attn_bwd(proj, o, dcat, g_attn_out, w_conv, [pb for _, pb in late])
    dproj, dconv, dg_conv = _conv_bwd(proj, dcat, conv_full, g_conv_out, dproj, w_conv)
    dw_in = _mm_tn("d_w_in", a, dproj, n_shard=in_shard, tn=in_shard)
    da, in_recv = _mm_nt("d_norm_mix", dproj, win_g, k_shard=in_shard, tk=in_shard, exchange=(dw_in[1],))
    grad_x, _, dg_mix = _rmsnorm_bwd("norm_mix_bwd", da, xs, g_mix, dh1)

    names = ["w_in", "w_out", "w_up", "w_down", "w_ple_gate", "w_ple_proj"]
    owns = [dw_in[0]] + [pf for pf, _ in late]
    recvs = [in_recv, *late_recv]
    moments = [(m_w_in, v_w_in), (m_w_out, v_w_out), (m_w_up, v_w_up), (m_w_down, v_w_down),
               (m_w_ple_gate, v_w_ple_gate), (m_w_ple_proj, v_w_ple_proj)]
    big_out = {}
    for n, own, rc, wt, (mm, vv) in zip(names, owns, recvs, big, moments):
        big_out[n] = [t[None] for t in _adam_sharded("adam_" + n, own, rc, wt, mm[0], vv[0], place)]

    small_g = jnp.concatenate(
        [_rows(dg_mix[0]), _rows(dg_conv[0]), _rows(dg_attn[0]), _rows(dg_mlp[0]), _rows(dg_ple[0]),
         _rows(dg_final[0]), _rows(dconv.reshape(-1))], axis=0)
    n_gain_rows = small_g.shape[0] - CONV_K * w_conv // LANES
    pad_rows = (-small_g.shape[0]) % HALO
    small_g = _all_reduce_small(jnp.pad(small_g, ((0, pad_rows), (0, 0))))
    dconv_full = small_g[n_gain_rows:n_gain_rows + CONV_K * w_conv // LANES].reshape(CONV_K, w_conv)
    dconv_mine = lax.dynamic_slice(dconv_full, (0, my_slot * cw), (CONV_K, cw))

    def pack(vecs, conv_part):
        rows = [_rows(t.reshape(-1)) for t in vecs]
        rows.append(jnp.pad(conv_part, ((0, HALO - CONV_K), (0, LANES - cw))))
        return jnp.concatenate(rows, axis=0)

    gains = [g_mix, g_conv_out, g_attn_out, g_mlp, g_ple, g_final]
    gains_m = [m_g_mix, m_g_conv_out, m_g_attn_out, m_g_mlp, m_g_ple, m_g_final]
    gains_v = [v_g_mix, v_g_conv_out, v_g_attn_out, v_g_mlp, v_g_ple, v_g_final]
    gpack = jnp.concatenate([small_g[:n_gain_rows], jnp.pad(dconv_mine, ((0, HALO - CONV_K), (0, LANES - cw)))], axis=0)
    sd, sm, sv = _adam_small(pack(gains, conv_w[0]), gpack, pack(gains_m, m_conv_w[0]), pack(gains_v, v_conv_w[0]))

    def unpack(packed):
        out, r0 = [], 0
        for t in gains:
            nr = t.size // LANES
            out.append(packed[r0:r0 + nr].reshape(t.shape))
            r0 += nr
        out.append(packed[r0:r0 + CONV_K, :cw][None])
        return out

    sg_l, sd_l, sm_l, sv_l = unpack(gpack), unpack(sd), unpack(sm), unpack(sv)
    small_names = ["g_mix", "g_conv_out", "g_attn_out", "g_mlp", "g_ple", "g_final", "conv_w"]
    small_out = {n: [sg_l[i], sd_l[i], sm_l[i], sv_l[i]] for i, n in enumerate(small_names)}

    order = ["g_mix", "w_in", "conv_w", "g_conv_out", "g_attn_out", "w_out", "g_mlp", "w_up", "w_down",
             "g_ple", "w_ple_gate", "w_ple_proj", "g_final"]
    table = {**big_out, **small_out}
    outs = [loss, grad_x[None]]
    for kind in range(4):
        outs.extend(table[n][kind] for n in order)
    return tuple(outs)
```

```python
import functools

import jax
import jax.numpy as jnp
from jax import lax
from jax.experimental import pallas as pl
from jax.experimental.pallas import tpu as pltpu

F32 = jnp.float32
BF16 = jnp.bfloat16
EPS = 1e-6
HEAD_DIM = 64
LANES = 128
CONV_K = 3
ATTN_BLOCK = 256
HALO = 8
N_DEV = 8
MESH = pl.DeviceIdType.MESH
VMEM_LIMIT = 56 * 1024 * 1024

ADAM_LR = 0.001
ADAM_B1 = 0.9
ADAM_B2 = 0.999
ADAM_EPS = 1e-08
ADAM_WD = 0.01
ADAM_STEP = 10


def _pcall(body, **kw):
    return pl.pallas_call(body, **kw)


def _params(sem=None, **kw):
    return pltpu.CompilerParams(dimension_semantics=sem, vmem_limit_bytes=VMEM_LIMIT, **kw)


def _tile(dim, pref):
    t = min(dim, pref)
    while dim % t:
        t -= LANES
    assert t > 0, (dim, pref)
    return t


_NN = (((1,), (0,)), ((), ()))
_NT = (((1,), (1,)), ((), ()))
_TN = (((0,), (0,)), ((), ()))


def _ep_store(acc, outs):
    outs[0][...] = acc.astype(outs[0].dtype)


def _ep_both(acc, outs):
    outs[0][...] = acc
    outs[1][...] = acc.astype(BF16)


def _ep_residual(acc, res, outs):
    outs[0][...] = acc + res[...]


def _ep_up(acc, outs):
    outs[0][...] = jnp.square(jnp.maximum(acc, 0.0)).astype(BF16)


def _ep_dact(acc, act, outs):
    outs[0][...] = (acc * (2.0 * jnp.sqrt(act[...].astype(F32)))).astype(BF16)


def _matmul(name, a, b, *, dims, grid, a_spec, b_spec, acc_shape, out_shapes, out_specs,
            epilogue=_ep_store, extras=(), extra_specs=(), exchange=()):
    nk = grid[2]
    n_ex, n_out, n_xc = len(extras), len(out_shapes), len(exchange)
    last = tuple(g - 1 for g in grid)

    def product(a_ref, b_ref):
        return lax.dot_general(a_ref[...].astype(BF16), b_ref[...].astype(BF16), dims,
                               preferred_element_type=F32)

    def body(a_ref, b_ref, *rest):
        ex, rest = rest[:n_ex], rest[n_ex:]
        partials, rest = rest[:n_xc], rest[n_xc:]
        outs, rest = rest[:n_out], rest[n_out:]
        received, rest = rest[:n_xc], rest[n_xc:]
        ids = [pl.program_id(axis) for axis in range(3)]
        if n_xc:
            @pl.when((ids[0] == 0) & (ids[1] == 0) & (ids[2] == 0))
            def _():
                for cp in _scatter_copies(partials, received, *rest[-2:]):
                    cp.start()

        if nk == 1:
            epilogue(product(a_ref, b_ref), *ex, outs)
        else:
            acc = rest[0]

            @pl.when(ids[2] == 0)
            def _():
                acc[...] = product(a_ref, b_ref)

            @pl.when(ids[2] > 0)
            def _():
                acc[...] += product(a_ref, b_ref)

            @pl.when(ids[2] == nk - 1)
            def _():
                epilogue(acc[...], *ex, outs)

        if n_xc:
            @pl.when((ids[0] == last[0]) & (ids[1] == last[1]) & (ids[2] == last[2]))
            def _():
                for cp in _scatter_copies(partials, received, *rest[-2:]):
                    cp.wait()

    anywhere = [pl.BlockSpec(memory_space=pl.ANY)] * n_xc
    return _pcall(
        body, name=name, grid=grid,
        in_specs=[a_spec, b_spec, *extra_specs, *anywhere],
        out_specs=[*out_specs, *anywhere],
        out_shape=[*out_shapes, *(jax.ShapeDtypeStruct((N_PEERS, *p.shape[1:]), BF16) for p in exchange)],
        scratch_shapes=([] if nk == 1 else [pltpu.VMEM(acc_shape, F32)]) + (_exchange_sems(n_xc) if n_xc else []),
        compiler_params=_params(("arbitrary",) * 3 if n_xc else ("parallel", "parallel", "arbitrary")),
    )(a, b, *extras, *exchange)


def _mm_nn(name, a, w, *, n_shard=None, epilogue=_ep_store, out_dtypes=(F32,), extras=(), tm=1024, tn=1024, tk=1024):
    m, kd = a.shape
    if n_shard is None:
        n = w.shape[1]
        tn = _tile(n, tn)
        tk = _tile(kd, tk)
        b_spec = pl.BlockSpec((tk, tn), lambda i, j, k: (k, j))
    else:
        n = N_DEV * n_shard
        tn = _tile(n_shard, tn)
        tk = _tile(kd, tk)
        per = n_shard // tn
        b_spec = pl.BlockSpec((None, tk, tn), lambda i, j, k: (j // per, k, j % per))
    tm = _tile(m, tm)
    o_spec = pl.BlockSpec((tm, tn), lambda i, j, k: (i, j))
    return _matmul(
        name, a, w, dims=_NN, grid=(m // tm, n // tn, kd // tk),
        a_spec=pl.BlockSpec((tm, tk), lambda i, j, k: (i, k)), b_spec=b_spec,
        acc_shape=(tm, tn),
        out_shapes=[jax.ShapeDtypeStruct((m, n), d) for d in out_dtypes],
        out_specs=[o_spec] * len(out_dtypes),
        epilogue=epilogue, extras=extras, extra_specs=[o_spec] * len(extras))


def _mm_nt(name, a, w, *, k_shard=None, epilogue=_ep_store, out_dtypes=(F32,), extras=(), exchange=(),
           tm=1024, tn=1024, tk=1024):
    m, kd = a.shape
    if k_shard is None:
        n = w.shape[0]
        tn = _tile(n, tn)
        tk = _tile(kd, tk)
        b_spec = pl.BlockSpec((tn, tk), lambda i, j, k: (j, k))
    else:
        n = w.shape[1]
        tn = _tile(n, tn)
        tk = _tile(k_shard, tk)
        per = k_shard // tk
        b_spec = pl.BlockSpec((None, tn, tk), lambda i, j, k: (k // per, j, k % per))
    tm = _tile(m, tm)
    o_spec = pl.BlockSpec((tm, tn), lambda i, j, k: (i, j))
    return _matmul(
        name, a, w, dims=_NT, grid=(m // tm, n // tn, kd // tk),
        a_spec=pl.BlockSpec((tm, tk), lambda i, j, k: (i, k)), b_spec=b_spec,
        acc_shape=(tm, tn),
        out_shapes=[jax.ShapeDtypeStruct((m, n), d) for d in out_dtypes],
        out_specs=[o_spec] * len(out_dtypes),
        epilogue=epilogue, extras=extras, extra_specs=[o_spec] * len(extras), exchange=exchange)


def _mm_tn(name, a, b, *, n_shard=None, tm=1024, tn=1024, tk=1024):
    t, m = a.shape
    n = b.shape[1]
    tm = _tile(m, tm)
    tk = _tile(t, tk)
    if n_shard is None:
        tn = _tile(n, tn)
        o_spec = pl.BlockSpec((tm, tn), lambda i, j, k: (i, j))
        shape = (m, n)
    else:
        tn = _tile(n_shard, tn)
        per = n_shard // tn
        o_spec = pl.BlockSpec((None, tm, tn), lambda i, j, k: (j // per, i, j % per))
        shape = (N_DEV, m, n_shard)
    return _matmul(
        name, a, b, dims=_TN, grid=(m // tm, n // tn, t // tk),
        a_spec=pl.BlockSpec((tk, tm), lambda i, j, k: (k, i)),
        b_spec=pl.BlockSpec((tk, tn), lambda i, j, k: (k, j)),
        acc_shape=(tm, tn), epilogue=_ep_both,
        out_shapes=[jax.ShapeDtypeStruct(shape, F32), jax.ShapeDtypeStruct(shape, BF16)],
        out_specs=[o_spec, o_spec])


def _ple_proj(p, w_g, tm=1024):
    s, kd = p.shape
    ns = w_g.shape[2]
    tm = _tile(s, tm)

    def body(p_ref, w_ref, o_ref):
        pv = p_ref[...].astype(BF16)
        for j in range(N_DEV):
            o_ref[:, j * ns:(j + 1) * ns] = jnp.dot(pv, w_ref[j], preferred_element_type=F32)

    return _pcall(body, name="ple_proj", grid=(s // tm,),
                  in_specs=[pl.BlockSpec((tm, kd), lambda i: (i, 0)),
                            pl.BlockSpec((N_DEV, kd, ns), lambda i: (0, 0, 0))],
                  out_specs=pl.BlockSpec((tm, N_DEV * ns), lambda i: (i, 0)),
                  out_shape=jax.ShapeDtypeStruct((s, N_DEV * ns), F32),
                  compiler_params=_params(("parallel",)))(p, w_g)


def _d_ple_proj(p, dpp, ns, tk=1024):
    s, kd = p.shape
    tk = _tile(s, tk)
    nk = s // tk

    def body(p_ref, d_ref, of_ref, ob_ref, acc):
        k = pl.program_id(0)

        @pl.when(k == 0)
        def _():
            acc[...] = jnp.zeros_like(acc)

        pv = p_ref[...].astype(BF16)
        for j in range(N_DEV):
            acc[j] += lax.dot_general(pv, d_ref[:, j * ns:(j + 1) * ns], _TN, preferred_element_type=F32)

        @pl.when(k == nk - 1)
        def _():
            of_ref[...] = acc[...]
            ob_ref[...] = acc[...].astype(BF16)

    whole = pl.BlockSpec((N_DEV, kd, ns), lambda k: (0, 0, 0))
    return _pcall(body, name="d_w_ple_proj", grid=(nk,),
                  in_specs=[pl.BlockSpec((tk, kd), lambda k: (k, 0)),
                            pl.BlockSpec((tk, N_DEV * ns), lambda k: (k, 0))],
                  out_specs=[whole, whole],
                  out_shape=[jax.ShapeDtypeStruct((N_DEV, kd, ns), F32), jax.ShapeDtypeStruct((N_DEV, kd, ns), BF16)],
                  scratch_shapes=[pltpu.VMEM((N_DEV, kd, ns), F32)],
                  compiler_params=_params(("arbitrary",)))(p, dpp)


def _row_spec(tr, d):
    return pl.BlockSpec((tr, d), lambda i: (i, 0))


def _vec_spec(d):
    return pl.BlockSpec((1, d), lambda i: (0, 0))


def _rmsnorm_fwd(name, x, g, tr=512):
    s, d = x.shape
    tr = _tile(s, tr)

    def body(x_ref, g_ref, o_ref):
        xv = x_ref[...]
        r = lax.rsqrt(jnp.mean(xv * xv, axis=-1, keepdims=True) + EPS)
        o_ref[...] = (xv * r * g_ref[...]).astype(BF16)

    return _pcall(body, name=name, grid=(s // tr,),
                  in_specs=[_row_spec(tr, d), _vec_spec(d)], out_specs=_row_spec(tr, d),
                  out_shape=jax.ShapeDtypeStruct((s, d), BF16),
                  compiler_params=_params(("parallel",)))(x, g)


def _rmsnorm_bwd(name, dn, h, g, dres, tr=512):
    s, d = h.shape
    tr = _tile(s, tr)

    def body(dn_ref, h_ref, g_ref, dres_ref, dh_ref, dhb_ref, dg_ref):
        @pl.when(pl.program_id(0) == 0)
        def _():
            dg_ref[...] = jnp.zeros_like(dg_ref)

        hv = h_ref[...]
        dnv = dn_ref[...]
        r = lax.rsqrt(jnp.mean(hv * hv, axis=-1, keepdims=True) + EPS)
        hn = hv * r
        dg_ref[...] += jnp.sum(dnv * hn, axis=0, keepdims=True)
        dhn = dnv * g_ref[...]
        dh = dres_ref[...] + r * (dhn - hn * jnp.mean(dhn * hn, axis=-1, keepdims=True))
        dh_ref[...] = dh
        dhb_ref[...] = dh.astype(BF16)

    return _pcall(body, name=name, grid=(s // tr,),
                  in_specs=[_row_spec(tr, d), _row_spec(tr, d), _vec_spec(d), _row_spec(tr, d)],
                  out_specs=[_row_spec(tr, d), _row_spec(tr, d), _vec_spec(d)],
                  out_shape=[jax.ShapeDtypeStruct((s, d), F32), jax.ShapeDtypeStruct((s, d), BF16),
                             jax.ShapeDtypeStruct((1, d), F32)],
                  compiler_params=_params(("arbitrary",)))(dn, h, g, dres)


def _ple_loss(h2, gl, pp, tgt, g_final, tr=512):
    s, d = h2.shape
    tr = _tile(s, tr)

    def body(h2_ref, gl_ref, pp_ref, t_ref, g_ref, loss_ref, dh3_ref, dgl_ref, dpp_ref, dg_ref):
        @pl.when(pl.program_id(0) == 0)
        def _():
            dg_ref[...] = jnp.zeros_like(dg_ref)
            loss_ref[...] = jnp.zeros_like(loss_ref)

        gate = jax.nn.sigmoid(gl_ref[...])
        ppv = pp_ref[...]
        h3 = h2_ref[...] + gate * ppv
        r = lax.rsqrt(jnp.mean(h3 * h3, axis=-1, keepdims=True) + EPS)
        hn = h3 * r
        gv = g_ref[...]
        diff = hn * gv - t_ref[...]
        row = jnp.mean(diff * diff, axis=-1, keepdims=True)
        loss_ref[...] += 0.5 * jnp.sum(row, axis=0, keepdims=True)
        dy = diff * (1.0 / d)
        dg_ref[...] += jnp.sum(dy * hn, axis=0, keepdims=True)
        dhn = dy * gv
        dh3 = r * (dhn - hn * jnp.mean(dhn * hn, axis=-1, keepdims=True))
        dh3_ref[...] = dh3
        dgl_ref[...] = (dh3 * ppv * gate * (1.0 - gate)).astype(BF16)
        dpp_ref[...] = (dh3 * gate).astype(BF16)

    return _pcall(body, name="ple_loss", grid=(s // tr,),
                  in_specs=[_row_spec(tr, d)] * 4 + [_vec_spec(d)],
                  out_specs=[_vec_spec(LANES), _row_spec(tr, d), _row_spec(tr, d), _row_spec(tr, d), _vec_spec(d)],
                  out_shape=[jax.ShapeDtypeStruct((1, LANES), F32), jax.ShapeDtypeStruct((s, d), F32),
                             jax.ShapeDtypeStruct((s, d), BF16), jax.ShapeDtypeStruct((s, d), BF16),
                             jax.ShapeDtypeStruct((1, d), F32)],
                  compiler_params=_params(("arbitrary",)))(h2, gl, pp, tgt, g_final)


def _low_half():
    return lax.broadcasted_iota(jnp.int32, (1, LANES), 1) < HEAD_DIM


def _half_mean(v, low):
    s_lo = jnp.sum(jnp.where(low, v, 0.0), axis=-1, keepdims=True)
    s_hi = jnp.sum(jnp.where(low, 0.0, v), axis=-1, keepdims=True)
    return jnp.where(low, s_lo, s_hi) * (1.0 / HEAD_DIM)


def _head_norm_bwd(val, dout, g, low):
    r = lax.rsqrt(_half_mean(val * val, low) + EPS)
    vn = val * r
    dvn = dout * g
    return r * (dvn - vn * _half_mean(dvn * vn, low)), dout * vn


def _conv_taps(vv_ext, w_ref, rows):
    v0 = vv_ext[HALO:]
    v1 = pltpu.roll(vv_ext, 1, 0)[HALO:]
    v2 = pltpu.roll(vv_ext, 2, 0)[HALO:]
    del rows
    return w_ref[2:3, :] * v0 + w_ref[1:2, :] * v1 + w_ref[0:1, :] * v2, (v0, v1, v2)


def _conv_fwd(proj, conv_w, g_conv, w_conv, d_model, tr=512):
    s = proj.shape[0]
    tr = _tile(s, tr)
    hb = tr // HALO

    def main(part):
        return pl.BlockSpec((tr, w_conv), lambda i: (i, part))

    def prev(part):
        return pl.BlockSpec((HALO, w_conv), lambda i: (jnp.maximum(i * hb - 1, 0), part))

    def body(cb_ref, cc_ref, cu_ref, ccp_ref, cup_ref, w_ref, g_ref, o_ref):
        i = pl.program_id(0)
        low = _low_half()
        for j in range(w_conv // LANES):
            cols = slice(j * LANES, (j + 1) * LANES)
            vv_prev = jnp.where(i > 0, ccp_ref[:, cols] * cup_ref[:, cols], 0.0)
            vv_ext = jnp.concatenate([vv_prev, cc_ref[:, cols] * cu_ref[:, cols]], axis=0)
            y, _ = _conv_taps(vv_ext, w_ref.at[:, cols], tr)
            co = cb_ref[:, cols] * y
            r = lax.rsqrt(_half_mean(co * co, low) + EPS)
            o_ref[:, cols] = (co * r * g_ref[:, cols]).astype(BF16)

    return _pcall(
        body, name="conv_fwd", grid=(s // tr,),
        in_specs=[main(0), main(1), main(2), prev(1), prev(2),
                  pl.BlockSpec((CONV_K, w_conv), lambda i: (0, 0)),
                  pl.BlockSpec((1, w_conv), lambda i: (0, 0))],
        out_specs=pl.BlockSpec((tr, w_conv), lambda i: (i, 0)),
        out_shape=jax.ShapeDtypeStruct((s, d_model), BF16),
        compiler_params=_params(("parallel",)),
    )(proj, proj, proj, proj, proj, conv_w, g_conv)


def _conv_bwd(proj, dcat, conv_w, g_conv, dproj, w_conv, tr=512):
    s = proj.shape[0]
    tr = _tile(s, tr)
    hb = tr // HALO
    last = s // HALO - 1
    nt = s // tr

    def main(part):
        return pl.BlockSpec((tr, w_conv), lambda i: (i, part))

    def prev(part):
        return pl.BlockSpec((HALO, w_conv), lambda i: (jnp.maximum(i * hb - 1, 0), part))

    def nxt(part):
        return pl.BlockSpec((HALO, w_conv), lambda i: (jnp.minimum((i + 1) * hb, last), part))

    def body(cb_ref, cc_ref, cu_ref, dc_ref, ccp_ref, cup_ref, cbn_ref, ccn_ref, cun_ref, dcn_ref,
             w_ref, g_ref, dproj_in, dproj_ref, dw_ref, dg_ref):
        del dproj_in
        i = pl.program_id(0)

        @pl.when(i == 0)
        def _():
            dw_ref[...] = jnp.zeros_like(dw_ref)
            dg_ref[...] = jnp.zeros_like(dg_ref)

        low = _low_half()
        n_ext = tr + HALO
        rowid = lax.broadcasted_iota(jnp.int32, (n_ext, 1), 0)
        for j in range(w_conv // LANES):
            cols = slice(j * LANES, (j + 1) * LANES)
            wj = w_ref.at[:, cols]
            cc, cu = cc_ref[:, cols], cu_ref[:, cols]
            vv_prev = jnp.where(i > 0, ccp_ref[:, cols] * cup_ref[:, cols], 0.0)
            vv_ext = jnp.concatenate([vv_prev, cc * cu, ccn_ref[:, cols] * cun_ref[:, cols]], axis=0)
            y_ext, (v0, v1, v2) = _conv_taps(vv_ext, wj, n_ext)
            cb_ext = jnp.concatenate([cb_ref[:, cols], cbn_ref[:, cols]], axis=0)
            dc_ext = jnp.concatenate([dc_ref[:, cols], dcn_ref[:, cols]], axis=0)
            dco, dgn = _head_norm_bwd(cb_ext * y_ext, dc_ext, g_ref[:, cols], low)
            dyc = jnp.where((rowid < tr) | (i < nt - 1), dco * cb_ext, 0.0)
            dvv = (wj[2:3, :] * dyc[:tr] + wj[1:2, :] * pltpu.roll(dyc, n_ext - 1, 0)[:tr]
                   + wj[0:1, :] * pltpu.roll(dyc, n_ext - 2, 0)[:tr])
            dproj_ref[:, cols] = (dco[:tr] * y_ext[:tr]).astype(BF16)
            dproj_ref[:, w_conv + j * LANES:w_conv + (j + 1) * LANES] = (dvv * cu).astype(BF16)
            dproj_ref[:, 2 * w_conv + j * LANES:2 * w_conv + (j + 1) * LANES] = (dvv * cc).astype(BF16)
            dyt = dyc[:tr]
            for tap, shifted in enumerate((v2, v1, v0)):
                dw_ref[tap:tap + 1, cols] += jnp.sum(dyt * shifted[:tr], axis=0, keepdims=True)
            dg_ref[:, cols] += jnp.sum(dgn[:tr], axis=0, keepdims=True)

    n_cols = dproj.shape[1]
    return _pcall(
        body, name="conv_bwd", grid=(nt,),
        in_specs=[main(0), main(1), main(2), main(0),
                  prev(1), prev(2), nxt(0), nxt(1), nxt(2), nxt(0),
                  pl.BlockSpec((CONV_K, w_conv), lambda i: (0, 0)),
                  pl.BlockSpec((1, w_conv), lambda i: (0, 0)),
                  pl.BlockSpec(memory_space=pl.ANY)],
        out_specs=[pl.BlockSpec((tr, 3 * w_conv), lambda i: (i, 0)),
                   pl.BlockSpec((CONV_K, w_conv), lambda i: (0, 0)),
                   pl.BlockSpec((1, w_conv), lambda i: (0, 0))],
        out_shape=[jax.ShapeDtypeStruct((s, n_cols), BF16),
                   jax.ShapeDtypeStruct((CONV_K, w_conv), F32),
                   jax.ShapeDtypeStruct((1, w_conv), F32)],
        input_output_aliases={12: 0},
        compiler_params=_params(("arbitrary",)),
    )(proj, proj, proj, dcat, proj, proj, proj, proj, proj, dcat, conv_w, g_conv, dproj)


STRIP = 16


def _suffix_operator(t):
    r = lax.broadcasted_iota(jnp.int32, (2 * t, t), 0)
    c = lax.broadcasted_iota(jnp.int32, (2 * t, t), 1)
    return jnp.where((r > c) & ((r < t) | (r - t > c)), 1.0, 0.0).astype(BF16)


def _strips(t):
    return [(i, slice(i * STRIP, (i + 1) * STRIP)) for i in range(t // STRIP)]


def _strip_mask(i, t):
    r = lax.broadcasted_iota(jnp.int32, (STRIP, t), 0) + i * STRIP
    c = lax.broadcasted_iota(jnp.int32, (STRIP, t), 1)
    return r > c


def _store_split(ref, rows, val, t):
    hi = val.astype(BF16)
    ref[rows, 0:t] = hi
    ref[rows, t:2 * t] = (val - hi.astype(F32)).astype(BF16)


def _sb_scores(z_s, split_s, zl_s, tot_s, keep_s, t, diag):
    for i, rows in _strips(t):
        z = z_s[rows, :]
        e = jnp.exp(-jnp.abs(z))
        den = 1.0 + e
        log_keep = jnp.minimum(-z, 0.0) - jnp.log(den)
        if diag:
            log_keep = jnp.where(_strip_mask(i, t), log_keep, 0.0)
        _store_split(split_s, rows, log_keep, t)
        zl_s[rows, :] = z + log_keep
        tot_s[rows, :] = _row_sum(log_keep)
        if keep_s is not None:
            keep_s[rows, :] = jnp.where(z >= 0.0, e, 1.0) / den


def _row_sum(v):
    return jnp.broadcast_to(jnp.sum(v, axis=-1, keepdims=True), (v.shape[0], LANES))


def _wide(r, t):
    return jnp.concatenate([r] * (t // LANES), axis=1)


def _sb_weights(zl_s, suf_s, run_s, tot_s, a_s, t, diag, da_s=None, glog_s=None, gsplit_s=None, gtot_s=None):
    for i, rows in _strips(t):
        run = run_s[rows, :]
        a = jnp.exp(zl_s[rows, :] + suf_s[rows, :] + _wide(run, t))
        if diag:
            a = jnp.where(_strip_mask(i, t), a, 0.0)
        ab = a.astype(BF16)
        a_s[rows, :] = ab
        run_s[rows, :] = run + tot_s[rows, :]
        if da_s is not None:
            glog = ab.astype(F32) * da_s[rows, :]
            glog_s[rows, :] = glog
            _store_split(gsplit_s, rows, glog, t)
            gtot_s[rows, :] = _row_sum(glog)


def _sb_dscores(glog_s, cum_s, rest_s, gtot_s, keep_s, dz_s, t, diag):
    for i, rows in _strips(t):
        glog = glog_s[rows, :]
        rest = rest_s[rows, :]
        before = _wide(rest, t) - cum_s[rows, :] - glog
        dz = (glog + before) * keep_s[rows, :] - before
        if diag:
            dz = jnp.where(_strip_mask(i, t), dz, 0.0)
        dz_s[rows, :] = dz.astype(BF16)
        rest_s[rows, :] = rest - gtot_s[rows, :]


def _attn_fwd(proj, g_attn, cat, w_conv, staged, t=ATTN_BLOCK):
    s = proj.shape[0]
    w_attn = g_attn.shape[1]
    nh = w_attn // LANES
    t = _tile(s, t)
    nq = s // t
    q0 = 3 * w_conv // LANES
    scale = HEAD_DIM ** -0.5
    nw = len(staged)

    def body(q_ref, k_ref, v_ref, g_ref, cat_in, *rest):
        staged_refs, rest = rest[:nw], rest[nw:]
        o_ref, cat_ref = rest[:2]
        gathered_refs, rest = rest[2:2 + nw], rest[2 + nw:]
        kb, vb, tri_s, qm_s, z_s, split_s, zl_s, suf_s, a_s, run_s, tot_s, acc_s = rest[:12]
        gather_sems = rest[12:]
        del cat_in
        qi = pl.program_id(1)

        @pl.when((pl.program_id(0) == 0) & (qi == 0))
        def _():
            for cp in _gather_copies(staged_refs, gathered_refs, *gather_sems):
                cp.start()

        @pl.when(qi == 0)
        def _():
            kb[...] = k_ref[...].astype(BF16)
            vb[...] = v_ref[...].astype(BF16)
            tri_s[...] = _suffix_operator(t)

        low = _low_half()
        q = q_ref[...] * scale
        for h, msk in enumerate((low, jnp.logical_not(low))):
            qm_s[h] = jnp.where(msk, q, 0.0).astype(BF16)
            run_s[h] = jnp.zeros((t, LANES), F32)
            acc_s[h] = jnp.zeros((t, LANES), F32)

        def key_rows(kblk):
            return pl.ds(pl.multiple_of(kblk * t, t), t)

        def scores_matmul(kblk):
            ks = kb[key_rows(kblk), :]
            for h in range(2):
                z_s[h] = lax.dot_general(qm_s[h], ks, _NT, preferred_element_type=F32)

        def front(kblk, diag):
            for h in range(2):
                _sb_scores(z_s.at[h], split_s.at[h], zl_s.at[h], tot_s.at[h], None, t, diag)
                suf_s[h] = jnp.dot(split_s[h], tri_s[...], preferred_element_type=F32)
            scores_matmul(jnp.maximum(kblk - 1, 0))
            for h in range(2):
                _sb_weights(zl_s.at[h], suf_s.at[h], run_s.at[h], tot_s.at[h], a_s.at[h], t, diag)

        def tail(kblk):
            vs = vb[key_rows(kblk), :]
            for h in range(2):
                acc_s[h] += jnp.dot(a_s[h], vs, preferred_element_type=F32)

        scores_matmul(qi)
        front(qi, True)

        def loop(it, carry):
            tail(qi - it)
            front(qi - 1 - it, False)
            return carry

        lax.fori_loop(0, qi, loop, 0)
        tail(0)
        o = jnp.where(low, acc_s[0], acc_s[1])
        o_ref[...] = o
        r = lax.rsqrt(_half_mean(o * o, low) + EPS)
        cat_ref[...] = (o * r * g_ref[...]).astype(BF16)

        @pl.when((pl.program_id(0) == nh - 1) & (qi == nq - 1))
        def _():
            for cp in _gather_copies(staged_refs, gathered_refs, *gather_sems):
                cp.wait()

    whole = lambda col0: pl.BlockSpec((s, LANES), lambda h, i: (0, col0 + h))
    res = _pcall(
        body, name="attn_fwd", grid=(nh, nq),
        in_specs=[pl.BlockSpec((t, LANES), lambda h, i: (i, q0 + h)),
                  whole(q0 + nh), whole(q0 + 2 * nh),
                  pl.BlockSpec((1, LANES), lambda h, i: (0, h)),
                  pl.BlockSpec(memory_space=pl.ANY)] + [pl.BlockSpec(memory_space=pl.ANY)] * nw,
        out_specs=[pl.BlockSpec((t, LANES), lambda h, i: (i, h)),
                   pl.BlockSpec((t, LANES), lambda h, i: (i, w_conv // LANES + h))]
        + [pl.BlockSpec(memory_space=pl.ANY)] * nw,
        out_shape=[jax.ShapeDtypeStruct((s, w_attn), F32),
                   jax.ShapeDtypeStruct(cat.shape, BF16)]
        + [jax.ShapeDtypeStruct((N_DEV, *a.shape), BF16) for a in staged],
        scratch_shapes=[pltpu.VMEM((s, LANES), BF16), pltpu.VMEM((s, LANES), BF16),
                        pltpu.VMEM((2 * t, t), BF16),
                        pltpu.VMEM((2, t, LANES), BF16),
                        pltpu.VMEM((2, t, t), F32),
                        pltpu.VMEM((2, t, 2 * t), BF16),
                        pltpu.VMEM((2, t, t), F32),
                        pltpu.VMEM((2, t, t), F32),
                        pltpu.VMEM((2, t, t), BF16),
                        pltpu.VMEM((2, t, LANES), F32),
                        pltpu.VMEM((2, t, LANES), F32),
                        pltpu.VMEM((2, t, LANES), F32)]
        + _exchange_sems(nw, local=True),
        input_output_aliases={4: 1},
        compiler_params=_params(("arbitrary", "arbitrary")),
    )(proj, proj, proj, g_attn, cat, *staged)
    return res[0], res[1], res[2:]


def _attn_bwd(proj, o, dcat, g_attn, w_conv, partials, t=ATTN_BLOCK):
    s, n_cols = proj.shape
    w_attn = g_attn.shape[1]
    nh = w_attn // LANES
    t = _tile(s, t)
    nq = s // t
    q0 = 3 * w_conv // LANES
    scale = HEAD_DIM ** -0.5
    nw = len(partials)

    def body(q_ref, k_ref, v_ref, o_ref, do_ref, g_ref, *rest):
        partial_refs, rest = rest[:nw], rest[nw:]
        dproj_ref, dg_ref = rest[:2]
        received_refs, rest = rest[2:2 + nw], rest[2 + nw:]
        (kb, vb, dkt_acc, dvt_acc, stash, tri_s, qm_s, dom_s, qt_s, dot_s, z_s, da_s, split_s, zl_s,
         keep_s, suf_s, a_s, glog_s, gsplit_s, cum_s, dz_s, run_s, tot_s, rest_s, gtot_s, dq_s) = rest[:26]
        scatter_sems = rest[26:]
        step_i = pl.program_id(1)
        which = pl.program_id(2)
        qi = nq - 1 - step_i
        head_pair = pl.program_id(0)

        @pl.when((head_pair == 0) & (step_i == 0) & (which == 0))
        def _():
            for cp in _scatter_copies(partial_refs, received_refs, *scatter_sems):
                cp.start()

        @pl.when((head_pair == nh - 1) & (step_i == nq - 1) & (which == 2))
        def _():
            for cp in _scatter_copies(partial_refs, received_refs, *scatter_sems):
                cp.wait()

        @pl.when(which == 0)
        def _():
            @pl.when(step_i == 0)
            def _():
                kb[...] = k_ref[...].astype(BF16)
                vb[...] = v_ref[...].astype(BF16)
                tri_s[...] = _suffix_operator(t)
                dkt_acc[...] = jnp.zeros_like(dkt_acc)
                dvt_acc[...] = jnp.zeros_like(dvt_acc)
                dg_ref[...] = jnp.zeros_like(dg_ref)

            low = _low_half()
            q = q_ref[...] * scale
            ov = o_ref[...]
            d_o, dgn = _head_norm_bwd(ov, do_ref[...], g_ref[...], low)
            dg_ref[...] += jnp.sum(dgn, axis=0, keepdims=True)
            for h, msk in enumerate((low, jnp.logical_not(low))):
                qh = jnp.where(msk, q, 0.0)
                doh = jnp.where(msk, d_o, 0.0)
                dom = doh.astype(BF16)
                qm_s[h] = qh.astype(BF16)
                dom_s[h] = dom
                qt_s[h] = qh.T.astype(BF16)
                dot_s[h] = doh.T.astype(BF16)
                rest_s[h] = _row_sum(dom.astype(F32) * ov)
                run_s[h] = jnp.zeros((t, LANES), F32)
                dq_s[h] = jnp.zeros((t, LANES), F32)

            def key_rows(kblk):
                return pl.ds(pl.multiple_of(kblk * t, t), t)

            def scores_matmul(kblk):
                ks = kb[key_rows(kblk), :]
                for h in range(2):
                    z_s[h] = lax.dot_general(qm_s[h], ks, _NT, preferred_element_type=F32)

            def da_matmul(kblk):
                vs = vb[key_rows(kblk), :]
                for h in range(2):
                    da_s[h] = lax.dot_general(dom_s[h], vs, _NT, preferred_element_type=F32)

            def front(kblk, diag):
                nxt = jnp.maximum(kblk - 1, 0)
                for h in range(2):
                    _sb_scores(z_s.at[h], split_s.at[h], zl_s.at[h], tot_s.at[h], keep_s.at[h], t, diag)
                    suf_s[h] = jnp.dot(split_s[h], tri_s[...], preferred_element_type=F32)
                scores_matmul(nxt)
                for h in range(2):
                    _sb_weights(zl_s.at[h], suf_s.at[h], run_s.at[h], tot_s.at[h], a_s.at[h], t, diag,
                                da_s.at[h], glog_s.at[h], gsplit_s.at[h], gtot_s.at[h])
                    cum_s[h] = jnp.dot(gsplit_s[h], tri_s[...], preferred_element_type=F32)
                da_matmul(nxt)
                for h in range(2):
                    _sb_dscores(glog_s.at[h], cum_s.at[h], rest_s.at[h], gtot_s.at[h], keep_s.at[h],
                                dz_s.at[h], t, diag)

            def tail(kblk):
                ks = kb[key_rows(kblk), :]
                dkt = dkt_acc[kblk]
                dvt = dvt_acc[kblk]
                for h in range(2):
                    dq_s[h] += jnp.dot(dz_s[h], ks, preferred_element_type=F32)
                    dkt = dkt + jnp.dot(qt_s[h], dz_s[h], preferred_element_type=F32)
                    dvt = dvt + jnp.dot(dot_s[h], a_s[h], preferred_element_type=F32)
                dkt_acc[kblk] = dkt
                dvt_acc[kblk] = dvt

            scores_matmul(qi)
            da_matmul(qi)
            front(qi, True)

            def loop(it, carry):
                tail(qi - it)
                front(qi - 1 - it, False)
                return carry

            lax.fori_loop(0, qi, loop, 0)
            tail(0)
            stash[0] = (jnp.where(low, dq_s[0], dq_s[1]) * scale).astype(BF16)
            stash[1] = dkt_acc[qi].T.astype(BF16)
            stash[2] = dvt_acc[qi].T.astype(BF16)

        dproj_ref[...] = stash[which]

    whole = lambda col0: pl.BlockSpec((s, LANES), lambda h, i, w: (0, col0 + h))
    blk = lambda col0: pl.BlockSpec((t, LANES), lambda h, i, w: (nq - 1 - i, col0 + h))
    res = _pcall(
        body, name="attn_bwd", grid=(nh, nq, 3),
        in_specs=[blk(q0), whole(q0 + nh), whole(q0 + 2 * nh), blk(0), blk(w_conv // LANES),
                  pl.BlockSpec((1, LANES), lambda h, i, w: (0, h))] + [pl.BlockSpec(memory_space=pl.ANY)] * nw,
        out_specs=[pl.BlockSpec((t, LANES), lambda h, i, w: (nq - 1 - i, q0 + w * nh + h)),
                   pl.BlockSpec((1, LANES), lambda h, i, w: (0, h))] + [pl.BlockSpec(memory_space=pl.ANY)] * nw,
        out_shape=[jax.ShapeDtypeStruct((s, n_cols), BF16), jax.ShapeDtypeStruct((1, w_attn), F32)]
        + [jax.ShapeDtypeStruct((N_PEERS, *a.shape[1:]), BF16) for a in partials],
        scratch_shapes=[pltpu.VMEM((s, LANES), BF16), pltpu.VMEM((s, LANES), BF16),
                        pltpu.VMEM((nq, LANES, t), F32),
                        pltpu.VMEM((nq, LANES, t), F32),
                        pltpu.VMEM((3, t, LANES), BF16),
                        pltpu.VMEM((2 * t, t), BF16),
                        pltpu.VMEM((2, t, LANES), BF16),
                        pltpu.VMEM((2, t, LANES), BF16),
                        pltpu.VMEM((2, LANES, t), BF16),
                        pltpu.VMEM((2, LANES, t), BF16),
                        pltpu.VMEM((2, t, t), F32),
                        pltpu.VMEM((2, t, t), F32),
                        pltpu.VMEM((2, t, 2 * t), BF16),
                        pltpu.VMEM((2, t, t), F32),
                        pltpu.VMEM((2, t, t), F32),
                        pltpu.VMEM((2, t, t), F32),
                        pltpu.VMEM((2, t, t), BF16),
                        pltpu.VMEM((2, t, t), F32),
                        pltpu.VMEM((2, t, 2 * t), BF16),
                        pltpu.VMEM((2, t, t), F32),
                        pltpu.VMEM((2, t, t), BF16),
                        pltpu.VMEM((2, t, LANES), F32),
                        pltpu.VMEM((2, t, LANES), F32),
                        pltpu.VMEM((2, t, LANES), F32),
                        pltpu.VMEM((2, t, LANES), F32),
                        pltpu.VMEM((2, t, LANES), F32)]
        + _exchange_sems(nw),
        compiler_params=_params(("arbitrary", "arbitrary", "arbitrary")),
    )(proj, proj, proj, o, dcat, g_attn, *partials)
    return res[0], res[1], res[2:]


def _place():
    return lax.axis_index("x"), lax.axis_index("y"), lax.axis_index("c")


def _other_chips(x, y):
    return [(1 - x, y), (x, 1 - y), (1 - x, 1 - y)]


def _slot(px, py, pc):
    return 4 * px + 2 * py + pc


def _all_gather(shards, out_dtypes):
    nw = len(shards)

    def body(*refs):
        ins, outs, stage = refs[:nw], refs[nw:2 * nw], refs[2 * nw:3 * nw]
        send_sems, recv_sems, local_sems = refs[3 * nw:]
        x, y, c = _place()
        me, sibling = (x, y, c), (x, y, 1 - c)
        chips = _other_chips(x, y)

        def copy(w, k, block, to, src=None):
            dst = outs[w].at[_slot(*block)]
            return pltpu.make_async_remote_copy(
                src_ref=dst if src is None else src, dst_ref=dst,
                send_sem=send_sems.at[w * 7 + k], recv_sem=recv_sems.at[w * 7 + k],
                device_id=to, device_id_type=MESH)

        started = []
        local = []
        for w in range(nw):
            stage[w][...] = ins[w][...].astype(stage[w].dtype)
            cp = pltpu.make_async_copy(stage[w], outs[w].at[_slot(*me)], local_sems.at[w])
            cp.start()
            local.append(cp)
            started.append(copy(w, 0, me, sibling, src=stage[w]))
            started[-1].start()
            for j, chip in enumerate(chips):
                started.append(copy(w, 1 + j, me, (*chip, c), src=stage[w]))
                started[-1].start()
        for j, chip in enumerate(chips):
            for w in range(nw):
                copy(w, 1 + j, (*chip, c), me).wait_recv()
                started.append(copy(w, 4 + j, (*chip, c), sibling))
                started[-1].start()
        for w in range(nw):
            copy(w, 0, sibling, me).wait_recv()
            for j, chip in enumerate(chips):
                copy(w, 4 + j, (*chip, 1 - c), me).wait_recv()
        for cp in started:
            cp.wait_send()
        for cp in local:
            cp.wait()

    return _pcall(
        body, name="all_gather_weights",
        in_specs=[pl.BlockSpec(memory_space=pltpu.VMEM)] * nw,
        out_specs=[pl.BlockSpec(memory_space=pl.ANY)] * nw,
        out_shape=[jax.ShapeDtypeStruct((N_DEV, *a.shape), d) for a, d in zip(shards, out_dtypes)],
        scratch_shapes=[pltpu.VMEM(a.shape, d) for a, d in zip(shards, out_dtypes)]
        + [pltpu.SemaphoreType.DMA((7 * nw,)), pltpu.SemaphoreType.DMA((7 * nw,)),
           pltpu.SemaphoreType.DMA((nw,))],
        compiler_params=_params(),
    )(*shards)


N_PEERS = N_DEV - 1


def _peer(k):
    x, y, c = _place()
    return (x ^ (k >> 2), y ^ ((k >> 1) & 1), c ^ (k & 1))


def _fanout(src_of, dst_of, nw, send_sems, recv_sems):
    return [pltpu.make_async_remote_copy(
        src_ref=src_of(w, k), dst_ref=dst_of(w, k),
        send_sem=send_sems.at[w * N_PEERS + k - 1], recv_sem=recv_sems.at[w * N_PEERS + k - 1],
        device_id=_peer(k), device_id_type=MESH) for w in range(nw) for k in range(1, N_DEV)]


def _gather_copies(staged, gathered, send_sems, recv_sems, local_sems):
    me = _slot(*_place())
    nw = len(staged)
    remote = _fanout(lambda w, k: staged[w], lambda w, k: gathered[w].at[me], nw, send_sems, recv_sems)
    local = [pltpu.make_async_copy(staged[w], gathered[w].at[me], local_sems.at[w]) for w in range(nw)]
    return remote + local


def _scatter_copies(partials, received, send_sems, recv_sems):
    me = _slot(*_place())
    return _fanout(lambda w, k: partials[w].at[me ^ k], lambda w, k: received[w].at[k - 1],
                   len(partials), send_sems, recv_sems)


def _exchange_sems(nw, local=False):
    sems = [pltpu.SemaphoreType.DMA((N_PEERS * nw,)), pltpu.SemaphoreType.DMA((N_PEERS * nw,))]
    return sems + ([pltpu.SemaphoreType.DMA((nw,))] if local else [])


def _cast_shards(shards):
    def body(*refs):
        for src, dst in zip(refs[:len(shards)], refs[len(shards):]):
            dst[...] = src[...].astype(BF16)

    return _pcall(
        body, name="cast_shards",
        in_specs=[pl.BlockSpec(memory_space=pltpu.VMEM)] * len(shards),
        out_specs=[pl.BlockSpec(memory_space=pltpu.VMEM)] * len(shards),
        out_shape=[jax.ShapeDtypeStruct(a.shape, BF16) for a in shards],
        compiler_params=_params(),
    )(*shards)


def _all_reduce_small(packed):
    r = packed.shape[0]

    def body(x_ref, o_ref, gathered, send_sems, recv_sems):
        x, y, c = _place()
        me = _slot(x, y, c)
        gathered[me] = x_ref[...]
        peers = [(px, py, pc) for px in range(2) for py in range(2) for pc in range(2)]
        started = []
        for k in range(1, N_DEV):
            to = (x ^ (k >> 2), y ^ ((k >> 1) & 1), c ^ (k & 1))
            cp = pltpu.make_async_remote_copy(
                src_ref=x_ref, dst_ref=gathered.at[me],
                send_sem=send_sems.at[k - 1], recv_sem=recv_sems.at[k - 1],
                device_id=to, device_id_type=MESH)
            cp.start()
            started.append(cp)
        del peers
        for cp in started:
            cp.wait()
        total = gathered[0]
        for k in range(1, N_DEV):
            total = total + gathered[k]
        o_ref[...] = total

    return _pcall(
        body, name="all_reduce_small",
        in_specs=[pl.BlockSpec(memory_space=pltpu.VMEM)],
        out_specs=pl.BlockSpec(memory_space=pltpu.VMEM),
        out_shape=jax.ShapeDtypeStruct(packed.shape, F32),
        scratch_shapes=[pltpu.VMEM((N_DEV, r, LANES), F32),
                        pltpu.SemaphoreType.DMA((N_DEV - 1,)), pltpu.SemaphoreType.DMA((N_DEV - 1,))],
        compiler_params=_params(),
    )(packed)


def _adam_math(w, g, m, v):
    m = ADAM_B1 * m + (1.0 - ADAM_B1) * g
    v = ADAM_B2 * v + (1.0 - ADAM_B2) * jnp.square(g)
    m_hat = m / (1.0 - ADAM_B1 ** ADAM_STEP)
    v_hat = v / (1.0 - ADAM_B2 ** ADAM_STEP)
    delta = -ADAM_LR * (m_hat / (jnp.sqrt(v_hat) + ADAM_EPS) + ADAM_WD * w)
    return delta, m, v


def _adam_sharded(name, own, received, w, m, v, place, tr=256):
    r, cdim = w.shape
    tr = _tile(r, tr) if r % LANES == 0 else r

    def body(place_ref, own_ref, rec_ref, w_ref, m_ref, v_ref, g_ref, d_ref, nm_ref, nv_ref):
        del place_ref
        g = own_ref[...]
        for j in range(N_PEERS):
            g = g + rec_ref[j].astype(F32)
        delta, nm, nv = _adam_math(w_ref[...], g, m_ref[...], v_ref[...])
        g_ref[...] = g
        d_ref[...] = delta
        nm_ref[...] = nm
        nv_ref[...] = nv

    blk = pl.BlockSpec((tr, cdim), lambda i, pr: (i, 0))
    grid_spec = pltpu.PrefetchScalarGridSpec(
        num_scalar_prefetch=1, grid=(r // tr,),
        in_specs=[pl.BlockSpec((None, tr, cdim), lambda i, pr: (4 * pr[0] + 2 * pr[1] + pr[2], i, 0)),
                  pl.BlockSpec((N_PEERS, tr, cdim), lambda i, pr: (0, i, 0)), blk, blk, blk],
        out_specs=[blk] * 4)
    return _pcall(body, name=name, grid_spec=grid_spec,
                  out_shape=[jax.ShapeDtypeStruct((r, cdim), F32)] * 4,
                  compiler_params=_params(("parallel",)))(place, own, received, w, m, v)


def _adam_small(w, g, m, v):
    def body(w_ref, g_ref, m_ref, v_ref, d_ref, nm_ref, nv_ref):
        delta, nm, nv = _adam_math(w_ref[...], g_ref[...], m_ref[...], v_ref[...])
        d_ref[...] = delta
        nm_ref[...] = nm
        nv_ref[...] = nv

    return _pcall(body, name="adam_small",
                  in_specs=[pl.BlockSpec(memory_space=pltpu.VMEM)] * 4,
                  out_specs=[pl.BlockSpec(memory_space=pltpu.VMEM)] * 3,
                  out_shape=[jax.ShapeDtypeStruct(w.shape, F32)] * 3,
                  compiler_params=_params())(w, g, m, v)


def _rows(vec):
    return vec.reshape(-1, LANES)


def kernel(x, p, g_mix, w_in, conv_w, g_conv_out, g_attn_out, w_out, g_mlp, w_up, w_down, g_ple, w_ple_gate, w_ple_proj, g_final, loss_target, m_g_mix, m_w_in, m_conv_w, m_g_conv_out, m_g_attn_out, m_w_out, m_g_mlp, m_w_up, m_w_down, m_g_ple, m_w_ple_gate, m_w_ple_proj, m_g_final, v_g_mix, v_w_in, v_conv_w, v_g_conv_out, v_g_attn_out, v_w_out, v_g_mlp, v_w_up, v_w_down, v_g_ple, v_w_ple_gate, v_w_ple_proj, v_g_final):
    s, d = x.shape[1], x.shape[2]
    w_conv = g_conv_out.shape[1]
    w_attn = g_attn_out.shape[1]
    cw = conv_w.shape[2]
    xs, ps, tgt = x[0], p[0, 0], loss_target[0]
    place = jnp.stack([lax.axis_index("x"), lax.axis_index("y"), lax.axis_index("c")]).astype(jnp.int32)
    my_slot = 4 * place[0] + 2 * place[1] + place[2]

    conv_tile = jnp.pad(conv_w[0], ((0, HALO - CONV_K), (0, LANES - cw)))
    big = [w_in[0], w_out[0], w_up[0], w_down[0], w_ple_gate[0], w_ple_proj[0]]
    win_g, conv_g = _all_gather([big[0], conv_tile], [BF16, F32])
    staged = _cast_shards(big[1:])
    conv_full = jnp.transpose(conv_g[:, :CONV_K, :cw], (1, 0, 2)).reshape(CONV_K, w_conv)
    in_shard, up_shard, proj_shard = big[0].shape[1], big[2].shape[1], big[5].shape[1]

    a = _rmsnorm_fwd("norm_mix", xs, g_mix)
    proj, = _mm_nn("in_proj", a, win_g, n_shard=in_shard, tn=in_shard)
    cat = _conv_fwd(proj, conv_full, g_conv_out, w_conv, d)
    o, cat, (wout_g, wup_g, wdown_g, wgate_g, wproj_g) = _attn_fwd(proj, g_attn_out, cat, w_conv, staged)
    wout_f = wout_g.reshape(-1, wout_g.shape[-1])
    wdown_f = wdown_g.reshape(-1, wdown_g.shape[-1])
    wgate_f = wgate_g.reshape(-1, wgate_g.shape[-1])
    h1, = _mm_nn("out_proj", cat, wout_f, epilogue=_ep_residual, extras=(xs,))
    mn = _rmsnorm_fwd("norm_mlp", h1, g_mlp)
    act, = _mm_nn("mlp_up", mn, wup_g, n_shard=up_shard, epilogue=_ep_up, out_dtypes=(BF16,))
    h2, = _mm_nn("mlp_down", act, wdown_f, epilogue=_ep_residual, extras=(h1,))
    n3 = _rmsnorm_fwd("norm_ple", h2, g_ple)
    gl, = _mm_nn("ple_gate", n3, wgate_f)
    pp = _ple_proj(ps, wproj_g)
    loss_part, dh3, dgl, dpp, dg_final = _ple_loss(h2, gl, pp, tgt, g_final.reshape(1, d))
    loss = lax.psum(loss_part[0, 0], ("x", "y", "c"))

    def slots(t2d):
        return t2d.reshape(N_DEV, -1, t2d.shape[-1])

    dw_proj = _d_ple_proj(ps, dpp, proj_shard)
    dw_gate = [slots(t) for t in _mm_tn("d_w_ple_gate", n3, dgl)]
    dn3, = _mm_nt("d_norm_ple", dgl, wgate_f)
    dh2, dh2b, dg_ple = _rmsnorm_bwd("norm_ple_bwd", dn3, h2, g_ple, dh3)
    du, = _mm_nt("d_mlp_act", dh2b, wdown_f, epilogue=_ep_dact, out_dtypes=(BF16,), extras=(act,))
    dw_down = [slots(t) for t in _mm_tn("d_w_down", act, dh2b)]
    dw_up = _mm_tn("d_w_up", mn, du, n_shard=up_shard)
    dmn, = _mm_nt("d_norm_mlp", du, wup_g, k_shard=up_shard)
    dh1, dh1b, dg_mlp = _rmsnorm_bwd("norm_mlp_bwd", dmn, h1, g_mlp, dh2)
    dcat, = _mm_nt("d_cat", dh1b, wout_f)
    dw_out = [slots(t) for t in _mm_tn("d_w_out", cat, dh1b)]
    late = [dw_out, dw_up, dw_down, dw_gate, dw_proj]
    dproj, dg_attn, late_recv = _attn_bwd(proj, o, dcat, g_attn_out, w_conv, [pb for _, pb in late])
    dproj, dconv, dg_conv = _conv_bwd(proj, dcat, conv_full, g_conv_out, dproj, w_conv)
    dw_in = _mm_tn("d_w_in", a, dproj, n_shard=in_shard, tn=in_shard)
    da, in_recv = _mm_nt("d_norm_mix", dproj, win_g, k_shard=in_shard, tk=in_shard, exchange=(dw_in[1],))
    grad_x, _, dg_mix = _rmsnorm_bwd("norm_mix_bwd", da, xs, g_mix, dh1)

    names = ["w_in", "w_out", "w_up", "w_down", "w_ple_gate", "w_ple_proj"]
    owns = [dw_in[0]] + [pf for pf, _ in late]
    recvs = [in_recv, *late_recv]
    moments = [(m_w_in, v_w_in), (m_w_out, v_w_out), (m_w_up, v_w_up), (m_w_down, v_w_down),
               (m_w_ple_gate, v_w_ple_gate), (m_w_ple_proj, v_w_ple_proj)]
    big_out = {}
    for n, own, rc, wt, (mm, vv) in zip(names, owns, recvs, big, moments):
        big_out[n] = [t[None] for t in _adam_sharded("adam_" + n, own, rc, wt, mm[0], vv[0], place)]

    small_g = jnp.concatenate(
        [_rows(dg_mix[0]), _rows(dg_conv[0]), _rows(dg_attn[0]), _rows(dg_mlp[0]), _rows(dg_ple[0]),
         _rows(dg_final[0]), _rows(dconv.reshape(-1))], axis=0)
    n_gain_rows = small_g.shape[0] - CONV_K * w_conv // LANES
    pad_rows = (-small_g.shape[0]) % HALO
    small_g = _all_reduce_small(jnp.pad(small_g, ((0, pad_rows), (0, 0))))
    dconv_full = small_g[n_gain_rows:n_gain_rows + CONV_K * w_conv // LANES].reshape(CONV_K, w_conv)
    dconv_mine = lax.dynamic_slice(dconv_full, (0, my_slot * cw), (CONV_K, cw))

    def pack(vecs, conv_part):
        rows = [_rows(t.reshape(-1)) for t in vecs]
        rows.append(jnp.pad(conv_part, ((0, HALO - CONV_K), (0, LANES - cw))))
        return jnp.concatenate(rows, axis=0)

    gains = [g_mix, g_conv_out, g_attn_out, g_mlp, g_ple, g_final]
    gains_m = [m_g_mix, m_g_conv_out, m_g_attn_out, m_g_mlp, m_g_ple, m_g_final]
    gains_v = [v_g_mix, v_g_conv_out, v_g_attn_out, v_g_mlp, v_g_ple, v_g_final]
    gpack = jnp.concatenate([small_g[:n_gain_rows], jnp.pad(dconv_mine, ((0, HALO - CONV_K), (0, LANES - cw)))], axis=0)
    sd, sm, sv = _adam_small(pack(gains, conv_w[0]), gpack, pack(gains_m, m_conv_w[0]), pack(gains_v, v_conv_w[0]))

    def unpack(packed):
        out, r0 = [], 0
        for t in gains:
            nr = t.size // LANES
            out.append(packed[r0:r0 + nr].reshape(t.shape))
            r0 += nr
        out.append(packed[r0:r0 + CONV_K, :cw][None])
        return out

    sg_l, sd_l, sm_l, sv_l = unpack(gpack), unpack(sd), unpack(sm), unpack(sv)
    small_names = ["g_mix", "g_conv_out", "g_attn_out", "g_mlp", "g_ple", "g_final", "conv_w"]
    small_out = {n: [sg_l[i], sd_l[i], sm_l[i], sv_l[i]] for i, n in enumerate(small_names)}

    order = ["g_mix", "w_in", "conv_w", "g_conv_out", "g_attn_out", "w_out", "g_mlp", "w_up", "w_down",
             "g_ple", "w_ple_gate", "w_ple_proj", "g_final"]
    table = {**big_out, **small_out}
    outs = [loss, grad_x[None]]
    for kind in range(4):
        outs.extend(table[n][kind] for n in order)
    return tuple(outs)
```

```python
import functools

import jax
import jax.numpy as jnp
from jax import lax
from jax.experimental import pallas as pl
from jax.experimental.pallas import tpu as pltpu

F32 = jnp.float32
BF16 = jnp.bfloat16
EPS = 1e-6
HEAD_DIM = 64
LANES = 128
CONV_K = 3
ATTN_BLOCK = 256
HALO = 8
N_DEV = 8
MESH = pl.DeviceIdType.MESH
VMEM_LIMIT = 56 * 1024 * 1024

ADAM_LR = 0.001
ADAM_B1 = 0.9
ADAM_B2 = 0.999
ADAM_EPS = 1e-08
ADAM_WD = 0.01
ADAM_STEP = 10


def _pcall(body, **kw):
    return pl.pallas_call(body, **kw)


def _params(sem=None, **kw):
    return pltpu.CompilerParams(dimension_semantics=sem, vmem_limit_bytes=VMEM_LIMIT, **kw)


def _tile(dim, pref):
    t = min(dim, pref)
    while dim % t:
        t -= LANES
    assert t > 0, (dim, pref)
    return t


_NN = (((1,), (0,)), ((), ()))
_NT = (((1,), (1,)), ((), ()))
_TN = (((0,), (0,)), ((), ()))


def _ep_store(acc, outs):
    outs[0][...] = acc.astype(outs[0].dtype)


def _ep_both(acc, outs):
    outs[0][...] = acc
    outs[1][...] = acc.astype(BF16)


def _ep_residual(acc, res, outs):
    outs[0][...] = acc + res[...]


def _ep_up(acc, outs):
    outs[0][...] = jnp.square(jnp.maximum(acc, 0.0)).astype(BF16)


def _ep_dact(acc, act, outs):
    outs[0][...] = (acc * (2.0 * jnp.sqrt(act[...].astype(F32)))).astype(BF16)


def _matmul(name, a, b, *, dims, grid, a_spec, b_spec, acc_shape, out_shapes, out_specs,
            epilogue=_ep_store, extras=(), extra_specs=(), exchange=()):
    nk = grid[2]
    n_ex, n_out, n_xc = len(extras), len(out_shapes), len(exchange)
    last = tuple(g - 1 for g in grid)

    def product(a_ref, b_ref):
        return lax.dot_general(a_ref[...].astype(BF16), b_ref[...].astype(BF16), dims,
                               preferred_element_type=F32)

    def body(a_ref, b_ref, *rest):
        ex, rest = rest[:n_ex], rest[n_ex:]
        partials, rest = rest[:n_xc], rest[n_xc:]
        outs, rest = rest[:n_out], rest[n_out:]
        received, rest = rest[:n_xc], rest[n_xc:]
        ids = [pl.program_id(axis) for axis in range(3)]
        if n_xc:
            @pl.when((ids[0] == 0) & (ids[1] == 0) & (ids[2] == 0))
            def _():
                for cp in _scatter_copies(partials, received, *rest[-2:]):
                    cp.start()

        if nk == 1:
            epilogue(product(a_ref, b_ref), *ex, outs)
        else:
            acc = rest[0]

            @pl.when(ids[2] == 0)
            def _():
                acc[...] = product(a_ref, b_ref)

            @pl.when(ids[2] > 0)
            def _():
                acc[...] += product(a_ref, b_ref)

            @pl.when(ids[2] == nk - 1)
            def _():
                epilogue(acc[...], *ex, outs)

        if n_xc:
            @pl.when((ids[0] == last[0]) & (ids[1] == last[1]) & (ids[2] == last[2]))
            def _():
                for cp in _scatter_copies(partials, received, *rest[-2:]):
                    cp.wait()

    anywhere = [pl.BlockSpec(memory_space=pl.ANY)] * n_xc
    return _pcall(
        body, name=name, grid=grid,
        in_specs=[a_spec, b_spec, *extra_specs, *anywhere],
        out_specs=[*out_specs, *anywhere],
        out_shape=[*out_shapes, *(jax.ShapeDtypeStruct((N_PEERS, *p.shape[1:]), BF16) for p in exchange)],
        scratch_shapes=([] if nk == 1 else [pltpu.VMEM(acc_shape, F32)]) + (_exchange_sems(n_xc) if n_xc else []),
        compiler_params=_params(("arbitrary",) * 3 if n_xc else ("parallel", "parallel", "arbitrary")),
    )(a, b, *extras, *exchange)


def _mm_nn(name, a, w, *, n_shard=None, epilogue=_ep_store, out_dtypes=(F32,), extras=(), tm=1024, tn=1024, tk=1024):
    m, kd = a.shape
    if n_shard is None:
        n = w.shape[1]
        tn = _tile(n, tn)
        tk = _tile(kd, tk)
        b_spec = pl.BlockSpec((tk, tn), lambda i, j, k: (k, j))
    else:
        n = N_DEV * n_shard
        tn = _tile(n_shard, tn)
        tk = _tile(kd, tk)
        per = n_shard // tn
        b_spec = pl.BlockSpec((None, tk, tn), lambda i, j, k: (j // per, k, j % per))
    tm = _tile(m, tm)
    o_spec = pl.BlockSpec((tm, tn), lambda i, j, k: (i, j))
    return _matmul(
        name, a, w, dims=_NN, grid=(m // tm, n // tn, kd // tk),
        a_spec=pl.BlockSpec((tm, tk), lambda i, j, k: (i, k)), b_spec=b_spec,
        acc_shape=(tm, tn),
        out_shapes=[jax.ShapeDtypeStruct((m, n), d) for d in out_dtypes],
        out_specs=[o_spec] * len(out_dtypes),
        epilogue=epilogue, extras=extras, extra_specs=[o_spec] * len(extras))


def _mm_nt(name, a, w, *, k_shard=None, epilogue=_ep_store, out_dtypes=(F32,), extras=(), exchange=(),
           tm=1024, tn=1024, tk=1024):
    m, kd = a.shape
    if k_shard is None:
        n = w.shape[0]
        tn = _tile(n, tn)
        tk = _tile(kd, tk)
        b_spec = pl.BlockSpec((tn, tk), lambda i, j, k: (j, k))
    else:
        n = w.shape[1]
        tn = _tile(n, tn)
        tk = _tile(k_shard, tk)
        per = k_shard // tk
        b_spec = pl.BlockSpec((None, tn, tk), lambda i, j, k: (k // per, j, k % per))
    tm = _tile(m, tm)
    o_spec = pl.BlockSpec((tm, tn), lambda i, j, k: (i, j))
    return _matmul(
        name, a, w, dims=_NT, grid=(m // tm, n // tn, kd // tk),
        a_spec=pl.BlockSpec((tm, tk), lambda i, j, k: (i, k)), b_spec=b_spec,
        acc_shape=(tm, tn),
        out_shapes=[jax.ShapeDtypeStruct((m, n), d) for d in out_dtypes],
        out_specs=[o_spec] * len(out_dtypes),
        epilogue=epilogue, extras=extras, extra_specs=[o_spec] * len(extras), exchange=exchange)


def _mm_tn(name, a, b, *, n_shard=None, tm=1024, tn=1024, tk=1024):
    t, m = a.shape
    n = b.shape[1]
    tm = _tile(m, tm)
    tk = _tile(t, tk)
    if n_shard is None:
        tn = _tile(n, tn)
        o_spec = pl.BlockSpec((tm, tn), lambda i, j, k: (i, j))
        shape = (m, n)
    else:
        tn = _tile(n_shard, tn)
        per = n_shard // tn
        o_spec = pl.BlockSpec((None, tm, tn), lambda i, j, k: (j // per, i, j % per))
        shape = (N_DEV, m, n_shard)
    return _matmul(
        name, a, b, dims=_TN, grid=(m // tm, n // tn, t // tk),
        a_spec=pl.BlockSpec((tk, tm), lambda i, j, k: (k, i)),
        b_spec=pl.BlockSpec((tk, tn), lambda i, j, k: (k, j)),
        acc_shape=(tm, tn), epilogue=_ep_both,
        out_shapes=[jax.ShapeDtypeStruct(shape, F32), jax.ShapeDtypeStruct(shape, BF16)],
        out_specs=[o_spec, o_spec])


def _ple_proj(p, w_g, tm=1024):
    s, kd = p.shape
    ns = w_g.shape[2]
    tm = _tile(s, tm)

    def body(p_ref, w_ref, o_ref):
        pv = p_ref[...].astype(BF16)
        for j in range(N_DEV):
            o_ref[:, j * ns:(j + 1) * ns] = jnp.dot(pv, w_ref[j], preferred_element_type=F32)

    return _pcall(body, name="ple_proj", grid=(s // tm,),
                  in_specs=[pl.BlockSpec((tm, kd), lambda i: (i, 0)),
                            pl.BlockSpec((N_DEV, kd, ns), lambda i: (0, 0, 0))],
                  out_specs=pl.BlockSpec((tm, N_DEV * ns), lambda i: (i, 0)),
                  out_shape=jax.ShapeDtypeStruct((s, N_DEV * ns), F32),
                  compiler_params=_params(("parallel",)))(p, w_g)


def _d_ple_proj(p, dpp, ns, tk=1024):
    s, kd = p.shape
    tk = _tile(s, tk)
    nk = s // tk

    def body(p_ref, d_ref, of_ref, ob_ref, acc):
        k = pl.program_id(0)

        @pl.when(k == 0)
        def _():
            acc[...] = jnp.zeros_like(acc)

        pv = p_ref[...].astype(BF16)
        for j in range(N_DEV):
            acc[j] += lax.dot_general(pv, d_ref[:, j * ns:(j + 1) * ns], _TN, preferred_element_type=F32)

        @pl.when(k == nk - 1)
        def _():
            of_ref[...] = acc[...]
            ob_ref[...] = acc[...].astype(BF16)

    whole = pl.BlockSpec((N_DEV, kd, ns), lambda k: (0, 0, 0))
    return _pcall(body, name="d_w_ple_proj", grid=(nk,),
                  in_specs=[pl.BlockSpec((tk, kd), lambda k: (k, 0)),
                            pl.BlockSpec((tk, N_DEV * ns), lambda k: (k, 0))],
                  out_specs=[whole, whole],
                  out_shape=[jax.ShapeDtypeStruct((N_DEV, kd, ns), F32), jax.ShapeDtypeStruct((N_DEV, kd, ns), BF16)],
                  scratch_shapes=[pltpu.VMEM((N_DEV, kd, ns), F32)],
                  compiler_params=_params(("arbitrary",)))(p, dpp)


def _row_spec(tr, d):
    return pl.BlockSpec((tr, d), lambda i: (i, 0))


def _vec_spec(d):
    return pl.BlockSpec((1, d), lambda i: (0, 0))


def _rmsnorm_fwd(name, x, g, tr=512):
    s, d = x.shape
    tr = _tile(s, tr)

    def body(x_ref, g_ref, o_ref):
        xv = x_ref[...]
        r = lax.rsqrt(jnp.mean(xv * xv, axis=-1, keepdims=True) + EPS)
        o_ref[...] = (xv * r * g_ref[...]).astype(BF16)

    return _pcall(body, name=name, grid=(s // tr,),
                  in_specs=[_row_spec(tr, d), _vec_spec(d)], out_specs=_row_spec(tr, d),
                  out_shape=jax.ShapeDtypeStruct((s, d), BF16),
                  compiler_params=_params(("parallel",)))(x, g)


def _rmsnorm_bwd(name, dn, h, g, dres, tr=512):
    s, d = h.shape
    tr = _tile(s, tr)

    def body(dn_ref, h_ref, g_ref, dres_ref, dh_ref, dhb_ref, dg_ref):
        @pl.when(pl.program_id(0) == 0)
        def _():
            dg_ref[...] = jnp.zeros_like(dg_ref)

        hv = h_ref[...]
        dnv = dn_ref[...]
        r = lax.rsqrt(jnp.mean(hv * hv, axis=-1, keepdims=True) + EPS)
        hn = hv * r
        dg_ref[...] += jnp.sum(dnv * hn, axis=0, keepdims=True)
        dhn = dnv * g_ref[...]
        dh = dres_ref[...] + r * (dhn - hn * jnp.mean(dhn * hn, axis=-1, keepdims=True))
        dh_ref[...] = dh
        dhb_ref[...] = dh.astype(BF16)

    return _pcall(body, name=name, grid=(s // tr,),
                  in_specs=[_row_spec(tr, d), _row_spec(tr, d), _vec_spec(d), _row_spec(tr, d)],
                  out_specs=[_row_spec(tr, d), _row_spec(tr, d), _vec_spec(d)],
                  out_shape=[jax.ShapeDtypeStruct((s, d), F32), jax.ShapeDtypeStruct((s, d), BF16),
                             jax.ShapeDtypeStruct((1, d), F32)],
                  compiler_params=_params(("arbitrary",)))(dn, h, g, dres)


def _ple_loss(h2, gl, pp, tgt, g_final, tr=512):
    s, d = h2.shape
    tr = _tile(s, tr)

    def body(h2_ref, gl_ref, pp_ref, t_ref, g_ref, loss_ref, dh3_ref, dgl_ref, dpp_ref, dg_ref):
        @pl.when(pl.program_id(0) == 0)
        def _():
            dg_ref[...] = jnp.zeros_like(dg_ref)
            loss_ref[...] = jnp.zeros_like(loss_ref)

        gate = jax.nn.sigmoid(gl_ref[...])
        ppv = pp_ref[...]
        h3 = h2_ref[...] + gate * ppv
        r = lax.rsqrt(jnp.mean(h3 * h3, axis=-1, keepdims=True) + EPS)
        hn = h3 * r
        gv = g_ref[...]
        diff = hn * gv - t_ref[...]
        row = jnp.mean(diff * diff, axis=-1, keepdims=True)
        loss_ref[...] += 0.5 * jnp.sum(row, axis=0, keepdims=True)
        dy = diff * (1.0 / d)
        dg_ref[...] += jnp.sum(dy * hn, axis=0, keepdims=True)
        dhn = dy * gv
        dh3 = r * (dhn - hn * jnp.mean(dhn * hn, axis=-1, keepdims=True))
        dh3_ref[...] = dh3
        dgl_ref[...] = (dh3 * ppv * gate * (1.0 - gate)).astype(BF16)
        dpp_ref[...] = (dh3 * gate).astype(BF16)

    return _pcall(body, name="ple_loss", grid=(s // tr,),
                  in_specs=[_row_spec(tr, d)] * 4 + [_vec_spec(d)],
                  out_specs=[_vec_spec(LANES), _row_spec(tr, d), _row_spec(tr, d), _row_spec(tr, d), _vec_spec(d)],
                  out_shape=[jax.ShapeDtypeStruct((1, LANES), F32), jax.ShapeDtypeStruct((s, d), F32),
                             jax.ShapeDtypeStruct((s, d), BF16), jax.ShapeDtypeStruct((s, d), BF16),
                             jax.ShapeDtypeStruct((1, d), F32)],
                  compiler_params=_params(("arbitrary",)))(h2, gl, pp, tgt, g_final)


def _low_half():
    return lax.broadcasted_iota(jnp.int32, (1, LANES), 1) < HEAD_DIM


def _half_mean(v, low):
    s_lo = jnp.sum(jnp.where(low, v, 0.0), axis=-1, keepdims=True)
    s_hi = jnp.sum(jnp.where(low, 0.0, v), axis=-1, keepdims=True)
    return jnp.where(low, s_lo, s_hi) * (1.0 / HEAD_DIM)


def _head_norm_bwd(val, dout, g, low):
    r = lax.rsqrt(_half_mean(val * val, low) + EPS)
    vn = val * r
    dvn = dout * g
    return r * (dvn - vn * _half_mean(dvn * vn, low)), dout * vn


def _conv_taps(vv_ext, w_ref, rows):
    v0 = vv_ext[HALO:]
    v1 = pltpu.roll(vv_ext, 1, 0)[HALO:]
    v2 = pltpu.roll(vv_ext, 2, 0)[HALO:]
    del rows
    return w_ref[2:3, :] * v0 + w_ref[1:2, :] * v1 + w_ref[0:1, :] * v2, (v0, v1, v2)


def _conv_fwd(proj, conv_w, g_conv, w_conv, d_model, tr=512):
    s = proj.shape[0]
    tr = _tile(s, tr)
    hb = tr // HALO

    def main(part):
        return pl.BlockSpec((tr, w_conv), lambda i: (i, part))

    def prev(part):
        return pl.BlockSpec((HALO, w_conv), lambda i: (jnp.maximum(i * hb - 1, 0), part))

    def body(cb_ref, cc_ref, cu_ref, ccp_ref, cup_ref, w_ref, g_ref, o_ref):
        i = pl.program_id(0)
        low = _low_half()
        for j in range(w_conv // LANES):
            cols = slice(j * LANES, (j + 1) * LANES)
            vv_prev = jnp.where(i > 0, ccp_ref[:, cols] * cup_ref[:, cols], 0.0)
            vv_ext = jnp.concatenate([vv_prev, cc_ref[:, cols] * cu_ref[:, cols]], axis=0)
            y, _ = _conv_taps(vv_ext, w_ref.at[:, cols], tr)
            co = cb_ref[:, cols] * y
            r = lax.rsqrt(_half_mean(co * co, low) + EPS)
            o_ref[:, cols] = (co * r * g_ref[:, cols]).astype(BF16)

    return _pcall(
        body, name="conv_fwd", grid=(s // tr,),
        in_specs=[main(0), main(1), main(2), prev(1), prev(2),
                  pl.BlockSpec((CONV_K, w_conv), lambda i: (0, 0)),
                  pl.BlockSpec((1, w_conv), lambda i: (0, 0))],
        out_specs=pl.BlockSpec((tr, w_conv), lambda i: (i, 0)),
        out_shape=jax.ShapeDtypeStruct((s, d_model), BF16),
        compiler_params=_params(("parallel",)),
    )(proj, proj, proj, proj, proj, conv_w, g_conv)


def _conv_bwd(proj, dcat, conv_w, g_conv, dproj, w_conv, tr=512):
    s = proj.shape[0]
    tr = _tile(s, tr)
    hb = tr // HALO
    last = s // HALO - 1
    nt = s // tr

    def main(part):
        return pl.BlockSpec((tr, w_conv), lambda i: (i, part))

    def prev(part):
        return pl.BlockSpec((HALO, w_conv), lambda i: (jnp.maximum(i * hb - 1, 0), part))

    def nxt(part):
        return pl.BlockSpec((HALO, w_conv), lambda i: (jnp.minimum((i + 1) * hb, last), part))

    def body(cb_ref, cc_ref, cu_ref, dc_ref, ccp_ref, cup_ref, cbn_ref, ccn_ref, cun_ref, dcn_ref,
             w_ref, g_ref, dproj_in, dproj_ref, dw_ref, dg_ref):
        del dproj_in
        i = pl.program_id(0)

        @pl.when(i == 0)
        def _():
            dw_ref[...] = jnp.zeros_like(dw_ref)
            dg_ref[...] = jnp.zeros_like(dg_ref)

        low = _low_half()
        n_ext = tr + HALO
        rowid = lax.broadcasted_iota(jnp.int32, (n_ext, 1), 0)
        for j in range(w_conv // LANES):
            cols = slice(j * LANES, (j + 1) * LANES)
            wj = w_ref.at[:, cols]
            cc, cu = cc_ref[:, cols], cu_ref[:, cols]
            vv_prev = jnp.where(i > 0, ccp_ref[:, cols] * cup_ref[:, cols], 0.0)
            vv_ext = jnp.concatenate([vv_prev, cc * cu, ccn_ref[:, cols] * cun_ref[:, cols]], axis=0)
            y_ext, (v0, v1, v2) = _conv_taps(vv_ext, wj, n_ext)
            cb_ext = jnp.concatenate([cb_ref[:, cols], cbn_ref[:, cols]], axis=0)
            dc_ext = jnp.concatenate([dc_ref[:, cols], dcn_ref[:, cols]], axis=0)
            dco, dgn = _head_norm_bwd(cb_ext * y_ext, dc_ext, g_ref[:, cols], low)
            dyc = jnp.where((rowid < tr) | (i < nt - 1), dco * cb_ext, 0.0)
            dvv = (wj[2:3, :] * dyc[:tr] + wj[1:2, :] * pltpu.roll(dyc, n_ext - 1, 0)[:tr]
                   + wj[0:1, :] * pltpu.roll(dyc, n_ext - 2, 0)[:tr])
            dproj_ref[:, cols] = (dco[:tr] * y_ext[:tr]).astype(BF16)
            dproj_ref[:, w_conv + j * LANES:w_conv + (j + 1) * LANES] = (dvv * cu).astype(BF16)
            dproj_ref[:, 2 * w_conv + j * LANES:2 * w_conv + (j + 1) * LANES] = (dvv * cc).astype(BF16)
            dyt = dyc[:tr]
            for tap, shifted in enumerate((v2, v1, v0)):
                dw_ref[tap:tap + 1, cols] += jnp.sum(dyt * shifted[:tr], axis=0, keepdims=True)
            dg_ref[:, cols] += jnp.sum(dgn[:tr], axis=0, keepdims=True)

    n_cols = dproj.shape[1]
    return _pcall(
        body, name="conv_bwd", grid=(nt,),
        in_specs=[main(0), main(1), main(2), main(0),
                  prev(1), prev(2), nxt(0), nxt(1), nxt(2), nxt(0),
                  pl.BlockSpec((CONV_K, w_conv), lambda i: (0, 0)),
                  pl.BlockSpec((1, w_conv), lambda i: (0, 0)),
                  pl.BlockSpec(memory_space=pl.ANY)],
        out_specs=[pl.BlockSpec((tr, 3 * w_conv), lambda i: (i, 0)),
                   pl.BlockSpec((CONV_K, w_conv), lambda i: (0, 0)),
                   pl.BlockSpec((1, w_conv), lambda i: (0, 0))],
        out_shape=[jax.ShapeDtypeStruct((s, n_cols), BF16),
                   jax.ShapeDtypeStruct((CONV_K, w_conv), F32),
                   jax.ShapeDtypeStruct((1, w_conv), F32)],
        input_output_aliases={12: 0},
        compiler_params=_params(("arbitrary",)),
    )(proj, proj, proj, dcat, proj, proj, proj, proj, proj, dcat, conv_w, g_conv, dproj)


STRIP = 16

ALL_CHAINS = (0, 1, 2, 3)
UPPER_CHAINS = (2, 3)


def _chains(low):
    return [(2 * half + h, half, msk) for half in range(2)
            for h, msk in enumerate((low, jnp.logical_not(low)))]


def _suffix_operator(t):
    r = lax.broadcasted_iota(jnp.int32, (2 * t, t), 0)
    c = lax.broadcasted_iota(jnp.int32, (2 * t, t), 1)
    return jnp.where((r > c) & ((r < t) | (r - t > c)), 1.0, 0.0).astype(BF16)


def _strips(t):
    return [(i, slice(i * STRIP, (i + 1) * STRIP)) for i in range(t // STRIP)]


def _strip_mask(i, t):
    r = lax.broadcasted_iota(jnp.int32, (STRIP, t), 0) + i * STRIP
    c = lax.broadcasted_iota(jnp.int32, (STRIP, t), 1)
    return r > c


def _store_split(ref, rows, val, t):
    hi = val.astype(BF16)
    ref[rows, 0:t] = hi
    ref[rows, t:2 * t] = (val - hi.astype(F32)).astype(BF16)


def _sb_scores(z_s, split_s, zl_s, tot_s, keep_s, t, diag):
    for i, rows in _strips(t):
        z = z_s[rows, :]
        log_beta = jnp.minimum(z, 0.0) - jnp.log(1.0 + jnp.exp(-jnp.abs(z)))
        log_keep = log_beta - z
        if diag:
            log_keep = jnp.where(_strip_mask(i, t), log_keep, 0.0)
        _store_split(split_s, rows, log_keep, t)
        zl_s[rows, :] = log_beta
        tot_s[rows, :] = _row_sum(log_keep)
        if keep_s is not None:
            keep_s[rows, :] = jnp.exp(log_keep)


def _row_sum(v):
    return jnp.broadcast_to(jnp.sum(v, axis=-1, keepdims=True), (v.shape[0], LANES))


def _wide(r, t):
    return jnp.concatenate([r] * (t // LANES), axis=1)


def _sb_weights(zl_s, suf_s, run_s, tot_s, a_s, t, diag, da_s=None, glog_s=None, gsplit_s=None, gtot_s=None):
    for i, rows in _strips(t):
        run = run_s[rows, :]
        a = jnp.exp(zl_s[rows, :] + suf_s[rows, :] + _wide(run, t))
        if diag:
            a = jnp.where(_strip_mask(i, t), a, 0.0)
        ab = a.astype(BF16)
        a_s[rows, :] = ab
        run_s[rows, :] = run + tot_s[rows, :]
        if da_s is not None:
            glog = ab.astype(F32) * da_s[rows, :]
            glog_s[rows, :] = glog
            _store_split(gsplit_s, rows, glog, t)
            gtot_s[rows, :] = _row_sum(glog)


def _sb_dscores(glog_s, cum_s, rest_s, gtot_s, keep_s, dz_s, t, diag):
    for i, rows in _strips(t):
        glog = glog_s[rows, :]
        rest = rest_s[rows, :]
        from_here = _wide(rest, t) - cum_s[rows, :]
        before = from_here - glog
        dz = from_here * keep_s[rows, :] - before
        if diag:
            dz = jnp.where(_strip_mask(i, t), dz, 0.0)
        dz_s[rows, :] = dz.astype(BF16)
        rest_s[rows, :] = rest - gtot_s[rows, :]


def _attn_fwd(proj, g_attn, cat, w_conv, staged, t=ATTN_BLOCK):
    s = proj.shape[0]
    w_attn = g_attn.shape[1]
    nh = w_attn // LANES
    t = _tile(s, t)
    tq = 2 * t
    nq = s // tq
    q0 = 3 * w_conv // LANES
    scale = HEAD_DIM ** -0.5
    nw = len(staged)

    def body(q_ref, k_ref, v_ref, g_ref, cat_in, *rest):
        staged_refs, rest = rest[:nw], rest[nw:]
        o_ref, cat_ref = rest[:2]
        gathered_refs, rest = rest[2:2 + nw], rest[2 + nw:]
        kb, vb, tri_s, qm_s, z_s, split_s, zl_s, suf_s, a_s, run_s, tot_s, acc_s = rest[:12]
        gather_sems = rest[12:]
        del cat_in
        qi = pl.program_id(1)

        @pl.when((pl.program_id(0) == 0) & (qi == 0))
        def _():
            for cp in _gather_copies(staged_refs, gathered_refs, *gather_sems):
                cp.start()

        @pl.when(qi == 0)
        def _():
            kb[...] = k_ref[...].astype(BF16)
            vb[...] = v_ref[...].astype(BF16)
            tri_s[...] = _suffix_operator(t)

        low = _low_half()
        for c, half, msk in _chains(low):
            qm_s[c] = jnp.where(msk, q_ref[half * t:(half + 1) * t, :] * scale, 0.0).astype(BF16)
            run_s[c] = jnp.zeros((t, LANES), F32)
            acc_s[c] = jnp.zeros((t, LANES), F32)

        def key_rows(kblk):
            return pl.ds(pl.multiple_of(kblk * t, t), t)

        def scores_matmul(kblk, chains):
            ks = kb[key_rows(kblk), :]
            for c in chains:
                z_s[c] = lax.dot_general(qm_s[c], ks, _NT, preferred_element_type=F32)

        def front(modes, nxt, prev=None):
            for c, diag in modes:
                _sb_scores(z_s.at[c], split_s.at[c], zl_s.at[c], tot_s.at[c], None, t, diag)
                suf_s[c] = jnp.dot(split_s[c], tri_s[...], preferred_element_type=F32)
            if prev is not None:
                tail(*prev)
            scores_matmul(nxt, ALL_CHAINS)
            for c, diag in modes:
                _sb_weights(zl_s.at[c], suf_s.at[c], run_s.at[c], tot_s.at[c], a_s.at[c], t, diag)

        def tail(kblk, chains):
            vs = vb[key_rows(kblk), :]
            for c in chains:
                acc_s[c] += jnp.dot(a_s[c], vs, preferred_element_type=F32)

        top = 2 * qi + 1
        scores_matmul(top, UPPER_CHAINS)
        front([(c, True) for c in UPPER_CHAINS], top - 1)
        front([(c, c not in UPPER_CHAINS) for c in ALL_CHAINS], jnp.maximum(top - 2, 0),
              prev=(top, UPPER_CHAINS))

        def loop(it, carry):
            cur = top - 2 - it
            front([(c, False) for c in ALL_CHAINS], jnp.maximum(cur - 1, 0), prev=(cur + 1, ALL_CHAINS))
            return carry

        lax.fori_loop(0, top - 1, loop, 0)
        tail(0, ALL_CHAINS)
        for half in range(2):
            rows = slice(half * t, (half + 1) * t)
            o = jnp.where(low, acc_s[2 * half], acc_s[2 * half + 1])
            o_ref[rows, :] = o
            r = lax.rsqrt(_half_mean(o * o, low) + EPS)
            cat_ref[rows, :] = (o * r * g_ref[...]).astype(BF16)

        @pl.when((pl.program_id(0) == nh - 1) & (qi == nq - 1))
        def _():
            for cp in _gather_copies(staged_refs, gathered_refs, *gather_sems):
                cp.wait()

    whole = lambda col0: pl.BlockSpec((s, LANES), lambda h, i: (0, col0 + h))
    n_ch = len(ALL_CHAINS)
    res = _pcall(
        body, name="attn_fwd", grid=(nh, nq),
        in_specs=[pl.BlockSpec((tq, LANES), lambda h, i: (i, q0 + h)),
                  whole(q0 + nh), whole(q0 + 2 * nh),
                  pl.BlockSpec((1, LANES), lambda h, i: (0, h)),
                  pl.BlockSpec(memory_space=pl.ANY)] + [pl.BlockSpec(memory_space=pl.ANY)] * nw,
        out_specs=[pl.BlockSpec((tq, LANES), lambda h, i: (i, h)),
                   pl.BlockSpec((tq, LANES), lambda h, i: (i, w_conv // LANES + h))]
        + [pl.BlockSpec(memory_space=pl.ANY)] * nw,
        out_shape=[jax.ShapeDtypeStruct((s, w_attn), F32),
                   jax.ShapeDtypeStruct(cat.shape, BF16)]
        + [jax.ShapeDtypeStruct((N_DEV, *a.shape), BF16) for a in staged],
        scratch_shapes=[pltpu.VMEM((s, LANES), BF16), pltpu.VMEM((s, LANES), BF16),
                        pltpu.VMEM((2 * t, t), BF16),
                        pltpu.VMEM((n_ch, t, LANES), BF16),
                        pltpu.VMEM((n_ch, t, t), F32),
                        pltpu.VMEM((n_ch, t, 2 * t), BF16),
                        pltpu.VMEM((n_ch, t, t), F32),
                        pltpu.VMEM((n_ch, t, t), F32),
                        pltpu.VMEM((n_ch, t, t), BF16),
                        pltpu.VMEM((n_ch, t, LANES), F32),
                        pltpu.VMEM((n_ch, t, LANES), F32),
                        pltpu.VMEM((n_ch, t, LANES), F32)]
        + _exchange_sems(nw, local=True),
        input_output_aliases={4: 1},
        compiler_params=_params(("arbitrary", "arbitrary")),
    )(proj, proj, proj, g_attn, cat, *staged)
    return res[0], res[1], res[2:]


def _attn_bwd(proj, o, dcat, g_attn, w_conv, partials, t=ATTN_BLOCK):
    s, n_cols = proj.shape
    w_attn = g_attn.shape[1]
    nh = w_attn // LANES
    t = _tile(s, t)
    tq = 2 * t
    nq = s // tq
    q0 = 3 * w_conv // LANES
    scale = HEAD_DIM ** -0.5
    nw = len(partials)

    def body(q_ref, k_ref, v_ref, o_ref, do_ref, g_ref, *rest):
        partial_refs, rest = rest[:nw], rest[nw:]
        dproj_ref, dg_ref = rest[:2]
        received_refs, rest = rest[2:2 + nw], rest[2 + nw:]
        (kb, vb, dkt_acc, dvt_acc, stash, tri_s, qm_s, dom_s, qt_s, dot_s, z_s, da_s, split_s, zl_s,
         keep_s, suf_s, a_s, glog_s, gsplit_s, cum_s, dz_s, run_s, tot_s, rest_s, gtot_s, dq_s) = rest[:26]
        scatter_sems = rest[26:]
        step_i = pl.program_id(1)
        which = pl.program_id(2)
        qi = nq - 1 - step_i
        head_pair = pl.program_id(0)

        @pl.when((head_pair == 0) & (step_i == 0) & (which == 0))
        def _():
            for cp in _scatter_copies(partial_refs, received_refs, *scatter_sems):
                cp.start()

        @pl.when((head_pair == nh - 1) & (step_i == nq - 1) & (which == 2))
        def _():
            for cp in _scatter_copies(partial_refs, received_refs, *scatter_sems):
                cp.wait()

        @pl.when(which == 0)
        def _():
            @pl.when(step_i == 0)
            def _():
                kb[...] = k_ref[...].astype(BF16)
                vb[...] = v_ref[...].astype(BF16)
                tri_s[...] = _suffix_operator(t)
                dkt_acc[...] = jnp.zeros_like(dkt_acc)
                dvt_acc[...] = jnp.zeros_like(dvt_acc)
                dg_ref[...] = jnp.zeros_like(dg_ref)

            low = _low_half()
            gv = g_ref[...]
            for half in range(2):
                rows = slice(half * t, (half + 1) * t)
                q = q_ref[rows, :] * scale
                ov = o_ref[rows, :]
                d_o, dgn = _head_norm_bwd(ov, do_ref[rows, :], gv, low)
                dg_ref[...] += jnp.sum(dgn, axis=0, keepdims=True)
                for h, msk in enumerate((low, jnp.logical_not(low))):
                    c = 2 * half + h
                    qh = jnp.where(msk, q, 0.0)
                    doh = jnp.where(msk, d_o, 0.0)
                    dom = doh.astype(BF16)
                    qm_s[c] = qh.astype(BF16)
                    dom_s[c] = dom
                    qt_s[c] = qh.T.astype(BF16)
                    dot_s[c] = doh.T.astype(BF16)
                    rest_s[c] = _row_sum(dom.astype(F32) * ov)
                    run_s[c] = jnp.zeros((t, LANES), F32)
                    dq_s[c] = jnp.zeros((t, LANES), F32)

            def key_rows(kblk):
                return pl.ds(pl.multiple_of(kblk * t, t), t)

            def scores_matmul(kblk, chains):
                ks = kb[key_rows(kblk), :]
                for c in chains:
                    z_s[c] = lax.dot_general(qm_s[c], ks, _NT, preferred_element_type=F32)

            def da_matmul(kblk, chains):
                vs = vb[key_rows(kblk), :]
                for c in chains:
                    da_s[c] = lax.dot_general(dom_s[c], vs, _NT, preferred_element_type=F32)

            def front(modes, nxt, prev=None):
                if prev is not None:
                    tail(*prev)
                for c, diag in modes:
                    _sb_scores(z_s.at[c], split_s.at[c], zl_s.at[c], tot_s.at[c], keep_s.at[c], t, diag)
                    suf_s[c] = jnp.dot(split_s[c], tri_s[...], preferred_element_type=F32)
                scores_matmul(nxt, ALL_CHAINS)
                for c, diag in modes:
                    _sb_weights(zl_s.at[c], suf_s.at[c], run_s.at[c], tot_s.at[c], a_s.at[c], t, diag,
                                da_s.at[c], glog_s.at[c], gsplit_s.at[c], gtot_s.at[c])
                    cum_s[c] = jnp.dot(gsplit_s[c], tri_s[...], preferred_element_type=F32)
                da_matmul(nxt, ALL_CHAINS)
                for c, diag in modes:
                    _sb_dscores(glog_s.at[c], cum_s.at[c], rest_s.at[c], gtot_s.at[c], keep_s.at[c],
                                dz_s.at[c], t, diag)

            def tail(kblk, chains):
                ks = kb[key_rows(kblk), :]
                dkt = dkt_acc[kblk]
                dvt = dvt_acc[kblk]
                for c in chains:
                    dq_s[c] += jnp.dot(dz_s[c], ks, preferred_element_type=F32)
                    dkt = dkt + jnp.dot(qt_s[c], dz_s[c], preferred_element_type=F32)
                    dvt = dvt + jnp.dot(dot_s[c], a_s[c], preferred_element_type=F32)
                dkt_acc[kblk] = dkt
                dvt_acc[kblk] = dvt

            top = 2 * qi + 1
            scores_matmul(top, UPPER_CHAINS)
            da_matmul(top, UPPER_CHAINS)
            front([(c, True) for c in UPPER_CHAINS], top - 1)
            front([(c, c not in UPPER_CHAINS) for c in ALL_CHAINS], jnp.maximum(top - 2, 0),
                  prev=(top, UPPER_CHAINS))

            def loop(it, carry):
                cur = top - 2 - it
                front([(c, False) for c in ALL_CHAINS], jnp.maximum(cur - 1, 0), prev=(cur + 1, ALL_CHAINS))
                return carry

            lax.fori_loop(0, top - 1, loop, 0)
            tail(0, ALL_CHAINS)
            for half in range(2):
                rows = slice(half * t, (half + 1) * t)
                stash[0, rows, :] = (jnp.where(low, dq_s[2 * half], dq_s[2 * half + 1]) * scale).astype(BF16)
                stash[1, rows, :] = dkt_acc[2 * qi + half].T.astype(BF16)
                stash[2, rows, :] = dvt_acc[2 * qi + half].T.astype(BF16)

        dproj_ref[...] = stash[which]

    whole = lambda col0: pl.BlockSpec((s, LANES), lambda h, i, w: (0, col0 + h))
    blk = lambda col0: pl.BlockSpec((tq, LANES), lambda h, i, w: (nq - 1 - i, col0 + h))
    n_ch = len(ALL_CHAINS)
    res = _pcall(
        body, name="attn_bwd", grid=(nh, nq, 3),
        in_specs=[blk(q0), whole(q0 + nh), whole(q0 + 2 * nh), blk(0), blk(w_conv // LANES),
                  pl.BlockSpec((1, LANES), lambda h, i, w: (0, h))] + [pl.BlockSpec(memory_space=pl.ANY)] * nw,
        out_specs=[pl.BlockSpec((tq, LANES), lambda h, i, w: (nq - 1 - i, q0 + w * nh + h)),
                   pl.BlockSpec((1, LANES), lambda h, i, w: (0, h))] + [pl.BlockSpec(memory_space=pl.ANY)] * nw,
        out_shape=[jax.ShapeDtypeStruct((s, n_cols), BF16), jax.ShapeDtypeStruct((1, w_attn), F32)]
        + [jax.ShapeDtypeStruct((N_PEERS, *a.shape[1:]), BF16) for a in partials],
        scratch_shapes=[pltpu.VMEM((s, LANES), BF16), pltpu.VMEM((s, LANES), BF16),
                        pltpu.VMEM((s // t, LANES, t), F32),
                        pltpu.VMEM((s // t, LANES, t), F32),
                        pltpu.VMEM((3, tq, LANES), BF16),
                        pltpu.VMEM((2 * t, t), BF16),
                        pltpu.VMEM((n_ch, t, LANES), BF16),
                        pltpu.VMEM((n_ch, t, LANES), BF16),
                        pltpu.VMEM((n_ch, LANES, t), BF16),
                        pltpu.VMEM((n_ch, LANES, t), BF16),
                        pltpu.VMEM((n_ch, t, t), F32),
                        pltpu.VMEM((n_ch, t, t), F32),
                        pltpu.VMEM((n_ch, t, 2 * t), BF16),
                        pltpu.VMEM((n_ch, t, t), F32),
                        pltpu.VMEM((n_ch, t, t), F32),
                        pltpu.VMEM((n_ch, t, t), F32),
                        pltpu.VMEM((n_ch, t, t), BF16),
                        pltpu.VMEM((n_ch, t, t), F32),
                        pltpu.VMEM((n_ch, t, 2 * t), BF16),
                        pltpu.VMEM((n_ch, t, t), F32),
                        pltpu.VMEM((n_ch, t, t), BF16),
                        pltpu.VMEM((n_ch, t, LANES), F32),
                        pltpu.VMEM((n_ch, t, LANES), F32),
                        pltpu.VMEM((n_ch, t, LANES), F32),
                        pltpu.VMEM((n_ch, t, LANES), F32),
                        pltpu.VMEM((n_ch, t, LANES), F32)]
        + _exchange_sems(nw),
        compiler_params=_params(("arbitrary", "arbitrary", "arbitrary")),
    )(proj, proj, proj, o, dcat, g_attn, *partials)
    return res[0], res[1], res[2:]


def _place():
    return lax.axis_index("x"), lax.axis_index("y"), lax.axis_index("c")


def _other_chips(x, y):
    return [(1 - x, y), (x, 1 - y), (1 - x, 1 - y)]


def _slot(px, py, pc):
    return 4 * px + 2 * py + pc


def _all_gather(shards, out_dtypes):
    nw = len(shards)

    def body(*refs):
        ins, outs, stage = refs[:nw], refs[nw:2 * nw], refs[2 * nw:3 * nw]
        send_sems, recv_sems, local_sems = refs[3 * nw:]
        x, y, c = _place()
        me, sibling = (x, y, c), (x, y, 1 - c)
        chips = _other_chips(x, y)

        def copy(w, k, block, to, src=None):
            dst = outs[w].at[_slot(*block)]
            return pltpu.make_async_remote_copy(
                src_ref=dst if src is None else src, dst_ref=dst,
                send_sem=send_sems.at[w * 7 + k], recv_sem=recv_sems.at[w * 7 + k],
                device_id=to, device_id_type=MESH)

        started = []
        local = []
        for w in range(nw):
            stage[w][...] = ins[w][...].astype(stage[w].dtype)
            cp = pltpu.make_async_copy(stage[w], outs[w].at[_slot(*me)], local_sems.at[w])
            cp.start()
            local.append(cp)
            started.append(copy(w, 0, me, sibling, src=stage[w]))
            started[-1].start()
            for j, chip in enumerate(chips):
                started.append(copy(w, 1 + j, me, (*chip, c), src=stage[w]))
                started[-1].start()
        for j, chip in enumerate(chips):
            for w in range(nw):
                copy(w, 1 + j, (*chip, c), me).wait_recv()
                started.append(copy(w, 4 + j, (*chip, c), sibling))
                started[-1].start()
        for w in range(nw):
            copy(w, 0, sibling, me).wait_recv()
            for j, chip in enumerate(chips):
                copy(w, 4 + j, (*chip, 1 - c), me).wait_recv()
        for cp in started:
            cp.wait_send()
        for cp in local:
            cp.wait()

    return _pcall(
        body, name="all_gather_weights",
        in_specs=[pl.BlockSpec(memory_space=pltpu.VMEM)] * nw,
        out_specs=[pl.BlockSpec(memory_space=pl.ANY)] * nw,
        out_shape=[jax.ShapeDtypeStruct((N_DEV, *a.shape), d) for a, d in zip(shards, out_dtypes)],
        scratch_shapes=[pltpu.VMEM(a.shape, d) for a, d in zip(shards, out_dtypes)]
        + [pltpu.SemaphoreType.DMA((7 * nw,)), pltpu.SemaphoreType.DMA((7 * nw,)),
           pltpu.SemaphoreType.DMA((nw,))],
        compiler_params=_params(),
    )(*shards)


N_PEERS = N_DEV - 1


def _peer(k):
    x, y, c = _place()
    return (x ^ (k >> 2), y ^ ((k >> 1) & 1), c ^ (k & 1))


def _fanout(src_of, dst_of, nw, send_sems, recv_sems):
    return [pltpu.make_async_remote_copy(
        src_ref=src_of(w, k), dst_ref=dst_of(w, k),
        send_sem=send_sems.at[w * N_PEERS + k - 1], recv_sem=recv_sems.at[w * N_PEERS + k - 1],
        device_id=_peer(k), device_id_type=MESH) for w in range(nw) for k in range(1, N_DEV)]


def _gather_copies(staged, gathered, send_sems, recv_sems, local_sems):
    me = _slot(*_place())
    nw = len(staged)
    remote = _fanout(lambda w, k: staged[w], lambda w, k: gathered[w].at[me], nw, send_sems, recv_sems)
    local = [pltpu.make_async_copy(staged[w], gathered[w].at[me], local_sems.at[w]) for w in range(nw)]
    return remote + local


def _scatter_copies(partials, received, send_sems, recv_sems):
    me = _slot(*_place())
    return _fanout(lambda w, k: partials[w].at[me ^ k], lambda w, k: received[w].at[k - 1],
                   len(partials), send_sems, recv_sems)


def _exchange_sems(nw, local=False):
    sems = [pltpu.SemaphoreType.DMA((N_PEERS * nw,)), pltpu.SemaphoreType.DMA((N_PEERS * nw,))]
    return sems + ([pltpu.SemaphoreType.DMA((nw,))] if local else [])


def _cast_shards(shards):
    def body(*refs):
        for src, dst in zip(refs[:len(shards)], refs[len(shards):]):
            dst[...] = src[...].astype(BF16)

    return _pcall(
        body, name="cast_shards",
        in_specs=[pl.BlockSpec(memory_space=pltpu.VMEM)] * len(shards),
        out_specs=[pl.BlockSpec(memory_space=pltpu.VMEM)] * len(shards),
        out_shape=[jax.ShapeDtypeStruct(a.shape, BF16) for a in shards],
        compiler_params=_params(),
    )(*shards)


def _all_reduce_small(packed):
    r = packed.shape[0]

    def body(x_ref, o_ref, gathered, send_sems, recv_sems):
        x, y, c = _place()
        me = _slot(x, y, c)
        gathered[me] = x_ref[...]
        peers = [(px, py, pc) for px in range(2) for py in range(2) for pc in range(2)]
        started = []
        for k in range(1, N_DEV):
            to = (x ^ (k >> 2), y ^ ((k >> 1) & 1), c ^ (k & 1))
            cp = pltpu.make_async_remote_copy(
                src_ref=x_ref, dst_ref=gathered.at[me],
                send_sem=send_sems.at[k - 1], recv_sem=recv_sems.at[k - 1],
                device_id=to, device_id_type=MESH)
            cp.start()
            started.append(cp)
        del peers
        for cp in started:
            cp.wait()
        total = gathered[0]
        for k in range(1, N_DEV):
            total = total + gathered[k]
        o_ref[...] = total

    return _pcall(
        body, name="all_reduce_small",
        in_specs=[pl.BlockSpec(memory_space=pltpu.VMEM)],
        out_specs=pl.BlockSpec(memory_space=pltpu.VMEM),
        out_shape=jax.ShapeDtypeStruct(packed.shape, F32),
        scratch_shapes=[pltpu.VMEM((N_DEV, r, LANES), F32),
                        pltpu.SemaphoreType.DMA((N_DEV - 1,)), pltpu.SemaphoreType.DMA((N_DEV - 1,))],
        compiler_params=_params(),
    )(packed)


def _adam_math(w, g, m, v):
    m = ADAM_B1 * m + (1.0 - ADAM_B1) * g
    v = ADAM_B2 * v + (1.0 - ADAM_B2) * jnp.square(g)
    m_hat = m / (1.0 - ADAM_B1 ** ADAM_STEP)
    v_hat = v / (1.0 - ADAM_B2 ** ADAM_STEP)
    delta = -ADAM_LR * (m_hat / (jnp.sqrt(v_hat) + ADAM_EPS) + ADAM_WD * w)
    return delta, m, v


def _adam_sharded(name, own, received, w, m, v, place, tr=256):
    r, cdim = w.shape
    tr = _tile(r, tr) if r % LANES == 0 else r

    def body(place_ref, own_ref, rec_ref, w_ref, m_ref, v_ref, g_ref, d_ref, nm_ref, nv_ref):
        del place_ref
        g = own_ref[...]
        for j in range(N_PEERS):
            g = g + rec_ref[j].astype(F32)
        delta, nm, nv = _adam_math(w_ref[...], g, m_ref[...], v_ref[...])
        g_ref[...] = g
        d_ref[...] = delta
        nm_ref[...] = nm
        nv_ref[...] = nv

    blk = pl.BlockSpec((tr, cdim), lambda i, pr: (i, 0))
    grid_spec = pltpu.PrefetchScalarGridSpec(
        num_scalar_prefetch=1, grid=(r // tr,),
        in_specs=[pl.BlockSpec((None, tr, cdim), lambda i, pr: (4 * pr[0] + 2 * pr[1] + pr[2], i, 0)),
                  pl.BlockSpec((N_PEERS, tr, cdim), lambda i, pr: (0, i, 0)), blk, blk, blk],
        out_specs=[blk] * 4)
    return _pcall(body, name=name, grid_spec=grid_spec,
                  out_shape=[jax.ShapeDtypeStruct((r, cdim), F32)] * 4,
                  compiler_params=_params(("parallel",)))(place, own, received, w, m, v)


def _adam_small(w, g, m, v):
    def body(w_ref, g_ref, m_ref, v_ref, d_ref, nm_ref, nv_ref):
        delta, nm, nv = _adam_math(w_ref[...], g_ref[...], m_ref[...], v_ref[...])
        d_ref[...] = delta
        nm_ref[...] = nm
        nv_ref[...] = nv

    return _pcall(body, name="adam_small",
                  in_specs=[pl.BlockSpec(memory_space=pltpu.VMEM)] * 4,
                  out_specs=[pl.BlockSpec(memory_space=pltpu.VMEM)] * 3,
                  out_shape=[jax.ShapeDtypeStruct(w.shape, F32)] * 3,
                  compiler_params=_params())(w, g, m, v)


def _rows(vec):
    return vec.reshape(-1, LANES)


def kernel(x, p, g_mix, w_in, conv_w, g_conv_out, g_attn_out, w_out, g_mlp, w_up, w_down, g_ple, w_ple_gate, w_ple_proj, g_final, loss_target, m_g_mix, m_w_in, m_conv_w, m_g_conv_out, m_g_attn_out, m_w_out, m_g_mlp, m_w_up, m_w_down, m_g_ple, m_w_ple_gate, m_w_ple_proj, m_g_final, v_g_mix, v_w_in, v_conv_w, v_g_conv_out, v_g_attn_out, v_w_out, v_g_mlp, v_w_up, v_w_down, v_g_ple, v_w_ple_gate, v_w_ple_proj, v_g_final):
    s, d = x.shape[1], x.shape[2]
    w_conv = g_conv_out.shape[1]
    w_attn = g_attn_out.shape[1]
    cw = conv_w.shape[2]
    xs, ps, tgt = x[0], p[0, 0], loss_target[0]
    place = jnp.stack([lax.axis_index("x"), lax.axis_index("y"), lax.axis_index("c")]).astype(jnp.int32)
    my_slot = 4 * place[0] + 2 * place[1] + place[2]

    conv_tile = jnp.pad(conv_w[0], ((0, HALO - CONV_K), (0, LANES - cw)))
    big = [w_in[0], w_out[0], w_up[0], w_down[0], w_ple_gate[0], w_ple_proj[0]]
    win_g, conv_g = _all_gather([big[0], conv_tile], [BF16, F32])
    staged = _cast_shards(big[1:])
    conv_full = jnp.transpose(conv_g[:, :CONV_K, :cw], (1, 0, 2)).reshape(CONV_K, w_conv)
    in_shard, up_shard, proj_shard = big[0].shape[1], big[2].shape[1], big[5].shape[1]

    a = _rmsnorm_fwd("norm_mix", xs, g_mix)
    proj, = _mm_nn("in_proj", a, win_g, n_shard=in_shard, tn=in_shard)
    cat = _conv_fwd(proj, conv_full, g_conv_out, w_conv, d)
    o, cat, (wout_g, wup_g, wdown_g, wgate_g, wproj_g) = _attn_fwd(proj, g_attn_out, cat, w_conv, staged)
    wout_f = wout_g.reshape(-1, wout_g.shape[-1])
    wdown_f = wdown_g.reshape(-1, wdown_g.shape[-1])
    wgate_f = wgate_g.reshape(-1, wgate_g.shape[-1])
    h1, = _mm_nn("out_proj", cat, wout_f, epilogue=_ep_residual, extras=(xs,))
    mn = _rmsnorm_fwd("norm_mlp", h1, g_mlp)
    act, = _mm_nn("mlp_up", mn, wup_g, n_shard=up_shard, epilogue=_ep_up, out_dtypes=(BF16,))
    h2, = _mm_nn("mlp_down", act, wdown_f, epilogue=_ep_residual, extras=(h1,))
    n3 = _rmsnorm_fwd("norm_ple", h2, g_ple)
    gl, = _mm_nn("ple_gate", n3, wgate_f)
    pp = _ple_proj(ps, wproj_g)
    loss_part, dh3, dgl, dpp, dg_final = _ple_loss(h2, gl, pp, tgt, g_final.reshape(1, d))
    loss = lax.psum(loss_part[0, 0], ("x", "y", "c"))

    def slots(t2d):
        return t2d.reshape(N_DEV, -1, t2d.shape[-1])

    dw_proj = _d_ple_proj(ps, dpp, proj_shard)
    dw_gate = [slots(t) for t in _mm_tn("d_w_ple_gate", n3, dgl)]
    dn3, = _mm_nt("d_norm_ple", dgl, wgate_f)
    dh2, dh2b, dg_ple = _rmsnorm_bwd("norm_ple_bwd", dn3, h2, g_ple, dh3)
    du, = _mm_nt("d_mlp_act", dh2b, wdown_f, epilogue=_ep_dact, out_dtypes=(BF16,), extras=(act,))
    dw_down = [slots(t) for t in _mm_tn("d_w_down", act, dh2b)]
    dw_up = _mm_tn("d_w_up", mn, du, n_shard=up_shard)
    dmn, = _mm_nt("d_norm_mlp", du, wup_g, k_shard=up_shard)
    dh1, dh1b, dg_mlp = _rmsnorm_bwd("norm_mlp_bwd", dmn, h1, g_mlp, dh2)
    dcat, = _mm_nt("d_cat", dh1b, wout_f)
    dw_out = [slots(t) for t in _mm_tn("d_w_out", cat, dh1b)]
    late = [dw_out, dw_up, dw_down, dw_gate, dw_proj]
    dproj, dg_attn, late_recv = _attn_bwd(proj, o, dcat, g_attn_out, w_conv, [pb for _, pb in late])
    dproj, dconv, dg_conv = _conv_bwd(proj, dcat, conv_full, g_conv_out, dproj, w_conv)
    dw_in = _mm_tn("d_w_in", a, dproj, n_shard=in_shard, tn=in_shard)
    da, in_recv = _mm_nt("d_norm_mix", dproj, win_g, k_shard=in_shard, tk=in_shard, exchange=(dw_in[1],))
    grad_x, _, dg_mix = _rmsnorm_bwd("norm_mix_bwd", da, xs, g_mix, dh1)

    names = ["w_in", "w_out", "w_up", "w_down", "w_ple_gate", "w_ple_proj"]
    owns = [dw_in[0]] + [pf for pf, _ in late]
    recvs = [in_recv, *late_recv]
    moments = [(m_w_in, v_w_in), (m_w_out, v_w_out), (m_w_up, v_w_up), (m_w_down, v_w_down),
               (m_w_ple_gate, v_w_ple_gate), (m_w_ple_proj, v_w_ple_proj)]
    big_out = {}
    for n, own, rc, wt, (mm, vv) in zip(names, owns, recvs, big, moments):
        big_out[n] = [t[None] for t in _adam_sharded("adam_" + n, own, rc, wt, mm[0], vv[0], place)]

    small_g = jnp.concatenate(
        [_rows(dg_mix[0]), _rows(dg_conv[0]), _rows(dg_attn[0]), _rows(dg_mlp[0]), _rows(dg_ple[0]),
         _rows(dg_final[0]), _rows(dconv.reshape(-1))], axis=0)
    n_gain_rows = small_g.shape[0] - CONV_K * w_conv // LANES
    pad_rows = (-small_g.shape[0]) % HALO
    small_g = _all_reduce_small(jnp.pad(small_g, ((0, pad_rows), (0, 0))))
    dconv_full = small_g[n_gain_rows:n_gain_rows + CONV_K * w_conv // LANES].reshape(CONV_K, w_conv)
    dconv_mine = lax.dynamic_slice(dconv_full, (0, my_slot * cw), (CONV_K, cw))

    def pack(vecs, conv_part):
        rows = [_rows(t.reshape(-1)) for t in vecs]
        rows.append(jnp.pad(conv_part, ((0, HALO - CONV_K), (0, LANES - cw))))
        return jnp.concatenate(rows, axis=0)

    gains = [g_mix, g_conv_out, g_attn_out, g_mlp, g_ple, g_final]
    gains_m = [m_g_mix, m_g_conv_out, m_g_attn_out, m_g_mlp, m_g_ple, m_g_final]
    gains_v = [v_g_mix, v_g_conv_out, v_g_attn_out, v_g_mlp, v_g_ple, v_g_final]
    gpack = jnp.concatenate([small_g[:n_gain_rows], jnp.pad(dconv_mine, ((0, HALO - CONV_K), (0, LANES - cw)))], axis=0)
    sd, sm, sv = _adam_small(pack(gains, conv_w[0]), gpack, pack(gains_m, m_conv_w[0]), pack(gains_v, v_conv_w[0]))

    def unpack(packed):
        out, r0 = [], 0
        for t in gains:
            nr = t.size // LANES
            out.append(packed[r0:r0 + nr].reshape(t.shape))
            r0 += nr
        out.append(packed[r0:r0 + CONV_K, :cw][None])
        return out

    sg_l, sd_l, sm_l, sv_l = unpack(gpack), unpack(sd), unpack(sm), unpack(sv)
    small_names = ["g_mix", "g_conv_out", "g_attn_out", "g_mlp", "g_ple", "g_final", "conv_w"]
    small_out = {n: [sg_l[i], sd_l[i], sm_l[i], sv_l[i]] for i, n in enumerate(small_names)}

    order = ["g_mix", "w_in", "conv_w", "g_conv_out", "g_attn_out", "w_out", "g_mlp", "w_up", "w_down",
             "g_ple", "w_ple_gate", "w_ple_proj", "g_final"]
    table = {**big_out, **small_out}
    outs = [loss, grad_x[None]]
    for kind in range(4):
        outs.extend(table[n][kind] for n in order)
    return tuple(outs)
```

```python
import functools

import jax
import jax.numpy as jnp
from jax import lax
from jax.experimental import pallas as pl
from jax.experimental.pallas import tpu as pltpu

F32 = jnp.float32
BF16 = jnp.bfloat16
EPS = 1e-6
HEAD_DIM = 64
LANES = 128
CONV_K = 3
ATTN_BLOCK = 256
HALO = 8
N_DEV = 8
MESH = pl.DeviceIdType.MESH
VMEM_LIMIT = 56 * 1024 * 1024

ADAM_LR = 0.001
ADAM_B1 = 0.9
ADAM_B2 = 0.999
ADAM_EPS = 1e-08
ADAM_WD = 0.01
ADAM_STEP = 10


def _pcall(body, **kw):
    return pl.pallas_call(body, **kw)


def _params(sem=None, **kw):
    return pltpu.CompilerParams(dimension_semantics=sem, vmem_limit_bytes=VMEM_LIMIT, **kw)


def _tile(dim, pref):
    t = min(dim, pref)
    while dim % t:
        t -= LANES
    assert t > 0, (dim, pref)
    return t


_NN = (((1,), (0,)), ((), ()))
_NT = (((1,), (1,)), ((), ()))
_TN = (((0,), (0,)), ((), ()))


def _ep_store(acc, outs):
    outs[0][...] = acc.astype(outs[0].dtype)


def _ep_both(acc, outs):
    outs[0][...] = acc
    outs[1][...] = acc.astype(BF16)


def _ep_residual(acc, res, outs):
    outs[0][...] = acc + res[...]


def _ep_up(acc, outs):
    outs[0][...] = jnp.square(jnp.maximum(acc, 0.0)).astype(BF16)


def _ep_dact(acc, act, outs):
    outs[0][...] = (acc * (2.0 * jnp.sqrt(act[...].astype(F32)))).astype(BF16)


def _matmul(name, a, b, *, dims, grid, a_spec, b_spec, acc_shape, out_shapes, out_specs,
            epilogue=_ep_store, extras=(), extra_specs=(), exchange=()):
    nk = grid[2]
    n_ex, n_out, n_xc = len(extras), len(out_shapes), len(exchange)
    last = tuple(g - 1 for g in grid)

    def product(a_ref, b_ref):
        return lax.dot_general(a_ref[...].astype(BF16), b_ref[...].astype(BF16), dims,
                               preferred_element_type=F32)

    def body(a_ref, b_ref, *rest):
        ex, rest = rest[:n_ex], rest[n_ex:]
        partials, rest = rest[:n_xc], rest[n_xc:]
        outs, rest = rest[:n_out], rest[n_out:]
        received, rest = rest[:n_xc], rest[n_xc:]
        ids = [pl.program_id(axis) for axis in range(3)]
        if n_xc:
            @pl.when((ids[0] == 0) & (ids[1] == 0) & (ids[2] == 0))
            def _():
                for cp in _scatter_copies(partials, received, *rest[-2:]):
                    cp.start()

        if nk == 1:
            epilogue(product(a_ref, b_ref), *ex, outs)
        else:
            acc = rest[0]

            @pl.when(ids[2] == 0)
            def _():
                acc[...] = product(a_ref, b_ref)

            @pl.when(ids[2] > 0)
            def _():
                acc[...] += product(a_ref, b_ref)

            @pl.when(ids[2] == nk - 1)
            def _():
                epilogue(acc[...], *ex, outs)

        if n_xc:
            @pl.when((ids[0] == last[0]) & (ids[1] == last[1]) & (ids[2] == last[2]))
            def _():
                for cp in _scatter_copies(partials, received, *rest[-2:]):
                    cp.wait()

    anywhere = [pl.BlockSpec(memory_space=pl.ANY)] * n_xc
    return _pcall(
        body, name=name, grid=grid,
        in_specs=[a_spec, b_spec, *extra_specs, *anywhere],
        out_specs=[*out_specs, *anywhere],
        out_shape=[*out_shapes, *(jax.ShapeDtypeStruct((N_PEERS, *p.shape[1:]), BF16) for p in exchange)],
        scratch_shapes=([] if nk == 1 else [pltpu.VMEM(acc_shape, F32)]) + (_exchange_sems(n_xc) if n_xc else []),
        compiler_params=_params(("arbitrary",) * 3 if n_xc else ("parallel", "parallel", "arbitrary")),
    )(a, b, *extras, *exchange)


def _mm_nn(name, a, w, *, n_shard=None, epilogue=_ep_store, out_dtypes=(F32,), extras=(), tm=1024, tn=1024, tk=1024):
    m, kd = a.shape
    if n_shard is None:
        n = w.shape[1]
        tn = _tile(n, tn)
        tk = _tile(kd, tk)
        b_spec = pl.BlockSpec((tk, tn), lambda i, j, k: (k, j))
    else:
        n = N_DEV * n_shard
        tn = _tile(n_shard, tn)
        tk = _tile(kd, tk)
        per = n_shard // tn
        b_spec = pl.BlockSpec((None, tk, tn), lambda i, j, k: (j // per, k, j % per))
    tm = _tile(m, tm)
    o_spec = pl.BlockSpec((tm, tn), lambda i, j, k: (i, j))
    return _matmul(
        name, a, w, dims=_NN, grid=(m // tm, n // tn, kd // tk),
        a_spec=pl.BlockSpec((tm, tk), lambda i, j, k: (i, k)), b_spec=b_spec,
        acc_shape=(tm, tn),
        out_shapes=[jax.ShapeDtypeStruct((m, n), d) for d in out_dtypes],
        out_specs=[o_spec] * len(out_dtypes),
        epilogue=epilogue, extras=extras, extra_specs=[o_spec] * len(extras))


def _mm_nt(name, a, w, *, k_shard=None, epilogue=_ep_store, out_dtypes=(F32,), extras=(), exchange=(),
           tm=1024, tn=1024, tk=1024):
    m, kd = a.shape
    if k_shard is None:
        n = w.shape[0]
        tn = _tile(n, tn)
        tk = _tile(kd, tk)
        b_spec = pl.BlockSpec((tn, tk), lambda i, j, k: (j, k))
    else:
        n = w.shape[1]
        tn = _tile(n, tn)
        tk = _tile(k_shard, tk)
        per = k_shard // tk
        b_spec = pl.BlockSpec((None, tn, tk), lambda i, j, k: (k // per, j, k % per))
    tm = _tile(m, tm)
    o_spec = pl.BlockSpec((tm, tn), lambda i, j, k: (i, j))
    return _matmul(
        name, a, w, dims=_NT, grid=(m // tm, n // tn, kd // tk),
        a_spec=pl.BlockSpec((tm, tk), lambda i, j, k: (i, k)), b_spec=b_spec,
        acc_shape=(tm, tn),
        out_shapes=[jax.ShapeDtypeStruct((m, n), d) for d in out_dtypes],
        out_specs=[o_spec] * len(out_dtypes),
        epilogue=epilogue, extras=extras, extra_specs=[o_spec] * len(extras), exchange=exchange)


def _mm_tn(name, a, b, *, n_shard=None, tm=1024, tn=1024, tk=1024):
    t, m = a.shape
    n = b.shape[1]
    tm = _tile(m, tm)
    tk = _tile(t, tk)
    if n_shard is None:
        tn = _tile(n, tn)
        o_spec = pl.BlockSpec((tm, tn), lambda i, j, k: (i, j))
        shape = (m, n)
    else:
        tn = _tile(n_shard, tn)
        per = n_shard // tn
        o_spec = pl.BlockSpec((None, tm, tn), lambda i, j, k: (j // per, i, j % per))
        shape = (N_DEV, m, n_shard)
    return _matmul(
        name, a, b, dims=_TN, grid=(m // tm, n // tn, t // tk),
        a_spec=pl.BlockSpec((tk, tm), lambda i, j, k: (k, i)),
        b_spec=pl.BlockSpec((tk, tn), lambda i, j, k: (k, j)),
        acc_shape=(tm, tn), epilogue=_ep_both,
        out_shapes=[jax.ShapeDtypeStruct(shape, F32), jax.ShapeDtypeStruct(shape, BF16)],
        out_specs=[o_spec, o_spec])


def _ple_proj(p, w_g, tm=1024):
    s, kd = p.shape
    ns = w_g.shape[2]
    tm = _tile(s, tm)

    def body(p_ref, w_ref, o_ref):
        pv = p_ref[...].astype(BF16)
        for j in range(N_DEV):
            o_ref[:, j * ns:(j + 1) * ns] = jnp.dot(pv, w_ref[j], preferred_element_type=F32)

    return _pcall(body, name="ple_proj", grid=(s // tm,),
                  in_specs=[pl.BlockSpec((tm, kd), lambda i: (i, 0)),
                            pl.BlockSpec((N_DEV, kd, ns), lambda i: (0, 0, 0))],
                  out_specs=pl.BlockSpec((tm, N_DEV * ns), lambda i: (i, 0)),
                  out_shape=jax.ShapeDtypeStruct((s, N_DEV * ns), F32),
                  compiler_params=_params(("parallel",)))(p, w_g)


def _d_ple_proj(p, dpp, ns, tk=1024):
    s, kd = p.shape
    tk = _tile(s, tk)
    nk = s // tk

    def body(p_ref, d_ref, of_ref, ob_ref, acc):
        k = pl.program_id(0)

        @pl.when(k == 0)
        def _():
            acc[...] = jnp.zeros_like(acc)

        pv = p_ref[...].astype(BF16)
        for j in range(N_DEV):
            acc[j] += lax.dot_general(pv, d_ref[:, j * ns:(j + 1) * ns], _TN, preferred_element_type=F32)

        @pl.when(k == nk - 1)
        def _():
            of_ref[...] = acc[...]
            ob_ref[...] = acc[...].astype(BF16)

    whole = pl.BlockSpec((N_DEV, kd, ns), lambda k: (0, 0, 0))
    return _pcall(body, name="d_w_ple_proj", grid=(nk,),
                  in_specs=[pl.BlockSpec((tk, kd), lambda k: (k, 0)),
                            pl.BlockSpec((tk, N_DEV * ns), lambda k: (k, 0))],
                  out_specs=[whole, whole],
                  out_shape=[jax.ShapeDtypeStruct((N_DEV, kd, ns), F32), jax.ShapeDtypeStruct((N_DEV, kd, ns), BF16)],
                  scratch_shapes=[pltpu.VMEM((N_DEV, kd, ns), F32)],
                  compiler_params=_params(("arbitrary",)))(p, dpp)


def _row_spec(tr, d):
    return pl.BlockSpec((tr, d), lambda i: (i, 0))


def _vec_spec(d):
    return pl.BlockSpec((1, d), lambda i: (0, 0))


def _rmsnorm_fwd(name, x, g, tr=512):
    s, d = x.shape
    tr = _tile(s, tr)

    def body(x_ref, g_ref, o_ref):
        xv = x_ref[...]
        r = lax.rsqrt(jnp.mean(xv * xv, axis=-1, keepdims=True) + EPS)
        o_ref[...] = (xv * r * g_ref[...]).astype(BF16)

    return _pcall(body, name=name, grid=(s // tr,),
                  in_specs=[_row_spec(tr, d), _vec_spec(d)], out_specs=_row_spec(tr, d),
                  out_shape=jax.ShapeDtypeStruct((s, d), BF16),
                  compiler_params=_params(("parallel",)))(x, g)


def _rmsnorm_bwd(name, dn, h, g, dres, tr=512):
    s, d = h.shape
    tr = _tile(s, tr)

    def body(dn_ref, h_ref, g_ref, dres_ref, dh_ref, dhb_ref, dg_ref):
        @pl.when(pl.program_id(0) == 0)
        def _():
            dg_ref[...] = jnp.zeros_like(dg_ref)

        hv = h_ref[...]
        dnv = dn_ref[...]
        r = lax.rsqrt(jnp.mean(hv * hv, axis=-1, keepdims=True) + EPS)
        hn = hv * r
        dg_ref[...] += jnp.sum(dnv * hn, axis=0, keepdims=True)
        dhn = dnv * g_ref[...]
        dh = dres_ref[...] + r * (dhn - hn * jnp.mean(dhn * hn, axis=-1, keepdims=True))
        dh_ref[...] = dh
        dhb_ref[...] = dh.astype(BF16)

    return _pcall(body, name=name, grid=(s // tr,),
                  in_specs=[_row_spec(tr, d), _row_spec(tr, d), _vec_spec(d), _row_spec(tr, d)],
                  out_specs=[_row_spec(tr, d), _row_spec(tr, d), _vec_spec(d)],
                  out_shape=[jax.ShapeDtypeStruct((s, d), F32), jax.ShapeDtypeStruct((s, d), BF16),
                             jax.ShapeDtypeStruct((1, d), F32)],
                  compiler_params=_params(("arbitrary",)))(dn, h, g, dres)


def _ple_loss(h2, gl, pp, tgt, g_final, tr=512):
    s, d = h2.shape
    tr = _tile(s, tr)

    def body(h2_ref, gl_ref, pp_ref, t_ref, g_ref, loss_ref, dh3_ref, dgl_ref, dpp_ref, dg_ref):
        @pl.when(pl.program_id(0) == 0)
        def _():
            dg_ref[...] = jnp.zeros_like(dg_ref)
            loss_ref[...] = jnp.zeros_like(loss_ref)

        gate = jax.nn.sigmoid(gl_ref[...])
        ppv = pp_ref[...]
        h3 = h2_ref[...] + gate * ppv
        r = lax.rsqrt(jnp.mean(h3 * h3, axis=-1, keepdims=True) + EPS)
        hn = h3 * r
        gv = g_ref[...]
        diff = hn * gv - t_ref[...]
        row = jnp.mean(diff * diff, axis=-1, keepdims=True)
        loss_ref[...] += 0.5 * jnp.sum(row, axis=0, keepdims=True)
        dy = diff * (1.0 / d)
        dg_ref[...] += jnp.sum(dy * hn, axis=0, keepdims=True)
        dhn = dy * gv
        dh3 = r * (dhn - hn * jnp.mean(dhn * hn, axis=-1, keepdims=True))
        dh3_ref[...] = dh3
        dgl_ref[...] = (dh3 * ppv * gate * (1.0 - gate)).astype(BF16)
        dpp_ref[...] = (dh3 * gate).astype(BF16)

    return _pcall(body, name="ple_loss", grid=(s // tr,),
                  in_specs=[_row_spec(tr, d)] * 4 + [_vec_spec(d)],
                  out_specs=[_vec_spec(LANES), _row_spec(tr, d), _row_spec(tr, d), _row_spec(tr, d), _vec_spec(d)],
                  out_shape=[jax.ShapeDtypeStruct((1, LANES), F32), jax.ShapeDtypeStruct((s, d), F32),
                             jax.ShapeDtypeStruct((s, d), BF16), jax.ShapeDtypeStruct((s, d), BF16),
                             jax.ShapeDtypeStruct((1, d), F32)],
                  compiler_params=_params(("arbitrary",)))(h2, gl, pp, tgt, g_final)


def _low_half():
    return lax.broadcasted_iota(jnp.int32, (1, LANES), 1) < HEAD_DIM


def _half_mean(v, low):
    s_lo = jnp.sum(jnp.where(low, v, 0.0), axis=-1, keepdims=True)
    s_hi = jnp.sum(jnp.where(low, 0.0, v), axis=-1, keepdims=True)
    return jnp.where(low, s_lo, s_hi) * (1.0 / HEAD_DIM)


def _head_norm_bwd(val, dout, g, low):
    r = lax.rsqrt(_half_mean(val * val, low) + EPS)
    vn = val * r
    dvn = dout * g
    return r * (dvn - vn * _half_mean(dvn * vn, low)), dout * vn


def _conv_taps(vv_ext, w_ref, rows):
    v0 = vv_ext[HALO:]
    v1 = pltpu.roll(vv_ext, 1, 0)[HALO:]
    v2 = pltpu.roll(vv_ext, 2, 0)[HALO:]
    del rows
    return w_ref[2:3, :] * v0 + w_ref[1:2, :] * v1 + w_ref[0:1, :] * v2, (v0, v1, v2)


def _conv_fwd(proj, conv_w, g_conv, w_conv, d_model, tr=512):
    s = proj.shape[0]
    tr = _tile(s, tr)
    hb = tr // HALO

    def main(part):
        return pl.BlockSpec((tr, w_conv), lambda i: (i, part))

    def prev(part):
        return pl.BlockSpec((HALO, w_conv), lambda i: (jnp.maximum(i * hb - 1, 0), part))

    def body(cb_ref, cc_ref, cu_ref, ccp_ref, cup_ref, w_ref, g_ref, o_ref):
        i = pl.program_id(0)
        low = _low_half()
        for j in range(w_conv // LANES):
            cols = slice(j * LANES, (j + 1) * LANES)
            vv_prev = jnp.where(i > 0, ccp_ref[:, cols] * cup_ref[:, cols], 0.0)
            vv_ext = jnp.concatenate([vv_prev, cc_ref[:, cols] * cu_ref[:, cols]], axis=0)
            y, _ = _conv_taps(vv_ext, w_ref.at[:, cols], tr)
            co = cb_ref[:, cols] * y
            r = lax.rsqrt(_half_mean(co * co, low) + EPS)
            o_ref[:, cols] = (co * r * g_ref[:, cols]).astype(BF16)

    return _pcall(
        body, name="conv_fwd", grid=(s // tr,),
        in_specs=[main(0), main(1), main(2), prev(1), prev(2),
                  pl.BlockSpec((CONV_K, w_conv), lambda i: (0, 0)),
                  pl.BlockSpec((1, w_conv), lambda i: (0, 0))],
        out_specs=pl.BlockSpec((tr, w_conv), lambda i: (i, 0)),
        out_shape=jax.ShapeDtypeStruct((s, d_model), BF16),
        compiler_params=_params(("parallel",)),
    )(proj, proj, proj, proj, proj, conv_w, g_conv)


def _conv_bwd(proj, dcat, conv_w, g_conv, dproj, w_conv, tr=512):
    s = proj.shape[0]
    tr = _tile(s, tr)
    hb = tr // HALO
    last = s // HALO - 1
    nt = s // tr

    def main(part):
        return pl.BlockSpec((tr, w_conv), lambda i: (i, part))

    def prev(part):
        return pl.BlockSpec((HALO, w_conv), lambda i: (jnp.maximum(i * hb - 1, 0), part))

    def nxt(part):
        return pl.BlockSpec((HALO, w_conv), lambda i: (jnp.minimum((i + 1) * hb, last), part))

    def body(cb_ref, cc_ref, cu_ref, dc_ref, ccp_ref, cup_ref, cbn_ref, ccn_ref, cun_ref, dcn_ref,
             w_ref, g_ref, dproj_in, dproj_ref, dw_ref, dg_ref):
        del dproj_in
        i = pl.program_id(0)

        @pl.when(i == 0)
        def _():
            dw_ref[...] = jnp.zeros_like(dw_ref)
            dg_ref[...] = jnp.zeros_like(dg_ref)

        low = _low_half()
        n_ext = tr + HALO
        rowid = lax.broadcasted_iota(jnp.int32, (n_ext, 1), 0)
        for j in range(w_conv // LANES):
            cols = slice(j * LANES, (j + 1) * LANES)
            wj = w_ref.at[:, cols]
            cc, cu = cc_ref[:, cols], cu_ref[:, cols]
            vv_prev = jnp.where(i > 0, ccp_ref[:, cols] * cup_ref[:, cols], 0.0)
            vv_ext = jnp.concatenate([vv_prev, cc * cu, ccn_ref[:, cols] * cun_ref[:, cols]], axis=0)
            y_ext, (v0, v1, v2) = _conv_taps(vv_ext, wj, n_ext)
            cb_ext = jnp.concatenate([cb_ref[:, cols], cbn_ref[:, cols]], axis=0)
            dc_ext = jnp.concatenate([dc_ref[:, cols], dcn_ref[:, cols]], axis=0)
            dco, dgn = _head_norm_bwd(cb_ext * y_ext, dc_ext, g_ref[:, cols], low)
            dyc = jnp.where((rowid < tr) | (i < nt - 1), dco * cb_ext, 0.0)
            dvv = (wj[2:3, :] * dyc[:tr] + wj[1:2, :] * pltpu.roll(dyc, n_ext - 1, 0)[:tr]
                   + wj[0:1, :] * pltpu.roll(dyc, n_ext - 2, 0)[:tr])
            dproj_ref[:, cols] = (dco[:tr] * y_ext[:tr]).astype(BF16)
            dproj_ref[:, w_conv + j * LANES:w_conv + (j + 1) * LANES] = (dvv * cu).astype(BF16)
            dproj_ref[:, 2 * w_conv + j * LANES:2 * w_conv + (j + 1) * LANES] = (dvv * cc).astype(BF16)
            dyt = dyc[:tr]
            for tap, shifted in enumerate((v2, v1, v0)):
                dw_ref[tap:tap + 1, cols] += jnp.sum(dyt * shifted[:tr], axis=0, keepdims=True)
            dg_ref[:, cols] += jnp.sum(dgn[:tr], axis=0, keepdims=True)

    n_cols = dproj.shape[1]
    return _pcall(
        body, name="conv_bwd", grid=(nt,),
        in_specs=[main(0), main(1), main(2), main(0),
                  prev(1), prev(2), nxt(0), nxt(1), nxt(2), nxt(0),
                  pl.BlockSpec((CONV_K, w_conv), lambda i: (0, 0)),
                  pl.BlockSpec((1, w_conv), lambda i: (0, 0)),
                  pl.BlockSpec(memory_space=pl.ANY)],
        out_specs=[pl.BlockSpec((tr, 3 * w_conv), lambda i: (i, 0)),
                   pl.BlockSpec((CONV_K, w_conv), lambda i: (0, 0)),
                   pl.BlockSpec((1, w_conv), lambda i: (0, 0))],
        out_shape=[jax.ShapeDtypeStruct((s, n_cols), BF16),
                   jax.ShapeDtypeStruct((CONV_K, w_conv), F32),
                   jax.ShapeDtypeStruct((1, w_conv), F32)],
        input_output_aliases={12: 0},
        compiler_params=_params(("arbitrary",)),
    )(proj, proj, proj, dcat, proj, proj, proj, proj, proj, dcat, conv_w, g_conv, dproj)


STRIP = 16

ALL_CHAINS = (0, 1, 2, 3)
UPPER_CHAINS = (2, 3)


RUN_FLOOR = -104.0


def _any_weight_left(run_s):
    return (jnp.max(run_s[...]) > RUN_FLOOR).astype(jnp.int32)


def _chains(low):
    return [(2 * half + h, half, msk) for half in range(2)
            for h, msk in enumerate((low, jnp.logical_not(low)))]


def _suffix_operator(t):
    r = lax.broadcasted_iota(jnp.int32, (2 * t, t), 0)
    c = lax.broadcasted_iota(jnp.int32, (2 * t, t), 1)
    return jnp.where((r > c) & ((r < t) | (r - t > c)), 1.0, 0.0).astype(BF16)


def _strips(t):
    return [(i, slice(i * STRIP, (i + 1) * STRIP)) for i in range(t // STRIP)]


def _strip_mask(i, t):
    r = lax.broadcasted_iota(jnp.int32, (STRIP, t), 0) + i * STRIP
    c = lax.broadcasted_iota(jnp.int32, (STRIP, t), 1)
    return r > c


def _store_split(ref, rows, val, t):
    hi = val.astype(BF16)
    ref[rows, 0:t] = hi
    ref[rows, t:2 * t] = (val - hi.astype(F32)).astype(BF16)


def _sb_scores(z_s, split_s, zl_s, tot_s, keep_s, t, diag):
    for i, rows in _strips(t):
        z = z_s[rows, :]
        log_beta = jnp.minimum(z, 0.0) - jnp.log(1.0 + jnp.exp(-jnp.abs(z)))
        log_keep = log_beta - z
        if diag:
            log_keep = jnp.where(_strip_mask(i, t), log_keep, 0.0)
        _store_split(split_s, rows, log_keep, t)
        zl_s[rows, :] = log_beta
        tot_s[rows, :] = _row_sum(log_keep)
        if keep_s is not None:
            keep_s[rows, :] = jnp.exp(log_keep)


def _row_sum(v):
    return jnp.broadcast_to(jnp.sum(v, axis=-1, keepdims=True), (v.shape[0], LANES))


def _wide(r, t):
    return jnp.concatenate([r] * (t // LANES), axis=1)


def _sb_weights(zl_s, suf_s, run_s, tot_s, a_s, t, diag, da_s=None, glog_s=None, gsplit_s=None, gtot_s=None):
    for i, rows in _strips(t):
        run = run_s[rows, :]
        a = jnp.exp(zl_s[rows, :] + suf_s[rows, :] + _wide(run, t))
        if diag:
            a = jnp.where(_strip_mask(i, t), a, 0.0)
        ab = a.astype(BF16)
        a_s[rows, :] = ab
        run_s[rows, :] = run + tot_s[rows, :]
        if da_s is not None:
            glog = ab.astype(F32) * da_s[rows, :]
            glog_s[rows, :] = glog
            _store_split(gsplit_s, rows, glog, t)
            gtot_s[rows, :] = _row_sum(glog)


def _sb_dscores(glog_s, cum_s, rest_s, gtot_s, keep_s, dz_s, t, diag):
    for i, rows in _strips(t):
        glog = glog_s[rows, :]
        rest = rest_s[rows, :]
        from_here = _wide(rest, t) - cum_s[rows, :]
        before = from_here - glog
        dz = from_here * keep_s[rows, :] - before
        if diag:
            dz = jnp.where(_strip_mask(i, t), dz, 0.0)
        dz_s[rows, :] = dz.astype(BF16)
        rest_s[rows, :] = rest - gtot_s[rows, :]


def _attn_fwd(proj, g_attn, cat, w_conv, staged, t=ATTN_BLOCK):
    s = proj.shape[0]
    w_attn = g_attn.shape[1]
    nh = w_attn // LANES
    t = _tile(s, t)
    tq = 2 * t
    nq = s // tq
    q0 = 3 * w_conv // LANES
    scale = HEAD_DIM ** -0.5
    nw = len(staged)

    def body(q_ref, k_ref, v_ref, g_ref, cat_in, *rest):
        staged_refs, rest = rest[:nw], rest[nw:]
        o_ref, cat_ref = rest[:2]
        gathered_refs, rest = rest[2:2 + nw], rest[2 + nw:]
        kb, vb, tri_s, qm_s, z_s, split_s, zl_s, suf_s, a_s, run_s, tot_s, acc_s = rest[:12]
        gather_sems = rest[12:]
        del cat_in
        qi = pl.program_id(1)

        @pl.when((pl.program_id(0) == 0) & (qi == 0))
        def _():
            for cp in _gather_copies(staged_refs, gathered_refs, *gather_sems):
                cp.start()

        @pl.when(qi == 0)
        def _():
            kb[...] = k_ref[...].astype(BF16)
            vb[...] = v_ref[...].astype(BF16)
            tri_s[...] = _suffix_operator(t)

        low = _low_half()
        for c, half, msk in _chains(low):
            qm_s[c] = jnp.where(msk, q_ref[half * t:(half + 1) * t, :] * scale, 0.0).astype(BF16)
            run_s[c] = jnp.zeros((t, LANES), F32)
            acc_s[c] = jnp.zeros((t, LANES), F32)

        def key_rows(kblk):
            return pl.ds(pl.multiple_of(kblk * t, t), t)

        def scores_matmul(kblk, chains):
            ks = kb[key_rows(kblk), :]
            for c in chains:
                z_s[c] = lax.dot_general(qm_s[c], ks, _NT, preferred_element_type=F32)

        def front(modes, nxt, prev=None):
            for c, diag in modes:
                _sb_scores(z_s.at[c], split_s.at[c], zl_s.at[c], tot_s.at[c], None, t, diag)
                suf_s[c] = jnp.dot(split_s[c], tri_s[...], preferred_element_type=F32)
            if prev is not None:
                tail(*prev)
            scores_matmul(nxt, ALL_CHAINS)
            for c, diag in modes:
                _sb_weights(zl_s.at[c], suf_s.at[c], run_s.at[c], tot_s.at[c], a_s.at[c], t, diag)

        def tail(kblk, chains):
            vs = vb[key_rows(kblk), :]
            for c in chains:
                acc_s[c] += jnp.dot(a_s[c], vs, preferred_element_type=F32)

        top = 2 * qi + 1
        scores_matmul(top, UPPER_CHAINS)
        front([(c, True) for c in UPPER_CHAINS], top - 1)
        front([(c, c not in UPPER_CHAINS) for c in ALL_CHAINS], jnp.maximum(top - 2, 0),
              prev=(top, UPPER_CHAINS))

        def loop(state):
            it = state[0]
            cur = top - 2 - it
            front([(c, False) for c in ALL_CHAINS], jnp.maximum(cur - 1, 0), prev=(cur + 1, ALL_CHAINS))
            return it + 1, _any_weight_left(run_s)

        done, _ = lax.while_loop(lambda state: (state[0] < top - 1) & (state[1] > 0), loop,
                                 (jnp.int32(0), jnp.int32(1)))
        tail(top - 1 - done, ALL_CHAINS)
        for half in range(2):
            rows = slice(half * t, (half + 1) * t)
            o = jnp.where(low, acc_s[2 * half], acc_s[2 * half + 1])
            o_ref[rows, :] = o
            r = lax.rsqrt(_half_mean(o * o, low) + EPS)
            cat_ref[rows, :] = (o * r * g_ref[...]).astype(BF16)

        @pl.when((pl.program_id(0) == nh - 1) & (qi == nq - 1))
        def _():
            for cp in _gather_copies(staged_refs, gathered_refs, *gather_sems):
                cp.wait()

    whole = lambda col0: pl.BlockSpec((s, LANES), lambda h, i: (0, col0 + h))
    n_ch = len(ALL_CHAINS)
    res = _pcall(
        body, name="attn_fwd", grid=(nh, nq),
        in_specs=[pl.BlockSpec((tq, LANES), lambda h, i: (i, q0 + h)),
                  whole(q0 + nh), whole(q0 + 2 * nh),
                  pl.BlockSpec((1, LANES), lambda h, i: (0, h)),
                  pl.BlockSpec(memory_space=pl.ANY)] + [pl.BlockSpec(memory_space=pl.ANY)] * nw,
        out_specs=[pl.BlockSpec((tq, LANES), lambda h, i: (i, h)),
                   pl.BlockSpec((tq, LANES), lambda h, i: (i, w_conv // LANES + h))]
        + [pl.BlockSpec(memory_space=pl.ANY)] * nw,
        out_shape=[jax.ShapeDtypeStruct((s, w_attn), F32),
                   jax.ShapeDtypeStruct(cat.shape, BF16)]
        + [jax.ShapeDtypeStruct((N_DEV, *a.shape), BF16) for a in staged],
        scratch_shapes=[pltpu.VMEM((s, LANES), BF16), pltpu.VMEM((s, LANES), BF16),
                        pltpu.VMEM((2 * t, t), BF16),
                        pltpu.VMEM((n_ch, t, LANES), BF16),
                        pltpu.VMEM((n_ch, t, t), F32),
                        pltpu.VMEM((n_ch, t, 2 * t), BF16),
                        pltpu.VMEM((n_ch, t, t), F32),
                        pltpu.VMEM((n_ch, t, t), F32),
                        pltpu.VMEM((n_ch, t, t), BF16),
                        pltpu.VMEM((n_ch, t, LANES), F32),
                        pltpu.VMEM((n_ch, t, LANES), F32),
                        pltpu.VMEM((n_ch, t, LANES), F32)]
        + _exchange_sems(nw, local=True),
        input_output_aliases={4: 1},
        compiler_params=_params(("arbitrary", "arbitrary")),
    )(proj, proj, proj, g_attn, cat, *staged)
    return res[0], res[1], res[2:]


def _attn_bwd(proj, o, dcat, g_attn, w_conv, partials, t=ATTN_BLOCK):
    s, n_cols = proj.shape
    w_attn = g_attn.shape[1]
    nh = w_attn // LANES
    t = _tile(s, t)
    tq = 2 * t
    nq = s // tq
    q0 = 3 * w_conv // LANES
    scale = HEAD_DIM ** -0.5
    nw = len(partials)

    def body(q_ref, k_ref, v_ref, o_ref, do_ref, g_ref, *rest):
        partial_refs, rest = rest[:nw], rest[nw:]
        dproj_ref, dg_ref = rest[:2]
        received_refs, rest = rest[2:2 + nw], rest[2 + nw:]
        (kb, vb, dkt_acc, dvt_acc, stash, tri_s, qm_s, dom_s, qt_s, dot_s, z_s, da_s, split_s, zl_s,
         keep_s, suf_s, a_s, glog_s, gsplit_s, cum_s, dz_s, run_s, tot_s, rest_s, gtot_s, dq_s) = rest[:26]
        scatter_sems = rest[26:]
        step_i = pl.program_id(1)
        which = pl.program_id(2)
        qi = nq - 1 - step_i
        head_pair = pl.program_id(0)

        @pl.when((head_pair == 0) & (step_i == 0) & (which == 0))
        def _():
            for cp in _scatter_copies(partial_refs, received_refs, *scatter_sems):
                cp.start()

        @pl.when((head_pair == nh - 1) & (step_i == nq - 1) & (which == 2))
        def _():
            for cp in _scatter_copies(partial_refs, received_refs, *scatter_sems):
                cp.wait()

        @pl.when(which == 0)
        def _():
            @pl.when(step_i == 0)
            def _():
                kb[...] = k_ref[...].astype(BF16)
                vb[...] = v_ref[...].astype(BF16)
                tri_s[...] = _suffix_operator(t)
                dkt_acc[...] = jnp.zeros_like(dkt_acc)
                dvt_acc[...] = jnp.zeros_like(dvt_acc)
                dg_ref[...] = jnp.zeros_like(dg_ref)

            low = _low_half()
            gv = g_ref[...]
            for half in range(2):
                rows = slice(half * t, (half + 1) * t)
                q = q_ref[rows, :] * scale
                ov = o_ref[rows, :]
                d_o, dgn = _head_norm_bwd(ov, do_ref[rows, :], gv, low)
                dg_ref[...] += jnp.sum(dgn, axis=0, keepdims=True)
                for h, msk in enumerate((low, jnp.logical_not(low))):
                    c = 2 * half + h
                    qh = jnp.where(msk, q, 0.0)
                    doh = jnp.where(msk, d_o, 0.0)
                    dom = doh.astype(BF16)
                    qm_s[c] = qh.astype(BF16)
                    dom_s[c] = dom
                    qt_s[c] = qh.T.astype(BF16)
                    dot_s[c] = doh.T.astype(BF16)
                    rest_s[c] = _row_sum(dom.astype(F32) * ov)
                    run_s[c] = jnp.zeros((t, LANES), F32)
                    dq_s[c] = jnp.zeros((t, LANES), F32)

            def key_rows(kblk):
                return pl.ds(pl.multiple_of(kblk * t, t), t)

            def scores_matmul(kblk, chains):
                ks = kb[key_rows(kblk), :]
                for c in chains:
                    z_s[c] = lax.dot_general(qm_s[c], ks, _NT, preferred_element_type=F32)

            def da_matmul(kblk, chains):
                vs = vb[key_rows(kblk), :]
                for c in chains:
                    da_s[c] = lax.dot_general(dom_s[c], vs, _NT, preferred_element_type=F32)

            def front(modes, nxt, prev=None):
                if prev is not None:
                    tail(*prev)
                for c, diag in modes:
                    _sb_scores(z_s.at[c], split_s.at[c], zl_s.at[c], tot_s.at[c], keep_s.at[c], t, diag)
                    suf_s[c] = jnp.dot(split_s[c], tri_s[...], preferred_element_type=F32)
                scores_matmul(nxt, ALL_CHAINS)
                for c, diag in modes:
                    _sb_weights(zl_s.at[c], suf_s.at[c], run_s.at[c], tot_s.at[c], a_s.at[c], t, diag,
                                da_s.at[c], glog_s.at[c], gsplit_s.at[c], gtot_s.at[c])
                    cum_s[c] = jnp.dot(gsplit_s[c], tri_s[...], preferred_element_type=F32)
                da_matmul(nxt, ALL_CHAINS)
                for c, diag in modes:
                    _sb_dscores(glog_s.at[c], cum_s.at[c], rest_s.at[c], gtot_s.at[c], keep_s.at[c],
                                dz_s.at[c], t, diag)

            def tail(kblk, chains):
                ks = kb[key_rows(kblk), :]
                dkt = dkt_acc[kblk]
                dvt = dvt_acc[kblk]
                for c in chains:
                    dq_s[c] += jnp.dot(dz_s[c], ks, preferred_element_type=F32)
                    dkt = dkt + jnp.dot(qt_s[c], dz_s[c], preferred_element_type=F32)
                    dvt = dvt + jnp.dot(dot_s[c], a_s[c], preferred_element_type=F32)
                dkt_acc[kblk] = dkt
                dvt_acc[kblk] = dvt

            top = 2 * qi + 1
            scores_matmul(top, UPPER_CHAINS)
            da_matmul(top, UPPER_CHAINS)
            front([(c, True) for c in UPPER_CHAINS], top - 1)
            front([(c, c not in UPPER_CHAINS) for c in ALL_CHAINS], jnp.maximum(top - 2, 0),
                  prev=(top, UPPER_CHAINS))

            def loop(state):
                it = state[0]
                cur = top - 2 - it
                front([(c, False) for c in ALL_CHAINS], jnp.maximum(cur - 1, 0), prev=(cur + 1, ALL_CHAINS))
                return it + 1, _any_weight_left(run_s)

            done, _ = lax.while_loop(lambda state: (state[0] < top - 1) & (state[1] > 0), loop,
                                     (jnp.int32(0), jnp.int32(1)))
            tail(top - 1 - done, ALL_CHAINS)
            for half in range(2):
                rows = slice(half * t, (half + 1) * t)
                stash[0, rows, :] = (jnp.where(low, dq_s[2 * half], dq_s[2 * half + 1]) * scale).astype(BF16)
                stash[1, rows, :] = dkt_acc[2 * qi + half].T.astype(BF16)
                stash[2, rows, :] = dvt_acc[2 * qi + half].T.astype(BF16)

        dproj_ref[...] = stash[which]

    whole = lambda col0: pl.BlockSpec((s, LANES), lambda h, i, w: (0, col0 + h))
    blk = lambda col0: pl.BlockSpec((tq, LANES), lambda h, i, w: (nq - 1 - i, col0 + h))
    n_ch = len(ALL_CHAINS)
    res = _pcall(
        body, name="attn_bwd", grid=(nh, nq, 3),
        in_specs=[blk(q0), whole(q0 + nh), whole(q0 + 2 * nh), blk(0), blk(w_conv // LANES),
                  pl.BlockSpec((1, LANES), lambda h, i, w: (0, h))] + [pl.BlockSpec(memory_space=pl.ANY)] * nw,
        out_specs=[pl.BlockSpec((tq, LANES), lambda h, i, w: (nq - 1 - i, q0 + w * nh + h)),
                   pl.BlockSpec((1, LANES), lambda h, i, w: (0, h))] + [pl.BlockSpec(memory_space=pl.ANY)] * nw,
        out_shape=[jax.ShapeDtypeStruct((s, n_cols), BF16), jax.ShapeDtypeStruct((1, w_attn), F32)]
        + [jax.ShapeDtypeStruct((N_PEERS, *a.shape[1:]), BF16) for a in partials],
        scratch_shapes=[pltpu.VMEM((s, LANES), BF16), pltpu.VMEM((s, LANES), BF16),
                        pltpu.VMEM((s // t, LANES, t), F32),
                        pltpu.VMEM((s // t, LANES, t), F32),
                        pltpu.VMEM((3, tq, LANES), BF16),
                        pltpu.VMEM((2 * t, t), BF16),
                        pltpu.VMEM((n_ch, t, LANES), BF16),
                        pltpu.VMEM((n_ch, t, LANES), BF16),
                        pltpu.VMEM((n_ch, LANES, t), BF16),
                        pltpu.VMEM((n_ch, LANES, t), BF16),
                        pltpu.VMEM((n_ch, t, t), F32),
                        pltpu.VMEM((n_ch, t, t), F32),
                        pltpu.VMEM((n_ch, t, 2 * t), BF16),
                        pltpu.VMEM((n_ch, t, t), F32),
                        pltpu.VMEM((n_ch, t, t), F32),
                        pltpu.VMEM((n_ch, t, t), F32),
                        pltpu.VMEM((n_ch, t, t), BF16),
                        pltpu.VMEM((n_ch, t, t), F32),
                        pltpu.VMEM((n_ch, t, 2 * t), BF16),
                        pltpu.VMEM((n_ch, t, t), F32),
                        pltpu.VMEM((n_ch, t, t), BF16),
                        pltpu.VMEM((n_ch, t, LANES), F32),
                        pltpu.VMEM((n_ch, t, LANES), F32),
                        pltpu.VMEM((n_ch, t, LANES), F32),
                        pltpu.VMEM((n_ch, t, LANES), F32),
                        pltpu.VMEM((n_ch, t, LANES), F32)]
        + _exchange_sems(nw),
        compiler_params=_params(("arbitrary", "arbitrary", "arbitrary")),
    )(proj, proj, proj, o, dcat, g_attn, *partials)
    return res[0], res[1], res[2:]


def _place():
    return lax.axis_index("x"), lax.axis_index("y"), lax.axis_index("c")


def _other_chips(x, y):
    return [(1 - x, y), (x, 1 - y), (1 - x, 1 - y)]


def _slot(px, py, pc):
    return 4 * px + 2 * py + pc


def _all_gather(shards, out_dtypes):
    nw = len(shards)

    def body(*refs):
        ins, outs, stage = refs[:nw], refs[nw:2 * nw], refs[2 * nw:3 * nw]
        send_sems, recv_sems, local_sems = refs[3 * nw:]
        x, y, c = _place()
        me, sibling = (x, y, c), (x, y, 1 - c)
        chips = _other_chips(x, y)

        def copy(w, k, block, to, src=None):
            dst = outs[w].at[_slot(*block)]
            return pltpu.make_async_remote_copy(
                src_ref=dst if src is None else src, dst_ref=dst,
                send_sem=send_sems.at[w * 7 + k], recv_sem=recv_sems.at[w * 7 + k],
                device_id=to, device_id_type=MESH)

        started = []
        local = []
        for w in range(nw):
            stage[w][...] = ins[w][...].astype(stage[w].dtype)
            cp = pltpu.make_async_copy(stage[w], outs[w].at[_slot(*me)], local_sems.at[w])
            cp.start()
            local.append(cp)
            started.append(copy(w, 0, me, sibling, src=stage[w]))
            started[-1].start()
            for j, chip in enumerate(chips):
                started.append(copy(w, 1 + j, me, (*chip, c), src=stage[w]))
                started[-1].start()
        for j, chip in enumerate(chips):
            for w in range(nw):
                copy(w, 1 + j, (*chip, c), me).wait_recv()
                started.append(copy(w, 4 + j, (*chip, c), sibling))
                started[-1].start()
        for w in range(nw):
            copy(w, 0, sibling, me).wait_recv()
            for j, chip in enumerate(chips):
                copy(w, 4 + j, (*chip, 1 - c), me).wait_recv()
        for cp in started:
            cp.wait_send()
        for cp in local:
            cp.wait()

    return _pcall(
        body, name="all_gather_weights",
        in_specs=[pl.BlockSpec(memory_space=pltpu.VMEM)] * nw,
        out_specs=[pl.BlockSpec(memory_space=pl.ANY)] * nw,
        out_shape=[jax.ShapeDtypeStruct((N_DEV, *a.shape), d) for a, d in zip(shards, out_dtypes)],
        scratch_shapes=[pltpu.VMEM(a.shape, d) for a, d in zip(shards, out_dtypes)]
        + [pltpu.SemaphoreType.DMA((7 * nw,)), pltpu.SemaphoreType.DMA((7 * nw,)),
           pltpu.SemaphoreType.DMA((nw,))],
        compiler_params=_params(),
    )(*shards)


N_PEERS = N_DEV - 1


def _peer(k):
    x, y, c = _place()
    return (x ^ (k >> 2), y ^ ((k >> 1) & 1), c ^ (k & 1))


def _fanout(src_of, dst_of, nw, send_sems, recv_sems):
    return [pltpu.make_async_remote_copy(
        src_ref=src_of(w, k), dst_ref=dst_of(w, k),
        send_sem=send_sems.at[w * N_PEERS + k - 1], recv_sem=recv_sems.at[w * N_PEERS + k - 1],
        device_id=_peer(k), device_id_type=MESH) for w in range(nw) for k in range(1, N_DEV)]


def _gather_copies(staged, gathered, send_sems, recv_sems, local_sems):
    me = _slot(*_place())
    nw = len(staged)
    remote = _fanout(lambda w, k: staged[w], lambda w, k: gathered[w].at[me], nw, send_sems, recv_sems)
    local = [pltpu.make_async_copy(staged[w], gathered[w].at[me], local_sems.at[w]) for w in range(nw)]
    return remote + local


def _scatter_copies(partials, received, send_sems, recv_sems):
    me = _slot(*_place())
    return _fanout(lambda w, k: partials[w].at[me ^ k], lambda w, k: received[w].at[k - 1],
                   len(partials), send_sems, recv_sems)


def _exchange_sems(nw, local=False):
    sems = [pltpu.SemaphoreType.DMA((N_PEERS * nw,)), pltpu.SemaphoreType.DMA((N_PEERS * nw,))]
    return sems + ([pltpu.SemaphoreType.DMA((nw,))] if local else [])


def _cast_shards(shards):
    def body(*refs):
        for src, dst in zip(refs[:len(shards)], refs[len(shards):]):
            dst[...] = src[...].astype(BF16)

    return _pcall(
        body, name="cast_shards",
        in_specs=[pl.BlockSpec(memory_space=pltpu.VMEM)] * len(shards),
        out_specs=[pl.BlockSpec(memory_space=pltpu.VMEM)] * len(shards),
        out_shape=[jax.ShapeDtypeStruct(a.shape, BF16) for a in shards],
        compiler_params=_params(),
    )(*shards)


def _all_reduce_small(packed):
    r = packed.shape[0]

    def body(x_ref, o_ref, gathered, send_sems, recv_sems):
        x, y, c = _place()
        me = _slot(x, y, c)
        gathered[me] = x_ref[...]
        peers = [(px, py, pc) for px in range(2) for py in range(2) for pc in range(2)]
        started = []
        for k in range(1, N_DEV):
            to = (x ^ (k >> 2), y ^ ((k >> 1) & 1), c ^ (k & 1))
            cp = pltpu.make_async_remote_copy(
                src_ref=x_ref, dst_ref=gathered.at[me],
                send_sem=send_sems.at[k - 1], recv_sem=recv_sems.at[k - 1],
                device_id=to, device_id_type=MESH)
            cp.start()
            started.append(cp)
        del peers
        for cp in started:
            cp.wait()
        total = gathered[0]
        for k in range(1, N_DEV):
            total = total + gathered[k]
        o_ref[...] = total

    return _pcall(
        body, name="all_reduce_small",
        in_specs=[pl.BlockSpec(memory_space=pltpu.VMEM)],
        out_specs=pl.BlockSpec(memory_space=pltpu.VMEM),
        out_shape=jax.ShapeDtypeStruct(packed.shape, F32),
        scratch_shapes=[pltpu.VMEM((N_DEV, r, LANES), F32),
                        pltpu.SemaphoreType.DMA((N_DEV - 1,)), pltpu.SemaphoreType.DMA((N_DEV - 1,))],
        compiler_params=_params(),
    )(packed)


def _adam_math(w, g, m, v):
    m = ADAM_B1 * m + (1.0 - ADAM_B1) * g
    v = ADAM_B2 * v + (1.0 - ADAM_B2) * jnp.square(g)
    m_hat = m / (1.0 - ADAM_B1 ** ADAM_STEP)
    v_hat = v / (1.0 - ADAM_B2 ** ADAM_STEP)
    delta = -ADAM_LR * (m_hat / (jnp.sqrt(v_hat) + ADAM_EPS) + ADAM_WD * w)
    return delta, m, v


def _adam_sharded(name, own, received, w, m, v, place, tr=256):
    r, cdim = w.shape
    tr = _tile(r, tr) if r % LANES == 0 else r

    def body(place_ref, own_ref, rec_ref, w_ref, m_ref, v_ref, g_ref, d_ref, nm_ref, nv_ref):
        del place_ref
        g = own_ref[...]
        for j in range(N_PEERS):
            g = g + rec_ref[j].astype(F32)
        delta, nm, nv = _adam_math(w_ref[...], g, m_ref[...], v_ref[...])
        g_ref[...] = g
        d_ref[...] = delta
        nm_ref[...] = nm
        nv_ref[...] = nv

    blk = pl.BlockSpec((tr, cdim), lambda i, pr: (i, 0))
    grid_spec = pltpu.PrefetchScalarGridSpec(
        num_scalar_prefetch=1, grid=(r // tr,),
        in_specs=[pl.BlockSpec((None, tr, cdim), lambda i, pr: (4 * pr[0] + 2 * pr[1] + pr[2], i, 0)),
                  pl.BlockSpec((N_PEERS, tr, cdim), lambda i, pr: (0, i, 0)), blk, blk, blk],
        out_specs=[blk] * 4)
    return _pcall(body, name=name, grid_spec=grid_spec,
                  out_shape=[jax.ShapeDtypeStruct((r, cdim), F32)] * 4,
                  compiler_params=_params(("parallel",)))(place, own, received, w, m, v)


def _adam_small(w, g, m, v):
    def body(w_ref, g_ref, m_ref, v_ref, d_ref, nm_ref, nv_ref):
        delta, nm, nv = _adam_math(w_ref[...], g_ref[...], m_ref[...], v_ref[...])
        d_ref[...] = delta
        nm_ref[...] = nm
        nv_ref[...] = nv

    return _pcall(body, name="adam_small",
                  in_specs=[pl.BlockSpec(memory_space=pltpu.VMEM)] * 4,
                  out_specs=[pl.BlockSpec(memory_space=pltpu.VMEM)] * 3,
                  out_shape=[jax.ShapeDtypeStruct(w.shape, F32)] * 3,
                  compiler_params=_params())(w, g, m, v)


def _rows(vec):
    return vec.reshape(-1, LANES)


def kernel(x, p, g_mix, w_in, conv_w, g_conv_out, g_attn_out, w_out, g_mlp, w_up, w_down, g_ple, w_ple_gate, w_ple_proj, g_final, loss_target, m_g_mix, m_w_in, m_conv_w, m_g_conv_out, m_g_attn_out, m_w_out, m_g_mlp, m_w_up, m_w_down, m_g_ple, m_w_ple_gate, m_w_ple_proj, m_g_final, v_g_mix, v_w_in, v_conv_w, v_g_conv_out, v_g_attn_out, v_w_out, v_g_mlp, v_w_up, v_w_down, v_g_ple, v_w_ple_gate, v_w_ple_proj, v_g_final):
    s, d = x.shape[1], x.shape[2]
    w_conv = g_conv_out.shape[1]
    w_attn = g_attn_out.shape[1]
    cw = conv_w.shape[2]
    xs, ps, tgt = x[0], p[0, 0], loss_target[0]
    place = jnp.stack([lax.axis_index("x"), lax.axis_index("y"), lax.axis_index("c")]).astype(jnp.int32)
    my_slot = 4 * place[0] + 2 * place[1] + place[2]

    conv_tile = jnp.pad(conv_w[0], ((0, HALO - CONV_K), (0, LANES - cw)))
    big = [w_in[0], w_out[0], w_up[0], w_down[0], w_ple_gate[0], w_ple_proj[0]]
    win_g, conv_g = _all_gather([big[0], conv_tile], [BF16, F32])
    staged = _cast_shards(big[1:])
    conv_full = jnp.transpose(conv_g[:, :CONV_K, :cw], (1, 0, 2)).reshape(CONV_K, w_conv)
    in_shard, up_shard, proj_shard = big[0].shape[1], big[2].shape[1], big[5].shape[1]

    a = _rmsnorm_fwd("norm_mix", xs, g_mix)
    proj, = _mm_nn("in_proj", a, win_g, n_shard=in_shard, tn=in_shard)
    cat = _conv_fwd(proj, conv_full, g_conv_out, w_conv, d)
    o, cat, (wout_g, wup_g, wdown_g, wgate_g, wproj_g) = _attn_fwd(proj, g_attn_out, cat, w_conv, staged)
    wout_f = wout_g.reshape(-1, wout_g.shape[-1])
    wdown_f = wdown_g.reshape(-1, wdown_g.shape[-1])
    wgate_f = wgate_g.reshape(-1, wgate_g.shape[-1])
    h1, = _mm_nn("out_proj", cat, wout_f, epilogue=_ep_residual, extras=(xs,))
    mn = _rmsnorm_fwd("norm_mlp", h1, g_mlp)
    act, = _mm_nn("mlp_up", mn, wup_g, n_shard=up_shard, epilogue=_ep_up, out_dtypes=(BF16,))
    h2, = _mm_nn("mlp_down", act, wdown_f, epilogue=_ep_residual, extras=(h1,))
    n3 = _rmsnorm_fwd("norm_ple", h2, g_ple)
    gl, = _mm_nn("ple_gate", n3, wgate_f)
    pp = _ple_proj(ps, wproj_g)
    loss_part, dh3, dgl, dpp, dg_final = _ple_loss(h2, gl, pp, tgt, g_final.reshape(1, d))
    loss = lax.psum(loss_part[0, 0], ("x", "y", "c"))

    def slots(t2d):
        return t2d.reshape(N_DEV, -1, t2d.shape[-1])

    dw_proj = _d_ple_proj(ps, dpp, proj_shard)
    dw_gate = [slots(t) for t in _mm_tn("d_w_ple_gate", n3, dgl)]
    dn3, = _mm_nt("d_norm_ple", dgl, wgate_f)
    dh2, dh2b, dg_ple = _rmsnorm_bwd("norm_ple_bwd", dn3, h2, g_ple, dh3)
    du, = _mm_nt("d_mlp_act", dh2b, wdown_f, epilogue=_ep_dact, out_dtypes=(BF16,), extras=(act,))
    dw_down = [slots(t) for t in _mm_tn("d_w_down", act, dh2b)]
    dw_up = _mm_tn("d_w_up", mn, du, n_shard=up_shard)
    dmn, = _mm_nt("d_norm_mlp", du, wup_g, k_shard=up_shard)
    dh1, dh1b, dg_mlp = _rmsnorm_bwd("norm_mlp_bwd", dmn, h1, g_mlp, dh2)
    dcat, = _mm_nt("d_cat", dh1b, wout_f)
    dw_out = [slots(t) for t in _mm_tn("d_w_out", cat, dh1b)]
    late = [dw_out, dw_up, dw_down, dw_gate, dw_proj]
    dproj, dg_attn, late_recv = _attn_bwd(proj, o, dcat, g_attn_out, w_conv, [pb for _, pb in late])
    dproj, dconv, dg_conv = _conv_bwd(proj, dcat, conv_full, g_conv_out, dproj, w_conv)
    dw_in = _mm_tn("d_w_in", a, dproj, n_shard=in_shard, tn=in_shard)
    da, in_recv = _mm_nt("d_norm_mix", dproj, win_g, k_shard=in_shard, tk=in_shard, exchange=(dw_in[1],))
    grad_x, _, dg_mix = _rmsnorm_bwd("norm_mix_bwd", da, xs, g_mix, dh1)

    names = ["w_in", "w_out", "w_up", "w_down", "w_ple_gate", "w_ple_proj"]
    owns = [dw_in[0]] + [pf for pf, _ in late]
    recvs = [in_recv, *late_recv]
    moments = [(m_w_in, v_w_in), (m_w_out, v_w_out), (m_w_up, v_w_up), (m_w_down, v_w_down),
               (m_w_ple_gate, v_w_ple_gate), (m_w_ple_proj, v_w_ple_proj)]
    big_out = {}
    for n, own, rc, wt, (mm, vv) in zip(names, owns, recvs, big, moments):
        big_out[n] = [t[None] for t in _adam_sharded("adam_" + n, own, rc, wt, mm[0], vv[0], place)]

    small_g = jnp.concatenate(
        [_rows(dg_mix[0]), _rows(dg_conv[0]), _rows(dg_attn[0]), _rows(dg_mlp[0]), _rows(dg_ple[0]),
         _rows(dg_final[0]), _rows(dconv.reshape(-1))], axis=0)
    n_gain_rows = small_g.shape[0] - CONV_K * w_conv // LANES
    pad_rows = (-small_g.shape[0]) % HALO
    small_g = _all_reduce_small(jnp.pad(small_g, ((0, pad_rows), (0, 0))))
    dconv_full = small_g[n_gain_rows:n_gain_rows + CONV_K * w_conv // LANES].reshape(CONV_K, w_conv)
    dconv_mine = lax.dynamic_slice(dconv_full, (0, my_slot * cw), (CONV_K, cw))

    def pack(vecs, conv_part):
        rows = [_rows(t.reshape(-1)) for t in vecs]
        rows.append(jnp.pad(conv_part, ((0, HALO - CONV_K), (0, LANES - cw))))
        return jnp.concatenate(rows, axis=0)

    gains = [g_mix, g_conv_out, g_attn_out, g_mlp, g_ple, g_final]
    gains_m = [m_g_mix, m_g_conv_out, m_g_attn_out, m_g_mlp, m_g_ple, m_g_final]
    gains_v = [v_g_mix, v_g_conv_out, v_g_attn_out, v_g_mlp, v_g_ple, v_g_final]
    gpack = jnp.concatenate([small_g[:n_gain_rows], jnp.pad(dconv_mine, ((0, HALO - CONV_K), (0, LANES - cw)))], axis=0)
    sd, sm, sv = _adam_small(pack(gains, conv_w[0]), gpack, pack(gains_m, m_conv_w[0]), pack(gains_v, v_conv_w[0]))

    def unpack(packed):
        out, r0 = [], 0
        for t in gains:
            nr = t.size // LANES
            out.append(packed[r0:r0 + nr].reshape(t.shape))
            r0 += nr
        out.append(packed[r0:r0 + CONV_K, :cw][None])
        return out

    sg_l, sd_l, sm_l, sv_l = unpack(gpack), unpack(sd), unpack(sm), unpack(sv)
    small_names = ["g_mix", "g_conv_out", "g_attn_out", "g_mlp", "g_ple", "g_final", "conv_w"]
    small_out = {n: [sg_l[i], sd_l[i], sm_l[i], sv_l[i]] for i, n in enumerate(small_names)}

    order = ["g_mix", "w_in", "conv_w", "g_conv_out", "g_attn_out", "w_out", "g_mlp", "w_up", "w_down",
             "g_ple", "w_ple_gate", "w_ple_proj", "g_final"]
    table = {**big_out, **small_out}
    outs = [loss, grad_x[None]]
    for kind in range(4):
        outs.extend(table[n][kind] for n in order)
    return tuple(outs)
```

```python
import functools

import jax
import jax.numpy as jnp
from jax import lax
from jax.experimental import pallas as pl
from jax.experimental.pallas import tpu as pltpu

F32 = jnp.float32
BF16 = jnp.bfloat16
EPS = 1e-6
HEAD_DIM = 64
LANES = 128
CONV_K = 3
ATTN_BLOCK = 256
HALO = 8
N_DEV = 8
MESH = pl.DeviceIdType.MESH
VMEM_LIMIT = 56 * 1024 * 1024

ADAM_LR = 0.001
ADAM_B1 = 0.9
ADAM_B2 = 0.999
ADAM_EPS = 1e-08
ADAM_WD = 0.01
ADAM_STEP = 10


def _pcall(body, **kw):
    return pl.pallas_call(body, **kw)


def _params(sem=None, **kw):
    return pltpu.CompilerParams(dimension_semantics=sem, vmem_limit_bytes=VMEM_LIMIT, **kw)


def _tile(dim, pref):
    t = min(dim, pref)
    while dim % t:
        t -= LANES
    assert t > 0, (dim, pref)
    return t


_NN = (((1,), (0,)), ((), ()))
_NT = (((1,), (1,)), ((), ()))
_TN = (((0,), (0,)), ((), ()))


def _ep_store(acc, outs):
    outs[0][...] = acc.astype(outs[0].dtype)


def _ep_both(acc, outs):
    outs[0][...] = acc
    outs[1][...] = acc.astype(BF16)


def _ep_residual(acc, res, outs):
    outs[0][...] = acc + res[...]


def _ep_up(acc, outs):
    outs[0][...] = jnp.square(jnp.maximum(acc, 0.0)).astype(BF16)


def _ep_dact(acc, act, outs):
    outs[0][...] = (acc * (2.0 * jnp.sqrt(act[...].astype(F32)))).astype(BF16)


def _matmul(name, a, b, *, dims, grid, a_spec, b_spec, acc_shape, out_shapes, out_specs,
            epilogue=_ep_store, extras=(), extra_specs=(), carry=("scatter", ())):
    nk = grid[2]
    kind, carried = carry
    n_ex, n_out, n_xc = len(extras), len(out_shapes), len(carried)
    n_sems = len(_carried_sems(kind, n_xc))
    last = tuple(g - 1 for g in grid)

    def product(a_ref, b_ref):
        return lax.dot_general(a_ref[...].astype(BF16), b_ref[...].astype(BF16), dims,
                               preferred_element_type=F32)

    def body(a_ref, b_ref, *rest):
        ex, rest = rest[:n_ex], rest[n_ex:]
        partials, rest = rest[:n_xc], rest[n_xc:]
        outs, rest = rest[:n_out], rest[n_out:]
        received, rest = rest[:n_xc], rest[n_xc:]
        ids = [pl.program_id(axis) for axis in range(3)]
        if n_xc:
            @pl.when((ids[0] == 0) & (ids[1] == 0) & (ids[2] == 0))
            def _():
                for cp in _carried_copies(kind, partials, received, rest[-n_sems:]):
                    cp.start()

        if nk == 1:
            epilogue(product(a_ref, b_ref), *ex, outs)
        else:
            acc = rest[0]

            @pl.when(ids[2] == 0)
            def _():
                acc[...] = product(a_ref, b_ref)

            @pl.when(ids[2] > 0)
            def _():
                acc[...] += product(a_ref, b_ref)

            @pl.when(ids[2] == nk - 1)
            def _():
                epilogue(acc[...], *ex, outs)

        if n_xc:
            @pl.when((ids[0] == last[0]) & (ids[1] == last[1]) & (ids[2] == last[2]))
            def _():
                for cp in _carried_copies(kind, partials, received, rest[-n_sems:]):
                    cp.wait()

    anywhere = [pl.BlockSpec(memory_space=pl.ANY)] * n_xc
    return _pcall(
        body, name=name, grid=grid,
        in_specs=[a_spec, b_spec, *extra_specs, *anywhere],
        out_specs=[*out_specs, *anywhere],
        out_shape=[*out_shapes, *_carried_shapes(kind, carried)],
        scratch_shapes=([] if nk == 1 else [pltpu.VMEM(acc_shape, F32)]) + _carried_sems(kind, n_xc),
        compiler_params=_params(("arbitrary",) * 3 if n_xc else ("parallel", "parallel", "arbitrary")),
    )(a, b, *extras, *carried)


_NO_CARRY = ("scatter", ())


def _mm_nn(name, a, w, *, n_shard=None, epilogue=_ep_store, out_dtypes=(F32,), extras=(), carry=_NO_CARRY,
           tm=1024, tn=1024, tk=1024):
    m, kd = a.shape
    if n_shard is None:
        n = w.shape[1]
        tn = _tile(n, tn)
        tk = _tile(kd, tk)
        b_spec = pl.BlockSpec((tk, tn), lambda i, j, k: (k, j))
    else:
        n = N_DEV * n_shard
        tn = _tile(n_shard, tn)
        tk = _tile(kd, tk)
        per = n_shard // tn
        b_spec = pl.BlockSpec((None, tk, tn), lambda i, j, k: (j // per, k, j % per))
    tm = _tile(m, tm)
    o_spec = pl.BlockSpec((tm, tn), lambda i, j, k: (i, j))
    return _matmul(
        name, a, w, dims=_NN, grid=(m // tm, n // tn, kd // tk),
        a_spec=pl.BlockSpec((tm, tk), lambda i, j, k: (i, k)), b_spec=b_spec,
        acc_shape=(tm, tn),
        out_shapes=[jax.ShapeDtypeStruct((m, n), d) for d in out_dtypes],
        out_specs=[o_spec] * len(out_dtypes),
        epilogue=epilogue, extras=extras, extra_specs=[o_spec] * len(extras), carry=carry)


def _mm_nt(name, a, w, *, k_shard=None, epilogue=_ep_store, out_dtypes=(F32,), extras=(), carry=_NO_CARRY,
           tm=1024, tn=1024, tk=1024):
    m, kd = a.shape
    if k_shard is None:
        n = w.shape[0]
        tn = _tile(n, tn)
        tk = _tile(kd, tk)
        b_spec = pl.BlockSpec((tn, tk), lambda i, j, k: (j, k))
    else:
        n = w.shape[1]
        tn = _tile(n, tn)
        tk = _tile(k_shard, tk)
        per = k_shard // tk
        b_spec = pl.BlockSpec((None, tn, tk), lambda i, j, k: (k // per, j, k % per))
    tm = _tile(m, tm)
    o_spec = pl.BlockSpec((tm, tn), lambda i, j, k: (i, j))
    return _matmul(
        name, a, w, dims=_NT, grid=(m // tm, n // tn, kd // tk),
        a_spec=pl.BlockSpec((tm, tk), lambda i, j, k: (i, k)), b_spec=b_spec,
        acc_shape=(tm, tn),
        out_shapes=[jax.ShapeDtypeStruct((m, n), d) for d in out_dtypes],
        out_specs=[o_spec] * len(out_dtypes),
        epilogue=epilogue, extras=extras, extra_specs=[o_spec] * len(extras), carry=carry)


def _mm_tn(name, a, b, *, n_shard=None, carry=_NO_CARRY, tm=1024, tn=1024, tk=1024):
    t, m = a.shape
    n = b.shape[1]
    tm = _tile(m, tm)
    tk = _tile(t, tk)
    if n_shard is None:
        tn = _tile(n, tn)
        o_spec = pl.BlockSpec((tm, tn), lambda i, j, k: (i, j))
        shape = (m, n)
    else:
        tn = _tile(n_shard, tn)
        per = n_shard // tn
        o_spec = pl.BlockSpec((None, tm, tn), lambda i, j, k: (j // per, i, j % per))
        shape = (N_DEV, m, n_shard)
    return _matmul(
        name, a, b, dims=_TN, grid=(m // tm, n // tn, t // tk),
        a_spec=pl.BlockSpec((tk, tm), lambda i, j, k: (k, i)),
        b_spec=pl.BlockSpec((tk, tn), lambda i, j, k: (k, j)),
        acc_shape=(tm, tn), epilogue=_ep_both, carry=carry,
        out_shapes=[jax.ShapeDtypeStruct(shape, F32), jax.ShapeDtypeStruct(shape, BF16)],
        out_specs=[o_spec, o_spec])


def _ple_proj(p, w_g, tm=1024):
    s, kd = p.shape
    ns = w_g.shape[2]
    tm = _tile(s, tm)

    def body(p_ref, w_ref, o_ref):
        pv = p_ref[...].astype(BF16)
        for j in range(N_DEV):
            o_ref[:, j * ns:(j + 1) * ns] = jnp.dot(pv, w_ref[j], preferred_element_type=F32)

    return _pcall(body, name="ple_proj", grid=(s // tm,),
                  in_specs=[pl.BlockSpec((tm, kd), lambda i: (i, 0)),
                            pl.BlockSpec((N_DEV, kd, ns), lambda i: (0, 0, 0))],
                  out_specs=pl.BlockSpec((tm, N_DEV * ns), lambda i: (i, 0)),
                  out_shape=jax.ShapeDtypeStruct((s, N_DEV * ns), F32),
                  compiler_params=_params(("parallel",)))(p, w_g)


def _d_ple_proj(p, dpp, ns, tk=1024):
    s, kd = p.shape
    tk = _tile(s, tk)
    nk = s // tk

    def body(p_ref, d_ref, of_ref, ob_ref, acc):
        k = pl.program_id(0)

        @pl.when(k == 0)
        def _():
            acc[...] = jnp.zeros_like(acc)

        pv = p_ref[...].astype(BF16)
        for j in range(N_DEV):
            acc[j] += lax.dot_general(pv, d_ref[:, j * ns:(j + 1) * ns], _TN, preferred_element_type=F32)

        @pl.when(k == nk - 1)
        def _():
            of_ref[...] = acc[...]
            ob_ref[...] = acc[...].astype(BF16)

    whole = pl.BlockSpec((N_DEV, kd, ns), lambda k: (0, 0, 0))
    return _pcall(body, name="d_w_ple_proj", grid=(nk,),
                  in_specs=[pl.BlockSpec((tk, kd), lambda k: (k, 0)),
                            pl.BlockSpec((tk, N_DEV * ns), lambda k: (k, 0))],
                  out_specs=[whole, whole],
                  out_shape=[jax.ShapeDtypeStruct((N_DEV, kd, ns), F32), jax.ShapeDtypeStruct((N_DEV, kd, ns), BF16)],
                  scratch_shapes=[pltpu.VMEM((N_DEV, kd, ns), F32)],
                  compiler_params=_params(("arbitrary",)))(p, dpp)


def _row_spec(tr, d):
    return pl.BlockSpec((tr, d), lambda i: (i, 0))


def _vec_spec(d):
    return pl.BlockSpec((1, d), lambda i: (0, 0))


def _rmsnorm_fwd(name, x, g, tr=512):
    s, d = x.shape
    tr = _tile(s, tr)

    def body(x_ref, g_ref, o_ref):
        xv = x_ref[...]
        r = lax.rsqrt(jnp.mean(xv * xv, axis=-1, keepdims=True) + EPS)
        o_ref[...] = (xv * r * g_ref[...]).astype(BF16)

    return _pcall(body, name=name, grid=(s // tr,),
                  in_specs=[_row_spec(tr, d), _vec_spec(d)], out_specs=_row_spec(tr, d),
                  out_shape=jax.ShapeDtypeStruct((s, d), BF16),
                  compiler_params=_params(("parallel",)))(x, g)


def _rmsnorm_bwd(name, dn, h, g, dres, tr=512):
    s, d = h.shape
    tr = _tile(s, tr)

    def body(dn_ref, h_ref, g_ref, dres_ref, dh_ref, dhb_ref, dg_ref):
        @pl.when(pl.program_id(0) == 0)
        def _():
            dg_ref[...] = jnp.zeros_like(dg_ref)

        hv = h_ref[...]
        dnv = dn_ref[...]
        r = lax.rsqrt(jnp.mean(hv * hv, axis=-1, keepdims=True) + EPS)
        hn = hv * r
        dg_ref[...] += jnp.sum(dnv * hn, axis=0, keepdims=True)
        dhn = dnv * g_ref[...]
        dh = dres_ref[...] + r * (dhn - hn * jnp.mean(dhn * hn, axis=-1, keepdims=True))
        dh_ref[...] = dh
        dhb_ref[...] = dh.astype(BF16)

    return _pcall(body, name=name, grid=(s // tr,),
                  in_specs=[_row_spec(tr, d), _row_spec(tr, d), _vec_spec(d), _row_spec(tr, d)],
                  out_specs=[_row_spec(tr, d), _row_spec(tr, d), _vec_spec(d)],
                  out_shape=[jax.ShapeDtypeStruct((s, d), F32), jax.ShapeDtypeStruct((s, d), BF16),
                             jax.ShapeDtypeStruct((1, d), F32)],
                  compiler_params=_params(("arbitrary",)))(dn, h, g, dres)


def _ple_loss(h2, gl, pp, tgt, g_final, tr=512):
    s, d = h2.shape
    tr = _tile(s, tr)

    def body(h2_ref, gl_ref, pp_ref, t_ref, g_ref, loss_ref, dh3_ref, dgl_ref, dpp_ref, dg_ref):
        @pl.when(pl.program_id(0) == 0)
        def _():
            dg_ref[...] = jnp.zeros_like(dg_ref)
            loss_ref[...] = jnp.zeros_like(loss_ref)

        gate = jax.nn.sigmoid(gl_ref[...])
        ppv = pp_ref[...]
        h3 = h2_ref[...] + gate * ppv
        r = lax.rsqrt(jnp.mean(h3 * h3, axis=-1, keepdims=True) + EPS)
        hn = h3 * r
        gv = g_ref[...]
        diff = hn * gv - t_ref[...]
        row = jnp.mean(diff * diff, axis=-1, keepdims=True)
        loss_ref[...] += 0.5 * jnp.sum(row, axis=0, keepdims=True)
        dy = diff * (1.0 / d)
        dg_ref[...] += jnp.sum(dy * hn, axis=0, keepdims=True)
        dhn = dy * gv
        dh3 = r * (dhn - hn * jnp.mean(dhn * hn, axis=-1, keepdims=True))
        dh3_ref[...] = dh3
        dgl_ref[...] = (dh3 * ppv * gate * (1.0 - gate)).astype(BF16)
        dpp_ref[...] = (dh3 * gate).astype(BF16)

    return _pcall(body, name="ple_loss", grid=(s // tr,),
                  in_specs=[_row_spec(tr, d)] * 4 + [_vec_spec(d)],
                  out_specs=[_vec_spec(LANES), _row_spec(tr, d), _row_spec(tr, d), _row_spec(tr, d), _vec_spec(d)],
                  out_shape=[jax.ShapeDtypeStruct((1, LANES), F32), jax.ShapeDtypeStruct((s, d), F32),
                             jax.ShapeDtypeStruct((s, d), BF16), jax.ShapeDtypeStruct((s, d), BF16),
                             jax.ShapeDtypeStruct((1, d), F32)],
                  compiler_params=_params(("arbitrary",)))(h2, gl, pp, tgt, g_final)


def _low_half():
    return lax.broadcasted_iota(jnp.int32, (1, LANES), 1) < HEAD_DIM


def _half_mean(v, low):
    s_lo = jnp.sum(jnp.where(low, v, 0.0), axis=-1, keepdims=True)
    s_hi = jnp.sum(jnp.where(low, 0.0, v), axis=-1, keepdims=True)
    return jnp.where(low, s_lo, s_hi) * (1.0 / HEAD_DIM)


def _head_norm_bwd(val, dout, g, low):
    r = lax.rsqrt(_half_mean(val * val, low) + EPS)
    vn = val * r
    dvn = dout * g
    return r * (dvn - vn * _half_mean(dvn * vn, low)), dout * vn


def _conv_taps(vv_ext, w_ref, rows):
    v0 = vv_ext[HALO:]
    v1 = pltpu.roll(vv_ext, 1, 0)[HALO:]
    v2 = pltpu.roll(vv_ext, 2, 0)[HALO:]
    del rows
    return w_ref[2:3, :] * v0 + w_ref[1:2, :] * v1 + w_ref[0:1, :] * v2, (v0, v1, v2)


def _conv_fwd(proj, conv_w, g_conv, w_conv, d_model, tr=512):
    s = proj.shape[0]
    tr = _tile(s, tr)
    hb = tr // HALO

    def main(part):
        return pl.BlockSpec((tr, w_conv), lambda i: (i, part))

    def prev(part):
        return pl.BlockSpec((HALO, w_conv), lambda i: (jnp.maximum(i * hb - 1, 0), part))

    def body(cb_ref, cc_ref, cu_ref, ccp_ref, cup_ref, w_ref, g_ref, o_ref):
        i = pl.program_id(0)
        low = _low_half()
        for j in range(w_conv // LANES):
            cols = slice(j * LANES, (j + 1) * LANES)
            vv_prev = jnp.where(i > 0, ccp_ref[:, cols] * cup_ref[:, cols], 0.0)
            vv_ext = jnp.concatenate([vv_prev, cc_ref[:, cols] * cu_ref[:, cols]], axis=0)
            y, _ = _conv_taps(vv_ext, w_ref.at[:, cols], tr)
            co = cb_ref[:, cols] * y
            r = lax.rsqrt(_half_mean(co * co, low) + EPS)
            o_ref[:, cols] = (co * r * g_ref[:, cols]).astype(BF16)

    return _pcall(
        body, name="conv_fwd", grid=(s // tr,),
        in_specs=[main(0), main(1), main(2), prev(1), prev(2),
                  pl.BlockSpec((CONV_K, w_conv), lambda i: (0, 0)),
                  pl.BlockSpec((1, w_conv), lambda i: (0, 0))],
        out_specs=pl.BlockSpec((tr, w_conv), lambda i: (i, 0)),
        out_shape=jax.ShapeDtypeStruct((s, d_model), BF16),
        compiler_params=_params(("parallel",)),
    )(proj, proj, proj, proj, proj, conv_w, g_conv)


def _conv_bwd(proj, dcat, conv_w, g_conv, dproj, w_conv, tr=512):
    s = proj.shape[0]
    tr = _tile(s, tr)
    hb = tr // HALO
    last = s // HALO - 1
    nt = s // tr

    def main(part):
        return pl.BlockSpec((tr, w_conv), lambda i: (i, part))

    def prev(part):
        return pl.BlockSpec((HALO, w_conv), lambda i: (jnp.maximum(i * hb - 1, 0), part))

    def nxt(part):
        return pl.BlockSpec((HALO, w_conv), lambda i: (jnp.minimum((i + 1) * hb, last), part))

    def body(cb_ref, cc_ref, cu_ref, dc_ref, ccp_ref, cup_ref, cbn_ref, ccn_ref, cun_ref, dcn_ref,
             w_ref, g_ref, dproj_in, dproj_ref, dw_ref, dg_ref):
        del dproj_in
        i = pl.program_id(0)

        @pl.when(i == 0)
        def _():
            dw_ref[...] = jnp.zeros_like(dw_ref)
            dg_ref[...] = jnp.zeros_like(dg_ref)

        low = _low_half()
        n_ext = tr + HALO
        rowid = lax.broadcasted_iota(jnp.int32, (n_ext, 1), 0)
        for j in range(w_conv // LANES):
            cols = slice(j * LANES, (j + 1) * LANES)
            wj = w_ref.at[:, cols]
            cc, cu = cc_ref[:, cols], cu_ref[:, cols]
            vv_prev = jnp.where(i > 0, ccp_ref[:, cols] * cup_ref[:, cols], 0.0)
            vv_ext = jnp.concatenate([vv_prev, cc * cu, ccn_ref[:, cols] * cun_ref[:, cols]], axis=0)
            y_ext, (v0, v1, v2) = _conv_taps(vv_ext, wj, n_ext)
            cb_ext = jnp.concatenate([cb_ref[:, cols], cbn_ref[:, cols]], axis=0)
            dc_ext = jnp.concatenate([dc_ref[:, cols], dcn_ref[:, cols]], axis=0)
            dco, dgn = _head_norm_bwd(cb_ext * y_ext, dc_ext, g_ref[:, cols], low)
            dyc = jnp.where((rowid < tr) | (i < nt - 1), dco * cb_ext, 0.0)
            dvv = (wj[2:3, :] * dyc[:tr] + wj[1:2, :] * pltpu.roll(dyc, n_ext - 1, 0)[:tr]
                   + wj[0:1, :] * pltpu.roll(dyc, n_ext - 2, 0)[:tr])
            dproj_ref[:, cols] = (dco[:tr] * y_ext[:tr]).astype(BF16)
            dproj_ref[:, w_conv + j * LANES:w_conv + (j + 1) * LANES] = (dvv * cu).astype(BF16)
            dproj_ref[:, 2 * w_conv + j * LANES:2 * w_conv + (j + 1) * LANES] = (dvv * cc).astype(BF16)
            dyt = dyc[:tr]
            for tap, shifted in enumerate((v2, v1, v0)):
                dw_ref[tap:tap + 1, cols] += jnp.sum(dyt * shifted[:tr], axis=0, keepdims=True)
            dg_ref[:, cols] += jnp.sum(dgn[:tr], axis=0, keepdims=True)

    n_cols = dproj.shape[1]
    return _pcall(
        body, name="conv_bwd", grid=(nt,),
        in_specs=[main(0), main(1), main(2), main(0),
                  prev(1), prev(2), nxt(0), nxt(1), nxt(2), nxt(0),
                  pl.BlockSpec((CONV_K, w_conv), lambda i: (0, 0)),
                  pl.BlockSpec((1, w_conv), lambda i: (0, 0)),
                  pl.BlockSpec(memory_space=pl.ANY)],
        out_specs=[pl.BlockSpec((tr, 3 * w_conv), lambda i: (i, 0)),
                   pl.BlockSpec((CONV_K, w_conv), lambda i: (0, 0)),
                   pl.BlockSpec((1, w_conv), lambda i: (0, 0))],
        out_shape=[jax.ShapeDtypeStruct((s, n_cols), BF16),
                   jax.ShapeDtypeStruct((CONV_K, w_conv), F32),
                   jax.ShapeDtypeStruct((1, w_conv), F32)],
        input_output_aliases={12: 0},
        compiler_params=_params(("arbitrary",)),
    )(proj, proj, proj, dcat, proj, proj, proj, proj, proj, dcat, conv_w, g_conv, dproj)


STRIP = 16

ALL_CHAINS = (0, 1, 2, 3)
UPPER_CHAINS = (2, 3)


RUN_FLOOR = -104.0


def _any_weight_left(run_s):
    return (jnp.max(run_s[...]) > RUN_FLOOR).astype(jnp.int32)


def _chains(low):
    return [(2 * half + h, half, msk) for half in range(2)
            for h, msk in enumerate((low, jnp.logical_not(low)))]


def _suffix_operator(t):
    r = lax.broadcasted_iota(jnp.int32, (2 * t, t), 0)
    c = lax.broadcasted_iota(jnp.int32, (2 * t, t), 1)
    return jnp.where((r > c) & ((r < t) | (r - t > c)), 1.0, 0.0).astype(BF16)


def _strips(t):
    return [(i, slice(i * STRIP, (i + 1) * STRIP)) for i in range(t // STRIP)]


def _strip_mask(i, t):
    r = lax.broadcasted_iota(jnp.int32, (STRIP, t), 0) + i * STRIP
    c = lax.broadcasted_iota(jnp.int32, (STRIP, t), 1)
    return r > c


def _store_split(ref, rows, val, t):
    hi = val.astype(BF16)
    ref[rows, 0:t] = hi
    ref[rows, t:2 * t] = (val - hi.astype(F32)).astype(BF16)


def _sb_scores(z_s, split_s, zl_s, tot_s, keep_s, t, diag):
    for i, rows in _strips(t):
        z = z_s[rows, :]
        log_beta = jnp.minimum(z, 0.0) - jnp.log(1.0 + jnp.exp(-jnp.abs(z)))
        log_keep = log_beta - z
        if diag:
            log_keep = jnp.where(_strip_mask(i, t), log_keep, 0.0)
        _store_split(split_s, rows, log_keep, t)
        zl_s[rows, :] = log_beta
        tot_s[rows, :] = _row_sum(log_keep)
        if keep_s is not None:
            keep_s[rows, :] = jnp.exp(log_keep)


def _row_sum(v):
    return jnp.broadcast_to(jnp.sum(v, axis=-1, keepdims=True), (v.shape[0], LANES))


def _wide(r, t):
    return jnp.concatenate([r] * (t // LANES), axis=1)


def _sb_weights(zl_s, suf_s, run_s, tot_s, a_s, t, diag, da_s=None, glog_s=None, gsplit_s=None, gtot_s=None):
    for i, rows in _strips(t):
        run = run_s[rows, :]
        a = jnp.exp(zl_s[rows, :] + suf_s[rows, :] + _wide(run, t))
        if diag:
            a = jnp.where(_strip_mask(i, t), a, 0.0)
        ab = a.astype(BF16)
        a_s[rows, :] = ab
        run_s[rows, :] = run + tot_s[rows, :]
        if da_s is not None:
            glog = ab.astype(F32) * da_s[rows, :]
            glog_s[rows, :] = glog
            _store_split(gsplit_s, rows, glog, t)
            gtot_s[rows, :] = _row_sum(glog)


def _sb_dscores(glog_s, cum_s, rest_s, gtot_s, keep_s, dz_s, t, diag):
    for i, rows in _strips(t):
        glog = glog_s[rows, :]
        rest = rest_s[rows, :]
        from_here = _wide(rest, t) - cum_s[rows, :]
        before = from_here - glog
        dz = from_here * keep_s[rows, :] - before
        if diag:
            dz = jnp.where(_strip_mask(i, t), dz, 0.0)
        dz_s[rows, :] = dz.astype(BF16)
        rest_s[rows, :] = rest - gtot_s[rows, :]


def _attn_fwd(proj, g_attn, cat, w_conv, staged, t=ATTN_BLOCK):
    s = proj.shape[0]
    w_attn = g_attn.shape[1]
    nh = w_attn // LANES
    t = _tile(s, t)
    tq = 2 * t
    nq = s // tq
    q0 = 3 * w_conv // LANES
    scale = HEAD_DIM ** -0.5
    nw = len(staged)

    def body(q_ref, k_ref, v_ref, g_ref, cat_in, *rest):
        staged_refs, rest = rest[:nw], rest[nw:]
        o_ref, cat_ref = rest[:2]
        gathered_refs, rest = rest[2:2 + nw], rest[2 + nw:]
        kb, vb, tri_s, qm_s, z_s, split_s, zl_s, suf_s, a_s, run_s, tot_s, acc_s = rest[:12]
        gather_sems = rest[12:]
        del cat_in
        qi = pl.program_id(1)

        @pl.when((pl.program_id(0) == 0) & (qi == 0))
        def _():
            for cp in _gather_copies(staged_refs, gathered_refs, *gather_sems):
                cp.start()

        @pl.when(qi == 0)
        def _():
            kb[...] = k_ref[...].astype(BF16)
            vb[...] = v_ref[...].astype(BF16)
            tri_s[...] = _suffix_operator(t)

        low = _low_half()
        for c, half, msk in _chains(low):
            qm_s[c] = jnp.where(msk, q_ref[half * t:(half + 1) * t, :] * scale, 0.0).astype(BF16)
            run_s[c] = jnp.zeros((t, LANES), F32)
            acc_s[c] = jnp.zeros((t, LANES), F32)

        def key_rows(kblk):
            return pl.ds(pl.multiple_of(kblk * t, t), t)

        def scores_matmul(kblk, chains):
            ks = kb[key_rows(kblk), :]
            for c in chains:
                z_s[c] = lax.dot_general(qm_s[c], ks, _NT, preferred_element_type=F32)

        def front(modes, nxt, prev=None):
            for c, diag in modes:
                _sb_scores(z_s.at[c], split_s.at[c], zl_s.at[c], tot_s.at[c], None, t, diag)
                suf_s[c] = jnp.dot(split_s[c], tri_s[...], preferred_element_type=F32)
            if prev is not None:
                tail(*prev)
            scores_matmul(nxt, ALL_CHAINS)
            for c, diag in modes:
                _sb_weights(zl_s.at[c], suf_s.at[c], run_s.at[c], tot_s.at[c], a_s.at[c], t, diag)

        def tail(kblk, chains):
            vs = vb[key_rows(kblk), :]
            for c in chains:
                acc_s[c] += jnp.dot(a_s[c], vs, preferred_element_type=F32)

        top = 2 * qi + 1
        scores_matmul(top, UPPER_CHAINS)
        front([(c, True) for c in UPPER_CHAINS], top - 1)
        front([(c, c not in UPPER_CHAINS) for c in ALL_CHAINS], jnp.maximum(top - 2, 0),
              prev=(top, UPPER_CHAINS))

        def loop(state):
            it = state[0]
            cur = top - 2 - it
            front([(c, False) for c in ALL_CHAINS], jnp.maximum(cur - 1, 0), prev=(cur + 1, ALL_CHAINS))
            return it + 1, _any_weight_left(run_s)

        done, _ = lax.while_loop(lambda state: (state[0] < top - 1) & (state[1] > 0), loop,
                                 (jnp.int32(0), jnp.int32(1)))
        tail(top - 1 - done, ALL_CHAINS)
        for half in range(2):
            rows = slice(half * t, (half + 1) * t)
            o = jnp.where(low, acc_s[2 * half], acc_s[2 * half + 1])
            o_ref[rows, :] = o
            r = lax.rsqrt(_half_mean(o * o, low) + EPS)
            cat_ref[rows, :] = (o * r * g_ref[...]).astype(BF16)

        @pl.when((pl.program_id(0) == nh - 1) & (qi == nq - 1))
        def _():
            for cp in _gather_copies(staged_refs, gathered_refs, *gather_sems):
                cp.wait()

    whole = lambda col0: pl.BlockSpec((s, LANES), lambda h, i: (0, col0 + h))
    n_ch = len(ALL_CHAINS)
    res = _pcall(
        body, name="attn_fwd", grid=(nh, nq),
        in_specs=[pl.BlockSpec((tq, LANES), lambda h, i: (i, q0 + h)),
                  whole(q0 + nh), whole(q0 + 2 * nh),
                  pl.BlockSpec((1, LANES), lambda h, i: (0, h)),
                  pl.BlockSpec(memory_space=pl.ANY)] + [pl.BlockSpec(memory_space=pl.ANY)] * nw,
        out_specs=[pl.BlockSpec((tq, LANES), lambda h, i: (i, h)),
                   pl.BlockSpec((tq, LANES), lambda h, i: (i, w_conv // LANES + h))]
        + [pl.BlockSpec(memory_space=pl.ANY)] * nw,
        out_shape=[jax.ShapeDtypeStruct((s, w_attn), F32),
                   jax.ShapeDtypeStruct(cat.shape, BF16)]
        + [jax.ShapeDtypeStruct((N_DEV, *a.shape), BF16) for a in staged],
        scratch_shapes=[pltpu.VMEM((s, LANES), BF16), pltpu.VMEM((s, LANES), BF16),
                        pltpu.VMEM((2 * t, t), BF16),
                        pltpu.VMEM((n_ch, t, LANES), BF16),
                        pltpu.VMEM((n_ch, t, t), F32),
                        pltpu.VMEM((n_ch, t, 2 * t), BF16),
                        pltpu.VMEM((n_ch, t, t), F32),
                        pltpu.VMEM((n_ch, t, t), F32),
                        pltpu.VMEM((n_ch, t, t), BF16),
                        pltpu.VMEM((n_ch, t, LANES), F32),
                        pltpu.VMEM((n_ch, t, LANES), F32),
                        pltpu.VMEM((n_ch, t, LANES), F32)]
        + _exchange_sems(nw, local=True),
        input_output_aliases={4: 1},
        compiler_params=_params(("arbitrary", "arbitrary")),
    )(proj, proj, proj, g_attn, cat, *staged)
    return res[0], res[1], res[2:]


def _attn_bwd(proj, o, dcat, g_attn, w_conv, partials, t=ATTN_BLOCK):
    s, n_cols = proj.shape
    w_attn = g_attn.shape[1]
    nh = w_attn // LANES
    t = _tile(s, t)
    tq = 2 * t
    nq = s // tq
    q0 = 3 * w_conv // LANES
    scale = HEAD_DIM ** -0.5
    nw = len(partials)

    def body(q_ref, k_ref, v_ref, o_ref, do_ref, g_ref, *rest):
        partial_refs, rest = rest[:nw], rest[nw:]
        dproj_ref, dg_ref = rest[:2]
        received_refs, rest = rest[2:2 + nw], rest[2 + nw:]
        (kb, vb, dkt_acc, dvt_acc, stash, tri_s, qm_s, dom_s, qt_s, dot_s, z_s, da_s, split_s, zl_s,
         keep_s, suf_s, a_s, glog_s, gsplit_s, cum_s, dz_s, run_s, tot_s, rest_s, gtot_s, dq_s) = rest[:26]
        scatter_sems = rest[26:]
        step_i = pl.program_id(1)
        which = pl.program_id(2)
        qi = nq - 1 - step_i
        head_pair = pl.program_id(0)

        @pl.when((head_pair == 0) & (step_i == 0) & (which == 0))
        def _():
            for cp in _scatter_copies(partial_refs, received_refs, *scatter_sems):
                cp.start()

        @pl.when((head_pair == nh - 1) & (step_i == nq - 1) & (which == 2))
        def _():
            for cp in _scatter_copies(partial_refs, received_refs, *scatter_sems):
                cp.wait()

        @pl.when(which == 0)
        def _():
            @pl.when(step_i == 0)
            def _():
                kb[...] = k_ref[...].astype(BF16)
                vb[...] = v_ref[...].astype(BF16)
                tri_s[...] = _suffix_operator(t)
                dkt_acc[...] = jnp.zeros_like(dkt_acc)
                dvt_acc[...] = jnp.zeros_like(dvt_acc)
                dg_ref[...] = jnp.zeros_like(dg_ref)

            low = _low_half()
            gv = g_ref[...]
            for half in range(2):
                rows = slice(half * t, (half + 1) * t)
                q = q_ref[rows, :] * scale
                ov = o_ref[rows, :]
                d_o, dgn = _head_norm_bwd(ov, do_ref[rows, :], gv, low)
                dg_ref[...] += jnp.sum(dgn, axis=0, keepdims=True)
                for h, msk in enumerate((low, jnp.logical_not(low))):
                    c = 2 * half + h
                    qh = jnp.where(msk, q, 0.0)
                    doh = jnp.where(msk, d_o, 0.0)
                    dom = doh.astype(BF16)
                    qm_s[c] = qh.astype(BF16)
                    dom_s[c] = dom
                    qt_s[c] = qh.T.astype(BF16)
                    dot_s[c] = doh.T.astype(BF16)
                    rest_s[c] = _row_sum(dom.astype(F32) * ov)
                    run_s[c] = jnp.zeros((t, LANES), F32)
                    dq_s[c] = jnp.zeros((t, LANES), F32)

            def key_rows(kblk):
                return pl.ds(pl.multiple_of(kblk * t, t), t)

            def scores_matmul(kblk, chains):
                ks = kb[key_rows(kblk), :]
                for c in chains:
                    z_s[c] = lax.dot_general(qm_s[c], ks, _NT, preferred_element_type=F32)

            def da_matmul(kblk, chains):
                vs = vb[key_rows(kblk), :]
                for c in chains:
                    da_s[c] = lax.dot_general(dom_s[c], vs, _NT, preferred_element_type=F32)

            def front(modes, nxt, prev=None):
                if prev is not None:
                    tail(*prev)
                for c, diag in modes:
                    _sb_scores(z_s.at[c], split_s.at[c], zl_s.at[c], tot_s.at[c], keep_s.at[c], t, diag)
                    suf_s[c] = jnp.dot(split_s[c], tri_s[...], preferred_element_type=F32)
                scores_matmul(nxt, ALL_CHAINS)
                for c, diag in modes:
                    _sb_weights(zl_s.at[c], suf_s.at[c], run_s.at[c], tot_s.at[c], a_s.at[c], t, diag,
                                da_s.at[c], glog_s.at[c], gsplit_s.at[c], gtot_s.at[c])
                    cum_s[c] = jnp.dot(gsplit_s[c], tri_s[...], preferred_element_type=F32)
                da_matmul(nxt, ALL_CHAINS)
                for c, diag in modes:
                    _sb_dscores(glog_s.at[c], cum_s.at[c], rest_s.at[c], gtot_s.at[c], keep_s.at[c],
                                dz_s.at[c], t, diag)

            def tail(kblk, chains):
                ks = kb[key_rows(kblk), :]
                dkt = dkt_acc[kblk]
                dvt = dvt_acc[kblk]
                for c in chains:
                    dq_s[c] += jnp.dot(dz_s[c], ks, preferred_element_type=F32)
                    dkt = dkt + jnp.dot(qt_s[c], dz_s[c], preferred_element_type=F32)
                    dvt = dvt + jnp.dot(dot_s[c], a_s[c], preferred_element_type=F32)
                dkt_acc[kblk] = dkt
                dvt_acc[kblk] = dvt

            top = 2 * qi + 1
            scores_matmul(top, UPPER_CHAINS)
            da_matmul(top, UPPER_CHAINS)
            front([(c, True) for c in UPPER_CHAINS], top - 1)
            front([(c, c not in UPPER_CHAINS) for c in ALL_CHAINS], jnp.maximum(top - 2, 0),
                  prev=(top, UPPER_CHAINS))

            def loop(state):
                it = state[0]
                cur = top - 2 - it
                front([(c, False) for c in ALL_CHAINS], jnp.maximum(cur - 1, 0), prev=(cur + 1, ALL_CHAINS))
                return it + 1, _any_weight_left(run_s)

            done, _ = lax.while_loop(lambda state: (state[0] < top - 1) & (state[1] > 0), loop,
                                     (jnp.int32(0), jnp.int32(1)))
            tail(top - 1 - done, ALL_CHAINS)
            for half in range(2):
                rows = slice(half * t, (half + 1) * t)
                stash[0, rows, :] = (jnp.where(low, dq_s[2 * half], dq_s[2 * half + 1]) * scale).astype(BF16)
                stash[1, rows, :] = dkt_acc[2 * qi + half].T.astype(BF16)
                stash[2, rows, :] = dvt_acc[2 * qi + half].T.astype(BF16)

        dproj_ref[...] = stash[which]

    whole = lambda col0: pl.BlockSpec((s, LANES), lambda h, i, w: (0, col0 + h))
    blk = lambda col0: pl.BlockSpec((tq, LANES), lambda h, i, w: (nq - 1 - i, col0 + h))
    n_ch = len(ALL_CHAINS)
    res = _pcall(
        body, name="attn_bwd", grid=(nh, nq, 3),
        in_specs=[blk(q0), whole(q0 + nh), whole(q0 + 2 * nh), blk(0), blk(w_conv // LANES),
                  pl.BlockSpec((1, LANES), lambda h, i, w: (0, h))] + [pl.BlockSpec(memory_space=pl.ANY)] * nw,
        out_specs=[pl.BlockSpec((tq, LANES), lambda h, i, w: (nq - 1 - i, q0 + w * nh + h)),
                   pl.BlockSpec((1, LANES), lambda h, i, w: (0, h))] + [pl.BlockSpec(memory_space=pl.ANY)] * nw,
        out_shape=[jax.ShapeDtypeStruct((s, n_cols), BF16), jax.ShapeDtypeStruct((1, w_attn), F32)]
        + [jax.ShapeDtypeStruct((N_PEERS, *a.shape[1:]), BF16) for a in partials],
        scratch_shapes=[pltpu.VMEM((s, LANES), BF16), pltpu.VMEM((s, LANES), BF16),
                        pltpu.VMEM((s // t, LANES, t), F32),
                        pltpu.VMEM((s // t, LANES, t), F32),
                        pltpu.VMEM((3, tq, LANES), BF16),
                        pltpu.VMEM((2 * t, t), BF16),
                        pltpu.VMEM((n_ch, t, LANES), BF16),
                        pltpu.VMEM((n_ch, t, LANES), BF16),
                        pltpu.VMEM((n_ch, LANES, t), BF16),
                        pltpu.VMEM((n_ch, LANES, t), BF16),
                        pltpu.VMEM((n_ch, t, t), F32),
                        pltpu.VMEM((n_ch, t, t), F32),
                        pltpu.VMEM((n_ch, t, 2 * t), BF16),
                        pltpu.VMEM((n_ch, t, t), F32),
                        pltpu.VMEM((n_ch, t, t), F32),
                        pltpu.VMEM((n_ch, t, t), F32),
                        pltpu.VMEM((n_ch, t, t), BF16),
                        pltpu.VMEM((n_ch, t, t), F32),
                        pltpu.VMEM((n_ch, t, 2 * t), BF16),
                        pltpu.VMEM((n_ch, t, t), F32),
                        pltpu.VMEM((n_ch, t, t), BF16),
                        pltpu.VMEM((n_ch, t, LANES), F32),
                        pltpu.VMEM((n_ch, t, LANES), F32),
                        pltpu.VMEM((n_ch, t, LANES), F32),
                        pltpu.VMEM((n_ch, t, LANES), F32),
                        pltpu.VMEM((n_ch, t, LANES), F32)]
        + _exchange_sems(nw),
        compiler_params=_params(("arbitrary", "arbitrary", "arbitrary")),
    )(proj, proj, proj, o, dcat, g_attn, *partials)
    return res[0], res[1], res[2:]


def _place():
    return lax.axis_index("x"), lax.axis_index("y"), lax.axis_index("c")


def _other_chips(x, y):
    return [(1 - x, y), (x, 1 - y), (1 - x, 1 - y)]


def _slot(px, py, pc):
    return 4 * px + 2 * py + pc


def _all_gather(shards, out_dtypes):
    nw = len(shards)

    def body(*refs):
        ins, outs, stage = refs[:nw], refs[nw:2 * nw], refs[2 * nw:3 * nw]
        send_sems, recv_sems, local_sems = refs[3 * nw:]
        x, y, c = _place()
        me, sibling = (x, y, c), (x, y, 1 - c)
        chips = _other_chips(x, y)

        def copy(w, k, block, to, src=None):
            dst = outs[w].at[_slot(*block)]
            return pltpu.make_async_remote_copy(
                src_ref=dst if src is None else src, dst_ref=dst,
                send_sem=send_sems.at[w * 7 + k], recv_sem=recv_sems.at[w * 7 + k],
                device_id=to, device_id_type=MESH)

        started = []
        local = []
        for w in range(nw):
            stage[w][...] = ins[w][...].astype(stage[w].dtype)
            cp = pltpu.make_async_copy(stage[w], outs[w].at[_slot(*me)], local_sems.at[w])
            cp.start()
            local.append(cp)
            started.append(copy(w, 0, me, sibling, src=stage[w]))
            started[-1].start()
            for j, chip in enumerate(chips):
                started.append(copy(w, 1 + j, me, (*chip, c), src=stage[w]))
                started[-1].start()
        for j, chip in enumerate(chips):
            for w in range(nw):
                copy(w, 1 + j, (*chip, c), me).wait_recv()
                started.append(copy(w, 4 + j, (*chip, c), sibling))
                started[-1].start()
        for w in range(nw):
            copy(w, 0, sibling, me).wait_recv()
            for j, chip in enumerate(chips):
                copy(w, 4 + j, (*chip, 1 - c), me).wait_recv()
        for cp in started:
            cp.wait_send()
        for cp in local:
            cp.wait()

    return _pcall(
        body, name="all_gather_weights",
        in_specs=[pl.BlockSpec(memory_space=pltpu.VMEM)] * nw,
        out_specs=[pl.BlockSpec(memory_space=pl.ANY)] * nw,
        out_shape=[jax.ShapeDtypeStruct((N_DEV, *a.shape), d) for a, d in zip(shards, out_dtypes)],
        scratch_shapes=[pltpu.VMEM(a.shape, d) for a, d in zip(shards, out_dtypes)]
        + [pltpu.SemaphoreType.DMA((7 * nw,)), pltpu.SemaphoreType.DMA((7 * nw,)),
           pltpu.SemaphoreType.DMA((nw,))],
        compiler_params=_params(),
    )(*shards)


N_PEERS = N_DEV - 1


def _peer(k):
    x, y, c = _place()
    return (x ^ (k >> 2), y ^ ((k >> 1) & 1), c ^ (k & 1))


def _fanout(src_of, dst_of, nw, send_sems, recv_sems):
    return [pltpu.make_async_remote_copy(
        src_ref=src_of(w, k), dst_ref=dst_of(w, k),
        send_sem=send_sems.at[w * N_PEERS + k - 1], recv_sem=recv_sems.at[w * N_PEERS + k - 1],
        device_id=_peer(k), device_id_type=MESH) for w in range(nw) for k in range(1, N_DEV)]


def _gather_copies(staged, gathered, send_sems, recv_sems, local_sems):
    me = _slot(*_place())
    nw = len(staged)
    remote = _fanout(lambda w, k: staged[w], lambda w, k: gathered[w].at[me], nw, send_sems, recv_sems)
    local = [pltpu.make_async_copy(staged[w], gathered[w].at[me], local_sems.at[w]) for w in range(nw)]
    return remote + local


def _scatter_copies(partials, received, send_sems, recv_sems):
    me = _slot(*_place())
    return _fanout(lambda w, k: partials[w].at[me ^ k], lambda w, k: received[w].at[k - 1],
                   len(partials), send_sems, recv_sems)


def _exchange_sems(nw, local=False):
    sems = [pltpu.SemaphoreType.DMA((N_PEERS * nw,)), pltpu.SemaphoreType.DMA((N_PEERS * nw,))]
    return sems + ([pltpu.SemaphoreType.DMA((nw,))] if local else [])


def _carried_shapes(kind, arrays):
    if kind == "gather":
        return [jax.ShapeDtypeStruct((N_DEV, *a.shape), BF16) for a in arrays]
    return [jax.ShapeDtypeStruct((N_PEERS, *a.shape[1:]), BF16) for a in arrays]


def _carried_sems(kind, nw):
    return _exchange_sems(nw, local=(kind == "gather")) if nw else []


def _carried_copies(kind, srcs, dsts, sems):
    return _gather_copies(srcs, dsts, *sems) if kind == "gather" else _scatter_copies(srcs, dsts, *sems)


def _cast_shards(shards):
    def body(*refs):
        for src, dst in zip(refs[:len(shards)], refs[len(shards):]):
            dst[...] = src[...].astype(BF16)

    return _pcall(
        body, name="cast_shards",
        in_specs=[pl.BlockSpec(memory_space=pltpu.VMEM)] * len(shards),
        out_specs=[pl.BlockSpec(memory_space=pltpu.VMEM)] * len(shards),
        out_shape=[jax.ShapeDtypeStruct(a.shape, BF16) for a in shards],
        compiler_params=_params(),
    )(*shards)


def _all_reduce_small(packed):
    r = packed.shape[0]

    def body(x_ref, o_ref, gathered, send_sems, recv_sems):
        x, y, c = _place()
        me = _slot(x, y, c)
        gathered[me] = x_ref[...]
        peers = [(px, py, pc) for px in range(2) for py in range(2) for pc in range(2)]
        started = []
        for k in range(1, N_DEV):
            to = (x ^ (k >> 2), y ^ ((k >> 1) & 1), c ^ (k & 1))
            cp = pltpu.make_async_remote_copy(
                src_ref=x_ref, dst_ref=gathered.at[me],
                send_sem=send_sems.at[k - 1], recv_sem=recv_sems.at[k - 1],
                device_id=to, device_id_type=MESH)
            cp.start()
            started.append(cp)
        del peers
        for cp in started:
            cp.wait()
        total = gathered[0]
        for k in range(1, N_DEV):
            total = total + gathered[k]
        o_ref[...] = total

    return _pcall(
        body, name="all_reduce_small",
        in_specs=[pl.BlockSpec(memory_space=pltpu.VMEM)],
        out_specs=pl.BlockSpec(memory_space=pltpu.VMEM),
        out_shape=jax.ShapeDtypeStruct(packed.shape, F32),
        scratch_shapes=[pltpu.VMEM((N_DEV, r, LANES), F32),
                        pltpu.SemaphoreType.DMA((N_DEV - 1,)), pltpu.SemaphoreType.DMA((N_DEV - 1,))],
        compiler_params=_params(),
    )(packed)


def _adam_math(w, g, m, v):
    m = ADAM_B1 * m + (1.0 - ADAM_B1) * g
    v = ADAM_B2 * v + (1.0 - ADAM_B2) * jnp.square(g)
    m_hat = m / (1.0 - ADAM_B1 ** ADAM_STEP)
    v_hat = v / (1.0 - ADAM_B2 ** ADAM_STEP)
    delta = -ADAM_LR * (m_hat / (jnp.sqrt(v_hat) + ADAM_EPS) + ADAM_WD * w)
    return delta, m, v


def _adam_sharded(name, own, received, w, m, v, place, tr=256):
    r, cdim = w.shape
    tr = _tile(r, tr) if r % LANES == 0 else r

    def body(place_ref, own_ref, rec_ref, w_ref, m_ref, v_ref, g_ref, d_ref, nm_ref, nv_ref):
        del place_ref
        g = own_ref[...]
        for j in range(N_PEERS):
            g = g + rec_ref[j].astype(F32)
        delta, nm, nv = _adam_math(w_ref[...], g, m_ref[...], v_ref[...])
        g_ref[...] = g
        d_ref[...] = delta
        nm_ref[...] = nm
        nv_ref[...] = nv

    blk = pl.BlockSpec((tr, cdim), lambda i, pr: (i, 0))
    grid_spec = pltpu.PrefetchScalarGridSpec(
        num_scalar_prefetch=1, grid=(r // tr,),
        in_specs=[pl.BlockSpec((None, tr, cdim), lambda i, pr: (4 * pr[0] + 2 * pr[1] + pr[2], i, 0)),
                  pl.BlockSpec((N_PEERS, tr, cdim), lambda i, pr: (0, i, 0)), blk, blk, blk],
        out_specs=[blk] * 4)
    return _pcall(body, name=name, grid_spec=grid_spec,
                  out_shape=[jax.ShapeDtypeStruct((r, cdim), F32)] * 4,
                  compiler_params=_params(("parallel",)))(place, own, received, w, m, v)


def _adam_small(w, g, m, v):
    def body(w_ref, g_ref, m_ref, v_ref, d_ref, nm_ref, nv_ref):
        delta, nm, nv = _adam_math(w_ref[...], g_ref[...], m_ref[...], v_ref[...])
        d_ref[...] = delta
        nm_ref[...] = nm
        nv_ref[...] = nv

    return _pcall(body, name="adam_small",
                  in_specs=[pl.BlockSpec(memory_space=pltpu.VMEM)] * 4,
                  out_specs=[pl.BlockSpec(memory_space=pltpu.VMEM)] * 3,
                  out_shape=[jax.ShapeDtypeStruct(w.shape, F32)] * 3,
                  compiler_params=_params())(w, g, m, v)


def _rows(vec):
    return vec.reshape(-1, LANES)


def kernel(x, p, g_mix, w_in, conv_w, g_conv_out, g_attn_out, w_out, g_mlp, w_up, w_down, g_ple, w_ple_gate, w_ple_proj, g_final, loss_target, m_g_mix, m_w_in, m_conv_w, m_g_conv_out, m_g_attn_out, m_w_out, m_g_mlp, m_w_up, m_w_down, m_g_ple, m_w_ple_gate, m_w_ple_proj, m_g_final, v_g_mix, v_w_in, v_conv_w, v_g_conv_out, v_g_attn_out, v_w_out, v_g_mlp, v_w_up, v_w_down, v_g_ple, v_w_ple_gate, v_w_ple_proj, v_g_final):
    s, d = x.shape[1], x.shape[2]
    w_conv = g_conv_out.shape[1]
    w_attn = g_attn_out.shape[1]
    cw = conv_w.shape[2]
    xs, ps, tgt = x[0], p[0, 0], loss_target[0]
    place = jnp.stack([lax.axis_index("x"), lax.axis_index("y"), lax.axis_index("c")]).astype(jnp.int32)
    my_slot = 4 * place[0] + 2 * place[1] + place[2]

    conv_tile = jnp.pad(conv_w[0], ((0, HALO - CONV_K), (0, LANES - cw)))
    big = [w_in[0], w_out[0], w_up[0], w_down[0], w_ple_gate[0], w_ple_proj[0]]
    win_g, conv_g = _all_gather([big[0], conv_tile], [BF16, F32])
    s_out, s_up, s_down, s_gate, s_proj = _cast_shards(big[1:])
    conv_full = jnp.transpose(conv_g[:, :CONV_K, :cw], (1, 0, 2)).reshape(CONV_K, w_conv)
    in_shard, up_shard, proj_shard = big[0].shape[1], big[2].shape[1], big[5].shape[1]

    a = _rmsnorm_fwd("norm_mix", xs, g_mix)
    proj, wout_g, wgate_g, wproj_g = _mm_nn("in_proj", a, win_g, n_shard=in_shard, tn=in_shard,
                                            carry=("gather", [s_out, s_gate, s_proj]))
    cat = _conv_fwd(proj, conv_full, g_conv_out, w_conv, d)
    o, cat, (wup_g,) = _attn_fwd(proj, g_attn_out, cat, w_conv, [s_up])
    wout_f = wout_g.reshape(-1, wout_g.shape[-1])
    wgate_f = wgate_g.reshape(-1, wgate_g.shape[-1])
    h1, = _mm_nn("out_proj", cat, wout_f, epilogue=_ep_residual, extras=(xs,))
    mn = _rmsnorm_fwd("norm_mlp", h1, g_mlp)
    act, wdown_g = _mm_nn("mlp_up", mn, wup_g, n_shard=up_shard, epilogue=_ep_up, out_dtypes=(BF16,),
                          carry=("gather", [s_down]))
    wdown_f = wdown_g.reshape(-1, wdown_g.shape[-1])
    h2, = _mm_nn("mlp_down", act, wdown_f, epilogue=_ep_residual, extras=(h1,))
    n3 = _rmsnorm_fwd("norm_ple", h2, g_ple)
    gl, = _mm_nn("ple_gate", n3, wgate_f)
    pp = _ple_proj(ps, wproj_g)
    loss_part, dh3, dgl, dpp, dg_final = _ple_loss(h2, gl, pp, tgt, g_final.reshape(1, d))
    loss = lax.psum(loss_part[0, 0], ("x", "y", "c"))

    def slots(t2d):
        return t2d.reshape(N_DEV, -1, t2d.shape[-1])

    dw_proj = _d_ple_proj(ps, dpp, proj_shard)
    dw_gate = [slots(t) for t in _mm_tn("d_w_ple_gate", n3, dgl)]
    dn3, = _mm_nt("d_norm_ple", dgl, wgate_f)
    dh2, dh2b, dg_ple = _rmsnorm_bwd("norm_ple_bwd", dn3, h2, g_ple, dh3)
    du, gate_recv, proj_recv = _mm_nt("d_mlp_act", dh2b, wdown_f, epilogue=_ep_dact, out_dtypes=(BF16,),
                                      extras=(act,), carry=("scatter", [dw_gate[1], dw_proj[1]]))
    dw_down = [slots(t) for t in _mm_tn("d_w_down", act, dh2b)]
    *dw_up, down_recv = _mm_tn("d_w_up", mn, du, n_shard=up_shard, carry=("scatter", [dw_down[1]]))
    dmn, = _mm_nt("d_norm_mlp", du, wup_g, k_shard=up_shard)
    dh1, dh1b, dg_mlp = _rmsnorm_bwd("norm_mlp_bwd", dmn, h1, g_mlp, dh2)
    dcat, = _mm_nt("d_cat", dh1b, wout_f)
    dw_out = [slots(t) for t in _mm_tn("d_w_out", cat, dh1b)]
    dproj, dg_attn, (up_recv,) = _attn_bwd(proj, o, dcat, g_attn_out, w_conv, [dw_up[1]])
    dproj, dconv, dg_conv = _conv_bwd(proj, dcat, conv_full, g_conv_out, dproj, w_conv)
    *dw_in, out_recv = _mm_tn("d_w_in", a, dproj, n_shard=in_shard, tn=in_shard, carry=("scatter", [dw_out[1]]))
    da, in_recv = _mm_nt("d_norm_mix", dproj, win_g, k_shard=in_shard, tk=in_shard, carry=("scatter", [dw_in[1]]))
    grad_x, _, dg_mix = _rmsnorm_bwd("norm_mix_bwd", da, xs, g_mix, dh1)

    names = ["w_in", "w_out", "w_up", "w_down", "w_ple_gate", "w_ple_proj"]
    owns = [dw_in[0], dw_out[0], dw_up[0], dw_down[0], dw_gate[0], dw_proj[0]]
    recvs = [in_recv, out_recv, up_recv, down_recv, gate_recv, proj_recv]
    moments = [(m_w_in, v_w_in), (m_w_out, v_w_out), (m_w_up, v_w_up), (m_w_down, v_w_down),
               (m_w_ple_gate, v_w_ple_gate), (m_w_ple_proj, v_w_ple_proj)]
    big_out = {}
    for n, own, rc, wt, (mm, vv) in zip(names, owns, recvs, big, moments):
        big_out[n] = [t[None] for t in _adam_sharded("adam_" + n, own, rc, wt, mm[0], vv[0], place)]

    small_g = jnp.concatenate(
        [_rows(dg_mix[0]), _rows(dg_conv[0]), _rows(dg_attn[0]), _rows(dg_mlp[0]), _rows(dg_ple[0]),
         _rows(dg_final[0]), _rows(dconv.reshape(-1))], axis=0)
    n_gain_rows = small_g.shape[0] - CONV_K * w_conv // LANES
    pad_rows = (-small_g.shape[0]) % HALO
    small_g = _all_reduce_small(jnp.pad(small_g, ((0, pad_rows), (0, 0))))
    dconv_full = small_g[n_gain_rows:n_gain_rows + CONV_K * w_conv // LANES].reshape(CONV_K, w_conv)
    dconv_mine = lax.dynamic_slice(dconv_full, (0, my_slot * cw), (CONV_K, cw))

    def pack(vecs, conv_part):
        rows = [_rows(t.reshape(-1)) for t in vecs]
        rows.append(jnp.pad(conv_part, ((0, HALO - CONV_K), (0, LANES - cw))))
        return jnp.concatenate(rows, axis=0)

    gains = [g_mix, g_conv_out, g_attn_out, g_mlp, g_ple, g_final]
    gains_m = [m_g_mix, m_g_conv_out, m_g_attn_out, m_g_mlp, m_g_ple, m_g_final]
    gains_v = [v_g_mix, v_g_conv_out, v_g_attn_out, v_g_mlp, v_g_ple, v_g_final]
    gpack = jnp.concatenate([small_g[:n_gain_rows], jnp.pad(dconv_mine, ((0, HALO - CONV_K), (0, LANES - cw)))], axis=0)
    sd, sm, sv = _adam_small(pack(gains, conv_w[0]), gpack, pack(gains_m, m_conv_w[0]), pack(gains_v, v_conv_w[0]))

    def unpack(packed):
        out, r0 = [], 0
        for t in gains:
            nr = t.size // LANES
            out.append(packed[r0:r0 + nr].reshape(t.shape))
            r0 += nr
        out.append(packed[r0:r0 + CONV_K, :cw][None])
        return out

    sg_l, sd_l, sm_l, sv_l = unpack(gpack), unpack(sd), unpack(sm), unpack(sv)
    small_names = ["g_mix", "g_conv_out", "g_attn_out", "g_mlp", "g_ple", "g_final", "conv_w"]
    small_out = {n: [sg_l[i], sd_l[i], sm_l[i], sv_l[i]] for i, n in enumerate(small_names)}

    order = ["g_mix", "w_in", "conv_w", "g_conv_out", "g_attn_out", "w_out", "g_mlp", "w_up", "w_down",
             "g_ple", "w_ple_gate", "w_ple_proj", "g_final"]
    table = {**big_out, **small_out}
    outs = [loss, grad_x[None]]
    for kind in range(4):
        outs.extend(table[n][kind] for n in order)
    return tuple(outs)
```

```python
import functools

import jax
import jax.numpy as jnp
from jax import lax
from jax.experimental import pallas as pl
from jax.experimental.pallas import tpu as pltpu

F32 = jnp.float32
BF16 = jnp.bfloat16
EPS = 1e-6
HEAD_DIM = 64
LANES = 128
CONV_K = 3
ATTN_BLOCK = 256
HALO = 8
N_DEV = 8
MESH = pl.DeviceIdType.MESH
VMEM_LIMIT = 56 * 1024 * 1024

ADAM_LR = 0.001
ADAM_B1 = 0.9
ADAM_B2 = 0.999
ADAM_EPS = 1e-08
ADAM_WD = 0.01
ADAM_STEP = 10


def _pcall(body, **kw):
    return pl.pallas_call(body, **kw)


def _params(sem=None, **kw):
    return pltpu.CompilerParams(dimension_semantics=sem, vmem_limit_bytes=VMEM_LIMIT, **kw)


def _tile(dim, pref):
    t = min(dim, pref)
    while dim % t:
        t -= LANES
    assert t > 0, (dim, pref)
    return t


_NN = (((1,), (0,)), ((), ()))
_NT = (((1,), (1,)), ((), ()))
_TN = (((0,), (0,)), ((), ()))


def _ep_store(acc, outs):
    outs[0][...] = acc.astype(outs[0].dtype)


def _ep_both(acc, outs):
    outs[0][...] = acc
    outs[1][...] = acc.astype(BF16)


def _ep_residual(acc, res, outs):
    outs[0][...] = acc + res[...]


def _ep_up(acc, outs):
    outs[0][...] = jnp.square(jnp.maximum(acc, 0.0)).astype(BF16)


def _ep_dact(acc, act, outs):
    outs[0][...] = (acc * (2.0 * jnp.sqrt(act[...].astype(F32)))).astype(BF16)


def _ep_norm_bwd(acc, h, g, dres, outs):
    @pl.when(pl.program_id(0) == 0)
    def _():
        outs[2][...] = jnp.zeros_like(outs[2])

    hv = h[...]
    r = lax.rsqrt(jnp.mean(hv * hv, axis=-1, keepdims=True) + EPS)
    hn = hv * r
    outs[2][...] += jnp.sum(acc * hn, axis=0, keepdims=True)
    dhn = acc * g[...]
    dh = dres[...] + r * (dhn - hn * jnp.mean(dhn * hn, axis=-1, keepdims=True))
    outs[0][...] = dh
    outs[1][...] = dh.astype(BF16)


def _matmul(name, a, b, *, dims, grid, a_spec, b_spec, acc_shape, out_shapes, out_specs,
            epilogue=_ep_store, extras=(), extra_specs=(), carry=("scatter", ()), sequential=False):
    nk = grid[2]
    kind, carried = carry
    n_ex, n_out, n_xc = len(extras), len(out_shapes), len(carried)
    n_sems = len(_carried_sems(kind, n_xc))
    last = tuple(g - 1 for g in grid)

    def product(a_ref, b_ref):
        return lax.dot_general(a_ref[...].astype(BF16), b_ref[...].astype(BF16), dims,
                               preferred_element_type=F32)

    def body(a_ref, b_ref, *rest):
        ex, rest = rest[:n_ex], rest[n_ex:]
        partials, rest = rest[:n_xc], rest[n_xc:]
        outs, rest = rest[:n_out], rest[n_out:]
        received, rest = rest[:n_xc], rest[n_xc:]
        ids = [pl.program_id(axis) for axis in range(3)]
        if n_xc:
            @pl.when((ids[0] == 0) & (ids[1] == 0) & (ids[2] == 0))
            def _():
                for cp in _carried_copies(kind, partials, received, rest[-n_sems:]):
                    cp.start()

        if nk == 1:
            epilogue(product(a_ref, b_ref), *ex, outs)
        else:
            acc = rest[0]

            @pl.when(ids[2] == 0)
            def _():
                acc[...] = product(a_ref, b_ref)

            @pl.when(ids[2] > 0)
            def _():
                acc[...] += product(a_ref, b_ref)

            @pl.when(ids[2] == nk - 1)
            def _():
                epilogue(acc[...], *ex, outs)

        if n_xc:
            @pl.when((ids[0] == last[0]) & (ids[1] == last[1]) & (ids[2] == last[2]))
            def _():
                for cp in _carried_copies(kind, partials, received, rest[-n_sems:]):
                    cp.wait()

    anywhere = [pl.BlockSpec(memory_space=pl.ANY)] * n_xc
    return _pcall(
        body, name=name, grid=grid,
        in_specs=[a_spec, b_spec, *extra_specs, *anywhere],
        out_specs=[*out_specs, *anywhere],
        out_shape=[*out_shapes, *_carried_shapes(kind, carried)],
        scratch_shapes=([] if nk == 1 else [pltpu.VMEM(acc_shape, F32)]) + _carried_sems(kind, n_xc),
        compiler_params=_params(("arbitrary",) * 3 if n_xc or sequential else ("parallel", "parallel", "arbitrary")),
    )(a, b, *extras, *carried)


_NO_CARRY = ("scatter", ())


def _mm_nn(name, a, w, *, n_shard=None, epilogue=_ep_store, out_dtypes=(F32,), extras=(), carry=_NO_CARRY,
           tm=1024, tn=1024, tk=1024):
    m, kd = a.shape
    if n_shard is None:
        n = w.shape[1]
        tn = _tile(n, tn)
        tk = _tile(kd, tk)
        b_spec = pl.BlockSpec((tk, tn), lambda i, j, k: (k, j))
    else:
        n = N_DEV * n_shard
        tn = _tile(n_shard, tn)
        tk = _tile(kd, tk)
        per = n_shard // tn
        b_spec = pl.BlockSpec((None, tk, tn), lambda i, j, k: (j // per, k, j % per))
    tm = _tile(m, tm)
    o_spec = pl.BlockSpec((tm, tn), lambda i, j, k: (i, j))
    return _matmul(
        name, a, w, dims=_NN, grid=(m // tm, n // tn, kd // tk),
        a_spec=pl.BlockSpec((tm, tk), lambda i, j, k: (i, k)), b_spec=b_spec,
        acc_shape=(tm, tn),
        out_shapes=[jax.ShapeDtypeStruct((m, n), d) for d in out_dtypes],
        out_specs=[o_spec] * len(out_dtypes),
        epilogue=epilogue, extras=extras, extra_specs=[o_spec] * len(extras), carry=carry)


def _mm_nt(name, a, w, *, k_shard=None, epilogue=_ep_store, out_dtypes=(F32,), extras=(), carry=_NO_CARRY,
           tm=1024, tn=1024, tk=1024):
    m, kd = a.shape
    if k_shard is None:
        n = w.shape[0]
        tn = _tile(n, tn)
        tk = _tile(kd, tk)
        b_spec = pl.BlockSpec((tn, tk), lambda i, j, k: (j, k))
    else:
        n = w.shape[1]
        tn = _tile(n, tn)
        tk = _tile(k_shard, tk)
        per = k_shard // tk
        b_spec = pl.BlockSpec((None, tn, tk), lambda i, j, k: (k // per, j, k % per))
    tm = _tile(m, tm)
    o_spec = pl.BlockSpec((tm, tn), lambda i, j, k: (i, j))
    return _matmul(
        name, a, w, dims=_NT, grid=(m // tm, n // tn, kd // tk),
        a_spec=pl.BlockSpec((tm, tk), lambda i, j, k: (i, k)), b_spec=b_spec,
        acc_shape=(tm, tn),
        out_shapes=[jax.ShapeDtypeStruct((m, n), d) for d in out_dtypes],
        out_specs=[o_spec] * len(out_dtypes),
        epilogue=epilogue, extras=extras, extra_specs=[o_spec] * len(extras), carry=carry)


def _mm_nt_norm_bwd(name, a, w, h, g, dres, *, k_shard=None, carry=_NO_CARRY, tm=512, tk=1024):
    m, kd = a.shape
    n = h.shape[1]
    if k_shard is None:
        tk = _tile(kd, tk)
        b_spec = pl.BlockSpec((n, tk), lambda i, j, k: (0, k))
    else:
        tk = _tile(k_shard, tk)
        per = k_shard // tk
        b_spec = pl.BlockSpec((None, n, tk), lambda i, j, k: (k // per, 0, k % per))
    tm = _tile(m, tm)
    rows = pl.BlockSpec((tm, n), lambda i, j, k: (i, 0))
    vec = pl.BlockSpec((1, n), lambda i, j, k: (0, 0))
    return _matmul(
        name, a, w, dims=_NT, grid=(m // tm, 1, kd // tk),
        a_spec=pl.BlockSpec((tm, tk), lambda i, j, k: (i, k)), b_spec=b_spec, acc_shape=(tm, n),
        out_shapes=[jax.ShapeDtypeStruct((m, n), F32), jax.ShapeDtypeStruct((m, n), BF16),
                    jax.ShapeDtypeStruct((1, n), F32)],
        out_specs=[rows, rows, vec], epilogue=_ep_norm_bwd,
        extras=(h, g, dres), extra_specs=[rows, vec, rows], carry=carry, sequential=True)


def _mm_tn(name, a, b, *, n_shard=None, carry=_NO_CARRY, tm=1024, tn=1024, tk=1024):
    t, m = a.shape
    n = b.shape[1]
    tm = _tile(m, tm)
    tk = _tile(t, tk)
    if n_shard is None:
        tn = _tile(n, tn)
        o_spec = pl.BlockSpec((tm, tn), lambda i, j, k: (i, j))
        shape = (m, n)
    else:
        tn = _tile(n_shard, tn)
        per = n_shard // tn
        o_spec = pl.BlockSpec((None, tm, tn), lambda i, j, k: (j // per, i, j % per))
        shape = (N_DEV, m, n_shard)
    return _matmul(
        name, a, b, dims=_TN, grid=(m // tm, n // tn, t // tk),
        a_spec=pl.BlockSpec((tk, tm), lambda i, j, k: (k, i)),
        b_spec=pl.BlockSpec((tk, tn), lambda i, j, k: (k, j)),
        acc_shape=(tm, tn), epilogue=_ep_both, carry=carry,
        out_shapes=[jax.ShapeDtypeStruct(shape, F32), jax.ShapeDtypeStruct(shape, BF16)],
        out_specs=[o_spec, o_spec])


def _ple_proj(p, w_g, tm=1024):
    s, kd = p.shape
    ns = w_g.shape[2]
    tm = _tile(s, tm)

    def body(p_ref, w_ref, o_ref):
        pv = p_ref[...].astype(BF16)
        for j in range(N_DEV):
            o_ref[:, j * ns:(j + 1) * ns] = jnp.dot(pv, w_ref[j], preferred_element_type=F32)

    return _pcall(body, name="ple_proj", grid=(s // tm,),
                  in_specs=[pl.BlockSpec((tm, kd), lambda i: (i, 0)),
                            pl.BlockSpec((N_DEV, kd, ns), lambda i: (0, 0, 0))],
                  out_specs=pl.BlockSpec((tm, N_DEV * ns), lambda i: (i, 0)),
                  out_shape=jax.ShapeDtypeStruct((s, N_DEV * ns), F32),
                  compiler_params=_params(("parallel",)))(p, w_g)


def _d_ple_proj(p, dpp, ns, tk=1024):
    s, kd = p.shape
    tk = _tile(s, tk)
    nk = s // tk

    def body(p_ref, d_ref, of_ref, ob_ref, acc):
        k = pl.program_id(0)

        @pl.when(k == 0)
        def _():
            acc[...] = jnp.zeros_like(acc)

        pv = p_ref[...].astype(BF16)
        for j in range(N_DEV):
            acc[j] += lax.dot_general(pv, d_ref[:, j * ns:(j + 1) * ns], _TN, preferred_element_type=F32)

        @pl.when(k == nk - 1)
        def _():
            of_ref[...] = acc[...]
            ob_ref[...] = acc[...].astype(BF16)

    whole = pl.BlockSpec((N_DEV, kd, ns), lambda k: (0, 0, 0))
    return _pcall(body, name="d_w_ple_proj", grid=(nk,),
                  in_specs=[pl.BlockSpec((tk, kd), lambda k: (k, 0)),
                            pl.BlockSpec((tk, N_DEV * ns), lambda k: (k, 0))],
                  out_specs=[whole, whole],
                  out_shape=[jax.ShapeDtypeStruct((N_DEV, kd, ns), F32), jax.ShapeDtypeStruct((N_DEV, kd, ns), BF16)],
                  scratch_shapes=[pltpu.VMEM((N_DEV, kd, ns), F32)],
                  compiler_params=_params(("arbitrary",)))(p, dpp)


def _row_spec(tr, d):
    return pl.BlockSpec((tr, d), lambda i: (i, 0))


def _vec_spec(d):
    return pl.BlockSpec((1, d), lambda i: (0, 0))


def _rmsnorm_fwd(name, x, g, tr=512):
    s, d = x.shape
    tr = _tile(s, tr)

    def body(x_ref, g_ref, o_ref):
        xv = x_ref[...]
        r = lax.rsqrt(jnp.mean(xv * xv, axis=-1, keepdims=True) + EPS)
        o_ref[...] = (xv * r * g_ref[...]).astype(BF16)

    return _pcall(body, name=name, grid=(s // tr,),
                  in_specs=[_row_spec(tr, d), _vec_spec(d)], out_specs=_row_spec(tr, d),
                  out_shape=jax.ShapeDtypeStruct((s, d), BF16),
                  compiler_params=_params(("parallel",)))(x, g)


def _rmsnorm_bwd(name, dn, h, g, dres, tr=512):
    s, d = h.shape
    tr = _tile(s, tr)

    def body(dn_ref, h_ref, g_ref, dres_ref, dh_ref, dhb_ref, dg_ref):
        @pl.when(pl.program_id(0) == 0)
        def _():
            dg_ref[...] = jnp.zeros_like(dg_ref)

        hv = h_ref[...]
        dnv = dn_ref[...]
        r = lax.rsqrt(jnp.mean(hv * hv, axis=-1, keepdims=True) + EPS)
        hn = hv * r
        dg_ref[...] += jnp.sum(dnv * hn, axis=0, keepdims=True)
        dhn = dnv * g_ref[...]
        dh = dres_ref[...] + r * (dhn - hn * jnp.mean(dhn * hn, axis=-1, keepdims=True))
        dh_ref[...] = dh
        dhb_ref[...] = dh.astype(BF16)

    return _pcall(body, name=name, grid=(s // tr,),
                  in_specs=[_row_spec(tr, d), _row_spec(tr, d), _vec_spec(d), _row_spec(tr, d)],
                  out_specs=[_row_spec(tr, d), _row_spec(tr, d), _vec_spec(d)],
                  out_shape=[jax.ShapeDtypeStruct((s, d), F32), jax.ShapeDtypeStruct((s, d), BF16),
                             jax.ShapeDtypeStruct((1, d), F32)],
                  compiler_params=_params(("arbitrary",)))(dn, h, g, dres)


def _ple_loss(h2, gl, pp, tgt, g_final, tr=512):
    s, d = h2.shape
    tr = _tile(s, tr)

    def body(h2_ref, gl_ref, pp_ref, t_ref, g_ref, loss_ref, dh3_ref, dgl_ref, dpp_ref, dg_ref):
        @pl.when(pl.program_id(0) == 0)
        def _():
            dg_ref[...] = jnp.zeros_like(dg_ref)
            loss_ref[...] = jnp.zeros_like(loss_ref)

        gate = jax.nn.sigmoid(gl_ref[...])
        ppv = pp_ref[...]
        h3 = h2_ref[...] + gate * ppv
        r = lax.rsqrt(jnp.mean(h3 * h3, axis=-1, keepdims=True) + EPS)
        hn = h3 * r
        gv = g_ref[...]
        diff = hn * gv - t_ref[...]
        row = jnp.mean(diff * diff, axis=-1, keepdims=True)
        loss_ref[...] += 0.5 * jnp.sum(row, axis=0, keepdims=True)
        dy = diff * (1.0 / d)
        dg_ref[...] += jnp.sum(dy * hn, axis=0, keepdims=True)
        dhn = dy * gv
        dh3 = r * (dhn - hn * jnp.mean(dhn * hn, axis=-1, keepdims=True))
        dh3_ref[...] = dh3
        dgl_ref[...] = (dh3 * ppv * gate * (1.0 - gate)).astype(BF16)
        dpp_ref[...] = (dh3 * gate).astype(BF16)

    return _pcall(body, name="ple_loss", grid=(s // tr,),
                  in_specs=[_row_spec(tr, d)] * 4 + [_vec_spec(d)],
                  out_specs=[_vec_spec(LANES), _row_spec(tr, d), _row_spec(tr, d), _row_spec(tr, d), _vec_spec(d)],
                  out_shape=[jax.ShapeDtypeStruct((1, LANES), F32), jax.ShapeDtypeStruct((s, d), F32),
                             jax.ShapeDtypeStruct((s, d), BF16), jax.ShapeDtypeStruct((s, d), BF16),
                             jax.ShapeDtypeStruct((1, d), F32)],
                  compiler_params=_params(("arbitrary",)))(h2, gl, pp, tgt, g_final)


def _low_half():
    return lax.broadcasted_iota(jnp.int32, (1, LANES), 1) < HEAD_DIM


def _half_mean(v, low):
    s_lo = jnp.sum(jnp.where(low, v, 0.0), axis=-1, keepdims=True)
    s_hi = jnp.sum(jnp.where(low, 0.0, v), axis=-1, keepdims=True)
    return jnp.where(low, s_lo, s_hi) * (1.0 / HEAD_DIM)


def _head_norm_bwd(val, dout, g, low):
    r = lax.rsqrt(_half_mean(val * val, low) + EPS)
    vn = val * r
    dvn = dout * g
    return r * (dvn - vn * _half_mean(dvn * vn, low)), dout * vn


def _conv_taps(vv_ext, w_ref, rows):
    v0 = vv_ext[HALO:]
    v1 = pltpu.roll(vv_ext, 1, 0)[HALO:]
    v2 = pltpu.roll(vv_ext, 2, 0)[HALO:]
    del rows
    return w_ref[2:3, :] * v0 + w_ref[1:2, :] * v1 + w_ref[0:1, :] * v2, (v0, v1, v2)


def _conv_fwd(proj, conv_w, g_conv, w_conv, d_model, tr=512):
    s = proj.shape[0]
    tr = _tile(s, tr)
    hb = tr // HALO

    def main(part):
        return pl.BlockSpec((tr, w_conv), lambda i: (i, part))

    def prev(part):
        return pl.BlockSpec((HALO, w_conv), lambda i: (jnp.maximum(i * hb - 1, 0), part))

    def body(cb_ref, cc_ref, cu_ref, ccp_ref, cup_ref, w_ref, g_ref, o_ref):
        i = pl.program_id(0)
        low = _low_half()
        for j in range(w_conv // LANES):
            cols = slice(j * LANES, (j + 1) * LANES)
            vv_prev = jnp.where(i > 0, ccp_ref[:, cols] * cup_ref[:, cols], 0.0)
            vv_ext = jnp.concatenate([vv_prev, cc_ref[:, cols] * cu_ref[:, cols]], axis=0)
            y, _ = _conv_taps(vv_ext, w_ref.at[:, cols], tr)
            co = cb_ref[:, cols] * y
            r = lax.rsqrt(_half_mean(co * co, low) + EPS)
            o_ref[:, cols] = (co * r * g_ref[:, cols]).astype(BF16)

    return _pcall(
        body, name="conv_fwd", grid=(s // tr,),
        in_specs=[main(0), main(1), main(2), prev(1), prev(2),
                  pl.BlockSpec((CONV_K, w_conv), lambda i: (0, 0)),
                  pl.BlockSpec((1, w_conv), lambda i: (0, 0))],
        out_specs=pl.BlockSpec((tr, w_conv), lambda i: (i, 0)),
        out_shape=jax.ShapeDtypeStruct((s, d_model), BF16),
        compiler_params=_params(("parallel",)),
    )(proj, proj, proj, proj, proj, conv_w, g_conv)


def _conv_bwd(proj, dcat, conv_w, g_conv, dproj, w_conv, tr=512):
    s = proj.shape[0]
    tr = _tile(s, tr)
    hb = tr // HALO
    last = s // HALO - 1
    nt = s // tr

    def main(part):
        return pl.BlockSpec((tr, w_conv), lambda i: (i, part))

    def prev(part):
        return pl.BlockSpec((HALO, w_conv), lambda i: (jnp.maximum(i * hb - 1, 0), part))

    def nxt(part):
        return pl.BlockSpec((HALO, w_conv), lambda i: (jnp.minimum((i + 1) * hb, last), part))

    def body(cb_ref, cc_ref, cu_ref, dc_ref, ccp_ref, cup_ref, cbn_ref, ccn_ref, cun_ref, dcn_ref,
             w_ref, g_ref, dproj_in, dproj_ref, dw_ref, dg_ref):
        del dproj_in
        i = pl.program_id(0)

        @pl.when(i == 0)
        def _():
            dw_ref[...] = jnp.zeros_like(dw_ref)
            dg_ref[...] = jnp.zeros_like(dg_ref)

        low = _low_half()
        n_ext = tr + HALO
        rowid = lax.broadcasted_iota(jnp.int32, (n_ext, 1), 0)
        for j in range(w_conv // LANES):
            cols = slice(j * LANES, (j + 1) * LANES)
            wj = w_ref.at[:, cols]
            cc, cu = cc_ref[:, cols], cu_ref[:, cols]
            vv_prev = jnp.where(i > 0, ccp_ref[:, cols] * cup_ref[:, cols], 0.0)
            vv_ext = jnp.concatenate([vv_prev, cc * cu, ccn_ref[:, cols] * cun_ref[:, cols]], axis=0)
            y_ext, (v0, v1, v2) = _conv_taps(vv_ext, wj, n_ext)
            cb_ext = jnp.concatenate([cb_ref[:, cols], cbn_ref[:, cols]], axis=0)
            dc_ext = jnp.concatenate([dc_ref[:, cols], dcn_ref[:, cols]], axis=0)
            dco, dgn = _head_norm_bwd(cb_ext * y_ext, dc_ext, g_ref[:, cols], low)
            dyc = jnp.where((rowid < tr) | (i < nt - 1), dco * cb_ext, 0.0)
            dvv = (wj[2:3, :] * dyc[:tr] + wj[1:2, :] * pltpu.roll(dyc, n_ext - 1, 0)[:tr]
                   + wj[0:1, :] * pltpu.roll(dyc, n_ext - 2, 0)[:tr])
            dproj_ref[:, cols] = (dco[:tr] * y_ext[:tr]).astype(BF16)
            dproj_ref[:, w_conv + j * LANES:w_conv + (j + 1) * LANES] = (dvv * cu).astype(BF16)
            dproj_ref[:, 2 * w_conv + j * LANES:2 * w_conv + (j + 1) * LANES] = (dvv * cc).astype(BF16)
            dyt = dyc[:tr]
            for tap, shifted in enumerate((v2, v1, v0)):
                dw_ref[tap:tap + 1, cols] += jnp.sum(dyt * shifted[:tr], axis=0, keepdims=True)
            dg_ref[:, cols] += jnp.sum(dgn[:tr], axis=0, keepdims=True)

    n_cols = dproj.shape[1]
    return _pcall(
        body, name="conv_bwd", grid=(nt,),
        in_specs=[main(0), main(1), main(2), main(0),
                  prev(1), prev(2), nxt(0), nxt(1), nxt(2), nxt(0),
                  pl.BlockSpec((CONV_K, w_conv), lambda i: (0, 0)),
                  pl.BlockSpec((1, w_conv), lambda i: (0, 0)),
                  pl.BlockSpec(memory_space=pl.ANY)],
        out_specs=[pl.BlockSpec((tr, 3 * w_conv), lambda i: (i, 0)),
                   pl.BlockSpec((CONV_K, w_conv), lambda i: (0, 0)),
                   pl.BlockSpec((1, w_conv), lambda i: (0, 0))],
        out_shape=[jax.ShapeDtypeStruct((s, n_cols), BF16),
                   jax.ShapeDtypeStruct((CONV_K, w_conv), F32),
                   jax.ShapeDtypeStruct((1, w_conv), F32)],
        input_output_aliases={12: 0},
        compiler_params=_params(("arbitrary",)),
    )(proj, proj, proj, dcat, proj, proj, proj, proj, proj, dcat, conv_w, g_conv, dproj)


STRIP = 16

ALL_CHAINS = (0, 1, 2, 3)
UPPER_CHAINS = (2, 3)


RUN_FLOOR = -104.0


def _any_weight_left(run_s):
    return (jnp.max(run_s[...]) > RUN_FLOOR).astype(jnp.int32)


def _chains(low):
    return [(2 * half + h, half, msk) for half in range(2)
            for h, msk in enumerate((low, jnp.logical_not(low)))]


def _suffix_operator(t):
    r = lax.broadcasted_iota(jnp.int32, (2 * t, t), 0)
    c = lax.broadcasted_iota(jnp.int32, (2 * t, t), 1)
    return jnp.where((r > c) & ((r < t) | (r - t > c)), 1.0, 0.0).astype(BF16)


def _strips(t):
    return [(i, slice(i * STRIP, (i + 1) * STRIP)) for i in range(t // STRIP)]


def _strip_mask(i, t):
    r = lax.broadcasted_iota(jnp.int32, (STRIP, t), 0) + i * STRIP
    c = lax.broadcasted_iota(jnp.int32, (STRIP, t), 1)
    return r > c


def _store_split(ref, rows, val, t):
    hi = val.astype(BF16)
    ref[rows, 0:t] = hi
    ref[rows, t:2 * t] = (val - hi.astype(F32)).astype(BF16)


def _sb_scores(z_s, split_s, zl_s, tot_s, keep_s, t, diag):
    for i, rows in _strips(t):
        z = z_s[rows, :]
        log_beta = jnp.minimum(z, 0.0) - jnp.log(1.0 + jnp.exp(-jnp.abs(z)))
        log_keep = log_beta - z
        if diag:
            log_keep = jnp.where(_strip_mask(i, t), log_keep, 0.0)
        _store_split(split_s, rows, log_keep, t)
        zl_s[rows, :] = log_beta
        tot_s[rows, :] = _row_sum(log_keep)
        if keep_s is not None:
            keep_s[rows, :] = jnp.exp(log_keep)


def _row_sum(v):
    return jnp.broadcast_to(jnp.sum(v, axis=-1, keepdims=True), (v.shape[0], LANES))


def _wide(r, t):
    return jnp.concatenate([r] * (t // LANES), axis=1)


def _sb_weights(zl_s, suf_s, run_s, tot_s, a_s, t, diag, da_s=None, glog_s=None, gsplit_s=None, gtot_s=None):
    for i, rows in _strips(t):
        run = run_s[rows, :]
        a = jnp.exp(zl_s[rows, :] + suf_s[rows, :] + _wide(run, t))
        if diag:
            a = jnp.where(_strip_mask(i, t), a, 0.0)
        ab = a.astype(BF16)
        a_s[rows, :] = ab
        run_s[rows, :] = run + tot_s[rows, :]
        if da_s is not None:
            glog = ab.astype(F32) * da_s[rows, :]
            glog_s[rows, :] = glog
            _store_split(gsplit_s, rows, glog, t)
            gtot_s[rows, :] = _row_sum(glog)


def _sb_dscores(glog_s, cum_s, rest_s, gtot_s, keep_s, dz_s, t, diag):
    for i, rows in _strips(t):
        glog = glog_s[rows, :]
        rest = rest_s[rows, :]
        from_here = _wide(rest, t) - cum_s[rows, :]
        before = from_here - glog
        dz = from_here * keep_s[rows, :] - before
        if diag:
            dz = jnp.where(_strip_mask(i, t), dz, 0.0)
        dz_s[rows, :] = dz.astype(BF16)
        rest_s[rows, :] = rest - gtot_s[rows, :]


def _attn_fwd(proj, g_attn, cat, w_conv, staged, t=ATTN_BLOCK):
    s = proj.shape[0]
    w_attn = g_attn.shape[1]
    nh = w_attn // LANES
    t = _tile(s, t)
    tq = 2 * t
    nq = s // tq
    q0 = 3 * w_conv // LANES
    scale = HEAD_DIM ** -0.5
    nw = len(staged)

    def body(q_ref, k_ref, v_ref, g_ref, cat_in, *rest):
        staged_refs, rest = rest[:nw], rest[nw:]
        o_ref, cat_ref = rest[:2]
        gathered_refs, rest = rest[2:2 + nw], rest[2 + nw:]
        kb, vb, tri_s, qm_s, z_s, split_s, zl_s, suf_s, a_s, run_s, tot_s, acc_s = rest[:12]
        gather_sems = rest[12:]
        del cat_in
        qi = pl.program_id(1)

        @pl.when((pl.program_id(0) == 0) & (qi == 0))
        def _():
            for cp in _gather_copies(staged_refs, gathered_refs, *gather_sems):
                cp.start()

        @pl.when(qi == 0)
        def _():
            kb[...] = k_ref[...].astype(BF16)
            vb[...] = v_ref[...].astype(BF16)
            tri_s[...] = _suffix_operator(t)

        low = _low_half()
        for c, half, msk in _chains(low):
            qm_s[c] = jnp.where(msk, q_ref[half * t:(half + 1) * t, :] * scale, 0.0).astype(BF16)
            run_s[c] = jnp.zeros((t, LANES), F32)
            acc_s[c] = jnp.zeros((t, LANES), F32)

        def key_rows(kblk):
            return pl.ds(pl.multiple_of(kblk * t, t), t)

        def scores_matmul(kblk, chains):
            ks = kb[key_rows(kblk), :]
            for c in chains:
                z_s[c] = lax.dot_general(qm_s[c], ks, _NT, preferred_element_type=F32)

        def front(modes, nxt, prev=None):
            for c, diag in modes:
                _sb_scores(z_s.at[c], split_s.at[c], zl_s.at[c], tot_s.at[c], None, t, diag)
                suf_s[c] = jnp.dot(split_s[c], tri_s[...], preferred_element_type=F32)
            if prev is not None:
                tail(*prev)
            scores_matmul(nxt, ALL_CHAINS)
            for c, diag in modes:
                _sb_weights(zl_s.at[c], suf_s.at[c], run_s.at[c], tot_s.at[c], a_s.at[c], t, diag)

        def tail(kblk, chains):
            vs = vb[key_rows(kblk), :]
            for c in chains:
                acc_s[c] += jnp.dot(a_s[c], vs, preferred_element_type=F32)

        top = 2 * qi + 1
        scores_matmul(top, UPPER_CHAINS)
        front([(c, True) for c in UPPER_CHAINS], top - 1)
        front([(c, c not in UPPER_CHAINS) for c in ALL_CHAINS], jnp.maximum(top - 2, 0),
              prev=(top, UPPER_CHAINS))

        def loop(state):
            it = state[0]
            cur = top - 2 - it
            front([(c, False) for c in ALL_CHAINS], jnp.maximum(cur - 1, 0), prev=(cur + 1, ALL_CHAINS))
            return it + 1, _any_weight_left(run_s)

        done, _ = lax.while_loop(lambda state: (state[0] < top - 1) & (state[1] > 0), loop,
                                 (jnp.int32(0), jnp.int32(1)))
        tail(top - 1 - done, ALL_CHAINS)
        for half in range(2):
            rows = slice(half * t, (half + 1) * t)
            o = jnp.where(low, acc_s[2 * half], acc_s[2 * half + 1])
            o_ref[rows, :] = o
            r = lax.rsqrt(_half_mean(o * o, low) + EPS)
            cat_ref[rows, :] = (o * r * g_ref[...]).astype(BF16)

        @pl.when((pl.program_id(0) == nh - 1) & (qi == nq - 1))
        def _():
            for cp in _gather_copies(staged_refs, gathered_refs, *gather_sems):
                cp.wait()

    whole = lambda col0: pl.BlockSpec((s, LANES), lambda h, i: (0, col0 + h))
    n_ch = len(ALL_CHAINS)
    res = _pcall(
        body, name="attn_fwd", grid=(nh, nq),
        in_specs=[pl.BlockSpec((tq, LANES), lambda h, i: (i, q0 + h)),
                  whole(q0 + nh), whole(q0 + 2 * nh),
                  pl.BlockSpec((1, LANES), lambda h, i: (0, h)),
                  pl.BlockSpec(memory_space=pl.ANY)] + [pl.BlockSpec(memory_space=pl.ANY)] * nw,
        out_specs=[pl.BlockSpec((tq, LANES), lambda h, i: (i, h)),
                   pl.BlockSpec((tq, LANES), lambda h, i: (i, w_conv // LANES + h))]
        + [pl.BlockSpec(memory_space=pl.ANY)] * nw,
        out_shape=[jax.ShapeDtypeStruct((s, w_attn), F32),
                   jax.ShapeDtypeStruct(cat.shape, BF16)]
        + [jax.ShapeDtypeStruct((N_DEV, *a.shape), BF16) for a in staged],
        scratch_shapes=[pltpu.VMEM((s, LANES), BF16), pltpu.VMEM((s, LANES), BF16),
                        pltpu.VMEM((2 * t, t), BF16),
                        pltpu.VMEM((n_ch, t, LANES), BF16),
                        pltpu.VMEM((n_ch, t, t), F32),
                        pltpu.VMEM((n_ch, t, 2 * t), BF16),
                        pltpu.VMEM((n_ch, t, t), F32),
                        pltpu.VMEM((n_ch, t, t), F32),
                        pltpu.VMEM((n_ch, t, t), BF16),
                        pltpu.VMEM((n_ch, t, LANES), F32),
                        pltpu.VMEM((n_ch, t, LANES), F32),
                        pltpu.VMEM((n_ch, t, LANES), F32)]
        + _exchange_sems(nw, local=True),
        input_output_aliases={4: 1},
        compiler_params=_params(("arbitrary", "arbitrary")),
    )(proj, proj, proj, g_attn, cat, *staged)
    return res[0], res[1], res[2:]


def _attn_bwd(proj, o, dcat, g_attn, w_conv, partials, t=ATTN_BLOCK):
    s, n_cols = proj.shape
    w_attn = g_attn.shape[1]
    nh = w_attn // LANES
    t = _tile(s, t)
    tq = 2 * t
    nq = s // tq
    q0 = 3 * w_conv // LANES
    scale = HEAD_DIM ** -0.5
    nw = len(partials)

    def body(q_ref, k_ref, v_ref, o_ref, do_ref, g_ref, *rest):
        partial_refs, rest = rest[:nw], rest[nw:]
        dproj_ref, dg_ref = rest[:2]
        received_refs, rest = rest[2:2 + nw], rest[2 + nw:]
        (kb, vb, dkt_acc, dvt_acc, stash, tri_s, qm_s, dom_s, qt_s, dot_s, z_s, da_s, split_s, zl_s,
         keep_s, suf_s, a_s, glog_s, gsplit_s, cum_s, dz_s, run_s, tot_s, rest_s, gtot_s, dq_s) = rest[:26]
        scatter_sems = rest[26:]
        step_i = pl.program_id(1)
        which = pl.program_id(2)
        qi = nq - 1 - step_i
        head_pair = pl.program_id(0)

        @pl.when((head_pair == 0) & (step_i == 0) & (which == 0))
        def _():
            for cp in _scatter_copies(partial_refs, received_refs, *scatter_sems):
                cp.start()

        @pl.when((head_pair == nh - 1) & (step_i == nq - 1) & (which == 2))
        def _():
            for cp in _scatter_copies(partial_refs, received_refs, *scatter_sems):
                cp.wait()

        @pl.when(which == 0)
        def _():
            @pl.when(step_i == 0)
            def _():
                kb[...] = k_ref[...].astype(BF16)
                vb[...] = v_ref[...].astype(BF16)
                tri_s[...] = _suffix_operator(t)
                dkt_acc[...] = jnp.zeros_like(dkt_acc)
                dvt_acc[...] = jnp.zeros_like(dvt_acc)
                dg_ref[...] = jnp.zeros_like(dg_ref)

            low = _low_half()
            gv = g_ref[...]
            for half in range(2):
                rows = slice(half * t, (half + 1) * t)
                q = q_ref[rows, :] * scale
                ov = o_ref[rows, :]
                d_o, dgn = _head_norm_bwd(ov, do_ref[rows, :], gv, low)
                dg_ref[...] += jnp.sum(dgn, axis=0, keepdims=True)
                for h, msk in enumerate((low, jnp.logical_not(low))):
                    c = 2 * half + h
                    qh = jnp.where(msk, q, 0.0)
                    doh = jnp.where(msk, d_o, 0.0)
                    dom = doh.astype(BF16)
                    qm_s[c] = qh.astype(BF16)
                    dom_s[c] = dom
                    qt_s[c] = qh.T.astype(BF16)
                    dot_s[c] = doh.T.astype(BF16)
                    rest_s[c] = _row_sum(dom.astype(F32) * ov)
                    run_s[c] = jnp.zeros((t, LANES), F32)
                    dq_s[c] = jnp.zeros((t, LANES), F32)

            def key_rows(kblk):
                return pl.ds(pl.multiple_of(kblk * t, t), t)

            def scores_matmul(kblk, chains):
                ks = kb[key_rows(kblk), :]
                for c in chains:
                    z_s[c] = lax.dot_general(qm_s[c], ks, _NT, preferred_element_type=F32)

            def da_matmul(kblk, chains):
                vs = vb[key_rows(kblk), :]
                for c in chains:
                    da_s[c] = lax.dot_general(dom_s[c], vs, _NT, preferred_element_type=F32)

            def front(modes, nxt, prev=None):
                if prev is not None:
                    tail(*prev)
                for c, diag in modes:
                    _sb_scores(z_s.at[c], split_s.at[c], zl_s.at[c], tot_s.at[c], keep_s.at[c], t, diag)
                    suf_s[c] = jnp.dot(split_s[c], tri_s[...], preferred_element_type=F32)
                scores_matmul(nxt, ALL_CHAINS)
                for c, diag in modes:
                    _sb_weights(zl_s.at[c], suf_s.at[c], run_s.at[c], tot_s.at[c], a_s.at[c], t, diag,
                                da_s.at[c], glog_s.at[c], gsplit_s.at[c], gtot_s.at[c])
                    cum_s[c] = jnp.dot(gsplit_s[c], tri_s[...], preferred_element_type=F32)
                da_matmul(nxt, ALL_CHAINS)
                for c, diag in modes:
                    _sb_dscores(glog_s.at[c], cum_s.at[c], rest_s.at[c], gtot_s.at[c], keep_s.at[c],
                                dz_s.at[c], t, diag)

            def tail(kblk, chains):
                ks = kb[key_rows(kblk), :]
                dkt = dkt_acc[kblk]
                dvt = dvt_acc[kblk]
                for c in chains:
                    dq_s[c] += jnp.dot(dz_s[c], ks, preferred_element_type=F32)
                    dkt = dkt + jnp.dot(qt_s[c], dz_s[c], preferred_element_type=F32)
                    dvt = dvt + jnp.dot(dot_s[c], a_s[c], preferred_element_type=F32)
                dkt_acc[kblk] = dkt
                dvt_acc[kblk] = dvt

            top = 2 * qi + 1
            scores_matmul(top, UPPER_CHAINS)
            da_matmul(top, UPPER_CHAINS)
            front([(c, True) for c in UPPER_CHAINS], top - 1)
            front([(c, c not in UPPER_CHAINS) for c in ALL_CHAINS], jnp.maximum(top - 2, 0),
                  prev=(top, UPPER_CHAINS))

            def loop(state):
                it = state[0]
                cur = top - 2 - it
                front([(c, False) for c in ALL_CHAINS], jnp.maximum(cur - 1, 0), prev=(cur + 1, ALL_CHAINS))
                return it + 1, _any_weight_left(run_s)

            done, _ = lax.while_loop(lambda state: (state[0] < top - 1) & (state[1] > 0), loop,
                                     (jnp.int32(0), jnp.int32(1)))
            tail(top - 1 - done, ALL_CHAINS)
            for half in range(2):
                rows = slice(half * t, (half + 1) * t)
                stash[0, rows, :] = (jnp.where(low, dq_s[2 * half], dq_s[2 * half + 1]) * scale).astype(BF16)
                stash[1, rows, :] = dkt_acc[2 * qi + half].T.astype(BF16)
                stash[2, rows, :] = dvt_acc[2 * qi + half].T.astype(BF16)

        dproj_ref[...] = stash[which]

    whole = lambda col0: pl.BlockSpec((s, LANES), lambda h, i, w: (0, col0 + h))
    blk = lambda col0: pl.BlockSpec((tq, LANES), lambda h, i, w: (nq - 1 - i, col0 + h))
    n_ch = len(ALL_CHAINS)
    res = _pcall(
        body, name="attn_bwd", grid=(nh, nq, 3),
        in_specs=[blk(q0), whole(q0 + nh), whole(q0 + 2 * nh), blk(0), blk(w_conv // LANES),
                  pl.BlockSpec((1, LANES), lambda h, i, w: (0, h))] + [pl.BlockSpec(memory_space=pl.ANY)] * nw,
        out_specs=[pl.BlockSpec((tq, LANES), lambda h, i, w: (nq - 1 - i, q0 + w * nh + h)),
                   pl.BlockSpec((1, LANES), lambda h, i, w: (0, h))] + [pl.BlockSpec(memory_space=pl.ANY)] * nw,
        out_shape=[jax.ShapeDtypeStruct((s, n_cols), BF16), jax.ShapeDtypeStruct((1, w_attn), F32)]
        + [jax.ShapeDtypeStruct((N_PEERS, *a.shape[1:]), BF16) for a in partials],
        scratch_shapes=[pltpu.VMEM((s, LANES), BF16), pltpu.VMEM((s, LANES), BF16),
                        pltpu.VMEM((s // t, LANES, t), F32),
                        pltpu.VMEM((s // t, LANES, t), F32),
                        pltpu.VMEM((3, tq, LANES), BF16),
                        pltpu.VMEM((2 * t, t), BF16),
                        pltpu.VMEM((n_ch, t, LANES), BF16),
                        pltpu.VMEM((n_ch, t, LANES), BF16),
                        pltpu.VMEM((n_ch, LANES, t), BF16),
                        pltpu.VMEM((n_ch, LANES, t), BF16),
                        pltpu.VMEM((n_ch, t, t), F32),
                        pltpu.VMEM((n_ch, t, t), F32),
                        pltpu.VMEM((n_ch, t, 2 * t), BF16),
                        pltpu.VMEM((n_ch, t, t), F32),
                        pltpu.VMEM((n_ch, t, t), F32),
                        pltpu.VMEM((n_ch, t, t), F32),
                        pltpu.VMEM((n_ch, t, t), BF16),
                        pltpu.VMEM((n_ch, t, t), F32),
                        pltpu.VMEM((n_ch, t, 2 * t), BF16),
                        pltpu.VMEM((n_ch, t, t), F32),
                        pltpu.VMEM((n_ch, t, t), BF16),
                        pltpu.VMEM((n_ch, t, LANES), F32),
                        pltpu.VMEM((n_ch, t, LANES), F32),
                        pltpu.VMEM((n_ch, t, LANES), F32),
                        pltpu.VMEM((n_ch, t, LANES), F32),
                        pltpu.VMEM((n_ch, t, LANES), F32)]
        + _exchange_sems(nw),
        compiler_params=_params(("arbitrary", "arbitrary", "arbitrary")),
    )(proj, proj, proj, o, dcat, g_attn, *partials)
    return res[0], res[1], res[2:]


def _place():
    return lax.axis_index("x"), lax.axis_index("y"), lax.axis_index("c")


def _other_chips(x, y):
    return [(1 - x, y), (x, 1 - y), (1 - x, 1 - y)]


def _slot(px, py, pc):
    return 4 * px + 2 * py + pc


def _all_gather(shards, out_dtypes):
    nw = len(shards)

    def body(*refs):
        ins, outs, stage = refs[:nw], refs[nw:2 * nw], refs[2 * nw:3 * nw]
        send_sems, recv_sems, local_sems = refs[3 * nw:]
        x, y, c = _place()
        me, sibling = (x, y, c), (x, y, 1 - c)
        chips = _other_chips(x, y)

        def copy(w, k, block, to, src=None):
            dst = outs[w].at[_slot(*block)]
            return pltpu.make_async_remote_copy(
                src_ref=dst if src is None else src, dst_ref=dst,
                send_sem=send_sems.at[w * 7 + k], recv_sem=recv_sems.at[w * 7 + k],
                device_id=to, device_id_type=MESH)

        started = []
        local = []
        for w in range(nw):
            stage[w][...] = ins[w][...].astype(stage[w].dtype)
            cp = pltpu.make_async_copy(stage[w], outs[w].at[_slot(*me)], local_sems.at[w])
            cp.start()
            local.append(cp)
            started.append(copy(w, 0, me, sibling, src=stage[w]))
            started[-1].start()
            for j, chip in enumerate(chips):
                started.append(copy(w, 1 + j, me, (*chip, c), src=stage[w]))
                started[-1].start()
        for j, chip in enumerate(chips):
            for w in range(nw):
                copy(w, 1 + j, (*chip, c), me).wait_recv()
                started.append(copy(w, 4 + j, (*chip, c), sibling))
                started[-1].start()
        for w in range(nw):
            copy(w, 0, sibling, me).wait_recv()
            for j, chip in enumerate(chips):
                copy(w, 4 + j, (*chip, 1 - c), me).wait_recv()
        for cp in started:
            cp.wait_send()
        for cp in local:
            cp.wait()

    return _pcall(
        body, name="all_gather_weights",
        in_specs=[pl.BlockSpec(memory_space=pltpu.VMEM)] * nw,
        out_specs=[pl.BlockSpec(memory_space=pl.ANY)] * nw,
        out_shape=[jax.ShapeDtypeStruct((N_DEV, *a.shape), d) for a, d in zip(shards, out_dtypes)],
        scratch_shapes=[pltpu.VMEM(a.shape, d) for a, d in zip(shards, out_dtypes)]
        + [pltpu.SemaphoreType.DMA((7 * nw,)), pltpu.SemaphoreType.DMA((7 * nw,)),
           pltpu.SemaphoreType.DMA((nw,))],
        compiler_params=_params(),
    )(*shards)


N_PEERS = N_DEV - 1


def _peer(k):
    x, y, c = _place()
    return (x ^ (k >> 2), y ^ ((k >> 1) & 1), c ^ (k & 1))


def _fanout(src_of, dst_of, nw, send_sems, recv_sems):
    return [pltpu.make_async_remote_copy(
        src_ref=src_of(w, k), dst_ref=dst_of(w, k),
        send_sem=send_sems.at[w * N_PEERS + k - 1], recv_sem=recv_sems.at[w * N_PEERS + k - 1],
        device_id=_peer(k), device_id_type=MESH) for w in range(nw) for k in range(1, N_DEV)]


def _gather_copies(staged, gathered, send_sems, recv_sems, local_sems):
    me = _slot(*_place())
    nw = len(staged)
    remote = _fanout(lambda w, k: staged[w], lambda w, k: gathered[w].at[me], nw, send_sems, recv_sems)
    local = [pltpu.make_async_copy(staged[w], gathered[w].at[me], local_sems.at[w]) for w in range(nw)]
    return remote + local


def _scatter_copies(partials, received, send_sems, recv_sems):
    me = _slot(*_place())
    return _fanout(lambda w, k: partials[w].at[me ^ k], lambda w, k: received[w].at[k - 1],
                   len(partials), send_sems, recv_sems)


def _exchange_sems(nw, local=False):
    sems = [pltpu.SemaphoreType.DMA((N_PEERS * nw,)), pltpu.SemaphoreType.DMA((N_PEERS * nw,))]
    return sems + ([pltpu.SemaphoreType.DMA((nw,))] if local else [])


def _carried_shapes(kind, arrays):
    if kind == "gather":
        return [jax.ShapeDtypeStruct((N_DEV, *a.shape), BF16) for a in arrays]
    return [jax.ShapeDtypeStruct((N_PEERS, *a.shape[1:]), BF16) for a in arrays]


def _carried_sems(kind, nw):
    return _exchange_sems(nw, local=(kind == "gather")) if nw else []


def _carried_copies(kind, srcs, dsts, sems):
    return _gather_copies(srcs, dsts, *sems) if kind == "gather" else _scatter_copies(srcs, dsts, *sems)


def _cast_shards(shards):
    def body(*refs):
        for src, dst in zip(refs[:len(shards)], refs[len(shards):]):
            dst[...] = src[...].astype(BF16)

    return _pcall(
        body, name="cast_shards",
        in_specs=[pl.BlockSpec(memory_space=pltpu.VMEM)] * len(shards),
        out_specs=[pl.BlockSpec(memory_space=pltpu.VMEM)] * len(shards),
        out_shape=[jax.ShapeDtypeStruct(a.shape, BF16) for a in shards],
        compiler_params=_params(),
    )(*shards)


def _all_reduce_small(packed):
    r = packed.shape[0]

    def body(x_ref, o_ref, gathered, send_sems, recv_sems):
        x, y, c = _place()
        me = _slot(x, y, c)
        gathered[me] = x_ref[...]
        peers = [(px, py, pc) for px in range(2) for py in range(2) for pc in range(2)]
        started = []
        for k in range(1, N_DEV):
            to = (x ^ (k >> 2), y ^ ((k >> 1) & 1), c ^ (k & 1))
            cp = pltpu.make_async_remote_copy(
                src_ref=x_ref, dst_ref=gathered.at[me],
                send_sem=send_sems.at[k - 1], recv_sem=recv_sems.at[k - 1],
                device_id=to, device_id_type=MESH)
            cp.start()
            started.append(cp)
        del peers
        for cp in started:
            cp.wait()
        total = gathered[0]
        for k in range(1, N_DEV):
            total = total + gathered[k]
        o_ref[...] = total

    return _pcall(
        body, name="all_reduce_small",
        in_specs=[pl.BlockSpec(memory_space=pltpu.VMEM)],
        out_specs=pl.BlockSpec(memory_space=pltpu.VMEM),
        out_shape=jax.ShapeDtypeStruct(packed.shape, F32),
        scratch_shapes=[pltpu.VMEM((N_DEV, r, LANES), F32),
                        pltpu.SemaphoreType.DMA((N_DEV - 1,)), pltpu.SemaphoreType.DMA((N_DEV - 1,))],
        compiler_params=_params(),
    )(packed)


def _adam_math(w, g, m, v):
    m = ADAM_B1 * m + (1.0 - ADAM_B1) * g
    v = ADAM_B2 * v + (1.0 - ADAM_B2) * jnp.square(g)
    m_hat = m / (1.0 - ADAM_B1 ** ADAM_STEP)
    v_hat = v / (1.0 - ADAM_B2 ** ADAM_STEP)
    delta = -ADAM_LR * (m_hat / (jnp.sqrt(v_hat) + ADAM_EPS) + ADAM_WD * w)
    return delta, m, v


def _adam_sharded(name, own, received, w, m, v, place, tr=256):
    r, cdim = w.shape
    tr = _tile(r, tr) if r % LANES == 0 else r

    def body(place_ref, own_ref, rec_ref, w_ref, m_ref, v_ref, g_ref, d_ref, nm_ref, nv_ref):
        del place_ref
        g = own_ref[...]
        for j in range(N_PEERS):
            g = g + rec_ref[j].astype(F32)
        delta, nm, nv = _adam_math(w_ref[...], g, m_ref[...], v_ref[...])
        g_ref[...] = g
        d_ref[...] = delta
        nm_ref[...] = nm
        nv_ref[...] = nv

    blk = pl.BlockSpec((tr, cdim), lambda i, pr: (i, 0))
    grid_spec = pltpu.PrefetchScalarGridSpec(
        num_scalar_prefetch=1, grid=(r // tr,),
        in_specs=[pl.BlockSpec((None, tr, cdim), lambda i, pr: (4 * pr[0] + 2 * pr[1] + pr[2], i, 0)),
                  pl.BlockSpec((N_PEERS, tr, cdim), lambda i, pr: (0, i, 0)), blk, blk, blk],
        out_specs=[blk] * 4)
    return _pcall(body, name=name, grid_spec=grid_spec,
                  out_shape=[jax.ShapeDtypeStruct((r, cdim), F32)] * 4,
                  compiler_params=_params(("parallel",)))(place, own, received, w, m, v)


def _adam_small(w, g, m, v):
    def body(w_ref, g_ref, m_ref, v_ref, d_ref, nm_ref, nv_ref):
        delta, nm, nv = _adam_math(w_ref[...], g_ref[...], m_ref[...], v_ref[...])
        d_ref[...] = delta
        nm_ref[...] = nm
        nv_ref[...] = nv

    return _pcall(body, name="adam_small",
                  in_specs=[pl.BlockSpec(memory_space=pltpu.VMEM)] * 4,
                  out_specs=[pl.BlockSpec(memory_space=pltpu.VMEM)] * 3,
                  out_shape=[jax.ShapeDtypeStruct(w.shape, F32)] * 3,
                  compiler_params=_params())(w, g, m, v)


def _rows(vec):
    return vec.reshape(-1, LANES)


def kernel(x, p, g_mix, w_in, conv_w, g_conv_out, g_attn_out, w_out, g_mlp, w_up, w_down, g_ple, w_ple_gate, w_ple_proj, g_final, loss_target, m_g_mix, m_w_in, m_conv_w, m_g_conv_out, m_g_attn_out, m_w_out, m_g_mlp, m_w_up, m_w_down, m_g_ple, m_w_ple_gate, m_w_ple_proj, m_g_final, v_g_mix, v_w_in, v_conv_w, v_g_conv_out, v_g_attn_out, v_w_out, v_g_mlp, v_w_up, v_w_down, v_g_ple, v_w_ple_gate, v_w_ple_proj, v_g_final):
    s, d = x.shape[1], x.shape[2]
    w_conv = g_conv_out.shape[1]
    w_attn = g_attn_out.shape[1]
    cw = conv_w.shape[2]
    xs, ps, tgt = x[0], p[0, 0], loss_target[0]
    place = jnp.stack([lax.axis_index("x"), lax.axis_index("y"), lax.axis_index("c")]).astype(jnp.int32)
    my_slot = 4 * place[0] + 2 * place[1] + place[2]

    conv_tile = jnp.pad(conv_w[0], ((0, HALO - CONV_K), (0, LANES - cw)))
    big = [w_in[0], w_out[0], w_up[0], w_down[0], w_ple_gate[0], w_ple_proj[0]]
    win_g, conv_g = _all_gather([big[0], conv_tile], [BF16, F32])
    s_out, s_up, s_down, s_gate, s_proj = _cast_shards(big[1:])
    conv_full = jnp.transpose(conv_g[:, :CONV_K, :cw], (1, 0, 2)).reshape(CONV_K, w_conv)
    in_shard, up_shard, proj_shard = big[0].shape[1], big[2].shape[1], big[5].shape[1]

    a = _rmsnorm_fwd("norm_mix", xs, g_mix)
    proj, wout_g, wgate_g, wproj_g = _mm_nn("in_proj", a, win_g, n_shard=in_shard, tn=in_shard,
                                            carry=("gather", [s_out, s_gate, s_proj]))
    cat = _conv_fwd(proj, conv_full, g_conv_out, w_conv, d)
    o, cat, (wup_g,) = _attn_fwd(proj, g_attn_out, cat, w_conv, [s_up])
    wout_f = wout_g.reshape(-1, wout_g.shape[-1])
    wgate_f = wgate_g.reshape(-1, wgate_g.shape[-1])
    h1, = _mm_nn("out_proj", cat, wout_f, epilogue=_ep_residual, extras=(xs,))
    mn = _rmsnorm_fwd("norm_mlp", h1, g_mlp)
    act, wdown_g = _mm_nn("mlp_up", mn, wup_g, n_shard=up_shard, epilogue=_ep_up, out_dtypes=(BF16,),
                          carry=("gather", [s_down]))
    wdown_f = wdown_g.reshape(-1, wdown_g.shape[-1])
    h2, = _mm_nn("mlp_down", act, wdown_f, epilogue=_ep_residual, extras=(h1,))
    n3 = _rmsnorm_fwd("norm_ple", h2, g_ple)
    gl, = _mm_nn("ple_gate", n3, wgate_f)
    pp = _ple_proj(ps, wproj_g)
    loss_part, dh3, dgl, dpp, dg_final = _ple_loss(h2, gl, pp, tgt, g_final.reshape(1, d))
    loss = lax.psum(loss_part[0, 0], ("x", "y", "c"))

    def slots(t2d):
        return t2d.reshape(N_DEV, -1, t2d.shape[-1])

    dw_proj = _d_ple_proj(ps, dpp, proj_shard)
    dw_gate = [slots(t) for t in _mm_tn("d_w_ple_gate", n3, dgl)]
    dh2, dh2b, dg_ple = _mm_nt_norm_bwd("d_norm_ple", dgl, wgate_f, h2, g_ple, dh3)
    du, gate_recv, proj_recv = _mm_nt("d_mlp_act", dh2b, wdown_f, epilogue=_ep_dact, out_dtypes=(BF16,),
                                      extras=(act,), carry=("scatter", [dw_gate[1], dw_proj[1]]))
    dw_down = [slots(t) for t in _mm_tn("d_w_down", act, dh2b)]
    dw_up = _mm_tn("d_w_up", mn, du, n_shard=up_shard)
    dh1, dh1b, dg_mlp = _mm_nt_norm_bwd("d_norm_mlp", du, wup_g, h1, g_mlp, dh2, k_shard=up_shard)
    dcat, = _mm_nt("d_cat", dh1b, wout_f)
    dw_out = [slots(t) for t in _mm_tn("d_w_out", cat, dh1b)]
    dproj, dg_attn, (up_recv, down_recv) = _attn_bwd(proj, o, dcat, g_attn_out, w_conv, [dw_up[1], dw_down[1]])
    dproj, dconv, dg_conv = _conv_bwd(proj, dcat, conv_full, g_conv_out, dproj, w_conv)
    *dw_in, out_recv = _mm_tn("d_w_in", a, dproj, n_shard=in_shard, tn=in_shard, carry=("scatter", [dw_out[1]]))
    grad_x, _, dg_mix, in_recv = _mm_nt_norm_bwd("d_norm_mix", dproj, win_g, xs, g_mix, dh1, k_shard=in_shard,
                                                 tk=in_shard, carry=("scatter", [dw_in[1]]))

    names = ["w_in", "w_out", "w_up", "w_down", "w_ple_gate", "w_ple_proj"]
    owns = [dw_in[0], dw_out[0], dw_up[0], dw_down[0], dw_gate[0], dw_proj[0]]
    recvs = [in_recv, out_recv, up_recv, down_recv, gate_recv, proj_recv]
    moments = [(m_w_in, v_w_in), (m_w_out, v_w_out), (m_w_up, v_w_up), (m_w_down, v_w_down),
               (m_w_ple_gate, v_w_ple_gate), (m_w_ple_proj, v_w_ple_proj)]
    big_out = {}
    for n, own, rc, wt, (mm, vv) in zip(names, owns, recvs, big, moments):
        big_out[n] = [t[None] for t in _adam_sharded("adam_" + n, own, rc, wt, mm[0], vv[0], place)]

    small_g = jnp.concatenate(
        [_rows(dg_mix[0]), _rows(dg_conv[0]), _rows(dg_attn[0]), _rows(dg_mlp[0]), _rows(dg_ple[0]),
         _rows(dg_final[0]), _rows(dconv.reshape(-1))], axis=0)
    n_gain_rows = small_g.shape[0] - CONV_K * w_conv // LANES
    pad_rows = (-small_g.shape[0]) % HALO
    small_g = _all_reduce_small(jnp.pad(small_g, ((0, pad_rows), (0, 0))))
    dconv_full = small_g[n_gain_rows:n_gain_rows + CONV_K * w_conv // LANES].reshape(CONV_K, w_conv)
    dconv_mine = lax.dynamic_slice(dconv_full, (0, my_slot * cw), (CONV_K, cw))

    def pack(vecs, conv_part):
        rows = [_rows(t.reshape(-1)) for t in vecs]
        rows.append(jnp.pad(conv_part, ((0, HALO - CONV_K), (0, LANES - cw))))
        return jnp.concatenate(rows, axis=0)

    gains = [g_mix, g_conv_out, g_attn_out, g_mlp, g_ple, g_final]
    gains_m = [m_g_mix, m_g_conv_out, m_g_attn_out, m_g_mlp, m_g_ple, m_g_final]
    gains_v = [v_g_mix, v_g_conv_out, v_g_attn_out, v_g_mlp, v_g_ple, v_g_final]
    gpack = jnp.concatenate([small_g[:n_gain_rows], jnp.pad(dconv_mine, ((0, HALO - CONV_K), (0, LANES - cw)))], axis=0)
    sd, sm, sv = _adam_small(pack(gains, conv_w[0]), gpack, pack(gains_m, m_conv_w[0]), pack(gains_v, v_conv_w[0]))

    def unpack(packed):
        out, r0 = [], 0
        for t in gains:
            nr = t.size // LANES
            out.append(packed[r0:r0 + nr].reshape(t.shape))
            r0 += nr
        out.append(packed[r0:r0 + CONV_K, :cw][None])
        return out

    sg_l, sd_l, sm_l, sv_l = unpack(gpack), unpack(sd), unpack(sm), unpack(sv)
    small_names = ["g_mix", "g_conv_out", "g_attn_out", "g_mlp", "g_ple", "g_final", "conv_w"]
    small_out = {n: [sg_l[i], sd_l[i], sm_l[i], sv_l[i]] for i, n in enumerate(small_names)}

    order = ["g_mix", "w_in", "conv_w", "g_conv_out", "g_attn_out", "w_out", "g_mlp", "w_up", "w_down",
             "g_ple", "w_ple_gate", "w_ple_proj", "g_final"]
    table = {**big_out, **small_out}
    outs = [loss, grad_x[None]]
    for kind in range(4):
        outs.extend(table[n][kind] for n in order)
    return tuple(outs)
```

```python
import jax
import jax.numpy as jnp
from jax import lax
from jax.experimental import pallas as pl
from jax.experimental.pallas import tpu as pltpu

F32 = jnp.float32
BF16 = jnp.bfloat16
EPS = 1e-6
HEAD_DIM = 64
LANES = 128
CONV_K = 3
ATTN_BLOCK = 256
HALO = 8
N_DEV = 8
MESH = pl.DeviceIdType.MESH
VMEM_LIMIT = 56 * 1024 * 1024

ADAM_LR = 0.001
ADAM_B1 = 0.9
ADAM_B2 = 0.999
ADAM_EPS = 1e-08
ADAM_WD = 0.01
ADAM_STEP = 10


def _pcall(body, **kw):
    return pl.pallas_call(body, **kw)


def _params(sem=None, **kw):
    return pltpu.CompilerParams(dimension_semantics=sem, vmem_limit_bytes=VMEM_LIMIT, **kw)


def _tile(dim, pref):
    t = min(dim, pref)
    while dim % t:
        t -= LANES
    assert t > 0, (dim, pref)
    return t


_NN = (((1,), (0,)), ((), ()))
_NT = (((1,), (1,)), ((), ()))
_TN = (((0,), (0,)), ((), ()))


def _ep_store(acc, outs):
    outs[0][...] = acc.astype(outs[0].dtype)


def _ep_both(acc, outs):
    outs[0][...] = acc
    outs[1][...] = acc.astype(BF16)


def _ep_residual(acc, res, outs):
    outs[0][...] = acc + res[...]


def _ep_up(acc, outs):
    outs[0][...] = jnp.square(jnp.maximum(acc, 0.0)).astype(BF16)


def _ep_dact(acc, act, outs):
    outs[0][...] = (acc * (2.0 * jnp.sqrt(act[...].astype(F32)))).astype(BF16)


def _ep_norm_bwd(acc, h, g, dres, outs):
    @pl.when(pl.program_id(0) == 0)
    def _():
        outs[2][...] = jnp.zeros_like(outs[2])

    hv = h[...]
    r = lax.rsqrt(jnp.mean(hv * hv, axis=-1, keepdims=True) + EPS)
    hn = hv * r
    outs[2][...] += jnp.sum(acc * hn, axis=0, keepdims=True)
    dhn = acc * g[...]
    dh = dres[...] + r * (dhn - hn * jnp.mean(dhn * hn, axis=-1, keepdims=True))
    outs[0][...] = dh
    outs[1][...] = dh.astype(BF16)


def _matmul(name, a, b, *, dims, grid, a_spec, b_spec, acc_shape, out_shapes, out_specs,
            epilogue=_ep_store, extras=(), extra_specs=(), carry=(), sequential=False):
    nk = grid[2]
    plan = _Carried(carry)
    n_ex, n_out, n_xc = len(extras), len(out_shapes), len(plan.inputs)
    n_sems = len(plan.sems)
    last = tuple(g - 1 for g in grid)

    def product(a_ref, b_ref):
        return lax.dot_general(a_ref[...].astype(BF16), b_ref[...].astype(BF16), dims,
                               preferred_element_type=F32)

    def body(a_ref, b_ref, *rest):
        ex, rest = rest[:n_ex], rest[n_ex:]
        partials, rest = rest[:n_xc], rest[n_xc:]
        outs, rest = rest[:n_out], rest[n_out:]
        received, rest = rest[:n_xc], rest[n_xc:]
        ids = [pl.program_id(axis) for axis in range(3)]
        if n_xc:
            @pl.when((ids[0] == 0) & (ids[1] == 0) & (ids[2] == 0))
            def _():
                for cp in plan.copies(partials, received, rest[-n_sems:]):
                    cp.start()

        if nk == 1:
            epilogue(product(a_ref, b_ref), *ex, outs)
        else:
            acc = rest[0]

            @pl.when(ids[2] == 0)
            def _():
                acc[...] = product(a_ref, b_ref)

            @pl.when(ids[2] > 0)
            def _():
                acc[...] += product(a_ref, b_ref)

            @pl.when(ids[2] == nk - 1)
            def _():
                epilogue(acc[...], *ex, outs)

        if n_xc:
            @pl.when((ids[0] == last[0]) & (ids[1] == last[1]) & (ids[2] == last[2]))
            def _():
                for cp in plan.copies(partials, received, rest[-n_sems:]):
                    cp.wait()

    anywhere = [pl.BlockSpec(memory_space=pl.ANY)] * n_xc
    return _pcall(
        body, name=name, grid=grid,
        in_specs=[a_spec, b_spec, *extra_specs, *anywhere],
        out_specs=[*out_specs, *anywhere],
        out_shape=[*out_shapes, *plan.out_shapes],
        scratch_shapes=([] if nk == 1 else [pltpu.VMEM(acc_shape, F32)]) + plan.sems,
        input_output_aliases=plan.aliases(2 + n_ex, n_out),
        compiler_params=_params(("arbitrary",) * 3 if n_xc or sequential else ("parallel", "parallel", "arbitrary")),
    )(a, b, *extras, *plan.inputs)


_NO_CARRY = ()


def _mm_nn(name, a, w, *, n_shard=None, epilogue=_ep_store, out_dtypes=(F32,), extras=(), carry=_NO_CARRY,
           tm=1024, tn=1024, tk=1024):
    m, kd = a.shape
    if n_shard is None:
        n = w.shape[1]
        tn = _tile(n, tn)
        tk = _tile(kd, tk)
        b_spec = pl.BlockSpec((tk, tn), lambda i, j, k: (k, j))
    else:
        n = N_DEV * n_shard
        tn = _tile(n_shard, tn)
        tk = _tile(kd, tk)
        per = n_shard // tn
        b_spec = pl.BlockSpec((None, tk, tn), lambda i, j, k: (j // per, k, j % per))
    tm = _tile(m, tm)
    o_spec = pl.BlockSpec((tm, tn), lambda i, j, k: (i, j))
    return _matmul(
        name, a, w, dims=_NN, grid=(m // tm, n // tn, kd // tk),
        a_spec=pl.BlockSpec((tm, tk), lambda i, j, k: (i, k)), b_spec=b_spec,
        acc_shape=(tm, tn),
        out_shapes=[jax.ShapeDtypeStruct((m, n), d) for d in out_dtypes],
        out_specs=[o_spec] * len(out_dtypes),
        epilogue=epilogue, extras=extras, extra_specs=[o_spec] * len(extras), carry=carry)


def _mm_nt(name, a, w, *, k_shard=None, epilogue=_ep_store, out_dtypes=(F32,), extras=(), carry=_NO_CARRY,
           tm=1024, tn=1024, tk=1024):
    m, kd = a.shape
    if k_shard is None:
        n = w.shape[0]
        tn = _tile(n, tn)
        tk = _tile(kd, tk)
        b_spec = pl.BlockSpec((tn, tk), lambda i, j, k: (j, k))
    else:
        n = w.shape[1]
        tn = _tile(n, tn)
        tk = _tile(k_shard, tk)
        per = k_shard // tk
        b_spec = pl.BlockSpec((None, tn, tk), lambda i, j, k: (k // per, j, k % per))
    tm = _tile(m, tm)
    o_spec = pl.BlockSpec((tm, tn), lambda i, j, k: (i, j))
    return _matmul(
        name, a, w, dims=_NT, grid=(m // tm, n // tn, kd // tk),
        a_spec=pl.BlockSpec((tm, tk), lambda i, j, k: (i, k)), b_spec=b_spec,
        acc_shape=(tm, tn),
        out_shapes=[jax.ShapeDtypeStruct((m, n), d) for d in out_dtypes],
        out_specs=[o_spec] * len(out_dtypes),
        epilogue=epilogue, extras=extras, extra_specs=[o_spec] * len(extras), carry=carry)


def _mm_nt_norm_bwd(name, a, w, h, g, dres, *, k_shard=None, carry=_NO_CARRY, tm=512, tk=1024):
    m, kd = a.shape
    n = h.shape[1]
    if k_shard is None:
        tk = _tile(kd, tk)
        b_spec = pl.BlockSpec((n, tk), lambda i, j, k: (0, k))
    else:
        tk = _tile(k_shard, tk)
        per = k_shard // tk
        b_spec = pl.BlockSpec((None, n, tk), lambda i, j, k: (k // per, 0, k % per))
    tm = _tile(m, tm)
    rows = pl.BlockSpec((tm, n), lambda i, j, k: (i, 0))
    vec = pl.BlockSpec((1, n), lambda i, j, k: (0, 0))
    return _matmul(
        name, a, w, dims=_NT, grid=(m // tm, 1, kd // tk),
        a_spec=pl.BlockSpec((tm, tk), lambda i, j, k: (i, k)), b_spec=b_spec, acc_shape=(tm, n),
        out_shapes=[jax.ShapeDtypeStruct((m, n), F32), jax.ShapeDtypeStruct((m, n), BF16),
                    jax.ShapeDtypeStruct((1, n), F32)],
        out_specs=[rows, rows, vec], epilogue=_ep_norm_bwd,
        extras=(h, g, dres), extra_specs=[rows, vec, rows], carry=carry, sequential=True)


def _mm_tn(name, a, b, *, n_shard=None, carry=_NO_CARRY, tm=1024, tn=1024, tk=1024):
    t, m = a.shape
    n = b.shape[1]
    tm = _tile(m, tm)
    tk = _tile(t, tk)
    if n_shard is None:
        tn = _tile(n, tn)
        o_spec = pl.BlockSpec((tm, tn), lambda i, j, k: (i, j))
        shape = (m, n)
    else:
        tn = _tile(n_shard, tn)
        per = n_shard // tn
        o_spec = pl.BlockSpec((None, tm, tn), lambda i, j, k: (j // per, i, j % per))
        shape = (N_DEV, m, n_shard)
    return _matmul(
        name, a, b, dims=_TN, grid=(m // tm, n // tn, t // tk),
        a_spec=pl.BlockSpec((tk, tm), lambda i, j, k: (k, i)),
        b_spec=pl.BlockSpec((tk, tn), lambda i, j, k: (k, j)),
        acc_shape=(tm, tn), epilogue=_ep_both, carry=carry,
        out_shapes=[jax.ShapeDtypeStruct(shape, F32), jax.ShapeDtypeStruct(shape, BF16)],
        out_specs=[o_spec, o_spec])


def _ple_proj(p, w_g, tm=1024):
    s, kd = p.shape
    ns = w_g.shape[2]
    tm = _tile(s, tm)

    def body(p_ref, w_ref, o_ref):
        pv = p_ref[...].astype(BF16)
        for j in range(N_DEV):
            o_ref[:, j * ns:(j + 1) * ns] = jnp.dot(pv, w_ref[j], preferred_element_type=F32)

    return _pcall(body, name="ple_proj", grid=(s // tm,),
                  in_specs=[pl.BlockSpec((tm, kd), lambda i: (i, 0)),
                            pl.BlockSpec((N_DEV, kd, ns), lambda i: (0, 0, 0))],
                  out_specs=pl.BlockSpec((tm, N_DEV * ns), lambda i: (i, 0)),
                  out_shape=jax.ShapeDtypeStruct((s, N_DEV * ns), F32),
                  compiler_params=_params(("parallel",)))(p, w_g)


def _d_ple_proj(p, dpp, ns, tk=1024):
    s, kd = p.shape
    tk = _tile(s, tk)
    nk = s // tk

    def body(p_ref, d_ref, of_ref, ob_ref, acc):
        k = pl.program_id(0)

        @pl.when(k == 0)
        def _():
            acc[...] = jnp.zeros_like(acc)

        pv = p_ref[...].astype(BF16)
        for j in range(N_DEV):
            acc[j] += lax.dot_general(pv, d_ref[:, j * ns:(j + 1) * ns], _TN, preferred_element_type=F32)

        @pl.when(k == nk - 1)
        def _():
            of_ref[...] = acc[...]
            ob_ref[...] = acc[...].astype(BF16)

    whole = pl.BlockSpec((N_DEV, kd, ns), lambda k: (0, 0, 0))
    return _pcall(body, name="d_w_ple_proj", grid=(nk,),
                  in_specs=[pl.BlockSpec((tk, kd), lambda k: (k, 0)),
                            pl.BlockSpec((tk, N_DEV * ns), lambda k: (k, 0))],
                  out_specs=[whole, whole],
                  out_shape=[jax.ShapeDtypeStruct((N_DEV, kd, ns), F32), jax.ShapeDtypeStruct((N_DEV, kd, ns), BF16)],
                  scratch_shapes=[pltpu.VMEM((N_DEV, kd, ns), F32)],
                  compiler_params=_params(("arbitrary",)))(p, dpp)


def _row_spec(tr, d):
    return pl.BlockSpec((tr, d), lambda i: (i, 0))


def _vec_spec(d):
    return pl.BlockSpec((1, d), lambda i: (0, 0))


def _rmsnorm_fwd(name, x, g, tr=512):
    s, d = x.shape
    tr = _tile(s, tr)

    def body(x_ref, g_ref, o_ref):
        xv = x_ref[...]
        r = lax.rsqrt(jnp.mean(xv * xv, axis=-1, keepdims=True) + EPS)
        o_ref[...] = (xv * r * g_ref[...]).astype(BF16)

    return _pcall(body, name=name, grid=(s // tr,),
                  in_specs=[_row_spec(tr, d), _vec_spec(d)], out_specs=_row_spec(tr, d),
                  out_shape=jax.ShapeDtypeStruct((s, d), BF16),
                  compiler_params=_params(("parallel",)))(x, g)


def _ple_loss(h2, gl, pp, tgt, g_final, tr=512):
    s, d = h2.shape
    tr = _tile(s, tr)

    def body(h2_ref, gl_ref, pp_ref, t_ref, g_ref, loss_ref, dh3_ref, dgl_ref, dpp_ref, dg_ref):
        @pl.when(pl.program_id(0) == 0)
        def _():
            dg_ref[...] = jnp.zeros_like(dg_ref)
            loss_ref[...] = jnp.zeros_like(loss_ref)

        gate = jax.nn.sigmoid(gl_ref[...])
        ppv = pp_ref[...]
        h3 = h2_ref[...] + gate * ppv
        r = lax.rsqrt(jnp.mean(h3 * h3, axis=-1, keepdims=True) + EPS)
        hn = h3 * r
        gv = g_ref[...]
        diff = hn * gv - t_ref[...]
        row = jnp.mean(diff * diff, axis=-1, keepdims=True)
        loss_ref[...] += 0.5 * jnp.sum(row, axis=0, keepdims=True)
        dy = diff * (1.0 / d)
        dg_ref[...] += jnp.sum(dy * hn, axis=0, keepdims=True)
        dhn = dy * gv
        dh3 = r * (dhn - hn * jnp.mean(dhn * hn, axis=-1, keepdims=True))
        dh3_ref[...] = dh3
        dgl_ref[...] = (dh3 * ppv * gate * (1.0 - gate)).astype(BF16)
        dpp_ref[...] = (dh3 * gate).astype(BF16)

    return _pcall(body, name="ple_loss", grid=(s // tr,),
                  in_specs=[_row_spec(tr, d)] * 4 + [_vec_spec(d)],
                  out_specs=[_vec_spec(LANES), _row_spec(tr, d), _row_spec(tr, d), _row_spec(tr, d), _vec_spec(d)],
                  out_shape=[jax.ShapeDtypeStruct((1, LANES), F32), jax.ShapeDtypeStruct((s, d), F32),
                             jax.ShapeDtypeStruct((s, d), BF16), jax.ShapeDtypeStruct((s, d), BF16),
                             jax.ShapeDtypeStruct((1, d), F32)],
                  compiler_params=_params(("arbitrary",)))(h2, gl, pp, tgt, g_final)


def _low_half():
    return lax.broadcasted_iota(jnp.int32, (1, LANES), 1) < HEAD_DIM


def _half_mean(v, low):
    s_lo = jnp.sum(jnp.where(low, v, 0.0), axis=-1, keepdims=True)
    s_hi = jnp.sum(jnp.where(low, 0.0, v), axis=-1, keepdims=True)
    return jnp.where(low, s_lo, s_hi) * (1.0 / HEAD_DIM)


def _head_norm_bwd(val, dout, g, low):
    r = lax.rsqrt(_half_mean(val * val, low) + EPS)
    vn = val * r
    dvn = dout * g
    return r * (dvn - vn * _half_mean(dvn * vn, low)), dout * vn


def _conv_taps(vv_ext, w_ref, rows):
    v0 = vv_ext[HALO:]
    v1 = pltpu.roll(vv_ext, 1, 0)[HALO:]
    v2 = pltpu.roll(vv_ext, 2, 0)[HALO:]
    del rows
    return w_ref[2:3, :] * v0 + w_ref[1:2, :] * v1 + w_ref[0:1, :] * v2, (v0, v1, v2)


def _conv_fwd(proj, conv_w, g_conv, w_conv, d_model, tr=512):
    s = proj.shape[0]
    tr = _tile(s, tr)
    hb = tr // HALO

    def main(part):
        return pl.BlockSpec((tr, w_conv), lambda i: (i, part))

    def prev(part):
        return pl.BlockSpec((HALO, w_conv), lambda i: (jnp.maximum(i * hb - 1, 0), part))

    def body(cb_ref, cc_ref, cu_ref, ccp_ref, cup_ref, w_ref, g_ref, o_ref):
        i = pl.program_id(0)
        low = _low_half()
        for j in range(w_conv // LANES):
            cols = slice(j * LANES, (j + 1) * LANES)
            vv_prev = jnp.where(i > 0, ccp_ref[:, cols] * cup_ref[:, cols], 0.0)
            vv_ext = jnp.concatenate([vv_prev, cc_ref[:, cols] * cu_ref[:, cols]], axis=0)
            y, _ = _conv_taps(vv_ext, w_ref.at[:, cols], tr)
            co = cb_ref[:, cols] * y
            r = lax.rsqrt(_half_mean(co * co, low) + EPS)
            o_ref[:, cols] = (co * r * g_ref[:, cols]).astype(BF16)

    return _pcall(
        body, name="conv_fwd", grid=(s // tr,),
        in_specs=[main(0), main(1), main(2), prev(1), prev(2),
                  pl.BlockSpec((CONV_K, w_conv), lambda i: (0, 0)),
                  pl.BlockSpec((1, w_conv), lambda i: (0, 0))],
        out_specs=pl.BlockSpec((tr, w_conv), lambda i: (i, 0)),
        out_shape=jax.ShapeDtypeStruct((s, d_model), BF16),
        compiler_params=_params(("parallel",)),
    )(proj, proj, proj, proj, proj, conv_w, g_conv)


def _conv_bwd(proj, dcat, conv_w, g_conv, dproj, w_conv, tr=512):
    s = proj.shape[0]
    tr = _tile(s, tr)
    hb = tr // HALO
    last = s // HALO - 1
    nt = s // tr

    def main(part):
        return pl.BlockSpec((tr, w_conv), lambda i: (i, part))

    def prev(part):
        return pl.BlockSpec((HALO, w_conv), lambda i: (jnp.maximum(i * hb - 1, 0), part))

    def nxt(part):
        return pl.BlockSpec((HALO, w_conv), lambda i: (jnp.minimum((i + 1) * hb, last), part))

    def body(cb_ref, cc_ref, cu_ref, dc_ref, ccp_ref, cup_ref, cbn_ref, ccn_ref, cun_ref, dcn_ref,
             w_ref, g_ref, dproj_in, dproj_ref, dw_ref, dg_ref):
        del dproj_in
        i = pl.program_id(0)

        @pl.when(i == 0)
        def _():
            dw_ref[...] = jnp.zeros_like(dw_ref)
            dg_ref[...] = jnp.zeros_like(dg_ref)

        low = _low_half()
        n_ext = tr + HALO
        rowid = lax.broadcasted_iota(jnp.int32, (n_ext, 1), 0)
        for j in range(w_conv // LANES):
            cols = slice(j * LANES, (j + 1) * LANES)
            wj = w_ref.at[:, cols]
            cc, cu = cc_ref[:, cols], cu_ref[:, cols]
            vv_prev = jnp.where(i > 0, ccp_ref[:, cols] * cup_ref[:, cols], 0.0)
            vv_ext = jnp.concatenate([vv_prev, cc * cu, ccn_ref[:, cols] * cun_ref[:, cols]], axis=0)
            y_ext, (v0, v1, v2) = _conv_taps(vv_ext, wj, n_ext)
            cb_ext = jnp.concatenate([cb_ref[:, cols], cbn_ref[:, cols]], axis=0)
            dc_ext = jnp.concatenate([dc_ref[:, cols], dcn_ref[:, cols]], axis=0)
            dco, dgn = _head_norm_bwd(cb_ext * y_ext, dc_ext, g_ref[:, cols], low)
            dyc = jnp.where((rowid < tr) | (i < nt - 1), dco * cb_ext, 0.0)
            dvv = (wj[2:3, :] * dyc[:tr] + wj[1:2, :] * pltpu.roll(dyc, n_ext - 1, 0)[:tr]
                   + wj[0:1, :] * pltpu.roll(dyc, n_ext - 2, 0)[:tr])
            dproj_ref[:, cols] = (dco[:tr] * y_ext[:tr]).astype(BF16)
            dproj_ref[:, w_conv + j * LANES:w_conv + (j + 1) * LANES] = (dvv * cu).astype(BF16)
            dproj_ref[:, 2 * w_conv + j * LANES:2 * w_conv + (j + 1) * LANES] = (dvv * cc).astype(BF16)
            dyt = dyc[:tr]
            for tap, shifted in enumerate((v2, v1, v0)):
                dw_ref[tap:tap + 1, cols] += jnp.sum(dyt * shifted[:tr], axis=0, keepdims=True)
            dg_ref[:, cols] += jnp.sum(dgn[:tr], axis=0, keepdims=True)

    n_cols = dproj.shape[1]
    return _pcall(
        body, name="conv_bwd", grid=(nt,),
        in_specs=[main(0), main(1), main(2), main(0),
                  prev(1), prev(2), nxt(0), nxt(1), nxt(2), nxt(0),
                  pl.BlockSpec((CONV_K, w_conv), lambda i: (0, 0)),
                  pl.BlockSpec((1, w_conv), lambda i: (0, 0)),
                  pl.BlockSpec(memory_space=pl.ANY)],
        out_specs=[pl.BlockSpec((tr, 3 * w_conv), lambda i: (i, 0)),
                   pl.BlockSpec((CONV_K, w_conv), lambda i: (0, 0)),
                   pl.BlockSpec((1, w_conv), lambda i: (0, 0))],
        out_shape=[jax.ShapeDtypeStruct((s, n_cols), BF16),
                   jax.ShapeDtypeStruct((CONV_K, w_conv), F32),
                   jax.ShapeDtypeStruct((1, w_conv), F32)],
        input_output_aliases={12: 0},
        compiler_params=_params(("arbitrary",)),
    )(proj, proj, proj, dcat, proj, proj, proj, proj, proj, dcat, conv_w, g_conv, dproj)


STRIP = 16

ALL_CHAINS = (0, 1, 2, 3)
UPPER_CHAINS = (2, 3)


RUN_FLOOR = -104.0


def _any_weight_left(run_s):
    return (jnp.max(run_s[...]) > RUN_FLOOR).astype(jnp.int32)


def _chains(low):
    return [(2 * half + h, half, msk) for half in range(2)
            for h, msk in enumerate((low, jnp.logical_not(low)))]


def _suffix_operator(t):
    r = lax.broadcasted_iota(jnp.int32, (2 * t, t), 0)
    c = lax.broadcasted_iota(jnp.int32, (2 * t, t), 1)
    return jnp.where((r > c) & ((r < t) | (r - t > c)), 1.0, 0.0).astype(BF16)


def _strips(t):
    return [(i, slice(i * STRIP, (i + 1) * STRIP)) for i in range(t // STRIP)]


def _strip_mask(i, t):
    r = lax.broadcasted_iota(jnp.int32, (STRIP, t), 0) + i * STRIP
    c = lax.broadcasted_iota(jnp.int32, (STRIP, t), 1)
    return r > c


def _store_split(ref, rows, val, t):
    hi = val.astype(BF16)
    ref[rows, 0:t] = hi
    ref[rows, t:2 * t] = (val - hi.astype(F32)).astype(BF16)


def _sb_scores(z_s, split_s, zl_s, tot_s, keep_s, t, diag):
    for i, rows in _strips(t):
        z = z_s[rows, :]
        log_beta = jnp.minimum(z, 0.0) - jnp.log(1.0 + jnp.exp(-jnp.abs(z)))
        log_keep = log_beta - z
        if diag:
            log_keep = jnp.where(_strip_mask(i, t), log_keep, 0.0)
        _store_split(split_s, rows, log_keep, t)
        zl_s[rows, :] = log_beta
        tot_s[rows, :] = _row_sum(log_keep)
        if keep_s is not None:
            keep_s[rows, :] = jnp.exp(log_keep)


def _row_sum(v):
    return jnp.broadcast_to(jnp.sum(v, axis=-1, keepdims=True), (v.shape[0], LANES))


def _wide(r, t):
    return jnp.concatenate([r] * (t // LANES), axis=1)


def _sb_weights(zl_s, suf_s, run_s, tot_s, a_s, t, diag, da_s=None, glog_s=None, gsplit_s=None, gtot_s=None):
    for i, rows in _strips(t):
        run = run_s[rows, :]
        a = jnp.exp(zl_s[rows, :] + suf_s[rows, :] + _wide(run, t))
        if diag:
            a = jnp.where(_strip_mask(i, t), a, 0.0)
        ab = a.astype(BF16)
        a_s[rows, :] = ab
        run_s[rows, :] = run + tot_s[rows, :]
        if da_s is not None:
            glog = ab.astype(F32) * da_s[rows, :]
            glog_s[rows, :] = glog
            _store_split(gsplit_s, rows, glog, t)
            gtot_s[rows, :] = _row_sum(glog)


def _sb_dscores(glog_s, cum_s, rest_s, gtot_s, keep_s, dz_s, t, diag):
    for i, rows in _strips(t):
        glog = glog_s[rows, :]
        rest = rest_s[rows, :]
        from_here = _wide(rest, t) - cum_s[rows, :]
        before = from_here - glog
        dz = from_here * keep_s[rows, :] - before
        if diag:
            dz = jnp.where(_strip_mask(i, t), dz, 0.0)
        dz_s[rows, :] = dz.astype(BF16)
        rest_s[rows, :] = rest - gtot_s[rows, :]


def _attn_fwd(proj, g_attn, cat, w_conv, carry, t=ATTN_BLOCK):
    s = proj.shape[0]
    w_attn = g_attn.shape[1]
    nh = w_attn // LANES
    t = _tile(s, t)
    tq = 2 * t
    nq = s // tq
    q0 = 3 * w_conv // LANES
    scale = HEAD_DIM ** -0.5
    plan = _Carried(carry)
    nw = len(plan.inputs)

    def body(q_ref, k_ref, v_ref, g_ref, cat_in, *rest):
        staged_refs, rest = rest[:nw], rest[nw:]
        o_ref, cat_ref = rest[:2]
        gathered_refs, rest = rest[2:2 + nw], rest[2 + nw:]
        kb, vb, tri_s, qm_s, z_s, split_s, zl_s, suf_s, a_s, run_s, tot_s, acc_s = rest[:12]
        gather_sems = rest[12:]
        del cat_in
        qi = pl.program_id(1)

        @pl.when((pl.program_id(0) == 0) & (qi == 0))
        def _():
            for cp in plan.copies(staged_refs, gathered_refs, gather_sems):
                cp.start()

        @pl.when(qi == 0)
        def _():
            kb[...] = k_ref[...].astype(BF16)
            vb[...] = v_ref[...].astype(BF16)
            tri_s[...] = _suffix_operator(t)

        low = _low_half()
        for c, half, msk in _chains(low):
            qm_s[c] = jnp.where(msk, q_ref[half * t:(half + 1) * t, :] * scale, 0.0).astype(BF16)
            run_s[c] = jnp.zeros((t, LANES), F32)
            acc_s[c] = jnp.zeros((t, LANES), F32)

        def key_rows(kblk):
            return pl.ds(pl.multiple_of(kblk * t, t), t)

        def scores_matmul(kblk, chains):
            ks = kb[key_rows(kblk), :]
            for c in chains:
                z_s[c] = lax.dot_general(qm_s[c], ks, _NT, preferred_element_type=F32)

        def front(modes, nxt, prev=None):
            for c, diag in modes:
                _sb_scores(z_s.at[c], split_s.at[c], zl_s.at[c], tot_s.at[c], None, t, diag)
                suf_s[c] = jnp.dot(split_s[c], tri_s[...], preferred_element_type=F32)
            if prev is not None:
                tail(*prev)
            scores_matmul(nxt, ALL_CHAINS)
            for c, diag in modes:
                _sb_weights(zl_s.at[c], suf_s.at[c], run_s.at[c], tot_s.at[c], a_s.at[c], t, diag)

        def tail(kblk, chains):
            vs = vb[key_rows(kblk), :]
            for c in chains:
                acc_s[c] += jnp.dot(a_s[c], vs, preferred_element_type=F32)

        top = 2 * qi + 1
        scores_matmul(top, UPPER_CHAINS)
        front([(c, True) for c in UPPER_CHAINS], top - 1)
        front([(c, c not in UPPER_CHAINS) for c in ALL_CHAINS], jnp.maximum(top - 2, 0),
              prev=(top, UPPER_CHAINS))

        def loop(state):
            it = state[0]
            cur = top - 2 - it
            front([(c, False) for c in ALL_CHAINS], jnp.maximum(cur - 1, 0), prev=(cur + 1, ALL_CHAINS))
            return it + 1, _any_weight_left(run_s)

        done, _ = lax.while_loop(lambda state: (state[0] < top - 1) & (state[1] > 0), loop,
                                 (jnp.int32(0), jnp.int32(1)))
        tail(top - 1 - done, ALL_CHAINS)
        for half in range(2):
            rows = slice(half * t, (half + 1) * t)
            o = jnp.where(low, acc_s[2 * half], acc_s[2 * half + 1])
            o_ref[rows, :] = o
            r = lax.rsqrt(_half_mean(o * o, low) + EPS)
            cat_ref[rows, :] = (o * r * g_ref[...]).astype(BF16)

        @pl.when((pl.program_id(0) == nh - 1) & (qi == nq - 1))
        def _():
            for cp in plan.copies(staged_refs, gathered_refs, gather_sems):
                cp.wait()

    whole = lambda col0: pl.BlockSpec((s, LANES), lambda h, i: (0, col0 + h))
    n_ch = len(ALL_CHAINS)
    res = _pcall(
        body, name="attn_fwd", grid=(nh, nq),
        in_specs=[pl.BlockSpec((tq, LANES), lambda h, i: (i, q0 + h)),
                  whole(q0 + nh), whole(q0 + 2 * nh),
                  pl.BlockSpec((1, LANES), lambda h, i: (0, h)),
                  pl.BlockSpec(memory_space=pl.ANY)] + [pl.BlockSpec(memory_space=pl.ANY)] * nw,
        out_specs=[pl.BlockSpec((tq, LANES), lambda h, i: (i, h)),
                   pl.BlockSpec((tq, LANES), lambda h, i: (i, w_conv // LANES + h))]
        + [pl.BlockSpec(memory_space=pl.ANY)] * nw,
        out_shape=[jax.ShapeDtypeStruct((s, w_attn), F32),
                   jax.ShapeDtypeStruct(cat.shape, BF16)] + plan.out_shapes,
        scratch_shapes=[pltpu.VMEM((s, LANES), BF16), pltpu.VMEM((s, LANES), BF16),
                        pltpu.VMEM((2 * t, t), BF16),
                        pltpu.VMEM((n_ch, t, LANES), BF16),
                        pltpu.VMEM((n_ch, t, t), F32),
                        pltpu.VMEM((n_ch, t, 2 * t), BF16),
                        pltpu.VMEM((n_ch, t, t), F32),
                        pltpu.VMEM((n_ch, t, t), F32),
                        pltpu.VMEM((n_ch, t, t), BF16),
                        pltpu.VMEM((n_ch, t, LANES), F32),
                        pltpu.VMEM((n_ch, t, LANES), F32),
                        pltpu.VMEM((n_ch, t, LANES), F32)]
        + plan.sems,
        input_output_aliases={4: 1, **plan.aliases(5, 2)},
        compiler_params=_params(("arbitrary", "arbitrary")),
    )(proj, proj, proj, g_attn, cat, *plan.inputs)
    return res[0], res[1], res[2:]


def _attn_bwd(proj, o, dcat, g_attn, w_conv, carry, t=ATTN_BLOCK):
    s, n_cols = proj.shape
    w_attn = g_attn.shape[1]
    nh = w_attn // LANES
    t = _tile(s, t)
    tq = 2 * t
    nq = s // tq
    q0 = 3 * w_conv // LANES
    scale = HEAD_DIM ** -0.5
    plan = _Carried(carry)
    nw = len(plan.inputs)

    def body(q_ref, k_ref, v_ref, o_ref, do_ref, g_ref, *rest):
        partial_refs, rest = rest[:nw], rest[nw:]
        dproj_ref, dg_ref = rest[:2]
        received_refs, rest = rest[2:2 + nw], rest[2 + nw:]
        (kb, vb, dkt_acc, dvt_acc, stash, tri_s, qm_s, dom_s, qt_s, dot_s, z_s, da_s, split_s, zl_s,
         keep_s, suf_s, a_s, glog_s, gsplit_s, cum_s, dz_s, run_s, tot_s, rest_s, gtot_s, dq_s) = rest[:26]
        scatter_sems = rest[26:]
        step_i = pl.program_id(1)
        which = pl.program_id(2)
        qi = nq - 1 - step_i
        head_pair = pl.program_id(0)

        @pl.when((head_pair == 0) & (step_i == 0) & (which == 0))
        def _():
            for cp in plan.copies(partial_refs, received_refs, scatter_sems):
                cp.start()

        @pl.when((head_pair == nh - 1) & (step_i == nq - 1) & (which == 2))
        def _():
            for cp in plan.copies(partial_refs, received_refs, scatter_sems):
                cp.wait()

        @pl.when(which == 0)
        def _():
            @pl.when(step_i == 0)
            def _():
                kb[...] = k_ref[...].astype(BF16)
                vb[...] = v_ref[...].astype(BF16)
                tri_s[...] = _suffix_operator(t)
                dkt_acc[...] = jnp.zeros_like(dkt_acc)
                dvt_acc[...] = jnp.zeros_like(dvt_acc)
                dg_ref[...] = jnp.zeros_like(dg_ref)

            low = _low_half()
            gv = g_ref[...]
            for half in range(2):
                rows = slice(half * t, (half + 1) * t)
                q = q_ref[rows, :] * scale
                ov = o_ref[rows, :]
                d_o, dgn = _head_norm_bwd(ov, do_ref[rows, :], gv, low)
                dg_ref[...] += jnp.sum(dgn, axis=0, keepdims=True)
                for h, msk in enumerate((low, jnp.logical_not(low))):
                    c = 2 * half + h
                    qh = jnp.where(msk, q, 0.0)
                    doh = jnp.where(msk, d_o, 0.0)
                    dom = doh.astype(BF16)
                    qm_s[c] = qh.astype(BF16)
                    dom_s[c] = dom
                    qt_s[c] = qh.T.astype(BF16)
                    dot_s[c] = doh.T.astype(BF16)
                    rest_s[c] = _row_sum(dom.astype(F32) * ov)
                    run_s[c] = jnp.zeros((t, LANES), F32)
                    dq_s[c] = jnp.zeros((t, LANES), F32)

            def key_rows(kblk):
                return pl.ds(pl.multiple_of(kblk * t, t), t)

            def scores_matmul(kblk, chains):
                ks = kb[key_rows(kblk), :]
                for c in chains:
                    z_s[c] = lax.dot_general(qm_s[c], ks, _NT, preferred_element_type=F32)

            def da_matmul(kblk, chains):
                vs = vb[key_rows(kblk), :]
                for c in chains:
                    da_s[c] = lax.dot_general(dom_s[c], vs, _NT, preferred_element_type=F32)

            def front(modes, nxt, prev=None):
                if prev is not None:
                    tail(*prev)
                for c, diag in modes:
                    _sb_scores(z_s.at[c], split_s.at[c], zl_s.at[c], tot_s.at[c], keep_s.at[c], t, diag)
                    suf_s[c] = jnp.dot(split_s[c], tri_s[...], preferred_element_type=F32)
                scores_matmul(nxt, ALL_CHAINS)
                for c, diag in modes:
                    _sb_weights(zl_s.at[c], suf_s.at[c], run_s.at[c], tot_s.at[c], a_s.at[c], t, diag,
                                da_s.at[c], glog_s.at[c], gsplit_s.at[c], gtot_s.at[c])
                    cum_s[c] = jnp.dot(gsplit_s[c], tri_s[...], preferred_element_type=F32)
                da_matmul(nxt, ALL_CHAINS)
                for c, diag in modes:
                    _sb_dscores(glog_s.at[c], cum_s.at[c], rest_s.at[c], gtot_s.at[c], keep_s.at[c],
                                dz_s.at[c], t, diag)

            def tail(kblk, chains):
                ks = kb[key_rows(kblk), :]
                dkt = dkt_acc[kblk]
                dvt = dvt_acc[kblk]
                for c in chains:
                    dq_s[c] += jnp.dot(dz_s[c], ks, preferred_element_type=F32)
                    dkt = dkt + jnp.dot(qt_s[c], dz_s[c], preferred_element_type=F32)
                    dvt = dvt + jnp.dot(dot_s[c], a_s[c], preferred_element_type=F32)
                dkt_acc[kblk] = dkt
                dvt_acc[kblk] = dvt

            top = 2 * qi + 1
            scores_matmul(top, UPPER_CHAINS)
            da_matmul(top, UPPER_CHAINS)
            front([(c, True) for c in UPPER_CHAINS], top - 1)
            front([(c, c not in UPPER_CHAINS) for c in ALL_CHAINS], jnp.maximum(top - 2, 0),
                  prev=(top, UPPER_CHAINS))

            def loop(state):
                it = state[0]
                cur = top - 2 - it
                front([(c, False) for c in ALL_CHAINS], jnp.maximum(cur - 1, 0), prev=(cur + 1, ALL_CHAINS))
                return it + 1, _any_weight_left(run_s)

            done, _ = lax.while_loop(lambda state: (state[0] < top - 1) & (state[1] > 0), loop,
                                     (jnp.int32(0), jnp.int32(1)))
            tail(top - 1 - done, ALL_CHAINS)
            for half in range(2):
                rows = slice(half * t, (half + 1) * t)
                stash[0, rows, :] = (jnp.where(low, dq_s[2 * half], dq_s[2 * half + 1]) * scale).astype(BF16)
                stash[1, rows, :] = dkt_acc[2 * qi + half].T.astype(BF16)
                stash[2, rows, :] = dvt_acc[2 * qi + half].T.astype(BF16)

        dproj_ref[...] = stash[which]

    whole = lambda col0: pl.BlockSpec((s, LANES), lambda h, i, w: (0, col0 + h))
    blk = lambda col0: pl.BlockSpec((tq, LANES), lambda h, i, w: (nq - 1 - i, col0 + h))
    n_ch = len(ALL_CHAINS)
    res = _pcall(
        body, name="attn_bwd", grid=(nh, nq, 3),
        in_specs=[blk(q0), whole(q0 + nh), whole(q0 + 2 * nh), blk(0), blk(w_conv // LANES),
                  pl.BlockSpec((1, LANES), lambda h, i, w: (0, h))] + [pl.BlockSpec(memory_space=pl.ANY)] * nw,
        out_specs=[pl.BlockSpec((tq, LANES), lambda h, i, w: (nq - 1 - i, q0 + w * nh + h)),
                   pl.BlockSpec((1, LANES), lambda h, i, w: (0, h))] + [pl.BlockSpec(memory_space=pl.ANY)] * nw,
        out_shape=[jax.ShapeDtypeStruct((s, n_cols), BF16), jax.ShapeDtypeStruct((1, w_attn), F32)]
        + plan.out_shapes,
        scratch_shapes=[pltpu.VMEM((s, LANES), BF16), pltpu.VMEM((s, LANES), BF16),
                        pltpu.VMEM((s // t, LANES, t), F32),
                        pltpu.VMEM((s // t, LANES, t), F32),
                        pltpu.VMEM((3, tq, LANES), BF16),
                        pltpu.VMEM((2 * t, t), BF16),
                        pltpu.VMEM((n_ch, t, LANES), BF16),
                        pltpu.VMEM((n_ch, t, LANES), BF16),
                        pltpu.VMEM((n_ch, LANES, t), BF16),
                        pltpu.VMEM((n_ch, LANES, t), BF16),
                        pltpu.VMEM((n_ch, t, t), F32),
                        pltpu.VMEM((n_ch, t, t), F32),
                        pltpu.VMEM((n_ch, t, 2 * t), BF16),
                        pltpu.VMEM((n_ch, t, t), F32),
                        pltpu.VMEM((n_ch, t, t), F32),
                        pltpu.VMEM((n_ch, t, t), F32),
                        pltpu.VMEM((n_ch, t, t), BF16),
                        pltpu.VMEM((n_ch, t, t), F32),
                        pltpu.VMEM((n_ch, t, 2 * t), BF16),
                        pltpu.VMEM((n_ch, t, t), F32),
                        pltpu.VMEM((n_ch, t, t), BF16),
                        pltpu.VMEM((n_ch, t, LANES), F32),
                        pltpu.VMEM((n_ch, t, LANES), F32),
                        pltpu.VMEM((n_ch, t, LANES), F32),
                        pltpu.VMEM((n_ch, t, LANES), F32),
                        pltpu.VMEM((n_ch, t, LANES), F32)]
        + plan.sems,
        input_output_aliases=plan.aliases(6, 2),
        compiler_params=_params(("arbitrary", "arbitrary", "arbitrary")),
    )(proj, proj, proj, o, dcat, g_attn, *plan.inputs)
    return res[0], res[1], res[2:]


def _place():
    return lax.axis_index("x"), lax.axis_index("y"), lax.axis_index("c")


def _other_chips(x, y):
    return [(1 - x, y), (x, 1 - y), (1 - x, 1 - y)]


def _slot(px, py, pc):
    return 4 * px + 2 * py + pc


def _all_gather(shards, out_dtypes):
    nw = len(shards)

    def body(*refs):
        ins, outs, stage = refs[:nw], refs[nw:2 * nw], refs[2 * nw:3 * nw]
        send_sems, recv_sems, local_sems = refs[3 * nw:]
        x, y, c = _place()
        me, sibling = (x, y, c), (x, y, 1 - c)
        chips = _other_chips(x, y)

        def copy(w, k, block, to, src=None):
            dst = outs[w].at[_slot(*block)]
            return pltpu.make_async_remote_copy(
                src_ref=dst if src is None else src, dst_ref=dst,
                send_sem=send_sems.at[w * 7 + k], recv_sem=recv_sems.at[w * 7 + k],
                device_id=to, device_id_type=MESH)

        started = []
        local = []
        for w in range(nw):
            stage[w][...] = ins[w][...].astype(stage[w].dtype)
            cp = pltpu.make_async_copy(stage[w], outs[w].at[_slot(*me)], local_sems.at[w])
            cp.start()
            local.append(cp)
            started.append(copy(w, 0, me, sibling, src=stage[w]))
            started[-1].start()
            for j, chip in enumerate(chips):
                started.append(copy(w, 1 + j, me, (*chip, c), src=stage[w]))
                started[-1].start()
        for j, chip in enumerate(chips):
            for w in range(nw):
                copy(w, 1 + j, (*chip, c), me).wait_recv()
                started.append(copy(w, 4 + j, (*chip, c), sibling))
                started[-1].start()
        for w in range(nw):
            copy(w, 0, sibling, me).wait_recv()
            for j, chip in enumerate(chips):
                copy(w, 4 + j, (*chip, 1 - c), me).wait_recv()
        for cp in started:
            cp.wait_send()
        for cp in local:
            cp.wait()

    return _pcall(
        body, name="all_gather_weights",
        in_specs=[pl.BlockSpec(memory_space=pltpu.VMEM)] * nw,
        out_specs=[pl.BlockSpec(memory_space=pl.ANY)] * nw,
        out_shape=[jax.ShapeDtypeStruct((N_DEV, *a.shape), d) for a, d in zip(shards, out_dtypes)],
        scratch_shapes=[pltpu.VMEM(a.shape, d) for a, d in zip(shards, out_dtypes)]
        + [pltpu.SemaphoreType.DMA((7 * nw,)), pltpu.SemaphoreType.DMA((7 * nw,)),
           pltpu.SemaphoreType.DMA((nw,))],
        compiler_params=_params(),
    )(*shards)


N_PEERS = N_DEV - 1


def _peer(k):
    x, y, c = _place()
    return (x ^ (k >> 2), y ^ ((k >> 1) & 1), c ^ (k & 1))


def _remote(src, dst, sems, index, to):
    return pltpu.make_async_remote_copy(src_ref=src, dst_ref=dst, send_sem=sems[0].at[index],
                                        recv_sem=sems[1].at[index], device_id=to, device_id_type=MESH)


def _gather_out_copies(staged, gathered, sems):
    x, y, c = _place()
    me = _slot(x, y, c)
    targets = [(x, y, 1 - c)] + [(*chip, c) for chip in _other_chips(x, y)]
    copies = []
    for w, (src, dst) in enumerate(zip(staged, gathered)):
        copies.append(pltpu.make_async_copy(src, dst.at[me], sems[2].at[w]))
        copies += [_remote(src, dst.at[me], sems, w * len(targets) + k, to) for k, to in enumerate(targets)]
    return copies


def _gather_pass_copies(arrived, gathered, sems):
    x, y, c = _place()
    chips = _other_chips(x, y)
    return [_remote(src.at[_slot(*chip, c)], dst.at[_slot(*chip, c)], sems, w * len(chips) + j, (x, y, 1 - c))
            for w, (src, dst) in enumerate(zip(arrived, gathered)) for j, chip in enumerate(chips)]


def _scatter_copies(partials, received, sems):
    me = _slot(*_place())
    return [_remote(src.at[me ^ k], dst.at[k - 1], sems, w * N_PEERS + k - 1, _peer(k))
            for w, (src, dst) in enumerate(zip(partials, received)) for k in range(1, N_DEV)]


class _Carried:
    COPIES = {"gather_out": (_gather_out_copies, 4, True), "gather_pass": (_gather_pass_copies, 3, False),
              "scatter": (_scatter_copies, N_PEERS, False)}

    def __init__(self, jobs):
        self.jobs = [(kind, list(arrays)) for kind, arrays in jobs if len(arrays)]
        self.inputs = [a for _, arrays in self.jobs for a in arrays]
        self.out_shapes, self.sems, self.sem_counts = [], [], []
        for kind, arrays in self.jobs:
            _, fan, local = self.COPIES[kind]
            for a in arrays:
                shape = {"gather_out": (N_DEV, *a.shape), "gather_pass": a.shape,
                         "scatter": (N_PEERS, *a.shape[1:])}[kind]
                self.out_shapes.append(jax.ShapeDtypeStruct(shape, BF16))
            job_sems = [pltpu.SemaphoreType.DMA((fan * len(arrays),))] * 2
            job_sems += [pltpu.SemaphoreType.DMA((len(arrays),))] if local else []
            self.sems += job_sems
            self.sem_counts.append(len(job_sems))

    def aliases(self, first_input, first_output):
        pairs, at = {}, 0
        for kind, arrays in self.jobs:
            if kind == "gather_pass":
                pairs.update({first_input + at + i: first_output + at + i for i in range(len(arrays))})
            at += len(arrays)
        return pairs

    def copies(self, in_refs, out_refs, sem_refs):
        out, at, sem_at = [], 0, 0
        for (kind, arrays), n_sems in zip(self.jobs, self.sem_counts):
            n = len(arrays)
            out += self.COPIES[kind][0](in_refs[at:at + n], out_refs[at:at + n], sem_refs[sem_at:sem_at + n_sems])
            at, sem_at = at + n, sem_at + n_sems
        return out


def _cast_shards(shards):
    def body(*refs):
        for src, dst in zip(refs[:len(shards)], refs[len(shards):]):
            dst[...] = src[...].astype(BF16)

    return _pcall(
        body, name="cast_shards",
        in_specs=[pl.BlockSpec(memory_space=pltpu.VMEM)] * len(shards),
        out_specs=[pl.BlockSpec(memory_space=pltpu.VMEM)] * len(shards),
        out_shape=[jax.ShapeDtypeStruct(a.shape, BF16) for a in shards],
        compiler_params=_params(),
    )(*shards)


def _all_reduce_small(packed):
    r = packed.shape[0]

    def body(x_ref, o_ref, gathered, send_sems, recv_sems):
        x, y, c = _place()
        me = _slot(x, y, c)
        gathered[me] = x_ref[...]
        peers = [(px, py, pc) for px in range(2) for py in range(2) for pc in range(2)]
        started = []
        for k in range(1, N_DEV):
            to = (x ^ (k >> 2), y ^ ((k >> 1) & 1), c ^ (k & 1))
            cp = pltpu.make_async_remote_copy(
                src_ref=x_ref, dst_ref=gathered.at[me],
                send_sem=send_sems.at[k - 1], recv_sem=recv_sems.at[k - 1],
                device_id=to, device_id_type=MESH)
            cp.start()
            started.append(cp)
        del peers
        for cp in started:
            cp.wait()
        total = gathered[0]
        for k in range(1, N_DEV):
            total = total + gathered[k]
        o_ref[...] = total

    return _pcall(
        body, name="all_reduce_small",
        in_specs=[pl.BlockSpec(memory_space=pltpu.VMEM)],
        out_specs=pl.BlockSpec(memory_space=pltpu.VMEM),
        out_shape=jax.ShapeDtypeStruct(packed.shape, F32),
        scratch_shapes=[pltpu.VMEM((N_DEV, r, LANES), F32),
                        pltpu.SemaphoreType.DMA((N_DEV - 1,)), pltpu.SemaphoreType.DMA((N_DEV - 1,))],
        compiler_params=_params(),
    )(packed)


def _adam_math(w, g, m, v):
    m = ADAM_B1 * m + (1.0 - ADAM_B1) * g
    v = ADAM_B2 * v + (1.0 - ADAM_B2) * jnp.square(g)
    m_hat = m / (1.0 - ADAM_B1 ** ADAM_STEP)
    v_hat = v / (1.0 - ADAM_B2 ** ADAM_STEP)
    delta = -ADAM_LR * (m_hat / (jnp.sqrt(v_hat) + ADAM_EPS) + ADAM_WD * w)
    return delta, m, v


def _adam_sharded(name, own, received, w, m, v, place, tr=256):
    r, cdim = w.shape
    tr = _tile(r, tr) if r % LANES == 0 else r

    def body(place_ref, own_ref, rec_ref, w_ref, m_ref, v_ref, g_ref, d_ref, nm_ref, nv_ref):
        del place_ref
        g = own_ref[...]
        for j in range(N_PEERS):
            g = g + rec_ref[j].astype(F32)
        delta, nm, nv = _adam_math(w_ref[...], g, m_ref[...], v_ref[...])
        g_ref[...] = g
        d_ref[...] = delta
        nm_ref[...] = nm
        nv_ref[...] = nv

    blk = pl.BlockSpec((tr, cdim), lambda i, pr: (i, 0))
    grid_spec = pltpu.PrefetchScalarGridSpec(
        num_scalar_prefetch=1, grid=(r // tr,),
        in_specs=[pl.BlockSpec((None, tr, cdim), lambda i, pr: (4 * pr[0] + 2 * pr[1] + pr[2], i, 0)),
                  pl.BlockSpec((N_PEERS, tr, cdim), lambda i, pr: (0, i, 0)), blk, blk, blk],
        out_specs=[blk] * 4)
    return _pcall(body, name=name, grid_spec=grid_spec,
                  out_shape=[jax.ShapeDtypeStruct((r, cdim), F32)] * 4,
                  compiler_params=_params(("parallel",)))(place, own, received, w, m, v)


def _adam_small(w, g, m, v):
    def body(w_ref, g_ref, m_ref, v_ref, d_ref, nm_ref, nv_ref):
        delta, nm, nv = _adam_math(w_ref[...], g_ref[...], m_ref[...], v_ref[...])
        d_ref[...] = delta
        nm_ref[...] = nm
        nv_ref[...] = nv

    return _pcall(body, name="adam_small",
                  in_specs=[pl.BlockSpec(memory_space=pltpu.VMEM)] * 4,
                  out_specs=[pl.BlockSpec(memory_space=pltpu.VMEM)] * 3,
                  out_shape=[jax.ShapeDtypeStruct(w.shape, F32)] * 3,
                  compiler_params=_params())(w, g, m, v)


def _rows(vec):
    return vec.reshape(-1, LANES)


def kernel(x, p, g_mix, w_in, conv_w, g_conv_out, g_attn_out, w_out, g_mlp, w_up, w_down, g_ple, w_ple_gate, w_ple_proj, g_final, loss_target, m_g_mix, m_w_in, m_conv_w, m_g_conv_out, m_g_attn_out, m_w_out, m_g_mlp, m_w_up, m_w_down, m_g_ple, m_w_ple_gate, m_w_ple_proj, m_g_final, v_g_mix, v_w_in, v_conv_w, v_g_conv_out, v_g_attn_out, v_w_out, v_g_mlp, v_w_up, v_w_down, v_g_ple, v_w_ple_gate, v_w_ple_proj, v_g_final):
    s, d = x.shape[1], x.shape[2]
    w_conv = g_conv_out.shape[1]
    w_attn = g_attn_out.shape[1]
    cw = conv_w.shape[2]
    xs, ps, tgt = x[0], p[0, 0], loss_target[0]
    place = jnp.stack([lax.axis_index("x"), lax.axis_index("y"), lax.axis_index("c")]).astype(jnp.int32)
    my_slot = 4 * place[0] + 2 * place[1] + place[2]

    conv_tile = jnp.pad(conv_w[0], ((0, HALO - CONV_K), (0, LANES - cw)))
    big = [w_in[0], w_out[0], w_up[0], w_down[0], w_ple_gate[0], w_ple_proj[0]]
    win_g, conv_g = _all_gather([big[0], conv_tile], [BF16, F32])
    s_out, s_up, s_down, s_gate, s_proj = _cast_shards(big[1:])
    conv_full = jnp.transpose(conv_g[:, :CONV_K, :cw], (1, 0, 2)).reshape(CONV_K, w_conv)
    in_shard, up_shard, proj_shard = big[0].shape[1], big[2].shape[1], big[5].shape[1]

    a = _rmsnorm_fwd("norm_mix", xs, g_mix)
    proj, g_out, g_gate, g_proj = _mm_nn("in_proj", a, win_g, n_shard=in_shard, tn=in_shard,
                                         carry=[("gather_out", [s_out, s_gate, s_proj])])
    cat = _conv_fwd(proj, conv_full, g_conv_out, w_conv, d)
    o, cat, (g_up, g_down, wout_g, wgate_g, wproj_g) = _attn_fwd(
        proj, g_attn_out, cat, w_conv,
        [("gather_out", [s_up, s_down]), ("gather_pass", [g_out, g_gate, g_proj])])
    wout_f = wout_g.reshape(-1, wout_g.shape[-1])
    wgate_f = wgate_g.reshape(-1, wgate_g.shape[-1])
    h1, wup_g = _mm_nn("out_proj", cat, wout_f, epilogue=_ep_residual, extras=(xs,),
                       carry=[("gather_pass", [g_up])])
    mn = _rmsnorm_fwd("norm_mlp", h1, g_mlp)
    act, wdown_g = _mm_nn("mlp_up", mn, wup_g, n_shard=up_shard, epilogue=_ep_up, out_dtypes=(BF16,),
                          carry=[("gather_pass", [g_down])])
    wdown_f = wdown_g.reshape(-1, wdown_g.shape[-1])
    h2, = _mm_nn("mlp_down", act, wdown_f, epilogue=_ep_residual, extras=(h1,))
    n3 = _rmsnorm_fwd("norm_ple", h2, g_ple)
    gl, = _mm_nn("ple_gate", n3, wgate_f)
    pp = _ple_proj(ps, wproj_g)
    loss_part, dh3, dgl, dpp, dg_final = _ple_loss(h2, gl, pp, tgt, g_final.reshape(1, d))

    def slots(t2d):
        return t2d.reshape(N_DEV, -1, t2d.shape[-1])

    dw_proj = _d_ple_proj(ps, dpp, proj_shard)
    dw_gate = [slots(t) for t in _mm_tn("d_w_ple_gate", n3, dgl)]
    dh2, dh2b, dg_ple = _mm_nt_norm_bwd("d_norm_ple", dgl, wgate_f, h2, g_ple, dh3)
    du, gate_recv, proj_recv = _mm_nt("d_mlp_act", dh2b, wdown_f, epilogue=_ep_dact, out_dtypes=(BF16,),
                                      extras=(act,), carry=[("scatter", [dw_gate[1], dw_proj[1]])])
    dw_down = [slots(t) for t in _mm_tn("d_w_down", act, dh2b)]
    dw_up = _mm_tn("d_w_up", mn, du, n_shard=up_shard)
    dh1, dh1b, dg_mlp = _mm_nt_norm_bwd("d_norm_mlp", du, wup_g, h1, g_mlp, dh2, k_shard=up_shard, tm=1024)
    dcat, = _mm_nt("d_cat", dh1b, wout_f)
    dw_out = [slots(t) for t in _mm_tn("d_w_out", cat, dh1b)]
    dproj, dg_attn, (up_recv, down_recv) = _attn_bwd(proj, o, dcat, g_attn_out, w_conv,
                                                     [("scatter", [dw_up[1], dw_down[1]])])
    dproj, dconv, dg_conv = _conv_bwd(proj, dcat, conv_full, g_conv_out, dproj, w_conv)
    *dw_in, out_recv = _mm_tn("d_w_in", a, dproj, n_shard=in_shard, tn=in_shard,
                              carry=[("scatter", [dw_out[1]])])
    grad_x, _, dg_mix, in_recv = _mm_nt_norm_bwd("d_norm_mix", dproj, win_g, xs, g_mix, dh1, k_shard=in_shard,
                                                 tk=in_shard, tm=1024, carry=[("scatter", [dw_in[1]])])

    names = ["w_in", "w_out", "w_up", "w_down", "w_ple_gate", "w_ple_proj"]
    owns = [dw_in[0], dw_out[0], dw_up[0], dw_down[0], dw_gate[0], dw_proj[0]]
    recvs = [in_recv, out_recv, up_recv, down_recv, gate_recv, proj_recv]
    moments = [(m_w_in, v_w_in), (m_w_out, v_w_out), (m_w_up, v_w_up), (m_w_down, v_w_down),
               (m_w_ple_gate, v_w_ple_gate), (m_w_ple_proj, v_w_ple_proj)]
    big_out = {}
    for n, own, rc, wt, (mm, vv) in zip(names, owns, recvs, big, moments):
        big_out[n] = [t[None] for t in _adam_sharded("adam_" + n, own, rc, wt, mm[0], vv[0], place)]

    n_conv_rows = CONV_K * w_conv // LANES
    small_g = jnp.concatenate(
        [_rows(dg_mix[0]), _rows(dg_conv[0]), _rows(dg_attn[0]), _rows(dg_mlp[0]), _rows(dg_ple[0]),
         _rows(dg_final[0]), _rows(dconv.reshape(-1)), loss_part], axis=0)
    n_gain_rows = small_g.shape[0] - n_conv_rows - 1
    pad_rows = (-small_g.shape[0]) % HALO
    small_g = _all_reduce_small(jnp.pad(small_g, ((0, pad_rows), (0, 0))))
    loss = small_g[n_gain_rows + n_conv_rows, 0]
    dconv_full = small_g[n_gain_rows:n_gain_rows + n_conv_rows].reshape(CONV_K, w_conv)
    dconv_mine = lax.dynamic_slice(dconv_full, (0, my_slot * cw), (CONV_K, cw))

    def pack(vecs, conv_part):
        rows = [_rows(t.reshape(-1)) for t in vecs]
        rows.append(jnp.pad(conv_part, ((0, HALO - CONV_K), (0, LANES - cw))))
        return jnp.concatenate(rows, axis=0)

    gains = [g_mix, g_conv_out, g_attn_out, g_mlp, g_ple, g_final]
    gains_m = [m_g_mix, m_g_conv_out, m_g_attn_out, m_g_mlp, m_g_ple, m_g_final]
    gains_v = [v_g_mix, v_g_conv_out, v_g_attn_out, v_g_mlp, v_g_ple, v_g_final]
    gpack = jnp.concatenate([small_g[:n_gain_rows], jnp.pad(dconv_mine, ((0, HALO - CONV_K), (0, LANES - cw)))], axis=0)
    sd, sm, sv = _adam_small(pack(gains, conv_w[0]), gpack, pack(gains_m, m_conv_w[0]), pack(gains_v, v_conv_w[0]))

    def unpack(packed):
        out, r0 = [], 0
        for t in gains:
            nr = t.size // LANES
            out.append(packed[r0:r0 + nr].reshape(t.shape))
            r0 += nr
        out.append(packed[r0:r0 + CONV_K, :cw][None])
        return out

    sg_l, sd_l, sm_l, sv_l = unpack(gpack), unpack(sd), unpack(sm), unpack(sv)
    small_names = ["g_mix", "g_conv_out", "g_attn_out", "g_mlp", "g_ple", "g_final", "conv_w"]
    small_out = {n: [sg_l[i], sd_l[i], sm_l[i], sv_l[i]] for i, n in enumerate(small_names)}

    order = ["g_mix", "w_in", "conv_w", "g_conv_out", "g_attn_out", "w_out", "g_mlp", "w_up", "w_down",
             "g_ple", "w_ple_gate", "w_ple_proj", "g_final"]
    table = {**big_out, **small_out}
    outs = [loss, grad_x[None]]
    for kind in range(4):
        outs.extend(table[n][kind] for n in order)
    return tuple(outs)
```

```python
import jax
import jax.numpy as jnp
from jax import lax
from jax.experimental import pallas as pl
from jax.experimental.pallas import tpu as pltpu

F32 = jnp.float32
BF16 = jnp.bfloat16
EPS = 1e-6
HEAD_DIM = 64
LANES = 128
CONV_K = 3
ATTN_BLOCK = 256
HALO = 8
N_DEV = 8
MESH = pl.DeviceIdType.MESH
VMEM_LIMIT = 56 * 1024 * 1024

ADAM_LR = 0.001
ADAM_B1 = 0.9
ADAM_B2 = 0.999
ADAM_EPS = 1e-08
ADAM_WD = 0.01
ADAM_STEP = 10


def _pcall(body, **kw):
    return pl.pallas_call(body, **kw)


def _params(sem=None, **kw):
    return pltpu.CompilerParams(dimension_semantics=sem, vmem_limit_bytes=VMEM_LIMIT, **kw)


def _tile(dim, pref):
    t = min(dim, pref)
    while dim % t:
        t -= LANES
    assert t > 0, (dim, pref)
    return t


_NN = (((1,), (0,)), ((), ()))
_NT = (((1,), (1,)), ((), ()))
_TN = (((0,), (0,)), ((), ()))


def _ep_store(acc, outs):
    outs[0][...] = acc.astype(outs[0].dtype)


def _ep_both(acc, outs):
    outs[0][...] = acc
    outs[1][...] = acc.astype(BF16)


def _ep_residual(acc, res, outs):
    outs[0][...] = acc + res[...]


def _ep_up(acc, outs):
    outs[0][...] = jnp.square(jnp.maximum(acc, 0.0)).astype(BF16)


def _ep_dact(acc, act, outs):
    outs[0][...] = (acc * (2.0 * jnp.sqrt(act[...].astype(F32)))).astype(BF16)


def _ep_norm_bwd(acc, h, g, dres, outs):
    @pl.when(pl.program_id(0) == 0)
    def _():
        outs[2][...] = jnp.zeros_like(outs[2])

    hv = h[...]
    r = lax.rsqrt(jnp.mean(hv * hv, axis=-1, keepdims=True) + EPS)
    hn = hv * r
    outs[2][...] += jnp.sum(acc * hn, axis=0, keepdims=True)
    dhn = acc * g[...]
    dh = dres[...] + r * (dhn - hn * jnp.mean(dhn * hn, axis=-1, keepdims=True))
    outs[0][...] = dh
    outs[1][...] = dh.astype(BF16)


def _matmul(name, a, b, *, dims, grid, a_spec, b_spec, acc_shape, out_shapes, out_specs,
            epilogue=_ep_store, extras=(), extra_specs=(), carry=(), sequential=False):
    nk = grid[2]
    plan = _Carried(carry)
    n_ex, n_out, n_xc = len(extras), len(out_shapes), len(plan.inputs)
    n_sems = len(plan.sems)
    last = tuple(g - 1 for g in grid)

    def product(a_ref, b_ref):
        return lax.dot_general(a_ref[...].astype(BF16), b_ref[...].astype(BF16), dims,
                               preferred_element_type=F32)

    def body(a_ref, b_ref, *rest):
        ex, rest = rest[:n_ex], rest[n_ex:]
        partials, rest = rest[:n_xc], rest[n_xc:]
        outs, rest = rest[:n_out], rest[n_out:]
        received, rest = rest[:n_xc], rest[n_xc:]
        ids = [pl.program_id(axis) for axis in range(3)]
        if n_xc:
            @pl.when((ids[0] == 0) & (ids[1] == 0) & (ids[2] == 0))
            def _():
                for cp in plan.copies(partials, received, rest[-n_sems:]):
                    cp.start()

        if nk == 1:
            epilogue(product(a_ref, b_ref), *ex, outs)
        else:
            acc = rest[0]

            @pl.when(ids[2] == 0)
            def _():
                acc[...] = product(a_ref, b_ref)

            @pl.when(ids[2] > 0)
            def _():
                acc[...] += product(a_ref, b_ref)

            @pl.when(ids[2] == nk - 1)
            def _():
                epilogue(acc[...], *ex, outs)

        if n_xc:
            @pl.when((ids[0] == last[0]) & (ids[1] == last[1]) & (ids[2] == last[2]))
            def _():
                for cp in plan.copies(partials, received, rest[-n_sems:]):
                    cp.wait()

    anywhere = [pl.BlockSpec(memory_space=pl.ANY)] * n_xc
    return _pcall(
        body, name=name, grid=grid,
        in_specs=[a_spec, b_spec, *extra_specs, *anywhere],
        out_specs=[*out_specs, *anywhere],
        out_shape=[*out_shapes, *plan.out_shapes],
        scratch_shapes=([] if nk == 1 else [pltpu.VMEM(acc_shape, F32)]) + plan.sems,
        input_output_aliases=plan.aliases(2 + n_ex, n_out),
        compiler_params=_params(("arbitrary",) * 3 if n_xc or sequential else ("parallel", "parallel", "arbitrary")),
    )(a, b, *extras, *plan.inputs)


_NO_CARRY = ()


def _mm_nn(name, a, w, *, n_shard=None, epilogue=_ep_store, out_dtypes=(F32,), extras=(), carry=_NO_CARRY,
           tm=1024, tn=1024, tk=1024):
    m, kd = a.shape
    if n_shard is None:
        n = w.shape[1]
        tn = _tile(n, tn)
        tk = _tile(kd, tk)
        b_spec = pl.BlockSpec((tk, tn), lambda i, j, k: (k, j))
    else:
        n = N_DEV * n_shard
        tn = _tile(n_shard, tn)
        tk = _tile(kd, tk)
        per = n_shard // tn
        b_spec = pl.BlockSpec((None, tk, tn), lambda i, j, k: (j // per, k, j % per))
    tm = _tile(m, tm)
    o_spec = pl.BlockSpec((tm, tn), lambda i, j, k: (i, j))
    return _matmul(
        name, a, w, dims=_NN, grid=(m // tm, n // tn, kd // tk),
        a_spec=pl.BlockSpec((tm, tk), lambda i, j, k: (i, k)), b_spec=b_spec,
        acc_shape=(tm, tn),
        out_shapes=[jax.ShapeDtypeStruct((m, n), d) for d in out_dtypes],
        out_specs=[o_spec] * len(out_dtypes),
        epilogue=epilogue, extras=extras, extra_specs=[o_spec] * len(extras), carry=carry)


def _mm_nt(name, a, w, *, k_shard=None, epilogue=_ep_store, out_dtypes=(F32,), extras=(), carry=_NO_CARRY,
           tm=1024, tn=1024, tk=1024):
    m, kd = a.shape
    if k_shard is None:
        n = w.shape[0]
        tn = _tile(n, tn)
        tk = _tile(kd, tk)
        b_spec = pl.BlockSpec((tn, tk), lambda i, j, k: (j, k))
    else:
        n = w.shape[1]
        tn = _tile(n, tn)
        tk = _tile(k_shard, tk)
        per = k_shard // tk
        b_spec = pl.BlockSpec((None, tn, tk), lambda i, j, k: (k // per, j, k % per))
    tm = _tile(m, tm)
    o_spec = pl.BlockSpec((tm, tn), lambda i, j, k: (i, j))
    return _matmul(
        name, a, w, dims=_NT, grid=(m // tm, n // tn, kd // tk),
        a_spec=pl.BlockSpec((tm, tk), lambda i, j, k: (i, k)), b_spec=b_spec,
        acc_shape=(tm, tn),
        out_shapes=[jax.ShapeDtypeStruct((m, n), d) for d in out_dtypes],
        out_specs=[o_spec] * len(out_dtypes),
        epilogue=epilogue, extras=extras, extra_specs=[o_spec] * len(extras), carry=carry)


def _mm_nt_norm_bwd(name, a, w, h, g, dres, *, k_shard=None, carry=_NO_CARRY, tm=512, tk=1024):
    m, kd = a.shape
    n = h.shape[1]
    if k_shard is None:
        tk = _tile(kd, tk)
        b_spec = pl.BlockSpec((n, tk), lambda i, j, k: (0, k))
    else:
        tk = _tile(k_shard, tk)
        per = k_shard // tk
        b_spec = pl.BlockSpec((None, n, tk), lambda i, j, k: (k // per, 0, k % per))
    tm = _tile(m, tm)
    rows = pl.BlockSpec((tm, n), lambda i, j, k: (i, 0))
    vec = pl.BlockSpec((1, n), lambda i, j, k: (0, 0))
    return _matmul(
        name, a, w, dims=_NT, grid=(m // tm, 1, kd // tk),
        a_spec=pl.BlockSpec((tm, tk), lambda i, j, k: (i, k)), b_spec=b_spec, acc_shape=(tm, n),
        out_shapes=[jax.ShapeDtypeStruct((m, n), F32), jax.ShapeDtypeStruct((m, n), BF16),
                    jax.ShapeDtypeStruct((1, n), F32)],
        out_specs=[rows, rows, vec], epilogue=_ep_norm_bwd,
        extras=(h, g, dres), extra_specs=[rows, vec, rows], carry=carry, sequential=True)


TN_TILE_BYTES = 40 * 1024 * 1024


def _mm_tn(name, a, b, *, n_shard=None, carry=_NO_CARRY, tm=1024, tn=1024):
    t, m = a.shape
    n = b.shape[1]
    tm = _tile(m, tm)
    tn = _tile(n if n_shard is None else n_shard, tn)
    tk = t
    while 2 * 2 * tk * (tm + tn) + 4 * tm * tn * 5 > TN_TILE_BYTES and tk % (2 * LANES) == 0:
        tk //= 2
    if n_shard is None:
        o_spec = pl.BlockSpec((tm, tn), lambda i, j, k: (i, j))
        shape = (m, n)
    else:
        per = n_shard // tn
        o_spec = pl.BlockSpec((None, tm, tn), lambda i, j, k: (j // per, i, j % per))
        shape = (N_DEV, m, n_shard)
    return _matmul(
        name, a, b, dims=_TN, grid=(m // tm, n // tn, t // tk),
        a_spec=pl.BlockSpec((tk, tm), lambda i, j, k: (k, i)),
        b_spec=pl.BlockSpec((tk, tn), lambda i, j, k: (k, j)),
        acc_shape=(tm, tn), epilogue=_ep_both, carry=carry,
        out_shapes=[jax.ShapeDtypeStruct(shape, F32), jax.ShapeDtypeStruct(shape, BF16)],
        out_specs=[o_spec, o_spec])


def _ple_proj(p, w_g, tm=1024):
    s, kd = p.shape
    ns = w_g.shape[2]
    tm = _tile(s, tm)

    def body(p_ref, w_ref, o_ref):
        pv = p_ref[...].astype(BF16)
        for j in range(N_DEV):
            o_ref[:, j * ns:(j + 1) * ns] = jnp.dot(pv, w_ref[j], preferred_element_type=F32)

    return _pcall(body, name="ple_proj", grid=(s // tm,),
                  in_specs=[pl.BlockSpec((tm, kd), lambda i: (i, 0)),
                            pl.BlockSpec((N_DEV, kd, ns), lambda i: (0, 0, 0))],
                  out_specs=pl.BlockSpec((tm, N_DEV * ns), lambda i: (i, 0)),
                  out_shape=jax.ShapeDtypeStruct((s, N_DEV * ns), F32),
                  compiler_params=_params(("parallel",)))(p, w_g)


def _d_ple_proj(p, dpp, ns, tk=1024):
    s, kd = p.shape
    tk = _tile(s, tk)
    nk = s // tk

    def body(p_ref, d_ref, of_ref, ob_ref, acc):
        k = pl.program_id(0)

        @pl.when(k == 0)
        def _():
            acc[...] = jnp.zeros_like(acc)

        pv = p_ref[...].astype(BF16)
        for j in range(N_DEV):
            acc[j] += lax.dot_general(pv, d_ref[:, j * ns:(j + 1) * ns], _TN, preferred_element_type=F32)

        @pl.when(k == nk - 1)
        def _():
            of_ref[...] = acc[...]
            ob_ref[...] = acc[...].astype(BF16)

    whole = pl.BlockSpec((N_DEV, kd, ns), lambda k: (0, 0, 0))
    return _pcall(body, name="d_w_ple_proj", grid=(nk,),
                  in_specs=[pl.BlockSpec((tk, kd), lambda k: (k, 0)),
                            pl.BlockSpec((tk, N_DEV * ns), lambda k: (k, 0))],
                  out_specs=[whole, whole],
                  out_shape=[jax.ShapeDtypeStruct((N_DEV, kd, ns), F32), jax.ShapeDtypeStruct((N_DEV, kd, ns), BF16)],
                  scratch_shapes=[pltpu.VMEM((N_DEV, kd, ns), F32)],
                  compiler_params=_params(("arbitrary",)))(p, dpp)


def _row_spec(tr, d):
    return pl.BlockSpec((tr, d), lambda i: (i, 0))


def _vec_spec(d):
    return pl.BlockSpec((1, d), lambda i: (0, 0))


def _rmsnorm_fwd(name, x, g, tr=512):
    s, d = x.shape
    tr = _tile(s, tr)

    def body(x_ref, g_ref, o_ref):
        xv = x_ref[...]
        r = lax.rsqrt(jnp.mean(xv * xv, axis=-1, keepdims=True) + EPS)
        o_ref[...] = (xv * r * g_ref[...]).astype(BF16)

    return _pcall(body, name=name, grid=(s // tr,),
                  in_specs=[_row_spec(tr, d), _vec_spec(d)], out_specs=_row_spec(tr, d),
                  out_shape=jax.ShapeDtypeStruct((s, d), BF16),
                  compiler_params=_params(("parallel",)))(x, g)


def _ple_loss(h2, gl, pp, tgt, g_final, tr=512):
    s, d = h2.shape
    tr = _tile(s, tr)

    def body(h2_ref, gl_ref, pp_ref, t_ref, g_ref, loss_ref, dh3_ref, dgl_ref, dpp_ref, dg_ref):
        @pl.when(pl.program_id(0) == 0)
        def _():
            dg_ref[...] = jnp.zeros_like(dg_ref)
            loss_ref[...] = jnp.zeros_like(loss_ref)

        gate = jax.nn.sigmoid(gl_ref[...])
        ppv = pp_ref[...]
        h3 = h2_ref[...] + gate * ppv
        r = lax.rsqrt(jnp.mean(h3 * h3, axis=-1, keepdims=True) + EPS)
        hn = h3 * r
        gv = g_ref[...]
        diff = hn * gv - t_ref[...]
        row = jnp.mean(diff * diff, axis=-1, keepdims=True)
        loss_ref[...] += 0.5 * jnp.sum(row, axis=0, keepdims=True)
        dy = diff * (1.0 / d)
        dg_ref[...] += jnp.sum(dy * hn, axis=0, keepdims=True)
        dhn = dy * gv
        dh3 = r * (dhn - hn * jnp.mean(dhn * hn, axis=-1, keepdims=True))
        dh3_ref[...] = dh3
        dgl_ref[...] = (dh3 * ppv * gate * (1.0 - gate)).astype(BF16)
        dpp_ref[...] = (dh3 * gate).astype(BF16)

    return _pcall(body, name="ple_loss", grid=(s // tr,),
                  in_specs=[_row_spec(tr, d)] * 4 + [_vec_spec(d)],
                  out_specs=[_vec_spec(LANES), _row_spec(tr, d), _row_spec(tr, d), _row_spec(tr, d), _vec_spec(d)],
                  out_shape=[jax.ShapeDtypeStruct((1, LANES), F32), jax.ShapeDtypeStruct((s, d), F32),
                             jax.ShapeDtypeStruct((s, d), BF16), jax.ShapeDtypeStruct((s, d), BF16),
                             jax.ShapeDtypeStruct((1, d), F32)],
                  compiler_params=_params(("arbitrary",)))(h2, gl, pp, tgt, g_final)


def _low_half():
    return lax.broadcasted_iota(jnp.int32, (1, LANES), 1) < HEAD_DIM


def _half_mean(v, low):
    s_lo = jnp.sum(jnp.where(low, v, 0.0), axis=-1, keepdims=True)
    s_hi = jnp.sum(jnp.where(low, 0.0, v), axis=-1, keepdims=True)
    return jnp.where(low, s_lo, s_hi) * (1.0 / HEAD_DIM)


def _head_norm_bwd(val, dout, g, low):
    r = lax.rsqrt(_half_mean(val * val, low) + EPS)
    vn = val * r
    dvn = dout * g
    return r * (dvn - vn * _half_mean(dvn * vn, low)), dout * vn


def _conv_taps(vv_ext, w_ref, rows):
    v0 = vv_ext[HALO:]
    v1 = pltpu.roll(vv_ext, 1, 0)[HALO:]
    v2 = pltpu.roll(vv_ext, 2, 0)[HALO:]
    del rows
    return w_ref[2:3, :] * v0 + w_ref[1:2, :] * v1 + w_ref[0:1, :] * v2, (v0, v1, v2)


def _conv_fwd(proj, conv_w, g_conv, w_conv, d_model, tr=512):
    s = proj.shape[0]
    tr = _tile(s, tr)
    hb = tr // HALO

    def main(part):
        return pl.BlockSpec((tr, w_conv), lambda i: (i, part))

    def prev(part):
        return pl.BlockSpec((HALO, w_conv), lambda i: (jnp.maximum(i * hb - 1, 0), part))

    def body(cb_ref, cc_ref, cu_ref, ccp_ref, cup_ref, w_ref, g_ref, o_ref):
        i = pl.program_id(0)
        low = _low_half()
        for j in range(w_conv // LANES):
            cols = slice(j * LANES, (j + 1) * LANES)
            vv_prev = jnp.where(i > 0, ccp_ref[:, cols] * cup_ref[:, cols], 0.0)
            vv_ext = jnp.concatenate([vv_prev, cc_ref[:, cols] * cu_ref[:, cols]], axis=0)
            y, _ = _conv_taps(vv_ext, w_ref.at[:, cols], tr)
            co = cb_ref[:, cols] * y
            r = lax.rsqrt(_half_mean(co * co, low) + EPS)
            o_ref[:, cols] = (co * r * g_ref[:, cols]).astype(BF16)

    return _pcall(
        body, name="conv_fwd", grid=(s // tr,),
        in_specs=[main(0), main(1), main(2), prev(1), prev(2),
                  pl.BlockSpec((CONV_K, w_conv), lambda i: (0, 0)),
                  pl.BlockSpec((1, w_conv), lambda i: (0, 0))],
        out_specs=pl.BlockSpec((tr, w_conv), lambda i: (i, 0)),
        out_shape=jax.ShapeDtypeStruct((s, d_model), BF16),
        compiler_params=_params(("parallel",)),
    )(proj, proj, proj, proj, proj, conv_w, g_conv)


def _conv_bwd(proj, dcat, conv_w, g_conv, dproj, w_conv, tr=512):
    s = proj.shape[0]
    tr = _tile(s, tr)
    hb = tr // HALO
    last = s // HALO - 1
    nt = s // tr

    def main(part):
        return pl.BlockSpec((tr, w_conv), lambda i: (i, part))

    def prev(part):
        return pl.BlockSpec((HALO, w_conv), lambda i: (jnp.maximum(i * hb - 1, 0), part))

    def nxt(part):
        return pl.BlockSpec((HALO, w_conv), lambda i: (jnp.minimum((i + 1) * hb, last), part))

    def body(cb_ref, cc_ref, cu_ref, dc_ref, ccp_ref, cup_ref, cbn_ref, ccn_ref, cun_ref, dcn_ref,
             w_ref, g_ref, dproj_in, dproj_ref, dw_ref, dg_ref):
        del dproj_in
        i = pl.program_id(0)

        @pl.when(i == 0)
        def _():
            dw_ref[...] = jnp.zeros_like(dw_ref)
            dg_ref[...] = jnp.zeros_like(dg_ref)

        low = _low_half()
        n_ext = tr + HALO
        rowid = lax.broadcasted_iota(jnp.int32, (n_ext, 1), 0)
        for j in range(w_conv // LANES):
            cols = slice(j * LANES, (j + 1) * LANES)
            wj = w_ref.at[:, cols]
            cc, cu = cc_ref[:, cols], cu_ref[:, cols]
            vv_prev = jnp.where(i > 0, ccp_ref[:, cols] * cup_ref[:, cols], 0.0)
            vv_ext = jnp.concatenate([vv_prev, cc * cu, ccn_ref[:, cols] * cun_ref[:, cols]], axis=0)
            y_ext, (v0, v1, v2) = _conv_taps(vv_ext, wj, n_ext)
            cb_ext = jnp.concatenate([cb_ref[:, cols], cbn_ref[:, cols]], axis=0)
            dc_ext = jnp.concatenate([dc_ref[:, cols], dcn_ref[:, cols]], axis=0)
            dco, dgn = _head_norm_bwd(cb_ext * y_ext, dc_ext, g_ref[:, cols], low)
            dyc = jnp.where((rowid < tr) | (i < nt - 1), dco * cb_ext, 0.0)
            dvv = (wj[2:3, :] * dyc[:tr] + wj[1:2, :] * pltpu.roll(dyc, n_ext - 1, 0)[:tr]
                   + wj[0:1, :] * pltpu.roll(dyc, n_ext - 2, 0)[:tr])
            dproj_ref[:, cols] = (dco[:tr] * y_ext[:tr]).astype(BF16)
            dproj_ref[:, w_conv + j * LANES:w_conv + (j + 1) * LANES] = (dvv * cu).astype(BF16)
            dproj_ref[:, 2 * w_conv + j * LANES:2 * w_conv + (j + 1) * LANES] = (dvv * cc).astype(BF16)
            dyt = dyc[:tr]
            for tap, shifted in enumerate((v2, v1, v0)):
                dw_ref[tap:tap + 1, cols] += jnp.sum(dyt * shifted[:tr], axis=0, keepdims=True)
            dg_ref[:, cols] += jnp.sum(dgn[:tr], axis=0, keepdims=True)

    n_cols = dproj.shape[1]
    return _pcall(
        body, name="conv_bwd", grid=(nt,),
        in_specs=[main(0), main(1), main(2), main(0),
                  prev(1), prev(2), nxt(0), nxt(1), nxt(2), nxt(0),
                  pl.BlockSpec((CONV_K, w_conv), lambda i: (0, 0)),
                  pl.BlockSpec((1, w_conv), lambda i: (0, 0)),
                  pl.BlockSpec(memory_space=pl.ANY)],
        out_specs=[pl.BlockSpec((tr, 3 * w_conv), lambda i: (i, 0)),
                   pl.BlockSpec((CONV_K, w_conv), lambda i: (0, 0)),
                   pl.BlockSpec((1, w_conv), lambda i: (0, 0))],
        out_shape=[jax.ShapeDtypeStruct((s, n_cols), BF16),
                   jax.ShapeDtypeStruct((CONV_K, w_conv), F32),
                   jax.ShapeDtypeStruct((1, w_conv), F32)],
        input_output_aliases={12: 0},
        compiler_params=_params(("arbitrary",)),
    )(proj, proj, proj, dcat, proj, proj, proj, proj, proj, dcat, conv_w, g_conv, dproj)


STRIP = 16

ALL_CHAINS = (0, 1, 2, 3)
UPPER_CHAINS = (2, 3)


RUN_FLOOR = -104.0


def _any_weight_left(run_s):
    return (jnp.max(run_s[...]) > RUN_FLOOR).astype(jnp.int32)


def _chains(low):
    return [(2 * half + h, half, msk) for half in range(2)
            for h, msk in enumerate((low, jnp.logical_not(low)))]


def _suffix_operator(t):
    r = lax.broadcasted_iota(jnp.int32, (2 * t, t), 0)
    c = lax.broadcasted_iota(jnp.int32, (2 * t, t), 1)
    return jnp.where((r > c) & ((r < t) | (r - t > c)), 1.0, 0.0).astype(BF16)


def _strips(t):
    return [(i, slice(i * STRIP, (i + 1) * STRIP)) for i in range(t // STRIP)]


def _strip_mask(i, t):
    r = lax.broadcasted_iota(jnp.int32, (STRIP, t), 0) + i * STRIP
    c = lax.broadcasted_iota(jnp.int32, (STRIP, t), 1)
    return r > c


def _store_split(ref, rows, val, t):
    hi = val.astype(BF16)
    ref[rows, 0:t] = hi
    ref[rows, t:2 * t] = (val - hi.astype(F32)).astype(BF16)


def _sb_scores(z_s, split_s, zl_s, tot_s, keep_s, t, diag):
    for i, rows in _strips(t):
        z = z_s[rows, :]
        log_beta = jnp.minimum(z, 0.0) - jnp.log(1.0 + jnp.exp(-jnp.abs(z)))
        log_keep = log_beta - z
        if diag:
            log_keep = jnp.where(_strip_mask(i, t), log_keep, 0.0)
        _store_split(split_s, rows, log_keep, t)
        zl_s[rows, :] = log_beta
        tot_s[rows, :] = _row_sum(log_keep)
        if keep_s is not None:
            keep_s[rows, :] = jnp.exp(log_keep)


def _row_sum(v):
    return jnp.broadcast_to(jnp.sum(v, axis=-1, keepdims=True), (v.shape[0], LANES))


def _wide(r, t):
    return jnp.concatenate([r] * (t // LANES), axis=1)


def _sb_weights(zl_s, suf_s, run_s, tot_s, a_s, t, diag, da_s=None, glog_s=None, gsplit_s=None, gtot_s=None):
    for i, rows in _strips(t):
        run = run_s[rows, :]
        a = jnp.exp(zl_s[rows, :] + suf_s[rows, :] + _wide(run, t))
        if diag:
            a = jnp.where(_strip_mask(i, t), a, 0.0)
        ab = a.astype(BF16)
        a_s[rows, :] = ab
        run_s[rows, :] = run + tot_s[rows, :]
        if da_s is not None:
            glog = ab.astype(F32) * da_s[rows, :]
            glog_s[rows, :] = glog
            _store_split(gsplit_s, rows, glog, t)
            gtot_s[rows, :] = _row_sum(glog)


def _sb_dscores(glog_s, cum_s, rest_s, gtot_s, keep_s, dz_s, t, diag):
    for i, rows in _strips(t):
        glog = glog_s[rows, :]
        rest = rest_s[rows, :]
        from_here = _wide(rest, t) - cum_s[rows, :]
        before = from_here - glog
        dz = from_here * keep_s[rows, :] - before
        if diag:
            dz = jnp.where(_strip_mask(i, t), dz, 0.0)
        dz_s[rows, :] = dz.astype(BF16)
        rest_s[rows, :] = rest - gtot_s[rows, :]


def _attn_fwd(proj, g_attn, cat, w_conv, carry, t=ATTN_BLOCK):
    s = proj.shape[0]
    w_attn = g_attn.shape[1]
    nh = w_attn // LANES
    t = _tile(s, t)
    tq = 2 * t
    nq = s // tq
    q0 = 3 * w_conv // LANES
    scale = HEAD_DIM ** -0.5
    plan = _Carried(carry)
    nw = len(plan.inputs)

    def body(q_ref, k_ref, v_ref, g_ref, cat_in, *rest):
        staged_refs, rest = rest[:nw], rest[nw:]
        o_ref, cat_ref = rest[:2]
        gathered_refs, rest = rest[2:2 + nw], rest[2 + nw:]
        kb, vb, tri_s, qm_s, z_s, split_s, zl_s, suf_s, a_s, run_s, tot_s, acc_s = rest[:12]
        gather_sems = rest[12:]
        del cat_in
        qi = pl.program_id(1)

        @pl.when((pl.program_id(0) == 0) & (qi == 0))
        def _():
            for cp in plan.copies(staged_refs, gathered_refs, gather_sems):
                cp.start()

        @pl.when(qi == 0)
        def _():
            kb[...] = k_ref[...].astype(BF16)
            vb[...] = v_ref[...].astype(BF16)
            tri_s[...] = _suffix_operator(t)

        low = _low_half()
        for c, half, msk in _chains(low):
            qm_s[c] = jnp.where(msk, q_ref[half * t:(half + 1) * t, :] * scale, 0.0).astype(BF16)
            run_s[c] = jnp.zeros((t, LANES), F32)
            acc_s[c] = jnp.zeros((t, LANES), F32)

        def key_rows(kblk):
            return pl.ds(pl.multiple_of(kblk * t, t), t)

        def scores_matmul(kblk, chains):
            ks = kb[key_rows(kblk), :]
            for c in chains:
                z_s[c] = lax.dot_general(qm_s[c], ks, _NT, preferred_element_type=F32)

        def front(modes, nxt, prev=None):
            for c, diag in modes:
                _sb_scores(z_s.at[c], split_s.at[c], zl_s.at[c], tot_s.at[c], None, t, diag)
                suf_s[c] = jnp.dot(split_s[c], tri_s[...], preferred_element_type=F32)
            if prev is not None:
                tail(*prev)
            scores_matmul(nxt, ALL_CHAINS)
            for c, diag in modes:
                _sb_weights(zl_s.at[c], suf_s.at[c], run_s.at[c], tot_s.at[c], a_s.at[c], t, diag)

        def tail(kblk, chains):
            vs = vb[key_rows(kblk), :]
            for c in chains:
                acc_s[c] += jnp.dot(a_s[c], vs, preferred_element_type=F32)

        top = 2 * qi + 1
        scores_matmul(top, UPPER_CHAINS)
        front([(c, True) for c in UPPER_CHAINS], top - 1)
        front([(c, c not in UPPER_CHAINS) for c in ALL_CHAINS], jnp.maximum(top - 2, 0),
              prev=(top, UPPER_CHAINS))

        def loop(state):
            it = state[0]
            cur = top - 2 - it
            front([(c, False) for c in ALL_CHAINS], jnp.maximum(cur - 1, 0), prev=(cur + 1, ALL_CHAINS))
            return it + 1, _any_weight_left(run_s)

        done, _ = lax.while_loop(lambda state: (state[0] < top - 1) & (state[1] > 0), loop,
                                 (jnp.int32(0), jnp.int32(1)))
        tail(top - 1 - done, ALL_CHAINS)
        for half in range(2):
            rows = slice(half * t, (half + 1) * t)
            o = jnp.where(low, acc_s[2 * half], acc_s[2 * half + 1])
            o_ref[rows, :] = o
            r = lax.rsqrt(_half_mean(o * o, low) + EPS)
            cat_ref[rows, :] = (o * r * g_ref[...]).astype(BF16)

        @pl.when((pl.program_id(0) == nh - 1) & (qi == nq - 1))
        def _():
            for cp in plan.copies(staged_refs, gathered_refs, gather_sems):
                cp.wait()

    whole = lambda col0: pl.BlockSpec((s, LANES), lambda h, i: (0, col0 + h))
    n_ch = len(ALL_CHAINS)
    res = _pcall(
        body, name="attn_fwd", grid=(nh, nq),
        in_specs=[pl.BlockSpec((tq, LANES), lambda h, i: (i, q0 + h)),
                  whole(q0 + nh), whole(q0 + 2 * nh),
                  pl.BlockSpec((1, LANES), lambda h, i: (0, h)),
                  pl.BlockSpec(memory_space=pl.ANY)] + [pl.BlockSpec(memory_space=pl.ANY)] * nw,
        out_specs=[pl.BlockSpec((tq, LANES), lambda h, i: (i, h)),
                   pl.BlockSpec((tq, LANES), lambda h, i: (i, w_conv // LANES + h))]
        + [pl.BlockSpec(memory_space=pl.ANY)] * nw,
        out_shape=[jax.ShapeDtypeStruct((s, w_attn), F32),
                   jax.ShapeDtypeStruct(cat.shape, BF16)] + plan.out_shapes,
        scratch_shapes=[pltpu.VMEM((s, LANES), BF16), pltpu.VMEM((s, LANES), BF16),
                        pltpu.VMEM((2 * t, t), BF16),
                        pltpu.VMEM((n_ch, t, LANES), BF16),
                        pltpu.VMEM((n_ch, t, t), F32),
                        pltpu.VMEM((n_ch, t, 2 * t), BF16),
                        pltpu.VMEM((n_ch, t, t), F32),
                        pltpu.VMEM((n_ch, t, t), F32),
                        pltpu.VMEM((n_ch, t, t), BF16),
                        pltpu.VMEM((n_ch, t, LANES), F32),
                        pltpu.VMEM((n_ch, t, LANES), F32),
                        pltpu.VMEM((n_ch, t, LANES), F32)]
        + plan.sems,
        input_output_aliases={4: 1, **plan.aliases(5, 2)},
        compiler_params=_params(("arbitrary", "arbitrary")),
    )(proj, proj, proj, g_attn, cat, *plan.inputs)
    return res[0], res[1], res[2:]


def _attn_bwd(proj, o, dcat, g_attn, w_conv, carry, t=ATTN_BLOCK):
    s, n_cols = proj.shape
    w_attn = g_attn.shape[1]
    nh = w_attn // LANES
    t = _tile(s, t)
    tq = 2 * t
    nq = s // tq
    q0 = 3 * w_conv // LANES
    scale = HEAD_DIM ** -0.5
    plan = _Carried(carry)
    nw = len(plan.inputs)

    def body(q_ref, k_ref, v_ref, o_ref, do_ref, g_ref, *rest):
        partial_refs, rest = rest[:nw], rest[nw:]
        dproj_ref, dg_ref = rest[:2]
        received_refs, rest = rest[2:2 + nw], rest[2 + nw:]
        (kb, vb, dkt_acc, dvt_acc, stash, tri_s, qm_s, dom_s, qt_s, dot_s, z_s, da_s, split_s, zl_s,
         keep_s, suf_s, a_s, glog_s, gsplit_s, cum_s, dz_s, run_s, tot_s, rest_s, gtot_s, dq_s) = rest[:26]
        scatter_sems = rest[26:]
        step_i = pl.program_id(1)
        which = pl.program_id(2)
        qi = nq - 1 - step_i
        head_pair = pl.program_id(0)

        @pl.when((head_pair == 0) & (step_i == 0) & (which == 0))
        def _():
            for cp in plan.copies(partial_refs, received_refs, scatter_sems):
                cp.start()

        @pl.when((head_pair == nh - 1) & (step_i == nq - 1) & (which == 2))
        def _():
            for cp in plan.copies(partial_refs, received_refs, scatter_sems):
                cp.wait()

        @pl.when(which == 0)
        def _():
            @pl.when(step_i == 0)
            def _():
                kb[...] = k_ref[...].astype(BF16)
                vb[...] = v_ref[...].astype(BF16)
                tri_s[...] = _suffix_operator(t)
                dkt_acc[...] = jnp.zeros_like(dkt_acc)
                dvt_acc[...] = jnp.zeros_like(dvt_acc)
                dg_ref[...] = jnp.zeros_like(dg_ref)

            low = _low_half()
            gv = g_ref[...]
            for half in range(2):
                rows = slice(half * t, (half + 1) * t)
                q = q_ref[rows, :] * scale
                ov = o_ref[rows, :]
                d_o, dgn = _head_norm_bwd(ov, do_ref[rows, :], gv, low)
                dg_ref[...] += jnp.sum(dgn, axis=0, keepdims=True)
                for h, msk in enumerate((low, jnp.logical_not(low))):
                    c = 2 * half + h
                    qh = jnp.where(msk, q, 0.0)
                    doh = jnp.where(msk, d_o, 0.0)
                    dom = doh.astype(BF16)
                    qm_s[c] = qh.astype(BF16)
                    dom_s[c] = dom
                    qt_s[c] = qh.T.astype(BF16)
                    dot_s[c] = doh.T.astype(BF16)
                    rest_s[c] = _row_sum(dom.astype(F32) * ov)
                    run_s[c] = jnp.zeros((t, LANES), F32)
                    dq_s[c] = jnp.zeros((t, LANES), F32)

            def key_rows(kblk):
                return pl.ds(pl.multiple_of(kblk * t, t), t)

            def scores_matmul(kblk, chains):
                ks = kb[key_rows(kblk), :]
                for c in chains:
                    z_s[c] = lax.dot_general(qm_s[c], ks, _NT, preferred_element_type=F32)

            def da_matmul(kblk, chains):
                vs = vb[key_rows(kblk), :]
                for c in chains:
                    da_s[c] = lax.dot_general(dom_s[c], vs, _NT, preferred_element_type=F32)

            def front(modes, nxt, prev=None):
                if prev is not None:
                    tail(*prev)
                for c, diag in modes:
                    _sb_scores(z_s.at[c], split_s.at[c], zl_s.at[c], tot_s.at[c], keep_s.at[c], t, diag)
                    suf_s[c] = jnp.dot(split_s[c], tri_s[...], preferred_element_type=F32)
                scores_matmul(nxt, ALL_CHAINS)
                for c, diag in modes:
                    _sb_weights(zl_s.at[c], suf_s.at[c], run_s.at[c], tot_s.at[c], a_s.at[c], t, diag,
                                da_s.at[c], glog_s.at[c], gsplit_s.at[c], gtot_s.at[c])
                    cum_s[c] = jnp.dot(gsplit_s[c], tri_s[...], preferred_element_type=F32)
                da_matmul(nxt, ALL_CHAINS)
                for c, diag in modes:
                    _sb_dscores(glog_s.at[c], cum_s.at[c], rest_s.at[c], gtot_s.at[c], keep_s.at[c],
                                dz_s.at[c], t, diag)

            def tail(kblk, chains):
                ks = kb[key_rows(kblk), :]
                dkt = dkt_acc[kblk]
                dvt = dvt_acc[kblk]
                for c in chains:
                    dq_s[c] += jnp.dot(dz_s[c], ks, preferred_element_type=F32)
                    dkt = dkt + jnp.dot(qt_s[c], dz_s[c], preferred_element_type=F32)
                    dvt = dvt + jnp.dot(dot_s[c], a_s[c], preferred_element_type=F32)
                dkt_acc[kblk] = dkt
                dvt_acc[kblk] = dvt

            top = 2 * qi + 1
            scores_matmul(top, UPPER_CHAINS)
            da_matmul(top, UPPER_CHAINS)
            front([(c, True) for c in UPPER_CHAINS], top - 1)
            front([(c, c not in UPPER_CHAINS) for c in ALL_CHAINS], jnp.maximum(top - 2, 0),
                  prev=(top, UPPER_CHAINS))

            def loop(state):
                it = state[0]
                cur = top - 2 - it
                front([(c, False) for c in ALL_CHAINS], jnp.maximum(cur - 1, 0), prev=(cur + 1, ALL_CHAINS))
                return it + 1, _any_weight_left(run_s)

            done, _ = lax.while_loop(lambda state: (state[0] < top - 1) & (state[1] > 0), loop,
                                     (jnp.int32(0), jnp.int32(1)))
            tail(top - 1 - done, ALL_CHAINS)
            for half in range(2):
                rows = slice(half * t, (half + 1) * t)
                stash[0, rows, :] = (jnp.where(low, dq_s[2 * half], dq_s[2 * half + 1]) * scale).astype(BF16)
                stash[1, rows, :] = dkt_acc[2 * qi + half].T.astype(BF16)
                stash[2, rows, :] = dvt_acc[2 * qi + half].T.astype(BF16)

        dproj_ref[...] = stash[which]

    whole = lambda col0: pl.BlockSpec((s, LANES), lambda h, i, w: (0, col0 + h))
    blk = lambda col0: pl.BlockSpec((tq, LANES), lambda h, i, w: (nq - 1 - i, col0 + h))
    n_ch = len(ALL_CHAINS)
    res = _pcall(
        body, name="attn_bwd", grid=(nh, nq, 3),
        in_specs=[blk(q0), whole(q0 + nh), whole(q0 + 2 * nh), blk(0), blk(w_conv // LANES),
                  pl.BlockSpec((1, LANES), lambda h, i, w: (0, h))] + [pl.BlockSpec(memory_space=pl.ANY)] * nw,
        out_specs=[pl.BlockSpec((tq, LANES), lambda h, i, w: (nq - 1 - i, q0 + w * nh + h)),
                   pl.BlockSpec((1, LANES), lambda h, i, w: (0, h))] + [pl.BlockSpec(memory_space=pl.ANY)] * nw,
        out_shape=[jax.ShapeDtypeStruct((s, n_cols), BF16), jax.ShapeDtypeStruct((1, w_attn), F32)]
        + plan.out_shapes,
        scratch_shapes=[pltpu.VMEM((s, LANES), BF16), pltpu.VMEM((s, LANES), BF16),
                        pltpu.VMEM((s // t, LANES, t), F32),
                        pltpu.VMEM((s // t, LANES, t), F32),
                        pltpu.VMEM((3, tq, LANES), BF16),
                        pltpu.VMEM((2 * t, t), BF16),
                        pltpu.VMEM((n_ch, t, LANES), BF16),
                        pltpu.VMEM((n_ch, t, LANES), BF16),
                        pltpu.VMEM((n_ch, LANES, t), BF16),
                        pltpu.VMEM((n_ch, LANES, t), BF16),
                        pltpu.VMEM((n_ch, t, t), F32),
                        pltpu.VMEM((n_ch, t, t), F32),
                        pltpu.VMEM((n_ch, t, 2 * t), BF16),
                        pltpu.VMEM((n_ch, t, t), F32),
                        pltpu.VMEM((n_ch, t, t), F32),
                        pltpu.VMEM((n_ch, t, t), F32),
                        pltpu.VMEM((n_ch, t, t), BF16),
                        pltpu.VMEM((n_ch, t, t), F32),
                        pltpu.VMEM((n_ch, t, 2 * t), BF16),
                        pltpu.VMEM((n_ch, t, t), F32),
                        pltpu.VMEM((n_ch, t, t), BF16),
                        pltpu.VMEM((n_ch, t, LANES), F32),
                        pltpu.VMEM((n_ch, t, LANES), F32),
                        pltpu.VMEM((n_ch, t, LANES), F32),
                        pltpu.VMEM((n_ch, t, LANES), F32),
                        pltpu.VMEM((n_ch, t, LANES), F32)]
        + plan.sems,
        input_output_aliases=plan.aliases(6, 2),
        compiler_params=_params(("arbitrary", "arbitrary", "arbitrary")),
    )(proj, proj, proj, o, dcat, g_attn, *plan.inputs)
    return res[0], res[1], res[2:]


def _place():
    return lax.axis_index("x"), lax.axis_index("y"), lax.axis_index("c")


def _other_chips(x, y):
    return [(1 - x, y), (x, 1 - y), (1 - x, 1 - y)]


def _slot(px, py, pc):
    return 4 * px + 2 * py + pc


def _all_gather(shards, out_dtypes):
    nw = len(shards)

    def body(*refs):
        ins, outs, stage = refs[:nw], refs[nw:2 * nw], refs[2 * nw:3 * nw]
        send_sems, recv_sems, local_sems = refs[3 * nw:]
        x, y, c = _place()
        me, sibling = (x, y, c), (x, y, 1 - c)
        chips = _other_chips(x, y)

        def copy(w, k, block, to, src=None):
            dst = outs[w].at[_slot(*block)]
            return pltpu.make_async_remote_copy(
                src_ref=dst if src is None else src, dst_ref=dst,
                send_sem=send_sems.at[w * 7 + k], recv_sem=recv_sems.at[w * 7 + k],
                device_id=to, device_id_type=MESH)

        started = []
        local = []
        for w in range(nw):
            stage[w][...] = ins[w][...].astype(stage[w].dtype)
            cp = pltpu.make_async_copy(stage[w], outs[w].at[_slot(*me)], local_sems.at[w])
            cp.start()
            local.append(cp)
            started.append(copy(w, 0, me, sibling, src=stage[w]))
            started[-1].start()
            for j, chip in enumerate(chips):
                started.append(copy(w, 1 + j, me, (*chip, c), src=stage[w]))
                started[-1].start()
        for j, chip in enumerate(chips):
            for w in range(nw):
                copy(w, 1 + j, (*chip, c), me).wait_recv()
                started.append(copy(w, 4 + j, (*chip, c), sibling))
                started[-1].start()
        for w in range(nw):
            copy(w, 0, sibling, me).wait_recv()
            for j, chip in enumerate(chips):
                copy(w, 4 + j, (*chip, 1 - c), me).wait_recv()
        for cp in started:
            cp.wait_send()
        for cp in local:
            cp.wait()

    return _pcall(
        body, name="all_gather_weights",
        in_specs=[pl.BlockSpec(memory_space=pltpu.VMEM)] * nw,
        out_specs=[pl.BlockSpec(memory_space=pl.ANY)] * nw,
        out_shape=[jax.ShapeDtypeStruct((N_DEV, *a.shape), d) for a, d in zip(shards, out_dtypes)],
        scratch_shapes=[pltpu.VMEM(a.shape, d) for a, d in zip(shards, out_dtypes)]
        + [pltpu.SemaphoreType.DMA((7 * nw,)), pltpu.SemaphoreType.DMA((7 * nw,)),
           pltpu.SemaphoreType.DMA((nw,))],
        compiler_params=_params(),
    )(*shards)


N_PEERS = N_DEV - 1


def _peer(k):
    x, y, c = _place()
    return (x ^ (k >> 2), y ^ ((k >> 1) & 1), c ^ (k & 1))


def _remote(src, dst, sems, index, to):
    return pltpu.make_async_remote_copy(src_ref=src, dst_ref=dst, send_sem=sems[0].at[index],
                                        recv_sem=sems[1].at[index], device_id=to, device_id_type=MESH)


def _gather_out_copies(staged, gathered, sems):
    x, y, c = _place()
    me = _slot(x, y, c)
    targets = [(x, y, 1 - c)] + [(*chip, c) for chip in _other_chips(x, y)]
    copies = []
    for w, (src, dst) in enumerate(zip(staged, gathered)):
        copies.append(pltpu.make_async_copy(src, dst.at[me], sems[2].at[w]))
        copies += [_remote(src, dst.at[me], sems, w * len(targets) + k, to) for k, to in enumerate(targets)]
    return copies


def _gather_pass_copies(arrived, gathered, sems):
    x, y, c = _place()
    chips = _other_chips(x, y)
    return [_remote(src.at[_slot(*chip, c)], dst.at[_slot(*chip, c)], sems, w * len(chips) + j, (x, y, 1 - c))
            for w, (src, dst) in enumerate(zip(arrived, gathered)) for j, chip in enumerate(chips)]


def _scatter_copies(partials, received, sems):
    me = _slot(*_place())
    return [_remote(src.at[me ^ k], dst.at[k - 1], sems, w * N_PEERS + k - 1, _peer(k))
            for w, (src, dst) in enumerate(zip(partials, received)) for k in range(1, N_DEV)]


class _Carried:
    COPIES = {"gather_out": (_gather_out_copies, 4, True), "gather_pass": (_gather_pass_copies, 3, False),
              "scatter": (_scatter_copies, N_PEERS, False)}

    def __init__(self, jobs):
        self.jobs = [(kind, list(arrays)) for kind, arrays in jobs if len(arrays)]
        self.inputs = [a for _, arrays in self.jobs for a in arrays]
        self.out_shapes, self.sems, self.sem_counts = [], [], []
        for kind, arrays in self.jobs:
            _, fan, local = self.COPIES[kind]
            for a in arrays:
                shape = {"gather_out": (N_DEV, *a.shape), "gather_pass": a.shape,
                         "scatter": (N_PEERS, *a.shape[1:])}[kind]
                self.out_shapes.append(jax.ShapeDtypeStruct(shape, BF16))
            job_sems = [pltpu.SemaphoreType.DMA((fan * len(arrays),))] * 2
            job_sems += [pltpu.SemaphoreType.DMA((len(arrays),))] if local else []
            self.sems += job_sems
            self.sem_counts.append(len(job_sems))

    def aliases(self, first_input, first_output):
        pairs, at = {}, 0
        for kind, arrays in self.jobs:
            if kind == "gather_pass":
                pairs.update({first_input + at + i: first_output + at + i for i in range(len(arrays))})
            at += len(arrays)
        return pairs

    def copies(self, in_refs, out_refs, sem_refs):
        out, at, sem_at = [], 0, 0
        for (kind, arrays), n_sems in zip(self.jobs, self.sem_counts):
            n = len(arrays)
            out += self.COPIES[kind][0](in_refs[at:at + n], out_refs[at:at + n], sem_refs[sem_at:sem_at + n_sems])
            at, sem_at = at + n, sem_at + n_sems
        return out


def _cast_shards(shards):
    def body(*refs):
        for src, dst in zip(refs[:len(shards)], refs[len(shards):]):
            dst[...] = src[...].astype(BF16)

    return _pcall(
        body, name="cast_shards",
        in_specs=[pl.BlockSpec(memory_space=pltpu.VMEM)] * len(shards),
        out_specs=[pl.BlockSpec(memory_space=pltpu.VMEM)] * len(shards),
        out_shape=[jax.ShapeDtypeStruct(a.shape, BF16) for a in shards],
        compiler_params=_params(),
    )(*shards)


def _all_reduce_small(packed):
    r = packed.shape[0]

    def body(x_ref, o_ref, gathered, send_sems, recv_sems):
        x, y, c = _place()
        me = _slot(x, y, c)
        gathered[me] = x_ref[...]
        peers = [(px, py, pc) for px in range(2) for py in range(2) for pc in range(2)]
        started = []
        for k in range(1, N_DEV):
            to = (x ^ (k >> 2), y ^ ((k >> 1) & 1), c ^ (k & 1))
            cp = pltpu.make_async_remote_copy(
                src_ref=x_ref, dst_ref=gathered.at[me],
                send_sem=send_sems.at[k - 1], recv_sem=recv_sems.at[k - 1],
                device_id=to, device_id_type=MESH)
            cp.start()
            started.append(cp)
        del peers
        for cp in started:
            cp.wait()
        total = gathered[0]
        for k in range(1, N_DEV):
            total = total + gathered[k]
        o_ref[...] = total

    return _pcall(
        body, name="all_reduce_small",
        in_specs=[pl.BlockSpec(memory_space=pltpu.VMEM)],
        out_specs=pl.BlockSpec(memory_space=pltpu.VMEM),
        out_shape=jax.ShapeDtypeStruct(packed.shape, F32),
        scratch_shapes=[pltpu.VMEM((N_DEV, r, LANES), F32),
                        pltpu.SemaphoreType.DMA((N_DEV - 1,)), pltpu.SemaphoreType.DMA((N_DEV - 1,))],
        compiler_params=_params(),
    )(packed)


def _adam_math(w, g, m, v):
    m = ADAM_B1 * m + (1.0 - ADAM_B1) * g
    v = ADAM_B2 * v + (1.0 - ADAM_B2) * jnp.square(g)
    m_hat = m / (1.0 - ADAM_B1 ** ADAM_STEP)
    v_hat = v / (1.0 - ADAM_B2 ** ADAM_STEP)
    delta = -ADAM_LR * (m_hat / (jnp.sqrt(v_hat) + ADAM_EPS) + ADAM_WD * w)
    return delta, m, v


def _adam_sharded(name, own, received, w, m, v, place, tr=256):
    r, cdim = w.shape
    tr = _tile(r, tr) if r % LANES == 0 else r

    def body(place_ref, own_ref, rec_ref, w_ref, m_ref, v_ref, g_ref, d_ref, nm_ref, nv_ref):
        del place_ref
        g = own_ref[...]
        for j in range(N_PEERS):
            g = g + rec_ref[j].astype(F32)
        delta, nm, nv = _adam_math(w_ref[...], g, m_ref[...], v_ref[...])
        g_ref[...] = g
        d_ref[...] = delta
        nm_ref[...] = nm
        nv_ref[...] = nv

    blk = pl.BlockSpec((tr, cdim), lambda i, pr: (i, 0))
    grid_spec = pltpu.PrefetchScalarGridSpec(
        num_scalar_prefetch=1, grid=(r // tr,),
        in_specs=[pl.BlockSpec((None, tr, cdim), lambda i, pr: (4 * pr[0] + 2 * pr[1] + pr[2], i, 0)),
                  pl.BlockSpec((N_PEERS, tr, cdim), lambda i, pr: (0, i, 0)), blk, blk, blk],
        out_specs=[blk] * 4)
    return _pcall(body, name=name, grid_spec=grid_spec,
                  out_shape=[jax.ShapeDtypeStruct((r, cdim), F32)] * 4,
                  compiler_params=_params(("parallel",)))(place, own, received, w, m, v)


def _adam_small(w, g, m, v):
    def body(w_ref, g_ref, m_ref, v_ref, d_ref, nm_ref, nv_ref):
        delta, nm, nv = _adam_math(w_ref[...], g_ref[...], m_ref[...], v_ref[...])
        d_ref[...] = delta
        nm_ref[...] = nm
        nv_ref[...] = nv

    return _pcall(body, name="adam_small",
                  in_specs=[pl.BlockSpec(memory_space=pltpu.VMEM)] * 4,
                  out_specs=[pl.BlockSpec(memory_space=pltpu.VMEM)] * 3,
                  out_shape=[jax.ShapeDtypeStruct(w.shape, F32)] * 3,
                  compiler_params=_params())(w, g, m, v)


def _rows(vec):
    return vec.reshape(-1, LANES)


def kernel(x, p, g_mix, w_in, conv_w, g_conv_out, g_attn_out, w_out, g_mlp, w_up, w_down, g_ple, w_ple_gate, w_ple_proj, g_final, loss_target, m_g_mix, m_w_in, m_conv_w, m_g_conv_out, m_g_attn_out, m_w_out, m_g_mlp, m_w_up, m_w_down, m_g_ple, m_w_ple_gate, m_w_ple_proj, m_g_final, v_g_mix, v_w_in, v_conv_w, v_g_conv_out, v_g_attn_out, v_w_out, v_g_mlp, v_w_up, v_w_down, v_g_ple, v_w_ple_gate, v_w_ple_proj, v_g_final):
    s, d = x.shape[1], x.shape[2]
    w_conv = g_conv_out.shape[1]
    w_attn = g_attn_out.shape[1]
    cw = conv_w.shape[2]
    xs, ps, tgt = x[0], p[0, 0], loss_target[0]
    place = jnp.stack([lax.axis_index("x"), lax.axis_index("y"), lax.axis_index("c")]).astype(jnp.int32)
    my_slot = 4 * place[0] + 2 * place[1] + place[2]

    conv_tile = jnp.pad(conv_w[0], ((0, HALO - CONV_K), (0, LANES - cw)))
    big = [w_in[0], w_out[0], w_up[0], w_down[0], w_ple_gate[0], w_ple_proj[0]]
    win_g, conv_g = _all_gather([big[0], conv_tile], [BF16, F32])
    s_out, s_up, s_down, s_gate, s_proj = _cast_shards(big[1:])
    conv_full = jnp.transpose(conv_g[:, :CONV_K, :cw], (1, 0, 2)).reshape(CONV_K, w_conv)
    in_shard, up_shard, proj_shard = big[0].shape[1], big[2].shape[1], big[5].shape[1]

    a = _rmsnorm_fwd("norm_mix", xs, g_mix)
    proj, g_out, g_gate, g_proj = _mm_nn("in_proj", a, win_g, n_shard=in_shard, tn=in_shard, tm=2048,
                                         carry=[("gather_out", [s_out, s_gate, s_proj])])
    cat = _conv_fwd(proj, conv_full, g_conv_out, w_conv, d)
    o, cat, (g_up, g_down, wout_g, wgate_g, wproj_g) = _attn_fwd(
        proj, g_attn_out, cat, w_conv,
        [("gather_out", [s_up, s_down]), ("gather_pass", [g_out, g_gate, g_proj])])
    wout_f = wout_g.reshape(-1, wout_g.shape[-1])
    wgate_f = wgate_g.reshape(-1, wgate_g.shape[-1])
    h1, wup_g = _mm_nn("out_proj", cat, wout_f, epilogue=_ep_residual, extras=(xs,),
                       carry=[("gather_pass", [g_up])])
    mn = _rmsnorm_fwd("norm_mlp", h1, g_mlp)
    act, wdown_g = _mm_nn("mlp_up", mn, wup_g, n_shard=up_shard, epilogue=_ep_up, out_dtypes=(BF16,), tm=2048,
                          carry=[("gather_pass", [g_down])])
    wdown_f = wdown_g.reshape(-1, wdown_g.shape[-1])
    h2, = _mm_nn("mlp_down", act, wdown_f, epilogue=_ep_residual, extras=(h1,))
    n3 = _rmsnorm_fwd("norm_ple", h2, g_ple)
    gl, = _mm_nn("ple_gate", n3, wgate_f)
    pp = _ple_proj(ps, wproj_g)
    loss_part, dh3, dgl, dpp, dg_final = _ple_loss(h2, gl, pp, tgt, g_final.reshape(1, d))

    def slots(t2d):
        return t2d.reshape(N_DEV, -1, t2d.shape[-1])

    dw_proj = _d_ple_proj(ps, dpp, proj_shard)
    dw_gate = [slots(t) for t in _mm_tn("d_w_ple_gate", n3, dgl)]
    dh2, dh2b, dg_ple = _mm_nt_norm_bwd("d_norm_ple", dgl, wgate_f, h2, g_ple, dh3)
    du, gate_recv, proj_recv = _mm_nt("d_mlp_act", dh2b, wdown_f, epilogue=_ep_dact, out_dtypes=(BF16,),
                                      extras=(act,), carry=[("scatter", [dw_gate[1], dw_proj[1]])])
    dw_down = [slots(t) for t in _mm_tn("d_w_down", act, dh2b)]
    dw_up = _mm_tn("d_w_up", mn, du, n_shard=up_shard)
    dh1, dh1b, dg_mlp = _mm_nt_norm_bwd("d_norm_mlp", du, wup_g, h1, g_mlp, dh2, k_shard=up_shard, tm=1024)
    dcat, = _mm_nt("d_cat", dh1b, wout_f)
    dw_out = [slots(t) for t in _mm_tn("d_w_out", cat, dh1b)]
    dproj, dg_attn, (up_recv, down_recv) = _attn_bwd(proj, o, dcat, g_attn_out, w_conv,
                                                     [("scatter", [dw_up[1], dw_down[1]])])
    dproj, dconv, dg_conv = _conv_bwd(proj, dcat, conv_full, g_conv_out, dproj, w_conv)
    *dw_in, out_recv = _mm_tn("d_w_in", a, dproj, n_shard=in_shard, tn=in_shard,
                              carry=[("scatter", [dw_out[1]])])
    grad_x, _, dg_mix, in_recv = _mm_nt_norm_bwd("d_norm_mix", dproj, win_g, xs, g_mix, dh1, k_shard=in_shard,
                                                 tk=in_shard, tm=1024, carry=[("scatter", [dw_in[1]])])

    names = ["w_in", "w_out", "w_up", "w_down", "w_ple_gate", "w_ple_proj"]
    owns = [dw_in[0], dw_out[0], dw_up[0], dw_down[0], dw_gate[0], dw_proj[0]]
    recvs = [in_recv, out_recv, up_recv, down_recv, gate_recv, proj_recv]
    moments = [(m_w_in, v_w_in), (m_w_out, v_w_out), (m_w_up, v_w_up), (m_w_down, v_w_down),
               (m_w_ple_gate, v_w_ple_gate), (m_w_ple_proj, v_w_ple_proj)]
    big_out = {}
    for n, own, rc, wt, (mm, vv) in zip(names, owns, recvs, big, moments):
        big_out[n] = [t[None] for t in _adam_sharded("adam_" + n, own, rc, wt, mm[0], vv[0], place)]

    n_conv_rows = CONV_K * w_conv // LANES
    small_g = jnp.concatenate(
        [_rows(dg_mix[0]), _rows(dg_conv[0]), _rows(dg_attn[0]), _rows(dg_mlp[0]), _rows(dg_ple[0]),
         _rows(dg_final[0]), _rows(dconv.reshape(-1)), loss_part], axis=0)
    n_gain_rows = small_g.shape[0] - n_conv_rows - 1
    pad_rows = (-small_g.shape[0]) % HALO
    small_g = _all_reduce_small(jnp.pad(small_g, ((0, pad_rows), (0, 0))))
    loss = small_g[n_gain_rows + n_conv_rows, 0]
    dconv_full = small_g[n_gain_rows:n_gain_rows + n_conv_rows].reshape(CONV_K, w_conv)
    dconv_mine = lax.dynamic_slice(dconv_full, (0, my_slot * cw), (CONV_K, cw))

    def pack(vecs, conv_part):
        rows = [_rows(t.reshape(-1)) for t in vecs]
        rows.append(jnp.pad(conv_part, ((0, HALO - CONV_K), (0, LANES - cw))))
        return jnp.concatenate(rows, axis=0)

    gains = [g_mix, g_conv_out, g_attn_out, g_mlp, g_ple, g_final]
    gains_m = [m_g_mix, m_g_conv_out, m_g_attn_out, m_g_mlp, m_g_ple, m_g_final]
    gains_v = [v_g_mix, v_g_conv_out, v_g_attn_out, v_g_mlp, v_g_ple, v_g_final]
    gpack = jnp.concatenate([small_g[:n_gain_rows], jnp.pad(dconv_mine, ((0, HALO - CONV_K), (0, LANES - cw)))], axis=0)
    sd, sm, sv = _adam_small(pack(gains, conv_w[0]), gpack, pack(gains_m, m_conv_w[0]), pack(gains_v, v_conv_w[0]))

    def unpack(packed):
        out, r0 = [], 0
        for t in gains:
            nr = t.size // LANES
            out.append(packed[r0:r0 + nr].reshape(t.shape))
            r0 += nr
        out.append(packed[r0:r0 + CONV_K, :cw][None])
        return out

    sg_l, sd_l, sm_l, sv_l = unpack(gpack), unpack(sd), unpack(sm), unpack(sv)
    small_names = ["g_mix", "g_conv_out", "g_attn_out", "g_mlp", "g_ple", "g_final", "conv_w"]
    small_out = {n: [sg_l[i], sd_l[i], sm_l[i], sv_l[i]] for i, n in enumerate(small_names)}

    order = ["g_mix", "w_in", "conv_w", "g_conv_out", "g_attn_out", "w_out", "g_mlp", "w_up", "w_down",
             "g_ple", "w_ple_gate", "w_ple_proj", "g_final"]
    table = {**big_out, **small_out}
    outs = [loss, grad_x[None]]
    for kind in range(4):
        outs.extend(table[n][kind] for n in order)
    return tuple(outs)
```

```python
import jax
import jax.numpy as jnp
from jax import lax
from jax.experimental import pallas as pl
from jax.experimental.pallas import tpu as pltpu

F32 = jnp.float32
BF16 = jnp.bfloat16
EPS = 1e-6
HEAD_DIM = 64
LANES = 128
CONV_K = 3
ATTN_BLOCK = 256
HALO = 8
N_DEV = 8
MESH = pl.DeviceIdType.MESH
VMEM_LIMIT = 56 * 1024 * 1024

ADAM_LR = 0.001
ADAM_B1 = 0.9
ADAM_B2 = 0.999
ADAM_EPS = 1e-08
ADAM_WD = 0.01
ADAM_STEP = 10


def _pcall(body, **kw):
    return pl.pallas_call(body, **kw)


def _params(sem=None, **kw):
    return pltpu.CompilerParams(dimension_semantics=sem, vmem_limit_bytes=VMEM_LIMIT, **kw)


def _tile(dim, pref):
    t = min(dim, pref)
    while dim % t:
        t -= LANES
    assert t > 0, (dim, pref)
    return t


_NN = (((1,), (0,)), ((), ()))
_NT = (((1,), (1,)), ((), ()))
_TN = (((0,), (0,)), ((), ()))


def _ep_store(acc, outs):
    outs[0][...] = acc.astype(outs[0].dtype)


def _ep_both(acc, outs):
    outs[0][...] = acc
    outs[1][...] = acc.astype(BF16)


def _ep_residual(acc, res, outs):
    outs[0][...] = acc + res[...]


def _ep_up(acc, outs):
    outs[0][...] = jnp.square(jnp.maximum(acc, 0.0)).astype(BF16)


def _ep_dact(acc, act, outs):
    outs[0][...] = (acc * (2.0 * jnp.sqrt(act[...].astype(F32)))).astype(BF16)


def _ep_norm_bwd(acc, h, g, dres, outs):
    @pl.when(pl.program_id(0) == 0)
    def _():
        outs[2][...] = jnp.zeros_like(outs[2])

    hv = h[...]
    r = lax.rsqrt(jnp.mean(hv * hv, axis=-1, keepdims=True) + EPS)
    hn = hv * r
    outs[2][...] += jnp.sum(acc * hn, axis=0, keepdims=True)
    dhn = acc * g[...]
    dh = dres[...] + r * (dhn - hn * jnp.mean(dhn * hn, axis=-1, keepdims=True))
    outs[0][...] = dh
    outs[1][...] = dh.astype(BF16)


def _matmul(name, a, b, *, dims, grid, a_spec, b_spec, acc_shape, out_shapes, out_specs,
            epilogue=_ep_store, extras=(), extra_specs=(), carry=(), sequential=False):
    nk = grid[2]
    plan = _Carried(carry)
    n_ex, n_out, n_xc = len(extras), len(out_shapes), len(plan.inputs)
    n_sems = len(plan.sems)
    last = tuple(g - 1 for g in grid)

    def product(a_ref, b_ref):
        return lax.dot_general(a_ref[...].astype(BF16), b_ref[...].astype(BF16), dims,
                               preferred_element_type=F32)

    def body(a_ref, b_ref, *rest):
        ex, rest = rest[:n_ex], rest[n_ex:]
        partials, rest = rest[:n_xc], rest[n_xc:]
        outs, rest = rest[:n_out], rest[n_out:]
        received, rest = rest[:n_xc], rest[n_xc:]
        ids = [pl.program_id(axis) for axis in range(3)]
        if n_xc:
            @pl.when((ids[0] == 0) & (ids[1] == 0) & (ids[2] == 0))
            def _():
                for cp in plan.copies(partials, received, rest[-n_sems:]):
                    cp.start()

        if nk == 1:
            epilogue(product(a_ref, b_ref), *ex, outs)
        else:
            acc = rest[0]

            @pl.when(ids[2] == 0)
            def _():
                acc[...] = product(a_ref, b_ref)

            @pl.when(ids[2] > 0)
            def _():
                acc[...] += product(a_ref, b_ref)

            @pl.when(ids[2] == nk - 1)
            def _():
                epilogue(acc[...], *ex, outs)

        if n_xc:
            @pl.when((ids[0] == last[0]) & (ids[1] == last[1]) & (ids[2] == last[2]))
            def _():
                for cp in plan.copies(partials, received, rest[-n_sems:]):
                    cp.wait()

    anywhere = [pl.BlockSpec(memory_space=pl.ANY)] * n_xc
    return _pcall(
        body, name=name, grid=grid,
        in_specs=[a_spec, b_spec, *extra_specs, *anywhere],
        out_specs=[*out_specs, *anywhere],
        out_shape=[*out_shapes, *plan.out_shapes],
        scratch_shapes=([] if nk == 1 else [pltpu.VMEM(acc_shape, F32)]) + plan.sems,
        input_output_aliases=plan.aliases(2 + n_ex, n_out),
        compiler_params=_params(("arbitrary",) * 3 if n_xc or sequential else ("parallel", "parallel", "arbitrary")),
    )(a, b, *extras, *plan.inputs)


_NO_CARRY = ()


def _mm_nn(name, a, w, *, n_shard=None, epilogue=_ep_store, out_dtypes=(F32,), extras=(), carry=_NO_CARRY,
           tm=1024, tn=1024, tk=1024):
    m, kd = a.shape
    if n_shard is None:
        n = w.shape[1]
        tn = _tile(n, tn)
        tk = _tile(kd, tk)
        b_spec = pl.BlockSpec((tk, tn), lambda i, j, k: (k, j))
    else:
        n = N_DEV * n_shard
        tn = _tile(n_shard, tn)
        tk = _tile(kd, tk)
        per = n_shard // tn
        b_spec = pl.BlockSpec((None, tk, tn), lambda i, j, k: (j // per, k, j % per))
    tm = _tile(m, tm)
    o_spec = pl.BlockSpec((tm, tn), lambda i, j, k: (i, j))
    return _matmul(
        name, a, w, dims=_NN, grid=(m // tm, n // tn, kd // tk),
        a_spec=pl.BlockSpec((tm, tk), lambda i, j, k: (i, k)), b_spec=b_spec,
        acc_shape=(tm, tn),
        out_shapes=[jax.ShapeDtypeStruct((m, n), d) for d in out_dtypes],
        out_specs=[o_spec] * len(out_dtypes),
        epilogue=epilogue, extras=extras, extra_specs=[o_spec] * len(extras), carry=carry)


def _mm_nt(name, a, w, *, k_shard=None, epilogue=_ep_store, out_dtypes=(F32,), extras=(), carry=_NO_CARRY,
           tm=1024, tn=1024, tk=1024):
    m, kd = a.shape
    if k_shard is None:
        n = w.shape[0]
        tn = _tile(n, tn)
        tk = _tile(kd, tk)
        b_spec = pl.BlockSpec((tn, tk), lambda i, j, k: (j, k))
    else:
        n = w.shape[1]
        tn = _tile(n, tn)
        tk = _tile(k_shard, tk)
        per = k_shard // tk
        b_spec = pl.BlockSpec((None, tn, tk), lambda i, j, k: (k // per, j, k % per))
    tm = _tile(m, tm)
    o_spec = pl.BlockSpec((tm, tn), lambda i, j, k: (i, j))
    return _matmul(
        name, a, w, dims=_NT, grid=(m // tm, n // tn, kd // tk),
        a_spec=pl.BlockSpec((tm, tk), lambda i, j, k: (i, k)), b_spec=b_spec,
        acc_shape=(tm, tn),
        out_shapes=[jax.ShapeDtypeStruct((m, n), d) for d in out_dtypes],
        out_specs=[o_spec] * len(out_dtypes),
        epilogue=epilogue, extras=extras, extra_specs=[o_spec] * len(extras), carry=carry)


def _mm_nt_norm_bwd(name, a, w, h, g, dres, *, k_shard=None, carry=_NO_CARRY, tm=512, tk=1024):
    m, kd = a.shape
    n = h.shape[1]
    if k_shard is None:
        tk = _tile(kd, tk)
        b_spec = pl.BlockSpec((n, tk), lambda i, j, k: (0, k))
    else:
        tk = _tile(k_shard, tk)
        per = k_shard // tk
        b_spec = pl.BlockSpec((None, n, tk), lambda i, j, k: (k // per, 0, k % per))
    tm = _tile(m, tm)
    rows = pl.BlockSpec((tm, n), lambda i, j, k: (i, 0))
    vec = pl.BlockSpec((1, n), lambda i, j, k: (0, 0))
    return _matmul(
        name, a, w, dims=_NT, grid=(m // tm, 1, kd // tk),
        a_spec=pl.BlockSpec((tm, tk), lambda i, j, k: (i, k)), b_spec=b_spec, acc_shape=(tm, n),
        out_shapes=[jax.ShapeDtypeStruct((m, n), F32), jax.ShapeDtypeStruct((m, n), BF16),
                    jax.ShapeDtypeStruct((1, n), F32)],
        out_specs=[rows, rows, vec], epilogue=_ep_norm_bwd,
        extras=(h, g, dres), extra_specs=[rows, vec, rows], carry=carry, sequential=True)


TN_TILE_BYTES = 40 * 1024 * 1024


def _mm_tn(name, a, b, *, n_shard=None, carry=_NO_CARRY, tm=1024, tn=1024):
    t, m = a.shape
    n = b.shape[1]
    tm = _tile(m, tm)
    tn = _tile(n if n_shard is None else n_shard, tn)
    tk = t
    while 2 * 2 * tk * (tm + tn) + 4 * tm * tn * 5 > TN_TILE_BYTES and tk % (2 * LANES) == 0:
        tk //= 2
    if n_shard is None:
        o_spec = pl.BlockSpec((tm, tn), lambda i, j, k: (i, j))
        shape = (m, n)
    else:
        per = n_shard // tn
        o_spec = pl.BlockSpec((None, tm, tn), lambda i, j, k: (j // per, i, j % per))
        shape = (N_DEV, m, n_shard)
    return _matmul(
        name, a, b, dims=_TN, grid=(m // tm, n // tn, t // tk),
        a_spec=pl.BlockSpec((tk, tm), lambda i, j, k: (k, i)),
        b_spec=pl.BlockSpec((tk, tn), lambda i, j, k: (k, j)),
        acc_shape=(tm, tn), epilogue=_ep_both, carry=carry,
        out_shapes=[jax.ShapeDtypeStruct(shape, F32), jax.ShapeDtypeStruct(shape, BF16)],
        out_specs=[o_spec, o_spec])


def _ple_proj(p, w_g, tm=1024):
    s, kd = p.shape
    ns = w_g.shape[2]
    tm = _tile(s, tm)

    def body(p_ref, w_ref, o_ref):
        pv = p_ref[...].astype(BF16)
        for j in range(N_DEV):
            o_ref[:, j * ns:(j + 1) * ns] = jnp.dot(pv, w_ref[j], preferred_element_type=F32)

    return _pcall(body, name="ple_proj", grid=(s // tm,),
                  in_specs=[pl.BlockSpec((tm, kd), lambda i: (i, 0)),
                            pl.BlockSpec((N_DEV, kd, ns), lambda i: (0, 0, 0))],
                  out_specs=pl.BlockSpec((tm, N_DEV * ns), lambda i: (i, 0)),
                  out_shape=jax.ShapeDtypeStruct((s, N_DEV * ns), F32),
                  compiler_params=_params(("parallel",)))(p, w_g)


def _d_ple_proj(p, dpp, ns, tk=1024):
    s, kd = p.shape
    tk = _tile(s, tk)
    nk = s // tk

    def body(p_ref, d_ref, of_ref, ob_ref, acc):
        k = pl.program_id(0)

        @pl.when(k == 0)
        def _():
            acc[...] = jnp.zeros_like(acc)

        pv = p_ref[...].astype(BF16)
        for j in range(N_DEV):
            acc[j] += lax.dot_general(pv, d_ref[:, j * ns:(j + 1) * ns], _TN, preferred_element_type=F32)

        @pl.when(k == nk - 1)
        def _():
            of_ref[...] = acc[...]
            ob_ref[...] = acc[...].astype(BF16)

    whole = pl.BlockSpec((N_DEV, kd, ns), lambda k: (0, 0, 0))
    return _pcall(body, name="d_w_ple_proj", grid=(nk,),
                  in_specs=[pl.BlockSpec((tk, kd), lambda k: (k, 0)),
                            pl.BlockSpec((tk, N_DEV * ns), lambda k: (k, 0))],
                  out_specs=[whole, whole],
                  out_shape=[jax.ShapeDtypeStruct((N_DEV, kd, ns), F32), jax.ShapeDtypeStruct((N_DEV, kd, ns), BF16)],
                  scratch_shapes=[pltpu.VMEM((N_DEV, kd, ns), F32)],
                  compiler_params=_params(("arbitrary",)))(p, dpp)


def _row_spec(tr, d):
    return pl.BlockSpec((tr, d), lambda i: (i, 0))


def _vec_spec(d):
    return pl.BlockSpec((1, d), lambda i: (0, 0))


def _rmsnorm_fwd(name, x, g, tr=512):
    s, d = x.shape
    tr = _tile(s, tr)

    def body(x_ref, g_ref, o_ref):
        xv = x_ref[...]
        r = lax.rsqrt(jnp.mean(xv * xv, axis=-1, keepdims=True) + EPS)
        o_ref[...] = (xv * r * g_ref[...]).astype(BF16)

    return _pcall(body, name=name, grid=(s // tr,),
                  in_specs=[_row_spec(tr, d), _vec_spec(d)], out_specs=_row_spec(tr, d),
                  out_shape=jax.ShapeDtypeStruct((s, d), BF16),
                  compiler_params=_params(("parallel",)))(x, g)


def _ple_loss(h2, gl, pp, tgt, g_final, tr=512):
    s, d = h2.shape
    tr = _tile(s, tr)

    def body(h2_ref, gl_ref, pp_ref, t_ref, g_ref, loss_ref, dh3_ref, dgl_ref, dpp_ref, dg_ref):
        @pl.when(pl.program_id(0) == 0)
        def _():
            dg_ref[...] = jnp.zeros_like(dg_ref)
            loss_ref[...] = jnp.zeros_like(loss_ref)

        gate = jax.nn.sigmoid(gl_ref[...])
        ppv = pp_ref[...]
        h3 = h2_ref[...] + gate * ppv
        r = lax.rsqrt(jnp.mean(h3 * h3, axis=-1, keepdims=True) + EPS)
        hn = h3 * r
        gv = g_ref[...]
        diff = hn * gv - t_ref[...]
        row = jnp.mean(diff * diff, axis=-1, keepdims=True)
        loss_ref[...] += 0.5 * jnp.sum(row, axis=0, keepdims=True)
        dy = diff * (1.0 / d)
        dg_ref[...] += jnp.sum(dy * hn, axis=0, keepdims=True)
        dhn = dy * gv
        dh3 = r * (dhn - hn * jnp.mean(dhn * hn, axis=-1, keepdims=True))
        dh3_ref[...] = dh3
        dgl_ref[...] = (dh3 * ppv * gate * (1.0 - gate)).astype(BF16)
        dpp_ref[...] = (dh3 * gate).astype(BF16)

    return _pcall(body, name="ple_loss", grid=(s // tr,),
                  in_specs=[_row_spec(tr, d)] * 4 + [_vec_spec(d)],
                  out_specs=[_vec_spec(LANES), _row_spec(tr, d), _row_spec(tr, d), _row_spec(tr, d), _vec_spec(d)],
                  out_shape=[jax.ShapeDtypeStruct((1, LANES), F32), jax.ShapeDtypeStruct((s, d), F32),
                             jax.ShapeDtypeStruct((s, d), BF16), jax.ShapeDtypeStruct((s, d), BF16),
                             jax.ShapeDtypeStruct((1, d), F32)],
                  compiler_params=_params(("arbitrary",)))(h2, gl, pp, tgt, g_final)


def _low_half():
    return lax.broadcasted_iota(jnp.int32, (1, LANES), 1) < HEAD_DIM


def _half_mean(v, low):
    s_lo = jnp.sum(jnp.where(low, v, 0.0), axis=-1, keepdims=True)
    s_hi = jnp.sum(jnp.where(low, 0.0, v), axis=-1, keepdims=True)
    return jnp.where(low, s_lo, s_hi) * (1.0 / HEAD_DIM)


def _head_norm_bwd(val, dout, g, low):
    r = lax.rsqrt(_half_mean(val * val, low) + EPS)
    vn = val * r
    dvn = dout * g
    return r * (dvn - vn * _half_mean(dvn * vn, low)), dout * vn


def _conv_taps(vv_ext, w_ref, rows):
    v0 = vv_ext[HALO:]
    v1 = pltpu.roll(vv_ext, 1, 0)[HALO:]
    v2 = pltpu.roll(vv_ext, 2, 0)[HALO:]
    del rows
    return w_ref[2:3, :] * v0 + w_ref[1:2, :] * v1 + w_ref[0:1, :] * v2, (v0, v1, v2)


def _conv_fwd(proj, conv_w, g_conv, w_conv, d_model, tr=512):
    s = proj.shape[0]
    tr = _tile(s, tr)
    hb = tr // HALO

    def main(part):
        return pl.BlockSpec((tr, w_conv), lambda i: (i, part))

    def prev(part):
        return pl.BlockSpec((HALO, w_conv), lambda i: (jnp.maximum(i * hb - 1, 0), part))

    def body(cb_ref, cc_ref, cu_ref, ccp_ref, cup_ref, w_ref, g_ref, o_ref):
        i = pl.program_id(0)
        low = _low_half()
        for j in range(w_conv // LANES):
            cols = slice(j * LANES, (j + 1) * LANES)
            vv_prev = jnp.where(i > 0, ccp_ref[:, cols] * cup_ref[:, cols], 0.0)
            vv_ext = jnp.concatenate([vv_prev, cc_ref[:, cols] * cu_ref[:, cols]], axis=0)
            y, _ = _conv_taps(vv_ext, w_ref.at[:, cols], tr)
            co = cb_ref[:, cols] * y
            r = lax.rsqrt(_half_mean(co * co, low) + EPS)
            o_ref[:, cols] = (co * r * g_ref[:, cols]).astype(BF16)

    return _pcall(
        body, name="conv_fwd", grid=(s // tr,),
        in_specs=[main(0), main(1), main(2), prev(1), prev(2),
                  pl.BlockSpec((CONV_K, w_conv), lambda i: (0, 0)),
                  pl.BlockSpec((1, w_conv), lambda i: (0, 0))],
        out_specs=pl.BlockSpec((tr, w_conv), lambda i: (i, 0)),
        out_shape=jax.ShapeDtypeStruct((s, d_model), BF16),
        compiler_params=_params(("parallel",)),
    )(proj, proj, proj, proj, proj, conv_w, g_conv)


def _conv_bwd(proj, dcat, conv_w, g_conv, dproj, w_conv, tr=512):
    s = proj.shape[0]
    tr = _tile(s, tr)
    hb = tr // HALO
    last = s // HALO - 1
    nt = s // tr

    def main(part):
        return pl.BlockSpec((tr, w_conv), lambda i: (i, part))

    def prev(part):
        return pl.BlockSpec((HALO, w_conv), lambda i: (jnp.maximum(i * hb - 1, 0), part))

    def nxt(part):
        return pl.BlockSpec((HALO, w_conv), lambda i: (jnp.minimum((i + 1) * hb, last), part))

    def body(cb_ref, cc_ref, cu_ref, dc_ref, ccp_ref, cup_ref, cbn_ref, ccn_ref, cun_ref, dcn_ref,
             w_ref, g_ref, dproj_in, dproj_ref, dw_ref, dg_ref):
        del dproj_in
        i = pl.program_id(0)

        @pl.when(i == 0)
        def _():
            dw_ref[...] = jnp.zeros_like(dw_ref)
            dg_ref[...] = jnp.zeros_like(dg_ref)

        low = _low_half()
        n_ext = tr + HALO
        rowid = lax.broadcasted_iota(jnp.int32, (n_ext, 1), 0)
        for j in range(w_conv // LANES):
            cols = slice(j * LANES, (j + 1) * LANES)
            wj = w_ref.at[:, cols]
            cc, cu = cc_ref[:, cols], cu_ref[:, cols]
            vv_prev = jnp.where(i > 0, ccp_ref[:, cols] * cup_ref[:, cols], 0.0)
            vv_ext = jnp.concatenate([vv_prev, cc * cu, ccn_ref[:, cols] * cun_ref[:, cols]], axis=0)
            y_ext, (v0, v1, v2) = _conv_taps(vv_ext, wj, n_ext)
            cb_ext = jnp.concatenate([cb_ref[:, cols], cbn_ref[:, cols]], axis=0)
            dc_ext = jnp.concatenate([dc_ref[:, cols], dcn_ref[:, cols]], axis=0)
            dco, dgn = _head_norm_bwd(cb_ext * y_ext, dc_ext, g_ref[:, cols], low)
            dyc = jnp.where((rowid < tr) | (i < nt - 1), dco * cb_ext, 0.0)
            dvv = (wj[2:3, :] * dyc[:tr] + wj[1:2, :] * pltpu.roll(dyc, n_ext - 1, 0)[:tr]
                   + wj[0:1, :] * pltpu.roll(dyc, n_ext - 2, 0)[:tr])
            dproj_ref[:, cols] = (dco[:tr] * y_ext[:tr]).astype(BF16)
            dproj_ref[:, w_conv + j * LANES:w_conv + (j + 1) * LANES] = (dvv * cu).astype(BF16)
            dproj_ref[:, 2 * w_conv + j * LANES:2 * w_conv + (j + 1) * LANES] = (dvv * cc).astype(BF16)
            dyt = dyc[:tr]
            for tap, shifted in enumerate((v2, v1, v0)):
                dw_ref[tap:tap + 1, cols] += jnp.sum(dyt * shifted[:tr], axis=0, keepdims=True)
            dg_ref[:, cols] += jnp.sum(dgn[:tr], axis=0, keepdims=True)

    n_cols = dproj.shape[1]
    return _pcall(
        body, name="conv_bwd", grid=(nt,),
        in_specs=[main(0), main(1), main(2), main(0),
                  prev(1), prev(2), nxt(0), nxt(1), nxt(2), nxt(0),
                  pl.BlockSpec((CONV_K, w_conv), lambda i: (0, 0)),
                  pl.BlockSpec((1, w_conv), lambda i: (0, 0)),
                  pl.BlockSpec(memory_space=pl.ANY)],
        out_specs=[pl.BlockSpec((tr, 3 * w_conv), lambda i: (i, 0)),
                   pl.BlockSpec((CONV_K, w_conv), lambda i: (0, 0)),
                   pl.BlockSpec((1, w_conv), lambda i: (0, 0))],
        out_shape=[jax.ShapeDtypeStruct((s, n_cols), BF16),
                   jax.ShapeDtypeStruct((CONV_K, w_conv), F32),
                   jax.ShapeDtypeStruct((1, w_conv), F32)],
        input_output_aliases={12: 0},
        compiler_params=_params(("arbitrary",)),
    )(proj, proj, proj, dcat, proj, proj, proj, proj, proj, dcat, conv_w, g_conv, dproj)


STRIP = 16

ALL_CHAINS = (0, 1, 2, 3)
UPPER_CHAINS = (2, 3)


RUN_FLOOR = -104.0


def _any_weight_left(run_s):
    return (jnp.max(run_s[...]) > RUN_FLOOR).astype(jnp.int32)


def _chains(low):
    return [(2 * half + h, half, msk) for half in range(2)
            for h, msk in enumerate((low, jnp.logical_not(low)))]


def _suffix_operator(t):
    r = lax.broadcasted_iota(jnp.int32, (2 * t, t), 0)
    c = lax.broadcasted_iota(jnp.int32, (2 * t, t), 1)
    return jnp.where((r > c) & ((r < t) | (r - t > c)), 1.0, 0.0).astype(BF16)


def _strips(t):
    return [(i, slice(i * STRIP, (i + 1) * STRIP)) for i in range(t // STRIP)]


def _strip_mask(i, t):
    r = lax.broadcasted_iota(jnp.int32, (STRIP, t), 0) + i * STRIP
    c = lax.broadcasted_iota(jnp.int32, (STRIP, t), 1)
    return r > c


def _store_split(ref, rows, val, t):
    hi = val.astype(BF16)
    ref[rows, 0:t] = hi
    ref[rows, t:2 * t] = (val - hi.astype(F32)).astype(BF16)


def _sb_scores(z_s, split_s, zl_s, tot_s, keep_s, t, diag):
    for i, rows in _strips(t):
        z = z_s[rows, :]
        log_beta = jnp.minimum(z, 0.0) - jnp.log(1.0 + jnp.exp(-jnp.abs(z)))
        log_keep = log_beta - z
        if diag:
            log_keep = jnp.where(_strip_mask(i, t), log_keep, 0.0)
        _store_split(split_s, rows, log_keep, t)
        zl_s[rows, :] = log_beta
        tot_s[rows, :] = _row_sum(log_keep)
        if keep_s is not None:
            keep_s[rows, :] = jnp.exp(log_keep)


def _row_sum(v):
    return jnp.broadcast_to(jnp.sum(v, axis=-1, keepdims=True), (v.shape[0], LANES))


def _wide(r, t):
    return jnp.concatenate([r] * (t // LANES), axis=1)


def _sb_weights(zl_s, suf_s, run_s, tot_s, a_s, t, diag, da_s=None, glog_s=None, gsplit_s=None, gtot_s=None):
    for i, rows in _strips(t):
        run = run_s[rows, :]
        a = jnp.exp(zl_s[rows, :] + suf_s[rows, :] + _wide(run, t))
        if diag:
            a = jnp.where(_strip_mask(i, t), a, 0.0)
        ab = a.astype(BF16)
        a_s[rows, :] = ab
        run_s[rows, :] = run + tot_s[rows, :]
        if da_s is not None:
            glog = ab.astype(F32) * da_s[rows, :]
            glog_s[rows, :] = glog
            _store_split(gsplit_s, rows, glog, t)
            gtot_s[rows, :] = _row_sum(glog)


def _sb_dscores(glog_s, cum_s, rest_s, gtot_s, keep_s, dz_s, t, diag):
    for i, rows in _strips(t):
        glog = glog_s[rows, :]
        rest = rest_s[rows, :]
        from_here = _wide(rest, t) - cum_s[rows, :]
        before = from_here - glog
        dz = from_here * keep_s[rows, :] - before
        if diag:
            dz = jnp.where(_strip_mask(i, t), dz, 0.0)
        dz_s[rows, :] = dz.astype(BF16)
        rest_s[rows, :] = rest - gtot_s[rows, :]


def _attn_fwd(proj, g_attn, cat, w_conv, carry, t=ATTN_BLOCK):
    s = proj.shape[0]
    w_attn = g_attn.shape[1]
    nh = w_attn // LANES
    t = _tile(s, t)
    tq = 2 * t
    nq = s // tq
    q0 = 3 * w_conv // LANES
    scale = HEAD_DIM ** -0.5
    plan = _Carried(carry)
    nw = len(plan.inputs)

    def body(q_ref, k_ref, v_ref, g_ref, cat_in, *rest):
        staged_refs, rest = rest[:nw], rest[nw:]
        o_ref, cat_ref = rest[:2]
        gathered_refs, rest = rest[2:2 + nw], rest[2 + nw:]
        kb, vb, tri_s, qm_s, z_s, split_s, zl_s, suf_s, a_s, run_s, tot_s, acc_s = rest[:12]
        gather_sems = rest[12:]
        del cat_in
        qi = pl.program_id(1)

        @pl.when((pl.program_id(0) == 0) & (qi == 0))
        def _():
            for cp in plan.copies(staged_refs, gathered_refs, gather_sems):
                cp.start()

        @pl.when(qi == 0)
        def _():
            kb[...] = k_ref[...].astype(BF16)
            vb[...] = v_ref[...].astype(BF16)
            tri_s[...] = _suffix_operator(t)

        low = _low_half()
        for c, half, msk in _chains(low):
            qm_s[c] = jnp.where(msk, q_ref[half * t:(half + 1) * t, :] * scale, 0.0).astype(BF16)
            run_s[c] = jnp.zeros((t, LANES), F32)
            acc_s[c] = jnp.zeros((t, LANES), F32)

        def key_rows(kblk):
            return pl.ds(pl.multiple_of(kblk * t, t), t)

        def key_block(base, c):
            return key_rows(jnp.maximum(base + c // 2, 0))

        def scores_matmul(base, chains):
            for c in chains:
                z_s[c] = lax.dot_general(qm_s[c], kb[key_block(base, c), :], _NT, preferred_element_type=F32)

        def front(modes, base, prev=None):
            for c, diag in modes:
                _sb_scores(z_s.at[c], split_s.at[c], zl_s.at[c], tot_s.at[c], None, t, diag)
                suf_s[c] = jnp.dot(split_s[c], tri_s[...], preferred_element_type=F32)
            if prev is not None:
                tail(*prev)
            scores_matmul(base - 1, ALL_CHAINS)
            for c, diag in modes:
                _sb_weights(zl_s.at[c], suf_s.at[c], run_s.at[c], tot_s.at[c], a_s.at[c], t, diag)

        def tail(base, chains):
            for c in chains:
                acc_s[c] += jnp.dot(a_s[c], vb[key_block(base, c), :], preferred_element_type=F32)

        first = 2 * qi
        scores_matmul(first, ALL_CHAINS)
        front([(c, True) for c in ALL_CHAINS], first)

        def loop(state):
            it = state[0]
            base = first - 1 - it
            front([(c, False) for c in ALL_CHAINS], base, prev=(base + 1, ALL_CHAINS))
            return it + 1, _any_weight_left(run_s)

        done, live = lax.while_loop(lambda state: (state[0] < first) & (state[1] > 0), loop,
                                    (jnp.int32(0), _any_weight_left(run_s)))
        one_more = (done == first) & (live > 0)

        @pl.when(one_more)
        def _():
            front([(c, False) for c in UPPER_CHAINS], -1, prev=(0, ALL_CHAINS))
            tail(-1, UPPER_CHAINS)

        @pl.when(jnp.logical_not(one_more))
        def _():
            tail(first - done, ALL_CHAINS)

        for half in range(2):
            rows = slice(half * t, (half + 1) * t)
            o = jnp.where(low, acc_s[2 * half], acc_s[2 * half + 1])
            o_ref[rows, :] = o
            r = lax.rsqrt(_half_mean(o * o, low) + EPS)
            cat_ref[rows, :] = (o * r * g_ref[...]).astype(BF16)

        @pl.when((pl.program_id(0) == nh - 1) & (qi == nq - 1))
        def _():
            for cp in plan.copies(staged_refs, gathered_refs, gather_sems):
                cp.wait()

    whole = lambda col0: pl.BlockSpec((s, LANES), lambda h, i: (0, col0 + h))
    n_ch = len(ALL_CHAINS)
    res = _pcall(
        body, name="attn_fwd", grid=(nh, nq),
        in_specs=[pl.BlockSpec((tq, LANES), lambda h, i: (i, q0 + h)),
                  whole(q0 + nh), whole(q0 + 2 * nh),
                  pl.BlockSpec((1, LANES), lambda h, i: (0, h)),
                  pl.BlockSpec(memory_space=pl.ANY)] + [pl.BlockSpec(memory_space=pl.ANY)] * nw,
        out_specs=[pl.BlockSpec((tq, LANES), lambda h, i: (i, h)),
                   pl.BlockSpec((tq, LANES), lambda h, i: (i, w_conv // LANES + h))]
        + [pl.BlockSpec(memory_space=pl.ANY)] * nw,
        out_shape=[jax.ShapeDtypeStruct((s, w_attn), F32),
                   jax.ShapeDtypeStruct(cat.shape, BF16)] + plan.out_shapes,
        scratch_shapes=[pltpu.VMEM((s, LANES), BF16), pltpu.VMEM((s, LANES), BF16),
                        pltpu.VMEM((2 * t, t), BF16),
                        pltpu.VMEM((n_ch, t, LANES), BF16),
                        pltpu.VMEM((n_ch, t, t), F32),
                        pltpu.VMEM((n_ch, t, 2 * t), BF16),
                        pltpu.VMEM((n_ch, t, t), F32),
                        pltpu.VMEM((n_ch, t, t), F32),
                        pltpu.VMEM((n_ch, t, t), BF16),
                        pltpu.VMEM((n_ch, t, LANES), F32),
                        pltpu.VMEM((n_ch, t, LANES), F32),
                        pltpu.VMEM((n_ch, t, LANES), F32)]
        + plan.sems,
        input_output_aliases={4: 1, **plan.aliases(5, 2)},
        compiler_params=_params(("arbitrary", "arbitrary")),
    )(proj, proj, proj, g_attn, cat, *plan.inputs)
    return res[0], res[1], res[2:]


def _attn_bwd(proj, o, dcat, g_attn, w_conv, carry, t=ATTN_BLOCK):
    s, n_cols = proj.shape
    w_attn = g_attn.shape[1]
    nh = w_attn // LANES
    t = _tile(s, t)
    tq = 2 * t
    nq = s // tq
    q0 = 3 * w_conv // LANES
    scale = HEAD_DIM ** -0.5
    plan = _Carried(carry)
    nw = len(plan.inputs)

    def body(q_ref, k_ref, v_ref, o_ref, do_ref, g_ref, *rest):
        partial_refs, rest = rest[:nw], rest[nw:]
        dproj_ref, dg_ref = rest[:2]
        received_refs, rest = rest[2:2 + nw], rest[2 + nw:]
        (kb, vb, dkt_acc, dvt_acc, stash, tri_s, qm_s, dom_s, qt_s, dot_s, z_s, da_s, split_s, zl_s,
         keep_s, suf_s, a_s, glog_s, gsplit_s, cum_s, dz_s, run_s, tot_s, rest_s, gtot_s, dq_s) = rest[:26]
        scatter_sems = rest[26:]
        step_i = pl.program_id(1)
        which = pl.program_id(2)
        qi = nq - 1 - step_i
        head_pair = pl.program_id(0)

        @pl.when((head_pair == 0) & (step_i == 0) & (which == 0))
        def _():
            for cp in plan.copies(partial_refs, received_refs, scatter_sems):
                cp.start()

        @pl.when((head_pair == nh - 1) & (step_i == nq - 1) & (which == 2))
        def _():
            for cp in plan.copies(partial_refs, received_refs, scatter_sems):
                cp.wait()

        @pl.when(which == 0)
        def _():
            @pl.when(step_i == 0)
            def _():
                kb[...] = k_ref[...].astype(BF16)
                vb[...] = v_ref[...].astype(BF16)
                tri_s[...] = _suffix_operator(t)
                dkt_acc[...] = jnp.zeros_like(dkt_acc)
                dvt_acc[...] = jnp.zeros_like(dvt_acc)
                dg_ref[...] = jnp.zeros_like(dg_ref)

            low = _low_half()
            gv = g_ref[...]
            for half in range(2):
                rows = slice(half * t, (half + 1) * t)
                q = q_ref[rows, :] * scale
                ov = o_ref[rows, :]
                d_o, dgn = _head_norm_bwd(ov, do_ref[rows, :], gv, low)
                dg_ref[...] += jnp.sum(dgn, axis=0, keepdims=True)
                for h, msk in enumerate((low, jnp.logical_not(low))):
                    c = 2 * half + h
                    qh = jnp.where(msk, q, 0.0)
                    doh = jnp.where(msk, d_o, 0.0)
                    dom = doh.astype(BF16)
                    qm_s[c] = qh.astype(BF16)
                    dom_s[c] = dom
                    qt_s[c] = qh.T.astype(BF16)
                    dot_s[c] = doh.T.astype(BF16)
                    rest_s[c] = _row_sum(dom.astype(F32) * ov)
                    run_s[c] = jnp.zeros((t, LANES), F32)
                    dq_s[c] = jnp.zeros((t, LANES), F32)

            def key_rows(kblk):
                return pl.ds(pl.multiple_of(kblk * t, t), t)

            def block_of(base, half):
                return jnp.maximum(base + half, 0)

            def scores_matmul(base, chains):
                for c in chains:
                    ks = kb[key_rows(block_of(base, c // 2)), :]
                    z_s[c] = lax.dot_general(qm_s[c], ks, _NT, preferred_element_type=F32)

            def da_matmul(base, chains):
                for c in chains:
                    vs = vb[key_rows(block_of(base, c // 2)), :]
                    da_s[c] = lax.dot_general(dom_s[c], vs, _NT, preferred_element_type=F32)

            def front(modes, base, prev=None):
                if prev is not None:
                    tail(*prev)
                for c, diag in modes:
                    _sb_scores(z_s.at[c], split_s.at[c], zl_s.at[c], tot_s.at[c], keep_s.at[c], t, diag)
                    suf_s[c] = jnp.dot(split_s[c], tri_s[...], preferred_element_type=F32)
                scores_matmul(base - 1, ALL_CHAINS)
                for c, diag in modes:
                    _sb_weights(zl_s.at[c], suf_s.at[c], run_s.at[c], tot_s.at[c], a_s.at[c], t, diag,
                                da_s.at[c], glog_s.at[c], gsplit_s.at[c], gtot_s.at[c])
                    cum_s[c] = jnp.dot(gsplit_s[c], tri_s[...], preferred_element_type=F32)
                da_matmul(base - 1, ALL_CHAINS)
                for c, diag in modes:
                    _sb_dscores(glog_s.at[c], cum_s.at[c], rest_s.at[c], gtot_s.at[c], keep_s.at[c],
                                dz_s.at[c], t, diag)

            def tail(base, chains):
                for half in range(2):
                    mine = [c for c in chains if c // 2 == half]
                    if not mine:
                        continue
                    kblk = block_of(base, half)
                    ks = kb[key_rows(kblk), :]
                    dkt = dkt_acc[kblk]
                    dvt = dvt_acc[kblk]
                    for c in mine:
                        dq_s[c] += jnp.dot(dz_s[c], ks, preferred_element_type=F32)
                        dkt = dkt + jnp.dot(qt_s[c], dz_s[c], preferred_element_type=F32)
                        dvt = dvt + jnp.dot(dot_s[c], a_s[c], preferred_element_type=F32)
                    dkt_acc[kblk] = dkt
                    dvt_acc[kblk] = dvt

            first = 2 * qi
            scores_matmul(first, ALL_CHAINS)
            da_matmul(first, ALL_CHAINS)
            front([(c, True) for c in ALL_CHAINS], first)

            def loop(state):
                it = state[0]
                base = first - 1 - it
                front([(c, False) for c in ALL_CHAINS], base, prev=(base + 1, ALL_CHAINS))
                return it + 1, _any_weight_left(run_s)

            done, live = lax.while_loop(lambda state: (state[0] < first) & (state[1] > 0), loop,
                                        (jnp.int32(0), _any_weight_left(run_s)))
            one_more = (done == first) & (live > 0)

            @pl.when(one_more)
            def _():
                front([(c, False) for c in UPPER_CHAINS], -1, prev=(0, ALL_CHAINS))
                tail(-1, UPPER_CHAINS)

            @pl.when(jnp.logical_not(one_more))
            def _():
                tail(first - done, ALL_CHAINS)

            for half in range(2):
                rows = slice(half * t, (half + 1) * t)
                stash[0, rows, :] = (jnp.where(low, dq_s[2 * half], dq_s[2 * half + 1]) * scale).astype(BF16)
                stash[1, rows, :] = dkt_acc[2 * qi + half].T.astype(BF16)
                stash[2, rows, :] = dvt_acc[2 * qi + half].T.astype(BF16)

        dproj_ref[...] = stash[which]

    whole = lambda col0: pl.BlockSpec((s, LANES), lambda h, i, w: (0, col0 + h))
    blk = lambda col0: pl.BlockSpec((tq, LANES), lambda h, i, w: (nq - 1 - i, col0 + h))
    n_ch = len(ALL_CHAINS)
    res = _pcall(
        body, name="attn_bwd", grid=(nh, nq, 3),
        in_specs=[blk(q0), whole(q0 + nh), whole(q0 + 2 * nh), blk(0), blk(w_conv // LANES),
                  pl.BlockSpec((1, LANES), lambda h, i, w: (0, h))] + [pl.BlockSpec(memory_space=pl.ANY)] * nw,
        out_specs=[pl.BlockSpec((tq, LANES), lambda h, i, w: (nq - 1 - i, q0 + w * nh + h)),
                   pl.BlockSpec((1, LANES), lambda h, i, w: (0, h))] + [pl.BlockSpec(memory_space=pl.ANY)] * nw,
        out_shape=[jax.ShapeDtypeStruct((s, n_cols), BF16), jax.ShapeDtypeStruct((1, w_attn), F32)]
        + plan.out_shapes,
        scratch_shapes=[pltpu.VMEM((s, LANES), BF16), pltpu.VMEM((s, LANES), BF16),
                        pltpu.VMEM((s // t, LANES, t), F32),
                        pltpu.VMEM((s // t, LANES, t), F32),
                        pltpu.VMEM((3, tq, LANES), BF16),
                        pltpu.VMEM((2 * t, t), BF16),
                        pltpu.VMEM((n_ch, t, LANES), BF16),
                        pltpu.VMEM((n_ch, t, LANES), BF16),
                        pltpu.VMEM((n_ch, LANES, t), BF16),
                        pltpu.VMEM((n_ch, LANES, t), BF16),
                        pltpu.VMEM((n_ch, t, t), F32),
                        pltpu.VMEM((n_ch, t, t), F32),
                        pltpu.VMEM((n_ch, t, 2 * t), BF16),
                        pltpu.VMEM((n_ch, t, t), F32),
                        pltpu.VMEM((n_ch, t, t), F32),
                        pltpu.VMEM((n_ch, t, t), F32),
                        pltpu.VMEM((n_ch, t, t), BF16),
                        pltpu.VMEM((n_ch, t, t), F32),
                        pltpu.VMEM((n_ch, t, 2 * t), BF16),
                        pltpu.VMEM((n_ch, t, t), F32),
                        pltpu.VMEM((n_ch, t, t), BF16),
                        pltpu.VMEM((n_ch, t, LANES), F32),
                        pltpu.VMEM((n_ch, t, LANES), F32),
                        pltpu.VMEM((n_ch, t, LANES), F32),
                        pltpu.VMEM((n_ch, t, LANES), F32),
                        pltpu.VMEM((n_ch, t, LANES), F32)]
        + plan.sems,
        input_output_aliases=plan.aliases(6, 2),
        compiler_params=_params(("arbitrary", "arbitrary", "arbitrary")),
    )(proj, proj, proj, o, dcat, g_attn, *plan.inputs)
    return res[0], res[1], res[2:]


def _place():
    return lax.axis_index("x"), lax.axis_index("y"), lax.axis_index("c")


def _other_chips(x, y):
    return [(1 - x, y), (x, 1 - y), (1 - x, 1 - y)]


def _slot(px, py, pc):
    return 4 * px + 2 * py + pc


def _all_gather(shards, out_dtypes):
    nw = len(shards)

    def body(*refs):
        ins, outs, stage = refs[:nw], refs[nw:2 * nw], refs[2 * nw:3 * nw]
        send_sems, recv_sems, local_sems = refs[3 * nw:]
        x, y, c = _place()
        me, sibling = (x, y, c), (x, y, 1 - c)
        chips = _other_chips(x, y)

        def copy(w, k, block, to, src=None):
            dst = outs[w].at[_slot(*block)]
            return pltpu.make_async_remote_copy(
                src_ref=dst if src is None else src, dst_ref=dst,
                send_sem=send_sems.at[w * 7 + k], recv_sem=recv_sems.at[w * 7 + k],
                device_id=to, device_id_type=MESH)

        started = []
        local = []
        for w in range(nw):
            stage[w][...] = ins[w][...].astype(stage[w].dtype)
            cp = pltpu.make_async_copy(stage[w], outs[w].at[_slot(*me)], local_sems.at[w])
            cp.start()
            local.append(cp)
            started.append(copy(w, 0, me, sibling, src=stage[w]))
            started[-1].start()
            for j, chip in enumerate(chips):
                started.append(copy(w, 1 + j, me, (*chip, c), src=stage[w]))
                started[-1].start()
        for j, chip in enumerate(chips):
            for w in range(nw):
                copy(w, 1 + j, (*chip, c), me).wait_recv()
                started.append(copy(w, 4 + j, (*chip, c), sibling))
                started[-1].start()
        for w in range(nw):
            copy(w, 0, sibling, me).wait_recv()
            for j, chip in enumerate(chips):
                copy(w, 4 + j, (*chip, 1 - c), me).wait_recv()
        for cp in started:
            cp.wait_send()
        for cp in local:
            cp.wait()

    return _pcall(
        body, name="all_gather_weights",
        in_specs=[pl.BlockSpec(memory_space=pltpu.VMEM)] * nw,
        out_specs=[pl.BlockSpec(memory_space=pl.ANY)] * nw,
        out_shape=[jax.ShapeDtypeStruct((N_DEV, *a.shape), d) for a, d in zip(shards, out_dtypes)],
        scratch_shapes=[pltpu.VMEM(a.shape, d) for a, d in zip(shards, out_dtypes)]
        + [pltpu.SemaphoreType.DMA((7 * nw,)), pltpu.SemaphoreType.DMA((7 * nw,)),
           pltpu.SemaphoreType.DMA((nw,))],
        compiler_params=_params(),
    )(*shards)


N_PEERS = N_DEV - 1


def _peer(k):
    x, y, c = _place()
    return (x ^ (k >> 2), y ^ ((k >> 1) & 1), c ^ (k & 1))


def _remote(src, dst, sems, index, to):
    return pltpu.make_async_remote_copy(src_ref=src, dst_ref=dst, send_sem=sems[0].at[index],
                                        recv_sem=sems[1].at[index], device_id=to, device_id_type=MESH)


def _gather_out_copies(staged, gathered, sems):
    x, y, c = _place()
    me = _slot(x, y, c)
    targets = [(x, y, 1 - c)] + [(*chip, c) for chip in _other_chips(x, y)]
    copies = []
    for w, (src, dst) in enumerate(zip(staged, gathered)):
        copies.append(pltpu.make_async_copy(src, dst.at[me], sems[2].at[w]))
        copies += [_remote(src, dst.at[me], sems, w * len(targets) + k, to) for k, to in enumerate(targets)]
    return copies


def _gather_pass_copies(arrived, gathered, sems):
    x, y, c = _place()
    chips = _other_chips(x, y)
    return [_remote(src.at[_slot(*chip, c)], dst.at[_slot(*chip, c)], sems, w * len(chips) + j, (x, y, 1 - c))
            for w, (src, dst) in enumerate(zip(arrived, gathered)) for j, chip in enumerate(chips)]


def _scatter_copies(partials, received, sems):
    me = _slot(*_place())
    return [_remote(src.at[me ^ k], dst.at[k - 1], sems, w * N_PEERS + k - 1, _peer(k))
            for w, (src, dst) in enumerate(zip(partials, received)) for k in range(1, N_DEV)]


class _Carried:
    COPIES = {"gather_out": (_gather_out_copies, 4, True), "gather_pass": (_gather_pass_copies, 3, False),
              "scatter": (_scatter_copies, N_PEERS, False)}

    def __init__(self, jobs):
        self.jobs = [(kind, list(arrays)) for kind, arrays in jobs if len(arrays)]
        self.inputs = [a for _, arrays in self.jobs for a in arrays]
        self.out_shapes, self.sems, self.sem_counts = [], [], []
        for kind, arrays in self.jobs:
            _, fan, local = self.COPIES[kind]
            for a in arrays:
                shape = {"gather_out": (N_DEV, *a.shape), "gather_pass": a.shape,
                         "scatter": (N_PEERS, *a.shape[1:])}[kind]
                self.out_shapes.append(jax.ShapeDtypeStruct(shape, BF16))
            job_sems = [pltpu.SemaphoreType.DMA((fan * len(arrays),))] * 2
            job_sems += [pltpu.SemaphoreType.DMA((len(arrays),))] if local else []
            self.sems += job_sems
            self.sem_counts.append(len(job_sems))

    def aliases(self, first_input, first_output):
        pairs, at = {}, 0
        for kind, arrays in self.jobs:
            if kind == "gather_pass":
                pairs.update({first_input + at + i: first_output + at + i for i in range(len(arrays))})
            at += len(arrays)
        return pairs

    def copies(self, in_refs, out_refs, sem_refs):
        out, at, sem_at = [], 0, 0
        for (kind, arrays), n_sems in zip(self.jobs, self.sem_counts):
            n = len(arrays)
            out += self.COPIES[kind][0](in_refs[at:at + n], out_refs[at:at + n], sem_refs[sem_at:sem_at + n_sems])
            at, sem_at = at + n, sem_at + n_sems
        return out


def _cast_shards(shards):
    def body(*refs):
        for src, dst in zip(refs[:len(shards)], refs[len(shards):]):
            dst[...] = src[...].astype(BF16)

    return _pcall(
        body, name="cast_shards",
        in_specs=[pl.BlockSpec(memory_space=pltpu.VMEM)] * len(shards),
        out_specs=[pl.BlockSpec(memory_space=pltpu.VMEM)] * len(shards),
        out_shape=[jax.ShapeDtypeStruct(a.shape, BF16) for a in shards],
        compiler_params=_params(),
    )(*shards)


def _all_reduce_small(packed):
    r = packed.shape[0]

    def body(x_ref, o_ref, gathered, send_sems, recv_sems):
        x, y, c = _place()
        me = _slot(x, y, c)
        gathered[me] = x_ref[...]
        peers = [(px, py, pc) for px in range(2) for py in range(2) for pc in range(2)]
        started = []
        for k in range(1, N_DEV):
            to = (x ^ (k >> 2), y ^ ((k >> 1) & 1), c ^ (k & 1))
            cp = pltpu.make_async_remote_copy(
                src_ref=x_ref, dst_ref=gathered.at[me],
                send_sem=send_sems.at[k - 1], recv_sem=recv_sems.at[k - 1],
                device_id=to, device_id_type=MESH)
            cp.start()
            started.append(cp)
        del peers
        for cp in started:
            cp.wait()
        total = gathered[0]
        for k in range(1, N_DEV):
            total = total + gathered[k]
        o_ref[...] = total

    return _pcall(
        body, name="all_reduce_small",
        in_specs=[pl.BlockSpec(memory_space=pltpu.VMEM)],
        out_specs=pl.BlockSpec(memory_space=pltpu.VMEM),
        out_shape=jax.ShapeDtypeStruct(packed.shape, F32),
        scratch_shapes=[pltpu.VMEM((N_DEV, r, LANES), F32),
                        pltpu.SemaphoreType.DMA((N_DEV - 1,)), pltpu.SemaphoreType.DMA((N_DEV - 1,))],
        compiler_params=_params(),
    )(packed)


def _adam_math(w, g, m, v):
    m = ADAM_B1 * m + (1.0 - ADAM_B1) * g
    v = ADAM_B2 * v + (1.0 - ADAM_B2) * jnp.square(g)
    m_hat = m / (1.0 - ADAM_B1 ** ADAM_STEP)
    v_hat = v / (1.0 - ADAM_B2 ** ADAM_STEP)
    delta = -ADAM_LR * (m_hat / (jnp.sqrt(v_hat) + ADAM_EPS) + ADAM_WD * w)
    return delta, m, v


def _adam_sharded(name, own, received, w, m, v, place, tr=256):
    r, cdim = w.shape
    tr = _tile(r, tr) if r % LANES == 0 else r

    def body(place_ref, own_ref, rec_ref, w_ref, m_ref, v_ref, g_ref, d_ref, nm_ref, nv_ref):
        del place_ref
        g = own_ref[...]
        for j in range(N_PEERS):
            g = g + rec_ref[j].astype(F32)
        delta, nm, nv = _adam_math(w_ref[...], g, m_ref[...], v_ref[...])
        g_ref[...] = g
        d_ref[...] = delta
        nm_ref[...] = nm
        nv_ref[...] = nv

    blk = pl.BlockSpec((tr, cdim), lambda i, pr: (i, 0))
    grid_spec = pltpu.PrefetchScalarGridSpec(
        num_scalar_prefetch=1, grid=(r // tr,),
        in_specs=[pl.BlockSpec((None, tr, cdim), lambda i, pr: (4 * pr[0] + 2 * pr[1] + pr[2], i, 0)),
                  pl.BlockSpec((N_PEERS, tr, cdim), lambda i, pr: (0, i, 0)), blk, blk, blk],
        out_specs=[blk] * 4)
    return _pcall(body, name=name, grid_spec=grid_spec,
                  out_shape=[jax.ShapeDtypeStruct((r, cdim), F32)] * 4,
                  compiler_params=_params(("parallel",)))(place, own, received, w, m, v)


def _adam_small(w, g, m, v):
    def body(w_ref, g_ref, m_ref, v_ref, d_ref, nm_ref, nv_ref):
        delta, nm, nv = _adam_math(w_ref[...], g_ref[...], m_ref[...], v_ref[...])
        d_ref[...] = delta
        nm_ref[...] = nm
        nv_ref[...] = nv

    return _pcall(body, name="adam_small",
                  in_specs=[pl.BlockSpec(memory_space=pltpu.VMEM)] * 4,
                  out_specs=[pl.BlockSpec(memory_space=pltpu.VMEM)] * 3,
                  out_shape=[jax.ShapeDtypeStruct(w.shape, F32)] * 3,
                  compiler_params=_params())(w, g, m, v)


def _rows(vec):
    return vec.reshape(-1, LANES)


def kernel(x, p, g_mix, w_in, conv_w, g_conv_out, g_attn_out, w_out, g_mlp, w_up, w_down, g_ple, w_ple_gate, w_ple_proj, g_final, loss_target, m_g_mix, m_w_in, m_conv_w, m_g_conv_out, m_g_attn_out, m_w_out, m_g_mlp, m_w_up, m_w_down, m_g_ple, m_w_ple_gate, m_w_ple_proj, m_g_final, v_g_mix, v_w_in, v_conv_w, v_g_conv_out, v_g_attn_out, v_w_out, v_g_mlp, v_w_up, v_w_down, v_g_ple, v_w_ple_gate, v_w_ple_proj, v_g_final):
    s, d = x.shape[1], x.shape[2]
    w_conv = g_conv_out.shape[1]
    w_attn = g_attn_out.shape[1]
    cw = conv_w.shape[2]
    xs, ps, tgt = x[0], p[0, 0], loss_target[0]
    place = jnp.stack([lax.axis_index("x"), lax.axis_index("y"), lax.axis_index("c")]).astype(jnp.int32)
    my_slot = 4 * place[0] + 2 * place[1] + place[2]

    conv_tile = jnp.pad(conv_w[0], ((0, HALO - CONV_K), (0, LANES - cw)))
    big = [w_in[0], w_out[0], w_up[0], w_down[0], w_ple_gate[0], w_ple_proj[0]]
    win_g, conv_g = _all_gather([big[0], conv_tile], [BF16, F32])
    s_out, s_up, s_down, s_gate, s_proj = _cast_shards(big[1:])
    conv_full = jnp.transpose(conv_g[:, :CONV_K, :cw], (1, 0, 2)).reshape(CONV_K, w_conv)
    in_shard, up_shard, proj_shard = big[0].shape[1], big[2].shape[1], big[5].shape[1]

    a = _rmsnorm_fwd("norm_mix", xs, g_mix)
    proj, g_out, g_gate, g_proj = _mm_nn("in_proj", a, win_g, n_shard=in_shard, tn=in_shard, tm=2048,
                                         carry=[("gather_out", [s_out, s_gate, s_proj])])
    cat = _conv_fwd(proj, conv_full, g_conv_out, w_conv, d)
    o, cat, (g_up, g_down, wout_g, wgate_g, wproj_g) = _attn_fwd(
        proj, g_attn_out, cat, w_conv,
        [("gather_out", [s_up, s_down]), ("gather_pass", [g_out, g_gate, g_proj])])
    wout_f = wout_g.reshape(-1, wout_g.shape[-1])
    wgate_f = wgate_g.reshape(-1, wgate_g.shape[-1])
    h1, wup_g = _mm_nn("out_proj", cat, wout_f, epilogue=_ep_residual, extras=(xs,),
                       carry=[("gather_pass", [g_up])])
    mn = _rmsnorm_fwd("norm_mlp", h1, g_mlp)
    act, wdown_g = _mm_nn("mlp_up", mn, wup_g, n_shard=up_shard, epilogue=_ep_up, out_dtypes=(BF16,), tm=2048,
                          carry=[("gather_pass", [g_down])])
    wdown_f = wdown_g.reshape(-1, wdown_g.shape[-1])
    h2, = _mm_nn("mlp_down", act, wdown_f, epilogue=_ep_residual, extras=(h1,))
    n3 = _rmsnorm_fwd("norm_ple", h2, g_ple)
    gl, = _mm_nn("ple_gate", n3, wgate_f)
    pp = _ple_proj(ps, wproj_g)
    loss_part, dh3, dgl, dpp, dg_final = _ple_loss(h2, gl, pp, tgt, g_final.reshape(1, d))

    def slots(t2d):
        return t2d.reshape(N_DEV, -1, t2d.shape[-1])

    dw_proj = _d_ple_proj(ps, dpp, proj_shard)
    dw_gate = [slots(t) for t in _mm_tn("d_w_ple_gate", n3, dgl)]
    dh2, dh2b, dg_ple = _mm_nt_norm_bwd("d_norm_ple", dgl, wgate_f, h2, g_ple, dh3)
    du, gate_recv, proj_recv = _mm_nt("d_mlp_act", dh2b, wdown_f, epilogue=_ep_dact, out_dtypes=(BF16,),
                                      extras=(act,), carry=[("scatter", [dw_gate[1], dw_proj[1]])])
    dw_down = [slots(t) for t in _mm_tn("d_w_down", act, dh2b)]
    dw_up = _mm_tn("d_w_up", mn, du, n_shard=up_shard)
    dh1, dh1b, dg_mlp = _mm_nt_norm_bwd("d_norm_mlp", du, wup_g, h1, g_mlp, dh2, k_shard=up_shard, tm=1024)
    dcat, = _mm_nt("d_cat", dh1b, wout_f)
    dw_out = [slots(t) for t in _mm_tn("d_w_out", cat, dh1b)]
    dproj, dg_attn, (up_recv, down_recv) = _attn_bwd(proj, o, dcat, g_attn_out, w_conv,
                                                     [("scatter", [dw_up[1], dw_down[1]])])
    dproj, dconv, dg_conv = _conv_bwd(proj, dcat, conv_full, g_conv_out, dproj, w_conv)
    *dw_in, out_recv = _mm_tn("d_w_in", a, dproj, n_shard=in_shard, tn=in_shard,
                              carry=[("scatter", [dw_out[1]])])
    grad_x, _, dg_mix, in_recv = _mm_nt_norm_bwd("d_norm_mix", dproj, win_g, xs, g_mix, dh1, k_shard=in_shard,
                                                 tk=in_shard, tm=1024, carry=[("scatter", [dw_in[1]])])

    names = ["w_in", "w_out", "w_up", "w_down", "w_ple_gate", "w_ple_proj"]
    owns = [dw_in[0], dw_out[0], dw_up[0], dw_down[0], dw_gate[0], dw_proj[0]]
    recvs = [in_recv, out_recv, up_recv, down_recv, gate_recv, proj_recv]
    moments = [(m_w_in, v_w_in), (m_w_out, v_w_out), (m_w_up, v_w_up), (m_w_down, v_w_down),
               (m_w_ple_gate, v_w_ple_gate), (m_w_ple_proj, v_w_ple_proj)]
    big_out = {}
    for n, own, rc, wt, (mm, vv) in zip(names, owns, recvs, big, moments):
        big_out[n] = [t[None] for t in _adam_sharded("adam_" + n, own, rc, wt, mm[0], vv[0], place)]

    n_conv_rows = CONV_K * w_conv // LANES
    small_g = jnp.concatenate(
        [_rows(dg_mix[0]), _rows(dg_conv[0]), _rows(dg_attn[0]), _rows(dg_mlp[0]), _rows(dg_ple[0]),
         _rows(dg_final[0]), _rows(dconv.reshape(-1)), loss_part], axis=0)
    n_gain_rows = small_g.shape[0] - n_conv_rows - 1
    pad_rows = (-small_g.shape[0]) % HALO
    small_g = _all_reduce_small(jnp.pad(small_g, ((0, pad_rows), (0, 0))))
    loss = small_g[n_gain_rows + n_conv_rows, 0]
    dconv_full = small_g[n_gain_rows:n_gain_rows + n_conv_rows].reshape(CONV_K, w_conv)
    dconv_mine = lax.dynamic_slice(dconv_full, (0, my_slot * cw), (CONV_K, cw))

    def pack(vecs, conv_part):
        rows = [_rows(t.reshape(-1)) for t in vecs]
        rows.append(jnp.pad(conv_part, ((0, HALO - CONV_K), (0, LANES - cw))))
        return jnp.concatenate(rows, axis=0)

    gains = [g_mix, g_conv_out, g_attn_out, g_mlp, g_ple, g_final]
    gains_m = [m_g_mix, m_g_conv_out, m_g_attn_out, m_g_mlp, m_g_ple, m_g_final]
    gains_v = [v_g_mix, v_g_conv_out, v_g_attn_out, v_g_mlp, v_g_ple, v_g_final]
    gpack = jnp.concatenate([small_g[:n_gain_rows], jnp.pad(dconv_mine, ((0, HALO - CONV_K), (0, LANES - cw)))], axis=0)
    sd, sm, sv = _adam_small(pack(gains, conv_w[0]), gpack, pack(gains_m, m_conv_w[0]), pack(gains_v, v_conv_w[0]))

    def unpack(packed):
        out, r0 = [], 0
        for t in gains:
            nr = t.size // LANES
            out.append(packed[r0:r0 + nr].reshape(t.shape))
            r0 += nr
        out.append(packed[r0:r0 + CONV_K, :cw][None])
        return out

    sg_l, sd_l, sm_l, sv_l = unpack(gpack), unpack(sd), unpack(sm), unpack(sv)
    small_names = ["g_mix", "g_conv_out", "g_attn_out", "g_mlp", "g_ple", "g_final", "conv_w"]
    small_out = {n: [sg_l[i], sd_l[i], sm_l[i], sv_l[i]] for i, n in enumerate(small_names)}

    order = ["g_mix", "w_in", "conv_w", "g_conv_out", "g_attn_out", "w_out", "g_mlp", "w_up", "w_down",
             "g_ple", "w_ple_gate", "w_ple_proj", "g_final"]
    table = {**big_out, **small_out}
    outs = [loss, grad_x[None]]
    for kind in range(4):
        outs.extend(table[n][kind] for n in order)
    return tuple(outs)
```

```python
import jax
import jax.numpy as jnp
from jax import lax
from jax.experimental import pallas as pl
from jax.experimental.pallas import tpu as pltpu

F32 = jnp.float32
BF16 = jnp.bfloat16
EPS = 1e-6
HEAD_DIM = 64
LANES = 128
CONV_K = 3
ATTN_BLOCK = 256
HALO = 8
N_DEV = 8
MESH = pl.DeviceIdType.MESH
VMEM_LIMIT = 56 * 1024 * 1024

ADAM_LR = 0.001
ADAM_B1 = 0.9
ADAM_B2 = 0.999
ADAM_EPS = 1e-08
ADAM_WD = 0.01
ADAM_STEP = 10


def _pcall(body, **kw):
    return pl.pallas_call(body, **kw)


def _params(sem=None, **kw):
    return pltpu.CompilerParams(dimension_semantics=sem, vmem_limit_bytes=VMEM_LIMIT, **kw)


def _tile(dim, pref):
    t = min(dim, pref)
    while dim % t:
        t -= LANES
    assert t > 0, (dim, pref)
    return t


_NN = (((1,), (0,)), ((), ()))
_NT = (((1,), (1,)), ((), ()))
_TN = (((0,), (0,)), ((), ()))


def _ep_store(acc, outs):
    outs[0][...] = acc.astype(outs[0].dtype)


def _ep_both(acc, outs):
    outs[0][...] = acc
    outs[1][...] = acc.astype(BF16)


def _ep_residual(acc, res, outs):
    outs[0][...] = acc + res[...]


def _ep_up(acc, outs):
    outs[0][...] = jnp.square(jnp.maximum(acc, 0.0)).astype(BF16)


def _ep_dact(acc, act, outs):
    outs[0][...] = (acc * (2.0 * jnp.sqrt(act[...].astype(F32)))).astype(BF16)


def _ep_norm_bwd(acc, h, g, dres, outs):
    @pl.when(pl.program_id(0) == 0)
    def _():
        outs[2][...] = jnp.zeros_like(outs[2])

    hv = h[...]
    r = lax.rsqrt(jnp.mean(hv * hv, axis=-1, keepdims=True) + EPS)
    hn = hv * r
    outs[2][...] += jnp.sum(acc * hn, axis=0, keepdims=True)
    dhn = acc * g[...]
    dh = dres[...] + r * (dhn - hn * jnp.mean(dhn * hn, axis=-1, keepdims=True))
    outs[0][...] = dh
    outs[1][...] = dh.astype(BF16)


def _matmul(name, a, b, *, dims, grid, a_spec, b_spec, acc_shape, out_shapes, out_specs,
            epilogue=_ep_store, extras=(), extra_specs=(), carry=(), sequential=False):
    nk = grid[2]
    plan = _Carried(carry)
    n_ex, n_out, n_xc = len(extras), len(out_shapes), len(plan.inputs)
    n_sems = len(plan.sems)
    last = tuple(g - 1 for g in grid)

    def product(a_ref, b_ref):
        return lax.dot_general(a_ref[...].astype(BF16), b_ref[...].astype(BF16), dims,
                               preferred_element_type=F32)

    def body(a_ref, b_ref, *rest):
        ex, rest = rest[:n_ex], rest[n_ex:]
        partials, rest = rest[:n_xc], rest[n_xc:]
        outs, rest = rest[:n_out], rest[n_out:]
        received, rest = rest[:n_xc], rest[n_xc:]
        ids = [pl.program_id(axis) for axis in range(3)]
        if n_xc:
            @pl.when((ids[0] == 0) & (ids[1] == 0) & (ids[2] == 0))
            def _():
                for cp in plan.copies(partials, received, rest[-n_sems:]):
                    cp.start()

        if nk == 1:
            epilogue(product(a_ref, b_ref), *ex, outs)
        else:
            acc = rest[0]

            @pl.when(ids[2] == 0)
            def _():
                acc[...] = product(a_ref, b_ref)

            @pl.when(ids[2] > 0)
            def _():
                acc[...] += product(a_ref, b_ref)

            @pl.when(ids[2] == nk - 1)
            def _():
                epilogue(acc[...], *ex, outs)

        if n_xc:
            @pl.when((ids[0] == last[0]) & (ids[1] == last[1]) & (ids[2] == last[2]))
            def _():
                for cp in plan.copies(partials, received, rest[-n_sems:]):
                    cp.wait()

    anywhere = [pl.BlockSpec(memory_space=pl.ANY)] * n_xc
    return _pcall(
        body, name=name, grid=grid,
        in_specs=[a_spec, b_spec, *extra_specs, *anywhere],
        out_specs=[*out_specs, *anywhere],
        out_shape=[*out_shapes, *plan.out_shapes],
        scratch_shapes=([] if nk == 1 else [pltpu.VMEM(acc_shape, F32)]) + plan.sems,
        input_output_aliases=plan.aliases(2 + n_ex, n_out),
        compiler_params=_params(("arbitrary",) * 3 if n_xc or sequential else ("parallel", "parallel", "arbitrary")),
    )(a, b, *extras, *plan.inputs)


_NO_CARRY = ()


def _mm_nn(name, a, w, *, n_shard=None, epilogue=_ep_store, out_dtypes=(F32,), extras=(), carry=_NO_CARRY,
           tm=1024, tn=1024, tk=1024):
    m, kd = a.shape
    if n_shard is None:
        n = w.shape[1]
        tn = _tile(n, tn)
        tk = _tile(kd, tk)
        b_spec = pl.BlockSpec((tk, tn), lambda i, j, k: (k, j))
    else:
        n = N_DEV * n_shard
        tn = _tile(n_shard, tn)
        tk = _tile(kd, tk)
        per = n_shard // tn
        b_spec = pl.BlockSpec((None, tk, tn), lambda i, j, k: (j // per, k, j % per))
    tm = _tile(m, tm)
    o_spec = pl.BlockSpec((tm, tn), lambda i, j, k: (i, j))
    return _matmul(
        name, a, w, dims=_NN, grid=(m // tm, n // tn, kd // tk),
        a_spec=pl.BlockSpec((tm, tk), lambda i, j, k: (i, k)), b_spec=b_spec,
        acc_shape=(tm, tn),
        out_shapes=[jax.ShapeDtypeStruct((m, n), d) for d in out_dtypes],
        out_specs=[o_spec] * len(out_dtypes),
        epilogue=epilogue, extras=extras, extra_specs=[o_spec] * len(extras), carry=carry)


def _mm_nt(name, a, w, *, k_shard=None, epilogue=_ep_store, out_dtypes=(F32,), extras=(), carry=_NO_CARRY,
           tm=1024, tn=1024, tk=1024):
    m, kd = a.shape
    if k_shard is None:
        n = w.shape[0]
        tn = _tile(n, tn)
        tk = _tile(kd, tk)
        b_spec = pl.BlockSpec((tn, tk), lambda i, j, k: (j, k))
    else:
        n = w.shape[1]
        tn = _tile(n, tn)
        tk = _tile(k_shard, tk)
        per = k_shard // tk
        b_spec = pl.BlockSpec((None, tn, tk), lambda i, j, k: (k // per, j, k % per))
    tm = _tile(m, tm)
    o_spec = pl.BlockSpec((tm, tn), lambda i, j, k: (i, j))
    return _matmul(
        name, a, w, dims=_NT, grid=(m // tm, n // tn, kd // tk),
        a_spec=pl.BlockSpec((tm, tk), lambda i, j, k: (i, k)), b_spec=b_spec,
        acc_shape=(tm, tn),
        out_shapes=[jax.ShapeDtypeStruct((m, n), d) for d in out_dtypes],
        out_specs=[o_spec] * len(out_dtypes),
        epilogue=epilogue, extras=extras, extra_specs=[o_spec] * len(extras), carry=carry)


def _mm_nt_norm_bwd(name, a, w, h, g, dres, *, k_shard=None, carry=_NO_CARRY, tm=512, tk=1024):
    m, kd = a.shape
    n = h.shape[1]
    if k_shard is None:
        tk = _tile(kd, tk)
        b_spec = pl.BlockSpec((n, tk), lambda i, j, k: (0, k))
    else:
        tk = _tile(k_shard, tk)
        per = k_shard // tk
        b_spec = pl.BlockSpec((None, n, tk), lambda i, j, k: (k // per, 0, k % per))
    tm = _tile(m, tm)
    rows = pl.BlockSpec((tm, n), lambda i, j, k: (i, 0))
    vec = pl.BlockSpec((1, n), lambda i, j, k: (0, 0))
    return _matmul(
        name, a, w, dims=_NT, grid=(m // tm, 1, kd // tk),
        a_spec=pl.BlockSpec((tm, tk), lambda i, j, k: (i, k)), b_spec=b_spec, acc_shape=(tm, n),
        out_shapes=[jax.ShapeDtypeStruct((m, n), F32), jax.ShapeDtypeStruct((m, n), BF16),
                    jax.ShapeDtypeStruct((1, n), F32)],
        out_specs=[rows, rows, vec], epilogue=_ep_norm_bwd,
        extras=(h, g, dres), extra_specs=[rows, vec, rows], carry=carry, sequential=True)


TN_TILE_BYTES = 40 * 1024 * 1024


def _mm_tn(name, a, b, *, n_shard=None, carry=_NO_CARRY, tm=1024, tn=1024):
    t, m = a.shape
    n = b.shape[1]
    tm = _tile(m, tm)
    tn = _tile(n if n_shard is None else n_shard, tn)
    tk = t
    while 2 * 2 * tk * (tm + tn) + 4 * tm * tn * 5 > TN_TILE_BYTES and tk % (2 * LANES) == 0:
        tk //= 2
    if n_shard is None:
        o_spec = pl.BlockSpec((tm, tn), lambda i, j, k: (i, j))
        shape = (m, n)
    else:
        per = n_shard // tn
        o_spec = pl.BlockSpec((None, tm, tn), lambda i, j, k: (j // per, i, j % per))
        shape = (N_DEV, m, n_shard)
    return _matmul(
        name, a, b, dims=_TN, grid=(m // tm, n // tn, t // tk),
        a_spec=pl.BlockSpec((tk, tm), lambda i, j, k: (k, i)),
        b_spec=pl.BlockSpec((tk, tn), lambda i, j, k: (k, j)),
        acc_shape=(tm, tn), epilogue=_ep_both, carry=carry,
        out_shapes=[jax.ShapeDtypeStruct(shape, F32), jax.ShapeDtypeStruct(shape, BF16)],
        out_specs=[o_spec, o_spec])


def _ple_proj(p, w_g, tm=1024):
    s, kd = p.shape
    ns = w_g.shape[2]
    tm = _tile(s, tm)

    def body(p_ref, w_ref, o_ref):
        pv = p_ref[...].astype(BF16)
        for j in range(N_DEV):
            o_ref[:, j * ns:(j + 1) * ns] = jnp.dot(pv, w_ref[j], preferred_element_type=F32)

    return _pcall(body, name="ple_proj", grid=(s // tm,),
                  in_specs=[pl.BlockSpec((tm, kd), lambda i: (i, 0)),
                            pl.BlockSpec((N_DEV, kd, ns), lambda i: (0, 0, 0))],
                  out_specs=pl.BlockSpec((tm, N_DEV * ns), lambda i: (i, 0)),
                  out_shape=jax.ShapeDtypeStruct((s, N_DEV * ns), F32),
                  compiler_params=_params(("parallel",)))(p, w_g)


def _d_ple_proj(p, dpp, ns, tk=1024):
    s, kd = p.shape
    tk = _tile(s, tk)
    nk = s // tk

    def body(p_ref, d_ref, of_ref, ob_ref, acc):
        k = pl.program_id(0)

        @pl.when(k == 0)
        def _():
            acc[...] = jnp.zeros_like(acc)

        pv = p_ref[...].astype(BF16)
        for j in range(N_DEV):
            acc[j] += lax.dot_general(pv, d_ref[:, j * ns:(j + 1) * ns], _TN, preferred_element_type=F32)

        @pl.when(k == nk - 1)
        def _():
            of_ref[...] = acc[...]
            ob_ref[...] = acc[...].astype(BF16)

    whole = pl.BlockSpec((N_DEV, kd, ns), lambda k: (0, 0, 0))
    return _pcall(body, name="d_w_ple_proj", grid=(nk,),
                  in_specs=[pl.BlockSpec((tk, kd), lambda k: (k, 0)),
                            pl.BlockSpec((tk, N_DEV * ns), lambda k: (k, 0))],
                  out_specs=[whole, whole],
                  out_shape=[jax.ShapeDtypeStruct((N_DEV, kd, ns), F32), jax.ShapeDtypeStruct((N_DEV, kd, ns), BF16)],
                  scratch_shapes=[pltpu.VMEM((N_DEV, kd, ns), F32)],
                  compiler_params=_params(("arbitrary",)))(p, dpp)


def _row_spec(tr, d):
    return pl.BlockSpec((tr, d), lambda i: (i, 0))


def _vec_spec(d):
    return pl.BlockSpec((1, d), lambda i: (0, 0))


def _rmsnorm_fwd(name, x, g, tr=512):
    s, d = x.shape
    tr = _tile(s, tr)

    def body(x_ref, g_ref, o_ref):
        xv = x_ref[...]
        r = lax.rsqrt(jnp.mean(xv * xv, axis=-1, keepdims=True) + EPS)
        o_ref[...] = (xv * r * g_ref[...]).astype(BF16)

    return _pcall(body, name=name, grid=(s // tr,),
                  in_specs=[_row_spec(tr, d), _vec_spec(d)], out_specs=_row_spec(tr, d),
                  out_shape=jax.ShapeDtypeStruct((s, d), BF16),
                  compiler_params=_params(("parallel",)))(x, g)


def _ple_loss(h2, gl, pp, tgt, g_final, tr=512):
    s, d = h2.shape
    tr = _tile(s, tr)

    def body(h2_ref, gl_ref, pp_ref, t_ref, g_ref, loss_ref, dh3_ref, dgl_ref, dpp_ref, dg_ref):
        @pl.when(pl.program_id(0) == 0)
        def _():
            dg_ref[...] = jnp.zeros_like(dg_ref)
            loss_ref[...] = jnp.zeros_like(loss_ref)

        gate = jax.nn.sigmoid(gl_ref[...])
        ppv = pp_ref[...]
        h3 = h2_ref[...] + gate * ppv
        r = lax.rsqrt(jnp.mean(h3 * h3, axis=-1, keepdims=True) + EPS)
        hn = h3 * r
        gv = g_ref[...]
        diff = hn * gv - t_ref[...]
        row = jnp.mean(diff * diff, axis=-1, keepdims=True)
        loss_ref[...] += 0.5 * jnp.sum(row, axis=0, keepdims=True)
        dy = diff * (1.0 / d)
        dg_ref[...] += jnp.sum(dy * hn, axis=0, keepdims=True)
        dhn = dy * gv
        dh3 = r * (dhn - hn * jnp.mean(dhn * hn, axis=-1, keepdims=True))
        dh3_ref[...] = dh3
        dgl_ref[...] = (dh3 * ppv * gate * (1.0 - gate)).astype(BF16)
        dpp_ref[...] = (dh3 * gate).astype(BF16)

    return _pcall(body, name="ple_loss", grid=(s // tr,),
                  in_specs=[_row_spec(tr, d)] * 4 + [_vec_spec(d)],
                  out_specs=[_vec_spec(LANES), _row_spec(tr, d), _row_spec(tr, d), _row_spec(tr, d), _vec_spec(d)],
                  out_shape=[jax.ShapeDtypeStruct((1, LANES), F32), jax.ShapeDtypeStruct((s, d), F32),
                             jax.ShapeDtypeStruct((s, d), BF16), jax.ShapeDtypeStruct((s, d), BF16),
                             jax.ShapeDtypeStruct((1, d), F32)],
                  compiler_params=_params(("arbitrary",)))(h2, gl, pp, tgt, g_final)


def _low_half():
    return lax.broadcasted_iota(jnp.int32, (1, LANES), 1) < HEAD_DIM


def _half_mean(v, low):
    s_lo = jnp.sum(jnp.where(low, v, 0.0), axis=-1, keepdims=True)
    s_hi = jnp.sum(jnp.where(low, 0.0, v), axis=-1, keepdims=True)
    return jnp.where(low, s_lo, s_hi) * (1.0 / HEAD_DIM)


def _head_norm_bwd(val, dout, g, low):
    r = lax.rsqrt(_half_mean(val * val, low) + EPS)
    vn = val * r
    dvn = dout * g
    return r * (dvn - vn * _half_mean(dvn * vn, low)), dout * vn


def _conv_taps(vv_ext, w_ref, rows):
    v0 = vv_ext[HALO:]
    v1 = pltpu.roll(vv_ext, 1, 0)[HALO:]
    v2 = pltpu.roll(vv_ext, 2, 0)[HALO:]
    del rows
    return w_ref[2:3, :] * v0 + w_ref[1:2, :] * v1 + w_ref[0:1, :] * v2, (v0, v1, v2)


def _conv_fwd(proj, conv_w, g_conv, w_conv, d_model, tr=512):
    s = proj.shape[0]
    tr = _tile(s, tr)
    hb = tr // HALO

    def main(part):
        return pl.BlockSpec((tr, w_conv), lambda i: (i, part))

    def prev(part):
        return pl.BlockSpec((HALO, w_conv), lambda i: (jnp.maximum(i * hb - 1, 0), part))

    def body(cb_ref, cc_ref, cu_ref, ccp_ref, cup_ref, w_ref, g_ref, o_ref):
        i = pl.program_id(0)
        low = _low_half()
        for j in range(w_conv // LANES):
            cols = slice(j * LANES, (j + 1) * LANES)
            vv_prev = jnp.where(i > 0, ccp_ref[:, cols] * cup_ref[:, cols], 0.0)
            vv_ext = jnp.concatenate([vv_prev, cc_ref[:, cols] * cu_ref[:, cols]], axis=0)
            y, _ = _conv_taps(vv_ext, w_ref.at[:, cols], tr)
            co = cb_ref[:, cols] * y
            r = lax.rsqrt(_half_mean(co * co, low) + EPS)
            o_ref[:, cols] = (co * r * g_ref[:, cols]).astype(BF16)

    return _pcall(
        body, name="conv_fwd", grid=(s // tr,),
        in_specs=[main(0), main(1), main(2), prev(1), prev(2),
                  pl.BlockSpec((CONV_K, w_conv), lambda i: (0, 0)),
                  pl.BlockSpec((1, w_conv), lambda i: (0, 0))],
        out_specs=pl.BlockSpec((tr, w_conv), lambda i: (i, 0)),
        out_shape=jax.ShapeDtypeStruct((s, d_model), BF16),
        compiler_params=_params(("parallel",)),
    )(proj, proj, proj, proj, proj, conv_w, g_conv)


def _conv_bwd(proj, dcat, conv_w, g_conv, dproj, w_conv, tr=512):
    s = proj.shape[0]
    tr = _tile(s, tr)
    hb = tr // HALO
    last = s // HALO - 1
    nt = s // tr

    def main(part):
        return pl.BlockSpec((tr, w_conv), lambda i: (i, part))

    def prev(part):
        return pl.BlockSpec((HALO, w_conv), lambda i: (jnp.maximum(i * hb - 1, 0), part))

    def nxt(part):
        return pl.BlockSpec((HALO, w_conv), lambda i: (jnp.minimum((i + 1) * hb, last), part))

    def body(cb_ref, cc_ref, cu_ref, dc_ref, ccp_ref, cup_ref, cbn_ref, ccn_ref, cun_ref, dcn_ref,
             w_ref, g_ref, dproj_in, dproj_ref, dw_ref, dg_ref):
        del dproj_in
        i = pl.program_id(0)

        @pl.when(i == 0)
        def _():
            dw_ref[...] = jnp.zeros_like(dw_ref)
            dg_ref[...] = jnp.zeros_like(dg_ref)

        low = _low_half()
        n_ext = tr + HALO
        rowid = lax.broadcasted_iota(jnp.int32, (n_ext, 1), 0)
        for j in range(w_conv // LANES):
            cols = slice(j * LANES, (j + 1) * LANES)
            wj = w_ref.at[:, cols]
            cc, cu = cc_ref[:, cols], cu_ref[:, cols]
            vv_prev = jnp.where(i > 0, ccp_ref[:, cols] * cup_ref[:, cols], 0.0)
            vv_ext = jnp.concatenate([vv_prev, cc * cu, ccn_ref[:, cols] * cun_ref[:, cols]], axis=0)
            y_ext, (v0, v1, v2) = _conv_taps(vv_ext, wj, n_ext)
            cb_ext = jnp.concatenate([cb_ref[:, cols], cbn_ref[:, cols]], axis=0)
            dc_ext = jnp.concatenate([dc_ref[:, cols], dcn_ref[:, cols]], axis=0)
            dco, dgn = _head_norm_bwd(cb_ext * y_ext, dc_ext, g_ref[:, cols], low)
            dyc = jnp.where((rowid < tr) | (i < nt - 1), dco * cb_ext, 0.0)
            dvv = (wj[2:3, :] * dyc[:tr] + wj[1:2, :] * pltpu.roll(dyc, n_ext - 1, 0)[:tr]
                   + wj[0:1, :] * pltpu.roll(dyc, n_ext - 2, 0)[:tr])
            dproj_ref[:, cols] = (dco[:tr] * y_ext[:tr]).astype(BF16)
            dproj_ref[:, w_conv + j * LANES:w_conv + (j + 1) * LANES] = (dvv * cu).astype(BF16)
            dproj_ref[:, 2 * w_conv + j * LANES:2 * w_conv + (j + 1) * LANES] = (dvv * cc).astype(BF16)
            dyt = dyc[:tr]
            for tap, shifted in enumerate((v2, v1, v0)):
                dw_ref[tap:tap + 1, cols] += jnp.sum(dyt * shifted[:tr], axis=0, keepdims=True)
            dg_ref[:, cols] += jnp.sum(dgn[:tr], axis=0, keepdims=True)

    n_cols = dproj.shape[1]
    return _pcall(
        body, name="conv_bwd", grid=(nt,),
        in_specs=[main(0), main(1), main(2), main(0),
                  prev(1), prev(2), nxt(0), nxt(1), nxt(2), nxt(0),
                  pl.BlockSpec((CONV_K, w_conv), lambda i: (0, 0)),
                  pl.BlockSpec((1, w_conv), lambda i: (0, 0)),
                  pl.BlockSpec(memory_space=pl.ANY)],
        out_specs=[pl.BlockSpec((tr, 3 * w_conv), lambda i: (i, 0)),
                   pl.BlockSpec((CONV_K, w_conv), lambda i: (0, 0)),
                   pl.BlockSpec((1, w_conv), lambda i: (0, 0))],
        out_shape=[jax.ShapeDtypeStruct((s, n_cols), BF16),
                   jax.ShapeDtypeStruct((CONV_K, w_conv), F32),
                   jax.ShapeDtypeStruct((1, w_conv), F32)],
        input_output_aliases={12: 0},
        compiler_params=_params(("arbitrary",)),
    )(proj, proj, proj, dcat, proj, proj, proj, proj, proj, dcat, conv_w, g_conv, dproj)


STRIP = 16

ALL_CHAINS = (0, 1, 2, 3)
UPPER_CHAINS = (2, 3)


RUN_FLOOR = -104.0


def _any_weight_left(run_s):
    return (jnp.max(run_s[...]) > RUN_FLOOR).astype(jnp.int32)


def _chains(low):
    return [(2 * half + h, half, msk) for half in range(2)
            for h, msk in enumerate((low, jnp.logical_not(low)))]


def _suffix_operator(t):
    r = lax.broadcasted_iota(jnp.int32, (2 * t, t), 0)
    c = lax.broadcasted_iota(jnp.int32, (2 * t, t), 1)
    return jnp.where((r > c) & ((r < t) | (r - t > c)), 1.0, 0.0).astype(BF16)


def _strips(t):
    return [(i, slice(i * STRIP, (i + 1) * STRIP)) for i in range(t // STRIP)]


def _strip_mask(i, t):
    r = lax.broadcasted_iota(jnp.int32, (STRIP, t), 0) + i * STRIP
    c = lax.broadcasted_iota(jnp.int32, (STRIP, t), 1)
    return r > c


def _store_split(ref, rows, val, t):
    hi = val.astype(BF16)
    ref[rows, 0:t] = hi
    ref[rows, t:2 * t] = (val - hi.astype(F32)).astype(BF16)


def _sb_scores(z_s, split_s, zl_s, tot_s, keep_s, t, diag):
    for i, rows in _strips(t):
        z = z_s[rows, :]
        log_beta = jnp.minimum(z, 0.0) - jnp.log(1.0 + jnp.exp(-jnp.abs(z)))
        log_keep = log_beta - z
        if diag:
            log_keep = jnp.where(_strip_mask(i, t), log_keep, 0.0)
        _store_split(split_s, rows, log_keep, t)
        zl_s[rows, :] = log_beta
        tot_s[rows, :] = _row_sum(log_keep)
        if keep_s is not None:
            keep_s[rows, :] = jnp.exp(log_keep)


def _row_sum(v):
    return jnp.broadcast_to(jnp.sum(v, axis=-1, keepdims=True), (v.shape[0], LANES))


def _wide(r, t):
    return jnp.concatenate([r] * (t // LANES), axis=1)


def _sb_weights(zl_s, suf_s, run_s, tot_s, a_s, t, diag, da_s=None, glog_s=None, gsplit_s=None, gtot_s=None):
    for i, rows in _strips(t):
        run = run_s[rows, :]
        a = jnp.exp(zl_s[rows, :] + suf_s[rows, :] + _wide(run, t))
        if diag:
            a = jnp.where(_strip_mask(i, t), a, 0.0)
        ab = a.astype(BF16)
        a_s[rows, :] = ab
        run_s[rows, :] = run + tot_s[rows, :]
        if da_s is not None:
            glog = ab.astype(F32) * da_s[rows, :]
            glog_s[rows, :] = glog
            _store_split(gsplit_s, rows, glog, t)
            gtot_s[rows, :] = _row_sum(glog)


def _sb_dscores(glog_s, cum_s, rest_s, gtot_s, keep_s, dz_s, t, diag):
    for i, rows in _strips(t):
        glog = glog_s[rows, :]
        rest = rest_s[rows, :]
        from_here = _wide(rest, t) - cum_s[rows, :]
        before = from_here - glog
        dz = from_here * keep_s[rows, :] - before
        if diag:
            dz = jnp.where(_strip_mask(i, t), dz, 0.0)
        dz_s[rows, :] = dz.astype(BF16)
        rest_s[rows, :] = rest - gtot_s[rows, :]


def _attn_fwd(proj, g_attn, cat, w_conv, carry, t=ATTN_BLOCK):
    s = proj.shape[0]
    w_attn = g_attn.shape[1]
    nh = w_attn // LANES
    t = _tile(s, t)
    tq = 2 * t
    nq = s // tq
    q0 = 3 * w_conv // LANES
    scale = HEAD_DIM ** -0.5
    plan = _Carried(carry)
    nw = len(plan.inputs)

    def body(q_ref, k_ref, v_ref, g_ref, cat_in, *rest):
        staged_refs, rest = rest[:nw], rest[nw:]
        o_ref, cat_ref = rest[:2]
        gathered_refs, rest = rest[2:2 + nw], rest[2 + nw:]
        kb, vb, tri_s, qm_s, z_s, split_s, zl_s, suf_s, a_s, run_s, tot_s, acc_s = rest[:12]
        gather_sems = rest[12:]
        del cat_in
        qi = pl.program_id(1)

        @pl.when((pl.program_id(0) == 0) & (qi == 0))
        def _():
            for cp in plan.copies(staged_refs, gathered_refs, gather_sems):
                cp.start()

        @pl.when(qi == 0)
        def _():
            kb[...] = k_ref[...].astype(BF16)
            vb[...] = v_ref[...].astype(BF16)
            tri_s[...] = _suffix_operator(t)

        low = _low_half()
        for c, half, msk in _chains(low):
            qm_s[c] = jnp.where(msk, q_ref[half * t:(half + 1) * t, :] * scale, 0.0).astype(BF16)
            run_s[c] = jnp.zeros((t, LANES), F32)
            acc_s[c] = jnp.zeros((t, LANES), F32)

        def key_rows(kblk):
            return pl.ds(pl.multiple_of(kblk * t, t), t)

        def key_block(base, c):
            return key_rows(jnp.maximum(base + c // 2, 0))

        def scores_matmul(base, chains):
            for c in chains:
                z_s[c] = lax.dot_general(qm_s[c], kb[key_block(base, c), :], _NT, preferred_element_type=F32)

        def front(modes, base, prev=None):
            for c, diag in modes:
                _sb_scores(z_s.at[c], split_s.at[c], zl_s.at[c], tot_s.at[c], None, t, diag)
                suf_s[c] = jnp.dot(split_s[c], tri_s[...], preferred_element_type=F32)
            if prev is not None:
                tail(*prev)
            scores_matmul(base - 1, ALL_CHAINS)
            for c, diag in modes:
                _sb_weights(zl_s.at[c], suf_s.at[c], run_s.at[c], tot_s.at[c], a_s.at[c], t, diag)

        def tail(base, chains):
            for c in chains:
                acc_s[c] += jnp.dot(a_s[c], vb[key_block(base, c), :], preferred_element_type=F32)

        first = 2 * qi
        scores_matmul(first, ALL_CHAINS)
        front([(c, True) for c in ALL_CHAINS], first)

        def loop(state):
            it = state[0]
            base = first - 1 - it
            front([(c, False) for c in ALL_CHAINS], base, prev=(base + 1, ALL_CHAINS))
            return it + 1, _any_weight_left(run_s)

        done, live = lax.while_loop(lambda state: (state[0] < first) & (state[1] > 0), loop,
                                    (jnp.int32(0), _any_weight_left(run_s)))
        one_more = (done == first) & (live > 0)

        @pl.when(one_more)
        def _():
            front([(c, False) for c in UPPER_CHAINS], -1, prev=(0, ALL_CHAINS))
            tail(-1, UPPER_CHAINS)

        @pl.when(jnp.logical_not(one_more))
        def _():
            tail(first - done, ALL_CHAINS)

        for half in range(2):
            rows = slice(half * t, (half + 1) * t)
            o = jnp.where(low, acc_s[2 * half], acc_s[2 * half + 1])
            o_ref[rows, :] = o
            r = lax.rsqrt(_half_mean(o * o, low) + EPS)
            cat_ref[rows, :] = (o * r * g_ref[...]).astype(BF16)

        @pl.when((pl.program_id(0) == nh - 1) & (qi == nq - 1))
        def _():
            for cp in plan.copies(staged_refs, gathered_refs, gather_sems):
                cp.wait()

    whole = lambda col0: pl.BlockSpec((s, LANES), lambda h, i: (0, col0 + h))
    n_ch = len(ALL_CHAINS)
    res = _pcall(
        body, name="attn_fwd", grid=(nh, nq),
        in_specs=[pl.BlockSpec((tq, LANES), lambda h, i: (i, q0 + h)),
                  whole(q0 + nh), whole(q0 + 2 * nh),
                  pl.BlockSpec((1, LANES), lambda h, i: (0, h)),
                  pl.BlockSpec(memory_space=pl.ANY)] + [pl.BlockSpec(memory_space=pl.ANY)] * nw,
        out_specs=[pl.BlockSpec((tq, LANES), lambda h, i: (i, h)),
                   pl.BlockSpec((tq, LANES), lambda h, i: (i, w_conv // LANES + h))]
        + [pl.BlockSpec(memory_space=pl.ANY)] * nw,
        out_shape=[jax.ShapeDtypeStruct((s, w_attn), F32),
                   jax.ShapeDtypeStruct(cat.shape, BF16)] + plan.out_shapes,
        scratch_shapes=[pltpu.VMEM((s, LANES), BF16), pltpu.VMEM((s, LANES), BF16),
                        pltpu.VMEM((2 * t, t), BF16),
                        pltpu.VMEM((n_ch, t, LANES), BF16),
                        pltpu.VMEM((n_ch, t, t), F32),
                        pltpu.VMEM((n_ch, t, 2 * t), BF16),
                        pltpu.VMEM((n_ch, t, t), F32),
                        pltpu.VMEM((n_ch, t, t), F32),
                        pltpu.VMEM((n_ch, t, t), BF16),
                        pltpu.VMEM((n_ch, t, LANES), F32),
                        pltpu.VMEM((n_ch, t, LANES), F32),
                        pltpu.VMEM((n_ch, t, LANES), F32)]
        + plan.sems,
        input_output_aliases={4: 1, **plan.aliases(5, 2)},
        compiler_params=_params(("arbitrary", "arbitrary")),
    )(proj, proj, proj, g_attn, cat, *plan.inputs)
    return res[0], res[1], res[2:]


def _attn_bwd(proj, o, dcat, g_attn, w_conv, carry, t=ATTN_BLOCK):
    s, n_cols = proj.shape
    w_attn = g_attn.shape[1]
    nh = w_attn // LANES
    t = _tile(s, t)
    tq = 2 * t
    nq = s // tq
    q0 = 3 * w_conv // LANES
    scale = HEAD_DIM ** -0.5
    plan = _Carried(carry)
    nw = len(plan.inputs)

    def body(q_ref, k_ref, v_ref, o_ref, do_ref, g_ref, *rest):
        partial_refs, rest = rest[:nw], rest[nw:]
        dproj_ref, dg_ref = rest[:2]
        received_refs, rest = rest[2:2 + nw], rest[2 + nw:]
        (kb, vb, dkt_acc, dvt_acc, stash, tri_s, qm_s, dom_s, qt_s, dot_s, z_s, da_s, split_s, zl_s,
         keep_s, suf_s, a_s, glog_s, gsplit_s, cum_s, dz_s, run_s, tot_s, rest_s, gtot_s, dq_s) = rest[:26]
        out_sems, scatter_sems = rest[26], rest[27:]
        step_i = pl.program_id(1)
        qi = nq - 1 - step_i
        head_pair = pl.program_id(0)
        first_step = (head_pair == 0) & (step_i == 0)
        last_step = (head_pair == nh - 1) & (step_i == nq - 1)

        @pl.when(first_step)
        def _():
            for cp in plan.copies(partial_refs, received_refs, scatter_sems):
                cp.start()

        def out_copies():
            rows = pl.ds(pl.multiple_of(qi * tq, tq), tq)
            return [pltpu.make_async_copy(
                stash.at[w], dproj_ref.at[rows, pl.ds(pl.multiple_of((q0 + w * nh + head_pair) * LANES, LANES), LANES)],
                out_sems.at[w]) for w in range(3)]

        def walk():
            @pl.when(step_i == 0)
            def _():
                kb[...] = k_ref[...].astype(BF16)
                vb[...] = v_ref[...].astype(BF16)
                tri_s[...] = _suffix_operator(t)
                dkt_acc[...] = jnp.zeros_like(dkt_acc)
                dvt_acc[...] = jnp.zeros_like(dvt_acc)
                dg_ref[...] = jnp.zeros_like(dg_ref)

            low = _low_half()
            gv = g_ref[...]
            for half in range(2):
                rows = slice(half * t, (half + 1) * t)
                q = q_ref[rows, :] * scale
                ov = o_ref[rows, :]
                d_o, dgn = _head_norm_bwd(ov, do_ref[rows, :], gv, low)
                dg_ref[...] += jnp.sum(dgn, axis=0, keepdims=True)
                for h, msk in enumerate((low, jnp.logical_not(low))):
                    c = 2 * half + h
                    qh = jnp.where(msk, q, 0.0)
                    doh = jnp.where(msk, d_o, 0.0)
                    dom = doh.astype(BF16)
                    qm_s[c] = qh.astype(BF16)
                    dom_s[c] = dom
                    qt_s[c] = qh.T.astype(BF16)
                    dot_s[c] = doh.T.astype(BF16)
                    rest_s[c] = _row_sum(dom.astype(F32) * ov)
                    run_s[c] = jnp.zeros((t, LANES), F32)
                    dq_s[c] = jnp.zeros((t, LANES), F32)

            def key_rows(kblk):
                return pl.ds(pl.multiple_of(kblk * t, t), t)

            def block_of(base, half):
                return jnp.maximum(base + half, 0)

            def scores_matmul(base, chains):
                for c in chains:
                    ks = kb[key_rows(block_of(base, c // 2)), :]
                    z_s[c] = lax.dot_general(qm_s[c], ks, _NT, preferred_element_type=F32)

            def da_matmul(base, chains):
                for c in chains:
                    vs = vb[key_rows(block_of(base, c // 2)), :]
                    da_s[c] = lax.dot_general(dom_s[c], vs, _NT, preferred_element_type=F32)

            def front(modes, base, prev=None):
                if prev is not None:
                    tail(*prev)
                for c, diag in modes:
                    _sb_scores(z_s.at[c], split_s.at[c], zl_s.at[c], tot_s.at[c], keep_s.at[c], t, diag)
                    suf_s[c] = jnp.dot(split_s[c], tri_s[...], preferred_element_type=F32)
                scores_matmul(base - 1, ALL_CHAINS)
                for c, diag in modes:
                    _sb_weights(zl_s.at[c], suf_s.at[c], run_s.at[c], tot_s.at[c], a_s.at[c], t, diag,
                                da_s.at[c], glog_s.at[c], gsplit_s.at[c], gtot_s.at[c])
                    cum_s[c] = jnp.dot(gsplit_s[c], tri_s[...], preferred_element_type=F32)
                da_matmul(base - 1, ALL_CHAINS)
                for c, diag in modes:
                    _sb_dscores(glog_s.at[c], cum_s.at[c], rest_s.at[c], gtot_s.at[c], keep_s.at[c],
                                dz_s.at[c], t, diag)

            def tail(base, chains):
                for half in range(2):
                    mine = [c for c in chains if c // 2 == half]
                    if not mine:
                        continue
                    kblk = block_of(base, half)
                    ks = kb[key_rows(kblk), :]
                    dkt = dkt_acc[kblk]
                    dvt = dvt_acc[kblk]
                    for c in mine:
                        dq_s[c] += jnp.dot(dz_s[c], ks, preferred_element_type=F32)
                        dkt = dkt + jnp.dot(qt_s[c], dz_s[c], preferred_element_type=F32)
                        dvt = dvt + jnp.dot(dot_s[c], a_s[c], preferred_element_type=F32)
                    dkt_acc[kblk] = dkt
                    dvt_acc[kblk] = dvt

            first = 2 * qi
            scores_matmul(first, ALL_CHAINS)
            da_matmul(first, ALL_CHAINS)
            front([(c, True) for c in ALL_CHAINS], first)

            def loop(state):
                it = state[0]
                base = first - 1 - it
                front([(c, False) for c in ALL_CHAINS], base, prev=(base + 1, ALL_CHAINS))
                return it + 1, _any_weight_left(run_s)

            done, live = lax.while_loop(lambda state: (state[0] < first) & (state[1] > 0), loop,
                                        (jnp.int32(0), _any_weight_left(run_s)))
            one_more = (done == first) & (live > 0)

            @pl.when(one_more)
            def _():
                front([(c, False) for c in UPPER_CHAINS], -1, prev=(0, ALL_CHAINS))
                tail(-1, UPPER_CHAINS)

            @pl.when(jnp.logical_not(one_more))
            def _():
                tail(first - done, ALL_CHAINS)

            @pl.when(jnp.logical_not(first_step))
            def _():
                for cp in out_copies():
                    cp.wait()

            for half in range(2):
                rows = slice(half * t, (half + 1) * t)
                stash[0, rows, :] = (jnp.where(low, dq_s[2 * half], dq_s[2 * half + 1]) * scale).astype(BF16)
                stash[1, rows, :] = dkt_acc[2 * qi + half].T.astype(BF16)
                stash[2, rows, :] = dvt_acc[2 * qi + half].T.astype(BF16)
            for cp in out_copies():
                cp.start()

        walk()

        @pl.when(last_step)
        def _():
            for cp in out_copies():
                cp.wait()
            for cp in plan.copies(partial_refs, received_refs, scatter_sems):
                cp.wait()

    whole = lambda col0: pl.BlockSpec((s, LANES), lambda h, i: (0, col0 + h))
    blk = lambda col0: pl.BlockSpec((tq, LANES), lambda h, i: (nq - 1 - i, col0 + h))
    n_ch = len(ALL_CHAINS)
    res = _pcall(
        body, name="attn_bwd", grid=(nh, nq),
        in_specs=[blk(q0), whole(q0 + nh), whole(q0 + 2 * nh), blk(0), blk(w_conv // LANES),
                  pl.BlockSpec((1, LANES), lambda h, i: (0, h))] + [pl.BlockSpec(memory_space=pl.ANY)] * nw,
        out_specs=[pl.BlockSpec(memory_space=pl.ANY),
                   pl.BlockSpec((1, LANES), lambda h, i: (0, h))] + [pl.BlockSpec(memory_space=pl.ANY)] * nw,
        out_shape=[jax.ShapeDtypeStruct((s, n_cols), BF16), jax.ShapeDtypeStruct((1, w_attn), F32)]
        + plan.out_shapes,
        scratch_shapes=[pltpu.VMEM((s, LANES), BF16), pltpu.VMEM((s, LANES), BF16),
                        pltpu.VMEM((s // t, LANES, t), F32),
                        pltpu.VMEM((s // t, LANES, t), F32),
                        pltpu.VMEM((3, tq, LANES), BF16),
                        pltpu.VMEM((2 * t, t), BF16),
                        pltpu.VMEM((n_ch, t, LANES), BF16),
                        pltpu.VMEM((n_ch, t, LANES), BF16),
                        pltpu.VMEM((n_ch, LANES, t), BF16),
                        pltpu.VMEM((n_ch, LANES, t), BF16),
                        pltpu.VMEM((n_ch, t, t), F32),
                        pltpu.VMEM((n_ch, t, t), F32),
                        pltpu.VMEM((n_ch, t, 2 * t), BF16),
                        pltpu.VMEM((n_ch, t, t), F32),
                        pltpu.VMEM((n_ch, t, t), F32),
                        pltpu.VMEM((n_ch, t, t), F32),
                        pltpu.VMEM((n_ch, t, t), BF16),
                        pltpu.VMEM((n_ch, t, t), F32),
                        pltpu.VMEM((n_ch, t, 2 * t), BF16),
                        pltpu.VMEM((n_ch, t, t), F32),
                        pltpu.VMEM((n_ch, t, t), BF16),
                        pltpu.VMEM((n_ch, t, LANES), F32),
                        pltpu.VMEM((n_ch, t, LANES), F32),
                        pltpu.VMEM((n_ch, t, LANES), F32),
                        pltpu.VMEM((n_ch, t, LANES), F32),
                        pltpu.VMEM((n_ch, t, LANES), F32),
                        pltpu.SemaphoreType.DMA((3,))]
        + plan.sems,
        input_output_aliases=plan.aliases(6, 2),
        compiler_params=_params(("arbitrary", "arbitrary")),
    )(proj, proj, proj, o, dcat, g_attn, *plan.inputs)
    return res[0], res[1], res[2:]


def _place():
    return lax.axis_index("x"), lax.axis_index("y"), lax.axis_index("c")


def _other_chips(x, y):
    return [(1 - x, y), (x, 1 - y), (1 - x, 1 - y)]


def _slot(px, py, pc):
    return 4 * px + 2 * py + pc


def _all_gather(shards, out_dtypes):
    nw = len(shards)

    def body(*refs):
        ins, outs, stage = refs[:nw], refs[nw:2 * nw], refs[2 * nw:3 * nw]
        send_sems, recv_sems, local_sems = refs[3 * nw:]
        x, y, c = _place()
        me, sibling = (x, y, c), (x, y, 1 - c)
        chips = _other_chips(x, y)

        def copy(w, k, block, to, src=None):
            dst = outs[w].at[_slot(*block)]
            return pltpu.make_async_remote_copy(
                src_ref=dst if src is None else src, dst_ref=dst,
                send_sem=send_sems.at[w * 7 + k], recv_sem=recv_sems.at[w * 7 + k],
                device_id=to, device_id_type=MESH)

        started = []
        local = []
        for w in range(nw):
            stage[w][...] = ins[w][...].astype(stage[w].dtype)
            cp = pltpu.make_async_copy(stage[w], outs[w].at[_slot(*me)], local_sems.at[w])
            cp.start()
            local.append(cp)
            started.append(copy(w, 0, me, sibling, src=stage[w]))
            started[-1].start()
            for j, chip in enumerate(chips):
                started.append(copy(w, 1 + j, me, (*chip, c), src=stage[w]))
                started[-1].start()
        for j, chip in enumerate(chips):
            for w in range(nw):
                copy(w, 1 + j, (*chip, c), me).wait_recv()
                started.append(copy(w, 4 + j, (*chip, c), sibling))
                started[-1].start()
        for w in range(nw):
            copy(w, 0, sibling, me).wait_recv()
            for j, chip in enumerate(chips):
                copy(w, 4 + j, (*chip, 1 - c), me).wait_recv()
        for cp in started:
            cp.wait_send()
        for cp in local:
            cp.wait()

    return _pcall(
        body, name="all_gather_weights",
        in_specs=[pl.BlockSpec(memory_space=pltpu.VMEM)] * nw,
        out_specs=[pl.BlockSpec(memory_space=pl.ANY)] * nw,
        out_shape=[jax.ShapeDtypeStruct((N_DEV, *a.shape), d) for a, d in zip(shards, out_dtypes)],
        scratch_shapes=[pltpu.VMEM(a.shape, d) for a, d in zip(shards, out_dtypes)]
        + [pltpu.SemaphoreType.DMA((7 * nw,)), pltpu.SemaphoreType.DMA((7 * nw,)),
           pltpu.SemaphoreType.DMA((nw,))],
        compiler_params=_params(),
    )(*shards)


N_PEERS = N_DEV - 1


def _peer(k):
    x, y, c = _place()
    return (x ^ (k >> 2), y ^ ((k >> 1) & 1), c ^ (k & 1))


def _remote(src, dst, sems, index, to):
    return pltpu.make_async_remote_copy(src_ref=src, dst_ref=dst, send_sem=sems[0].at[index],
                                        recv_sem=sems[1].at[index], device_id=to, device_id_type=MESH)


def _gather_out_copies(staged, gathered, sems):
    x, y, c = _place()
    me = _slot(x, y, c)
    targets = [(x, y, 1 - c)] + [(*chip, c) for chip in _other_chips(x, y)]
    copies = []
    for w, (src, dst) in enumerate(zip(staged, gathered)):
        copies.append(pltpu.make_async_copy(src, dst.at[me], sems[2].at[w]))
        copies += [_remote(src, dst.at[me], sems, w * len(targets) + k, to) for k, to in enumerate(targets)]
    return copies


def _gather_pass_copies(arrived, gathered, sems):
    x, y, c = _place()
    chips = _other_chips(x, y)
    return [_remote(src.at[_slot(*chip, c)], dst.at[_slot(*chip, c)], sems, w * len(chips) + j, (x, y, 1 - c))
            for w, (src, dst) in enumerate(zip(arrived, gathered)) for j, chip in enumerate(chips)]


def _scatter_copies(partials, received, sems):
    me = _slot(*_place())
    return [_remote(src.at[me ^ k], dst.at[k - 1], sems, w * N_PEERS + k - 1, _peer(k))
            for w, (src, dst) in enumerate(zip(partials, received)) for k in range(1, N_DEV)]


class _Carried:
    COPIES = {"gather_out": (_gather_out_copies, 4, True), "gather_pass": (_gather_pass_copies, 3, False),
              "scatter": (_scatter_copies, N_PEERS, False)}

    def __init__(self, jobs):
        self.jobs = [(kind, list(arrays)) for kind, arrays in jobs if len(arrays)]
        self.inputs = [a for _, arrays in self.jobs for a in arrays]
        self.out_shapes, self.sems, self.sem_counts = [], [], []
        for kind, arrays in self.jobs:
            _, fan, local = self.COPIES[kind]
            for a in arrays:
                shape = {"gather_out": (N_DEV, *a.shape), "gather_pass": a.shape,
                         "scatter": (N_PEERS, *a.shape[1:])}[kind]
                self.out_shapes.append(jax.ShapeDtypeStruct(shape, BF16))
            job_sems = [pltpu.SemaphoreType.DMA((fan * len(arrays),))] * 2
            job_sems += [pltpu.SemaphoreType.DMA((len(arrays),))] if local else []
            self.sems += job_sems
            self.sem_counts.append(len(job_sems))

    def aliases(self, first_input, first_output):
        pairs, at = {}, 0
        for kind, arrays in self.jobs:
            if kind == "gather_pass":
                pairs.update({first_input + at + i: first_output + at + i for i in range(len(arrays))})
            at += len(arrays)
        return pairs

    def copies(self, in_refs, out_refs, sem_refs):
        out, at, sem_at = [], 0, 0
        for (kind, arrays), n_sems in zip(self.jobs, self.sem_counts):
            n = len(arrays)
            out += self.COPIES[kind][0](in_refs[at:at + n], out_refs[at:at + n], sem_refs[sem_at:sem_at + n_sems])
            at, sem_at = at + n, sem_at + n_sems
        return out


def _cast_shards(shards):
    def body(*refs):
        for src, dst in zip(refs[:len(shards)], refs[len(shards):]):
            dst[...] = src[...].astype(BF16)

    return _pcall(
        body, name="cast_shards",
        in_specs=[pl.BlockSpec(memory_space=pltpu.VMEM)] * len(shards),
        out_specs=[pl.BlockSpec(memory_space=pltpu.VMEM)] * len(shards),
        out_shape=[jax.ShapeDtypeStruct(a.shape, BF16) for a in shards],
        compiler_params=_params(),
    )(*shards)


def _all_reduce_small(packed):
    r = packed.shape[0]

    def body(x_ref, o_ref, gathered, send_sems, recv_sems):
        x, y, c = _place()
        me = _slot(x, y, c)
        gathered[me] = x_ref[...]
        peers = [(px, py, pc) for px in range(2) for py in range(2) for pc in range(2)]
        started = []
        for k in range(1, N_DEV):
            to = (x ^ (k >> 2), y ^ ((k >> 1) & 1), c ^ (k & 1))
            cp = pltpu.make_async_remote_copy(
                src_ref=x_ref, dst_ref=gathered.at[me],
                send_sem=send_sems.at[k - 1], recv_sem=recv_sems.at[k - 1],
                device_id=to, device_id_type=MESH)
            cp.start()
            started.append(cp)
        del peers
        for cp in started:
            cp.wait()
        total = gathered[0]
        for k in range(1, N_DEV):
            total = total + gathered[k]
        o_ref[...] = total

    return _pcall(
        body, name="all_reduce_small",
        in_specs=[pl.BlockSpec(memory_space=pltpu.VMEM)],
        out_specs=pl.BlockSpec(memory_space=pltpu.VMEM),
        out_shape=jax.ShapeDtypeStruct(packed.shape, F32),
        scratch_shapes=[pltpu.VMEM((N_DEV, r, LANES), F32),
                        pltpu.SemaphoreType.DMA((N_DEV - 1,)), pltpu.SemaphoreType.DMA((N_DEV - 1,))],
        compiler_params=_params(),
    )(packed)


def _adam_math(w, g, m, v):
    m = ADAM_B1 * m + (1.0 - ADAM_B1) * g
    v = ADAM_B2 * v + (1.0 - ADAM_B2) * jnp.square(g)
    m_hat = m / (1.0 - ADAM_B1 ** ADAM_STEP)
    v_hat = v / (1.0 - ADAM_B2 ** ADAM_STEP)
    delta = -ADAM_LR * (m_hat / (jnp.sqrt(v_hat) + ADAM_EPS) + ADAM_WD * w)
    return delta, m, v


def _adam_sharded(name, own, received, w, m, v, place, tr=256):
    r, cdim = w.shape
    tr = _tile(r, tr) if r % LANES == 0 else r

    def body(place_ref, own_ref, rec_ref, w_ref, m_ref, v_ref, g_ref, d_ref, nm_ref, nv_ref):
        del place_ref
        g = own_ref[...]
        for j in range(N_PEERS):
            g = g + rec_ref[j].astype(F32)
        delta, nm, nv = _adam_math(w_ref[...], g, m_ref[...], v_ref[...])
        g_ref[...] = g
        d_ref[...] = delta
        nm_ref[...] = nm
        nv_ref[...] = nv

    blk = pl.BlockSpec((tr, cdim), lambda i, pr: (i, 0))
    grid_spec = pltpu.PrefetchScalarGridSpec(
        num_scalar_prefetch=1, grid=(r // tr,),
        in_specs=[pl.BlockSpec((None, tr, cdim), lambda i, pr: (4 * pr[0] + 2 * pr[1] + pr[2], i, 0)),
                  pl.BlockSpec((N_PEERS, tr, cdim), lambda i, pr: (0, i, 0)), blk, blk, blk],
        out_specs=[blk] * 4)
    return _pcall(body, name=name, grid_spec=grid_spec,
                  out_shape=[jax.ShapeDtypeStruct((r, cdim), F32)] * 4,
                  compiler_params=_params(("parallel",)))(place, own, received, w, m, v)


def _adam_small(w, g, m, v):
    def body(w_ref, g_ref, m_ref, v_ref, d_ref, nm_ref, nv_ref):
        delta, nm, nv = _adam_math(w_ref[...], g_ref[...], m_ref[...], v_ref[...])
        d_ref[...] = delta
        nm_ref[...] = nm
        nv_ref[...] = nv

    return _pcall(body, name="adam_small",
                  in_specs=[pl.BlockSpec(memory_space=pltpu.VMEM)] * 4,
                  out_specs=[pl.BlockSpec(memory_space=pltpu.VMEM)] * 3,
                  out_shape=[jax.ShapeDtypeStruct(w.shape, F32)] * 3,
                  compiler_params=_params())(w, g, m, v)


def _rows(vec):
    return vec.reshape(-1, LANES)


def kernel(x, p, g_mix, w_in, conv_w, g_conv_out, g_attn_out, w_out, g_mlp, w_up, w_down, g_ple, w_ple_gate, w_ple_proj, g_final, loss_target, m_g_mix, m_w_in, m_conv_w, m_g_conv_out, m_g_attn_out, m_w_out, m_g_mlp, m_w_up, m_w_down, m_g_ple, m_w_ple_gate, m_w_ple_proj, m_g_final, v_g_mix, v_w_in, v_conv_w, v_g_conv_out, v_g_attn_out, v_w_out, v_g_mlp, v_w_up, v_w_down, v_g_ple, v_w_ple_gate, v_w_ple_proj, v_g_final):
    s, d = x.shape[1], x.shape[2]
    w_conv = g_conv_out.shape[1]
    w_attn = g_attn_out.shape[1]
    cw = conv_w.shape[2]
    xs, ps, tgt = x[0], p[0, 0], loss_target[0]
    place = jnp.stack([lax.axis_index("x"), lax.axis_index("y"), lax.axis_index("c")]).astype(jnp.int32)
    my_slot = 4 * place[0] + 2 * place[1] + place[2]

    conv_tile = jnp.pad(conv_w[0], ((0, HALO - CONV_K), (0, LANES - cw)))
    big = [w_in[0], w_out[0], w_up[0], w_down[0], w_ple_gate[0], w_ple_proj[0]]
    win_g, conv_g = _all_gather([big[0], conv_tile], [BF16, F32])
    s_out, s_up, s_down, s_gate, s_proj = _cast_shards(big[1:])
    conv_full = jnp.transpose(conv_g[:, :CONV_K, :cw], (1, 0, 2)).reshape(CONV_K, w_conv)
    in_shard, up_shard, proj_shard = big[0].shape[1], big[2].shape[1], big[5].shape[1]

    a = _rmsnorm_fwd("norm_mix", xs, g_mix)
    proj, g_out, g_gate, g_proj = _mm_nn("in_proj", a, win_g, n_shard=in_shard, tn=in_shard, tm=2048,
                                         carry=[("gather_out", [s_out, s_gate, s_proj])])
    cat = _conv_fwd(proj, conv_full, g_conv_out, w_conv, d)
    o, cat, (g_up, g_down, wout_g, wgate_g, wproj_g) = _attn_fwd(
        proj, g_attn_out, cat, w_conv,
        [("gather_out", [s_up, s_down]), ("gather_pass", [g_out, g_gate, g_proj])])
    wout_f = wout_g.reshape(-1, wout_g.shape[-1])
    wgate_f = wgate_g.reshape(-1, wgate_g.shape[-1])
    h1, wup_g = _mm_nn("out_proj", cat, wout_f, epilogue=_ep_residual, extras=(xs,),
                       carry=[("gather_pass", [g_up])])
    mn = _rmsnorm_fwd("norm_mlp", h1, g_mlp)
    act, wdown_g = _mm_nn("mlp_up", mn, wup_g, n_shard=up_shard, epilogue=_ep_up, out_dtypes=(BF16,), tm=2048,
                          carry=[("gather_pass", [g_down])])
    wdown_f = wdown_g.reshape(-1, wdown_g.shape[-1])
    h2, = _mm_nn("mlp_down", act, wdown_f, epilogue=_ep_residual, extras=(h1,))
    n3 = _rmsnorm_fwd("norm_ple", h2, g_ple)
    gl, = _mm_nn("ple_gate", n3, wgate_f)
    pp = _ple_proj(ps, wproj_g)
    loss_part, dh3, dgl, dpp, dg_final = _ple_loss(h2, gl, pp, tgt, g_final.reshape(1, d))

    def slots(t2d):
        return t2d.reshape(N_DEV, -1, t2d.shape[-1])

    dw_proj = _d_ple_proj(ps, dpp, proj_shard)
    dw_gate = [slots(t) for t in _mm_tn("d_w_ple_gate", n3, dgl)]
    dh2, dh2b, dg_ple = _mm_nt_norm_bwd("d_norm_ple", dgl, wgate_f, h2, g_ple, dh3)
    du, gate_recv, proj_recv = _mm_nt("d_mlp_act", dh2b, wdown_f, epilogue=_ep_dact, out_dtypes=(BF16,),
                                      extras=(act,), tm=2048, carry=[("scatter", [dw_gate[1], dw_proj[1]])])
    dw_down = [slots(t) for t in _mm_tn("d_w_down", act, dh2b)]
    dw_up = _mm_tn("d_w_up", mn, du, n_shard=up_shard)
    dh1, dh1b, dg_mlp = _mm_nt_norm_bwd("d_norm_mlp", du, wup_g, h1, g_mlp, dh2, k_shard=up_shard, tm=1024)
    dcat, = _mm_nt("d_cat", dh1b, wout_f)
    dw_out = [slots(t) for t in _mm_tn("d_w_out", cat, dh1b)]
    dproj, dg_attn, (up_recv, down_recv) = _attn_bwd(proj, o, dcat, g_attn_out, w_conv,
                                                     [("scatter", [dw_up[1], dw_down[1]])])
    dproj, dconv, dg_conv = _conv_bwd(proj, dcat, conv_full, g_conv_out, dproj, w_conv)
    *dw_in, out_recv = _mm_tn("d_w_in", a, dproj, n_shard=in_shard, tn=in_shard,
                              carry=[("scatter", [dw_out[1]])])
    grad_x, _, dg_mix, in_recv = _mm_nt_norm_bwd("d_norm_mix", dproj, win_g, xs, g_mix, dh1, k_shard=in_shard,
                                                 tk=in_shard, tm=1024, carry=[("scatter", [dw_in[1]])])

    names = ["w_in", "w_out", "w_up", "w_down", "w_ple_gate", "w_ple_proj"]
    owns = [dw_in[0], dw_out[0], dw_up[0], dw_down[0], dw_gate[0], dw_proj[0]]
    recvs = [in_recv, out_recv, up_recv, down_recv, gate_recv, proj_recv]
    moments = [(m_w_in, v_w_in), (m_w_out, v_w_out), (m_w_up, v_w_up), (m_w_down, v_w_down),
               (m_w_ple_gate, v_w_ple_gate), (m_w_ple_proj, v_w_ple_proj)]
    big_out = {}
    for n, own, rc, wt, (mm, vv) in zip(names, owns, recvs, big, moments):
        big_out[n] = [t[None] for t in _adam_sharded("adam_" + n, own, rc, wt, mm[0], vv[0], place)]

    n_conv_rows = CONV_K * w_conv // LANES
    small_g = jnp.concatenate(
        [_rows(dg_mix[0]), _rows(dg_conv[0]), _rows(dg_attn[0]), _rows(dg_mlp[0]), _rows(dg_ple[0]),
         _rows(dg_final[0]), _rows(dconv.reshape(-1)), loss_part], axis=0)
    n_gain_rows = small_g.shape[0] - n_conv_rows - 1
    pad_rows = (-small_g.shape[0]) % HALO
    small_g = _all_reduce_small(jnp.pad(small_g, ((0, pad_rows), (0, 0))))
    loss = small_g[n_gain_rows + n_conv_rows, 0]
    dconv_full = small_g[n_gain_rows:n_gain_rows + n_conv_rows].reshape(CONV_K, w_conv)
    dconv_mine = lax.dynamic_slice(dconv_full, (0, my_slot * cw), (CONV_K, cw))

    def pack(vecs, conv_part):
        rows = [_rows(t.reshape(-1)) for t in vecs]
        rows.append(jnp.pad(conv_part, ((0, HALO - CONV_K), (0, LANES - cw))))
        return jnp.concatenate(rows, axis=0)

    gains = [g_mix, g_conv_out, g_attn_out, g_mlp, g_ple, g_final]
    gains_m = [m_g_mix, m_g_conv_out, m_g_attn_out, m_g_mlp, m_g_ple, m_g_final]
    gains_v = [v_g_mix, v_g_conv_out, v_g_attn_out, v_g_mlp, v_g_ple, v_g_final]
    gpack = jnp.concatenate([small_g[:n_gain_rows], jnp.pad(dconv_mine, ((0, HALO - CONV_K), (0, LANES - cw)))], axis=0)
    sd, sm, sv = _adam_small(pack(gains, conv_w[0]), gpack, pack(gains_m, m_conv_w[0]), pack(gains_v, v_conv_w[0]))

    def unpack(packed):
        out, r0 = [], 0
        for t in gains:
            nr = t.size // LANES
            out.append(packed[r0:r0 + nr].reshape(t.shape))
            r0 += nr
        out.append(packed[r0:r0 + CONV_K, :cw][None])
        return out

    sg_l, sd_l, sm_l, sv_l = unpack(gpack), unpack(sd), unpack(sm), unpack(sv)
    small_names = ["g_mix", "g_conv_out", "g_attn_out", "g_mlp", "g_ple", "g_final", "conv_w"]
    small_out = {n: [sg_l[i], sd_l[i], sm_l[i], sv_l[i]] for i, n in enumerate(small_names)}

    order = ["g_mix", "w_in", "conv_w", "g_conv_out", "g_attn_out", "w_out", "g_mlp", "w_up", "w_down",
             "g_ple", "w_ple_gate", "w_ple_proj", "g_final"]
    table = {**big_out, **small_out}
    outs = [loss, grad_x[None]]
    for kind in range(4):
        outs.extend(table[n][kind] for n in order)
    return tuple(outs)
```

```python
import jax
import jax.numpy as jnp
from jax import lax
from jax.experimental import pallas as pl
from jax.experimental.pallas import tpu as pltpu

F32 = jnp.float32
BF16 = jnp.bfloat16
EPS = 1e-6
HEAD_DIM = 64
LANES = 128
CONV_K = 3
MXU_WIDTH = 256
ATTN_BLOCK = MXU_WIDTH
HALO = 8
N_DEV = 8
MESH = pl.DeviceIdType.MESH
VMEM_LIMIT = 56 * 1024 * 1024

ADAM_LR = 0.001
ADAM_B1 = 0.9
ADAM_B2 = 0.999
ADAM_EPS = 1e-08
ADAM_WD = 0.01
ADAM_STEP = 10


def _pcall(body, **kw):
    return pl.pallas_call(body, **kw)


def _params(sem=None, **kw):
    return pltpu.CompilerParams(dimension_semantics=sem, vmem_limit_bytes=VMEM_LIMIT, **kw)


def _tile(dim, pref):
    t = min(dim, pref)
    while dim % t:
        t -= LANES
    assert t > 0, (dim, pref)
    return t


_NN = (((1,), (0,)), ((), ()))
_NT = (((1,), (1,)), ((), ()))
_TN = (((0,), (0,)), ((), ()))


def _ep_store(acc, outs):
    outs[0][...] = acc.astype(outs[0].dtype)


def _ep_both(acc, outs):
    outs[0][...] = acc
    outs[1][...] = acc.astype(BF16)


def _ep_residual(acc, res, outs):
    outs[0][...] = acc + res[...]


def _ep_up(acc, outs):
    outs[0][...] = jnp.square(jnp.maximum(acc, 0.0)).astype(BF16)


def _ep_dact(acc, act, outs):
    outs[0][...] = (acc * (2.0 * jnp.sqrt(act[...].astype(F32)))).astype(BF16)


def _ep_norm_bwd(acc, h, g, dres, outs):
    @pl.when(pl.program_id(0) == 0)
    def _():
        outs[2][...] = jnp.zeros_like(outs[2])

    hv = h[...]
    r = lax.rsqrt(jnp.mean(hv * hv, axis=-1, keepdims=True) + EPS)
    hn = hv * r
    outs[2][...] += jnp.sum(acc * hn, axis=0, keepdims=True)
    dhn = acc * g[...]
    dh = dres[...] + r * (dhn - hn * jnp.mean(dhn * hn, axis=-1, keepdims=True))
    outs[0][...] = dh
    outs[1][...] = dh.astype(BF16)


def _matmul(name, a, b, *, dims, grid, a_spec, b_spec, acc_shape, out_shapes, out_specs,
            epilogue=_ep_store, extras=(), extra_specs=(), carry=(), sequential=False):
    nk = grid[2]
    plan = _Carried(carry)
    n_ex, n_out, n_xc = len(extras), len(out_shapes), len(plan.inputs)
    n_sems = len(plan.sems)
    last = tuple(g - 1 for g in grid)

    def product(a_ref, b_ref):
        if len(b_ref.shape) == 2:
            return lax.dot_general(a_ref[...].astype(BF16), b_ref[...].astype(BF16), dims,
                                   preferred_element_type=F32)
        width = b_ref.shape[2]
        return sum(lax.dot_general(a_ref[:, g * width:(g + 1) * width].astype(BF16), b_ref[g].astype(BF16), dims,
                                   preferred_element_type=F32) for g in range(b_ref.shape[0]))

    def body(a_ref, b_ref, *rest):
        ex, rest = rest[:n_ex], rest[n_ex:]
        partials, rest = rest[:n_xc], rest[n_xc:]
        outs, rest = rest[:n_out], rest[n_out:]
        received, rest = rest[:n_xc], rest[n_xc:]
        ids = [pl.program_id(axis) for axis in range(3)]
        if n_xc:
            @pl.when((ids[0] == 0) & (ids[1] == 0) & (ids[2] == 0))
            def _():
                for cp in plan.copies(partials, received, rest[-n_sems:]):
                    cp.start()

        if nk == 1 and not sequential:
            for n0 in range(0, acc_shape[1], MXU_WIDTH):
                cols = slice(n0, min(n0 + MXU_WIDTH, acc_shape[1]))
                b_cols = b_ref.at[cols, :] if dims == _NT else b_ref.at[:, cols]
                epilogue(product(a_ref, b_cols), *[e.at[:, cols] for e in ex], [o.at[:, cols] for o in outs])
        elif nk == 1:
            epilogue(product(a_ref, b_ref), *ex, outs)
        else:
            acc = rest[0]

            @pl.when(ids[2] == 0)
            def _():
                acc[...] = product(a_ref, b_ref)

            @pl.when(ids[2] > 0)
            def _():
                acc[...] += product(a_ref, b_ref)

            @pl.when(ids[2] == nk - 1)
            def _():
                epilogue(acc[...], *ex, outs)

        if n_xc:
            @pl.when((ids[0] == last[0]) & (ids[1] == last[1]) & (ids[2] == last[2]))
            def _():
                for cp in plan.copies(partials, received, rest[-n_sems:]):
                    cp.wait()

    anywhere = [pl.BlockSpec(memory_space=pl.ANY)] * n_xc
    return _pcall(
        body, name=name, grid=grid,
        in_specs=[a_spec, b_spec, *extra_specs, *anywhere],
        out_specs=[*out_specs, *anywhere],
        out_shape=[*out_shapes, *plan.out_shapes],
        scratch_shapes=([] if nk == 1 else [pltpu.VMEM(acc_shape, F32)]) + plan.sems,
        input_output_aliases=plan.aliases(2 + n_ex, n_out),
        compiler_params=_params(("arbitrary",) * 3 if n_xc or sequential else ("parallel", "parallel", "arbitrary")),
    )(a, b, *extras, *plan.inputs)


_NO_CARRY = ()


def _mm_nn(name, a, w, *, n_shard=None, epilogue=_ep_store, out_dtypes=(F32,), extras=(), carry=_NO_CARRY,
           tm=1024, tn=1024, tk=1024):
    m, kd = a.shape
    if n_shard is None:
        n = w.shape[1]
        tn = _tile(n, tn)
        tk = _tile(kd, tk)
        b_spec = pl.BlockSpec((tk, tn), lambda i, j, k: (k, j))
    else:
        n = N_DEV * n_shard
        tn = _tile(n_shard, tn)
        tk = _tile(kd, tk)
        per = n_shard // tn
        b_spec = pl.BlockSpec((None, tk, tn), lambda i, j, k: (j // per, k, j % per))
    tm = _tile(m, tm)
    o_spec = pl.BlockSpec((tm, tn), lambda i, j, k: (i, j))
    return _matmul(
        name, a, w, dims=_NN, grid=(m // tm, n // tn, kd // tk),
        a_spec=pl.BlockSpec((tm, tk), lambda i, j, k: (i, k)), b_spec=b_spec,
        acc_shape=(tm, tn),
        out_shapes=[jax.ShapeDtypeStruct((m, n), d) for d in out_dtypes],
        out_specs=[o_spec] * len(out_dtypes),
        epilogue=epilogue, extras=extras, extra_specs=[o_spec] * len(extras), carry=carry)


def _mm_nt(name, a, w, *, k_shard=None, epilogue=_ep_store, out_dtypes=(F32,), extras=(), carry=_NO_CARRY,
           tm=1024, tn=1024, tk=1024):
    m, kd = a.shape
    if k_shard is None:
        n = w.shape[0]
        tn = _tile(n, tn)
        tk = _tile(kd, tk)
        b_spec = pl.BlockSpec((tn, tk), lambda i, j, k: (j, k))
    else:
        n = w.shape[1]
        tn = _tile(n, tn)
        tk = _tile(k_shard, tk)
        per = k_shard // tk
        b_spec = pl.BlockSpec((None, tn, tk), lambda i, j, k: (k // per, j, k % per))
    tm = _tile(m, tm)
    o_spec = pl.BlockSpec((tm, tn), lambda i, j, k: (i, j))
    return _matmul(
        name, a, w, dims=_NT, grid=(m // tm, n // tn, kd // tk),
        a_spec=pl.BlockSpec((tm, tk), lambda i, j, k: (i, k)), b_spec=b_spec,
        acc_shape=(tm, tn),
        out_shapes=[jax.ShapeDtypeStruct((m, n), d) for d in out_dtypes],
        out_specs=[o_spec] * len(out_dtypes),
        epilogue=epilogue, extras=extras, extra_specs=[o_spec] * len(extras), carry=carry)


def _mm_nt_norm_bwd(name, a, w, h, g, dres, *, k_shard=None, carry=_NO_CARRY, tm=512, tk=1024):
    m, kd = a.shape
    n = h.shape[1]
    if k_shard is None:
        tk = _tile(kd, tk)
        b_spec = pl.BlockSpec((n, tk), lambda i, j, k: (0, k))
    else:
        group = max(1, min(tk // k_shard, N_DEV))
        while N_DEV % group:
            group -= 1
        tk = group * k_shard
        b_spec = pl.BlockSpec((group, n, k_shard), lambda i, j, k: (k, 0, 0))
    tm = _tile(m, tm)
    rows = pl.BlockSpec((tm, n), lambda i, j, k: (i, 0))
    vec = pl.BlockSpec((1, n), lambda i, j, k: (0, 0))
    return _matmul(
        name, a, w, dims=_NT, grid=(m // tm, 1, kd // tk),
        a_spec=pl.BlockSpec((tm, tk), lambda i, j, k: (i, k)), b_spec=b_spec, acc_shape=(tm, n),
        out_shapes=[jax.ShapeDtypeStruct((m, n), F32), jax.ShapeDtypeStruct((m, n), BF16),
                    jax.ShapeDtypeStruct((1, n), F32)],
        out_specs=[rows, rows, vec], epilogue=_ep_norm_bwd,
        extras=(h, g, dres), extra_specs=[rows, vec, rows], carry=carry, sequential=True)


TN_TILE_BYTES = 40 * 1024 * 1024


def _mm_tn(name, a, b, *, n_shard=None, carry=_NO_CARRY, tm=1024, tn=1024):
    t, m = a.shape
    n = b.shape[1]
    tm = _tile(m, tm)
    tn = _tile(n if n_shard is None else n_shard, tn)
    tk = t
    while 2 * 2 * tk * (tm + tn) + 4 * tm * tn * 5 > TN_TILE_BYTES and tk % (2 * LANES) == 0:
        tk //= 2
    if n_shard is None:
        o_spec = pl.BlockSpec((tm, tn), lambda i, j, k: (i, j))
        shape = (m, n)
    else:
        per = n_shard // tn
        o_spec = pl.BlockSpec((None, tm, tn), lambda i, j, k: (j // per, i, j % per))
        shape = (N_DEV, m, n_shard)
    return _matmul(
        name, a, b, dims=_TN, grid=(m // tm, n // tn, t // tk),
        a_spec=pl.BlockSpec((tk, tm), lambda i, j, k: (k, i)),
        b_spec=pl.BlockSpec((tk, tn), lambda i, j, k: (k, j)),
        acc_shape=(tm, tn), epilogue=_ep_both, carry=carry,
        out_shapes=[jax.ShapeDtypeStruct(shape, F32), jax.ShapeDtypeStruct(shape, BF16)],
        out_specs=[o_spec, o_spec])


def _ple_proj(p, w_g, tm=1024):
    s, kd = p.shape
    ns = w_g.shape[2]
    tm = _tile(s, tm)

    def body(p_ref, w_ref, o_ref):
        pv = p_ref[...].astype(BF16)
        for j in range(N_DEV):
            o_ref[:, j * ns:(j + 1) * ns] = jnp.dot(pv, w_ref[j], preferred_element_type=F32)

    return _pcall(body, name="ple_proj", grid=(s // tm,),
                  in_specs=[pl.BlockSpec((tm, kd), lambda i: (i, 0)),
                            pl.BlockSpec((N_DEV, kd, ns), lambda i: (0, 0, 0))],
                  out_specs=pl.BlockSpec((tm, N_DEV * ns), lambda i: (i, 0)),
                  out_shape=jax.ShapeDtypeStruct((s, N_DEV * ns), F32),
                  compiler_params=_params(("parallel",)))(p, w_g)


def _d_ple_proj(p, dpp, ns, tk=1024):
    s, kd = p.shape
    tk = _tile(s, tk)
    nk = s // tk

    def body(p_ref, d_ref, of_ref, ob_ref, acc):
        k = pl.program_id(0)

        @pl.when(k == 0)
        def _():
            acc[...] = jnp.zeros_like(acc)

        pv = p_ref[...].astype(BF16)
        for j in range(N_DEV):
            acc[j] += lax.dot_general(pv, d_ref[:, j * ns:(j + 1) * ns], _TN, preferred_element_type=F32)

        @pl.when(k == nk - 1)
        def _():
            of_ref[...] = acc[...]
            ob_ref[...] = acc[...].astype(BF16)

    whole = pl.BlockSpec((N_DEV, kd, ns), lambda k: (0, 0, 0))
    return _pcall(body, name="d_w_ple_proj", grid=(nk,),
                  in_specs=[pl.BlockSpec((tk, kd), lambda k: (k, 0)),
                            pl.BlockSpec((tk, N_DEV * ns), lambda k: (k, 0))],
                  out_specs=[whole, whole],
                  out_shape=[jax.ShapeDtypeStruct((N_DEV, kd, ns), F32), jax.ShapeDtypeStruct((N_DEV, kd, ns), BF16)],
                  scratch_shapes=[pltpu.VMEM((N_DEV, kd, ns), F32)],
                  compiler_params=_params(("arbitrary",)))(p, dpp)


def _row_spec(tr, d):
    return pl.BlockSpec((tr, d), lambda i: (i, 0))


def _vec_spec(d):
    return pl.BlockSpec((1, d), lambda i: (0, 0))


def _rmsnorm_fwd(name, x, g, tr=1024):
    s, d = x.shape
    tr = _tile(s, tr)

    def body(x_ref, g_ref, o_ref):
        xv = x_ref[...]
        r = lax.rsqrt(jnp.mean(xv * xv, axis=-1, keepdims=True) + EPS)
        o_ref[...] = (xv * r * g_ref[...]).astype(BF16)

    return _pcall(body, name=name, grid=(s // tr,),
                  in_specs=[_row_spec(tr, d), _vec_spec(d)], out_specs=_row_spec(tr, d),
                  out_shape=jax.ShapeDtypeStruct((s, d), BF16),
                  compiler_params=_params(("parallel",)))(x, g)


def _ple_loss(h2, gl, pp, tgt, g_final, tr=512):
    s, d = h2.shape
    tr = _tile(s, tr)

    def body(h2_ref, gl_ref, pp_ref, t_ref, g_ref, loss_ref, dh3_ref, dgl_ref, dpp_ref, dg_ref):
        @pl.when(pl.program_id(0) == 0)
        def _():
            dg_ref[...] = jnp.zeros_like(dg_ref)
            loss_ref[...] = jnp.zeros_like(loss_ref)

        gate = jax.nn.sigmoid(gl_ref[...])
        ppv = pp_ref[...]
        h3 = h2_ref[...] + gate * ppv
        r = lax.rsqrt(jnp.mean(h3 * h3, axis=-1, keepdims=True) + EPS)
        hn = h3 * r
        gv = g_ref[...]
        diff = hn * gv - t_ref[...]
        row = jnp.mean(diff * diff, axis=-1, keepdims=True)
        loss_ref[...] += 0.5 * jnp.sum(row, axis=0, keepdims=True)
        dy = diff * (1.0 / d)
        dg_ref[...] += jnp.sum(dy * hn, axis=0, keepdims=True)
        dhn = dy * gv
        dh3 = r * (dhn - hn * jnp.mean(dhn * hn, axis=-1, keepdims=True))
        dh3_ref[...] = dh3
        dgl_ref[...] = (dh3 * ppv * gate * (1.0 - gate)).astype(BF16)
        dpp_ref[...] = (dh3 * gate).astype(BF16)

    return _pcall(body, name="ple_loss", grid=(s // tr,),
                  in_specs=[_row_spec(tr, d)] * 4 + [_vec_spec(d)],
                  out_specs=[_vec_spec(LANES), _row_spec(tr, d), _row_spec(tr, d), _row_spec(tr, d), _vec_spec(d)],
                  out_shape=[jax.ShapeDtypeStruct((1, LANES), F32), jax.ShapeDtypeStruct((s, d), F32),
                             jax.ShapeDtypeStruct((s, d), BF16), jax.ShapeDtypeStruct((s, d), BF16),
                             jax.ShapeDtypeStruct((1, d), F32)],
                  compiler_params=_params(("arbitrary",)))(h2, gl, pp, tgt, g_final)


def _low_half():
    return lax.broadcasted_iota(jnp.int32, (1, LANES), 1) < HEAD_DIM


def _half_mean(v, low):
    s_lo = jnp.sum(jnp.where(low, v, 0.0), axis=-1, keepdims=True)
    s_hi = jnp.sum(jnp.where(low, 0.0, v), axis=-1, keepdims=True)
    return jnp.where(low, s_lo, s_hi) * (1.0 / HEAD_DIM)


def _head_norm_bwd(val, dout, g, low):
    r = lax.rsqrt(_half_mean(val * val, low) + EPS)
    vn = val * r
    dvn = dout * g
    return r * (dvn - vn * _half_mean(dvn * vn, low)), dout * vn


def _conv_taps(vv_ext, w_ref, rows):
    v0 = vv_ext[HALO:]
    v1 = pltpu.roll(vv_ext, 1, 0)[HALO:]
    v2 = pltpu.roll(vv_ext, 2, 0)[HALO:]
    del rows
    return w_ref[2:3, :] * v0 + w_ref[1:2, :] * v1 + w_ref[0:1, :] * v2, (v0, v1, v2)


def _conv_fwd(proj, conv_w, g_conv, w_conv, d_model, tr=1024):
    s = proj.shape[0]
    tr = _tile(s, tr)
    hb = tr // HALO

    def main(part):
        return pl.BlockSpec((tr, w_conv), lambda i: (i, part))

    def prev(part):
        return pl.BlockSpec((HALO, w_conv), lambda i: (jnp.maximum(i * hb - 1, 0), part))

    def body(cb_ref, cc_ref, cu_ref, ccp_ref, cup_ref, w_ref, g_ref, o_ref):
        i = pl.program_id(0)
        low = _low_half()
        for j in range(w_conv // LANES):
            cols = slice(j * LANES, (j + 1) * LANES)
            vv_prev = jnp.where(i > 0, ccp_ref[:, cols] * cup_ref[:, cols], 0.0)
            vv_ext = jnp.concatenate([vv_prev, cc_ref[:, cols] * cu_ref[:, cols]], axis=0)
            y, _ = _conv_taps(vv_ext, w_ref.at[:, cols], tr)
            co = cb_ref[:, cols] * y
            r = lax.rsqrt(_half_mean(co * co, low) + EPS)
            o_ref[:, cols] = (co * r * g_ref[:, cols]).astype(BF16)

    return _pcall(
        body, name="conv_fwd", grid=(s // tr,),
        in_specs=[main(0), main(1), main(2), prev(1), prev(2),
                  pl.BlockSpec((CONV_K, w_conv), lambda i: (0, 0)),
                  pl.BlockSpec((1, w_conv), lambda i: (0, 0))],
        out_specs=pl.BlockSpec((tr, w_conv), lambda i: (i, 0)),
        out_shape=jax.ShapeDtypeStruct((s, d_model), BF16),
        compiler_params=_params(("parallel",)),
    )(proj, proj, proj, proj, proj, conv_w, g_conv)


def _conv_bwd(proj, dcat, conv_w, g_conv, dproj, w_conv, tr=1024):
    s = proj.shape[0]
    tr = _tile(s, tr)
    hb = tr // HALO
    last = s // HALO - 1
    nt = s // tr

    def main(part):
        return pl.BlockSpec((tr, w_conv), lambda i: (i, part))

    def prev(part):
        return pl.BlockSpec((HALO, w_conv), lambda i: (jnp.maximum(i * hb - 1, 0), part))

    def nxt(part):
        return pl.BlockSpec((HALO, w_conv), lambda i: (jnp.minimum((i + 1) * hb, last), part))

    def body(cb_ref, cc_ref, cu_ref, dc_ref, ccp_ref, cup_ref, cbn_ref, ccn_ref, cun_ref, dcn_ref,
             w_ref, g_ref, dproj_in, dproj_ref, dw_ref, dg_ref):
        del dproj_in
        i = pl.program_id(0)

        @pl.when(i == 0)
        def _():
            dw_ref[...] = jnp.zeros_like(dw_ref)
            dg_ref[...] = jnp.zeros_like(dg_ref)

        low = _low_half()
        n_ext = tr + HALO
        rowid = lax.broadcasted_iota(jnp.int32, (n_ext, 1), 0)
        for j in range(w_conv // LANES):
            cols = slice(j * LANES, (j + 1) * LANES)
            wj = w_ref.at[:, cols]
            cc, cu = cc_ref[:, cols], cu_ref[:, cols]
            vv_prev = jnp.where(i > 0, ccp_ref[:, cols] * cup_ref[:, cols], 0.0)
            vv_ext = jnp.concatenate([vv_prev, cc * cu, ccn_ref[:, cols] * cun_ref[:, cols]], axis=0)
            y_ext, (v0, v1, v2) = _conv_taps(vv_ext, wj, n_ext)
            cb_ext = jnp.concatenate([cb_ref[:, cols], cbn_ref[:, cols]], axis=0)
            dc_ext = jnp.concatenate([dc_ref[:, cols], dcn_ref[:, cols]], axis=0)
            dco, dgn = _head_norm_bwd(cb_ext * y_ext, dc_ext, g_ref[:, cols], low)
            dyc = jnp.where((rowid < tr) | (i < nt - 1), dco * cb_ext, 0.0)
            dvv = (wj[2:3, :] * dyc[:tr] + wj[1:2, :] * pltpu.roll(dyc, n_ext - 1, 0)[:tr]
                   + wj[0:1, :] * pltpu.roll(dyc, n_ext - 2, 0)[:tr])
            dproj_ref[:, cols] = (dco[:tr] * y_ext[:tr]).astype(BF16)
            dproj_ref[:, w_conv + j * LANES:w_conv + (j + 1) * LANES] = (dvv * cu).astype(BF16)
            dproj_ref[:, 2 * w_conv + j * LANES:2 * w_conv + (j + 1) * LANES] = (dvv * cc).astype(BF16)
            dyt = dyc[:tr]
            for tap, shifted in enumerate((v2, v1, v0)):
                dw_ref[tap:tap + 1, cols] += jnp.sum(dyt * shifted[:tr], axis=0, keepdims=True)
            dg_ref[:, cols] += jnp.sum(dgn[:tr], axis=0, keepdims=True)

    n_cols = dproj.shape[1]
    return _pcall(
        body, name="conv_bwd", grid=(nt,),
        in_specs=[main(0), main(1), main(2), main(0),
                  prev(1), prev(2), nxt(0), nxt(1), nxt(2), nxt(0),
                  pl.BlockSpec((CONV_K, w_conv), lambda i: (0, 0)),
                  pl.BlockSpec((1, w_conv), lambda i: (0, 0)),
                  pl.BlockSpec(memory_space=pl.ANY)],
        out_specs=[pl.BlockSpec((tr, 3 * w_conv), lambda i: (i, 0)),
                   pl.BlockSpec((CONV_K, w_conv), lambda i: (0, 0)),
                   pl.BlockSpec((1, w_conv), lambda i: (0, 0))],
        out_shape=[jax.ShapeDtypeStruct((s, n_cols), BF16),
                   jax.ShapeDtypeStruct((CONV_K, w_conv), F32),
                   jax.ShapeDtypeStruct((1, w_conv), F32)],
        input_output_aliases={12: 0},
        compiler_params=_params(("arbitrary",)),
    )(proj, proj, proj, dcat, proj, proj, proj, proj, proj, dcat, conv_w, g_conv, dproj)


STRIP = 16

ALL_CHAINS = (0, 1, 2, 3)
UPPER_CHAINS = (2, 3)


RUN_FLOOR = -104.0


def _any_weight_left(run_s):
    return (jnp.max(run_s[...]) > RUN_FLOOR).astype(jnp.int32)


def _chains(low):
    return [(2 * half + h, half, msk) for half in range(2)
            for h, msk in enumerate((low, jnp.logical_not(low)))]


def _suffix_operator(t):
    r = lax.broadcasted_iota(jnp.int32, (2 * t, t), 0)
    c = lax.broadcasted_iota(jnp.int32, (2 * t, t), 1)
    return jnp.where((r > c) & ((r < t) | (r - t > c)), 1.0, 0.0).astype(BF16)


def _strips(t):
    return [(i, slice(i * STRIP, (i + 1) * STRIP)) for i in range(t // STRIP)]


def _strip_mask(i, t):
    r = lax.broadcasted_iota(jnp.int32, (STRIP, t), 0) + i * STRIP
    c = lax.broadcasted_iota(jnp.int32, (STRIP, t), 1)
    return r > c


def _store_split(ref, rows, val, t):
    hi = val.astype(BF16)
    ref[rows, 0:t] = hi
    ref[rows, t:2 * t] = (val - hi.astype(F32)).astype(BF16)


def _sb_scores(z_s, split_s, zl_s, tot_s, keep_s, t, diag):
    for i, rows in _strips(t):
        z = z_s[rows, :]
        log_beta = jnp.minimum(z, 0.0) - jnp.log(1.0 + jnp.exp(-jnp.abs(z)))
        log_keep = log_beta - z
        if diag:
            log_keep = jnp.where(_strip_mask(i, t), log_keep, 0.0)
        _store_split(split_s, rows, log_keep, t)
        zl_s[rows, :] = log_beta
        tot_s[rows, :] = _row_sum(log_keep)
        if keep_s is not None:
            keep_s[rows, :] = jnp.exp(log_keep)


def _row_sum(v):
    return jnp.broadcast_to(jnp.sum(v, axis=-1, keepdims=True), (v.shape[0], LANES))


def _wide(r, t):
    return jnp.concatenate([r] * (t // LANES), axis=1)


def _sb_weights(zl_s, suf_s, run_s, tot_s, a_s, t, diag, da_s=None, glog_s=None, gsplit_s=None, gtot_s=None):
    for i, rows in _strips(t):
        run = run_s[rows, :]
        a = jnp.exp(zl_s[rows, :] + suf_s[rows, :] + _wide(run, t))
        if diag:
            a = jnp.where(_strip_mask(i, t), a, 0.0)
        ab = a.astype(BF16)
        a_s[rows, :] = ab
        run_s[rows, :] = run + tot_s[rows, :]
        if da_s is not None:
            glog = ab.astype(F32) * da_s[rows, :]
            glog_s[rows, :] = glog
            _store_split(gsplit_s, rows, glog, t)
            gtot_s[rows, :] = _row_sum(glog)


def _sb_dscores(glog_s, cum_s, rest_s, gtot_s, keep_s, dz_s, t, diag):
    for i, rows in _strips(t):
        glog = glog_s[rows, :]
        rest = rest_s[rows, :]
        from_here = _wide(rest, t) - cum_s[rows, :]
        before = from_here - glog
        dz = from_here * keep_s[rows, :] - before
        if diag:
            dz = jnp.where(_strip_mask(i, t), dz, 0.0)
        dz_s[rows, :] = dz.astype(BF16)
        rest_s[rows, :] = rest - gtot_s[rows, :]


def _attn_fwd(proj, g_attn, cat, w_conv, carry, t=ATTN_BLOCK):
    s = proj.shape[0]
    w_attn = g_attn.shape[1]
    nh = w_attn // LANES
    t = _tile(s, t)
    tq = 2 * t
    nq = s // tq
    q0 = 3 * w_conv // LANES
    scale = HEAD_DIM ** -0.5
    plan = _Carried(carry)
    nw = len(plan.inputs)

    def body(q_ref, k_ref, v_ref, g_ref, cat_in, *rest):
        staged_refs, rest = rest[:nw], rest[nw:]
        o_ref, cat_ref = rest[:2]
        gathered_refs, rest = rest[2:2 + nw], rest[2 + nw:]
        kb, vb, tri_s, qm_s, z_s, split_s, zl_s, suf_s, a_s, run_s, tot_s, acc_s = rest[:12]
        gather_sems = rest[12:]
        del cat_in
        qi = pl.program_id(1)

        @pl.when((pl.program_id(0) == 0) & (qi == 0))
        def _():
            for cp in plan.copies(staged_refs, gathered_refs, gather_sems):
                cp.start()

        @pl.when(qi == 0)
        def _():
            kb[...] = k_ref[...].astype(BF16)
            vb[...] = v_ref[...].astype(BF16)
            tri_s[...] = _suffix_operator(t)

        low = _low_half()
        for c, half, msk in _chains(low):
            qm_s[c] = jnp.where(msk, q_ref[half * t:(half + 1) * t, :] * scale, 0.0).astype(BF16)
            run_s[c] = jnp.zeros((t, LANES), F32)
            acc_s[c] = jnp.zeros((t, LANES), F32)

        def key_rows(kblk):
            return pl.ds(pl.multiple_of(kblk * t, t), t)

        def key_block(base, c):
            return key_rows(jnp.maximum(base + c // 2, 0))

        def scores_matmul(base, chains):
            for c in chains:
                z_s[c] = lax.dot_general(qm_s[c], kb[key_block(base, c), :], _NT, preferred_element_type=F32)

        def front(modes, base, prev=None):
            for c, diag in modes:
                _sb_scores(z_s.at[c], split_s.at[c], zl_s.at[c], tot_s.at[c], None, t, diag)
                suf_s[c] = jnp.dot(split_s[c], tri_s[...], preferred_element_type=F32)
            if prev is not None:
                tail(*prev)
            scores_matmul(base - 1, ALL_CHAINS)
            for c, diag in modes:
                _sb_weights(zl_s.at[c], suf_s.at[c], run_s.at[c], tot_s.at[c], a_s.at[c], t, diag)

        def tail(base, chains):
            for c in chains:
                acc_s[c] += jnp.dot(a_s[c], vb[key_block(base, c), :], preferred_element_type=F32)

        first = 2 * qi
        scores_matmul(first, ALL_CHAINS)
        front([(c, True) for c in ALL_CHAINS], first)

        def loop(state):
            it = state[0]
            base = first - 1 - it
            front([(c, False) for c in ALL_CHAINS], base, prev=(base + 1, ALL_CHAINS))
            return it + 1, _any_weight_left(run_s)

        done, live = lax.while_loop(lambda state: (state[0] < first) & (state[1] > 0), loop,
                                    (jnp.int32(0), _any_weight_left(run_s)))
        one_more = (done == first) & (live > 0)

        @pl.when(one_more)
        def _():
            front([(c, False) for c in UPPER_CHAINS], -1, prev=(0, ALL_CHAINS))
            tail(-1, UPPER_CHAINS)

        @pl.when(jnp.logical_not(one_more))
        def _():
            tail(first - done, ALL_CHAINS)

        for half in range(2):
            rows = slice(half * t, (half + 1) * t)
            o = jnp.where(low, acc_s[2 * half], acc_s[2 * half + 1])
            o_ref[rows, :] = o
            r = lax.rsqrt(_half_mean(o * o, low) + EPS)
            cat_ref[rows, :] = (o * r * g_ref[...]).astype(BF16)

        @pl.when((pl.program_id(0) == nh - 1) & (qi == nq - 1))
        def _():
            for cp in plan.copies(staged_refs, gathered_refs, gather_sems):
                cp.wait()

    whole = lambda col0: pl.BlockSpec((s, LANES), lambda h, i: (0, col0 + h))
    n_ch = len(ALL_CHAINS)
    res = _pcall(
        body, name="attn_fwd", grid=(nh, nq),
        in_specs=[pl.BlockSpec((tq, LANES), lambda h, i: (i, q0 + h)),
                  whole(q0 + nh), whole(q0 + 2 * nh),
                  pl.BlockSpec((1, LANES), lambda h, i: (0, h)),
                  pl.BlockSpec(memory_space=pl.ANY)] + [pl.BlockSpec(memory_space=pl.ANY)] * nw,
        out_specs=[pl.BlockSpec((tq, LANES), lambda h, i: (i, h)),
                   pl.BlockSpec((tq, LANES), lambda h, i: (i, w_conv // LANES + h))]
        + [pl.BlockSpec(memory_space=pl.ANY)] * nw,
        out_shape=[jax.ShapeDtypeStruct((s, w_attn), F32),
                   jax.ShapeDtypeStruct(cat.shape, BF16)] + plan.out_shapes,
        scratch_shapes=[pltpu.VMEM((s, LANES), BF16), pltpu.VMEM((s, LANES), BF16),
                        pltpu.VMEM((2 * t, t), BF16),
                        pltpu.VMEM((n_ch, t, LANES), BF16),
                        pltpu.VMEM((n_ch, t, t), F32),
                        pltpu.VMEM((n_ch, t, 2 * t), BF16),
                        pltpu.VMEM((n_ch, t, t), F32),
                        pltpu.VMEM((n_ch, t, t), F32),
                        pltpu.VMEM((n_ch, t, t), BF16),
                        pltpu.VMEM((n_ch, t, LANES), F32),
                        pltpu.VMEM((n_ch, t, LANES), F32),
                        pltpu.VMEM((n_ch, t, LANES), F32)]
        + plan.sems,
        input_output_aliases={4: 1, **plan.aliases(5, 2)},
        compiler_params=_params(("arbitrary", "arbitrary")),
    )(proj, proj, proj, g_attn, cat, *plan.inputs)
    return res[0], res[1], res[2:]


def _attn_bwd(proj, o, dcat, g_attn, w_conv, carry, t=ATTN_BLOCK):
    s, n_cols = proj.shape
    w_attn = g_attn.shape[1]
    nh = w_attn // LANES
    t = _tile(s, t)
    tq = 2 * t
    nq = s // tq
    q0 = 3 * w_conv // LANES
    scale = HEAD_DIM ** -0.5
    plan = _Carried(carry)
    nw = len(plan.inputs)

    def body(q_ref, k_ref, v_ref, o_ref, do_ref, g_ref, *rest):
        partial_refs, rest = rest[:nw], rest[nw:]
        dproj_ref, dg_ref = rest[:2]
        received_refs, rest = rest[2:2 + nw], rest[2 + nw:]
        (kb, vb, dkt_acc, dvt_acc, stash, tri_s, qm_s, dom_s, qt_s, dot_s, z_s, da_s, split_s, zl_s,
         keep_s, suf_s, a_s, glog_s, gsplit_s, cum_s, dz_s, run_s, tot_s, rest_s, gtot_s, dq_s) = rest[:26]
        out_sems, scatter_sems = rest[26], rest[27:]
        step_i = pl.program_id(1)
        qi = nq - 1 - step_i
        head_pair = pl.program_id(0)
        first_step = (head_pair == 0) & (step_i == 0)
        last_step = (head_pair == nh - 1) & (step_i == nq - 1)

        @pl.when(first_step)
        def _():
            for cp in plan.copies(partial_refs, received_refs, scatter_sems):
                cp.start()

        def out_copies():
            rows = pl.ds(pl.multiple_of(qi * tq, tq), tq)
            return [pltpu.make_async_copy(
                stash.at[w], dproj_ref.at[rows, pl.ds(pl.multiple_of((q0 + w * nh + head_pair) * LANES, LANES), LANES)],
                out_sems.at[w]) for w in range(3)]

        def walk():
            @pl.when(step_i == 0)
            def _():
                kb[...] = k_ref[...].astype(BF16)
                vb[...] = v_ref[...].astype(BF16)
                tri_s[...] = _suffix_operator(t)
                dkt_acc[...] = jnp.zeros_like(dkt_acc)
                dvt_acc[...] = jnp.zeros_like(dvt_acc)
                dg_ref[...] = jnp.zeros_like(dg_ref)

            low = _low_half()
            gv = g_ref[...]
            for half in range(2):
                rows = slice(half * t, (half + 1) * t)
                q = q_ref[rows, :] * scale
                ov = o_ref[rows, :]
                d_o, dgn = _head_norm_bwd(ov, do_ref[rows, :], gv, low)
                dg_ref[...] += jnp.sum(dgn, axis=0, keepdims=True)
                for h, msk in enumerate((low, jnp.logical_not(low))):
                    c = 2 * half + h
                    qh = jnp.where(msk, q, 0.0)
                    doh = jnp.where(msk, d_o, 0.0)
                    dom = doh.astype(BF16)
                    qm_s[c] = qh.astype(BF16)
                    dom_s[c] = dom
                    qt_s[c] = qh.T.astype(BF16)
                    dot_s[c] = doh.T.astype(BF16)
                    rest_s[c] = _row_sum(dom.astype(F32) * ov)
                    run_s[c] = jnp.zeros((t, LANES), F32)
                    dq_s[c] = jnp.zeros((t, LANES), F32)

            def key_rows(kblk):
                return pl.ds(pl.multiple_of(kblk * t, t), t)

            def block_of(base, half):
                return jnp.maximum(base + half, 0)

            def scores_matmul(base, chains):
                for c in chains:
                    ks = kb[key_rows(block_of(base, c // 2)), :]
                    z_s[c] = lax.dot_general(qm_s[c], ks, _NT, preferred_element_type=F32)

            def da_matmul(base, chains):
                for c in chains:
                    vs = vb[key_rows(block_of(base, c // 2)), :]
                    da_s[c] = lax.dot_general(dom_s[c], vs, _NT, preferred_element_type=F32)

            def front(modes, base, prev=None):
                if prev is not None:
                    tail(*prev)
                for c, diag in modes:
                    _sb_scores(z_s.at[c], split_s.at[c], zl_s.at[c], tot_s.at[c], keep_s.at[c], t, diag)
                    suf_s[c] = jnp.dot(split_s[c], tri_s[...], preferred_element_type=F32)
                scores_matmul(base - 1, ALL_CHAINS)
                for c, diag in modes:
                    _sb_weights(zl_s.at[c], suf_s.at[c], run_s.at[c], tot_s.at[c], a_s.at[c], t, diag,
                                da_s.at[c], glog_s.at[c], gsplit_s.at[c], gtot_s.at[c])
                    cum_s[c] = jnp.dot(gsplit_s[c], tri_s[...], preferred_element_type=F32)
                da_matmul(base - 1, ALL_CHAINS)
                for c, diag in modes:
                    _sb_dscores(glog_s.at[c], cum_s.at[c], rest_s.at[c], gtot_s.at[c], keep_s.at[c],
                                dz_s.at[c], t, diag)

            def tail(base, chains):
                for half in range(2):
                    mine = [c for c in chains if c // 2 == half]
                    if not mine:
                        continue
                    kblk = block_of(base, half)
                    ks = kb[key_rows(kblk), :]
                    dkt = dkt_acc[kblk]
                    dvt = dvt_acc[kblk]
                    for c in mine:
                        dq_s[c] += jnp.dot(dz_s[c], ks, preferred_element_type=F32)
                        dkt = dkt + jnp.dot(qt_s[c], dz_s[c], preferred_element_type=F32)
                        dvt = dvt + jnp.dot(dot_s[c], a_s[c], preferred_element_type=F32)
                    dkt_acc[kblk] = dkt
                    dvt_acc[kblk] = dvt

            first = 2 * qi
            scores_matmul(first, ALL_CHAINS)
            da_matmul(first, ALL_CHAINS)
            front([(c, True) for c in ALL_CHAINS], first)

            def loop(state):
                it = state[0]
                base = first - 1 - it
                front([(c, False) for c in ALL_CHAINS], base, prev=(base + 1, ALL_CHAINS))
                return it + 1, _any_weight_left(run_s)

            done, live = lax.while_loop(lambda state: (state[0] < first) & (state[1] > 0), loop,
                                        (jnp.int32(0), _any_weight_left(run_s)))
            one_more = (done == first) & (live > 0)

            @pl.when(one_more)
            def _():
                front([(c, False) for c in UPPER_CHAINS], -1, prev=(0, ALL_CHAINS))
                tail(-1, UPPER_CHAINS)

            @pl.when(jnp.logical_not(one_more))
            def _():
                tail(first - done, ALL_CHAINS)

            @pl.when(jnp.logical_not(first_step))
            def _():
                for cp in out_copies():
                    cp.wait()

            for half in range(2):
                rows = slice(half * t, (half + 1) * t)
                stash[0, rows, :] = (jnp.where(low, dq_s[2 * half], dq_s[2 * half + 1]) * scale).astype(BF16)
                stash[1, rows, :] = dkt_acc[2 * qi + half].T.astype(BF16)
                stash[2, rows, :] = dvt_acc[2 * qi + half].T.astype(BF16)
            for cp in out_copies():
                cp.start()

        walk()

        @pl.when(last_step)
        def _():
            for cp in out_copies():
                cp.wait()
            for cp in plan.copies(partial_refs, received_refs, scatter_sems):
                cp.wait()

    whole = lambda col0: pl.BlockSpec((s, LANES), lambda h, i: (0, col0 + h))
    blk = lambda col0: pl.BlockSpec((tq, LANES), lambda h, i: (nq - 1 - i, col0 + h))
    n_ch = len(ALL_CHAINS)
    res = _pcall(
        body, name="attn_bwd", grid=(nh, nq),
        in_specs=[blk(q0), whole(q0 + nh), whole(q0 + 2 * nh), blk(0), blk(w_conv // LANES),
                  pl.BlockSpec((1, LANES), lambda h, i: (0, h))] + [pl.BlockSpec(memory_space=pl.ANY)] * nw,
        out_specs=[pl.BlockSpec(memory_space=pl.ANY),
                   pl.BlockSpec((1, LANES), lambda h, i: (0, h))] + [pl.BlockSpec(memory_space=pl.ANY)] * nw,
        out_shape=[jax.ShapeDtypeStruct((s, n_cols), BF16), jax.ShapeDtypeStruct((1, w_attn), F32)]
        + plan.out_shapes,
        scratch_shapes=[pltpu.VMEM((s, LANES), BF16), pltpu.VMEM((s, LANES), BF16),
                        pltpu.VMEM((s // t, LANES, t), F32),
                        pltpu.VMEM((s // t, LANES, t), F32),
                        pltpu.VMEM((3, tq, LANES), BF16),
                        pltpu.VMEM((2 * t, t), BF16),
                        pltpu.VMEM((n_ch, t, LANES), BF16),
                        pltpu.VMEM((n_ch, t, LANES), BF16),
                        pltpu.VMEM((n_ch, LANES, t), BF16),
                        pltpu.VMEM((n_ch, LANES, t), BF16),
                        pltpu.VMEM((n_ch, t, t), F32),
                        pltpu.VMEM((n_ch, t, t), F32),
                        pltpu.VMEM((n_ch, t, 2 * t), BF16),
                        pltpu.VMEM((n_ch, t, t), F32),
                        pltpu.VMEM((n_ch, t, t), F32),
                        pltpu.VMEM((n_ch, t, t), F32),
                        pltpu.VMEM((n_ch, t, t), BF16),
                        pltpu.VMEM((n_ch, t, t), F32),
                        pltpu.VMEM((n_ch, t, 2 * t), BF16),
                        pltpu.VMEM((n_ch, t, t), F32),
                        pltpu.VMEM((n_ch, t, t), BF16),
                        pltpu.VMEM((n_ch, t, LANES), F32),
                        pltpu.VMEM((n_ch, t, LANES), F32),
                        pltpu.VMEM((n_ch, t, LANES), F32),
                        pltpu.VMEM((n_ch, t, LANES), F32),
                        pltpu.VMEM((n_ch, t, LANES), F32),
                        pltpu.SemaphoreType.DMA((3,))]
        + plan.sems,
        input_output_aliases=plan.aliases(6, 2),
        compiler_params=_params(("arbitrary", "arbitrary")),
    )(proj, proj, proj, o, dcat, g_attn, *plan.inputs)
    return res[0], res[1], res[2:]


def _place():
    return lax.axis_index("x"), lax.axis_index("y"), lax.axis_index("c")


def _other_chips(x, y):
    return [(1 - x, y), (x, 1 - y), (1 - x, 1 - y)]


def _slot(px, py, pc):
    return 4 * px + 2 * py + pc


def _all_gather(shards, out_dtypes):
    nw = len(shards)

    def body(*refs):
        ins, outs, stage = refs[:nw], refs[nw:2 * nw], refs[2 * nw:3 * nw]
        send_sems, recv_sems, local_sems = refs[3 * nw:]
        x, y, c = _place()
        me, sibling = (x, y, c), (x, y, 1 - c)
        chips = _other_chips(x, y)

        def copy(w, k, block, to, src=None):
            dst = outs[w].at[_slot(*block)]
            return pltpu.make_async_remote_copy(
                src_ref=dst if src is None else src, dst_ref=dst,
                send_sem=send_sems.at[w * 7 + k], recv_sem=recv_sems.at[w * 7 + k],
                device_id=to, device_id_type=MESH)

        started = []
        local = []
        for w in range(nw):
            stage[w][...] = ins[w][...].astype(stage[w].dtype)
            cp = pltpu.make_async_copy(stage[w], outs[w].at[_slot(*me)], local_sems.at[w])
            cp.start()
            local.append(cp)
            started.append(copy(w, 0, me, sibling, src=stage[w]))
            started[-1].start()
            for j, chip in enumerate(chips):
                started.append(copy(w, 1 + j, me, (*chip, c), src=stage[w]))
                started[-1].start()
        for j, chip in enumerate(chips):
            for w in range(nw):
                copy(w, 1 + j, (*chip, c), me).wait_recv()
                started.append(copy(w, 4 + j, (*chip, c), sibling))
                started[-1].start()
        for w in range(nw):
            copy(w, 0, sibling, me).wait_recv()
            for j, chip in enumerate(chips):
                copy(w, 4 + j, (*chip, 1 - c), me).wait_recv()
        for cp in started:
            cp.wait_send()
        for cp in local:
            cp.wait()

    return _pcall(
        body, name="all_gather_weights",
        in_specs=[pl.BlockSpec(memory_space=pltpu.VMEM)] * nw,
        out_specs=[pl.BlockSpec(memory_space=pl.ANY)] * nw,
        out_shape=[jax.ShapeDtypeStruct((N_DEV, *a.shape), d) for a, d in zip(shards, out_dtypes)],
        scratch_shapes=[pltpu.VMEM(a.shape, d) for a, d in zip(shards, out_dtypes)]
        + [pltpu.SemaphoreType.DMA((7 * nw,)), pltpu.SemaphoreType.DMA((7 * nw,)),
           pltpu.SemaphoreType.DMA((nw,))],
        compiler_params=_params(),
    )(*shards)


N_PEERS = N_DEV - 1


def _peer(k):
    x, y, c = _place()
    return (x ^ (k >> 2), y ^ ((k >> 1) & 1), c ^ (k & 1))


def _remote(src, dst, sems, index, to):
    return pltpu.make_async_remote_copy(src_ref=src, dst_ref=dst, send_sem=sems[0].at[index],
                                        recv_sem=sems[1].at[index], device_id=to, device_id_type=MESH)


def _gather_out_copies(staged, gathered, sems):
    x, y, c = _place()
    me = _slot(x, y, c)
    targets = [(x, y, 1 - c)] + [(*chip, c) for chip in _other_chips(x, y)]
    copies = []
    for w, (src, dst) in enumerate(zip(staged, gathered)):
        copies.append(pltpu.make_async_copy(src, dst.at[me], sems[2].at[w]))
        copies += [_remote(src, dst.at[me], sems, w * len(targets) + k, to) for k, to in enumerate(targets)]
    return copies


def _gather_pass_copies(arrived, gathered, sems):
    x, y, c = _place()
    chips = _other_chips(x, y)
    return [_remote(src.at[_slot(*chip, c)], dst.at[_slot(*chip, c)], sems, w * len(chips) + j, (x, y, 1 - c))
            for w, (src, dst) in enumerate(zip(arrived, gathered)) for j, chip in enumerate(chips)]


def _scatter_copies(partials, received, sems):
    me = _slot(*_place())
    return [_remote(src.at[me ^ k], dst.at[k - 1], sems, w * N_PEERS + k - 1, _peer(k))
            for w, (src, dst) in enumerate(zip(partials, received)) for k in range(1, N_DEV)]


class _Carried:
    COPIES = {"gather_out": (_gather_out_copies, 4, True), "gather_pass": (_gather_pass_copies, 3, False),
              "scatter": (_scatter_copies, N_PEERS, False)}

    def __init__(self, jobs):
        self.jobs = [(kind, list(arrays)) for kind, arrays in jobs if len(arrays)]
        self.inputs = [a for _, arrays in self.jobs for a in arrays]
        self.out_shapes, self.sems, self.sem_counts = [], [], []
        for kind, arrays in self.jobs:
            _, fan, local = self.COPIES[kind]
            for a in arrays:
                shape = {"gather_out": (N_DEV, *a.shape), "gather_pass": a.shape,
                         "scatter": (N_PEERS, *a.shape[1:])}[kind]
                self.out_shapes.append(jax.ShapeDtypeStruct(shape, BF16))
            job_sems = [pltpu.SemaphoreType.DMA((fan * len(arrays),))] * 2
            job_sems += [pltpu.SemaphoreType.DMA((len(arrays),))] if local else []
            self.sems += job_sems
            self.sem_counts.append(len(job_sems))

    def aliases(self, first_input, first_output):
        pairs, at = {}, 0
        for kind, arrays in self.jobs:
            if kind == "gather_pass":
                pairs.update({first_input + at + i: first_output + at + i for i in range(len(arrays))})
            at += len(arrays)
        return pairs

    def copies(self, in_refs, out_refs, sem_refs):
        out, at, sem_at = [], 0, 0
        for (kind, arrays), n_sems in zip(self.jobs, self.sem_counts):
            n = len(arrays)
            out += self.COPIES[kind][0](in_refs[at:at + n], out_refs[at:at + n], sem_refs[sem_at:sem_at + n_sems])
            at, sem_at = at + n, sem_at + n_sems
        return out


def _cast_shards(shards):
    def body(*refs):
        for src, dst in zip(refs[:len(shards)], refs[len(shards):]):
            dst[...] = src[...].astype(BF16)

    return _pcall(
        body, name="cast_shards",
        in_specs=[pl.BlockSpec(memory_space=pltpu.VMEM)] * len(shards),
        out_specs=[pl.BlockSpec(memory_space=pltpu.VMEM)] * len(shards),
        out_shape=[jax.ShapeDtypeStruct(a.shape, BF16) for a in shards],
        compiler_params=_params(),
    )(*shards)


def _all_reduce_small(packed):
    r = packed.shape[0]

    def body(x_ref, o_ref, gathered, send_sems, recv_sems):
        x, y, c = _place()
        me = _slot(x, y, c)
        gathered[me] = x_ref[...]
        peers = [(px, py, pc) for px in range(2) for py in range(2) for pc in range(2)]
        started = []
        for k in range(1, N_DEV):
            to = (x ^ (k >> 2), y ^ ((k >> 1) & 1), c ^ (k & 1))
            cp = pltpu.make_async_remote_copy(
                src_ref=x_ref, dst_ref=gathered.at[me],
                send_sem=send_sems.at[k - 1], recv_sem=recv_sems.at[k - 1],
                device_id=to, device_id_type=MESH)
            cp.start()
            started.append(cp)
        del peers
        for cp in started:
            cp.wait()
        total = gathered[0]
        for k in range(1, N_DEV):
            total = total + gathered[k]
        o_ref[...] = total

    return _pcall(
        body, name="all_reduce_small",
        in_specs=[pl.BlockSpec(memory_space=pltpu.VMEM)],
        out_specs=pl.BlockSpec(memory_space=pltpu.VMEM),
        out_shape=jax.ShapeDtypeStruct(packed.shape, F32),
        scratch_shapes=[pltpu.VMEM((N_DEV, r, LANES), F32),
                        pltpu.SemaphoreType.DMA((N_DEV - 1,)), pltpu.SemaphoreType.DMA((N_DEV - 1,))],
        compiler_params=_params(),
    )(packed)


def _adam_math(w, g, m, v):
    m = ADAM_B1 * m + (1.0 - ADAM_B1) * g
    v = ADAM_B2 * v + (1.0 - ADAM_B2) * jnp.square(g)
    m_hat = m / (1.0 - ADAM_B1 ** ADAM_STEP)
    v_hat = v / (1.0 - ADAM_B2 ** ADAM_STEP)
    delta = -ADAM_LR * (m_hat / (jnp.sqrt(v_hat) + ADAM_EPS) + ADAM_WD * w)
    return delta, m, v


ADAM_TILE_BYTES = 24 * 1024 * 1024


def _adam_sharded(name, own, received, w, m, v, place):
    r, cdim = w.shape
    row_bytes = 2 * cdim * (4 + 2 * N_PEERS + 3 * 4 + 4 * 4)
    tr = _tile(r, max(LANES, ADAM_TILE_BYTES // row_bytes // LANES * LANES)) if r % LANES == 0 else r

    def body(place_ref, own_ref, rec_ref, w_ref, m_ref, v_ref, g_ref, d_ref, nm_ref, nv_ref):
        del place_ref
        g = own_ref[...]
        for j in range(N_PEERS):
            g = g + rec_ref[j].astype(F32)
        delta, nm, nv = _adam_math(w_ref[...], g, m_ref[...], v_ref[...])
        g_ref[...] = g
        d_ref[...] = delta
        nm_ref[...] = nm
        nv_ref[...] = nv

    blk = pl.BlockSpec((tr, cdim), lambda i, pr: (i, 0))
    grid_spec = pltpu.PrefetchScalarGridSpec(
        num_scalar_prefetch=1, grid=(r // tr,),
        in_specs=[pl.BlockSpec((None, tr, cdim), lambda i, pr: (4 * pr[0] + 2 * pr[1] + pr[2], i, 0)),
                  pl.BlockSpec((N_PEERS, tr, cdim), lambda i, pr: (0, i, 0)), blk, blk, blk],
        out_specs=[blk] * 4)
    return _pcall(body, name=name, grid_spec=grid_spec,
                  out_shape=[jax.ShapeDtypeStruct((r, cdim), F32)] * 4,
                  compiler_params=_params(("parallel",)))(place, own, received, w, m, v)


def _adam_small(w, g, m, v):
    def body(w_ref, g_ref, m_ref, v_ref, d_ref, nm_ref, nv_ref):
        delta, nm, nv = _adam_math(w_ref[...], g_ref[...], m_ref[...], v_ref[...])
        d_ref[...] = delta
        nm_ref[...] = nm
        nv_ref[...] = nv

    return _pcall(body, name="adam_small",
                  in_specs=[pl.BlockSpec(memory_space=pltpu.VMEM)] * 4,
                  out_specs=[pl.BlockSpec(memory_space=pltpu.VMEM)] * 3,
                  out_shape=[jax.ShapeDtypeStruct(w.shape, F32)] * 3,
                  compiler_params=_params())(w, g, m, v)


def _rows(vec):
    return vec.reshape(-1, LANES)


def kernel(x, p, g_mix, w_in, conv_w, g_conv_out, g_attn_out, w_out, g_mlp, w_up, w_down, g_ple, w_ple_gate, w_ple_proj, g_final, loss_target, m_g_mix, m_w_in, m_conv_w, m_g_conv_out, m_g_attn_out, m_w_out, m_g_mlp, m_w_up, m_w_down, m_g_ple, m_w_ple_gate, m_w_ple_proj, m_g_final, v_g_mix, v_w_in, v_conv_w, v_g_conv_out, v_g_attn_out, v_w_out, v_g_mlp, v_w_up, v_w_down, v_g_ple, v_w_ple_gate, v_w_ple_proj, v_g_final):
    s, d = x.shape[1], x.shape[2]
    w_conv = g_conv_out.shape[1]
    w_attn = g_attn_out.shape[1]
    cw = conv_w.shape[2]
    xs, ps, tgt = x[0], p[0, 0], loss_target[0]
    place = jnp.stack([lax.axis_index("x"), lax.axis_index("y"), lax.axis_index("c")]).astype(jnp.int32)
    my_slot = 4 * place[0] + 2 * place[1] + place[2]

    conv_tile = jnp.pad(conv_w[0], ((0, HALO - CONV_K), (0, LANES - cw)))
    big = [w_in[0], w_out[0], w_up[0], w_down[0], w_ple_gate[0], w_ple_proj[0]]
    win_g, conv_g = _all_gather([big[0], conv_tile], [BF16, F32])
    s_out, s_up, s_down, s_gate, s_proj = _cast_shards(big[1:])
    conv_full = jnp.transpose(conv_g[:, :CONV_K, :cw], (1, 0, 2)).reshape(CONV_K, w_conv)
    in_shard, up_shard, proj_shard = big[0].shape[1], big[2].shape[1], big[5].shape[1]

    a = _rmsnorm_fwd("norm_mix", xs, g_mix)
    proj, g_out, g_gate, g_proj = _mm_nn("in_proj", a, win_g, n_shard=in_shard, tn=in_shard, tm=2048,
                                         carry=[("gather_out", [s_out, s_gate, s_proj])])
    cat = _conv_fwd(proj, conv_full, g_conv_out, w_conv, d)
    o, cat, (g_up, g_down, wout_g, wgate_g, wproj_g) = _attn_fwd(
        proj, g_attn_out, cat, w_conv,
        [("gather_out", [s_up, s_down]), ("gather_pass", [g_out, g_gate, g_proj])])
    wout_f = wout_g.reshape(-1, wout_g.shape[-1])
    wgate_f = wgate_g.reshape(-1, wgate_g.shape[-1])
    h1, wup_g = _mm_nn("out_proj", cat, wout_f, epilogue=_ep_residual, extras=(xs,),
                       carry=[("gather_pass", [g_up])])
    mn = _rmsnorm_fwd("norm_mlp", h1, g_mlp)
    act, wdown_g = _mm_nn("mlp_up", mn, wup_g, n_shard=up_shard, epilogue=_ep_up, out_dtypes=(BF16,), tm=2048,
                          carry=[("gather_pass", [g_down])])
    wdown_f = wdown_g.reshape(-1, wdown_g.shape[-1])
    h2, = _mm_nn("mlp_down", act, wdown_f, epilogue=_ep_residual, extras=(h1,))
    n3 = _rmsnorm_fwd("norm_ple", h2, g_ple)
    gl, = _mm_nn("ple_gate", n3, wgate_f)
    pp = _ple_proj(ps, wproj_g)
    loss_part, dh3, dgl, dpp, dg_final = _ple_loss(h2, gl, pp, tgt, g_final.reshape(1, d))

    def slots(t2d):
        return t2d.reshape(N_DEV, -1, t2d.shape[-1])

    dw_proj = _d_ple_proj(ps, dpp, proj_shard)
    dw_gate = [slots(t) for t in _mm_tn("d_w_ple_gate", n3, dgl)]
    dh2, dh2b, dg_ple = _mm_nt_norm_bwd("d_norm_ple", dgl, wgate_f, h2, g_ple, dh3)
    du, gate_recv, proj_recv = _mm_nt("d_mlp_act", dh2b, wdown_f, epilogue=_ep_dact, out_dtypes=(BF16,),
                                      extras=(act,), tm=2048, carry=[("scatter", [dw_gate[1], dw_proj[1]])])
    dw_down = [slots(t) for t in _mm_tn("d_w_down", act, dh2b)]
    dw_up = _mm_tn("d_w_up", mn, du, n_shard=up_shard)
    dh1, dh1b, dg_mlp = _mm_nt_norm_bwd("d_norm_mlp", du, wup_g, h1, g_mlp, dh2, k_shard=up_shard, tm=1024)
    dcat, = _mm_nt("d_cat", dh1b, wout_f)
    dw_out = [slots(t) for t in _mm_tn("d_w_out", cat, dh1b)]
    dproj, dg_attn, (up_recv, down_recv) = _attn_bwd(proj, o, dcat, g_attn_out, w_conv,
                                                     [("scatter", [dw_up[1], dw_down[1]])])
    dproj, dconv, dg_conv = _conv_bwd(proj, dcat, conv_full, g_conv_out, dproj, w_conv)
    *dw_in, out_recv = _mm_tn("d_w_in", a, dproj, n_shard=in_shard, tn=in_shard,
                              carry=[("scatter", [dw_out[1]])])
    grad_x, _, dg_mix, in_recv = _mm_nt_norm_bwd("d_norm_mix", dproj, win_g, xs, g_mix, dh1, k_shard=in_shard,
                                                 tk=2 * in_shard, tm=1024, carry=[("scatter", [dw_in[1]])])

    names = ["w_in", "w_out", "w_up", "w_down", "w_ple_gate", "w_ple_proj"]
    owns = [dw_in[0], dw_out[0], dw_up[0], dw_down[0], dw_gate[0], dw_proj[0]]
    recvs = [in_recv, out_recv, up_recv, down_recv, gate_recv, proj_recv]
    moments = [(m_w_in, v_w_in), (m_w_out, v_w_out), (m_w_up, v_w_up), (m_w_down, v_w_down),
               (m_w_ple_gate, v_w_ple_gate), (m_w_ple_proj, v_w_ple_proj)]
    big_out = {}
    for n, own, rc, wt, (mm, vv) in zip(names, owns, recvs, big, moments):
        big_out[n] = [t[None] for t in _adam_sharded("adam_" + n, own, rc, wt, mm[0], vv[0], place)]

    n_conv_rows = CONV_K * w_conv // LANES
    small_g = jnp.concatenate(
        [_rows(dg_mix[0]), _rows(dg_conv[0]), _rows(dg_attn[0]), _rows(dg_mlp[0]), _rows(dg_ple[0]),
         _rows(dg_final[0]), _rows(dconv.reshape(-1)), loss_part], axis=0)
    n_gain_rows = small_g.shape[0] - n_conv_rows - 1
    pad_rows = (-small_g.shape[0]) % HALO
    small_g = _all_reduce_small(jnp.pad(small_g, ((0, pad_rows), (0, 0))))
    loss = small_g[n_gain_rows + n_conv_rows, 0]
    dconv_full = small_g[n_gain_rows:n_gain_rows + n_conv_rows].reshape(CONV_K, w_conv)
    dconv_mine = lax.dynamic_slice(dconv_full, (0, my_slot * cw), (CONV_K, cw))

    def pack(vecs, conv_part):
        rows = [_rows(t.reshape(-1)) for t in vecs]
        rows.append(jnp.pad(conv_part, ((0, HALO - CONV_K), (0, LANES - cw))))
        return jnp.concatenate(rows, axis=0)

    gains = [g_mix, g_conv_out, g_attn_out, g_mlp, g_ple, g_final]
    gains_m = [m_g_mix, m_g_conv_out, m_g_attn_out, m_g_mlp, m_g_ple, m_g_final]
    gains_v = [v_g_mix, v_g_conv_out, v_g_attn_out, v_g_mlp, v_g_ple, v_g_final]
    gpack = jnp.concatenate([small_g[:n_gain_rows], jnp.pad(dconv_mine, ((0, HALO - CONV_K), (0, LANES - cw)))], axis=0)
    sd, sm, sv = _adam_small(pack(gains, conv_w[0]), gpack, pack(gains_m, m_conv_w[0]), pack(gains_v, v_conv_w[0]))

    def unpack(packed):
        out, r0 = [], 0
        for t in gains:
            nr = t.size // LANES
            out.append(packed[r0:r0 + nr].reshape(t.shape))
            r0 += nr
        out.append(packed[r0:r0 + CONV_K, :cw][None])
        return out

    sg_l, sd_l, sm_l, sv_l = unpack(gpack), unpack(sd), unpack(sm), unpack(sv)
    small_names = ["g_mix", "g_conv_out", "g_attn_out", "g_mlp", "g_ple", "g_final", "conv_w"]
    small_out = {n: [sg_l[i], sd_l[i], sm_l[i], sv_l[i]] for i, n in enumerate(small_names)}

    order = ["g_mix", "w_in", "conv_w", "g_conv_out", "g_attn_out", "w_out", "g_mlp", "w_up", "w_down",
             "g_ple", "w_ple_gate", "w_ple_proj", "g_final"]
    table = {**big_out, **small_out}
    outs = [loss, grad_x[None]]
    for kind in range(4):
        outs.extend(table[n][kind] for n in order)
    return tuple(outs)
```

```python
import jax
import jax.numpy as jnp
from jax import lax
from jax.experimental import pallas as pl
from jax.experimental.pallas import tpu as pltpu

F32 = jnp.float32
BF16 = jnp.bfloat16
EPS = 1e-6
HEAD_DIM = 64
LANES = 128
CONV_K = 3
MXU_WIDTH = 256
ATTN_BLOCK = MXU_WIDTH
HALO = 8
N_DEV = 8
MESH = pl.DeviceIdType.MESH
VMEM_LIMIT = 56 * 1024 * 1024

ADAM_LR = 0.001
ADAM_B1 = 0.9
ADAM_B2 = 0.999
ADAM_EPS = 1e-08
ADAM_WD = 0.01
ADAM_STEP = 10


def _pcall(body, **kw):
    return pl.pallas_call(body, **kw)


def _params(sem=None, **kw):
    return pltpu.CompilerParams(dimension_semantics=sem, vmem_limit_bytes=VMEM_LIMIT, **kw)


def _tile(dim, pref):
    t = min(dim, pref)
    while dim % t:
        t -= LANES
    assert t > 0, (dim, pref)
    return t


_NN = (((1,), (0,)), ((), ()))
_NT = (((1,), (1,)), ((), ()))
_TN = (((0,), (0,)), ((), ()))


def _ep_store(acc, outs):
    outs[0][...] = acc.astype(outs[0].dtype)


def _ep_both(acc, outs):
    outs[0][...] = acc
    outs[1][...] = acc.astype(BF16)


def _ep_residual(acc, res, outs):
    outs[0][...] = acc + res[...]


def _ep_up(acc, outs):
    outs[0][...] = jnp.square(jnp.maximum(acc, 0.0)).astype(BF16)


def _ep_dact(acc, act, outs):
    outs[0][...] = (acc * (2.0 * jnp.sqrt(act[...].astype(F32)))).astype(BF16)


def _ep_norm_bwd(acc, h, g, dres, outs):
    @pl.when(pl.program_id(0) == 0)
    def _():
        outs[2][...] = jnp.zeros_like(outs[2])

    hv = h[...]
    r = lax.rsqrt(jnp.mean(hv * hv, axis=-1, keepdims=True) + EPS)
    hn = hv * r
    outs[2][...] += jnp.sum(acc * hn, axis=0, keepdims=True)
    dhn = acc * g[...]
    dh = dres[...] + r * (dhn - hn * jnp.mean(dhn * hn, axis=-1, keepdims=True))
    outs[0][...] = dh
    outs[1][...] = dh.astype(BF16)


def _matmul(name, a, b, *, dims, grid, a_spec, b_spec, acc_shape, out_shapes, out_specs,
            epilogue=_ep_store, extras=(), extra_specs=(), carry=(), sequential=False):
    nk = grid[2]
    plan = _Carried(carry)
    n_ex, n_out, n_xc = len(extras), len(out_shapes), len(plan.inputs)
    n_sems = len(plan.sems)
    last = tuple(g - 1 for g in grid)

    def product(a_ref, b_ref):
        if len(b_ref.shape) == 2:
            return lax.dot_general(a_ref[...].astype(BF16), b_ref[...].astype(BF16), dims,
                                   preferred_element_type=F32)
        width = b_ref.shape[2]
        return sum(lax.dot_general(a_ref[:, g * width:(g + 1) * width].astype(BF16), b_ref[g].astype(BF16), dims,
                                   preferred_element_type=F32) for g in range(b_ref.shape[0]))

    def body(a_ref, b_ref, *rest):
        ex, rest = rest[:n_ex], rest[n_ex:]
        partials, rest = rest[:n_xc], rest[n_xc:]
        outs, rest = rest[:n_out], rest[n_out:]
        received, rest = rest[:n_xc], rest[n_xc:]
        ids = [pl.program_id(axis) for axis in range(3)]
        if n_xc:
            @pl.when((ids[0] == 0) & (ids[1] == 0) & (ids[2] == 0))
            def _():
                for cp in plan.copies(partials, received, rest[-n_sems:]):
                    cp.start()

        if nk == 1 and not sequential and dims != _TN:
            for n0 in range(0, acc_shape[1], MXU_WIDTH):
                cols = slice(n0, min(n0 + MXU_WIDTH, acc_shape[1]))
                b_cols = b_ref.at[cols, :] if dims == _NT else b_ref.at[:, cols]
                for m0 in range(0, acc_shape[0], MXU_WIDTH):
                    rows = slice(m0, min(m0 + MXU_WIDTH, acc_shape[0]))
                    epilogue(product(a_ref.at[rows, :], b_cols), *[e.at[rows, cols] for e in ex],
                             [o.at[rows, cols] for o in outs])
        elif nk == 1:
            epilogue(product(a_ref, b_ref), *ex, outs)
        else:
            acc = rest[0]

            @pl.when(ids[2] == 0)
            def _():
                acc[...] = product(a_ref, b_ref)

            @pl.when(ids[2] > 0)
            def _():
                acc[...] += product(a_ref, b_ref)

            @pl.when(ids[2] == nk - 1)
            def _():
                epilogue(acc[...], *ex, outs)

        if n_xc:
            @pl.when((ids[0] == last[0]) & (ids[1] == last[1]) & (ids[2] == last[2]))
            def _():
                for cp in plan.copies(partials, received, rest[-n_sems:]):
                    cp.wait()

    anywhere = [pl.BlockSpec(memory_space=pl.ANY)] * n_xc
    return _pcall(
        body, name=name, grid=grid,
        in_specs=[a_spec, b_spec, *extra_specs, *anywhere],
        out_specs=[*out_specs, *anywhere],
        out_shape=[*out_shapes, *plan.out_shapes],
        scratch_shapes=([] if nk == 1 else [pltpu.VMEM(acc_shape, F32)]) + plan.sems,
        input_output_aliases=plan.aliases(2 + n_ex, n_out),
        compiler_params=_params(("arbitrary",) * 3 if n_xc or sequential else ("parallel", "parallel", "arbitrary")),
    )(a, b, *extras, *plan.inputs)


_NO_CARRY = ()


def _mm_nn(name, a, w, *, n_shard=None, epilogue=_ep_store, out_dtypes=(F32,), extras=(), carry=_NO_CARRY,
           tm=1024, tn=1024, tk=1024):
    m, kd = a.shape
    if n_shard is None:
        n = w.shape[1]
        tn = _tile(n, tn)
        tk = _tile(kd, tk)
        b_spec = pl.BlockSpec((tk, tn), lambda i, j, k: (k, j))
    else:
        n = N_DEV * n_shard
        tn = _tile(n_shard, tn)
        tk = _tile(kd, tk)
        per = n_shard // tn
        b_spec = pl.BlockSpec((None, tk, tn), lambda i, j, k: (j // per, k, j % per))
    tm = _tile(m, tm)
    o_spec = pl.BlockSpec((tm, tn), lambda i, j, k: (i, j))
    return _matmul(
        name, a, w, dims=_NN, grid=(m // tm, n // tn, kd // tk),
        a_spec=pl.BlockSpec((tm, tk), lambda i, j, k: (i, k)), b_spec=b_spec,
        acc_shape=(tm, tn),
        out_shapes=[jax.ShapeDtypeStruct((m, n), d) for d in out_dtypes],
        out_specs=[o_spec] * len(out_dtypes),
        epilogue=epilogue, extras=extras, extra_specs=[o_spec] * len(extras), carry=carry)


def _mm_nt(name, a, w, *, k_shard=None, epilogue=_ep_store, out_dtypes=(F32,), extras=(), carry=_NO_CARRY,
           tm=1024, tn=1024, tk=1024):
    m, kd = a.shape
    if k_shard is None:
        n = w.shape[0]
        tn = _tile(n, tn)
        tk = _tile(kd, tk)
        b_spec = pl.BlockSpec((tn, tk), lambda i, j, k: (j, k))
    else:
        n = w.shape[1]
        tn = _tile(n, tn)
        tk = _tile(k_shard, tk)
        per = k_shard // tk
        b_spec = pl.BlockSpec((None, tn, tk), lambda i, j, k: (k // per, j, k % per))
    tm = _tile(m, tm)
    o_spec = pl.BlockSpec((tm, tn), lambda i, j, k: (i, j))
    return _matmul(
        name, a, w, dims=_NT, grid=(m // tm, n // tn, kd // tk),
        a_spec=pl.BlockSpec((tm, tk), lambda i, j, k: (i, k)), b_spec=b_spec,
        acc_shape=(tm, tn),
        out_shapes=[jax.ShapeDtypeStruct((m, n), d) for d in out_dtypes],
        out_specs=[o_spec] * len(out_dtypes),
        epilogue=epilogue, extras=extras, extra_specs=[o_spec] * len(extras), carry=carry)


def _mm_nt_norm_bwd(name, a, w, h, g, dres, *, k_shard=None, carry=_NO_CARRY, tm=512, tk=1024):
    m, kd = a.shape
    n = h.shape[1]
    if k_shard is None:
        tk = _tile(kd, tk)
        b_spec = pl.BlockSpec((n, tk), lambda i, j, k: (0, k))
    else:
        group = max(1, min(tk // k_shard, N_DEV))
        while N_DEV % group:
            group -= 1
        tk = group * k_shard
        b_spec = pl.BlockSpec((group, n, k_shard), lambda i, j, k: (k, 0, 0))
    tm = _tile(m, tm)
    rows = pl.BlockSpec((tm, n), lambda i, j, k: (i, 0))
    vec = pl.BlockSpec((1, n), lambda i, j, k: (0, 0))
    return _matmul(
        name, a, w, dims=_NT, grid=(m // tm, 1, kd // tk),
        a_spec=pl.BlockSpec((tm, tk), lambda i, j, k: (i, k)), b_spec=b_spec, acc_shape=(tm, n),
        out_shapes=[jax.ShapeDtypeStruct((m, n), F32), jax.ShapeDtypeStruct((m, n), BF16),
                    jax.ShapeDtypeStruct((1, n), F32)],
        out_specs=[rows, rows, vec], epilogue=_ep_norm_bwd,
        extras=(h, g, dres), extra_specs=[rows, vec, rows], carry=carry, sequential=True)


TN_TILE_BYTES = 40 * 1024 * 1024


def _mm_tn(name, a, b, *, n_shard=None, carry=_NO_CARRY, tm=1024, tn=1024):
    t, m = a.shape
    n = b.shape[1]
    tm = _tile(m, tm)
    tn = _tile(n if n_shard is None else n_shard, tn)
    tk = t
    while 2 * 2 * tk * (tm + tn) + 4 * tm * tn * 5 > TN_TILE_BYTES and tk % (2 * LANES) == 0:
        tk //= 2
    if n_shard is None:
        o_spec = pl.BlockSpec((tm, tn), lambda i, j, k: (i, j))
        shape = (m, n)
    else:
        per = n_shard // tn
        o_spec = pl.BlockSpec((None, tm, tn), lambda i, j, k: (j // per, i, j % per))
        shape = (N_DEV, m, n_shard)
    return _matmul(
        name, a, b, dims=_TN, grid=(m // tm, n // tn, t // tk),
        a_spec=pl.BlockSpec((tk, tm), lambda i, j, k: (k, i)),
        b_spec=pl.BlockSpec((tk, tn), lambda i, j, k: (k, j)),
        acc_shape=(tm, tn), epilogue=_ep_both, carry=carry,
        out_shapes=[jax.ShapeDtypeStruct(shape, F32), jax.ShapeDtypeStruct(shape, BF16)],
        out_specs=[o_spec, o_spec])


def _ple_proj(p, w_g, tm=1024):
    s, kd = p.shape
    ns = w_g.shape[2]
    tm = _tile(s, tm)

    def body(p_ref, w_ref, o_ref):
        pv = p_ref[...].astype(BF16)
        for j in range(N_DEV):
            o_ref[:, j * ns:(j + 1) * ns] = jnp.dot(pv, w_ref[j], preferred_element_type=F32)

    return _pcall(body, name="ple_proj", grid=(s // tm,),
                  in_specs=[pl.BlockSpec((tm, kd), lambda i: (i, 0)),
                            pl.BlockSpec((N_DEV, kd, ns), lambda i: (0, 0, 0))],
                  out_specs=pl.BlockSpec((tm, N_DEV * ns), lambda i: (i, 0)),
                  out_shape=jax.ShapeDtypeStruct((s, N_DEV * ns), F32),
                  compiler_params=_params(("parallel",)))(p, w_g)


def _d_ple_proj(p, dpp, ns, tk=1024):
    s, kd = p.shape
    tk = _tile(s, tk)
    nk = s // tk

    def body(p_ref, d_ref, of_ref, ob_ref, acc):
        k = pl.program_id(0)

        @pl.when(k == 0)
        def _():
            acc[...] = jnp.zeros_like(acc)

        pv = p_ref[...].astype(BF16)
        for j in range(N_DEV):
            acc[j] += lax.dot_general(pv, d_ref[:, j * ns:(j + 1) * ns], _TN, preferred_element_type=F32)

        @pl.when(k == nk - 1)
        def _():
            of_ref[...] = acc[...]
            ob_ref[...] = acc[...].astype(BF16)

    whole = pl.BlockSpec((N_DEV, kd, ns), lambda k: (0, 0, 0))
    return _pcall(body, name="d_w_ple_proj", grid=(nk,),
                  in_specs=[pl.BlockSpec((tk, kd), lambda k: (k, 0)),
                            pl.BlockSpec((tk, N_DEV * ns), lambda k: (k, 0))],
                  out_specs=[whole, whole],
                  out_shape=[jax.ShapeDtypeStruct((N_DEV, kd, ns), F32), jax.ShapeDtypeStruct((N_DEV, kd, ns), BF16)],
                  scratch_shapes=[pltpu.VMEM((N_DEV, kd, ns), F32)],
                  compiler_params=_params(("arbitrary",)))(p, dpp)


def _row_spec(tr, d):
    return pl.BlockSpec((tr, d), lambda i: (i, 0))


def _vec_spec(d):
    return pl.BlockSpec((1, d), lambda i: (0, 0))


def _rmsnorm_fwd(name, x, g, tr=1024):
    s, d = x.shape
    tr = _tile(s, tr)

    def body(x_ref, g_ref, o_ref):
        xv = x_ref[...]
        r = lax.rsqrt(jnp.mean(xv * xv, axis=-1, keepdims=True) + EPS)
        o_ref[...] = (xv * r * g_ref[...]).astype(BF16)

    return _pcall(body, name=name, grid=(s // tr,),
                  in_specs=[_row_spec(tr, d), _vec_spec(d)], out_specs=_row_spec(tr, d),
                  out_shape=jax.ShapeDtypeStruct((s, d), BF16),
                  compiler_params=_params(("parallel",)))(x, g)


def _ple_loss(h2, gl, pp, tgt, g_final, tr=512):
    s, d = h2.shape
    tr = _tile(s, tr)

    def body(h2_ref, gl_ref, pp_ref, t_ref, g_ref, loss_ref, dh3_ref, dgl_ref, dpp_ref, dg_ref):
        @pl.when(pl.program_id(0) == 0)
        def _():
            dg_ref[...] = jnp.zeros_like(dg_ref)
            loss_ref[...] = jnp.zeros_like(loss_ref)

        gate = jax.nn.sigmoid(gl_ref[...])
        ppv = pp_ref[...]
        h3 = h2_ref[...] + gate * ppv
        r = lax.rsqrt(jnp.mean(h3 * h3, axis=-1, keepdims=True) + EPS)
        hn = h3 * r
        gv = g_ref[...]
        diff = hn * gv - t_ref[...]
        row = jnp.mean(diff * diff, axis=-1, keepdims=True)
        loss_ref[...] += 0.5 * jnp.sum(row, axis=0, keepdims=True)
        dy = diff * (1.0 / d)
        dg_ref[...] += jnp.sum(dy * hn, axis=0, keepdims=True)
        dhn = dy * gv
        dh3 = r * (dhn - hn * jnp.mean(dhn * hn, axis=-1, keepdims=True))
        dh3_ref[...] = dh3
        dgl_ref[...] = (dh3 * ppv * gate * (1.0 - gate)).astype(BF16)
        dpp_ref[...] = (dh3 * gate).astype(BF16)

    return _pcall(body, name="ple_loss", grid=(s // tr,),
                  in_specs=[_row_spec(tr, d)] * 4 + [_vec_spec(d)],
                  out_specs=[_vec_spec(LANES), _row_spec(tr, d), _row_spec(tr, d), _row_spec(tr, d), _vec_spec(d)],
                  out_shape=[jax.ShapeDtypeStruct((1, LANES), F32), jax.ShapeDtypeStruct((s, d), F32),
                             jax.ShapeDtypeStruct((s, d), BF16), jax.ShapeDtypeStruct((s, d), BF16),
                             jax.ShapeDtypeStruct((1, d), F32)],
                  compiler_params=_params(("arbitrary",)))(h2, gl, pp, tgt, g_final)


def _low_half():
    return lax.broadcasted_iota(jnp.int32, (1, LANES), 1) < HEAD_DIM


def _half_mean(v, low):
    s_lo = jnp.sum(jnp.where(low, v, 0.0), axis=-1, keepdims=True)
    s_hi = jnp.sum(jnp.where(low, 0.0, v), axis=-1, keepdims=True)
    return jnp.where(low, s_lo, s_hi) * (1.0 / HEAD_DIM)


def _head_norm_bwd(val, dout, g, low):
    r = lax.rsqrt(_half_mean(val * val, low) + EPS)
    vn = val * r
    dvn = dout * g
    return r * (dvn - vn * _half_mean(dvn * vn, low)), dout * vn


def _conv_taps(vv_ext, w_ref, rows):
    v0 = vv_ext[HALO:]
    v1 = pltpu.roll(vv_ext, 1, 0)[HALO:]
    v2 = pltpu.roll(vv_ext, 2, 0)[HALO:]
    del rows
    return w_ref[2:3, :] * v0 + w_ref[1:2, :] * v1 + w_ref[0:1, :] * v2, (v0, v1, v2)


def _conv_fwd(proj, conv_w, g_conv, w_conv, d_model, tr=1024):
    s = proj.shape[0]
    tr = _tile(s, tr)
    hb = tr // HALO

    def main(part):
        return pl.BlockSpec((tr, w_conv), lambda i: (i, part))

    def prev(part):
        return pl.BlockSpec((HALO, w_conv), lambda i: (jnp.maximum(i * hb - 1, 0), part))

    def body(cb_ref, cc_ref, cu_ref, ccp_ref, cup_ref, w_ref, g_ref, o_ref):
        i = pl.program_id(0)
        low = _low_half()
        for j in range(w_conv // LANES):
            cols = slice(j * LANES, (j + 1) * LANES)
            vv_prev = jnp.where(i > 0, ccp_ref[:, cols] * cup_ref[:, cols], 0.0)
            vv_ext = jnp.concatenate([vv_prev, cc_ref[:, cols] * cu_ref[:, cols]], axis=0)
            y, _ = _conv_taps(vv_ext, w_ref.at[:, cols], tr)
            co = cb_ref[:, cols] * y
            r = lax.rsqrt(_half_mean(co * co, low) + EPS)
            o_ref[:, cols] = (co * r * g_ref[:, cols]).astype(BF16)

    return _pcall(
        body, name="conv_fwd", grid=(s // tr,),
        in_specs=[main(0), main(1), main(2), prev(1), prev(2),
                  pl.BlockSpec((CONV_K, w_conv), lambda i: (0, 0)),
                  pl.BlockSpec((1, w_conv), lambda i: (0, 0))],
        out_specs=pl.BlockSpec((tr, w_conv), lambda i: (i, 0)),
        out_shape=jax.ShapeDtypeStruct((s, d_model), BF16),
        compiler_params=_params(("parallel",)),
    )(proj, proj, proj, proj, proj, conv_w, g_conv)


def _conv_bwd(proj, dcat, conv_w, g_conv, dproj, w_conv, tr=1024):
    s = proj.shape[0]
    tr = _tile(s, tr)
    hb = tr // HALO
    last = s // HALO - 1
    nt = s // tr

    def main(part):
        return pl.BlockSpec((tr, w_conv), lambda i: (i, part))

    def prev(part):
        return pl.BlockSpec((HALO, w_conv), lambda i: (jnp.maximum(i * hb - 1, 0), part))

    def nxt(part):
        return pl.BlockSpec((HALO, w_conv), lambda i: (jnp.minimum((i + 1) * hb, last), part))

    def body(cb_ref, cc_ref, cu_ref, dc_ref, ccp_ref, cup_ref, cbn_ref, ccn_ref, cun_ref, dcn_ref,
             w_ref, g_ref, dproj_in, dproj_ref, dw_ref, dg_ref):
        del dproj_in
        i = pl.program_id(0)

        @pl.when(i == 0)
        def _():
            dw_ref[...] = jnp.zeros_like(dw_ref)
            dg_ref[...] = jnp.zeros_like(dg_ref)

        low = _low_half()
        n_ext = tr + HALO
        rowid = lax.broadcasted_iota(jnp.int32, (n_ext, 1), 0)
        for j in range(w_conv // LANES):
            cols = slice(j * LANES, (j + 1) * LANES)
            wj = w_ref.at[:, cols]
            cc, cu = cc_ref[:, cols], cu_ref[:, cols]
            vv_prev = jnp.where(i > 0, ccp_ref[:, cols] * cup_ref[:, cols], 0.0)
            vv_ext = jnp.concatenate([vv_prev, cc * cu, ccn_ref[:, cols] * cun_ref[:, cols]], axis=0)
            y_ext, (v0, v1, v2) = _conv_taps(vv_ext, wj, n_ext)
            cb_ext = jnp.concatenate([cb_ref[:, cols], cbn_ref[:, cols]], axis=0)
            dc_ext = jnp.concatenate([dc_ref[:, cols], dcn_ref[:, cols]], axis=0)
            dco, dgn = _head_norm_bwd(cb_ext * y_ext, dc_ext, g_ref[:, cols], low)
            dyc = jnp.where((rowid < tr) | (i < nt - 1), dco * cb_ext, 0.0)
            dvv = (wj[2:3, :] * dyc[:tr] + wj[1:2, :] * pltpu.roll(dyc, n_ext - 1, 0)[:tr]
                   + wj[0:1, :] * pltpu.roll(dyc, n_ext - 2, 0)[:tr])
            dproj_ref[:, cols] = (dco[:tr] * y_ext[:tr]).astype(BF16)
            dproj_ref[:, w_conv + j * LANES:w_conv + (j + 1) * LANES] = (dvv * cu).astype(BF16)
            dproj_ref[:, 2 * w_conv + j * LANES:2 * w_conv + (j + 1) * LANES] = (dvv * cc).astype(BF16)
            dyt = dyc[:tr]
            for tap, shifted in enumerate((v2, v1, v0)):
                dw_ref[tap:tap + 1, cols] += jnp.sum(dyt * shifted[:tr], axis=0, keepdims=True)
            dg_ref[:, cols] += jnp.sum(dgn[:tr], axis=0, keepdims=True)

    n_cols = dproj.shape[1]
    return _pcall(
        body, name="conv_bwd", grid=(nt,),
        in_specs=[main(0), main(1), main(2), main(0),
                  prev(1), prev(2), nxt(0), nxt(1), nxt(2), nxt(0),
                  pl.BlockSpec((CONV_K, w_conv), lambda i: (0, 0)),
                  pl.BlockSpec((1, w_conv), lambda i: (0, 0)),
                  pl.BlockSpec(memory_space=pl.ANY)],
        out_specs=[pl.BlockSpec((tr, 3 * w_conv), lambda i: (i, 0)),
                   pl.BlockSpec((CONV_K, w_conv), lambda i: (0, 0)),
                   pl.BlockSpec((1, w_conv), lambda i: (0, 0))],
        out_shape=[jax.ShapeDtypeStruct((s, n_cols), BF16),
                   jax.ShapeDtypeStruct((CONV_K, w_conv), F32),
                   jax.ShapeDtypeStruct((1, w_conv), F32)],
        input_output_aliases={12: 0},
        compiler_params=_params(("arbitrary",)),
    )(proj, proj, proj, dcat, proj, proj, proj, proj, proj, dcat, conv_w, g_conv, dproj)


STRIP = 16

ALL_CHAINS = (0, 1, 2, 3)
UPPER_CHAINS = (2, 3)


RUN_FLOOR = -104.0


def _any_weight_left(run_s):
    return (jnp.max(run_s[...]) > RUN_FLOOR).astype(jnp.int32)


def _chains(low):
    return [(2 * half + h, half, msk) for half in range(2)
            for h, msk in enumerate((low, jnp.logical_not(low)))]


def _suffix_operator(t):
    r = lax.broadcasted_iota(jnp.int32, (2 * t, t), 0)
    c = lax.broadcasted_iota(jnp.int32, (2 * t, t), 1)
    return jnp.where((r > c) & ((r < t) | (r - t > c)), 1.0, 0.0).astype(BF16)


def _strips(t, diag):
    return [(i, slice(i * STRIP, (i + 1) * STRIP), t // 2 if diag and (i + 1) * STRIP <= t // 2 else t)
            for i in range(t // STRIP)]


def _strip_mask(i, w):
    r = lax.broadcasted_iota(jnp.int32, (STRIP, w), 0) + i * STRIP
    c = lax.broadcasted_iota(jnp.int32, (STRIP, w), 1)
    return r > c


def _store_trimmed(ref, rows, val, w, t, at=0):
    ref[rows, at:at + w] = val
    if w < t:
        ref[rows, at + w:at + t] = jnp.zeros((STRIP, t - w), val.dtype)


def _store_split(ref, rows, val, w, t):
    hi = val.astype(BF16)
    _store_trimmed(ref, rows, hi, w, t)
    _store_trimmed(ref, rows, (val - hi.astype(F32)).astype(BF16), w, t, at=t)


def _sb_scores(z_s, split_s, zl_s, tot_s, keep_s, t, diag):
    for i, rows, w in _strips(t, diag):
        z = z_s[rows, :w]
        log_beta = jnp.minimum(z, 0.0) - jnp.log(1.0 + jnp.exp(-jnp.abs(z)))
        log_keep = log_beta - z
        if diag:
            log_keep = jnp.where(_strip_mask(i, w), log_keep, 0.0)
        _store_split(split_s, rows, log_keep, w, t)
        zl_s[rows, :w] = log_beta
        tot_s[rows, :] = _row_sum(log_keep)
        if keep_s is not None:
            keep_s[rows, :w] = jnp.exp(log_keep)


def _row_sum(v):
    return jnp.broadcast_to(jnp.sum(v, axis=-1, keepdims=True), (v.shape[0], LANES))


def _wide(r, t):
    return jnp.concatenate([r] * (t // LANES), axis=1)


def _sb_weights(zl_s, suf_s, run_s, tot_s, a_s, t, diag, da_s=None, glog_s=None, gsplit_s=None, gtot_s=None):
    for i, rows, w in _strips(t, diag):
        run = run_s[rows, :]
        a = jnp.exp(zl_s[rows, :w] + suf_s[rows, :w] + _wide(run, w))
        if diag:
            a = jnp.where(_strip_mask(i, w), a, 0.0)
        ab = a.astype(BF16)
        _store_trimmed(a_s, rows, ab, w, t)
        run_s[rows, :] = run + tot_s[rows, :]
        if da_s is not None:
            glog = ab.astype(F32) * da_s[rows, :w]
            glog_s[rows, :w] = glog
            _store_split(gsplit_s, rows, glog, w, t)
            gtot_s[rows, :] = _row_sum(glog)


def _sb_dscores(glog_s, cum_s, rest_s, gtot_s, keep_s, dz_s, t, diag):
    for i, rows, w in _strips(t, diag):
        glog = glog_s[rows, :w]
        rest = rest_s[rows, :]
        from_here = _wide(rest, w) - cum_s[rows, :w]
        before = from_here - glog
        dz = from_here * keep_s[rows, :w] - before
        if diag:
            dz = jnp.where(_strip_mask(i, w), dz, 0.0)
        _store_trimmed(dz_s, rows, dz.astype(BF16), w, t)
        rest_s[rows, :] = rest - gtot_s[rows, :]


def _attn_fwd(proj, g_attn, cat, w_conv, carry, t=ATTN_BLOCK):
    s = proj.shape[0]
    w_attn = g_attn.shape[1]
    nh = w_attn // LANES
    t = _tile(s, t)
    tq = 2 * t
    nq = s // tq
    q0 = 3 * w_conv // LANES
    scale = HEAD_DIM ** -0.5
    plan = _Carried(carry)
    nw = len(plan.inputs)

    def body(q_ref, k_ref, v_ref, g_ref, cat_in, *rest):
        staged_refs, rest = rest[:nw], rest[nw:]
        o_ref, cat_ref = rest[:2]
        gathered_refs, rest = rest[2:2 + nw], rest[2 + nw:]
        kb, vb, tri_s, qm_s, z_s, split_s, zl_s, suf_s, a_s, run_s, tot_s, acc_s = rest[:12]
        gather_sems = rest[12:]
        del cat_in
        qi = pl.program_id(1)

        @pl.when((pl.program_id(0) == 0) & (qi == 0))
        def _():
            for cp in plan.copies(staged_refs, gathered_refs, gather_sems):
                cp.start()

        @pl.when(qi == 0)
        def _():
            kb[...] = k_ref[...].astype(BF16)
            vb[...] = v_ref[...].astype(BF16)
            tri_s[...] = _suffix_operator(t)

        low = _low_half()
        for c, half, msk in _chains(low):
            qm_s[c] = jnp.where(msk, q_ref[half * t:(half + 1) * t, :] * scale, 0.0).astype(BF16)
            run_s[c] = jnp.zeros((t, LANES), F32)
            acc_s[c] = jnp.zeros((t, LANES), F32)

        def key_rows(kblk):
            return pl.ds(pl.multiple_of(kblk * t, t), t)

        def key_block(base, c):
            return key_rows(jnp.maximum(base + c // 2, 0))

        def scores_matmul(base, chains):
            for c in chains:
                z_s[c] = lax.dot_general(qm_s[c], kb[key_block(base, c), :], _NT, preferred_element_type=F32)

        def front(modes, base, prev=None):
            for c, diag in modes:
                _sb_scores(z_s.at[c], split_s.at[c], zl_s.at[c], tot_s.at[c], None, t, diag)
                suf_s[c] = jnp.dot(split_s[c], tri_s[...], preferred_element_type=F32)
            if prev is not None:
                tail(*prev)
            scores_matmul(base - 1, ALL_CHAINS)
            for c, diag in modes:
                _sb_weights(zl_s.at[c], suf_s.at[c], run_s.at[c], tot_s.at[c], a_s.at[c], t, diag)

        def tail(base, chains):
            for c in chains:
                acc_s[c] += jnp.dot(a_s[c], vb[key_block(base, c), :], preferred_element_type=F32)

        first = 2 * qi
        scores_matmul(first, ALL_CHAINS)
        front([(c, True) for c in ALL_CHAINS], first)

        def loop(state):
            it = state[0]
            base = first - 1 - it
            front([(c, False) for c in ALL_CHAINS], base, prev=(base + 1, ALL_CHAINS))
            return it + 1, _any_weight_left(run_s)

        done, live = lax.while_loop(lambda state: (state[0] < first) & (state[1] > 0), loop,
                                    (jnp.int32(0), jnp.int32(1)))
        one_more = (done == first) & (live > 0)

        @pl.when(one_more)
        def _():
            front([(c, False) for c in UPPER_CHAINS], -1, prev=(0, ALL_CHAINS))
            tail(-1, UPPER_CHAINS)

        @pl.when(jnp.logical_not(one_more))
        def _():
            tail(first - done, ALL_CHAINS)

        for half in range(2):
            rows = slice(half * t, (half + 1) * t)
            o = jnp.where(low, acc_s[2 * half], acc_s[2 * half + 1])
            o_ref[rows, :] = o
            r = lax.rsqrt(_half_mean(o * o, low) + EPS)
            cat_ref[rows, :] = (o * r * g_ref[...]).astype(BF16)

        @pl.when((pl.program_id(0) == nh - 1) & (qi == nq - 1))
        def _():
            for cp in plan.copies(staged_refs, gathered_refs, gather_sems):
                cp.wait()

    whole = lambda col0: pl.BlockSpec((s, LANES), lambda h, i: (0, col0 + h))
    n_ch = len(ALL_CHAINS)
    res = _pcall(
        body, name="attn_fwd", grid=(nh, nq),
        in_specs=[pl.BlockSpec((tq, LANES), lambda h, i: (i, q0 + h)),
                  whole(q0 + nh), whole(q0 + 2 * nh),
                  pl.BlockSpec((1, LANES), lambda h, i: (0, h)),
                  pl.BlockSpec(memory_space=pl.ANY)] + [pl.BlockSpec(memory_space=pl.ANY)] * nw,
        out_specs=[pl.BlockSpec((tq, LANES), lambda h, i: (i, h)),
                   pl.BlockSpec((tq, LANES), lambda h, i: (i, w_conv // LANES + h))]
        + [pl.BlockSpec(memory_space=pl.ANY)] * nw,
        out_shape=[jax.ShapeDtypeStruct((s, w_attn), F32),
                   jax.ShapeDtypeStruct(cat.shape, BF16)] + plan.out_shapes,
        scratch_shapes=[pltpu.VMEM((s, LANES), BF16), pltpu.VMEM((s, LANES), BF16),
                        pltpu.VMEM((2 * t, t), BF16),
                        pltpu.VMEM((n_ch, t, LANES), BF16),
                        pltpu.VMEM((n_ch, t, t), F32),
                        pltpu.VMEM((n_ch, t, 2 * t), BF16),
                        pltpu.VMEM((n_ch, t, t), F32),
                        pltpu.VMEM((n_ch, t, t), F32),
                        pltpu.VMEM((n_ch, t, t), BF16),
                        pltpu.VMEM((n_ch, t, LANES), F32),
                        pltpu.VMEM((n_ch, t, LANES), F32),
                        pltpu.VMEM((n_ch, t, LANES), F32)]
        + plan.sems,
        input_output_aliases={4: 1, **plan.aliases(5, 2)},
        compiler_params=_params(("arbitrary", "arbitrary")),
    )(proj, proj, proj, g_attn, cat, *plan.inputs)
    return res[0], res[1], res[2:]


def _attn_bwd(proj, o, dcat, g_attn, w_conv, carry, t=ATTN_BLOCK):
    s, n_cols = proj.shape
    w_attn = g_attn.shape[1]
    nh = w_attn // LANES
    t = _tile(s, t)
    tq = 2 * t
    nq = s // tq
    q0 = 3 * w_conv // LANES
    scale = HEAD_DIM ** -0.5
    plan = _Carried(carry)
    nw = len(plan.inputs)

    def body(q_ref, k_ref, v_ref, o_ref, do_ref, g_ref, *rest):
        partial_refs, rest = rest[:nw], rest[nw:]
        dproj_ref, dg_ref = rest[:2]
        received_refs, rest = rest[2:2 + nw], rest[2 + nw:]
        (kb, vb, dkt_acc, dvt_acc, stash, tri_s, qm_s, dom_s, qt_s, dot_s, z_s, da_s, split_s, zl_s,
         keep_s, suf_s, a_s, glog_s, gsplit_s, cum_s, dz_s, run_s, tot_s, rest_s, gtot_s, dq_s) = rest[:26]
        out_sems, scatter_sems = rest[26], rest[27:]
        step_i = pl.program_id(1)
        qi = nq - 1 - step_i
        head_pair = pl.program_id(0)
        first_step = (head_pair == 0) & (step_i == 0)
        last_step = (head_pair == nh - 1) & (step_i == nq - 1)

        @pl.when(first_step)
        def _():
            for cp in plan.copies(partial_refs, received_refs, scatter_sems):
                cp.start()

        def out_copies():
            rows = pl.ds(pl.multiple_of(qi * tq, tq), tq)
            return [pltpu.make_async_copy(
                stash.at[w], dproj_ref.at[rows, pl.ds(pl.multiple_of((q0 + w * nh + head_pair) * LANES, LANES), LANES)],
                out_sems.at[w]) for w in range(3)]

        def walk():
            @pl.when(step_i == 0)
            def _():
                kb[...] = k_ref[...].astype(BF16)
                vb[...] = v_ref[...].astype(BF16)
                tri_s[...] = _suffix_operator(t)
                dkt_acc[...] = jnp.zeros_like(dkt_acc)
                dvt_acc[...] = jnp.zeros_like(dvt_acc)
                dg_ref[...] = jnp.zeros_like(dg_ref)

            low = _low_half()
            gv = g_ref[...]
            for half in range(2):
                rows = slice(half * t, (half + 1) * t)
                q = q_ref[rows, :] * scale
                ov = o_ref[rows, :]
                d_o, dgn = _head_norm_bwd(ov, do_ref[rows, :], gv, low)
                dg_ref[...] += jnp.sum(dgn, axis=0, keepdims=True)
                for h, msk in enumerate((low, jnp.logical_not(low))):
                    c = 2 * half + h
                    qh = jnp.where(msk, q, 0.0)
                    doh = jnp.where(msk, d_o, 0.0)
                    dom = doh.astype(BF16)
                    qm_s[c] = qh.astype(BF16)
                    dom_s[c] = dom
                    qt_s[c] = qh.T.astype(BF16)
                    dot_s[c] = doh.T.astype(BF16)
                    rest_s[c] = _row_sum(dom.astype(F32) * ov)
                    run_s[c] = jnp.zeros((t, LANES), F32)
                    dq_s[c] = jnp.zeros((t, LANES), F32)

            def key_rows(kblk):
                return pl.ds(pl.multiple_of(kblk * t, t), t)

            def block_of(base, half):
                return jnp.maximum(base + half, 0)

            def scores_matmul(base, chains):
                for c in chains:
                    ks = kb[key_rows(block_of(base, c // 2)), :]
                    z_s[c] = lax.dot_general(qm_s[c], ks, _NT, preferred_element_type=F32)

            def da_matmul(base, chains):
                for c in chains:
                    vs = vb[key_rows(block_of(base, c // 2)), :]
                    da_s[c] = lax.dot_general(dom_s[c], vs, _NT, preferred_element_type=F32)

            def front(modes, base, prev=None):
                if prev is not None:
                    tail(*prev)
                for c, diag in modes:
                    _sb_scores(z_s.at[c], split_s.at[c], zl_s.at[c], tot_s.at[c], keep_s.at[c], t, diag)
                    suf_s[c] = jnp.dot(split_s[c], tri_s[...], preferred_element_type=F32)
                scores_matmul(base - 1, ALL_CHAINS)
                for c, diag in modes:
                    _sb_weights(zl_s.at[c], suf_s.at[c], run_s.at[c], tot_s.at[c], a_s.at[c], t, diag,
                                da_s.at[c], glog_s.at[c], gsplit_s.at[c], gtot_s.at[c])
                    cum_s[c] = jnp.dot(gsplit_s[c], tri_s[...], preferred_element_type=F32)
                da_matmul(base - 1, ALL_CHAINS)
                for c, diag in modes:
                    _sb_dscores(glog_s.at[c], cum_s.at[c], rest_s.at[c], gtot_s.at[c], keep_s.at[c],
                                dz_s.at[c], t, diag)

            def tail(base, chains):
                for half in range(2):
                    mine = [c for c in chains if c // 2 == half]
                    if not mine:
                        continue
                    kblk = block_of(base, half)
                    ks = kb[key_rows(kblk), :]
                    dkt = dkt_acc[kblk]
                    dvt = dvt_acc[kblk]
                    for c in mine:
                        dq_s[c] += jnp.dot(dz_s[c], ks, preferred_element_type=F32)
                        dkt = dkt + jnp.dot(qt_s[c], dz_s[c], preferred_element_type=F32)
                        dvt = dvt + jnp.dot(dot_s[c], a_s[c], preferred_element_type=F32)
                    dkt_acc[kblk] = dkt
                    dvt_acc[kblk] = dvt

            first = 2 * qi
            scores_matmul(first, ALL_CHAINS)
            da_matmul(first, ALL_CHAINS)
            front([(c, True) for c in ALL_CHAINS], first)

            def loop(state):
                it = state[0]
                base = first - 1 - it
                front([(c, False) for c in ALL_CHAINS], base, prev=(base + 1, ALL_CHAINS))
                return it + 1, _any_weight_left(run_s)

            done, live = lax.while_loop(lambda state: (state[0] < first) & (state[1] > 0), loop,
                                        (jnp.int32(0), jnp.int32(1)))
            one_more = (done == first) & (live > 0)

            @pl.when(one_more)
            def _():
                front([(c, False) for c in UPPER_CHAINS], -1, prev=(0, ALL_CHAINS))
                tail(-1, UPPER_CHAINS)

            @pl.when(jnp.logical_not(one_more))
            def _():
                tail(first - done, ALL_CHAINS)

            @pl.when(jnp.logical_not(first_step))
            def _():
                for cp in out_copies():
                    cp.wait()

            for half in range(2):
                rows = slice(half * t, (half + 1) * t)
                stash[0, rows, :] = (jnp.where(low, dq_s[2 * half], dq_s[2 * half + 1]) * scale).astype(BF16)
                stash[1, rows, :] = dkt_acc[2 * qi + half].T.astype(BF16)
                stash[2, rows, :] = dvt_acc[2 * qi + half].T.astype(BF16)
            for cp in out_copies():
                cp.start()

        walk()

        @pl.when(last_step)
        def _():
            for cp in out_copies():
                cp.wait()
            for cp in plan.copies(partial_refs, received_refs, scatter_sems):
                cp.wait()

    whole = lambda col0: pl.BlockSpec((s, LANES), lambda h, i: (0, col0 + h))
    blk = lambda col0: pl.BlockSpec((tq, LANES), lambda h, i: (nq - 1 - i, col0 + h))
    n_ch = len(ALL_CHAINS)
    res = _pcall(
        body, name="attn_bwd", grid=(nh, nq),
        in_specs=[blk(q0), whole(q0 + nh), whole(q0 + 2 * nh), blk(0), blk(w_conv // LANES),
                  pl.BlockSpec((1, LANES), lambda h, i: (0, h))] + [pl.BlockSpec(memory_space=pl.ANY)] * nw,
        out_specs=[pl.BlockSpec(memory_space=pl.ANY),
                   pl.BlockSpec((1, LANES), lambda h, i: (0, h))] + [pl.BlockSpec(memory_space=pl.ANY)] * nw,
        out_shape=[jax.ShapeDtypeStruct((s, n_cols), BF16), jax.ShapeDtypeStruct((1, w_attn), F32)]
        + plan.out_shapes,
        scratch_shapes=[pltpu.VMEM((s, LANES), BF16), pltpu.VMEM((s, LANES), BF16),
                        pltpu.VMEM((s // t, LANES, t), F32),
                        pltpu.VMEM((s // t, LANES, t), F32),
                        pltpu.VMEM((3, tq, LANES), BF16),
                        pltpu.VMEM((2 * t, t), BF16),
                        pltpu.VMEM((n_ch, t, LANES), BF16),
                        pltpu.VMEM((n_ch, t, LANES), BF16),
                        pltpu.VMEM((n_ch, LANES, t), BF16),
                        pltpu.VMEM((n_ch, LANES, t), BF16),
                        pltpu.VMEM((n_ch, t, t), F32),
                        pltpu.VMEM((n_ch, t, t), F32),
                        pltpu.VMEM((n_ch, t, 2 * t), BF16),
                        pltpu.VMEM((n_ch, t, t), F32),
                        pltpu.VMEM((n_ch, t, t), F32),
                        pltpu.VMEM((n_ch, t, t), F32),
                        pltpu.VMEM((n_ch, t, t), BF16),
                        pltpu.VMEM((n_ch, t, t), F32),
                        pltpu.VMEM((n_ch, t, 2 * t), BF16),
                        pltpu.VMEM((n_ch, t, t), F32),
                        pltpu.VMEM((n_ch, t, t), BF16),
                        pltpu.VMEM((n_ch, t, LANES), F32),
                        pltpu.VMEM((n_ch, t, LANES), F32),
                        pltpu.VMEM((n_ch, t, LANES), F32),
                        pltpu.VMEM((n_ch, t, LANES), F32),
                        pltpu.VMEM((n_ch, t, LANES), F32),
                        pltpu.SemaphoreType.DMA((3,))]
        + plan.sems,
        input_output_aliases=plan.aliases(6, 2),
        compiler_params=_params(("arbitrary", "arbitrary")),
    )(proj, proj, proj, o, dcat, g_attn, *plan.inputs)
    return res[0], res[1], res[2:]


def _place():
    return lax.axis_index("x"), lax.axis_index("y"), lax.axis_index("c")


def _other_chips(x, y):
    return [(1 - x, y), (x, 1 - y), (1 - x, 1 - y)]


def _slot(px, py, pc):
    return 4 * px + 2 * py + pc


def _all_gather(shards, out_dtypes):
    nw = len(shards)

    def body(*refs):
        ins, outs, stage = refs[:nw], refs[nw:2 * nw], refs[2 * nw:3 * nw]
        send_sems, recv_sems, local_sems = refs[3 * nw:]
        x, y, c = _place()
        me, sibling = (x, y, c), (x, y, 1 - c)
        chips = _other_chips(x, y)

        def copy(w, k, block, to, src=None):
            dst = outs[w].at[_slot(*block)]
            return pltpu.make_async_remote_copy(
                src_ref=dst if src is None else src, dst_ref=dst,
                send_sem=send_sems.at[w * 7 + k], recv_sem=recv_sems.at[w * 7 + k],
                device_id=to, device_id_type=MESH)

        started = []
        local = []
        for w in range(nw):
            stage[w][...] = ins[w][...].astype(stage[w].dtype)
            cp = pltpu.make_async_copy(stage[w], outs[w].at[_slot(*me)], local_sems.at[w])
            cp.start()
            local.append(cp)
            started.append(copy(w, 0, me, sibling, src=stage[w]))
            started[-1].start()
            for j, chip in enumerate(chips):
                started.append(copy(w, 1 + j, me, (*chip, c), src=stage[w]))
                started[-1].start()
        for j, chip in enumerate(chips):
            for w in range(nw):
                copy(w, 1 + j, (*chip, c), me).wait_recv()
                started.append(copy(w, 4 + j, (*chip, c), sibling))
                started[-1].start()
        for w in range(nw):
            copy(w, 0, sibling, me).wait_recv()
            for j, chip in enumerate(chips):
                copy(w, 4 + j, (*chip, 1 - c), me).wait_recv()
        for cp in started:
            cp.wait_send()
        for cp in local:
            cp.wait()

    return _pcall(
        body, name="all_gather_weights",
        in_specs=[pl.BlockSpec(memory_space=pltpu.VMEM)] * nw,
        out_specs=[pl.BlockSpec(memory_space=pl.ANY)] * nw,
        out_shape=[jax.ShapeDtypeStruct((N_DEV, *a.shape), d) for a, d in zip(shards, out_dtypes)],
        scratch_shapes=[pltpu.VMEM(a.shape, d) for a, d in zip(shards, out_dtypes)]
        + [pltpu.SemaphoreType.DMA((7 * nw,)), pltpu.SemaphoreType.DMA((7 * nw,)),
           pltpu.SemaphoreType.DMA((nw,))],
        compiler_params=_params(),
    )(*shards)


N_PEERS = N_DEV - 1


def _peer(k):
    x, y, c = _place()
    return (x ^ (k >> 2), y ^ ((k >> 1) & 1), c ^ (k & 1))


def _remote(src, dst, sems, index, to):
    return pltpu.make_async_remote_copy(src_ref=src, dst_ref=dst, send_sem=sems[0].at[index],
                                        recv_sem=sems[1].at[index], device_id=to, device_id_type=MESH)


def _gather_out_copies(staged, gathered, sems):
    x, y, c = _place()
    me = _slot(x, y, c)
    targets = [(x, y, 1 - c)] + [(*chip, c) for chip in _other_chips(x, y)]
    copies = []
    for w, (src, dst) in enumerate(zip(staged, gathered)):
        copies.append(pltpu.make_async_copy(src, dst.at[me], sems[2].at[w]))
        copies += [_remote(src, dst.at[me], sems, w * len(targets) + k, to) for k, to in enumerate(targets)]
    return copies


def _gather_pass_copies(arrived, gathered, sems):
    x, y, c = _place()
    chips = _other_chips(x, y)
    return [_remote(src.at[_slot(*chip, c)], dst.at[_slot(*chip, c)], sems, w * len(chips) + j, (x, y, 1 - c))
            for w, (src, dst) in enumerate(zip(arrived, gathered)) for j, chip in enumerate(chips)]


def _scatter_copies(partials, received, sems):
    me = _slot(*_place())
    return [_remote(src.at[me ^ k], dst.at[k - 1], sems, w * N_PEERS + k - 1, _peer(k))
            for w, (src, dst) in enumerate(zip(partials, received)) for k in range(1, N_DEV)]


class _Carried:
    COPIES = {"gather_out": (_gather_out_copies, 4, True), "gather_pass": (_gather_pass_copies, 3, False),
              "scatter": (_scatter_copies, N_PEERS, False)}

    def __init__(self, jobs):
        self.jobs = [(kind, list(arrays)) for kind, arrays in jobs if len(arrays)]
        self.inputs = [a for _, arrays in self.jobs for a in arrays]
        self.out_shapes, self.sems, self.sem_counts = [], [], []
        for kind, arrays in self.jobs:
            _, fan, local = self.COPIES[kind]
            for a in arrays:
                shape = {"gather_out": (N_DEV, *a.shape), "gather_pass": a.shape,
                         "scatter": (N_PEERS, *a.shape[1:])}[kind]
                self.out_shapes.append(jax.ShapeDtypeStruct(shape, BF16))
            job_sems = [pltpu.SemaphoreType.DMA((fan * len(arrays),))] * 2
            job_sems += [pltpu.SemaphoreType.DMA((len(arrays),))] if local else []
            self.sems += job_sems
            self.sem_counts.append(len(job_sems))

    def aliases(self, first_input, first_output):
        pairs, at = {}, 0
        for kind, arrays in self.jobs:
            if kind == "gather_pass":
                pairs.update({first_input + at + i: first_output + at + i for i in range(len(arrays))})
            at += len(arrays)
        return pairs

    def copies(self, in_refs, out_refs, sem_refs):
        out, at, sem_at = [], 0, 0
        for (kind, arrays), n_sems in zip(self.jobs, self.sem_counts):
            n = len(arrays)
            out += self.COPIES[kind][0](in_refs[at:at + n], out_refs[at:at + n], sem_refs[sem_at:sem_at + n_sems])
            at, sem_at = at + n, sem_at + n_sems
        return out


def _cast_shards(shards):
    def body(*refs):
        for src, dst in zip(refs[:len(shards)], refs[len(shards):]):
            dst[...] = src[...].astype(BF16)

    return _pcall(
        body, name="cast_shards",
        in_specs=[pl.BlockSpec(memory_space=pltpu.VMEM)] * len(shards),
        out_specs=[pl.BlockSpec(memory_space=pltpu.VMEM)] * len(shards),
        out_shape=[jax.ShapeDtypeStruct(a.shape, BF16) for a in shards],
        compiler_params=_params(),
    )(*shards)


def _all_reduce_small(packed):
    r = packed.shape[0]

    def body(x_ref, o_ref, gathered, send_sems, recv_sems):
        x, y, c = _place()
        me = _slot(x, y, c)
        gathered[me] = x_ref[...]
        peers = [(px, py, pc) for px in range(2) for py in range(2) for pc in range(2)]
        started = []
        for k in range(1, N_DEV):
            to = (x ^ (k >> 2), y ^ ((k >> 1) & 1), c ^ (k & 1))
            cp = pltpu.make_async_remote_copy(
                src_ref=x_ref, dst_ref=gathered.at[me],
                send_sem=send_sems.at[k - 1], recv_sem=recv_sems.at[k - 1],
                device_id=to, device_id_type=MESH)
            cp.start()
            started.append(cp)
        del peers
        for cp in started:
            cp.wait()
        total = gathered[0]
        for k in range(1, N_DEV):
            total = total + gathered[k]
        o_ref[...] = total

    return _pcall(
        body, name="all_reduce_small",
        in_specs=[pl.BlockSpec(memory_space=pltpu.VMEM)],
        out_specs=pl.BlockSpec(memory_space=pltpu.VMEM),
        out_shape=jax.ShapeDtypeStruct(packed.shape, F32),
        scratch_shapes=[pltpu.VMEM((N_DEV, r, LANES), F32),
                        pltpu.SemaphoreType.DMA((N_DEV - 1,)), pltpu.SemaphoreType.DMA((N_DEV - 1,))],
        compiler_params=_params(),
    )(packed)


def _adam_math(w, g, m, v):
    m = ADAM_B1 * m + (1.0 - ADAM_B1) * g
    v = ADAM_B2 * v + (1.0 - ADAM_B2) * jnp.square(g)
    m_hat = m / (1.0 - ADAM_B1 ** ADAM_STEP)
    v_hat = v / (1.0 - ADAM_B2 ** ADAM_STEP)
    delta = -ADAM_LR * (m_hat / (jnp.sqrt(v_hat) + ADAM_EPS) + ADAM_WD * w)
    return delta, m, v


ADAM_TILE_BYTES = 24 * 1024 * 1024


def _adam_sharded(name, own, received, w, m, v, place):
    r, cdim = w.shape
    row_bytes = 2 * cdim * (4 + 2 * N_PEERS + 3 * 4 + 4 * 4)
    tr = _tile(r, max(LANES, ADAM_TILE_BYTES // row_bytes // LANES * LANES)) if r % LANES == 0 else r

    def body(place_ref, own_ref, rec_ref, w_ref, m_ref, v_ref, g_ref, d_ref, nm_ref, nv_ref):
        del place_ref
        g = own_ref[...]
        for j in range(N_PEERS):
            g = g + rec_ref[j].astype(F32)
        delta, nm, nv = _adam_math(w_ref[...], g, m_ref[...], v_ref[...])
        g_ref[...] = g
        d_ref[...] = delta
        nm_ref[...] = nm
        nv_ref[...] = nv

    blk = pl.BlockSpec((tr, cdim), lambda i, pr: (i, 0))
    grid_spec = pltpu.PrefetchScalarGridSpec(
        num_scalar_prefetch=1, grid=(r // tr,),
        in_specs=[pl.BlockSpec((None, tr, cdim), lambda i, pr: (4 * pr[0] + 2 * pr[1] + pr[2], i, 0)),
                  pl.BlockSpec((N_PEERS, tr, cdim), lambda i, pr: (0, i, 0)), blk, blk, blk],
        out_specs=[blk] * 4)
    return _pcall(body, name=name, grid_spec=grid_spec,
                  out_shape=[jax.ShapeDtypeStruct((r, cdim), F32)] * 4,
                  compiler_params=_params(("parallel",)))(place, own, received, w, m, v)


def _adam_small(w, g, m, v):
    def body(w_ref, g_ref, m_ref, v_ref, d_ref, nm_ref, nv_ref):
        delta, nm, nv = _adam_math(w_ref[...], g_ref[...], m_ref[...], v_ref[...])
        d_ref[...] = delta
        nm_ref[...] = nm
        nv_ref[...] = nv

    return _pcall(body, name="adam_small",
                  in_specs=[pl.BlockSpec(memory_space=pltpu.VMEM)] * 4,
                  out_specs=[pl.BlockSpec(memory_space=pltpu.VMEM)] * 3,
                  out_shape=[jax.ShapeDtypeStruct(w.shape, F32)] * 3,
                  compiler_params=_params())(w, g, m, v)


def _rows(vec):
    return vec.reshape(-1, LANES)


def kernel(x, p, g_mix, w_in, conv_w, g_conv_out, g_attn_out, w_out, g_mlp, w_up, w_down, g_ple, w_ple_gate, w_ple_proj, g_final, loss_target, m_g_mix, m_w_in, m_conv_w, m_g_conv_out, m_g_attn_out, m_w_out, m_g_mlp, m_w_up, m_w_down, m_g_ple, m_w_ple_gate, m_w_ple_proj, m_g_final, v_g_mix, v_w_in, v_conv_w, v_g_conv_out, v_g_attn_out, v_w_out, v_g_mlp, v_w_up, v_w_down, v_g_ple, v_w_ple_gate, v_w_ple_proj, v_g_final):
    s, d = x.shape[1], x.shape[2]
    w_conv = g_conv_out.shape[1]
    w_attn = g_attn_out.shape[1]
    cw = conv_w.shape[2]
    xs, ps, tgt = x[0], p[0, 0], loss_target[0]
    place = jnp.stack([lax.axis_index("x"), lax.axis_index("y"), lax.axis_index("c")]).astype(jnp.int32)
    my_slot = 4 * place[0] + 2 * place[1] + place[2]

    conv_tile = jnp.pad(conv_w[0], ((0, HALO - CONV_K), (0, LANES - cw)))
    big = [w_in[0], w_out[0], w_up[0], w_down[0], w_ple_gate[0], w_ple_proj[0]]
    win_g, conv_g = _all_gather([big[0], conv_tile], [BF16, F32])
    s_out, s_up, s_down, s_gate, s_proj = _cast_shards(big[1:])
    conv_full = jnp.transpose(conv_g[:, :CONV_K, :cw], (1, 0, 2)).reshape(CONV_K, w_conv)
    in_shard, up_shard, proj_shard = big[0].shape[1], big[2].shape[1], big[5].shape[1]

    a = _rmsnorm_fwd("norm_mix", xs, g_mix)
    proj, g_out, g_gate, g_proj = _mm_nn("in_proj", a, win_g, n_shard=in_shard, tn=in_shard, tm=2048,
                                         carry=[("gather_out", [s_out, s_gate, s_proj])])
    cat = _conv_fwd(proj, conv_full, g_conv_out, w_conv, d)
    o, cat, (g_up, g_down, wout_g, wgate_g, wproj_g) = _attn_fwd(
        proj, g_attn_out, cat, w_conv,
        [("gather_out", [s_up, s_down]), ("gather_pass", [g_out, g_gate, g_proj])])
    wout_f = wout_g.reshape(-1, wout_g.shape[-1])
    wgate_f = wgate_g.reshape(-1, wgate_g.shape[-1])
    h1, wup_g = _mm_nn("out_proj", cat, wout_f, epilogue=_ep_residual, extras=(xs,),
                       carry=[("gather_pass", [g_up])])
    mn = _rmsnorm_fwd("norm_mlp", h1, g_mlp)
    act, wdown_g = _mm_nn("mlp_up", mn, wup_g, n_shard=up_shard, epilogue=_ep_up, out_dtypes=(BF16,), tm=2048,
                          carry=[("gather_pass", [g_down])])
    wdown_f = wdown_g.reshape(-1, wdown_g.shape[-1])
    h2, = _mm_nn("mlp_down", act, wdown_f, epilogue=_ep_residual, extras=(h1,))
    n3 = _rmsnorm_fwd("norm_ple", h2, g_ple)
    gl, = _mm_nn("ple_gate", n3, wgate_f)
    pp = _ple_proj(ps, wproj_g)
    loss_part, dh3, dgl, dpp, dg_final = _ple_loss(h2, gl, pp, tgt, g_final.reshape(1, d))

    def slots(t2d):
        return t2d.reshape(N_DEV, -1, t2d.shape[-1])

    dw_proj = _d_ple_proj(ps, dpp, proj_shard)
    dw_gate = [slots(t) for t in _mm_tn("d_w_ple_gate", n3, dgl)]
    dh2, dh2b, dg_ple = _mm_nt_norm_bwd("d_norm_ple", dgl, wgate_f, h2, g_ple, dh3)
    du, gate_recv, proj_recv = _mm_nt("d_mlp_act", dh2b, wdown_f, epilogue=_ep_dact, out_dtypes=(BF16,),
                                      extras=(act,), tm=2048, carry=[("scatter", [dw_gate[1], dw_proj[1]])])
    dw_down = [slots(t) for t in _mm_tn("d_w_down", act, dh2b)]
    dw_up = _mm_tn("d_w_up", mn, du, n_shard=up_shard)
    dh1, dh1b, dg_mlp = _mm_nt_norm_bwd("d_norm_mlp", du, wup_g, h1, g_mlp, dh2, k_shard=up_shard, tm=1024)
    dcat, = _mm_nt("d_cat", dh1b, wout_f)
    dw_out = [slots(t) for t in _mm_tn("d_w_out", cat, dh1b)]
    dproj, dg_attn, (up_recv, down_recv) = _attn_bwd(proj, o, dcat, g_attn_out, w_conv,
                                                     [("scatter", [dw_up[1], dw_down[1]])])
    dproj, dconv, dg_conv = _conv_bwd(proj, dcat, conv_full, g_conv_out, dproj, w_conv)
    *dw_in, out_recv = _mm_tn("d_w_in", a, dproj, n_shard=in_shard, tn=in_shard,
                              carry=[("scatter", [dw_out[1]])])
    grad_x, _, dg_mix, in_recv = _mm_nt_norm_bwd("d_norm_mix", dproj, win_g, xs, g_mix, dh1, k_shard=in_shard,
                                                 tk=2 * in_shard, tm=1024, carry=[("scatter", [dw_in[1]])])

    names = ["w_in", "w_out", "w_up", "w_down", "w_ple_gate", "w_ple_proj"]
    owns = [dw_in[0], dw_out[0], dw_up[0], dw_down[0], dw_gate[0], dw_proj[0]]
    recvs = [in_recv, out_recv, up_recv, down_recv, gate_recv, proj_recv]
    moments = [(m_w_in, v_w_in), (m_w_out, v_w_out), (m_w_up, v_w_up), (m_w_down, v_w_down),
               (m_w_ple_gate, v_w_ple_gate), (m_w_ple_proj, v_w_ple_proj)]
    big_out = {}
    for n, own, rc, wt, (mm, vv) in zip(names, owns, recvs, big, moments):
        big_out[n] = [t[None] for t in _adam_sharded("adam_" + n, own, rc, wt, mm[0], vv[0], place)]

    n_conv_rows = CONV_K * w_conv // LANES
    small_g = jnp.concatenate(
        [_rows(dg_mix[0]), _rows(dg_conv[0]), _rows(dg_attn[0]), _rows(dg_mlp[0]), _rows(dg_ple[0]),
         _rows(dg_final[0]), _rows(dconv.reshape(-1)), loss_part], axis=0)
    n_gain_rows = small_g.shape[0] - n_conv_rows - 1
    pad_rows = (-small_g.shape[0]) % HALO
    small_g = _all_reduce_small(jnp.pad(small_g, ((0, pad_rows), (0, 0))))
    loss = small_g[n_gain_rows + n_conv_rows, 0]
    dconv_full = small_g[n_gain_rows:n_gain_rows + n_conv_rows].reshape(CONV_K, w_conv)
    dconv_mine = lax.dynamic_slice(dconv_full, (0, my_slot * cw), (CONV_K, cw))

    def pack(vecs, conv_part):
        rows = [_rows(t.reshape(-1)) for t in vecs]
        rows.append(jnp.pad(conv_part, ((0, HALO - CONV_K), (0, LANES - cw))))
        return jnp.concatenate(rows, axis=0)

    gains = [g_mix, g_conv_out, g_attn_out, g_mlp, g_ple, g_final]
    gains_m = [m_g_mix, m_g_conv_out, m_g_attn_out, m_g_mlp, m_g_ple, m_g_final]
    gains_v = [v_g_mix, v_g_conv_out, v_g_attn_out, v_g_mlp, v_g_ple, v_g_final]
    gpack = jnp.concatenate([small_g[:n_gain_rows], jnp.pad(dconv_mine, ((0, HALO - CONV_K), (0, LANES - cw)))], axis=0)
    sd, sm, sv = _adam_small(pack(gains, conv_w[0]), gpack, pack(gains_m, m_conv_w[0]), pack(gains_v, v_conv_w[0]))

    def unpack(packed):
        out, r0 = [], 0
        for t in gains:
            nr = t.size // LANES
            out.append(packed[r0:r0 + nr].reshape(t.shape))
            r0 += nr
        out.append(packed[r0:r0 + CONV_K, :cw][None])
        return out

    sg_l, sd_l, sm_l, sv_l = unpack(gpack), unpack(sd), unpack(sm), unpack(sv)
    small_names = ["g_mix", "g_conv_out", "g_attn_out", "g_mlp", "g_ple", "g_final", "conv_w"]
    small_out = {n: [sg_l[i], sd_l[i], sm_l[i], sv_l[i]] for i, n in enumerate(small_names)}

    order = ["g_mix", "w_in", "conv_w", "g_conv_out", "g_attn_out", "w_out", "g_mlp", "w_up", "w_down",
             "g_ple", "w_ple_gate", "w_ple_proj", "g_final"]
    table = {**big_out, **small_out}
    outs = [loss, grad_x[None]]
    for kind in range(4):
        outs.extend(table[n][kind] for n in order)
    return tuple(outs)
```

```python
import jax
import jax.numpy as jnp
from jax import lax
from jax.experimental import pallas as pl
from jax.experimental.pallas import tpu as pltpu

F32 = jnp.float32
BF16 = jnp.bfloat16
EPS = 1e-6
HEAD_DIM = 64
LANES = 128
CONV_K = 3
MXU_WIDTH = 256
ATTN_BLOCK = MXU_WIDTH
HALO = 8
N_DEV = 8
MESH = pl.DeviceIdType.MESH
VMEM_LIMIT = 56 * 1024 * 1024

ADAM_LR = 0.001
ADAM_B1 = 0.9
ADAM_B2 = 0.999
ADAM_EPS = 1e-08
ADAM_WD = 0.01
ADAM_STEP = 10


def _pcall(body, **kw):
    return pl.pallas_call(body, **kw)


def _params(sem=None, **kw):
    return pltpu.CompilerParams(dimension_semantics=sem, vmem_limit_bytes=VMEM_LIMIT, **kw)


def _tile(dim, pref):
    t = min(dim, pref)
    while dim % t:
        t -= LANES
    assert t > 0, (dim, pref)
    return t


_NN = (((1,), (0,)), ((), ()))
_NT = (((1,), (1,)), ((), ()))
_TN = (((0,), (0,)), ((), ()))


def _ep_store(acc, outs):
    outs[0][...] = acc.astype(outs[0].dtype)


def _ep_both(acc, outs):
    outs[0][...] = acc
    outs[1][...] = acc.astype(BF16)


def _ep_residual(acc, res, outs):
    outs[0][...] = acc + res[...]


def _ep_up(acc, outs):
    outs[0][...] = jnp.square(jnp.maximum(acc, 0.0)).astype(BF16)


def _ep_dact(acc, act, outs):
    outs[0][...] = (acc * (2.0 * jnp.sqrt(act[...].astype(F32)))).astype(BF16)


def _ep_norm_bwd(acc, h, g, dres, outs):
    @pl.when(pl.program_id(0) == 0)
    def _():
        outs[2][...] = jnp.zeros_like(outs[2])

    hv = h[...]
    r = lax.rsqrt(jnp.mean(hv * hv, axis=-1, keepdims=True) + EPS)
    hn = hv * r
    outs[2][...] += jnp.sum(acc * hn, axis=0, keepdims=True)
    dhn = acc * g[...]
    dh = dres[...] + r * (dhn - hn * jnp.mean(dhn * hn, axis=-1, keepdims=True))
    outs[0][...] = dh
    outs[1][...] = dh.astype(BF16)


def _matmul(name, a, b, *, dims, grid, a_spec, b_spec, acc_shape, out_shapes, out_specs,
            epilogue=_ep_store, extras=(), extra_specs=(), carry=(), sequential=False):
    nk = grid[2]
    plan = _Carried(carry)
    n_ex, n_out, n_xc = len(extras), len(out_shapes), len(plan.inputs)
    n_sems = len(plan.sems)
    last = tuple(g - 1 for g in grid)

    def product(a_ref, b_ref):
        if len(b_ref.shape) == 2:
            return lax.dot_general(a_ref[...].astype(BF16), b_ref[...].astype(BF16), dims,
                                   preferred_element_type=F32)
        width = b_ref.shape[2]
        return sum(lax.dot_general(a_ref[:, g * width:(g + 1) * width].astype(BF16), b_ref[g].astype(BF16), dims,
                                   preferred_element_type=F32) for g in range(b_ref.shape[0]))

    def body(a_ref, b_ref, *rest):
        ex, rest = rest[:n_ex], rest[n_ex:]
        partials, rest = rest[:n_xc], rest[n_xc:]
        outs, rest = rest[:n_out], rest[n_out:]
        received, rest = rest[:n_xc], rest[n_xc:]
        ids = [pl.program_id(axis) for axis in range(3)]
        if n_xc:
            @pl.when((ids[0] == 0) & (ids[1] == 0) & (ids[2] == 0))
            def _():
                for cp in plan.copies(partials, received, rest[-n_sems:]):
                    cp.start()

        if nk == 1 and not sequential and dims != _TN:
            for n0 in range(0, acc_shape[1], MXU_WIDTH):
                cols = slice(n0, min(n0 + MXU_WIDTH, acc_shape[1]))
                b_cols = b_ref.at[cols, :] if dims == _NT else b_ref.at[:, cols]
                for m0 in range(0, acc_shape[0], MXU_WIDTH):
                    rows = slice(m0, min(m0 + MXU_WIDTH, acc_shape[0]))
                    epilogue(product(a_ref.at[rows, :], b_cols), *[e.at[rows, cols] for e in ex],
                             [o.at[rows, cols] for o in outs])
        elif nk == 1:
            epilogue(product(a_ref, b_ref), *ex, outs)
        else:
            acc = rest[0]

            @pl.when(ids[2] == 0)
            def _():
                acc[...] = product(a_ref, b_ref)

            @pl.when(ids[2] > 0)
            def _():
                acc[...] += product(a_ref, b_ref)

            @pl.when(ids[2] == nk - 1)
            def _():
                epilogue(acc[...], *ex, outs)

        if n_xc:
            @pl.when((ids[0] == last[0]) & (ids[1] == last[1]) & (ids[2] == last[2]))
            def _():
                for cp in plan.copies(partials, received, rest[-n_sems:]):
                    cp.wait()

    anywhere = [pl.BlockSpec(memory_space=pl.ANY)] * n_xc
    return _pcall(
        body, name=name, grid=grid,
        in_specs=[a_spec, b_spec, *extra_specs, *anywhere],
        out_specs=[*out_specs, *anywhere],
        out_shape=[*out_shapes, *plan.out_shapes],
        scratch_shapes=([] if nk == 1 else [pltpu.VMEM(acc_shape, F32)]) + plan.sems,
        input_output_aliases=plan.aliases(2 + n_ex, n_out),
        compiler_params=_params(("arbitrary",) * 3 if n_xc or sequential else ("parallel", "parallel", "arbitrary")),
    )(a, b, *extras, *plan.inputs)


_NO_CARRY = ()


def _mm_nn(name, a, w, *, n_shard=None, epilogue=_ep_store, out_dtypes=(F32,), extras=(), carry=_NO_CARRY,
           tm=1024, tn=1024, tk=1024):
    m, kd = a.shape
    if n_shard is None:
        n = w.shape[1]
        tn = _tile(n, tn)
        tk = _tile(kd, tk)
        b_spec = pl.BlockSpec((tk, tn), lambda i, j, k: (k, j))
    else:
        n = N_DEV * n_shard
        tn = _tile(n_shard, tn)
        tk = _tile(kd, tk)
        per = n_shard // tn
        b_spec = pl.BlockSpec((None, tk, tn), lambda i, j, k: (j // per, k, j % per))
    tm = _tile(m, tm)
    o_spec = pl.BlockSpec((tm, tn), lambda i, j, k: (i, j))
    return _matmul(
        name, a, w, dims=_NN, grid=(m // tm, n // tn, kd // tk),
        a_spec=pl.BlockSpec((tm, tk), lambda i, j, k: (i, k)), b_spec=b_spec,
        acc_shape=(tm, tn),
        out_shapes=[jax.ShapeDtypeStruct((m, n), d) for d in out_dtypes],
        out_specs=[o_spec] * len(out_dtypes),
        epilogue=epilogue, extras=extras, extra_specs=[o_spec] * len(extras), carry=carry)


def _mm_nt(name, a, w, *, k_shard=None, epilogue=_ep_store, out_dtypes=(F32,), extras=(), carry=_NO_CARRY,
           tm=1024, tn=1024, tk=1024):
    m, kd = a.shape
    if k_shard is None:
        n = w.shape[0]
        tn = _tile(n, tn)
        tk = _tile(kd, tk)
        b_spec = pl.BlockSpec((tn, tk), lambda i, j, k: (j, k))
    else:
        n = w.shape[1]
        tn = _tile(n, tn)
        tk = _tile(k_shard, tk)
        per = k_shard // tk
        b_spec = pl.BlockSpec((None, tn, tk), lambda i, j, k: (k // per, j, k % per))
    tm = _tile(m, tm)
    o_spec = pl.BlockSpec((tm, tn), lambda i, j, k: (i, j))
    return _matmul(
        name, a, w, dims=_NT, grid=(m // tm, n // tn, kd // tk),
        a_spec=pl.BlockSpec((tm, tk), lambda i, j, k: (i, k)), b_spec=b_spec,
        acc_shape=(tm, tn),
        out_shapes=[jax.ShapeDtypeStruct((m, n), d) for d in out_dtypes],
        out_specs=[o_spec] * len(out_dtypes),
        epilogue=epilogue, extras=extras, extra_specs=[o_spec] * len(extras), carry=carry)


def _mm_nt_norm_bwd(name, a, w, h, g, dres, *, k_shard=None, carry=_NO_CARRY, tm=512, tk=1024):
    m, kd = a.shape
    n = h.shape[1]
    if k_shard is None:
        tk = _tile(kd, tk)
        b_spec = pl.BlockSpec((n, tk), lambda i, j, k: (0, k))
    else:
        group = max(1, min(tk // k_shard, N_DEV))
        while N_DEV % group:
            group -= 1
        tk = group * k_shard
        b_spec = pl.BlockSpec((group, n, k_shard), lambda i, j, k: (k, 0, 0))
    tm = _tile(m, tm)
    rows = pl.BlockSpec((tm, n), lambda i, j, k: (i, 0))
    vec = pl.BlockSpec((1, n), lambda i, j, k: (0, 0))
    return _matmul(
        name, a, w, dims=_NT, grid=(m // tm, 1, kd // tk),
        a_spec=pl.BlockSpec((tm, tk), lambda i, j, k: (i, k)), b_spec=b_spec, acc_shape=(tm, n),
        out_shapes=[jax.ShapeDtypeStruct((m, n), F32), jax.ShapeDtypeStruct((m, n), BF16),
                    jax.ShapeDtypeStruct((1, n), F32)],
        out_specs=[rows, rows, vec], epilogue=_ep_norm_bwd,
        extras=(h, g, dres), extra_specs=[rows, vec, rows], carry=carry, sequential=True)


TN_TILE_BYTES = 40 * 1024 * 1024


def _mm_tn(name, a, b, *, n_shard=None, carry=_NO_CARRY, tm=1024, tn=1024):
    t, m = a.shape
    n = b.shape[1]
    tm = _tile(m, tm)
    tn = _tile(n if n_shard is None else n_shard, tn)
    tk = t
    while 2 * 2 * tk * (tm + tn) + 4 * tm * tn * 5 > TN_TILE_BYTES and tk % (2 * LANES) == 0:
        tk //= 2
    if n_shard is None:
        o_spec = pl.BlockSpec((tm, tn), lambda i, j, k: (i, j))
        shape = (m, n)
    else:
        per = n_shard // tn
        o_spec = pl.BlockSpec((None, tm, tn), lambda i, j, k: (j // per, i, j % per))
        shape = (N_DEV, m, n_shard)
    return _matmul(
        name, a, b, dims=_TN, grid=(m // tm, n // tn, t // tk),
        a_spec=pl.BlockSpec((tk, tm), lambda i, j, k: (k, i)),
        b_spec=pl.BlockSpec((tk, tn), lambda i, j, k: (k, j)),
        acc_shape=(tm, tn), epilogue=_ep_both, carry=carry,
        out_shapes=[jax.ShapeDtypeStruct(shape, F32), jax.ShapeDtypeStruct(shape, BF16)],
        out_specs=[o_spec, o_spec])


def _ple_proj(p, w_g, tm=1024):
    s, kd = p.shape
    ns = w_g.shape[2]
    tm = _tile(s, tm)

    def body(p_ref, w_ref, o_ref):
        pv = p_ref[...].astype(BF16)
        for j in range(N_DEV):
            o_ref[:, j * ns:(j + 1) * ns] = jnp.dot(pv, w_ref[j], preferred_element_type=F32)

    return _pcall(body, name="ple_proj", grid=(s // tm,),
                  in_specs=[pl.BlockSpec((tm, kd), lambda i: (i, 0)),
                            pl.BlockSpec((N_DEV, kd, ns), lambda i: (0, 0, 0))],
                  out_specs=pl.BlockSpec((tm, N_DEV * ns), lambda i: (i, 0)),
                  out_shape=jax.ShapeDtypeStruct((s, N_DEV * ns), F32),
                  compiler_params=_params(("parallel",)))(p, w_g)


def _d_ple_proj(p, dpp, ns, tk=1024):
    s, kd = p.shape
    tk = _tile(s, tk)
    nk = s // tk

    def body(p_ref, d_ref, of_ref, ob_ref, acc):
        k = pl.program_id(0)

        @pl.when(k == 0)
        def _():
            acc[...] = jnp.zeros_like(acc)

        pv = p_ref[...].astype(BF16)
        for j in range(N_DEV):
            acc[j] += lax.dot_general(pv, d_ref[:, j * ns:(j + 1) * ns], _TN, preferred_element_type=F32)

        @pl.when(k == nk - 1)
        def _():
            of_ref[...] = acc[...]
            ob_ref[...] = acc[...].astype(BF16)

    whole = pl.BlockSpec((N_DEV, kd, ns), lambda k: (0, 0, 0))
    return _pcall(body, name="d_w_ple_proj", grid=(nk,),
                  in_specs=[pl.BlockSpec((tk, kd), lambda k: (k, 0)),
                            pl.BlockSpec((tk, N_DEV * ns), lambda k: (k, 0))],
                  out_specs=[whole, whole],
                  out_shape=[jax.ShapeDtypeStruct((N_DEV, kd, ns), F32), jax.ShapeDtypeStruct((N_DEV, kd, ns), BF16)],
                  scratch_shapes=[pltpu.VMEM((N_DEV, kd, ns), F32)],
                  compiler_params=_params(("arbitrary",)))(p, dpp)


def _row_spec(tr, d):
    return pl.BlockSpec((tr, d), lambda i: (i, 0))


def _vec_spec(d):
    return pl.BlockSpec((1, d), lambda i: (0, 0))


def _rmsnorm_fwd(name, x, g, tr=1024):
    s, d = x.shape
    tr = _tile(s, tr)

    def body(x_ref, g_ref, o_ref):
        xv = x_ref[...]
        r = lax.rsqrt(jnp.mean(xv * xv, axis=-1, keepdims=True) + EPS)
        o_ref[...] = (xv * r * g_ref[...]).astype(BF16)

    return _pcall(body, name=name, grid=(s // tr,),
                  in_specs=[_row_spec(tr, d), _vec_spec(d)], out_specs=_row_spec(tr, d),
                  out_shape=jax.ShapeDtypeStruct((s, d), BF16),
                  compiler_params=_params(("parallel",)))(x, g)


def _ple_loss(h2, gl, pp, tgt, g_final, tr=512):
    s, d = h2.shape
    tr = _tile(s, tr)

    def body(h2_ref, gl_ref, pp_ref, t_ref, g_ref, loss_ref, dh3_ref, dgl_ref, dpp_ref, dg_ref):
        @pl.when(pl.program_id(0) == 0)
        def _():
            dg_ref[...] = jnp.zeros_like(dg_ref)
            loss_ref[...] = jnp.zeros_like(loss_ref)

        gate = jax.nn.sigmoid(gl_ref[...])
        ppv = pp_ref[...]
        h3 = h2_ref[...] + gate * ppv
        r = lax.rsqrt(jnp.mean(h3 * h3, axis=-1, keepdims=True) + EPS)
        hn = h3 * r
        gv = g_ref[...]
        diff = hn * gv - t_ref[...]
        row = jnp.mean(diff * diff, axis=-1, keepdims=True)
        loss_ref[...] += 0.5 * jnp.sum(row, axis=0, keepdims=True)
        dy = diff * (1.0 / d)
        dg_ref[...] += jnp.sum(dy * hn, axis=0, keepdims=True)
        dhn = dy * gv
        dh3 = r * (dhn - hn * jnp.mean(dhn * hn, axis=-1, keepdims=True))
        dh3_ref[...] = dh3
        dgl_ref[...] = (dh3 * ppv * gate * (1.0 - gate)).astype(BF16)
        dpp_ref[...] = (dh3 * gate).astype(BF16)

    return _pcall(body, name="ple_loss", grid=(s // tr,),
                  in_specs=[_row_spec(tr, d)] * 4 + [_vec_spec(d)],
                  out_specs=[_vec_spec(LANES), _row_spec(tr, d), _row_spec(tr, d), _row_spec(tr, d), _vec_spec(d)],
                  out_shape=[jax.ShapeDtypeStruct((1, LANES), F32), jax.ShapeDtypeStruct((s, d), F32),
                             jax.ShapeDtypeStruct((s, d), BF16), jax.ShapeDtypeStruct((s, d), BF16),
                             jax.ShapeDtypeStruct((1, d), F32)],
                  compiler_params=_params(("arbitrary",)))(h2, gl, pp, tgt, g_final)


def _low_half():
    return lax.broadcasted_iota(jnp.int32, (1, LANES), 1) < HEAD_DIM


def _half_mean(v, low):
    s_lo = jnp.sum(jnp.where(low, v, 0.0), axis=-1, keepdims=True)
    s_hi = jnp.sum(jnp.where(low, 0.0, v), axis=-1, keepdims=True)
    return jnp.where(low, s_lo, s_hi) * (1.0 / HEAD_DIM)


def _head_norm_bwd(val, dout, g, low):
    r = lax.rsqrt(_half_mean(val * val, low) + EPS)
    vn = val * r
    dvn = dout * g
    return r * (dvn - vn * _half_mean(dvn * vn, low)), dout * vn


def _conv_taps(vv_ext, w_ref, rows):
    v0 = vv_ext[HALO:]
    v1 = pltpu.roll(vv_ext, 1, 0)[HALO:]
    v2 = pltpu.roll(vv_ext, 2, 0)[HALO:]
    del rows
    return w_ref[2:3, :] * v0 + w_ref[1:2, :] * v1 + w_ref[0:1, :] * v2, (v0, v1, v2)


def _conv_fwd(proj, conv_w, g_conv, w_conv, d_model, tr=1024):
    s = proj.shape[0]
    tr = _tile(s, tr)
    hb = tr // HALO

    def main(part):
        return pl.BlockSpec((tr, w_conv), lambda i: (i, part))

    def prev(part):
        return pl.BlockSpec((HALO, w_conv), lambda i: (jnp.maximum(i * hb - 1, 0), part))

    def body(cb_ref, cc_ref, cu_ref, ccp_ref, cup_ref, w_ref, g_ref, o_ref):
        i = pl.program_id(0)
        low = _low_half()
        for j in range(w_conv // LANES):
            cols = slice(j * LANES, (j + 1) * LANES)
            vv_prev = jnp.where(i > 0, ccp_ref[:, cols] * cup_ref[:, cols], 0.0)
            vv_ext = jnp.concatenate([vv_prev, cc_ref[:, cols] * cu_ref[:, cols]], axis=0)
            y, _ = _conv_taps(vv_ext, w_ref.at[:, cols], tr)
            co = cb_ref[:, cols] * y
            r = lax.rsqrt(_half_mean(co * co, low) + EPS)
            o_ref[:, cols] = (co * r * g_ref[:, cols]).astype(BF16)

    return _pcall(
        body, name="conv_fwd", grid=(s // tr,),
        in_specs=[main(0), main(1), main(2), prev(1), prev(2),
                  pl.BlockSpec((CONV_K, w_conv), lambda i: (0, 0)),
                  pl.BlockSpec((1, w_conv), lambda i: (0, 0))],
        out_specs=pl.BlockSpec((tr, w_conv), lambda i: (i, 0)),
        out_shape=jax.ShapeDtypeStruct((s, d_model), BF16),
        compiler_params=_params(("parallel",)),
    )(proj, proj, proj, proj, proj, conv_w, g_conv)


def _conv_bwd(proj, dcat, conv_w, g_conv, dproj, w_conv, tr=1024):
    s = proj.shape[0]
    tr = _tile(s, tr)
    hb = tr // HALO
    last = s // HALO - 1
    nt = s // tr

    def main(part):
        return pl.BlockSpec((tr, w_conv), lambda i: (i, part))

    def prev(part):
        return pl.BlockSpec((HALO, w_conv), lambda i: (jnp.maximum(i * hb - 1, 0), part))

    def nxt(part):
        return pl.BlockSpec((HALO, w_conv), lambda i: (jnp.minimum((i + 1) * hb, last), part))

    def body(cb_ref, cc_ref, cu_ref, dc_ref, ccp_ref, cup_ref, cbn_ref, ccn_ref, cun_ref, dcn_ref,
             w_ref, g_ref, dproj_in, dproj_ref, dw_ref, dg_ref):
        del dproj_in
        i = pl.program_id(0)

        @pl.when(i == 0)
        def _():
            dw_ref[...] = jnp.zeros_like(dw_ref)
            dg_ref[...] = jnp.zeros_like(dg_ref)

        low = _low_half()
        n_ext = tr + HALO
        rowid = lax.broadcasted_iota(jnp.int32, (n_ext, 1), 0)
        for j in range(w_conv // LANES):
            cols = slice(j * LANES, (j + 1) * LANES)
            wj = w_ref.at[:, cols]
            cc, cu = cc_ref[:, cols], cu_ref[:, cols]
            vv_prev = jnp.where(i > 0, ccp_ref[:, cols] * cup_ref[:, cols], 0.0)
            vv_ext = jnp.concatenate([vv_prev, cc * cu, ccn_ref[:, cols] * cun_ref[:, cols]], axis=0)
            y_ext, (v0, v1, v2) = _conv_taps(vv_ext, wj, n_ext)
            cb_ext = jnp.concatenate([cb_ref[:, cols], cbn_ref[:, cols]], axis=0)
            dc_ext = jnp.concatenate([dc_ref[:, cols], dcn_ref[:, cols]], axis=0)
            dco, dgn = _head_norm_bwd(cb_ext * y_ext, dc_ext, g_ref[:, cols], low)
            dyc = jnp.where((rowid < tr) | (i < nt - 1), dco * cb_ext, 0.0)
            dvv = (wj[2:3, :] * dyc[:tr] + wj[1:2, :] * pltpu.roll(dyc, n_ext - 1, 0)[:tr]
                   + wj[0:1, :] * pltpu.roll(dyc, n_ext - 2, 0)[:tr])
            dproj_ref[:, cols] = (dco[:tr] * y_ext[:tr]).astype(BF16)
            dproj_ref[:, w_conv + j * LANES:w_conv + (j + 1) * LANES] = (dvv * cu).astype(BF16)
            dproj_ref[:, 2 * w_conv + j * LANES:2 * w_conv + (j + 1) * LANES] = (dvv * cc).astype(BF16)
            dyt = dyc[:tr]
            for tap, shifted in enumerate((v2, v1, v0)):
                dw_ref[tap:tap + 1, cols] += jnp.sum(dyt * shifted[:tr], axis=0, keepdims=True)
            dg_ref[:, cols] += jnp.sum(dgn[:tr], axis=0, keepdims=True)

    n_cols = dproj.shape[1]
    return _pcall(
        body, name="conv_bwd", grid=(nt,),
        in_specs=[main(0), main(1), main(2), main(0),
                  prev(1), prev(2), nxt(0), nxt(1), nxt(2), nxt(0),
                  pl.BlockSpec((CONV_K, w_conv), lambda i: (0, 0)),
                  pl.BlockSpec((1, w_conv), lambda i: (0, 0)),
                  pl.BlockSpec(memory_space=pl.ANY)],
        out_specs=[pl.BlockSpec((tr, 3 * w_conv), lambda i: (i, 0)),
                   pl.BlockSpec((CONV_K, w_conv), lambda i: (0, 0)),
                   pl.BlockSpec((1, w_conv), lambda i: (0, 0))],
        out_shape=[jax.ShapeDtypeStruct((s, n_cols), BF16),
                   jax.ShapeDtypeStruct((CONV_K, w_conv), F32),
                   jax.ShapeDtypeStruct((1, w_conv), F32)],
        input_output_aliases={12: 0},
        compiler_params=_params(("arbitrary",)),
    )(proj, proj, proj, dcat, proj, proj, proj, proj, proj, dcat, conv_w, g_conv, dproj)


STRIP = 16

ALL_CHAINS = (0, 1, 2, 3)
UPPER_CHAINS = (2, 3)


RUN_FLOOR = -104.0


def _any_weight_left(run_s):
    return (jnp.max(run_s[...]) > RUN_FLOOR).astype(jnp.int32)


def _chains(low):
    return [(2 * half + h, half, msk) for half in range(2)
            for h, msk in enumerate((low, jnp.logical_not(low)))]


def _suffix_operator(t):
    r = lax.broadcasted_iota(jnp.int32, (2 * t, t), 0)
    c = lax.broadcasted_iota(jnp.int32, (2 * t, t), 1)
    return jnp.where((r > c) & ((r < t) | (r - t > c)), 1.0, 0.0).astype(BF16)


def _strips(t, diag):
    return [(i, slice(i * STRIP, (i + 1) * STRIP), t // 2 if diag and (i + 1) * STRIP <= t // 2 else t)
            for i in range(t // STRIP)]


def _strip_mask(i, w):
    r = lax.broadcasted_iota(jnp.int32, (STRIP, w), 0) + i * STRIP
    c = lax.broadcasted_iota(jnp.int32, (STRIP, w), 1)
    return r > c


def _store_trimmed(ref, rows, val, w, t, at=0):
    ref[rows, at:at + w] = val
    if w < t:
        ref[rows, at + w:at + t] = jnp.zeros((STRIP, t - w), val.dtype)


def _store_split(ref, rows, val, w, t):
    hi = val.astype(BF16)
    _store_trimmed(ref, rows, hi, w, t)
    _store_trimmed(ref, rows, (val - hi.astype(F32)).astype(BF16), w, t, at=t)


def _sb_scores(z_s, split_s, zl_s, tot_s, keep_s, t, diag):
    for i, rows, w in _strips(t, diag):
        z = z_s[rows, :w]
        log_beta = jnp.minimum(z, 0.0) - jnp.log(1.0 + jnp.exp(-jnp.abs(z)))
        log_keep = log_beta - z
        if diag:
            log_keep = jnp.where(_strip_mask(i, w), log_keep, 0.0)
        _store_split(split_s, rows, log_keep, w, t)
        zl_s[rows, :w] = log_beta
        tot_s[rows, :] = _row_sum(log_keep)
        if keep_s is not None:
            keep_s[rows, :w] = jnp.exp(log_keep)


def _row_sum(v):
    return jnp.broadcast_to(jnp.sum(v, axis=-1, keepdims=True), (v.shape[0], LANES))


def _wide(r, t):
    return jnp.concatenate([r] * (t // LANES), axis=1)


def _sb_weights(zl_s, suf_s, run_s, tot_s, a_s, t, diag, da_s=None, glog_s=None, gsplit_s=None, gtot_s=None):
    for i, rows, w in _strips(t, diag):
        run = run_s[rows, :]
        a = jnp.exp(zl_s[rows, :w] + suf_s[rows, :w] + _wide(run, w))
        if diag:
            a = jnp.where(_strip_mask(i, w), a, 0.0)
        ab = a.astype(BF16)
        _store_trimmed(a_s, rows, ab, w, t)
        run_s[rows, :] = run + tot_s[rows, :]
        if da_s is not None:
            glog = ab.astype(F32) * da_s[rows, :w]
            glog_s[rows, :w] = glog
            _store_split(gsplit_s, rows, glog, w, t)
            gtot_s[rows, :] = _row_sum(glog)


def _sb_dscores(glog_s, cum_s, rest_s, gtot_s, keep_s, dz_s, t, diag):
    for i, rows, w in _strips(t, diag):
        glog = glog_s[rows, :w]
        rest = rest_s[rows, :]
        from_here = _wide(rest, w) - cum_s[rows, :w]
        before = from_here - glog
        dz = from_here * keep_s[rows, :w] - before
        if diag:
            dz = jnp.where(_strip_mask(i, w), dz, 0.0)
        _store_trimmed(dz_s, rows, dz.astype(BF16), w, t)
        rest_s[rows, :] = rest - gtot_s[rows, :]


def _attn_fwd(proj, g_attn, cat, w_conv, carry, t=ATTN_BLOCK):
    s = proj.shape[0]
    w_attn = g_attn.shape[1]
    nh = w_attn // LANES
    t = _tile(s, t)
    tq = 2 * t
    nq = s // tq
    q0 = 3 * w_conv // LANES
    scale = HEAD_DIM ** -0.5
    plan = _Carried(carry)
    nw = len(plan.inputs)

    def body(q_ref, k_ref, v_ref, g_ref, cat_in, *rest):
        staged_refs, rest = rest[:nw], rest[nw:]
        o_ref, cat_ref = rest[:2]
        gathered_refs, rest = rest[2:2 + nw], rest[2 + nw:]
        kb, vb, tri_s, qm_s, z_s, split_s, zl_s, suf_s, a_s, run_s, tot_s, acc_s = rest[:12]
        gather_sems = rest[12:]
        del cat_in
        qi = pl.program_id(1)

        @pl.when((pl.program_id(0) == 0) & (qi == 0))
        def _():
            for cp in plan.copies(staged_refs, gathered_refs, gather_sems):
                cp.start()

        @pl.when(qi == 0)
        def _():
            kb[...] = k_ref[...].astype(BF16)
            vb[...] = v_ref[...].astype(BF16)
            tri_s[...] = _suffix_operator(t)

        low = _low_half()
        for c, half, msk in _chains(low):
            qm_s[c] = jnp.where(msk, q_ref[half * t:(half + 1) * t, :] * scale, 0.0).astype(BF16)
            run_s[c] = jnp.zeros((t, LANES), F32)
            acc_s[c] = jnp.zeros((t, LANES), F32)

        def key_rows(kblk):
            return pl.ds(pl.multiple_of(kblk * t, t), t)

        def key_block(base, c):
            return key_rows(jnp.maximum(base + c // 2, 0))

        def scores_matmul(base, chains):
            for c in chains:
                z_s[c] = lax.dot_general(qm_s[c], kb[key_block(base, c), :], _NT, preferred_element_type=F32)

        def front(modes, base, prev=None):
            for c, diag in modes:
                _sb_scores(z_s.at[c], split_s.at[c], zl_s.at[c], tot_s.at[c], None, t, diag)
                suf_s[c] = jnp.dot(split_s[c], tri_s[...], preferred_element_type=F32)
            if prev is not None:
                tail(*prev)
            scores_matmul(base - 1, ALL_CHAINS)
            for c, diag in modes:
                _sb_weights(zl_s.at[c], suf_s.at[c], run_s.at[c], tot_s.at[c], a_s.at[c], t, diag)

        def tail(base, chains):
            for c in chains:
                acc_s[c] += jnp.dot(a_s[c], vb[key_block(base, c), :], preferred_element_type=F32)

        first = 2 * qi
        scores_matmul(first, ALL_CHAINS)
        front([(c, True) for c in ALL_CHAINS], first)

        def loop(state):
            it = state[0]
            base = first - 1 - it
            front([(c, False) for c in ALL_CHAINS], base, prev=(base + 1, ALL_CHAINS))
            return it + 1, _any_weight_left(run_s)

        done, live = lax.while_loop(lambda state: (state[0] < first) & (state[1] > 0), loop,
                                    (jnp.int32(0), jnp.int32(1)))
        one_more = (done == first) & (live > 0)

        @pl.when(one_more)
        def _():
            front([(c, False) for c in UPPER_CHAINS], -1, prev=(0, ALL_CHAINS))
            tail(-1, UPPER_CHAINS)

        @pl.when(jnp.logical_not(one_more))
        def _():
            tail(first - done, ALL_CHAINS)

        for half in range(2):
            rows = slice(half * t, (half + 1) * t)
            o = jnp.where(low, acc_s[2 * half], acc_s[2 * half + 1])
            o_ref[rows, :] = o
            r = lax.rsqrt(_half_mean(o * o, low) + EPS)
            cat_ref[rows, :] = (o * r * g_ref[...]).astype(BF16)

        @pl.when((pl.program_id(0) == nh - 1) & (qi == nq - 1))
        def _():
            for cp in plan.copies(staged_refs, gathered_refs, gather_sems):
                cp.wait()

    whole = lambda col0: pl.BlockSpec((s, LANES), lambda h, i: (0, col0 + h))
    n_ch = len(ALL_CHAINS)
    res = _pcall(
        body, name="attn_fwd", grid=(nh, nq),
        in_specs=[pl.BlockSpec((tq, LANES), lambda h, i: (i, q0 + h)),
                  whole(q0 + nh), whole(q0 + 2 * nh),
                  pl.BlockSpec((1, LANES), lambda h, i: (0, h)),
                  pl.BlockSpec(memory_space=pl.ANY)] + [pl.BlockSpec(memory_space=pl.ANY)] * nw,
        out_specs=[pl.BlockSpec((tq, LANES), lambda h, i: (i, h)),
                   pl.BlockSpec((tq, LANES), lambda h, i: (i, w_conv // LANES + h))]
        + [pl.BlockSpec(memory_space=pl.ANY)] * nw,
        out_shape=[jax.ShapeDtypeStruct((s, w_attn), F32),
                   jax.ShapeDtypeStruct(cat.shape, BF16)] + plan.out_shapes,
        scratch_shapes=[pltpu.VMEM((s, LANES), BF16), pltpu.VMEM((s, LANES), BF16),
                        pltpu.VMEM((2 * t, t), BF16),
                        pltpu.VMEM((n_ch, t, LANES), BF16),
                        pltpu.VMEM((n_ch, t, t), F32),
                        pltpu.VMEM((n_ch, t, 2 * t), BF16),
                        pltpu.VMEM((n_ch, t, t), F32),
                        pltpu.VMEM((n_ch, t, t), F32),
                        pltpu.VMEM((n_ch, t, t), BF16),
                        pltpu.VMEM((n_ch, t, LANES), F32),
                        pltpu.VMEM((n_ch, t, LANES), F32),
                        pltpu.VMEM((n_ch, t, LANES), F32)]
        + plan.sems,
        input_output_aliases={4: 1, **plan.aliases(5, 2)},
        compiler_params=_params(("arbitrary", "arbitrary")),
    )(proj, proj, proj, g_attn, cat, *plan.inputs)
    return res[0], res[1], res[2:]


def _attn_bwd(proj, o, dcat, g_attn, w_conv, carry, t=ATTN_BLOCK):
    s, n_cols = proj.shape
    w_attn = g_attn.shape[1]
    nh = w_attn // LANES
    t = _tile(s, t)
    tq = 2 * t
    nq = s // tq
    q0 = 3 * w_conv // LANES
    scale = HEAD_DIM ** -0.5
    plan = _Carried(carry)
    nw = len(plan.inputs)

    def body(q_ref, k_ref, v_ref, o_ref, do_ref, g_ref, *rest):
        partial_refs, rest = rest[:nw], rest[nw:]
        dproj_ref, dg_ref = rest[:2]
        received_refs, rest = rest[2:2 + nw], rest[2 + nw:]
        (kb, vb, dkt_acc, dvt_acc, stash, tri_s, qm_s, dom_s, qt_s, dot_s, z_s, da_s, split_s, zl_s,
         keep_s, suf_s, a_s, glog_s, gsplit_s, cum_s, dz_s, run_s, tot_s, rest_s, gtot_s, dq_s) = rest[:26]
        out_sems, scatter_sems = rest[26], rest[27:]
        step_i = pl.program_id(1)
        qi = nq - 1 - step_i
        head_pair = pl.program_id(0)
        first_step = (head_pair == 0) & (step_i == 0)
        last_step = (head_pair == nh - 1) & (step_i == nq - 1)

        @pl.when(first_step)
        def _():
            for cp in plan.copies(partial_refs, received_refs, scatter_sems):
                cp.start()

        def out_copies():
            rows = pl.ds(pl.multiple_of(qi * tq, tq), tq)
            return [pltpu.make_async_copy(
                stash.at[w], dproj_ref.at[rows, pl.ds(pl.multiple_of((q0 + w * nh + head_pair) * LANES, LANES), LANES)],
                out_sems.at[w]) for w in range(3)]

        def walk():
            @pl.when(step_i == 0)
            def _():
                kb[...] = k_ref[...].astype(BF16)
                vb[...] = v_ref[...].astype(BF16)
                tri_s[...] = _suffix_operator(t)
                dkt_acc[...] = jnp.zeros_like(dkt_acc)
                dvt_acc[...] = jnp.zeros_like(dvt_acc)
                dg_ref[...] = jnp.zeros_like(dg_ref)

            low = _low_half()
            gv = g_ref[...]
            for half in range(2):
                rows = slice(half * t, (half + 1) * t)
                q = q_ref[rows, :] * scale
                ov = o_ref[rows, :]
                d_o, dgn = _head_norm_bwd(ov, do_ref[rows, :], gv, low)
                dg_ref[...] += jnp.sum(dgn, axis=0, keepdims=True)
                for h, msk in enumerate((low, jnp.logical_not(low))):
                    c = 2 * half + h
                    qh = jnp.where(msk, q, 0.0)
                    doh = jnp.where(msk, d_o, 0.0)
                    dom = doh.astype(BF16)
                    qm_s[c] = qh.astype(BF16)
                    dom_s[c] = dom
                    qt_s[c] = qh.T.astype(BF16)
                    dot_s[c] = doh.T.astype(BF16)
                    rest_s[c] = _row_sum(dom.astype(F32) * ov)
                    run_s[c] = jnp.zeros((t, LANES), F32)
                    dq_s[c] = jnp.zeros((t, LANES), F32)

            def key_rows(kblk):
                return pl.ds(pl.multiple_of(kblk * t, t), t)

            def block_of(base, half):
                return jnp.maximum(base + half, 0)

            def scores_matmul(base, chains):
                for c in chains:
                    ks = kb[key_rows(block_of(base, c // 2)), :]
                    z_s[c] = lax.dot_general(qm_s[c], ks, _NT, preferred_element_type=F32)

            def da_matmul(base, chains):
                for c in chains:
                    vs = vb[key_rows(block_of(base, c // 2)), :]
                    da_s[c] = lax.dot_general(dom_s[c], vs, _NT, preferred_element_type=F32)

            def front(modes, base, prev=None):
                if prev is not None:
                    tail(*prev)
                for c, diag in modes:
                    _sb_scores(z_s.at[c], split_s.at[c], zl_s.at[c], tot_s.at[c], keep_s.at[c], t, diag)
                    suf_s[c] = jnp.dot(split_s[c], tri_s[...], preferred_element_type=F32)
                scores_matmul(base - 1, ALL_CHAINS)
                for c, diag in modes:
                    _sb_weights(zl_s.at[c], suf_s.at[c], run_s.at[c], tot_s.at[c], a_s.at[c], t, diag,
                                da_s.at[c], glog_s.at[c], gsplit_s.at[c], gtot_s.at[c])
                    cum_s[c] = jnp.dot(gsplit_s[c], tri_s[...], preferred_element_type=F32)
                da_matmul(base - 1, ALL_CHAINS)
                for c, diag in modes:
                    _sb_dscores(glog_s.at[c], cum_s.at[c], rest_s.at[c], gtot_s.at[c], keep_s.at[c],
                                dz_s.at[c], t, diag)

            def tail(base, chains):
                for half in range(2):
                    mine = [c for c in chains if c // 2 == half]
                    if not mine:
                        continue
                    kblk = block_of(base, half)
                    ks = kb[key_rows(kblk), :]
                    dkt = dkt_acc[kblk]
                    dvt = dvt_acc[kblk]
                    for c in mine:
                        dq_s[c] += jnp.dot(dz_s[c], ks, preferred_element_type=F32)
                        dkt = dkt + jnp.dot(qt_s[c], dz_s[c], preferred_element_type=F32)
                        dvt = dvt + jnp.dot(dot_s[c], a_s[c], preferred_element_type=F32)
                    dkt_acc[kblk] = dkt
                    dvt_acc[kblk] = dvt

            first = 2 * qi
            scores_matmul(first, ALL_CHAINS)
            da_matmul(first, ALL_CHAINS)
            front([(c, True) for c in ALL_CHAINS], first)

            def loop(state):
                it = state[0]
                base = first - 1 - it
                front([(c, False) for c in ALL_CHAINS], base, prev=(base + 1, ALL_CHAINS))
                return it + 1, _any_weight_left(run_s)

            done, live = lax.while_loop(lambda state: (state[0] < first) & (state[1] > 0), loop,
                                        (jnp.int32(0), jnp.int32(1)))
            one_more = (done == first) & (live > 0)

            @pl.when(one_more)
            def _():
                front([(c, False) for c in UPPER_CHAINS], -1, prev=(0, ALL_CHAINS))
                tail(-1, UPPER_CHAINS)

            @pl.when(jnp.logical_not(one_more))
            def _():
                tail(first - done, ALL_CHAINS)

            @pl.when(jnp.logical_not(first_step))
            def _():
                for cp in out_copies():
                    cp.wait()

            for half in range(2):
                rows = slice(half * t, (half + 1) * t)
                stash[0, rows, :] = (jnp.where(low, dq_s[2 * half], dq_s[2 * half + 1]) * scale).astype(BF16)
                stash[1, rows, :] = dkt_acc[2 * qi + half].T.astype(BF16)
                stash[2, rows, :] = dvt_acc[2 * qi + half].T.astype(BF16)
            for cp in out_copies():
                cp.start()

        walk()

        @pl.when(last_step)
        def _():
            for cp in out_copies():
                cp.wait()
            for cp in plan.copies(partial_refs, received_refs, scatter_sems):
                cp.wait()

    whole = lambda col0: pl.BlockSpec((s, LANES), lambda h, i: (0, col0 + h))
    blk = lambda col0: pl.BlockSpec((tq, LANES), lambda h, i: (nq - 1 - i, col0 + h))
    n_ch = len(ALL_CHAINS)
    res = _pcall(
        body, name="attn_bwd", grid=(nh, nq),
        in_specs=[blk(q0), whole(q0 + nh), whole(q0 + 2 * nh), blk(0), blk(w_conv // LANES),
                  pl.BlockSpec((1, LANES), lambda h, i: (0, h))] + [pl.BlockSpec(memory_space=pl.ANY)] * nw,
        out_specs=[pl.BlockSpec(memory_space=pl.ANY),
                   pl.BlockSpec((1, LANES), lambda h, i: (0, h))] + [pl.BlockSpec(memory_space=pl.ANY)] * nw,
        out_shape=[jax.ShapeDtypeStruct((s, n_cols), BF16), jax.ShapeDtypeStruct((1, w_attn), F32)]
        + plan.out_shapes,
        scratch_shapes=[pltpu.VMEM((s, LANES), BF16), pltpu.VMEM((s, LANES), BF16),
                        pltpu.VMEM((s // t, LANES, t), F32),
                        pltpu.VMEM((s // t, LANES, t), F32),
                        pltpu.VMEM((3, tq, LANES), BF16),
                        pltpu.VMEM((2 * t, t), BF16),
                        pltpu.VMEM((n_ch, t, LANES), BF16),
                        pltpu.VMEM((n_ch, t, LANES), BF16),
                        pltpu.VMEM((n_ch, LANES, t), BF16),
                        pltpu.VMEM((n_ch, LANES, t), BF16),
                        pltpu.VMEM((n_ch, t, t), F32),
                        pltpu.VMEM((n_ch, t, t), F32),
                        pltpu.VMEM((n_ch, t, 2 * t), BF16),
                        pltpu.VMEM((n_ch, t, t), F32),
                        pltpu.VMEM((n_ch, t, t), F32),
                        pltpu.VMEM((n_ch, t, t), F32),
                        pltpu.VMEM((n_ch, t, t), BF16),
                        pltpu.VMEM((n_ch, t, t), F32),
                        pltpu.VMEM((n_ch, t, 2 * t), BF16),
                        pltpu.VMEM((n_ch, t, t), F32),
                        pltpu.VMEM((n_ch, t, t), BF16),
                        pltpu.VMEM((n_ch, t, LANES), F32),
                        pltpu.VMEM((n_ch, t, LANES), F32),
                        pltpu.VMEM((n_ch, t, LANES), F32),
                        pltpu.VMEM((n_ch, t, LANES), F32),
                        pltpu.VMEM((n_ch, t, LANES), F32),
                        pltpu.SemaphoreType.DMA((3,))]
        + plan.sems,
        input_output_aliases=plan.aliases(6, 2),
        compiler_params=_params(("arbitrary", "arbitrary")),
    )(proj, proj, proj, o, dcat, g_attn, *plan.inputs)
    return res[0], res[1], res[2:]


def _place():
    return lax.axis_index("x"), lax.axis_index("y"), lax.axis_index("c")


def _other_chips(x, y):
    return [(1 - x, y), (x, 1 - y), (1 - x, 1 - y)]


def _slot(px, py, pc):
    return 4 * px + 2 * py + pc


def _all_gather(shards, out_dtypes):
    nw = len(shards)

    def body(*refs):
        ins, outs, stage = refs[:nw], refs[nw:2 * nw], refs[2 * nw:3 * nw]
        send_sems, recv_sems, local_sems = refs[3 * nw:]
        x, y, c = _place()
        me, sibling = (x, y, c), (x, y, 1 - c)
        chips = _other_chips(x, y)

        def copy(w, k, block, to, src=None):
            dst = outs[w].at[_slot(*block)]
            return pltpu.make_async_remote_copy(
                src_ref=dst if src is None else src, dst_ref=dst,
                send_sem=send_sems.at[w * 7 + k], recv_sem=recv_sems.at[w * 7 + k],
                device_id=to, device_id_type=MESH)

        started = []
        local = []
        for w in range(nw):
            stage[w][...] = ins[w][...].astype(stage[w].dtype)
            cp = pltpu.make_async_copy(stage[w], outs[w].at[_slot(*me)], local_sems.at[w])
            cp.start()
            local.append(cp)
            started.append(copy(w, 0, me, sibling, src=stage[w]))
            started[-1].start()
            for j, chip in enumerate(chips):
                started.append(copy(w, 1 + j, me, (*chip, c), src=stage[w]))
                started[-1].start()
        for j, chip in enumerate(chips):
            for w in range(nw):
                copy(w, 1 + j, (*chip, c), me).wait_recv()
                started.append(copy(w, 4 + j, (*chip, c), sibling))
                started[-1].start()
        for w in range(nw):
            copy(w, 0, sibling, me).wait_recv()
            for j, chip in enumerate(chips):
                copy(w, 4 + j, (*chip, 1 - c), me).wait_recv()
        for cp in started:
            cp.wait_send()
        for cp in local:
            cp.wait()

    return _pcall(
        body, name="all_gather_weights",
        in_specs=[pl.BlockSpec(memory_space=pltpu.VMEM)] * nw,
        out_specs=[pl.BlockSpec(memory_space=pl.ANY)] * nw,
        out_shape=[jax.ShapeDtypeStruct((N_DEV, *a.shape), d) for a, d in zip(shards, out_dtypes)],
        scratch_shapes=[pltpu.VMEM(a.shape, d) for a, d in zip(shards, out_dtypes)]
        + [pltpu.SemaphoreType.DMA((7 * nw,)), pltpu.SemaphoreType.DMA((7 * nw,)),
           pltpu.SemaphoreType.DMA((nw,))],
        compiler_params=_params(),
    )(*shards)


N_PEERS = N_DEV - 1


def _peer(k):
    x, y, c = _place()
    return (x ^ (k >> 2), y ^ ((k >> 1) & 1), c ^ (k & 1))


def _remote(src, dst, sems, index, to):
    return pltpu.make_async_remote_copy(src_ref=src, dst_ref=dst, send_sem=sems[0].at[index],
                                        recv_sem=sems[1].at[index], device_id=to, device_id_type=MESH)


def _gather_out_copies(staged, gathered, sems):
    x, y, c = _place()
    me = _slot(x, y, c)
    targets = [(x, y, 1 - c)] + [(*chip, c) for chip in _other_chips(x, y)]
    copies = []
    for w, (src, dst) in enumerate(zip(staged, gathered)):
        copies.append(pltpu.make_async_copy(src, dst.at[me], sems[2].at[w]))
        copies += [_remote(src, dst.at[me], sems, w * len(targets) + k, to) for k, to in enumerate(targets)]
    return copies


def _gather_pass_copies(arrived, gathered, sems):
    x, y, c = _place()
    chips = _other_chips(x, y)
    return [_remote(src.at[_slot(*chip, c)], dst.at[_slot(*chip, c)], sems, w * len(chips) + j, (x, y, 1 - c))
            for w, (src, dst) in enumerate(zip(arrived, gathered)) for j, chip in enumerate(chips)]


def _scatter_copies(partials, received, sems):
    me = _slot(*_place())
    return [_remote(src.at[me ^ k], dst.at[k - 1], sems, w * N_PEERS + k - 1, _peer(k))
            for w, (src, dst) in enumerate(zip(partials, received)) for k in range(1, N_DEV)]


class _Carried:
    COPIES = {"gather_out": (_gather_out_copies, 4, True), "gather_pass": (_gather_pass_copies, 3, False),
              "scatter": (_scatter_copies, N_PEERS, False)}

    def __init__(self, jobs):
        self.jobs = [(kind, list(arrays)) for kind, arrays in jobs if len(arrays)]
        self.inputs = [a for _, arrays in self.jobs for a in arrays]
        self.out_shapes, self.sems, self.sem_counts = [], [], []
        for kind, arrays in self.jobs:
            _, fan, local = self.COPIES[kind]
            for a in arrays:
                shape = {"gather_out": (N_DEV, *a.shape), "gather_pass": a.shape,
                         "scatter": (N_PEERS, *a.shape[1:])}[kind]
                self.out_shapes.append(jax.ShapeDtypeStruct(shape, BF16))
            job_sems = [pltpu.SemaphoreType.DMA((fan * len(arrays),))] * 2
            job_sems += [pltpu.SemaphoreType.DMA((len(arrays),))] if local else []
            self.sems += job_sems
            self.sem_counts.append(len(job_sems))

    def aliases(self, first_input, first_output):
        pairs, at = {}, 0
        for kind, arrays in self.jobs:
            if kind == "gather_pass":
                pairs.update({first_input + at + i: first_output + at + i for i in range(len(arrays))})
            at += len(arrays)
        return pairs

    def copies(self, in_refs, out_refs, sem_refs):
        out, at, sem_at = [], 0, 0
        for (kind, arrays), n_sems in zip(self.jobs, self.sem_counts):
            n = len(arrays)
            out += self.COPIES[kind][0](in_refs[at:at + n], out_refs[at:at + n], sem_refs[sem_at:sem_at + n_sems])
            at, sem_at = at + n, sem_at + n_sems
        return out


def _cast_shards(shards):
    def body(*refs):
        for src, dst in zip(refs[:len(shards)], refs[len(shards):]):
            dst[...] = src[...].astype(BF16)

    return _pcall(
        body, name="cast_shards",
        in_specs=[pl.BlockSpec(memory_space=pltpu.VMEM)] * len(shards),
        out_specs=[pl.BlockSpec(memory_space=pltpu.VMEM)] * len(shards),
        out_shape=[jax.ShapeDtypeStruct(a.shape, BF16) for a in shards],
        compiler_params=_params(),
    )(*shards)


def _all_reduce_small(packed):
    r = packed.shape[0]

    def body(x_ref, o_ref, gathered, send_sems, recv_sems):
        x, y, c = _place()
        me = _slot(x, y, c)
        gathered[me] = x_ref[...]
        peers = [(px, py, pc) for px in range(2) for py in range(2) for pc in range(2)]
        started = []
        for k in range(1, N_DEV):
            to = (x ^ (k >> 2), y ^ ((k >> 1) & 1), c ^ (k & 1))
            cp = pltpu.make_async_remote_copy(
                src_ref=x_ref, dst_ref=gathered.at[me],
                send_sem=send_sems.at[k - 1], recv_sem=recv_sems.at[k - 1],
                device_id=to, device_id_type=MESH)
            cp.start()
            started.append(cp)
        del peers
        for cp in started:
            cp.wait()
        total = gathered[0]
        for k in range(1, N_DEV):
            total = total + gathered[k]
        o_ref[...] = total

    return _pcall(
        body, name="all_reduce_small",
        in_specs=[pl.BlockSpec(memory_space=pltpu.VMEM)],
        out_specs=pl.BlockSpec(memory_space=pltpu.VMEM),
        out_shape=jax.ShapeDtypeStruct(packed.shape, F32),
        scratch_shapes=[pltpu.VMEM((N_DEV, r, LANES), F32),
                        pltpu.SemaphoreType.DMA((N_DEV - 1,)), pltpu.SemaphoreType.DMA((N_DEV - 1,))],
        compiler_params=_params(),
    )(packed)


def _adam_math(w, g, m, v):
    m = ADAM_B1 * m + (1.0 - ADAM_B1) * g
    v = ADAM_B2 * v + (1.0 - ADAM_B2) * jnp.square(g)
    m_hat = m / (1.0 - ADAM_B1 ** ADAM_STEP)
    v_hat = v / (1.0 - ADAM_B2 ** ADAM_STEP)
    delta = -ADAM_LR * (m_hat / (jnp.sqrt(v_hat) + ADAM_EPS) + ADAM_WD * w)
    return delta, m, v


ADAM_TILE_BYTES = 24 * 1024 * 1024


def _adam_sharded(name, own, received, w, m, v, place):
    r, cdim = w.shape
    row_bytes = 2 * cdim * (4 + 2 * N_PEERS + 3 * 4 + 4 * 4)
    tr = _tile(r, max(LANES, ADAM_TILE_BYTES // row_bytes // LANES * LANES)) if r % LANES == 0 else r

    def body(place_ref, own_ref, rec_ref, w_ref, m_ref, v_ref, g_ref, d_ref, nm_ref, nv_ref):
        del place_ref
        g = own_ref[...]
        for j in range(N_PEERS):
            g = g + rec_ref[j].astype(F32)
        delta, nm, nv = _adam_math(w_ref[...], g, m_ref[...], v_ref[...])
        g_ref[...] = g
        d_ref[...] = delta
        nm_ref[...] = nm
        nv_ref[...] = nv

    blk = pl.BlockSpec((tr, cdim), lambda i, pr: (i, 0))
    grid_spec = pltpu.PrefetchScalarGridSpec(
        num_scalar_prefetch=1, grid=(r // tr,),
        in_specs=[pl.BlockSpec((None, tr, cdim), lambda i, pr: (4 * pr[0] + 2 * pr[1] + pr[2], i, 0)),
                  pl.BlockSpec((N_PEERS, tr, cdim), lambda i, pr: (0, i, 0)), blk, blk, blk],
        out_specs=[blk] * 4)
    return _pcall(body, name=name, grid_spec=grid_spec,
                  out_shape=[jax.ShapeDtypeStruct((r, cdim), F32)] * 4,
                  compiler_params=_params(("parallel",)))(place, own, received, w, m, v)


def _adam_small(w, g, m, v):
    def body(w_ref, g_ref, m_ref, v_ref, d_ref, nm_ref, nv_ref):
        delta, nm, nv = _adam_math(w_ref[...], g_ref[...], m_ref[...], v_ref[...])
        d_ref[...] = delta
        nm_ref[...] = nm
        nv_ref[...] = nv

    return _pcall(body, name="adam_small",
                  in_specs=[pl.BlockSpec(memory_space=pltpu.VMEM)] * 4,
                  out_specs=[pl.BlockSpec(memory_space=pltpu.VMEM)] * 3,
                  out_shape=[jax.ShapeDtypeStruct(w.shape, F32)] * 3,
                  compiler_params=_params())(w, g, m, v)


def _rows(vec):
    return vec.reshape(-1, LANES)


def kernel(x, p, g_mix, w_in, conv_w, g_conv_out, g_attn_out, w_out, g_mlp, w_up, w_down, g_ple, w_ple_gate, w_ple_proj, g_final, loss_target, m_g_mix, m_w_in, m_conv_w, m_g_conv_out, m_g_attn_out, m_w_out, m_g_mlp, m_w_up, m_w_down, m_g_ple, m_w_ple_gate, m_w_ple_proj, m_g_final, v_g_mix, v_w_in, v_conv_w, v_g_conv_out, v_g_attn_out, v_w_out, v_g_mlp, v_w_up, v_w_down, v_g_ple, v_w_ple_gate, v_w_ple_proj, v_g_final):
    s, d = x.shape[1], x.shape[2]
    w_conv = g_conv_out.shape[1]
    w_attn = g_attn_out.shape[1]
    cw = conv_w.shape[2]
    xs, ps, tgt = x[0], p[0, 0], loss_target[0]
    place = jnp.stack([lax.axis_index("x"), lax.axis_index("y"), lax.axis_index("c")]).astype(jnp.int32)
    my_slot = 4 * place[0] + 2 * place[1] + place[2]

    conv_tile = jnp.pad(conv_w[0], ((0, HALO - CONV_K), (0, LANES - cw)))
    big = [w_in[0], w_out[0], w_up[0], w_down[0], w_ple_gate[0], w_ple_proj[0]]
    win_g, conv_g = _all_gather([big[0], conv_tile], [BF16, F32])
    s_out, s_up, s_down, s_gate, s_proj = _cast_shards(big[1:])
    conv_full = jnp.transpose(conv_g[:, :CONV_K, :cw], (1, 0, 2)).reshape(CONV_K, w_conv)
    in_shard, up_shard, proj_shard = big[0].shape[1], big[2].shape[1], big[5].shape[1]

    a = _rmsnorm_fwd("norm_mix", xs, g_mix)
    proj, g_out, g_gate, g_proj = _mm_nn("in_proj", a, win_g, n_shard=in_shard, tn=in_shard, tm=2048,
                                         carry=[("gather_out", [s_out, s_gate, s_proj])])
    cat = _conv_fwd(proj, conv_full, g_conv_out, w_conv, d)
    o, cat, (g_up, g_down, wout_g, wgate_g, wproj_g) = _attn_fwd(
        proj, g_attn_out, cat, w_conv,
        [("gather_out", [s_up, s_down]), ("gather_pass", [g_out, g_gate, g_proj])])
    wout_f = wout_g.reshape(-1, wout_g.shape[-1])
    wgate_f = wgate_g.reshape(-1, wgate_g.shape[-1])
    h1, wup_g = _mm_nn("out_proj", cat, wout_f, epilogue=_ep_residual, extras=(xs,),
                       carry=[("gather_pass", [g_up])])
    mn = _rmsnorm_fwd("norm_mlp", h1, g_mlp)
    act, wdown_g = _mm_nn("mlp_up", mn, wup_g, n_shard=up_shard, epilogue=_ep_up, out_dtypes=(BF16,), tm=2048,
                          carry=[("gather_pass", [g_down])])
    wdown_f = wdown_g.reshape(-1, wdown_g.shape[-1])
    h2, = _mm_nn("mlp_down", act, wdown_f, epilogue=_ep_residual, extras=(h1,))
    n3 = _rmsnorm_fwd("norm_ple", h2, g_ple)
    gl, = _mm_nn("ple_gate", n3, wgate_f)
    pp = _ple_proj(ps, wproj_g)
    loss_part, dh3, dgl, dpp, dg_final = _ple_loss(h2, gl, pp, tgt, g_final.reshape(1, d))

    def slots(t2d):
        return t2d.reshape(N_DEV, -1, t2d.shape[-1])

    dw_proj = _d_ple_proj(ps, dpp, proj_shard)
    dw_gate = [slots(t) for t in _mm_tn("d_w_ple_gate", n3, dgl)]
    dh2, dh2b, dg_ple = _mm_nt_norm_bwd("d_norm_ple", dgl, wgate_f, h2, g_ple, dh3)
    du, gate_recv, proj_recv = _mm_nt("d_mlp_act", dh2b, wdown_f, epilogue=_ep_dact, out_dtypes=(BF16,),
                                      extras=(act,), tm=2048, carry=[("scatter", [dw_gate[1], dw_proj[1]])])
    dw_down = [slots(t) for t in _mm_tn("d_w_down", act, dh2b)]
    dw_up = _mm_tn("d_w_up", mn, du, n_shard=up_shard)
    dh1, dh1b, dg_mlp, down_recv = _mm_nt_norm_bwd("d_norm_mlp", du, wup_g, h1, g_mlp, dh2, k_shard=up_shard,
                                                   tm=1024, carry=[("scatter", [dw_down[1]])])
    dcat, = _mm_nt("d_cat", dh1b, wout_f)
    dw_out = [slots(t) for t in _mm_tn("d_w_out", cat, dh1b)]
    dproj, dg_attn, (up_recv,) = _attn_bwd(proj, o, dcat, g_attn_out, w_conv, [("scatter", [dw_up[1]])])
    dproj, dconv, dg_conv = _conv_bwd(proj, dcat, conv_full, g_conv_out, dproj, w_conv)
    *dw_in, out_recv = _mm_tn("d_w_in", a, dproj, n_shard=in_shard, tn=in_shard,
                              carry=[("scatter", [dw_out[1]])])
    grad_x, _, dg_mix, in_recv = _mm_nt_norm_bwd("d_norm_mix", dproj, win_g, xs, g_mix, dh1, k_shard=in_shard,
                                                 tk=2 * in_shard, tm=1024, carry=[("scatter", [dw_in[1]])])

    names = ["w_in", "w_out", "w_up", "w_down", "w_ple_gate", "w_ple_proj"]
    owns = [dw_in[0], dw_out[0], dw_up[0], dw_down[0], dw_gate[0], dw_proj[0]]
    recvs = [in_recv, out_recv, up_recv, down_recv, gate_recv, proj_recv]
    moments = [(m_w_in, v_w_in), (m_w_out, v_w_out), (m_w_up, v_w_up), (m_w_down, v_w_down),
               (m_w_ple_gate, v_w_ple_gate), (m_w_ple_proj, v_w_ple_proj)]
    big_out = {}
    for n, own, rc, wt, (mm, vv) in zip(names, owns, recvs, big, moments):
        big_out[n] = [t[None] for t in _adam_sharded("adam_" + n, own, rc, wt, mm[0], vv[0], place)]

    n_conv_rows = CONV_K * w_conv // LANES
    small_g = jnp.concatenate(
        [_rows(dg_mix[0]), _rows(dg_conv[0]), _rows(dg_attn[0]), _rows(dg_mlp[0]), _rows(dg_ple[0]),
         _rows(dg_final[0]), _rows(dconv.reshape(-1)), loss_part], axis=0)
    n_gain_rows = small_g.shape[0] - n_conv_rows - 1
    pad_rows = (-small_g.shape[0]) % HALO
    small_g = _all_reduce_small(jnp.pad(small_g, ((0, pad_rows), (0, 0))))
    loss = small_g[n_gain_rows + n_conv_rows, 0]
    dconv_full = small_g[n_gain_rows:n_gain_rows + n_conv_rows].reshape(CONV_K, w_conv)
    dconv_mine = lax.dynamic_slice(dconv_full, (0, my_slot * cw), (CONV_K, cw))

    def pack(vecs, conv_part):
        rows = [_rows(t.reshape(-1)) for t in vecs]
        rows.append(jnp.pad(conv_part, ((0, HALO - CONV_K), (0, LANES - cw))))
        return jnp.concatenate(rows, axis=0)

    gains = [g_mix, g_conv_out, g_attn_out, g_mlp, g_ple, g_final]
    gains_m = [m_g_mix, m_g_conv_out, m_g_attn_out, m_g_mlp, m_g_ple, m_g_final]
    gains_v = [v_g_mix, v_g_conv_out, v_g_attn_out, v_g_mlp, v_g_ple, v_g_final]
    gpack = jnp.concatenate([small_g[:n_gain_rows], jnp.pad(dconv_mine, ((0, HALO - CONV_K), (0, LANES - cw)))], axis=0)
    sd, sm, sv = _adam_small(pack(gains, conv_w[0]), gpack, pack(gains_m, m_conv_w[0]), pack(gains_v, v_conv_w[0]))

    def unpack(packed):
        out, r0 = [], 0
        for t in gains:
            nr = t.size // LANES
            out.append(packed[r0:r0 + nr].reshape(t.shape))
            r0 += nr
        out.append(packed[r0:r0 + CONV_K, :cw][None])
        return out

    sg_l, sd_l, sm_l, sv_l = unpack(gpack), unpack(sd), unpack(sm), unpack(sv)
    small_names = ["g_mix", "g_conv_out", "g_attn_out", "g_mlp", "g_ple", "g_final", "conv_w"]
    small_out = {n: [sg_l[i], sd_l[i], sm_l[i], sv_l[i]] for i, n in enumerate(small_names)}

    order = ["g_mix", "w_in", "conv_w", "g_conv_out", "g_attn_out", "w_out", "g_mlp", "w_up", "w_down",
             "g_ple", "w_ple_gate", "w_ple_proj", "g_final"]
    table = {**big_out, **small_out}
    outs = [loss, grad_x[None]]
    for kind in range(4):
        outs.extend(table[n][kind] for n in order)
    return tuple(outs)
```

```python
import jax
import jax.numpy as jnp
from jax import lax
from jax.experimental import pallas as pl
from jax.experimental.pallas import tpu as pltpu

F32 = jnp.float32
BF16 = jnp.bfloat16
EPS = 1e-6
HEAD_DIM = 64
LANES = 128
CONV_K = 3
MXU_WIDTH = 256
ATTN_BLOCK = MXU_WIDTH
HALO = 8
N_DEV = 8
MESH = pl.DeviceIdType.MESH
VMEM_LIMIT = 56 * 1024 * 1024

ADAM_LR = 0.001
ADAM_B1 = 0.9
ADAM_B2 = 0.999
ADAM_EPS = 1e-08
ADAM_WD = 0.01
ADAM_STEP = 10


def _pcall(body, **kw):
    return pl.pallas_call(body, **kw)


def _params(sem=None, **kw):
    return pltpu.CompilerParams(dimension_semantics=sem, vmem_limit_bytes=VMEM_LIMIT, **kw)


def _tile(dim, pref):
    t = min(dim, pref)
    while dim % t:
        t -= LANES
    assert t > 0, (dim, pref)
    return t


_NN = (((1,), (0,)), ((), ()))
_NT = (((1,), (1,)), ((), ()))
_TN = (((0,), (0,)), ((), ()))


def _ep_store(acc, outs):
    outs[0][...] = acc.astype(outs[0].dtype)


def _ep_both(acc, outs):
    outs[0][...] = acc
    outs[1][...] = acc.astype(BF16)


def _ep_residual(acc, res, outs):
    outs[0][...] = acc + res[...]


def _ep_up(acc, outs):
    outs[0][...] = jnp.square(jnp.maximum(acc, 0.0)).astype(BF16)


def _ep_dact(acc, act, outs):
    outs[0][...] = (acc * (2.0 * jnp.sqrt(act[...].astype(F32)))).astype(BF16)


def _ep_norm_bwd(acc, h, g, dres, outs):
    @pl.when(pl.program_id(0) == 0)
    def _():
        outs[2][...] = jnp.zeros_like(outs[2])

    hv = h[...]
    r = lax.rsqrt(jnp.mean(hv * hv, axis=-1, keepdims=True) + EPS)
    hn = hv * r
    outs[2][...] += jnp.sum(acc * hn, axis=0, keepdims=True)
    dhn = acc * g[...]
    dh = dres[...] + r * (dhn - hn * jnp.mean(dhn * hn, axis=-1, keepdims=True))
    outs[0][...] = dh
    outs[1][...] = dh.astype(BF16)


def _matmul(name, a, b, *, dims, grid, a_spec, b_spec, acc_shape, out_shapes, out_specs,
            epilogue=_ep_store, extras=(), extra_specs=(), carry=(), sequential=False, lhs_norm=False):
    nk = grid[2]
    plan = _Carried(carry)
    n_ex, n_out, n_xc = len(extras), len(out_shapes), len(plan.inputs)
    n_sems = len(plan.sems)
    last = tuple(g - 1 for g in grid)
    assert not lhs_norm or nk == 1

    def product(a_ref, b_ref):
        if len(b_ref.shape) == 2:
            return lax.dot_general(a_ref[...].astype(BF16), b_ref[...].astype(BF16), dims,
                                   preferred_element_type=F32)
        width = b_ref.shape[2]
        return sum(lax.dot_general(a_ref[:, g * width:(g + 1) * width].astype(BF16), b_ref[g].astype(BF16), dims,
                                   preferred_element_type=F32) for g in range(b_ref.shape[0]))

    def body(a_ref, b_ref, *rest):
        ex, rest = rest[:n_ex], rest[n_ex:]
        partials, rest = rest[:n_xc], rest[n_xc:]
        outs, rest = rest[:n_out], rest[n_out:]
        received, rest = rest[:n_xc], rest[n_xc:]
        ids = [pl.program_id(axis) for axis in range(3)]
        if n_xc:
            @pl.when((ids[0] == 0) & (ids[1] == 0) & (ids[2] == 0))
            def _():
                for cp in plan.copies(partials, received, rest[-n_sems:]):
                    cp.start()

        if lhs_norm:
            x_ref, a_ref, gain, ex, outs = a_ref, outs[-1], ex[-1], ex[:-1], outs[:-1]

            @pl.when(ids[1] == 0)
            def _():
                for m0 in range(0, acc_shape[0], MXU_WIDTH):
                    rows = slice(m0, min(m0 + MXU_WIDTH, acc_shape[0]))
                    xv = x_ref[rows, :]
                    r = lax.rsqrt(jnp.mean(xv * xv, axis=-1, keepdims=True) + EPS)
                    a_ref[rows, :] = (xv * r * gain[...]).astype(BF16)

        if nk == 1 and not sequential and dims != _TN:
            for n0 in range(0, acc_shape[1], MXU_WIDTH):
                cols = slice(n0, min(n0 + MXU_WIDTH, acc_shape[1]))
                b_cols = b_ref.at[cols, :] if dims == _NT else b_ref.at[:, cols]
                for m0 in range(0, acc_shape[0], MXU_WIDTH):
                    rows = slice(m0, min(m0 + MXU_WIDTH, acc_shape[0]))
                    epilogue(product(a_ref.at[rows, :], b_cols), *[e.at[rows, cols] for e in ex],
                             [o.at[rows, cols] for o in outs])
        elif nk == 1:
            epilogue(product(a_ref, b_ref), *ex, outs)
        else:
            acc = rest[0]

            @pl.when(ids[2] == 0)
            def _():
                acc[...] = product(a_ref, b_ref)

            @pl.when(ids[2] > 0)
            def _():
                acc[...] += product(a_ref, b_ref)

            @pl.when(ids[2] == nk - 1)
            def _():
                epilogue(acc[...], *ex, outs)

        if n_xc:
            @pl.when((ids[0] == last[0]) & (ids[1] == last[1]) & (ids[2] == last[2]))
            def _():
                for cp in plan.copies(partials, received, rest[-n_sems:]):
                    cp.wait()

    anywhere = [pl.BlockSpec(memory_space=pl.ANY)] * n_xc
    return _pcall(
        body, name=name, grid=grid,
        in_specs=[a_spec, b_spec, *extra_specs, *anywhere],
        out_specs=[*out_specs, *anywhere],
        out_shape=[*out_shapes, *plan.out_shapes],
        scratch_shapes=([] if nk == 1 else [pltpu.VMEM(acc_shape, F32)]) + plan.sems,
        input_output_aliases=plan.aliases(2 + n_ex, n_out),
        compiler_params=_params(("arbitrary",) * 3 if n_xc or sequential or lhs_norm
                                else ("parallel", "parallel", "arbitrary")),
    )(a, b, *extras, *plan.inputs)


_NO_CARRY = ()


def _mm_nn(name, a, w, *, n_shard=None, epilogue=_ep_store, out_dtypes=(F32,), extras=(), carry=_NO_CARRY,
           lhs_norm=None, tm=1024, tn=1024, tk=1024):
    m, kd = a.shape
    if lhs_norm is not None:
        tk = kd
    if n_shard is None:
        n = w.shape[1]
        tn = _tile(n, tn)
        tk = _tile(kd, tk)
        b_spec = pl.BlockSpec((tk, tn), lambda i, j, k: (k, j))
    else:
        n = N_DEV * n_shard
        tn = _tile(n_shard, tn)
        tk = _tile(kd, tk)
        per = n_shard // tn
        b_spec = pl.BlockSpec((None, tk, tn), lambda i, j, k: (j // per, k, j % per))
    tm = _tile(m, tm)
    o_spec = pl.BlockSpec((tm, tn), lambda i, j, k: (i, j))
    out_shapes = [jax.ShapeDtypeStruct((m, n), d) for d in out_dtypes]
    out_specs = [o_spec] * len(out_dtypes)
    extra_specs = [o_spec] * len(extras)
    if lhs_norm is not None:
        extras = (*extras, lhs_norm)
        extra_specs.append(pl.BlockSpec((1, kd), lambda i, j, k: (0, 0)))
        out_shapes.append(jax.ShapeDtypeStruct((m, kd), BF16))
        out_specs.append(pl.BlockSpec((tm, kd), lambda i, j, k: (i, 0)))
    return _matmul(
        name, a, w, dims=_NN, grid=(m // tm, n // tn, kd // tk),
        a_spec=pl.BlockSpec((tm, tk), lambda i, j, k: (i, k)), b_spec=b_spec,
        acc_shape=(tm, tn), out_shapes=out_shapes, out_specs=out_specs,
        epilogue=epilogue, extras=extras, extra_specs=extra_specs, carry=carry, lhs_norm=lhs_norm is not None)


def _mm_nt(name, a, w, *, epilogue=_ep_store, out_dtypes=(F32,), extras=(), carry=_NO_CARRY,
           tm=1024, tn=1024, tk=1024):
    m, kd = a.shape
    n = w.shape[0]
    tm, tn, tk = _tile(m, tm), _tile(n, tn), _tile(kd, tk)
    o_spec = pl.BlockSpec((tm, tn), lambda i, j, k: (i, j))
    return _matmul(
        name, a, w, dims=_NT, grid=(m // tm, n // tn, kd // tk),
        a_spec=pl.BlockSpec((tm, tk), lambda i, j, k: (i, k)),
        b_spec=pl.BlockSpec((tn, tk), lambda i, j, k: (j, k)),
        acc_shape=(tm, tn),
        out_shapes=[jax.ShapeDtypeStruct((m, n), d) for d in out_dtypes],
        out_specs=[o_spec] * len(out_dtypes),
        epilogue=epilogue, extras=extras, extra_specs=[o_spec] * len(extras), carry=carry)


def _mm_nt_norm_bwd(name, a, w, h, g, dres, *, k_shard=None, carry=_NO_CARRY, tm=512, tk=1024):
    m, kd = a.shape
    n = h.shape[1]
    if k_shard is None:
        tk = _tile(kd, tk)
        b_spec = pl.BlockSpec((n, tk), lambda i, j, k: (0, k))
    else:
        group = max(1, min(tk // k_shard, N_DEV))
        while N_DEV % group:
            group -= 1
        tk = group * k_shard
        b_spec = pl.BlockSpec((group, n, k_shard), lambda i, j, k: (k, 0, 0))
    tm = _tile(m, tm)
    rows = pl.BlockSpec((tm, n), lambda i, j, k: (i, 0))
    vec = pl.BlockSpec((1, n), lambda i, j, k: (0, 0))
    return _matmul(
        name, a, w, dims=_NT, grid=(m // tm, 1, kd // tk),
        a_spec=pl.BlockSpec((tm, tk), lambda i, j, k: (i, k)), b_spec=b_spec, acc_shape=(tm, n),
        out_shapes=[jax.ShapeDtypeStruct((m, n), F32), jax.ShapeDtypeStruct((m, n), BF16),
                    jax.ShapeDtypeStruct((1, n), F32)],
        out_specs=[rows, rows, vec], epilogue=_ep_norm_bwd,
        extras=(h, g, dres), extra_specs=[rows, vec, rows], carry=carry, sequential=True)


TN_TILE_BYTES = 40 * 1024 * 1024


def _mm_tn(name, a, b, *, n_shard=None, carry=_NO_CARRY, tm=1024, tn=1024):
    t, m = a.shape
    n = b.shape[1]
    tm = _tile(m, tm)
    tn = _tile(n if n_shard is None else n_shard, tn)
    tk = t
    while 2 * 2 * tk * (tm + tn) + 4 * tm * tn * 5 > TN_TILE_BYTES and tk % (2 * LANES) == 0:
        tk //= 2
    if n_shard is None:
        o_spec = pl.BlockSpec((tm, tn), lambda i, j, k: (i, j))
        shape = (m, n)
    else:
        per = n_shard // tn
        o_spec = pl.BlockSpec((None, tm, tn), lambda i, j, k: (j // per, i, j % per))
        shape = (N_DEV, m, n_shard)
    return _matmul(
        name, a, b, dims=_TN, grid=(m // tm, n // tn, t // tk),
        a_spec=pl.BlockSpec((tk, tm), lambda i, j, k: (k, i)),
        b_spec=pl.BlockSpec((tk, tn), lambda i, j, k: (k, j)),
        acc_shape=(tm, tn), epilogue=_ep_both, carry=carry,
        out_shapes=[jax.ShapeDtypeStruct(shape, F32), jax.ShapeDtypeStruct(shape, BF16)],
        out_specs=[o_spec, o_spec])


def _ple_proj(p, w_g, tm=1024):
    s, kd = p.shape
    ns = w_g.shape[2]
    tm = _tile(s, tm)

    def body(p_ref, w_ref, o_ref):
        pv = p_ref[...].astype(BF16)
        for j in range(N_DEV):
            o_ref[:, j * ns:(j + 1) * ns] = jnp.dot(pv, w_ref[j], preferred_element_type=F32)

    return _pcall(body, name="ple_proj", grid=(s // tm,),
                  in_specs=[pl.BlockSpec((tm, kd), lambda i: (i, 0)),
                            pl.BlockSpec((N_DEV, kd, ns), lambda i: (0, 0, 0))],
                  out_specs=pl.BlockSpec((tm, N_DEV * ns), lambda i: (i, 0)),
                  out_shape=jax.ShapeDtypeStruct((s, N_DEV * ns), F32),
                  compiler_params=_params(("parallel",)))(p, w_g)


def _d_ple_proj(p, dpp, ns, tk=1024):
    s, kd = p.shape
    tk = _tile(s, tk)
    nk = s // tk

    def body(p_ref, d_ref, of_ref, ob_ref, acc):
        k = pl.program_id(0)

        @pl.when(k == 0)
        def _():
            acc[...] = jnp.zeros_like(acc)

        pv = p_ref[...].astype(BF16)
        for j in range(N_DEV):
            acc[j] += lax.dot_general(pv, d_ref[:, j * ns:(j + 1) * ns], _TN, preferred_element_type=F32)

        @pl.when(k == nk - 1)
        def _():
            of_ref[...] = acc[...]
            ob_ref[...] = acc[...].astype(BF16)

    whole = pl.BlockSpec((N_DEV, kd, ns), lambda k: (0, 0, 0))
    return _pcall(body, name="d_w_ple_proj", grid=(nk,),
                  in_specs=[pl.BlockSpec((tk, kd), lambda k: (k, 0)),
                            pl.BlockSpec((tk, N_DEV * ns), lambda k: (k, 0))],
                  out_specs=[whole, whole],
                  out_shape=[jax.ShapeDtypeStruct((N_DEV, kd, ns), F32), jax.ShapeDtypeStruct((N_DEV, kd, ns), BF16)],
                  scratch_shapes=[pltpu.VMEM((N_DEV, kd, ns), F32)],
                  compiler_params=_params(("arbitrary",)))(p, dpp)


def _row_spec(tr, d):
    return pl.BlockSpec((tr, d), lambda i: (i, 0))


def _vec_spec(d):
    return pl.BlockSpec((1, d), lambda i: (0, 0))


def _ple_loss(h2, gl, pp, tgt, g_final, tr=512):
    s, d = h2.shape
    tr = _tile(s, tr)

    def body(h2_ref, gl_ref, pp_ref, t_ref, g_ref, loss_ref, dh3_ref, dgl_ref, dpp_ref, dg_ref):
        @pl.when(pl.program_id(0) == 0)
        def _():
            dg_ref[...] = jnp.zeros_like(dg_ref)
            loss_ref[...] = jnp.zeros_like(loss_ref)

        gate = jax.nn.sigmoid(gl_ref[...])
        ppv = pp_ref[...]
        h3 = h2_ref[...] + gate * ppv
        r = lax.rsqrt(jnp.mean(h3 * h3, axis=-1, keepdims=True) + EPS)
        hn = h3 * r
        gv = g_ref[...]
        diff = hn * gv - t_ref[...]
        row = jnp.mean(diff * diff, axis=-1, keepdims=True)
        loss_ref[...] += 0.5 * jnp.sum(row, axis=0, keepdims=True)
        dy = diff * (1.0 / d)
        dg_ref[...] += jnp.sum(dy * hn, axis=0, keepdims=True)
        dhn = dy * gv
        dh3 = r * (dhn - hn * jnp.mean(dhn * hn, axis=-1, keepdims=True))
        dh3_ref[...] = dh3
        dgl_ref[...] = (dh3 * ppv * gate * (1.0 - gate)).astype(BF16)
        dpp_ref[...] = (dh3 * gate).astype(BF16)

    return _pcall(body, name="ple_loss", grid=(s // tr,),
                  in_specs=[_row_spec(tr, d)] * 4 + [_vec_spec(d)],
                  out_specs=[_vec_spec(LANES), _row_spec(tr, d), _row_spec(tr, d), _row_spec(tr, d), _vec_spec(d)],
                  out_shape=[jax.ShapeDtypeStruct((1, LANES), F32), jax.ShapeDtypeStruct((s, d), F32),
                             jax.ShapeDtypeStruct((s, d), BF16), jax.ShapeDtypeStruct((s, d), BF16),
                             jax.ShapeDtypeStruct((1, d), F32)],
                  compiler_params=_params(("arbitrary",)))(h2, gl, pp, tgt, g_final)


def _low_half():
    return lax.broadcasted_iota(jnp.int32, (1, LANES), 1) < HEAD_DIM


def _half_mean(v, low):
    s_lo = jnp.sum(jnp.where(low, v, 0.0), axis=-1, keepdims=True)
    s_hi = jnp.sum(jnp.where(low, 0.0, v), axis=-1, keepdims=True)
    return jnp.where(low, s_lo, s_hi) * (1.0 / HEAD_DIM)


def _head_norm_bwd(val, dout, g, low):
    r = lax.rsqrt(_half_mean(val * val, low) + EPS)
    vn = val * r
    dvn = dout * g
    return r * (dvn - vn * _half_mean(dvn * vn, low)), dout * vn


def _conv_taps(vv_ext, w_ref):
    v0 = vv_ext[HALO:]
    v1 = pltpu.roll(vv_ext, 1, 0)[HALO:]
    v2 = pltpu.roll(vv_ext, 2, 0)[HALO:]
    return w_ref[2:3, :] * v0 + w_ref[1:2, :] * v1 + w_ref[0:1, :] * v2, (v0, v1, v2)


def _conv_fwd(proj, conv_w, g_conv, w_conv, d_model, tr=1024):
    s = proj.shape[0]
    tr = _tile(s, tr)
    hb = tr // HALO

    def main(part):
        return pl.BlockSpec((tr, w_conv), lambda i: (i, part))

    def prev(part):
        return pl.BlockSpec((HALO, w_conv), lambda i: (jnp.maximum(i * hb - 1, 0), part))

    def body(cb_ref, cc_ref, cu_ref, ccp_ref, cup_ref, w_ref, g_ref, o_ref):
        i = pl.program_id(0)
        low = _low_half()
        for j in range(w_conv // LANES):
            cols = slice(j * LANES, (j + 1) * LANES)
            vv_prev = jnp.where(i > 0, ccp_ref[:, cols] * cup_ref[:, cols], 0.0)
            vv_ext = jnp.concatenate([vv_prev, cc_ref[:, cols] * cu_ref[:, cols]], axis=0)
            y, _ = _conv_taps(vv_ext, w_ref.at[:, cols])
            co = cb_ref[:, cols] * y
            r = lax.rsqrt(_half_mean(co * co, low) + EPS)
            o_ref[:, cols] = (co * r * g_ref[:, cols]).astype(BF16)

    return _pcall(
        body, name="conv_fwd", grid=(s // tr,),
        in_specs=[main(0), main(1), main(2), prev(1), prev(2),
                  pl.BlockSpec((CONV_K, w_conv), lambda i: (0, 0)),
                  pl.BlockSpec((1, w_conv), lambda i: (0, 0))],
        out_specs=pl.BlockSpec((tr, w_conv), lambda i: (i, 0)),
        out_shape=jax.ShapeDtypeStruct((s, d_model), BF16),
        compiler_params=_params(("parallel",)),
    )(proj, proj, proj, proj, proj, conv_w, g_conv)


def _conv_bwd(proj, dcat, conv_w, g_conv, dproj, w_conv, tr=1024):
    s = proj.shape[0]
    tr = _tile(s, tr)
    hb = tr // HALO
    last = s // HALO - 1
    nt = s // tr

    def main(part):
        return pl.BlockSpec((tr, w_conv), lambda i: (i, part))

    def prev(part):
        return pl.BlockSpec((HALO, w_conv), lambda i: (jnp.maximum(i * hb - 1, 0), part))

    def nxt(part):
        return pl.BlockSpec((HALO, w_conv), lambda i: (jnp.minimum((i + 1) * hb, last), part))

    def body(cb_ref, cc_ref, cu_ref, dc_ref, ccp_ref, cup_ref, cbn_ref, ccn_ref, cun_ref, dcn_ref,
             w_ref, g_ref, dproj_in, dproj_ref, dw_ref, dg_ref):
        del dproj_in
        i = pl.program_id(0)

        @pl.when(i == 0)
        def _():
            dw_ref[...] = jnp.zeros_like(dw_ref)
            dg_ref[...] = jnp.zeros_like(dg_ref)

        low = _low_half()
        n_ext = tr + HALO
        rowid = lax.broadcasted_iota(jnp.int32, (n_ext, 1), 0)
        for j in range(w_conv // LANES):
            cols = slice(j * LANES, (j + 1) * LANES)
            wj = w_ref.at[:, cols]
            cc, cu = cc_ref[:, cols], cu_ref[:, cols]
            vv_prev = jnp.where(i > 0, ccp_ref[:, cols] * cup_ref[:, cols], 0.0)
            vv_ext = jnp.concatenate([vv_prev, cc * cu, ccn_ref[:, cols] * cun_ref[:, cols]], axis=0)
            y_ext, (v0, v1, v2) = _conv_taps(vv_ext, wj)
            cb_ext = jnp.concatenate([cb_ref[:, cols], cbn_ref[:, cols]], axis=0)
            dc_ext = jnp.concatenate([dc_ref[:, cols], dcn_ref[:, cols]], axis=0)
            dco, dgn = _head_norm_bwd(cb_ext * y_ext, dc_ext, g_ref[:, cols], low)
            dyc = jnp.where((rowid < tr) | (i < nt - 1), dco * cb_ext, 0.0)
            dvv = (wj[2:3, :] * dyc[:tr] + wj[1:2, :] * pltpu.roll(dyc, n_ext - 1, 0)[:tr]
                   + wj[0:1, :] * pltpu.roll(dyc, n_ext - 2, 0)[:tr])
            dproj_ref[:, cols] = (dco[:tr] * y_ext[:tr]).astype(BF16)
            dproj_ref[:, w_conv + j * LANES:w_conv + (j + 1) * LANES] = (dvv * cu).astype(BF16)
            dproj_ref[:, 2 * w_conv + j * LANES:2 * w_conv + (j + 1) * LANES] = (dvv * cc).astype(BF16)
            dyt = dyc[:tr]
            for tap, shifted in enumerate((v2, v1, v0)):
                dw_ref[tap:tap + 1, cols] += jnp.sum(dyt * shifted[:tr], axis=0, keepdims=True)
            dg_ref[:, cols] += jnp.sum(dgn[:tr], axis=0, keepdims=True)

    n_cols = dproj.shape[1]
    return _pcall(
        body, name="conv_bwd", grid=(nt,),
        in_specs=[main(0), main(1), main(2), main(0),
                  prev(1), prev(2), nxt(0), nxt(1), nxt(2), nxt(0),
                  pl.BlockSpec((CONV_K, w_conv), lambda i: (0, 0)),
                  pl.BlockSpec((1, w_conv), lambda i: (0, 0)),
                  pl.BlockSpec(memory_space=pl.ANY)],
        out_specs=[pl.BlockSpec((tr, 3 * w_conv), lambda i: (i, 0)),
                   pl.BlockSpec((CONV_K, w_conv), lambda i: (0, 0)),
                   pl.BlockSpec((1, w_conv), lambda i: (0, 0))],
        out_shape=[jax.ShapeDtypeStruct((s, n_cols), BF16),
                   jax.ShapeDtypeStruct((CONV_K, w_conv), F32),
                   jax.ShapeDtypeStruct((1, w_conv), F32)],
        input_output_aliases={12: 0},
        compiler_params=_params(("arbitrary",)),
    )(proj, proj, proj, dcat, proj, proj, proj, proj, proj, dcat, conv_w, g_conv, dproj)


STRIP = 16

ALL_CHAINS = (0, 1, 2, 3)
UPPER_CHAINS = (2, 3)


RUN_FLOOR = -104.0


def _any_weight_left(run_s):
    return (jnp.max(run_s[...]) > RUN_FLOOR).astype(jnp.int32)


def _chains(low):
    return [(2 * half + h, half, msk) for half in range(2)
            for h, msk in enumerate((low, jnp.logical_not(low)))]


def _suffix_operator(t):
    r = lax.broadcasted_iota(jnp.int32, (2 * t, t), 0)
    c = lax.broadcasted_iota(jnp.int32, (2 * t, t), 1)
    return jnp.where((r > c) & ((r < t) | (r - t > c)), 1.0, 0.0).astype(BF16)


def _strips(t, diag):
    return [(i, slice(i * STRIP, (i + 1) * STRIP), t // 2 if diag and (i + 1) * STRIP <= t // 2 else t)
            for i in range(t // STRIP)]


def _strip_mask(i, w):
    r = lax.broadcasted_iota(jnp.int32, (STRIP, w), 0) + i * STRIP
    c = lax.broadcasted_iota(jnp.int32, (STRIP, w), 1)
    return r > c


def _store_trimmed(ref, rows, val, w, t, at=0):
    ref[rows, at:at + w] = val
    if w < t:
        ref[rows, at + w:at + t] = jnp.zeros((STRIP, t - w), val.dtype)


def _store_split(ref, rows, val, w, t):
    hi = val.astype(BF16)
    _store_trimmed(ref, rows, hi, w, t)
    _store_trimmed(ref, rows, (val - hi.astype(F32)).astype(BF16), w, t, at=t)


def _sb_scores(z_s, split_s, zl_s, tot_s, keep_s, t, diag):
    for i, rows, w in _strips(t, diag):
        z = z_s[rows, :w]
        log_beta = jnp.minimum(z, 0.0) - jnp.log(1.0 + jnp.exp(-jnp.abs(z)))
        log_keep = log_beta - z
        if diag:
            log_keep = jnp.where(_strip_mask(i, w), log_keep, 0.0)
        _store_split(split_s, rows, log_keep, w, t)
        zl_s[rows, :w] = log_beta
        tot_s[rows, :] = _row_sum(log_keep)
        if keep_s is not None:
            keep_s[rows, :w] = jnp.exp(log_keep)


def _row_sum(v):
    return jnp.broadcast_to(jnp.sum(v, axis=-1, keepdims=True), (v.shape[0], LANES))


def _wide(r, t):
    return jnp.concatenate([r] * (t // LANES), axis=1)


def _sb_weights(zl_s, suf_s, run_s, tot_s, a_s, t, diag, da_s=None, glog_s=None, gsplit_s=None, gtot_s=None):
    for i, rows, w in _strips(t, diag):
        run = run_s[rows, :]
        a = jnp.exp(zl_s[rows, :w] + suf_s[rows, :w] + _wide(run, w))
        if diag:
            a = jnp.where(_strip_mask(i, w), a, 0.0)
        ab = a.astype(BF16)
        _store_trimmed(a_s, rows, ab, w, t)
        run_s[rows, :] = run + tot_s[rows, :]
        if da_s is not None:
            glog = ab.astype(F32) * da_s[rows, :w]
            glog_s[rows, :w] = glog
            _store_split(gsplit_s, rows, glog, w, t)
            gtot_s[rows, :] = _row_sum(glog)


def _sb_dscores(glog_s, cum_s, rest_s, gtot_s, keep_s, dz_s, t, diag):
    for i, rows, w in _strips(t, diag):
        glog = glog_s[rows, :w]
        rest = rest_s[rows, :]
        from_here = _wide(rest, w) - cum_s[rows, :w]
        before = from_here - glog
        dz = from_here * keep_s[rows, :w] - before
        if diag:
            dz = jnp.where(_strip_mask(i, w), dz, 0.0)
        _store_trimmed(dz_s, rows, dz.astype(BF16), w, t)
        rest_s[rows, :] = rest - gtot_s[rows, :]


def _attn_fwd(proj, g_attn, cat, w_conv, carry, t=ATTN_BLOCK):
    s = proj.shape[0]
    w_attn = g_attn.shape[1]
    nh = w_attn // LANES
    t = _tile(s, t)
    tq = 2 * t
    nq = s // tq
    q0 = 3 * w_conv // LANES
    scale = HEAD_DIM ** -0.5
    plan = _Carried(carry)
    nw = len(plan.inputs)

    def body(q_ref, k_ref, v_ref, g_ref, cat_in, *rest):
        staged_refs, rest = rest[:nw], rest[nw:]
        o_ref, cat_ref = rest[:2]
        gathered_refs, rest = rest[2:2 + nw], rest[2 + nw:]
        kb, vb, tri_s, qm_s, z_s, split_s, zl_s, suf_s, a_s, run_s, tot_s, acc_s = rest[:12]
        gather_sems = rest[12:]
        del cat_in
        qi = pl.program_id(1)

        @pl.when((pl.program_id(0) == 0) & (qi == 0))
        def _():
            for cp in plan.copies(staged_refs, gathered_refs, gather_sems):
                cp.start()

        @pl.when(qi == 0)
        def _():
            kb[...] = k_ref[...].astype(BF16)
            vb[...] = v_ref[...].astype(BF16)
            tri_s[...] = _suffix_operator(t)

        low = _low_half()
        for c, half, msk in _chains(low):
            qm_s[c] = jnp.where(msk, q_ref[half * t:(half + 1) * t, :] * scale, 0.0).astype(BF16)
            run_s[c] = jnp.zeros((t, LANES), F32)
            acc_s[c] = jnp.zeros((t, LANES), F32)

        def key_rows(kblk):
            return pl.ds(pl.multiple_of(kblk * t, t), t)

        def key_block(base, c):
            return key_rows(jnp.maximum(base + c // 2, 0))

        def scores_matmul(base, chains):
            for c in chains:
                z_s[c] = lax.dot_general(qm_s[c], kb[key_block(base, c), :], _NT, preferred_element_type=F32)

        def front(modes, base, prev=None):
            for c, diag in modes:
                _sb_scores(z_s.at[c], split_s.at[c], zl_s.at[c], tot_s.at[c], None, t, diag)
                suf_s[c] = jnp.dot(split_s[c], tri_s[...], preferred_element_type=F32)
            if prev is not None:
                tail(*prev)
            scores_matmul(base - 1, ALL_CHAINS)
            for c, diag in modes:
                _sb_weights(zl_s.at[c], suf_s.at[c], run_s.at[c], tot_s.at[c], a_s.at[c], t, diag)

        def tail(base, chains):
            for c in chains:
                acc_s[c] += jnp.dot(a_s[c], vb[key_block(base, c), :], preferred_element_type=F32)

        first = 2 * qi
        scores_matmul(first, ALL_CHAINS)
        front([(c, True) for c in ALL_CHAINS], first)

        def loop(state):
            it = state[0]
            base = first - 1 - it
            front([(c, False) for c in ALL_CHAINS], base, prev=(base + 1, ALL_CHAINS))
            return it + 1, _any_weight_left(run_s)

        done, live = lax.while_loop(lambda state: (state[0] < first) & (state[1] > 0), loop,
                                    (jnp.int32(0), jnp.int32(1)))
        one_more = (done == first) & (live > 0)

        @pl.when(one_more)
        def _():
            front([(c, False) for c in UPPER_CHAINS], -1, prev=(0, ALL_CHAINS))
            tail(-1, UPPER_CHAINS)

        @pl.when(jnp.logical_not(one_more))
        def _():
            tail(first - done, ALL_CHAINS)

        for half in range(2):
            rows = slice(half * t, (half + 1) * t)
            o = jnp.where(low, acc_s[2 * half], acc_s[2 * half + 1])
            o_ref[rows, :] = o
            r = lax.rsqrt(_half_mean(o * o, low) + EPS)
            cat_ref[rows, :] = (o * r * g_ref[...]).astype(BF16)

        @pl.when((pl.program_id(0) == nh - 1) & (qi == nq - 1))
        def _():
            for cp in plan.copies(staged_refs, gathered_refs, gather_sems):
                cp.wait()

    whole = lambda col0: pl.BlockSpec((s, LANES), lambda h, i: (0, col0 + h))
    n_ch = len(ALL_CHAINS)
    res = _pcall(
        body, name="attn_fwd", grid=(nh, nq),
        in_specs=[pl.BlockSpec((tq, LANES), lambda h, i: (i, q0 + h)),
                  whole(q0 + nh), whole(q0 + 2 * nh),
                  pl.BlockSpec((1, LANES), lambda h, i: (0, h)),
                  pl.BlockSpec(memory_space=pl.ANY)] + [pl.BlockSpec(memory_space=pl.ANY)] * nw,
        out_specs=[pl.BlockSpec((tq, LANES), lambda h, i: (i, h)),
                   pl.BlockSpec((tq, LANES), lambda h, i: (i, w_conv // LANES + h))]
        + [pl.BlockSpec(memory_space=pl.ANY)] * nw,
        out_shape=[jax.ShapeDtypeStruct((s, w_attn), F32),
                   jax.ShapeDtypeStruct(cat.shape, BF16)] + plan.out_shapes,
        scratch_shapes=[pltpu.VMEM((s, LANES), BF16), pltpu.VMEM((s, LANES), BF16),
                        pltpu.VMEM((2 * t, t), BF16),
                        pltpu.VMEM((n_ch, t, LANES), BF16),
                        pltpu.VMEM((n_ch, t, t), F32),
                        pltpu.VMEM((n_ch, t, 2 * t), BF16),
                        pltpu.VMEM((n_ch, t, t), F32),
                        pltpu.VMEM((n_ch, t, t), F32),
                        pltpu.VMEM((n_ch, t, t), BF16),
                        pltpu.VMEM((n_ch, t, LANES), F32),
                        pltpu.VMEM((n_ch, t, LANES), F32),
                        pltpu.VMEM((n_ch, t, LANES), F32)]
        + plan.sems,
        input_output_aliases={4: 1, **plan.aliases(5, 2)},
        compiler_params=_params(("arbitrary", "arbitrary")),
    )(proj, proj, proj, g_attn, cat, *plan.inputs)
    return res[0], res[1], res[2:]


def _attn_bwd(proj, o, dcat, g_attn, w_conv, carry, t=ATTN_BLOCK):
    s, n_cols = proj.shape
    w_attn = g_attn.shape[1]
    nh = w_attn // LANES
    t = _tile(s, t)
    tq = 2 * t
    nq = s // tq
    q0 = 3 * w_conv // LANES
    scale = HEAD_DIM ** -0.5
    plan = _Carried(carry)
    nw = len(plan.inputs)

    def body(q_ref, k_ref, v_ref, o_ref, do_ref, g_ref, *rest):
        partial_refs, rest = rest[:nw], rest[nw:]
        dproj_ref, dg_ref = rest[:2]
        received_refs, rest = rest[2:2 + nw], rest[2 + nw:]
        (kb, vb, dkt_acc, dvt_acc, stash, tri_s, qm_s, dom_s, qt_s, dot_s, z_s, da_s, split_s, zl_s,
         keep_s, suf_s, a_s, glog_s, gsplit_s, cum_s, dz_s, run_s, tot_s, rest_s, gtot_s, dq_s) = rest[:26]
        out_sems, scatter_sems = rest[26], rest[27:]
        step_i = pl.program_id(1)
        qi = nq - 1 - step_i
        head_pair = pl.program_id(0)
        first_step = (head_pair == 0) & (step_i == 0)
        last_step = (head_pair == nh - 1) & (step_i == nq - 1)

        @pl.when(first_step)
        def _():
            for cp in plan.copies(partial_refs, received_refs, scatter_sems):
                cp.start()

        def out_copies():
            rows = pl.ds(pl.multiple_of(qi * tq, tq), tq)
            return [pltpu.make_async_copy(
                stash.at[w], dproj_ref.at[rows, pl.ds(pl.multiple_of((q0 + w * nh + head_pair) * LANES, LANES), LANES)],
                out_sems.at[w]) for w in range(3)]

        def walk():
            @pl.when(step_i == 0)
            def _():
                kb[...] = k_ref[...].astype(BF16)
                vb[...] = v_ref[...].astype(BF16)
                tri_s[...] = _suffix_operator(t)
                dkt_acc[...] = jnp.zeros_like(dkt_acc)
                dvt_acc[...] = jnp.zeros_like(dvt_acc)
                dg_ref[...] = jnp.zeros_like(dg_ref)

            low = _low_half()
            gv = g_ref[...]
            for half in range(2):
                rows = slice(half * t, (half + 1) * t)
                q = q_ref[rows, :] * scale
                ov = o_ref[rows, :]
                d_o, dgn = _head_norm_bwd(ov, do_ref[rows, :], gv, low)
                dg_ref[...] += jnp.sum(dgn, axis=0, keepdims=True)
                for h, msk in enumerate((low, jnp.logical_not(low))):
                    c = 2 * half + h
                    qh = jnp.where(msk, q, 0.0)
                    doh = jnp.where(msk, d_o, 0.0)
                    dom = doh.astype(BF16)
                    qm_s[c] = qh.astype(BF16)
                    dom_s[c] = dom
                    qt_s[c] = qh.T.astype(BF16)
                    dot_s[c] = doh.T.astype(BF16)
                    rest_s[c] = _row_sum(dom.astype(F32) * ov)
                    run_s[c] = jnp.zeros((t, LANES), F32)
                    dq_s[c] = jnp.zeros((t, LANES), F32)

            def key_rows(kblk):
                return pl.ds(pl.multiple_of(kblk * t, t), t)

            def block_of(base, half):
                return jnp.maximum(base + half, 0)

            def scores_matmul(base, chains):
                for c in chains:
                    ks = kb[key_rows(block_of(base, c // 2)), :]
                    z_s[c] = lax.dot_general(qm_s[c], ks, _NT, preferred_element_type=F32)

            def da_matmul(base, chains):
                for c in chains:
                    vs = vb[key_rows(block_of(base, c // 2)), :]
                    da_s[c] = lax.dot_general(dom_s[c], vs, _NT, preferred_element_type=F32)

            def front(modes, base, prev=None):
                if prev is not None:
                    tail(*prev)
                for c, diag in modes:
                    _sb_scores(z_s.at[c], split_s.at[c], zl_s.at[c], tot_s.at[c], keep_s.at[c], t, diag)
                    suf_s[c] = jnp.dot(split_s[c], tri_s[...], preferred_element_type=F32)
                scores_matmul(base - 1, ALL_CHAINS)
                for c, diag in modes:
                    _sb_weights(zl_s.at[c], suf_s.at[c], run_s.at[c], tot_s.at[c], a_s.at[c], t, diag,
                                da_s.at[c], glog_s.at[c], gsplit_s.at[c], gtot_s.at[c])
                    cum_s[c] = jnp.dot(gsplit_s[c], tri_s[...], preferred_element_type=F32)
                da_matmul(base - 1, ALL_CHAINS)
                for c, diag in modes:
                    _sb_dscores(glog_s.at[c], cum_s.at[c], rest_s.at[c], gtot_s.at[c], keep_s.at[c],
                                dz_s.at[c], t, diag)

            def tail(base, chains):
                for half in range(2):
                    mine = [c for c in chains if c // 2 == half]
                    if not mine:
                        continue
                    kblk = block_of(base, half)
                    ks = kb[key_rows(kblk), :]
                    dkt = dkt_acc[kblk]
                    dvt = dvt_acc[kblk]
                    for c in mine:
                        dq_s[c] += jnp.dot(dz_s[c], ks, preferred_element_type=F32)
                        dkt = dkt + jnp.dot(qt_s[c], dz_s[c], preferred_element_type=F32)
                        dvt = dvt + jnp.dot(dot_s[c], a_s[c], preferred_element_type=F32)
                    dkt_acc[kblk] = dkt
                    dvt_acc[kblk] = dvt

            first = 2 * qi
            scores_matmul(first, ALL_CHAINS)
            da_matmul(first, ALL_CHAINS)
            front([(c, True) for c in ALL_CHAINS], first)

            def loop(state):
                it = state[0]
                base = first - 1 - it
                front([(c, False) for c in ALL_CHAINS], base, prev=(base + 1, ALL_CHAINS))
                return it + 1, _any_weight_left(run_s)

            done, live = lax.while_loop(lambda state: (state[0] < first) & (state[1] > 0), loop,
                                        (jnp.int32(0), jnp.int32(1)))
            one_more = (done == first) & (live > 0)

            @pl.when(one_more)
            def _():
                front([(c, False) for c in UPPER_CHAINS], -1, prev=(0, ALL_CHAINS))
                tail(-1, UPPER_CHAINS)

            @pl.when(jnp.logical_not(one_more))
            def _():
                tail(first - done, ALL_CHAINS)

            @pl.when(jnp.logical_not(first_step))
            def _():
                for cp in out_copies():
                    cp.wait()

            for half in range(2):
                rows = slice(half * t, (half + 1) * t)
                stash[0, rows, :] = (jnp.where(low, dq_s[2 * half], dq_s[2 * half + 1]) * scale).astype(BF16)
                stash[1, rows, :] = dkt_acc[2 * qi + half].T.astype(BF16)
                stash[2, rows, :] = dvt_acc[2 * qi + half].T.astype(BF16)
            for cp in out_copies():
                cp.start()

        walk()

        @pl.when(last_step)
        def _():
            for cp in out_copies():
                cp.wait()
            for cp in plan.copies(partial_refs, received_refs, scatter_sems):
                cp.wait()

    whole = lambda col0: pl.BlockSpec((s, LANES), lambda h, i: (0, col0 + h))
    blk = lambda col0: pl.BlockSpec((tq, LANES), lambda h, i: (nq - 1 - i, col0 + h))
    n_ch = len(ALL_CHAINS)
    res = _pcall(
        body, name="attn_bwd", grid=(nh, nq),
        in_specs=[blk(q0), whole(q0 + nh), whole(q0 + 2 * nh), blk(0), blk(w_conv // LANES),
                  pl.BlockSpec((1, LANES), lambda h, i: (0, h))] + [pl.BlockSpec(memory_space=pl.ANY)] * nw,
        out_specs=[pl.BlockSpec(memory_space=pl.ANY),
                   pl.BlockSpec((1, LANES), lambda h, i: (0, h))] + [pl.BlockSpec(memory_space=pl.ANY)] * nw,
        out_shape=[jax.ShapeDtypeStruct((s, n_cols), BF16), jax.ShapeDtypeStruct((1, w_attn), F32)]
        + plan.out_shapes,
        scratch_shapes=[pltpu.VMEM((s, LANES), BF16), pltpu.VMEM((s, LANES), BF16),
                        pltpu.VMEM((s // t, LANES, t), F32),
                        pltpu.VMEM((s // t, LANES, t), F32),
                        pltpu.VMEM((3, tq, LANES), BF16),
                        pltpu.VMEM((2 * t, t), BF16),
                        pltpu.VMEM((n_ch, t, LANES), BF16),
                        pltpu.VMEM((n_ch, t, LANES), BF16),
                        pltpu.VMEM((n_ch, LANES, t), BF16),
                        pltpu.VMEM((n_ch, LANES, t), BF16),
                        pltpu.VMEM((n_ch, t, t), F32),
                        pltpu.VMEM((n_ch, t, t), F32),
                        pltpu.VMEM((n_ch, t, 2 * t), BF16),
                        pltpu.VMEM((n_ch, t, t), F32),
                        pltpu.VMEM((n_ch, t, t), F32),
                        pltpu.VMEM((n_ch, t, t), F32),
                        pltpu.VMEM((n_ch, t, t), BF16),
                        pltpu.VMEM((n_ch, t, t), F32),
                        pltpu.VMEM((n_ch, t, 2 * t), BF16),
                        pltpu.VMEM((n_ch, t, t), F32),
                        pltpu.VMEM((n_ch, t, t), BF16),
                        pltpu.VMEM((n_ch, t, LANES), F32),
                        pltpu.VMEM((n_ch, t, LANES), F32),
                        pltpu.VMEM((n_ch, t, LANES), F32),
                        pltpu.VMEM((n_ch, t, LANES), F32),
                        pltpu.VMEM((n_ch, t, LANES), F32),
                        pltpu.SemaphoreType.DMA((3,))]
        + plan.sems,
        input_output_aliases=plan.aliases(6, 2),
        compiler_params=_params(("arbitrary", "arbitrary")),
    )(proj, proj, proj, o, dcat, g_attn, *plan.inputs)
    return res[0], res[1], res[2:]


def _place():
    return lax.axis_index("x"), lax.axis_index("y"), lax.axis_index("c")


def _other_chips(x, y):
    return [(1 - x, y), (x, 1 - y), (1 - x, 1 - y)]


def _slot(px, py, pc):
    return 4 * px + 2 * py + pc


def _all_gather(shards, out_dtypes):
    nw = len(shards)

    def body(*refs):
        ins, outs, stage = refs[:nw], refs[nw:2 * nw], refs[2 * nw:3 * nw]
        send_sems, recv_sems, local_sems = refs[3 * nw:]
        x, y, c = _place()
        me, sibling = (x, y, c), (x, y, 1 - c)
        chips = _other_chips(x, y)

        def copy(w, k, block, to, src=None):
            dst = outs[w].at[_slot(*block)]
            return pltpu.make_async_remote_copy(
                src_ref=dst if src is None else src, dst_ref=dst,
                send_sem=send_sems.at[w * 7 + k], recv_sem=recv_sems.at[w * 7 + k],
                device_id=to, device_id_type=MESH)

        started = []
        local = []
        for w in range(nw):
            stage[w][...] = ins[w][...].astype(stage[w].dtype)
            cp = pltpu.make_async_copy(stage[w], outs[w].at[_slot(*me)], local_sems.at[w])
            cp.start()
            local.append(cp)
            started.append(copy(w, 0, me, sibling, src=stage[w]))
            started[-1].start()
            for j, chip in enumerate(chips):
                started.append(copy(w, 1 + j, me, (*chip, c), src=stage[w]))
                started[-1].start()
        for j, chip in enumerate(chips):
            for w in range(nw):
                copy(w, 1 + j, (*chip, c), me).wait_recv()
                started.append(copy(w, 4 + j, (*chip, c), sibling))
                started[-1].start()
        for w in range(nw):
            copy(w, 0, sibling, me).wait_recv()
            for j, chip in enumerate(chips):
                copy(w, 4 + j, (*chip, 1 - c), me).wait_recv()
        for cp in started:
            cp.wait_send()
        for cp in local:
            cp.wait()

    return _pcall(
        body, name="all_gather_weights",
        in_specs=[pl.BlockSpec(memory_space=pltpu.VMEM)] * nw,
        out_specs=[pl.BlockSpec(memory_space=pl.ANY)] * nw,
        out_shape=[jax.ShapeDtypeStruct((N_DEV, *a.shape), d) for a, d in zip(shards, out_dtypes)],
        scratch_shapes=[pltpu.VMEM(a.shape, d) for a, d in zip(shards, out_dtypes)]
        + [pltpu.SemaphoreType.DMA((7 * nw,)), pltpu.SemaphoreType.DMA((7 * nw,)),
           pltpu.SemaphoreType.DMA((nw,))],
        compiler_params=_params(),
    )(*shards)


N_PEERS = N_DEV - 1


def _peer(k):
    x, y, c = _place()
    return (x ^ (k >> 2), y ^ ((k >> 1) & 1), c ^ (k & 1))


def _remote(src, dst, sems, index, to):
    return pltpu.make_async_remote_copy(src_ref=src, dst_ref=dst, send_sem=sems[0].at[index],
                                        recv_sem=sems[1].at[index], device_id=to, device_id_type=MESH)


def _gather_out_copies(staged, gathered, sems):
    x, y, c = _place()
    me = _slot(x, y, c)
    targets = [(x, y, 1 - c)] + [(*chip, c) for chip in _other_chips(x, y)]
    copies = []
    for w, (src, dst) in enumerate(zip(staged, gathered)):
        copies.append(pltpu.make_async_copy(src, dst.at[me], sems[2].at[w]))
        copies += [_remote(src, dst.at[me], sems, w * len(targets) + k, to) for k, to in enumerate(targets)]
    return copies


def _gather_pass_copies(arrived, gathered, sems):
    x, y, c = _place()
    chips = _other_chips(x, y)
    return [_remote(src.at[_slot(*chip, c)], dst.at[_slot(*chip, c)], sems, w * len(chips) + j, (x, y, 1 - c))
            for w, (src, dst) in enumerate(zip(arrived, gathered)) for j, chip in enumerate(chips)]


def _scatter_copies(partials, received, sems):
    me = _slot(*_place())
    return [_remote(src.at[me ^ k], dst.at[k - 1], sems, w * N_PEERS + k - 1, _peer(k))
            for w, (src, dst) in enumerate(zip(partials, received)) for k in range(1, N_DEV)]


class _Carried:
    COPIES = {"gather_out": (_gather_out_copies, 4, True), "gather_pass": (_gather_pass_copies, 3, False),
              "scatter": (_scatter_copies, N_PEERS, False)}

    def __init__(self, jobs):
        self.jobs = [(kind, list(arrays)) for kind, arrays in jobs if len(arrays)]
        self.inputs = [a for _, arrays in self.jobs for a in arrays]
        self.out_shapes, self.sems, self.sem_counts = [], [], []
        for kind, arrays in self.jobs:
            _, fan, local = self.COPIES[kind]
            for a in arrays:
                shape = {"gather_out": (N_DEV, *a.shape), "gather_pass": a.shape,
                         "scatter": (N_PEERS, *a.shape[1:])}[kind]
                self.out_shapes.append(jax.ShapeDtypeStruct(shape, BF16))
            job_sems = [pltpu.SemaphoreType.DMA((fan * len(arrays),))] * 2
            job_sems += [pltpu.SemaphoreType.DMA((len(arrays),))] if local else []
            self.sems += job_sems
            self.sem_counts.append(len(job_sems))

    def aliases(self, first_input, first_output):
        pairs, at = {}, 0
        for kind, arrays in self.jobs:
            if kind == "gather_pass":
                pairs.update({first_input + at + i: first_output + at + i for i in range(len(arrays))})
            at += len(arrays)
        return pairs

    def copies(self, in_refs, out_refs, sem_refs):
        out, at, sem_at = [], 0, 0
        for (kind, arrays), n_sems in zip(self.jobs, self.sem_counts):
            n = len(arrays)
            out += self.COPIES[kind][0](in_refs[at:at + n], out_refs[at:at + n], sem_refs[sem_at:sem_at + n_sems])
            at, sem_at = at + n, sem_at + n_sems
        return out


def _cast_shards(shards):
    def body(*refs):
        for src, dst in zip(refs[:len(shards)], refs[len(shards):]):
            dst[...] = src[...].astype(BF16)

    return _pcall(
        body, name="cast_shards",
        in_specs=[pl.BlockSpec(memory_space=pltpu.VMEM)] * len(shards),
        out_specs=[pl.BlockSpec(memory_space=pltpu.VMEM)] * len(shards),
        out_shape=[jax.ShapeDtypeStruct(a.shape, BF16) for a in shards],
        compiler_params=_params(),
    )(*shards)


def _all_reduce_small(packed):
    r = packed.shape[0]

    def body(x_ref, o_ref, gathered, send_sems, recv_sems):
        x, y, c = _place()
        me = _slot(x, y, c)
        gathered[me] = x_ref[...]
        peers = [(px, py, pc) for px in range(2) for py in range(2) for pc in range(2)]
        started = []
        for k in range(1, N_DEV):
            to = (x ^ (k >> 2), y ^ ((k >> 1) & 1), c ^ (k & 1))
            cp = pltpu.make_async_remote_copy(
                src_ref=x_ref, dst_ref=gathered.at[me],
                send_sem=send_sems.at[k - 1], recv_sem=recv_sems.at[k - 1],
                device_id=to, device_id_type=MESH)
            cp.start()
            started.append(cp)
        del peers
        for cp in started:
            cp.wait()
        total = gathered[0]
        for k in range(1, N_DEV):
            total = total + gathered[k]
        o_ref[...] = total

    return _pcall(
        body, name="all_reduce_small",
        in_specs=[pl.BlockSpec(memory_space=pltpu.VMEM)],
        out_specs=pl.BlockSpec(memory_space=pltpu.VMEM),
        out_shape=jax.ShapeDtypeStruct(packed.shape, F32),
        scratch_shapes=[pltpu.VMEM((N_DEV, r, LANES), F32),
                        pltpu.SemaphoreType.DMA((N_DEV - 1,)), pltpu.SemaphoreType.DMA((N_DEV - 1,))],
        compiler_params=_params(),
    )(packed)


def _adam_math(w, g, m, v):
    m = ADAM_B1 * m + (1.0 - ADAM_B1) * g
    v = ADAM_B2 * v + (1.0 - ADAM_B2) * jnp.square(g)
    m_hat = m / (1.0 - ADAM_B1 ** ADAM_STEP)
    v_hat = v / (1.0 - ADAM_B2 ** ADAM_STEP)
    delta = -ADAM_LR * (m_hat / (jnp.sqrt(v_hat) + ADAM_EPS) + ADAM_WD * w)
    return delta, m, v


ADAM_TILE_BYTES = 24 * 1024 * 1024


def _adam_sharded(name, own, received, w, m, v, place):
    r, cdim = w.shape
    row_bytes = 2 * cdim * (4 + 2 * N_PEERS + 3 * 4 + 4 * 4)
    tr = _tile(r, max(LANES, ADAM_TILE_BYTES // row_bytes // LANES * LANES)) if r % LANES == 0 else r

    def body(place_ref, own_ref, rec_ref, w_ref, m_ref, v_ref, g_ref, d_ref, nm_ref, nv_ref):
        del place_ref
        g = own_ref[...]
        for j in range(N_PEERS):
            g = g + rec_ref[j].astype(F32)
        delta, nm, nv = _adam_math(w_ref[...], g, m_ref[...], v_ref[...])
        g_ref[...] = g
        d_ref[...] = delta
        nm_ref[...] = nm
        nv_ref[...] = nv

    blk = pl.BlockSpec((tr, cdim), lambda i, pr: (i, 0))
    grid_spec = pltpu.PrefetchScalarGridSpec(
        num_scalar_prefetch=1, grid=(r // tr,),
        in_specs=[pl.BlockSpec((None, tr, cdim), lambda i, pr: (4 * pr[0] + 2 * pr[1] + pr[2], i, 0)),
                  pl.BlockSpec((N_PEERS, tr, cdim), lambda i, pr: (0, i, 0)), blk, blk, blk],
        out_specs=[blk] * 4)
    return _pcall(body, name=name, grid_spec=grid_spec,
                  out_shape=[jax.ShapeDtypeStruct((r, cdim), F32)] * 4,
                  compiler_params=_params(("parallel",)))(place, own, received, w, m, v)


def _adam_small(w, g, m, v):
    def body(w_ref, g_ref, m_ref, v_ref, d_ref, nm_ref, nv_ref):
        delta, nm, nv = _adam_math(w_ref[...], g_ref[...], m_ref[...], v_ref[...])
        d_ref[...] = delta
        nm_ref[...] = nm
        nv_ref[...] = nv

    return _pcall(body, name="adam_small",
                  in_specs=[pl.BlockSpec(memory_space=pltpu.VMEM)] * 4,
                  out_specs=[pl.BlockSpec(memory_space=pltpu.VMEM)] * 3,
                  out_shape=[jax.ShapeDtypeStruct(w.shape, F32)] * 3,
                  compiler_params=_params())(w, g, m, v)


def _rows(vec):
    return vec.reshape(-1, LANES)


def kernel(x, p, g_mix, w_in, conv_w, g_conv_out, g_attn_out, w_out, g_mlp, w_up, w_down, g_ple, w_ple_gate, w_ple_proj, g_final, loss_target, m_g_mix, m_w_in, m_conv_w, m_g_conv_out, m_g_attn_out, m_w_out, m_g_mlp, m_w_up, m_w_down, m_g_ple, m_w_ple_gate, m_w_ple_proj, m_g_final, v_g_mix, v_w_in, v_conv_w, v_g_conv_out, v_g_attn_out, v_w_out, v_g_mlp, v_w_up, v_w_down, v_g_ple, v_w_ple_gate, v_w_ple_proj, v_g_final):
    s, d = x.shape[1], x.shape[2]
    w_conv = g_conv_out.shape[1]
    w_attn = g_attn_out.shape[1]
    cw = conv_w.shape[2]
    xs, ps, tgt = x[0], p[0, 0], loss_target[0]
    place = jnp.stack([lax.axis_index("x"), lax.axis_index("y"), lax.axis_index("c")]).astype(jnp.int32)
    my_slot = 4 * place[0] + 2 * place[1] + place[2]

    conv_tile = jnp.pad(conv_w[0], ((0, HALO - CONV_K), (0, LANES - cw)))
    big = [w_in[0], w_out[0], w_up[0], w_down[0], w_ple_gate[0], w_ple_proj[0]]
    win_g, conv_g = _all_gather([big[0], conv_tile], [BF16, F32])
    s_out, s_up, s_down, s_gate, s_proj = _cast_shards(big[1:])
    conv_full = jnp.transpose(conv_g[:, :CONV_K, :cw], (1, 0, 2)).reshape(CONV_K, w_conv)
    in_shard, up_shard, proj_shard = big[0].shape[1], big[2].shape[1], big[5].shape[1]

    proj, a, g_out, g_gate, g_proj = _mm_nn("in_proj", xs, win_g, n_shard=in_shard, tn=in_shard, tm=2048,
                                            lhs_norm=g_mix, carry=[("gather_out", [s_out, s_gate, s_proj])])
    cat = _conv_fwd(proj, conv_full, g_conv_out, w_conv, d)
    o, cat, (g_up, g_down, wout_g, wgate_g, wproj_g) = _attn_fwd(
        proj, g_attn_out, cat, w_conv,
        [("gather_out", [s_up, s_down]), ("gather_pass", [g_out, g_gate, g_proj])])
    wout_f = wout_g.reshape(-1, wout_g.shape[-1])
    wgate_f = wgate_g.reshape(-1, wgate_g.shape[-1])
    h1, wup_g = _mm_nn("out_proj", cat, wout_f, epilogue=_ep_residual, extras=(xs,),
                       carry=[("gather_pass", [g_up])])
    act, mn, wdown_g = _mm_nn("mlp_up", h1, wup_g, n_shard=up_shard, epilogue=_ep_up, out_dtypes=(BF16,), tm=2048,
                              lhs_norm=g_mlp, carry=[("gather_pass", [g_down])])
    wdown_f = wdown_g.reshape(-1, wdown_g.shape[-1])
    h2, = _mm_nn("mlp_down", act, wdown_f, epilogue=_ep_residual, extras=(h1,))
    gl, n3 = _mm_nn("ple_gate", h2, wgate_f, lhs_norm=g_ple)
    pp = _ple_proj(ps, wproj_g)
    loss_part, dh3, dgl, dpp, dg_final = _ple_loss(h2, gl, pp, tgt, g_final.reshape(1, d))

    def slots(t2d):
        return t2d.reshape(N_DEV, -1, t2d.shape[-1])

    dw_proj = _d_ple_proj(ps, dpp, proj_shard)
    dw_gate = [slots(t) for t in _mm_tn("d_w_ple_gate", n3, dgl)]
    dh2, dh2b, dg_ple = _mm_nt_norm_bwd("d_norm_ple", dgl, wgate_f, h2, g_ple, dh3)
    du, gate_recv, proj_recv = _mm_nt("d_mlp_act", dh2b, wdown_f, epilogue=_ep_dact, out_dtypes=(BF16,),
                                      extras=(act,), tm=2048, carry=[("scatter", [dw_gate[1], dw_proj[1]])])
    dw_down = [slots(t) for t in _mm_tn("d_w_down", act, dh2b)]
    dw_up = _mm_tn("d_w_up", mn, du, n_shard=up_shard)
    dh1, dh1b, dg_mlp = _mm_nt_norm_bwd("d_norm_mlp", du, wup_g, h1, g_mlp, dh2, k_shard=up_shard, tm=1024)
    dcat, = _mm_nt("d_cat", dh1b, wout_f)
    dw_out = [slots(t) for t in _mm_tn("d_w_out", cat, dh1b)]
    dproj, dg_attn, (up_recv, down_recv) = _attn_bwd(proj, o, dcat, g_attn_out, w_conv,
                                                     [("scatter", [dw_up[1], dw_down[1]])])
    dproj, dconv, dg_conv = _conv_bwd(proj, dcat, conv_full, g_conv_out, dproj, w_conv)
    *dw_in, out_recv = _mm_tn("d_w_in", a, dproj, n_shard=in_shard, tn=in_shard,
                              carry=[("scatter", [dw_out[1]])])
    grad_x, _, dg_mix, in_recv = _mm_nt_norm_bwd("d_norm_mix", dproj, win_g, xs, g_mix, dh1, k_shard=in_shard,
                                                 tk=2 * in_shard, tm=1024, carry=[("scatter", [dw_in[1]])])

    names = ["w_in", "w_out", "w_up", "w_down", "w_ple_gate", "w_ple_proj"]
    owns = [dw_in[0], dw_out[0], dw_up[0], dw_down[0], dw_gate[0], dw_proj[0]]
    recvs = [in_recv, out_recv, up_recv, down_recv, gate_recv, proj_recv]
    moments = [(m_w_in, v_w_in), (m_w_out, v_w_out), (m_w_up, v_w_up), (m_w_down, v_w_down),
               (m_w_ple_gate, v_w_ple_gate), (m_w_ple_proj, v_w_ple_proj)]
    big_out = {}
    for n, own, rc, wt, (mm, vv) in zip(names, owns, recvs, big, moments):
        big_out[n] = [t[None] for t in _adam_sharded("adam_" + n, own, rc, wt, mm[0], vv[0], place)]

    n_conv_rows = CONV_K * w_conv // LANES
    small_g = jnp.concatenate(
        [_rows(dg_mix[0]), _rows(dg_conv[0]), _rows(dg_attn[0]), _rows(dg_mlp[0]), _rows(dg_ple[0]),
         _rows(dg_final[0]), _rows(dconv.reshape(-1)), loss_part], axis=0)
    n_gain_rows = small_g.shape[0] - n_conv_rows - 1
    pad_rows = (-small_g.shape[0]) % HALO
    small_g = _all_reduce_small(jnp.pad(small_g, ((0, pad_rows), (0, 0))))
    loss = small_g[n_gain_rows + n_conv_rows, 0]
    dconv_full = small_g[n_gain_rows:n_gain_rows + n_conv_rows].reshape(CONV_K, w_conv)
    dconv_mine = lax.dynamic_slice(dconv_full, (0, my_slot * cw), (CONV_K, cw))

    def pack(vecs, conv_part):
        rows = [_rows(t.reshape(-1)) for t in vecs]
        rows.append(jnp.pad(conv_part, ((0, HALO - CONV_K), (0, LANES - cw))))
        return jnp.concatenate(rows, axis=0)

    gains = [g_mix, g_conv_out, g_attn_out, g_mlp, g_ple, g_final]
    gains_m = [m_g_mix, m_g_conv_out, m_g_attn_out, m_g_mlp, m_g_ple, m_g_final]
    gains_v = [v_g_mix, v_g_conv_out, v_g_attn_out, v_g_mlp, v_g_ple, v_g_final]
    gpack = jnp.concatenate([small_g[:n_gain_rows], jnp.pad(dconv_mine, ((0, HALO - CONV_K), (0, LANES - cw)))], axis=0)
    sd, sm, sv = _adam_small(pack(gains, conv_w[0]), gpack, pack(gains_m, m_conv_w[0]), pack(gains_v, v_conv_w[0]))

    def unpack(packed):
        out, r0 = [], 0
        for t in gains:
            nr = t.size // LANES
            out.append(packed[r0:r0 + nr].reshape(t.shape))
            r0 += nr
        out.append(packed[r0:r0 + CONV_K, :cw][None])
        return out

    sg_l, sd_l, sm_l, sv_l = unpack(gpack), unpack(sd), unpack(sm), unpack(sv)
    small_names = ["g_mix", "g_conv_out", "g_attn_out", "g_mlp", "g_ple", "g_final", "conv_w"]
    small_out = {n: [sg_l[i], sd_l[i], sm_l[i], sv_l[i]] for i, n in enumerate(small_names)}

    order = ["g_mix", "w_in", "conv_w", "g_conv_out", "g_attn_out", "w_out", "g_mlp", "w_up", "w_down",
             "g_ple", "w_ple_gate", "w_ple_proj", "g_final"]
    table = {**big_out, **small_out}
    outs = [loss, grad_x[None]]
    for kind in range(4):
        outs.extend(table[n][kind] for n in order)
    return tuple(outs)
```

```python
import jax
import jax.numpy as jnp
from jax import lax
from jax.experimental import pallas as pl
from jax.experimental.pallas import tpu as pltpu

F32 = jnp.float32
BF16 = jnp.bfloat16
EPS = 1e-6
HEAD_DIM = 64
LANES = 128
CONV_K = 3
MXU_WIDTH = 256
ATTN_BLOCK = MXU_WIDTH
HALO = 8
N_DEV = 8
MESH = pl.DeviceIdType.MESH
VMEM_LIMIT = 56 * 1024 * 1024

ADAM_LR = 0.001
ADAM_B1 = 0.9
ADAM_B2 = 0.999
ADAM_EPS = 1e-08
ADAM_WD = 0.01
ADAM_STEP = 10


def _pcall(body, **kw):
    return pl.pallas_call(body, **kw)


def _params(sem=None, **kw):
    return pltpu.CompilerParams(dimension_semantics=sem, vmem_limit_bytes=VMEM_LIMIT, **kw)


def _tile(dim, pref):
    t = min(dim, pref)
    while dim % t:
        t -= LANES
    assert t > 0, (dim, pref)
    return t


_NN = (((1,), (0,)), ((), ()))
_NT = (((1,), (1,)), ((), ()))
_TN = (((0,), (0,)), ((), ()))


def _ep_store(acc, outs):
    outs[0][...] = acc.astype(outs[0].dtype)


def _ep_both(acc, outs):
    outs[0][...] = acc
    outs[1][...] = acc.astype(BF16)


def _ep_residual(acc, res, outs):
    outs[0][...] = acc + res[...]


def _ep_up(acc, outs):
    outs[0][...] = jnp.square(jnp.maximum(acc, 0.0)).astype(BF16)


def _ep_dact(acc, act, outs):
    outs[0][...] = (acc * (2.0 * jnp.sqrt(act[...].astype(F32)))).astype(BF16)


def _ep_norm_bwd(acc, h, g, dres, outs):
    @pl.when(pl.program_id(0) == 0)
    def _():
        outs[2][...] = jnp.zeros_like(outs[2])

    hv = h[...]
    r = lax.rsqrt(jnp.mean(hv * hv, axis=-1, keepdims=True) + EPS)
    hn = hv * r
    outs[2][...] += jnp.sum(acc * hn, axis=0, keepdims=True)
    dhn = acc * g[...]
    dh = dres[...] + r * (dhn - hn * jnp.mean(dhn * hn, axis=-1, keepdims=True))
    outs[0][...] = dh
    outs[1][...] = dh.astype(BF16)


def _matmul(name, a, b, *, dims, grid, a_spec, b_spec, acc_shape, out_shapes, out_specs,
            epilogue=_ep_store, extras=(), extra_specs=(), carry=(), sequential=False, lhs_norm=False):
    nk = grid[2]
    plan = _Carried(carry)
    n_ex, n_out, n_xc = len(extras), len(out_shapes), len(plan.inputs)
    n_sems = len(plan.sems)
    last = tuple(g - 1 for g in grid)
    assert not lhs_norm or nk == 1

    def product(a_ref, b_ref):
        if len(b_ref.shape) == 2:
            return lax.dot_general(a_ref[...].astype(BF16), b_ref[...].astype(BF16), dims,
                                   preferred_element_type=F32)
        width = b_ref.shape[2]
        return sum(lax.dot_general(a_ref[:, g * width:(g + 1) * width].astype(BF16), b_ref[g].astype(BF16), dims,
                                   preferred_element_type=F32) for g in range(b_ref.shape[0]))

    def body(a_ref, b_ref, *rest):
        ex, rest = rest[:n_ex], rest[n_ex:]
        partials, rest = rest[:n_xc], rest[n_xc:]
        outs, rest = rest[:n_out], rest[n_out:]
        received, rest = rest[:n_xc], rest[n_xc:]
        ids = [pl.program_id(axis) for axis in range(3)]
        if n_xc:
            @pl.when((ids[0] == 0) & (ids[1] == 0) & (ids[2] == 0))
            def _():
                for cp in plan.copies(partials, received, rest[-n_sems:]):
                    cp.start()

        if lhs_norm:
            x_ref, a_ref, gain, ex, outs = a_ref, outs[-1], ex[-1], ex[:-1], outs[:-1]

            @pl.when(ids[1] == 0)
            def _():
                for m0 in range(0, acc_shape[0], MXU_WIDTH):
                    rows = slice(m0, min(m0 + MXU_WIDTH, acc_shape[0]))
                    xv = x_ref[rows, :]
                    r = lax.rsqrt(jnp.mean(xv * xv, axis=-1, keepdims=True) + EPS)
                    a_ref[rows, :] = (xv * r * gain[...]).astype(BF16)

        if nk == 1 and not sequential and dims != _TN:
            for n0 in range(0, acc_shape[1], MXU_WIDTH):
                cols = slice(n0, min(n0 + MXU_WIDTH, acc_shape[1]))
                b_cols = b_ref.at[cols, :] if dims == _NT else b_ref.at[:, cols]
                for m0 in range(0, acc_shape[0], MXU_WIDTH):
                    rows = slice(m0, min(m0 + MXU_WIDTH, acc_shape[0]))
                    epilogue(product(a_ref.at[rows, :], b_cols), *[e.at[rows, cols] for e in ex],
                             [o.at[rows, cols] for o in outs])
        elif nk == 1:
            epilogue(product(a_ref, b_ref), *ex, outs)
        else:
            acc = rest[0]

            @pl.when(ids[2] == 0)
            def _():
                acc[...] = product(a_ref, b_ref)

            @pl.when(ids[2] > 0)
            def _():
                acc[...] += product(a_ref, b_ref)

            @pl.when(ids[2] == nk - 1)
            def _():
                epilogue(acc[...], *ex, outs)

        if n_xc:
            @pl.when((ids[0] == last[0]) & (ids[1] == last[1]) & (ids[2] == last[2]))
            def _():
                for cp in plan.copies(partials, received, rest[-n_sems:]):
                    cp.wait()

    anywhere = [pl.BlockSpec(memory_space=pl.ANY)] * n_xc
    return _pcall(
        body, name=name, grid=grid,
        in_specs=[a_spec, b_spec, *extra_specs, *anywhere],
        out_specs=[*out_specs, *anywhere],
        out_shape=[*out_shapes, *plan.out_shapes],
        scratch_shapes=([] if nk == 1 else [pltpu.VMEM(acc_shape, F32)]) + plan.sems,
        input_output_aliases=plan.aliases(2 + n_ex, n_out),
        compiler_params=_params(("arbitrary",) * 3 if n_xc or sequential or lhs_norm
                                else ("parallel", "parallel", "arbitrary")),
    )(a, b, *extras, *plan.inputs)


_NO_CARRY = ()


def _mm_nn(name, a, w, *, n_shard=None, epilogue=_ep_store, out_dtypes=(F32,), extras=(), carry=_NO_CARRY,
           lhs_norm=None, tm=1024, tn=1024, tk=1024):
    m, kd = a.shape
    if lhs_norm is not None:
        tk = kd
    if n_shard is None:
        n = w.shape[1]
        tn = _tile(n, tn)
        tk = _tile(kd, tk)
        b_spec = pl.BlockSpec((tk, tn), lambda i, j, k: (k, j))
    else:
        n = N_DEV * n_shard
        tn = _tile(n_shard, tn)
        tk = _tile(kd, tk)
        per = n_shard // tn
        b_spec = pl.BlockSpec((None, tk, tn), lambda i, j, k: (j // per, k, j % per))
    tm = _tile(m, tm)
    o_spec = pl.BlockSpec((tm, tn), lambda i, j, k: (i, j))
    out_shapes = [jax.ShapeDtypeStruct((m, n), d) for d in out_dtypes]
    out_specs = [o_spec] * len(out_dtypes)
    extra_specs = [o_spec] * len(extras)
    if lhs_norm is not None:
        extras = (*extras, lhs_norm)
        extra_specs.append(pl.BlockSpec((1, kd), lambda i, j, k: (0, 0)))
        out_shapes.append(jax.ShapeDtypeStruct((m, kd), BF16))
        out_specs.append(pl.BlockSpec((tm, kd), lambda i, j, k: (i, 0)))
    return _matmul(
        name, a, w, dims=_NN, grid=(m // tm, n // tn, kd // tk),
        a_spec=pl.BlockSpec((tm, tk), lambda i, j, k: (i, k)), b_spec=b_spec,
        acc_shape=(tm, tn), out_shapes=out_shapes, out_specs=out_specs,
        epilogue=epilogue, extras=extras, extra_specs=extra_specs, carry=carry, lhs_norm=lhs_norm is not None)


def _mm_nt(name, a, w, *, epilogue=_ep_store, out_dtypes=(F32,), extras=(), carry=_NO_CARRY,
           tm=1024, tn=1024, tk=1024):
    m, kd = a.shape
    n = w.shape[0]
    tm, tn, tk = _tile(m, tm), _tile(n, tn), _tile(kd, tk)
    o_spec = pl.BlockSpec((tm, tn), lambda i, j, k: (i, j))
    return _matmul(
        name, a, w, dims=_NT, grid=(m // tm, n // tn, kd // tk),
        a_spec=pl.BlockSpec((tm, tk), lambda i, j, k: (i, k)),
        b_spec=pl.BlockSpec((tn, tk), lambda i, j, k: (j, k)),
        acc_shape=(tm, tn),
        out_shapes=[jax.ShapeDtypeStruct((m, n), d) for d in out_dtypes],
        out_specs=[o_spec] * len(out_dtypes),
        epilogue=epilogue, extras=extras, extra_specs=[o_spec] * len(extras), carry=carry)


def _mm_nt_norm_bwd(name, a, w, h, g, dres, *, k_shard=None, carry=_NO_CARRY, tm=512, tk=1024):
    m, kd = a.shape
    n = h.shape[1]
    if k_shard is None:
        tk = _tile(kd, tk)
        b_spec = pl.BlockSpec((n, tk), lambda i, j, k: (0, k))
    else:
        group = max(1, min(tk // k_shard, N_DEV))
        while N_DEV % group:
            group -= 1
        tk = group * k_shard
        b_spec = pl.BlockSpec((group, n, k_shard), lambda i, j, k: (k, 0, 0))
    tm = _tile(m, tm)
    rows = pl.BlockSpec((tm, n), lambda i, j, k: (i, 0))
    vec = pl.BlockSpec((1, n), lambda i, j, k: (0, 0))
    return _matmul(
        name, a, w, dims=_NT, grid=(m // tm, 1, kd // tk),
        a_spec=pl.BlockSpec((tm, tk), lambda i, j, k: (i, k)), b_spec=b_spec, acc_shape=(tm, n),
        out_shapes=[jax.ShapeDtypeStruct((m, n), F32), jax.ShapeDtypeStruct((m, n), BF16),
                    jax.ShapeDtypeStruct((1, n), F32)],
        out_specs=[rows, rows, vec], epilogue=_ep_norm_bwd,
        extras=(h, g, dres), extra_specs=[rows, vec, rows], carry=carry, sequential=True)


TN_TILE_BYTES = 40 * 1024 * 1024


def _mm_tn(name, a, b, *, n_shard=None, carry=_NO_CARRY, tm=1024, tn=1024):
    t, m = a.shape
    n = b.shape[1]
    tm = _tile(m, tm)
    tn = _tile(n if n_shard is None else n_shard, tn)
    tk = t
    while 2 * 2 * tk * (tm + tn) + 4 * tm * tn * 5 > TN_TILE_BYTES and tk % (2 * LANES) == 0:
        tk //= 2
    if n_shard is None:
        o_spec = pl.BlockSpec((tm, tn), lambda i, j, k: (i, j))
        shape = (m, n)
    else:
        per = n_shard // tn
        o_spec = pl.BlockSpec((None, tm, tn), lambda i, j, k: (j // per, i, j % per))
        shape = (N_DEV, m, n_shard)
    return _matmul(
        name, a, b, dims=_TN, grid=(m // tm, n // tn, t // tk),
        a_spec=pl.BlockSpec((tk, tm), lambda i, j, k: (k, i)),
        b_spec=pl.BlockSpec((tk, tn), lambda i, j, k: (k, j)),
        acc_shape=(tm, tn), epilogue=_ep_both, carry=carry,
        out_shapes=[jax.ShapeDtypeStruct(shape, F32), jax.ShapeDtypeStruct(shape, BF16)],
        out_specs=[o_spec, o_spec])


def _ple_proj(p, w_g, tm=1024):
    s, kd = p.shape
    ns = w_g.shape[2]
    tm = _tile(s, tm)

    def body(p_ref, w_ref, o_ref):
        pv = p_ref[...].astype(BF16)
        for j in range(N_DEV):
            o_ref[:, j * ns:(j + 1) * ns] = jnp.dot(pv, w_ref[j], preferred_element_type=F32)

    return _pcall(body, name="ple_proj", grid=(s // tm,),
                  in_specs=[pl.BlockSpec((tm, kd), lambda i: (i, 0)),
                            pl.BlockSpec((N_DEV, kd, ns), lambda i: (0, 0, 0))],
                  out_specs=pl.BlockSpec((tm, N_DEV * ns), lambda i: (i, 0)),
                  out_shape=jax.ShapeDtypeStruct((s, N_DEV * ns), F32),
                  compiler_params=_params(("parallel",)))(p, w_g)


def _d_ple_proj(p, dpp, ns, tk=1024):
    s, kd = p.shape
    tk = _tile(s, tk)
    nk = s // tk

    def body(p_ref, d_ref, of_ref, ob_ref, acc):
        k = pl.program_id(0)

        @pl.when(k == 0)
        def _():
            acc[...] = jnp.zeros_like(acc)

        pv = p_ref[...].astype(BF16)
        for j in range(N_DEV):
            acc[j] += lax.dot_general(pv, d_ref[:, j * ns:(j + 1) * ns], _TN, preferred_element_type=F32)

        @pl.when(k == nk - 1)
        def _():
            of_ref[...] = acc[...]
            ob_ref[...] = acc[...].astype(BF16)

    whole = pl.BlockSpec((N_DEV, kd, ns), lambda k: (0, 0, 0))
    return _pcall(body, name="d_w_ple_proj", grid=(nk,),
                  in_specs=[pl.BlockSpec((tk, kd), lambda k: (k, 0)),
                            pl.BlockSpec((tk, N_DEV * ns), lambda k: (k, 0))],
                  out_specs=[whole, whole],
                  out_shape=[jax.ShapeDtypeStruct((N_DEV, kd, ns), F32), jax.ShapeDtypeStruct((N_DEV, kd, ns), BF16)],
                  scratch_shapes=[pltpu.VMEM((N_DEV, kd, ns), F32)],
                  compiler_params=_params(("arbitrary",)))(p, dpp)


def _ep_ple_loss(gl, h2, pp, tgt, g_final, outs):
    loss_ref, dh3_ref, dgl_ref, dpp_ref, dg_ref = outs

    @pl.when(pl.program_id(0) == 0)
    def _():
        dg_ref[...] = jnp.zeros_like(dg_ref)
        loss_ref[...] = jnp.zeros_like(loss_ref)

    gate = jax.nn.sigmoid(gl)
    ppv = pp[...]
    h3 = h2[...] + gate * ppv
    r = lax.rsqrt(jnp.mean(h3 * h3, axis=-1, keepdims=True) + EPS)
    hn = h3 * r
    gv = g_final[...]
    diff = hn * gv - tgt[...]
    row = jnp.mean(diff * diff, axis=-1, keepdims=True)
    loss_ref[...] += 0.5 * jnp.sum(row, axis=0, keepdims=True)
    dy = diff * (1.0 / h3.shape[-1])
    dg_ref[...] += jnp.sum(dy * hn, axis=0, keepdims=True)
    dhn = dy * gv
    dh3 = r * (dhn - hn * jnp.mean(dhn * hn, axis=-1, keepdims=True))
    dh3_ref[...] = dh3
    dgl_ref[...] = (dh3 * ppv * gate * (1.0 - gate)).astype(BF16)
    dpp_ref[...] = (dh3 * gate).astype(BF16)


def _ple_gate_loss(h2, g_ple, w_gate, pp, tgt, g_final, tm=512):
    s, d = h2.shape
    tm = _tile(s, tm)
    rows = pl.BlockSpec((tm, d), lambda i, j, k: (i, 0))
    vec = pl.BlockSpec((1, d), lambda i, j, k: (0, 0))
    return _matmul(
        "ple_gate_loss", h2, w_gate, dims=_NN, grid=(s // tm, 1, 1),
        a_spec=rows, b_spec=pl.BlockSpec((d, d), lambda i, j, k: (0, 0)), acc_shape=(tm, d),
        out_shapes=[jax.ShapeDtypeStruct((1, LANES), F32), jax.ShapeDtypeStruct((s, d), F32),
                    jax.ShapeDtypeStruct((s, d), BF16), jax.ShapeDtypeStruct((s, d), BF16),
                    jax.ShapeDtypeStruct((1, d), F32), jax.ShapeDtypeStruct((s, d), BF16)],
        out_specs=[pl.BlockSpec((1, LANES), lambda i, j, k: (0, 0)), rows, rows, rows, vec, rows],
        epilogue=_ep_ple_loss, extras=(h2, pp, tgt, g_final, g_ple), extra_specs=[rows, rows, rows, vec, vec],
        sequential=True, lhs_norm=True)


def _low_half():
    return lax.broadcasted_iota(jnp.int32, (1, LANES), 1) < HEAD_DIM


def _half_mean(v, low):
    s_lo = jnp.sum(jnp.where(low, v, 0.0), axis=-1, keepdims=True)
    s_hi = jnp.sum(jnp.where(low, 0.0, v), axis=-1, keepdims=True)
    return jnp.where(low, s_lo, s_hi) * (1.0 / HEAD_DIM)


def _head_norm_bwd(val, dout, g, low):
    r = lax.rsqrt(_half_mean(val * val, low) + EPS)
    vn = val * r
    dvn = dout * g
    return r * (dvn - vn * _half_mean(dvn * vn, low)), dout * vn


def _conv_taps(vv_ext, w_ref):
    v0 = vv_ext[HALO:]
    v1 = pltpu.roll(vv_ext, 1, 0)[HALO:]
    v2 = pltpu.roll(vv_ext, 2, 0)[HALO:]
    return w_ref[2:3, :] * v0 + w_ref[1:2, :] * v1 + w_ref[0:1, :] * v2, (v0, v1, v2)


def _conv_fwd(proj, conv_w, g_conv, w_conv, d_model, tr=1024):
    s = proj.shape[0]
    tr = _tile(s, tr)
    hb = tr // HALO

    def main(part):
        return pl.BlockSpec((tr, w_conv), lambda i: (i, part))

    def prev(part):
        return pl.BlockSpec((HALO, w_conv), lambda i: (jnp.maximum(i * hb - 1, 0), part))

    def body(cb_ref, cc_ref, cu_ref, ccp_ref, cup_ref, w_ref, g_ref, o_ref):
        i = pl.program_id(0)
        low = _low_half()
        for j in range(w_conv // LANES):
            cols = slice(j * LANES, (j + 1) * LANES)
            vv_prev = jnp.where(i > 0, ccp_ref[:, cols] * cup_ref[:, cols], 0.0)
            vv_ext = jnp.concatenate([vv_prev, cc_ref[:, cols] * cu_ref[:, cols]], axis=0)
            y, _ = _conv_taps(vv_ext, w_ref.at[:, cols])
            co = cb_ref[:, cols] * y
            r = lax.rsqrt(_half_mean(co * co, low) + EPS)
            o_ref[:, cols] = (co * r * g_ref[:, cols]).astype(BF16)

    return _pcall(
        body, name="conv_fwd", grid=(s // tr,),
        in_specs=[main(0), main(1), main(2), prev(1), prev(2),
                  pl.BlockSpec((CONV_K, w_conv), lambda i: (0, 0)),
                  pl.BlockSpec((1, w_conv), lambda i: (0, 0))],
        out_specs=pl.BlockSpec((tr, w_conv), lambda i: (i, 0)),
        out_shape=jax.ShapeDtypeStruct((s, d_model), BF16),
        compiler_params=_params(("parallel",)),
    )(proj, proj, proj, proj, proj, conv_w, g_conv)


def _conv_bwd(proj, dcat, conv_w, g_conv, dproj, w_conv, tr=1024):
    s = proj.shape[0]
    tr = _tile(s, tr)
    hb = tr // HALO
    last = s // HALO - 1
    nt = s // tr

    def main(part):
        return pl.BlockSpec((tr, w_conv), lambda i: (i, part))

    def prev(part):
        return pl.BlockSpec((HALO, w_conv), lambda i: (jnp.maximum(i * hb - 1, 0), part))

    def nxt(part):
        return pl.BlockSpec((HALO, w_conv), lambda i: (jnp.minimum((i + 1) * hb, last), part))

    def body(cb_ref, cc_ref, cu_ref, dc_ref, ccp_ref, cup_ref, cbn_ref, ccn_ref, cun_ref, dcn_ref,
             w_ref, g_ref, dproj_in, dproj_ref, dw_ref, dg_ref):
        del dproj_in
        i = pl.program_id(0)

        @pl.when(i == 0)
        def _():
            dw_ref[...] = jnp.zeros_like(dw_ref)
            dg_ref[...] = jnp.zeros_like(dg_ref)

        low = _low_half()
        n_ext = tr + HALO
        rowid = lax.broadcasted_iota(jnp.int32, (n_ext, 1), 0)
        for j in range(w_conv // LANES):
            cols = slice(j * LANES, (j + 1) * LANES)
            wj = w_ref.at[:, cols]
            cc, cu = cc_ref[:, cols], cu_ref[:, cols]
            vv_prev = jnp.where(i > 0, ccp_ref[:, cols] * cup_ref[:, cols], 0.0)
            vv_ext = jnp.concatenate([vv_prev, cc * cu, ccn_ref[:, cols] * cun_ref[:, cols]], axis=0)
            y_ext, (v0, v1, v2) = _conv_taps(vv_ext, wj)
            cb_ext = jnp.concatenate([cb_ref[:, cols], cbn_ref[:, cols]], axis=0)
            dc_ext = jnp.concatenate([dc_ref[:, cols], dcn_ref[:, cols]], axis=0)
            dco, dgn = _head_norm_bwd(cb_ext * y_ext, dc_ext, g_ref[:, cols], low)
            dyc = jnp.where((rowid < tr) | (i < nt - 1), dco * cb_ext, 0.0)
            dvv = (wj[2:3, :] * dyc[:tr] + wj[1:2, :] * pltpu.roll(dyc, n_ext - 1, 0)[:tr]
                   + wj[0:1, :] * pltpu.roll(dyc, n_ext - 2, 0)[:tr])
            dproj_ref[:, cols] = (dco[:tr] * y_ext[:tr]).astype(BF16)
            dproj_ref[:, w_conv + j * LANES:w_conv + (j + 1) * LANES] = (dvv * cu).astype(BF16)
            dproj_ref[:, 2 * w_conv + j * LANES:2 * w_conv + (j + 1) * LANES] = (dvv * cc).astype(BF16)
            dyt = dyc[:tr]
            for tap, shifted in enumerate((v2, v1, v0)):
                dw_ref[tap:tap + 1, cols] += jnp.sum(dyt * shifted[:tr], axis=0, keepdims=True)
            dg_ref[:, cols] += jnp.sum(dgn[:tr], axis=0, keepdims=True)

    n_cols = dproj.shape[1]
    return _pcall(
        body, name="conv_bwd", grid=(nt,),
        in_specs=[main(0), main(1), main(2), main(0),
                  prev(1), prev(2), nxt(0), nxt(1), nxt(2), nxt(0),
                  pl.BlockSpec((CONV_K, w_conv), lambda i: (0, 0)),
                  pl.BlockSpec((1, w_conv), lambda i: (0, 0)),
                  pl.BlockSpec(memory_space=pl.ANY)],
        out_specs=[pl.BlockSpec((tr, 3 * w_conv), lambda i: (i, 0)),
                   pl.BlockSpec((CONV_K, w_conv), lambda i: (0, 0)),
                   pl.BlockSpec((1, w_conv), lambda i: (0, 0))],
        out_shape=[jax.ShapeDtypeStruct((s, n_cols), BF16),
                   jax.ShapeDtypeStruct((CONV_K, w_conv), F32),
                   jax.ShapeDtypeStruct((1, w_conv), F32)],
        input_output_aliases={12: 0},
        compiler_params=_params(("arbitrary",)),
    )(proj, proj, proj, dcat, proj, proj, proj, proj, proj, dcat, conv_w, g_conv, dproj)


STRIP = 16

ALL_CHAINS = (0, 1, 2, 3)
UPPER_CHAINS = (2, 3)


RUN_FLOOR = -104.0


def _any_weight_left(run_s):
    return (jnp.max(run_s[...]) > RUN_FLOOR).astype(jnp.int32)


def _chains(low):
    return [(2 * half + h, half, msk) for half in range(2)
            for h, msk in enumerate((low, jnp.logical_not(low)))]


def _suffix_operator(t):
    r = lax.broadcasted_iota(jnp.int32, (2 * t, t), 0)
    c = lax.broadcasted_iota(jnp.int32, (2 * t, t), 1)
    return jnp.where((r > c) & ((r < t) | (r - t > c)), 1.0, 0.0).astype(BF16)


def _strips(t, diag):
    return [(i, slice(i * STRIP, (i + 1) * STRIP), t // 2 if diag and (i + 1) * STRIP <= t // 2 else t)
            for i in range(t // STRIP)]


def _strip_mask(i, w):
    r = lax.broadcasted_iota(jnp.int32, (STRIP, w), 0) + i * STRIP
    c = lax.broadcasted_iota(jnp.int32, (STRIP, w), 1)
    return r > c


def _store_trimmed(ref, rows, val, w, t, at=0):
    ref[rows, at:at + w] = val
    if w < t:
        ref[rows, at + w:at + t] = jnp.zeros((STRIP, t - w), val.dtype)


def _store_split(ref, rows, val, w, t):
    hi = val.astype(BF16)
    _store_trimmed(ref, rows, hi, w, t)
    _store_trimmed(ref, rows, (val - hi.astype(F32)).astype(BF16), w, t, at=t)


def _sb_scores(z_s, split_s, zl_s, tot_s, keep_s, t, diag):
    for i, rows, w in _strips(t, diag):
        z = z_s[rows, :w]
        log_beta = jnp.minimum(z, 0.0) - jnp.log(1.0 + jnp.exp(-jnp.abs(z)))
        log_keep = log_beta - z
        if diag:
            log_keep = jnp.where(_strip_mask(i, w), log_keep, 0.0)
        _store_split(split_s, rows, log_keep, w, t)
        zl_s[rows, :w] = log_beta
        tot_s[rows, :] = _row_sum(log_keep)
        if keep_s is not None:
            keep_s[rows, :w] = jnp.exp(log_keep)


def _row_sum(v):
    return jnp.broadcast_to(jnp.sum(v, axis=-1, keepdims=True), (v.shape[0], LANES))


def _wide(r, t):
    return jnp.concatenate([r] * (t // LANES), axis=1)


def _sb_weights(zl_s, suf_s, run_s, tot_s, a_s, t, diag, da_s=None, glog_s=None, gsplit_s=None, gtot_s=None):
    for i, rows, w in _strips(t, diag):
        run = run_s[rows, :]
        a = jnp.exp(zl_s[rows, :w] + suf_s[rows, :w] + _wide(run, w))
        if diag:
            a = jnp.where(_strip_mask(i, w), a, 0.0)
        ab = a.astype(BF16)
        _store_trimmed(a_s, rows, ab, w, t)
        run_s[rows, :] = run + tot_s[rows, :]
        if da_s is not None:
            glog = ab.astype(F32) * da_s[rows, :w]
            glog_s[rows, :w] = glog
            _store_split(gsplit_s, rows, glog, w, t)
            gtot_s[rows, :] = _row_sum(glog)


def _sb_dscores(glog_s, cum_s, rest_s, gtot_s, keep_s, dz_s, t, diag):
    for i, rows, w in _strips(t, diag):
        glog = glog_s[rows, :w]
        rest = rest_s[rows, :]
        from_here = _wide(rest, w) - cum_s[rows, :w]
        before = from_here - glog
        dz = from_here * keep_s[rows, :w] - before
        if diag:
            dz = jnp.where(_strip_mask(i, w), dz, 0.0)
        _store_trimmed(dz_s, rows, dz.astype(BF16), w, t)
        rest_s[rows, :] = rest - gtot_s[rows, :]


def _attn_fwd(proj, g_attn, cat, w_conv, carry, t=ATTN_BLOCK):
    s = proj.shape[0]
    w_attn = g_attn.shape[1]
    nh = w_attn // LANES
    t = _tile(s, t)
    tq = 2 * t
    nq = s // tq
    q0 = 3 * w_conv // LANES
    scale = HEAD_DIM ** -0.5
    plan = _Carried(carry)
    nw = len(plan.inputs)

    def body(q_ref, k_ref, v_ref, g_ref, cat_in, *rest):
        staged_refs, rest = rest[:nw], rest[nw:]
        o_ref, cat_ref = rest[:2]
        gathered_refs, rest = rest[2:2 + nw], rest[2 + nw:]
        kb, vb, tri_s, qm_s, z_s, split_s, zl_s, suf_s, a_s, run_s, tot_s, acc_s = rest[:12]
        gather_sems = rest[12:]
        del cat_in
        qi = pl.program_id(1)

        @pl.when((pl.program_id(0) == 0) & (qi == 0))
        def _():
            for cp in plan.copies(staged_refs, gathered_refs, gather_sems):
                cp.start()

        @pl.when(qi == 0)
        def _():
            kb[...] = k_ref[...].astype(BF16)
            vb[...] = v_ref[...].astype(BF16)
            tri_s[...] = _suffix_operator(t)

        low = _low_half()
        for c, half, msk in _chains(low):
            qm_s[c] = jnp.where(msk, q_ref[half * t:(half + 1) * t, :] * scale, 0.0).astype(BF16)
            run_s[c] = jnp.zeros((t, LANES), F32)
            acc_s[c] = jnp.zeros((t, LANES), F32)

        def key_rows(kblk):
            return pl.ds(pl.multiple_of(kblk * t, t), t)

        def key_block(base, c):
            return key_rows(jnp.maximum(base + c // 2, 0))

        def scores_matmul(base, chains):
            for c in chains:
                z_s[c] = lax.dot_general(qm_s[c], kb[key_block(base, c), :], _NT, preferred_element_type=F32)

        def front(modes, base, prev=None):
            for c, diag in modes:
                _sb_scores(z_s.at[c], split_s.at[c], zl_s.at[c], tot_s.at[c], None, t, diag)
                suf_s[c] = jnp.dot(split_s[c], tri_s[...], preferred_element_type=F32)
            if prev is not None:
                tail(*prev)
            scores_matmul(base - 1, ALL_CHAINS)
            for c, diag in modes:
                _sb_weights(zl_s.at[c], suf_s.at[c], run_s.at[c], tot_s.at[c], a_s.at[c], t, diag)

        def tail(base, chains):
            for c in chains:
                acc_s[c] += jnp.dot(a_s[c], vb[key_block(base, c), :], preferred_element_type=F32)

        first = 2 * qi
        scores_matmul(first, ALL_CHAINS)
        front([(c, True) for c in ALL_CHAINS], first)

        def loop(state):
            it = state[0]
            base = first - 1 - it
            front([(c, False) for c in ALL_CHAINS], base, prev=(base + 1, ALL_CHAINS))
            return it + 1, _any_weight_left(run_s)

        done, live = lax.while_loop(lambda state: (state[0] < first) & (state[1] > 0), loop,
                                    (jnp.int32(0), jnp.int32(1)))
        one_more = (done == first) & (live > 0)

        @pl.when(one_more)
        def _():
            front([(c, False) for c in UPPER_CHAINS], -1, prev=(0, ALL_CHAINS))
            tail(-1, UPPER_CHAINS)

        @pl.when(jnp.logical_not(one_more))
        def _():
            tail(first - done, ALL_CHAINS)

        for half in range(2):
            rows = slice(half * t, (half + 1) * t)
            o = jnp.where(low, acc_s[2 * half], acc_s[2 * half + 1])
            o_ref[rows, :] = o
            r = lax.rsqrt(_half_mean(o * o, low) + EPS)
            cat_ref[rows, :] = (o * r * g_ref[...]).astype(BF16)

        @pl.when((pl.program_id(0) == nh - 1) & (qi == nq - 1))
        def _():
            for cp in plan.copies(staged_refs, gathered_refs, gather_sems):
                cp.wait()

    whole = lambda col0: pl.BlockSpec((s, LANES), lambda h, i: (0, col0 + h))
    n_ch = len(ALL_CHAINS)
    res = _pcall(
        body, name="attn_fwd", grid=(nh, nq),
        in_specs=[pl.BlockSpec((tq, LANES), lambda h, i: (i, q0 + h)),
                  whole(q0 + nh), whole(q0 + 2 * nh),
                  pl.BlockSpec((1, LANES), lambda h, i: (0, h)),
                  pl.BlockSpec(memory_space=pl.ANY)] + [pl.BlockSpec(memory_space=pl.ANY)] * nw,
        out_specs=[pl.BlockSpec((tq, LANES), lambda h, i: (i, h)),
                   pl.BlockSpec((tq, LANES), lambda h, i: (i, w_conv // LANES + h))]
        + [pl.BlockSpec(memory_space=pl.ANY)] * nw,
        out_shape=[jax.ShapeDtypeStruct((s, w_attn), F32),
                   jax.ShapeDtypeStruct(cat.shape, BF16)] + plan.out_shapes,
        scratch_shapes=[pltpu.VMEM((s, LANES), BF16), pltpu.VMEM((s, LANES), BF16),
                        pltpu.VMEM((2 * t, t), BF16),
                        pltpu.VMEM((n_ch, t, LANES), BF16),
                        pltpu.VMEM((n_ch, t, t), F32),
                        pltpu.VMEM((n_ch, t, 2 * t), BF16),
                        pltpu.VMEM((n_ch, t, t), F32),
                        pltpu.VMEM((n_ch, t, t), F32),
                        pltpu.VMEM((n_ch, t, t), BF16),
                        pltpu.VMEM((n_ch, t, LANES), F32),
                        pltpu.VMEM((n_ch, t, LANES), F32),
                        pltpu.VMEM((n_ch, t, LANES), F32)]
        + plan.sems,
        input_output_aliases={4: 1, **plan.aliases(5, 2)},
        compiler_params=_params(("arbitrary", "arbitrary")),
    )(proj, proj, proj, g_attn, cat, *plan.inputs)
    return res[0], res[1], res[2:]


def _attn_bwd(proj, o, dcat, g_attn, w_conv, carry, t=ATTN_BLOCK):
    s, n_cols = proj.shape
    w_attn = g_attn.shape[1]
    nh = w_attn // LANES
    t = _tile(s, t)
    tq = 2 * t
    nq = s // tq
    q0 = 3 * w_conv // LANES
    scale = HEAD_DIM ** -0.5
    plan = _Carried(carry)
    nw = len(plan.inputs)

    def body(q_ref, k_ref, v_ref, o_ref, do_ref, g_ref, *rest):
        partial_refs, rest = rest[:nw], rest[nw:]
        dproj_ref, dg_ref = rest[:2]
        received_refs, rest = rest[2:2 + nw], rest[2 + nw:]
        (kb, vb, dkt_acc, dvt_acc, stash, tri_s, qm_s, dom_s, qt_s, dot_s, z_s, da_s, split_s, zl_s,
         keep_s, suf_s, a_s, glog_s, gsplit_s, cum_s, dz_s, run_s, tot_s, rest_s, gtot_s, dq_s) = rest[:26]
        out_sems, scatter_sems = rest[26], rest[27:]
        step_i = pl.program_id(1)
        qi = nq - 1 - step_i
        head_pair = pl.program_id(0)
        first_step = (head_pair == 0) & (step_i == 0)
        last_step = (head_pair == nh - 1) & (step_i == nq - 1)

        @pl.when(first_step)
        def _():
            for cp in plan.copies(partial_refs, received_refs, scatter_sems):
                cp.start()

        def out_copies():
            rows = pl.ds(pl.multiple_of(qi * tq, tq), tq)
            return [pltpu.make_async_copy(
                stash.at[w], dproj_ref.at[rows, pl.ds(pl.multiple_of((q0 + w * nh + head_pair) * LANES, LANES), LANES)],
                out_sems.at[w]) for w in range(3)]

        def walk():
            @pl.when(step_i == 0)
            def _():
                kb[...] = k_ref[...].astype(BF16)
                vb[...] = v_ref[...].astype(BF16)
                tri_s[...] = _suffix_operator(t)
                dkt_acc[...] = jnp.zeros_like(dkt_acc)
                dvt_acc[...] = jnp.zeros_like(dvt_acc)
                dg_ref[...] = jnp.zeros_like(dg_ref)

            low = _low_half()
            gv = g_ref[...]
            for half in range(2):
                rows = slice(half * t, (half + 1) * t)
                q = q_ref[rows, :] * scale
                ov = o_ref[rows, :]
                d_o, dgn = _head_norm_bwd(ov, do_ref[rows, :], gv, low)
                dg_ref[...] += jnp.sum(dgn, axis=0, keepdims=True)
                for h, msk in enumerate((low, jnp.logical_not(low))):
                    c = 2 * half + h
                    qh = jnp.where(msk, q, 0.0)
                    doh = jnp.where(msk, d_o, 0.0)
                    dom = doh.astype(BF16)
                    qm_s[c] = qh.astype(BF16)
                    dom_s[c] = dom
                    qt_s[c] = qh.T.astype(BF16)
                    dot_s[c] = doh.T.astype(BF16)
                    rest_s[c] = _row_sum(dom.astype(F32) * ov)
                    run_s[c] = jnp.zeros((t, LANES), F32)
                    dq_s[c] = jnp.zeros((t, LANES), F32)

            def key_rows(kblk):
                return pl.ds(pl.multiple_of(kblk * t, t), t)

            def block_of(base, half):
                return jnp.maximum(base + half, 0)

            def scores_matmul(base, chains):
                for c in chains:
                    ks = kb[key_rows(block_of(base, c // 2)), :]
                    z_s[c] = lax.dot_general(qm_s[c], ks, _NT, preferred_element_type=F32)

            def da_matmul(base, chains):
                for c in chains:
                    vs = vb[key_rows(block_of(base, c // 2)), :]
                    da_s[c] = lax.dot_general(dom_s[c], vs, _NT, preferred_element_type=F32)

            def front(modes, base, prev=None):
                if prev is not None:
                    tail(*prev)
                for c, diag in modes:
                    _sb_scores(z_s.at[c], split_s.at[c], zl_s.at[c], tot_s.at[c], keep_s.at[c], t, diag)
                    suf_s[c] = jnp.dot(split_s[c], tri_s[...], preferred_element_type=F32)
                scores_matmul(base - 1, ALL_CHAINS)
                for c, diag in modes:
                    _sb_weights(zl_s.at[c], suf_s.at[c], run_s.at[c], tot_s.at[c], a_s.at[c], t, diag,
                                da_s.at[c], glog_s.at[c], gsplit_s.at[c], gtot_s.at[c])
                    cum_s[c] = jnp.dot(gsplit_s[c], tri_s[...], preferred_element_type=F32)
                da_matmul(base - 1, ALL_CHAINS)
                for c, diag in modes:
                    _sb_dscores(glog_s.at[c], cum_s.at[c], rest_s.at[c], gtot_s.at[c], keep_s.at[c],
                                dz_s.at[c], t, diag)

            def tail(base, chains):
                for half in range(2):
                    mine = [c for c in chains if c // 2 == half]
                    if not mine:
                        continue
                    kblk = block_of(base, half)
                    ks = kb[key_rows(kblk), :]
                    dkt = dkt_acc[kblk]
                    dvt = dvt_acc[kblk]
                    for c in mine:
                        dq_s[c] += jnp.dot(dz_s[c], ks, preferred_element_type=F32)
                        dkt = dkt + jnp.dot(qt_s[c], dz_s[c], preferred_element_type=F32)
                        dvt = dvt + jnp.dot(dot_s[c], a_s[c], preferred_element_type=F32)
                    dkt_acc[kblk] = dkt
                    dvt_acc[kblk] = dvt

            first = 2 * qi
            scores_matmul(first, ALL_CHAINS)
            da_matmul(first, ALL_CHAINS)
            front([(c, True) for c in ALL_CHAINS], first)

            def loop(state):
                it = state[0]
                base = first - 1 - it
                front([(c, False) for c in ALL_CHAINS], base, prev=(base + 1, ALL_CHAINS))
                return it + 1, _any_weight_left(run_s)

            done, live = lax.while_loop(lambda state: (state[0] < first) & (state[1] > 0), loop,
                                        (jnp.int32(0), jnp.int32(1)))
            one_more = (done == first) & (live > 0)

            @pl.when(one_more)
            def _():
                front([(c, False) for c in UPPER_CHAINS], -1, prev=(0, ALL_CHAINS))
                tail(-1, UPPER_CHAINS)

            @pl.when(jnp.logical_not(one_more))
            def _():
                tail(first - done, ALL_CHAINS)

            @pl.when(jnp.logical_not(first_step))
            def _():
                for cp in out_copies():
                    cp.wait()

            for half in range(2):
                rows = slice(half * t, (half + 1) * t)
                stash[0, rows, :] = (jnp.where(low, dq_s[2 * half], dq_s[2 * half + 1]) * scale).astype(BF16)
                stash[1, rows, :] = dkt_acc[2 * qi + half].T.astype(BF16)
                stash[2, rows, :] = dvt_acc[2 * qi + half].T.astype(BF16)
            for cp in out_copies():
                cp.start()

        walk()

        @pl.when(last_step)
        def _():
            for cp in out_copies():
                cp.wait()
            for cp in plan.copies(partial_refs, received_refs, scatter_sems):
                cp.wait()

    whole = lambda col0: pl.BlockSpec((s, LANES), lambda h, i: (0, col0 + h))
    blk = lambda col0: pl.BlockSpec((tq, LANES), lambda h, i: (nq - 1 - i, col0 + h))
    n_ch = len(ALL_CHAINS)
    res = _pcall(
        body, name="attn_bwd", grid=(nh, nq),
        in_specs=[blk(q0), whole(q0 + nh), whole(q0 + 2 * nh), blk(0), blk(w_conv // LANES),
                  pl.BlockSpec((1, LANES), lambda h, i: (0, h))] + [pl.BlockSpec(memory_space=pl.ANY)] * nw,
        out_specs=[pl.BlockSpec(memory_space=pl.ANY),
                   pl.BlockSpec((1, LANES), lambda h, i: (0, h))] + [pl.BlockSpec(memory_space=pl.ANY)] * nw,
        out_shape=[jax.ShapeDtypeStruct((s, n_cols), BF16), jax.ShapeDtypeStruct((1, w_attn), F32)]
        + plan.out_shapes,
        scratch_shapes=[pltpu.VMEM((s, LANES), BF16), pltpu.VMEM((s, LANES), BF16),
                        pltpu.VMEM((s // t, LANES, t), F32),
                        pltpu.VMEM((s // t, LANES, t), F32),
                        pltpu.VMEM((3, tq, LANES), BF16),
                        pltpu.VMEM((2 * t, t), BF16),
                        pltpu.VMEM((n_ch, t, LANES), BF16),
                        pltpu.VMEM((n_ch, t, LANES), BF16),
                        pltpu.VMEM((n_ch, LANES, t), BF16),
                        pltpu.VMEM((n_ch, LANES, t), BF16),
                        pltpu.VMEM((n_ch, t, t), F32),
                        pltpu.VMEM((n_ch, t, t), F32),
                        pltpu.VMEM((n_ch, t, 2 * t), BF16),
                        pltpu.VMEM((n_ch, t, t), F32),
                        pltpu.VMEM((n_ch, t, t), F32),
                        pltpu.VMEM((n_ch, t, t), F32),
                        pltpu.VMEM((n_ch, t, t), BF16),
                        pltpu.VMEM((n_ch, t, t), F32),
                        pltpu.VMEM((n_ch, t, 2 * t), BF16),
                        pltpu.VMEM((n_ch, t, t), F32),
                        pltpu.VMEM((n_ch, t, t), BF16),
                        pltpu.VMEM((n_ch, t, LANES), F32),
                        pltpu.VMEM((n_ch, t, LANES), F32),
                        pltpu.VMEM((n_ch, t, LANES), F32),
                        pltpu.VMEM((n_ch, t, LANES), F32),
                        pltpu.VMEM((n_ch, t, LANES), F32),
                        pltpu.SemaphoreType.DMA((3,))]
        + plan.sems,
        input_output_aliases=plan.aliases(6, 2),
        compiler_params=_params(("arbitrary", "arbitrary")),
    )(proj, proj, proj, o, dcat, g_attn, *plan.inputs)
    return res[0], res[1], res[2:]


def _place():
    return lax.axis_index("x"), lax.axis_index("y"), lax.axis_index("c")


def _other_chips(x, y):
    return [(1 - x, y), (x, 1 - y), (1 - x, 1 - y)]


def _slot(px, py, pc):
    return 4 * px + 2 * py + pc


def _all_gather(shards, out_dtypes):
    nw = len(shards)

    def body(*refs):
        ins, outs, stage = refs[:nw], refs[nw:2 * nw], refs[2 * nw:3 * nw]
        send_sems, recv_sems, local_sems = refs[3 * nw:]
        x, y, c = _place()
        me, sibling = (x, y, c), (x, y, 1 - c)
        chips = _other_chips(x, y)

        def copy(w, k, block, to, src=None):
            dst = outs[w].at[_slot(*block)]
            return pltpu.make_async_remote_copy(
                src_ref=dst if src is None else src, dst_ref=dst,
                send_sem=send_sems.at[w * 7 + k], recv_sem=recv_sems.at[w * 7 + k],
                device_id=to, device_id_type=MESH)

        started = []
        local = []
        for w in range(nw):
            stage[w][...] = ins[w][...].astype(stage[w].dtype)
            cp = pltpu.make_async_copy(stage[w], outs[w].at[_slot(*me)], local_sems.at[w])
            cp.start()
            local.append(cp)
            started.append(copy(w, 0, me, sibling, src=stage[w]))
            started[-1].start()
            for j, chip in enumerate(chips):
                started.append(copy(w, 1 + j, me, (*chip, c), src=stage[w]))
                started[-1].start()
        for j, chip in enumerate(chips):
            for w in range(nw):
                copy(w, 1 + j, (*chip, c), me).wait_recv()
                started.append(copy(w, 4 + j, (*chip, c), sibling))
                started[-1].start()
        for w in range(nw):
            copy(w, 0, sibling, me).wait_recv()
            for j, chip in enumerate(chips):
                copy(w, 4 + j, (*chip, 1 - c), me).wait_recv()
        for cp in started:
            cp.wait_send()
        for cp in local:
            cp.wait()

    return _pcall(
        body, name="all_gather_weights",
        in_specs=[pl.BlockSpec(memory_space=pltpu.VMEM)] * nw,
        out_specs=[pl.BlockSpec(memory_space=pl.ANY)] * nw,
        out_shape=[jax.ShapeDtypeStruct((N_DEV, *a.shape), d) for a, d in zip(shards, out_dtypes)],
        scratch_shapes=[pltpu.VMEM(a.shape, d) for a, d in zip(shards, out_dtypes)]
        + [pltpu.SemaphoreType.DMA((7 * nw,)), pltpu.SemaphoreType.DMA((7 * nw,)),
           pltpu.SemaphoreType.DMA((nw,))],
        compiler_params=_params(),
    )(*shards)


N_PEERS = N_DEV - 1


def _peer(k):
    x, y, c = _place()
    return (x ^ (k >> 2), y ^ ((k >> 1) & 1), c ^ (k & 1))


def _remote(src, dst, sems, index, to):
    return pltpu.make_async_remote_copy(src_ref=src, dst_ref=dst, send_sem=sems[0].at[index],
                                        recv_sem=sems[1].at[index], device_id=to, device_id_type=MESH)


def _gather_out_copies(staged, gathered, sems):
    x, y, c = _place()
    me = _slot(x, y, c)
    targets = [(x, y, 1 - c)] + [(*chip, c) for chip in _other_chips(x, y)]
    copies = []
    for w, (src, dst) in enumerate(zip(staged, gathered)):
        copies.append(pltpu.make_async_copy(src, dst.at[me], sems[2].at[w]))
        copies += [_remote(src, dst.at[me], sems, w * len(targets) + k, to) for k, to in enumerate(targets)]
    return copies


def _gather_pass_copies(arrived, gathered, sems):
    x, y, c = _place()
    chips = _other_chips(x, y)
    return [_remote(src.at[_slot(*chip, c)], dst.at[_slot(*chip, c)], sems, w * len(chips) + j, (x, y, 1 - c))
            for w, (src, dst) in enumerate(zip(arrived, gathered)) for j, chip in enumerate(chips)]


def _scatter_copies(partials, received, sems):
    me = _slot(*_place())
    return [_remote(src.at[me ^ k], dst.at[k - 1], sems, w * N_PEERS + k - 1, _peer(k))
            for w, (src, dst) in enumerate(zip(partials, received)) for k in range(1, N_DEV)]


class _Carried:
    COPIES = {"gather_out": (_gather_out_copies, 4, True), "gather_pass": (_gather_pass_copies, 3, False),
              "scatter": (_scatter_copies, N_PEERS, False)}

    def __init__(self, jobs):
        self.jobs = [(kind, list(arrays)) for kind, arrays in jobs if len(arrays)]
        self.inputs = [a for _, arrays in self.jobs for a in arrays]
        self.out_shapes, self.sems, self.sem_counts = [], [], []
        for kind, arrays in self.jobs:
            _, fan, local = self.COPIES[kind]
            for a in arrays:
                shape = {"gather_out": (N_DEV, *a.shape), "gather_pass": a.shape,
                         "scatter": (N_PEERS, *a.shape[1:])}[kind]
                self.out_shapes.append(jax.ShapeDtypeStruct(shape, BF16))
            job_sems = [pltpu.SemaphoreType.DMA((fan * len(arrays),))] * 2
            job_sems += [pltpu.SemaphoreType.DMA((len(arrays),))] if local else []
            self.sems += job_sems
            self.sem_counts.append(len(job_sems))

    def aliases(self, first_input, first_output):
        pairs, at = {}, 0
        for kind, arrays in self.jobs:
            if kind == "gather_pass":
                pairs.update({first_input + at + i: first_output + at + i for i in range(len(arrays))})
            at += len(arrays)
        return pairs

    def copies(self, in_refs, out_refs, sem_refs):
        out, at, sem_at = [], 0, 0
        for (kind, arrays), n_sems in zip(self.jobs, self.sem_counts):
            n = len(arrays)
            out += self.COPIES[kind][0](in_refs[at:at + n], out_refs[at:at + n], sem_refs[sem_at:sem_at + n_sems])
            at, sem_at = at + n, sem_at + n_sems
        return out


def _cast_shards(shards):
    def body(*refs):
        for src, dst in zip(refs[:len(shards)], refs[len(shards):]):
            dst[...] = src[...].astype(BF16)

    return _pcall(
        body, name="cast_shards",
        in_specs=[pl.BlockSpec(memory_space=pltpu.VMEM)] * len(shards),
        out_specs=[pl.BlockSpec(memory_space=pltpu.VMEM)] * len(shards),
        out_shape=[jax.ShapeDtypeStruct(a.shape, BF16) for a in shards],
        compiler_params=_params(),
    )(*shards)


def _all_reduce_small(packed):
    r = packed.shape[0]

    def body(x_ref, o_ref, gathered, send_sems, recv_sems):
        x, y, c = _place()
        me = _slot(x, y, c)
        gathered[me] = x_ref[...]
        peers = [(px, py, pc) for px in range(2) for py in range(2) for pc in range(2)]
        started = []
        for k in range(1, N_DEV):
            to = (x ^ (k >> 2), y ^ ((k >> 1) & 1), c ^ (k & 1))
            cp = pltpu.make_async_remote_copy(
                src_ref=x_ref, dst_ref=gathered.at[me],
                send_sem=send_sems.at[k - 1], recv_sem=recv_sems.at[k - 1],
                device_id=to, device_id_type=MESH)
            cp.start()
            started.append(cp)
        del peers
        for cp in started:
            cp.wait()
        total = gathered[0]
        for k in range(1, N_DEV):
            total = total + gathered[k]
        o_ref[...] = total

    return _pcall(
        body, name="all_reduce_small",
        in_specs=[pl.BlockSpec(memory_space=pltpu.VMEM)],
        out_specs=pl.BlockSpec(memory_space=pltpu.VMEM),
        out_shape=jax.ShapeDtypeStruct(packed.shape, F32),
        scratch_shapes=[pltpu.VMEM((N_DEV, r, LANES), F32),
                        pltpu.SemaphoreType.DMA((N_DEV - 1,)), pltpu.SemaphoreType.DMA((N_DEV - 1,))],
        compiler_params=_params(),
    )(packed)


def _adam_math(w, g, m, v):
    m = ADAM_B1 * m + (1.0 - ADAM_B1) * g
    v = ADAM_B2 * v + (1.0 - ADAM_B2) * jnp.square(g)
    m_hat = m / (1.0 - ADAM_B1 ** ADAM_STEP)
    v_hat = v / (1.0 - ADAM_B2 ** ADAM_STEP)
    delta = -ADAM_LR * (m_hat / (jnp.sqrt(v_hat) + ADAM_EPS) + ADAM_WD * w)
    return delta, m, v


ADAM_TILE_BYTES = 24 * 1024 * 1024


def _adam_sharded(name, own, received, w, m, v, place):
    r, cdim = w.shape
    row_bytes = 2 * cdim * (4 + 2 * N_PEERS + 3 * 4 + 4 * 4)
    tr = _tile(r, max(LANES, ADAM_TILE_BYTES // row_bytes // LANES * LANES)) if r % LANES == 0 else r

    def body(place_ref, own_ref, rec_ref, w_ref, m_ref, v_ref, g_ref, d_ref, nm_ref, nv_ref):
        del place_ref
        g = own_ref[...]
        for j in range(N_PEERS):
            g = g + rec_ref[j].astype(F32)
        delta, nm, nv = _adam_math(w_ref[...], g, m_ref[...], v_ref[...])
        g_ref[...] = g
        d_ref[...] = delta
        nm_ref[...] = nm
        nv_ref[...] = nv

    blk = pl.BlockSpec((tr, cdim), lambda i, pr: (i, 0))
    grid_spec = pltpu.PrefetchScalarGridSpec(
        num_scalar_prefetch=1, grid=(r // tr,),
        in_specs=[pl.BlockSpec((None, tr, cdim), lambda i, pr: (4 * pr[0] + 2 * pr[1] + pr[2], i, 0)),
                  pl.BlockSpec((N_PEERS, tr, cdim), lambda i, pr: (0, i, 0)), blk, blk, blk],
        out_specs=[blk] * 4)
    return _pcall(body, name=name, grid_spec=grid_spec,
                  out_shape=[jax.ShapeDtypeStruct((r, cdim), F32)] * 4,
                  compiler_params=_params(("parallel",)))(place, own, received, w, m, v)


def _adam_small(w, g, m, v):
    def body(w_ref, g_ref, m_ref, v_ref, d_ref, nm_ref, nv_ref):
        delta, nm, nv = _adam_math(w_ref[...], g_ref[...], m_ref[...], v_ref[...])
        d_ref[...] = delta
        nm_ref[...] = nm
        nv_ref[...] = nv

    return _pcall(body, name="adam_small",
                  in_specs=[pl.BlockSpec(memory_space=pltpu.VMEM)] * 4,
                  out_specs=[pl.BlockSpec(memory_space=pltpu.VMEM)] * 3,
                  out_shape=[jax.ShapeDtypeStruct(w.shape, F32)] * 3,
                  compiler_params=_params())(w, g, m, v)


def _rows(vec):
    return vec.reshape(-1, LANES)


def kernel(x, p, g_mix, w_in, conv_w, g_conv_out, g_attn_out, w_out, g_mlp, w_up, w_down, g_ple, w_ple_gate, w_ple_proj, g_final, loss_target, m_g_mix, m_w_in, m_conv_w, m_g_conv_out, m_g_attn_out, m_w_out, m_g_mlp, m_w_up, m_w_down, m_g_ple, m_w_ple_gate, m_w_ple_proj, m_g_final, v_g_mix, v_w_in, v_conv_w, v_g_conv_out, v_g_attn_out, v_w_out, v_g_mlp, v_w_up, v_w_down, v_g_ple, v_w_ple_gate, v_w_ple_proj, v_g_final):
    s, d = x.shape[1], x.shape[2]
    w_conv = g_conv_out.shape[1]
    w_attn = g_attn_out.shape[1]
    cw = conv_w.shape[2]
    xs, ps, tgt = x[0], p[0, 0], loss_target[0]
    place = jnp.stack([lax.axis_index("x"), lax.axis_index("y"), lax.axis_index("c")]).astype(jnp.int32)
    my_slot = 4 * place[0] + 2 * place[1] + place[2]

    conv_tile = jnp.pad(conv_w[0], ((0, HALO - CONV_K), (0, LANES - cw)))
    big = [w_in[0], w_out[0], w_up[0], w_down[0], w_ple_gate[0], w_ple_proj[0]]
    win_g, conv_g = _all_gather([big[0], conv_tile], [BF16, F32])
    s_out, s_up, s_down, s_gate, s_proj = _cast_shards(big[1:])
    conv_full = jnp.transpose(conv_g[:, :CONV_K, :cw], (1, 0, 2)).reshape(CONV_K, w_conv)
    in_shard, up_shard, proj_shard = big[0].shape[1], big[2].shape[1], big[5].shape[1]

    proj, a, g_out, g_gate, g_proj = _mm_nn("in_proj", xs, win_g, n_shard=in_shard, tn=in_shard, tm=2048,
                                            lhs_norm=g_mix, carry=[("gather_out", [s_out, s_gate, s_proj])])
    cat = _conv_fwd(proj, conv_full, g_conv_out, w_conv, d)
    o, cat, (g_up, g_down, wout_g, wgate_g, wproj_g) = _attn_fwd(
        proj, g_attn_out, cat, w_conv,
        [("gather_out", [s_up, s_down]), ("gather_pass", [g_out, g_gate, g_proj])])
    wout_f = wout_g.reshape(-1, wout_g.shape[-1])
    wgate_f = wgate_g.reshape(-1, wgate_g.shape[-1])
    h1, wup_g = _mm_nn("out_proj", cat, wout_f, epilogue=_ep_residual, extras=(xs,),
                       carry=[("gather_pass", [g_up])])
    act, mn, wdown_g = _mm_nn("mlp_up", h1, wup_g, n_shard=up_shard, epilogue=_ep_up, out_dtypes=(BF16,), tm=2048,
                              lhs_norm=g_mlp, carry=[("gather_pass", [g_down])])
    wdown_f = wdown_g.reshape(-1, wdown_g.shape[-1])
    h2, = _mm_nn("mlp_down", act, wdown_f, epilogue=_ep_residual, extras=(h1,))
    pp = _ple_proj(ps, wproj_g)
    loss_part, dh3, dgl, dpp, dg_final, n3 = _ple_gate_loss(h2, g_ple, wgate_f, pp, tgt, g_final.reshape(1, d))

    def slots(t2d):
        return t2d.reshape(N_DEV, -1, t2d.shape[-1])

    dw_proj = _d_ple_proj(ps, dpp, proj_shard)
    dw_gate = [slots(t) for t in _mm_tn("d_w_ple_gate", n3, dgl)]
    dh2, dh2b, dg_ple = _mm_nt_norm_bwd("d_norm_ple", dgl, wgate_f, h2, g_ple, dh3)
    du, gate_recv, proj_recv = _mm_nt("d_mlp_act", dh2b, wdown_f, epilogue=_ep_dact, out_dtypes=(BF16,),
                                      extras=(act,), tm=2048, carry=[("scatter", [dw_gate[1], dw_proj[1]])])
    dw_down = [slots(t) for t in _mm_tn("d_w_down", act, dh2b)]
    dw_up = _mm_tn("d_w_up", mn, du, n_shard=up_shard)
    dh1, dh1b, dg_mlp = _mm_nt_norm_bwd("d_norm_mlp", du, wup_g, h1, g_mlp, dh2, k_shard=up_shard, tm=1024)
    dcat, = _mm_nt("d_cat", dh1b, wout_f)
    dw_out = [slots(t) for t in _mm_tn("d_w_out", cat, dh1b)]
    dproj, dg_attn, (up_recv, down_recv) = _attn_bwd(proj, o, dcat, g_attn_out, w_conv,
                                                     [("scatter", [dw_up[1], dw_down[1]])])
    dproj, dconv, dg_conv = _conv_bwd(proj, dcat, conv_full, g_conv_out, dproj, w_conv)
    *dw_in, out_recv = _mm_tn("d_w_in", a, dproj, n_shard=in_shard, tn=in_shard,
                              carry=[("scatter", [dw_out[1]])])
    grad_x, _, dg_mix, in_recv = _mm_nt_norm_bwd("d_norm_mix", dproj, win_g, xs, g_mix, dh1, k_shard=in_shard,
                                                 tk=2 * in_shard, tm=1024, carry=[("scatter", [dw_in[1]])])

    names = ["w_in", "w_out", "w_up", "w_down", "w_ple_gate", "w_ple_proj"]
    owns = [dw_in[0], dw_out[0], dw_up[0], dw_down[0], dw_gate[0], dw_proj[0]]
    recvs = [in_recv, out_recv, up_recv, down_recv, gate_recv, proj_recv]
    moments = [(m_w_in, v_w_in), (m_w_out, v_w_out), (m_w_up, v_w_up), (m_w_down, v_w_down),
               (m_w_ple_gate, v_w_ple_gate), (m_w_ple_proj, v_w_ple_proj)]
    big_out = {}
    for n, own, rc, wt, (mm, vv) in zip(names, owns, recvs, big, moments):
        big_out[n] = [t[None] for t in _adam_sharded("adam_" + n, own, rc, wt, mm[0], vv[0], place)]

    n_conv_rows = CONV_K * w_conv // LANES
    small_g = jnp.concatenate(
        [_rows(dg_mix[0]), _rows(dg_conv[0]), _rows(dg_attn[0]), _rows(dg_mlp[0]), _rows(dg_ple[0]),
         _rows(dg_final[0]), _rows(dconv.reshape(-1)), loss_part], axis=0)
    n_gain_rows = small_g.shape[0] - n_conv_rows - 1
    pad_rows = (-small_g.shape[0]) % HALO
    small_g = _all_reduce_small(jnp.pad(small_g, ((0, pad_rows), (0, 0))))
    loss = small_g[n_gain_rows + n_conv_rows, 0]
    dconv_full = small_g[n_gain_rows:n_gain_rows + n_conv_rows].reshape(CONV_K, w_conv)
    dconv_mine = lax.dynamic_slice(dconv_full, (0, my_slot * cw), (CONV_K, cw))

    def pack(vecs, conv_part):
        rows = [_rows(t.reshape(-1)) for t in vecs]
        rows.append(jnp.pad(conv_part, ((0, HALO - CONV_K), (0, LANES - cw))))
        return jnp.concatenate(rows, axis=0)

    gains = [g_mix, g_conv_out, g_attn_out, g_mlp, g_ple, g_final]
    gains_m = [m_g_mix, m_g_conv_out, m_g_attn_out, m_g_mlp, m_g_ple, m_g_final]
    gains_v = [v_g_mix, v_g_conv_out, v_g_attn_out, v_g_mlp, v_g_ple, v_g_final]
    gpack = jnp.concatenate([small_g[:n_gain_rows], jnp.pad(dconv_mine, ((0, HALO - CONV_K), (0, LANES - cw)))], axis=0)
    sd, sm, sv = _adam_small(pack(gains, conv_w[0]), gpack, pack(gains_m, m_conv_w[0]), pack(gains_v, v_conv_w[0]))

    def unpack(packed):
        out, r0 = [], 0
        for t in gains:
            nr = t.size // LANES
            out.append(packed[r0:r0 + nr].reshape(t.shape))
            r0 += nr
        out.append(packed[r0:r0 + CONV_K, :cw][None])
        return out

    sg_l, sd_l, sm_l, sv_l = unpack(gpack), unpack(sd), unpack(sm), unpack(sv)
    small_names = ["g_mix", "g_conv_out", "g_attn_out", "g_mlp", "g_ple", "g_final", "conv_w"]
    small_out = {n: [sg_l[i], sd_l[i], sm_l[i], sv_l[i]] for i, n in enumerate(small_names)}

    order = ["g_mix", "w_in", "conv_w", "g_conv_out", "g_attn_out", "w_out", "g_mlp", "w_up", "w_down",
             "g_ple", "w_ple_gate", "w_ple_proj", "g_final"]
    table = {**big_out, **small_out}
    outs = [loss, grad_x[None]]
    for kind in range(4):
        outs.extend(table[n][kind] for n in order)
    return tuple(outs)
```

```python
import jax
import jax.numpy as jnp
from jax import lax
from jax.experimental import pallas as pl
from jax.experimental.pallas import tpu as pltpu

F32 = jnp.float32
BF16 = jnp.bfloat16
EPS = 1e-6
HEAD_DIM = 64
LANES = 128
CONV_K = 3
MXU_WIDTH = 256
ATTN_BLOCK = MXU_WIDTH
HALO = 8
N_DEV = 8
MESH = pl.DeviceIdType.MESH
VMEM_LIMIT = 56 * 1024 * 1024

ADAM_LR = 0.001
ADAM_B1 = 0.9
ADAM_B2 = 0.999
ADAM_EPS = 1e-08
ADAM_WD = 0.01
ADAM_STEP = 10


def _pcall(body, **kw):
    return pl.pallas_call(body, **kw)


def _params(sem=None, **kw):
    return pltpu.CompilerParams(dimension_semantics=sem, vmem_limit_bytes=VMEM_LIMIT, **kw)


def _tile(dim, pref):
    t = min(dim, pref)
    while dim % t:
        t -= LANES
    assert t > 0, (dim, pref)
    return t


_NN = (((1,), (0,)), ((), ()))
_NT = (((1,), (1,)), ((), ()))
_TN = (((0,), (0,)), ((), ()))


def _ep_store(acc, outs):
    outs[0][...] = acc.astype(outs[0].dtype)


def _ep_both(acc, outs):
    outs[0][...] = acc
    outs[1][...] = acc.astype(BF16)


def _ep_residual(acc, res, outs):
    outs[0][...] = acc + res[...]


def _ep_up(acc, outs):
    outs[0][...] = jnp.square(jnp.maximum(acc, 0.0)).astype(BF16)


def _ep_dact(acc, act, outs):
    outs[0][...] = (acc * (2.0 * jnp.sqrt(act[...].astype(F32)))).astype(BF16)


def _ep_norm_bwd(acc, h, g, dres, outs):
    @pl.when(pl.program_id(0) == 0)
    def _():
        outs[2][...] = jnp.zeros_like(outs[2])

    hv = h[...]
    r = lax.rsqrt(jnp.mean(hv * hv, axis=-1, keepdims=True) + EPS)
    hn = hv * r
    outs[2][...] += jnp.sum(acc * hn, axis=0, keepdims=True)
    dhn = acc * g[...]
    dh = dres[...] + r * (dhn - hn * jnp.mean(dhn * hn, axis=-1, keepdims=True))
    outs[0][...] = dh
    outs[1][...] = dh.astype(BF16)


def _matmul(name, a, b, *, dims, grid, a_spec, b_spec, acc_shape, out_shapes, out_specs,
            epilogue=_ep_store, extras=(), extra_specs=(), carry=(), sequential=False, lhs_norm=False):
    nk = grid[2]
    plan = _Carried(carry)
    n_ex, n_out, n_xc, n_xo = len(extras), len(out_shapes), len(plan.inputs), len(plan.out_shapes)
    n_sems = len(plan.sems)
    last = tuple(g - 1 for g in grid)
    assert not lhs_norm or nk == 1

    def product(a_ref, b_ref):
        if len(b_ref.shape) == 2:
            return lax.dot_general(a_ref[...].astype(BF16), b_ref[...].astype(BF16), dims,
                                   preferred_element_type=F32)
        width = b_ref.shape[2]
        return sum(lax.dot_general(a_ref[:, g * width:(g + 1) * width].astype(BF16), b_ref[g].astype(BF16), dims,
                                   preferred_element_type=F32) for g in range(b_ref.shape[0]))

    def body(a_ref, b_ref, *rest):
        ex, rest = rest[:n_ex], rest[n_ex:]
        partials, rest = rest[:n_xc], rest[n_xc:]
        outs, rest = rest[:n_out], rest[n_out:]
        received, rest = rest[:n_xo], rest[n_xo:]
        ids = [pl.program_id(axis) for axis in range(3)]
        if n_xc:
            @pl.when((ids[0] == 0) & (ids[1] == 0) & (ids[2] == 0))
            def _():
                for cp in plan.copies(partials, received, rest[-n_sems:]):
                    cp.start()

        if lhs_norm:
            x_ref, a_ref, gain, ex, outs = a_ref, outs[-1], ex[-1], ex[:-1], outs[:-1]

            @pl.when(ids[1] == 0)
            def _():
                for m0 in range(0, acc_shape[0], MXU_WIDTH):
                    rows = slice(m0, min(m0 + MXU_WIDTH, acc_shape[0]))
                    xv = x_ref[rows, :]
                    r = lax.rsqrt(jnp.mean(xv * xv, axis=-1, keepdims=True) + EPS)
                    a_ref[rows, :] = (xv * r * gain[...]).astype(BF16)

        if nk == 1 and not sequential and dims != _TN:
            for n0 in range(0, acc_shape[1], MXU_WIDTH):
                cols = slice(n0, min(n0 + MXU_WIDTH, acc_shape[1]))
                b_cols = b_ref.at[cols, :] if dims == _NT else b_ref.at[:, cols]
                for m0 in range(0, acc_shape[0], MXU_WIDTH):
                    rows = slice(m0, min(m0 + MXU_WIDTH, acc_shape[0]))
                    epilogue(product(a_ref.at[rows, :], b_cols), *[e.at[rows, cols] for e in ex],
                             [o.at[rows, cols] for o in outs])
        elif nk == 1:
            epilogue(product(a_ref, b_ref), *ex, outs)
        else:
            acc = rest[0]

            @pl.when(ids[2] == 0)
            def _():
                acc[...] = product(a_ref, b_ref)

            @pl.when(ids[2] > 0)
            def _():
                acc[...] += product(a_ref, b_ref)

            @pl.when(ids[2] == nk - 1)
            def _():
                epilogue(acc[...], *ex, outs)

        if n_xc:
            @pl.when((ids[0] == last[0]) & (ids[1] == last[1]) & (ids[2] == last[2]))
            def _():
                for cp in plan.copies(partials, received, rest[-n_sems:]):
                    cp.wait()

    anywhere = pl.BlockSpec(memory_space=pl.ANY)
    return _pcall(
        body, name=name, grid=grid,
        in_specs=[a_spec, b_spec, *extra_specs, *[anywhere] * n_xc],
        out_specs=[*out_specs, *[anywhere] * n_xo],
        out_shape=[*out_shapes, *plan.out_shapes],
        scratch_shapes=([] if nk == 1 else [pltpu.VMEM(acc_shape, F32)]) + plan.sems,
        input_output_aliases=plan.aliases(2 + n_ex, n_out),
        compiler_params=_params(("arbitrary",) * 3 if n_xc or sequential or lhs_norm
                                else ("parallel", "parallel", "arbitrary")),
    )(a, b, *extras, *plan.inputs)


_NO_CARRY = ()


def _mm_nn(name, a, w, *, n_shard=None, epilogue=_ep_store, out_dtypes=(F32,), extras=(), carry=_NO_CARRY,
           lhs_norm=None, tm=1024, tn=1024, tk=1024):
    m, kd = a.shape
    if lhs_norm is not None:
        tk = kd
    if n_shard is None:
        n = w.shape[1]
        tn = _tile(n, tn)
        tk = _tile(kd, tk)
        b_spec = pl.BlockSpec((tk, tn), lambda i, j, k: (k, j))
    else:
        n = N_DEV * n_shard
        tn = _tile(n_shard, tn)
        tk = _tile(kd, tk)
        per = n_shard // tn
        b_spec = pl.BlockSpec((None, tk, tn), lambda i, j, k: (j // per, k, j % per))
    tm = _tile(m, tm)
    o_spec = pl.BlockSpec((tm, tn), lambda i, j, k: (i, j))
    out_shapes = [jax.ShapeDtypeStruct((m, n), d) for d in out_dtypes]
    out_specs = [o_spec] * len(out_dtypes)
    extra_specs = [o_spec] * len(extras)
    if lhs_norm is not None:
        extras = (*extras, lhs_norm)
        extra_specs.append(pl.BlockSpec((1, kd), lambda i, j, k: (0, 0)))
        out_shapes.append(jax.ShapeDtypeStruct((m, kd), BF16))
        out_specs.append(pl.BlockSpec((tm, kd), lambda i, j, k: (i, 0)))
    return _matmul(
        name, a, w, dims=_NN, grid=(m // tm, n // tn, kd // tk),
        a_spec=pl.BlockSpec((tm, tk), lambda i, j, k: (i, k)), b_spec=b_spec,
        acc_shape=(tm, tn), out_shapes=out_shapes, out_specs=out_specs,
        epilogue=epilogue, extras=extras, extra_specs=extra_specs, carry=carry, lhs_norm=lhs_norm is not None)


def _mm_nt(name, a, w, *, epilogue=_ep_store, out_dtypes=(F32,), extras=(), carry=_NO_CARRY,
           tm=1024, tn=1024, tk=1024):
    m, kd = a.shape
    n = w.shape[0]
    tm, tn, tk = _tile(m, tm), _tile(n, tn), _tile(kd, tk)
    o_spec = pl.BlockSpec((tm, tn), lambda i, j, k: (i, j))
    return _matmul(
        name, a, w, dims=_NT, grid=(m // tm, n // tn, kd // tk),
        a_spec=pl.BlockSpec((tm, tk), lambda i, j, k: (i, k)),
        b_spec=pl.BlockSpec((tn, tk), lambda i, j, k: (j, k)),
        acc_shape=(tm, tn),
        out_shapes=[jax.ShapeDtypeStruct((m, n), d) for d in out_dtypes],
        out_specs=[o_spec] * len(out_dtypes),
        epilogue=epilogue, extras=extras, extra_specs=[o_spec] * len(extras), carry=carry)


def _mm_nt_norm_bwd(name, a, w, h, g, dres, *, k_shard=None, carry=_NO_CARRY, tm=512, tk=1024):
    m, kd = a.shape
    n = h.shape[1]
    if k_shard is None:
        tk = _tile(kd, tk)
        b_spec = pl.BlockSpec((n, tk), lambda i, j, k: (0, k))
    else:
        group = max(1, min(tk // k_shard, N_DEV))
        while N_DEV % group:
            group -= 1
        tk = group * k_shard
        b_spec = pl.BlockSpec((group, n, k_shard), lambda i, j, k: (k, 0, 0))
    tm = _tile(m, tm)
    rows = pl.BlockSpec((tm, n), lambda i, j, k: (i, 0))
    vec = pl.BlockSpec((1, n), lambda i, j, k: (0, 0))
    return _matmul(
        name, a, w, dims=_NT, grid=(m // tm, 1, kd // tk),
        a_spec=pl.BlockSpec((tm, tk), lambda i, j, k: (i, k)), b_spec=b_spec, acc_shape=(tm, n),
        out_shapes=[jax.ShapeDtypeStruct((m, n), F32), jax.ShapeDtypeStruct((m, n), BF16),
                    jax.ShapeDtypeStruct((1, n), F32)],
        out_specs=[rows, rows, vec], epilogue=_ep_norm_bwd,
        extras=(h, g, dres), extra_specs=[rows, vec, rows], carry=carry, sequential=True)


TN_TILE_BYTES = 40 * 1024 * 1024


def _mm_tn(name, a, b, *, n_shard=None, carry=_NO_CARRY, tm=1024, tn=1024):
    t, m = a.shape
    n = b.shape[1]
    tm = _tile(m, tm)
    tn = _tile(n if n_shard is None else n_shard, tn)
    tk = t
    while 2 * 2 * tk * (tm + tn) + 4 * tm * tn * 5 > TN_TILE_BYTES and tk % (2 * LANES) == 0:
        tk //= 2
    if n_shard is None:
        o_spec = pl.BlockSpec((tm, tn), lambda i, j, k: (i, j))
        shape = (m, n)
    else:
        per = n_shard // tn
        o_spec = pl.BlockSpec((None, tm, tn), lambda i, j, k: (j // per, i, j % per))
        shape = (N_DEV, m, n_shard)
    return _matmul(
        name, a, b, dims=_TN, grid=(m // tm, n // tn, t // tk),
        a_spec=pl.BlockSpec((tk, tm), lambda i, j, k: (k, i)),
        b_spec=pl.BlockSpec((tk, tn), lambda i, j, k: (k, j)),
        acc_shape=(tm, tn), epilogue=_ep_both, carry=carry,
        out_shapes=[jax.ShapeDtypeStruct(shape, F32), jax.ShapeDtypeStruct(shape, BF16)],
        out_specs=[o_spec, o_spec])


def _ple_proj(p, w_g, tm=1024):
    s, kd = p.shape
    ns = w_g.shape[2]
    tm = _tile(s, tm)

    def body(p_ref, w_ref, o_ref):
        pv = p_ref[...].astype(BF16)
        for j in range(N_DEV):
            o_ref[:, j * ns:(j + 1) * ns] = jnp.dot(pv, w_ref[j], preferred_element_type=F32)

    return _pcall(body, name="ple_proj", grid=(s // tm,),
                  in_specs=[pl.BlockSpec((tm, kd), lambda i: (i, 0)),
                            pl.BlockSpec((N_DEV, kd, ns), lambda i: (0, 0, 0))],
                  out_specs=pl.BlockSpec((tm, N_DEV * ns), lambda i: (i, 0)),
                  out_shape=jax.ShapeDtypeStruct((s, N_DEV * ns), F32),
                  compiler_params=_params(("parallel",)))(p, w_g)


def _d_ple_proj(p, dpp, ns, tk=1024):
    s, kd = p.shape
    tk = _tile(s, tk)
    nk = s // tk

    def body(p_ref, d_ref, of_ref, ob_ref, acc):
        k = pl.program_id(0)

        @pl.when(k == 0)
        def _():
            acc[...] = jnp.zeros_like(acc)

        pv = p_ref[...].astype(BF16)
        for j in range(N_DEV):
            acc[j] += lax.dot_general(pv, d_ref[:, j * ns:(j + 1) * ns], _TN, preferred_element_type=F32)

        @pl.when(k == nk - 1)
        def _():
            of_ref[...] = acc[...]
            ob_ref[...] = acc[...].astype(BF16)

    whole = pl.BlockSpec((N_DEV, kd, ns), lambda k: (0, 0, 0))
    return _pcall(body, name="d_w_ple_proj", grid=(nk,),
                  in_specs=[pl.BlockSpec((tk, kd), lambda k: (k, 0)),
                            pl.BlockSpec((tk, N_DEV * ns), lambda k: (k, 0))],
                  out_specs=[whole, whole],
                  out_shape=[jax.ShapeDtypeStruct((N_DEV, kd, ns), F32), jax.ShapeDtypeStruct((N_DEV, kd, ns), BF16)],
                  scratch_shapes=[pltpu.VMEM((N_DEV, kd, ns), F32)],
                  compiler_params=_params(("arbitrary",)))(p, dpp)


def _ep_ple_loss(gl, h2, pp, tgt, g_final, outs):
    loss_ref, dh3_ref, dgl_ref, dpp_ref, dg_ref = outs

    @pl.when(pl.program_id(0) == 0)
    def _():
        dg_ref[...] = jnp.zeros_like(dg_ref)
        loss_ref[...] = jnp.zeros_like(loss_ref)

    gate = jax.nn.sigmoid(gl)
    ppv = pp[...]
    h3 = h2[...] + gate * ppv
    r = lax.rsqrt(jnp.mean(h3 * h3, axis=-1, keepdims=True) + EPS)
    hn = h3 * r
    gv = g_final[...]
    diff = hn * gv - tgt[...]
    row = jnp.mean(diff * diff, axis=-1, keepdims=True)
    loss_ref[...] += 0.5 * jnp.sum(row, axis=0, keepdims=True)
    dy = diff * (1.0 / h3.shape[-1])
    dg_ref[...] += jnp.sum(dy * hn, axis=0, keepdims=True)
    dhn = dy * gv
    dh3 = r * (dhn - hn * jnp.mean(dhn * hn, axis=-1, keepdims=True))
    dh3_ref[...] = dh3
    dgl_ref[...] = (dh3 * ppv * gate * (1.0 - gate)).astype(BF16)
    dpp_ref[...] = (dh3 * gate).astype(BF16)


def _ple_gate_loss(h2, g_ple, w_gate, pp, tgt, g_final, tm=512):
    s, d = h2.shape
    tm = _tile(s, tm)
    rows = pl.BlockSpec((tm, d), lambda i, j, k: (i, 0))
    vec = pl.BlockSpec((1, d), lambda i, j, k: (0, 0))
    return _matmul(
        "ple_gate_loss", h2, w_gate, dims=_NN, grid=(s // tm, 1, 1),
        a_spec=rows, b_spec=pl.BlockSpec((d, d), lambda i, j, k: (0, 0)), acc_shape=(tm, d),
        out_shapes=[jax.ShapeDtypeStruct((1, LANES), F32), jax.ShapeDtypeStruct((s, d), F32),
                    jax.ShapeDtypeStruct((s, d), BF16), jax.ShapeDtypeStruct((s, d), BF16),
                    jax.ShapeDtypeStruct((1, d), F32), jax.ShapeDtypeStruct((s, d), BF16)],
        out_specs=[pl.BlockSpec((1, LANES), lambda i, j, k: (0, 0)), rows, rows, rows, vec, rows],
        epilogue=_ep_ple_loss, extras=(h2, pp, tgt, g_final, g_ple), extra_specs=[rows, rows, rows, vec, vec],
        sequential=True, lhs_norm=True)


def _low_half():
    return lax.broadcasted_iota(jnp.int32, (1, LANES), 1) < HEAD_DIM


def _half_mean(v, low):
    s_lo = jnp.sum(jnp.where(low, v, 0.0), axis=-1, keepdims=True)
    s_hi = jnp.sum(jnp.where(low, 0.0, v), axis=-1, keepdims=True)
    return jnp.where(low, s_lo, s_hi) * (1.0 / HEAD_DIM)


def _head_norm_bwd(val, dout, g, low):
    r = lax.rsqrt(_half_mean(val * val, low) + EPS)
    vn = val * r
    dvn = dout * g
    return r * (dvn - vn * _half_mean(dvn * vn, low)), dout * vn


def _conv_taps(vv_ext, w_ref):
    v0 = vv_ext[HALO:]
    v1 = pltpu.roll(vv_ext, 1, 0)[HALO:]
    v2 = pltpu.roll(vv_ext, 2, 0)[HALO:]
    return w_ref[2:3, :] * v0 + w_ref[1:2, :] * v1 + w_ref[0:1, :] * v2, (v0, v1, v2)


def _conv_fwd(proj, conv_w, g_conv, w_conv, d_model, tr=1024):
    s = proj.shape[0]
    tr = _tile(s, tr)
    hb = tr // HALO

    def main(part):
        return pl.BlockSpec((tr, w_conv), lambda i: (i, part))

    def prev(part):
        return pl.BlockSpec((HALO, w_conv), lambda i: (jnp.maximum(i * hb - 1, 0), part))

    def body(cb_ref, cc_ref, cu_ref, ccp_ref, cup_ref, w_ref, g_ref, o_ref):
        i = pl.program_id(0)
        low = _low_half()
        for j in range(w_conv // LANES):
            cols = slice(j * LANES, (j + 1) * LANES)
            vv_prev = jnp.where(i > 0, ccp_ref[:, cols] * cup_ref[:, cols], 0.0)
            vv_ext = jnp.concatenate([vv_prev, cc_ref[:, cols] * cu_ref[:, cols]], axis=0)
            y, _ = _conv_taps(vv_ext, w_ref.at[:, cols])
            co = cb_ref[:, cols] * y
            r = lax.rsqrt(_half_mean(co * co, low) + EPS)
            o_ref[:, cols] = (co * r * g_ref[:, cols]).astype(BF16)

    return _pcall(
        body, name="conv_fwd", grid=(s // tr,),
        in_specs=[main(0), main(1), main(2), prev(1), prev(2),
                  pl.BlockSpec((CONV_K, w_conv), lambda i: (0, 0)),
                  pl.BlockSpec((1, w_conv), lambda i: (0, 0))],
        out_specs=pl.BlockSpec((tr, w_conv), lambda i: (i, 0)),
        out_shape=jax.ShapeDtypeStruct((s, d_model), BF16),
        compiler_params=_params(("parallel",)),
    )(proj, proj, proj, proj, proj, conv_w, g_conv)


def _conv_bwd(proj, dcat, conv_w, g_conv, dproj, w_conv, tr=1024):
    s = proj.shape[0]
    tr = _tile(s, tr)
    hb = tr // HALO
    last = s // HALO - 1
    nt = s // tr

    def main(part):
        return pl.BlockSpec((tr, w_conv), lambda i: (i, part))

    def prev(part):
        return pl.BlockSpec((HALO, w_conv), lambda i: (jnp.maximum(i * hb - 1, 0), part))

    def nxt(part):
        return pl.BlockSpec((HALO, w_conv), lambda i: (jnp.minimum((i + 1) * hb, last), part))

    def body(cb_ref, cc_ref, cu_ref, dc_ref, ccp_ref, cup_ref, cbn_ref, ccn_ref, cun_ref, dcn_ref,
             w_ref, g_ref, dproj_in, dproj_ref, dw_ref, dg_ref):
        del dproj_in
        i = pl.program_id(0)

        @pl.when(i == 0)
        def _():
            dw_ref[...] = jnp.zeros_like(dw_ref)
            dg_ref[...] = jnp.zeros_like(dg_ref)

        low = _low_half()
        n_ext = tr + HALO
        rowid = lax.broadcasted_iota(jnp.int32, (n_ext, 1), 0)
        for j in range(w_conv // LANES):
            cols = slice(j * LANES, (j + 1) * LANES)
            wj = w_ref.at[:, cols]
            cc, cu = cc_ref[:, cols], cu_ref[:, cols]
            vv_prev = jnp.where(i > 0, ccp_ref[:, cols] * cup_ref[:, cols], 0.0)
            vv_ext = jnp.concatenate([vv_prev, cc * cu, ccn_ref[:, cols] * cun_ref[:, cols]], axis=0)
            y_ext, (v0, v1, v2) = _conv_taps(vv_ext, wj)
            cb_ext = jnp.concatenate([cb_ref[:, cols], cbn_ref[:, cols]], axis=0)
            dc_ext = jnp.concatenate([dc_ref[:, cols], dcn_ref[:, cols]], axis=0)
            dco, dgn = _head_norm_bwd(cb_ext * y_ext, dc_ext, g_ref[:, cols], low)
            dyc = jnp.where((rowid < tr) | (i < nt - 1), dco * cb_ext, 0.0)
            dvv = (wj[2:3, :] * dyc[:tr] + wj[1:2, :] * pltpu.roll(dyc, n_ext - 1, 0)[:tr]
                   + wj[0:1, :] * pltpu.roll(dyc, n_ext - 2, 0)[:tr])
            dproj_ref[:, cols] = (dco[:tr] * y_ext[:tr]).astype(BF16)
            dproj_ref[:, w_conv + j * LANES:w_conv + (j + 1) * LANES] = (dvv * cu).astype(BF16)
            dproj_ref[:, 2 * w_conv + j * LANES:2 * w_conv + (j + 1) * LANES] = (dvv * cc).astype(BF16)
            dyt = dyc[:tr]
            for tap, shifted in enumerate((v2, v1, v0)):
                dw_ref[tap:tap + 1, cols] += jnp.sum(dyt * shifted[:tr], axis=0, keepdims=True)
            dg_ref[:, cols] += jnp.sum(dgn[:tr], axis=0, keepdims=True)

    n_cols = dproj.shape[1]
    return _pcall(
        body, name="conv_bwd", grid=(nt,),
        in_specs=[main(0), main(1), main(2), main(0),
                  prev(1), prev(2), nxt(0), nxt(1), nxt(2), nxt(0),
                  pl.BlockSpec((CONV_K, w_conv), lambda i: (0, 0)),
                  pl.BlockSpec((1, w_conv), lambda i: (0, 0)),
                  pl.BlockSpec(memory_space=pl.ANY)],
        out_specs=[pl.BlockSpec((tr, 3 * w_conv), lambda i: (i, 0)),
                   pl.BlockSpec((CONV_K, w_conv), lambda i: (0, 0)),
                   pl.BlockSpec((1, w_conv), lambda i: (0, 0))],
        out_shape=[jax.ShapeDtypeStruct((s, n_cols), BF16),
                   jax.ShapeDtypeStruct((CONV_K, w_conv), F32),
                   jax.ShapeDtypeStruct((1, w_conv), F32)],
        input_output_aliases={12: 0},
        compiler_params=_params(("arbitrary",)),
    )(proj, proj, proj, dcat, proj, proj, proj, proj, proj, dcat, conv_w, g_conv, dproj)


STRIP = 16

ALL_CHAINS = (0, 1, 2, 3)
UPPER_CHAINS = (2, 3)


RUN_FLOOR = -104.0


def _any_weight_left(run_s):
    return (jnp.max(run_s[...]) > RUN_FLOOR).astype(jnp.int32)


def _chains(low):
    return [(2 * half + h, half, msk) for half in range(2)
            for h, msk in enumerate((low, jnp.logical_not(low)))]


def _suffix_operator(t):
    r = lax.broadcasted_iota(jnp.int32, (2 * t, t), 0)
    c = lax.broadcasted_iota(jnp.int32, (2 * t, t), 1)
    return jnp.where((r > c) & ((r < t) | (r - t > c)), 1.0, 0.0).astype(BF16)


def _strips(t, diag):
    return [(i, slice(i * STRIP, (i + 1) * STRIP), t // 2 if diag and (i + 1) * STRIP <= t // 2 else t)
            for i in range(t // STRIP)]


def _strip_mask(i, w):
    r = lax.broadcasted_iota(jnp.int32, (STRIP, w), 0) + i * STRIP
    c = lax.broadcasted_iota(jnp.int32, (STRIP, w), 1)
    return r > c


def _store_trimmed(ref, rows, val, w, t, at=0):
    ref[rows, at:at + w] = val
    if w < t:
        ref[rows, at + w:at + t] = jnp.zeros((STRIP, t - w), val.dtype)


def _store_split(ref, rows, val, w, t):
    hi = val.astype(BF16)
    _store_trimmed(ref, rows, hi, w, t)
    _store_trimmed(ref, rows, (val - hi.astype(F32)).astype(BF16), w, t, at=t)


def _sb_scores(z_s, split_s, zl_s, tot_s, keep_s, t, diag):
    for i, rows, w in _strips(t, diag):
        z = z_s[rows, :w]
        log_beta = jnp.minimum(z, 0.0) - jnp.log(1.0 + jnp.exp(-jnp.abs(z)))
        log_keep = log_beta - z
        if diag:
            log_keep = jnp.where(_strip_mask(i, w), log_keep, 0.0)
        _store_split(split_s, rows, log_keep, w, t)
        zl_s[rows, :w] = log_beta
        tot_s[rows, :] = _row_sum(log_keep)
        if keep_s is not None:
            keep_s[rows, :w] = jnp.exp(log_keep)


def _row_sum(v):
    return jnp.broadcast_to(jnp.sum(v, axis=-1, keepdims=True), (v.shape[0], LANES))


def _wide(r, t):
    return jnp.concatenate([r] * (t // LANES), axis=1)


def _sb_weights(zl_s, suf_s, run_s, tot_s, a_s, t, diag, da_s=None, glog_s=None, gsplit_s=None, gtot_s=None):
    for i, rows, w in _strips(t, diag):
        run = run_s[rows, :]
        a = jnp.exp(zl_s[rows, :w] + suf_s[rows, :w] + _wide(run, w))
        if diag:
            a = jnp.where(_strip_mask(i, w), a, 0.0)
        ab = a.astype(BF16)
        _store_trimmed(a_s, rows, ab, w, t)
        run_s[rows, :] = run + tot_s[rows, :]
        if da_s is not None:
            glog = ab.astype(F32) * da_s[rows, :w]
            glog_s[rows, :w] = glog
            _store_split(gsplit_s, rows, glog, w, t)
            gtot_s[rows, :] = _row_sum(glog)


def _sb_dscores(glog_s, cum_s, rest_s, gtot_s, keep_s, dz_s, t, diag):
    for i, rows, w in _strips(t, diag):
        glog = glog_s[rows, :w]
        rest = rest_s[rows, :]
        from_here = _wide(rest, w) - cum_s[rows, :w]
        before = from_here - glog
        dz = from_here * keep_s[rows, :w] - before
        if diag:
            dz = jnp.where(_strip_mask(i, w), dz, 0.0)
        _store_trimmed(dz_s, rows, dz.astype(BF16), w, t)
        rest_s[rows, :] = rest - gtot_s[rows, :]


def _attn_fwd(proj, g_attn, cat, w_conv, carry, t=ATTN_BLOCK):
    s = proj.shape[0]
    w_attn = g_attn.shape[1]
    nh = w_attn // LANES
    t = _tile(s, t)
    tq = 2 * t
    nq = s // tq
    q0 = 3 * w_conv // LANES
    scale = HEAD_DIM ** -0.5
    plan = _Carried(carry)
    nw, n_res = len(plan.inputs), len(plan.out_shapes)

    def body(q_ref, k_ref, v_ref, g_ref, cat_in, *rest):
        staged_refs, rest = rest[:nw], rest[nw:]
        o_ref, cat_ref = rest[:2]
        gathered_refs, rest = rest[2:2 + n_res], rest[2 + n_res:]
        kb, vb, tri_s, qm_s, z_s, split_s, zl_s, suf_s, a_s, run_s, tot_s, acc_s = rest[:12]
        gather_sems = rest[12:]
        del cat_in
        qi = pl.program_id(1)

        @pl.when((pl.program_id(0) == 0) & (qi == 0))
        def _():
            for cp in plan.copies(staged_refs, gathered_refs, gather_sems):
                cp.start()

        @pl.when(qi == 0)
        def _():
            kb[...] = k_ref[...].astype(BF16)
            vb[...] = v_ref[...].astype(BF16)
            tri_s[...] = _suffix_operator(t)

        low = _low_half()
        for c, half, msk in _chains(low):
            qm_s[c] = jnp.where(msk, q_ref[half * t:(half + 1) * t, :] * scale, 0.0).astype(BF16)
            run_s[c] = jnp.zeros((t, LANES), F32)
            acc_s[c] = jnp.zeros((t, LANES), F32)

        def key_rows(kblk):
            return pl.ds(pl.multiple_of(kblk * t, t), t)

        def key_block(base, c):
            return key_rows(jnp.maximum(base + c // 2, 0))

        def scores_matmul(base, chains):
            for c in chains:
                z_s[c] = lax.dot_general(qm_s[c], kb[key_block(base, c), :], _NT, preferred_element_type=F32)

        def front(modes, base, prev=None):
            for c, diag in modes:
                _sb_scores(z_s.at[c], split_s.at[c], zl_s.at[c], tot_s.at[c], None, t, diag)
                suf_s[c] = jnp.dot(split_s[c], tri_s[...], preferred_element_type=F32)
            if prev is not None:
                tail(*prev)
            scores_matmul(base - 1, ALL_CHAINS)
            for c, diag in modes:
                _sb_weights(zl_s.at[c], suf_s.at[c], run_s.at[c], tot_s.at[c], a_s.at[c], t, diag)

        def tail(base, chains):
            for c in chains:
                acc_s[c] += jnp.dot(a_s[c], vb[key_block(base, c), :], preferred_element_type=F32)

        first = 2 * qi
        scores_matmul(first, ALL_CHAINS)
        front([(c, True) for c in ALL_CHAINS], first)

        def loop(state):
            it = state[0]
            base = first - 1 - it
            front([(c, False) for c in ALL_CHAINS], base, prev=(base + 1, ALL_CHAINS))
            return it + 1, _any_weight_left(run_s)

        done, live = lax.while_loop(lambda state: (state[0] < first) & (state[1] > 0), loop,
                                    (jnp.int32(0), jnp.int32(1)))
        one_more = (done == first) & (live > 0)

        @pl.when(one_more)
        def _():
            front([(c, False) for c in UPPER_CHAINS], -1, prev=(0, ALL_CHAINS))
            tail(-1, UPPER_CHAINS)

        @pl.when(jnp.logical_not(one_more))
        def _():
            tail(first - done, ALL_CHAINS)

        for half in range(2):
            rows = slice(half * t, (half + 1) * t)
            o = jnp.where(low, acc_s[2 * half], acc_s[2 * half + 1])
            o_ref[rows, :] = o
            r = lax.rsqrt(_half_mean(o * o, low) + EPS)
            cat_ref[rows, :] = (o * r * g_ref[...]).astype(BF16)

        @pl.when((pl.program_id(0) == nh - 1) & (qi == nq - 1))
        def _():
            for cp in plan.copies(staged_refs, gathered_refs, gather_sems):
                cp.wait()

    whole = lambda col0: pl.BlockSpec((s, LANES), lambda h, i: (0, col0 + h))
    n_ch = len(ALL_CHAINS)
    res = _pcall(
        body, name="attn_fwd", grid=(nh, nq),
        in_specs=[pl.BlockSpec((tq, LANES), lambda h, i: (i, q0 + h)),
                  whole(q0 + nh), whole(q0 + 2 * nh),
                  pl.BlockSpec((1, LANES), lambda h, i: (0, h)),
                  pl.BlockSpec(memory_space=pl.ANY)] + [pl.BlockSpec(memory_space=pl.ANY)] * nw,
        out_specs=[pl.BlockSpec((tq, LANES), lambda h, i: (i, h)),
                   pl.BlockSpec((tq, LANES), lambda h, i: (i, w_conv // LANES + h))]
        + [pl.BlockSpec(memory_space=pl.ANY)] * n_res,
        out_shape=[jax.ShapeDtypeStruct((s, w_attn), F32),
                   jax.ShapeDtypeStruct(cat.shape, BF16)] + plan.out_shapes,
        scratch_shapes=[pltpu.VMEM((s, LANES), BF16), pltpu.VMEM((s, LANES), BF16),
                        pltpu.VMEM((2 * t, t), BF16),
                        pltpu.VMEM((n_ch, t, LANES), BF16),
                        pltpu.VMEM((n_ch, t, t), F32),
                        pltpu.VMEM((n_ch, t, 2 * t), BF16),
                        pltpu.VMEM((n_ch, t, t), F32),
                        pltpu.VMEM((n_ch, t, t), F32),
                        pltpu.VMEM((n_ch, t, t), BF16),
                        pltpu.VMEM((n_ch, t, LANES), F32),
                        pltpu.VMEM((n_ch, t, LANES), F32),
                        pltpu.VMEM((n_ch, t, LANES), F32)]
        + plan.sems,
        input_output_aliases={4: 1, **plan.aliases(5, 2)},
        compiler_params=_params(("arbitrary", "arbitrary")),
    )(proj, proj, proj, g_attn, cat, *plan.inputs)
    return res[0], res[1], res[2:]


def _attn_bwd(proj, o, dcat, g_attn, w_conv, carry, t=ATTN_BLOCK):
    s, n_cols = proj.shape
    w_attn = g_attn.shape[1]
    nh = w_attn // LANES
    t = _tile(s, t)
    tq = 2 * t
    nq = s // tq
    q0 = 3 * w_conv // LANES
    scale = HEAD_DIM ** -0.5
    plan = _Carried(carry)
    nw, n_res = len(plan.inputs), len(plan.out_shapes)

    def body(q_ref, k_ref, v_ref, o_ref, do_ref, g_ref, *rest):
        partial_refs, rest = rest[:nw], rest[nw:]
        dproj_ref, dg_ref = rest[:2]
        received_refs, rest = rest[2:2 + n_res], rest[2 + n_res:]
        (kb, vb, dkt_acc, dvt_acc, stash, tri_s, qm_s, dom_s, qt_s, dot_s, z_s, da_s, split_s, zl_s,
         keep_s, suf_s, a_s, glog_s, gsplit_s, cum_s, dz_s, run_s, tot_s, rest_s, gtot_s, dq_s) = rest[:26]
        out_sems, scatter_sems = rest[26], rest[27:]
        step_i = pl.program_id(1)
        qi = nq - 1 - step_i
        head_pair = pl.program_id(0)
        first_step = (head_pair == 0) & (step_i == 0)
        last_step = (head_pair == nh - 1) & (step_i == nq - 1)

        @pl.when(first_step)
        def _():
            for cp in plan.copies(partial_refs, received_refs, scatter_sems):
                cp.start()

        def out_copies():
            rows = pl.ds(pl.multiple_of(qi * tq, tq), tq)
            return [pltpu.make_async_copy(
                stash.at[w], dproj_ref.at[rows, pl.ds(pl.multiple_of((q0 + w * nh + head_pair) * LANES, LANES), LANES)],
                out_sems.at[w]) for w in range(3)]

        def walk():
            @pl.when(step_i == 0)
            def _():
                kb[...] = k_ref[...].astype(BF16)
                vb[...] = v_ref[...].astype(BF16)
                tri_s[...] = _suffix_operator(t)
                dkt_acc[...] = jnp.zeros_like(dkt_acc)
                dvt_acc[...] = jnp.zeros_like(dvt_acc)
                dg_ref[...] = jnp.zeros_like(dg_ref)

            low = _low_half()
            gv = g_ref[...]
            for half in range(2):
                rows = slice(half * t, (half + 1) * t)
                q = q_ref[rows, :] * scale
                ov = o_ref[rows, :]
                d_o, dgn = _head_norm_bwd(ov, do_ref[rows, :], gv, low)
                dg_ref[...] += jnp.sum(dgn, axis=0, keepdims=True)
                for h, msk in enumerate((low, jnp.logical_not(low))):
                    c = 2 * half + h
                    qh = jnp.where(msk, q, 0.0)
                    doh = jnp.where(msk, d_o, 0.0)
                    dom = doh.astype(BF16)
                    qm_s[c] = qh.astype(BF16)
                    dom_s[c] = dom
                    qt_s[c] = qh.T.astype(BF16)
                    dot_s[c] = doh.T.astype(BF16)
                    rest_s[c] = _row_sum(dom.astype(F32) * ov)
                    run_s[c] = jnp.zeros((t, LANES), F32)
                    dq_s[c] = jnp.zeros((t, LANES), F32)

            def key_rows(kblk):
                return pl.ds(pl.multiple_of(kblk * t, t), t)

            def block_of(base, half):
                return jnp.maximum(base + half, 0)

            def scores_matmul(base, chains):
                for c in chains:
                    ks = kb[key_rows(block_of(base, c // 2)), :]
                    z_s[c] = lax.dot_general(qm_s[c], ks, _NT, preferred_element_type=F32)

            def da_matmul(base, chains):
                for c in chains:
                    vs = vb[key_rows(block_of(base, c // 2)), :]
                    da_s[c] = lax.dot_general(dom_s[c], vs, _NT, preferred_element_type=F32)

            def front(modes, base, prev=None):
                if prev is not None:
                    tail(*prev)
                for c, diag in modes:
                    _sb_scores(z_s.at[c], split_s.at[c], zl_s.at[c], tot_s.at[c], keep_s.at[c], t, diag)
                    suf_s[c] = jnp.dot(split_s[c], tri_s[...], preferred_element_type=F32)
                scores_matmul(base - 1, ALL_CHAINS)
                for c, diag in modes:
                    _sb_weights(zl_s.at[c], suf_s.at[c], run_s.at[c], tot_s.at[c], a_s.at[c], t, diag,
                                da_s.at[c], glog_s.at[c], gsplit_s.at[c], gtot_s.at[c])
                    cum_s[c] = jnp.dot(gsplit_s[c], tri_s[...], preferred_element_type=F32)
                da_matmul(base - 1, ALL_CHAINS)
                for c, diag in modes:
                    _sb_dscores(glog_s.at[c], cum_s.at[c], rest_s.at[c], gtot_s.at[c], keep_s.at[c],
                                dz_s.at[c], t, diag)

            def tail(base, chains):
                for half in range(2):
                    mine = [c for c in chains if c // 2 == half]
                    if not mine:
                        continue
                    kblk = block_of(base, half)
                    ks = kb[key_rows(kblk), :]
                    dkt = dkt_acc[kblk]
                    dvt = dvt_acc[kblk]
                    for c in mine:
                        dq_s[c] += jnp.dot(dz_s[c], ks, preferred_element_type=F32)
                        dkt = dkt + jnp.dot(qt_s[c], dz_s[c], preferred_element_type=F32)
                        dvt = dvt + jnp.dot(dot_s[c], a_s[c], preferred_element_type=F32)
                    dkt_acc[kblk] = dkt
                    dvt_acc[kblk] = dvt

            first = 2 * qi
            scores_matmul(first, ALL_CHAINS)
            da_matmul(first, ALL_CHAINS)
            front([(c, True) for c in ALL_CHAINS], first)

            def loop(state):
                it = state[0]
                base = first - 1 - it
                front([(c, False) for c in ALL_CHAINS], base, prev=(base + 1, ALL_CHAINS))
                return it + 1, _any_weight_left(run_s)

            done, live = lax.while_loop(lambda state: (state[0] < first) & (state[1] > 0), loop,
                                        (jnp.int32(0), jnp.int32(1)))
            one_more = (done == first) & (live > 0)

            @pl.when(one_more)
            def _():
                front([(c, False) for c in UPPER_CHAINS], -1, prev=(0, ALL_CHAINS))
                tail(-1, UPPER_CHAINS)

            @pl.when(jnp.logical_not(one_more))
            def _():
                tail(first - done, ALL_CHAINS)

            @pl.when(jnp.logical_not(first_step))
            def _():
                for cp in out_copies():
                    cp.wait()

            for half in range(2):
                rows = slice(half * t, (half + 1) * t)
                stash[0, rows, :] = (jnp.where(low, dq_s[2 * half], dq_s[2 * half + 1]) * scale).astype(BF16)
                stash[1, rows, :] = dkt_acc[2 * qi + half].T.astype(BF16)
                stash[2, rows, :] = dvt_acc[2 * qi + half].T.astype(BF16)
            for cp in out_copies():
                cp.start()

        walk()

        @pl.when(last_step)
        def _():
            for cp in out_copies():
                cp.wait()
            for cp in plan.copies(partial_refs, received_refs, scatter_sems):
                cp.wait()

    whole = lambda col0: pl.BlockSpec((s, LANES), lambda h, i: (0, col0 + h))
    blk = lambda col0: pl.BlockSpec((tq, LANES), lambda h, i: (nq - 1 - i, col0 + h))
    n_ch = len(ALL_CHAINS)
    res = _pcall(
        body, name="attn_bwd", grid=(nh, nq),
        in_specs=[blk(q0), whole(q0 + nh), whole(q0 + 2 * nh), blk(0), blk(w_conv // LANES),
                  pl.BlockSpec((1, LANES), lambda h, i: (0, h))] + [pl.BlockSpec(memory_space=pl.ANY)] * nw,
        out_specs=[pl.BlockSpec(memory_space=pl.ANY),
                   pl.BlockSpec((1, LANES), lambda h, i: (0, h))] + [pl.BlockSpec(memory_space=pl.ANY)] * n_res,
        out_shape=[jax.ShapeDtypeStruct((s, n_cols), BF16), jax.ShapeDtypeStruct((1, w_attn), F32)]
        + plan.out_shapes,
        scratch_shapes=[pltpu.VMEM((s, LANES), BF16), pltpu.VMEM((s, LANES), BF16),
                        pltpu.VMEM((s // t, LANES, t), F32),
                        pltpu.VMEM((s // t, LANES, t), F32),
                        pltpu.VMEM((3, tq, LANES), BF16),
                        pltpu.VMEM((2 * t, t), BF16),
                        pltpu.VMEM((n_ch, t, LANES), BF16),
                        pltpu.VMEM((n_ch, t, LANES), BF16),
                        pltpu.VMEM((n_ch, LANES, t), BF16),
                        pltpu.VMEM((n_ch, LANES, t), BF16),
                        pltpu.VMEM((n_ch, t, t), F32),
                        pltpu.VMEM((n_ch, t, t), F32),
                        pltpu.VMEM((n_ch, t, 2 * t), BF16),
                        pltpu.VMEM((n_ch, t, t), F32),
                        pltpu.VMEM((n_ch, t, t), F32),
                        pltpu.VMEM((n_ch, t, t), F32),
                        pltpu.VMEM((n_ch, t, t), BF16),
                        pltpu.VMEM((n_ch, t, t), F32),
                        pltpu.VMEM((n_ch, t, 2 * t), BF16),
                        pltpu.VMEM((n_ch, t, t), F32),
                        pltpu.VMEM((n_ch, t, t), BF16),
                        pltpu.VMEM((n_ch, t, LANES), F32),
                        pltpu.VMEM((n_ch, t, LANES), F32),
                        pltpu.VMEM((n_ch, t, LANES), F32),
                        pltpu.VMEM((n_ch, t, LANES), F32),
                        pltpu.VMEM((n_ch, t, LANES), F32),
                        pltpu.SemaphoreType.DMA((3,))]
        + plan.sems,
        input_output_aliases=plan.aliases(6, 2),
        compiler_params=_params(("arbitrary", "arbitrary")),
    )(proj, proj, proj, o, dcat, g_attn, *plan.inputs)
    return res[0], res[1], res[2:]


def _place():
    return lax.axis_index("x"), lax.axis_index("y"), lax.axis_index("c")


def _other_chips(x, y):
    return [(1 - x, y), (x, 1 - y), (1 - x, 1 - y)]


def _slot(px, py, pc):
    return 4 * px + 2 * py + pc


def _all_gather(shards, out_dtypes):
    nw = len(shards)

    def body(*refs):
        ins, outs, stage = refs[:nw], refs[nw:2 * nw], refs[2 * nw:3 * nw]
        send_sems, recv_sems, local_sems = refs[3 * nw:]
        x, y, c = _place()
        me, sibling = (x, y, c), (x, y, 1 - c)
        chips = _other_chips(x, y)

        def copy(w, k, block, to, src=None):
            dst = outs[w].at[_slot(*block)]
            return pltpu.make_async_remote_copy(
                src_ref=dst if src is None else src, dst_ref=dst,
                send_sem=send_sems.at[w * 7 + k], recv_sem=recv_sems.at[w * 7 + k],
                device_id=to, device_id_type=MESH)

        started = []
        local = []
        for w in range(nw):
            stage[w][...] = ins[w][...].astype(stage[w].dtype)
            cp = pltpu.make_async_copy(stage[w], outs[w].at[_slot(*me)], local_sems.at[w])
            cp.start()
            local.append(cp)
            started.append(copy(w, 0, me, sibling, src=stage[w]))
            started[-1].start()
            for j, chip in enumerate(chips):
                started.append(copy(w, 1 + j, me, (*chip, c), src=stage[w]))
                started[-1].start()
        for j, chip in enumerate(chips):
            for w in range(nw):
                copy(w, 1 + j, (*chip, c), me).wait_recv()
                started.append(copy(w, 4 + j, (*chip, c), sibling))
                started[-1].start()
        for w in range(nw):
            copy(w, 0, sibling, me).wait_recv()
            for j, chip in enumerate(chips):
                copy(w, 4 + j, (*chip, 1 - c), me).wait_recv()
        for cp in started:
            cp.wait_send()
        for cp in local:
            cp.wait()

    return _pcall(
        body, name="all_gather_weights",
        in_specs=[pl.BlockSpec(memory_space=pltpu.VMEM)] * nw,
        out_specs=[pl.BlockSpec(memory_space=pl.ANY)] * nw,
        out_shape=[jax.ShapeDtypeStruct((N_DEV, *a.shape), d) for a, d in zip(shards, out_dtypes)],
        scratch_shapes=[pltpu.VMEM(a.shape, d) for a, d in zip(shards, out_dtypes)]
        + [pltpu.SemaphoreType.DMA((7 * nw,)), pltpu.SemaphoreType.DMA((7 * nw,)),
           pltpu.SemaphoreType.DMA((nw,))],
        compiler_params=_params(),
    )(*shards)


N_PEERS = N_DEV - 1


def _peer(k):
    x, y, c = _place()
    return (x ^ (k >> 2), y ^ ((k >> 1) & 1), c ^ (k & 1))


def _remote(src, dst, sems, index, to):
    return pltpu.make_async_remote_copy(src_ref=src, dst_ref=dst, send_sem=sems[0].at[index],
                                        recv_sem=sems[1].at[index], device_id=to, device_id_type=MESH)


def _gather_out_copies(staged, gathered, sems):
    x, y, c = _place()
    me = _slot(x, y, c)
    targets = [(x, y, 1 - c)] + [(*chip, c) for chip in _other_chips(x, y)]
    copies = []
    for w, (src, dst) in enumerate(zip(staged, gathered)):
        copies.append(pltpu.make_async_copy(src, dst.at[me], sems[2].at[w]))
        copies += [_remote(src, dst.at[me], sems, w * len(targets) + k, to) for k, to in enumerate(targets)]
    return copies


def _gather_pass_copies(arrived, gathered, sems):
    x, y, c = _place()
    chips = _other_chips(x, y)
    return [_remote(src.at[_slot(*chip, c)], dst.at[_slot(*chip, c)], sems, w * len(chips) + j, (x, y, 1 - c))
            for w, (src, dst) in enumerate(zip(arrived, gathered)) for j, chip in enumerate(chips)]


ALL_PEERS = tuple(range(1, N_DEV))


def _scatter_copies(partials, received, sems, peers=ALL_PEERS):
    me = _slot(*_place())
    return [_remote(src.at[me ^ k], dst.at[k - 1], sems, w * N_PEERS + k - 1, _peer(k))
            for w, (src, dst) in enumerate(zip(partials, received)) for k in peers]


class _Carried:
    def __init__(self, jobs):
        self.jobs = [(job[0], list(job[1]), job[2] if len(job) > 2 else ALL_PEERS) for job in jobs if len(job[1])]
        self.inputs, self.out_shapes, self.sems, self.counts = [], [], [], []
        for kind, arrays, _ in self.jobs:
            n_out = len(arrays) // 2 if kind == "scatter_more" else len(arrays)
            fan = {"gather_out": 4, "gather_pass": 3}.get(kind, N_PEERS)
            for a in arrays[len(arrays) - n_out:]:
                shape = {"gather_out": (N_DEV, *a.shape), "scatter": (N_PEERS, *a.shape[1:])}.get(kind, a.shape)
                self.out_shapes.append(jax.ShapeDtypeStruct(shape, BF16))
            job_sems = [pltpu.SemaphoreType.DMA((fan * n_out,))] * 2
            job_sems += [pltpu.SemaphoreType.DMA((n_out,))] if kind == "gather_out" else []
            self.inputs += arrays
            self.sems += job_sems
            self.counts.append((len(arrays), n_out, len(job_sems)))

    def aliases(self, first_input, first_output):
        pairs, at_in, at_out = {}, first_input, first_output
        for (kind, _, _), (n_in, n_out, _) in zip(self.jobs, self.counts):
            if kind in ("gather_pass", "scatter_more"):
                pairs.update({at_in + n_in - n_out + i: at_out + i for i in range(n_out)})
            at_in, at_out = at_in + n_in, at_out + n_out
        return pairs

    def copies(self, in_refs, out_refs, sem_refs):
        out, at_in, at_out, at_sem = [], 0, 0, 0
        for (kind, _, peers), (n_in, n_out, n_sems) in zip(self.jobs, self.counts):
            srcs, dsts = in_refs[at_in:at_in + n_out], out_refs[at_out:at_out + n_out]
            sems = sem_refs[at_sem:at_sem + n_sems]
            if kind == "gather_out":
                out += _gather_out_copies(srcs, dsts, sems)
            elif kind == "gather_pass":
                out += _gather_pass_copies(srcs, dsts, sems)
            else:
                out += _scatter_copies(srcs, dsts, sems, peers)
            at_in, at_out, at_sem = at_in + n_in, at_out + n_out, at_sem + n_sems
        return out


def _cast_shards(shards):
    def body(*refs):
        for src, dst in zip(refs[:len(shards)], refs[len(shards):]):
            dst[...] = src[...].astype(BF16)

    return _pcall(
        body, name="cast_shards",
        in_specs=[pl.BlockSpec(memory_space=pltpu.VMEM)] * len(shards),
        out_specs=[pl.BlockSpec(memory_space=pltpu.VMEM)] * len(shards),
        out_shape=[jax.ShapeDtypeStruct(a.shape, BF16) for a in shards],
        compiler_params=_params(),
    )(*shards)


def _all_reduce_small(packed):
    r = packed.shape[0]

    def body(x_ref, o_ref, gathered, send_sems, recv_sems):
        x, y, c = _place()
        me = _slot(x, y, c)
        gathered[me] = x_ref[...]
        copies = [_remote(x_ref, gathered.at[me], (send_sems, recv_sems), k - 1, _peer(k)) for k in ALL_PEERS]
        for cp in copies:
            cp.start()
        for cp in copies:
            cp.wait()
        total = gathered[0]
        for k in range(1, N_DEV):
            total = total + gathered[k]
        o_ref[...] = total

    return _pcall(
        body, name="all_reduce_small",
        in_specs=[pl.BlockSpec(memory_space=pltpu.VMEM)],
        out_specs=pl.BlockSpec(memory_space=pltpu.VMEM),
        out_shape=jax.ShapeDtypeStruct(packed.shape, F32),
        scratch_shapes=[pltpu.VMEM((N_DEV, r, LANES), F32),
                        pltpu.SemaphoreType.DMA((N_DEV - 1,)), pltpu.SemaphoreType.DMA((N_DEV - 1,))],
        compiler_params=_params(),
    )(packed)


def _adam_math(w, g, m, v):
    m = ADAM_B1 * m + (1.0 - ADAM_B1) * g
    v = ADAM_B2 * v + (1.0 - ADAM_B2) * jnp.square(g)
    m_hat = m / (1.0 - ADAM_B1 ** ADAM_STEP)
    v_hat = v / (1.0 - ADAM_B2 ** ADAM_STEP)
    delta = -ADAM_LR * (m_hat / (jnp.sqrt(v_hat) + ADAM_EPS) + ADAM_WD * w)
    return delta, m, v


ADAM_TILE_BYTES = 24 * 1024 * 1024


def _adam_sharded(name, own, received, w, m, v, place):
    r, cdim = w.shape
    row_bytes = 2 * cdim * (4 + 2 * N_PEERS + 3 * 4 + 4 * 4)
    tr = _tile(r, max(LANES, ADAM_TILE_BYTES // row_bytes // LANES * LANES)) if r % LANES == 0 else r

    def body(place_ref, own_ref, rec_ref, w_ref, m_ref, v_ref, g_ref, d_ref, nm_ref, nv_ref):
        del place_ref
        g = own_ref[...]
        for j in range(N_PEERS):
            g = g + rec_ref[j].astype(F32)
        delta, nm, nv = _adam_math(w_ref[...], g, m_ref[...], v_ref[...])
        g_ref[...] = g
        d_ref[...] = delta
        nm_ref[...] = nm
        nv_ref[...] = nv

    blk = pl.BlockSpec((tr, cdim), lambda i, pr: (i, 0))
    grid_spec = pltpu.PrefetchScalarGridSpec(
        num_scalar_prefetch=1, grid=(r // tr,),
        in_specs=[pl.BlockSpec((None, tr, cdim), lambda i, pr: (4 * pr[0] + 2 * pr[1] + pr[2], i, 0)),
                  pl.BlockSpec((N_PEERS, tr, cdim), lambda i, pr: (0, i, 0)), blk, blk, blk],
        out_specs=[blk] * 4)
    return _pcall(body, name=name, grid_spec=grid_spec,
                  out_shape=[jax.ShapeDtypeStruct((r, cdim), F32)] * 4,
                  compiler_params=_params(("parallel",)))(place, own, received, w, m, v)


def _adam_small(w, g, m, v):
    def body(w_ref, g_ref, m_ref, v_ref, d_ref, nm_ref, nv_ref):
        delta, nm, nv = _adam_math(w_ref[...], g_ref[...], m_ref[...], v_ref[...])
        d_ref[...] = delta
        nm_ref[...] = nm
        nv_ref[...] = nv

    return _pcall(body, name="adam_small",
                  in_specs=[pl.BlockSpec(memory_space=pltpu.VMEM)] * 4,
                  out_specs=[pl.BlockSpec(memory_space=pltpu.VMEM)] * 3,
                  out_shape=[jax.ShapeDtypeStruct(w.shape, F32)] * 3,
                  compiler_params=_params())(w, g, m, v)


def _rows(vec):
    return vec.reshape(-1, LANES)


def kernel(x, p, g_mix, w_in, conv_w, g_conv_out, g_attn_out, w_out, g_mlp, w_up, w_down, g_ple, w_ple_gate, w_ple_proj, g_final, loss_target, m_g_mix, m_w_in, m_conv_w, m_g_conv_out, m_g_attn_out, m_w_out, m_g_mlp, m_w_up, m_w_down, m_g_ple, m_w_ple_gate, m_w_ple_proj, m_g_final, v_g_mix, v_w_in, v_conv_w, v_g_conv_out, v_g_attn_out, v_w_out, v_g_mlp, v_w_up, v_w_down, v_g_ple, v_w_ple_gate, v_w_ple_proj, v_g_final):
    s, d = x.shape[1], x.shape[2]
    w_conv = g_conv_out.shape[1]
    w_attn = g_attn_out.shape[1]
    cw = conv_w.shape[2]
    xs, ps, tgt = x[0], p[0, 0], loss_target[0]
    place = jnp.stack([lax.axis_index("x"), lax.axis_index("y"), lax.axis_index("c")]).astype(jnp.int32)
    my_slot = 4 * place[0] + 2 * place[1] + place[2]

    conv_tile = jnp.pad(conv_w[0], ((0, HALO - CONV_K), (0, LANES - cw)))
    big = [w_in[0], w_out[0], w_up[0], w_down[0], w_ple_gate[0], w_ple_proj[0]]
    win_g, conv_g = _all_gather([big[0], conv_tile], [BF16, F32])
    s_out, s_up, s_down, s_gate, s_proj = _cast_shards(big[1:])
    conv_full = jnp.transpose(conv_g[:, :CONV_K, :cw], (1, 0, 2)).reshape(CONV_K, w_conv)
    in_shard, up_shard, proj_shard = big[0].shape[1], big[2].shape[1], big[5].shape[1]

    proj, a, g_out, g_gate, g_proj = _mm_nn("in_proj", xs, win_g, n_shard=in_shard, tn=in_shard, tm=2048,
                                            lhs_norm=g_mix, carry=[("gather_out", [s_out, s_gate, s_proj])])
    cat = _conv_fwd(proj, conv_full, g_conv_out, w_conv, d)
    o, cat, (g_up, g_down, wout_g, wgate_g, wproj_g) = _attn_fwd(
        proj, g_attn_out, cat, w_conv,
        [("gather_out", [s_up, s_down]), ("gather_pass", [g_out, g_gate, g_proj])])
    wout_f = wout_g.reshape(-1, wout_g.shape[-1])
    wgate_f = wgate_g.reshape(-1, wgate_g.shape[-1])
    h1, wup_g = _mm_nn("out_proj", cat, wout_f, epilogue=_ep_residual, extras=(xs,),
                       carry=[("gather_pass", [g_up])])
    act, mn, wdown_g = _mm_nn("mlp_up", h1, wup_g, n_shard=up_shard, epilogue=_ep_up, out_dtypes=(BF16,), tm=2048,
                              lhs_norm=g_mlp, carry=[("gather_pass", [g_down])])
    wdown_f = wdown_g.reshape(-1, wdown_g.shape[-1])
    h2, = _mm_nn("mlp_down", act, wdown_f, epilogue=_ep_residual, extras=(h1,))
    pp = _ple_proj(ps, wproj_g)
    loss_part, dh3, dgl, dpp, dg_final, n3 = _ple_gate_loss(h2, g_ple, wgate_f, pp, tgt, g_final.reshape(1, d))

    def slots(t2d):
        return t2d.reshape(N_DEV, -1, t2d.shape[-1])

    dw_proj = _d_ple_proj(ps, dpp, proj_shard)
    dw_gate = [slots(t) for t in _mm_tn("d_w_ple_gate", n3, dgl)]
    dh2, dh2b, dg_ple = _mm_nt_norm_bwd("d_norm_ple", dgl, wgate_f, h2, g_ple, dh3)
    du, gate_recv, proj_recv = _mm_nt("d_mlp_act", dh2b, wdown_f, epilogue=_ep_dact, out_dtypes=(BF16,),
                                      extras=(act,), tm=2048, carry=[("scatter", [dw_gate[1], dw_proj[1]])])
    dw_down = [slots(t) for t in _mm_tn("d_w_down", act, dh2b)]
    near_y, near_x, far = (1, 2, 3), (4, 5), (6, 7)
    *dw_up, down_part = _mm_tn("d_w_up", mn, du, n_shard=up_shard, carry=[("scatter", [dw_down[1]], near_y)])
    dh1, dh1b, dg_mlp, down_part, up_part = _mm_nt_norm_bwd(
        "d_norm_mlp", du, wup_g, h1, g_mlp, dh2, k_shard=up_shard, tm=1024,
        carry=[("scatter_more", [dw_down[1], down_part], near_x), ("scatter", [dw_up[1]], near_y)])
    dcat, = _mm_nt("d_cat", dh1b, wout_f)
    dw_out = [slots(t) for t in _mm_tn("d_w_out", cat, dh1b)]
    dproj, dg_attn, (down_recv, up_recv) = _attn_bwd(
        proj, o, dcat, g_attn_out, w_conv,
        [("scatter_more", [dw_down[1], down_part], far), ("scatter_more", [dw_up[1], up_part], near_x + far)])
    dproj, dconv, dg_conv = _conv_bwd(proj, dcat, conv_full, g_conv_out, dproj, w_conv)
    *dw_in, out_recv = _mm_tn("d_w_in", a, dproj, n_shard=in_shard, tn=in_shard,
                              carry=[("scatter", [dw_out[1]])])
    grad_x, _, dg_mix, in_recv = _mm_nt_norm_bwd("d_norm_mix", dproj, win_g, xs, g_mix, dh1, k_shard=in_shard,
                                                 tk=2 * in_shard, tm=1024, carry=[("scatter", [dw_in[1]])])

    names = ["w_in", "w_out", "w_up", "w_down", "w_ple_gate", "w_ple_proj"]
    owns = [dw_in[0], dw_out[0], dw_up[0], dw_down[0], dw_gate[0], dw_proj[0]]
    recvs = [in_recv, out_recv, up_recv, down_recv, gate_recv, proj_recv]
    moments = [(m_w_in, v_w_in), (m_w_out, v_w_out), (m_w_up, v_w_up), (m_w_down, v_w_down),
               (m_w_ple_gate, v_w_ple_gate), (m_w_ple_proj, v_w_ple_proj)]
    big_out = {}
    for n, own, rc, wt, (mm, vv) in zip(names, owns, recvs, big, moments):
        big_out[n] = [t[None] for t in _adam_sharded("adam_" + n, own, rc, wt, mm[0], vv[0], place)]

    n_conv_rows = CONV_K * w_conv // LANES
    small_g = jnp.concatenate(
        [_rows(dg_mix[0]), _rows(dg_conv[0]), _rows(dg_attn[0]), _rows(dg_mlp[0]), _rows(dg_ple[0]),
         _rows(dg_final[0]), _rows(dconv.reshape(-1)), loss_part], axis=0)
    n_gain_rows = small_g.shape[0] - n_conv_rows - 1
    pad_rows = (-small_g.shape[0]) % HALO
    small_g = _all_reduce_small(jnp.pad(small_g, ((0, pad_rows), (0, 0))))
    loss = small_g[n_gain_rows + n_conv_rows, 0]
    dconv_full = small_g[n_gain_rows:n_gain_rows + n_conv_rows].reshape(CONV_K, w_conv)
    dconv_mine = lax.dynamic_slice(dconv_full, (0, my_slot * cw), (CONV_K, cw))

    def pack(vecs, conv_part):
        rows = [_rows(t.reshape(-1)) for t in vecs]
        rows.append(jnp.pad(conv_part, ((0, HALO - CONV_K), (0, LANES - cw))))
        return jnp.concatenate(rows, axis=0)

    gains = [g_mix, g_conv_out, g_attn_out, g_mlp, g_ple, g_final]
    gains_m = [m_g_mix, m_g_conv_out, m_g_attn_out, m_g_mlp, m_g_ple, m_g_final]
    gains_v = [v_g_mix, v_g_conv_out, v_g_attn_out, v_g_mlp, v_g_ple, v_g_final]
    gpack = jnp.concatenate([small_g[:n_gain_rows], jnp.pad(dconv_mine, ((0, HALO - CONV_K), (0, LANES - cw)))], axis=0)
    sd, sm, sv = _adam_small(pack(gains, conv_w[0]), gpack, pack(gains_m, m_conv_w[0]), pack(gains_v, v_conv_w[0]))

    def unpack(packed):
        out, r0 = [], 0
        for t in gains:
            nr = t.size // LANES
            out.append(packed[r0:r0 + nr].reshape(t.shape))
            r0 += nr
        out.append(packed[r0:r0 + CONV_K, :cw][None])
        return out

    sg_l, sd_l, sm_l, sv_l = unpack(gpack), unpack(sd), unpack(sm), unpack(sv)
    small_names = ["g_mix", "g_conv_out", "g_attn_out", "g_mlp", "g_ple", "g_final", "conv_w"]
    small_out = {n: [sg_l[i], sd_l[i], sm_l[i], sv_l[i]] for i, n in enumerate(small_names)}

    order = ["g_mix", "w_in", "conv_w", "g_conv_out", "g_attn_out", "w_out", "g_mlp", "w_up", "w_down",
             "g_ple", "w_ple_gate", "w_ple_proj", "g_final"]
    table = {**big_out, **small_out}
    outs = [loss, grad_x[None]]
    for kind in range(4):
        outs.extend(table[n][kind] for n in order)
    return tuple(outs)
```

```python
import jax
import jax.numpy as jnp
from jax import lax
from jax.experimental import pallas as pl
from jax.experimental.pallas import tpu as pltpu

F32 = jnp.float32
BF16 = jnp.bfloat16
EPS = 1e-6
HEAD_DIM = 64
LANES = 128
CONV_K = 3
MXU_WIDTH = 256
ATTN_BLOCK = MXU_WIDTH
HALO = 8
N_DEV = 8
MESH = pl.DeviceIdType.MESH
VMEM_LIMIT = 56 * 1024 * 1024

ADAM_LR = 0.001
ADAM_B1 = 0.9
ADAM_B2 = 0.999
ADAM_EPS = 1e-08
ADAM_WD = 0.01
ADAM_STEP = 10


def _pcall(body, **kw):
    return pl.pallas_call(body, **kw)


def _params(sem=None, **kw):
    return pltpu.CompilerParams(dimension_semantics=sem, vmem_limit_bytes=VMEM_LIMIT, **kw)


def _tile(dim, pref):
    t = min(dim, pref)
    while dim % t:
        t -= LANES
    assert t > 0, (dim, pref)
    return t


_NN = (((1,), (0,)), ((), ()))
_NT = (((1,), (1,)), ((), ()))
_TN = (((0,), (0,)), ((), ()))


def _ep_store(acc, outs):
    outs[0][...] = acc.astype(outs[0].dtype)


def _ep_both(acc, outs):
    outs[0][...] = acc
    outs[1][...] = acc.astype(BF16)


def _ep_residual(acc, res, outs):
    outs[0][...] = acc + res[...]


def _ep_up(acc, outs):
    outs[0][...] = jnp.square(jnp.maximum(acc, 0.0)).astype(BF16)


def _ep_dact(acc, act, outs):
    outs[0][...] = (acc * (2.0 * jnp.sqrt(act[...].astype(F32)))).astype(BF16)


def _ep_norm_bwd(acc, h, g, dres, outs):
    @pl.when(pl.program_id(0) == 0)
    def _():
        outs[2][...] = jnp.zeros_like(outs[2])

    hv = h[...]
    r = lax.rsqrt(jnp.mean(hv * hv, axis=-1, keepdims=True) + EPS)
    hn = hv * r
    outs[2][...] += jnp.sum(acc * hn, axis=0, keepdims=True)
    dhn = acc * g[...]
    dh = dres[...] + r * (dhn - hn * jnp.mean(dhn * hn, axis=-1, keepdims=True))
    outs[0][...] = dh
    outs[1][...] = dh.astype(BF16)


def _matmul(name, a, b, *, dims, grid, a_spec, b_spec, acc_shape, out_shapes, out_specs,
            epilogue=_ep_store, extras=(), extra_specs=(), carry=(), sequential=False, lhs_norm=False):
    nk = grid[2]
    plan = _Carried(carry)
    n_ex, n_out, n_xc, n_xo = len(extras), len(out_shapes), len(plan.inputs), len(plan.out_shapes)
    n_sems = len(plan.sems)
    last = tuple(g - 1 for g in grid)
    assert not lhs_norm or nk == 1

    def product(a_ref, b_ref):
        if len(b_ref.shape) == 2:
            return lax.dot_general(a_ref[...].astype(BF16), b_ref[...].astype(BF16), dims,
                                   preferred_element_type=F32)
        width = b_ref.shape[2]
        return sum(lax.dot_general(a_ref[:, g * width:(g + 1) * width].astype(BF16), b_ref[g].astype(BF16), dims,
                                   preferred_element_type=F32) for g in range(b_ref.shape[0]))

    def body(a_ref, b_ref, *rest):
        ex, rest = rest[:n_ex], rest[n_ex:]
        partials, rest = rest[:n_xc], rest[n_xc:]
        outs, rest = rest[:n_out], rest[n_out:]
        received, rest = rest[:n_xo], rest[n_xo:]
        ids = [pl.program_id(axis) for axis in range(3)]
        if n_xc:
            @pl.when((ids[0] == 0) & (ids[1] == 0) & (ids[2] == 0))
            def _():
                for cp in plan.copies(partials, received, rest[-n_sems:]):
                    cp.start()

        if lhs_norm:
            x_ref, a_ref, gain, ex, outs = a_ref, outs[-1], ex[-1], ex[:-1], outs[:-1]

            @pl.when(ids[1] == 0)
            def _():
                for m0 in range(0, acc_shape[0], MXU_WIDTH):
                    rows = slice(m0, min(m0 + MXU_WIDTH, acc_shape[0]))
                    xv = x_ref[rows, :]
                    r = lax.rsqrt(jnp.mean(xv * xv, axis=-1, keepdims=True) + EPS)
                    a_ref[rows, :] = (xv * r * gain[...]).astype(BF16)

        if nk == 1 and not sequential and dims != _TN:
            for n0 in range(0, acc_shape[1], MXU_WIDTH):
                cols = slice(n0, min(n0 + MXU_WIDTH, acc_shape[1]))
                b_cols = b_ref.at[cols, :] if dims == _NT else b_ref.at[:, cols]
                for m0 in range(0, acc_shape[0], MXU_WIDTH):
                    rows = slice(m0, min(m0 + MXU_WIDTH, acc_shape[0]))
                    epilogue(product(a_ref.at[rows, :], b_cols), *[e.at[rows, cols] for e in ex],
                             [o.at[rows, cols] for o in outs])
        elif nk == 1:
            epilogue(product(a_ref, b_ref), *ex, outs)
        else:
            acc = rest[0]

            @pl.when(ids[2] == 0)
            def _():
                acc[...] = product(a_ref, b_ref)

            @pl.when(ids[2] > 0)
            def _():
                acc[...] += product(a_ref, b_ref)

            @pl.when(ids[2] == nk - 1)
            def _():
                epilogue(acc[...], *ex, outs)

        if n_xc:
            @pl.when((ids[0] == last[0]) & (ids[1] == last[1]) & (ids[2] == last[2]))
            def _():
                for cp in plan.copies(partials, received, rest[-n_sems:]):
                    cp.wait()

    anywhere = pl.BlockSpec(memory_space=pl.ANY)
    return _pcall(
        body, name=name, grid=grid,
        in_specs=[a_spec, b_spec, *extra_specs, *[anywhere] * n_xc],
        out_specs=[*out_specs, *[anywhere] * n_xo],
        out_shape=[*out_shapes, *plan.out_shapes],
        scratch_shapes=([] if nk == 1 else [pltpu.VMEM(acc_shape, F32)]) + plan.sems,
        input_output_aliases=plan.aliases(2 + n_ex, n_out),
        compiler_params=_params(("arbitrary",) * 3 if n_xc or sequential or lhs_norm
                                else ("parallel", "parallel", "arbitrary")),
    )(a, b, *extras, *plan.inputs)


_NO_CARRY = ()


def _mm_nn(name, a, w, *, n_shard=None, epilogue=_ep_store, out_dtypes=(F32,), extras=(), carry=_NO_CARRY,
           lhs_norm=None, tm=1024, tn=1024, tk=1024):
    m, kd = a.shape
    if lhs_norm is not None:
        tk = kd
    if n_shard is None:
        n = w.shape[1]
        tn = _tile(n, tn)
        tk = _tile(kd, tk)
        b_spec = pl.BlockSpec((tk, tn), lambda i, j, k: (k, j))
    else:
        n = N_DEV * n_shard
        tn = _tile(n_shard, tn)
        tk = _tile(kd, tk)
        per = n_shard // tn
        b_spec = pl.BlockSpec((None, tk, tn), lambda i, j, k: (j // per, k, j % per))
    tm = _tile(m, tm)
    o_spec = pl.BlockSpec((tm, tn), lambda i, j, k: (i, j))
    out_shapes = [jax.ShapeDtypeStruct((m, n), d) for d in out_dtypes]
    out_specs = [o_spec] * len(out_dtypes)
    extra_specs = [o_spec] * len(extras)
    if lhs_norm is not None:
        extras = (*extras, lhs_norm)
        extra_specs.append(pl.BlockSpec((1, kd), lambda i, j, k: (0, 0)))
        out_shapes.append(jax.ShapeDtypeStruct((m, kd), BF16))
        out_specs.append(pl.BlockSpec((tm, kd), lambda i, j, k: (i, 0)))
    return _matmul(
        name, a, w, dims=_NN, grid=(m // tm, n // tn, kd // tk),
        a_spec=pl.BlockSpec((tm, tk), lambda i, j, k: (i, k)), b_spec=b_spec,
        acc_shape=(tm, tn), out_shapes=out_shapes, out_specs=out_specs,
        epilogue=epilogue, extras=extras, extra_specs=extra_specs, carry=carry, lhs_norm=lhs_norm is not None)


def _mm_nt(name, a, w, *, epilogue=_ep_store, out_dtypes=(F32,), extras=(), carry=_NO_CARRY,
           tm=1024, tn=1024, tk=1024):
    m, kd = a.shape
    n = w.shape[0]
    tm, tn, tk = _tile(m, tm), _tile(n, tn), _tile(kd, tk)
    o_spec = pl.BlockSpec((tm, tn), lambda i, j, k: (i, j))
    return _matmul(
        name, a, w, dims=_NT, grid=(m // tm, n // tn, kd // tk),
        a_spec=pl.BlockSpec((tm, tk), lambda i, j, k: (i, k)),
        b_spec=pl.BlockSpec((tn, tk), lambda i, j, k: (j, k)),
        acc_shape=(tm, tn),
        out_shapes=[jax.ShapeDtypeStruct((m, n), d) for d in out_dtypes],
        out_specs=[o_spec] * len(out_dtypes),
        epilogue=epilogue, extras=extras, extra_specs=[o_spec] * len(extras), carry=carry)


def _mm_nt_norm_bwd(name, a, w, h, g, dres, *, k_shard=None, carry=_NO_CARRY, tm=512, tk=1024):
    m, kd = a.shape
    n = h.shape[1]
    if k_shard is None:
        tk = _tile(kd, tk)
        b_spec = pl.BlockSpec((n, tk), lambda i, j, k: (0, k))
    else:
        group = max(1, min(tk // k_shard, N_DEV))
        while N_DEV % group:
            group -= 1
        tk = group * k_shard
        b_spec = pl.BlockSpec((group, n, k_shard), lambda i, j, k: (k, 0, 0))
    tm = _tile(m, tm)
    rows = pl.BlockSpec((tm, n), lambda i, j, k: (i, 0))
    vec = pl.BlockSpec((1, n), lambda i, j, k: (0, 0))
    return _matmul(
        name, a, w, dims=_NT, grid=(m // tm, 1, kd // tk),
        a_spec=pl.BlockSpec((tm, tk), lambda i, j, k: (i, k)), b_spec=b_spec, acc_shape=(tm, n),
        out_shapes=[jax.ShapeDtypeStruct((m, n), F32), jax.ShapeDtypeStruct((m, n), BF16),
                    jax.ShapeDtypeStruct((1, n), F32)],
        out_specs=[rows, rows, vec], epilogue=_ep_norm_bwd,
        extras=(h, g, dres), extra_specs=[rows, vec, rows], carry=carry, sequential=True)


TN_TILE_BYTES = 40 * 1024 * 1024


def _mm_tn(name, a, b, *, n_shard=None, carry=_NO_CARRY, tm=1024, tn=1024):
    t, m = a.shape
    n = b.shape[1]
    tm = _tile(m, tm)
    tn = _tile(n if n_shard is None else n_shard, tn)
    tk = t
    while 2 * 2 * tk * (tm + tn) + 4 * tm * tn * 5 > TN_TILE_BYTES and tk % (2 * LANES) == 0:
        tk //= 2
    if n_shard is None:
        o_spec = pl.BlockSpec((tm, tn), lambda i, j, k: (i, j))
        shape = (m, n)
    else:
        per = n_shard // tn
        o_spec = pl.BlockSpec((None, tm, tn), lambda i, j, k: (j // per, i, j % per))
        shape = (N_DEV, m, n_shard)
    return _matmul(
        name, a, b, dims=_TN, grid=(m // tm, n // tn, t // tk),
        a_spec=pl.BlockSpec((tk, tm), lambda i, j, k: (k, i)),
        b_spec=pl.BlockSpec((tk, tn), lambda i, j, k: (k, j)),
        acc_shape=(tm, tn), epilogue=_ep_both, carry=carry,
        out_shapes=[jax.ShapeDtypeStruct(shape, F32), jax.ShapeDtypeStruct(shape, BF16)],
        out_specs=[o_spec, o_spec])


def _ple_proj(p, w_g, tm=1024):
    s, kd = p.shape
    ns = w_g.shape[2]
    tm = _tile(s, tm)

    def body(p_ref, w_ref, o_ref):
        pv = p_ref[...].astype(BF16)
        for j in range(N_DEV):
            o_ref[:, j * ns:(j + 1) * ns] = jnp.dot(pv, w_ref[j], preferred_element_type=F32)

    return _pcall(body, name="ple_proj", grid=(s // tm,),
                  in_specs=[pl.BlockSpec((tm, kd), lambda i: (i, 0)),
                            pl.BlockSpec((N_DEV, kd, ns), lambda i: (0, 0, 0))],
                  out_specs=pl.BlockSpec((tm, N_DEV * ns), lambda i: (i, 0)),
                  out_shape=jax.ShapeDtypeStruct((s, N_DEV * ns), F32),
                  compiler_params=_params(("parallel",)))(p, w_g)


def _d_ple_proj(p, dpp, ns, tk=1024):
    s, kd = p.shape
    tk = _tile(s, tk)
    nk = s // tk

    def body(p_ref, d_ref, of_ref, ob_ref, acc):
        k = pl.program_id(0)

        @pl.when(k == 0)
        def _():
            acc[...] = jnp.zeros_like(acc)

        pv = p_ref[...].astype(BF16)
        for j in range(N_DEV):
            acc[j] += lax.dot_general(pv, d_ref[:, j * ns:(j + 1) * ns], _TN, preferred_element_type=F32)

        @pl.when(k == nk - 1)
        def _():
            of_ref[...] = acc[...]
            ob_ref[...] = acc[...].astype(BF16)

    whole = pl.BlockSpec((N_DEV, kd, ns), lambda k: (0, 0, 0))
    return _pcall(body, name="d_w_ple_proj", grid=(nk,),
                  in_specs=[pl.BlockSpec((tk, kd), lambda k: (k, 0)),
                            pl.BlockSpec((tk, N_DEV * ns), lambda k: (k, 0))],
                  out_specs=[whole, whole],
                  out_shape=[jax.ShapeDtypeStruct((N_DEV, kd, ns), F32), jax.ShapeDtypeStruct((N_DEV, kd, ns), BF16)],
                  scratch_shapes=[pltpu.VMEM((N_DEV, kd, ns), F32)],
                  compiler_params=_params(("arbitrary",)))(p, dpp)


def _ep_ple_loss(gl, h2, pp, tgt, g_final, outs):
    loss_ref, dh3_ref, dgl_ref, dpp_ref, dg_ref = outs

    @pl.when(pl.program_id(0) == 0)
    def _():
        dg_ref[...] = jnp.zeros_like(dg_ref)
        loss_ref[...] = jnp.zeros_like(loss_ref)

    gate = jax.nn.sigmoid(gl)
    ppv = pp[...]
    h3 = h2[...] + gate * ppv
    r = lax.rsqrt(jnp.mean(h3 * h3, axis=-1, keepdims=True) + EPS)
    hn = h3 * r
    gv = g_final[...]
    diff = hn * gv - tgt[...]
    row = jnp.mean(diff * diff, axis=-1, keepdims=True)
    loss_ref[...] += 0.5 * jnp.sum(row, axis=0, keepdims=True)
    dy = diff * (1.0 / h3.shape[-1])
    dg_ref[...] += jnp.sum(dy * hn, axis=0, keepdims=True)
    dhn = dy * gv
    dh3 = r * (dhn - hn * jnp.mean(dhn * hn, axis=-1, keepdims=True))
    dh3_ref[...] = dh3
    dgl_ref[...] = (dh3 * ppv * gate * (1.0 - gate)).astype(BF16)
    dpp_ref[...] = (dh3 * gate).astype(BF16)


def _ple_gate_loss(h2, g_ple, w_gate, pp, tgt, g_final, tm=512):
    s, d = h2.shape
    tm = _tile(s, tm)
    rows = pl.BlockSpec((tm, d), lambda i, j, k: (i, 0))
    vec = pl.BlockSpec((1, d), lambda i, j, k: (0, 0))
    return _matmul(
        "ple_gate_loss", h2, w_gate, dims=_NN, grid=(s // tm, 1, 1),
        a_spec=rows, b_spec=pl.BlockSpec((d, d), lambda i, j, k: (0, 0)), acc_shape=(tm, d),
        out_shapes=[jax.ShapeDtypeStruct((1, LANES), F32), jax.ShapeDtypeStruct((s, d), F32),
                    jax.ShapeDtypeStruct((s, d), BF16), jax.ShapeDtypeStruct((s, d), BF16),
                    jax.ShapeDtypeStruct((1, d), F32), jax.ShapeDtypeStruct((s, d), BF16)],
        out_specs=[pl.BlockSpec((1, LANES), lambda i, j, k: (0, 0)), rows, rows, rows, vec, rows],
        epilogue=_ep_ple_loss, extras=(h2, pp, tgt, g_final, g_ple), extra_specs=[rows, rows, rows, vec, vec],
        sequential=True, lhs_norm=True)


def _low_half():
    return lax.broadcasted_iota(jnp.int32, (1, LANES), 1) < HEAD_DIM


def _half_mean(v, low):
    s_lo = jnp.sum(jnp.where(low, v, 0.0), axis=-1, keepdims=True)
    s_hi = jnp.sum(jnp.where(low, 0.0, v), axis=-1, keepdims=True)
    return jnp.where(low, s_lo, s_hi) * (1.0 / HEAD_DIM)


def _head_norm_bwd(val, dout, g, low):
    r = lax.rsqrt(_half_mean(val * val, low) + EPS)
    vn = val * r
    dvn = dout * g
    return r * (dvn - vn * _half_mean(dvn * vn, low)), dout * vn


def _conv_taps(vv_ext, w_ref):
    v0 = vv_ext[HALO:]
    v1 = pltpu.roll(vv_ext, 1, 0)[HALO:]
    v2 = pltpu.roll(vv_ext, 2, 0)[HALO:]
    return w_ref[2:3, :] * v0 + w_ref[1:2, :] * v1 + w_ref[0:1, :] * v2, (v0, v1, v2)


def _conv_fwd(proj, conv_w, g_conv, w_conv, d_model, tr=1024):
    s = proj.shape[0]
    tr = _tile(s, tr)
    hb = tr // HALO

    def main(part):
        return pl.BlockSpec((tr, w_conv), lambda i: (i, part))

    def prev(part):
        return pl.BlockSpec((HALO, w_conv), lambda i: (jnp.maximum(i * hb - 1, 0), part))

    def body(cb_ref, cc_ref, cu_ref, ccp_ref, cup_ref, w_ref, g_ref, o_ref):
        i = pl.program_id(0)
        low = _low_half()
        for j in range(w_conv // LANES):
            cols = slice(j * LANES, (j + 1) * LANES)
            vv_prev = jnp.where(i > 0, ccp_ref[:, cols] * cup_ref[:, cols], 0.0)
            vv_ext = jnp.concatenate([vv_prev, cc_ref[:, cols] * cu_ref[:, cols]], axis=0)
            y, _ = _conv_taps(vv_ext, w_ref.at[:, cols])
            co = cb_ref[:, cols] * y
            r = lax.rsqrt(_half_mean(co * co, low) + EPS)
            o_ref[:, cols] = (co * r * g_ref[:, cols]).astype(BF16)

    return _pcall(
        body, name="conv_fwd", grid=(s // tr,),
        in_specs=[main(0), main(1), main(2), prev(1), prev(2),
                  pl.BlockSpec((CONV_K, w_conv), lambda i: (0, 0)),
                  pl.BlockSpec((1, w_conv), lambda i: (0, 0))],
        out_specs=pl.BlockSpec((tr, w_conv), lambda i: (i, 0)),
        out_shape=jax.ShapeDtypeStruct((s, d_model), BF16),
        compiler_params=_params(("parallel",)),
    )(proj, proj, proj, proj, proj, conv_w, g_conv)


def _conv_bwd(proj, dcat, conv_w, g_conv, dproj, w_conv, tr=1024):
    s = proj.shape[0]
    tr = _tile(s, tr)
    hb = tr // HALO
    last = s // HALO - 1
    nt = s // tr

    def main(part):
        return pl.BlockSpec((tr, w_conv), lambda i: (i, part))

    def prev(part):
        return pl.BlockSpec((HALO, w_conv), lambda i: (jnp.maximum(i * hb - 1, 0), part))

    def nxt(part):
        return pl.BlockSpec((HALO, w_conv), lambda i: (jnp.minimum((i + 1) * hb, last), part))

    def body(cb_ref, cc_ref, cu_ref, dc_ref, ccp_ref, cup_ref, cbn_ref, ccn_ref, cun_ref, dcn_ref,
             w_ref, g_ref, dproj_in, dproj_ref, dw_ref, dg_ref):
        del dproj_in
        i = pl.program_id(0)

        @pl.when(i == 0)
        def _():
            dw_ref[...] = jnp.zeros_like(dw_ref)
            dg_ref[...] = jnp.zeros_like(dg_ref)

        low = _low_half()
        n_ext = tr + HALO
        rowid = lax.broadcasted_iota(jnp.int32, (n_ext, 1), 0)
        for j in range(w_conv // LANES):
            cols = slice(j * LANES, (j + 1) * LANES)
            wj = w_ref.at[:, cols]
            cc, cu = cc_ref[:, cols], cu_ref[:, cols]
            vv_prev = jnp.where(i > 0, ccp_ref[:, cols] * cup_ref[:, cols], 0.0)
            vv_ext = jnp.concatenate([vv_prev, cc * cu, ccn_ref[:, cols] * cun_ref[:, cols]], axis=0)
            y_ext, (v0, v1, v2) = _conv_taps(vv_ext, wj)
            cb_ext = jnp.concatenate([cb_ref[:, cols], cbn_ref[:, cols]], axis=0)
            dc_ext = jnp.concatenate([dc_ref[:, cols], dcn_ref[:, cols]], axis=0)
            dco, dgn = _head_norm_bwd(cb_ext * y_ext, dc_ext, g_ref[:, cols], low)
            dyc = jnp.where((rowid < tr) | (i < nt - 1), dco * cb_ext, 0.0)
            dvv = (wj[2:3, :] * dyc[:tr] + wj[1:2, :] * pltpu.roll(dyc, n_ext - 1, 0)[:tr]
                   + wj[0:1, :] * pltpu.roll(dyc, n_ext - 2, 0)[:tr])
            dproj_ref[:, cols] = (dco[:tr] * y_ext[:tr]).astype(BF16)
            dproj_ref[:, w_conv + j * LANES:w_conv + (j + 1) * LANES] = (dvv * cu).astype(BF16)
            dproj_ref[:, 2 * w_conv + j * LANES:2 * w_conv + (j + 1) * LANES] = (dvv * cc).astype(BF16)
            dyt = dyc[:tr]
            for tap, shifted in enumerate((v2, v1, v0)):
                dw_ref[tap:tap + 1, cols] += jnp.sum(dyt * shifted[:tr], axis=0, keepdims=True)
            dg_ref[:, cols] += jnp.sum(dgn[:tr], axis=0, keepdims=True)

    n_cols = dproj.shape[1]
    return _pcall(
        body, name="conv_bwd", grid=(nt,),
        in_specs=[main(0), main(1), main(2), main(0),
                  prev(1), prev(2), nxt(0), nxt(1), nxt(2), nxt(0),
                  pl.BlockSpec((CONV_K, w_conv), lambda i: (0, 0)),
                  pl.BlockSpec((1, w_conv), lambda i: (0, 0)),
                  pl.BlockSpec(memory_space=pl.ANY)],
        out_specs=[pl.BlockSpec((tr, 3 * w_conv), lambda i: (i, 0)),
                   pl.BlockSpec((CONV_K, w_conv), lambda i: (0, 0)),
                   pl.BlockSpec((1, w_conv), lambda i: (0, 0))],
        out_shape=[jax.ShapeDtypeStruct((s, n_cols), BF16),
                   jax.ShapeDtypeStruct((CONV_K, w_conv), F32),
                   jax.ShapeDtypeStruct((1, w_conv), F32)],
        input_output_aliases={12: 0},
        compiler_params=_params(("arbitrary",)),
    )(proj, proj, proj, dcat, proj, proj, proj, proj, proj, dcat, conv_w, g_conv, dproj)


STRIP = 16

ALL_CHAINS = (0, 1, 2, 3)
UPPER_CHAINS = (2, 3)


RUN_FLOOR = -104.0


def _any_weight_left(run_s):
    return (jnp.max(run_s[...]) > RUN_FLOOR).astype(jnp.int32)


def _chains(low):
    return [(2 * half + h, half, msk) for half in range(2)
            for h, msk in enumerate((low, jnp.logical_not(low)))]


def _suffix_operator(t):
    r = lax.broadcasted_iota(jnp.int32, (2 * t, t), 0)
    c = lax.broadcasted_iota(jnp.int32, (2 * t, t), 1)
    return jnp.where((r > c) & ((r < t) | (r - t > c)), 1.0, 0.0).astype(BF16)


def _strips(t, diag):
    return [(i, slice(i * STRIP, (i + 1) * STRIP), t // 2 if diag and (i + 1) * STRIP <= t // 2 else t)
            for i in range(t // STRIP)]


def _strip_mask(i, w):
    r = lax.broadcasted_iota(jnp.int32, (STRIP, w), 0) + i * STRIP
    c = lax.broadcasted_iota(jnp.int32, (STRIP, w), 1)
    return r > c


def _store_trimmed(ref, rows, val, w, t, at=0):
    ref[rows, at:at + w] = val
    if w < t:
        ref[rows, at + w:at + t] = jnp.zeros((STRIP, t - w), val.dtype)


def _store_split(ref, rows, val, w, t):
    hi = val.astype(BF16)
    _store_trimmed(ref, rows, hi, w, t)
    _store_trimmed(ref, rows, (val - hi.astype(F32)).astype(BF16), w, t, at=t)


def _sb_scores(z_s, split_s, zl_s, tot_s, keep_s, t, diag):
    for i, rows, w in _strips(t, diag):
        z = z_s[rows, :w]
        log_beta = jnp.minimum(z, 0.0) - jnp.log(1.0 + jnp.exp(-jnp.abs(z)))
        log_keep = log_beta - z
        if diag:
            log_keep = jnp.where(_strip_mask(i, w), log_keep, 0.0)
        _store_split(split_s, rows, log_keep, w, t)
        zl_s[rows, :w] = log_beta
        tot_s[rows, :] = _row_sum(log_keep)
        if keep_s is not None:
            keep_s[rows, :w] = jnp.exp(log_keep)


def _row_sum(v):
    return jnp.broadcast_to(jnp.sum(v, axis=-1, keepdims=True), (v.shape[0], LANES))


def _wide(r, t):
    return jnp.concatenate([r] * (t // LANES), axis=1)


def _sb_weights(zl_s, suf_s, run_s, tot_s, a_s, t, diag, da_s=None, glog_s=None, gsplit_s=None, gtot_s=None):
    for i, rows, w in _strips(t, diag):
        run = run_s[rows, :]
        a = jnp.exp(zl_s[rows, :w] + suf_s[rows, :w] + _wide(run, w))
        if diag:
            a = jnp.where(_strip_mask(i, w), a, 0.0)
        ab = a.astype(BF16)
        _store_trimmed(a_s, rows, ab, w, t)
        run_s[rows, :] = run + tot_s[rows, :]
        if da_s is not None:
            glog = ab.astype(F32) * da_s[rows, :w]
            glog_s[rows, :w] = glog
            _store_split(gsplit_s, rows, glog, w, t)
            gtot_s[rows, :] = _row_sum(glog)


def _sb_dscores(glog_s, cum_s, rest_s, gtot_s, keep_s, dz_s, t, diag):
    for i, rows, w in _strips(t, diag):
        glog = glog_s[rows, :w]
        rest = rest_s[rows, :]
        from_here = _wide(rest, w) - cum_s[rows, :w]
        before = from_here - glog
        dz = from_here * keep_s[rows, :w] - before
        if diag:
            dz = jnp.where(_strip_mask(i, w), dz, 0.0)
        _store_trimmed(dz_s, rows, dz.astype(BF16), w, t)
        rest_s[rows, :] = rest - gtot_s[rows, :]


def _attn_fwd(proj, g_attn, cat, w_conv, carry, t=ATTN_BLOCK):
    s = proj.shape[0]
    w_attn = g_attn.shape[1]
    nh = w_attn // LANES
    t = _tile(s, t)
    tq = 2 * t
    nq = s // tq
    q0 = 3 * w_conv // LANES
    scale = HEAD_DIM ** -0.5
    plan = _Carried(carry)
    nw, n_res = len(plan.inputs), len(plan.out_shapes)

    def body(q_ref, k_ref, v_ref, g_ref, cat_in, *rest):
        staged_refs, rest = rest[:nw], rest[nw:]
        o_ref, cat_ref = rest[:2]
        gathered_refs, rest = rest[2:2 + n_res], rest[2 + n_res:]
        kb, vb, tri_s, qm_s, z_s, split_s, zl_s, suf_s, a_s, run_s, tot_s, acc_s = rest[:12]
        gather_sems = rest[12:]
        del cat_in
        qi = pl.program_id(1)

        @pl.when((pl.program_id(0) == 0) & (qi == 0))
        def _():
            for cp in plan.copies(staged_refs, gathered_refs, gather_sems):
                cp.start()

        @pl.when(qi == 0)
        def _():
            kb[...] = k_ref[...].astype(BF16)
            vb[...] = v_ref[...].astype(BF16)
            tri_s[...] = _suffix_operator(t)

        low = _low_half()
        for c, half, msk in _chains(low):
            qm_s[c] = jnp.where(msk, q_ref[half * t:(half + 1) * t, :] * scale, 0.0).astype(BF16)
            run_s[c] = jnp.zeros((t, LANES), F32)
            acc_s[c] = jnp.zeros((t, LANES), F32)

        def key_rows(kblk):
            return pl.ds(pl.multiple_of(kblk * t, t), t)

        def key_block(base, c):
            return key_rows(jnp.maximum(base + c // 2, 0))

        def scores_matmul(base, chains):
            for c in chains:
                z_s[c] = lax.dot_general(qm_s[c], kb[key_block(base, c), :], _NT, preferred_element_type=F32)

        def front(modes, base, prev=None):
            for c, diag in modes:
                _sb_scores(z_s.at[c], split_s.at[c], zl_s.at[c], tot_s.at[c], None, t, diag)
                suf_s[c] = jnp.dot(split_s[c], tri_s[...], preferred_element_type=F32)
            if prev is not None:
                tail(*prev)
            scores_matmul(base - 1, ALL_CHAINS)
            for c, diag in modes:
                _sb_weights(zl_s.at[c], suf_s.at[c], run_s.at[c], tot_s.at[c], a_s.at[c], t, diag)

        def tail(base, chains):
            for c in chains:
                acc_s[c] += jnp.dot(a_s[c], vb[key_block(base, c), :], preferred_element_type=F32)

        first = 2 * qi
        scores_matmul(first, ALL_CHAINS)
        front([(c, True) for c in ALL_CHAINS], first)

        def loop(state):
            it = state[0]
            base = first - 1 - it
            front([(c, False) for c in ALL_CHAINS], base, prev=(base + 1, ALL_CHAINS))
            return it + 1, _any_weight_left(run_s)

        done, live = lax.while_loop(lambda state: (state[0] < first) & (state[1] > 0), loop,
                                    (jnp.int32(0), jnp.int32(1)))
        one_more = (done == first) & (live > 0)

        @pl.when(one_more)
        def _():
            front([(c, False) for c in UPPER_CHAINS], -1, prev=(0, ALL_CHAINS))
            tail(-1, UPPER_CHAINS)

        @pl.when(jnp.logical_not(one_more))
        def _():
            tail(first - done, ALL_CHAINS)

        for half in range(2):
            rows = slice(half * t, (half + 1) * t)
            o = jnp.where(low, acc_s[2 * half], acc_s[2 * half + 1])
            o_ref[rows, :] = o
            r = lax.rsqrt(_half_mean(o * o, low) + EPS)
            cat_ref[rows, :] = (o * r * g_ref[...]).astype(BF16)

        @pl.when((pl.program_id(0) == nh - 1) & (qi == nq - 1))
        def _():
            for cp in plan.copies(staged_refs, gathered_refs, gather_sems):
                cp.wait()

    whole = lambda col0: pl.BlockSpec((s, LANES), lambda h, i: (0, col0 + h))
    n_ch = len(ALL_CHAINS)
    res = _pcall(
        body, name="attn_fwd", grid=(nh, nq),
        in_specs=[pl.BlockSpec((tq, LANES), lambda h, i: (i, q0 + h)),
                  whole(q0 + nh), whole(q0 + 2 * nh),
                  pl.BlockSpec((1, LANES), lambda h, i: (0, h)),
                  pl.BlockSpec(memory_space=pl.ANY)] + [pl.BlockSpec(memory_space=pl.ANY)] * nw,
        out_specs=[pl.BlockSpec((tq, LANES), lambda h, i: (i, h)),
                   pl.BlockSpec((tq, LANES), lambda h, i: (i, w_conv // LANES + h))]
        + [pl.BlockSpec(memory_space=pl.ANY)] * n_res,
        out_shape=[jax.ShapeDtypeStruct((s, w_attn), F32),
                   jax.ShapeDtypeStruct(cat.shape, BF16)] + plan.out_shapes,
        scratch_shapes=[pltpu.VMEM((s, LANES), BF16), pltpu.VMEM((s, LANES), BF16),
                        pltpu.VMEM((2 * t, t), BF16),
                        pltpu.VMEM((n_ch, t, LANES), BF16),
                        pltpu.VMEM((n_ch, t, t), F32),
                        pltpu.VMEM((n_ch, t, 2 * t), BF16),
                        pltpu.VMEM((n_ch, t, t), F32),
                        pltpu.VMEM((n_ch, t, t), F32),
                        pltpu.VMEM((n_ch, t, t), BF16),
                        pltpu.VMEM((n_ch, t, LANES), F32),
                        pltpu.VMEM((n_ch, t, LANES), F32),
                        pltpu.VMEM((n_ch, t, LANES), F32)]
        + plan.sems,
        input_output_aliases={4: 1, **plan.aliases(5, 2)},
        compiler_params=_params(("arbitrary", "arbitrary")),
    )(proj, proj, proj, g_attn, cat, *plan.inputs)
    return res[0], res[1], res[2:]


def _attn_bwd(proj, o, dcat, g_attn, w_conv, carry, t=ATTN_BLOCK):
    s, n_cols = proj.shape
    w_attn = g_attn.shape[1]
    nh = w_attn // LANES
    t = _tile(s, t)
    tq = 2 * t
    nq = s // tq
    q0 = 3 * w_conv // LANES
    scale = HEAD_DIM ** -0.5
    plan = _Carried(carry)
    nw, n_res = len(plan.inputs), len(plan.out_shapes)

    def body(q_ref, k_ref, v_ref, o_ref, do_ref, g_ref, *rest):
        partial_refs, rest = rest[:nw], rest[nw:]
        dproj_ref, dg_ref = rest[:2]
        received_refs, rest = rest[2:2 + n_res], rest[2 + n_res:]
        (kb, vb, dkt_acc, dvt_acc, stash, tri_s, qm_s, dom_s, qt_s, dot_s, z_s, da_s, split_s, zl_s,
         keep_s, suf_s, a_s, glog_s, gsplit_s, cum_s, dz_s, run_s, tot_s, rest_s, gtot_s, dq_s) = rest[:26]
        out_sems, scatter_sems = rest[26], rest[27:]
        step_i = pl.program_id(1)
        qi = nq - 1 - step_i
        head_pair = pl.program_id(0)
        first_step = (head_pair == 0) & (step_i == 0)
        last_step = (head_pair == nh - 1) & (step_i == nq - 1)

        @pl.when(first_step)
        def _():
            for cp in plan.copies(partial_refs, received_refs, scatter_sems):
                cp.start()

        def out_copies():
            rows = pl.ds(pl.multiple_of(qi * tq, tq), tq)
            return [pltpu.make_async_copy(
                stash.at[w], dproj_ref.at[rows, pl.ds(pl.multiple_of((q0 + w * nh + head_pair) * LANES, LANES), LANES)],
                out_sems.at[w]) for w in range(3)]

        def walk():
            @pl.when(step_i == 0)
            def _():
                kb[...] = k_ref[...].astype(BF16)
                vb[...] = v_ref[...].astype(BF16)
                tri_s[...] = _suffix_operator(t)
                dkt_acc[...] = jnp.zeros_like(dkt_acc)
                dvt_acc[...] = jnp.zeros_like(dvt_acc)
                dg_ref[...] = jnp.zeros_like(dg_ref)

            low = _low_half()
            gv = g_ref[...]
            for half in range(2):
                rows = slice(half * t, (half + 1) * t)
                q = q_ref[rows, :] * scale
                ov = o_ref[rows, :]
                d_o, dgn = _head_norm_bwd(ov, do_ref[rows, :], gv, low)
                dg_ref[...] += jnp.sum(dgn, axis=0, keepdims=True)
                for h, msk in enumerate((low, jnp.logical_not(low))):
                    c = 2 * half + h
                    qh = jnp.where(msk, q, 0.0)
                    doh = jnp.where(msk, d_o, 0.0)
                    dom = doh.astype(BF16)
                    qm_s[c] = qh.astype(BF16)
                    dom_s[c] = dom
                    qt_s[c] = qh.T.astype(BF16)
                    dot_s[c] = doh.T.astype(BF16)
                    rest_s[c] = _row_sum(dom.astype(F32) * ov)
                    run_s[c] = jnp.zeros((t, LANES), F32)
                    dq_s[c] = jnp.zeros((t, LANES), F32)

            def key_rows(kblk):
                return pl.ds(pl.multiple_of(kblk * t, t), t)

            def block_of(base, half):
                return jnp.maximum(base + half, 0)

            def scores_matmul(base, chains):
                for c in chains:
                    ks = kb[key_rows(block_of(base, c // 2)), :]
                    z_s[c] = lax.dot_general(qm_s[c], ks, _NT, preferred_element_type=F32)

            def da_matmul(base, chains):
                for c in chains:
                    vs = vb[key_rows(block_of(base, c // 2)), :]
                    da_s[c] = lax.dot_general(dom_s[c], vs, _NT, preferred_element_type=F32)

            def front(modes, base, prev=None):
                if prev is not None:
                    tail(*prev)
                for c, diag in modes:
                    _sb_scores(z_s.at[c], split_s.at[c], zl_s.at[c], tot_s.at[c], keep_s.at[c], t, diag)
                    suf_s[c] = jnp.dot(split_s[c], tri_s[...], preferred_element_type=F32)
                scores_matmul(base - 1, ALL_CHAINS)
                for c, diag in modes:
                    _sb_weights(zl_s.at[c], suf_s.at[c], run_s.at[c], tot_s.at[c], a_s.at[c], t, diag,
                                da_s.at[c], glog_s.at[c], gsplit_s.at[c], gtot_s.at[c])
                    cum_s[c] = jnp.dot(gsplit_s[c], tri_s[...], preferred_element_type=F32)
                da_matmul(base - 1, ALL_CHAINS)
                for c, diag in modes:
                    _sb_dscores(glog_s.at[c], cum_s.at[c], rest_s.at[c], gtot_s.at[c], keep_s.at[c],
                                dz_s.at[c], t, diag)

            def tail(base, chains):
                for half in range(2):
                    mine = [c for c in chains if c // 2 == half]
                    if not mine:
                        continue
                    kblk = block_of(base, half)
                    ks = kb[key_rows(kblk), :]
                    dkt = dkt_acc[kblk]
                    dvt = dvt_acc[kblk]
                    for c in mine:
                        dq_s[c] += jnp.dot(dz_s[c], ks, preferred_element_type=F32)
                        dkt = dkt + jnp.dot(qt_s[c], dz_s[c], preferred_element_type=F32)
                        dvt = dvt + jnp.dot(dot_s[c], a_s[c], preferred_element_type=F32)
                    dkt_acc[kblk] = dkt
                    dvt_acc[kblk] = dvt

            first = 2 * qi
            scores_matmul(first, ALL_CHAINS)
            da_matmul(first, ALL_CHAINS)
            front([(c, True) for c in ALL_CHAINS], first)

            def loop(state):
                it = state[0]
                base = first - 1 - it
                front([(c, False) for c in ALL_CHAINS], base, prev=(base + 1, ALL_CHAINS))
                return it + 1, _any_weight_left(run_s)

            done, live = lax.while_loop(lambda state: (state[0] < first) & (state[1] > 0), loop,
                                        (jnp.int32(0), jnp.int32(1)))
            one_more = (done == first) & (live > 0)

            @pl.when(one_more)
            def _():
                front([(c, False) for c in UPPER_CHAINS], -1, prev=(0, ALL_CHAINS))
                tail(-1, UPPER_CHAINS)

            @pl.when(jnp.logical_not(one_more))
            def _():
                tail(first - done, ALL_CHAINS)

            @pl.when(jnp.logical_not(first_step))
            def _():
                for cp in out_copies():
                    cp.wait()

            for half in range(2):
                rows = slice(half * t, (half + 1) * t)
                stash[0, rows, :] = (jnp.where(low, dq_s[2 * half], dq_s[2 * half + 1]) * scale).astype(BF16)
                stash[1, rows, :] = dkt_acc[2 * qi + half].T.astype(BF16)
                stash[2, rows, :] = dvt_acc[2 * qi + half].T.astype(BF16)
            for cp in out_copies():
                cp.start()

        walk()

        @pl.when(last_step)
        def _():
            for cp in out_copies():
                cp.wait()
            for cp in plan.copies(partial_refs, received_refs, scatter_sems):
                cp.wait()

    whole = lambda col0: pl.BlockSpec((s, LANES), lambda h, i: (0, col0 + h))
    blk = lambda col0: pl.BlockSpec((tq, LANES), lambda h, i: (nq - 1 - i, col0 + h))
    n_ch = len(ALL_CHAINS)
    res = _pcall(
        body, name="attn_bwd", grid=(nh, nq),
        in_specs=[blk(q0), whole(q0 + nh), whole(q0 + 2 * nh), blk(0), blk(w_conv // LANES),
                  pl.BlockSpec((1, LANES), lambda h, i: (0, h))] + [pl.BlockSpec(memory_space=pl.ANY)] * nw,
        out_specs=[pl.BlockSpec(memory_space=pl.ANY),
                   pl.BlockSpec((1, LANES), lambda h, i: (0, h))] + [pl.BlockSpec(memory_space=pl.ANY)] * n_res,
        out_shape=[jax.ShapeDtypeStruct((s, n_cols), BF16), jax.ShapeDtypeStruct((1, w_attn), F32)]
        + plan.out_shapes,
        scratch_shapes=[pltpu.VMEM((s, LANES), BF16), pltpu.VMEM((s, LANES), BF16),
                        pltpu.VMEM((s // t, LANES, t), F32),
                        pltpu.VMEM((s // t, LANES, t), F32),
                        pltpu.VMEM((3, tq, LANES), BF16),
                        pltpu.VMEM((2 * t, t), BF16),
                        pltpu.VMEM((n_ch, t, LANES), BF16),
                        pltpu.VMEM((n_ch, t, LANES), BF16),
                        pltpu.VMEM((n_ch, LANES, t), BF16),
                        pltpu.VMEM((n_ch, LANES, t), BF16),
                        pltpu.VMEM((n_ch, t, t), F32),
                        pltpu.VMEM((n_ch, t, t), F32),
                        pltpu.VMEM((n_ch, t, 2 * t), BF16),
                        pltpu.VMEM((n_ch, t, t), F32),
                        pltpu.VMEM((n_ch, t, t), F32),
                        pltpu.VMEM((n_ch, t, t), F32),
                        pltpu.VMEM((n_ch, t, t), BF16),
                        pltpu.VMEM((n_ch, t, t), F32),
                        pltpu.VMEM((n_ch, t, 2 * t), BF16),
                        pltpu.VMEM((n_ch, t, t), F32),
                        pltpu.VMEM((n_ch, t, t), BF16),
                        pltpu.VMEM((n_ch, t, LANES), F32),
                        pltpu.VMEM((n_ch, t, LANES), F32),
                        pltpu.VMEM((n_ch, t, LANES), F32),
                        pltpu.VMEM((n_ch, t, LANES), F32),
                        pltpu.VMEM((n_ch, t, LANES), F32),
                        pltpu.SemaphoreType.DMA((3,))]
        + plan.sems,
        input_output_aliases=plan.aliases(6, 2),
        compiler_params=_params(("arbitrary", "arbitrary")),
    )(proj, proj, proj, o, dcat, g_attn, *plan.inputs)
    return res[0], res[1], res[2:]


def _place():
    return lax.axis_index("x"), lax.axis_index("y"), lax.axis_index("c")


def _other_chips(x, y):
    return [(1 - x, y), (x, 1 - y), (1 - x, 1 - y)]


def _slot(px, py, pc):
    return 4 * px + 2 * py + pc


def _all_gather(shards, out_dtypes):
    nw = len(shards)

    def body(*refs):
        ins, outs, stage = refs[:nw], refs[nw:2 * nw], refs[2 * nw:3 * nw]
        send_sems, recv_sems, local_sems = refs[3 * nw:]
        x, y, c = _place()
        me, sibling = (x, y, c), (x, y, 1 - c)
        chips = _other_chips(x, y)

        def copy(w, k, block, to, src=None):
            dst = outs[w].at[_slot(*block)]
            return pltpu.make_async_remote_copy(
                src_ref=dst if src is None else src, dst_ref=dst,
                send_sem=send_sems.at[w * 7 + k], recv_sem=recv_sems.at[w * 7 + k],
                device_id=to, device_id_type=MESH)

        started = []
        local = []
        for w in range(nw):
            stage[w][...] = ins[w][...].astype(stage[w].dtype)
            cp = pltpu.make_async_copy(stage[w], outs[w].at[_slot(*me)], local_sems.at[w])
            cp.start()
            local.append(cp)
            started.append(copy(w, 0, me, sibling, src=stage[w]))
            started[-1].start()
            for j, chip in enumerate(chips):
                started.append(copy(w, 1 + j, me, (*chip, c), src=stage[w]))
                started[-1].start()
        for j, chip in enumerate(chips):
            for w in range(nw):
                copy(w, 1 + j, (*chip, c), me).wait_recv()
                started.append(copy(w, 4 + j, (*chip, c), sibling))
                started[-1].start()
        for w in range(nw):
            copy(w, 0, sibling, me).wait_recv()
            for j, chip in enumerate(chips):
                copy(w, 4 + j, (*chip, 1 - c), me).wait_recv()
        for cp in started:
            cp.wait_send()
        for cp in local:
            cp.wait()

    return _pcall(
        body, name="all_gather_weights",
        in_specs=[pl.BlockSpec(memory_space=pltpu.VMEM)] * nw,
        out_specs=[pl.BlockSpec(memory_space=pl.ANY)] * nw,
        out_shape=[jax.ShapeDtypeStruct((N_DEV, *a.shape), d) for a, d in zip(shards, out_dtypes)],
        scratch_shapes=[pltpu.VMEM(a.shape, d) for a, d in zip(shards, out_dtypes)]
        + [pltpu.SemaphoreType.DMA((7 * nw,)), pltpu.SemaphoreType.DMA((7 * nw,)),
           pltpu.SemaphoreType.DMA((nw,))],
        compiler_params=_params(),
    )(*shards)


N_PEERS = N_DEV - 1


def _peer(k):
    x, y, c = _place()
    return (x ^ (k >> 2), y ^ ((k >> 1) & 1), c ^ (k & 1))


def _remote(src, dst, sems, index, to):
    return pltpu.make_async_remote_copy(src_ref=src, dst_ref=dst, send_sem=sems[0].at[index],
                                        recv_sem=sems[1].at[index], device_id=to, device_id_type=MESH)


def _gather_out_copies(staged, gathered, sems):
    x, y, c = _place()
    me = _slot(x, y, c)
    targets = [(x, y, 1 - c)] + [(*chip, c) for chip in _other_chips(x, y)]
    copies = []
    for w, (src, dst) in enumerate(zip(staged, gathered)):
        copies.append(pltpu.make_async_copy(src, dst.at[me], sems[2].at[w]))
        copies += [_remote(src, dst.at[me], sems, w * len(targets) + k, to) for k, to in enumerate(targets)]
    return copies


def _gather_pass_copies(arrived, gathered, sems):
    x, y, c = _place()
    chips = _other_chips(x, y)
    return [_remote(src.at[_slot(*chip, c)], dst.at[_slot(*chip, c)], sems, w * len(chips) + j, (x, y, 1 - c))
            for w, (src, dst) in enumerate(zip(arrived, gathered)) for j, chip in enumerate(chips)]


ALL_PEERS = tuple(range(1, N_DEV))


def _scatter_copies(partials, received, sems, peers=ALL_PEERS):
    me = _slot(*_place())
    return [_remote(src.at[me ^ k], dst.at[k - 1], sems, w * N_PEERS + k - 1, _peer(k))
            for w, (src, dst) in enumerate(zip(partials, received)) for k in peers]


class _Carried:
    def __init__(self, jobs):
        self.jobs = [(job[0], list(job[1]), job[2] if len(job) > 2 else ALL_PEERS) for job in jobs if len(job[1])]
        self.inputs, self.out_shapes, self.sems, self.counts = [], [], [], []
        for kind, arrays, _ in self.jobs:
            n_out = len(arrays) // 2 if kind == "scatter_more" else len(arrays)
            fan = {"gather_out": 4, "gather_pass": 3}.get(kind, N_PEERS)
            for a in arrays[len(arrays) - n_out:]:
                shape = {"gather_out": (N_DEV, *a.shape), "scatter": (N_PEERS, *a.shape[1:])}.get(kind, a.shape)
                self.out_shapes.append(jax.ShapeDtypeStruct(shape, BF16))
            job_sems = [pltpu.SemaphoreType.DMA((fan * n_out,))] * 2
            job_sems += [pltpu.SemaphoreType.DMA((n_out,))] if kind == "gather_out" else []
            self.inputs += arrays
            self.sems += job_sems
            self.counts.append((len(arrays), n_out, len(job_sems)))

    def aliases(self, first_input, first_output):
        pairs, at_in, at_out = {}, first_input, first_output
        for (kind, _, _), (n_in, n_out, _) in zip(self.jobs, self.counts):
            if kind in ("gather_pass", "scatter_more"):
                pairs.update({at_in + n_in - n_out + i: at_out + i for i in range(n_out)})
            at_in, at_out = at_in + n_in, at_out + n_out
        return pairs

    def copies(self, in_refs, out_refs, sem_refs):
        out, at_in, at_out, at_sem = [], 0, 0, 0
        for (kind, _, peers), (n_in, n_out, n_sems) in zip(self.jobs, self.counts):
            srcs, dsts = in_refs[at_in:at_in + n_out], out_refs[at_out:at_out + n_out]
            sems = sem_refs[at_sem:at_sem + n_sems]
            if kind == "gather_out":
                out += _gather_out_copies(srcs, dsts, sems)
            elif kind == "gather_pass":
                out += _gather_pass_copies(srcs, dsts, sems)
            else:
                out += _scatter_copies(srcs, dsts, sems, peers)
            at_in, at_out, at_sem = at_in + n_in, at_out + n_out, at_sem + n_sems
        return out


def _cast_shards(shards):
    def body(*refs):
        for src, dst in zip(refs[:len(shards)], refs[len(shards):]):
            dst[...] = src[...].astype(BF16)

    return _pcall(
        body, name="cast_shards",
        in_specs=[pl.BlockSpec(memory_space=pltpu.VMEM)] * len(shards),
        out_specs=[pl.BlockSpec(memory_space=pltpu.VMEM)] * len(shards),
        out_shape=[jax.ShapeDtypeStruct(a.shape, BF16) for a in shards],
        compiler_params=_params(),
    )(*shards)


def _all_reduce_small(packed):
    r = packed.shape[0]

    def body(x_ref, o_ref, gathered, send_sems, recv_sems):
        x, y, c = _place()
        me = _slot(x, y, c)
        gathered[me] = x_ref[...]
        copies = [_remote(x_ref, gathered.at[me], (send_sems, recv_sems), k - 1, _peer(k)) for k in ALL_PEERS]
        for cp in copies:
            cp.start()
        for cp in copies:
            cp.wait()
        total = gathered[0]
        for k in range(1, N_DEV):
            total = total + gathered[k]
        o_ref[...] = total

    return _pcall(
        body, name="all_reduce_small",
        in_specs=[pl.BlockSpec(memory_space=pltpu.VMEM)],
        out_specs=pl.BlockSpec(memory_space=pltpu.VMEM),
        out_shape=jax.ShapeDtypeStruct(packed.shape, F32),
        scratch_shapes=[pltpu.VMEM((N_DEV, r, LANES), F32),
                        pltpu.SemaphoreType.DMA((N_DEV - 1,)), pltpu.SemaphoreType.DMA((N_DEV - 1,))],
        compiler_params=_params(),
    )(packed)


def _adam_math(w, g, m, v):
    m = ADAM_B1 * m + (1.0 - ADAM_B1) * g
    v = ADAM_B2 * v + (1.0 - ADAM_B2) * jnp.square(g)
    m_hat = m / (1.0 - ADAM_B1 ** ADAM_STEP)
    v_hat = v / (1.0 - ADAM_B2 ** ADAM_STEP)
    delta = -ADAM_LR * (m_hat / (jnp.sqrt(v_hat) + ADAM_EPS) + ADAM_WD * w)
    return delta, m, v


ADAM_TILE_BYTES = 24 * 1024 * 1024


def _adam_sharded(name, own, received, w, m, v, place):
    r, cdim = w.shape
    row_bytes = 2 * cdim * (4 + 2 * N_PEERS + 3 * 4 + 4 * 4)
    tr = _tile(r, max(LANES, ADAM_TILE_BYTES // row_bytes // LANES * LANES)) if r % LANES == 0 else r

    def body(place_ref, own_ref, rec_ref, w_ref, m_ref, v_ref, g_ref, d_ref, nm_ref, nv_ref):
        del place_ref
        g = own_ref[...]
        for j in range(N_PEERS):
            g = g + rec_ref[j].astype(F32)
        delta, nm, nv = _adam_math(w_ref[...], g, m_ref[...], v_ref[...])
        g_ref[...] = g
        d_ref[...] = delta
        nm_ref[...] = nm
        nv_ref[...] = nv

    blk = pl.BlockSpec((tr, cdim), lambda i, pr: (i, 0))
    grid_spec = pltpu.PrefetchScalarGridSpec(
        num_scalar_prefetch=1, grid=(r // tr,),
        in_specs=[pl.BlockSpec((None, tr, cdim), lambda i, pr: (4 * pr[0] + 2 * pr[1] + pr[2], i, 0)),
                  pl.BlockSpec((N_PEERS, tr, cdim), lambda i, pr: (0, i, 0)), blk, blk, blk],
        out_specs=[blk] * 4)
    return _pcall(body, name=name, grid_spec=grid_spec,
                  out_shape=[jax.ShapeDtypeStruct((r, cdim), F32)] * 4,
                  compiler_params=_params(("parallel",)))(place, own, received, w, m, v)


def _adam_small(w, g, m, v):
    def body(w_ref, g_ref, m_ref, v_ref, d_ref, nm_ref, nv_ref):
        delta, nm, nv = _adam_math(w_ref[...], g_ref[...], m_ref[...], v_ref[...])
        d_ref[...] = delta
        nm_ref[...] = nm
        nv_ref[...] = nv

    return _pcall(body, name="adam_small",
                  in_specs=[pl.BlockSpec(memory_space=pltpu.VMEM)] * 4,
                  out_specs=[pl.BlockSpec(memory_space=pltpu.VMEM)] * 3,
                  out_shape=[jax.ShapeDtypeStruct(w.shape, F32)] * 3,
                  compiler_params=_params())(w, g, m, v)


def _rows(vec):
    return vec.reshape(-1, LANES)


def kernel(x, p, g_mix, w_in, conv_w, g_conv_out, g_attn_out, w_out, g_mlp, w_up, w_down, g_ple, w_ple_gate, w_ple_proj, g_final, loss_target, m_g_mix, m_w_in, m_conv_w, m_g_conv_out, m_g_attn_out, m_w_out, m_g_mlp, m_w_up, m_w_down, m_g_ple, m_w_ple_gate, m_w_ple_proj, m_g_final, v_g_mix, v_w_in, v_conv_w, v_g_conv_out, v_g_attn_out, v_w_out, v_g_mlp, v_w_up, v_w_down, v_g_ple, v_w_ple_gate, v_w_ple_proj, v_g_final):
    s, d = x.shape[1], x.shape[2]
    w_conv = g_conv_out.shape[1]
    w_attn = g_attn_out.shape[1]
    cw = conv_w.shape[2]
    xs, ps, tgt = x[0], p[0, 0], loss_target[0]
    place = jnp.stack([lax.axis_index("x"), lax.axis_index("y"), lax.axis_index("c")]).astype(jnp.int32)
    my_slot = 4 * place[0] + 2 * place[1] + place[2]

    conv_tile = jnp.pad(conv_w[0], ((0, HALO - CONV_K), (0, LANES - cw)))
    big = [w_in[0], w_out[0], w_up[0], w_down[0], w_ple_gate[0], w_ple_proj[0]]
    win_g, conv_g = _all_gather([big[0], conv_tile], [BF16, F32])
    s_out, s_up, s_down, s_gate, s_proj = _cast_shards(big[1:])
    conv_full = jnp.transpose(conv_g[:, :CONV_K, :cw], (1, 0, 2)).reshape(CONV_K, w_conv)
    in_shard, up_shard, proj_shard = big[0].shape[1], big[2].shape[1], big[5].shape[1]

    proj, a, g_out, g_gate, g_proj = _mm_nn("in_proj", xs, win_g, n_shard=in_shard, tn=in_shard, tm=2048,
                                            lhs_norm=g_mix, carry=[("gather_out", [s_out, s_gate, s_proj])])
    cat = _conv_fwd(proj, conv_full, g_conv_out, w_conv, d)
    o, cat, (g_up, g_down, wout_g, wgate_g, wproj_g) = _attn_fwd(
        proj, g_attn_out, cat, w_conv,
        [("gather_out", [s_up, s_down]), ("gather_pass", [g_out, g_gate, g_proj])])
    wout_f = wout_g.reshape(-1, wout_g.shape[-1])
    wgate_f = wgate_g.reshape(-1, wgate_g.shape[-1])
    h1, wup_g = _mm_nn("out_proj", cat, wout_f, epilogue=_ep_residual, extras=(xs,),
                       carry=[("gather_pass", [g_up])])
    act, mn, wdown_g = _mm_nn("mlp_up", h1, wup_g, n_shard=up_shard, epilogue=_ep_up, out_dtypes=(BF16,), tm=2048,
                              lhs_norm=g_mlp, carry=[("gather_pass", [g_down])])
    wdown_f = wdown_g.reshape(-1, wdown_g.shape[-1])
    h2, = _mm_nn("mlp_down", act, wdown_f, epilogue=_ep_residual, extras=(h1,))
    pp = _ple_proj(ps, wproj_g)
    loss_part, dh3, dgl, dpp, dg_final, n3 = _ple_gate_loss(h2, g_ple, wgate_f, pp, tgt, g_final.reshape(1, d))

    def slots(t2d):
        return t2d.reshape(N_DEV, -1, t2d.shape[-1])

    dw_proj = _d_ple_proj(ps, dpp, proj_shard)
    dw_gate = [slots(t) for t in _mm_tn("d_w_ple_gate", n3, dgl)]
    dh2, dh2b, dg_ple = _mm_nt_norm_bwd("d_norm_ple", dgl, wgate_f, h2, g_ple, dh3)
    du, gate_recv, proj_recv = _mm_nt("d_mlp_act", dh2b, wdown_f, epilogue=_ep_dact, out_dtypes=(BF16,),
                                      extras=(act,), tm=2048, carry=[("scatter", [dw_gate[1], dw_proj[1]])])
    dw_down = [slots(t) for t in _mm_tn("d_w_down", act, dh2b)]
    near, far = (1, 2, 3, 4, 5), (6, 7)
    dw_up = _mm_tn("d_w_up", mn, du, n_shard=up_shard)
    dh1, dh1b, dg_mlp, down_part = _mm_nt_norm_bwd(
        "d_norm_mlp", du, wup_g, h1, g_mlp, dh2, k_shard=up_shard, tm=1024,
        carry=[("scatter", [dw_down[1]], near)])
    dcat, = _mm_nt("d_cat", dh1b, wout_f)
    dw_out = [slots(t) for t in _mm_tn("d_w_out", cat, dh1b)]
    dproj, dg_attn, (down_recv, up_recv) = _attn_bwd(
        proj, o, dcat, g_attn_out, w_conv,
        [("scatter_more", [dw_down[1], down_part], far), ("scatter", [dw_up[1]])])
    dproj, dconv, dg_conv = _conv_bwd(proj, dcat, conv_full, g_conv_out, dproj, w_conv)
    *dw_in, out_recv = _mm_tn("d_w_in", a, dproj, n_shard=in_shard, tn=in_shard,
                              carry=[("scatter", [dw_out[1]])])
    grad_x, _, dg_mix, in_recv = _mm_nt_norm_bwd("d_norm_mix", dproj, win_g, xs, g_mix, dh1, k_shard=in_shard,
                                                 tk=2 * in_shard, tm=1024, carry=[("scatter", [dw_in[1]])])

    names = ["w_in", "w_out", "w_up", "w_down", "w_ple_gate", "w_ple_proj"]
    owns = [dw_in[0], dw_out[0], dw_up[0], dw_down[0], dw_gate[0], dw_proj[0]]
    recvs = [in_recv, out_recv, up_recv, down_recv, gate_recv, proj_recv]
    moments = [(m_w_in, v_w_in), (m_w_out, v_w_out), (m_w_up, v_w_up), (m_w_down, v_w_down),
               (m_w_ple_gate, v_w_ple_gate), (m_w_ple_proj, v_w_ple_proj)]
    big_out = {}
    for n, own, rc, wt, (mm, vv) in zip(names, owns, recvs, big, moments):
        big_out[n] = [t[None] for t in _adam_sharded("adam_" + n, own, rc, wt, mm[0], vv[0], place)]

    n_conv_rows = CONV_K * w_conv // LANES
    small_g = jnp.concatenate(
        [_rows(dg_mix[0]), _rows(dg_conv[0]), _rows(dg_attn[0]), _rows(dg_mlp[0]), _rows(dg_ple[0]),
         _rows(dg_final[0]), _rows(dconv.reshape(-1)), loss_part], axis=0)
    n_gain_rows = small_g.shape[0] - n_conv_rows - 1
    pad_rows = (-small_g.shape[0]) % HALO
    small_g = _all_reduce_small(jnp.pad(small_g, ((0, pad_rows), (0, 0))))
    loss = small_g[n_gain_rows + n_conv_rows, 0]
    dconv_full = small_g[n_gain_rows:n_gain_rows + n_conv_rows].reshape(CONV_K, w_conv)
    dconv_mine = lax.dynamic_slice(dconv_full, (0, my_slot * cw), (CONV_K, cw))

    def pack(vecs, conv_part):
        rows = [_rows(t.reshape(-1)) for t in vecs]
        rows.append(jnp.pad(conv_part, ((0, HALO - CONV_K), (0, LANES - cw))))
        return jnp.concatenate(rows, axis=0)

    gains = [g_mix, g_conv_out, g_attn_out, g_mlp, g_ple, g_final]
    gains_m = [m_g_mix, m_g_conv_out, m_g_attn_out, m_g_mlp, m_g_ple, m_g_final]
    gains_v = [v_g_mix, v_g_conv_out, v_g_attn_out, v_g_mlp, v_g_ple, v_g_final]
    gpack = jnp.concatenate([small_g[:n_gain_rows], jnp.pad(dconv_mine, ((0, HALO - CONV_K), (0, LANES - cw)))], axis=0)
    sd, sm, sv = _adam_small(pack(gains, conv_w[0]), gpack, pack(gains_m, m_conv_w[0]), pack(gains_v, v_conv_w[0]))

    def unpack(packed):
        out, r0 = [], 0
        for t in gains:
            nr = t.size // LANES
            out.append(packed[r0:r0 + nr].reshape(t.shape))
            r0 += nr
        out.append(packed[r0:r0 + CONV_K, :cw][None])
        return out

    sg_l, sd_l, sm_l, sv_l = unpack(gpack), unpack(sd), unpack(sm), unpack(sv)
    small_names = ["g_mix", "g_conv_out", "g_attn_out", "g_mlp", "g_ple", "g_final", "conv_w"]
    small_out = {n: [sg_l[i], sd_l[i], sm_l[i], sv_l[i]] for i, n in enumerate(small_names)}

    order = ["g_mix", "w_in", "conv_w", "g_conv_out", "g_attn_out", "w_out", "g_mlp", "w_up", "w_down",
             "g_ple", "w_ple_gate", "w_ple_proj", "g_final"]
    table = {**big_out, **small_out}
    outs = [loss, grad_x[None]]
    for kind in range(4):
        outs.extend(table[n][kind] for n in order)
    return tuple(outs)
```

```python
import jax
import jax.numpy as jnp
from jax import lax
from jax.experimental import pallas as pl
from jax.experimental.pallas import tpu as pltpu

F32 = jnp.float32
BF16 = jnp.bfloat16
EPS = 1e-6
HEAD_DIM = 64
LANES = 128
CONV_K = 3
MXU_WIDTH = 256
ATTN_BLOCK = MXU_WIDTH
HALO = 8
N_DEV = 8
MESH = pl.DeviceIdType.MESH
VMEM_LIMIT = 56 * 1024 * 1024

ADAM_LR = 0.001
ADAM_B1 = 0.9
ADAM_B2 = 0.999
ADAM_EPS = 1e-08
ADAM_WD = 0.01
ADAM_STEP = 10


def _pcall(body, **kw):
    return pl.pallas_call(body, **kw)


def _params(sem=None, **kw):
    return pltpu.CompilerParams(dimension_semantics=sem, vmem_limit_bytes=VMEM_LIMIT, **kw)


def _tile(dim, pref):
    t = min(dim, pref)
    while dim % t:
        t -= LANES
    assert t > 0, (dim, pref)
    return t


_NN = (((1,), (0,)), ((), ()))
_NT = (((1,), (1,)), ((), ()))
_TN = (((0,), (0,)), ((), ()))


def _ep_store(acc, outs):
    outs[0][...] = acc.astype(outs[0].dtype)


def _ep_both(acc, outs):
    outs[0][...] = acc
    outs[1][...] = acc.astype(BF16)


def _ep_residual(acc, res, outs):
    outs[0][...] = acc + res[...]


def _ep_up(acc, outs):
    outs[0][...] = jnp.square(jnp.maximum(acc, 0.0)).astype(BF16)


def _ep_dact(acc, act, outs):
    outs[0][...] = (acc * (2.0 * jnp.sqrt(act[...].astype(F32)))).astype(BF16)


def _ep_norm_bwd(acc, h, g, dres, outs):
    @pl.when(pl.program_id(0) == 0)
    def _():
        outs[2][...] = jnp.zeros_like(outs[2])

    hv = h[...]
    r = lax.rsqrt(jnp.mean(hv * hv, axis=-1, keepdims=True) + EPS)
    hn = hv * r
    outs[2][...] += jnp.sum(acc * hn, axis=0, keepdims=True)
    dhn = acc * g[...]
    dh = dres[...] + r * (dhn - hn * jnp.mean(dhn * hn, axis=-1, keepdims=True))
    outs[0][...] = dh
    outs[1][...] = dh.astype(BF16)


def _matmul(name, a, b, *, dims, grid, a_spec, b_spec, acc_shape, out_shapes, out_specs,
            epilogue=_ep_store, extras=(), extra_specs=(), carry=(), sequential=False, lhs_norm=False):
    nk = grid[2]
    plan = _Carried(carry)
    n_ex, n_out, n_xc, n_xo = len(extras), len(out_shapes), len(plan.inputs), len(plan.out_shapes)
    n_sems = len(plan.sems)
    last = tuple(g - 1 for g in grid)
    assert not lhs_norm or nk == 1

    def product(a_ref, b_ref):
        if len(b_ref.shape) == 2:
            return lax.dot_general(a_ref[...].astype(BF16), b_ref[...].astype(BF16), dims,
                                   preferred_element_type=F32)
        width = b_ref.shape[2]
        return sum(lax.dot_general(a_ref[:, g * width:(g + 1) * width].astype(BF16), b_ref[g].astype(BF16), dims,
                                   preferred_element_type=F32) for g in range(b_ref.shape[0]))

    def body(a_ref, b_ref, *rest):
        ex, rest = rest[:n_ex], rest[n_ex:]
        partials, rest = rest[:n_xc], rest[n_xc:]
        outs, rest = rest[:n_out], rest[n_out:]
        received, rest = rest[:n_xo], rest[n_xo:]
        ids = [pl.program_id(axis) for axis in range(3)]
        if n_xc:
            @pl.when((ids[0] == 0) & (ids[1] == 0) & (ids[2] == 0))
            def _():
                for cp in plan.copies(partials, received, rest[-n_sems:]):
                    cp.start()

        if lhs_norm:
            x_ref, a_ref, gain, ex, outs = a_ref, outs[-1], ex[-1], ex[:-1], outs[:-1]

            @pl.when(ids[1] == 0)
            def _():
                for m0 in range(0, acc_shape[0], MXU_WIDTH):
                    rows = slice(m0, min(m0 + MXU_WIDTH, acc_shape[0]))
                    xv = x_ref[rows, :]
                    r = lax.rsqrt(jnp.mean(xv * xv, axis=-1, keepdims=True) + EPS)
                    a_ref[rows, :] = (xv * r * gain[...]).astype(BF16)

        if nk == 1 and not sequential and dims != _TN:
            for n0 in range(0, acc_shape[1], MXU_WIDTH):
                cols = slice(n0, min(n0 + MXU_WIDTH, acc_shape[1]))
                b_cols = b_ref.at[cols, :] if dims == _NT else b_ref.at[:, cols]
                for m0 in range(0, acc_shape[0], MXU_WIDTH):
                    rows = slice(m0, min(m0 + MXU_WIDTH, acc_shape[0]))
                    epilogue(product(a_ref.at[rows, :], b_cols), *[e.at[rows, cols] for e in ex],
                             [o.at[rows, cols] for o in outs])
        elif nk == 1:
            epilogue(product(a_ref, b_ref), *ex, outs)
        else:
            acc = rest[0]

            @pl.when(ids[2] == 0)
            def _():
                acc[...] = product(a_ref, b_ref)

            @pl.when(ids[2] > 0)
            def _():
                acc[...] += product(a_ref, b_ref)

            @pl.when(ids[2] == nk - 1)
            def _():
                epilogue(acc[...], *ex, outs)

        if n_xc:
            @pl.when((ids[0] == last[0]) & (ids[1] == last[1]) & (ids[2] == last[2]))
            def _():
                for cp in plan.copies(partials, received, rest[-n_sems:]):
                    cp.wait()

    anywhere = pl.BlockSpec(memory_space=pl.ANY)
    return _pcall(
        body, name=name, grid=grid,
        in_specs=[a_spec, b_spec, *extra_specs, *[anywhere] * n_xc],
        out_specs=[*out_specs, *[anywhere] * n_xo],
        out_shape=[*out_shapes, *plan.out_shapes],
        scratch_shapes=([] if nk == 1 else [pltpu.VMEM(acc_shape, F32)]) + plan.sems,
        input_output_aliases=plan.aliases(2 + n_ex, n_out),
        compiler_params=_params(("arbitrary",) * 3 if n_xc or sequential or lhs_norm
                                else ("parallel", "parallel", "arbitrary")),
    )(a, b, *extras, *plan.inputs)


_NO_CARRY = ()


def _mm_nn(name, a, w, *, n_shard=None, epilogue=_ep_store, out_dtypes=(F32,), extras=(), carry=_NO_CARRY,
           lhs_norm=None, tm=1024, tn=1024, tk=1024):
    m, kd = a.shape
    if lhs_norm is not None:
        tk = kd
    if n_shard is None:
        n = w.shape[1]
        tn = _tile(n, tn)
        tk = _tile(kd, tk)
        b_spec = pl.BlockSpec((tk, tn), lambda i, j, k: (k, j))
    else:
        n = N_DEV * n_shard
        tn = _tile(n_shard, tn)
        tk = _tile(kd, tk)
        per = n_shard // tn
        b_spec = pl.BlockSpec((None, tk, tn), lambda i, j, k: (j // per, k, j % per))
    tm = _tile(m, tm)
    o_spec = pl.BlockSpec((tm, tn), lambda i, j, k: (i, j))
    out_shapes = [jax.ShapeDtypeStruct((m, n), d) for d in out_dtypes]
    out_specs = [o_spec] * len(out_dtypes)
    extra_specs = [o_spec] * len(extras)
    if lhs_norm is not None:
        extras = (*extras, lhs_norm)
        extra_specs.append(pl.BlockSpec((1, kd), lambda i, j, k: (0, 0)))
        out_shapes.append(jax.ShapeDtypeStruct((m, kd), BF16))
        out_specs.append(pl.BlockSpec((tm, kd), lambda i, j, k: (i, 0)))
    return _matmul(
        name, a, w, dims=_NN, grid=(m // tm, n // tn, kd // tk),
        a_spec=pl.BlockSpec((tm, tk), lambda i, j, k: (i, k)), b_spec=b_spec,
        acc_shape=(tm, tn), out_shapes=out_shapes, out_specs=out_specs,
        epilogue=epilogue, extras=extras, extra_specs=extra_specs, carry=carry, lhs_norm=lhs_norm is not None)


def _mm_nt(name, a, w, *, epilogue=_ep_store, out_dtypes=(F32,), extras=(), carry=_NO_CARRY,
           tm=1024, tn=1024, tk=1024):
    m, kd = a.shape
    n = w.shape[0]
    tm, tn, tk = _tile(m, tm), _tile(n, tn), _tile(kd, tk)
    o_spec = pl.BlockSpec((tm, tn), lambda i, j, k: (i, j))
    return _matmul(
        name, a, w, dims=_NT, grid=(m // tm, n // tn, kd // tk),
        a_spec=pl.BlockSpec((tm, tk), lambda i, j, k: (i, k)),
        b_spec=pl.BlockSpec((tn, tk), lambda i, j, k: (j, k)),
        acc_shape=(tm, tn),
        out_shapes=[jax.ShapeDtypeStruct((m, n), d) for d in out_dtypes],
        out_specs=[o_spec] * len(out_dtypes),
        epilogue=epilogue, extras=extras, extra_specs=[o_spec] * len(extras), carry=carry)


def _mm_nt_norm_bwd(name, a, w, h, g, dres, *, k_shard=None, carry=_NO_CARRY, tm=512, tk=1024):
    m, kd = a.shape
    n = h.shape[1]
    if k_shard is None:
        tk = _tile(kd, tk)
        b_spec = pl.BlockSpec((n, tk), lambda i, j, k: (0, k))
    else:
        group = max(1, min(tk // k_shard, N_DEV))
        while N_DEV % group:
            group -= 1
        tk = group * k_shard
        b_spec = pl.BlockSpec((group, n, k_shard), lambda i, j, k: (k, 0, 0))
    tm = _tile(m, tm)
    rows = pl.BlockSpec((tm, n), lambda i, j, k: (i, 0))
    vec = pl.BlockSpec((1, n), lambda i, j, k: (0, 0))
    return _matmul(
        name, a, w, dims=_NT, grid=(m // tm, 1, kd // tk),
        a_spec=pl.BlockSpec((tm, tk), lambda i, j, k: (i, k)), b_spec=b_spec, acc_shape=(tm, n),
        out_shapes=[jax.ShapeDtypeStruct((m, n), F32), jax.ShapeDtypeStruct((m, n), BF16),
                    jax.ShapeDtypeStruct((1, n), F32)],
        out_specs=[rows, rows, vec], epilogue=_ep_norm_bwd,
        extras=(h, g, dres), extra_specs=[rows, vec, rows], carry=carry, sequential=True)


TN_TILE_BYTES = 40 * 1024 * 1024


def _mm_tn(name, a, b, *, n_shard=None, carry=_NO_CARRY, tm=1024, tn=1024):
    t, m = a.shape
    n = b.shape[1]
    tm = _tile(m, tm)
    tn = _tile(n if n_shard is None else n_shard, tn)
    tk = t
    while 2 * 2 * tk * (tm + tn) + 4 * tm * tn * 5 > TN_TILE_BYTES and tk % (2 * LANES) == 0:
        tk //= 2
    if n_shard is None:
        o_spec = pl.BlockSpec((tm, tn), lambda i, j, k: (i, j))
        shape = (m, n)
    else:
        per = n_shard // tn
        o_spec = pl.BlockSpec((None, tm, tn), lambda i, j, k: (j // per, i, j % per))
        shape = (N_DEV, m, n_shard)
    return _matmul(
        name, a, b, dims=_TN, grid=(m // tm, n // tn, t // tk),
        a_spec=pl.BlockSpec((tk, tm), lambda i, j, k: (k, i)),
        b_spec=pl.BlockSpec((tk, tn), lambda i, j, k: (k, j)),
        acc_shape=(tm, tn), epilogue=_ep_both, carry=carry,
        out_shapes=[jax.ShapeDtypeStruct(shape, F32), jax.ShapeDtypeStruct(shape, BF16)],
        out_specs=[o_spec, o_spec])


def _ple_proj(p, w_g, tm=1024):
    s, kd = p.shape
    ns = w_g.shape[2]
    tm = _tile(s, tm)

    def body(p_ref, w_ref, o_ref):
        pv = p_ref[...].astype(BF16)
        for j in range(N_DEV):
            o_ref[:, j * ns:(j + 1) * ns] = jnp.dot(pv, w_ref[j], preferred_element_type=F32)

    return _pcall(body, name="ple_proj", grid=(s // tm,),
                  in_specs=[pl.BlockSpec((tm, kd), lambda i: (i, 0)),
                            pl.BlockSpec((N_DEV, kd, ns), lambda i: (0, 0, 0))],
                  out_specs=pl.BlockSpec((tm, N_DEV * ns), lambda i: (i, 0)),
                  out_shape=jax.ShapeDtypeStruct((s, N_DEV * ns), F32),
                  compiler_params=_params(("parallel",)))(p, w_g)


def _d_ple_proj(p, dpp, ns, tk=1024):
    s, kd = p.shape
    tk = _tile(s, tk)
    nk = s // tk

    def body(p_ref, d_ref, of_ref, ob_ref, acc):
        k = pl.program_id(0)

        @pl.when(k == 0)
        def _():
            acc[...] = jnp.zeros_like(acc)

        pv = p_ref[...].astype(BF16)
        for j in range(N_DEV):
            acc[j] += lax.dot_general(pv, d_ref[:, j * ns:(j + 1) * ns], _TN, preferred_element_type=F32)

        @pl.when(k == nk - 1)
        def _():
            of_ref[...] = acc[...]
            ob_ref[...] = acc[...].astype(BF16)

    whole = pl.BlockSpec((N_DEV, kd, ns), lambda k: (0, 0, 0))
    return _pcall(body, name="d_w_ple_proj", grid=(nk,),
                  in_specs=[pl.BlockSpec((tk, kd), lambda k: (k, 0)),
                            pl.BlockSpec((tk, N_DEV * ns), lambda k: (k, 0))],
                  out_specs=[whole, whole],
                  out_shape=[jax.ShapeDtypeStruct((N_DEV, kd, ns), F32), jax.ShapeDtypeStruct((N_DEV, kd, ns), BF16)],
                  scratch_shapes=[pltpu.VMEM((N_DEV, kd, ns), F32)],
                  compiler_params=_params(("arbitrary",)))(p, dpp)


def _ep_ple_loss(gl, h2, pp, tgt, g_final, outs):
    loss_ref, dh3_ref, dgl_ref, dpp_ref, dg_ref = outs

    @pl.when(pl.program_id(0) == 0)
    def _():
        dg_ref[...] = jnp.zeros_like(dg_ref)
        loss_ref[...] = jnp.zeros_like(loss_ref)

    gate = jax.nn.sigmoid(gl)
    ppv = pp[...]
    h3 = h2[...] + gate * ppv
    r = lax.rsqrt(jnp.mean(h3 * h3, axis=-1, keepdims=True) + EPS)
    hn = h3 * r
    gv = g_final[...]
    diff = hn * gv - tgt[...]
    row = jnp.mean(diff * diff, axis=-1, keepdims=True)
    loss_ref[...] += 0.5 * jnp.sum(row, axis=0, keepdims=True)
    dy = diff * (1.0 / h3.shape[-1])
    dg_ref[...] += jnp.sum(dy * hn, axis=0, keepdims=True)
    dhn = dy * gv
    dh3 = r * (dhn - hn * jnp.mean(dhn * hn, axis=-1, keepdims=True))
    dh3_ref[...] = dh3
    dgl_ref[...] = (dh3 * ppv * gate * (1.0 - gate)).astype(BF16)
    dpp_ref[...] = (dh3 * gate).astype(BF16)


def _ple_gate_loss(h2, g_ple, w_gate, pp, tgt, g_final, tm=512):
    s, d = h2.shape
    tm = _tile(s, tm)
    rows = pl.BlockSpec((tm, d), lambda i, j, k: (i, 0))
    vec = pl.BlockSpec((1, d), lambda i, j, k: (0, 0))
    return _matmul(
        "ple_gate_loss", h2, w_gate, dims=_NN, grid=(s // tm, 1, 1),
        a_spec=rows, b_spec=pl.BlockSpec((d, d), lambda i, j, k: (0, 0)), acc_shape=(tm, d),
        out_shapes=[jax.ShapeDtypeStruct((1, LANES), F32), jax.ShapeDtypeStruct((s, d), F32),
                    jax.ShapeDtypeStruct((s, d), BF16), jax.ShapeDtypeStruct((s, d), BF16),
                    jax.ShapeDtypeStruct((1, d), F32), jax.ShapeDtypeStruct((s, d), BF16)],
        out_specs=[pl.BlockSpec((1, LANES), lambda i, j, k: (0, 0)), rows, rows, rows, vec, rows],
        epilogue=_ep_ple_loss, extras=(h2, pp, tgt, g_final, g_ple), extra_specs=[rows, rows, rows, vec, vec],
        sequential=True, lhs_norm=True)


def _low_half():
    return lax.broadcasted_iota(jnp.int32, (1, LANES), 1) < HEAD_DIM


def _half_mean(v, low):
    s_lo = jnp.sum(jnp.where(low, v, 0.0), axis=-1, keepdims=True)
    s_hi = jnp.sum(jnp.where(low, 0.0, v), axis=-1, keepdims=True)
    return jnp.where(low, s_lo, s_hi) * (1.0 / HEAD_DIM)


def _head_norm_bwd(val, dout, g, low):
    r = lax.rsqrt(_half_mean(val * val, low) + EPS)
    vn = val * r
    dvn = dout * g
    return r * (dvn - vn * _half_mean(dvn * vn, low)), dout * vn


def _conv_taps(vv_ext, w_ref):
    v0 = vv_ext[HALO:]
    v1 = pltpu.roll(vv_ext, 1, 0)[HALO:]
    v2 = pltpu.roll(vv_ext, 2, 0)[HALO:]
    return w_ref[2:3, :] * v0 + w_ref[1:2, :] * v1 + w_ref[0:1, :] * v2, (v0, v1, v2)


def _conv_fwd(proj, conv_w, g_conv, w_conv, d_model, tr=1024):
    s = proj.shape[0]
    tr = _tile(s, tr)
    hb = tr // HALO

    def main(part):
        return pl.BlockSpec((tr, w_conv), lambda i: (i, part))

    def prev(part):
        return pl.BlockSpec((HALO, w_conv), lambda i: (jnp.maximum(i * hb - 1, 0), part))

    def body(cb_ref, cc_ref, cu_ref, ccp_ref, cup_ref, w_ref, g_ref, o_ref):
        i = pl.program_id(0)
        low = _low_half()
        for j in range(w_conv // LANES):
            cols = slice(j * LANES, (j + 1) * LANES)
            vv_prev = jnp.where(i > 0, ccp_ref[:, cols] * cup_ref[:, cols], 0.0)
            vv_ext = jnp.concatenate([vv_prev, cc_ref[:, cols] * cu_ref[:, cols]], axis=0)
            y, _ = _conv_taps(vv_ext, w_ref.at[:, cols])
            co = cb_ref[:, cols] * y
            r = lax.rsqrt(_half_mean(co * co, low) + EPS)
            o_ref[:, cols] = (co * r * g_ref[:, cols]).astype(BF16)

    return _pcall(
        body, name="conv_fwd", grid=(s // tr,),
        in_specs=[main(0), main(1), main(2), prev(1), prev(2),
                  pl.BlockSpec((CONV_K, w_conv), lambda i: (0, 0)),
                  pl.BlockSpec((1, w_conv), lambda i: (0, 0))],
        out_specs=pl.BlockSpec((tr, w_conv), lambda i: (i, 0)),
        out_shape=jax.ShapeDtypeStruct((s, d_model), BF16),
        compiler_params=_params(("parallel",)),
    )(proj, proj, proj, proj, proj, conv_w, g_conv)


def _conv_bwd(proj, dcat, conv_w, g_conv, dproj, w_conv, tr=1024):
    s = proj.shape[0]
    tr = _tile(s, tr)
    hb = tr // HALO
    last = s // HALO - 1
    nt = s // tr

    def main(part):
        return pl.BlockSpec((tr, w_conv), lambda i: (i, part))

    def prev(part):
        return pl.BlockSpec((HALO, w_conv), lambda i: (jnp.maximum(i * hb - 1, 0), part))

    def nxt(part):
        return pl.BlockSpec((HALO, w_conv), lambda i: (jnp.minimum((i + 1) * hb, last), part))

    def body(cb_ref, cc_ref, cu_ref, dc_ref, ccp_ref, cup_ref, cbn_ref, ccn_ref, cun_ref, dcn_ref,
             w_ref, g_ref, dproj_in, dproj_ref, dw_ref, dg_ref):
        del dproj_in
        i = pl.program_id(0)

        @pl.when(i == 0)
        def _():
            dw_ref[...] = jnp.zeros_like(dw_ref)
            dg_ref[...] = jnp.zeros_like(dg_ref)

        low = _low_half()
        n_ext = tr + HALO
        rowid = lax.broadcasted_iota(jnp.int32, (n_ext, 1), 0)
        for j in range(w_conv // LANES):
            cols = slice(j * LANES, (j + 1) * LANES)
            wj = w_ref.at[:, cols]
            cc, cu = cc_ref[:, cols], cu_ref[:, cols]
            vv_prev = jnp.where(i > 0, ccp_ref[:, cols] * cup_ref[:, cols], 0.0)
            vv_ext = jnp.concatenate([vv_prev, cc * cu, ccn_ref[:, cols] * cun_ref[:, cols]], axis=0)
            y_ext, (v0, v1, v2) = _conv_taps(vv_ext, wj)
            cb_ext = jnp.concatenate([cb_ref[:, cols], cbn_ref[:, cols]], axis=0)
            dc_ext = jnp.concatenate([dc_ref[:, cols], dcn_ref[:, cols]], axis=0)
            dco, dgn = _head_norm_bwd(cb_ext * y_ext, dc_ext, g_ref[:, cols], low)
            dyc = jnp.where((rowid < tr) | (i < nt - 1), dco * cb_ext, 0.0)
            dvv = (wj[2:3, :] * dyc[:tr] + wj[1:2, :] * pltpu.roll(dyc, n_ext - 1, 0)[:tr]
                   + wj[0:1, :] * pltpu.roll(dyc, n_ext - 2, 0)[:tr])
            dproj_ref[:, cols] = (dco[:tr] * y_ext[:tr]).astype(BF16)
            dproj_ref[:, w_conv + j * LANES:w_conv + (j + 1) * LANES] = (dvv * cu).astype(BF16)
            dproj_ref[:, 2 * w_conv + j * LANES:2 * w_conv + (j + 1) * LANES] = (dvv * cc).astype(BF16)
            dyt = dyc[:tr]
            for tap, shifted in enumerate((v2, v1, v0)):
                dw_ref[tap:tap + 1, cols] += jnp.sum(dyt * shifted[:tr], axis=0, keepdims=True)
            dg_ref[:, cols] += jnp.sum(dgn[:tr], axis=0, keepdims=True)

    n_cols = dproj.shape[1]
    return _pcall(
        body, name="conv_bwd", grid=(nt,),
        in_specs=[main(0), main(1), main(2), main(0),
                  prev(1), prev(2), nxt(0), nxt(1), nxt(2), nxt(0),
                  pl.BlockSpec((CONV_K, w_conv), lambda i: (0, 0)),
                  pl.BlockSpec((1, w_conv), lambda i: (0, 0)),
                  pl.BlockSpec(memory_space=pl.ANY)],
        out_specs=[pl.BlockSpec((tr, 3 * w_conv), lambda i: (i, 0)),
                   pl.BlockSpec((CONV_K, w_conv), lambda i: (0, 0)),
                   pl.BlockSpec((1, w_conv), lambda i: (0, 0))],
        out_shape=[jax.ShapeDtypeStruct((s, n_cols), BF16),
                   jax.ShapeDtypeStruct((CONV_K, w_conv), F32),
                   jax.ShapeDtypeStruct((1, w_conv), F32)],
        input_output_aliases={12: 0},
        compiler_params=_params(("arbitrary",)),
    )(proj, proj, proj, dcat, proj, proj, proj, proj, proj, dcat, conv_w, g_conv, dproj)


STRIP = 16

ALL_CHAINS = (0, 1, 2, 3)
UPPER_CHAINS = (2, 3)


RUN_FLOOR = -104.0


def _any_weight_left(run_s):
    return (jnp.max(run_s[...]) > RUN_FLOOR).astype(jnp.int32)


def _chains(low):
    return [(2 * half + h, half, msk) for half in range(2)
            for h, msk in enumerate((low, jnp.logical_not(low)))]


def _suffix_operator(t):
    r = lax.broadcasted_iota(jnp.int32, (2 * t, t), 0)
    c = lax.broadcasted_iota(jnp.int32, (2 * t, t), 1)
    return jnp.where((r > c) & ((r < t) | (r - t > c)), 1.0, 0.0).astype(BF16)


def _strips(t, diag):
    return [(i, slice(i * STRIP, (i + 1) * STRIP), t // 2 if diag and (i + 1) * STRIP <= t // 2 else t)
            for i in range(t // STRIP)]


def _strip_mask(i, w):
    r = lax.broadcasted_iota(jnp.int32, (STRIP, w), 0) + i * STRIP
    c = lax.broadcasted_iota(jnp.int32, (STRIP, w), 1)
    return r > c


def _store_trimmed(ref, rows, val, w, t, at=0):
    ref[rows, at:at + w] = val
    if w < t:
        ref[rows, at + w:at + t] = jnp.zeros((STRIP, t - w), val.dtype)


def _store_split(ref, rows, val, w, t):
    hi = val.astype(BF16)
    _store_trimmed(ref, rows, hi, w, t)
    _store_trimmed(ref, rows, (val - hi.astype(F32)).astype(BF16), w, t, at=t)


def _sb_scores(z_s, split_s, zl_s, tot_s, keep_s, t, diag):
    for i, rows, w in _strips(t, diag):
        z = z_s[rows, :w]
        log_beta = jnp.minimum(z, 0.0) - jnp.log(1.0 + jnp.exp(-jnp.abs(z)))
        log_keep = log_beta - z
        if diag:
            log_keep = jnp.where(_strip_mask(i, w), log_keep, 0.0)
        _store_split(split_s, rows, log_keep, w, t)
        zl_s[rows, :w] = log_beta
        tot_s[rows, :] = _row_sum(log_keep)
        if keep_s is not None:
            keep_s[rows, :w] = jnp.exp(log_keep)


def _row_sum(v):
    return jnp.broadcast_to(jnp.sum(v, axis=-1, keepdims=True), (v.shape[0], LANES))


def _wide(r, t):
    return jnp.concatenate([r] * (t // LANES), axis=1)


def _sb_weights(zl_s, suf_s, run_s, tot_s, a_s, t, diag, da_s=None, glog_s=None, gsplit_s=None, gtot_s=None):
    for i, rows, w in _strips(t, diag):
        run = run_s[rows, :]
        a = jnp.exp(zl_s[rows, :w] + suf_s[rows, :w] + _wide(run, w))
        if diag:
            a = jnp.where(_strip_mask(i, w), a, 0.0)
        ab = a.astype(BF16)
        _store_trimmed(a_s, rows, ab, w, t)
        run_s[rows, :] = run + tot_s[rows, :]
        if da_s is not None:
            glog = ab.astype(F32) * da_s[rows, :w]
            glog_s[rows, :w] = glog
            _store_split(gsplit_s, rows, glog, w, t)
            gtot_s[rows, :] = _row_sum(glog)


def _sb_dscores(glog_s, cum_s, rest_s, gtot_s, keep_s, dz_s, t, diag):
    for i, rows, w in _strips(t, diag):
        glog = glog_s[rows, :w]
        rest = rest_s[rows, :]
        from_here = _wide(rest, w) - cum_s[rows, :w]
        before = from_here - glog
        dz = from_here * keep_s[rows, :w] - before
        if diag:
            dz = jnp.where(_strip_mask(i, w), dz, 0.0)
        _store_trimmed(dz_s, rows, dz.astype(BF16), w, t)
        rest_s[rows, :] = rest - gtot_s[rows, :]


def _attn_fwd(proj, g_attn, cat, w_conv, carry, t=ATTN_BLOCK):
    s = proj.shape[0]
    w_attn = g_attn.shape[1]
    nh = w_attn // LANES
    t = _tile(s, t)
    tq = 2 * t
    nq = s // tq
    q0 = 3 * w_conv // LANES
    scale = HEAD_DIM ** -0.5
    plan = _Carried(carry)
    nw, n_res = len(plan.inputs), len(plan.out_shapes)

    def body(q_ref, k_ref, v_ref, g_ref, cat_in, *rest):
        staged_refs, rest = rest[:nw], rest[nw:]
        o_ref, cat_ref = rest[:2]
        gathered_refs, rest = rest[2:2 + n_res], rest[2 + n_res:]
        kb, vb, tri_s, qm_s, z_s, split_s, zl_s, suf_s, a_s, run_s, tot_s, acc_s = rest[:12]
        gather_sems = rest[12:]
        del cat_in
        qi = pl.program_id(1)

        @pl.when((pl.program_id(0) == 0) & (qi == 0))
        def _():
            for cp in plan.copies(staged_refs, gathered_refs, gather_sems):
                cp.start()

        @pl.when(qi == 0)
        def _():
            kb[...] = k_ref[...].astype(BF16)
            vb[...] = v_ref[...].astype(BF16)
            tri_s[...] = _suffix_operator(t)

        low = _low_half()
        for c, half, msk in _chains(low):
            qm_s[c] = jnp.where(msk, q_ref[half * t:(half + 1) * t, :] * scale, 0.0).astype(BF16)
            run_s[c] = jnp.zeros((t, LANES), F32)
            acc_s[c] = jnp.zeros((t, LANES), F32)

        def key_rows(kblk):
            return pl.ds(pl.multiple_of(kblk * t, t), t)

        def key_block(base, c):
            return key_rows(jnp.maximum(base + c // 2, 0))

        def scores_matmul(base, chains):
            for c in chains:
                z_s[c] = lax.dot_general(qm_s[c], kb[key_block(base, c), :], _NT, preferred_element_type=F32)

        def front(modes, base, prev=None):
            for c, diag in modes:
                _sb_scores(z_s.at[c], split_s.at[c], zl_s.at[c], tot_s.at[c], None, t, diag)
                suf_s[c] = jnp.dot(split_s[c], tri_s[...], preferred_element_type=F32)
            if prev is not None:
                tail(*prev)
            scores_matmul(base - 1, ALL_CHAINS)
            for c, diag in modes:
                _sb_weights(zl_s.at[c], suf_s.at[c], run_s.at[c], tot_s.at[c], a_s.at[c], t, diag)

        def tail(base, chains):
            for c in chains:
                acc_s[c] += jnp.dot(a_s[c], vb[key_block(base, c), :], preferred_element_type=F32)

        first = 2 * qi
        scores_matmul(first, ALL_CHAINS)
        front([(c, True) for c in ALL_CHAINS], first)

        def loop(state):
            it = state[0]
            base = first - 1 - it
            front([(c, False) for c in ALL_CHAINS], base, prev=(base + 1, ALL_CHAINS))
            return it + 1, _any_weight_left(run_s)

        done, live = lax.while_loop(lambda state: (state[0] < first) & (state[1] > 0), loop,
                                    (jnp.int32(0), jnp.int32(1)))
        one_more = (done == first) & (live > 0)

        @pl.when(one_more)
        def _():
            front([(c, False) for c in UPPER_CHAINS], -1, prev=(0, ALL_CHAINS))
            tail(-1, UPPER_CHAINS)

        @pl.when(jnp.logical_not(one_more))
        def _():
            tail(first - done, ALL_CHAINS)

        for half in range(2):
            rows = slice(half * t, (half + 1) * t)
            o = jnp.where(low, acc_s[2 * half], acc_s[2 * half + 1])
            o_ref[rows, :] = o
            r = lax.rsqrt(_half_mean(o * o, low) + EPS)
            cat_ref[rows, :] = (o * r * g_ref[...]).astype(BF16)

        @pl.when((pl.program_id(0) == nh - 1) & (qi == nq - 1))
        def _():
            for cp in plan.copies(staged_refs, gathered_refs, gather_sems):
                cp.wait()

    whole = lambda col0: pl.BlockSpec((s, LANES), lambda h, i: (0, col0 + h))
    n_ch = len(ALL_CHAINS)
    res = _pcall(
        body, name="attn_fwd", grid=(nh, nq),
        in_specs=[pl.BlockSpec((tq, LANES), lambda h, i: (i, q0 + h)),
                  whole(q0 + nh), whole(q0 + 2 * nh),
                  pl.BlockSpec((1, LANES), lambda h, i: (0, h)),
                  pl.BlockSpec(memory_space=pl.ANY)] + [pl.BlockSpec(memory_space=pl.ANY)] * nw,
        out_specs=[pl.BlockSpec((tq, LANES), lambda h, i: (i, h)),
                   pl.BlockSpec((tq, LANES), lambda h, i: (i, w_conv // LANES + h))]
        + [pl.BlockSpec(memory_space=pl.ANY)] * n_res,
        out_shape=[jax.ShapeDtypeStruct((s, w_attn), F32),
                   jax.ShapeDtypeStruct(cat.shape, BF16)] + plan.out_shapes,
        scratch_shapes=[pltpu.VMEM((s, LANES), BF16), pltpu.VMEM((s, LANES), BF16),
                        pltpu.VMEM((2 * t, t), BF16),
                        pltpu.VMEM((n_ch, t, LANES), BF16),
                        pltpu.VMEM((n_ch, t, t), F32),
                        pltpu.VMEM((n_ch, t, 2 * t), BF16),
                        pltpu.VMEM((n_ch, t, t), F32),
                        pltpu.VMEM((n_ch, t, t), F32),
                        pltpu.VMEM((n_ch, t, t), BF16),
                        pltpu.VMEM((n_ch, t, LANES), F32),
                        pltpu.VMEM((n_ch, t, LANES), F32),
                        pltpu.VMEM((n_ch, t, LANES), F32)]
        + plan.sems,
        input_output_aliases={4: 1, **plan.aliases(5, 2)},
        compiler_params=_params(("arbitrary", "arbitrary")),
    )(proj, proj, proj, g_attn, cat, *plan.inputs)
    return res[0], res[1], res[2:]


def _attn_bwd(proj, o, dcat, g_attn, w_conv, carry, t=ATTN_BLOCK):
    s, n_cols = proj.shape
    w_attn = g_attn.shape[1]
    nh = w_attn // LANES
    t = _tile(s, t)
    tq = 2 * t
    nq = s // tq
    q0 = 3 * w_conv // LANES
    scale = HEAD_DIM ** -0.5
    plan = _Carried(carry)
    nw, n_res = len(plan.inputs), len(plan.out_shapes)

    def body(q_ref, k_ref, v_ref, o_ref, do_ref, g_ref, *rest):
        partial_refs, rest = rest[:nw], rest[nw:]
        dproj_ref, dg_ref = rest[:2]
        received_refs, rest = rest[2:2 + n_res], rest[2 + n_res:]
        (kb, vb, dkt_acc, dvt_acc, stash, tri_s, qm_s, dom_s, qt_s, dot_s, z_s, da_s, split_s, zl_s,
         keep_s, suf_s, a_s, glog_s, gsplit_s, cum_s, dz_s, run_s, tot_s, rest_s, gtot_s, dq_s) = rest[:26]
        out_sems, scatter_sems = rest[26], rest[27:]
        step_i = pl.program_id(1)
        qi = nq - 1 - step_i
        head_pair = pl.program_id(0)
        first_step = (head_pair == 0) & (step_i == 0)
        last_step = (head_pair == nh - 1) & (step_i == nq - 1)

        @pl.when(first_step)
        def _():
            for cp in plan.copies(partial_refs, received_refs, scatter_sems):
                cp.start()

        def out_copies():
            rows = pl.ds(pl.multiple_of(qi * tq, tq), tq)
            return [pltpu.make_async_copy(
                stash.at[w], dproj_ref.at[rows, pl.ds(pl.multiple_of((q0 + w * nh + head_pair) * LANES, LANES), LANES)],
                out_sems.at[w]) for w in range(3)]

        def walk():
            @pl.when(step_i == 0)
            def _():
                kb[...] = k_ref[...].astype(BF16)
                vb[...] = v_ref[...].astype(BF16)
                tri_s[...] = _suffix_operator(t)
                dkt_acc[...] = jnp.zeros_like(dkt_acc)
                dvt_acc[...] = jnp.zeros_like(dvt_acc)
                dg_ref[...] = jnp.zeros_like(dg_ref)

            low = _low_half()
            gv = g_ref[...]
            for half in range(2):
                rows = slice(half * t, (half + 1) * t)
                q = q_ref[rows, :] * scale
                ov = o_ref[rows, :]
                d_o, dgn = _head_norm_bwd(ov, do_ref[rows, :], gv, low)
                dg_ref[...] += jnp.sum(dgn, axis=0, keepdims=True)
                for h, msk in enumerate((low, jnp.logical_not(low))):
                    c = 2 * half + h
                    qh = jnp.where(msk, q, 0.0)
                    doh = jnp.where(msk, d_o, 0.0)
                    dom = doh.astype(BF16)
                    qm_s[c] = qh.astype(BF16)
                    dom_s[c] = dom
                    qt_s[c] = qh.T.astype(BF16)
                    dot_s[c] = doh.T.astype(BF16)
                    rest_s[c] = _row_sum(dom.astype(F32) * ov)
                    run_s[c] = jnp.zeros((t, LANES), F32)
                    dq_s[c] = jnp.zeros((t, LANES), F32)

            def key_rows(kblk):
                return pl.ds(pl.multiple_of(kblk * t, t), t)

            def block_of(base, half):
                return jnp.maximum(base + half, 0)

            def scores_matmul(base, chains):
                for c in chains:
                    ks = kb[key_rows(block_of(base, c // 2)), :]
                    z_s[c] = lax.dot_general(qm_s[c], ks, _NT, preferred_element_type=F32)

            def da_matmul(base, chains):
                for c in chains:
                    vs = vb[key_rows(block_of(base, c // 2)), :]
                    da_s[c] = lax.dot_general(dom_s[c], vs, _NT, preferred_element_type=F32)

            def front(modes, base, prev=None):
                if prev is not None:
                    tail(*prev)
                for c, diag in modes:
                    _sb_scores(z_s.at[c], split_s.at[c], zl_s.at[c], tot_s.at[c], keep_s.at[c], t, diag)
                    suf_s[c] = jnp.dot(split_s[c], tri_s[...], preferred_element_type=F32)
                scores_matmul(base - 1, ALL_CHAINS)
                for c, diag in modes:
                    _sb_weights(zl_s.at[c], suf_s.at[c], run_s.at[c], tot_s.at[c], a_s.at[c], t, diag,
                                da_s.at[c], glog_s.at[c], gsplit_s.at[c], gtot_s.at[c])
                    cum_s[c] = jnp.dot(gsplit_s[c], tri_s[...], preferred_element_type=F32)
                da_matmul(base - 1, ALL_CHAINS)
                for c, diag in modes:
                    _sb_dscores(glog_s.at[c], cum_s.at[c], rest_s.at[c], gtot_s.at[c], keep_s.at[c],
                                dz_s.at[c], t, diag)

            def tail(base, chains):
                for half in range(2):
                    mine = [c for c in chains if c // 2 == half]
                    if not mine:
                        continue
                    kblk = block_of(base, half)
                    ks = kb[key_rows(kblk), :]
                    dkt = dkt_acc[kblk]
                    dvt = dvt_acc[kblk]
                    for c in mine:
                        dq_s[c] += jnp.dot(dz_s[c], ks, preferred_element_type=F32)
                        dkt = dkt + jnp.dot(qt_s[c], dz_s[c], preferred_element_type=F32)
                        dvt = dvt + jnp.dot(dot_s[c], a_s[c], preferred_element_type=F32)
                    dkt_acc[kblk] = dkt
                    dvt_acc[kblk] = dvt

            first = 2 * qi
            scores_matmul(first, ALL_CHAINS)
            da_matmul(first, ALL_CHAINS)
            front([(c, True) for c in ALL_CHAINS], first)

            def loop(state):
                it = state[0]
                base = first - 1 - it
                front([(c, False) for c in ALL_CHAINS], base, prev=(base + 1, ALL_CHAINS))
                return it + 1, _any_weight_left(run_s)

            done, live = lax.while_loop(lambda state: (state[0] < first) & (state[1] > 0), loop,
                                        (jnp.int32(0), jnp.int32(1)))
            one_more = (done == first) & (live > 0)

            @pl.when(one_more)
            def _():
                front([(c, False) for c in UPPER_CHAINS], -1, prev=(0, ALL_CHAINS))
                tail(-1, UPPER_CHAINS)

            @pl.when(jnp.logical_not(one_more))
            def _():
                tail(first - done, ALL_CHAINS)

            @pl.when(jnp.logical_not(first_step))
            def _():
                for cp in out_copies():
                    cp.wait()

            for half in range(2):
                rows = slice(half * t, (half + 1) * t)
                stash[0, rows, :] = (jnp.where(low, dq_s[2 * half], dq_s[2 * half + 1]) * scale).astype(BF16)
                stash[1, rows, :] = dkt_acc[2 * qi + half].T.astype(BF16)
                stash[2, rows, :] = dvt_acc[2 * qi + half].T.astype(BF16)
            for cp in out_copies():
                cp.start()

        walk()

        @pl.when(last_step)
        def _():
            for cp in out_copies():
                cp.wait()
            for cp in plan.copies(partial_refs, received_refs, scatter_sems):
                cp.wait()

    whole = lambda col0: pl.BlockSpec((s, LANES), lambda h, i: (0, col0 + h))
    blk = lambda col0: pl.BlockSpec((tq, LANES), lambda h, i: (nq - 1 - i, col0 + h))
    n_ch = len(ALL_CHAINS)
    res = _pcall(
        body, name="attn_bwd", grid=(nh, nq),
        in_specs=[blk(q0), whole(q0 + nh), whole(q0 + 2 * nh), blk(0), blk(w_conv // LANES),
                  pl.BlockSpec((1, LANES), lambda h, i: (0, h))] + [pl.BlockSpec(memory_space=pl.ANY)] * nw,
        out_specs=[pl.BlockSpec(memory_space=pl.ANY),
                   pl.BlockSpec((1, LANES), lambda h, i: (0, h))] + [pl.BlockSpec(memory_space=pl.ANY)] * n_res,
        out_shape=[jax.ShapeDtypeStruct((s, n_cols), BF16), jax.ShapeDtypeStruct((1, w_attn), F32)]
        + plan.out_shapes,
        scratch_shapes=[pltpu.VMEM((s, LANES), BF16), pltpu.VMEM((s, LANES), BF16),
                        pltpu.VMEM((s // t, LANES, t), F32),
                        pltpu.VMEM((s // t, LANES, t), F32),
                        pltpu.VMEM((3, tq, LANES), BF16),
                        pltpu.VMEM((2 * t, t), BF16),
                        pltpu.VMEM((n_ch, t, LANES), BF16),
                        pltpu.VMEM((n_ch, t, LANES), BF16),
                        pltpu.VMEM((n_ch, LANES, t), BF16),
                        pltpu.VMEM((n_ch, LANES, t), BF16),
                        pltpu.VMEM((n_ch, t, t), F32),
                        pltpu.VMEM((n_ch, t, t), F32),
                        pltpu.VMEM((n_ch, t, 2 * t), BF16),
                        pltpu.VMEM((n_ch, t, t), F32),
                        pltpu.VMEM((n_ch, t, t), F32),
                        pltpu.VMEM((n_ch, t, t), F32),
                        pltpu.VMEM((n_ch, t, t), BF16),
                        pltpu.VMEM((n_ch, t, t), F32),
                        pltpu.VMEM((n_ch, t, 2 * t), BF16),
                        pltpu.VMEM((n_ch, t, t), F32),
                        pltpu.VMEM((n_ch, t, t), BF16),
                        pltpu.VMEM((n_ch, t, LANES), F32),
                        pltpu.VMEM((n_ch, t, LANES), F32),
                        pltpu.VMEM((n_ch, t, LANES), F32),
                        pltpu.VMEM((n_ch, t, LANES), F32),
                        pltpu.VMEM((n_ch, t, LANES), F32),
                        pltpu.SemaphoreType.DMA((3,))]
        + plan.sems,
        input_output_aliases=plan.aliases(6, 2),
        compiler_params=_params(("arbitrary", "arbitrary")),
    )(proj, proj, proj, o, dcat, g_attn, *plan.inputs)
    return res[0], res[1], res[2:]


def _place():
    return lax.axis_index("x"), lax.axis_index("y"), lax.axis_index("c")


def _other_chips(x, y):
    return [(1 - x, y), (x, 1 - y), (1 - x, 1 - y)]


def _slot(px, py, pc):
    return 4 * px + 2 * py + pc


def _all_gather(shards, out_dtypes):
    nw = len(shards)

    def body(*refs):
        ins, outs, stage = refs[:nw], refs[nw:2 * nw], refs[2 * nw:3 * nw]
        send_sems, recv_sems, local_sems = refs[3 * nw:]
        x, y, c = _place()
        me, sibling = (x, y, c), (x, y, 1 - c)
        chips = _other_chips(x, y)

        def copy(w, k, block, to, src=None):
            dst = outs[w].at[_slot(*block)]
            return pltpu.make_async_remote_copy(
                src_ref=dst if src is None else src, dst_ref=dst,
                send_sem=send_sems.at[w * 7 + k], recv_sem=recv_sems.at[w * 7 + k],
                device_id=to, device_id_type=MESH)

        started = []
        local = []
        for w in range(nw):
            stage[w][...] = ins[w][...].astype(stage[w].dtype)
            cp = pltpu.make_async_copy(stage[w], outs[w].at[_slot(*me)], local_sems.at[w])
            cp.start()
            local.append(cp)
            started.append(copy(w, 0, me, sibling, src=stage[w]))
            started[-1].start()
            for j, chip in enumerate(chips):
                started.append(copy(w, 1 + j, me, (*chip, c), src=stage[w]))
                started[-1].start()
        for j, chip in enumerate(chips):
            for w in range(nw):
                copy(w, 1 + j, (*chip, c), me).wait_recv()
                started.append(copy(w, 4 + j, (*chip, c), sibling))
                started[-1].start()
        for w in range(nw):
            copy(w, 0, sibling, me).wait_recv()
            for j, chip in enumerate(chips):
                copy(w, 4 + j, (*chip, 1 - c), me).wait_recv()
        for cp in started:
            cp.wait_send()
        for cp in local:
            cp.wait()

    return _pcall(
        body, name="all_gather_weights",
        in_specs=[pl.BlockSpec(memory_space=pltpu.VMEM)] * nw,
        out_specs=[pl.BlockSpec(memory_space=pl.ANY)] * nw,
        out_shape=[jax.ShapeDtypeStruct((N_DEV, *a.shape), d) for a, d in zip(shards, out_dtypes)],
        scratch_shapes=[pltpu.VMEM(a.shape, d) for a, d in zip(shards, out_dtypes)]
        + [pltpu.SemaphoreType.DMA((7 * nw,)), pltpu.SemaphoreType.DMA((7 * nw,)),
           pltpu.SemaphoreType.DMA((nw,))],
        compiler_params=_params(),
    )(*shards)


N_PEERS = N_DEV - 1


def _peer(k):
    x, y, c = _place()
    return (x ^ (k >> 2), y ^ ((k >> 1) & 1), c ^ (k & 1))


def _remote(src, dst, sems, index, to):
    return pltpu.make_async_remote_copy(src_ref=src, dst_ref=dst, send_sem=sems[0].at[index],
                                        recv_sem=sems[1].at[index], device_id=to, device_id_type=MESH)


def _gather_out_copies(staged, gathered, sems):
    x, y, c = _place()
    me = _slot(x, y, c)
    targets = [(x, y, 1 - c)] + [(*chip, c) for chip in _other_chips(x, y)]
    copies = []
    for w, (src, dst) in enumerate(zip(staged, gathered)):
        copies.append(pltpu.make_async_copy(src, dst.at[me], sems[2].at[w]))
        copies += [_remote(src, dst.at[me], sems, w * len(targets) + k, to) for k, to in enumerate(targets)]
    return copies


def _gather_pass_copies(arrived, gathered, sems):
    x, y, c = _place()
    chips = _other_chips(x, y)
    return [_remote(src.at[_slot(*chip, c)], dst.at[_slot(*chip, c)], sems, w * len(chips) + j, (x, y, 1 - c))
            for w, (src, dst) in enumerate(zip(arrived, gathered)) for j, chip in enumerate(chips)]


ALL_PEERS = tuple(range(1, N_DEV))


def _scatter_copies(partials, received, sems, peers=ALL_PEERS):
    me = _slot(*_place())
    return [_remote(src.at[me ^ k], dst.at[k - 1], sems, w * N_PEERS + k - 1, _peer(k))
            for w, (src, dst) in enumerate(zip(partials, received)) for k in peers]


class _Carried:
    def __init__(self, jobs):
        self.jobs = [(job[0], list(job[1]), job[2] if len(job) > 2 else ALL_PEERS) for job in jobs if len(job[1])]
        self.inputs, self.out_shapes, self.sems, self.counts = [], [], [], []
        for kind, arrays, _ in self.jobs:
            n_out = len(arrays) // 2 if kind == "scatter_more" else len(arrays)
            fan = {"gather_out": 4, "gather_pass": 3}.get(kind, N_PEERS)
            for a in arrays[len(arrays) - n_out:]:
                shape = {"gather_out": (N_DEV, *a.shape), "scatter": (N_PEERS, *a.shape[1:])}.get(kind, a.shape)
                self.out_shapes.append(jax.ShapeDtypeStruct(shape, BF16))
            job_sems = [pltpu.SemaphoreType.DMA((fan * n_out,))] * 2
            job_sems += [pltpu.SemaphoreType.DMA((n_out,))] if kind == "gather_out" else []
            self.inputs += arrays
            self.sems += job_sems
            self.counts.append((len(arrays), n_out, len(job_sems)))

    def aliases(self, first_input, first_output):
        pairs, at_in, at_out = {}, first_input, first_output
        for (kind, _, _), (n_in, n_out, _) in zip(self.jobs, self.counts):
            if kind in ("gather_pass", "scatter_more"):
                pairs.update({at_in + n_in - n_out + i: at_out + i for i in range(n_out)})
            at_in, at_out = at_in + n_in, at_out + n_out
        return pairs

    def copies(self, in_refs, out_refs, sem_refs):
        out, at_in, at_out, at_sem = [], 0, 0, 0
        for (kind, _, peers), (n_in, n_out, n_sems) in zip(self.jobs, self.counts):
            srcs, dsts = in_refs[at_in:at_in + n_out], out_refs[at_out:at_out + n_out]
            sems = sem_refs[at_sem:at_sem + n_sems]
            if kind == "gather_out":
                out += _gather_out_copies(srcs, dsts, sems)
            elif kind == "gather_pass":
                out += _gather_pass_copies(srcs, dsts, sems)
            else:
                out += _scatter_copies(srcs, dsts, sems, peers)
            at_in, at_out, at_sem = at_in + n_in, at_out + n_out, at_sem + n_sems
        return out


def _cast_shards(shards):
    def body(*refs):
        for src, dst in zip(refs[:len(shards)], refs[len(shards):]):
            dst[...] = src[...].astype(BF16)

    return _pcall(
        body, name="cast_shards",
        in_specs=[pl.BlockSpec(memory_space=pltpu.VMEM)] * len(shards),
        out_specs=[pl.BlockSpec(memory_space=pltpu.VMEM)] * len(shards),
        out_shape=[jax.ShapeDtypeStruct(a.shape, BF16) for a in shards],
        compiler_params=_params(),
    )(*shards)


def _all_reduce_small(packed):
    r = packed.shape[0]

    def body(x_ref, o_ref, gathered, send_sems, recv_sems):
        x, y, c = _place()
        me = _slot(x, y, c)
        gathered[me] = x_ref[...]
        copies = [_remote(x_ref, gathered.at[me], (send_sems, recv_sems), k - 1, _peer(k)) for k in ALL_PEERS]
        for cp in copies:
            cp.start()
        for cp in copies:
            cp.wait()
        total = gathered[0]
        for k in range(1, N_DEV):
            total = total + gathered[k]
        o_ref[...] = total

    return _pcall(
        body, name="all_reduce_small",
        in_specs=[pl.BlockSpec(memory_space=pltpu.VMEM)],
        out_specs=pl.BlockSpec(memory_space=pltpu.VMEM),
        out_shape=jax.ShapeDtypeStruct(packed.shape, F32),
        scratch_shapes=[pltpu.VMEM((N_DEV, r, LANES), F32),
                        pltpu.SemaphoreType.DMA((N_DEV - 1,)), pltpu.SemaphoreType.DMA((N_DEV - 1,))],
        compiler_params=_params(),
    )(packed)


def _adam_math(w, g, m, v):
    m = ADAM_B1 * m + (1.0 - ADAM_B1) * g
    v = ADAM_B2 * v + (1.0 - ADAM_B2) * jnp.square(g)
    m_hat = m / (1.0 - ADAM_B1 ** ADAM_STEP)
    v_hat = v / (1.0 - ADAM_B2 ** ADAM_STEP)
    delta = -ADAM_LR * (m_hat / (jnp.sqrt(v_hat) + ADAM_EPS) + ADAM_WD * w)
    return delta, m, v


ADAM_TILE_BYTES = 24 * 1024 * 1024


def _adam_sharded(name, own, received, w, m, v, place):
    r, cdim = w.shape
    row_bytes = 2 * cdim * (4 + 2 * N_PEERS + 3 * 4 + 4 * 4)
    tr = _tile(r, max(LANES, ADAM_TILE_BYTES // row_bytes // LANES * LANES)) if r % LANES == 0 else r

    def body(place_ref, own_ref, rec_ref, w_ref, m_ref, v_ref, g_ref, d_ref, nm_ref, nv_ref):
        del place_ref
        g = own_ref[...]
        for j in range(N_PEERS):
            g = g + rec_ref[j].astype(F32)
        delta, nm, nv = _adam_math(w_ref[...], g, m_ref[...], v_ref[...])
        g_ref[...] = g
        d_ref[...] = delta
        nm_ref[...] = nm
        nv_ref[...] = nv

    blk = pl.BlockSpec((tr, cdim), lambda i, pr: (i, 0))
    grid_spec = pltpu.PrefetchScalarGridSpec(
        num_scalar_prefetch=1, grid=(r // tr,),
        in_specs=[pl.BlockSpec((None, tr, cdim), lambda i, pr: (4 * pr[0] + 2 * pr[1] + pr[2], i, 0)),
                  pl.BlockSpec((N_PEERS, tr, cdim), lambda i, pr: (0, i, 0)), blk, blk, blk],
        out_specs=[blk] * 4)
    return _pcall(body, name=name, grid_spec=grid_spec,
                  out_shape=[jax.ShapeDtypeStruct((r, cdim), F32)] * 4,
                  compiler_params=_params(("parallel",)))(place, own, received, w, m, v)


def _adam_small(w, g, m, v):
    def body(w_ref, g_ref, m_ref, v_ref, d_ref, nm_ref, nv_ref):
        delta, nm, nv = _adam_math(w_ref[...], g_ref[...], m_ref[...], v_ref[...])
        d_ref[...] = delta
        nm_ref[...] = nm
        nv_ref[...] = nv

    return _pcall(body, name="adam_small",
                  in_specs=[pl.BlockSpec(memory_space=pltpu.VMEM)] * 4,
                  out_specs=[pl.BlockSpec(memory_space=pltpu.VMEM)] * 3,
                  out_shape=[jax.ShapeDtypeStruct(w.shape, F32)] * 3,
                  compiler_params=_params())(w, g, m, v)


def _rows(vec):
    return vec.reshape(-1, LANES)


def kernel(x, p, g_mix, w_in, conv_w, g_conv_out, g_attn_out, w_out, g_mlp, w_up, w_down, g_ple, w_ple_gate, w_ple_proj, g_final, loss_target, m_g_mix, m_w_in, m_conv_w, m_g_conv_out, m_g_attn_out, m_w_out, m_g_mlp, m_w_up, m_w_down, m_g_ple, m_w_ple_gate, m_w_ple_proj, m_g_final, v_g_mix, v_w_in, v_conv_w, v_g_conv_out, v_g_attn_out, v_w_out, v_g_mlp, v_w_up, v_w_down, v_g_ple, v_w_ple_gate, v_w_ple_proj, v_g_final):
    s, d = x.shape[1], x.shape[2]
    w_conv = g_conv_out.shape[1]
    w_attn = g_attn_out.shape[1]
    cw = conv_w.shape[2]
    xs, ps, tgt = x[0], p[0, 0], loss_target[0]
    place = jnp.stack([lax.axis_index("x"), lax.axis_index("y"), lax.axis_index("c")]).astype(jnp.int32)
    my_slot = 4 * place[0] + 2 * place[1] + place[2]

    conv_tile = jnp.pad(conv_w[0], ((0, HALO - CONV_K), (0, LANES - cw)))
    big = [w_in[0], w_out[0], w_up[0], w_down[0], w_ple_gate[0], w_ple_proj[0]]
    win_g, conv_g = _all_gather([big[0], conv_tile], [BF16, F32])
    s_out, s_up, s_down, s_gate, s_proj = _cast_shards(big[1:])
    conv_full = jnp.transpose(conv_g[:, :CONV_K, :cw], (1, 0, 2)).reshape(CONV_K, w_conv)
    in_shard, up_shard, proj_shard = big[0].shape[1], big[2].shape[1], big[5].shape[1]

    proj, a, g_out, g_gate, g_proj, g_up = _mm_nn(
        "in_proj", xs, win_g, n_shard=in_shard, tn=in_shard, tm=2048, lhs_norm=g_mix,
        carry=[("gather_out", [s_out, s_gate, s_proj, s_up])])
    cat = _conv_fwd(proj, conv_full, g_conv_out, w_conv, d)
    o, cat, (g_down, wout_g, wgate_g, wproj_g, wup_g) = _attn_fwd(
        proj, g_attn_out, cat, w_conv,
        [("gather_out", [s_down]), ("gather_pass", [g_out, g_gate, g_proj, g_up])])
    wout_f = wout_g.reshape(-1, wout_g.shape[-1])
    wgate_f = wgate_g.reshape(-1, wgate_g.shape[-1])
    h1, = _mm_nn("out_proj", cat, wout_f, epilogue=_ep_residual, extras=(xs,))
    act, mn, wdown_g = _mm_nn("mlp_up", h1, wup_g, n_shard=up_shard, epilogue=_ep_up, out_dtypes=(BF16,), tm=2048,
                              lhs_norm=g_mlp, carry=[("gather_pass", [g_down])])
    wdown_f = wdown_g.reshape(-1, wdown_g.shape[-1])
    h2, = _mm_nn("mlp_down", act, wdown_f, epilogue=_ep_residual, extras=(h1,))
    pp = _ple_proj(ps, wproj_g)
    loss_part, dh3, dgl, dpp, dg_final, n3 = _ple_gate_loss(h2, g_ple, wgate_f, pp, tgt, g_final.reshape(1, d))

    def slots(t2d):
        return t2d.reshape(N_DEV, -1, t2d.shape[-1])

    dw_proj = _d_ple_proj(ps, dpp, proj_shard)
    dw_gate = [slots(t) for t in _mm_tn("d_w_ple_gate", n3, dgl)]
    dh2, dh2b, dg_ple = _mm_nt_norm_bwd("d_norm_ple", dgl, wgate_f, h2, g_ple, dh3)
    du, gate_recv, proj_recv = _mm_nt("d_mlp_act", dh2b, wdown_f, epilogue=_ep_dact, out_dtypes=(BF16,),
                                      extras=(act,), tm=2048, carry=[("scatter", [dw_gate[1], dw_proj[1]])])
    dw_down = [slots(t) for t in _mm_tn("d_w_down", act, dh2b)]
    near, far = (1, 2, 3, 4, 5), (6, 7)
    dw_up = _mm_tn("d_w_up", mn, du, n_shard=up_shard)
    dh1, dh1b, dg_mlp, down_part = _mm_nt_norm_bwd(
        "d_norm_mlp", du, wup_g, h1, g_mlp, dh2, k_shard=up_shard, tm=1024,
        carry=[("scatter", [dw_down[1]], near)])
    dcat, = _mm_nt("d_cat", dh1b, wout_f)
    dw_out = [slots(t) for t in _mm_tn("d_w_out", cat, dh1b)]
    dproj, dg_attn, (down_recv, up_recv) = _attn_bwd(
        proj, o, dcat, g_attn_out, w_conv,
        [("scatter_more", [dw_down[1], down_part], far), ("scatter", [dw_up[1]])])
    dproj, dconv, dg_conv = _conv_bwd(proj, dcat, conv_full, g_conv_out, dproj, w_conv)
    *dw_in, out_recv = _mm_tn("d_w_in", a, dproj, n_shard=in_shard, tn=in_shard,
                              carry=[("scatter", [dw_out[1]])])
    grad_x, _, dg_mix, in_recv = _mm_nt_norm_bwd("d_norm_mix", dproj, win_g, xs, g_mix, dh1, k_shard=in_shard,
                                                 tk=2 * in_shard, tm=1024, carry=[("scatter", [dw_in[1]])])

    names = ["w_in", "w_out", "w_up", "w_down", "w_ple_gate", "w_ple_proj"]
    owns = [dw_in[0], dw_out[0], dw_up[0], dw_down[0], dw_gate[0], dw_proj[0]]
    recvs = [in_recv, out_recv, up_recv, down_recv, gate_recv, proj_recv]
    moments = [(m_w_in, v_w_in), (m_w_out, v_w_out), (m_w_up, v_w_up), (m_w_down, v_w_down),
               (m_w_ple_gate, v_w_ple_gate), (m_w_ple_proj, v_w_ple_proj)]
    big_out = {}
    for n, own, rc, wt, (mm, vv) in zip(names, owns, recvs, big, moments):
        big_out[n] = [t[None] for t in _adam_sharded("adam_" + n, own, rc, wt, mm[0], vv[0], place)]

    n_conv_rows = CONV_K * w_conv // LANES
    small_g = jnp.concatenate(
        [_rows(dg_mix[0]), _rows(dg_conv[0]), _rows(dg_attn[0]), _rows(dg_mlp[0]), _rows(dg_ple[0]),
         _rows(dg_final[0]), _rows(dconv.reshape(-1)), loss_part], axis=0)
    n_gain_rows = small_g.shape[0] - n_conv_rows - 1
    pad_rows = (-small_g.shape[0]) % HALO
    small_g = _all_reduce_small(jnp.pad(small_g, ((0, pad_rows), (0, 0))))
    loss = small_g[n_gain_rows + n_conv_rows, 0]
    dconv_full = small_g[n_gain_rows:n_gain_rows + n_conv_rows].reshape(CONV_K, w_conv)
    dconv_mine = lax.dynamic_slice(dconv_full, (0, my_slot * cw), (CONV_K, cw))

    def pack(vecs, conv_part):
        rows = [_rows(t.reshape(-1)) for t in vecs]
        rows.append(jnp.pad(conv_part, ((0, HALO - CONV_K), (0, LANES - cw))))
        return jnp.concatenate(rows, axis=0)

    gains = [g_mix, g_conv_out, g_attn_out, g_mlp, g_ple, g_final]
    gains_m = [m_g_mix, m_g_conv_out, m_g_attn_out, m_g_mlp, m_g_ple, m_g_final]
    gains_v = [v_g_mix, v_g_conv_out, v_g_attn_out, v_g_mlp, v_g_ple, v_g_final]
    gpack = jnp.concatenate([small_g[:n_gain_rows], jnp.pad(dconv_mine, ((0, HALO - CONV_K), (0, LANES - cw)))], axis=0)
    sd, sm, sv = _adam_small(pack(gains, conv_w[0]), gpack, pack(gains_m, m_conv_w[0]), pack(gains_v, v_conv_w[0]))

    def unpack(packed):
        out, r0 = [], 0
        for t in gains:
            nr = t.size // LANES
            out.append(packed[r0:r0 + nr].reshape(t.shape))
            r0 += nr
        out.append(packed[r0:r0 + CONV_K, :cw][None])
        return out

    sg_l, sd_l, sm_l, sv_l = unpack(gpack), unpack(sd), unpack(sm), unpack(sv)
    small_names = ["g_mix", "g_conv_out", "g_attn_out", "g_mlp", "g_ple", "g_final", "conv_w"]
    small_out = {n: [sg_l[i], sd_l[i], sm_l[i], sv_l[i]] for i, n in enumerate(small_names)}

    order = ["g_mix", "w_in", "conv_w", "g_conv_out", "g_attn_out", "w_out", "g_mlp", "w_up", "w_down",
             "g_ple", "w_ple_gate", "w_ple_proj", "g_final"]
    table = {**big_out, **small_out}
    outs = [loss, grad_x[None]]
    for kind in range(4):
        outs.extend(table[n][kind] for n in order)
    return tuple(outs)
```

```python
import jax
import jax.numpy as jnp
from jax import lax
from jax.experimental import pallas as pl
from jax.experimental.pallas import tpu as pltpu

F32 = jnp.float32
BF16 = jnp.bfloat16
EPS = 1e-6
HEAD_DIM = 64
LANES = 128
CONV_K = 3
MXU_WIDTH = 256
ATTN_BLOCK = MXU_WIDTH
HALO = 8
N_DEV = 8
MESH = pl.DeviceIdType.MESH
VMEM_LIMIT = 56 * 1024 * 1024

ADAM_LR = 0.001
ADAM_B1 = 0.9
ADAM_B2 = 0.999
ADAM_EPS = 1e-08
ADAM_WD = 0.01
ADAM_STEP = 10


def _pcall(body, **kw):
    return pl.pallas_call(body, **kw)


def _params(sem=None, **kw):
    return pltpu.CompilerParams(dimension_semantics=sem, vmem_limit_bytes=VMEM_LIMIT, **kw)


def _tile(dim, pref):
    t = min(dim, pref)
    while dim % t:
        t -= LANES
    assert t > 0, (dim, pref)
    return t


_NN = (((1,), (0,)), ((), ()))
_NT = (((1,), (1,)), ((), ()))
_TN = (((0,), (0,)), ((), ()))


def _ep_store(acc, outs):
    outs[0][...] = acc.astype(outs[0].dtype)


def _ep_both(acc, outs):
    outs[0][...] = acc
    outs[1][...] = acc.astype(BF16)


def _ep_residual(acc, res, outs):
    outs[0][...] = acc + res[...]


def _ep_up(acc, outs):
    outs[0][...] = jnp.square(jnp.maximum(acc, 0.0)).astype(BF16)


def _ep_dact(acc, act, outs):
    outs[0][...] = (acc * (2.0 * jnp.sqrt(act[...].astype(F32)))).astype(BF16)


def _row_chunked(epilogue):
    def run(acc, *rest):
        *ex, outs = rest
        n = acc.shape[0]
        for m0 in range(0, n, MXU_WIDTH):
            rows = slice(m0, min(m0 + MXU_WIDTH, n))
            pick = lambda ref: ref.at[rows, :] if ref.shape[0] == n else ref
            epilogue(acc[rows, :], *[pick(e) for e in ex], [pick(o) for o in outs])
    return run


def _ep_norm_bwd(acc, h, g, dres, outs):
    hv = h[...]
    r = lax.rsqrt(jnp.mean(hv * hv, axis=-1, keepdims=True) + EPS)
    hn = hv * r
    outs[2][...] += jnp.sum(acc * hn, axis=0, keepdims=True)
    dhn = acc * g[...]
    dh = dres[...] + r * (dhn - hn * jnp.mean(dhn * hn, axis=-1, keepdims=True))
    outs[0][...] = dh
    outs[1][...] = dh.astype(BF16)


def _matmul(name, a, b, *, dims, grid, a_spec, b_spec, acc_shape, out_shapes, out_specs,
            epilogue=_ep_store, extras=(), extra_specs=(), carry=(), sequential=False, lhs_norm=False):
    nk = grid[2]
    plan = _Carried(carry)
    n_ex, n_out, n_xc, n_xo = len(extras), len(out_shapes), len(plan.inputs), len(plan.out_shapes)
    n_sems = len(plan.sems)
    last = tuple(g - 1 for g in grid)
    assert not lhs_norm or nk == 1

    def product(a_ref, b_ref):
        if len(b_ref.shape) == 2:
            return lax.dot_general(a_ref[...].astype(BF16), b_ref[...].astype(BF16), dims,
                                   preferred_element_type=F32)
        width = b_ref.shape[2]
        return sum(lax.dot_general(a_ref[:, g * width:(g + 1) * width].astype(BF16), b_ref[g].astype(BF16), dims,
                                   preferred_element_type=F32) for g in range(b_ref.shape[0]))

    def body(a_ref, b_ref, *rest):
        ex, rest = rest[:n_ex], rest[n_ex:]
        partials, rest = rest[:n_xc], rest[n_xc:]
        outs, rest = rest[:n_out], rest[n_out:]
        received, rest = rest[:n_xo], rest[n_xo:]
        ids = [pl.program_id(axis) for axis in range(3)]
        if n_xc:
            @pl.when((ids[0] == 0) & (ids[1] == 0) & (ids[2] == 0))
            def _():
                for cp in plan.copies(partials, received, rest[-n_sems:]):
                    cp.start()

        if lhs_norm:
            x_ref, a_ref, gain, ex, outs = a_ref, outs[-1], ex[-1], ex[:-1], outs[:-1]

            @pl.when(ids[1] == 0)
            def _():
                for m0 in range(0, acc_shape[0], MXU_WIDTH):
                    rows = slice(m0, min(m0 + MXU_WIDTH, acc_shape[0]))
                    xv = x_ref[rows, :]
                    r = lax.rsqrt(jnp.mean(xv * xv, axis=-1, keepdims=True) + EPS)
                    a_ref[rows, :] = (xv * r * gain[...]).astype(BF16)

        if nk == 1 and not sequential and dims != _TN:
            for n0 in range(0, acc_shape[1], MXU_WIDTH):
                cols = slice(n0, min(n0 + MXU_WIDTH, acc_shape[1]))
                b_cols = b_ref.at[cols, :] if dims == _NT else b_ref.at[:, cols]
                for m0 in range(0, acc_shape[0], MXU_WIDTH):
                    rows = slice(m0, min(m0 + MXU_WIDTH, acc_shape[0]))
                    epilogue(product(a_ref.at[rows, :], b_cols), *[e.at[rows, cols] for e in ex],
                             [o.at[rows, cols] for o in outs])
        else:
            if sequential:
                @pl.when((ids[0] == 0) & (ids[2] == 0))
                def _():
                    for o in outs:
                        if o.shape[0] != acc_shape[0]:
                            o[...] = jnp.zeros_like(o)

            if nk == 1:
                _row_chunked(epilogue)(product(a_ref, b_ref), *ex, outs)
            else:
                acc = rest[0]

                @pl.when(ids[2] == 0)
                def _():
                    acc[...] = product(a_ref, b_ref)

                @pl.when(ids[2] > 0)
                def _():
                    acc[...] += product(a_ref, b_ref)

                @pl.when(ids[2] == nk - 1)
                def _():
                    _row_chunked(epilogue)(acc, *ex, outs)

        if n_xc:
            @pl.when((ids[0] == last[0]) & (ids[1] == last[1]) & (ids[2] == last[2]))
            def _():
                for cp in plan.copies(partials, received, rest[-n_sems:]):
                    cp.wait()

    anywhere = pl.BlockSpec(memory_space=pl.ANY)
    return _pcall(
        body, name=name, grid=grid,
        in_specs=[a_spec, b_spec, *extra_specs, *[anywhere] * n_xc],
        out_specs=[*out_specs, *[anywhere] * n_xo],
        out_shape=[*out_shapes, *plan.out_shapes],
        scratch_shapes=([] if nk == 1 else [pltpu.VMEM(acc_shape, F32)]) + plan.sems,
        input_output_aliases=plan.aliases(2 + n_ex, n_out),
        compiler_params=_params(("arbitrary",) * 3 if n_xc or sequential or lhs_norm
                                else ("parallel", "parallel", "arbitrary")),
    )(a, b, *extras, *plan.inputs)


_NO_CARRY = ()


def _mm_nn(name, a, w, *, n_shard=None, epilogue=_ep_store, out_dtypes=(F32,), extras=(), carry=_NO_CARRY,
           lhs_norm=None, tm=1024, tn=1024, tk=1024):
    m, kd = a.shape
    if lhs_norm is not None:
        tk = kd
    if n_shard is None:
        n = w.shape[1]
        tn = _tile(n, tn)
        tk = _tile(kd, tk)
        b_spec = pl.BlockSpec((tk, tn), lambda i, j, k: (k, j))
    else:
        n = N_DEV * n_shard
        tn = _tile(n_shard, tn)
        tk = _tile(kd, tk)
        per = n_shard // tn
        b_spec = pl.BlockSpec((None, tk, tn), lambda i, j, k: (j // per, k, j % per))
    tm = _tile(m, tm)
    o_spec = pl.BlockSpec((tm, tn), lambda i, j, k: (i, j))
    out_shapes = [jax.ShapeDtypeStruct((m, n), d) for d in out_dtypes]
    out_specs = [o_spec] * len(out_dtypes)
    extra_specs = [o_spec] * len(extras)
    if lhs_norm is not None:
        extras = (*extras, lhs_norm)
        extra_specs.append(pl.BlockSpec((1, kd), lambda i, j, k: (0, 0)))
        out_shapes.append(jax.ShapeDtypeStruct((m, kd), BF16))
        out_specs.append(pl.BlockSpec((tm, kd), lambda i, j, k: (i, 0)))
    return _matmul(
        name, a, w, dims=_NN, grid=(m // tm, n // tn, kd // tk),
        a_spec=pl.BlockSpec((tm, tk), lambda i, j, k: (i, k)), b_spec=b_spec,
        acc_shape=(tm, tn), out_shapes=out_shapes, out_specs=out_specs,
        epilogue=epilogue, extras=extras, extra_specs=extra_specs, carry=carry, lhs_norm=lhs_norm is not None)


def _mm_nt(name, a, w, *, epilogue=_ep_store, out_dtypes=(F32,), extras=(), carry=_NO_CARRY,
           tm=1024, tn=1024, tk=1024):
    m, kd = a.shape
    n = w.shape[0]
    tm, tn, tk = _tile(m, tm), _tile(n, tn), _tile(kd, tk)
    o_spec = pl.BlockSpec((tm, tn), lambda i, j, k: (i, j))
    return _matmul(
        name, a, w, dims=_NT, grid=(m // tm, n // tn, kd // tk),
        a_spec=pl.BlockSpec((tm, tk), lambda i, j, k: (i, k)),
        b_spec=pl.BlockSpec((tn, tk), lambda i, j, k: (j, k)),
        acc_shape=(tm, tn),
        out_shapes=[jax.ShapeDtypeStruct((m, n), d) for d in out_dtypes],
        out_specs=[o_spec] * len(out_dtypes),
        epilogue=epilogue, extras=extras, extra_specs=[o_spec] * len(extras), carry=carry)


def _mm_nt_norm_bwd(name, a, w, h, g, dres, *, k_shard=None, carry=_NO_CARRY, tm=512, tk=1024):
    m, kd = a.shape
    n = h.shape[1]
    if k_shard is None:
        tk = _tile(kd, tk)
        b_spec = pl.BlockSpec((n, tk), lambda i, j, k: (0, k))
    else:
        group = max(1, min(tk // k_shard, N_DEV))
        while N_DEV % group:
            group -= 1
        tk = group * k_shard
        b_spec = pl.BlockSpec((group, n, k_shard), lambda i, j, k: (k, 0, 0))
    tm = _tile(m, tm)
    rows = pl.BlockSpec((tm, n), lambda i, j, k: (i, 0))
    vec = pl.BlockSpec((1, n), lambda i, j, k: (0, 0))
    return _matmul(
        name, a, w, dims=_NT, grid=(m // tm, 1, kd // tk),
        a_spec=pl.BlockSpec((tm, tk), lambda i, j, k: (i, k)), b_spec=b_spec, acc_shape=(tm, n),
        out_shapes=[jax.ShapeDtypeStruct((m, n), F32), jax.ShapeDtypeStruct((m, n), BF16),
                    jax.ShapeDtypeStruct((1, n), F32)],
        out_specs=[rows, rows, vec], epilogue=_ep_norm_bwd,
        extras=(h, g, dres), extra_specs=[rows, vec, rows], carry=carry, sequential=True)


TN_TILE_BYTES = 40 * 1024 * 1024


def _mm_tn(name, a, b, *, n_shard=None, carry=_NO_CARRY, tm=1024, tn=1024):
    t, m = a.shape
    n = b.shape[1]
    tm = _tile(m, tm)
    tn = _tile(n if n_shard is None else n_shard, tn)
    tk = t
    while 2 * 2 * tk * (tm + tn) + 4 * tm * tn * 5 > TN_TILE_BYTES and tk % (2 * LANES) == 0:
        tk //= 2
    if n_shard is None:
        o_spec = pl.BlockSpec((tm, tn), lambda i, j, k: (i, j))
        shape = (m, n)
    else:
        per = n_shard // tn
        o_spec = pl.BlockSpec((None, tm, tn), lambda i, j, k: (j // per, i, j % per))
        shape = (N_DEV, m, n_shard)
    return _matmul(
        name, a, b, dims=_TN, grid=(m // tm, n // tn, t // tk),
        a_spec=pl.BlockSpec((tk, tm), lambda i, j, k: (k, i)),
        b_spec=pl.BlockSpec((tk, tn), lambda i, j, k: (k, j)),
        acc_shape=(tm, tn), epilogue=_ep_both, carry=carry,
        out_shapes=[jax.ShapeDtypeStruct(shape, F32), jax.ShapeDtypeStruct(shape, BF16)],
        out_specs=[o_spec, o_spec])


def _ple_proj(p, w_g, tm=1024):
    s, kd = p.shape
    ns = w_g.shape[2]
    tm = _tile(s, tm)

    def body(p_ref, w_ref, o_ref):
        pv = p_ref[...].astype(BF16)
        for j in range(N_DEV):
            o_ref[:, j * ns:(j + 1) * ns] = jnp.dot(pv, w_ref[j], preferred_element_type=F32)

    return _pcall(body, name="ple_proj", grid=(s // tm,),
                  in_specs=[pl.BlockSpec((tm, kd), lambda i: (i, 0)),
                            pl.BlockSpec((N_DEV, kd, ns), lambda i: (0, 0, 0))],
                  out_specs=pl.BlockSpec((tm, N_DEV * ns), lambda i: (i, 0)),
                  out_shape=jax.ShapeDtypeStruct((s, N_DEV * ns), F32),
                  compiler_params=_params(("parallel",)))(p, w_g)


def _d_ple_proj(p, dpp, ns, tk=1024):
    s, kd = p.shape
    tk = _tile(s, tk)
    nk = s // tk

    def body(p_ref, d_ref, of_ref, ob_ref, acc):
        k = pl.program_id(0)

        @pl.when(k == 0)
        def _():
            acc[...] = jnp.zeros_like(acc)

        pv = p_ref[...].astype(BF16)
        for j in range(N_DEV):
            acc[j] += lax.dot_general(pv, d_ref[:, j * ns:(j + 1) * ns], _TN, preferred_element_type=F32)

        @pl.when(k == nk - 1)
        def _():
            of_ref[...] = acc[...]
            ob_ref[...] = acc[...].astype(BF16)

    whole = pl.BlockSpec((N_DEV, kd, ns), lambda k: (0, 0, 0))
    return _pcall(body, name="d_w_ple_proj", grid=(nk,),
                  in_specs=[pl.BlockSpec((tk, kd), lambda k: (k, 0)),
                            pl.BlockSpec((tk, N_DEV * ns), lambda k: (k, 0))],
                  out_specs=[whole, whole],
                  out_shape=[jax.ShapeDtypeStruct((N_DEV, kd, ns), F32), jax.ShapeDtypeStruct((N_DEV, kd, ns), BF16)],
                  scratch_shapes=[pltpu.VMEM((N_DEV, kd, ns), F32)],
                  compiler_params=_params(("arbitrary",)))(p, dpp)


def _ep_ple_loss(gl, h2, pp, tgt, g_final, outs):
    loss_ref, dh3_ref, dgl_ref, dpp_ref, dg_ref = outs
    gate = jax.nn.sigmoid(gl)
    ppv = pp[...]
    h3 = h2[...] + gate * ppv
    r = lax.rsqrt(jnp.mean(h3 * h3, axis=-1, keepdims=True) + EPS)
    hn = h3 * r
    gv = g_final[...]
    diff = hn * gv - tgt[...]
    row = jnp.mean(diff * diff, axis=-1, keepdims=True)
    loss_ref[...] += 0.5 * jnp.sum(row, axis=0, keepdims=True)
    dy = diff * (1.0 / h3.shape[-1])
    dg_ref[...] += jnp.sum(dy * hn, axis=0, keepdims=True)
    dhn = dy * gv
    dh3 = r * (dhn - hn * jnp.mean(dhn * hn, axis=-1, keepdims=True))
    dh3_ref[...] = dh3
    dgl_ref[...] = (dh3 * ppv * gate * (1.0 - gate)).astype(BF16)
    dpp_ref[...] = (dh3 * gate).astype(BF16)


def _ple_gate_loss(h2, g_ple, w_gate, pp, tgt, g_final, tm=512):
    s, d = h2.shape
    tm = _tile(s, tm)
    rows = pl.BlockSpec((tm, d), lambda i, j, k: (i, 0))
    vec = pl.BlockSpec((1, d), lambda i, j, k: (0, 0))
    return _matmul(
        "ple_gate_loss", h2, w_gate, dims=_NN, grid=(s // tm, 1, 1),
        a_spec=rows, b_spec=pl.BlockSpec((d, d), lambda i, j, k: (0, 0)), acc_shape=(tm, d),
        out_shapes=[jax.ShapeDtypeStruct((1, LANES), F32), jax.ShapeDtypeStruct((s, d), F32),
                    jax.ShapeDtypeStruct((s, d), BF16), jax.ShapeDtypeStruct((s, d), BF16),
                    jax.ShapeDtypeStruct((1, d), F32), jax.ShapeDtypeStruct((s, d), BF16)],
        out_specs=[pl.BlockSpec((1, LANES), lambda i, j, k: (0, 0)), rows, rows, rows, vec, rows],
        epilogue=_ep_ple_loss, extras=(h2, pp, tgt, g_final, g_ple), extra_specs=[rows, rows, rows, vec, vec],
        sequential=True, lhs_norm=True)


def _low_half():
    return lax.broadcasted_iota(jnp.int32, (1, LANES), 1) < HEAD_DIM


def _half_mean(v, low):
    s_lo = jnp.sum(jnp.where(low, v, 0.0), axis=-1, keepdims=True)
    s_hi = jnp.sum(jnp.where(low, 0.0, v), axis=-1, keepdims=True)
    return jnp.where(low, s_lo, s_hi) * (1.0 / HEAD_DIM)


def _head_norm_bwd(val, dout, g, low):
    r = lax.rsqrt(_half_mean(val * val, low) + EPS)
    vn = val * r
    dvn = dout * g
    return r * (dvn - vn * _half_mean(dvn * vn, low)), dout * vn


def _conv_taps(vv_ext, w_ref):
    v0 = vv_ext[HALO:]
    v1 = pltpu.roll(vv_ext, 1, 0)[HALO:]
    v2 = pltpu.roll(vv_ext, 2, 0)[HALO:]
    return w_ref[2:3, :] * v0 + w_ref[1:2, :] * v1 + w_ref[0:1, :] * v2, (v0, v1, v2)


def _conv_fwd(proj, conv_w, g_conv, w_conv, d_model, tr=1024):
    s = proj.shape[0]
    tr = _tile(s, tr)
    hb = tr // HALO

    def main(part):
        return pl.BlockSpec((tr, w_conv), lambda i: (i, part))

    def prev(part):
        return pl.BlockSpec((HALO, w_conv), lambda i: (jnp.maximum(i * hb - 1, 0), part))

    def body(cb_ref, cc_ref, cu_ref, ccp_ref, cup_ref, w_ref, g_ref, o_ref):
        i = pl.program_id(0)
        low = _low_half()
        for j in range(w_conv // LANES):
            cols = slice(j * LANES, (j + 1) * LANES)
            vv_prev = jnp.where(i > 0, ccp_ref[:, cols] * cup_ref[:, cols], 0.0)
            vv_ext = jnp.concatenate([vv_prev, cc_ref[:, cols] * cu_ref[:, cols]], axis=0)
            y, _ = _conv_taps(vv_ext, w_ref.at[:, cols])
            co = cb_ref[:, cols] * y
            r = lax.rsqrt(_half_mean(co * co, low) + EPS)
            o_ref[:, cols] = (co * r * g_ref[:, cols]).astype(BF16)

    return _pcall(
        body, name="conv_fwd", grid=(s // tr,),
        in_specs=[main(0), main(1), main(2), prev(1), prev(2),
                  pl.BlockSpec((CONV_K, w_conv), lambda i: (0, 0)),
                  pl.BlockSpec((1, w_conv), lambda i: (0, 0))],
        out_specs=pl.BlockSpec((tr, w_conv), lambda i: (i, 0)),
        out_shape=jax.ShapeDtypeStruct((s, d_model), BF16),
        compiler_params=_params(("parallel",)),
    )(proj, proj, proj, proj, proj, conv_w, g_conv)


def _conv_bwd(proj, dcat, conv_w, g_conv, dproj, w_conv, tr=1024):
    s = proj.shape[0]
    tr = _tile(s, tr)
    hb = tr // HALO
    last = s // HALO - 1
    nt = s // tr

    def main(part):
        return pl.BlockSpec((tr, w_conv), lambda i: (i, part))

    def prev(part):
        return pl.BlockSpec((HALO, w_conv), lambda i: (jnp.maximum(i * hb - 1, 0), part))

    def nxt(part):
        return pl.BlockSpec((HALO, w_conv), lambda i: (jnp.minimum((i + 1) * hb, last), part))

    def body(cb_ref, cc_ref, cu_ref, dc_ref, ccp_ref, cup_ref, cbn_ref, ccn_ref, cun_ref, dcn_ref,
             w_ref, g_ref, dproj_in, dproj_ref, dw_ref, dg_ref):
        del dproj_in
        i = pl.program_id(0)

        @pl.when(i == 0)
        def _():
            dw_ref[...] = jnp.zeros_like(dw_ref)
            dg_ref[...] = jnp.zeros_like(dg_ref)

        low = _low_half()
        n_ext = tr + HALO
        rowid = lax.broadcasted_iota(jnp.int32, (n_ext, 1), 0)
        for j in range(w_conv // LANES):
            cols = slice(j * LANES, (j + 1) * LANES)
            wj = w_ref.at[:, cols]
            cc, cu = cc_ref[:, cols], cu_ref[:, cols]
            vv_prev = jnp.where(i > 0, ccp_ref[:, cols] * cup_ref[:, cols], 0.0)
            vv_ext = jnp.concatenate([vv_prev, cc * cu, ccn_ref[:, cols] * cun_ref[:, cols]], axis=0)
            y_ext, (v0, v1, v2) = _conv_taps(vv_ext, wj)
            cb_ext = jnp.concatenate([cb_ref[:, cols], cbn_ref[:, cols]], axis=0)
            dc_ext = jnp.concatenate([dc_ref[:, cols], dcn_ref[:, cols]], axis=0)
            dco, dgn = _head_norm_bwd(cb_ext * y_ext, dc_ext, g_ref[:, cols], low)
            dyc = jnp.where((rowid < tr) | (i < nt - 1), dco * cb_ext, 0.0)
            dvv = (wj[2:3, :] * dyc[:tr] + wj[1:2, :] * pltpu.roll(dyc, n_ext - 1, 0)[:tr]
                   + wj[0:1, :] * pltpu.roll(dyc, n_ext - 2, 0)[:tr])
            dproj_ref[:, cols] = (dco[:tr] * y_ext[:tr]).astype(BF16)
            dproj_ref[:, w_conv + j * LANES:w_conv + (j + 1) * LANES] = (dvv * cu).astype(BF16)
            dproj_ref[:, 2 * w_conv + j * LANES:2 * w_conv + (j + 1) * LANES] = (dvv * cc).astype(BF16)
            dyt = dyc[:tr]
            for tap, shifted in enumerate((v2, v1, v0)):
                dw_ref[tap:tap + 1, cols] += jnp.sum(dyt * shifted[:tr], axis=0, keepdims=True)
            dg_ref[:, cols] += jnp.sum(dgn[:tr], axis=0, keepdims=True)

    n_cols = dproj.shape[1]
    return _pcall(
        body, name="conv_bwd", grid=(nt,),
        in_specs=[main(0), main(1), main(2), main(0),
                  prev(1), prev(2), nxt(0), nxt(1), nxt(2), nxt(0),
                  pl.BlockSpec((CONV_K, w_conv), lambda i: (0, 0)),
                  pl.BlockSpec((1, w_conv), lambda i: (0, 0)),
                  pl.BlockSpec(memory_space=pl.ANY)],
        out_specs=[pl.BlockSpec((tr, 3 * w_conv), lambda i: (i, 0)),
                   pl.BlockSpec((CONV_K, w_conv), lambda i: (0, 0)),
                   pl.BlockSpec((1, w_conv), lambda i: (0, 0))],
        out_shape=[jax.ShapeDtypeStruct((s, n_cols), BF16),
                   jax.ShapeDtypeStruct((CONV_K, w_conv), F32),
                   jax.ShapeDtypeStruct((1, w_conv), F32)],
        input_output_aliases={12: 0},
        compiler_params=_params(("arbitrary",)),
    )(proj, proj, proj, dcat, proj, proj, proj, proj, proj, dcat, conv_w, g_conv, dproj)


STRIP = 16

ALL_CHAINS = (0, 1, 2, 3)
UPPER_CHAINS = (2, 3)


RUN_FLOOR = -104.0


def _any_weight_left(run_s):
    return (jnp.max(run_s[...]) > RUN_FLOOR).astype(jnp.int32)


def _chains(low):
    return [(2 * half + h, half, msk) for half in range(2)
            for h, msk in enumerate((low, jnp.logical_not(low)))]


def _suffix_operator(t):
    r = lax.broadcasted_iota(jnp.int32, (2 * t, t), 0)
    c = lax.broadcasted_iota(jnp.int32, (2 * t, t), 1)
    return jnp.where((r > c) & ((r < t) | (r - t > c)), 1.0, 0.0).astype(BF16)


def _strips(t, diag):
    return [(i, slice(i * STRIP, (i + 1) * STRIP), t // 2 if diag and (i + 1) * STRIP <= t // 2 else t)
            for i in range(t // STRIP)]


def _strip_mask(i, w):
    r = lax.broadcasted_iota(jnp.int32, (STRIP, w), 0) + i * STRIP
    c = lax.broadcasted_iota(jnp.int32, (STRIP, w), 1)
    return r > c


def _store_trimmed(ref, rows, val, w, t, at=0):
    ref[rows, at:at + w] = val
    if w < t:
        ref[rows, at + w:at + t] = jnp.zeros((STRIP, t - w), val.dtype)


def _store_split(ref, rows, val, w, t):
    hi = val.astype(BF16)
    _store_trimmed(ref, rows, hi, w, t)
    _store_trimmed(ref, rows, (val - hi.astype(F32)).astype(BF16), w, t, at=t)


def _sb_scores(z_s, split_s, zl_s, tot_s, keep_s, t, diag):
    for i, rows, w in _strips(t, diag):
        z = z_s[rows, :w]
        log_beta = jnp.minimum(z, 0.0) - jnp.log(1.0 + jnp.exp(-jnp.abs(z)))
        log_keep = log_beta - z
        if diag:
            log_keep = jnp.where(_strip_mask(i, w), log_keep, 0.0)
        _store_split(split_s, rows, log_keep, w, t)
        zl_s[rows, :w] = log_beta
        tot_s[rows, :] = _row_sum(log_keep)
        if keep_s is not None:
            keep_s[rows, :w] = jnp.exp(log_keep)


def _row_sum(v):
    return jnp.broadcast_to(jnp.sum(v, axis=-1, keepdims=True), (v.shape[0], LANES))


def _wide(r, t):
    return jnp.concatenate([r] * (t // LANES), axis=1)


def _sb_weights(zl_s, suf_s, run_s, tot_s, a_s, t, diag, da_s=None, glog_s=None, gsplit_s=None, gtot_s=None):
    for i, rows, w in _strips(t, diag):
        run = run_s[rows, :]
        a = jnp.exp(zl_s[rows, :w] + suf_s[rows, :w] + _wide(run, w))
        if diag:
            a = jnp.where(_strip_mask(i, w), a, 0.0)
        ab = a.astype(BF16)
        _store_trimmed(a_s, rows, ab, w, t)
        run_s[rows, :] = run + tot_s[rows, :]
        if da_s is not None:
            glog = ab.astype(F32) * da_s[rows, :w]
            glog_s[rows, :w] = glog
            _store_split(gsplit_s, rows, glog, w, t)
            gtot_s[rows, :] = _row_sum(glog)


def _sb_dscores(glog_s, cum_s, rest_s, gtot_s, keep_s, dz_s, t, diag):
    for i, rows, w in _strips(t, diag):
        glog = glog_s[rows, :w]
        rest = rest_s[rows, :]
        from_here = _wide(rest, w) - cum_s[rows, :w]
        before = from_here - glog
        dz = from_here * keep_s[rows, :w] - before
        if diag:
            dz = jnp.where(_strip_mask(i, w), dz, 0.0)
        _store_trimmed(dz_s, rows, dz.astype(BF16), w, t)
        rest_s[rows, :] = rest - gtot_s[rows, :]


def _attn_fwd(proj, g_attn, cat, w_conv, carry, t=ATTN_BLOCK):
    s = proj.shape[0]
    w_attn = g_attn.shape[1]
    nh = w_attn // LANES
    t = _tile(s, t)
    tq = 2 * t
    nq = s // tq
    q0 = 3 * w_conv // LANES
    scale = HEAD_DIM ** -0.5
    plan = _Carried(carry)
    nw, n_res = len(plan.inputs), len(plan.out_shapes)

    def body(q_ref, k_ref, v_ref, g_ref, cat_in, *rest):
        staged_refs, rest = rest[:nw], rest[nw:]
        o_ref, cat_ref = rest[:2]
        gathered_refs, rest = rest[2:2 + n_res], rest[2 + n_res:]
        kb, vb, tri_s, qm_s, z_s, split_s, zl_s, suf_s, a_s, run_s, tot_s, acc_s = rest[:12]
        gather_sems = rest[12:]
        del cat_in
        qi = pl.program_id(1)

        @pl.when((pl.program_id(0) == 0) & (qi == 0))
        def _():
            for cp in plan.copies(staged_refs, gathered_refs, gather_sems):
                cp.start()

        @pl.when(qi == 0)
        def _():
            kb[...] = k_ref[...].astype(BF16)
            vb[...] = v_ref[...].astype(BF16)
            tri_s[...] = _suffix_operator(t)

        low = _low_half()
        for c, half, msk in _chains(low):
            qm_s[c] = jnp.where(msk, q_ref[half * t:(half + 1) * t, :] * scale, 0.0).astype(BF16)
            run_s[c] = jnp.zeros((t, LANES), F32)
            acc_s[c] = jnp.zeros((t, LANES), F32)

        def key_rows(kblk):
            return pl.ds(pl.multiple_of(kblk * t, t), t)

        def key_block(base, c):
            return key_rows(jnp.maximum(base + c // 2, 0))

        def scores_matmul(base, chains):
            for c in chains:
                z_s[c] = lax.dot_general(qm_s[c], kb[key_block(base, c), :], _NT, preferred_element_type=F32)

        def front(modes, base, prev=None):
            for c, diag in modes:
                _sb_scores(z_s.at[c], split_s.at[c], zl_s.at[c], tot_s.at[c], None, t, diag)
                suf_s[c] = jnp.dot(split_s[c], tri_s[...], preferred_element_type=F32)
            if prev is not None:
                tail(*prev)
            scores_matmul(base - 1, ALL_CHAINS)
            for c, diag in modes:
                _sb_weights(zl_s.at[c], suf_s.at[c], run_s.at[c], tot_s.at[c], a_s.at[c], t, diag)

        def tail(base, chains):
            for c in chains:
                acc_s[c] += jnp.dot(a_s[c], vb[key_block(base, c), :], preferred_element_type=F32)

        first = 2 * qi
        scores_matmul(first, ALL_CHAINS)
        front([(c, True) for c in ALL_CHAINS], first)

        def loop(state):
            it = state[0]
            base = first - 1 - it
            front([(c, False) for c in ALL_CHAINS], base, prev=(base + 1, ALL_CHAINS))
            return it + 1, _any_weight_left(run_s)

        done, live = lax.while_loop(lambda state: (state[0] < first) & (state[1] > 0), loop,
                                    (jnp.int32(0), jnp.int32(1)))
        one_more = (done == first) & (live > 0)

        @pl.when(one_more)
        def _():
            front([(c, False) for c in UPPER_CHAINS], -1, prev=(0, ALL_CHAINS))
            tail(-1, UPPER_CHAINS)

        @pl.when(jnp.logical_not(one_more))
        def _():
            tail(first - done, ALL_CHAINS)

        for half in range(2):
            rows = slice(half * t, (half + 1) * t)
            o = jnp.where(low, acc_s[2 * half], acc_s[2 * half + 1])
            o_ref[rows, :] = o
            r = lax.rsqrt(_half_mean(o * o, low) + EPS)
            cat_ref[rows, :] = (o * r * g_ref[...]).astype(BF16)

        @pl.when((pl.program_id(0) == nh - 1) & (qi == nq - 1))
        def _():
            for cp in plan.copies(staged_refs, gathered_refs, gather_sems):
                cp.wait()

    whole = lambda col0: pl.BlockSpec((s, LANES), lambda h, i: (0, col0 + h))
    n_ch = len(ALL_CHAINS)
    res = _pcall(
        body, name="attn_fwd", grid=(nh, nq),
        in_specs=[pl.BlockSpec((tq, LANES), lambda h, i: (i, q0 + h)),
                  whole(q0 + nh), whole(q0 + 2 * nh),
                  pl.BlockSpec((1, LANES), lambda h, i: (0, h)),
                  pl.BlockSpec(memory_space=pl.ANY)] + [pl.BlockSpec(memory_space=pl.ANY)] * nw,
        out_specs=[pl.BlockSpec((tq, LANES), lambda h, i: (i, h)),
                   pl.BlockSpec((tq, LANES), lambda h, i: (i, w_conv // LANES + h))]
        + [pl.BlockSpec(memory_space=pl.ANY)] * n_res,
        out_shape=[jax.ShapeDtypeStruct((s, w_attn), F32),
                   jax.ShapeDtypeStruct(cat.shape, BF16)] + plan.out_shapes,
        scratch_shapes=[pltpu.VMEM((s, LANES), BF16), pltpu.VMEM((s, LANES), BF16),
                        pltpu.VMEM((2 * t, t), BF16),
                        pltpu.VMEM((n_ch, t, LANES), BF16),
                        pltpu.VMEM((n_ch, t, t), F32),
                        pltpu.VMEM((n_ch, t, 2 * t), BF16),
                        pltpu.VMEM((n_ch, t, t), F32),
                        pltpu.VMEM((n_ch, t, t), F32),
                        pltpu.VMEM((n_ch, t, t), BF16),
                        pltpu.VMEM((n_ch, t, LANES), F32),
                        pltpu.VMEM((n_ch, t, LANES), F32),
                        pltpu.VMEM((n_ch, t, LANES), F32)]
        + plan.sems,
        input_output_aliases={4: 1, **plan.aliases(5, 2)},
        compiler_params=_params(("arbitrary", "arbitrary")),
    )(proj, proj, proj, g_attn, cat, *plan.inputs)
    return res[0], res[1], res[2:]


def _attn_bwd(proj, o, dcat, g_attn, w_conv, carry, t=ATTN_BLOCK):
    s, n_cols = proj.shape
    w_attn = g_attn.shape[1]
    nh = w_attn // LANES
    t = _tile(s, t)
    tq = 2 * t
    nq = s // tq
    q0 = 3 * w_conv // LANES
    scale = HEAD_DIM ** -0.5
    plan = _Carried(carry)
    nw, n_res = len(plan.inputs), len(plan.out_shapes)

    def body(q_ref, k_ref, v_ref, o_ref, do_ref, g_ref, *rest):
        partial_refs, rest = rest[:nw], rest[nw:]
        dproj_ref, dg_ref = rest[:2]
        received_refs, rest = rest[2:2 + n_res], rest[2 + n_res:]
        (kb, vb, dkt_acc, dvt_acc, stash, tri_s, qm_s, dom_s, qt_s, dot_s, z_s, da_s, split_s, zl_s,
         keep_s, suf_s, a_s, glog_s, gsplit_s, cum_s, dz_s, run_s, tot_s, rest_s, gtot_s, dq_s) = rest[:26]
        out_sems, scatter_sems = rest[26], rest[27:]
        step_i = pl.program_id(1)
        qi = nq - 1 - step_i
        head_pair = pl.program_id(0)
        first_step = (head_pair == 0) & (step_i == 0)
        last_step = (head_pair == nh - 1) & (step_i == nq - 1)

        @pl.when(first_step)
        def _():
            for cp in plan.copies(partial_refs, received_refs, scatter_sems):
                cp.start()

        def out_copies():
            rows = pl.ds(pl.multiple_of(qi * tq, tq), tq)
            return [pltpu.make_async_copy(
                stash.at[w], dproj_ref.at[rows, pl.ds(pl.multiple_of((q0 + w * nh + head_pair) * LANES, LANES), LANES)],
                out_sems.at[w]) for w in range(3)]

        def walk():
            @pl.when(step_i == 0)
            def _():
                kb[...] = k_ref[...].astype(BF16)
                vb[...] = v_ref[...].astype(BF16)
                tri_s[...] = _suffix_operator(t)
                dkt_acc[...] = jnp.zeros_like(dkt_acc)
                dvt_acc[...] = jnp.zeros_like(dvt_acc)
                dg_ref[...] = jnp.zeros_like(dg_ref)

            low = _low_half()
            gv = g_ref[...]
            for half in range(2):
                rows = slice(half * t, (half + 1) * t)
                q = q_ref[rows, :] * scale
                ov = o_ref[rows, :]
                d_o, dgn = _head_norm_bwd(ov, do_ref[rows, :], gv, low)
                dg_ref[...] += jnp.sum(dgn, axis=0, keepdims=True)
                for h, msk in enumerate((low, jnp.logical_not(low))):
                    c = 2 * half + h
                    qh = jnp.where(msk, q, 0.0)
                    doh = jnp.where(msk, d_o, 0.0)
                    dom = doh.astype(BF16)
                    qm_s[c] = qh.astype(BF16)
                    dom_s[c] = dom
                    qt_s[c] = qh.T.astype(BF16)
                    dot_s[c] = doh.T.astype(BF16)
                    rest_s[c] = _row_sum(dom.astype(F32) * ov)
                    run_s[c] = jnp.zeros((t, LANES), F32)
                    dq_s[c] = jnp.zeros((t, LANES), F32)

            def key_rows(kblk):
                return pl.ds(pl.multiple_of(kblk * t, t), t)

            def block_of(base, half):
                return jnp.maximum(base + half, 0)

            def scores_matmul(base, chains):
                for c in chains:
                    ks = kb[key_rows(block_of(base, c // 2)), :]
                    z_s[c] = lax.dot_general(qm_s[c], ks, _NT, preferred_element_type=F32)

            def da_matmul(base, chains):
                for c in chains:
                    vs = vb[key_rows(block_of(base, c // 2)), :]
                    da_s[c] = lax.dot_general(dom_s[c], vs, _NT, preferred_element_type=F32)

            def front(modes, base, prev=None):
                if prev is not None:
                    tail(*prev)
                for c, diag in modes:
                    _sb_scores(z_s.at[c], split_s.at[c], zl_s.at[c], tot_s.at[c], keep_s.at[c], t, diag)
                    suf_s[c] = jnp.dot(split_s[c], tri_s[...], preferred_element_type=F32)
                scores_matmul(base - 1, ALL_CHAINS)
                for c, diag in modes:
                    _sb_weights(zl_s.at[c], suf_s.at[c], run_s.at[c], tot_s.at[c], a_s.at[c], t, diag,
                                da_s.at[c], glog_s.at[c], gsplit_s.at[c], gtot_s.at[c])
                    cum_s[c] = jnp.dot(gsplit_s[c], tri_s[...], preferred_element_type=F32)
                da_matmul(base - 1, ALL_CHAINS)
                for c, diag in modes:
                    _sb_dscores(glog_s.at[c], cum_s.at[c], rest_s.at[c], gtot_s.at[c], keep_s.at[c],
                                dz_s.at[c], t, diag)

            def tail(base, chains):
                for half in range(2):
                    mine = [c for c in chains if c // 2 == half]
                    if not mine:
                        continue
                    kblk = block_of(base, half)
                    ks = kb[key_rows(kblk), :]
                    dkt = dkt_acc[kblk]
                    dvt = dvt_acc[kblk]
                    for c in mine:
                        dq_s[c] += jnp.dot(dz_s[c], ks, preferred_element_type=F32)
                        dkt = dkt + jnp.dot(qt_s[c], dz_s[c], preferred_element_type=F32)
                        dvt = dvt + jnp.dot(dot_s[c], a_s[c], preferred_element_type=F32)
                    dkt_acc[kblk] = dkt
                    dvt_acc[kblk] = dvt

            first = 2 * qi
            scores_matmul(first, ALL_CHAINS)
            da_matmul(first, ALL_CHAINS)
            front([(c, True) for c in ALL_CHAINS], first)

            def loop(state):
                it = state[0]
                base = first - 1 - it
                front([(c, False) for c in ALL_CHAINS], base, prev=(base + 1, ALL_CHAINS))
                return it + 1, _any_weight_left(run_s)

            done, live = lax.while_loop(lambda state: (state[0] < first) & (state[1] > 0), loop,
                                        (jnp.int32(0), jnp.int32(1)))
            one_more = (done == first) & (live > 0)

            @pl.when(one_more)
            def _():
                front([(c, False) for c in UPPER_CHAINS], -1, prev=(0, ALL_CHAINS))
                tail(-1, UPPER_CHAINS)

            @pl.when(jnp.logical_not(one_more))
            def _():
                tail(first - done, ALL_CHAINS)

            @pl.when(jnp.logical_not(first_step))
            def _():
                for cp in out_copies():
                    cp.wait()

            for half in range(2):
                rows = slice(half * t, (half + 1) * t)
                stash[0, rows, :] = (jnp.where(low, dq_s[2 * half], dq_s[2 * half + 1]) * scale).astype(BF16)
                stash[1, rows, :] = dkt_acc[2 * qi + half].T.astype(BF16)
                stash[2, rows, :] = dvt_acc[2 * qi + half].T.astype(BF16)
            for cp in out_copies():
                cp.start()

        walk()

        @pl.when(last_step)
        def _():
            for cp in out_copies():
                cp.wait()
            for cp in plan.copies(partial_refs, received_refs, scatter_sems):
                cp.wait()

    whole = lambda col0: pl.BlockSpec((s, LANES), lambda h, i: (0, col0 + h))
    blk = lambda col0: pl.BlockSpec((tq, LANES), lambda h, i: (nq - 1 - i, col0 + h))
    n_ch = len(ALL_CHAINS)
    res = _pcall(
        body, name="attn_bwd", grid=(nh, nq),
        in_specs=[blk(q0), whole(q0 + nh), whole(q0 + 2 * nh), blk(0), blk(w_conv // LANES),
                  pl.BlockSpec((1, LANES), lambda h, i: (0, h))] + [pl.BlockSpec(memory_space=pl.ANY)] * nw,
        out_specs=[pl.BlockSpec(memory_space=pl.ANY),
                   pl.BlockSpec((1, LANES), lambda h, i: (0, h))] + [pl.BlockSpec(memory_space=pl.ANY)] * n_res,
        out_shape=[jax.ShapeDtypeStruct((s, n_cols), BF16), jax.ShapeDtypeStruct((1, w_attn), F32)]
        + plan.out_shapes,
        scratch_shapes=[pltpu.VMEM((s, LANES), BF16), pltpu.VMEM((s, LANES), BF16),
                        pltpu.VMEM((s // t, LANES, t), F32),
                        pltpu.VMEM((s // t, LANES, t), F32),
                        pltpu.VMEM((3, tq, LANES), BF16),
                        pltpu.VMEM((2 * t, t), BF16),
                        pltpu.VMEM((n_ch, t, LANES), BF16),
                        pltpu.VMEM((n_ch, t, LANES), BF16),
                        pltpu.VMEM((n_ch, LANES, t), BF16),
                        pltpu.VMEM((n_ch, LANES, t), BF16),
                        pltpu.VMEM((n_ch, t, t), F32),
                        pltpu.VMEM((n_ch, t, t), F32),
                        pltpu.VMEM((n_ch, t, 2 * t), BF16),
                        pltpu.VMEM((n_ch, t, t), F32),
                        pltpu.VMEM((n_ch, t, t), F32),
                        pltpu.VMEM((n_ch, t, t), F32),
                        pltpu.VMEM((n_ch, t, t), BF16),
                        pltpu.VMEM((n_ch, t, t), F32),
                        pltpu.VMEM((n_ch, t, 2 * t), BF16),
                        pltpu.VMEM((n_ch, t, t), F32),
                        pltpu.VMEM((n_ch, t, t), BF16),
                        pltpu.VMEM((n_ch, t, LANES), F32),
                        pltpu.VMEM((n_ch, t, LANES), F32),
                        pltpu.VMEM((n_ch, t, LANES), F32),
                        pltpu.VMEM((n_ch, t, LANES), F32),
                        pltpu.VMEM((n_ch, t, LANES), F32),
                        pltpu.SemaphoreType.DMA((3,))]
        + plan.sems,
        input_output_aliases=plan.aliases(6, 2),
        compiler_params=_params(("arbitrary", "arbitrary")),
    )(proj, proj, proj, o, dcat, g_attn, *plan.inputs)
    return res[0], res[1], res[2:]


def _place():
    return lax.axis_index("x"), lax.axis_index("y"), lax.axis_index("c")


def _other_chips(x, y):
    return [(1 - x, y), (x, 1 - y), (1 - x, 1 - y)]


def _slot(px, py, pc):
    return 4 * px + 2 * py + pc


def _all_gather(shards, out_dtypes):
    nw = len(shards)

    def body(*refs):
        ins, outs, stage = refs[:nw], refs[nw:2 * nw], refs[2 * nw:3 * nw]
        send_sems, recv_sems, local_sems = refs[3 * nw:]
        x, y, c = _place()
        me, sibling = (x, y, c), (x, y, 1 - c)
        chips = _other_chips(x, y)

        def copy(w, k, block, to, src=None):
            dst = outs[w].at[_slot(*block)]
            return pltpu.make_async_remote_copy(
                src_ref=dst if src is None else src, dst_ref=dst,
                send_sem=send_sems.at[w * 7 + k], recv_sem=recv_sems.at[w * 7 + k],
                device_id=to, device_id_type=MESH)

        started = []
        local = []
        for w in range(nw):
            stage[w][...] = ins[w][...].astype(stage[w].dtype)
            cp = pltpu.make_async_copy(stage[w], outs[w].at[_slot(*me)], local_sems.at[w])
            cp.start()
            local.append(cp)
            started.append(copy(w, 0, me, sibling, src=stage[w]))
            started[-1].start()
            for j, chip in enumerate(chips):
                started.append(copy(w, 1 + j, me, (*chip, c), src=stage[w]))
                started[-1].start()
        for j, chip in enumerate(chips):
            for w in range(nw):
                copy(w, 1 + j, (*chip, c), me).wait_recv()
                started.append(copy(w, 4 + j, (*chip, c), sibling))
                started[-1].start()
        for w in range(nw):
            copy(w, 0, sibling, me).wait_recv()
            for j, chip in enumerate(chips):
                copy(w, 4 + j, (*chip, 1 - c), me).wait_recv()
        for cp in started:
            cp.wait_send()
        for cp in local:
            cp.wait()

    return _pcall(
        body, name="all_gather_weights",
        in_specs=[pl.BlockSpec(memory_space=pltpu.VMEM)] * nw,
        out_specs=[pl.BlockSpec(memory_space=pl.ANY)] * nw,
        out_shape=[jax.ShapeDtypeStruct((N_DEV, *a.shape), d) for a, d in zip(shards, out_dtypes)],
        scratch_shapes=[pltpu.VMEM(a.shape, d) for a, d in zip(shards, out_dtypes)]
        + [pltpu.SemaphoreType.DMA((7 * nw,)), pltpu.SemaphoreType.DMA((7 * nw,)),
           pltpu.SemaphoreType.DMA((nw,))],
        compiler_params=_params(),
    )(*shards)


N_PEERS = N_DEV - 1


def _peer(k):
    x, y, c = _place()
    return (x ^ (k >> 2), y ^ ((k >> 1) & 1), c ^ (k & 1))


def _remote(src, dst, sems, index, to):
    return pltpu.make_async_remote_copy(src_ref=src, dst_ref=dst, send_sem=sems[0].at[index],
                                        recv_sem=sems[1].at[index], device_id=to, device_id_type=MESH)


def _gather_out_copies(staged, gathered, sems):
    x, y, c = _place()
    me = _slot(x, y, c)
    targets = [(x, y, 1 - c)] + [(*chip, c) for chip in _other_chips(x, y)]
    copies = []
    for w, (src, dst) in enumerate(zip(staged, gathered)):
        copies.append(pltpu.make_async_copy(src, dst.at[me], sems[2].at[w]))
        copies += [_remote(src, dst.at[me], sems, w * len(targets) + k, to) for k, to in enumerate(targets)]
    return copies


def _gather_pass_copies(arrived, gathered, sems):
    x, y, c = _place()
    chips = _other_chips(x, y)
    return [_remote(src.at[_slot(*chip, c)], dst.at[_slot(*chip, c)], sems, w * len(chips) + j, (x, y, 1 - c))
            for w, (src, dst) in enumerate(zip(arrived, gathered)) for j, chip in enumerate(chips)]


ALL_PEERS = tuple(range(1, N_DEV))


def _scatter_copies(partials, received, sems, peers=ALL_PEERS):
    me = _slot(*_place())
    return [_remote(src.at[me ^ k], dst.at[k - 1], sems, w * N_PEERS + k - 1, _peer(k))
            for w, (src, dst) in enumerate(zip(partials, received)) for k in peers]


class _Carried:
    def __init__(self, jobs):
        self.jobs = [(job[0], list(job[1]), job[2] if len(job) > 2 else ALL_PEERS) for job in jobs if len(job[1])]
        self.inputs, self.out_shapes, self.sems, self.counts = [], [], [], []
        for kind, arrays, _ in self.jobs:
            n_out = len(arrays) // 2 if kind == "scatter_more" else len(arrays)
            fan = {"gather_out": 4, "gather_pass": 3}.get(kind, N_PEERS)
            for a in arrays[len(arrays) - n_out:]:
                shape = {"gather_out": (N_DEV, *a.shape), "scatter": (N_PEERS, *a.shape[1:])}.get(kind, a.shape)
                self.out_shapes.append(jax.ShapeDtypeStruct(shape, BF16))
            job_sems = [pltpu.SemaphoreType.DMA((fan * n_out,))] * 2
            job_sems += [pltpu.SemaphoreType.DMA((n_out,))] if kind == "gather_out" else []
            self.inputs += arrays
            self.sems += job_sems
            self.counts.append((len(arrays), n_out, len(job_sems)))

    def aliases(self, first_input, first_output):
        pairs, at_in, at_out = {}, first_input, first_output
        for (kind, _, _), (n_in, n_out, _) in zip(self.jobs, self.counts):
            if kind in ("gather_pass", "scatter_more"):
                pairs.update({at_in + n_in - n_out + i: at_out + i for i in range(n_out)})
            at_in, at_out = at_in + n_in, at_out + n_out
        return pairs

    def copies(self, in_refs, out_refs, sem_refs):
        out, at_in, at_out, at_sem = [], 0, 0, 0
        for (kind, _, peers), (n_in, n_out, n_sems) in zip(self.jobs, self.counts):
            srcs, dsts = in_refs[at_in:at_in + n_out], out_refs[at_out:at_out + n_out]
            sems = sem_refs[at_sem:at_sem + n_sems]
            if kind == "gather_out":
                out += _gather_out_copies(srcs, dsts, sems)
            elif kind == "gather_pass":
                out += _gather_pass_copies(srcs, dsts, sems)
            else:
                out += _scatter_copies(srcs, dsts, sems, peers)
            at_in, at_out, at_sem = at_in + n_in, at_out + n_out, at_sem + n_sems
        return out


def _cast_shards(shards):
    def body(*refs):
        for src, dst in zip(refs[:len(shards)], refs[len(shards):]):
            dst[...] = src[...].astype(BF16)

    return _pcall(
        body, name="cast_shards",
        in_specs=[pl.BlockSpec(memory_space=pltpu.VMEM)] * len(shards),
        out_specs=[pl.BlockSpec(memory_space=pltpu.VMEM)] * len(shards),
        out_shape=[jax.ShapeDtypeStruct(a.shape, BF16) for a in shards],
        compiler_params=_params(),
    )(*shards)


def _all_reduce_small(packed):
    r = packed.shape[0]

    def body(x_ref, o_ref, gathered, send_sems, recv_sems):
        x, y, c = _place()
        me = _slot(x, y, c)
        gathered[me] = x_ref[...]
        copies = [_remote(x_ref, gathered.at[me], (send_sems, recv_sems), k - 1, _peer(k)) for k in ALL_PEERS]
        for cp in copies:
            cp.start()
        for cp in copies:
            cp.wait()
        total = gathered[0]
        for k in range(1, N_DEV):
            total = total + gathered[k]
        o_ref[...] = total

    return _pcall(
        body, name="all_reduce_small",
        in_specs=[pl.BlockSpec(memory_space=pltpu.VMEM)],
        out_specs=pl.BlockSpec(memory_space=pltpu.VMEM),
        out_shape=jax.ShapeDtypeStruct(packed.shape, F32),
        scratch_shapes=[pltpu.VMEM((N_DEV, r, LANES), F32),
                        pltpu.SemaphoreType.DMA((N_DEV - 1,)), pltpu.SemaphoreType.DMA((N_DEV - 1,))],
        compiler_params=_params(),
    )(packed)


def _adam_math(w, g, m, v):
    m = ADAM_B1 * m + (1.0 - ADAM_B1) * g
    v = ADAM_B2 * v + (1.0 - ADAM_B2) * jnp.square(g)
    m_hat = m / (1.0 - ADAM_B1 ** ADAM_STEP)
    v_hat = v / (1.0 - ADAM_B2 ** ADAM_STEP)
    delta = -ADAM_LR * (m_hat / (jnp.sqrt(v_hat) + ADAM_EPS) + ADAM_WD * w)
    return delta, m, v


ADAM_TILE_BYTES = 24 * 1024 * 1024


def _adam_sharded(name, own, received, w, m, v, place):
    r, cdim = w.shape
    row_bytes = 2 * cdim * (4 + 2 * N_PEERS + 3 * 4 + 4 * 4)
    tr = _tile(r, max(LANES, ADAM_TILE_BYTES // row_bytes // LANES * LANES)) if r % LANES == 0 else r

    def body(place_ref, own_ref, rec_ref, w_ref, m_ref, v_ref, g_ref, d_ref, nm_ref, nv_ref):
        del place_ref
        g = own_ref[...]
        for j in range(N_PEERS):
            g = g + rec_ref[j].astype(F32)
        delta, nm, nv = _adam_math(w_ref[...], g, m_ref[...], v_ref[...])
        g_ref[...] = g
        d_ref[...] = delta
        nm_ref[...] = nm
        nv_ref[...] = nv

    blk = pl.BlockSpec((tr, cdim), lambda i, pr: (i, 0))
    grid_spec = pltpu.PrefetchScalarGridSpec(
        num_scalar_prefetch=1, grid=(r // tr,),
        in_specs=[pl.BlockSpec((None, tr, cdim), lambda i, pr: (4 * pr[0] + 2 * pr[1] + pr[2], i, 0)),
                  pl.BlockSpec((N_PEERS, tr, cdim), lambda i, pr: (0, i, 0)), blk, blk, blk],
        out_specs=[blk] * 4)
    return _pcall(body, name=name, grid_spec=grid_spec,
                  out_shape=[jax.ShapeDtypeStruct((r, cdim), F32)] * 4,
                  compiler_params=_params(("parallel",)))(place, own, received, w, m, v)


def _adam_small(w, g, m, v):
    def body(w_ref, g_ref, m_ref, v_ref, d_ref, nm_ref, nv_ref):
        delta, nm, nv = _adam_math(w_ref[...], g_ref[...], m_ref[...], v_ref[...])
        d_ref[...] = delta
        nm_ref[...] = nm
        nv_ref[...] = nv

    return _pcall(body, name="adam_small",
                  in_specs=[pl.BlockSpec(memory_space=pltpu.VMEM)] * 4,
                  out_specs=[pl.BlockSpec(memory_space=pltpu.VMEM)] * 3,
                  out_shape=[jax.ShapeDtypeStruct(w.shape, F32)] * 3,
                  compiler_params=_params())(w, g, m, v)


def _rows(vec):
    return vec.reshape(-1, LANES)


def kernel(x, p, g_mix, w_in, conv_w, g_conv_out, g_attn_out, w_out, g_mlp, w_up, w_down, g_ple, w_ple_gate, w_ple_proj, g_final, loss_target, m_g_mix, m_w_in, m_conv_w, m_g_conv_out, m_g_attn_out, m_w_out, m_g_mlp, m_w_up, m_w_down, m_g_ple, m_w_ple_gate, m_w_ple_proj, m_g_final, v_g_mix, v_w_in, v_conv_w, v_g_conv_out, v_g_attn_out, v_w_out, v_g_mlp, v_w_up, v_w_down, v_g_ple, v_w_ple_gate, v_w_ple_proj, v_g_final):
    s, d = x.shape[1], x.shape[2]
    w_conv = g_conv_out.shape[1]
    w_attn = g_attn_out.shape[1]
    cw = conv_w.shape[2]
    xs, ps, tgt = x[0], p[0, 0], loss_target[0]
    place = jnp.stack([lax.axis_index("x"), lax.axis_index("y"), lax.axis_index("c")]).astype(jnp.int32)
    my_slot = 4 * place[0] + 2 * place[1] + place[2]

    conv_tile = jnp.pad(conv_w[0], ((0, HALO - CONV_K), (0, LANES - cw)))
    big = [w_in[0], w_out[0], w_up[0], w_down[0], w_ple_gate[0], w_ple_proj[0]]
    win_g, conv_g = _all_gather([big[0], conv_tile], [BF16, F32])
    s_out, s_up, s_down, s_gate, s_proj = _cast_shards(big[1:])
    conv_full = jnp.transpose(conv_g[:, :CONV_K, :cw], (1, 0, 2)).reshape(CONV_K, w_conv)
    in_shard, up_shard, proj_shard = big[0].shape[1], big[2].shape[1], big[5].shape[1]

    proj, a, g_out, g_gate, g_proj = _mm_nn("in_proj", xs, win_g, n_shard=in_shard, tn=in_shard, tm=2048,
                                            lhs_norm=g_mix, carry=[("gather_out", [s_out, s_gate, s_proj])])
    cat = _conv_fwd(proj, conv_full, g_conv_out, w_conv, d)
    o, cat, (g_up, g_down, wout_g, wgate_g, wproj_g) = _attn_fwd(
        proj, g_attn_out, cat, w_conv,
        [("gather_out", [s_up, s_down]), ("gather_pass", [g_out, g_gate, g_proj])])
    wout_f = wout_g.reshape(-1, wout_g.shape[-1])
    wgate_f = wgate_g.reshape(-1, wgate_g.shape[-1])
    h1, wup_g = _mm_nn("out_proj", cat, wout_f, epilogue=_ep_residual, extras=(xs,),
                       carry=[("gather_pass", [g_up])])
    act, mn, wdown_g = _mm_nn("mlp_up", h1, wup_g, n_shard=up_shard, epilogue=_ep_up, out_dtypes=(BF16,), tm=2048,
                              lhs_norm=g_mlp, carry=[("gather_pass", [g_down])])
    wdown_f = wdown_g.reshape(-1, wdown_g.shape[-1])
    h2, = _mm_nn("mlp_down", act, wdown_f, epilogue=_ep_residual, extras=(h1,))
    pp = _ple_proj(ps, wproj_g)
    loss_part, dh3, dgl, dpp, dg_final, n3 = _ple_gate_loss(h2, g_ple, wgate_f, pp, tgt, g_final.reshape(1, d))

    def slots(t2d):
        return t2d.reshape(N_DEV, -1, t2d.shape[-1])

    dw_proj = _d_ple_proj(ps, dpp, proj_shard)
    dw_gate = [slots(t) for t in _mm_tn("d_w_ple_gate", n3, dgl)]
    dh2, dh2b, dg_ple = _mm_nt_norm_bwd("d_norm_ple", dgl, wgate_f, h2, g_ple, dh3)
    du, gate_recv, proj_recv = _mm_nt("d_mlp_act", dh2b, wdown_f, epilogue=_ep_dact, out_dtypes=(BF16,),
                                      extras=(act,), tm=2048, carry=[("scatter", [dw_gate[1], dw_proj[1]])])
    dw_down = [slots(t) for t in _mm_tn("d_w_down", act, dh2b)]
    near, far = (1, 2, 3, 4, 5), (6, 7)
    dw_up = _mm_tn("d_w_up", mn, du, n_shard=up_shard)
    dh1, dh1b, dg_mlp, down_part = _mm_nt_norm_bwd(
        "d_norm_mlp", du, wup_g, h1, g_mlp, dh2, k_shard=up_shard, tm=1024,
        carry=[("scatter", [dw_down[1]], near)])
    dcat, = _mm_nt("d_cat", dh1b, wout_f)
    dw_out = [slots(t) for t in _mm_tn("d_w_out", cat, dh1b)]
    dproj, dg_attn, (down_recv, up_recv) = _attn_bwd(
        proj, o, dcat, g_attn_out, w_conv,
        [("scatter_more", [dw_down[1], down_part], far), ("scatter", [dw_up[1]])])
    dproj, dconv, dg_conv = _conv_bwd(proj, dcat, conv_full, g_conv_out, dproj, w_conv)
    *dw_in, out_recv = _mm_tn("d_w_in", a, dproj, n_shard=in_shard, tn=in_shard,
                              carry=[("scatter", [dw_out[1]])])
    grad_x, _, dg_mix, in_recv = _mm_nt_norm_bwd("d_norm_mix", dproj, win_g, xs, g_mix, dh1, k_shard=in_shard,
                                                 tk=2 * in_shard, tm=1024, carry=[("scatter", [dw_in[1]])])

    names = ["w_in", "w_out", "w_up", "w_down", "w_ple_gate", "w_ple_proj"]
    owns = [dw_in[0], dw_out[0], dw_up[0], dw_down[0], dw_gate[0], dw_proj[0]]
    recvs = [in_recv, out_recv, up_recv, down_recv, gate_recv, proj_recv]
    moments = [(m_w_in, v_w_in), (m_w_out, v_w_out), (m_w_up, v_w_up), (m_w_down, v_w_down),
               (m_w_ple_gate, v_w_ple_gate), (m_w_ple_proj, v_w_ple_proj)]
    big_out = {}
    for n, own, rc, wt, (mm, vv) in zip(names, owns, recvs, big, moments):
        big_out[n] = [t[None] for t in _adam_sharded("adam_" + n, own, rc, wt, mm[0], vv[0], place)]

    n_conv_rows = CONV_K * w_conv // LANES
    small_g = jnp.concatenate(
        [_rows(dg_mix[0]), _rows(dg_conv[0]), _rows(dg_attn[0]), _rows(dg_mlp[0]), _rows(dg_ple[0]),
         _rows(dg_final[0]), _rows(dconv.reshape(-1)), loss_part], axis=0)
    n_gain_rows = small_g.shape[0] - n_conv_rows - 1
    pad_rows = (-small_g.shape[0]) % HALO
    small_g = _all_reduce_small(jnp.pad(small_g, ((0, pad_rows), (0, 0))))
    loss = small_g[n_gain_rows + n_conv_rows, 0]
    dconv_full = small_g[n_gain_rows:n_gain_rows + n_conv_rows].reshape(CONV_K, w_conv)
    dconv_mine = lax.dynamic_slice(dconv_full, (0, my_slot * cw), (CONV_K, cw))

    def pack(vecs, conv_part):
        rows = [_rows(t.reshape(-1)) for t in vecs]
        rows.append(jnp.pad(conv_part, ((0, HALO - CONV_K), (0, LANES - cw))))
        return jnp.concatenate(rows, axis=0)

    gains = [g_mix, g_conv_out, g_attn_out, g_mlp, g_ple, g_final]
    gains_m = [m_g_mix, m_g_conv_out, m_g_attn_out, m_g_mlp, m_g_ple, m_g_final]
    gains_v = [v_g_mix, v_g_conv_out, v_g_attn_out, v_g_mlp, v_g_ple, v_g_final]
    gpack = jnp.concatenate([small_g[:n_gain_rows], jnp.pad(dconv_mine, ((0, HALO - CONV_K), (0, LANES - cw)))], axis=0)
    sd, sm, sv = _adam_small(pack(gains, conv_w[0]), gpack, pack(gains_m, m_conv_w[0]), pack(gains_v, v_conv_w[0]))

    def unpack(packed):
        out, r0 = [], 0
        for t in gains:
            nr = t.size // LANES
            out.append(packed[r0:r0 + nr].reshape(t.shape))
            r0 += nr
        out.append(packed[r0:r0 + CONV_K, :cw][None])
        return out

    sg_l, sd_l, sm_l, sv_l = unpack(gpack), unpack(sd), unpack(sm), unpack(sv)
    small_names = ["g_mix", "g_conv_out", "g_attn_out", "g_mlp", "g_ple", "g_final", "conv_w"]
    small_out = {n: [sg_l[i], sd_l[i], sm_l[i], sv_l[i]] for i, n in enumerate(small_names)}

    order = ["g_mix", "w_in", "conv_w", "g_conv_out", "g_attn_out", "w_out", "g_mlp", "w_up", "w_down",
             "g_ple", "w_ple_gate", "w_ple_proj", "g_final"]
    table = {**big_out, **small_out}
    outs = [loss, grad_x[None]]
    for kind in range(4):
        outs.extend(table[n][kind] for n in order)
    return tuple(outs)
```

```python
import jax
import jax.numpy as jnp
from jax import lax
from jax.experimental import pallas as pl
from jax.experimental.pallas import tpu as pltpu

F32 = jnp.float32
BF16 = jnp.bfloat16
EPS = 1e-6
HEAD_DIM = 64
LANES = 128
CONV_K = 3
MXU_WIDTH = 256
ATTN_BLOCK = MXU_WIDTH
HALO = 8
N_DEV = 8
MESH = pl.DeviceIdType.MESH
VMEM_LIMIT = 56 * 1024 * 1024

ADAM_LR = 0.001
ADAM_B1 = 0.9
ADAM_B2 = 0.999
ADAM_EPS = 1e-08
ADAM_WD = 0.01
ADAM_STEP = 10


def _pcall(body, **kw):
    return pl.pallas_call(body, **kw)


def _params(sem=None, **kw):
    return pltpu.CompilerParams(dimension_semantics=sem, vmem_limit_bytes=VMEM_LIMIT, **kw)


def _tile(dim, pref):
    t = min(dim, pref)
    while dim % t:
        t -= LANES
    assert t > 0, (dim, pref)
    return t


_NN = (((1,), (0,)), ((), ()))
_NT = (((1,), (1,)), ((), ()))
_TN = (((0,), (0,)), ((), ()))


def _ep_store(acc, outs):
    outs[0][...] = acc.astype(outs[0].dtype)


def _ep_both(acc, outs):
    outs[0][...] = acc
    outs[1][...] = acc.astype(BF16)


def _ep_residual(acc, res, outs):
    outs[0][...] = acc + res[...]


def _ep_up(acc, outs):
    outs[0][...] = jnp.square(jnp.maximum(acc, 0.0)).astype(BF16)


def _ep_dact(acc, act, outs):
    outs[0][...] = (acc * (2.0 * jnp.sqrt(act[...].astype(F32)))).astype(BF16)


def _row_chunked(epilogue):
    def run(acc, *rest):
        *ex, outs = rest
        n = acc.shape[0]
        for m0 in range(0, n, MXU_WIDTH):
            rows = slice(m0, min(m0 + MXU_WIDTH, n))
            pick = lambda ref: ref.at[rows, :] if ref.shape[0] == n else ref
            epilogue(acc[rows, :], *[pick(e) for e in ex], [pick(o) for o in outs])
    return run


def _ep_norm_bwd(acc, h, g, dres, outs):
    hv = h[...]
    r = lax.rsqrt(jnp.mean(hv * hv, axis=-1, keepdims=True) + EPS)
    hn = hv * r
    outs[2][...] += jnp.sum(acc * hn, axis=0, keepdims=True)
    dhn = acc * g[...]
    dh = dres[...] + r * (dhn - hn * jnp.mean(dhn * hn, axis=-1, keepdims=True))
    outs[0][...] = dh
    outs[1][...] = dh.astype(BF16)


def _matmul(name, a, b, *, dims, grid, a_spec, b_spec, acc_shape, out_shapes, out_specs,
            epilogue=_ep_store, extras=(), extra_specs=(), carry=(), sequential=False, lhs_norm=False):
    nk = grid[2]
    plan = _Carried(carry)
    n_ex, n_out, n_xc, n_xo = len(extras), len(out_shapes), len(plan.inputs), len(plan.out_shapes)
    n_sems = len(plan.sems)
    last = tuple(g - 1 for g in grid)
    assert not lhs_norm or nk == 1

    def product(a_ref, b_ref):
        if len(b_ref.shape) == 2:
            return lax.dot_general(a_ref[...].astype(BF16), b_ref[...].astype(BF16), dims,
                                   preferred_element_type=F32)
        width = b_ref.shape[2]
        return sum(lax.dot_general(a_ref[:, g * width:(g + 1) * width].astype(BF16), b_ref[g].astype(BF16), dims,
                                   preferred_element_type=F32) for g in range(b_ref.shape[0]))

    def body(a_ref, b_ref, *rest):
        ex, rest = rest[:n_ex], rest[n_ex:]
        partials, rest = rest[:n_xc], rest[n_xc:]
        outs, rest = rest[:n_out], rest[n_out:]
        received, rest = rest[:n_xo], rest[n_xo:]
        ids = [pl.program_id(axis) for axis in range(3)]
        if n_xc:
            @pl.when((ids[0] == 0) & (ids[1] == 0) & (ids[2] == 0))
            def _():
                for cp in plan.copies(partials, received, rest[-n_sems:]):
                    cp.start()

        if lhs_norm:
            x_ref, a_ref, gain, ex, outs = a_ref, outs[-1], ex[-1], ex[:-1], outs[:-1]

            @pl.when(ids[1] == 0)
            def _():
                for m0 in range(0, acc_shape[0], MXU_WIDTH):
                    rows = slice(m0, min(m0 + MXU_WIDTH, acc_shape[0]))
                    xv = x_ref[rows, :]
                    r = lax.rsqrt(jnp.mean(xv * xv, axis=-1, keepdims=True) + EPS)
                    a_ref[rows, :] = (xv * r * gain[...]).astype(BF16)

        if nk == 1 and not sequential and dims != _TN:
            if len(b_ref.shape) == 3:
                ns = b_ref.shape[2]
                pieces = [(b_ref.at[g, :, n0:min(n0 + MXU_WIDTH, ns)], slice(g * ns + n0, g * ns + min(n0 + MXU_WIDTH, ns)))
                          for g in range(b_ref.shape[0]) for n0 in range(0, ns, MXU_WIDTH)]
            else:
                spans = [slice(n0, min(n0 + MXU_WIDTH, acc_shape[1])) for n0 in range(0, acc_shape[1], MXU_WIDTH)]
                pieces = [(b_ref.at[cols, :] if dims == _NT else b_ref.at[:, cols], cols) for cols in spans]
            for b_cols, cols in pieces:
                for m0 in range(0, acc_shape[0], MXU_WIDTH):
                    rows = slice(m0, min(m0 + MXU_WIDTH, acc_shape[0]))
                    epilogue(product(a_ref.at[rows, :], b_cols), *[e.at[rows, cols] for e in ex],
                             [o.at[rows, cols] for o in outs])
        else:
            if sequential:
                @pl.when((ids[0] == 0) & (ids[2] == 0))
                def _():
                    for o in outs:
                        if o.shape[0] != acc_shape[0]:
                            o[...] = jnp.zeros_like(o)

            if nk == 1:
                _row_chunked(epilogue)(product(a_ref, b_ref), *ex, outs)
            else:
                acc = rest[0]

                @pl.when(ids[2] == 0)
                def _():
                    acc[...] = product(a_ref, b_ref)

                @pl.when(ids[2] > 0)
                def _():
                    acc[...] += product(a_ref, b_ref)

                @pl.when(ids[2] == nk - 1)
                def _():
                    _row_chunked(epilogue)(acc, *ex, outs)

        if n_xc:
            @pl.when((ids[0] == last[0]) & (ids[1] == last[1]) & (ids[2] == last[2]))
            def _():
                for cp in plan.copies(partials, received, rest[-n_sems:]):
                    cp.wait()

    anywhere = pl.BlockSpec(memory_space=pl.ANY)
    return _pcall(
        body, name=name, grid=grid,
        in_specs=[a_spec, b_spec, *extra_specs, *[anywhere] * n_xc],
        out_specs=[*out_specs, *[anywhere] * n_xo],
        out_shape=[*out_shapes, *plan.out_shapes],
        scratch_shapes=([] if nk == 1 else [pltpu.VMEM(acc_shape, F32)]) + plan.sems,
        input_output_aliases=plan.aliases(2 + n_ex, n_out),
        compiler_params=_params(("arbitrary",) * 3 if n_xc or sequential or lhs_norm
                                else ("parallel", "parallel", "arbitrary")),
    )(a, b, *extras, *plan.inputs)


_NO_CARRY = ()


def _mm_nn(name, a, w, *, n_shard=None, epilogue=_ep_store, out_dtypes=(F32,), extras=(), carry=_NO_CARRY,
           lhs_norm=None, tm=1024, tn=1024, tk=1024):
    m, kd = a.shape
    if lhs_norm is not None:
        tk = kd
    if n_shard is None:
        n = w.shape[1]
        tn = _tile(n, tn)
        tk = _tile(kd, tk)
        b_spec = pl.BlockSpec((tk, tn), lambda i, j, k: (k, j))
    elif tn >= 2 * n_shard and tk >= kd:
        n = N_DEV * n_shard
        group = min(tn // n_shard, N_DEV)
        while N_DEV % group:
            group -= 1
        tn, tk = group * n_shard, kd
        b_spec = pl.BlockSpec((group, tk, n_shard), lambda i, j, k: (j, 0, 0))
    else:
        n = N_DEV * n_shard
        tn = _tile(n_shard, tn)
        tk = _tile(kd, tk)
        per = n_shard // tn
        b_spec = pl.BlockSpec((None, tk, tn), lambda i, j, k: (j // per, k, j % per))
    tm = _tile(m, tm)
    o_spec = pl.BlockSpec((tm, tn), lambda i, j, k: (i, j))
    out_shapes = [jax.ShapeDtypeStruct((m, n), d) for d in out_dtypes]
    out_specs = [o_spec] * len(out_dtypes)
    extra_specs = [o_spec] * len(extras)
    if lhs_norm is not None:
        extras = (*extras, lhs_norm)
        extra_specs.append(pl.BlockSpec((1, kd), lambda i, j, k: (0, 0)))
        out_shapes.append(jax.ShapeDtypeStruct((m, kd), BF16))
        out_specs.append(pl.BlockSpec((tm, kd), lambda i, j, k: (i, 0)))
    return _matmul(
        name, a, w, dims=_NN, grid=(m // tm, n // tn, kd // tk),
        a_spec=pl.BlockSpec((tm, tk), lambda i, j, k: (i, k)), b_spec=b_spec,
        acc_shape=(tm, tn), out_shapes=out_shapes, out_specs=out_specs,
        epilogue=epilogue, extras=extras, extra_specs=extra_specs, carry=carry, lhs_norm=lhs_norm is not None)


def _mm_nt(name, a, w, *, epilogue=_ep_store, out_dtypes=(F32,), extras=(), carry=_NO_CARRY,
           tm=1024, tn=1024, tk=1024):
    m, kd = a.shape
    n = w.shape[0]
    tm, tn, tk = _tile(m, tm), _tile(n, tn), _tile(kd, tk)
    o_spec = pl.BlockSpec((tm, tn), lambda i, j, k: (i, j))
    return _matmul(
        name, a, w, dims=_NT, grid=(m // tm, n // tn, kd // tk),
        a_spec=pl.BlockSpec((tm, tk), lambda i, j, k: (i, k)),
        b_spec=pl.BlockSpec((tn, tk), lambda i, j, k: (j, k)),
        acc_shape=(tm, tn),
        out_shapes=[jax.ShapeDtypeStruct((m, n), d) for d in out_dtypes],
        out_specs=[o_spec] * len(out_dtypes),
        epilogue=epilogue, extras=extras, extra_specs=[o_spec] * len(extras), carry=carry)


def _mm_nt_norm_bwd(name, a, w, h, g, dres, *, k_shard=None, carry=_NO_CARRY, tm=512, tk=1024):
    m, kd = a.shape
    n = h.shape[1]
    if k_shard is None:
        tk = _tile(kd, tk)
        b_spec = pl.BlockSpec((n, tk), lambda i, j, k: (0, k))
    else:
        group = max(1, min(tk // k_shard, N_DEV))
        while N_DEV % group:
            group -= 1
        tk = group * k_shard
        b_spec = pl.BlockSpec((group, n, k_shard), lambda i, j, k: (k, 0, 0))
    tm = _tile(m, tm)
    rows = pl.BlockSpec((tm, n), lambda i, j, k: (i, 0))
    vec = pl.BlockSpec((1, n), lambda i, j, k: (0, 0))
    return _matmul(
        name, a, w, dims=_NT, grid=(m // tm, 1, kd // tk),
        a_spec=pl.BlockSpec((tm, tk), lambda i, j, k: (i, k)), b_spec=b_spec, acc_shape=(tm, n),
        out_shapes=[jax.ShapeDtypeStruct((m, n), F32), jax.ShapeDtypeStruct((m, n), BF16),
                    jax.ShapeDtypeStruct((1, n), F32)],
        out_specs=[rows, rows, vec], epilogue=_ep_norm_bwd,
        extras=(h, g, dres), extra_specs=[rows, vec, rows], carry=carry, sequential=True)


TN_TILE_BYTES = 40 * 1024 * 1024


def _mm_tn(name, a, b, *, n_shard=None, carry=_NO_CARRY, tm=1024, tn=1024):
    t, m = a.shape
    n = b.shape[1]
    tm = _tile(m, tm)
    tn = _tile(n if n_shard is None else n_shard, tn)
    tk = t
    while 2 * 2 * tk * (tm + tn) + 4 * tm * tn * 5 > TN_TILE_BYTES and tk % (2 * LANES) == 0:
        tk //= 2
    if n_shard is None:
        o_spec = pl.BlockSpec((tm, tn), lambda i, j, k: (i, j))
        shape = (m, n)
    else:
        per = n_shard // tn
        o_spec = pl.BlockSpec((None, tm, tn), lambda i, j, k: (j // per, i, j % per))
        shape = (N_DEV, m, n_shard)
    return _matmul(
        name, a, b, dims=_TN, grid=(m // tm, n // tn, t // tk),
        a_spec=pl.BlockSpec((tk, tm), lambda i, j, k: (k, i)),
        b_spec=pl.BlockSpec((tk, tn), lambda i, j, k: (k, j)),
        acc_shape=(tm, tn), epilogue=_ep_both, carry=carry,
        out_shapes=[jax.ShapeDtypeStruct(shape, F32), jax.ShapeDtypeStruct(shape, BF16)],
        out_specs=[o_spec, o_spec])


def _ple_proj(p, w_g, tm=1024):
    s, kd = p.shape
    ns = w_g.shape[2]
    tm = _tile(s, tm)

    def body(p_ref, w_ref, o_ref):
        pv = p_ref[...].astype(BF16)
        for j in range(N_DEV):
            o_ref[:, j * ns:(j + 1) * ns] = jnp.dot(pv, w_ref[j], preferred_element_type=F32)

    return _pcall(body, name="ple_proj", grid=(s // tm,),
                  in_specs=[pl.BlockSpec((tm, kd), lambda i: (i, 0)),
                            pl.BlockSpec((N_DEV, kd, ns), lambda i: (0, 0, 0))],
                  out_specs=pl.BlockSpec((tm, N_DEV * ns), lambda i: (i, 0)),
                  out_shape=jax.ShapeDtypeStruct((s, N_DEV * ns), F32),
                  compiler_params=_params(("parallel",)))(p, w_g)


def _d_ple_proj(p, dpp, ns, tk=1024):
    s, kd = p.shape
    tk = _tile(s, tk)
    nk = s // tk

    def body(p_ref, d_ref, of_ref, ob_ref, acc):
        k = pl.program_id(0)

        @pl.when(k == 0)
        def _():
            acc[...] = jnp.zeros_like(acc)

        pv = p_ref[...].astype(BF16)
        for j in range(N_DEV):
            acc[j] += lax.dot_general(pv, d_ref[:, j * ns:(j + 1) * ns], _TN, preferred_element_type=F32)

        @pl.when(k == nk - 1)
        def _():
            of_ref[...] = acc[...]
            ob_ref[...] = acc[...].astype(BF16)

    whole = pl.BlockSpec((N_DEV, kd, ns), lambda k: (0, 0, 0))
    return _pcall(body, name="d_w_ple_proj", grid=(nk,),
                  in_specs=[pl.BlockSpec((tk, kd), lambda k: (k, 0)),
                            pl.BlockSpec((tk, N_DEV * ns), lambda k: (k, 0))],
                  out_specs=[whole, whole],
                  out_shape=[jax.ShapeDtypeStruct((N_DEV, kd, ns), F32), jax.ShapeDtypeStruct((N_DEV, kd, ns), BF16)],
                  scratch_shapes=[pltpu.VMEM((N_DEV, kd, ns), F32)],
                  compiler_params=_params(("arbitrary",)))(p, dpp)


def _ep_ple_loss(gl, h2, pp, tgt, g_final, outs):
    loss_ref, dh3_ref, dgl_ref, dpp_ref, dg_ref = outs
    gate = jax.nn.sigmoid(gl)
    ppv = pp[...]
    h3 = h2[...] + gate * ppv
    r = lax.rsqrt(jnp.mean(h3 * h3, axis=-1, keepdims=True) + EPS)
    hn = h3 * r
    gv = g_final[...]
    diff = hn * gv - tgt[...]
    row = jnp.mean(diff * diff, axis=-1, keepdims=True)
    loss_ref[...] += 0.5 * jnp.sum(row, axis=0, keepdims=True)
    dy = diff * (1.0 / h3.shape[-1])
    dg_ref[...] += jnp.sum(dy * hn, axis=0, keepdims=True)
    dhn = dy * gv
    dh3 = r * (dhn - hn * jnp.mean(dhn * hn, axis=-1, keepdims=True))
    dh3_ref[...] = dh3
    dgl_ref[...] = (dh3 * ppv * gate * (1.0 - gate)).astype(BF16)
    dpp_ref[...] = (dh3 * gate).astype(BF16)


def _ple_gate_loss(h2, g_ple, w_gate, pp, tgt, g_final, tm=512):
    s, d = h2.shape
    tm = _tile(s, tm)
    rows = pl.BlockSpec((tm, d), lambda i, j, k: (i, 0))
    vec = pl.BlockSpec((1, d), lambda i, j, k: (0, 0))
    return _matmul(
        "ple_gate_loss", h2, w_gate, dims=_NN, grid=(s // tm, 1, 1),
        a_spec=rows, b_spec=pl.BlockSpec((d, d), lambda i, j, k: (0, 0)), acc_shape=(tm, d),
        out_shapes=[jax.ShapeDtypeStruct((1, LANES), F32), jax.ShapeDtypeStruct((s, d), F32),
                    jax.ShapeDtypeStruct((s, d), BF16), jax.ShapeDtypeStruct((s, d), BF16),
                    jax.ShapeDtypeStruct((1, d), F32), jax.ShapeDtypeStruct((s, d), BF16)],
        out_specs=[pl.BlockSpec((1, LANES), lambda i, j, k: (0, 0)), rows, rows, rows, vec, rows],
        epilogue=_ep_ple_loss, extras=(h2, pp, tgt, g_final, g_ple), extra_specs=[rows, rows, rows, vec, vec],
        sequential=True, lhs_norm=True)


def _low_half():
    return lax.broadcasted_iota(jnp.int32, (1, LANES), 1) < HEAD_DIM


def _half_mean(v, low):
    s_lo = jnp.sum(jnp.where(low, v, 0.0), axis=-1, keepdims=True)
    s_hi = jnp.sum(jnp.where(low, 0.0, v), axis=-1, keepdims=True)
    return jnp.where(low, s_lo, s_hi) * (1.0 / HEAD_DIM)


def _head_norm_bwd(val, dout, g, low):
    r = lax.rsqrt(_half_mean(val * val, low) + EPS)
    vn = val * r
    dvn = dout * g
    return r * (dvn - vn * _half_mean(dvn * vn, low)), dout * vn


def _conv_taps(vv_ext, w_ref):
    v0 = vv_ext[HALO:]
    v1 = pltpu.roll(vv_ext, 1, 0)[HALO:]
    v2 = pltpu.roll(vv_ext, 2, 0)[HALO:]
    return w_ref[2:3, :] * v0 + w_ref[1:2, :] * v1 + w_ref[0:1, :] * v2, (v0, v1, v2)


def _conv_fwd(proj, conv_w, g_conv, w_conv, d_model, tr=1024):
    s = proj.shape[0]
    tr = _tile(s, tr)
    hb = tr // HALO

    def main(part):
        return pl.BlockSpec((tr, w_conv), lambda i: (i, part))

    def prev(part):
        return pl.BlockSpec((HALO, w_conv), lambda i: (jnp.maximum(i * hb - 1, 0), part))

    def body(cb_ref, cc_ref, cu_ref, ccp_ref, cup_ref, w_ref, g_ref, o_ref):
        i = pl.program_id(0)
        low = _low_half()
        for j in range(w_conv // LANES):
            cols = slice(j * LANES, (j + 1) * LANES)
            vv_prev = jnp.where(i > 0, ccp_ref[:, cols] * cup_ref[:, cols], 0.0)
            vv_ext = jnp.concatenate([vv_prev, cc_ref[:, cols] * cu_ref[:, cols]], axis=0)
            y, _ = _conv_taps(vv_ext, w_ref.at[:, cols])
            co = cb_ref[:, cols] * y
            r = lax.rsqrt(_half_mean(co * co, low) + EPS)
            o_ref[:, cols] = (co * r * g_ref[:, cols]).astype(BF16)

    return _pcall(
        body, name="conv_fwd", grid=(s // tr,),
        in_specs=[main(0), main(1), main(2), prev(1), prev(2),
                  pl.BlockSpec((CONV_K, w_conv), lambda i: (0, 0)),
                  pl.BlockSpec((1, w_conv), lambda i: (0, 0))],
        out_specs=pl.BlockSpec((tr, w_conv), lambda i: (i, 0)),
        out_shape=jax.ShapeDtypeStruct((s, d_model), BF16),
        compiler_params=_params(("parallel",)),
    )(proj, proj, proj, proj, proj, conv_w, g_conv)


def _conv_bwd(proj, dcat, conv_w, g_conv, dproj, w_conv, tr=1024):
    s = proj.shape[0]
    tr = _tile(s, tr)
    hb = tr // HALO
    last = s // HALO - 1
    nt = s // tr

    def main(part):
        return pl.BlockSpec((tr, w_conv), lambda i: (i, part))

    def prev(part):
        return pl.BlockSpec((HALO, w_conv), lambda i: (jnp.maximum(i * hb - 1, 0), part))

    def nxt(part):
        return pl.BlockSpec((HALO, w_conv), lambda i: (jnp.minimum((i + 1) * hb, last), part))

    def body(cb_ref, cc_ref, cu_ref, dc_ref, ccp_ref, cup_ref, cbn_ref, ccn_ref, cun_ref, dcn_ref,
             w_ref, g_ref, dproj_in, dproj_ref, dw_ref, dg_ref):
        del dproj_in
        i = pl.program_id(0)

        @pl.when(i == 0)
        def _():
            dw_ref[...] = jnp.zeros_like(dw_ref)
            dg_ref[...] = jnp.zeros_like(dg_ref)

        low = _low_half()
        n_ext = tr + HALO
        rowid = lax.broadcasted_iota(jnp.int32, (n_ext, 1), 0)
        for j in range(w_conv // LANES):
            cols = slice(j * LANES, (j + 1) * LANES)
            wj = w_ref.at[:, cols]
            cc, cu = cc_ref[:, cols], cu_ref[:, cols]
            vv_prev = jnp.where(i > 0, ccp_ref[:, cols] * cup_ref[:, cols], 0.0)
            vv_ext = jnp.concatenate([vv_prev, cc * cu, ccn_ref[:, cols] * cun_ref[:, cols]], axis=0)
            y_ext, (v0, v1, v2) = _conv_taps(vv_ext, wj)
            cb_ext = jnp.concatenate([cb_ref[:, cols], cbn_ref[:, cols]], axis=0)
            dc_ext = jnp.concatenate([dc_ref[:, cols], dcn_ref[:, cols]], axis=0)
            dco, dgn = _head_norm_bwd(cb_ext * y_ext, dc_ext, g_ref[:, cols], low)
            dyc = jnp.where((rowid < tr) | (i < nt - 1), dco * cb_ext, 0.0)
            dvv = (wj[2:3, :] * dyc[:tr] + wj[1:2, :] * pltpu.roll(dyc, n_ext - 1, 0)[:tr]
                   + wj[0:1, :] * pltpu.roll(dyc, n_ext - 2, 0)[:tr])
            dproj_ref[:, cols] = (dco[:tr] * y_ext[:tr]).astype(BF16)
            dproj_ref[:, w_conv + j * LANES:w_conv + (j + 1) * LANES] = (dvv * cu).astype(BF16)
            dproj_ref[:, 2 * w_conv + j * LANES:2 * w_conv + (j + 1) * LANES] = (dvv * cc).astype(BF16)
            dyt = dyc[:tr]
            for tap, shifted in enumerate((v2, v1, v0)):
                dw_ref[tap:tap + 1, cols] += jnp.sum(dyt * shifted[:tr], axis=0, keepdims=True)
            dg_ref[:, cols] += jnp.sum(dgn[:tr], axis=0, keepdims=True)

    n_cols = dproj.shape[1]
    return _pcall(
        body, name="conv_bwd", grid=(nt,),
        in_specs=[main(0), main(1), main(2), main(0),
                  prev(1), prev(2), nxt(0), nxt(1), nxt(2), nxt(0),
                  pl.BlockSpec((CONV_K, w_conv), lambda i: (0, 0)),
                  pl.BlockSpec((1, w_conv), lambda i: (0, 0)),
                  pl.BlockSpec(memory_space=pl.ANY)],
        out_specs=[pl.BlockSpec((tr, 3 * w_conv), lambda i: (i, 0)),
                   pl.BlockSpec((CONV_K, w_conv), lambda i: (0, 0)),
                   pl.BlockSpec((1, w_conv), lambda i: (0, 0))],
        out_shape=[jax.ShapeDtypeStruct((s, n_cols), BF16),
                   jax.ShapeDtypeStruct((CONV_K, w_conv), F32),
                   jax.ShapeDtypeStruct((1, w_conv), F32)],
        input_output_aliases={12: 0},
        compiler_params=_params(("arbitrary",)),
    )(proj, proj, proj, dcat, proj, proj, proj, proj, proj, dcat, conv_w, g_conv, dproj)


STRIP = 16

ALL_CHAINS = (0, 1, 2, 3)
UPPER_CHAINS = (2, 3)


RUN_FLOOR = -104.0


def _any_weight_left(run_s):
    return (jnp.max(run_s[...]) > RUN_FLOOR).astype(jnp.int32)


def _chains(low):
    return [(2 * half + h, half, msk) for half in range(2)
            for h, msk in enumerate((low, jnp.logical_not(low)))]


def _suffix_operator(t):
    r = lax.broadcasted_iota(jnp.int32, (2 * t, t), 0)
    c = lax.broadcasted_iota(jnp.int32, (2 * t, t), 1)
    return jnp.where((r > c) & ((r < t) | (r - t > c)), 1.0, 0.0).astype(BF16)


def _strips(t, diag):
    return [(i, slice(i * STRIP, (i + 1) * STRIP), t // 2 if diag and (i + 1) * STRIP <= t // 2 else t)
            for i in range(t // STRIP)]


def _strip_mask(i, w):
    r = lax.broadcasted_iota(jnp.int32, (STRIP, w), 0) + i * STRIP
    c = lax.broadcasted_iota(jnp.int32, (STRIP, w), 1)
    return r > c


def _store_trimmed(ref, rows, val, w, t, at=0):
    ref[rows, at:at + w] = val
    if w < t:
        ref[rows, at + w:at + t] = jnp.zeros((STRIP, t - w), val.dtype)


def _store_split(ref, rows, val, w, t):
    hi = val.astype(BF16)
    _store_trimmed(ref, rows, hi, w, t)
    _store_trimmed(ref, rows, (val - hi.astype(F32)).astype(BF16), w, t, at=t)


def _sb_scores(z_s, split_s, zl_s, tot_s, keep_s, t, diag):
    for i, rows, w in _strips(t, diag):
        z = z_s[rows, :w]
        log_beta = jnp.minimum(z, 0.0) - jnp.log(1.0 + jnp.exp(-jnp.abs(z)))
        log_keep = log_beta - z
        if diag:
            log_keep = jnp.where(_strip_mask(i, w), log_keep, 0.0)
        _store_split(split_s, rows, log_keep, w, t)
        zl_s[rows, :w] = log_beta
        tot_s[rows, :] = _row_sum(log_keep)
        if keep_s is not None:
            keep_s[rows, :w] = jnp.exp(log_keep)


def _row_sum(v):
    return jnp.broadcast_to(jnp.sum(v, axis=-1, keepdims=True), (v.shape[0], LANES))


def _wide(r, t):
    return jnp.concatenate([r] * (t // LANES), axis=1)


def _sb_weights(zl_s, suf_s, run_s, tot_s, a_s, t, diag, da_s=None, glog_s=None, gsplit_s=None, gtot_s=None):
    for i, rows, w in _strips(t, diag):
        run = run_s[rows, :]
        a = jnp.exp(zl_s[rows, :w] + suf_s[rows, :w] + _wide(run, w))
        if diag:
            a = jnp.where(_strip_mask(i, w), a, 0.0)
        ab = a.astype(BF16)
        _store_trimmed(a_s, rows, ab, w, t)
        run_s[rows, :] = run + tot_s[rows, :]
        if da_s is not None:
            glog = ab.astype(F32) * da_s[rows, :w]
            glog_s[rows, :w] = glog
            _store_split(gsplit_s, rows, glog, w, t)
            gtot_s[rows, :] = _row_sum(glog)


def _sb_dscores(glog_s, cum_s, rest_s, gtot_s, keep_s, dz_s, t, diag):
    for i, rows, w in _strips(t, diag):
        glog = glog_s[rows, :w]
        rest = rest_s[rows, :]
        from_here = _wide(rest, w) - cum_s[rows, :w]
        before = from_here - glog
        dz = from_here * keep_s[rows, :w] - before
        if diag:
            dz = jnp.where(_strip_mask(i, w), dz, 0.0)
        _store_trimmed(dz_s, rows, dz.astype(BF16), w, t)
        rest_s[rows, :] = rest - gtot_s[rows, :]


def _attn_fwd(proj, g_attn, cat, w_conv, carry, t=ATTN_BLOCK):
    s = proj.shape[0]
    w_attn = g_attn.shape[1]
    nh = w_attn // LANES
    t = _tile(s, t)
    tq = 2 * t
    nq = s // tq
    q0 = 3 * w_conv // LANES
    scale = HEAD_DIM ** -0.5
    plan = _Carried(carry)
    nw, n_res = len(plan.inputs), len(plan.out_shapes)

    def body(q_ref, k_ref, v_ref, g_ref, cat_in, *rest):
        staged_refs, rest = rest[:nw], rest[nw:]
        o_ref, cat_ref = rest[:2]
        gathered_refs, rest = rest[2:2 + n_res], rest[2 + n_res:]
        kb, vb, tri_s, qm_s, z_s, split_s, zl_s, suf_s, a_s, run_s, tot_s, acc_s = rest[:12]
        gather_sems = rest[12:]
        del cat_in
        qi = pl.program_id(1)

        @pl.when((pl.program_id(0) == 0) & (qi == 0))
        def _():
            for cp in plan.copies(staged_refs, gathered_refs, gather_sems):
                cp.start()

        @pl.when(qi == 0)
        def _():
            kb[...] = k_ref[...].astype(BF16)
            vb[...] = v_ref[...].astype(BF16)
            tri_s[...] = _suffix_operator(t)

        low = _low_half()
        for c, half, msk in _chains(low):
            qm_s[c] = jnp.where(msk, q_ref[half * t:(half + 1) * t, :] * scale, 0.0).astype(BF16)
            run_s[c] = jnp.zeros((t, LANES), F32)
            acc_s[c] = jnp.zeros((t, LANES), F32)

        def key_rows(kblk):
            return pl.ds(pl.multiple_of(kblk * t, t), t)

        def key_block(base, c):
            return key_rows(jnp.maximum(base + c // 2, 0))

        def scores_matmul(base, chains):
            for c in chains:
                z_s[c] = lax.dot_general(qm_s[c], kb[key_block(base, c), :], _NT, preferred_element_type=F32)

        def front(modes, base, prev=None):
            for c, diag in modes:
                _sb_scores(z_s.at[c], split_s.at[c], zl_s.at[c], tot_s.at[c], None, t, diag)
                suf_s[c] = jnp.dot(split_s[c], tri_s[...], preferred_element_type=F32)
            if prev is not None:
                tail(*prev)
            scores_matmul(base - 1, ALL_CHAINS)
            for c, diag in modes:
                _sb_weights(zl_s.at[c], suf_s.at[c], run_s.at[c], tot_s.at[c], a_s.at[c], t, diag)

        def tail(base, chains):
            for c in chains:
                acc_s[c] += jnp.dot(a_s[c], vb[key_block(base, c), :], preferred_element_type=F32)

        first = 2 * qi
        scores_matmul(first, ALL_CHAINS)
        front([(c, True) for c in ALL_CHAINS], first)

        def loop(state):
            it = state[0]
            base = first - 1 - it
            front([(c, False) for c in ALL_CHAINS], base, prev=(base + 1, ALL_CHAINS))
            return it + 1, _any_weight_left(run_s)

        done, live = lax.while_loop(lambda state: (state[0] < first) & (state[1] > 0), loop,
                                    (jnp.int32(0), jnp.int32(1)))
        one_more = (done == first) & (live > 0)

        @pl.when(one_more)
        def _():
            front([(c, False) for c in UPPER_CHAINS], -1, prev=(0, ALL_CHAINS))
            tail(-1, UPPER_CHAINS)

        @pl.when(jnp.logical_not(one_more))
        def _():
            tail(first - done, ALL_CHAINS)

        for half in range(2):
            rows = slice(half * t, (half + 1) * t)
            o = jnp.where(low, acc_s[2 * half], acc_s[2 * half + 1])
            o_ref[rows, :] = o
            r = lax.rsqrt(_half_mean(o * o, low) + EPS)
            cat_ref[rows, :] = (o * r * g_ref[...]).astype(BF16)

        @pl.when((pl.program_id(0) == nh - 1) & (qi == nq - 1))
        def _():
            for cp in plan.copies(staged_refs, gathered_refs, gather_sems):
                cp.wait()

    whole = lambda col0: pl.BlockSpec((s, LANES), lambda h, i: (0, col0 + h))
    n_ch = len(ALL_CHAINS)
    res = _pcall(
        body, name="attn_fwd", grid=(nh, nq),
        in_specs=[pl.BlockSpec((tq, LANES), lambda h, i: (i, q0 + h)),
                  whole(q0 + nh), whole(q0 + 2 * nh),
                  pl.BlockSpec((1, LANES), lambda h, i: (0, h)),
                  pl.BlockSpec(memory_space=pl.ANY)] + [pl.BlockSpec(memory_space=pl.ANY)] * nw,
        out_specs=[pl.BlockSpec((tq, LANES), lambda h, i: (i, h)),
                   pl.BlockSpec((tq, LANES), lambda h, i: (i, w_conv // LANES + h))]
        + [pl.BlockSpec(memory_space=pl.ANY)] * n_res,
        out_shape=[jax.ShapeDtypeStruct((s, w_attn), F32),
                   jax.ShapeDtypeStruct(cat.shape, BF16)] + plan.out_shapes,
        scratch_shapes=[pltpu.VMEM((s, LANES), BF16), pltpu.VMEM((s, LANES), BF16),
                        pltpu.VMEM((2 * t, t), BF16),
                        pltpu.VMEM((n_ch, t, LANES), BF16),
                        pltpu.VMEM((n_ch, t, t), F32),
                        pltpu.VMEM((n_ch, t, 2 * t), BF16),
                        pltpu.VMEM((n_ch, t, t), F32),
                        pltpu.VMEM((n_ch, t, t), F32),
                        pltpu.VMEM((n_ch, t, t), BF16),
                        pltpu.VMEM((n_ch, t, LANES), F32),
                        pltpu.VMEM((n_ch, t, LANES), F32),
                        pltpu.VMEM((n_ch, t, LANES), F32)]
        + plan.sems,
        input_output_aliases={4: 1, **plan.aliases(5, 2)},
        compiler_params=_params(("arbitrary", "arbitrary")),
    )(proj, proj, proj, g_attn, cat, *plan.inputs)
    return res[0], res[1], res[2:]


def _attn_bwd(proj, o, dcat, g_attn, w_conv, carry, t=ATTN_BLOCK):
    s, n_cols = proj.shape
    w_attn = g_attn.shape[1]
    nh = w_attn // LANES
    t = _tile(s, t)
    tq = 2 * t
    nq = s // tq
    q0 = 3 * w_conv // LANES
    scale = HEAD_DIM ** -0.5
    plan = _Carried(carry)
    nw, n_res = len(plan.inputs), len(plan.out_shapes)

    def body(q_ref, k_ref, v_ref, o_ref, do_ref, g_ref, *rest):
        partial_refs, rest = rest[:nw], rest[nw:]
        dproj_ref, dg_ref = rest[:2]
        received_refs, rest = rest[2:2 + n_res], rest[2 + n_res:]
        (kb, vb, dkt_acc, dvt_acc, stash, tri_s, qm_s, dom_s, qt_s, dot_s, z_s, da_s, split_s, zl_s,
         keep_s, suf_s, a_s, glog_s, gsplit_s, cum_s, dz_s, run_s, tot_s, rest_s, gtot_s, dq_s) = rest[:26]
        out_sems, scatter_sems = rest[26], rest[27:]
        step_i = pl.program_id(1)
        qi = nq - 1 - step_i
        head_pair = pl.program_id(0)
        first_step = (head_pair == 0) & (step_i == 0)
        last_step = (head_pair == nh - 1) & (step_i == nq - 1)

        @pl.when(first_step)
        def _():
            for cp in plan.copies(partial_refs, received_refs, scatter_sems):
                cp.start()

        def out_copies():
            rows = pl.ds(pl.multiple_of(qi * tq, tq), tq)
            return [pltpu.make_async_copy(
                stash.at[w], dproj_ref.at[rows, pl.ds(pl.multiple_of((q0 + w * nh + head_pair) * LANES, LANES), LANES)],
                out_sems.at[w]) for w in range(3)]

        def walk():
            @pl.when(step_i == 0)
            def _():
                kb[...] = k_ref[...].astype(BF16)
                vb[...] = v_ref[...].astype(BF16)
                tri_s[...] = _suffix_operator(t)
                dkt_acc[...] = jnp.zeros_like(dkt_acc)
                dvt_acc[...] = jnp.zeros_like(dvt_acc)
                dg_ref[...] = jnp.zeros_like(dg_ref)

            low = _low_half()
            gv = g_ref[...]
            for half in range(2):
                rows = slice(half * t, (half + 1) * t)
                q = q_ref[rows, :] * scale
                ov = o_ref[rows, :]
                d_o, dgn = _head_norm_bwd(ov, do_ref[rows, :], gv, low)
                dg_ref[...] += jnp.sum(dgn, axis=0, keepdims=True)
                for h, msk in enumerate((low, jnp.logical_not(low))):
                    c = 2 * half + h
                    qh = jnp.where(msk, q, 0.0)
                    doh = jnp.where(msk, d_o, 0.0)
                    dom = doh.astype(BF16)
                    qm_s[c] = qh.astype(BF16)
                    dom_s[c] = dom
                    qt_s[c] = qh.T.astype(BF16)
                    dot_s[c] = doh.T.astype(BF16)
                    rest_s[c] = _row_sum(dom.astype(F32) * ov)
                    run_s[c] = jnp.zeros((t, LANES), F32)
                    dq_s[c] = jnp.zeros((t, LANES), F32)

            def key_rows(kblk):
                return pl.ds(pl.multiple_of(kblk * t, t), t)

            def block_of(base, half):
                return jnp.maximum(base + half, 0)

            def scores_matmul(base, chains):
                for c in chains:
                    ks = kb[key_rows(block_of(base, c // 2)), :]
                    z_s[c] = lax.dot_general(qm_s[c], ks, _NT, preferred_element_type=F32)

            def da_matmul(base, chains):
                for c in chains:
                    vs = vb[key_rows(block_of(base, c // 2)), :]
                    da_s[c] = lax.dot_general(dom_s[c], vs, _NT, preferred_element_type=F32)

            def front(modes, base, prev=None):
                if prev is not None:
                    tail(*prev)
                for c, diag in modes:
                    _sb_scores(z_s.at[c], split_s.at[c], zl_s.at[c], tot_s.at[c], keep_s.at[c], t, diag)
                    suf_s[c] = jnp.dot(split_s[c], tri_s[...], preferred_element_type=F32)
                scores_matmul(base - 1, ALL_CHAINS)
                for c, diag in modes:
                    _sb_weights(zl_s.at[c], suf_s.at[c], run_s.at[c], tot_s.at[c], a_s.at[c], t, diag,
                                da_s.at[c], glog_s.at[c], gsplit_s.at[c], gtot_s.at[c])
                    cum_s[c] = jnp.dot(gsplit_s[c], tri_s[...], preferred_element_type=F32)
                da_matmul(base - 1, ALL_CHAINS)
                for c, diag in modes:
                    _sb_dscores(glog_s.at[c], cum_s.at[c], rest_s.at[c], gtot_s.at[c], keep_s.at[c],
                                dz_s.at[c], t, diag)

            def tail(base, chains):
                for half in range(2):
                    mine = [c for c in chains if c // 2 == half]
                    if not mine:
                        continue
                    kblk = block_of(base, half)
                    ks = kb[key_rows(kblk), :]
                    dkt = dkt_acc[kblk]
                    dvt = dvt_acc[kblk]
                    for c in mine:
                        dq_s[c] += jnp.dot(dz_s[c], ks, preferred_element_type=F32)
                        dkt = dkt + jnp.dot(qt_s[c], dz_s[c], preferred_element_type=F32)
                        dvt = dvt + jnp.dot(dot_s[c], a_s[c], preferred_element_type=F32)
                    dkt_acc[kblk] = dkt
                    dvt_acc[kblk] = dvt

            first = 2 * qi
            scores_matmul(first, ALL_CHAINS)
            da_matmul(first, ALL_CHAINS)
            front([(c, True) for c in ALL_CHAINS], first)

            def loop(state):
                it = state[0]
                base = first - 1 - it
                front([(c, False) for c in ALL_CHAINS], base, prev=(base + 1, ALL_CHAINS))
                return it + 1, _any_weight_left(run_s)

            done, live = lax.while_loop(lambda state: (state[0] < first) & (state[1] > 0), loop,
                                        (jnp.int32(0), jnp.int32(1)))
            one_more = (done == first) & (live > 0)

            @pl.when(one_more)
            def _():
                front([(c, False) for c in UPPER_CHAINS], -1, prev=(0, ALL_CHAINS))
                tail(-1, UPPER_CHAINS)

            @pl.when(jnp.logical_not(one_more))
            def _():
                tail(first - done, ALL_CHAINS)

            @pl.when(jnp.logical_not(first_step))
            def _():
                for cp in out_copies():
                    cp.wait()

            for half in range(2):
                rows = slice(half * t, (half + 1) * t)
                stash[0, rows, :] = (jnp.where(low, dq_s[2 * half], dq_s[2 * half + 1]) * scale).astype(BF16)
                stash[1, rows, :] = dkt_acc[2 * qi + half].T.astype(BF16)
                stash[2, rows, :] = dvt_acc[2 * qi + half].T.astype(BF16)
            for cp in out_copies():
                cp.start()

        walk()

        @pl.when(last_step)
        def _():
            for cp in out_copies():
                cp.wait()
            for cp in plan.copies(partial_refs, received_refs, scatter_sems):
                cp.wait()

    whole = lambda col0: pl.BlockSpec((s, LANES), lambda h, i: (0, col0 + h))
    blk = lambda col0: pl.BlockSpec((tq, LANES), lambda h, i: (nq - 1 - i, col0 + h))
    n_ch = len(ALL_CHAINS)
    res = _pcall(
        body, name="attn_bwd", grid=(nh, nq),
        in_specs=[blk(q0), whole(q0 + nh), whole(q0 + 2 * nh), blk(0), blk(w_conv // LANES),
                  pl.BlockSpec((1, LANES), lambda h, i: (0, h))] + [pl.BlockSpec(memory_space=pl.ANY)] * nw,
        out_specs=[pl.BlockSpec(memory_space=pl.ANY),
                   pl.BlockSpec((1, LANES), lambda h, i: (0, h))] + [pl.BlockSpec(memory_space=pl.ANY)] * n_res,
        out_shape=[jax.ShapeDtypeStruct((s, n_cols), BF16), jax.ShapeDtypeStruct((1, w_attn), F32)]
        + plan.out_shapes,
        scratch_shapes=[pltpu.VMEM((s, LANES), BF16), pltpu.VMEM((s, LANES), BF16),
                        pltpu.VMEM((s // t, LANES, t), F32),
                        pltpu.VMEM((s // t, LANES, t), F32),
                        pltpu.VMEM((3, tq, LANES), BF16),
                        pltpu.VMEM((2 * t, t), BF16),
                        pltpu.VMEM((n_ch, t, LANES), BF16),
                        pltpu.VMEM((n_ch, t, LANES), BF16),
                        pltpu.VMEM((n_ch, LANES, t), BF16),
                        pltpu.VMEM((n_ch, LANES, t), BF16),
                        pltpu.VMEM((n_ch, t, t), F32),
                        pltpu.VMEM((n_ch, t, t), F32),
                        pltpu.VMEM((n_ch, t, 2 * t), BF16),
                        pltpu.VMEM((n_ch, t, t), F32),
                        pltpu.VMEM((n_ch, t, t), F32),
                        pltpu.VMEM((n_ch, t, t), F32),
                        pltpu.VMEM((n_ch, t, t), BF16),
                        pltpu.VMEM((n_ch, t, t), F32),
                        pltpu.VMEM((n_ch, t, 2 * t), BF16),
                        pltpu.VMEM((n_ch, t, t), F32),
                        pltpu.VMEM((n_ch, t, t), BF16),
                        pltpu.VMEM((n_ch, t, LANES), F32),
                        pltpu.VMEM((n_ch, t, LANES), F32),
                        pltpu.VMEM((n_ch, t, LANES), F32),
                        pltpu.VMEM((n_ch, t, LANES), F32),
                        pltpu.VMEM((n_ch, t, LANES), F32),
                        pltpu.SemaphoreType.DMA((3,))]
        + plan.sems,
        input_output_aliases=plan.aliases(6, 2),
        compiler_params=_params(("arbitrary", "arbitrary")),
    )(proj, proj, proj, o, dcat, g_attn, *plan.inputs)
    return res[0], res[1], res[2:]


def _place():
    return lax.axis_index("x"), lax.axis_index("y"), lax.axis_index("c")


def _other_chips(x, y):
    return [(1 - x, y), (x, 1 - y), (1 - x, 1 - y)]


def _slot(px, py, pc):
    return 4 * px + 2 * py + pc


def _all_gather(shards, out_dtypes):
    nw = len(shards)

    def body(*refs):
        ins, outs, stage = refs[:nw], refs[nw:2 * nw], refs[2 * nw:3 * nw]
        send_sems, recv_sems, local_sems = refs[3 * nw:]
        x, y, c = _place()
        me, sibling = (x, y, c), (x, y, 1 - c)
        chips = _other_chips(x, y)

        def copy(w, k, block, to, src=None):
            dst = outs[w].at[_slot(*block)]
            return pltpu.make_async_remote_copy(
                src_ref=dst if src is None else src, dst_ref=dst,
                send_sem=send_sems.at[w * 7 + k], recv_sem=recv_sems.at[w * 7 + k],
                device_id=to, device_id_type=MESH)

        started = []
        local = []
        for w in range(nw):
            stage[w][...] = ins[w][...].astype(stage[w].dtype)
            cp = pltpu.make_async_copy(stage[w], outs[w].at[_slot(*me)], local_sems.at[w])
            cp.start()
            local.append(cp)
            started.append(copy(w, 0, me, sibling, src=stage[w]))
            started[-1].start()
            for j, chip in enumerate(chips):
                started.append(copy(w, 1 + j, me, (*chip, c), src=stage[w]))
                started[-1].start()
        for j, chip in enumerate(chips):
            for w in range(nw):
                copy(w, 1 + j, (*chip, c), me).wait_recv()
                started.append(copy(w, 4 + j, (*chip, c), sibling))
                started[-1].start()
        for w in range(nw):
            copy(w, 0, sibling, me).wait_recv()
            for j, chip in enumerate(chips):
                copy(w, 4 + j, (*chip, 1 - c), me).wait_recv()
        for cp in started:
            cp.wait_send()
        for cp in local:
            cp.wait()

    return _pcall(
        body, name="all_gather_weights",
        in_specs=[pl.BlockSpec(memory_space=pltpu.VMEM)] * nw,
        out_specs=[pl.BlockSpec(memory_space=pl.ANY)] * nw,
        out_shape=[jax.ShapeDtypeStruct((N_DEV, *a.shape), d) for a, d in zip(shards, out_dtypes)],
        scratch_shapes=[pltpu.VMEM(a.shape, d) for a, d in zip(shards, out_dtypes)]
        + [pltpu.SemaphoreType.DMA((7 * nw,)), pltpu.SemaphoreType.DMA((7 * nw,)),
           pltpu.SemaphoreType.DMA((nw,))],
        compiler_params=_params(),
    )(*shards)


N_PEERS = N_DEV - 1


def _peer(k):
    x, y, c = _place()
    return (x ^ (k >> 2), y ^ ((k >> 1) & 1), c ^ (k & 1))


def _remote(src, dst, sems, index, to):
    return pltpu.make_async_remote_copy(src_ref=src, dst_ref=dst, send_sem=sems[0].at[index],
                                        recv_sem=sems[1].at[index], device_id=to, device_id_type=MESH)


def _gather_out_copies(staged, gathered, sems):
    x, y, c = _place()
    me = _slot(x, y, c)
    targets = [(x, y, 1 - c)] + [(*chip, c) for chip in _other_chips(x, y)]
    copies = []
    for w, (src, dst) in enumerate(zip(staged, gathered)):
        copies.append(pltpu.make_async_copy(src, dst.at[me], sems[2].at[w]))
        copies += [_remote(src, dst.at[me], sems, w * len(targets) + k, to) for k, to in enumerate(targets)]
    return copies


def _gather_pass_copies(arrived, gathered, sems):
    x, y, c = _place()
    chips = _other_chips(x, y)
    return [_remote(src.at[_slot(*chip, c)], dst.at[_slot(*chip, c)], sems, w * len(chips) + j, (x, y, 1 - c))
            for w, (src, dst) in enumerate(zip(arrived, gathered)) for j, chip in enumerate(chips)]


ALL_PEERS = tuple(range(1, N_DEV))


def _scatter_copies(partials, received, sems, peers=ALL_PEERS):
    me = _slot(*_place())
    return [_remote(src.at[me ^ k], dst.at[k - 1], sems, w * N_PEERS + k - 1, _peer(k))
            for w, (src, dst) in enumerate(zip(partials, received)) for k in peers]


class _Carried:
    def __init__(self, jobs):
        self.jobs = [(job[0], list(job[1]), job[2] if len(job) > 2 else ALL_PEERS) for job in jobs if len(job[1])]
        self.inputs, self.out_shapes, self.sems, self.counts = [], [], [], []
        for kind, arrays, _ in self.jobs:
            n_out = len(arrays) // 2 if kind == "scatter_more" else len(arrays)
            fan = {"gather_out": 4, "gather_pass": 3}.get(kind, N_PEERS)
            for a in arrays[len(arrays) - n_out:]:
                shape = {"gather_out": (N_DEV, *a.shape), "scatter": (N_PEERS, *a.shape[1:])}.get(kind, a.shape)
                self.out_shapes.append(jax.ShapeDtypeStruct(shape, BF16))
            job_sems = [pltpu.SemaphoreType.DMA((fan * n_out,))] * 2
            job_sems += [pltpu.SemaphoreType.DMA((n_out,))] if kind == "gather_out" else []
            self.inputs += arrays
            self.sems += job_sems
            self.counts.append((len(arrays), n_out, len(job_sems)))

    def aliases(self, first_input, first_output):
        pairs, at_in, at_out = {}, first_input, first_output
        for (kind, _, _), (n_in, n_out, _) in zip(self.jobs, self.counts):
            if kind in ("gather_pass", "scatter_more"):
                pairs.update({at_in + n_in - n_out + i: at_out + i for i in range(n_out)})
            at_in, at_out = at_in + n_in, at_out + n_out
        return pairs

    def copies(self, in_refs, out_refs, sem_refs):
        out, at_in, at_out, at_sem = [], 0, 0, 0
        for (kind, _, peers), (n_in, n_out, n_sems) in zip(self.jobs, self.counts):
            srcs, dsts = in_refs[at_in:at_in + n_out], out_refs[at_out:at_out + n_out]
            sems = sem_refs[at_sem:at_sem + n_sems]
            if kind == "gather_out":
                out += _gather_out_copies(srcs, dsts, sems)
            elif kind == "gather_pass":
                out += _gather_pass_copies(srcs, dsts, sems)
            else:
                out += _scatter_copies(srcs, dsts, sems, peers)
            at_in, at_out, at_sem = at_in + n_in, at_out + n_out, at_sem + n_sems
        return out


def _cast_shards(shards):
    def body(*refs):
        for src, dst in zip(refs[:len(shards)], refs[len(shards):]):
            dst[...] = src[...].astype(BF16)

    return _pcall(
        body, name="cast_shards",
        in_specs=[pl.BlockSpec(memory_space=pltpu.VMEM)] * len(shards),
        out_specs=[pl.BlockSpec(memory_space=pltpu.VMEM)] * len(shards),
        out_shape=[jax.ShapeDtypeStruct(a.shape, BF16) for a in shards],
        compiler_params=_params(),
    )(*shards)


def _all_reduce_small(packed):
    r = packed.shape[0]

    def body(x_ref, o_ref, gathered, send_sems, recv_sems):
        x, y, c = _place()
        me = _slot(x, y, c)
        gathered[me] = x_ref[...]
        copies = [_remote(x_ref, gathered.at[me], (send_sems, recv_sems), k - 1, _peer(k)) for k in ALL_PEERS]
        for cp in copies:
            cp.start()
        for cp in copies:
            cp.wait()
        total = gathered[0]
        for k in range(1, N_DEV):
            total = total + gathered[k]
        o_ref[...] = total

    return _pcall(
        body, name="all_reduce_small",
        in_specs=[pl.BlockSpec(memory_space=pltpu.VMEM)],
        out_specs=pl.BlockSpec(memory_space=pltpu.VMEM),
        out_shape=jax.ShapeDtypeStruct(packed.shape, F32),
        scratch_shapes=[pltpu.VMEM((N_DEV, r, LANES), F32),
                        pltpu.SemaphoreType.DMA((N_DEV - 1,)), pltpu.SemaphoreType.DMA((N_DEV - 1,))],
        compiler_params=_params(),
    )(packed)


def _adam_math(w, g, m, v):
    m = ADAM_B1 * m + (1.0 - ADAM_B1) * g
    v = ADAM_B2 * v + (1.0 - ADAM_B2) * jnp.square(g)
    m_hat = m / (1.0 - ADAM_B1 ** ADAM_STEP)
    v_hat = v / (1.0 - ADAM_B2 ** ADAM_STEP)
    delta = -ADAM_LR * (m_hat / (jnp.sqrt(v_hat) + ADAM_EPS) + ADAM_WD * w)
    return delta, m, v


ADAM_TILE_BYTES = 24 * 1024 * 1024


def _adam_sharded(name, own, received, w, m, v, place):
    r, cdim = w.shape
    row_bytes = 2 * cdim * (4 + 2 * N_PEERS + 3 * 4 + 4 * 4)
    tr = _tile(r, max(LANES, ADAM_TILE_BYTES // row_bytes // LANES * LANES)) if r % LANES == 0 else r

    def body(place_ref, own_ref, rec_ref, w_ref, m_ref, v_ref, g_ref, d_ref, nm_ref, nv_ref):
        del place_ref
        g = own_ref[...]
        for j in range(N_PEERS):
            g = g + rec_ref[j].astype(F32)
        delta, nm, nv = _adam_math(w_ref[...], g, m_ref[...], v_ref[...])
        g_ref[...] = g
        d_ref[...] = delta
        nm_ref[...] = nm
        nv_ref[...] = nv

    blk = pl.BlockSpec((tr, cdim), lambda i, pr: (i, 0))
    grid_spec = pltpu.PrefetchScalarGridSpec(
        num_scalar_prefetch=1, grid=(r // tr,),
        in_specs=[pl.BlockSpec((None, tr, cdim), lambda i, pr: (4 * pr[0] + 2 * pr[1] + pr[2], i, 0)),
                  pl.BlockSpec((N_PEERS, tr, cdim), lambda i, pr: (0, i, 0)), blk, blk, blk],
        out_specs=[blk] * 4)
    return _pcall(body, name=name, grid_spec=grid_spec,
                  out_shape=[jax.ShapeDtypeStruct((r, cdim), F32)] * 4,
                  compiler_params=_params(("parallel",)))(place, own, received, w, m, v)


def _adam_small(w, g, m, v):
    def body(w_ref, g_ref, m_ref, v_ref, d_ref, nm_ref, nv_ref):
        delta, nm, nv = _adam_math(w_ref[...], g_ref[...], m_ref[...], v_ref[...])
        d_ref[...] = delta
        nm_ref[...] = nm
        nv_ref[...] = nv

    return _pcall(body, name="adam_small",
                  in_specs=[pl.BlockSpec(memory_space=pltpu.VMEM)] * 4,
                  out_specs=[pl.BlockSpec(memory_space=pltpu.VMEM)] * 3,
                  out_shape=[jax.ShapeDtypeStruct(w.shape, F32)] * 3,
                  compiler_params=_params())(w, g, m, v)


def _rows(vec):
    return vec.reshape(-1, LANES)


def kernel(x, p, g_mix, w_in, conv_w, g_conv_out, g_attn_out, w_out, g_mlp, w_up, w_down, g_ple, w_ple_gate, w_ple_proj, g_final, loss_target, m_g_mix, m_w_in, m_conv_w, m_g_conv_out, m_g_attn_out, m_w_out, m_g_mlp, m_w_up, m_w_down, m_g_ple, m_w_ple_gate, m_w_ple_proj, m_g_final, v_g_mix, v_w_in, v_conv_w, v_g_conv_out, v_g_attn_out, v_w_out, v_g_mlp, v_w_up, v_w_down, v_g_ple, v_w_ple_gate, v_w_ple_proj, v_g_final):
    s, d = x.shape[1], x.shape[2]
    w_conv = g_conv_out.shape[1]
    w_attn = g_attn_out.shape[1]
    cw = conv_w.shape[2]
    xs, ps, tgt = x[0], p[0, 0], loss_target[0]
    place = jnp.stack([lax.axis_index("x"), lax.axis_index("y"), lax.axis_index("c")]).astype(jnp.int32)
    my_slot = 4 * place[0] + 2 * place[1] + place[2]

    conv_tile = jnp.pad(conv_w[0], ((0, HALO - CONV_K), (0, LANES - cw)))
    big = [w_in[0], w_out[0], w_up[0], w_down[0], w_ple_gate[0], w_ple_proj[0]]
    win_g, conv_g = _all_gather([big[0], conv_tile], [BF16, F32])
    s_out, s_up, s_down, s_gate, s_proj = _cast_shards(big[1:])
    conv_full = jnp.transpose(conv_g[:, :CONV_K, :cw], (1, 0, 2)).reshape(CONV_K, w_conv)
    in_shard, up_shard, proj_shard = big[0].shape[1], big[2].shape[1], big[5].shape[1]

    proj, a, g_out, g_gate, g_proj = _mm_nn("in_proj", xs, win_g, n_shard=in_shard, tn=2 * in_shard, tm=2048,
                                            lhs_norm=g_mix, carry=[("gather_out", [s_out, s_gate, s_proj])])
    cat = _conv_fwd(proj, conv_full, g_conv_out, w_conv, d)
    o, cat, (g_up, g_down, wout_g, wgate_g, wproj_g) = _attn_fwd(
        proj, g_attn_out, cat, w_conv,
        [("gather_out", [s_up, s_down]), ("gather_pass", [g_out, g_gate, g_proj])])
    wout_f = wout_g.reshape(-1, wout_g.shape[-1])
    wgate_f = wgate_g.reshape(-1, wgate_g.shape[-1])
    h1, wup_g = _mm_nn("out_proj", cat, wout_f, epilogue=_ep_residual, extras=(xs,),
                       carry=[("gather_pass", [g_up])])
    act, mn, wdown_g = _mm_nn("mlp_up", h1, wup_g, n_shard=up_shard, epilogue=_ep_up, out_dtypes=(BF16,), tm=2048,
                              lhs_norm=g_mlp, carry=[("gather_pass", [g_down])])
    wdown_f = wdown_g.reshape(-1, wdown_g.shape[-1])
    h2, = _mm_nn("mlp_down", act, wdown_f, epilogue=_ep_residual, extras=(h1,))
    pp = _ple_proj(ps, wproj_g)
    loss_part, dh3, dgl, dpp, dg_final, n3 = _ple_gate_loss(h2, g_ple, wgate_f, pp, tgt, g_final.reshape(1, d))

    def slots(t2d):
        return t2d.reshape(N_DEV, -1, t2d.shape[-1])

    dw_proj = _d_ple_proj(ps, dpp, proj_shard)
    dw_gate = [slots(t) for t in _mm_tn("d_w_ple_gate", n3, dgl)]
    dh2, dh2b, dg_ple = _mm_nt_norm_bwd("d_norm_ple", dgl, wgate_f, h2, g_ple, dh3)
    du, gate_recv, proj_recv = _mm_nt("d_mlp_act", dh2b, wdown_f, epilogue=_ep_dact, out_dtypes=(BF16,),
                                      extras=(act,), tm=2048, carry=[("scatter", [dw_gate[1], dw_proj[1]])])
    dw_down = [slots(t) for t in _mm_tn("d_w_down", act, dh2b)]
    near, far = (1, 2, 3, 4, 5), (6, 7)
    dw_up = _mm_tn("d_w_up", mn, du, n_shard=up_shard)
    dh1, dh1b, dg_mlp, down_part = _mm_nt_norm_bwd(
        "d_norm_mlp", du, wup_g, h1, g_mlp, dh2, k_shard=up_shard, tm=1024,
        carry=[("scatter", [dw_down[1]], near)])
    dcat, = _mm_nt("d_cat", dh1b, wout_f)
    dw_out = [slots(t) for t in _mm_tn("d_w_out", cat, dh1b)]
    dproj, dg_attn, (down_recv, up_recv) = _attn_bwd(
        proj, o, dcat, g_attn_out, w_conv,
        [("scatter_more", [dw_down[1], down_part], far), ("scatter", [dw_up[1]])])
    dproj, dconv, dg_conv = _conv_bwd(proj, dcat, conv_full, g_conv_out, dproj, w_conv)
    *dw_in, out_recv = _mm_tn("d_w_in", a, dproj, n_shard=in_shard, tn=in_shard,
                              carry=[("scatter", [dw_out[1]])])
    grad_x, _, dg_mix, in_recv = _mm_nt_norm_bwd("d_norm_mix", dproj, win_g, xs, g_mix, dh1, k_shard=in_shard,
                                                 tk=2 * in_shard, tm=1024, carry=[("scatter", [dw_in[1]])])

    names = ["w_in", "w_out", "w_up", "w_down", "w_ple_gate", "w_ple_proj"]
    owns = [dw_in[0], dw_out[0], dw_up[0], dw_down[0], dw_gate[0], dw_proj[0]]
    recvs = [in_recv, out_recv, up_recv, down_recv, gate_recv, proj_recv]
    moments = [(m_w_in, v_w_in), (m_w_out, v_w_out), (m_w_up, v_w_up), (m_w_down, v_w_down),
               (m_w_ple_gate, v_w_ple_gate), (m_w_ple_proj, v_w_ple_proj)]
    big_out = {}
    for n, own, rc, wt, (mm, vv) in zip(names, owns, recvs, big, moments):
        big_out[n] = [t[None] for t in _adam_sharded("adam_" + n, own, rc, wt, mm[0], vv[0], place)]

    n_conv_rows = CONV_K * w_conv // LANES
    small_g = jnp.concatenate(
        [_rows(dg_mix[0]), _rows(dg_conv[0]), _rows(dg_attn[0]), _rows(dg_mlp[0]), _rows(dg_ple[0]),
         _rows(dg_final[0]), _rows(dconv.reshape(-1)), loss_part], axis=0)
    n_gain_rows = small_g.shape[0] - n_conv_rows - 1
    pad_rows = (-small_g.shape[0]) % HALO
    small_g = _all_reduce_small(jnp.pad(small_g, ((0, pad_rows), (0, 0))))
    loss = small_g[n_gain_rows + n_conv_rows, 0]
    dconv_full = small_g[n_gain_rows:n_gain_rows + n_conv_rows].reshape(CONV_K, w_conv)
    dconv_mine = lax.dynamic_slice(dconv_full, (0, my_slot * cw), (CONV_K, cw))

    def pack(vecs, conv_part):
        rows = [_rows(t.reshape(-1)) for t in vecs]
        rows.append(jnp.pad(conv_part, ((0, HALO - CONV_K), (0, LANES - cw))))
        return jnp.concatenate(rows, axis=0)

    gains = [g_mix, g_conv_out, g_attn_out, g_mlp, g_ple, g_final]
    gains_m = [m_g_mix, m_g_conv_out, m_g_attn_out, m_g_mlp, m_g_ple, m_g_final]
    gains_v = [v_g_mix, v_g_conv_out, v_g_attn_out, v_g_mlp, v_g_ple, v_g_final]
    gpack = jnp.concatenate([small_g[:n_gain_rows], jnp.pad(dconv_mine, ((0, HALO - CONV_K), (0, LANES - cw)))], axis=0)
    sd, sm, sv = _adam_small(pack(gains, conv_w[0]), gpack, pack(gains_m, m_conv_w[0]), pack(gains_v, v_conv_w[0]))

    def unpack(packed):
        out, r0 = [], 0
        for t in gains:
            nr = t.size // LANES
            out.append(packed[r0:r0 + nr].reshape(t.shape))
            r0 += nr
        out.append(packed[r0:r0 + CONV_K, :cw][None])
        return out

    sg_l, sd_l, sm_l, sv_l = unpack(gpack), unpack(sd), unpack(sm), unpack(sv)
    small_names = ["g_mix", "g_conv_out", "g_attn_out", "g_mlp", "g_ple", "g_final", "conv_w"]
    small_out = {n: [sg_l[i], sd_l[i], sm_l[i], sv_l[i]] for i, n in enumerate(small_names)}

    order = ["g_mix", "w_in", "conv_w", "g_conv_out", "g_attn_out", "w_out", "g_mlp", "w_up", "w_down",
             "g_ple", "w_ple_gate", "w_ple_proj", "g_final"]
    table = {**big_out, **small_out}
    outs = [loss, grad_x[None]]
    for kind in range(4):
        outs.extend(table[n][kind] for n in order)
    return tuple(outs)
```

```python
import jax
import jax.numpy as jnp
from jax import lax
from jax.experimental import pallas as pl
from jax.experimental.pallas import tpu as pltpu

F32 = jnp.float32
BF16 = jnp.bfloat16
EPS = 1e-6
HEAD_DIM = 64
LANES = 128
CONV_K = 3
MXU_WIDTH = 256
ATTN_BLOCK = MXU_WIDTH
HALO = 8
N_DEV = 8
MESH = pl.DeviceIdType.MESH
VMEM_LIMIT = 56 * 1024 * 1024

ADAM_LR = 0.001
ADAM_B1 = 0.9
ADAM_B2 = 0.999
ADAM_EPS = 1e-08
ADAM_WD = 0.01
ADAM_STEP = 10


def _pcall(body, **kw):
    return pl.pallas_call(body, **kw)


def _params(sem=None, **kw):
    return pltpu.CompilerParams(dimension_semantics=sem, vmem_limit_bytes=VMEM_LIMIT, **kw)


def _tile(dim, pref):
    t = min(dim, pref)
    while dim % t:
        t -= LANES
    assert t > 0, (dim, pref)
    return t


_NN = (((1,), (0,)), ((), ()))
_NT = (((1,), (1,)), ((), ()))
_TN = (((0,), (0,)), ((), ()))


def _ep_store(acc, outs):
    outs[0][...] = acc.astype(outs[0].dtype)


def _ep_both(acc, outs):
    outs[0][...] = acc
    outs[1][...] = acc.astype(BF16)


def _ep_residual(acc, res, outs):
    outs[0][...] = acc + res[...]


def _ep_up(acc, outs):
    outs[0][...] = jnp.square(jnp.maximum(acc, 0.0)).astype(BF16)


def _ep_dact(acc, act, outs):
    outs[0][...] = (acc * (2.0 * jnp.sqrt(act[...].astype(F32)))).astype(BF16)


def _row_chunked(epilogue):
    def run(acc, *rest):
        *ex, outs = rest
        n = acc.shape[0]
        for m0 in range(0, n, MXU_WIDTH):
            rows = slice(m0, min(m0 + MXU_WIDTH, n))
            pick = lambda ref: ref.at[rows, :] if ref.shape[0] == n else ref
            epilogue(acc[rows, :], *[pick(e) for e in ex], [pick(o) for o in outs])
    return run


def _ep_norm_bwd(acc, h, g, dres, outs):
    hv = h[...]
    r = lax.rsqrt(jnp.mean(hv * hv, axis=-1, keepdims=True) + EPS)
    hn = hv * r
    outs[2][...] += jnp.sum(acc * hn, axis=0, keepdims=True)
    dhn = acc * g[...]
    dh = dres[...] + r * (dhn - hn * jnp.mean(dhn * hn, axis=-1, keepdims=True))
    outs[0][...] = dh
    outs[1][...] = dh.astype(BF16)


def _matmul(name, a, b, *, dims, grid, a_spec, b_spec, acc_shape, out_shapes, out_specs,
            epilogue=_ep_store, extras=(), extra_specs=(), carry=(), sequential=False, lhs_norm=False):
    nk = grid[2]
    plan = _Carried(carry)
    n_ex, n_out, n_xc, n_xo = len(extras), len(out_shapes), len(plan.inputs), len(plan.out_shapes)
    n_sems = len(plan.sems)
    last = tuple(g - 1 for g in grid)
    assert not lhs_norm or nk == 1

    def product(a_ref, b_ref):
        if len(b_ref.shape) == 2:
            return lax.dot_general(a_ref[...].astype(BF16), b_ref[...].astype(BF16), dims,
                                   preferred_element_type=F32)
        width = b_ref.shape[2]
        return sum(lax.dot_general(a_ref[:, g * width:(g + 1) * width].astype(BF16), b_ref[g].astype(BF16), dims,
                                   preferred_element_type=F32) for g in range(b_ref.shape[0]))

    def body(a_ref, b_ref, *rest):
        ex, rest = rest[:n_ex], rest[n_ex:]
        partials, rest = rest[:n_xc], rest[n_xc:]
        outs, rest = rest[:n_out], rest[n_out:]
        received, rest = rest[:n_xo], rest[n_xo:]
        ids = [pl.program_id(axis) for axis in range(3)]
        if n_xc:
            @pl.when((ids[0] == 0) & (ids[1] == 0) & (ids[2] == 0))
            def _():
                for cp in plan.copies(partials, received, rest[-n_sems:]):
                    cp.start()

        if lhs_norm:
            x_ref, a_ref, gain, ex, outs = a_ref, outs[-1], ex[-1], ex[:-1], outs[:-1]

            @pl.when(ids[1] == 0)
            def _():
                for m0 in range(0, acc_shape[0], MXU_WIDTH):
                    rows = slice(m0, min(m0 + MXU_WIDTH, acc_shape[0]))
                    xv = x_ref[rows, :]
                    r = lax.rsqrt(jnp.mean(xv * xv, axis=-1, keepdims=True) + EPS)
                    a_ref[rows, :] = (xv * r * gain[...]).astype(BF16)

        if nk == 1 and not sequential and dims != _TN:
            if len(b_ref.shape) == 3:
                ns = b_ref.shape[2]
                pieces = [(b_ref.at[g, :, n0:min(n0 + MXU_WIDTH, ns)], slice(g * ns + n0, g * ns + min(n0 + MXU_WIDTH, ns)))
                          for g in range(b_ref.shape[0]) for n0 in range(0, ns, MXU_WIDTH)]
            else:
                spans = [slice(n0, min(n0 + MXU_WIDTH, acc_shape[1])) for n0 in range(0, acc_shape[1], MXU_WIDTH)]
                pieces = [(b_ref.at[cols, :] if dims == _NT else b_ref.at[:, cols], cols) for cols in spans]
            for b_cols, cols in pieces:
                for m0 in range(0, acc_shape[0], MXU_WIDTH):
                    rows = slice(m0, min(m0 + MXU_WIDTH, acc_shape[0]))
                    epilogue(product(a_ref.at[rows, :], b_cols), *[e.at[rows, cols] for e in ex],
                             [o.at[rows, cols] for o in outs])
        else:
            if sequential:
                @pl.when((ids[0] == 0) & (ids[2] == 0))
                def _():
                    for o in outs:
                        if o.shape[0] != acc_shape[0]:
                            o[...] = jnp.zeros_like(o)

            if nk == 1:
                _row_chunked(epilogue)(product(a_ref, b_ref), *ex, outs)
            else:
                acc = rest[0]

                @pl.when(ids[2] == 0)
                def _():
                    acc[...] = product(a_ref, b_ref)

                @pl.when(ids[2] > 0)
                def _():
                    acc[...] += product(a_ref, b_ref)

                @pl.when(ids[2] == nk - 1)
                def _():
                    _row_chunked(epilogue)(acc, *ex, outs)

        if n_xc:
            @pl.when((ids[0] == last[0]) & (ids[1] == last[1]) & (ids[2] == last[2]))
            def _():
                for cp in plan.copies(partials, received, rest[-n_sems:]):
                    cp.wait()

    anywhere = pl.BlockSpec(memory_space=pl.ANY)
    return _pcall(
        body, name=name, grid=grid,
        in_specs=[a_spec, b_spec, *extra_specs, *[anywhere] * n_xc],
        out_specs=[*out_specs, *[anywhere] * n_xo],
        out_shape=[*out_shapes, *plan.out_shapes],
        scratch_shapes=([] if nk == 1 else [pltpu.VMEM(acc_shape, F32)]) + plan.sems,
        input_output_aliases=plan.aliases(2 + n_ex, n_out),
        compiler_params=_params(("arbitrary",) * 3 if n_xc or sequential or lhs_norm
                                else ("parallel", "parallel", "arbitrary")),
    )(a, b, *extras, *plan.inputs)


_NO_CARRY = ()


def _mm_nn(name, a, w, *, n_shard=None, epilogue=_ep_store, out_dtypes=(F32,), extras=(), carry=_NO_CARRY,
           lhs_norm=None, tm=1024, tn=1024, tk=1024):
    m, kd = a.shape
    if lhs_norm is not None:
        tk = kd
    if n_shard is None:
        n = w.shape[1]
        tn = _tile(n, tn)
        tk = _tile(kd, tk)
        b_spec = pl.BlockSpec((tk, tn), lambda i, j, k: (k, j))
    elif tn >= 2 * n_shard and tk >= kd:
        n = N_DEV * n_shard
        group = min(tn // n_shard, N_DEV)
        while N_DEV % group:
            group -= 1
        tn, tk = group * n_shard, kd
        b_spec = pl.BlockSpec((group, tk, n_shard), lambda i, j, k: (j, 0, 0))
    else:
        n = N_DEV * n_shard
        tn = _tile(n_shard, tn)
        tk = _tile(kd, tk)
        per = n_shard // tn
        b_spec = pl.BlockSpec((None, tk, tn), lambda i, j, k: (j // per, k, j % per))
    tm = _tile(m, tm)
    o_spec = pl.BlockSpec((tm, tn), lambda i, j, k: (i, j))
    out_shapes = [jax.ShapeDtypeStruct((m, n), d) for d in out_dtypes]
    out_specs = [o_spec] * len(out_dtypes)
    extra_specs = [o_spec] * len(extras)
    if lhs_norm is not None:
        extras = (*extras, lhs_norm)
        extra_specs.append(pl.BlockSpec((1, kd), lambda i, j, k: (0, 0)))
        out_shapes.append(jax.ShapeDtypeStruct((m, kd), BF16))
        out_specs.append(pl.BlockSpec((tm, kd), lambda i, j, k: (i, 0)))
    return _matmul(
        name, a, w, dims=_NN, grid=(m // tm, n // tn, kd // tk),
        a_spec=pl.BlockSpec((tm, tk), lambda i, j, k: (i, k)), b_spec=b_spec,
        acc_shape=(tm, tn), out_shapes=out_shapes, out_specs=out_specs,
        epilogue=epilogue, extras=extras, extra_specs=extra_specs, carry=carry, lhs_norm=lhs_norm is not None)


def _mm_nt(name, a, w, *, epilogue=_ep_store, out_dtypes=(F32,), extras=(), carry=_NO_CARRY,
           tm=1024, tn=1024, tk=1024):
    m, kd = a.shape
    n = w.shape[0]
    tm, tn, tk = _tile(m, tm), _tile(n, tn), _tile(kd, tk)
    o_spec = pl.BlockSpec((tm, tn), lambda i, j, k: (i, j))
    return _matmul(
        name, a, w, dims=_NT, grid=(m // tm, n // tn, kd // tk),
        a_spec=pl.BlockSpec((tm, tk), lambda i, j, k: (i, k)),
        b_spec=pl.BlockSpec((tn, tk), lambda i, j, k: (j, k)),
        acc_shape=(tm, tn),
        out_shapes=[jax.ShapeDtypeStruct((m, n), d) for d in out_dtypes],
        out_specs=[o_spec] * len(out_dtypes),
        epilogue=epilogue, extras=extras, extra_specs=[o_spec] * len(extras), carry=carry)


def _mm_nt_norm_bwd(name, a, w, h, g, dres, *, k_shard=None, carry=_NO_CARRY, tm=512, tk=1024):
    m, kd = a.shape
    n = h.shape[1]
    if k_shard is None:
        tk = _tile(kd, tk)
        b_spec = pl.BlockSpec((n, tk), lambda i, j, k: (0, k))
    else:
        group = max(1, min(tk // k_shard, N_DEV))
        while N_DEV % group:
            group -= 1
        tk = group * k_shard
        b_spec = pl.BlockSpec((group, n, k_shard), lambda i, j, k: (k, 0, 0))
    tm = _tile(m, tm)
    rows = pl.BlockSpec((tm, n), lambda i, j, k: (i, 0))
    vec = pl.BlockSpec((1, n), lambda i, j, k: (0, 0))
    return _matmul(
        name, a, w, dims=_NT, grid=(m // tm, 1, kd // tk),
        a_spec=pl.BlockSpec((tm, tk), lambda i, j, k: (i, k)), b_spec=b_spec, acc_shape=(tm, n),
        out_shapes=[jax.ShapeDtypeStruct((m, n), F32), jax.ShapeDtypeStruct((m, n), BF16),
                    jax.ShapeDtypeStruct((1, n), F32)],
        out_specs=[rows, rows, vec], epilogue=_ep_norm_bwd,
        extras=(h, g, dres), extra_specs=[rows, vec, rows], carry=carry, sequential=True)


TN_TILE_BYTES = 40 * 1024 * 1024


def _mm_tn(name, a, b, *, n_shard=None, carry=_NO_CARRY, tm=1024, tn=1024):
    t, m = a.shape
    n = b.shape[1]
    tm = _tile(m, tm)
    tn = _tile(n if n_shard is None else n_shard, tn)
    tk = t
    while 2 * 2 * tk * (tm + tn) + 4 * tm * tn * 5 > TN_TILE_BYTES and tk % (2 * LANES) == 0:
        tk //= 2
    if n_shard is None:
        o_spec = pl.BlockSpec((tm, tn), lambda i, j, k: (i, j))
        shape = (m, n)
    else:
        per = n_shard // tn
        o_spec = pl.BlockSpec((None, tm, tn), lambda i, j, k: (j // per, i, j % per))
        shape = (N_DEV, m, n_shard)
    return _matmul(
        name, a, b, dims=_TN, grid=(m // tm, n // tn, t // tk),
        a_spec=pl.BlockSpec((tk, tm), lambda i, j, k: (k, i)),
        b_spec=pl.BlockSpec((tk, tn), lambda i, j, k: (k, j)),
        acc_shape=(tm, tn), epilogue=_ep_both, carry=carry,
        out_shapes=[jax.ShapeDtypeStruct(shape, F32), jax.ShapeDtypeStruct(shape, BF16)],
        out_specs=[o_spec, o_spec])


def _ple_proj(p, w_g, tm=1024):
    s, kd = p.shape
    ns = w_g.shape[2]
    tm = _tile(s, tm)

    def body(p_ref, w_ref, o_ref):
        pv = p_ref[...].astype(BF16)
        for j in range(N_DEV):
            o_ref[:, j * ns:(j + 1) * ns] = jnp.dot(pv, w_ref[j], preferred_element_type=F32)

    return _pcall(body, name="ple_proj", grid=(s // tm,),
                  in_specs=[pl.BlockSpec((tm, kd), lambda i: (i, 0)),
                            pl.BlockSpec((N_DEV, kd, ns), lambda i: (0, 0, 0))],
                  out_specs=pl.BlockSpec((tm, N_DEV * ns), lambda i: (i, 0)),
                  out_shape=jax.ShapeDtypeStruct((s, N_DEV * ns), F32),
                  compiler_params=_params(("parallel",)))(p, w_g)


def _d_ple_proj(p, dpp, ns, tk=1024):
    s, kd = p.shape
    tk = _tile(s, tk)
    nk = s // tk

    def body(p_ref, d_ref, of_ref, ob_ref, acc):
        k = pl.program_id(0)

        @pl.when(k == 0)
        def _():
            acc[...] = jnp.zeros_like(acc)

        pv = p_ref[...].astype(BF16)
        for j in range(N_DEV):
            acc[j] += lax.dot_general(pv, d_ref[:, j * ns:(j + 1) * ns], _TN, preferred_element_type=F32)

        @pl.when(k == nk - 1)
        def _():
            of_ref[...] = acc[...]
            ob_ref[...] = acc[...].astype(BF16)

    whole = pl.BlockSpec((N_DEV, kd, ns), lambda k: (0, 0, 0))
    return _pcall(body, name="d_w_ple_proj", grid=(nk,),
                  in_specs=[pl.BlockSpec((tk, kd), lambda k: (k, 0)),
                            pl.BlockSpec((tk, N_DEV * ns), lambda k: (k, 0))],
                  out_specs=[whole, whole],
                  out_shape=[jax.ShapeDtypeStruct((N_DEV, kd, ns), F32), jax.ShapeDtypeStruct((N_DEV, kd, ns), BF16)],
                  scratch_shapes=[pltpu.VMEM((N_DEV, kd, ns), F32)],
                  compiler_params=_params(("arbitrary",)))(p, dpp)


def _ep_ple_loss(gl, h2, pp, tgt, g_final, outs):
    loss_ref, dh3_ref, dgl_ref, dpp_ref, dg_ref = outs
    gate = jax.nn.sigmoid(gl)
    ppv = pp[...]
    h3 = h2[...] + gate * ppv
    r = lax.rsqrt(jnp.mean(h3 * h3, axis=-1, keepdims=True) + EPS)
    hn = h3 * r
    gv = g_final[...]
    diff = hn * gv - tgt[...]
    row = jnp.mean(diff * diff, axis=-1, keepdims=True)
    loss_ref[...] += 0.5 * jnp.sum(row, axis=0, keepdims=True)
    dy = diff * (1.0 / h3.shape[-1])
    dg_ref[...] += jnp.sum(dy * hn, axis=0, keepdims=True)
    dhn = dy * gv
    dh3 = r * (dhn - hn * jnp.mean(dhn * hn, axis=-1, keepdims=True))
    dh3_ref[...] = dh3
    dgl_ref[...] = (dh3 * ppv * gate * (1.0 - gate)).astype(BF16)
    dpp_ref[...] = (dh3 * gate).astype(BF16)


def _ple_gate_loss(h2, g_ple, w_gate, pp, tgt, g_final, tm=512):
    s, d = h2.shape
    tm = _tile(s, tm)
    rows = pl.BlockSpec((tm, d), lambda i, j, k: (i, 0))
    vec = pl.BlockSpec((1, d), lambda i, j, k: (0, 0))
    return _matmul(
        "ple_gate_loss", h2, w_gate, dims=_NN, grid=(s // tm, 1, 1),
        a_spec=rows, b_spec=pl.BlockSpec((d, d), lambda i, j, k: (0, 0)), acc_shape=(tm, d),
        out_shapes=[jax.ShapeDtypeStruct((1, LANES), F32), jax.ShapeDtypeStruct((s, d), F32),
                    jax.ShapeDtypeStruct((s, d), BF16), jax.ShapeDtypeStruct((s, d), BF16),
                    jax.ShapeDtypeStruct((1, d), F32), jax.ShapeDtypeStruct((s, d), BF16)],
        out_specs=[pl.BlockSpec((1, LANES), lambda i, j, k: (0, 0)), rows, rows, rows, vec, rows],
        epilogue=_ep_ple_loss, extras=(h2, pp, tgt, g_final, g_ple), extra_specs=[rows, rows, rows, vec, vec],
        sequential=True, lhs_norm=True)


def _low_half():
    return lax.broadcasted_iota(jnp.int32, (1, LANES), 1) < HEAD_DIM


def _half_mean(v, low):
    s_lo = jnp.sum(jnp.where(low, v, 0.0), axis=-1, keepdims=True)
    s_hi = jnp.sum(jnp.where(low, 0.0, v), axis=-1, keepdims=True)
    return jnp.where(low, s_lo, s_hi) * (1.0 / HEAD_DIM)


def _head_norm_bwd(val, dout, g, low):
    r = lax.rsqrt(_half_mean(val * val, low) + EPS)
    vn = val * r
    dvn = dout * g
    return r * (dvn - vn * _half_mean(dvn * vn, low)), dout * vn


def _conv_taps(vv_ext, w_ref):
    v0 = vv_ext[HALO:]
    v1 = pltpu.roll(vv_ext, 1, 0)[HALO:]
    v2 = pltpu.roll(vv_ext, 2, 0)[HALO:]
    return w_ref[2:3, :] * v0 + w_ref[1:2, :] * v1 + w_ref[0:1, :] * v2, (v0, v1, v2)


def _conv_fwd(proj, conv_w, g_conv, w_conv, d_model, tr=1024):
    s = proj.shape[0]
    tr = _tile(s, tr)
    hb = tr // HALO

    def main(part):
        return pl.BlockSpec((tr, w_conv), lambda i: (i, part))

    def prev(part):
        return pl.BlockSpec((HALO, w_conv), lambda i: (jnp.maximum(i * hb - 1, 0), part))

    def body(cb_ref, cc_ref, cu_ref, ccp_ref, cup_ref, w_ref, g_ref, o_ref):
        i = pl.program_id(0)
        low = _low_half()
        for j in range(w_conv // LANES):
            cols = slice(j * LANES, (j + 1) * LANES)
            vv_prev = jnp.where(i > 0, ccp_ref[:, cols] * cup_ref[:, cols], 0.0)
            vv_ext = jnp.concatenate([vv_prev, cc_ref[:, cols] * cu_ref[:, cols]], axis=0)
            y, _ = _conv_taps(vv_ext, w_ref.at[:, cols])
            co = cb_ref[:, cols] * y
            r = lax.rsqrt(_half_mean(co * co, low) + EPS)
            o_ref[:, cols] = (co * r * g_ref[:, cols]).astype(BF16)

    return _pcall(
        body, name="conv_fwd", grid=(s // tr,),
        in_specs=[main(0), main(1), main(2), prev(1), prev(2),
                  pl.BlockSpec((CONV_K, w_conv), lambda i: (0, 0)),
                  pl.BlockSpec((1, w_conv), lambda i: (0, 0))],
        out_specs=pl.BlockSpec((tr, w_conv), lambda i: (i, 0)),
        out_shape=jax.ShapeDtypeStruct((s, d_model), BF16),
        compiler_params=_params(("parallel",)),
    )(proj, proj, proj, proj, proj, conv_w, g_conv)


def _conv_bwd(proj, dcat, conv_w, g_conv, dproj, w_conv, tr=1024):
    s = proj.shape[0]
    tr = _tile(s, tr)
    hb = tr // HALO
    last = s // HALO - 1
    nt = s // tr

    def main(part):
        return pl.BlockSpec((tr, w_conv), lambda i: (i, part))

    def prev(part):
        return pl.BlockSpec((HALO, w_conv), lambda i: (jnp.maximum(i * hb - 1, 0), part))

    def nxt(part):
        return pl.BlockSpec((HALO, w_conv), lambda i: (jnp.minimum((i + 1) * hb, last), part))

    def body(cb_ref, cc_ref, cu_ref, dc_ref, ccp_ref, cup_ref, cbn_ref, ccn_ref, cun_ref, dcn_ref,
             w_ref, g_ref, dproj_in, dproj_ref, dw_ref, dg_ref):
        del dproj_in
        i = pl.program_id(0)

        @pl.when(i == 0)
        def _():
            dw_ref[...] = jnp.zeros_like(dw_ref)
            dg_ref[...] = jnp.zeros_like(dg_ref)

        low = _low_half()
        n_ext = tr + HALO
        rowid = lax.broadcasted_iota(jnp.int32, (n_ext, 1), 0)
        for j in range(w_conv // LANES):
            cols = slice(j * LANES, (j + 1) * LANES)
            wj = w_ref.at[:, cols]
            cc, cu = cc_ref[:, cols], cu_ref[:, cols]
            vv_prev = jnp.where(i > 0, ccp_ref[:, cols] * cup_ref[:, cols], 0.0)
            vv_ext = jnp.concatenate([vv_prev, cc * cu, ccn_ref[:, cols] * cun_ref[:, cols]], axis=0)
            y_ext, (v0, v1, v2) = _conv_taps(vv_ext, wj)
            cb_ext = jnp.concatenate([cb_ref[:, cols], cbn_ref[:, cols]], axis=0)
            dc_ext = jnp.concatenate([dc_ref[:, cols], dcn_ref[:, cols]], axis=0)
            dco, dgn = _head_norm_bwd(cb_ext * y_ext, dc_ext, g_ref[:, cols], low)
            dyc = jnp.where((rowid < tr) | (i < nt - 1), dco * cb_ext, 0.0)
            dvv = (wj[2:3, :] * dyc[:tr] + wj[1:2, :] * pltpu.roll(dyc, n_ext - 1, 0)[:tr]
                   + wj[0:1, :] * pltpu.roll(dyc, n_ext - 2, 0)[:tr])
            dproj_ref[:, cols] = (dco[:tr] * y_ext[:tr]).astype(BF16)
            dproj_ref[:, w_conv + j * LANES:w_conv + (j + 1) * LANES] = (dvv * cu).astype(BF16)
            dproj_ref[:, 2 * w_conv + j * LANES:2 * w_conv + (j + 1) * LANES] = (dvv * cc).astype(BF16)
            dyt = dyc[:tr]
            for tap, shifted in enumerate((v2, v1, v0)):
                dw_ref[tap:tap + 1, cols] += jnp.sum(dyt * shifted[:tr], axis=0, keepdims=True)
            dg_ref[:, cols] += jnp.sum(dgn[:tr], axis=0, keepdims=True)

    n_cols = dproj.shape[1]
    return _pcall(
        body, name="conv_bwd", grid=(nt,),
        in_specs=[main(0), main(1), main(2), main(0),
                  prev(1), prev(2), nxt(0), nxt(1), nxt(2), nxt(0),
                  pl.BlockSpec((CONV_K, w_conv), lambda i: (0, 0)),
                  pl.BlockSpec((1, w_conv), lambda i: (0, 0)),
                  pl.BlockSpec(memory_space=pl.ANY)],
        out_specs=[pl.BlockSpec((tr, 3 * w_conv), lambda i: (i, 0)),
                   pl.BlockSpec((CONV_K, w_conv), lambda i: (0, 0)),
                   pl.BlockSpec((1, w_conv), lambda i: (0, 0))],
        out_shape=[jax.ShapeDtypeStruct((s, n_cols), BF16),
                   jax.ShapeDtypeStruct((CONV_K, w_conv), F32),
                   jax.ShapeDtypeStruct((1, w_conv), F32)],
        input_output_aliases={12: 0},
        compiler_params=_params(("arbitrary",)),
    )(proj, proj, proj, dcat, proj, proj, proj, proj, proj, dcat, conv_w, g_conv, dproj)


STRIP = 16

ALL_CHAINS = (0, 1, 2, 3)
UPPER_CHAINS = (2, 3)


RUN_FLOOR = -104.0


def _any_weight_left(run_s):
    return (jnp.max(run_s[...]) > RUN_FLOOR).astype(jnp.int32)


def _chains(low):
    return [(2 * half + h, half, msk) for half in range(2)
            for h, msk in enumerate((low, jnp.logical_not(low)))]


def _suffix_operator(t):
    r = lax.broadcasted_iota(jnp.int32, (2 * t, t), 0)
    c = lax.broadcasted_iota(jnp.int32, (2 * t, t), 1)
    return jnp.where((r > c) & ((r < t) | (r - t > c)), 1.0, 0.0).astype(BF16)


def _strips(t, diag):
    return [(i, slice(i * STRIP, (i + 1) * STRIP), t // 2 if diag and (i + 1) * STRIP <= t // 2 else t)
            for i in range(t // STRIP)]


def _strip_mask(i, w):
    r = lax.broadcasted_iota(jnp.int32, (STRIP, w), 0) + i * STRIP
    c = lax.broadcasted_iota(jnp.int32, (STRIP, w), 1)
    return r > c


def _store_trimmed(ref, rows, val, w, t, at=0):
    ref[rows, at:at + w] = val
    if w < t:
        ref[rows, at + w:at + t] = jnp.zeros((STRIP, t - w), val.dtype)


def _store_split(ref, rows, val, w, t):
    hi = val.astype(BF16)
    _store_trimmed(ref, rows, hi, w, t)
    _store_trimmed(ref, rows, (val - hi.astype(F32)).astype(BF16), w, t, at=t)


def _sb_scores(z_s, split_s, zl_s, tot_s, keep_s, t, diag):
    for i, rows, w in _strips(t, diag):
        z = z_s[rows, :w]
        log_beta = jnp.minimum(z, 0.0) - jnp.log(1.0 + jnp.exp(-jnp.abs(z)))
        log_keep = log_beta - z
        if diag:
            log_keep = jnp.where(_strip_mask(i, w), log_keep, 0.0)
        _store_split(split_s, rows, log_keep, w, t)
        zl_s[rows, :w] = log_beta
        tot_s[rows, :] = _row_sum(log_keep)
        if keep_s is not None:
            keep_s[rows, :w] = jnp.exp(log_keep)


def _row_sum(v):
    return jnp.broadcast_to(jnp.sum(v, axis=-1, keepdims=True), (v.shape[0], LANES))


def _wide(r, t):
    return jnp.concatenate([r] * (t // LANES), axis=1)


def _sb_weights(zl_s, suf_s, run_s, tot_s, a_s, t, diag, da_s=None, glog_s=None, gsplit_s=None, gtot_s=None):
    for i, rows, w in _strips(t, diag):
        run = run_s[rows, :]
        a = jnp.exp(zl_s[rows, :w] + suf_s[rows, :w] + _wide(run, w))
        if diag:
            a = jnp.where(_strip_mask(i, w), a, 0.0)
        ab = a.astype(BF16)
        _store_trimmed(a_s, rows, ab, w, t)
        run_s[rows, :] = run + tot_s[rows, :]
        if da_s is not None:
            glog = ab.astype(F32) * da_s[rows, :w]
            glog_s[rows, :w] = glog
            _store_split(gsplit_s, rows, glog, w, t)
            gtot_s[rows, :] = _row_sum(glog)


def _sb_dscores(glog_s, cum_s, rest_s, gtot_s, keep_s, dz_s, t, diag):
    for i, rows, w in _strips(t, diag):
        glog = glog_s[rows, :w]
        rest = rest_s[rows, :]
        from_here = _wide(rest, w) - cum_s[rows, :w]
        before = from_here - glog
        dz = from_here * keep_s[rows, :w] - before
        if diag:
            dz = jnp.where(_strip_mask(i, w), dz, 0.0)
        _store_trimmed(dz_s, rows, dz.astype(BF16), w, t)
        rest_s[rows, :] = rest - gtot_s[rows, :]


def _attn_fwd(proj, g_attn, cat, w_conv, carry, t=ATTN_BLOCK):
    s = proj.shape[0]
    w_attn = g_attn.shape[1]
    nh = w_attn // LANES
    t = _tile(s, t)
    tq = 2 * t
    nq = s // tq
    q0 = 3 * w_conv // LANES
    scale = HEAD_DIM ** -0.5
    plan = _Carried(carry)
    nw, n_res = len(plan.inputs), len(plan.out_shapes)

    def body(q_ref, k_ref, v_ref, g_ref, cat_in, *rest):
        staged_refs, rest = rest[:nw], rest[nw:]
        o_ref, cat_ref = rest[:2]
        gathered_refs, rest = rest[2:2 + n_res], rest[2 + n_res:]
        kb, vb, tri_s, qm_s, z_s, split_s, zl_s, suf_s, a_s, run_s, tot_s, acc_s = rest[:12]
        gather_sems = rest[12:]
        del cat_in
        qi = pl.program_id(1)

        @pl.when((pl.program_id(0) == 0) & (qi == 0))
        def _():
            for cp in plan.copies(staged_refs, gathered_refs, gather_sems):
                cp.start()

        @pl.when(qi == 0)
        def _():
            kb[...] = k_ref[...].astype(BF16)
            vb[...] = v_ref[...].astype(BF16)
            tri_s[...] = _suffix_operator(t)

        low = _low_half()
        for c, half, msk in _chains(low):
            qm_s[c] = jnp.where(msk, q_ref[half * t:(half + 1) * t, :] * scale, 0.0).astype(BF16)
            run_s[c] = jnp.zeros((t, LANES), F32)
            acc_s[c] = jnp.zeros((t, LANES), F32)

        def key_rows(kblk):
            return pl.ds(pl.multiple_of(kblk * t, t), t)

        def key_block(base, c):
            return key_rows(jnp.maximum(base + c // 2, 0))

        def scores_matmul(base, chains):
            for c in chains:
                z_s[c] = lax.dot_general(qm_s[c], kb[key_block(base, c), :], _NT, preferred_element_type=F32)

        def front(modes, base, prev=None):
            for c, diag in modes:
                _sb_scores(z_s.at[c], split_s.at[c], zl_s.at[c], tot_s.at[c], None, t, diag)
                suf_s[c] = jnp.dot(split_s[c], tri_s[...], preferred_element_type=F32)
            if prev is not None:
                tail(*prev)
            scores_matmul(base - 1, ALL_CHAINS)
            for c, diag in modes:
                _sb_weights(zl_s.at[c], suf_s.at[c], run_s.at[c], tot_s.at[c], a_s.at[c], t, diag)

        def tail(base, chains):
            for c in chains:
                acc_s[c] += jnp.dot(a_s[c], vb[key_block(base, c), :], preferred_element_type=F32)

        first = 2 * qi
        scores_matmul(first, ALL_CHAINS)
        front([(c, True) for c in ALL_CHAINS], first)

        def loop(state):
            it = state[0]
            base = first - 1 - it
            front([(c, False) for c in ALL_CHAINS], base, prev=(base + 1, ALL_CHAINS))
            return it + 1, _any_weight_left(run_s)

        done, live = lax.while_loop(lambda state: (state[0] < first) & (state[1] > 0), loop,
                                    (jnp.int32(0), jnp.int32(1)))
        one_more = (done == first) & (live > 0)

        @pl.when(one_more)
        def _():
            front([(c, False) for c in UPPER_CHAINS], -1, prev=(0, ALL_CHAINS))
            tail(-1, UPPER_CHAINS)

        @pl.when(jnp.logical_not(one_more))
        def _():
            tail(first - done, ALL_CHAINS)

        for half in range(2):
            rows = slice(half * t, (half + 1) * t)
            o = jnp.where(low, acc_s[2 * half], acc_s[2 * half + 1])
            o_ref[rows, :] = o
            r = lax.rsqrt(_half_mean(o * o, low) + EPS)
            cat_ref[rows, :] = (o * r * g_ref[...]).astype(BF16)

        @pl.when((pl.program_id(0) == nh - 1) & (qi == nq - 1))
        def _():
            for cp in plan.copies(staged_refs, gathered_refs, gather_sems):
                cp.wait()

    whole = lambda col0: pl.BlockSpec((s, LANES), lambda h, i: (0, col0 + h))
    n_ch = len(ALL_CHAINS)
    res = _pcall(
        body, name="attn_fwd", grid=(nh, nq),
        in_specs=[pl.BlockSpec((tq, LANES), lambda h, i: (i, q0 + h)),
                  whole(q0 + nh), whole(q0 + 2 * nh),
                  pl.BlockSpec((1, LANES), lambda h, i: (0, h)),
                  pl.BlockSpec(memory_space=pl.ANY)] + [pl.BlockSpec(memory_space=pl.ANY)] * nw,
        out_specs=[pl.BlockSpec((tq, LANES), lambda h, i: (i, h)),
                   pl.BlockSpec((tq, LANES), lambda h, i: (i, w_conv // LANES + h))]
        + [pl.BlockSpec(memory_space=pl.ANY)] * n_res,
        out_shape=[jax.ShapeDtypeStruct((s, w_attn), F32),
                   jax.ShapeDtypeStruct(cat.shape, BF16)] + plan.out_shapes,
        scratch_shapes=[pltpu.VMEM((s, LANES), BF16), pltpu.VMEM((s, LANES), BF16),
                        pltpu.VMEM((2 * t, t), BF16),
                        pltpu.VMEM((n_ch, t, LANES), BF16),
                        pltpu.VMEM((n_ch, t, t), F32),
                        pltpu.VMEM((n_ch, t, 2 * t), BF16),
                        pltpu.VMEM((n_ch, t, t), F32),
                        pltpu.VMEM((n_ch, t, t), F32),
                        pltpu.VMEM((n_ch, t, t), BF16),
                        pltpu.VMEM((n_ch, t, LANES), F32),
                        pltpu.VMEM((n_ch, t, LANES), F32),
                        pltpu.VMEM((n_ch, t, LANES), F32)]
        + plan.sems,
        input_output_aliases={4: 1, **plan.aliases(5, 2)},
        compiler_params=_params(("arbitrary", "arbitrary")),
    )(proj, proj, proj, g_attn, cat, *plan.inputs)
    return res[0], res[1], res[2:]


def _attn_bwd(proj, o, dcat, g_attn, w_conv, carry, t=ATTN_BLOCK):
    s, n_cols = proj.shape
    w_attn = g_attn.shape[1]
    nh = w_attn // LANES
    t = _tile(s, t)
    tq = 2 * t
    nq = s // tq
    q0 = 3 * w_conv // LANES
    scale = HEAD_DIM ** -0.5
    plan = _Carried(carry)
    nw, n_res = len(plan.inputs), len(plan.out_shapes)

    def body(q_ref, k_ref, v_ref, o_ref, do_ref, g_ref, *rest):
        partial_refs, rest = rest[:nw], rest[nw:]
        dproj_ref, dg_ref = rest[:2]
        received_refs, rest = rest[2:2 + n_res], rest[2 + n_res:]
        (kb, vb, dkt_acc, dvt_acc, stash, tri_s, qm_s, dom_s, qt_s, dot_s, z_s, da_s, split_s, zl_s,
         keep_s, suf_s, a_s, glog_s, gsplit_s, cum_s, dz_s, run_s, tot_s, rest_s, gtot_s, dq_s) = rest[:26]
        out_sems, scatter_sems = rest[26], rest[27:]
        step_i = pl.program_id(1)
        qi = nq - 1 - step_i
        head_pair = pl.program_id(0)
        first_step = (head_pair == 0) & (step_i == 0)
        last_step = (head_pair == nh - 1) & (step_i == nq - 1)

        @pl.when(first_step)
        def _():
            for cp in plan.copies(partial_refs, received_refs, scatter_sems):
                cp.start()

        def out_copies():
            rows = pl.ds(pl.multiple_of(qi * tq, tq), tq)
            return [pltpu.make_async_copy(
                stash.at[w], dproj_ref.at[rows, pl.ds(pl.multiple_of((q0 + w * nh + head_pair) * LANES, LANES), LANES)],
                out_sems.at[w]) for w in range(3)]

        def walk():
            @pl.when(step_i == 0)
            def _():
                kb[...] = k_ref[...].astype(BF16)
                vb[...] = v_ref[...].astype(BF16)
                tri_s[...] = _suffix_operator(t)
                dkt_acc[...] = jnp.zeros_like(dkt_acc)
                dvt_acc[...] = jnp.zeros_like(dvt_acc)
                dg_ref[...] = jnp.zeros_like(dg_ref)

            low = _low_half()
            gv = g_ref[...]
            for half in range(2):
                rows = slice(half * t, (half + 1) * t)
                q = q_ref[rows, :] * scale
                ov = o_ref[rows, :]
                d_o, dgn = _head_norm_bwd(ov, do_ref[rows, :], gv, low)
                dg_ref[...] += jnp.sum(dgn, axis=0, keepdims=True)
                for h, msk in enumerate((low, jnp.logical_not(low))):
                    c = 2 * half + h
                    qh = jnp.where(msk, q, 0.0)
                    doh = jnp.where(msk, d_o, 0.0)
                    dom = doh.astype(BF16)
                    qm_s[c] = qh.astype(BF16)
                    dom_s[c] = dom
                    qt_s[c] = qh.T.astype(BF16)
                    dot_s[c] = doh.T.astype(BF16)
                    rest_s[c] = _row_sum(dom.astype(F32) * ov)
                    run_s[c] = jnp.zeros((t, LANES), F32)
                    dq_s[c] = jnp.zeros((t, LANES), F32)

            def key_rows(kblk):
                return pl.ds(pl.multiple_of(kblk * t, t), t)

            def block_of(base, half):
                return jnp.maximum(base + half, 0)

            def scores_matmul(base, chains):
                for c in chains:
                    ks = kb[key_rows(block_of(base, c // 2)), :]
                    z_s[c] = lax.dot_general(qm_s[c], ks, _NT, preferred_element_type=F32)

            def da_matmul(base, chains):
                for c in chains:
                    vs = vb[key_rows(block_of(base, c // 2)), :]
                    da_s[c] = lax.dot_general(dom_s[c], vs, _NT, preferred_element_type=F32)

            def front(modes, base, prev=None):
                if prev is not None:
                    tail(*prev)
                for c, diag in modes:
                    _sb_scores(z_s.at[c], split_s.at[c], zl_s.at[c], tot_s.at[c], keep_s.at[c], t, diag)
                    suf_s[c] = jnp.dot(split_s[c], tri_s[...], preferred_element_type=F32)
                scores_matmul(base - 1, ALL_CHAINS)
                for c, diag in modes:
                    _sb_weights(zl_s.at[c], suf_s.at[c], run_s.at[c], tot_s.at[c], a_s.at[c], t, diag,
                                da_s.at[c], glog_s.at[c], gsplit_s.at[c], gtot_s.at[c])
                    cum_s[c] = jnp.dot(gsplit_s[c], tri_s[...], preferred_element_type=F32)
                da_matmul(base - 1, ALL_CHAINS)
                for c, diag in modes:
                    _sb_dscores(glog_s.at[c], cum_s.at[c], rest_s.at[c], gtot_s.at[c], keep_s.at[c],
                                dz_s.at[c], t, diag)

            def tail(base, chains):
                for half in range(2):
                    mine = [c for c in chains if c // 2 == half]
                    if not mine:
                        continue
                    kblk = block_of(base, half)
                    ks = kb[key_rows(kblk), :]
                    dkt = dkt_acc[kblk]
                    dvt = dvt_acc[kblk]
                    for c in mine:
                        dq_s[c] += jnp.dot(dz_s[c], ks, preferred_element_type=F32)
                        dkt = dkt + jnp.dot(qt_s[c], dz_s[c], preferred_element_type=F32)
                        dvt = dvt + jnp.dot(dot_s[c], a_s[c], preferred_element_type=F32)
                    dkt_acc[kblk] = dkt
                    dvt_acc[kblk] = dvt

            first = 2 * qi
            scores_matmul(first, ALL_CHAINS)
            da_matmul(first, ALL_CHAINS)
            front([(c, True) for c in ALL_CHAINS], first)

            def loop(state):
                it = state[0]
                base = first - 1 - it
                front([(c, False) for c in ALL_CHAINS], base, prev=(base + 1, ALL_CHAINS))
                return it + 1, _any_weight_left(run_s)

            done, live = lax.while_loop(lambda state: (state[0] < first) & (state[1] > 0), loop,
                                        (jnp.int32(0), jnp.int32(1)))
            one_more = (done == first) & (live > 0)

            @pl.when(one_more)
            def _():
                front([(c, False) for c in UPPER_CHAINS], -1, prev=(0, ALL_CHAINS))
                tail(-1, UPPER_CHAINS)

            @pl.when(jnp.logical_not(one_more))
            def _():
                tail(first - done, ALL_CHAINS)

            @pl.when(jnp.logical_not(first_step))
            def _():
                for cp in out_copies():
                    cp.wait()

            for half in range(2):
                rows = slice(half * t, (half + 1) * t)
                stash[0, rows, :] = (jnp.where(low, dq_s[2 * half], dq_s[2 * half + 1]) * scale).astype(BF16)
                stash[1, rows, :] = dkt_acc[2 * qi + half].T.astype(BF16)
                stash[2, rows, :] = dvt_acc[2 * qi + half].T.astype(BF16)
            for cp in out_copies():
                cp.start()

        walk()

        @pl.when(last_step)
        def _():
            for cp in out_copies():
                cp.wait()
            for cp in plan.copies(partial_refs, received_refs, scatter_sems):
                cp.wait()

    whole = lambda col0: pl.BlockSpec((s, LANES), lambda h, i: (0, col0 + h))
    blk = lambda col0: pl.BlockSpec((tq, LANES), lambda h, i: (nq - 1 - i, col0 + h))
    n_ch = len(ALL_CHAINS)
    res = _pcall(
        body, name="attn_bwd", grid=(nh, nq),
        in_specs=[blk(q0), whole(q0 + nh), whole(q0 + 2 * nh), blk(0), blk(w_conv // LANES),
                  pl.BlockSpec((1, LANES), lambda h, i: (0, h))] + [pl.BlockSpec(memory_space=pl.ANY)] * nw,
        out_specs=[pl.BlockSpec(memory_space=pl.ANY),
                   pl.BlockSpec((1, LANES), lambda h, i: (0, h))] + [pl.BlockSpec(memory_space=pl.ANY)] * n_res,
        out_shape=[jax.ShapeDtypeStruct((s, n_cols), BF16), jax.ShapeDtypeStruct((1, w_attn), F32)]
        + plan.out_shapes,
        scratch_shapes=[pltpu.VMEM((s, LANES), BF16), pltpu.VMEM((s, LANES), BF16),
                        pltpu.VMEM((s // t, LANES, t), F32),
                        pltpu.VMEM((s // t, LANES, t), F32),
                        pltpu.VMEM((3, tq, LANES), BF16),
                        pltpu.VMEM((2 * t, t), BF16),
                        pltpu.VMEM((n_ch, t, LANES), BF16),
                        pltpu.VMEM((n_ch, t, LANES), BF16),
                        pltpu.VMEM((n_ch, LANES, t), BF16),
                        pltpu.VMEM((n_ch, LANES, t), BF16),
                        pltpu.VMEM((n_ch, t, t), F32),
                        pltpu.VMEM((n_ch, t, t), F32),
                        pltpu.VMEM((n_ch, t, 2 * t), BF16),
                        pltpu.VMEM((n_ch, t, t), F32),
                        pltpu.VMEM((n_ch, t, t), F32),
                        pltpu.VMEM((n_ch, t, t), F32),
                        pltpu.VMEM((n_ch, t, t), BF16),
                        pltpu.VMEM((n_ch, t, t), F32),
                        pltpu.VMEM((n_ch, t, 2 * t), BF16),
                        pltpu.VMEM((n_ch, t, t), F32),
                        pltpu.VMEM((n_ch, t, t), BF16),
                        pltpu.VMEM((n_ch, t, LANES), F32),
                        pltpu.VMEM((n_ch, t, LANES), F32),
                        pltpu.VMEM((n_ch, t, LANES), F32),
                        pltpu.VMEM((n_ch, t, LANES), F32),
                        pltpu.VMEM((n_ch, t, LANES), F32),
                        pltpu.SemaphoreType.DMA((3,))]
        + plan.sems,
        input_output_aliases=plan.aliases(6, 2),
        compiler_params=_params(("arbitrary", "arbitrary")),
    )(proj, proj, proj, o, dcat, g_attn, *plan.inputs)
    return res[0], res[1], res[2:]


def _place():
    return lax.axis_index("x"), lax.axis_index("y"), lax.axis_index("c")


def _other_chips(x, y):
    return [(1 - x, y), (x, 1 - y), (1 - x, 1 - y)]


def _slot(px, py, pc):
    return 4 * px + 2 * py + pc


def _all_gather(shards, out_dtypes):
    nw = len(shards)

    def body(*refs):
        ins, outs, stage = refs[:nw], refs[nw:2 * nw], refs[2 * nw:3 * nw]
        send_sems, recv_sems, local_sems = refs[3 * nw:]
        x, y, c = _place()
        me, sibling = (x, y, c), (x, y, 1 - c)
        chips = _other_chips(x, y)

        def copy(w, k, block, to, src=None):
            dst = outs[w].at[_slot(*block)]
            return pltpu.make_async_remote_copy(
                src_ref=dst if src is None else src, dst_ref=dst,
                send_sem=send_sems.at[w * 7 + k], recv_sem=recv_sems.at[w * 7 + k],
                device_id=to, device_id_type=MESH)

        started = []
        local = []
        for w in range(nw):
            stage[w][...] = ins[w][...].astype(stage[w].dtype)
            cp = pltpu.make_async_copy(stage[w], outs[w].at[_slot(*me)], local_sems.at[w])
            cp.start()
            local.append(cp)
            started.append(copy(w, 0, me, sibling, src=stage[w]))
            started[-1].start()
            for j, chip in enumerate(chips):
                started.append(copy(w, 1 + j, me, (*chip, c), src=stage[w]))
                started[-1].start()
        for j, chip in enumerate(chips):
            for w in range(nw):
                copy(w, 1 + j, (*chip, c), me).wait_recv()
                started.append(copy(w, 4 + j, (*chip, c), sibling))
                started[-1].start()
        for w in range(nw):
            copy(w, 0, sibling, me).wait_recv()
            for j, chip in enumerate(chips):
                copy(w, 4 + j, (*chip, 1 - c), me).wait_recv()
        for cp in started:
            cp.wait_send()
        for cp in local:
            cp.wait()

    return _pcall(
        body, name="all_gather_weights",
        in_specs=[pl.BlockSpec(memory_space=pltpu.VMEM)] * nw,
        out_specs=[pl.BlockSpec(memory_space=pl.ANY)] * nw,
        out_shape=[jax.ShapeDtypeStruct((N_DEV, *a.shape), d) for a, d in zip(shards, out_dtypes)],
        scratch_shapes=[pltpu.VMEM(a.shape, d) for a, d in zip(shards, out_dtypes)]
        + [pltpu.SemaphoreType.DMA((7 * nw,)), pltpu.SemaphoreType.DMA((7 * nw,)),
           pltpu.SemaphoreType.DMA((nw,))],
        compiler_params=_params(),
    )(*shards)


N_PEERS = N_DEV - 1


def _peer(k):
    x, y, c = _place()
    return (x ^ (k >> 2), y ^ ((k >> 1) & 1), c ^ (k & 1))


def _remote(src, dst, sems, index, to):
    return pltpu.make_async_remote_copy(src_ref=src, dst_ref=dst, send_sem=sems[0].at[index],
                                        recv_sem=sems[1].at[index], device_id=to, device_id_type=MESH)


def _gather_out_copies(staged, gathered, sems):
    x, y, c = _place()
    me = _slot(x, y, c)
    targets = [(x, y, 1 - c)] + [(*chip, c) for chip in _other_chips(x, y)]
    copies = []
    for w, (src, dst) in enumerate(zip(staged, gathered)):
        copies.append(pltpu.make_async_copy(src, dst.at[me], sems[2].at[w]))
        copies += [_remote(src, dst.at[me], sems, w * len(targets) + k, to) for k, to in enumerate(targets)]
    return copies


def _gather_pass_copies(arrived, gathered, sems):
    x, y, c = _place()
    chips = _other_chips(x, y)
    return [_remote(src.at[_slot(*chip, c)], dst.at[_slot(*chip, c)], sems, w * len(chips) + j, (x, y, 1 - c))
            for w, (src, dst) in enumerate(zip(arrived, gathered)) for j, chip in enumerate(chips)]


ALL_PEERS = tuple(range(1, N_DEV))


def _scatter_copies(partials, received, sems, peers=ALL_PEERS):
    me = _slot(*_place())
    return [_remote(src.at[me ^ k], dst.at[k - 1], sems, w * N_PEERS + k - 1, _peer(k))
            for w, (src, dst) in enumerate(zip(partials, received)) for k in peers]


class _Carried:
    def __init__(self, jobs):
        self.jobs = [(job[0], list(job[1]), job[2] if len(job) > 2 else ALL_PEERS) for job in jobs if len(job[1])]
        self.inputs, self.out_shapes, self.sems, self.counts = [], [], [], []
        for kind, arrays, _ in self.jobs:
            n_out = len(arrays) // 2 if kind == "scatter_more" else len(arrays)
            fan = {"gather_out": 4, "gather_pass": 3}.get(kind, N_PEERS)
            for a in arrays[len(arrays) - n_out:]:
                shape = {"gather_out": (N_DEV, *a.shape), "scatter": (N_PEERS, *a.shape[1:])}.get(kind, a.shape)
                self.out_shapes.append(jax.ShapeDtypeStruct(shape, BF16))
            job_sems = [pltpu.SemaphoreType.DMA((fan * n_out,))] * 2
            job_sems += [pltpu.SemaphoreType.DMA((n_out,))] if kind == "gather_out" else []
            self.inputs += arrays
            self.sems += job_sems
            self.counts.append((len(arrays), n_out, len(job_sems)))

    def aliases(self, first_input, first_output):
        pairs, at_in, at_out = {}, first_input, first_output
        for (kind, _, _), (n_in, n_out, _) in zip(self.jobs, self.counts):
            if kind in ("gather_pass", "scatter_more"):
                pairs.update({at_in + n_in - n_out + i: at_out + i for i in range(n_out)})
            at_in, at_out = at_in + n_in, at_out + n_out
        return pairs

    def copies(self, in_refs, out_refs, sem_refs):
        out, at_in, at_out, at_sem = [], 0, 0, 0
        for (kind, _, peers), (n_in, n_out, n_sems) in zip(self.jobs, self.counts):
            srcs, dsts = in_refs[at_in:at_in + n_out], out_refs[at_out:at_out + n_out]
            sems = sem_refs[at_sem:at_sem + n_sems]
            if kind == "gather_out":
                out += _gather_out_copies(srcs, dsts, sems)
            elif kind == "gather_pass":
                out += _gather_pass_copies(srcs, dsts, sems)
            else:
                out += _scatter_copies(srcs, dsts, sems, peers)
            at_in, at_out, at_sem = at_in + n_in, at_out + n_out, at_sem + n_sems
        return out


def _cast_shards(shards):
    def body(*refs):
        for src, dst in zip(refs[:len(shards)], refs[len(shards):]):
            dst[...] = src[...].astype(BF16)

    return _pcall(
        body, name="cast_shards",
        in_specs=[pl.BlockSpec(memory_space=pltpu.VMEM)] * len(shards),
        out_specs=[pl.BlockSpec(memory_space=pltpu.VMEM)] * len(shards),
        out_shape=[jax.ShapeDtypeStruct(a.shape, BF16) for a in shards],
        compiler_params=_params(),
    )(*shards)


def _all_reduce_small(packed):
    r = packed.shape[0]

    def body(x_ref, o_ref, gathered, send_sems, recv_sems):
        x, y, c = _place()
        me = _slot(x, y, c)
        gathered[me] = x_ref[...]
        copies = [_remote(x_ref, gathered.at[me], (send_sems, recv_sems), k - 1, _peer(k)) for k in ALL_PEERS]
        for cp in copies:
            cp.start()
        for cp in copies:
            cp.wait()
        total = gathered[0]
        for k in range(1, N_DEV):
            total = total + gathered[k]
        o_ref[...] = total

    return _pcall(
        body, name="all_reduce_small",
        in_specs=[pl.BlockSpec(memory_space=pltpu.VMEM)],
        out_specs=pl.BlockSpec(memory_space=pltpu.VMEM),
        out_shape=jax.ShapeDtypeStruct(packed.shape, F32),
        scratch_shapes=[pltpu.VMEM((N_DEV, r, LANES), F32),
                        pltpu.SemaphoreType.DMA((N_DEV - 1,)), pltpu.SemaphoreType.DMA((N_DEV - 1,))],
        compiler_params=_params(),
    )(packed)


def _adam_math(w, g, m, v):
    m = ADAM_B1 * m + (1.0 - ADAM_B1) * g
    v = ADAM_B2 * v + (1.0 - ADAM_B2) * jnp.square(g)
    m_hat = m / (1.0 - ADAM_B1 ** ADAM_STEP)
    v_hat = v / (1.0 - ADAM_B2 ** ADAM_STEP)
    delta = -ADAM_LR * (m_hat / (jnp.sqrt(v_hat) + ADAM_EPS) + ADAM_WD * w)
    return delta, m, v


ADAM_TILE_BYTES = 24 * 1024 * 1024


def _adam_sharded(name, own, received, w, m, v, place):
    r, cdim = w.shape
    row_bytes = 2 * cdim * (4 + 2 * N_PEERS + 3 * 4 + 4 * 4)
    tr = _tile(r, max(LANES, ADAM_TILE_BYTES // row_bytes // LANES * LANES)) if r % LANES == 0 else r

    def body(place_ref, own_ref, rec_ref, w_ref, m_ref, v_ref, g_ref, d_ref, nm_ref, nv_ref):
        del place_ref
        g = own_ref[...]
        for j in range(N_PEERS):
            g = g + rec_ref[j].astype(F32)
        delta, nm, nv = _adam_math(w_ref[...], g, m_ref[...], v_ref[...])
        g_ref[...] = g
        d_ref[...] = delta
        nm_ref[...] = nm
        nv_ref[...] = nv

    blk = pl.BlockSpec((tr, cdim), lambda i, pr: (i, 0))
    grid_spec = pltpu.PrefetchScalarGridSpec(
        num_scalar_prefetch=1, grid=(r // tr,),
        in_specs=[pl.BlockSpec((None, tr, cdim), lambda i, pr: (4 * pr[0] + 2 * pr[1] + pr[2], i, 0)),
                  pl.BlockSpec((N_PEERS, tr, cdim), lambda i, pr: (0, i, 0)), blk, blk, blk],
        out_specs=[blk] * 4)
    return _pcall(body, name=name, grid_spec=grid_spec,
                  out_shape=[jax.ShapeDtypeStruct((r, cdim), F32)] * 4,
                  compiler_params=_params(("parallel",)))(place, own, received, w, m, v)


def _adam_small(w, g, m, v):
    def body(w_ref, g_ref, m_ref, v_ref, d_ref, nm_ref, nv_ref):
        delta, nm, nv = _adam_math(w_ref[...], g_ref[...], m_ref[...], v_ref[...])
        d_ref[...] = delta
        nm_ref[...] = nm
        nv_ref[...] = nv

    return _pcall(body, name="adam_small",
                  in_specs=[pl.BlockSpec(memory_space=pltpu.VMEM)] * 4,
                  out_specs=[pl.BlockSpec(memory_space=pltpu.VMEM)] * 3,
                  out_shape=[jax.ShapeDtypeStruct(w.shape, F32)] * 3,
                  compiler_params=_params())(w, g, m, v)


def _rows(vec):
    return vec.reshape(-1, LANES)


def kernel(x, p, g_mix, w_in, conv_w, g_conv_out, g_attn_out, w_out, g_mlp, w_up, w_down, g_ple, w_ple_gate, w_ple_proj, g_final, loss_target, m_g_mix, m_w_in, m_conv_w, m_g_conv_out, m_g_attn_out, m_w_out, m_g_mlp, m_w_up, m_w_down, m_g_ple, m_w_ple_gate, m_w_ple_proj, m_g_final, v_g_mix, v_w_in, v_conv_w, v_g_conv_out, v_g_attn_out, v_w_out, v_g_mlp, v_w_up, v_w_down, v_g_ple, v_w_ple_gate, v_w_ple_proj, v_g_final):
    s, d = x.shape[1], x.shape[2]
    w_conv = g_conv_out.shape[1]
    w_attn = g_attn_out.shape[1]
    cw = conv_w.shape[2]
    xs, ps, tgt = x[0], p[0, 0], loss_target[0]
    place = jnp.stack([lax.axis_index("x"), lax.axis_index("y"), lax.axis_index("c")]).astype(jnp.int32)
    my_slot = 4 * place[0] + 2 * place[1] + place[2]

    conv_tile = jnp.pad(conv_w[0], ((0, HALO - CONV_K), (0, LANES - cw)))
    big = [w_in[0], w_out[0], w_up[0], w_down[0], w_ple_gate[0], w_ple_proj[0]]
    win_g, conv_g = _all_gather([big[0], conv_tile], [BF16, F32])
    s_out, s_up, s_down, s_gate, s_proj = _cast_shards(big[1:])
    conv_full = jnp.transpose(conv_g[:, :CONV_K, :cw], (1, 0, 2)).reshape(CONV_K, w_conv)
    in_shard, up_shard, proj_shard = big[0].shape[1], big[2].shape[1], big[5].shape[1]

    proj, a, g_out, g_gate, g_proj = _mm_nn("in_proj", xs, win_g, n_shard=in_shard, tn=2 * in_shard, tm=2048,
                                            lhs_norm=g_mix, carry=[("gather_out", [s_out, s_gate, s_proj])])
    cat = _conv_fwd(proj, conv_full, g_conv_out, w_conv, d)
    o, cat, (g_up, g_down, wout_g, wgate_g, wproj_g) = _attn_fwd(
        proj, g_attn_out, cat, w_conv,
        [("gather_out", [s_up, s_down]), ("gather_pass", [g_out, g_gate, g_proj])])
    wout_f = wout_g.reshape(-1, wout_g.shape[-1])
    wgate_f = wgate_g.reshape(-1, wgate_g.shape[-1])
    h1, wup_g = _mm_nn("out_proj", cat, wout_f, epilogue=_ep_residual, extras=(xs,),
                       carry=[("gather_pass", [g_up])])
    act, mn, wdown_g = _mm_nn("mlp_up", h1, wup_g, n_shard=up_shard, epilogue=_ep_up, out_dtypes=(BF16,), tm=2048,
                              lhs_norm=g_mlp, carry=[("gather_pass", [g_down])])
    wdown_f = wdown_g.reshape(-1, wdown_g.shape[-1])
    h2, = _mm_nn("mlp_down", act, wdown_f, epilogue=_ep_residual, extras=(h1,), tk=2048)
    pp = _ple_proj(ps, wproj_g)
    loss_part, dh3, dgl, dpp, dg_final, n3 = _ple_gate_loss(h2, g_ple, wgate_f, pp, tgt, g_final.reshape(1, d))

    def slots(t2d):
        return t2d.reshape(N_DEV, -1, t2d.shape[-1])

    dw_proj = _d_ple_proj(ps, dpp, proj_shard)
    dw_gate = [slots(t) for t in _mm_tn("d_w_ple_gate", n3, dgl)]
    dh2, dh2b, dg_ple = _mm_nt_norm_bwd("d_norm_ple", dgl, wgate_f, h2, g_ple, dh3)
    du, gate_recv, proj_recv = _mm_nt("d_mlp_act", dh2b, wdown_f, epilogue=_ep_dact, out_dtypes=(BF16,),
                                      extras=(act,), tm=2048, carry=[("scatter", [dw_gate[1], dw_proj[1]])])
    dw_down = [slots(t) for t in _mm_tn("d_w_down", act, dh2b)]
    near, far = (1, 2, 3, 4, 5), (6, 7)
    dw_up = _mm_tn("d_w_up", mn, du, n_shard=up_shard)
    dh1, dh1b, dg_mlp, down_part = _mm_nt_norm_bwd(
        "d_norm_mlp", du, wup_g, h1, g_mlp, dh2, k_shard=up_shard, tm=1024,
        carry=[("scatter", [dw_down[1]], near)])
    dcat, = _mm_nt("d_cat", dh1b, wout_f)
    dw_out = [slots(t) for t in _mm_tn("d_w_out", cat, dh1b)]
    dproj, dg_attn, (down_recv, up_recv) = _attn_bwd(
        proj, o, dcat, g_attn_out, w_conv,
        [("scatter_more", [dw_down[1], down_part], far), ("scatter", [dw_up[1]])])
    dproj, dconv, dg_conv = _conv_bwd(proj, dcat, conv_full, g_conv_out, dproj, w_conv)
    *dw_in, out_recv = _mm_tn("d_w_in", a, dproj, n_shard=in_shard, tn=in_shard,
                              carry=[("scatter", [dw_out[1]])])
    grad_x, _, dg_mix, in_recv = _mm_nt_norm_bwd("d_norm_mix", dproj, win_g, xs, g_mix, dh1, k_shard=in_shard,
                                                 tk=2 * in_shard, tm=1024, carry=[("scatter", [dw_in[1]])])

    names = ["w_in", "w_out", "w_up", "w_down", "w_ple_gate", "w_ple_proj"]
    owns = [dw_in[0], dw_out[0], dw_up[0], dw_down[0], dw_gate[0], dw_proj[0]]
    recvs = [in_recv, out_recv, up_recv, down_recv, gate_recv, proj_recv]
    moments = [(m_w_in, v_w_in), (m_w_out, v_w_out), (m_w_up, v_w_up), (m_w_down, v_w_down),
               (m_w_ple_gate, v_w_ple_gate), (m_w_ple_proj, v_w_ple_proj)]
    big_out = {}
    for n, own, rc, wt, (mm, vv) in zip(names, owns, recvs, big, moments):
        big_out[n] = [t[None] for t in _adam_sharded("adam_" + n, own, rc, wt, mm[0], vv[0], place)]

    n_conv_rows = CONV_K * w_conv // LANES
    small_g = jnp.concatenate(
        [_rows(dg_mix[0]), _rows(dg_conv[0]), _rows(dg_attn[0]), _rows(dg_mlp[0]), _rows(dg_ple[0]),
         _rows(dg_final[0]), _rows(dconv.reshape(-1)), loss_part], axis=0)
    n_gain_rows = small_g.shape[0] - n_conv_rows - 1
    pad_rows = (-small_g.shape[0]) % HALO
    small_g = _all_reduce_small(jnp.pad(small_g, ((0, pad_rows), (0, 0))))
    loss = small_g[n_gain_rows + n_conv_rows, 0]
    dconv_full = small_g[n_gain_rows:n_gain_rows + n_conv_rows].reshape(CONV_K, w_conv)
    dconv_mine = lax.dynamic_slice(dconv_full, (0, my_slot * cw), (CONV_K, cw))

    def pack(vecs, conv_part):
        rows = [_rows(t.reshape(-1)) for t in vecs]
        rows.append(jnp.pad(conv_part, ((0, HALO - CONV_K), (0, LANES - cw))))
        return jnp.concatenate(rows, axis=0)

    gains = [g_mix, g_conv_out, g_attn_out, g_mlp, g_ple, g_final]
    gains_m = [m_g_mix, m_g_conv_out, m_g_attn_out, m_g_mlp, m_g_ple, m_g_final]
    gains_v = [v_g_mix, v_g_conv_out, v_g_attn_out, v_g_mlp, v_g_ple, v_g_final]
    gpack = jnp.concatenate([small_g[:n_gain_rows], jnp.pad(dconv_mine, ((0, HALO - CONV_K), (0, LANES - cw)))], axis=0)
    sd, sm, sv = _adam_small(pack(gains, conv_w[0]), gpack, pack(gains_m, m_conv_w[0]), pack(gains_v, v_conv_w[0]))

    def unpack(packed):
        out, r0 = [], 0
        for t in gains:
            nr = t.size // LANES
            out.append(packed[r0:r0 + nr].reshape(t.shape))
            r0 += nr
        out.append(packed[r0:r0 + CONV_K, :cw][None])
        return out

    sg_l, sd_l, sm_l, sv_l = unpack(gpack), unpack(sd), unpack(sm), unpack(sv)
    small_names = ["g_mix", "g_conv_out", "g_attn_out", "g_mlp", "g_ple", "g_final", "conv_w"]
    small_out = {n: [sg_l[i], sd_l[i], sm_l[i], sv_l[i]] for i, n in enumerate(small_names)}

    order = ["g_mix", "w_in", "conv_w", "g_conv_out", "g_attn_out", "w_out", "g_mlp", "w_up", "w_down",
             "g_ple", "w_ple_gate", "w_ple_proj", "g_final"]
    table = {**big_out, **small_out}
    outs = [loss, grad_x[None]]
    for kind in range(4):
        outs.extend(table[n][kind] for n in order)
    return tuple(outs)
```

```python
import jax
import jax.numpy as jnp
from jax import lax
from jax.experimental import pallas as pl
from jax.experimental.pallas import tpu as pltpu

F32 = jnp.float32
BF16 = jnp.bfloat16
EPS = 1e-6
HEAD_DIM = 64
LANES = 128
CONV_K = 3
MXU_WIDTH = 256
ATTN_BLOCK = MXU_WIDTH
HALO = 8
N_DEV = 8
MESH = pl.DeviceIdType.MESH
VMEM_LIMIT = 56 * 1024 * 1024

ADAM_LR = 0.001
ADAM_B1 = 0.9
ADAM_B2 = 0.999
ADAM_EPS = 1e-08
ADAM_WD = 0.01
ADAM_STEP = 10


def _pcall(body, **kw):
    return pl.pallas_call(body, **kw)


def _params(sem=None, **kw):
    return pltpu.CompilerParams(dimension_semantics=sem, vmem_limit_bytes=VMEM_LIMIT, **kw)


def _tile(dim, pref):
    t = min(dim, pref)
    while dim % t:
        t -= LANES
    assert t > 0, (dim, pref)
    return t


_NN = (((1,), (0,)), ((), ()))
_NT = (((1,), (1,)), ((), ()))
_TN = (((0,), (0,)), ((), ()))


def _ep_store(acc, outs):
    outs[0][...] = acc.astype(outs[0].dtype)


def _ep_both(acc, outs):
    outs[0][...] = acc
    outs[1][...] = acc.astype(BF16)


def _ep_residual(acc, res, outs):
    outs[0][...] = acc + res[...]


def _ep_up(acc, outs):
    outs[0][...] = jnp.square(jnp.maximum(acc, 0.0)).astype(BF16)


def _ep_dact(acc, act, outs):
    outs[0][...] = (acc * (2.0 * jnp.sqrt(act[...].astype(F32)))).astype(BF16)


def _row_chunked(epilogue):
    def run(acc, *rest):
        *ex, outs = rest
        n = acc.shape[0]
        for m0 in range(0, n, MXU_WIDTH):
            rows = slice(m0, min(m0 + MXU_WIDTH, n))
            pick = lambda ref: ref.at[rows, :] if ref.shape[0] == n else ref
            epilogue(acc[rows, :], *[pick(e) for e in ex], [pick(o) for o in outs])
    return run


def _ep_norm_bwd(acc, h, g, dres, outs):
    hv = h[...]
    r = lax.rsqrt(jnp.mean(hv * hv, axis=-1, keepdims=True) + EPS)
    hn = hv * r
    outs[2][...] += jnp.sum(acc * hn, axis=0, keepdims=True)
    dhn = acc * g[...]
    dh = dres[...] + r * (dhn - hn * jnp.mean(dhn * hn, axis=-1, keepdims=True))
    outs[0][...] = dh
    outs[1][...] = dh.astype(BF16)


def _matmul(name, a, b, *, dims, grid, a_spec, b_spec, acc_shape, out_shapes, out_specs,
            epilogue=_ep_store, extras=(), extra_specs=(), carry=(), sequential=False, lhs_norm=False):
    nk = grid[2]
    plan = _Carried(carry)
    n_ex, n_out, n_xc, n_xo = len(extras), len(out_shapes), len(plan.inputs), len(plan.out_shapes)
    n_sems = len(plan.sems)
    last = tuple(g - 1 for g in grid)
    assert not lhs_norm or nk == 1

    def product(a_ref, b_ref):
        if len(b_ref.shape) == 2:
            return lax.dot_general(a_ref[...].astype(BF16), b_ref[...].astype(BF16), dims,
                                   preferred_element_type=F32)
        width = b_ref.shape[2]
        return sum(lax.dot_general(a_ref[:, g * width:(g + 1) * width].astype(BF16), b_ref[g].astype(BF16), dims,
                                   preferred_element_type=F32) for g in range(b_ref.shape[0]))

    def body(a_ref, b_ref, *rest):
        ex, rest = rest[:n_ex], rest[n_ex:]
        partials, rest = rest[:n_xc], rest[n_xc:]
        outs, rest = rest[:n_out], rest[n_out:]
        received, rest = rest[:n_xo], rest[n_xo:]
        ids = [pl.program_id(axis) for axis in range(3)]
        if n_xc:
            @pl.when((ids[0] == 0) & (ids[1] == 0) & (ids[2] == 0))
            def _():
                for cp in plan.copies(partials, received, rest[-n_sems:]):
                    cp.start()

        if lhs_norm:
            x_ref, a_ref, gain, ex, outs = a_ref, outs[-1], ex[-1], ex[:-1], outs[:-1]

            @pl.when(ids[1] == 0)
            def _():
                for m0 in range(0, acc_shape[0], MXU_WIDTH):
                    rows = slice(m0, min(m0 + MXU_WIDTH, acc_shape[0]))
                    xv = x_ref[rows, :]
                    r = lax.rsqrt(jnp.mean(xv * xv, axis=-1, keepdims=True) + EPS)
                    a_ref[rows, :] = (xv * r * gain[...]).astype(BF16)

        if nk == 1 and not sequential and dims != _TN:
            if len(b_ref.shape) == 3:
                ns = b_ref.shape[2]
                pieces = [(b_ref.at[g, :, n0:min(n0 + MXU_WIDTH, ns)], slice(g * ns + n0, g * ns + min(n0 + MXU_WIDTH, ns)))
                          for g in range(b_ref.shape[0]) for n0 in range(0, ns, MXU_WIDTH)]
            else:
                spans = [slice(n0, min(n0 + MXU_WIDTH, acc_shape[1])) for n0 in range(0, acc_shape[1], MXU_WIDTH)]
                pieces = [(b_ref.at[cols, :] if dims == _NT else b_ref.at[:, cols], cols) for cols in spans]
            for b_cols, cols in pieces:
                for m0 in range(0, acc_shape[0], MXU_WIDTH):
                    rows = slice(m0, min(m0 + MXU_WIDTH, acc_shape[0]))
                    epilogue(product(a_ref.at[rows, :], b_cols), *[e.at[rows, cols] for e in ex],
                             [o.at[rows, cols] for o in outs])
        else:
            if sequential:
                @pl.when((ids[0] == 0) & (ids[2] == 0))
                def _():
                    for o in outs:
                        if o.shape[0] != acc_shape[0]:
                            o[...] = jnp.zeros_like(o)

            if nk == 1:
                _row_chunked(epilogue)(product(a_ref, b_ref), *ex, outs)
            else:
                acc = rest[0]

                @pl.when(ids[2] == 0)
                def _():
                    acc[...] = product(a_ref, b_ref)

                @pl.when(ids[2] > 0)
                def _():
                    acc[...] += product(a_ref, b_ref)

                @pl.when(ids[2] == nk - 1)
                def _():
                    _row_chunked(epilogue)(acc, *ex, outs)

        if n_xc:
            @pl.when((ids[0] == last[0]) & (ids[1] == last[1]) & (ids[2] == last[2]))
            def _():
                for cp in plan.copies(partials, received, rest[-n_sems:]):
                    cp.wait()

    anywhere = pl.BlockSpec(memory_space=pl.ANY)
    return _pcall(
        body, name=name, grid=grid,
        in_specs=[a_spec, b_spec, *extra_specs, *[anywhere] * n_xc],
        out_specs=[*out_specs, *[anywhere] * n_xo],
        out_shape=[*out_shapes, *plan.out_shapes],
        scratch_shapes=([] if nk == 1 else [pltpu.VMEM(acc_shape, F32)]) + plan.sems,
        input_output_aliases=plan.aliases(2 + n_ex, n_out),
        compiler_params=_params(("arbitrary",) * 3 if n_xc or sequential or lhs_norm
                                else ("parallel", "parallel", "arbitrary")),
    )(a, b, *extras, *plan.inputs)


_NO_CARRY = ()


def _mm_nn(name, a, w, *, n_shard=None, epilogue=_ep_store, out_dtypes=(F32,), extras=(), carry=_NO_CARRY,
           lhs_norm=None, tm=1024, tn=1024, tk=1024):
    m, kd = a.shape
    if lhs_norm is not None:
        tk = kd
    if n_shard is None:
        n = w.shape[1]
        tn = _tile(n, tn)
        tk = _tile(kd, tk)
        b_spec = pl.BlockSpec((tk, tn), lambda i, j, k: (k, j))
    elif tn >= 2 * n_shard and tk >= kd:
        n = N_DEV * n_shard
        group = min(tn // n_shard, N_DEV)
        while N_DEV % group:
            group -= 1
        tn, tk = group * n_shard, kd
        b_spec = pl.BlockSpec((group, tk, n_shard), lambda i, j, k: (j, 0, 0))
    else:
        n = N_DEV * n_shard
        tn = _tile(n_shard, tn)
        tk = _tile(kd, tk)
        per = n_shard // tn
        b_spec = pl.BlockSpec((None, tk, tn), lambda i, j, k: (j // per, k, j % per))
    tm = _tile(m, tm)
    o_spec = pl.BlockSpec((tm, tn), lambda i, j, k: (i, j))
    out_shapes = [jax.ShapeDtypeStruct((m, n), d) for d in out_dtypes]
    out_specs = [o_spec] * len(out_dtypes)
    extra_specs = [o_spec] * len(extras)
    if lhs_norm is not None:
        extras = (*extras, lhs_norm)
        extra_specs.append(pl.BlockSpec((1, kd), lambda i, j, k: (0, 0)))
        out_shapes.append(jax.ShapeDtypeStruct((m, kd), BF16))
        out_specs.append(pl.BlockSpec((tm, kd), lambda i, j, k: (i, 0)))
    return _matmul(
        name, a, w, dims=_NN, grid=(m // tm, n // tn, kd // tk),
        a_spec=pl.BlockSpec((tm, tk), lambda i, j, k: (i, k)), b_spec=b_spec,
        acc_shape=(tm, tn), out_shapes=out_shapes, out_specs=out_specs,
        epilogue=epilogue, extras=extras, extra_specs=extra_specs, carry=carry, lhs_norm=lhs_norm is not None)


def _mm_nt(name, a, w, *, epilogue=_ep_store, out_dtypes=(F32,), extras=(), carry=_NO_CARRY,
           tm=1024, tn=1024, tk=1024):
    m, kd = a.shape
    n = w.shape[0]
    tm, tn, tk = _tile(m, tm), _tile(n, tn), _tile(kd, tk)
    o_spec = pl.BlockSpec((tm, tn), lambda i, j, k: (i, j))
    return _matmul(
        name, a, w, dims=_NT, grid=(m // tm, n // tn, kd // tk),
        a_spec=pl.BlockSpec((tm, tk), lambda i, j, k: (i, k)),
        b_spec=pl.BlockSpec((tn, tk), lambda i, j, k: (j, k)),
        acc_shape=(tm, tn),
        out_shapes=[jax.ShapeDtypeStruct((m, n), d) for d in out_dtypes],
        out_specs=[o_spec] * len(out_dtypes),
        epilogue=epilogue, extras=extras, extra_specs=[o_spec] * len(extras), carry=carry)


def _mm_nt_norm_bwd(name, a, w, h, g, dres, *, k_shard=None, carry=_NO_CARRY, tm=512, tk=1024):
    m, kd = a.shape
    n = h.shape[1]
    if k_shard is None:
        tk = _tile(kd, tk)
        b_spec = pl.BlockSpec((n, tk), lambda i, j, k: (0, k))
    else:
        group = max(1, min(tk // k_shard, N_DEV))
        while N_DEV % group:
            group -= 1
        tk = group * k_shard
        b_spec = pl.BlockSpec((group, n, k_shard), lambda i, j, k: (k, 0, 0))
    tm = _tile(m, tm)
    rows = pl.BlockSpec((tm, n), lambda i, j, k: (i, 0))
    vec = pl.BlockSpec((1, n), lambda i, j, k: (0, 0))
    return _matmul(
        name, a, w, dims=_NT, grid=(m // tm, 1, kd // tk),
        a_spec=pl.BlockSpec((tm, tk), lambda i, j, k: (i, k)), b_spec=b_spec, acc_shape=(tm, n),
        out_shapes=[jax.ShapeDtypeStruct((m, n), F32), jax.ShapeDtypeStruct((m, n), BF16),
                    jax.ShapeDtypeStruct((1, n), F32)],
        out_specs=[rows, rows, vec], epilogue=_ep_norm_bwd,
        extras=(h, g, dres), extra_specs=[rows, vec, rows], carry=carry, sequential=True)


TN_TILE_BYTES = 40 * 1024 * 1024


def _mm_tn(name, a, b, *, n_shard=None, carry=_NO_CARRY, tm=1024, tn=1024):
    t, m = a.shape
    n = b.shape[1]
    tm = _tile(m, tm)
    tn = _tile(n if n_shard is None else n_shard, tn)
    tk = t
    while 2 * 2 * tk * (tm + tn) + 4 * tm * tn * 5 > TN_TILE_BYTES and tk % (2 * LANES) == 0:
        tk //= 2
    if n_shard is None:
        o_spec = pl.BlockSpec((tm, tn), lambda i, j, k: (i, j))
        shape = (m, n)
    else:
        per = n_shard // tn
        o_spec = pl.BlockSpec((None, tm, tn), lambda i, j, k: (j // per, i, j % per))
        shape = (N_DEV, m, n_shard)
    return _matmul(
        name, a, b, dims=_TN, grid=(m // tm, n // tn, t // tk),
        a_spec=pl.BlockSpec((tk, tm), lambda i, j, k: (k, i)),
        b_spec=pl.BlockSpec((tk, tn), lambda i, j, k: (k, j)),
        acc_shape=(tm, tn), epilogue=_ep_both, carry=carry,
        out_shapes=[jax.ShapeDtypeStruct(shape, F32), jax.ShapeDtypeStruct(shape, BF16)],
        out_specs=[o_spec, o_spec])


def _ple_proj(p, w_g, tm=1024):
    s, kd = p.shape
    ns = w_g.shape[2]
    tm = _tile(s, tm)

    def body(p_ref, w_ref, o_ref):
        pv = p_ref[...].astype(BF16)
        for j in range(N_DEV):
            o_ref[:, j * ns:(j + 1) * ns] = jnp.dot(pv, w_ref[j], preferred_element_type=F32)

    return _pcall(body, name="ple_proj", grid=(s // tm,),
                  in_specs=[pl.BlockSpec((tm, kd), lambda i: (i, 0)),
                            pl.BlockSpec((N_DEV, kd, ns), lambda i: (0, 0, 0))],
                  out_specs=pl.BlockSpec((tm, N_DEV * ns), lambda i: (i, 0)),
                  out_shape=jax.ShapeDtypeStruct((s, N_DEV * ns), F32),
                  compiler_params=_params(("parallel",)))(p, w_g)


def _d_ple_proj(p, dpp, ns, tk=1024):
    s, kd = p.shape
    tk = _tile(s, tk)
    nk = s // tk

    def body(p_ref, d_ref, of_ref, ob_ref, acc):
        k = pl.program_id(0)

        @pl.when(k == 0)
        def _():
            acc[...] = jnp.zeros_like(acc)

        pv = p_ref[...].astype(BF16)
        for j in range(N_DEV):
            acc[j] += lax.dot_general(pv, d_ref[:, j * ns:(j + 1) * ns], _TN, preferred_element_type=F32)

        @pl.when(k == nk - 1)
        def _():
            of_ref[...] = acc[...]
            ob_ref[...] = acc[...].astype(BF16)

    whole = pl.BlockSpec((N_DEV, kd, ns), lambda k: (0, 0, 0))
    return _pcall(body, name="d_w_ple_proj", grid=(nk,),
                  in_specs=[pl.BlockSpec((tk, kd), lambda k: (k, 0)),
                            pl.BlockSpec((tk, N_DEV * ns), lambda k: (k, 0))],
                  out_specs=[whole, whole],
                  out_shape=[jax.ShapeDtypeStruct((N_DEV, kd, ns), F32), jax.ShapeDtypeStruct((N_DEV, kd, ns), BF16)],
                  scratch_shapes=[pltpu.VMEM((N_DEV, kd, ns), F32)],
                  compiler_params=_params(("arbitrary",)))(p, dpp)


def _ep_ple_loss(gl, h2, pp, tgt, g_final, outs):
    loss_ref, dh3_ref, dgl_ref, dpp_ref, dg_ref = outs
    gate = jax.nn.sigmoid(gl)
    ppv = pp[...]
    h3 = h2[...] + gate * ppv
    r = lax.rsqrt(jnp.mean(h3 * h3, axis=-1, keepdims=True) + EPS)
    hn = h3 * r
    gv = g_final[...]
    diff = hn * gv - tgt[...]
    row = jnp.mean(diff * diff, axis=-1, keepdims=True)
    loss_ref[...] += 0.5 * jnp.sum(row, axis=0, keepdims=True)
    dy = diff * (1.0 / h3.shape[-1])
    dg_ref[...] += jnp.sum(dy * hn, axis=0, keepdims=True)
    dhn = dy * gv
    dh3 = r * (dhn - hn * jnp.mean(dhn * hn, axis=-1, keepdims=True))
    dh3_ref[...] = dh3
    dgl_ref[...] = (dh3 * ppv * gate * (1.0 - gate)).astype(BF16)
    dpp_ref[...] = (dh3 * gate).astype(BF16)


def _ple_gate_loss(h2, g_ple, w_gate, pp, tgt, g_final, tm=512):
    s, d = h2.shape
    tm = _tile(s, tm)
    rows = pl.BlockSpec((tm, d), lambda i, j, k: (i, 0))
    vec = pl.BlockSpec((1, d), lambda i, j, k: (0, 0))
    return _matmul(
        "ple_gate_loss", h2, w_gate, dims=_NN, grid=(s // tm, 1, 1),
        a_spec=rows, b_spec=pl.BlockSpec((d, d), lambda i, j, k: (0, 0)), acc_shape=(tm, d),
        out_shapes=[jax.ShapeDtypeStruct((1, LANES), F32), jax.ShapeDtypeStruct((s, d), F32),
                    jax.ShapeDtypeStruct((s, d), BF16), jax.ShapeDtypeStruct((s, d), BF16),
                    jax.ShapeDtypeStruct((1, d), F32), jax.ShapeDtypeStruct((s, d), BF16)],
        out_specs=[pl.BlockSpec((1, LANES), lambda i, j, k: (0, 0)), rows, rows, rows, vec, rows],
        epilogue=_ep_ple_loss, extras=(h2, pp, tgt, g_final, g_ple), extra_specs=[rows, rows, rows, vec, vec],
        sequential=True, lhs_norm=True)


def _low_half():
    return lax.broadcasted_iota(jnp.int32, (1, LANES), 1) < HEAD_DIM


def _half_mean(v, low):
    s_lo = jnp.sum(jnp.where(low, v, 0.0), axis=-1, keepdims=True)
    s_hi = jnp.sum(jnp.where(low, 0.0, v), axis=-1, keepdims=True)
    return jnp.where(low, s_lo, s_hi) * (1.0 / HEAD_DIM)


def _head_norm_bwd(val, dout, g, low):
    r = lax.rsqrt(_half_mean(val * val, low) + EPS)
    vn = val * r
    dvn = dout * g
    return r * (dvn - vn * _half_mean(dvn * vn, low)), dout * vn


def _conv_taps(vv_ext, w_ref):
    v0 = vv_ext[HALO:]
    v1 = pltpu.roll(vv_ext, 1, 0)[HALO:]
    v2 = pltpu.roll(vv_ext, 2, 0)[HALO:]
    return w_ref[2:3, :] * v0 + w_ref[1:2, :] * v1 + w_ref[0:1, :] * v2, (v0, v1, v2)


def _conv_fwd(proj, conv_w, g_conv, w_conv, d_model, tr=1024):
    s = proj.shape[0]
    tr = _tile(s, tr)
    hb = tr // HALO

    def main(part):
        return pl.BlockSpec((tr, w_conv), lambda i: (i, part))

    def prev(part):
        return pl.BlockSpec((HALO, w_conv), lambda i: (jnp.maximum(i * hb - 1, 0), part))

    def body(cb_ref, cc_ref, cu_ref, ccp_ref, cup_ref, w_ref, g_ref, o_ref):
        i = pl.program_id(0)
        low = _low_half()
        for j in range(w_conv // LANES):
            cols = slice(j * LANES, (j + 1) * LANES)
            vv_prev = jnp.where(i > 0, ccp_ref[:, cols] * cup_ref[:, cols], 0.0)
            vv_ext = jnp.concatenate([vv_prev, cc_ref[:, cols] * cu_ref[:, cols]], axis=0)
            y, _ = _conv_taps(vv_ext, w_ref.at[:, cols])
            co = cb_ref[:, cols] * y
            r = lax.rsqrt(_half_mean(co * co, low) + EPS)
            o_ref[:, cols] = (co * r * g_ref[:, cols]).astype(BF16)

    return _pcall(
        body, name="conv_fwd", grid=(s // tr,),
        in_specs=[main(0), main(1), main(2), prev(1), prev(2),
                  pl.BlockSpec((CONV_K, w_conv), lambda i: (0, 0)),
                  pl.BlockSpec((1, w_conv), lambda i: (0, 0))],
        out_specs=pl.BlockSpec((tr, w_conv), lambda i: (i, 0)),
        out_shape=jax.ShapeDtypeStruct((s, d_model), BF16),
        compiler_params=_params(("parallel",)),
    )(proj, proj, proj, proj, proj, conv_w, g_conv)


def _conv_bwd(proj, dcat, conv_w, g_conv, dproj, w_conv, tr=1024):
    s = proj.shape[0]
    tr = _tile(s, tr)
    hb = tr // HALO
    last = s // HALO - 1
    nt = s // tr

    def main(part):
        return pl.BlockSpec((tr, w_conv), lambda i: (i, part))

    def prev(part):
        return pl.BlockSpec((HALO, w_conv), lambda i: (jnp.maximum(i * hb - 1, 0), part))

    def nxt(part):
        return pl.BlockSpec((HALO, w_conv), lambda i: (jnp.minimum((i + 1) * hb, last), part))

    def body(cb_ref, cc_ref, cu_ref, dc_ref, ccp_ref, cup_ref, cbn_ref, ccn_ref, cun_ref, dcn_ref,
             w_ref, g_ref, dproj_in, dproj_ref, dw_ref, dg_ref):
        del dproj_in
        i = pl.program_id(0)

        @pl.when(i == 0)
        def _():
            dw_ref[...] = jnp.zeros_like(dw_ref)
            dg_ref[...] = jnp.zeros_like(dg_ref)

        low = _low_half()
        n_ext = tr + HALO
        rowid = lax.broadcasted_iota(jnp.int32, (n_ext, 1), 0)
        for j in range(w_conv // LANES):
            cols = slice(j * LANES, (j + 1) * LANES)
            wj = w_ref.at[:, cols]
            cc, cu = cc_ref[:, cols], cu_ref[:, cols]
            vv_prev = jnp.where(i > 0, ccp_ref[:, cols] * cup_ref[:, cols], 0.0)
            vv_ext = jnp.concatenate([vv_prev, cc * cu, ccn_ref[:, cols] * cun_ref[:, cols]], axis=0)
            y_ext, (v0, v1, v2) = _conv_taps(vv_ext, wj)
            cb_ext = jnp.concatenate([cb_ref[:, cols], cbn_ref[:, cols]], axis=0)
            dc_ext = jnp.concatenate([dc_ref[:, cols], dcn_ref[:, cols]], axis=0)
            dco, dgn = _head_norm_bwd(cb_ext * y_ext, dc_ext, g_ref[:, cols], low)
            dyc = jnp.where((rowid < tr) | (i < nt - 1), dco * cb_ext, 0.0)
            dvv = (wj[2:3, :] * dyc[:tr] + wj[1:2, :] * pltpu.roll(dyc, n_ext - 1, 0)[:tr]
                   + wj[0:1, :] * pltpu.roll(dyc, n_ext - 2, 0)[:tr])
            dproj_ref[:, cols] = (dco[:tr] * y_ext[:tr]).astype(BF16)
            dproj_ref[:, w_conv + j * LANES:w_conv + (j + 1) * LANES] = (dvv * cu).astype(BF16)
            dproj_ref[:, 2 * w_conv + j * LANES:2 * w_conv + (j + 1) * LANES] = (dvv * cc).astype(BF16)
            dyt = dyc[:tr]
            for tap, shifted in enumerate((v2, v1, v0)):
                dw_ref[tap:tap + 1, cols] += jnp.sum(dyt * shifted[:tr], axis=0, keepdims=True)
            dg_ref[:, cols] += jnp.sum(dgn[:tr], axis=0, keepdims=True)

    n_cols = dproj.shape[1]
    return _pcall(
        body, name="conv_bwd", grid=(nt,),
        in_specs=[main(0), main(1), main(2), main(0),
                  prev(1), prev(2), nxt(0), nxt(1), nxt(2), nxt(0),
                  pl.BlockSpec((CONV_K, w_conv), lambda i: (0, 0)),
                  pl.BlockSpec((1, w_conv), lambda i: (0, 0)),
                  pl.BlockSpec(memory_space=pl.ANY)],
        out_specs=[pl.BlockSpec((tr, 3 * w_conv), lambda i: (i, 0)),
                   pl.BlockSpec((CONV_K, w_conv), lambda i: (0, 0)),
                   pl.BlockSpec((1, w_conv), lambda i: (0, 0))],
        out_shape=[jax.ShapeDtypeStruct((s, n_cols), BF16),
                   jax.ShapeDtypeStruct((CONV_K, w_conv), F32),
                   jax.ShapeDtypeStruct((1, w_conv), F32)],
        input_output_aliases={12: 0},
        compiler_params=_params(("arbitrary",)),
    )(proj, proj, proj, dcat, proj, proj, proj, proj, proj, dcat, conv_w, g_conv, dproj)


STRIP = 16

ALL_CHAINS = (0, 1, 2, 3)
UPPER_CHAINS = (2, 3)


RUN_FLOOR = -104.0


def _any_weight_left(run_s):
    return (jnp.max(run_s[...]) > RUN_FLOOR).astype(jnp.int32)


def _chains(low):
    return [(2 * half + h, half, msk) for half in range(2)
            for h, msk in enumerate((low, jnp.logical_not(low)))]


def _suffix_operator(t):
    r = lax.broadcasted_iota(jnp.int32, (2 * t, t), 0)
    c = lax.broadcasted_iota(jnp.int32, (2 * t, t), 1)
    return jnp.where((r > c) & ((r < t) | (r - t > c)), 1.0, 0.0).astype(BF16)


def _strips(t, diag):
    return [(i, slice(i * STRIP, (i + 1) * STRIP), t // 2 if diag and (i + 1) * STRIP <= t // 2 else t)
            for i in range(t // STRIP)]


def _strip_mask(i, w):
    r = lax.broadcasted_iota(jnp.int32, (STRIP, w), 0) + i * STRIP
    c = lax.broadcasted_iota(jnp.int32, (STRIP, w), 1)
    return r > c


def _store_trimmed(ref, rows, val, w, t, at=0):
    ref[rows, at:at + w] = val
    if w < t:
        ref[rows, at + w:at + t] = jnp.zeros((STRIP, t - w), val.dtype)


def _store_split(ref, rows, val, w, t):
    hi = val.astype(BF16)
    _store_trimmed(ref, rows, hi, w, t)
    _store_trimmed(ref, rows, (val - hi.astype(F32)).astype(BF16), w, t, at=t)


def _sb_scores(z_s, split_s, zl_s, tot_s, keep_s, t, diag):
    for i, rows, w in _strips(t, diag):
        z = z_s[rows, :w]
        log_beta = jnp.minimum(z, 0.0) - jnp.log(1.0 + jnp.exp(-jnp.abs(z)))
        log_keep = log_beta - z
        if diag:
            log_keep = jnp.where(_strip_mask(i, w), log_keep, 0.0)
        _store_split(split_s, rows, log_keep, w, t)
        zl_s[rows, :w] = log_beta
        tot_s[rows, :] = _row_sum(log_keep)
        if keep_s is not None:
            keep_s[rows, :w] = jnp.exp(log_keep)


def _row_sum(v):
    return jnp.broadcast_to(jnp.sum(v, axis=-1, keepdims=True), (v.shape[0], LANES))


def _wide(r, t):
    return jnp.concatenate([r] * (t // LANES), axis=1)


def _sb_weights(zl_s, suf_s, run_s, tot_s, a_s, t, diag, da_s=None, glog_s=None, gsplit_s=None, gtot_s=None):
    for i, rows, w in _strips(t, diag):
        run = run_s[rows, :]
        a = jnp.exp(zl_s[rows, :w] + suf_s[rows, :w] + _wide(run, w))
        if diag:
            a = jnp.where(_strip_mask(i, w), a, 0.0)
        ab = a.astype(BF16)
        _store_trimmed(a_s, rows, ab, w, t)
        run_s[rows, :] = run + tot_s[rows, :]
        if da_s is not None:
            glog = ab.astype(F32) * da_s[rows, :w]
            glog_s[rows, :w] = glog
            _store_split(gsplit_s, rows, glog, w, t)
            gtot_s[rows, :] = _row_sum(glog)


def _sb_dscores(glog_s, cum_s, rest_s, gtot_s, keep_s, dz_s, t, diag):
    for i, rows, w in _strips(t, diag):
        glog = glog_s[rows, :w]
        rest = rest_s[rows, :]
        from_here = _wide(rest, w) - cum_s[rows, :w]
        before = from_here - glog
        dz = from_here * keep_s[rows, :w] - before
        if diag:
            dz = jnp.where(_strip_mask(i, w), dz, 0.0)
        _store_trimmed(dz_s, rows, dz.astype(BF16), w, t)
        rest_s[rows, :] = rest - gtot_s[rows, :]


def _attn_fwd(proj, g_attn, cat, w_conv, carry, t=ATTN_BLOCK):
    s = proj.shape[0]
    w_attn = g_attn.shape[1]
    nh = w_attn // LANES
    t = _tile(s, t)
    tq = 2 * t
    nq = s // tq
    q0 = 3 * w_conv // LANES
    scale = HEAD_DIM ** -0.5
    plan = _Carried(carry)
    nw, n_res = len(plan.inputs), len(plan.out_shapes)

    def body(q_ref, k_ref, v_ref, g_ref, cat_in, *rest):
        staged_refs, rest = rest[:nw], rest[nw:]
        o_ref, cat_ref = rest[:2]
        gathered_refs, rest = rest[2:2 + n_res], rest[2 + n_res:]
        kb, vb, tri_s, qm_s, z_s, split_s, zl_s, suf_s, a_s, run_s, tot_s, acc_s = rest[:12]
        gather_sems = rest[12:]
        del cat_in
        qi = pl.program_id(1)

        @pl.when((pl.program_id(0) == 0) & (qi == 0))
        def _():
            for cp in plan.copies(staged_refs, gathered_refs, gather_sems):
                cp.start()

        @pl.when(qi == 0)
        def _():
            kb[...] = k_ref[...].astype(BF16)
            vb[...] = v_ref[...].astype(BF16)
            tri_s[...] = _suffix_operator(t)

        low = _low_half()
        for c, half, msk in _chains(low):
            qm_s[c] = jnp.where(msk, q_ref[half * t:(half + 1) * t, :] * scale, 0.0).astype(BF16)
            run_s[c] = jnp.zeros((t, LANES), F32)
            acc_s[c] = jnp.zeros((t, LANES), F32)

        def key_rows(kblk):
            return pl.ds(pl.multiple_of(kblk * t, t), t)

        def key_block(base, c):
            return key_rows(jnp.maximum(base + c // 2, 0))

        def scores_matmul(base, chains):
            for c in chains:
                z_s[c] = lax.dot_general(qm_s[c], kb[key_block(base, c), :], _NT, preferred_element_type=F32)

        def front(modes, base, prev=None):
            for c, diag in modes:
                _sb_scores(z_s.at[c], split_s.at[c], zl_s.at[c], tot_s.at[c], None, t, diag)
                suf_s[c] = jnp.dot(split_s[c], tri_s[...], preferred_element_type=F32)
            if prev is not None:
                tail(*prev)
            scores_matmul(base - 1, ALL_CHAINS)
            for c, diag in modes:
                _sb_weights(zl_s.at[c], suf_s.at[c], run_s.at[c], tot_s.at[c], a_s.at[c], t, diag)

        def tail(base, chains):
            for c in chains:
                acc_s[c] += jnp.dot(a_s[c], vb[key_block(base, c), :], preferred_element_type=F32)

        first = 2 * qi
        scores_matmul(first, ALL_CHAINS)
        front([(c, True) for c in ALL_CHAINS], first)

        def loop(state):
            it = state[0]
            base = first - 1 - it
            front([(c, False) for c in ALL_CHAINS], base, prev=(base + 1, ALL_CHAINS))
            return it + 1, _any_weight_left(run_s)

        done, live = lax.while_loop(lambda state: (state[0] < first) & (state[1] > 0), loop,
                                    (jnp.int32(0), jnp.int32(1)))
        one_more = (done == first) & (live > 0)

        @pl.when(one_more)
        def _():
            front([(c, False) for c in UPPER_CHAINS], -1, prev=(0, ALL_CHAINS))
            tail(-1, UPPER_CHAINS)

        @pl.when(jnp.logical_not(one_more))
        def _():
            tail(first - done, ALL_CHAINS)

        for half in range(2):
            rows = slice(half * t, (half + 1) * t)
            o = jnp.where(low, acc_s[2 * half], acc_s[2 * half + 1])
            o_ref[rows, :] = o
            r = lax.rsqrt(_half_mean(o * o, low) + EPS)
            cat_ref[rows, :] = (o * r * g_ref[...]).astype(BF16)

        @pl.when((pl.program_id(0) == nh - 1) & (qi == nq - 1))
        def _():
            for cp in plan.copies(staged_refs, gathered_refs, gather_sems):
                cp.wait()

    whole = lambda col0: pl.BlockSpec((s, LANES), lambda h, i: (0, col0 + h))
    n_ch = len(ALL_CHAINS)
    res = _pcall(
        body, name="attn_fwd", grid=(nh, nq),
        in_specs=[pl.BlockSpec((tq, LANES), lambda h, i: (i, q0 + h)),
                  whole(q0 + nh), whole(q0 + 2 * nh),
                  pl.BlockSpec((1, LANES), lambda h, i: (0, h)),
                  pl.BlockSpec(memory_space=pl.ANY)] + [pl.BlockSpec(memory_space=pl.ANY)] * nw,
        out_specs=[pl.BlockSpec((tq, LANES), lambda h, i: (i, h)),
                   pl.BlockSpec((tq, LANES), lambda h, i: (i, w_conv // LANES + h))]
        + [pl.BlockSpec(memory_space=pl.ANY)] * n_res,
        out_shape=[jax.ShapeDtypeStruct((s, w_attn), F32),
                   jax.ShapeDtypeStruct(cat.shape, BF16)] + plan.out_shapes,
        scratch_shapes=[pltpu.VMEM((s, LANES), BF16), pltpu.VMEM((s, LANES), BF16),
                        pltpu.VMEM((2 * t, t), BF16),
                        pltpu.VMEM((n_ch, t, LANES), BF16),
                        pltpu.VMEM((n_ch, t, t), F32),
                        pltpu.VMEM((n_ch, t, 2 * t), BF16),
                        pltpu.VMEM((n_ch, t, t), F32),
                        pltpu.VMEM((n_ch, t, t), F32),
                        pltpu.VMEM((n_ch, t, t), BF16),
                        pltpu.VMEM((n_ch, t, LANES), F32),
                        pltpu.VMEM((n_ch, t, LANES), F32),
                        pltpu.VMEM((n_ch, t, LANES), F32)]
        + plan.sems,
        input_output_aliases={4: 1, **plan.aliases(5, 2)},
        compiler_params=_params(("arbitrary", "arbitrary")),
    )(proj, proj, proj, g_attn, cat, *plan.inputs)
    return res[0], res[1], res[2:]


def _attn_bwd(proj, o, dcat, g_attn, w_conv, carry, t=ATTN_BLOCK):
    s, n_cols = proj.shape
    w_attn = g_attn.shape[1]
    nh = w_attn // LANES
    t = _tile(s, t)
    tq = 2 * t
    nq = s // tq
    q0 = 3 * w_conv // LANES
    scale = HEAD_DIM ** -0.5
    plan = _Carried(carry)
    nw, n_res = len(plan.inputs), len(plan.out_shapes)

    def body(q_ref, k_ref, v_ref, o_ref, do_ref, g_ref, *rest):
        partial_refs, rest = rest[:nw], rest[nw:]
        dproj_ref, dg_ref = rest[:2]
        received_refs, rest = rest[2:2 + n_res], rest[2 + n_res:]
        (kb, vb, dkt_acc, dvt_acc, stash, tri_s, qm_s, dom_s, qt_s, dot_s, z_s, da_s, split_s, zl_s,
         keep_s, suf_s, a_s, glog_s, gsplit_s, cum_s, dz_s, run_s, tot_s, rest_s, gtot_s, dq_s) = rest[:26]
        out_sems, scatter_sems = rest[26], rest[27:]
        step_i = pl.program_id(1)
        qi = nq - 1 - step_i
        head_pair = pl.program_id(0)
        first_step = (head_pair == 0) & (step_i == 0)
        last_step = (head_pair == nh - 1) & (step_i == nq - 1)

        @pl.when(first_step)
        def _():
            for cp in plan.copies(partial_refs, received_refs, scatter_sems):
                cp.start()

        def out_copies():
            rows = pl.ds(pl.multiple_of(qi * tq, tq), tq)
            return [pltpu.make_async_copy(
                stash.at[w], dproj_ref.at[rows, pl.ds(pl.multiple_of((q0 + w * nh + head_pair) * LANES, LANES), LANES)],
                out_sems.at[w]) for w in range(3)]

        def walk():
            @pl.when(step_i == 0)
            def _():
                kb[...] = k_ref[...].astype(BF16)
                vb[...] = v_ref[...].astype(BF16)
                tri_s[...] = _suffix_operator(t)
                dkt_acc[...] = jnp.zeros_like(dkt_acc)
                dvt_acc[...] = jnp.zeros_like(dvt_acc)
                dg_ref[...] = jnp.zeros_like(dg_ref)

            low = _low_half()
            gv = g_ref[...]
            for half in range(2):
                rows = slice(half * t, (half + 1) * t)
                q = q_ref[rows, :] * scale
                ov = o_ref[rows, :]
                d_o, dgn = _head_norm_bwd(ov, do_ref[rows, :], gv, low)
                dg_ref[...] += jnp.sum(dgn, axis=0, keepdims=True)
                for h, msk in enumerate((low, jnp.logical_not(low))):
                    c = 2 * half + h
                    qh = jnp.where(msk, q, 0.0)
                    doh = jnp.where(msk, d_o, 0.0)
                    dom = doh.astype(BF16)
                    qm_s[c] = qh.astype(BF16)
                    dom_s[c] = dom
                    qt_s[c] = qh.T.astype(BF16)
                    dot_s[c] = doh.T.astype(BF16)
                    rest_s[c] = _row_sum(dom.astype(F32) * ov)
                    run_s[c] = jnp.zeros((t, LANES), F32)
                    dq_s[c] = jnp.zeros((t, LANES), F32)

            def key_rows(kblk):
                return pl.ds(pl.multiple_of(kblk * t, t), t)

            def block_of(base, half):
                return jnp.maximum(base + half, 0)

            def scores_matmul(base, chains):
                for c in chains:
                    ks = kb[key_rows(block_of(base, c // 2)), :]
                    z_s[c] = lax.dot_general(qm_s[c], ks, _NT, preferred_element_type=F32)

            def da_matmul(base, chains):
                for c in chains:
                    vs = vb[key_rows(block_of(base, c // 2)), :]
                    da_s[c] = lax.dot_general(dom_s[c], vs, _NT, preferred_element_type=F32)

            def front(modes, base, prev=None):
                if prev is not None:
                    tail(*prev)
                for c, diag in modes:
                    _sb_scores(z_s.at[c], split_s.at[c], zl_s.at[c], tot_s.at[c], keep_s.at[c], t, diag)
                    suf_s[c] = jnp.dot(split_s[c], tri_s[...], preferred_element_type=F32)
                scores_matmul(base - 1, ALL_CHAINS)
                for c, diag in modes:
                    _sb_weights(zl_s.at[c], suf_s.at[c], run_s.at[c], tot_s.at[c], a_s.at[c], t, diag,
                                da_s.at[c], glog_s.at[c], gsplit_s.at[c], gtot_s.at[c])
                    cum_s[c] = jnp.dot(gsplit_s[c], tri_s[...], preferred_element_type=F32)
                da_matmul(base - 1, ALL_CHAINS)
                for c, diag in modes:
                    _sb_dscores(glog_s.at[c], cum_s.at[c], rest_s.at[c], gtot_s.at[c], keep_s.at[c],
                                dz_s.at[c], t, diag)

            def tail(base, chains):
                for half in range(2):
                    mine = [c for c in chains if c // 2 == half]
                    if not mine:
                        continue
                    kblk = block_of(base, half)
                    ks = kb[key_rows(kblk), :]
                    dkt = dkt_acc[kblk]
                    dvt = dvt_acc[kblk]
                    for c in mine:
                        dq_s[c] += jnp.dot(dz_s[c], ks, preferred_element_type=F32)
                        dkt = dkt + jnp.dot(qt_s[c], dz_s[c], preferred_element_type=F32)
                        dvt = dvt + jnp.dot(dot_s[c], a_s[c], preferred_element_type=F32)
                    dkt_acc[kblk] = dkt
                    dvt_acc[kblk] = dvt

            first = 2 * qi
            scores_matmul(first, ALL_CHAINS)
            da_matmul(first, ALL_CHAINS)
            front([(c, True) for c in ALL_CHAINS], first)

            def loop(state):
                it = state[0]
                base = first - 1 - it
                front([(c, False) for c in ALL_CHAINS], base, prev=(base + 1, ALL_CHAINS))
                return it + 1, _any_weight_left(run_s)

            done, live = lax.while_loop(lambda state: (state[0] < first) & (state[1] > 0), loop,
                                        (jnp.int32(0), jnp.int32(1)))
            one_more = (done == first) & (live > 0)

            @pl.when(one_more)
            def _():
                front([(c, False) for c in UPPER_CHAINS], -1, prev=(0, ALL_CHAINS))
                tail(-1, UPPER_CHAINS)

            @pl.when(jnp.logical_not(one_more))
            def _():
                tail(first - done, ALL_CHAINS)

            @pl.when(jnp.logical_not(first_step))
            def _():
                for cp in out_copies():
                    cp.wait()

            for half in range(2):
                rows = slice(half * t, (half + 1) * t)
                stash[0, rows, :] = (jnp.where(low, dq_s[2 * half], dq_s[2 * half + 1]) * scale).astype(BF16)
                stash[1, rows, :] = dkt_acc[2 * qi + half].T.astype(BF16)
                stash[2, rows, :] = dvt_acc[2 * qi + half].T.astype(BF16)
            for cp in out_copies():
                cp.start()

        walk()

        @pl.when(last_step)
        def _():
            for cp in out_copies():
                cp.wait()
            for cp in plan.copies(partial_refs, received_refs, scatter_sems):
                cp.wait()

    whole = lambda col0: pl.BlockSpec((s, LANES), lambda h, i: (0, col0 + h))
    blk = lambda col0: pl.BlockSpec((tq, LANES), lambda h, i: (nq - 1 - i, col0 + h))
    n_ch = len(ALL_CHAINS)
    res = _pcall(
        body, name="attn_bwd", grid=(nh, nq),
        in_specs=[blk(q0), whole(q0 + nh), whole(q0 + 2 * nh), blk(0), blk(w_conv // LANES),
                  pl.BlockSpec((1, LANES), lambda h, i: (0, h))] + [pl.BlockSpec(memory_space=pl.ANY)] * nw,
        out_specs=[pl.BlockSpec(memory_space=pl.ANY),
                   pl.BlockSpec((1, LANES), lambda h, i: (0, h))] + [pl.BlockSpec(memory_space=pl.ANY)] * n_res,
        out_shape=[jax.ShapeDtypeStruct((s, n_cols), BF16), jax.ShapeDtypeStruct((1, w_attn), F32)]
        + plan.out_shapes,
        scratch_shapes=[pltpu.VMEM((s, LANES), BF16), pltpu.VMEM((s, LANES), BF16),
                        pltpu.VMEM((s // t, LANES, t), F32),
                        pltpu.VMEM((s // t, LANES, t), F32),
                        pltpu.VMEM((3, tq, LANES), BF16),
                        pltpu.VMEM((2 * t, t), BF16),
                        pltpu.VMEM((n_ch, t, LANES), BF16),
                        pltpu.VMEM((n_ch, t, LANES), BF16),
                        pltpu.VMEM((n_ch, LANES, t), BF16),
                        pltpu.VMEM((n_ch, LANES, t), BF16),
                        pltpu.VMEM((n_ch, t, t), F32),
                        pltpu.VMEM((n_ch, t, t), F32),
                        pltpu.VMEM((n_ch, t, 2 * t), BF16),
                        pltpu.VMEM((n_ch, t, t), F32),
                        pltpu.VMEM((n_ch, t, t), F32),
                        pltpu.VMEM((n_ch, t, t), F32),
                        pltpu.VMEM((n_ch, t, t), BF16),
                        pltpu.VMEM((n_ch, t, t), F32),
                        pltpu.VMEM((n_ch, t, 2 * t), BF16),
                        pltpu.VMEM((n_ch, t, t), F32),
                        pltpu.VMEM((n_ch, t, t), BF16),
                        pltpu.VMEM((n_ch, t, LANES), F32),
                        pltpu.VMEM((n_ch, t, LANES), F32),
                        pltpu.VMEM((n_ch, t, LANES), F32),
                        pltpu.VMEM((n_ch, t, LANES), F32),
                        pltpu.VMEM((n_ch, t, LANES), F32),
                        pltpu.SemaphoreType.DMA((3,))]
        + plan.sems,
        input_output_aliases=plan.aliases(6, 2),
        compiler_params=_params(("arbitrary", "arbitrary")),
    )(proj, proj, proj, o, dcat, g_attn, *plan.inputs)
    return res[0], res[1], res[2:]


def _place():
    return lax.axis_index("x"), lax.axis_index("y"), lax.axis_index("c")


def _other_chips(x, y):
    return [(1 - x, y), (x, 1 - y), (1 - x, 1 - y)]


def _slot(px, py, pc):
    return 4 * px + 2 * py + pc


def _all_gather(shards, out_dtypes, to_stage=()):
    nw, ns = len(shards), len(to_stage)

    def body(*refs):
        ins, rest = refs[:nw], refs[nw:]
        raw, rest = rest[:ns], rest[ns:]
        outs, rest = rest[:nw], rest[nw:]
        staged, rest = rest[:ns], rest[ns:]
        stage, (send_sems, recv_sems, local_sems) = rest[:nw], rest[nw:]
        x, y, c = _place()
        me, sibling = (x, y, c), (x, y, 1 - c)
        chips = _other_chips(x, y)

        def copy(w, k, block, to, src=None):
            dst = outs[w].at[_slot(*block)]
            return pltpu.make_async_remote_copy(
                src_ref=dst if src is None else src, dst_ref=dst,
                send_sem=send_sems.at[w * 7 + k], recv_sem=recv_sems.at[w * 7 + k],
                device_id=to, device_id_type=MESH)

        started = []
        local = []
        for w in range(nw):
            stage[w][...] = ins[w][...].astype(stage[w].dtype)
            cp = pltpu.make_async_copy(stage[w], outs[w].at[_slot(*me)], local_sems.at[w])
            cp.start()
            local.append(cp)
            started.append(copy(w, 0, me, sibling, src=stage[w]))
            started[-1].start()
            for j, chip in enumerate(chips):
                started.append(copy(w, 1 + j, me, (*chip, c), src=stage[w]))
                started[-1].start()
        for src, dst in zip(raw, staged):
            dst[...] = src[...].astype(BF16)
        for j, chip in enumerate(chips):
            for w in range(nw):
                copy(w, 1 + j, (*chip, c), me).wait_recv()
                started.append(copy(w, 4 + j, (*chip, c), sibling))
                started[-1].start()
        for w in range(nw):
            copy(w, 0, sibling, me).wait_recv()
            for j, chip in enumerate(chips):
                copy(w, 4 + j, (*chip, 1 - c), me).wait_recv()
        for cp in started:
            cp.wait_send()
        for cp in local:
            cp.wait()

    res = _pcall(
        body, name="all_gather_weights",
        in_specs=[pl.BlockSpec(memory_space=pltpu.VMEM)] * (nw + ns),
        out_specs=[pl.BlockSpec(memory_space=pl.ANY)] * nw + [pl.BlockSpec(memory_space=pltpu.VMEM)] * ns,
        out_shape=[jax.ShapeDtypeStruct((N_DEV, *a.shape), d) for a, d in zip(shards, out_dtypes)]
        + [jax.ShapeDtypeStruct(a.shape, BF16) for a in to_stage],
        scratch_shapes=[pltpu.VMEM(a.shape, d) for a, d in zip(shards, out_dtypes)]
        + [pltpu.SemaphoreType.DMA((7 * nw,)), pltpu.SemaphoreType.DMA((7 * nw,)),
           pltpu.SemaphoreType.DMA((nw,))],
        compiler_params=_params(),
    )(*shards, *to_stage)
    return res[:nw], res[nw:]


N_PEERS = N_DEV - 1


def _peer(k):
    x, y, c = _place()
    return (x ^ (k >> 2), y ^ ((k >> 1) & 1), c ^ (k & 1))


def _remote(src, dst, sems, index, to):
    return pltpu.make_async_remote_copy(src_ref=src, dst_ref=dst, send_sem=sems[0].at[index],
                                        recv_sem=sems[1].at[index], device_id=to, device_id_type=MESH)


def _gather_out_copies(staged, gathered, sems):
    x, y, c = _place()
    me = _slot(x, y, c)
    targets = [(x, y, 1 - c)] + [(*chip, c) for chip in _other_chips(x, y)]
    copies = []
    for w, (src, dst) in enumerate(zip(staged, gathered)):
        copies.append(pltpu.make_async_copy(src, dst.at[me], sems[2].at[w]))
        copies += [_remote(src, dst.at[me], sems, w * len(targets) + k, to) for k, to in enumerate(targets)]
    return copies


def _gather_pass_copies(arrived, gathered, sems):
    x, y, c = _place()
    chips = _other_chips(x, y)
    return [_remote(src.at[_slot(*chip, c)], dst.at[_slot(*chip, c)], sems, w * len(chips) + j, (x, y, 1 - c))
            for w, (src, dst) in enumerate(zip(arrived, gathered)) for j, chip in enumerate(chips)]


ALL_PEERS = tuple(range(1, N_DEV))


def _scatter_copies(partials, received, sems, peers=ALL_PEERS):
    me = _slot(*_place())
    return [_remote(src.at[me ^ k], dst.at[k - 1], sems, w * N_PEERS + k - 1, _peer(k))
            for w, (src, dst) in enumerate(zip(partials, received)) for k in peers]


class _Carried:
    def __init__(self, jobs):
        self.jobs = [(job[0], list(job[1]), job[2] if len(job) > 2 else ALL_PEERS) for job in jobs if len(job[1])]
        self.inputs, self.out_shapes, self.sems, self.counts = [], [], [], []
        for kind, arrays, _ in self.jobs:
            n_out = len(arrays) // 2 if kind == "scatter_more" else len(arrays)
            fan = {"gather_out": 4, "gather_pass": 3}.get(kind, N_PEERS)
            for a in arrays[len(arrays) - n_out:]:
                shape = {"gather_out": (N_DEV, *a.shape), "scatter": (N_PEERS, *a.shape[1:])}.get(kind, a.shape)
                self.out_shapes.append(jax.ShapeDtypeStruct(shape, BF16))
            job_sems = [pltpu.SemaphoreType.DMA((fan * n_out,))] * 2
            job_sems += [pltpu.SemaphoreType.DMA((n_out,))] if kind == "gather_out" else []
            self.inputs += arrays
            self.sems += job_sems
            self.counts.append((len(arrays), n_out, len(job_sems)))

    def aliases(self, first_input, first_output):
        pairs, at_in, at_out = {}, first_input, first_output
        for (kind, _, _), (n_in, n_out, _) in zip(self.jobs, self.counts):
            if kind in ("gather_pass", "scatter_more"):
                pairs.update({at_in + n_in - n_out + i: at_out + i for i in range(n_out)})
            at_in, at_out = at_in + n_in, at_out + n_out
        return pairs

    def copies(self, in_refs, out_refs, sem_refs):
        out, at_in, at_out, at_sem = [], 0, 0, 0
        for (kind, _, peers), (n_in, n_out, n_sems) in zip(self.jobs, self.counts):
            srcs, dsts = in_refs[at_in:at_in + n_out], out_refs[at_out:at_out + n_out]
            sems = sem_refs[at_sem:at_sem + n_sems]
            if kind == "gather_out":
                out += _gather_out_copies(srcs, dsts, sems)
            elif kind == "gather_pass":
                out += _gather_pass_copies(srcs, dsts, sems)
            else:
                out += _scatter_copies(srcs, dsts, sems, peers)
            at_in, at_out, at_sem = at_in + n_in, at_out + n_out, at_sem + n_sems
        return out


def _all_reduce_small(packed):
    r = packed.shape[0]

    def body(x_ref, o_ref, gathered, send_sems, recv_sems):
        x, y, c = _place()
        me = _slot(x, y, c)
        gathered[me] = x_ref[...]
        copies = [_remote(x_ref, gathered.at[me], (send_sems, recv_sems), k - 1, _peer(k)) for k in ALL_PEERS]
        for cp in copies:
            cp.start()
        for cp in copies:
            cp.wait()
        total = gathered[0]
        for k in range(1, N_DEV):
            total = total + gathered[k]
        o_ref[...] = total

    return _pcall(
        body, name="all_reduce_small",
        in_specs=[pl.BlockSpec(memory_space=pltpu.VMEM)],
        out_specs=pl.BlockSpec(memory_space=pltpu.VMEM),
        out_shape=jax.ShapeDtypeStruct(packed.shape, F32),
        scratch_shapes=[pltpu.VMEM((N_DEV, r, LANES), F32),
                        pltpu.SemaphoreType.DMA((N_DEV - 1,)), pltpu.SemaphoreType.DMA((N_DEV - 1,))],
        compiler_params=_params(),
    )(packed)


def _adam_math(w, g, m, v):
    m = ADAM_B1 * m + (1.0 - ADAM_B1) * g
    v = ADAM_B2 * v + (1.0 - ADAM_B2) * jnp.square(g)
    m_hat = m / (1.0 - ADAM_B1 ** ADAM_STEP)
    v_hat = v / (1.0 - ADAM_B2 ** ADAM_STEP)
    delta = -ADAM_LR * (m_hat / (jnp.sqrt(v_hat) + ADAM_EPS) + ADAM_WD * w)
    return delta, m, v


ADAM_TILE_BYTES = 24 * 1024 * 1024


def _adam_sharded(name, own, received, w, m, v, place):
    r, cdim = w.shape
    row_bytes = 2 * cdim * (4 + 2 * N_PEERS + 3 * 4 + 4 * 4)
    tr = _tile(r, max(LANES, ADAM_TILE_BYTES // row_bytes // LANES * LANES)) if r % LANES == 0 else r

    def body(place_ref, own_ref, rec_ref, w_ref, m_ref, v_ref, g_ref, d_ref, nm_ref, nv_ref):
        del place_ref
        g = own_ref[...]
        for j in range(N_PEERS):
            g = g + rec_ref[j].astype(F32)
        delta, nm, nv = _adam_math(w_ref[...], g, m_ref[...], v_ref[...])
        g_ref[...] = g
        d_ref[...] = delta
        nm_ref[...] = nm
        nv_ref[...] = nv

    blk = pl.BlockSpec((tr, cdim), lambda i, pr: (i, 0))
    grid_spec = pltpu.PrefetchScalarGridSpec(
        num_scalar_prefetch=1, grid=(r // tr,),
        in_specs=[pl.BlockSpec((None, tr, cdim), lambda i, pr: (4 * pr[0] + 2 * pr[1] + pr[2], i, 0)),
                  pl.BlockSpec((N_PEERS, tr, cdim), lambda i, pr: (0, i, 0)), blk, blk, blk],
        out_specs=[blk] * 4)
    return _pcall(body, name=name, grid_spec=grid_spec,
                  out_shape=[jax.ShapeDtypeStruct((r, cdim), F32)] * 4,
                  compiler_params=_params(("parallel",)))(place, own, received, w, m, v)


def _adam_small(w, g, m, v):
    def body(w_ref, g_ref, m_ref, v_ref, d_ref, nm_ref, nv_ref):
        delta, nm, nv = _adam_math(w_ref[...], g_ref[...], m_ref[...], v_ref[...])
        d_ref[...] = delta
        nm_ref[...] = nm
        nv_ref[...] = nv

    return _pcall(body, name="adam_small",
                  in_specs=[pl.BlockSpec(memory_space=pltpu.VMEM)] * 4,
                  out_specs=[pl.BlockSpec(memory_space=pltpu.VMEM)] * 3,
                  out_shape=[jax.ShapeDtypeStruct(w.shape, F32)] * 3,
                  compiler_params=_params())(w, g, m, v)


def _rows(vec):
    return vec.reshape(-1, LANES)


def kernel(x, p, g_mix, w_in, conv_w, g_conv_out, g_attn_out, w_out, g_mlp, w_up, w_down, g_ple, w_ple_gate, w_ple_proj, g_final, loss_target, m_g_mix, m_w_in, m_conv_w, m_g_conv_out, m_g_attn_out, m_w_out, m_g_mlp, m_w_up, m_w_down, m_g_ple, m_w_ple_gate, m_w_ple_proj, m_g_final, v_g_mix, v_w_in, v_conv_w, v_g_conv_out, v_g_attn_out, v_w_out, v_g_mlp, v_w_up, v_w_down, v_g_ple, v_w_ple_gate, v_w_ple_proj, v_g_final):
    s, d = x.shape[1], x.shape[2]
    w_conv = g_conv_out.shape[1]
    w_attn = g_attn_out.shape[1]
    cw = conv_w.shape[2]
    xs, ps, tgt = x[0], p[0, 0], loss_target[0]
    place = jnp.stack([lax.axis_index("x"), lax.axis_index("y"), lax.axis_index("c")]).astype(jnp.int32)
    my_slot = 4 * place[0] + 2 * place[1] + place[2]

    conv_tile = jnp.pad(conv_w[0], ((0, HALO - CONV_K), (0, LANES - cw)))
    big = [w_in[0], w_out[0], w_up[0], w_down[0], w_ple_gate[0], w_ple_proj[0]]
    (win_g, conv_g), (s_out, s_up, s_down, s_gate, s_proj) = _all_gather(
        [big[0], conv_tile], [BF16, F32], to_stage=big[1:])
    conv_full = jnp.transpose(conv_g[:, :CONV_K, :cw], (1, 0, 2)).reshape(CONV_K, w_conv)
    in_shard, up_shard, proj_shard = big[0].shape[1], big[2].shape[1], big[5].shape[1]

    proj, a, g_out, g_gate, g_proj = _mm_nn("in_proj", xs, win_g, n_shard=in_shard, tn=2 * in_shard, tm=2048,
                                            lhs_norm=g_mix, carry=[("gather_out", [s_out, s_gate, s_proj])])
    cat = _conv_fwd(proj, conv_full, g_conv_out, w_conv, d)
    o, cat, (g_up, g_down, wout_g, wgate_g, wproj_g) = _attn_fwd(
        proj, g_attn_out, cat, w_conv,
        [("gather_out", [s_up, s_down]), ("gather_pass", [g_out, g_gate, g_proj])])
    wout_f = wout_g.reshape(-1, wout_g.shape[-1])
    wgate_f = wgate_g.reshape(-1, wgate_g.shape[-1])
    h1, wup_g = _mm_nn("out_proj", cat, wout_f, epilogue=_ep_residual, extras=(xs,),
                       carry=[("gather_pass", [g_up])])
    act, mn, wdown_g = _mm_nn("mlp_up", h1, wup_g, n_shard=up_shard, epilogue=_ep_up, out_dtypes=(BF16,), tm=2048,
                              lhs_norm=g_mlp, carry=[("gather_pass", [g_down])])
    wdown_f = wdown_g.reshape(-1, wdown_g.shape[-1])
    h2, = _mm_nn("mlp_down", act, wdown_f, epilogue=_ep_residual, extras=(h1,))
    pp = _ple_proj(ps, wproj_g)
    loss_part, dh3, dgl, dpp, dg_final, n3 = _ple_gate_loss(h2, g_ple, wgate_f, pp, tgt, g_final.reshape(1, d))

    def slots(t2d):
        return t2d.reshape(N_DEV, -1, t2d.shape[-1])

    dw_proj = _d_ple_proj(ps, dpp, proj_shard)
    dw_gate = [slots(t) for t in _mm_tn("d_w_ple_gate", n3, dgl)]
    dh2, dh2b, dg_ple = _mm_nt_norm_bwd("d_norm_ple", dgl, wgate_f, h2, g_ple, dh3)
    du, gate_recv, proj_recv = _mm_nt("d_mlp_act", dh2b, wdown_f, epilogue=_ep_dact, out_dtypes=(BF16,),
                                      extras=(act,), tm=2048, carry=[("scatter", [dw_gate[1], dw_proj[1]])])
    dw_down = [slots(t) for t in _mm_tn("d_w_down", act, dh2b)]
    near, far = (1, 2, 3, 4, 5), (6, 7)
    dw_up = _mm_tn("d_w_up", mn, du, n_shard=up_shard)
    dh1, dh1b, dg_mlp, down_part = _mm_nt_norm_bwd(
        "d_norm_mlp", du, wup_g, h1, g_mlp, dh2, k_shard=up_shard, tm=1024,
        carry=[("scatter", [dw_down[1]], near)])
    dcat, = _mm_nt("d_cat", dh1b, wout_f)
    dw_out = [slots(t) for t in _mm_tn("d_w_out", cat, dh1b)]
    dproj, dg_attn, (down_recv, up_recv) = _attn_bwd(
        proj, o, dcat, g_attn_out, w_conv,
        [("scatter_more", [dw_down[1], down_part], far), ("scatter", [dw_up[1]])])
    dproj, dconv, dg_conv = _conv_bwd(proj, dcat, conv_full, g_conv_out, dproj, w_conv)
    *dw_in, out_recv = _mm_tn("d_w_in", a, dproj, n_shard=in_shard, tn=in_shard,
                              carry=[("scatter", [dw_out[1]])])
    grad_x, _, dg_mix, in_recv = _mm_nt_norm_bwd("d_norm_mix", dproj, win_g, xs, g_mix, dh1, k_shard=in_shard,
                                                 tk=2 * in_shard, tm=1024, carry=[("scatter", [dw_in[1]])])

    names = ["w_in", "w_out", "w_up", "w_down", "w_ple_gate", "w_ple_proj"]
    owns = [dw_in[0], dw_out[0], dw_up[0], dw_down[0], dw_gate[0], dw_proj[0]]
    recvs = [in_recv, out_recv, up_recv, down_recv, gate_recv, proj_recv]
    moments = [(m_w_in, v_w_in), (m_w_out, v_w_out), (m_w_up, v_w_up), (m_w_down, v_w_down),
               (m_w_ple_gate, v_w_ple_gate), (m_w_ple_proj, v_w_ple_proj)]
    big_out = {}
    for n, own, rc, wt, (mm, vv) in zip(names, owns, recvs, big, moments):
        big_out[n] = [t[None] for t in _adam_sharded("adam_" + n, own, rc, wt, mm[0], vv[0], place)]

    n_conv_rows = CONV_K * w_conv // LANES
    small_g = jnp.concatenate(
        [_rows(dg_mix[0]), _rows(dg_conv[0]), _rows(dg_attn[0]), _rows(dg_mlp[0]), _rows(dg_ple[0]),
         _rows(dg_final[0]), _rows(dconv.reshape(-1)), loss_part], axis=0)
    n_gain_rows = small_g.shape[0] - n_conv_rows - 1
    pad_rows = (-small_g.shape[0]) % HALO
    small_g = _all_reduce_small(jnp.pad(small_g, ((0, pad_rows), (0, 0))))
    loss = small_g[n_gain_rows + n_conv_rows, 0]
    dconv_full = small_g[n_gain_rows:n_gain_rows + n_conv_rows].reshape(CONV_K, w_conv)
    dconv_mine = lax.dynamic_slice(dconv_full, (0, my_slot * cw), (CONV_K, cw))

    def pack(vecs, conv_part):
        rows = [_rows(t.reshape(-1)) for t in vecs]
        rows.append(jnp.pad(conv_part, ((0, HALO - CONV_K), (0, LANES - cw))))
        return jnp.concatenate(rows, axis=0)

    gains = [g_mix, g_conv_out, g_attn_out, g_mlp, g_ple, g_final]
    gains_m = [m_g_mix, m_g_conv_out, m_g_attn_out, m_g_mlp, m_g_ple, m_g_final]
    gains_v = [v_g_mix, v_g_conv_out, v_g_attn_out, v_g_mlp, v_g_ple, v_g_final]
    gpack = jnp.concatenate([small_g[:n_gain_rows], jnp.pad(dconv_mine, ((0, HALO - CONV_K), (0, LANES - cw)))], axis=0)
    sd, sm, sv = _adam_small(pack(gains, conv_w[0]), gpack, pack(gains_m, m_conv_w[0]), pack(gains_v, v_conv_w[0]))

    def unpack(packed):
        out, r0 = [], 0
        for t in gains:
            nr = t.size // LANES
            out.append(packed[r0:r0 + nr].reshape(t.shape))
            r0 += nr
        out.append(packed[r0:r0 + CONV_K, :cw][None])
        return out

    sg_l, sd_l, sm_l, sv_l = unpack(gpack), unpack(sd), unpack(sm), unpack(sv)
    small_names = ["g_mix", "g_conv_out", "g_attn_out", "g_mlp", "g_ple", "g_final", "conv_w"]
    small_out = {n: [sg_l[i], sd_l[i], sm_l[i], sv_l[i]] for i, n in enumerate(small_names)}

    order = ["g_mix", "w_in", "conv_w", "g_conv_out", "g_attn_out", "w_out", "g_mlp", "w_up", "w_down",
             "g_ple", "w_ple_gate", "w_ple_proj", "g_final"]
    table = {**big_out, **small_out}
    outs = [loss, grad_x[None]]
    for kind in range(4):
        outs.extend(table[n][kind] for n in order)
    return tuple(outs)
```

```python
import jax
import jax.numpy as jnp
from jax import lax
from jax.experimental import pallas as pl
from jax.experimental.pallas import tpu as pltpu

F32 = jnp.float32
BF16 = jnp.bfloat16
EPS = 1e-6
HEAD_DIM = 64
LANES = 128
CONV_K = 3
MXU_WIDTH = 256
ATTN_BLOCK = MXU_WIDTH
HALO = 8
N_DEV = 8
MESH = pl.DeviceIdType.MESH
VMEM_LIMIT = 56 * 1024 * 1024

ADAM_LR = 0.001
ADAM_B1 = 0.9
ADAM_B2 = 0.999
ADAM_EPS = 1e-08
ADAM_WD = 0.01
ADAM_STEP = 10


def _pcall(body, **kw):
    return pl.pallas_call(body, **kw)


def _params(sem=None, **kw):
    return pltpu.CompilerParams(dimension_semantics=sem, vmem_limit_bytes=VMEM_LIMIT, **kw)


def _tile(dim, pref):
    t = min(dim, pref)
    while dim % t:
        t -= LANES
    assert t > 0, (dim, pref)
    return t


_NN = (((1,), (0,)), ((), ()))
_NT = (((1,), (1,)), ((), ()))
_TN = (((0,), (0,)), ((), ()))


def _ep_store(acc, outs):
    outs[0][...] = acc.astype(outs[0].dtype)


def _ep_both(acc, outs):
    outs[0][...] = acc
    outs[1][...] = acc.astype(BF16)


def _ep_residual(acc, res, outs):
    outs[0][...] = acc + res[...]


def _ep_up(acc, outs):
    outs[0][...] = jnp.square(jnp.maximum(acc, 0.0)).astype(BF16)


def _ep_dact(acc, act, outs):
    outs[0][...] = (acc * (2.0 * jnp.sqrt(act[...].astype(F32)))).astype(BF16)


def _row_chunked(epilogue):
    def run(acc, *rest):
        *ex, outs = rest
        n = acc.shape[0]
        for m0 in range(0, n, MXU_WIDTH):
            rows = slice(m0, min(m0 + MXU_WIDTH, n))
            pick = lambda ref: ref.at[rows, :] if ref.shape[0] == n else ref
            epilogue(acc[rows, :], *[pick(e) for e in ex], [pick(o) for o in outs])
    return run


def _ep_norm_bwd(acc, h, g, dres, outs):
    hv = h[...]
    r = lax.rsqrt(jnp.mean(hv * hv, axis=-1, keepdims=True) + EPS)
    hn = hv * r
    outs[2][...] += jnp.sum(acc * hn, axis=0, keepdims=True)
    dhn = acc * g[...]
    dh = dres[...] + r * (dhn - hn * jnp.mean(dhn * hn, axis=-1, keepdims=True))
    outs[0][...] = dh
    outs[1][...] = dh.astype(BF16)


def _matmul(name, a, b, *, dims, grid, a_spec, b_spec, acc_shape, out_shapes, out_specs,
            epilogue=_ep_store, extras=(), extra_specs=(), carry=(), sequential=False, lhs_norm=False):
    nk = grid[2]
    plan = _Carried(carry)
    n_ex, n_out, n_xc, n_xo = len(extras), len(out_shapes), len(plan.inputs), len(plan.out_shapes)
    n_sems = len(plan.sems)
    last = tuple(g - 1 for g in grid)
    assert not lhs_norm or nk == 1

    def product(a_ref, b_ref):
        if len(b_ref.shape) == 2:
            return lax.dot_general(a_ref[...].astype(BF16), b_ref[...].astype(BF16), dims,
                                   preferred_element_type=F32)
        width = b_ref.shape[2]
        return sum(lax.dot_general(a_ref[:, g * width:(g + 1) * width].astype(BF16), b_ref[g].astype(BF16), dims,
                                   preferred_element_type=F32) for g in range(b_ref.shape[0]))

    def body(a_ref, b_ref, *rest):
        ex, rest = rest[:n_ex], rest[n_ex:]
        partials, rest = rest[:n_xc], rest[n_xc:]
        outs, rest = rest[:n_out], rest[n_out:]
        received, rest = rest[:n_xo], rest[n_xo:]
        ids = [pl.program_id(axis) for axis in range(3)]
        if n_xc:
            @pl.when((ids[0] == 0) & (ids[1] == 0) & (ids[2] == 0))
            def _():
                for cp in plan.copies(partials, received, rest[-n_sems:]):
                    cp.start()

        if lhs_norm:
            x_ref, a_ref, gain, ex, outs = a_ref, outs[-1], ex[-1], ex[:-1], outs[:-1]

            @pl.when(ids[1] == 0)
            def _():
                for m0 in range(0, acc_shape[0], MXU_WIDTH):
                    rows = slice(m0, min(m0 + MXU_WIDTH, acc_shape[0]))
                    xv = x_ref[rows, :]
                    r = lax.rsqrt(jnp.mean(xv * xv, axis=-1, keepdims=True) + EPS)
                    a_ref[rows, :] = (xv * r * gain[...]).astype(BF16)

        if nk == 1 and not sequential and dims != _TN:
            if len(b_ref.shape) == 3:
                ns = b_ref.shape[2]
                pieces = [(b_ref.at[g, :, n0:min(n0 + MXU_WIDTH, ns)], slice(g * ns + n0, g * ns + min(n0 + MXU_WIDTH, ns)))
                          for g in range(b_ref.shape[0]) for n0 in range(0, ns, MXU_WIDTH)]
            else:
                spans = [slice(n0, min(n0 + MXU_WIDTH, acc_shape[1])) for n0 in range(0, acc_shape[1], MXU_WIDTH)]
                pieces = [(b_ref.at[cols, :] if dims == _NT else b_ref.at[:, cols], cols) for cols in spans]
            for b_cols, cols in pieces:
                for m0 in range(0, acc_shape[0], MXU_WIDTH):
                    rows = slice(m0, min(m0 + MXU_WIDTH, acc_shape[0]))
                    epilogue(product(a_ref.at[rows, :], b_cols), *[e.at[rows, cols] for e in ex],
                             [o.at[rows, cols] for o in outs])
        else:
            if sequential:
                @pl.when((ids[0] == 0) & (ids[2] == 0))
                def _():
                    for o in outs:
                        if o.shape[0] != acc_shape[0]:
                            o[...] = jnp.zeros_like(o)

            if nk == 1:
                _row_chunked(epilogue)(product(a_ref, b_ref), *ex, outs)
            else:
                acc = rest[0]

                @pl.when(ids[2] == 0)
                def _():
                    acc[...] = product(a_ref, b_ref)

                @pl.when(ids[2] > 0)
                def _():
                    acc[...] += product(a_ref, b_ref)

                @pl.when(ids[2] == nk - 1)
                def _():
                    _row_chunked(epilogue)(acc, *ex, outs)

        if n_xc:
            @pl.when((ids[0] == last[0]) & (ids[1] == last[1]) & (ids[2] == last[2]))
            def _():
                for cp in plan.copies(partials, received, rest[-n_sems:]):
                    cp.wait()

    anywhere = pl.BlockSpec(memory_space=pl.ANY)
    return _pcall(
        body, name=name, grid=grid,
        in_specs=[a_spec, b_spec, *extra_specs, *[anywhere] * n_xc],
        out_specs=[*out_specs, *[anywhere] * n_xo],
        out_shape=[*out_shapes, *plan.out_shapes],
        scratch_shapes=([] if nk == 1 else [pltpu.VMEM(acc_shape, F32)]) + plan.sems,
        input_output_aliases=plan.aliases(2 + n_ex, n_out),
        compiler_params=_params(("arbitrary",) * 3 if n_xc or sequential or lhs_norm
                                else ("parallel", "parallel", "arbitrary")),
    )(a, b, *extras, *plan.inputs)


_NO_CARRY = ()


def _mm_nn(name, a, w, *, n_shard=None, epilogue=_ep_store, out_dtypes=(F32,), extras=(), carry=_NO_CARRY,
           lhs_norm=None, tm=1024, tn=1024, tk=1024):
    m, kd = a.shape
    if lhs_norm is not None:
        tk = kd
    if n_shard is None:
        n = w.shape[1]
        tn = _tile(n, tn)
        tk = _tile(kd, tk)
        b_spec = pl.BlockSpec((tk, tn), lambda i, j, k: (k, j))
    elif tn >= 2 * n_shard and tk >= kd:
        n = N_DEV * n_shard
        group = min(tn // n_shard, N_DEV)
        while N_DEV % group:
            group -= 1
        tn, tk = group * n_shard, kd
        b_spec = pl.BlockSpec((group, tk, n_shard), lambda i, j, k: (j, 0, 0))
    else:
        n = N_DEV * n_shard
        tn = _tile(n_shard, tn)
        tk = _tile(kd, tk)
        per = n_shard // tn
        b_spec = pl.BlockSpec((None, tk, tn), lambda i, j, k: (j // per, k, j % per))
    tm = _tile(m, tm)
    o_spec = pl.BlockSpec((tm, tn), lambda i, j, k: (i, j))
    out_shapes = [jax.ShapeDtypeStruct((m, n), d) for d in out_dtypes]
    out_specs = [o_spec] * len(out_dtypes)
    extra_specs = [o_spec] * len(extras)
    if lhs_norm is not None:
        extras = (*extras, lhs_norm)
        extra_specs.append(pl.BlockSpec((1, kd), lambda i, j, k: (0, 0)))
        out_shapes.append(jax.ShapeDtypeStruct((m, kd), BF16))
        out_specs.append(pl.BlockSpec((tm, kd), lambda i, j, k: (i, 0)))
    return _matmul(
        name, a, w, dims=_NN, grid=(m // tm, n // tn, kd // tk),
        a_spec=pl.BlockSpec((tm, tk), lambda i, j, k: (i, k)), b_spec=b_spec,
        acc_shape=(tm, tn), out_shapes=out_shapes, out_specs=out_specs,
        epilogue=epilogue, extras=extras, extra_specs=extra_specs, carry=carry, lhs_norm=lhs_norm is not None)


def _mm_nt(name, a, w, *, epilogue=_ep_store, out_dtypes=(F32,), extras=(), carry=_NO_CARRY,
           tm=1024, tn=1024, tk=1024):
    m, kd = a.shape
    n = w.shape[0]
    tm, tn, tk = _tile(m, tm), _tile(n, tn), _tile(kd, tk)
    o_spec = pl.BlockSpec((tm, tn), lambda i, j, k: (i, j))
    return _matmul(
        name, a, w, dims=_NT, grid=(m // tm, n // tn, kd // tk),
        a_spec=pl.BlockSpec((tm, tk), lambda i, j, k: (i, k)),
        b_spec=pl.BlockSpec((tn, tk), lambda i, j, k: (j, k)),
        acc_shape=(tm, tn),
        out_shapes=[jax.ShapeDtypeStruct((m, n), d) for d in out_dtypes],
        out_specs=[o_spec] * len(out_dtypes),
        epilogue=epilogue, extras=extras, extra_specs=[o_spec] * len(extras), carry=carry)


def _mm_nt_norm_bwd(name, a, w, h, g, dres, *, k_shard=None, carry=_NO_CARRY, tm=512, tk=1024):
    m, kd = a.shape
    n = h.shape[1]
    if k_shard is None:
        tk = _tile(kd, tk)
        b_spec = pl.BlockSpec((n, tk), lambda i, j, k: (0, k))
    else:
        group = max(1, min(tk // k_shard, N_DEV))
        while N_DEV % group:
            group -= 1
        tk = group * k_shard
        b_spec = pl.BlockSpec((group, n, k_shard), lambda i, j, k: (k, 0, 0))
    tm = _tile(m, tm)
    rows = pl.BlockSpec((tm, n), lambda i, j, k: (i, 0))
    vec = pl.BlockSpec((1, n), lambda i, j, k: (0, 0))
    return _matmul(
        name, a, w, dims=_NT, grid=(m // tm, 1, kd // tk),
        a_spec=pl.BlockSpec((tm, tk), lambda i, j, k: (i, k)), b_spec=b_spec, acc_shape=(tm, n),
        out_shapes=[jax.ShapeDtypeStruct((m, n), F32), jax.ShapeDtypeStruct((m, n), BF16),
                    jax.ShapeDtypeStruct((1, n), F32)],
        out_specs=[rows, rows, vec], epilogue=_ep_norm_bwd,
        extras=(h, g, dres), extra_specs=[rows, vec, rows], carry=carry, sequential=True)


TN_TILE_BYTES = 40 * 1024 * 1024


def _mm_tn(name, a, b, *, n_shard=None, carry=_NO_CARRY, tm=1024, tn=1024):
    t, m = a.shape
    n = b.shape[1]
    tm = _tile(m, tm)
    tn = _tile(n if n_shard is None else n_shard, tn)
    tk = t
    while 2 * 2 * tk * (tm + tn) + 4 * tm * tn * 5 > TN_TILE_BYTES and tk % (2 * LANES) == 0:
        tk //= 2
    if n_shard is None:
        o_spec = pl.BlockSpec((tm, tn), lambda i, j, k: (i, j))
        shape = (m, n)
    else:
        per = n_shard // tn
        o_spec = pl.BlockSpec((None, tm, tn), lambda i, j, k: (j // per, i, j % per))
        shape = (N_DEV, m, n_shard)
    return _matmul(
        name, a, b, dims=_TN, grid=(m // tm, n // tn, t // tk),
        a_spec=pl.BlockSpec((tk, tm), lambda i, j, k: (k, i)),
        b_spec=pl.BlockSpec((tk, tn), lambda i, j, k: (k, j)),
        acc_shape=(tm, tn), epilogue=_ep_both, carry=carry,
        out_shapes=[jax.ShapeDtypeStruct(shape, F32), jax.ShapeDtypeStruct(shape, BF16)],
        out_specs=[o_spec, o_spec])


def _ple_proj(p, w_g, tm=1024):
    s, kd = p.shape
    ns = w_g.shape[2]
    tm = _tile(s, tm)

    def body(p_ref, w_ref, o_ref):
        pv = p_ref[...].astype(BF16)
        for j in range(N_DEV):
            o_ref[:, j * ns:(j + 1) * ns] = jnp.dot(pv, w_ref[j], preferred_element_type=F32)

    return _pcall(body, name="ple_proj", grid=(s // tm,),
                  in_specs=[pl.BlockSpec((tm, kd), lambda i: (i, 0)),
                            pl.BlockSpec((N_DEV, kd, ns), lambda i: (0, 0, 0))],
                  out_specs=pl.BlockSpec((tm, N_DEV * ns), lambda i: (i, 0)),
                  out_shape=jax.ShapeDtypeStruct((s, N_DEV * ns), F32),
                  compiler_params=_params(("parallel",)))(p, w_g)


def _d_ple_proj(p, dpp, ns, tk=1024):
    s, kd = p.shape
    tk = _tile(s, tk)
    nk = s // tk

    def body(p_ref, d_ref, of_ref, ob_ref, acc):
        k = pl.program_id(0)

        @pl.when(k == 0)
        def _():
            acc[...] = jnp.zeros_like(acc)

        pv = p_ref[...].astype(BF16)
        for j in range(N_DEV):
            acc[j] += lax.dot_general(pv, d_ref[:, j * ns:(j + 1) * ns], _TN, preferred_element_type=F32)

        @pl.when(k == nk - 1)
        def _():
            of_ref[...] = acc[...]
            ob_ref[...] = acc[...].astype(BF16)

    whole = pl.BlockSpec((N_DEV, kd, ns), lambda k: (0, 0, 0))
    return _pcall(body, name="d_w_ple_proj", grid=(nk,),
                  in_specs=[pl.BlockSpec((tk, kd), lambda k: (k, 0)),
                            pl.BlockSpec((tk, N_DEV * ns), lambda k: (k, 0))],
                  out_specs=[whole, whole],
                  out_shape=[jax.ShapeDtypeStruct((N_DEV, kd, ns), F32), jax.ShapeDtypeStruct((N_DEV, kd, ns), BF16)],
                  scratch_shapes=[pltpu.VMEM((N_DEV, kd, ns), F32)],
                  compiler_params=_params(("arbitrary",)))(p, dpp)


def _ep_ple_loss(gl, h2, pp, tgt, g_final, outs):
    loss_ref, dh3_ref, dgl_ref, dpp_ref, dg_ref = outs
    gate = jax.nn.sigmoid(gl)
    ppv = pp[...]
    h3 = h2[...] + gate * ppv
    r = lax.rsqrt(jnp.mean(h3 * h3, axis=-1, keepdims=True) + EPS)
    hn = h3 * r
    gv = g_final[...]
    diff = hn * gv - tgt[...]
    row = jnp.mean(diff * diff, axis=-1, keepdims=True)
    loss_ref[...] += 0.5 * jnp.sum(row, axis=0, keepdims=True)
    dy = diff * (1.0 / h3.shape[-1])
    dg_ref[...] += jnp.sum(dy * hn, axis=0, keepdims=True)
    dhn = dy * gv
    dh3 = r * (dhn - hn * jnp.mean(dhn * hn, axis=-1, keepdims=True))
    dh3_ref[...] = dh3
    dgl_ref[...] = (dh3 * ppv * gate * (1.0 - gate)).astype(BF16)
    dpp_ref[...] = (dh3 * gate).astype(BF16)


def _ple_gate_loss(h2, g_ple, w_gate, pp, tgt, g_final, tm=512):
    s, d = h2.shape
    tm = _tile(s, tm)
    rows = pl.BlockSpec((tm, d), lambda i, j, k: (i, 0))
    vec = pl.BlockSpec((1, d), lambda i, j, k: (0, 0))
    return _matmul(
        "ple_gate_loss", h2, w_gate, dims=_NN, grid=(s // tm, 1, 1),
        a_spec=rows, b_spec=pl.BlockSpec((d, d), lambda i, j, k: (0, 0)), acc_shape=(tm, d),
        out_shapes=[jax.ShapeDtypeStruct((1, LANES), F32), jax.ShapeDtypeStruct((s, d), F32),
                    jax.ShapeDtypeStruct((s, d), BF16), jax.ShapeDtypeStruct((s, d), BF16),
                    jax.ShapeDtypeStruct((1, d), F32), jax.ShapeDtypeStruct((s, d), BF16)],
        out_specs=[pl.BlockSpec((1, LANES), lambda i, j, k: (0, 0)), rows, rows, rows, vec, rows],
        epilogue=_ep_ple_loss, extras=(h2, pp, tgt, g_final, g_ple), extra_specs=[rows, rows, rows, vec, vec],
        sequential=True, lhs_norm=True)


def _low_half():
    return lax.broadcasted_iota(jnp.int32, (1, LANES), 1) < HEAD_DIM


def _half_mean(v, low):
    s_lo = jnp.sum(jnp.where(low, v, 0.0), axis=-1, keepdims=True)
    s_hi = jnp.sum(jnp.where(low, 0.0, v), axis=-1, keepdims=True)
    return jnp.where(low, s_lo, s_hi) * (1.0 / HEAD_DIM)


def _head_norm_bwd(val, dout, g, low):
    r = lax.rsqrt(_half_mean(val * val, low) + EPS)
    vn = val * r
    dvn = dout * g
    return r * (dvn - vn * _half_mean(dvn * vn, low)), dout * vn


def _conv_taps(vv_ext, w_ref):
    v0 = vv_ext[HALO:]
    v1 = pltpu.roll(vv_ext, 1, 0)[HALO:]
    v2 = pltpu.roll(vv_ext, 2, 0)[HALO:]
    return w_ref[2:3, :] * v0 + w_ref[1:2, :] * v1 + w_ref[0:1, :] * v2, (v0, v1, v2)


def _conv_fwd(proj, conv_w, g_conv, w_conv, d_model, tr=1024):
    s = proj.shape[0]
    tr = _tile(s, tr)
    hb = tr // HALO

    def main(part):
        return pl.BlockSpec((tr, w_conv), lambda i: (i, part))

    def prev(part):
        return pl.BlockSpec((HALO, w_conv), lambda i: (jnp.maximum(i * hb - 1, 0), part))

    def body(cb_ref, cc_ref, cu_ref, ccp_ref, cup_ref, w_ref, g_ref, o_ref):
        i = pl.program_id(0)
        low = _low_half()
        for j in range(w_conv // LANES):
            cols = slice(j * LANES, (j + 1) * LANES)
            vv_prev = jnp.where(i > 0, ccp_ref[:, cols] * cup_ref[:, cols], 0.0)
            vv_ext = jnp.concatenate([vv_prev, cc_ref[:, cols] * cu_ref[:, cols]], axis=0)
            y, _ = _conv_taps(vv_ext, w_ref.at[:, cols])
            co = cb_ref[:, cols] * y
            r = lax.rsqrt(_half_mean(co * co, low) + EPS)
            o_ref[:, cols] = (co * r * g_ref[:, cols]).astype(BF16)

    return _pcall(
        body, name="conv_fwd", grid=(s // tr,),
        in_specs=[main(0), main(1), main(2), prev(1), prev(2),
                  pl.BlockSpec((CONV_K, w_conv), lambda i: (0, 0)),
                  pl.BlockSpec((1, w_conv), lambda i: (0, 0))],
        out_specs=pl.BlockSpec((tr, w_conv), lambda i: (i, 0)),
        out_shape=jax.ShapeDtypeStruct((s, d_model), BF16),
        compiler_params=_params(("parallel",)),
    )(proj, proj, proj, proj, proj, conv_w, g_conv)


def _conv_bwd(proj, dcat, conv_w, g_conv, dproj, w_conv, tr=1024):
    s = proj.shape[0]
    tr = _tile(s, tr)
    hb = tr // HALO
    last = s // HALO - 1
    nt = s // tr

    def main(part):
        return pl.BlockSpec((tr, w_conv), lambda i: (i, part))

    def prev(part):
        return pl.BlockSpec((HALO, w_conv), lambda i: (jnp.maximum(i * hb - 1, 0), part))

    def nxt(part):
        return pl.BlockSpec((HALO, w_conv), lambda i: (jnp.minimum((i + 1) * hb, last), part))

    def body(cb_ref, cc_ref, cu_ref, dc_ref, ccp_ref, cup_ref, cbn_ref, ccn_ref, cun_ref, dcn_ref,
             w_ref, g_ref, dproj_in, dproj_ref, dw_ref, dg_ref):
        del dproj_in
        i = pl.program_id(0)

        @pl.when(i == 0)
        def _():
            dw_ref[...] = jnp.zeros_like(dw_ref)
            dg_ref[...] = jnp.zeros_like(dg_ref)

        low = _low_half()
        n_ext = tr + HALO
        rowid = lax.broadcasted_iota(jnp.int32, (n_ext, 1), 0)
        for j in range(w_conv // LANES):
            cols = slice(j * LANES, (j + 1) * LANES)
            wj = w_ref.at[:, cols]
            cc, cu = cc_ref[:, cols], cu_ref[:, cols]
            vv_prev = jnp.where(i > 0, ccp_ref[:, cols] * cup_ref[:, cols], 0.0)
            vv_ext = jnp.concatenate([vv_prev, cc * cu, ccn_ref[:, cols] * cun_ref[:, cols]], axis=0)
            y_ext, (v0, v1, v2) = _conv_taps(vv_ext, wj)
            cb_ext = jnp.concatenate([cb_ref[:, cols], cbn_ref[:, cols]], axis=0)
            dc_ext = jnp.concatenate([dc_ref[:, cols], dcn_ref[:, cols]], axis=0)
            dco, dgn = _head_norm_bwd(cb_ext * y_ext, dc_ext, g_ref[:, cols], low)
            dyc = jnp.where((rowid < tr) | (i < nt - 1), dco * cb_ext, 0.0)
            dvv = (wj[2:3, :] * dyc[:tr] + wj[1:2, :] * pltpu.roll(dyc, n_ext - 1, 0)[:tr]
                   + wj[0:1, :] * pltpu.roll(dyc, n_ext - 2, 0)[:tr])
            dproj_ref[:, cols] = (dco[:tr] * y_ext[:tr]).astype(BF16)
            dproj_ref[:, w_conv + j * LANES:w_conv + (j + 1) * LANES] = (dvv * cu).astype(BF16)
            dproj_ref[:, 2 * w_conv + j * LANES:2 * w_conv + (j + 1) * LANES] = (dvv * cc).astype(BF16)
            dyt = dyc[:tr]
            for tap, shifted in enumerate((v2, v1, v0)):
                dw_ref[tap:tap + 1, cols] += jnp.sum(dyt * shifted[:tr], axis=0, keepdims=True)
            dg_ref[:, cols] += jnp.sum(dgn[:tr], axis=0, keepdims=True)

    n_cols = dproj.shape[1]
    return _pcall(
        body, name="conv_bwd", grid=(nt,),
        in_specs=[main(0), main(1), main(2), main(0),
                  prev(1), prev(2), nxt(0), nxt(1), nxt(2), nxt(0),
                  pl.BlockSpec((CONV_K, w_conv), lambda i: (0, 0)),
                  pl.BlockSpec((1, w_conv), lambda i: (0, 0)),
                  pl.BlockSpec(memory_space=pl.ANY)],
        out_specs=[pl.BlockSpec((tr, 3 * w_conv), lambda i: (i, 0)),
                   pl.BlockSpec((CONV_K, w_conv), lambda i: (0, 0)),
                   pl.BlockSpec((1, w_conv), lambda i: (0, 0))],
        out_shape=[jax.ShapeDtypeStruct((s, n_cols), BF16),
                   jax.ShapeDtypeStruct((CONV_K, w_conv), F32),
                   jax.ShapeDtypeStruct((1, w_conv), F32)],
        input_output_aliases={12: 0},
        compiler_params=_params(("arbitrary",)),
    )(proj, proj, proj, dcat, proj, proj, proj, proj, proj, dcat, conv_w, g_conv, dproj)


STRIP = 16

ALL_CHAINS = (0, 1, 2, 3)
UPPER_CHAINS = (2, 3)


RUN_FLOOR = -104.0


def _any_weight_left(run_s):
    return (jnp.max(run_s[...]) > RUN_FLOOR).astype(jnp.int32)


def _chains(low):
    return [(2 * half + h, half, msk) for half in range(2)
            for h, msk in enumerate((low, jnp.logical_not(low)))]


def _suffix_operator(t):
    r = lax.broadcasted_iota(jnp.int32, (2 * t, t), 0)
    c = lax.broadcasted_iota(jnp.int32, (2 * t, t), 1)
    return jnp.where((r > c) & ((r < t) | (r - t > c)), 1.0, 0.0).astype(BF16)


def _strips(t, diag):
    return [(i, slice(i * STRIP, (i + 1) * STRIP), t // 2 if diag and (i + 1) * STRIP <= t // 2 else t)
            for i in range(t // STRIP)]


def _strip_mask(i, w):
    r = lax.broadcasted_iota(jnp.int32, (STRIP, w), 0) + i * STRIP
    c = lax.broadcasted_iota(jnp.int32, (STRIP, w), 1)
    return r > c


def _store_trimmed(ref, rows, val, w, t, at=0):
    ref[rows, at:at + w] = val
    if w < t:
        ref[rows, at + w:at + t] = jnp.zeros((STRIP, t - w), val.dtype)


def _store_split(ref, rows, val, w, t):
    hi = val.astype(BF16)
    _store_trimmed(ref, rows, hi, w, t)
    _store_trimmed(ref, rows, (val - hi.astype(F32)).astype(BF16), w, t, at=t)


def _sb_scores(z_s, split_s, zl_s, tot_s, keep_s, t, diag):
    for i, rows, w in _strips(t, diag):
        z = z_s[rows, :w]
        log_beta = jnp.minimum(z, 0.0) - jnp.log(1.0 + jnp.exp(-jnp.abs(z)))
        log_keep = log_beta - z
        if diag:
            log_keep = jnp.where(_strip_mask(i, w), log_keep, 0.0)
        _store_trimmed(split_s, rows, log_keep.astype(BF16), w, t)
        zl_s[rows, :w] = log_beta
        tot_s[rows, :] = _row_sum(log_keep)
        if keep_s is not None:
            keep_s[rows, :w] = jnp.exp(log_keep)


def _row_sum(v):
    return jnp.broadcast_to(jnp.sum(v, axis=-1, keepdims=True), (v.shape[0], LANES))


def _wide(r, t):
    return jnp.concatenate([r] * (t // LANES), axis=1)


def _sb_weights(zl_s, suf_s, run_s, tot_s, a_s, t, diag, da_s=None, glog_s=None, gsplit_s=None, gtot_s=None):
    for i, rows, w in _strips(t, diag):
        run = run_s[rows, :]
        a = jnp.exp(zl_s[rows, :w] + suf_s[rows, :w] + _wide(run, w))
        if diag:
            a = jnp.where(_strip_mask(i, w), a, 0.0)
        ab = a.astype(BF16)
        _store_trimmed(a_s, rows, ab, w, t)
        run_s[rows, :] = run + tot_s[rows, :]
        if da_s is not None:
            glog = ab.astype(F32) * da_s[rows, :w]
            glog_s[rows, :w] = glog
            _store_split(gsplit_s, rows, glog, w, t)
            gtot_s[rows, :] = _row_sum(glog)


def _sb_dscores(glog_s, cum_s, rest_s, gtot_s, keep_s, dz_s, t, diag):
    for i, rows, w in _strips(t, diag):
        glog = glog_s[rows, :w]
        rest = rest_s[rows, :]
        from_here = _wide(rest, w) - cum_s[rows, :w]
        before = from_here - glog
        dz = from_here * keep_s[rows, :w] - before
        if diag:
            dz = jnp.where(_strip_mask(i, w), dz, 0.0)
        _store_trimmed(dz_s, rows, dz.astype(BF16), w, t)
        rest_s[rows, :] = rest - gtot_s[rows, :]


def _attn_fwd(proj, g_attn, cat, w_conv, carry, t=ATTN_BLOCK):
    s = proj.shape[0]
    w_attn = g_attn.shape[1]
    nh = w_attn // LANES
    t = _tile(s, t)
    tq = 2 * t
    nq = s // tq
    q0 = 3 * w_conv // LANES
    scale = HEAD_DIM ** -0.5
    plan = _Carried(carry)
    nw, n_res = len(plan.inputs), len(plan.out_shapes)

    def body(q_ref, k_ref, v_ref, g_ref, cat_in, *rest):
        staged_refs, rest = rest[:nw], rest[nw:]
        o_ref, cat_ref = rest[:2]
        gathered_refs, rest = rest[2:2 + n_res], rest[2 + n_res:]
        kb, vb, tri_s, qm_s, z_s, split_s, zl_s, suf_s, a_s, run_s, tot_s, acc_s = rest[:12]
        gather_sems = rest[12:]
        del cat_in
        qi = pl.program_id(1)

        @pl.when((pl.program_id(0) == 0) & (qi == 0))
        def _():
            for cp in plan.copies(staged_refs, gathered_refs, gather_sems):
                cp.start()

        @pl.when(qi == 0)
        def _():
            kb[...] = k_ref[...].astype(BF16)
            vb[...] = v_ref[...].astype(BF16)
            tri_s[...] = _suffix_operator(t)

        low = _low_half()
        for c, half, msk in _chains(low):
            qm_s[c] = jnp.where(msk, q_ref[half * t:(half + 1) * t, :] * scale, 0.0).astype(BF16)
            run_s[c] = jnp.zeros((t, LANES), F32)
            acc_s[c] = jnp.zeros((t, LANES), F32)

        def key_rows(kblk):
            return pl.ds(pl.multiple_of(kblk * t, t), t)

        def key_block(base, c):
            return key_rows(jnp.maximum(base + c // 2, 0))

        def scores_matmul(base, chains):
            for c in chains:
                z_s[c] = lax.dot_general(qm_s[c], kb[key_block(base, c), :], _NT, preferred_element_type=F32)

        def front(modes, base, prev=None):
            for c, diag in modes:
                _sb_scores(z_s.at[c], split_s.at[c], zl_s.at[c], tot_s.at[c], None, t, diag)
                suf_s[c] = jnp.dot(split_s[c, :, 0:t], tri_s[0:t, :], preferred_element_type=F32)
            if prev is not None:
                tail(*prev)
            scores_matmul(base - 1, ALL_CHAINS)
            for c, diag in modes:
                _sb_weights(zl_s.at[c], suf_s.at[c], run_s.at[c], tot_s.at[c], a_s.at[c], t, diag)

        def tail(base, chains):
            for c in chains:
                acc_s[c] += jnp.dot(a_s[c], vb[key_block(base, c), :], preferred_element_type=F32)

        first = 2 * qi
        scores_matmul(first, ALL_CHAINS)
        front([(c, True) for c in ALL_CHAINS], first)

        def loop(state):
            it = state[0]
            base = first - 1 - it
            front([(c, False) for c in ALL_CHAINS], base, prev=(base + 1, ALL_CHAINS))
            return it + 1, _any_weight_left(run_s)

        done, live = lax.while_loop(lambda state: (state[0] < first) & (state[1] > 0), loop,
                                    (jnp.int32(0), jnp.int32(1)))
        one_more = (done == first) & (live > 0)

        @pl.when(one_more)
        def _():
            front([(c, False) for c in UPPER_CHAINS], -1, prev=(0, ALL_CHAINS))
            tail(-1, UPPER_CHAINS)

        @pl.when(jnp.logical_not(one_more))
        def _():
            tail(first - done, ALL_CHAINS)

        for half in range(2):
            rows = slice(half * t, (half + 1) * t)
            o = jnp.where(low, acc_s[2 * half], acc_s[2 * half + 1])
            o_ref[rows, :] = o
            r = lax.rsqrt(_half_mean(o * o, low) + EPS)
            cat_ref[rows, :] = (o * r * g_ref[...]).astype(BF16)

        @pl.when((pl.program_id(0) == nh - 1) & (qi == nq - 1))
        def _():
            for cp in plan.copies(staged_refs, gathered_refs, gather_sems):
                cp.wait()

    whole = lambda col0: pl.BlockSpec((s, LANES), lambda h, i: (0, col0 + h))
    n_ch = len(ALL_CHAINS)
    res = _pcall(
        body, name="attn_fwd", grid=(nh, nq),
        in_specs=[pl.BlockSpec((tq, LANES), lambda h, i: (i, q0 + h)),
                  whole(q0 + nh), whole(q0 + 2 * nh),
                  pl.BlockSpec((1, LANES), lambda h, i: (0, h)),
                  pl.BlockSpec(memory_space=pl.ANY)] + [pl.BlockSpec(memory_space=pl.ANY)] * nw,
        out_specs=[pl.BlockSpec((tq, LANES), lambda h, i: (i, h)),
                   pl.BlockSpec((tq, LANES), lambda h, i: (i, w_conv // LANES + h))]
        + [pl.BlockSpec(memory_space=pl.ANY)] * n_res,
        out_shape=[jax.ShapeDtypeStruct((s, w_attn), F32),
                   jax.ShapeDtypeStruct(cat.shape, BF16)] + plan.out_shapes,
        scratch_shapes=[pltpu.VMEM((s, LANES), BF16), pltpu.VMEM((s, LANES), BF16),
                        pltpu.VMEM((2 * t, t), BF16),
                        pltpu.VMEM((n_ch, t, LANES), BF16),
                        pltpu.VMEM((n_ch, t, t), F32),
                        pltpu.VMEM((n_ch, t, 2 * t), BF16),
                        pltpu.VMEM((n_ch, t, t), F32),
                        pltpu.VMEM((n_ch, t, t), F32),
                        pltpu.VMEM((n_ch, t, t), BF16),
                        pltpu.VMEM((n_ch, t, LANES), F32),
                        pltpu.VMEM((n_ch, t, LANES), F32),
                        pltpu.VMEM((n_ch, t, LANES), F32)]
        + plan.sems,
        input_output_aliases={4: 1, **plan.aliases(5, 2)},
        compiler_params=_params(("arbitrary", "arbitrary")),
    )(proj, proj, proj, g_attn, cat, *plan.inputs)
    return res[0], res[1], res[2:]


def _attn_bwd(proj, o, dcat, g_attn, w_conv, carry, t=ATTN_BLOCK):
    s, n_cols = proj.shape
    w_attn = g_attn.shape[1]
    nh = w_attn // LANES
    t = _tile(s, t)
    tq = 2 * t
    nq = s // tq
    q0 = 3 * w_conv // LANES
    scale = HEAD_DIM ** -0.5
    plan = _Carried(carry)
    nw, n_res = len(plan.inputs), len(plan.out_shapes)

    def body(q_ref, k_ref, v_ref, o_ref, do_ref, g_ref, *rest):
        partial_refs, rest = rest[:nw], rest[nw:]
        dproj_ref, dg_ref = rest[:2]
        received_refs, rest = rest[2:2 + n_res], rest[2 + n_res:]
        (kb, vb, dkt_acc, dvt_acc, stash, tri_s, qm_s, dom_s, qt_s, dot_s, z_s, da_s, split_s, zl_s,
         keep_s, suf_s, a_s, glog_s, gsplit_s, cum_s, dz_s, run_s, tot_s, rest_s, gtot_s, dq_s) = rest[:26]
        out_sems, scatter_sems = rest[26], rest[27:]
        step_i = pl.program_id(1)
        qi = nq - 1 - step_i
        head_pair = pl.program_id(0)
        first_step = (head_pair == 0) & (step_i == 0)
        last_step = (head_pair == nh - 1) & (step_i == nq - 1)

        @pl.when(first_step)
        def _():
            for cp in plan.copies(partial_refs, received_refs, scatter_sems):
                cp.start()

        def out_copies():
            rows = pl.ds(pl.multiple_of(qi * tq, tq), tq)
            return [pltpu.make_async_copy(
                stash.at[w], dproj_ref.at[rows, pl.ds(pl.multiple_of((q0 + w * nh + head_pair) * LANES, LANES), LANES)],
                out_sems.at[w]) for w in range(3)]

        def walk():
            @pl.when(step_i == 0)
            def _():
                kb[...] = k_ref[...].astype(BF16)
                vb[...] = v_ref[...].astype(BF16)
                tri_s[...] = _suffix_operator(t)
                dkt_acc[...] = jnp.zeros_like(dkt_acc)
                dvt_acc[...] = jnp.zeros_like(dvt_acc)
                dg_ref[...] = jnp.zeros_like(dg_ref)

            low = _low_half()
            gv = g_ref[...]
            for half in range(2):
                rows = slice(half * t, (half + 1) * t)
                q = q_ref[rows, :] * scale
                ov = o_ref[rows, :]
                d_o, dgn = _head_norm_bwd(ov, do_ref[rows, :], gv, low)
                dg_ref[...] += jnp.sum(dgn, axis=0, keepdims=True)
                for h, msk in enumerate((low, jnp.logical_not(low))):
                    c = 2 * half + h
                    qh = jnp.where(msk, q, 0.0)
                    doh = jnp.where(msk, d_o, 0.0)
                    dom = doh.astype(BF16)
                    qm_s[c] = qh.astype(BF16)
                    dom_s[c] = dom
                    qt_s[c] = qh.T.astype(BF16)
                    dot_s[c] = doh.T.astype(BF16)
                    rest_s[c] = _row_sum(dom.astype(F32) * ov)
                    run_s[c] = jnp.zeros((t, LANES), F32)
                    dq_s[c] = jnp.zeros((t, LANES), F32)

            def key_rows(kblk):
                return pl.ds(pl.multiple_of(kblk * t, t), t)

            def block_of(base, half):
                return jnp.maximum(base + half, 0)

            def scores_matmul(base, chains):
                for c in chains:
                    ks = kb[key_rows(block_of(base, c // 2)), :]
                    z_s[c] = lax.dot_general(qm_s[c], ks, _NT, preferred_element_type=F32)

            def da_matmul(base, chains):
                for c in chains:
                    vs = vb[key_rows(block_of(base, c // 2)), :]
                    da_s[c] = lax.dot_general(dom_s[c], vs, _NT, preferred_element_type=F32)

            def front(modes, base, prev=None):
                if prev is not None:
                    tail(*prev)
                for c, diag in modes:
                    _sb_scores(z_s.at[c], split_s.at[c], zl_s.at[c], tot_s.at[c], keep_s.at[c], t, diag)
                    suf_s[c] = jnp.dot(split_s[c, :, 0:t], tri_s[0:t, :], preferred_element_type=F32)
                scores_matmul(base - 1, ALL_CHAINS)
                for c, diag in modes:
                    _sb_weights(zl_s.at[c], suf_s.at[c], run_s.at[c], tot_s.at[c], a_s.at[c], t, diag,
                                da_s.at[c], glog_s.at[c], gsplit_s.at[c], gtot_s.at[c])
                    cum_s[c] = jnp.dot(gsplit_s[c], tri_s[...], preferred_element_type=F32)
                da_matmul(base - 1, ALL_CHAINS)
                for c, diag in modes:
                    _sb_dscores(glog_s.at[c], cum_s.at[c], rest_s.at[c], gtot_s.at[c], keep_s.at[c],
                                dz_s.at[c], t, diag)

            def tail(base, chains):
                for half in range(2):
                    mine = [c for c in chains if c // 2 == half]
                    if not mine:
                        continue
                    kblk = block_of(base, half)
                    ks = kb[key_rows(kblk), :]
                    dkt = dkt_acc[kblk]
                    dvt = dvt_acc[kblk]
                    for c in mine:
                        dq_s[c] += jnp.dot(dz_s[c], ks, preferred_element_type=F32)
                        dkt = dkt + jnp.dot(qt_s[c], dz_s[c], preferred_element_type=F32)
                        dvt = dvt + jnp.dot(dot_s[c], a_s[c], preferred_element_type=F32)
                    dkt_acc[kblk] = dkt
                    dvt_acc[kblk] = dvt

            first = 2 * qi
            scores_matmul(first, ALL_CHAINS)
            da_matmul(first, ALL_CHAINS)
            front([(c, True) for c in ALL_CHAINS], first)

            def loop(state):
                it = state[0]
                base = first - 1 - it
                front([(c, False) for c in ALL_CHAINS], base, prev=(base + 1, ALL_CHAINS))
                return it + 1, _any_weight_left(run_s)

            done, live = lax.while_loop(lambda state: (state[0] < first) & (state[1] > 0), loop,
                                        (jnp.int32(0), jnp.int32(1)))
            one_more = (done == first) & (live > 0)

            @pl.when(one_more)
            def _():
                front([(c, False) for c in UPPER_CHAINS], -1, prev=(0, ALL_CHAINS))
                tail(-1, UPPER_CHAINS)

            @pl.when(jnp.logical_not(one_more))
            def _():
                tail(first - done, ALL_CHAINS)

            @pl.when(jnp.logical_not(first_step))
            def _():
                for cp in out_copies():
                    cp.wait()

            for half in range(2):
                rows = slice(half * t, (half + 1) * t)
                stash[0, rows, :] = (jnp.where(low, dq_s[2 * half], dq_s[2 * half + 1]) * scale).astype(BF16)
                stash[1, rows, :] = dkt_acc[2 * qi + half].T.astype(BF16)
                stash[2, rows, :] = dvt_acc[2 * qi + half].T.astype(BF16)
            for cp in out_copies():
                cp.start()

        walk()

        @pl.when(last_step)
        def _():
            for cp in out_copies():
                cp.wait()
            for cp in plan.copies(partial_refs, received_refs, scatter_sems):
                cp.wait()

    whole = lambda col0: pl.BlockSpec((s, LANES), lambda h, i: (0, col0 + h))
    blk = lambda col0: pl.BlockSpec((tq, LANES), lambda h, i: (nq - 1 - i, col0 + h))
    n_ch = len(ALL_CHAINS)
    res = _pcall(
        body, name="attn_bwd", grid=(nh, nq),
        in_specs=[blk(q0), whole(q0 + nh), whole(q0 + 2 * nh), blk(0), blk(w_conv // LANES),
                  pl.BlockSpec((1, LANES), lambda h, i: (0, h))] + [pl.BlockSpec(memory_space=pl.ANY)] * nw,
        out_specs=[pl.BlockSpec(memory_space=pl.ANY),
                   pl.BlockSpec((1, LANES), lambda h, i: (0, h))] + [pl.BlockSpec(memory_space=pl.ANY)] * n_res,
        out_shape=[jax.ShapeDtypeStruct((s, n_cols), BF16), jax.ShapeDtypeStruct((1, w_attn), F32)]
        + plan.out_shapes,
        scratch_shapes=[pltpu.VMEM((s, LANES), BF16), pltpu.VMEM((s, LANES), BF16),
                        pltpu.VMEM((s // t, LANES, t), F32),
                        pltpu.VMEM((s // t, LANES, t), F32),
                        pltpu.VMEM((3, tq, LANES), BF16),
                        pltpu.VMEM((2 * t, t), BF16),
                        pltpu.VMEM((n_ch, t, LANES), BF16),
                        pltpu.VMEM((n_ch, t, LANES), BF16),
                        pltpu.VMEM((n_ch, LANES, t), BF16),
                        pltpu.VMEM((n_ch, LANES, t), BF16),
                        pltpu.VMEM((n_ch, t, t), F32),
                        pltpu.VMEM((n_ch, t, t), F32),
                        pltpu.VMEM((n_ch, t, 2 * t), BF16),
                        pltpu.VMEM((n_ch, t, t), F32),
                        pltpu.VMEM((n_ch, t, t), F32),
                        pltpu.VMEM((n_ch, t, t), F32),
                        pltpu.VMEM((n_ch, t, t), BF16),
                        pltpu.VMEM((n_ch, t, t), F32),
                        pltpu.VMEM((n_ch, t, 2 * t), BF16),
                        pltpu.VMEM((n_ch, t, t), F32),
                        pltpu.VMEM((n_ch, t, t), BF16),
                        pltpu.VMEM((n_ch, t, LANES), F32),
                        pltpu.VMEM((n_ch, t, LANES), F32),
                        pltpu.VMEM((n_ch, t, LANES), F32),
                        pltpu.VMEM((n_ch, t, LANES), F32),
                        pltpu.VMEM((n_ch, t, LANES), F32),
                        pltpu.SemaphoreType.DMA((3,))]
        + plan.sems,
        input_output_aliases=plan.aliases(6, 2),
        compiler_params=_params(("arbitrary", "arbitrary")),
    )(proj, proj, proj, o, dcat, g_attn, *plan.inputs)
    return res[0], res[1], res[2:]


def _place():
    return lax.axis_index("x"), lax.axis_index("y"), lax.axis_index("c")


def _other_chips(x, y):
    return [(1 - x, y), (x, 1 - y), (1 - x, 1 - y)]


def _slot(px, py, pc):
    return 4 * px + 2 * py + pc


def _all_gather(shards, out_dtypes):
    nw = len(shards)

    def body(*refs):
        ins, outs, stage = refs[:nw], refs[nw:2 * nw], refs[2 * nw:3 * nw]
        send_sems, recv_sems, local_sems = refs[3 * nw:]
        x, y, c = _place()
        me, sibling = (x, y, c), (x, y, 1 - c)
        chips = _other_chips(x, y)

        def copy(w, k, block, to, src=None):
            dst = outs[w].at[_slot(*block)]
            return pltpu.make_async_remote_copy(
                src_ref=dst if src is None else src, dst_ref=dst,
                send_sem=send_sems.at[w * 7 + k], recv_sem=recv_sems.at[w * 7 + k],
                device_id=to, device_id_type=MESH)

        started = []
        local = []
        for w in range(nw):
            stage[w][...] = ins[w][...].astype(stage[w].dtype)
            cp = pltpu.make_async_copy(stage[w], outs[w].at[_slot(*me)], local_sems.at[w])
            cp.start()
            local.append(cp)
            started.append(copy(w, 0, me, sibling, src=stage[w]))
            started[-1].start()
            for j, chip in enumerate(chips):
                started.append(copy(w, 1 + j, me, (*chip, c), src=stage[w]))
                started[-1].start()
        for j, chip in enumerate(chips):
            for w in range(nw):
                copy(w, 1 + j, (*chip, c), me).wait_recv()
                started.append(copy(w, 4 + j, (*chip, c), sibling))
                started[-1].start()
        for w in range(nw):
            copy(w, 0, sibling, me).wait_recv()
            for j, chip in enumerate(chips):
                copy(w, 4 + j, (*chip, 1 - c), me).wait_recv()
        for cp in started:
            cp.wait_send()
        for cp in local:
            cp.wait()

    return _pcall(
        body, name="all_gather_weights",
        in_specs=[pl.BlockSpec(memory_space=pltpu.VMEM)] * nw,
        out_specs=[pl.BlockSpec(memory_space=pl.ANY)] * nw,
        out_shape=[jax.ShapeDtypeStruct((N_DEV, *a.shape), d) for a, d in zip(shards, out_dtypes)],
        scratch_shapes=[pltpu.VMEM(a.shape, d) for a, d in zip(shards, out_dtypes)]
        + [pltpu.SemaphoreType.DMA((7 * nw,)), pltpu.SemaphoreType.DMA((7 * nw,)),
           pltpu.SemaphoreType.DMA((nw,))],
        compiler_params=_params(),
    )(*shards)


N_PEERS = N_DEV - 1


def _peer(k):
    x, y, c = _place()
    return (x ^ (k >> 2), y ^ ((k >> 1) & 1), c ^ (k & 1))


def _remote(src, dst, sems, index, to):
    return pltpu.make_async_remote_copy(src_ref=src, dst_ref=dst, send_sem=sems[0].at[index],
                                        recv_sem=sems[1].at[index], device_id=to, device_id_type=MESH)


def _gather_out_copies(staged, gathered, sems):
    x, y, c = _place()
    me = _slot(x, y, c)
    targets = [(x, y, 1 - c)] + [(*chip, c) for chip in _other_chips(x, y)]
    copies = []
    for w, (src, dst) in enumerate(zip(staged, gathered)):
        copies.append(pltpu.make_async_copy(src, dst.at[me], sems[2].at[w]))
        copies += [_remote(src, dst.at[me], sems, w * len(targets) + k, to) for k, to in enumerate(targets)]
    return copies


def _gather_pass_copies(arrived, gathered, sems):
    x, y, c = _place()
    chips = _other_chips(x, y)
    return [_remote(src.at[_slot(*chip, c)], dst.at[_slot(*chip, c)], sems, w * len(chips) + j, (x, y, 1 - c))
            for w, (src, dst) in enumerate(zip(arrived, gathered)) for j, chip in enumerate(chips)]


ALL_PEERS = tuple(range(1, N_DEV))


def _scatter_copies(partials, received, sems, peers=ALL_PEERS):
    me = _slot(*_place())
    return [_remote(src.at[me ^ k], dst.at[k - 1], sems, w * N_PEERS + k - 1, _peer(k))
            for w, (src, dst) in enumerate(zip(partials, received)) for k in peers]


class _Carried:
    def __init__(self, jobs):
        self.jobs = [(job[0], list(job[1]), job[2] if len(job) > 2 else ALL_PEERS) for job in jobs if len(job[1])]
        self.inputs, self.out_shapes, self.sems, self.counts = [], [], [], []
        for kind, arrays, _ in self.jobs:
            n_out = len(arrays) // 2 if kind == "scatter_more" else len(arrays)
            fan = {"gather_out": 4, "gather_pass": 3}.get(kind, N_PEERS)
            for a in arrays[len(arrays) - n_out:]:
                shape = {"gather_out": (N_DEV, *a.shape), "scatter": (N_PEERS, *a.shape[1:])}.get(kind, a.shape)
                self.out_shapes.append(jax.ShapeDtypeStruct(shape, BF16))
            job_sems = [pltpu.SemaphoreType.DMA((fan * n_out,))] * 2
            job_sems += [pltpu.SemaphoreType.DMA((n_out,))] if kind == "gather_out" else []
            self.inputs += arrays
            self.sems += job_sems
            self.counts.append((len(arrays), n_out, len(job_sems)))

    def aliases(self, first_input, first_output):
        pairs, at_in, at_out = {}, first_input, first_output
        for (kind, _, _), (n_in, n_out, _) in zip(self.jobs, self.counts):
            if kind in ("gather_pass", "scatter_more"):
                pairs.update({at_in + n_in - n_out + i: at_out + i for i in range(n_out)})
            at_in, at_out = at_in + n_in, at_out + n_out
        return pairs

    def copies(self, in_refs, out_refs, sem_refs):
        out, at_in, at_out, at_sem = [], 0, 0, 0
        for (kind, _, peers), (n_in, n_out, n_sems) in zip(self.jobs, self.counts):
            srcs, dsts = in_refs[at_in:at_in + n_out], out_refs[at_out:at_out + n_out]
            sems = sem_refs[at_sem:at_sem + n_sems]
            if kind == "gather_out":
                out += _gather_out_copies(srcs, dsts, sems)
            elif kind == "gather_pass":
                out += _gather_pass_copies(srcs, dsts, sems)
            else:
                out += _scatter_copies(srcs, dsts, sems, peers)
            at_in, at_out, at_sem = at_in + n_in, at_out + n_out, at_sem + n_sems
        return out


def _cast_shards(shards):
    def body(*refs):
        for src, dst in zip(refs[:len(shards)], refs[len(shards):]):
            dst[...] = src[...].astype(BF16)

    return _pcall(
        body, name="cast_shards",
        in_specs=[pl.BlockSpec(memory_space=pltpu.VMEM)] * len(shards),
        out_specs=[pl.BlockSpec(memory_space=pltpu.VMEM)] * len(shards),
        out_shape=[jax.ShapeDtypeStruct(a.shape, BF16) for a in shards],
        compiler_params=_params(),
    )(*shards)


def _all_reduce_small(packed):
    r = packed.shape[0]

    def body(x_ref, o_ref, gathered, send_sems, recv_sems):
        x, y, c = _place()
        me = _slot(x, y, c)
        gathered[me] = x_ref[...]
        copies = [_remote(x_ref, gathered.at[me], (send_sems, recv_sems), k - 1, _peer(k)) for k in ALL_PEERS]
        for cp in copies:
            cp.start()
        for cp in copies:
            cp.wait()
        total = gathered[0]
        for k in range(1, N_DEV):
            total = total + gathered[k]
        o_ref[...] = total

    return _pcall(
        body, name="all_reduce_small",
        in_specs=[pl.BlockSpec(memory_space=pltpu.VMEM)],
        out_specs=pl.BlockSpec(memory_space=pltpu.VMEM),
        out_shape=jax.ShapeDtypeStruct(packed.shape, F32),
        scratch_shapes=[pltpu.VMEM((N_DEV, r, LANES), F32),
                        pltpu.SemaphoreType.DMA((N_DEV - 1,)), pltpu.SemaphoreType.DMA((N_DEV - 1,))],
        compiler_params=_params(),
    )(packed)


def _adam_math(w, g, m, v):
    m = ADAM_B1 * m + (1.0 - ADAM_B1) * g
    v = ADAM_B2 * v + (1.0 - ADAM_B2) * jnp.square(g)
    m_hat = m / (1.0 - ADAM_B1 ** ADAM_STEP)
    v_hat = v / (1.0 - ADAM_B2 ** ADAM_STEP)
    delta = -ADAM_LR * (m_hat / (jnp.sqrt(v_hat) + ADAM_EPS) + ADAM_WD * w)
    return delta, m, v


ADAM_TILE_BYTES = 24 * 1024 * 1024


def _adam_sharded(name, own, received, w, m, v, place):
    r, cdim = w.shape
    row_bytes = 2 * cdim * (4 + 2 * N_PEERS + 3 * 4 + 4 * 4)
    tr = _tile(r, max(LANES, ADAM_TILE_BYTES // row_bytes // LANES * LANES)) if r % LANES == 0 else r

    def body(place_ref, own_ref, rec_ref, w_ref, m_ref, v_ref, g_ref, d_ref, nm_ref, nv_ref):
        del place_ref
        g = own_ref[...]
        for j in range(N_PEERS):
            g = g + rec_ref[j].astype(F32)
        delta, nm, nv = _adam_math(w_ref[...], g, m_ref[...], v_ref[...])
        g_ref[...] = g
        d_ref[...] = delta
        nm_ref[...] = nm
        nv_ref[...] = nv

    blk = pl.BlockSpec((tr, cdim), lambda i, pr: (i, 0))
    grid_spec = pltpu.PrefetchScalarGridSpec(
        num_scalar_prefetch=1, grid=(r // tr,),
        in_specs=[pl.BlockSpec((None, tr, cdim), lambda i, pr: (4 * pr[0] + 2 * pr[1] + pr[2], i, 0)),
                  pl.BlockSpec((N_PEERS, tr, cdim), lambda i, pr: (0, i, 0)), blk, blk, blk],
        out_specs=[blk] * 4)
    return _pcall(body, name=name, grid_spec=grid_spec,
                  out_shape=[jax.ShapeDtypeStruct((r, cdim), F32)] * 4,
                  compiler_params=_params(("parallel",)))(place, own, received, w, m, v)


def _adam_small(w, g, m, v):
    def body(w_ref, g_ref, m_ref, v_ref, d_ref, nm_ref, nv_ref):
        delta, nm, nv = _adam_math(w_ref[...], g_ref[...], m_ref[...], v_ref[...])
        d_ref[...] = delta
        nm_ref[...] = nm
        nv_ref[...] = nv

    return _pcall(body, name="adam_small",
                  in_specs=[pl.BlockSpec(memory_space=pltpu.VMEM)] * 4,
                  out_specs=[pl.BlockSpec(memory_space=pltpu.VMEM)] * 3,
                  out_shape=[jax.ShapeDtypeStruct(w.shape, F32)] * 3,
                  compiler_params=_params())(w, g, m, v)


def _rows(vec):
    return vec.reshape(-1, LANES)


def kernel(x, p, g_mix, w_in, conv_w, g_conv_out, g_attn_out, w_out, g_mlp, w_up, w_down, g_ple, w_ple_gate, w_ple_proj, g_final, loss_target, m_g_mix, m_w_in, m_conv_w, m_g_conv_out, m_g_attn_out, m_w_out, m_g_mlp, m_w_up, m_w_down, m_g_ple, m_w_ple_gate, m_w_ple_proj, m_g_final, v_g_mix, v_w_in, v_conv_w, v_g_conv_out, v_g_attn_out, v_w_out, v_g_mlp, v_w_up, v_w_down, v_g_ple, v_w_ple_gate, v_w_ple_proj, v_g_final):
    s, d = x.shape[1], x.shape[2]
    w_conv = g_conv_out.shape[1]
    w_attn = g_attn_out.shape[1]
    cw = conv_w.shape[2]
    xs, ps, tgt = x[0], p[0, 0], loss_target[0]
    place = jnp.stack([lax.axis_index("x"), lax.axis_index("y"), lax.axis_index("c")]).astype(jnp.int32)
    my_slot = 4 * place[0] + 2 * place[1] + place[2]

    conv_tile = jnp.pad(conv_w[0], ((0, HALO - CONV_K), (0, LANES - cw)))
    big = [w_in[0], w_out[0], w_up[0], w_down[0], w_ple_gate[0], w_ple_proj[0]]
    win_g, conv_g = _all_gather([big[0], conv_tile], [BF16, F32])
    s_out, s_up, s_down, s_gate, s_proj = _cast_shards(big[1:])
    conv_full = jnp.transpose(conv_g[:, :CONV_K, :cw], (1, 0, 2)).reshape(CONV_K, w_conv)
    in_shard, up_shard, proj_shard = big[0].shape[1], big[2].shape[1], big[5].shape[1]

    proj, a, g_out, g_gate, g_proj = _mm_nn("in_proj", xs, win_g, n_shard=in_shard, tn=2 * in_shard, tm=2048,
                                            lhs_norm=g_mix, carry=[("gather_out", [s_out, s_gate, s_proj])])
    cat = _conv_fwd(proj, conv_full, g_conv_out, w_conv, d)
    o, cat, (g_up, g_down, wout_g, wgate_g, wproj_g) = _attn_fwd(
        proj, g_attn_out, cat, w_conv,
        [("gather_out", [s_up, s_down]), ("gather_pass", [g_out, g_gate, g_proj])])
    wout_f = wout_g.reshape(-1, wout_g.shape[-1])
    wgate_f = wgate_g.reshape(-1, wgate_g.shape[-1])
    h1, wup_g = _mm_nn("out_proj", cat, wout_f, epilogue=_ep_residual, extras=(xs,),
                       carry=[("gather_pass", [g_up])])
    act, mn, wdown_g = _mm_nn("mlp_up", h1, wup_g, n_shard=up_shard, epilogue=_ep_up, out_dtypes=(BF16,), tm=2048,
                              lhs_norm=g_mlp, carry=[("gather_pass", [g_down])])
    wdown_f = wdown_g.reshape(-1, wdown_g.shape[-1])
    h2, = _mm_nn("mlp_down", act, wdown_f, epilogue=_ep_residual, extras=(h1,))
    pp = _ple_proj(ps, wproj_g)
    loss_part, dh3, dgl, dpp, dg_final, n3 = _ple_gate_loss(h2, g_ple, wgate_f, pp, tgt, g_final.reshape(1, d))

    def slots(t2d):
        return t2d.reshape(N_DEV, -1, t2d.shape[-1])

    dw_proj = _d_ple_proj(ps, dpp, proj_shard)
    dw_gate = [slots(t) for t in _mm_tn("d_w_ple_gate", n3, dgl)]
    dh2, dh2b, dg_ple = _mm_nt_norm_bwd("d_norm_ple", dgl, wgate_f, h2, g_ple, dh3)
    du, gate_recv, proj_recv = _mm_nt("d_mlp_act", dh2b, wdown_f, epilogue=_ep_dact, out_dtypes=(BF16,),
                                      extras=(act,), tm=2048, carry=[("scatter", [dw_gate[1], dw_proj[1]])])
    dw_down = [slots(t) for t in _mm_tn("d_w_down", act, dh2b)]
    near, far = (1, 2, 3, 4, 5), (6, 7)
    dw_up = _mm_tn("d_w_up", mn, du, n_shard=up_shard)
    dh1, dh1b, dg_mlp, down_part = _mm_nt_norm_bwd(
        "d_norm_mlp", du, wup_g, h1, g_mlp, dh2, k_shard=up_shard, tm=1024,
        carry=[("scatter", [dw_down[1]], near)])
    dcat, = _mm_nt("d_cat", dh1b, wout_f)
    dw_out = [slots(t) for t in _mm_tn("d_w_out", cat, dh1b)]
    dproj, dg_attn, (down_recv, up_recv) = _attn_bwd(
        proj, o, dcat, g_attn_out, w_conv,
        [("scatter_more", [dw_down[1], down_part], far), ("scatter", [dw_up[1]])])
    dproj, dconv, dg_conv = _conv_bwd(proj, dcat, conv_full, g_conv_out, dproj, w_conv)
    *dw_in, out_recv = _mm_tn("d_w_in", a, dproj, n_shard=in_shard, tn=in_shard,
                              carry=[("scatter", [dw_out[1]])])
    grad_x, _, dg_mix, in_recv = _mm_nt_norm_bwd("d_norm_mix", dproj, win_g, xs, g_mix, dh1, k_shard=in_shard,
                                                 tk=2 * in_shard, tm=1024, carry=[("scatter", [dw_in[1]])])

    names = ["w_in", "w_out", "w_up", "w_down", "w_ple_gate", "w_ple_proj"]
    owns = [dw_in[0], dw_out[0], dw_up[0], dw_down[0], dw_gate[0], dw_proj[0]]
    recvs = [in_recv, out_recv, up_recv, down_recv, gate_recv, proj_recv]
    moments = [(m_w_in, v_w_in), (m_w_out, v_w_out), (m_w_up, v_w_up), (m_w_down, v_w_down),
               (m_w_ple_gate, v_w_ple_gate), (m_w_ple_proj, v_w_ple_proj)]
    big_out = {}
    for n, own, rc, wt, (mm, vv) in zip(names, owns, recvs, big, moments):
        big_out[n] = [t[None] for t in _adam_sharded("adam_" + n, own, rc, wt, mm[0], vv[0], place)]

    n_conv_rows = CONV_K * w_conv // LANES
    small_g = jnp.concatenate(
        [_rows(dg_mix[0]), _rows(dg_conv[0]), _rows(dg_attn[0]), _rows(dg_mlp[0]), _rows(dg_ple[0]),
         _rows(dg_final[0]), _rows(dconv.reshape(-1)), loss_part], axis=0)
    n_gain_rows = small_g.shape[0] - n_conv_rows - 1
    pad_rows = (-small_g.shape[0]) % HALO
    small_g = _all_reduce_small(jnp.pad(small_g, ((0, pad_rows), (0, 0))))
    loss = small_g[n_gain_rows + n_conv_rows, 0]
    dconv_full = small_g[n_gain_rows:n_gain_rows + n_conv_rows].reshape(CONV_K, w_conv)
    dconv_mine = lax.dynamic_slice(dconv_full, (0, my_slot * cw), (CONV_K, cw))

    def pack(vecs, conv_part):
        rows = [_rows(t.reshape(-1)) for t in vecs]
        rows.append(jnp.pad(conv_part, ((0, HALO - CONV_K), (0, LANES - cw))))
        return jnp.concatenate(rows, axis=0)

    gains = [g_mix, g_conv_out, g_attn_out, g_mlp, g_ple, g_final]
    gains_m = [m_g_mix, m_g_conv_out, m_g_attn_out, m_g_mlp, m_g_ple, m_g_final]
    gains_v = [v_g_mix, v_g_conv_out, v_g_attn_out, v_g_mlp, v_g_ple, v_g_final]
    gpack = jnp.concatenate([small_g[:n_gain_rows], jnp.pad(dconv_mine, ((0, HALO - CONV_K), (0, LANES - cw)))], axis=0)
    sd, sm, sv = _adam_small(pack(gains, conv_w[0]), gpack, pack(gains_m, m_conv_w[0]), pack(gains_v, v_conv_w[0]))

    def unpack(packed):
        out, r0 = [], 0
        for t in gains:
            nr = t.size // LANES
            out.append(packed[r0:r0 + nr].reshape(t.shape))
            r0 += nr
        out.append(packed[r0:r0 + CONV_K, :cw][None])
        return out

    sg_l, sd_l, sm_l, sv_l = unpack(gpack), unpack(sd), unpack(sm), unpack(sv)
    small_names = ["g_mix", "g_conv_out", "g_attn_out", "g_mlp", "g_ple", "g_final", "conv_w"]
    small_out = {n: [sg_l[i], sd_l[i], sm_l[i], sv_l[i]] for i, n in enumerate(small_names)}

    order = ["g_mix", "w_in", "conv_w", "g_conv_out", "g_attn_out", "w_out", "g_mlp", "w_up", "w_down",
             "g_ple", "w_ple_gate", "w_ple_proj", "g_final"]
    table = {**big_out, **small_out}
    outs = [loss, grad_x[None]]
    for kind in range(4):
        outs.extend(table[n][kind] for n in order)
    return tuple(outs)
```

```python
import jax
import jax.numpy as jnp
from jax import lax
from jax.experimental import pallas as pl
from jax.experimental.pallas import tpu as pltpu

F32 = jnp.float32
BF16 = jnp.bfloat16
EPS = 1e-6
HEAD_DIM = 64
LANES = 128
CONV_K = 3
MXU_WIDTH = 256
ATTN_BLOCK = MXU_WIDTH
HALO = 8
N_DEV = 8
MESH = pl.DeviceIdType.MESH
VMEM_LIMIT = 56 * 1024 * 1024

ADAM_LR = 0.001
ADAM_B1 = 0.9
ADAM_B2 = 0.999
ADAM_EPS = 1e-08
ADAM_WD = 0.01
ADAM_STEP = 10


def _pcall(body, **kw):
    return pl.pallas_call(body, **kw)


def _params(sem=None, **kw):
    return pltpu.CompilerParams(dimension_semantics=sem, vmem_limit_bytes=VMEM_LIMIT, **kw)


def _tile(dim, pref):
    t = min(dim, pref)
    while dim % t:
        t -= LANES
    assert t > 0, (dim, pref)
    return t


_NN = (((1,), (0,)), ((), ()))
_NT = (((1,), (1,)), ((), ()))
_TN = (((0,), (0,)), ((), ()))


def _ep_store(acc, outs):
    outs[0][...] = acc.astype(outs[0].dtype)


def _ep_both(acc, outs):
    outs[0][...] = acc
    outs[1][...] = acc.astype(BF16)


def _ep_residual(acc, res, outs):
    outs[0][...] = acc + res[...]


def _ep_up(acc, outs):
    outs[0][...] = jnp.square(jnp.maximum(acc, 0.0)).astype(BF16)


def _ep_dact(acc, act, outs):
    outs[0][...] = (acc * (2.0 * jnp.sqrt(act[...].astype(F32)))).astype(BF16)


def _row_chunked(epilogue):
    def run(acc, *rest):
        *ex, outs = rest
        n = acc.shape[0]
        for m0 in range(0, n, MXU_WIDTH):
            rows = slice(m0, min(m0 + MXU_WIDTH, n))
            pick = lambda ref: ref.at[rows, :] if ref.shape[0] == n else ref
            epilogue(acc[rows, :], *[pick(e) for e in ex], [pick(o) for o in outs])
    return run


def _ep_norm_bwd(acc, h, g, dres, outs):
    hv = h[...]
    r = lax.rsqrt(jnp.mean(hv * hv, axis=-1, keepdims=True) + EPS)
    hn = hv * r
    outs[2][...] += jnp.sum(acc * hn, axis=0, keepdims=True)
    dhn = acc * g[...]
    dh = dres[...] + r * (dhn - hn * jnp.mean(dhn * hn, axis=-1, keepdims=True))
    outs[0][...] = dh
    outs[1][...] = dh.astype(BF16)


def _matmul(name, a, b, *, dims, grid, a_spec, b_spec, acc_shape, out_shapes, out_specs,
            epilogue=_ep_store, extras=(), extra_specs=(), carry=(), sequential=False, lhs_norm=False):
    nk = grid[2]
    plan = _Carried(carry)
    n_ex, n_out, n_xc, n_xo = len(extras), len(out_shapes), len(plan.inputs), len(plan.out_shapes)
    n_sems = len(plan.sems)
    last = tuple(g - 1 for g in grid)
    assert not lhs_norm or nk == 1

    def product(a_ref, b_ref):
        if len(b_ref.shape) == 2:
            return lax.dot_general(a_ref[...].astype(BF16), b_ref[...].astype(BF16), dims,
                                   preferred_element_type=F32)
        width = b_ref.shape[2]
        return sum(lax.dot_general(a_ref[:, g * width:(g + 1) * width].astype(BF16), b_ref[g].astype(BF16), dims,
                                   preferred_element_type=F32) for g in range(b_ref.shape[0]))

    def body(a_ref, b_ref, *rest):
        ex, rest = rest[:n_ex], rest[n_ex:]
        partials, rest = rest[:n_xc], rest[n_xc:]
        outs, rest = rest[:n_out], rest[n_out:]
        received, rest = rest[:n_xo], rest[n_xo:]
        ids = [pl.program_id(axis) for axis in range(3)]
        if n_xc:
            @pl.when((ids[0] == 0) & (ids[1] == 0) & (ids[2] == 0))
            def _():
                for cp in plan.copies(partials, received, rest[-n_sems:]):
                    cp.start()

        if lhs_norm:
            x_ref, a_ref, gain, ex, outs = a_ref, outs[-1], ex[-1], ex[:-1], outs[:-1]

            @pl.when(ids[1] == 0)
            def _():
                for m0 in range(0, acc_shape[0], MXU_WIDTH):
                    rows = slice(m0, min(m0 + MXU_WIDTH, acc_shape[0]))
                    xv = x_ref[rows, :]
                    r = lax.rsqrt(jnp.mean(xv * xv, axis=-1, keepdims=True) + EPS)
                    a_ref[rows, :] = (xv * r * gain[...]).astype(BF16)

        if nk == 1 and not sequential and dims != _TN:
            if len(b_ref.shape) == 3:
                ns = b_ref.shape[2]
                pieces = [(b_ref.at[g, :, n0:min(n0 + MXU_WIDTH, ns)], slice(g * ns + n0, g * ns + min(n0 + MXU_WIDTH, ns)))
                          for g in range(b_ref.shape[0]) for n0 in range(0, ns, MXU_WIDTH)]
            else:
                spans = [slice(n0, min(n0 + MXU_WIDTH, acc_shape[1])) for n0 in range(0, acc_shape[1], MXU_WIDTH)]
                pieces = [(b_ref.at[cols, :] if dims == _NT else b_ref.at[:, cols], cols) for cols in spans]
            for b_cols, cols in pieces:
                for m0 in range(0, acc_shape[0], MXU_WIDTH):
                    rows = slice(m0, min(m0 + MXU_WIDTH, acc_shape[0]))
                    epilogue(product(a_ref.at[rows, :], b_cols), *[e.at[rows, cols] for e in ex],
                             [o.at[rows, cols] for o in outs])
        else:
            if sequential:
                @pl.when((ids[0] == 0) & (ids[2] == 0))
                def _():
                    for o in outs:
                        if o.shape[0] != acc_shape[0]:
                            o[...] = jnp.zeros_like(o)

            if nk == 1:
                _row_chunked(epilogue)(product(a_ref, b_ref), *ex, outs)
            else:
                acc = rest[0]

                @pl.when(ids[2] == 0)
                def _():
                    acc[...] = product(a_ref, b_ref)

                @pl.when(ids[2] > 0)
                def _():
                    acc[...] += product(a_ref, b_ref)

                @pl.when(ids[2] == nk - 1)
                def _():
                    _row_chunked(epilogue)(acc, *ex, outs)

        if n_xc:
            @pl.when((ids[0] == last[0]) & (ids[1] == last[1]) & (ids[2] == last[2]))
            def _():
                for cp in plan.copies(partials, received, rest[-n_sems:]):
                    cp.wait()

    anywhere = pl.BlockSpec(memory_space=pl.ANY)
    return _pcall(
        body, name=name, grid=grid,
        in_specs=[a_spec, b_spec, *extra_specs, *[anywhere] * n_xc],
        out_specs=[*out_specs, *[anywhere] * n_xo],
        out_shape=[*out_shapes, *plan.out_shapes],
        scratch_shapes=([] if nk == 1 else [pltpu.VMEM(acc_shape, F32)]) + plan.sems,
        input_output_aliases=plan.aliases(2 + n_ex, n_out),
        compiler_params=_params(("arbitrary",) * 3 if n_xc or sequential or lhs_norm
                                else ("parallel", "parallel", "arbitrary")),
    )(a, b, *extras, *plan.inputs)


_NO_CARRY = ()


def _mm_nn(name, a, w, *, n_shard=None, epilogue=_ep_store, out_dtypes=(F32,), extras=(), carry=_NO_CARRY,
           lhs_norm=None, tm=1024, tn=1024, tk=1024):
    m, kd = a.shape
    if lhs_norm is not None:
        tk = kd
    if n_shard is None:
        n = w.shape[1]
        tn = _tile(n, tn)
        tk = _tile(kd, tk)
        b_spec = pl.BlockSpec((tk, tn), lambda i, j, k: (k, j))
    elif tn >= 2 * n_shard and tk >= kd:
        n = N_DEV * n_shard
        group = min(tn // n_shard, N_DEV)
        while N_DEV % group:
            group -= 1
        tn, tk = group * n_shard, kd
        b_spec = pl.BlockSpec((group, tk, n_shard), lambda i, j, k: (j, 0, 0))
    else:
        n = N_DEV * n_shard
        tn = _tile(n_shard, tn)
        tk = _tile(kd, tk)
        per = n_shard // tn
        b_spec = pl.BlockSpec((None, tk, tn), lambda i, j, k: (j // per, k, j % per))
    tm = _tile(m, tm)
    o_spec = pl.BlockSpec((tm, tn), lambda i, j, k: (i, j))
    out_shapes = [jax.ShapeDtypeStruct((m, n), d) for d in out_dtypes]
    out_specs = [o_spec] * len(out_dtypes)
    extra_specs = [o_spec] * len(extras)
    if lhs_norm is not None:
        extras = (*extras, lhs_norm)
        extra_specs.append(pl.BlockSpec((1, kd), lambda i, j, k: (0, 0)))
        out_shapes.append(jax.ShapeDtypeStruct((m, kd), BF16))
        out_specs.append(pl.BlockSpec((tm, kd), lambda i, j, k: (i, 0)))
    return _matmul(
        name, a, w, dims=_NN, grid=(m // tm, n // tn, kd // tk),
        a_spec=pl.BlockSpec((tm, tk), lambda i, j, k: (i, k)), b_spec=b_spec,
        acc_shape=(tm, tn), out_shapes=out_shapes, out_specs=out_specs,
        epilogue=epilogue, extras=extras, extra_specs=extra_specs, carry=carry, lhs_norm=lhs_norm is not None)


def _mm_nt(name, a, w, *, epilogue=_ep_store, out_dtypes=(F32,), extras=(), carry=_NO_CARRY,
           tm=1024, tn=1024, tk=1024):
    m, kd = a.shape
    n = w.shape[0]
    tm, tn, tk = _tile(m, tm), _tile(n, tn), _tile(kd, tk)
    o_spec = pl.BlockSpec((tm, tn), lambda i, j, k: (i, j))
    return _matmul(
        name, a, w, dims=_NT, grid=(m // tm, n // tn, kd // tk),
        a_spec=pl.BlockSpec((tm, tk), lambda i, j, k: (i, k)),
        b_spec=pl.BlockSpec((tn, tk), lambda i, j, k: (j, k)),
        acc_shape=(tm, tn),
        out_shapes=[jax.ShapeDtypeStruct((m, n), d) for d in out_dtypes],
        out_specs=[o_spec] * len(out_dtypes),
        epilogue=epilogue, extras=extras, extra_specs=[o_spec] * len(extras), carry=carry)


def _mm_nt_norm_bwd(name, a, w, h, g, dres, *, k_shard=None, carry=_NO_CARRY, tm=512, tk=1024):
    m, kd = a.shape
    n = h.shape[1]
    if k_shard is None:
        tk = _tile(kd, tk)
        b_spec = pl.BlockSpec((n, tk), lambda i, j, k: (0, k))
    else:
        group = max(1, min(tk // k_shard, N_DEV))
        while N_DEV % group:
            group -= 1
        tk = group * k_shard
        b_spec = pl.BlockSpec((group, n, k_shard), lambda i, j, k: (k, 0, 0))
    tm = _tile(m, tm)
    rows = pl.BlockSpec((tm, n), lambda i, j, k: (i, 0))
    vec = pl.BlockSpec((1, n), lambda i, j, k: (0, 0))
    return _matmul(
        name, a, w, dims=_NT, grid=(m // tm, 1, kd // tk),
        a_spec=pl.BlockSpec((tm, tk), lambda i, j, k: (i, k)), b_spec=b_spec, acc_shape=(tm, n),
        out_shapes=[jax.ShapeDtypeStruct((m, n), F32), jax.ShapeDtypeStruct((m, n), BF16),
                    jax.ShapeDtypeStruct((1, n), F32)],
        out_specs=[rows, rows, vec], epilogue=_ep_norm_bwd,
        extras=(h, g, dres), extra_specs=[rows, vec, rows], carry=carry, sequential=True)


TN_TILE_BYTES = 40 * 1024 * 1024


def _mm_tn(name, a, b, *, n_shard=None, carry=_NO_CARRY, tm=1024, tn=1024):
    t, m = a.shape
    n = b.shape[1]
    tm = _tile(m, tm)
    tn = _tile(n if n_shard is None else n_shard, tn)
    tk = t
    while 2 * 2 * tk * (tm + tn) + 4 * tm * tn * 5 > TN_TILE_BYTES and tk % (2 * LANES) == 0:
        tk //= 2
    if n_shard is None:
        o_spec = pl.BlockSpec((tm, tn), lambda i, j, k: (i, j))
        shape = (m, n)
    else:
        per = n_shard // tn
        o_spec = pl.BlockSpec((None, tm, tn), lambda i, j, k: (j // per, i, j % per))
        shape = (N_DEV, m, n_shard)
    return _matmul(
        name, a, b, dims=_TN, grid=(m // tm, n // tn, t // tk),
        a_spec=pl.BlockSpec((tk, tm), lambda i, j, k: (k, i)),
        b_spec=pl.BlockSpec((tk, tn), lambda i, j, k: (k, j)),
        acc_shape=(tm, tn), epilogue=_ep_both, carry=carry,
        out_shapes=[jax.ShapeDtypeStruct(shape, F32), jax.ShapeDtypeStruct(shape, BF16)],
        out_specs=[o_spec, o_spec])


def _ple_proj(p, w_g, tm=1024):
    s, kd = p.shape
    ns = w_g.shape[2]
    tm = _tile(s, tm)

    def body(p_ref, w_ref, o_ref):
        pv = p_ref[...].astype(BF16)
        for j in range(N_DEV):
            o_ref[:, j * ns:(j + 1) * ns] = jnp.dot(pv, w_ref[j], preferred_element_type=F32)

    return _pcall(body, name="ple_proj", grid=(s // tm,),
                  in_specs=[pl.BlockSpec((tm, kd), lambda i: (i, 0)),
                            pl.BlockSpec((N_DEV, kd, ns), lambda i: (0, 0, 0))],
                  out_specs=pl.BlockSpec((tm, N_DEV * ns), lambda i: (i, 0)),
                  out_shape=jax.ShapeDtypeStruct((s, N_DEV * ns), F32),
                  compiler_params=_params(("parallel",)))(p, w_g)


def _d_ple_proj(p, dpp, ns, tk=1024):
    s, kd = p.shape
    tk = _tile(s, tk)
    nk = s // tk

    def body(p_ref, d_ref, of_ref, ob_ref, acc):
        k = pl.program_id(0)

        @pl.when(k == 0)
        def _():
            acc[...] = jnp.zeros_like(acc)

        pv = p_ref[...].astype(BF16)
        for j in range(N_DEV):
            acc[j] += lax.dot_general(pv, d_ref[:, j * ns:(j + 1) * ns], _TN, preferred_element_type=F32)

        @pl.when(k == nk - 1)
        def _():
            of_ref[...] = acc[...]
            ob_ref[...] = acc[...].astype(BF16)

    whole = pl.BlockSpec((N_DEV, kd, ns), lambda k: (0, 0, 0))
    return _pcall(body, name="d_w_ple_proj", grid=(nk,),
                  in_specs=[pl.BlockSpec((tk, kd), lambda k: (k, 0)),
                            pl.BlockSpec((tk, N_DEV * ns), lambda k: (k, 0))],
                  out_specs=[whole, whole],
                  out_shape=[jax.ShapeDtypeStruct((N_DEV, kd, ns), F32), jax.ShapeDtypeStruct((N_DEV, kd, ns), BF16)],
                  scratch_shapes=[pltpu.VMEM((N_DEV, kd, ns), F32)],
                  compiler_params=_params(("arbitrary",)))(p, dpp)


def _ep_ple_loss(gl, h2, pp, tgt, g_final, outs):
    loss_ref, dh3_ref, dgl_ref, dpp_ref, dg_ref = outs
    gate = jax.nn.sigmoid(gl)
    ppv = pp[...]
    h3 = h2[...] + gate * ppv
    r = lax.rsqrt(jnp.mean(h3 * h3, axis=-1, keepdims=True) + EPS)
    hn = h3 * r
    gv = g_final[...]
    diff = hn * gv - tgt[...]
    row = jnp.mean(diff * diff, axis=-1, keepdims=True)
    loss_ref[...] += 0.5 * jnp.sum(row, axis=0, keepdims=True)
    dy = diff * (1.0 / h3.shape[-1])
    dg_ref[...] += jnp.sum(dy * hn, axis=0, keepdims=True)
    dhn = dy * gv
    dh3 = r * (dhn - hn * jnp.mean(dhn * hn, axis=-1, keepdims=True))
    dh3_ref[...] = dh3
    dgl_ref[...] = (dh3 * ppv * gate * (1.0 - gate)).astype(BF16)
    dpp_ref[...] = (dh3 * gate).astype(BF16)


def _ple_gate_loss(h2, g_ple, w_gate, pp, tgt, g_final, tm=512):
    s, d = h2.shape
    tm = _tile(s, tm)
    rows = pl.BlockSpec((tm, d), lambda i, j, k: (i, 0))
    vec = pl.BlockSpec((1, d), lambda i, j, k: (0, 0))
    return _matmul(
        "ple_gate_loss", h2, w_gate, dims=_NN, grid=(s // tm, 1, 1),
        a_spec=rows, b_spec=pl.BlockSpec((d, d), lambda i, j, k: (0, 0)), acc_shape=(tm, d),
        out_shapes=[jax.ShapeDtypeStruct((1, LANES), F32), jax.ShapeDtypeStruct((s, d), F32),
                    jax.ShapeDtypeStruct((s, d), BF16), jax.ShapeDtypeStruct((s, d), BF16),
                    jax.ShapeDtypeStruct((1, d), F32), jax.ShapeDtypeStruct((s, d), BF16)],
        out_specs=[pl.BlockSpec((1, LANES), lambda i, j, k: (0, 0)), rows, rows, rows, vec, rows],
        epilogue=_ep_ple_loss, extras=(h2, pp, tgt, g_final, g_ple), extra_specs=[rows, rows, rows, vec, vec],
        sequential=True, lhs_norm=True)


def _low_half():
    return lax.broadcasted_iota(jnp.int32, (1, LANES), 1) < HEAD_DIM


def _half_mean(v, low):
    s_lo = jnp.sum(jnp.where(low, v, 0.0), axis=-1, keepdims=True)
    s_hi = jnp.sum(jnp.where(low, 0.0, v), axis=-1, keepdims=True)
    return jnp.where(low, s_lo, s_hi) * (1.0 / HEAD_DIM)


def _head_norm_bwd(val, dout, g, low):
    r = lax.rsqrt(_half_mean(val * val, low) + EPS)
    vn = val * r
    dvn = dout * g
    return r * (dvn - vn * _half_mean(dvn * vn, low)), dout * vn


def _conv_taps(vv_ext, w_ref):
    v0 = vv_ext[HALO:]
    v1 = pltpu.roll(vv_ext, 1, 0)[HALO:]
    v2 = pltpu.roll(vv_ext, 2, 0)[HALO:]
    return w_ref[2:3, :] * v0 + w_ref[1:2, :] * v1 + w_ref[0:1, :] * v2, (v0, v1, v2)


def _conv_fwd(proj, conv_w, g_conv, w_conv, d_model, tr=1024):
    s = proj.shape[0]
    tr = _tile(s, tr)
    hb = tr // HALO

    def main(part):
        return pl.BlockSpec((tr, w_conv), lambda i: (i, part))

    def prev(part):
        return pl.BlockSpec((HALO, w_conv), lambda i: (jnp.maximum(i * hb - 1, 0), part))

    def body(cb_ref, cc_ref, cu_ref, ccp_ref, cup_ref, w_ref, g_ref, o_ref):
        i = pl.program_id(0)
        low = _low_half()
        for j in range(w_conv // LANES):
            cols = slice(j * LANES, (j + 1) * LANES)
            vv_prev = jnp.where(i > 0, ccp_ref[:, cols] * cup_ref[:, cols], 0.0)
            vv_ext = jnp.concatenate([vv_prev, cc_ref[:, cols] * cu_ref[:, cols]], axis=0)
            y, _ = _conv_taps(vv_ext, w_ref.at[:, cols])
            co = cb_ref[:, cols] * y
            r = lax.rsqrt(_half_mean(co * co, low) + EPS)
            o_ref[:, cols] = (co * r * g_ref[:, cols]).astype(BF16)

    return _pcall(
        body, name="conv_fwd", grid=(s // tr,),
        in_specs=[main(0), main(1), main(2), prev(1), prev(2),
                  pl.BlockSpec((CONV_K, w_conv), lambda i: (0, 0)),
                  pl.BlockSpec((1, w_conv), lambda i: (0, 0))],
        out_specs=pl.BlockSpec((tr, w_conv), lambda i: (i, 0)),
        out_shape=jax.ShapeDtypeStruct((s, d_model), BF16),
        compiler_params=_params(("parallel",)),
    )(proj, proj, proj, proj, proj, conv_w, g_conv)


def _conv_bwd(proj, dcat, conv_w, g_conv, dproj, w_conv, tr=1024):
    s = proj.shape[0]
    tr = _tile(s, tr)
    hb = tr // HALO
    last = s // HALO - 1
    nt = s // tr

    def main(part):
        return pl.BlockSpec((tr, w_conv), lambda i: (i, part))

    def prev(part):
        return pl.BlockSpec((HALO, w_conv), lambda i: (jnp.maximum(i * hb - 1, 0), part))

    def nxt(part):
        return pl.BlockSpec((HALO, w_conv), lambda i: (jnp.minimum((i + 1) * hb, last), part))

    def body(cb_ref, cc_ref, cu_ref, dc_ref, ccp_ref, cup_ref, cbn_ref, ccn_ref, cun_ref, dcn_ref,
             w_ref, g_ref, dproj_in, dproj_ref, dw_ref, dg_ref):
        del dproj_in
        i = pl.program_id(0)

        @pl.when(i == 0)
        def _():
            dw_ref[...] = jnp.zeros_like(dw_ref)
            dg_ref[...] = jnp.zeros_like(dg_ref)

        low = _low_half()
        n_ext = tr + HALO
        rowid = lax.broadcasted_iota(jnp.int32, (n_ext, 1), 0)
        for j in range(w_conv // LANES):
            cols = slice(j * LANES, (j + 1) * LANES)
            wj = w_ref.at[:, cols]
            cc, cu = cc_ref[:, cols], cu_ref[:, cols]
            vv_prev = jnp.where(i > 0, ccp_ref[:, cols] * cup_ref[:, cols], 0.0)
            vv_ext = jnp.concatenate([vv_prev, cc * cu, ccn_ref[:, cols] * cun_ref[:, cols]], axis=0)
            y_ext, (v0, v1, v2) = _conv_taps(vv_ext, wj)
            cb_ext = jnp.concatenate([cb_ref[:, cols], cbn_ref[:, cols]], axis=0)
            dc_ext = jnp.concatenate([dc_ref[:, cols], dcn_ref[:, cols]], axis=0)
            dco, dgn = _head_norm_bwd(cb_ext * y_ext, dc_ext, g_ref[:, cols], low)
            dyc = jnp.where((rowid < tr) | (i < nt - 1), dco * cb_ext, 0.0)
            dvv = (wj[2:3, :] * dyc[:tr] + wj[1:2, :] * pltpu.roll(dyc, n_ext - 1, 0)[:tr]
                   + wj[0:1, :] * pltpu.roll(dyc, n_ext - 2, 0)[:tr])
            dproj_ref[:, cols] = (dco[:tr] * y_ext[:tr]).astype(BF16)
            dproj_ref[:, w_conv + j * LANES:w_conv + (j + 1) * LANES] = (dvv * cu).astype(BF16)
            dproj_ref[:, 2 * w_conv + j * LANES:2 * w_conv + (j + 1) * LANES] = (dvv * cc).astype(BF16)
            dyt = dyc[:tr]
            for tap, shifted in enumerate((v2, v1, v0)):
                dw_ref[tap:tap + 1, cols] += jnp.sum(dyt * shifted[:tr], axis=0, keepdims=True)
            dg_ref[:, cols] += jnp.sum(dgn[:tr], axis=0, keepdims=True)

    n_cols = dproj.shape[1]
    return _pcall(
        body, name="conv_bwd", grid=(nt,),
        in_specs=[main(0), main(1), main(2), main(0),
                  prev(1), prev(2), nxt(0), nxt(1), nxt(2), nxt(0),
                  pl.BlockSpec((CONV_K, w_conv), lambda i: (0, 0)),
                  pl.BlockSpec((1, w_conv), lambda i: (0, 0)),
                  pl.BlockSpec(memory_space=pl.ANY)],
        out_specs=[pl.BlockSpec((tr, 3 * w_conv), lambda i: (i, 0)),
                   pl.BlockSpec((CONV_K, w_conv), lambda i: (0, 0)),
                   pl.BlockSpec((1, w_conv), lambda i: (0, 0))],
        out_shape=[jax.ShapeDtypeStruct((s, n_cols), BF16),
                   jax.ShapeDtypeStruct((CONV_K, w_conv), F32),
                   jax.ShapeDtypeStruct((1, w_conv), F32)],
        input_output_aliases={12: 0},
        compiler_params=_params(("arbitrary",)),
    )(proj, proj, proj, dcat, proj, proj, proj, proj, proj, dcat, conv_w, g_conv, dproj)


STRIP = 16

ALL_CHAINS = (0, 1, 2, 3)
UPPER_CHAINS = (2, 3)


RUN_FLOOR = -104.0


def _any_weight_left(run_s):
    return (jnp.max(run_s[...]) > RUN_FLOOR).astype(jnp.int32)


def _chains(low):
    return [(2 * half + h, half, msk) for half in range(2)
            for h, msk in enumerate((low, jnp.logical_not(low)))]


def _suffix_operator(t):
    r = lax.broadcasted_iota(jnp.int32, (2 * t, t), 0)
    c = lax.broadcasted_iota(jnp.int32, (2 * t, t), 1)
    return jnp.where((r > c) & ((r < t) | (r - t > c)), 1.0, 0.0).astype(BF16)


def _strips(t, diag):
    return [(i, slice(i * STRIP, (i + 1) * STRIP), t // 2 if diag and (i + 1) * STRIP <= t // 2 else t)
            for i in range(t // STRIP)]


def _strip_mask(i, w):
    r = lax.broadcasted_iota(jnp.int32, (STRIP, w), 0) + i * STRIP
    c = lax.broadcasted_iota(jnp.int32, (STRIP, w), 1)
    return r > c


def _store_trimmed(ref, rows, val, w, t, at=0):
    ref[rows, at:at + w] = val
    if w < t:
        ref[rows, at + w:at + t] = jnp.zeros((STRIP, t - w), val.dtype)


def _store_split(ref, rows, val, w, t):
    hi = val.astype(BF16)
    _store_trimmed(ref, rows, hi, w, t)
    _store_trimmed(ref, rows, (val - hi.astype(F32)).astype(BF16), w, t, at=t)


def _sb_scores(z_s, split_s, zl_s, tot_s, keep_s, t, diag):
    for i, rows, w in _strips(t, diag):
        z = z_s[rows, :w]
        log_beta = jnp.minimum(z, 0.0) - jnp.log(1.0 + jnp.exp(-jnp.abs(z)))
        log_keep = log_beta - z
        if diag:
            log_keep = jnp.where(_strip_mask(i, w), log_keep, 0.0)
        _store_trimmed(split_s, rows, log_keep.astype(BF16), w, t)
        zl_s[rows, :w] = log_beta
        tot_s[rows, :] = _row_sum(log_keep)
        if keep_s is not None:
            keep_s[rows, :w] = jnp.exp(log_keep)


def _row_sum(v):
    return jnp.broadcast_to(jnp.sum(v, axis=-1, keepdims=True), (v.shape[0], LANES))


def _wide(r, t):
    return jnp.concatenate([r] * (t // LANES), axis=1)


def _sb_weights(zl_s, suf_s, run_s, tot_s, a_s, t, diag, da_s=None, glog_s=None, gsplit_s=None, gtot_s=None):
    for i, rows, w in _strips(t, diag):
        run = run_s[rows, :]
        a = jnp.exp(zl_s[rows, :w] + suf_s[rows, :w] + _wide(run, w))
        if diag:
            a = jnp.where(_strip_mask(i, w), a, 0.0)
        ab = a.astype(BF16)
        _store_trimmed(a_s, rows, ab, w, t)
        run_s[rows, :] = run + tot_s[rows, :]
        if da_s is not None:
            glog = ab.astype(F32) * da_s[rows, :w]
            glog_s[rows, :w] = glog
            _store_split(gsplit_s, rows, glog, w, t)
            gtot_s[rows, :] = _row_sum(glog)


def _sb_dscores(glog_s, cum_s, rest_s, gtot_s, keep_s, dz_s, t, diag):
    for i, rows, w in _strips(t, diag):
        glog = glog_s[rows, :w]
        rest = rest_s[rows, :]
        from_here = _wide(rest, w) - cum_s[rows, :w]
        before = from_here - glog
        dz = from_here * keep_s[rows, :w] - before
        if diag:
            dz = jnp.where(_strip_mask(i, w), dz, 0.0)
        _store_trimmed(dz_s, rows, dz.astype(BF16), w, t)
        rest_s[rows, :] = rest - gtot_s[rows, :]


def _attn_fwd(proj, g_attn, cat, w_conv, carry, t=ATTN_BLOCK):
    s = proj.shape[0]
    w_attn = g_attn.shape[1]
    nh = w_attn // LANES
    t = _tile(s, t)
    tq = 2 * t
    nq = s // tq
    q0 = 3 * w_conv // LANES
    scale = HEAD_DIM ** -0.5
    plan = _Carried(carry)
    nw, n_res = len(plan.inputs), len(plan.out_shapes)

    def body(q_ref, k_ref, v_ref, g_ref, cat_in, *rest):
        staged_refs, rest = rest[:nw], rest[nw:]
        o_ref, cat_ref = rest[:2]
        gathered_refs, rest = rest[2:2 + n_res], rest[2 + n_res:]
        kb, vb, tri_s, qm_s, z_s, split_s, zl_s, suf_s, a_s, run_s, tot_s, acc_s = rest[:12]
        gather_sems = rest[12:]
        del cat_in
        qi = pl.program_id(1)

        @pl.when((pl.program_id(0) == 0) & (qi == 0))
        def _():
            for cp in plan.copies(staged_refs, gathered_refs, gather_sems):
                cp.start()

        @pl.when(qi == 0)
        def _():
            kb[...] = k_ref[...].astype(BF16)
            vb[...] = v_ref[...].astype(BF16)
            tri_s[...] = _suffix_operator(t)

        low = _low_half()
        for c, half, msk in _chains(low):
            qm_s[c] = jnp.where(msk, q_ref[half * t:(half + 1) * t, :] * scale, 0.0).astype(BF16)
            run_s[c] = jnp.zeros((t, LANES), F32)
            acc_s[c] = jnp.zeros((t, LANES), F32)

        def key_rows(kblk):
            return pl.ds(pl.multiple_of(kblk * t, t), t)

        def key_block(base, c):
            return key_rows(jnp.maximum(base + c // 2, 0))

        def scores_matmul(base, chains):
            for c in chains:
                z_s[c] = lax.dot_general(qm_s[c], kb[key_block(base, c), :], _NT, preferred_element_type=F32)

        def front(modes, base, prev=None):
            for c, diag in modes:
                _sb_scores(z_s.at[c], split_s.at[c], zl_s.at[c], tot_s.at[c], None, t, diag)
                suf_s[c] = jnp.dot(split_s[c, :, 0:t], tri_s[0:t, :], preferred_element_type=F32)
            if prev is not None:
                tail(*prev)
            scores_matmul(base - 1, ALL_CHAINS)
            for c, diag in modes:
                _sb_weights(zl_s.at[c], suf_s.at[c], run_s.at[c], tot_s.at[c], a_s.at[c], t, diag)

        def tail(base, chains):
            for c in chains:
                acc_s[c] += jnp.dot(a_s[c], vb[key_block(base, c), :], preferred_element_type=F32)

        first = 2 * qi
        scores_matmul(first, ALL_CHAINS)
        front([(c, True) for c in ALL_CHAINS], first)

        def loop(state):
            it = state[0]
            base = first - 1 - it
            front([(c, False) for c in ALL_CHAINS], base, prev=(base + 1, ALL_CHAINS))
            return it + 1, _any_weight_left(run_s)

        done, live = lax.while_loop(lambda state: (state[0] < first) & (state[1] > 0), loop,
                                    (jnp.int32(0), jnp.int32(1)))
        one_more = (done == first) & (live > 0)

        @pl.when(one_more)
        def _():
            front([(c, False) for c in UPPER_CHAINS], -1, prev=(0, ALL_CHAINS))
            tail(-1, UPPER_CHAINS)

        @pl.when(jnp.logical_not(one_more))
        def _():
            tail(first - done, ALL_CHAINS)

        for half in range(2):
            rows = slice(half * t, (half + 1) * t)
            o = jnp.where(low, acc_s[2 * half], acc_s[2 * half + 1])
            o_ref[rows, :] = o
            r = lax.rsqrt(_half_mean(o * o, low) + EPS)
            cat_ref[rows, :] = (o * r * g_ref[...]).astype(BF16)

        @pl.when((pl.program_id(0) == nh - 1) & (qi == nq - 1))
        def _():
            for cp in plan.copies(staged_refs, gathered_refs, gather_sems):
                cp.wait()

    whole = lambda col0: pl.BlockSpec((s, LANES), lambda h, i: (0, col0 + h))
    n_ch = len(ALL_CHAINS)
    res = _pcall(
        body, name="attn_fwd", grid=(nh, nq),
        in_specs=[pl.BlockSpec((tq, LANES), lambda h, i: (i, q0 + h)),
                  whole(q0 + nh), whole(q0 + 2 * nh),
                  pl.BlockSpec((1, LANES), lambda h, i: (0, h)),
                  pl.BlockSpec(memory_space=pl.ANY)] + [pl.BlockSpec(memory_space=pl.ANY)] * nw,
        out_specs=[pl.BlockSpec((tq, LANES), lambda h, i: (i, h)),
                   pl.BlockSpec((tq, LANES), lambda h, i: (i, w_conv // LANES + h))]
        + [pl.BlockSpec(memory_space=pl.ANY)] * n_res,
        out_shape=[jax.ShapeDtypeStruct((s, w_attn), F32),
                   jax.ShapeDtypeStruct(cat.shape, BF16)] + plan.out_shapes,
        scratch_shapes=[pltpu.VMEM((s, LANES), BF16), pltpu.VMEM((s, LANES), BF16),
                        pltpu.VMEM((2 * t, t), BF16),
                        pltpu.VMEM((n_ch, t, LANES), BF16),
                        pltpu.VMEM((n_ch, t, t), F32),
                        pltpu.VMEM((n_ch, t, 2 * t), BF16),
                        pltpu.VMEM((n_ch, t, t), F32),
                        pltpu.VMEM((n_ch, t, t), F32),
                        pltpu.VMEM((n_ch, t, t), BF16),
                        pltpu.VMEM((n_ch, t, LANES), F32),
                        pltpu.VMEM((n_ch, t, LANES), F32),
                        pltpu.VMEM((n_ch, t, LANES), F32)]
        + plan.sems,
        input_output_aliases={4: 1, **plan.aliases(5, 2)},
        compiler_params=_params(("arbitrary", "arbitrary")),
    )(proj, proj, proj, g_attn, cat, *plan.inputs)
    return res[0], res[1], res[2:]


def _attn_bwd(proj, o, dcat, g_attn, w_conv, carry, t=ATTN_BLOCK):
    s, n_cols = proj.shape
    w_attn = g_attn.shape[1]
    nh = w_attn // LANES
    t = _tile(s, t)
    tq = 2 * t
    nq = s // tq
    q0 = 3 * w_conv // LANES
    scale = HEAD_DIM ** -0.5
    plan = _Carried(carry)
    nw, n_res = len(plan.inputs), len(plan.out_shapes)

    def body(q_ref, k_ref, v_ref, o_ref, do_ref, g_ref, *rest):
        partial_refs, rest = rest[:nw], rest[nw:]
        dproj_ref, dg_ref = rest[:2]
        received_refs, rest = rest[2:2 + n_res], rest[2 + n_res:]
        (kb, vb, dkt_acc, dvt_acc, stash, tri_s, qm_s, dom_s, qt_s, dot_s, z_s, da_s, split_s, zl_s,
         keep_s, suf_s, a_s, glog_s, gsplit_s, cum_s, dz_s, run_s, tot_s, rest_s, gtot_s, dq_s) = rest[:26]
        out_sems, scatter_sems = rest[26], rest[27:]
        step_i = pl.program_id(1)
        qi = nq - 1 - step_i
        head_pair = pl.program_id(0)
        first_step = (head_pair == 0) & (step_i == 0)
        last_step = (head_pair == nh - 1) & (step_i == nq - 1)

        @pl.when(first_step)
        def _():
            for cp in plan.copies(partial_refs, received_refs, scatter_sems):
                cp.start()

        def out_copies():
            rows = pl.ds(pl.multiple_of(qi * tq, tq), tq)
            return [pltpu.make_async_copy(
                stash.at[w], dproj_ref.at[rows, pl.ds(pl.multiple_of((q0 + w * nh + head_pair) * LANES, LANES), LANES)],
                out_sems.at[w]) for w in range(3)]

        def walk():
            @pl.when(step_i == 0)
            def _():
                kb[...] = k_ref[...].astype(BF16)
                vb[...] = v_ref[...].astype(BF16)
                tri_s[...] = _suffix_operator(t)
                dkt_acc[...] = jnp.zeros_like(dkt_acc)
                dvt_acc[...] = jnp.zeros_like(dvt_acc)
                dg_ref[...] = jnp.zeros_like(dg_ref)

            low = _low_half()
            gv = g_ref[...]
            for half in range(2):
                rows = slice(half * t, (half + 1) * t)
                q = q_ref[rows, :] * scale
                ov = o_ref[rows, :]
                d_o, dgn = _head_norm_bwd(ov, do_ref[rows, :], gv, low)
                dg_ref[...] += jnp.sum(dgn, axis=0, keepdims=True)
                for h, msk in enumerate((low, jnp.logical_not(low))):
                    c = 2 * half + h
                    qh = jnp.where(msk, q, 0.0)
                    doh = jnp.where(msk, d_o, 0.0)
                    dom = doh.astype(BF16)
                    qm_s[c] = qh.astype(BF16)
                    dom_s[c] = dom
                    qt_s[c] = qh.T.astype(BF16)
                    dot_s[c] = doh.T.astype(BF16)
                    rest_s[c] = _row_sum(dom.astype(F32) * ov)
                    run_s[c] = jnp.zeros((t, LANES), F32)
                    dq_s[c] = jnp.zeros((t, LANES), F32)

            def key_rows(kblk):
                return pl.ds(pl.multiple_of(kblk * t, t), t)

            def block_of(base, half):
                return jnp.maximum(base + half, 0)

            def scores_matmul(base, chains):
                for c in chains:
                    ks = kb[key_rows(block_of(base, c // 2)), :]
                    z_s[c] = lax.dot_general(qm_s[c], ks, _NT, preferred_element_type=F32)

            def da_matmul(base, chains):
                for c in chains:
                    vs = vb[key_rows(block_of(base, c // 2)), :]
                    da_s[c] = lax.dot_general(dom_s[c], vs, _NT, preferred_element_type=F32)

            def front(modes, base, prev=None):
                if prev is not None:
                    tail(*prev)
                for c, diag in modes:
                    _sb_scores(z_s.at[c], split_s.at[c], zl_s.at[c], tot_s.at[c], keep_s.at[c], t, diag)
                    suf_s[c] = jnp.dot(split_s[c, :, 0:t], tri_s[0:t, :], preferred_element_type=F32)
                scores_matmul(base - 1, ALL_CHAINS)
                for c, diag in modes:
                    _sb_weights(zl_s.at[c], suf_s.at[c], run_s.at[c], tot_s.at[c], a_s.at[c], t, diag,
                                da_s.at[c], glog_s.at[c], gsplit_s.at[c], gtot_s.at[c])
                    cum_s[c] = jnp.dot(gsplit_s[c], tri_s[...], preferred_element_type=F32)
                da_matmul(base - 1, ALL_CHAINS)
                for c, diag in modes:
                    _sb_dscores(glog_s.at[c], cum_s.at[c], rest_s.at[c], gtot_s.at[c], keep_s.at[c],
                                dz_s.at[c], t, diag)

            def tail(base, chains):
                for half in range(2):
                    mine = [c for c in chains if c // 2 == half]
                    if not mine:
                        continue
                    kblk = block_of(base, half)
                    ks = kb[key_rows(kblk), :]
                    dkt = dkt_acc[kblk]
                    dvt = dvt_acc[kblk]
                    for c in mine:
                        dq_s[c] += jnp.dot(dz_s[c], ks, preferred_element_type=F32)
                        dkt = dkt + jnp.dot(qt_s[c], dz_s[c], preferred_element_type=F32)
                        dvt = dvt + jnp.dot(dot_s[c], a_s[c], preferred_element_type=F32)
                    dkt_acc[kblk] = dkt
                    dvt_acc[kblk] = dvt

            first = 2 * qi
            scores_matmul(first, ALL_CHAINS)
            da_matmul(first, ALL_CHAINS)
            front([(c, True) for c in ALL_CHAINS], first)

            def loop(state):
                it = state[0]
                base = first - 1 - it
                front([(c, False) for c in ALL_CHAINS], base, prev=(base + 1, ALL_CHAINS))
                return it + 1, _any_weight_left(run_s)

            done, live = lax.while_loop(lambda state: (state[0] < first) & (state[1] > 0), loop,
                                        (jnp.int32(0), jnp.int32(1)))
            one_more = (done == first) & (live > 0)

            @pl.when(one_more)
            def _():
                front([(c, False) for c in UPPER_CHAINS], -1, prev=(0, ALL_CHAINS))
                tail(-1, UPPER_CHAINS)

            @pl.when(jnp.logical_not(one_more))
            def _():
                tail(first - done, ALL_CHAINS)

            @pl.when(jnp.logical_not(first_step))
            def _():
                for cp in out_copies():
                    cp.wait()

            for half in range(2):
                rows = slice(half * t, (half + 1) * t)
                stash[0, rows, :] = (jnp.where(low, dq_s[2 * half], dq_s[2 * half + 1]) * scale).astype(BF16)
                stash[1, rows, :] = dkt_acc[2 * qi + half].T.astype(BF16)
                stash[2, rows, :] = dvt_acc[2 * qi + half].T.astype(BF16)
            for cp in out_copies():
                cp.start()

        walk()

        @pl.when(last_step)
        def _():
            for cp in out_copies():
                cp.wait()
            for cp in plan.copies(partial_refs, received_refs, scatter_sems):
                cp.wait()

    whole = lambda col0: pl.BlockSpec((s, LANES), lambda h, i: (0, col0 + h))
    blk = lambda col0: pl.BlockSpec((tq, LANES), lambda h, i: (nq - 1 - i, col0 + h))
    n_ch = len(ALL_CHAINS)
    res = _pcall(
        body, name="attn_bwd", grid=(nh, nq),
        in_specs=[blk(q0), whole(q0 + nh), whole(q0 + 2 * nh), blk(0), blk(w_conv // LANES),
                  pl.BlockSpec((1, LANES), lambda h, i: (0, h))] + [pl.BlockSpec(memory_space=pl.ANY)] * nw,
        out_specs=[pl.BlockSpec(memory_space=pl.ANY),
                   pl.BlockSpec((1, LANES), lambda h, i: (0, h))] + [pl.BlockSpec(memory_space=pl.ANY)] * n_res,
        out_shape=[jax.ShapeDtypeStruct((s, n_cols), BF16), jax.ShapeDtypeStruct((1, w_attn), F32)]
        + plan.out_shapes,
        scratch_shapes=[pltpu.VMEM((s, LANES), BF16), pltpu.VMEM((s, LANES), BF16),
                        pltpu.VMEM((s // t, LANES, t), F32),
                        pltpu.VMEM((s // t, LANES, t), F32),
                        pltpu.VMEM((3, tq, LANES), BF16),
                        pltpu.VMEM((2 * t, t), BF16),
                        pltpu.VMEM((n_ch, t, LANES), BF16),
                        pltpu.VMEM((n_ch, t, LANES), BF16),
                        pltpu.VMEM((n_ch, LANES, t), BF16),
                        pltpu.VMEM((n_ch, LANES, t), BF16),
                        pltpu.VMEM((n_ch, t, t), F32),
                        pltpu.VMEM((n_ch, t, t), F32),
                        pltpu.VMEM((n_ch, t, 2 * t), BF16),
                        pltpu.VMEM((n_ch, t, t), F32),
                        pltpu.VMEM((n_ch, t, t), F32),
                        pltpu.VMEM((n_ch, t, t), F32),
                        pltpu.VMEM((n_ch, t, t), BF16),
                        pltpu.VMEM((n_ch, t, t), F32),
                        pltpu.VMEM((n_ch, t, 2 * t), BF16),
                        pltpu.VMEM((n_ch, t, t), F32),
                        pltpu.VMEM((n_ch, t, t), BF16),
                        pltpu.VMEM((n_ch, t, LANES), F32),
                        pltpu.VMEM((n_ch, t, LANES), F32),
                        pltpu.VMEM((n_ch, t, LANES), F32),
                        pltpu.VMEM((n_ch, t, LANES), F32),
                        pltpu.VMEM((n_ch, t, LANES), F32),
                        pltpu.SemaphoreType.DMA((3,))]
        + plan.sems,
        input_output_aliases=plan.aliases(6, 2),
        compiler_params=_params(("arbitrary", "arbitrary")),
    )(proj, proj, proj, o, dcat, g_attn, *plan.inputs)
    return res[0], res[1], res[2:]


def _place():
    return lax.axis_index("x"), lax.axis_index("y"), lax.axis_index("c")


def _other_chips(x, y):
    return [(1 - x, y), (x, 1 - y), (1 - x, 1 - y)]


def _slot(px, py, pc):
    return 4 * px + 2 * py + pc


def _all_gather(shards, out_dtypes):
    nw = len(shards)

    def body(*refs):
        ins, outs, stage = refs[:nw], refs[nw:2 * nw], refs[2 * nw:3 * nw]
        send_sems, recv_sems, local_sems = refs[3 * nw:]
        x, y, c = _place()
        me, sibling = (x, y, c), (x, y, 1 - c)
        chips = _other_chips(x, y)

        def copy(w, k, block, to, src=None):
            dst = outs[w].at[_slot(*block)]
            return pltpu.make_async_remote_copy(
                src_ref=dst if src is None else src, dst_ref=dst,
                send_sem=send_sems.at[w * 7 + k], recv_sem=recv_sems.at[w * 7 + k],
                device_id=to, device_id_type=MESH)

        started = []
        local = []
        for w in range(nw):
            stage[w][...] = ins[w][...].astype(stage[w].dtype)
            cp = pltpu.make_async_copy(stage[w], outs[w].at[_slot(*me)], local_sems.at[w])
            cp.start()
            local.append(cp)
            started.append(copy(w, 0, me, sibling, src=stage[w]))
            started[-1].start()
            for j, chip in enumerate(chips):
                started.append(copy(w, 1 + j, me, (*chip, c), src=stage[w]))
                started[-1].start()
        for j, chip in enumerate(chips):
            for w in range(nw):
                copy(w, 1 + j, (*chip, c), me).wait_recv()
                started.append(copy(w, 4 + j, (*chip, c), sibling))
                started[-1].start()
        for w in range(nw):
            copy(w, 0, sibling, me).wait_recv()
            for j, chip in enumerate(chips):
                copy(w, 4 + j, (*chip, 1 - c), me).wait_recv()
        for cp in started:
            cp.wait_send()
        for cp in local:
            cp.wait()

    return _pcall(
        body, name="all_gather_weights",
        in_specs=[pl.BlockSpec(memory_space=pltpu.VMEM)] * nw,
        out_specs=[pl.BlockSpec(memory_space=pl.ANY)] * nw,
        out_shape=[jax.ShapeDtypeStruct((N_DEV, *a.shape), d) for a, d in zip(shards, out_dtypes)],
        scratch_shapes=[pltpu.VMEM(a.shape, d) for a, d in zip(shards, out_dtypes)]
        + [pltpu.SemaphoreType.DMA((7 * nw,)), pltpu.SemaphoreType.DMA((7 * nw,)),
           pltpu.SemaphoreType.DMA((nw,))],
        compiler_params=_params(),
    )(*shards)


N_PEERS = N_DEV - 1


def _peer(k):
    x, y, c = _place()
    return (x ^ (k >> 2), y ^ ((k >> 1) & 1), c ^ (k & 1))


def _remote(src, dst, sems, index, to):
    return pltpu.make_async_remote_copy(src_ref=src, dst_ref=dst, send_sem=sems[0].at[index],
                                        recv_sem=sems[1].at[index], device_id=to, device_id_type=MESH)


def _gather_out_copies(staged, gathered, sems):
    x, y, c = _place()
    me = _slot(x, y, c)
    targets = [(x, y, 1 - c)] + [(*chip, c) for chip in _other_chips(x, y)]
    copies = []
    for w, (src, dst) in enumerate(zip(staged, gathered)):
        copies.append(pltpu.make_async_copy(src, dst.at[me], sems[2].at[w]))
        copies += [_remote(src, dst.at[me], sems, w * len(targets) + k, to) for k, to in enumerate(targets)]
    return copies


def _gather_pass_copies(arrived, gathered, sems):
    x, y, c = _place()
    chips = _other_chips(x, y)
    return [_remote(src.at[_slot(*chip, c)], dst.at[_slot(*chip, c)], sems, w * len(chips) + j, (x, y, 1 - c))
            for w, (src, dst) in enumerate(zip(arrived, gathered)) for j, chip in enumerate(chips)]


ALL_PEERS = tuple(range(1, N_DEV))


def _scatter_copies(partials, received, sems, peers=ALL_PEERS):
    me = _slot(*_place())
    return [_remote(src.at[me ^ k], dst.at[k - 1], sems, w * N_PEERS + k - 1, _peer(k))
            for w, (src, dst) in enumerate(zip(partials, received)) for k in peers]


class _Carried:
    def __init__(self, jobs):
        self.jobs = [(job[0], list(job[1]), job[2] if len(job) > 2 else ALL_PEERS) for job in jobs if len(job[1])]
        self.inputs, self.out_shapes, self.sems, self.counts = [], [], [], []
        for kind, arrays, _ in self.jobs:
            n_out = len(arrays) // 2 if kind == "scatter_more" else len(arrays)
            fan = {"gather_out": 4, "gather_pass": 3}.get(kind, N_PEERS)
            for a in arrays[len(arrays) - n_out:]:
                shape = {"gather_out": (N_DEV, *a.shape), "scatter": (N_PEERS, *a.shape[1:])}.get(kind, a.shape)
                self.out_shapes.append(jax.ShapeDtypeStruct(shape, BF16))
            job_sems = [pltpu.SemaphoreType.DMA((fan * n_out,))] * 2
            job_sems += [pltpu.SemaphoreType.DMA((n_out,))] if kind == "gather_out" else []
            self.inputs += arrays
            self.sems += job_sems
            self.counts.append((len(arrays), n_out, len(job_sems)))

    def aliases(self, first_input, first_output):
        pairs, at_in, at_out = {}, first_input, first_output
        for (kind, _, _), (n_in, n_out, _) in zip(self.jobs, self.counts):
            if kind in ("gather_pass", "scatter_more"):
                pairs.update({at_in + n_in - n_out + i: at_out + i for i in range(n_out)})
            at_in, at_out = at_in + n_in, at_out + n_out
        return pairs

    def copies(self, in_refs, out_refs, sem_refs):
        out, at_in, at_out, at_sem = [], 0, 0, 0
        for (kind, _, peers), (n_in, n_out, n_sems) in zip(self.jobs, self.counts):
            srcs, dsts = in_refs[at_in:at_in + n_out], out_refs[at_out:at_out + n_out]
            sems = sem_refs[at_sem:at_sem + n_sems]
            if kind == "gather_out":
                out += _gather_out_copies(srcs, dsts, sems)
            elif kind == "gather_pass":
                out += _gather_pass_copies(srcs, dsts, sems)
            else:
                out += _scatter_copies(srcs, dsts, sems, peers)
            at_in, at_out, at_sem = at_in + n_in, at_out + n_out, at_sem + n_sems
        return out


def _cast_shards(shards):
    def body(*refs):
        for src, dst in zip(refs[:len(shards)], refs[len(shards):]):
            dst[...] = src[...].astype(BF16)

    return _pcall(
        body, name="cast_shards",
        in_specs=[pl.BlockSpec(memory_space=pltpu.VMEM)] * len(shards),
        out_specs=[pl.BlockSpec(memory_space=pltpu.VMEM)] * len(shards),
        out_shape=[jax.ShapeDtypeStruct(a.shape, BF16) for a in shards],
        compiler_params=_params(),
    )(*shards)


def _all_reduce_small(packed):
    r = packed.shape[0]

    def body(x_ref, o_ref, gathered, send_sems, recv_sems):
        x, y, c = _place()
        me = _slot(x, y, c)
        gathered[me] = x_ref[...]
        copies = [_remote(x_ref, gathered.at[me], (send_sems, recv_sems), k - 1, _peer(k)) for k in ALL_PEERS]
        for cp in copies:
            cp.start()
        for cp in copies:
            cp.wait()
        total = gathered[0]
        for k in range(1, N_DEV):
            total = total + gathered[k]
        o_ref[...] = total

    return _pcall(
        body, name="all_reduce_small",
        in_specs=[pl.BlockSpec(memory_space=pltpu.VMEM)],
        out_specs=pl.BlockSpec(memory_space=pltpu.VMEM),
        out_shape=jax.ShapeDtypeStruct(packed.shape, F32),
        scratch_shapes=[pltpu.VMEM((N_DEV, r, LANES), F32),
                        pltpu.SemaphoreType.DMA((N_DEV - 1,)), pltpu.SemaphoreType.DMA((N_DEV - 1,))],
        compiler_params=_params(),
    )(packed)


def _adam_math(w, g, m, v):
    m = ADAM_B1 * m + (1.0 - ADAM_B1) * g
    v = ADAM_B2 * v + (1.0 - ADAM_B2) * jnp.square(g)
    m_hat = m / (1.0 - ADAM_B1 ** ADAM_STEP)
    v_hat = v / (1.0 - ADAM_B2 ** ADAM_STEP)
    delta = -ADAM_LR * (m_hat / (jnp.sqrt(v_hat) + ADAM_EPS) + ADAM_WD * w)
    return delta, m, v


ADAM_TILE_BYTES = 24 * 1024 * 1024


def _adam_sharded(name, own, received, w, m, v, place):
    r, cdim = w.shape
    row_bytes = 2 * cdim * (4 + 2 * N_PEERS + 3 * 4 + 4 * 4)
    tr = _tile(r, max(LANES, ADAM_TILE_BYTES // row_bytes // LANES * LANES)) if r % LANES == 0 else r

    def body(place_ref, own_ref, rec_ref, w_ref, m_ref, v_ref, g_ref, d_ref, nm_ref, nv_ref):
        del place_ref
        g = own_ref[...]
        for j in range(N_PEERS):
            g = g + rec_ref[j].astype(F32)
        delta, nm, nv = _adam_math(w_ref[...], g, m_ref[...], v_ref[...])
        g_ref[...] = g
        d_ref[...] = delta
        nm_ref[...] = nm
        nv_ref[...] = nv

    blk = pl.BlockSpec((tr, cdim), lambda i, pr: (i, 0))
    grid_spec = pltpu.PrefetchScalarGridSpec(
        num_scalar_prefetch=1, grid=(r // tr,),
        in_specs=[pl.BlockSpec((None, tr, cdim), lambda i, pr: (4 * pr[0] + 2 * pr[1] + pr[2], i, 0)),
                  pl.BlockSpec((N_PEERS, tr, cdim), lambda i, pr: (0, i, 0)), blk, blk, blk],
        out_specs=[blk] * 4)
    return _pcall(body, name=name, grid_spec=grid_spec,
                  out_shape=[jax.ShapeDtypeStruct((r, cdim), F32)] * 4,
                  compiler_params=_params(("parallel",)))(place, own, received, w, m, v)


def _adam_small(w, g, m, v):
    def body(w_ref, g_ref, m_ref, v_ref, d_ref, nm_ref, nv_ref):
        delta, nm, nv = _adam_math(w_ref[...], g_ref[...], m_ref[...], v_ref[...])
        d_ref[...] = delta
        nm_ref[...] = nm
        nv_ref[...] = nv

    return _pcall(body, name="adam_small",
                  in_specs=[pl.BlockSpec(memory_space=pltpu.VMEM)] * 4,
                  out_specs=[pl.BlockSpec(memory_space=pltpu.VMEM)] * 3,
                  out_shape=[jax.ShapeDtypeStruct(w.shape, F32)] * 3,
                  compiler_params=_params())(w, g, m, v)


def _reduce_adam_small(packed, w, m, v, n_gain_rows, n_conv_rows):
    r = packed.shape[0]
    per_tap = n_conv_rows // CONV_K

    def body(x_ref, w_ref, m_ref, v_ref, loss_ref, g_ref, d_ref, nm_ref, nv_ref, gathered, total_s,
             send_sems, recv_sems):
        me = _slot(*_place())
        gathered[me] = x_ref[...]
        copies = [_remote(x_ref, gathered.at[me], (send_sems, recv_sems), k - 1, _peer(k)) for k in ALL_PEERS]
        for cp in copies:
            cp.start()
        for cp in copies:
            cp.wait()
        total = gathered[0]
        for k in range(1, N_DEV):
            total = total + gathered[k]
        total_s[...] = total
        loss_ref[...] = total_s[n_gain_rows + n_conv_rows:n_gain_rows + n_conv_rows + 1, :]
        g_ref[0:n_gain_rows, :] = total_s[0:n_gain_rows, :]
        tap_of_row = lax.broadcasted_iota(jnp.int32, (HALO, LANES), 0)
        tile = jnp.zeros((HALO, LANES), F32)
        for tap in range(CONV_K):
            base = n_gain_rows + tap * per_tap
            row = total_s[base:base + 1, :]
            for q in range(1, per_tap):
                row = jnp.where(me // 2 == q, total_s[base + q:base + q + 1, :], row)
            tile = jnp.where(tap_of_row == tap, jnp.broadcast_to(row, (HALO, LANES)), tile)
        tile = jnp.where(me % 2 == 0, tile, pltpu.roll(tile, HEAD_DIM, 1))
        g_ref[n_gain_rows:n_gain_rows + HALO, :] = jnp.where(_low_half(), tile, 0.0)
        delta, nm, nv = _adam_math(w_ref[...], g_ref[...], m_ref[...], v_ref[...])
        d_ref[...] = delta
        nm_ref[...] = nm
        nv_ref[...] = nv

    vm = pl.BlockSpec(memory_space=pltpu.VMEM)
    return _pcall(body, name="reduce_adam_small", in_specs=[vm] * 4, out_specs=[vm] * 5,
                  out_shape=[jax.ShapeDtypeStruct((1, LANES), F32)] + [jax.ShapeDtypeStruct(w.shape, F32)] * 4,
                  scratch_shapes=[pltpu.VMEM((N_DEV, r, LANES), F32), pltpu.VMEM((r, LANES), F32),
                                  pltpu.SemaphoreType.DMA((N_PEERS,)), pltpu.SemaphoreType.DMA((N_PEERS,))],
                  compiler_params=_params())(packed, w, m, v)


def _rows(vec):
    return vec.reshape(-1, LANES)


def kernel(x, p, g_mix, w_in, conv_w, g_conv_out, g_attn_out, w_out, g_mlp, w_up, w_down, g_ple, w_ple_gate, w_ple_proj, g_final, loss_target, m_g_mix, m_w_in, m_conv_w, m_g_conv_out, m_g_attn_out, m_w_out, m_g_mlp, m_w_up, m_w_down, m_g_ple, m_w_ple_gate, m_w_ple_proj, m_g_final, v_g_mix, v_w_in, v_conv_w, v_g_conv_out, v_g_attn_out, v_w_out, v_g_mlp, v_w_up, v_w_down, v_g_ple, v_w_ple_gate, v_w_ple_proj, v_g_final):
    s, d = x.shape[1], x.shape[2]
    w_conv = g_conv_out.shape[1]
    w_attn = g_attn_out.shape[1]
    cw = conv_w.shape[2]
    xs, ps, tgt = x[0], p[0, 0], loss_target[0]
    place = jnp.stack([lax.axis_index("x"), lax.axis_index("y"), lax.axis_index("c")]).astype(jnp.int32)
    my_slot = 4 * place[0] + 2 * place[1] + place[2]

    conv_tile = jnp.pad(conv_w[0], ((0, HALO - CONV_K), (0, LANES - cw)))
    big = [w_in[0], w_out[0], w_up[0], w_down[0], w_ple_gate[0], w_ple_proj[0]]
    win_g, conv_g = _all_gather([big[0], conv_tile], [BF16, F32])
    s_out, s_up, s_down, s_gate, s_proj = _cast_shards(big[1:])
    conv_full = jnp.transpose(conv_g[:, :CONV_K, :cw], (1, 0, 2)).reshape(CONV_K, w_conv)
    in_shard, up_shard, proj_shard = big[0].shape[1], big[2].shape[1], big[5].shape[1]

    proj, a, g_out, g_gate, g_proj = _mm_nn("in_proj", xs, win_g, n_shard=in_shard, tn=2 * in_shard, tm=2048,
                                            lhs_norm=g_mix, carry=[("gather_out", [s_out, s_gate, s_proj])])
    cat = _conv_fwd(proj, conv_full, g_conv_out, w_conv, d)
    o, cat, (g_up, g_down, wout_g, wgate_g, wproj_g) = _attn_fwd(
        proj, g_attn_out, cat, w_conv,
        [("gather_out", [s_up, s_down]), ("gather_pass", [g_out, g_gate, g_proj])])
    wout_f = wout_g.reshape(-1, wout_g.shape[-1])
    wgate_f = wgate_g.reshape(-1, wgate_g.shape[-1])
    h1, wup_g = _mm_nn("out_proj", cat, wout_f, epilogue=_ep_residual, extras=(xs,),
                       carry=[("gather_pass", [g_up])])
    act, mn, wdown_g = _mm_nn("mlp_up", h1, wup_g, n_shard=up_shard, epilogue=_ep_up, out_dtypes=(BF16,), tm=2048,
                              lhs_norm=g_mlp, carry=[("gather_pass", [g_down])])
    wdown_f = wdown_g.reshape(-1, wdown_g.shape[-1])
    h2, = _mm_nn("mlp_down", act, wdown_f, epilogue=_ep_residual, extras=(h1,))
    pp = _ple_proj(ps, wproj_g)
    loss_part, dh3, dgl, dpp, dg_final, n3 = _ple_gate_loss(h2, g_ple, wgate_f, pp, tgt, g_final.reshape(1, d))

    def slots(t2d):
        return t2d.reshape(N_DEV, -1, t2d.shape[-1])

    dw_proj = _d_ple_proj(ps, dpp, proj_shard)
    dw_gate = [slots(t) for t in _mm_tn("d_w_ple_gate", n3, dgl)]
    dh2, dh2b, dg_ple = _mm_nt_norm_bwd("d_norm_ple", dgl, wgate_f, h2, g_ple, dh3)
    du, gate_recv, proj_recv = _mm_nt("d_mlp_act", dh2b, wdown_f, epilogue=_ep_dact, out_dtypes=(BF16,),
                                      extras=(act,), tm=2048, carry=[("scatter", [dw_gate[1], dw_proj[1]])])
    dw_down = [slots(t) for t in _mm_tn("d_w_down", act, dh2b)]
    near, far = (1, 2, 3, 4, 5), (6, 7)
    dw_up = _mm_tn("d_w_up", mn, du, n_shard=up_shard)
    dh1, dh1b, dg_mlp, down_part = _mm_nt_norm_bwd(
        "d_norm_mlp", du, wup_g, h1, g_mlp, dh2, k_shard=up_shard, tm=1024,
        carry=[("scatter", [dw_down[1]], near)])
    dcat, = _mm_nt("d_cat", dh1b, wout_f)
    dw_out = [slots(t) for t in _mm_tn("d_w_out", cat, dh1b)]
    dproj, dg_attn, (down_recv, up_recv) = _attn_bwd(
        proj, o, dcat, g_attn_out, w_conv,
        [("scatter_more", [dw_down[1], down_part], far), ("scatter", [dw_up[1]])])
    dproj, dconv, dg_conv = _conv_bwd(proj, dcat, conv_full, g_conv_out, dproj, w_conv)
    *dw_in, out_recv = _mm_tn("d_w_in", a, dproj, n_shard=in_shard, tn=in_shard,
                              carry=[("scatter", [dw_out[1]])])
    grad_x, _, dg_mix, in_recv = _mm_nt_norm_bwd("d_norm_mix", dproj, win_g, xs, g_mix, dh1, k_shard=in_shard,
                                                 tk=2 * in_shard, tm=1024, carry=[("scatter", [dw_in[1]])])

    names = ["w_in", "w_out", "w_up", "w_down", "w_ple_gate", "w_ple_proj"]
    owns = [dw_in[0], dw_out[0], dw_up[0], dw_down[0], dw_gate[0], dw_proj[0]]
    recvs = [in_recv, out_recv, up_recv, down_recv, gate_recv, proj_recv]
    moments = [(m_w_in, v_w_in), (m_w_out, v_w_out), (m_w_up, v_w_up), (m_w_down, v_w_down),
               (m_w_ple_gate, v_w_ple_gate), (m_w_ple_proj, v_w_ple_proj)]
    big_out = {}
    for n, own, rc, wt, (mm, vv) in zip(names, owns, recvs, big, moments):
        big_out[n] = [t[None] for t in _adam_sharded("adam_" + n, own, rc, wt, mm[0], vv[0], place)]

    n_conv_rows = CONV_K * w_conv // LANES
    small_g = jnp.concatenate(
        [_rows(dg_mix[0]), _rows(dg_conv[0]), _rows(dg_attn[0]), _rows(dg_mlp[0]), _rows(dg_ple[0]),
         _rows(dg_final[0]), _rows(dconv.reshape(-1)), loss_part], axis=0)
    n_gain_rows = small_g.shape[0] - n_conv_rows - 1
    pad_rows = (-small_g.shape[0]) % HALO
    small_g = jnp.pad(small_g, ((0, pad_rows), (0, 0)))
    assert 2 * cw == LANES and n_gain_rows % HALO == 0

    def pack(vecs, conv_part):
        rows = [_rows(t.reshape(-1)) for t in vecs]
        rows.append(jnp.pad(conv_part, ((0, HALO - CONV_K), (0, LANES - cw))))
        return jnp.concatenate(rows, axis=0)

    gains = [g_mix, g_conv_out, g_attn_out, g_mlp, g_ple, g_final]
    gains_m = [m_g_mix, m_g_conv_out, m_g_attn_out, m_g_mlp, m_g_ple, m_g_final]
    gains_v = [v_g_mix, v_g_conv_out, v_g_attn_out, v_g_mlp, v_g_ple, v_g_final]
    loss_row, gpack, sd, sm, sv = _reduce_adam_small(
        small_g, pack(gains, conv_w[0]), pack(gains_m, m_conv_w[0]), pack(gains_v, v_conv_w[0]),
        n_gain_rows, n_conv_rows)
    loss = loss_row[0, 0]

    def unpack(packed):
        out, r0 = [], 0
        for t in gains:
            nr = t.size // LANES
            out.append(packed[r0:r0 + nr].reshape(t.shape))
            r0 += nr
        out.append(packed[r0:r0 + CONV_K, :cw][None])
        return out

    sg_l, sd_l, sm_l, sv_l = unpack(gpack), unpack(sd), unpack(sm), unpack(sv)
    small_names = ["g_mix", "g_conv_out", "g_attn_out", "g_mlp", "g_ple", "g_final", "conv_w"]
    small_out = {n: [sg_l[i], sd_l[i], sm_l[i], sv_l[i]] for i, n in enumerate(small_names)}

    order = ["g_mix", "w_in", "conv_w", "g_conv_out", "g_attn_out", "w_out", "g_mlp", "w_up", "w_down",
             "g_ple", "w_ple_gate", "w_ple_proj", "g_final"]
    table = {**big_out, **small_out}
    outs = [loss, grad_x[None]]
    for kind in range(4):
        outs.extend(table[n][kind] for n in order)
    return tuple(outs)
```
